```python
import math
import jax, jax.numpy as jnp
from jax import lax
import numpy as np

D_MODEL = 1024
BATCH = 8
SEQ = 4096
DEPTH = 4

HEAD_DIM = 64
D_ATTN = D_MODEL // 2
ATTN_HEADS = D_ATTN // HEAD_DIM
Q_BLOCK = 128
D_CONV = D_MODEL // 4
CONV_K = 3
D_POOL = D_MODEL // 4
POOL_WINDOWS = (2, 4, 8, 16)
POOL_GROUPS = len(POOL_WINDOWS)
POOL_GROUP_DIM = D_POOL // POOL_GROUPS
POOL_OUT_DIM = D_MODEL // POOL_GROUPS
N_BRANCHES = 3
D_FF = -(-8 * D_MODEL // (3 * 256)) * 256
EPS = 1e-6

IN_SIZES = (D_ATTN, D_ATTN, D_ATTN, ATTN_HEADS, D_CONV, D_CONV, D_CONV, D_POOL, N_BRANCHES * D_MODEL)
D_IN = sum(IN_SIZES)
IN_SPLITS = tuple(int(v) for v in np.cumsum(IN_SIZES)[:-1])

kernel_name = "fox_conv_pool_gated_hybrid"


def rmsnorm(x, g):
    xf = x.astype(jnp.float32)
    y = xf * lax.rsqrt(jnp.mean(xf * xf, axis=-1, keepdims=True) + EPS)
    return (y * g.astype(jnp.float32)).astype(x.dtype)


def forgetting_attention(q, k, v, logf):
    S = q.shape[2]
    c = jnp.cumsum(logf, axis=-1)
    scale = HEAD_DIM ** -0.5
    outs = []
    for i in range(S // Q_BLOCK):
        q0, q1 = i * Q_BLOCK, (i + 1) * Q_BLOCK
        qb = q[:, :, q0:q1]
        kb = k[:, :, :q1]
        vb = v[:, :, :q1]
        logits = jnp.einsum('bhqd,bhkd->bhqk', qb, kb, preferred_element_type=jnp.float32) * scale
        logits = logits + (c[:, :, q0:q1, None] - c[:, :, None, :q1])
        causal = (q0 + jnp.arange(Q_BLOCK))[:, None] >= jnp.arange(q1)[None, :]
        logits = jnp.where(causal, logits, -jnp.inf)
        p = jax.nn.softmax(logits, axis=-1)
        outs.append(jnp.einsum('bhqk,bhkd->bhqd', p.astype(vb.dtype), vb))
    return jnp.concatenate(outs, axis=2)


def short_conv_mixer(u, b_gate, c_gate, w):
    S = u.shape[1]
    z = c_gate * u
    zp = jnp.pad(z, ((0, 0), (CONV_K - 1, 0), (0, 0)))
    conv = w[0] * zp[:, 0:S]
    for j in range(1, CONV_K):
        conv = conv + w[j] * zp[:, j:j + S]
    return b_gate * conv


def pooling_mixer(u, w_grp, scale):
    Bsz, S, _ = u.shape
    uf = u.astype(jnp.float32)
    cs = jnp.cumsum(uf, axis=1)
    t = jnp.arange(S, dtype=jnp.float32)
    groups = []
    for g, w in enumerate(POOL_WINDOWS):
        sl = slice(g * POOL_GROUP_DIM, (g + 1) * POOL_GROUP_DIM)
        csg = cs[:, :, sl]
        lagged = jnp.pad(csg[:, :S - w], ((0, 0), (w, 0), (0, 0)))
        counts = jnp.minimum(t + 1.0, float(w))[None, :, None]
        groups.append((csg - lagged) / counts - uf[:, :, sl])
    d = jnp.stack(groups, axis=2).astype(u.dtype)
    out = jnp.einsum('bsgc,gcd->bsgd', d, w_grp).reshape(Bsz, S, D_MODEL)
    return out * scale


def _fwd_setup_inputs(seed: int = 0) -> dict:
    key = jax.random.key(seed)
    ks = jax.random.split(key, 16)
    nrm = lambda k, shape, fan_in: jax.random.normal(k, shape, jnp.float32) * (fan_in ** -0.5)
    return {
        "x": jax.random.normal(ks[0], (BATCH, SEQ, D_MODEL), jnp.float32),
        "norm_mix_g": 1.0 + 0.02 * jax.random.normal(ks[1], (DEPTH, D_MODEL), jnp.float32),
        "w_in": nrm(ks[2], (DEPTH, D_MODEL, D_IN), D_MODEL),
        "forget_b": jax.random.uniform(ks[3], (DEPTH, ATTN_HEADS), jnp.float32, 2.0, 5.0),
        "q_norm_g": 1.0 + 0.02 * jax.random.normal(ks[4], (DEPTH, HEAD_DIM), jnp.float32),
        "k_norm_g": 1.0 + 0.02 * jax.random.normal(ks[5], (DEPTH, HEAD_DIM), jnp.float32),
        "w_attn_out": nrm(ks[6], (DEPTH, D_ATTN, D_MODEL), D_ATTN),
        "conv_w": nrm(ks[7], (DEPTH, CONV_K, D_CONV), CONV_K),
        "w_conv_out": nrm(ks[8], (DEPTH, D_CONV, D_MODEL), D_CONV),
        "pool_w": nrm(ks[9], (DEPTH, POOL_GROUPS, POOL_GROUP_DIM, POOL_OUT_DIM), POOL_GROUP_DIM),
        "pool_scale": 1.0 + 0.1 * jax.random.normal(ks[10], (DEPTH, D_MODEL), jnp.float32),
        "w_o": nrm(ks[11], (DEPTH, D_MODEL, D_MODEL), D_MODEL),
        "norm_ffn_g": 1.0 + 0.02 * jax.random.normal(ks[12], (DEPTH, D_MODEL), jnp.float32),
        "w_ffn_in": nrm(ks[13], (DEPTH, D_MODEL, 2 * D_FF), D_MODEL),
        "w_ffn_out": nrm(ks[14], (DEPTH, D_FF, D_MODEL), D_FF),
    }


def _fwd_reference(x, norm_mix_g, w_in, forget_b, q_norm_g, k_norm_g, w_attn_out, conv_w,
              w_conv_out, pool_w, pool_scale, w_o, norm_ffn_g, w_ffn_in, w_ffn_out):
    Bsz, S, _ = x.shape

    def heads(t):
        return t.reshape(Bsz, S, ATTN_HEADS, HEAD_DIM).transpose(0, 2, 1, 3)

    for l in range(DEPTH):
        h = rmsnorm(x, norm_mix_g[l])
        proj = h @ w_in[l]
        q, k, v, f_logit, cx, cb, cc, px, gate_logit = jnp.split(proj, IN_SPLITS, axis=-1)

        qh = rmsnorm(heads(q), q_norm_g[l])
        kh = rmsnorm(heads(k), k_norm_g[l])
        vh = heads(v)
        logf = jax.nn.log_sigmoid((f_logit + forget_b[l]).astype(jnp.float32)).transpose(0, 2, 1)
        a = forgetting_attention(qh, kh, vh, logf).transpose(0, 2, 1, 3).reshape(Bsz, S, D_ATTN)
        y_attn = a @ w_attn_out[l]

        y_conv = short_conv_mixer(cx, cb, cc, conv_w[l]) @ w_conv_out[l]

        y_pool = pooling_mixer(px, pool_w[l], pool_scale[l])

        g = jax.nn.sigmoid(gate_logit).reshape(Bsz, S, N_BRANCHES, D_MODEL)
        merged = g[:, :, 0] * y_attn + g[:, :, 1] * y_conv + g[:, :, 2] * y_pool
        x = x + merged @ w_o[l]

        h = rmsnorm(x, norm_ffn_g[l])
        gt, up = jnp.split(h @ w_ffn_in[l], 2, axis=-1)
        x = x + (jax.nn.silu(gt) * up) @ w_ffn_out[l]
    return x


import jax as _jax
import jax.numpy as _jnp

TWIN_FORMAT = 'train_step'
FWD_PARAMS = ['x', 'norm_mix_g', 'w_in', 'forget_b', 'q_norm_g', 'k_norm_g', 'w_attn_out', 'conv_w', 'w_conv_out', 'pool_w', 'pool_scale', 'w_o', 'norm_ffn_g', 'w_ffn_in', 'w_ffn_out']
TWIN_WEIGHTS = ['norm_mix_g', 'w_in', 'forget_b', 'q_norm_g', 'k_norm_g', 'w_attn_out', 'conv_w', 'w_conv_out', 'pool_w', 'pool_scale', 'w_o', 'norm_ffn_g', 'w_ffn_in', 'w_ffn_out']
TWIN_DIFF_INPUT = 'x'
TWIN_INPUTS = ['x', 'norm_mix_g', 'w_in', 'forget_b', 'q_norm_g', 'k_norm_g', 'w_attn_out', 'conv_w', 'w_conv_out', 'pool_w', 'pool_scale', 'w_o', 'norm_ffn_g', 'w_ffn_in', 'w_ffn_out', 'loss_target', 'm_norm_mix_g', 'm_w_in', 'm_forget_b', 'm_q_norm_g', 'm_k_norm_g', 'm_w_attn_out', 'm_conv_w', 'm_w_conv_out', 'm_pool_w', 'm_pool_scale', 'm_w_o', 'm_norm_ffn_g', 'm_w_ffn_in', 'm_w_ffn_out', 'v_norm_mix_g', 'v_w_in', 'v_forget_b', 'v_q_norm_g', 'v_k_norm_g', 'v_w_attn_out', 'v_conv_w', 'v_w_conv_out', 'v_pool_w', 'v_pool_scale', 'v_w_o', 'v_norm_ffn_g', 'v_w_ffn_in', 'v_w_ffn_out']
TWIN_OUTPUTS = ['loss', 'grad_x', 'grad_norm_mix_g', 'grad_w_in', 'grad_forget_b', 'grad_q_norm_g', 'grad_k_norm_g', 'grad_w_attn_out', 'grad_conv_w', 'grad_w_conv_out', 'grad_pool_w', 'grad_pool_scale', 'grad_w_o', 'grad_norm_ffn_g', 'grad_w_ffn_in', 'grad_w_ffn_out', 'delta_norm_mix_g', 'delta_w_in', 'delta_forget_b', 'delta_q_norm_g', 'delta_k_norm_g', 'delta_w_attn_out', 'delta_conv_w', 'delta_w_conv_out', 'delta_pool_w', 'delta_pool_scale', 'delta_w_o', 'delta_norm_ffn_g', 'delta_w_ffn_in', 'delta_w_ffn_out', 'new_m_norm_mix_g', 'new_m_w_in', 'new_m_forget_b', 'new_m_q_norm_g', 'new_m_k_norm_g', 'new_m_w_attn_out', 'new_m_conv_w', 'new_m_w_conv_out', 'new_m_pool_w', 'new_m_pool_scale', 'new_m_w_o', 'new_m_norm_ffn_g', 'new_m_w_ffn_in', 'new_m_w_ffn_out', 'new_v_norm_mix_g', 'new_v_w_in', 'new_v_forget_b', 'new_v_q_norm_g', 'new_v_k_norm_g', 'new_v_w_attn_out', 'new_v_conv_w', 'new_v_w_conv_out', 'new_v_pool_w', 'new_v_pool_scale', 'new_v_w_o', 'new_v_norm_ffn_g', 'new_v_w_ffn_in', 'new_v_w_ffn_out']
TWIN_LEAF_KINDS = {'loss': 'loss', 'grad_x': 'grad_x', 'grad_norm_mix_g': 'grad_w', 'grad_w_in': 'grad_w', 'grad_forget_b': 'grad_w', 'grad_q_norm_g': 'grad_w', 'grad_k_norm_g': 'grad_w', 'grad_w_attn_out': 'grad_w', 'grad_conv_w': 'grad_w', 'grad_w_conv_out': 'grad_w', 'grad_pool_w': 'grad_w', 'grad_pool_scale': 'grad_w', 'grad_w_o': 'grad_w', 'grad_norm_ffn_g': 'grad_w', 'grad_w_ffn_in': 'grad_w', 'grad_w_ffn_out': 'grad_w', 'delta_norm_mix_g': 'delta_w', 'delta_w_in': 'delta_w', 'delta_forget_b': 'delta_w', 'delta_q_norm_g': 'delta_w', 'delta_k_norm_g': 'delta_w', 'delta_w_attn_out': 'delta_w', 'delta_conv_w': 'delta_w', 'delta_w_conv_out': 'delta_w', 'delta_pool_w': 'delta_w', 'delta_pool_scale': 'delta_w', 'delta_w_o': 'delta_w', 'delta_norm_ffn_g': 'delta_w', 'delta_w_ffn_in': 'delta_w', 'delta_w_ffn_out': 'delta_w', 'new_m_norm_mix_g': 'new_m', 'new_m_w_in': 'new_m', 'new_m_forget_b': 'new_m', 'new_m_q_norm_g': 'new_m', 'new_m_k_norm_g': 'new_m', 'new_m_w_attn_out': 'new_m', 'new_m_conv_w': 'new_m', 'new_m_w_conv_out': 'new_m', 'new_m_pool_w': 'new_m', 'new_m_pool_scale': 'new_m', 'new_m_w_o': 'new_m', 'new_m_norm_ffn_g': 'new_m', 'new_m_w_ffn_in': 'new_m', 'new_m_w_ffn_out': 'new_m', 'new_v_norm_mix_g': 'new_v', 'new_v_w_in': 'new_v', 'new_v_forget_b': 'new_v', 'new_v_q_norm_g': 'new_v', 'new_v_k_norm_g': 'new_v', 'new_v_w_attn_out': 'new_v', 'new_v_conv_w': 'new_v', 'new_v_w_conv_out': 'new_v', 'new_v_pool_w': 'new_v', 'new_v_pool_scale': 'new_v', 'new_v_w_o': 'new_v', 'new_v_norm_ffn_g': 'new_v', 'new_v_w_ffn_in': 'new_v', 'new_v_w_ffn_out': 'new_v'}


def _forward(args):
    return _fwd_reference(*[args[k] for k in FWD_PARAMS])


def _output_shape():
    out = _jax.eval_shape(lambda: _forward(_fwd_setup_inputs(0)))
    return out.shape, out.dtype

N_MICROBATCH = 1
ADAM_LR = 0.001
ADAM_B1 = 0.9
ADAM_B2 = 0.999
ADAM_EPS = 1e-08
ADAM_WD = 0.01
ADAM_STEP = 10
PER_EXAMPLE_BATCH_AXIS = {'x': 0, 'loss_target': 0}
SHARED_INPUTS = []
_WEIGHT_DTYPES = {'norm_mix_g': _jnp.float32, 'w_in': _jnp.float32, 'forget_b': _jnp.float32, 'q_norm_g': _jnp.float32, 'k_norm_g': _jnp.float32, 'w_attn_out': _jnp.float32, 'conv_w': _jnp.float32, 'w_conv_out': _jnp.float32, 'pool_w': _jnp.float32, 'pool_scale': _jnp.float32, 'w_o': _jnp.float32, 'norm_ffn_g': _jnp.float32, 'w_ffn_in': _jnp.float32, 'w_ffn_out': _jnp.float32}
MOMENT_SCALE = {'norm_mix_g': 3.782491e+01, 'w_in': 7.542179e-01, 'forget_b': 3.343737e+01, 'q_norm_g': 4.800964e+00, 'k_norm_g': 4.807718e+00, 'w_attn_out': 1.887175e-01, 'conv_w': 2.101889e+01, 'w_conv_out': 9.070979e-01, 'pool_w': 1.046649e+00, 'pool_scale': 7.188030e+00, 'w_o': 1.212542e+00, 'norm_ffn_g': 2.442751e+01, 'w_ffn_in': 3.906548e-01, 'w_ffn_out': 6.610571e-01}


def _to_microbatches(a, axis):
    t = _jnp.moveaxis(a, axis, 0)
    t = t.reshape((N_MICROBATCH, t.shape[0] // N_MICROBATCH) + t.shape[1:])
    return _jnp.moveaxis(t, 1, axis + 1)


def setup_inputs(seed: int = 0) -> dict:
    inp = _fwd_setup_inputs(seed)
    key = _jax.random.fold_in(_jax.random.key(seed), 7919)
    shape, _ = _output_shape()
    out = dict(inp)
    out["loss_target"] = _jax.random.normal(_jax.random.fold_in(key, 0), shape, _jnp.float32)
    for i, name in enumerate(TWIN_WEIGHTS):
        w = inp[name].astype(_jnp.float32)
        if MOMENT_SCALE is None:
            s = _jnp.sqrt(_jnp.mean(_jnp.square(w)) + 1e-30)
        else:
            s = MOMENT_SCALE[name]
        km, kv = _jax.random.split(_jax.random.fold_in(key, i + 1))
        out[name] = w
        out["m_" + name] = s * _jax.random.normal(km, w.shape, _jnp.float32)
        out["v_" + name] = (s * s) * _jax.random.uniform(kv, w.shape, _jnp.float32, 0.5, 1.5)
    if N_MICROBATCH > 1:
        for name, axis in PER_EXAMPLE_BATCH_AXIS.items():
            out[name] = _to_microbatches(out[name], axis)
    return {'x': out['x'], 'norm_mix_g': out['norm_mix_g'], 'w_in': out['w_in'], 'forget_b': out['forget_b'], 'q_norm_g': out['q_norm_g'], 'k_norm_g': out['k_norm_g'], 'w_attn_out': out['w_attn_out'], 'conv_w': out['conv_w'], 'w_conv_out': out['w_conv_out'], 'pool_w': out['pool_w'], 'pool_scale': out['pool_scale'], 'w_o': out['w_o'], 'norm_ffn_g': out['norm_ffn_g'], 'w_ffn_in': out['w_ffn_in'], 'w_ffn_out': out['w_ffn_out'], 'loss_target': out['loss_target'], 'm_norm_mix_g': out['m_norm_mix_g'], 'm_w_in': out['m_w_in'], 'm_forget_b': out['m_forget_b'], 'm_q_norm_g': out['m_q_norm_g'], 'm_k_norm_g': out['m_k_norm_g'], 'm_w_attn_out': out['m_w_attn_out'], 'm_conv_w': out['m_conv_w'], 'm_w_conv_out': out['m_w_conv_out'], 'm_pool_w': out['m_pool_w'], 'm_pool_scale': out['m_pool_scale'], 'm_w_o': out['m_w_o'], 'm_norm_ffn_g': out['m_norm_ffn_g'], 'm_w_ffn_in': out['m_w_ffn_in'], 'm_w_ffn_out': out['m_w_ffn_out'], 'v_norm_mix_g': out['v_norm_mix_g'], 'v_w_in': out['v_w_in'], 'v_forget_b': out['v_forget_b'], 'v_q_norm_g': out['v_q_norm_g'], 'v_k_norm_g': out['v_k_norm_g'], 'v_w_attn_out': out['v_w_attn_out'], 'v_conv_w': out['v_conv_w'], 'v_w_conv_out': out['v_w_conv_out'], 'v_pool_w': out['v_pool_w'], 'v_pool_scale': out['v_pool_scale'], 'v_w_o': out['v_w_o'], 'v_norm_ffn_g': out['v_norm_ffn_g'], 'v_w_ffn_in': out['v_w_ffn_in'], 'v_w_ffn_out': out['v_w_ffn_out']}


def _loss(weights, diff, rest, loss_target):
    with _jax.named_scope("forward"):
        args = {**rest, TWIN_DIFF_INPUT: diff, **{k: w.astype(_WEIGHT_DTYPES[k]) for k, w in weights.items()}}
        y = _forward(args)
    with _jax.named_scope("loss_head"):
        err = _jnp.square(y.astype(_jnp.float32) - loss_target)
        return 0.5 * _jnp.sum(_jnp.mean(err, axis=-1)) if err.ndim else 0.5 * err


def _adamw(w, g, m, v):
    m = ADAM_B1 * m + (1.0 - ADAM_B1) * g
    v = ADAM_B2 * v + (1.0 - ADAM_B2) * _jnp.square(g)
    m_hat = m / (1.0 - ADAM_B1 ** ADAM_STEP)
    v_hat = v / (1.0 - ADAM_B2 ** ADAM_STEP)
    delta = -ADAM_LR * (m_hat / (_jnp.sqrt(v_hat) + ADAM_EPS) + ADAM_WD * w)
    return delta, m, v


def reference(x, norm_mix_g, w_in, forget_b, q_norm_g, k_norm_g, w_attn_out, conv_w, w_conv_out, pool_w, pool_scale, w_o, norm_ffn_g, w_ffn_in, w_ffn_out, loss_target, m_norm_mix_g, m_w_in, m_forget_b, m_q_norm_g, m_k_norm_g, m_w_attn_out, m_conv_w, m_w_conv_out, m_pool_w, m_pool_scale, m_w_o, m_norm_ffn_g, m_w_ffn_in, m_w_ffn_out, v_norm_mix_g, v_w_in, v_forget_b, v_q_norm_g, v_k_norm_g, v_w_attn_out, v_conv_w, v_w_conv_out, v_pool_w, v_pool_scale, v_w_o, v_norm_ffn_g, v_w_ffn_in, v_w_ffn_out):
    given = dict(x=x, norm_mix_g=norm_mix_g, w_in=w_in, forget_b=forget_b, q_norm_g=q_norm_g, k_norm_g=k_norm_g, w_attn_out=w_attn_out, conv_w=conv_w, w_conv_out=w_conv_out, pool_w=pool_w, pool_scale=pool_scale, w_o=w_o, norm_ffn_g=norm_ffn_g, w_ffn_in=w_ffn_in, w_ffn_out=w_ffn_out, loss_target=loss_target, m_norm_mix_g=m_norm_mix_g, m_w_in=m_w_in, m_forget_b=m_forget_b, m_q_norm_g=m_q_norm_g, m_k_norm_g=m_k_norm_g, m_w_attn_out=m_w_attn_out, m_conv_w=m_conv_w, m_w_conv_out=m_w_conv_out, m_pool_w=m_pool_w, m_pool_scale=m_pool_scale, m_w_o=m_w_o, m_norm_ffn_g=m_norm_ffn_g, m_w_ffn_in=m_w_ffn_in, m_w_ffn_out=m_w_ffn_out, v_norm_mix_g=v_norm_mix_g, v_w_in=v_w_in, v_forget_b=v_forget_b, v_q_norm_g=v_q_norm_g, v_k_norm_g=v_k_norm_g, v_w_attn_out=v_w_attn_out, v_conv_w=v_conv_w, v_w_conv_out=v_w_conv_out, v_pool_w=v_pool_w, v_pool_scale=v_pool_scale, v_w_o=v_w_o, v_norm_ffn_g=v_norm_ffn_g, v_w_ffn_in=v_w_ffn_in, v_w_ffn_out=v_w_ffn_out)
    weights = {n: given[n] for n in TWIN_WEIGHTS}
    shared = {n: given[n] for n in SHARED_INPUTS}
    per_example = {n: given[n] for n in ['x']}
    grad_fn = _jax.value_and_grad(_loss, argnums=(0, 1))

    def one_microbatch(ex, loss_target):
        ex = dict(ex)
        diff = ex.pop(TWIN_DIFF_INPUT)
        return grad_fn(weights, diff, {**shared, **ex}, loss_target)

    if N_MICROBATCH == 1:
        loss, (grad_w, grad_x) = one_microbatch(per_example, given["loss_target"])
    else:
        def body(carry, xs):
            loss_sum, grad_sum = carry
            l_k, (gw_k, gx_k) = one_microbatch(xs[0], xs[1])
            with _jax.named_scope("update"):
                return (loss_sum + l_k, _jax.tree.map(_jnp.add, grad_sum, gw_k)), gx_k

        init = (_jnp.zeros((), _jnp.float32), _jax.tree.map(_jnp.zeros_like, weights))
        (loss, grad_w), grad_x = _jax.lax.scan(body, init, (per_example, given["loss_target"]))
    with _jax.named_scope("update"):
        delta_w, new_m, new_v = {}, {}, {}
        for n in TWIN_WEIGHTS:
            delta_w[n], new_m[n], new_v[n] = _adamw(weights[n], grad_w[n], given["m_" + n], given["v_" + n])
    return (loss, grad_x, *[grad_w[n] for n in TWIN_WEIGHTS], *[delta_w[n] for n in TWIN_WEIGHTS],
            *[new_m[n] for n in TWIN_WEIGHTS], *[new_v[n] for n in TWIN_WEIGHTS])
```

```python
import functools

import jax
import jax.numpy as jnp
from jax import lax
from jax.experimental import pallas as pl
from jax.experimental.pallas import tpu as pltpu

F32 = jnp.float32
BF16 = jnp.bfloat16

N_DEV = 8
DEPTH = 4
D_MODEL = 1024
HEAD_DIM = 64
HEADS = 8
D_ATTN = 512
D_CONV = 256
D_POOL = 256
D_FF = 2816
D_IN = 5640
EPS = 1e-6
ATTN_SCALE = HEAD_DIM ** -0.5

N_REST = 4096
N_MAIN = 5632
N_FULL = 5760
FF_BLK = 256
N_FF_BLKS = D_FF // FF_BLK
HALO = 16

ADAM_LR = 0.001
ADAM_B1 = 0.9
ADAM_B2 = 0.999
ADAM_EPS = 1e-08
ADAM_WD = 0.01
ADAM_STEP = 10

PACK_COLS = 1024
PACK_ROWS = 8192
SMALL_ROWS = 128

VMEM_LIMIT = 48 * 2 ** 20


def _cparams(sem, vmem=None):
    return pltpu.CompilerParams(dimension_semantics=sem, vmem_limit_bytes=vmem or VMEM_LIMIT)


def _pick(n, cands):
    for c in cands:
        if n % c == 0:
            return c
    raise ValueError(f"no tile for {n}")


def _sigmoid(v):
    return 1.0 / (1.0 + jnp.exp(-v))


def _rstd(v):
    return lax.rsqrt(jnp.mean(v * v, axis=-1, keepdims=True) + EPS)


def _dot(a, b):
    return jnp.dot(a, b, preferred_element_type=F32)


def _dot_tn(a, b):
    return lax.dot_general(a, b, (((0,), (0,)), ((), ())), preferred_element_type=F32)


def _dot_nt(a, b):
    return lax.dot_general(a, b, (((1,), (1,)), ((), ())), preferred_element_type=F32)


def norm_matmul(x, g, w, n_cols, name):
    s, d = x.shape
    tm, tn = min(512, s), 512

    def body(x_ref, g_ref, w_ref, o_ref, h_ref):
        @pl.when(pl.program_id(1) == 0)
        def _():
            xv = x_ref[...]
            h_ref[...] = (xv * _rstd(xv) * g_ref[...]).astype(BF16)

        o_ref[...] = _dot(h_ref[...], w_ref[...]).astype(BF16)

    return pl.pallas_call(
        body, grid=(s // tm, n_cols // tn),
        in_specs=[pl.BlockSpec((tm, d), lambda i, j: (i, 0)), pl.BlockSpec((1, d), lambda i, j: (0, 0)),
                  pl.BlockSpec((d, tn), lambda i, j: (0, j))],
        out_specs=[pl.BlockSpec((tm, tn), lambda i, j: (i, j)), pl.BlockSpec((tm, d), lambda i, j: (i, 0))],
        out_shape=[jax.ShapeDtypeStruct((s, n_cols), BF16), jax.ShapeDtypeStruct((s, d), BF16)],
        compiler_params=_cparams(("parallel", "arbitrary")), name=name)(x, g, w)


def tn_matmul(a, b, name):
    t, m = a.shape
    n = b.shape[1]
    tk = min(512, t)
    tmm = _pick(m, (1024, 1408, 512, 256))
    tn = _pick(n, (1152, 1024, 512, 128))

    def body(a_ref, b_ref, o_ref):
        @pl.when(pl.program_id(2) == 0)
        def _():
            o_ref[...] = jnp.zeros_like(o_ref)

        o_ref[...] += _dot_tn(a_ref[...].astype(BF16), b_ref[...].astype(BF16))

    return pl.pallas_call(
        body, grid=(m // tmm, n // tn, t // tk),
        in_specs=[pl.BlockSpec((tk, tmm), lambda i, j, k: (k, i)), pl.BlockSpec((tk, tn), lambda i, j, k: (k, j))],
        out_specs=pl.BlockSpec((tmm, tn), lambda i, j, k: (i, j)),
        out_shape=jax.ShapeDtypeStruct((m, n), F32),
        compiler_params=_cparams(("parallel", "parallel", "arbitrary")), name=name)(a, b)


def matmul_normbwd(a, wt, x, g, dres, name):
    s, k = a.shape
    d = wt.shape[1]
    tm = min(512, s)
    tk = _pick(k, (1152, 512))
    nk = k // tk

    def body(a_ref, w_ref, x_ref, g_ref, r_ref, dx_ref, dg_ref, acc_ref):
        i, kk = pl.program_id(0), pl.program_id(1)

        @pl.when(kk == 0)
        def _():
            acc_ref[...] = jnp.zeros_like(acc_ref)

        @pl.when((i == 0) & (kk == 0))
        def _():
            dg_ref[...] = jnp.zeros_like(dg_ref)

        acc_ref[...] += _dot(a_ref[...], w_ref[...])

        @pl.when(kk == nk - 1)
        def _():
            xv = x_ref[...]
            r = _rstd(xv)
            y = xv * r
            dh = acc_ref[...]
            dy = dh * g_ref[...]
            dx_ref[...] = r_ref[...] + r * (dy - y * jnp.mean(dy * y, axis=-1, keepdims=True))
            dg_ref[...] += jnp.sum(dh * y, axis=0, keepdims=True)

    return pl.pallas_call(
        body, grid=(s // tm, nk),
        in_specs=[pl.BlockSpec((tm, tk), lambda i, kk: (i, kk)), pl.BlockSpec((tk, d), lambda i, kk: (kk, 0)),
                  pl.BlockSpec((tm, d), lambda i, kk: (i, 0)), pl.BlockSpec((1, d), lambda i, kk: (0, 0)),
                  pl.BlockSpec((tm, d), lambda i, kk: (i, 0))],
        out_specs=[pl.BlockSpec((tm, d), lambda i, kk: (i, 0)), pl.BlockSpec((1, d), lambda i, kk: (0, 0))],
        out_shape=[jax.ShapeDtypeStruct((s, d), F32), jax.ShapeDtypeStruct((1, d), F32)],
        scratch_shapes=[pltpu.VMEM((tm, d), F32)],
        compiler_params=_cparams(("arbitrary", "arbitrary")), name=name)(a, wt, x, g, dres)


def swiglu_matmul(gu, w, x1, name):
    s = gu.shape[0]
    d = w.shape[1]
    tm = min(512, s)

    def body(gu_ref, w_ref, x_ref, o_ref):
        @pl.when(pl.program_id(1) == 0)
        def _():
            o_ref[...] = x_ref[...]

        gt = gu_ref[:, :FF_BLK].astype(F32)
        up = gu_ref[:, FF_BLK:].astype(F32)
        act = (gt * _sigmoid(gt) * up).astype(BF16)
        o_ref[...] += _dot(act, w_ref[...])

    return pl.pallas_call(
        body, grid=(s // tm, N_FF_BLKS),
        in_specs=[pl.BlockSpec((tm, 2 * FF_BLK), lambda i, j: (i, j)), pl.BlockSpec((FF_BLK, d), lambda i, j: (j, 0)),
                  pl.BlockSpec((tm, d), lambda i, j: (i, 0))],
        out_specs=pl.BlockSpec((tm, d), lambda i, j: (i, 0)),
        out_shape=jax.ShapeDtypeStruct((s, d), F32),
        compiler_params=_cparams(("parallel", "arbitrary")), name=name)(gu, w, x1)


def swiglu_bwd(dx2, gu, wt, name):
    s, d = dx2.shape
    tm = min(512, s)

    def body(dx_ref, gu_ref, w_ref, dgu_ref, act_ref):
        dact = _dot(dx_ref[...].astype(BF16), w_ref[...])
        gt = gu_ref[:, :FF_BLK].astype(F32)
        up = gu_ref[:, FF_BLK:].astype(F32)
        sg = _sigmoid(gt)
        act_ref[...] = (gt * sg * up).astype(BF16)
        dgu_ref[:, :FF_BLK] = (dact * up * (sg * (1.0 + gt * (1.0 - sg)))).astype(BF16)
        dgu_ref[:, FF_BLK:] = (dact * gt * sg).astype(BF16)

    return pl.pallas_call(
        body, grid=(s // tm, N_FF_BLKS),
        in_specs=[pl.BlockSpec((tm, d), lambda i, j: (i, 0)), pl.BlockSpec((tm, 2 * FF_BLK), lambda i, j: (i, j)),
                  pl.BlockSpec((d, FF_BLK), lambda i, j: (0, j))],
        out_specs=[pl.BlockSpec((tm, 2 * FF_BLK), lambda i, j: (i, j)), pl.BlockSpec((tm, FF_BLK), lambda i, j: (i, j))],
        out_shape=[jax.ShapeDtypeStruct((s, 2 * D_FF), BF16), jax.ShapeDtypeStruct((s, D_FF), BF16)],
        compiler_params=_cparams(("parallel", "arbitrary")), name=name)(dx2, gu, wt)


def loss_kernel(y, tgt, name):
    s, d = y.shape
    tm = min(512, s)

    def body(y_ref, t_ref, l_ref, dy_ref):
        @pl.when(pl.program_id(0) == 0)
        def _():
            l_ref[...] = jnp.zeros_like(l_ref)

        err = y_ref[...] - t_ref[...]
        dy_ref[...] = err * (1.0 / d)
        l_ref[...] += jnp.sum(jnp.sum(err * err, axis=1, keepdims=True), axis=0, keepdims=True)

    return pl.pallas_call(
        body, grid=(s // tm,),
        in_specs=[pl.BlockSpec((tm, d), lambda i: (i, 0)), pl.BlockSpec((tm, d), lambda i: (i, 0))],
        out_specs=[pl.BlockSpec((8, 128), lambda i: (0, 0)), pl.BlockSpec((tm, d), lambda i: (i, 0))],
        out_shape=[jax.ShapeDtypeStruct((8, 128), F32), jax.ShapeDtypeStruct((s, d), F32)],
        compiler_params=_cparams(("arbitrary",)), name=name)(y, tgt)


def _split3(v):
    a1 = v.astype(BF16)
    r1 = v - a1.astype(F32)
    a2 = r1.astype(BF16)
    a3 = (r1 - a2.astype(F32)).astype(BF16)
    return a1, a2, a3


def forget_fwd(h, wf, b, name):
    s, d = h.shape
    tm = min(512, s)

    def body(h_ref, w_ref, b_ref, z_ref, c_ref, carry_ref):
        @pl.when(pl.program_id(0) == 0)
        def _():
            carry_ref[...] = jnp.zeros_like(carry_ref)

        z = _dot(h_ref[...], w_ref[...]) + b_ref[...]
        z_ref[...] = z
        logf = jnp.minimum(z, 0.0) - jnp.log(1.0 + jnp.exp(-jnp.abs(z)))
        row = lax.broadcasted_iota(jnp.int32, (tm, tm), 0)
        col = lax.broadcasted_iota(jnp.int32, (tm, tm), 1)
        tri = (row >= col).astype(BF16)
        a1, a2, a3 = _split3(logf)
        c = _dot(tri, a1) + _dot(tri, a2) + _dot(tri, a3) + carry_ref[...]
        c_ref[...] = c
        carry_ref[...] = c[tm - 1:tm, :]

    return pl.pallas_call(
        body, grid=(s // tm,),
        in_specs=[pl.BlockSpec((tm, d), lambda i: (i, 0)), pl.BlockSpec((d, 128), lambda i: (0, 0)),
                  pl.BlockSpec((1, 128), lambda i: (0, 0))],
        out_specs=[pl.BlockSpec((tm, 128), lambda i: (i, 0)), pl.BlockSpec((tm, 128), lambda i: (i, 0))],
        out_shape=[jax.ShapeDtypeStruct((s, 128), F32), jax.ShapeDtypeStruct((s, 128), F32)],
        scratch_shapes=[pltpu.VMEM((1, 128), F32)],
        compiler_params=_cparams(("arbitrary",)), name=name)(h, wf, b)


def forget_bwd(dc, z, name):
    s = dc.shape[0]
    tm = min(512, s)
    nt = s // tm

    def body(dc_ref, z_ref, dz_ref, db_ref, carry_ref):
        @pl.when(pl.program_id(0) == 0)
        def _():
            carry_ref[...] = jnp.zeros_like(carry_ref)
            db_ref[...] = jnp.zeros_like(db_ref)

        row = lax.broadcasted_iota(jnp.int32, (tm, tm), 0)
        col = lax.broadcasted_iota(jnp.int32, (tm, tm), 1)
        tri = (col >= row).astype(BF16)
        a1, a2, a3 = _split3(dc_ref[...])
        dlogf = _dot(tri, a1) + _dot(tri, a2) + _dot(tri, a3) + carry_ref[...]
        carry_ref[...] = dlogf[0:1, :]
        dz = dlogf * (1.0 - _sigmoid(z_ref[...]))
        dz_ref[...] = dz.astype(BF16)
        db_ref[...] += jnp.sum(dz, axis=0, keepdims=True)

    return pl.pallas_call(
        body, grid=(nt,),
        in_specs=[pl.BlockSpec((tm, 128), lambda i: (nt - 1 - i, 0)), pl.BlockSpec((tm, 128), lambda i: (nt - 1 - i, 0))],
        out_specs=[pl.BlockSpec((tm, 128), lambda i: (nt - 1 - i, 0)), pl.BlockSpec((1, 128), lambda i: (0, 0))],
        out_shape=[jax.ShapeDtypeStruct((s, 128), BF16), jax.ShapeDtypeStruct((1, 128), F32)],
        scratch_shapes=[pltpu.VMEM((1, 128), F32)],
        compiler_params=_cparams(("arbitrary",)), name=name)(dc, z)


def _qk_hat(v_ref, g_ref, scale):
    v = v_ref[...].astype(F32)
    return (v * _rstd(v) * (g_ref[...] * scale)).astype(BF16)


def _norm_bwd(v, g, dhat, scale):
    r = _rstd(v)
    y = v * r
    dg = jnp.sum(dhat * y, axis=0, keepdims=True) * scale
    dy = dhat * (g * scale)
    return r * (dy - y * jnp.mean(dy * y, axis=-1, keepdims=True)), dg


def _causal(s_blk, qi, ki, tq, tk):
    row = qi * tq + lax.broadcasted_iota(jnp.int32, (tq, tk), 0)
    col = ki * tk + lax.broadcasted_iota(jnp.int32, (tq, tk), 1)
    return jnp.where(row >= col, s_blk, -jnp.inf)


def attn_fwd(q, k, v, ccol, crow, gq, gk, name):
    hh, s, hd = q.shape
    tq = tk = min(512, s)
    nq = s // tq

    def body(q_ref, k_ref, v_ref, cc_ref, cr_ref, gq_ref, gk_ref, o_ref, lse_ref, qn_ref, m_ref, l_ref, acc_ref):
        qi, ki = pl.program_id(1), pl.program_id(2)

        @pl.when(ki == 0)
        def _():
            qn_ref[...] = _qk_hat(q_ref, gq_ref, ATTN_SCALE)
            m_ref[...] = jnp.full_like(m_ref, -jnp.inf)
            l_ref[...] = jnp.zeros_like(l_ref)
            acc_ref[...] = jnp.zeros_like(acc_ref)

        @pl.when(ki <= qi)
        def _():
            kn = _qk_hat(k_ref, gk_ref, 1.0)
            sb = _dot_nt(qn_ref[...], kn) + (cc_ref[...] - cr_ref[...])
            sb = _causal(sb, qi, ki, tq, tk)
            m_new = jnp.maximum(m_ref[...], jnp.max(sb, axis=-1, keepdims=True))
            alpha = jnp.exp(m_ref[...] - m_new)
            p = jnp.exp(sb - m_new)
            l_ref[...] = alpha * l_ref[...] + jnp.sum(p, axis=-1, keepdims=True)
            acc_ref[...] = alpha * acc_ref[...] + _dot(p.astype(BF16), v_ref[...])
            m_ref[...] = m_new

        @pl.when(ki == qi)
        def _():
            o_ref[...] = (acc_ref[...] / l_ref[...]).astype(BF16)
            lse_ref[...] = m_ref[...] + jnp.log(l_ref[...])

    qspec = pl.BlockSpec((None, tq, hd), lambda h, i, j: (h, i, 0))
    kspec = pl.BlockSpec((None, tk, hd), lambda h, i, j: (h, jnp.minimum(i, j), 0))
    gspec = pl.BlockSpec((1, hd), lambda h, i, j: (0, 0))
    return pl.pallas_call(
        body, grid=(hh, nq, nq),
        in_specs=[qspec, kspec, kspec,
                  pl.BlockSpec((None, tq, 1), lambda h, i, j: (h, i, 0)),
                  pl.BlockSpec((None, 1, tk), lambda h, i, j: (h, 0, jnp.minimum(i, j))), gspec, gspec],
        out_specs=[qspec, pl.BlockSpec((None, tq, 1), lambda h, i, j: (h, i, 0))],
        out_shape=[jax.ShapeDtypeStruct((hh, s, hd), BF16), jax.ShapeDtypeStruct((hh, s, 1), F32)],
        scratch_shapes=[pltpu.VMEM((tq, hd), BF16), pltpu.VMEM((tq, 1), F32), pltpu.VMEM((tq, 1), F32),
                        pltpu.VMEM((tq, hd), F32)],
        compiler_params=_cparams(("parallel", "parallel", "arbitrary")), name=name)(q, k, v, ccol, crow, gq, gk)


def attn_bwd_dq(q, k, v, o, do, lse, ccol, crow, gq, gk, name):
    hh, s, hd = q.shape
    tq = tk = min(512, s)
    nq = s // tq

    def body(q_ref, k_ref, v_ref, o_ref, do_ref, lse_ref, cc_ref, cr_ref, gq_ref, gk_ref,
             dq_ref, dcc_ref, dg_ref, qn_ref, dl_ref, acc_ref, dca_ref):
        h, qi, ki = pl.program_id(0), pl.program_id(1), pl.program_id(2)

        @pl.when((h == 0) & (qi == 0) & (ki == 0))
        def _():
            dg_ref[...] = jnp.zeros_like(dg_ref)

        @pl.when(ki == 0)
        def _():
            qn_ref[...] = _qk_hat(q_ref, gq_ref, ATTN_SCALE)
            dl_ref[...] = jnp.sum(do_ref[...].astype(F32) * o_ref[...].astype(F32), axis=-1, keepdims=True)
            acc_ref[...] = jnp.zeros_like(acc_ref)
            dca_ref[...] = jnp.zeros_like(dca_ref)

        @pl.when(ki <= qi)
        def _():
            kn = _qk_hat(k_ref, gk_ref, 1.0)
            sb = _dot_nt(qn_ref[...], kn) + (cc_ref[...] - cr_ref[...])
            p = jnp.exp(_causal(sb, qi, ki, tq, tk) - lse_ref[...])
            dp = _dot_nt(do_ref[...], v_ref[...])
            ds = p * (dp - dl_ref[...])
            acc_ref[...] += _dot(ds.astype(BF16), kn)
            dca_ref[...] += jnp.sum(ds, axis=-1, keepdims=True)

        @pl.when(ki == qi)
        def _():
            dq, dg = _norm_bwd(q_ref[...].astype(F32), gq_ref[...], acc_ref[...], ATTN_SCALE)
            dq_ref[...] = dq.astype(BF16)
            dcc_ref[...] = dca_ref[...]
            dg_ref[...] += dg

    qspec = pl.BlockSpec((None, tq, hd), lambda h, i, j: (h, i, 0))
    kspec = pl.BlockSpec((None, tk, hd), lambda h, i, j: (h, jnp.minimum(i, j), 0))
    cspec = pl.BlockSpec((None, tq, 1), lambda h, i, j: (h, i, 0))
    gspec = pl.BlockSpec((1, hd), lambda h, i, j: (0, 0))
    return pl.pallas_call(
        body, grid=(hh, nq, nq),
        in_specs=[qspec, kspec, kspec, qspec, qspec, cspec, cspec,
                  pl.BlockSpec((None, 1, tk), lambda h, i, j: (h, 0, jnp.minimum(i, j))), gspec, gspec],
        out_specs=[qspec, cspec, gspec],
        out_shape=[jax.ShapeDtypeStruct((hh, s, hd), BF16), jax.ShapeDtypeStruct((hh, s, 1), F32),
                   jax.ShapeDtypeStruct((1, hd), F32)],
        scratch_shapes=[pltpu.VMEM((tq, hd), BF16), pltpu.VMEM((tq, 1), F32), pltpu.VMEM((tq, hd), F32),
                        pltpu.VMEM((tq, 1), F32)],
        compiler_params=_cparams(("arbitrary", "arbitrary", "arbitrary")), name=name)(
            q, k, v, o, do, lse, ccol, crow, gq, gk)


def attn_bwd_dkv(q, k, v, o, do, lse, ccol, crow, gq, gk, name):
    hh, s, hd = q.shape
    tq = tk = min(512, s)
    nq = s // tq

    def body(q_ref, k_ref, v_ref, o_ref, do_ref, lse_ref, cc_ref, cr_ref, gq_ref, gk_ref,
             dk_ref, dv_ref, dcr_ref, dg_ref, kn_ref, dka_ref, dva_ref, dca_ref):
        h, ki, qi = pl.program_id(0), pl.program_id(1), pl.program_id(2)

        @pl.when((h == 0) & (ki == 0) & (qi == 0))
        def _():
            dg_ref[...] = jnp.zeros_like(dg_ref)

        @pl.when(qi == 0)
        def _():
            kn_ref[...] = _qk_hat(k_ref, gk_ref, 1.0)
            dka_ref[...] = jnp.zeros_like(dka_ref)
            dva_ref[...] = jnp.zeros_like(dva_ref)
            dca_ref[...] = jnp.zeros_like(dca_ref)

        @pl.when(qi >= ki)
        def _():
            qn = _qk_hat(q_ref, gq_ref, ATTN_SCALE)
            do = do_ref[...]
            delta = jnp.sum(do.astype(F32) * o_ref[...].astype(F32), axis=-1, keepdims=True)
            sb = _dot_nt(qn, kn_ref[...]) + (cc_ref[...] - cr_ref[...])
            p = jnp.exp(_causal(sb, qi, ki, tq, tk) - lse_ref[...])
            dva_ref[...] += _dot_tn(p.astype(BF16), do)
            ds = p * (_dot_nt(do, v_ref[...]) - delta)
            dka_ref[...] += _dot_tn(ds.astype(BF16), qn)
            dca_ref[...] += jnp.sum(ds, axis=0, keepdims=True)

        @pl.when(qi == nq - 1)
        def _():
            dk, dg = _norm_bwd(k_ref[...].astype(F32), gk_ref[...], dka_ref[...], 1.0)
            dk_ref[...] = dk.astype(BF16)
            dv_ref[...] = dva_ref[...].astype(BF16)
            dcr_ref[...] = dca_ref[...]
            dg_ref[...] += dg

    kspec = pl.BlockSpec((None, tk, hd), lambda h, j, i: (h, j, 0))
    qspec = pl.BlockSpec((None, tq, hd), lambda h, j, i: (h, jnp.maximum(i, j), 0))
    cspec = pl.BlockSpec((None, tq, 1), lambda h, j, i: (h, jnp.maximum(i, j), 0))
    rspec = pl.BlockSpec((None, 1, tk), lambda h, j, i: (h, 0, j))
    gspec = pl.BlockSpec((1, hd), lambda h, j, i: (0, 0))
    return pl.pallas_call(
        body, grid=(hh, nq, nq),
        in_specs=[qspec, kspec, kspec, qspec, qspec, cspec, cspec, rspec, gspec, gspec],
        out_specs=[kspec, kspec, rspec, gspec],
        out_shape=[jax.ShapeDtypeStruct((hh, s, hd), BF16), jax.ShapeDtypeStruct((hh, s, hd), BF16),
                   jax.ShapeDtypeStruct((hh, 1, s), F32), jax.ShapeDtypeStruct((1, hd), F32)],
        scratch_shapes=[pltpu.VMEM((tk, hd), BF16), pltpu.VMEM((tk, hd), F32), pltpu.VMEM((tk, hd), F32),
                        pltpu.VMEM((1, tk), F32)],
        compiler_params=_cparams(("arbitrary", "arbitrary", "arbitrary")), name=name)(
            q, k, v, o, do, lse, ccol, crow, gq, gk)


def _pool_groups(tm):
    gid = lax.broadcasted_iota(jnp.int32, (1, D_POOL), 1) // (D_POOL // 4)
    win = jnp.where(gid == 0, 2.0, jnp.where(gid == 1, 4.0, jnp.where(gid == 2, 8.0, 16.0)))
    return gid, win


def _by_group(gid, v2, v4, v8, v16):
    return jnp.where(gid == 0, v2, jnp.where(gid == 1, v4, jnp.where(gid == 2, v8, v16)))


def _branches(rest_ref, halo_ref, a_ref, wa_ref, wc_ref, wp_ref, sc_ref, cw_ref, ti, tm):
    f = lambda v: v.astype(F32)
    cx, cb, cc, px = f(rest_ref[:, 0:256]), f(rest_ref[:, 256:512]), f(rest_ref[:, 512:768]), f(rest_ref[:, 768:1024])
    live = jnp.where(ti > 0, 1.0, 0.0)
    hz = f(halo_ref[:, 0:256]) * f(halo_ref[:, 512:768]) * live
    hp = f(halo_ref[:, 768:1024]) * live
    z = cc * cx
    zf = jnp.concatenate([hz, z], axis=0)
    z1 = pltpu.roll(zf, 1, 0)[HALO:]
    z2 = pltpu.roll(zf, 2, 0)[HALO:]
    cw = cw_ref[...]
    conv = cw[2:3] * z + cw[1:2] * z1 + cw[0:1] * z2
    uc = cb * conv
    pf = jnp.concatenate([hp, px], axis=0)
    s2 = pf + pltpu.roll(pf, 1, 0)
    s4 = s2 + pltpu.roll(s2, 2, 0)
    s8 = s4 + pltpu.roll(s4, 4, 0)
    s16 = s8 + pltpu.roll(s8, 8, 0)
    gid, win = _pool_groups(tm)
    t = (ti * tm + lax.broadcasted_iota(jnp.int32, (tm, 1), 0)).astype(F32)
    inv = 1.0 / jnp.minimum(t + 1.0, win)
    dpool = _by_group(gid, s2[HALO:], s4[HALO:], s8[HALO:], s16[HALO:]) * inv - px
    y_attn = _dot(a_ref[...], wa_ref[...])
    y_conv = _dot(uc.astype(BF16), wc_ref[...])
    y_pool_raw = _dot(dpool.astype(BF16), wp_ref[...])
    sg = [_sigmoid(f(rest_ref[:, 1024 + i * D_MODEL:1024 + (i + 1) * D_MODEL])) for i in range(3)]
    return dict(cx=cx, cb=cb, cc=cc, z=z, z1=z1, z2=z2, conv=conv, uc=uc, dpool=dpool, inv=inv, gid=gid,
                y_attn=y_attn, y_conv=y_conv, y_pool_raw=y_pool_raw, sg=sg, cw=cw)


def _mix_specs(tm, ti_of):
    blocks_per_tile = tm // HALO
    return [
        pl.BlockSpec((tm, N_REST), lambda i: (ti_of(i), 0)),
        pl.BlockSpec((HALO, 1024), lambda i: (jnp.maximum(ti_of(i) * blocks_per_tile - 1, 0), 0)),
        pl.BlockSpec((tm, D_ATTN), lambda i: (ti_of(i), 0)),
        pl.BlockSpec((D_ATTN, D_MODEL), lambda i: (0, 0)),
        pl.BlockSpec((D_CONV, D_MODEL), lambda i: (0, 0)),
        pl.BlockSpec((D_POOL, D_MODEL), lambda i: (0, 0)),
        pl.BlockSpec((1, D_MODEL), lambda i: (0, 0)),
        pl.BlockSpec((8, D_CONV), lambda i: (0, 0)),
    ]


def mix_fwd(proj, a, x, wa, wc, wp, scale, cw, wo, name):
    s = x.shape[0]
    tm = min(256, s)

    def body(rest_ref, halo_ref, a_ref, wa_ref, wc_ref, wp_ref, sc_ref, cw_ref, wo_ref, x_ref, o_ref):
        b = _branches(rest_ref, halo_ref, a_ref, wa_ref, wc_ref, wp_ref, sc_ref, cw_ref, pl.program_id(0), tm)
        merged = b["sg"][0] * b["y_attn"] + b["sg"][1] * b["y_conv"] + b["sg"][2] * (b["y_pool_raw"] * sc_ref[...])
        o_ref[...] = x_ref[...] + _dot(merged.astype(BF16), wo_ref[...])

    return pl.pallas_call(
        body, grid=(s // tm,),
        in_specs=_mix_specs(tm, lambda i: i) + [pl.BlockSpec((D_MODEL, D_MODEL), lambda i: (0, 0)),
                                                 pl.BlockSpec((tm, D_MODEL), lambda i: (i, 0))],
        out_specs=pl.BlockSpec((tm, D_MODEL), lambda i: (i, 0)),
        out_shape=jax.ShapeDtypeStruct((s, D_MODEL), F32),
        compiler_params=_cparams(("parallel",)), name=name)(proj, proj, a, wa, wc, wp, scale, cw, wo, x)


def mix_bwd(proj, a, dx1, wa, wc, wp, scale, cw, wot, wat, wct, wpt, name):
    s = dx1.shape[0]
    tm = min(256, s)
    nt = s // tm
    ti_of = lambda i: nt - 1 - i
    n = tm + HALO

    def body(rest_ref, halo_ref, a_ref, wa_ref, wc_ref, wp_ref, sc_ref, cw_ref, wot_ref, wat_ref, wct_ref, wpt_ref,
             dx_ref, dp_ref, da_ref, mg_ref, dya_ref, dyc_ref, dyp_ref, uc_ref, dd_ref, dsc_ref, dcw_ref,
             cdc_ref, cde_ref):
        i = pl.program_id(0)
        ti = ti_of(i)

        @pl.when(i == 0)
        def _():
            cdc_ref[...] = jnp.zeros_like(cdc_ref)
            cde_ref[...] = jnp.zeros_like(cde_ref)
            dsc_ref[...] = jnp.zeros_like(dsc_ref)
            dcw_ref[...] = jnp.zeros_like(dcw_ref)

        b = _branches(rest_ref, halo_ref, a_ref, wa_ref, wc_ref, wp_ref, sc_ref, cw_ref, ti, tm)
        sg, sc = b["sg"], sc_ref[...]
        y_pool = b["y_pool_raw"] * sc
        merged = sg[0] * b["y_attn"] + sg[1] * b["y_conv"] + sg[2] * y_pool
        mg_ref[...] = merged.astype(BF16)
        dm = _dot(dx_ref[...].astype(BF16), wot_ref[...])
        for j, y in enumerate((b["y_attn"], b["y_conv"], y_pool)):
            dp_ref[:, 1024 + j * D_MODEL:1024 + (j + 1) * D_MODEL] = (dm * y * sg[j] * (1.0 - sg[j])).astype(BF16)
        dya = (dm * sg[0]).astype(BF16)
        dya_ref[...] = dya
        da_ref[...] = _dot(dya, wat_ref[...]).astype(BF16)
        dyc = (dm * sg[1]).astype(BF16)
        dyc_ref[...] = dyc
        duc = _dot(dyc, wct_ref[...])
        dyp = dm * sg[2]
        dsc_ref[...] += jnp.sum(dyp * b["y_pool_raw"], axis=0, keepdims=True)
        dypr = (dyp * sc).astype(BF16)
        dyp_ref[...] = dypr
        ddp = _dot(dypr, wpt_ref[...])
        uc_ref[...] = b["uc"].astype(BF16)
        dd_ref[...] = b["dpool"].astype(BF16)

        dconv = duc * b["cb"]
        dp_ref[:, 256:512] = (duc * b["conv"]).astype(BF16)
        dcf = jnp.concatenate([dconv, cdc_ref[...]], axis=0)
        cw = b["cw"]
        dz = cw[2:3] * dconv + cw[1:2] * pltpu.roll(dcf, n - 1, 0)[:tm] + cw[0:1] * pltpu.roll(dcf, n - 2, 0)[:tm]
        dp_ref[:, 0:256] = (dz * b["cc"]).astype(BF16)
        dp_ref[:, 512:768] = (dz * b["cx"]).astype(BF16)
        dcw_ref[0:1, :] += jnp.sum(dconv * b["z2"], axis=0, keepdims=True)
        dcw_ref[1:2, :] += jnp.sum(dconv * b["z1"], axis=0, keepdims=True)
        dcw_ref[2:3, :] += jnp.sum(dconv * b["z"], axis=0, keepdims=True)
        cdc_ref[...] = dconv[:HALO]

        e = ddp * b["inv"]
        ef = jnp.concatenate([e, cde_ref[...]], axis=0)
        r2 = ef + pltpu.roll(ef, n - 1, 0)
        r4 = r2 + pltpu.roll(r2, n - 2, 0)
        r8 = r4 + pltpu.roll(r4, n - 4, 0)
        r16 = r8 + pltpu.roll(r8, n - 8, 0)
        dp_ref[:, 768:1024] = (_by_group(b["gid"], r2[:tm], r4[:tm], r8[:tm], r16[:tm]) - ddp).astype(BF16)
        cde_ref[...] = e[:HALO]

    tile = lambda w: pl.BlockSpec((tm, w), lambda i: (ti_of(i), 0))
    whole = lambda r, c: pl.BlockSpec((r, c), lambda i: (0, 0))
    bf = lambda w: jax.ShapeDtypeStruct((s, w), BF16)
    return pl.pallas_call(
        body, grid=(nt,),
        in_specs=_mix_specs(tm, ti_of) + [whole(D_MODEL, D_MODEL), whole(D_MODEL, D_ATTN), whole(D_MODEL, D_CONV),
                                          whole(D_MODEL, D_POOL), tile(D_MODEL)],
        out_specs=[tile(N_REST), tile(D_ATTN), tile(D_MODEL), tile(D_MODEL), tile(D_MODEL), tile(D_MODEL),
                   tile(D_CONV), tile(D_POOL), whole(1, D_MODEL), whole(8, D_CONV)],
        out_shape=[bf(N_REST), bf(D_ATTN), bf(D_MODEL), bf(D_MODEL), bf(D_MODEL), bf(D_MODEL), bf(D_CONV), bf(D_POOL),
                   jax.ShapeDtypeStruct((1, D_MODEL), F32), jax.ShapeDtypeStruct((8, D_CONV), F32)],
        scratch_shapes=[pltpu.VMEM((HALO, D_CONV), F32), pltpu.VMEM((HALO, D_POOL), F32)],
        compiler_params=_cparams(("arbitrary",)), name=name)(
            proj, proj, a, wa, wc, wp, scale, cw, wot, wat, wct, wpt, dx1)


def _adamw_math(w, g, m, v):
    m = ADAM_B1 * m + (1.0 - ADAM_B1) * g
    v = ADAM_B2 * v + (1.0 - ADAM_B2) * (g * g)
    m_hat = m / (1.0 - ADAM_B1 ** ADAM_STEP)
    v_hat = v / (1.0 - ADAM_B2 ** ADAM_STEP)
    delta = -ADAM_LR * (m_hat / (jnp.sqrt(v_hat) + ADAM_EPS) + ADAM_WD * w)
    return delta, m, v


def adamw_sum(parts, w, m, v, name):
    rows, cols = w.shape
    tr = min(256, rows)

    def body(p_ref, w_ref, m_ref, v_ref, g_ref, d_ref, nm_ref, nv_ref):
        g = p_ref[0].astype(F32)
        for i in range(1, N_DEV):
            g = g + p_ref[i].astype(F32)
        g_ref[...] = g
        d_ref[...], nm_ref[...], nv_ref[...] = _adamw_math(w_ref[...], g, m_ref[...], v_ref[...])

    spec = pl.BlockSpec((tr, cols), lambda i: (i, 0))
    return pl.pallas_call(
        body, grid=(rows // tr,),
        in_specs=[pl.BlockSpec((N_DEV, tr, cols), lambda i: (0, i, 0)), spec, spec, spec],
        out_specs=[spec] * 4, out_shape=[jax.ShapeDtypeStruct((rows, cols), F32)] * 4,
        compiler_params=_cparams(("parallel",)), name=name)(parts, w, m, v)


def _me():
    return lax.axis_index("x"), lax.axis_index("y"), lax.axis_index("c")


def all_gather(shard, name):
    any_spec = pl.BlockSpec(memory_space=pl.ANY)

    def body(x_ref, out_ref, send_sems, recv_sems, local_sem):
        x, y, c = _me()
        me, sibling = (x, y, c), (x, y, 1 - c)
        chips = [(1 - x, y), (x, 1 - y), (1 - x, 1 - y)]

        def slot(px, py, pc):
            return out_ref.at[4 * px + 2 * py + pc]

        def copy(k, block, to, src=None):
            return pltpu.make_async_remote_copy(
                src_ref=slot(*block) if src is None else src, dst_ref=slot(*block),
                send_sem=send_sems.at[k], recv_sem=recv_sems.at[k], device_id=to, device_id_type=pl.DeviceIdType.MESH)

        mine = pltpu.make_async_copy(x_ref, slot(*me), local_sem)
        mine.start()
        first = [copy(0, me, sibling, src=x_ref)]
        first += [copy(1 + j, me, (*chip, c), src=x_ref) for j, chip in enumerate(chips)]
        for cp in first:
            cp.start()
        passed = [copy(4 + j, (*chip, c), sibling) for j, chip in enumerate(chips)]
        for j, chip in enumerate(chips):
            copy(1 + j, (*chip, c), me).wait_recv()
            passed[j].start()
        copy(0, sibling, me).wait_recv()
        for j, chip in enumerate(chips):
            copy(4 + j, (*chip, 1 - c), me).wait_recv()
        for cp in first + passed:
            cp.wait_send()
        mine.wait()

    return pl.pallas_call(
        body, out_shape=jax.ShapeDtypeStruct((N_DEV,) + shard.shape, shard.dtype),
        in_specs=[any_spec], out_specs=any_spec,
        scratch_shapes=[pltpu.SemaphoreType.DMA((7,)), pltpu.SemaphoreType.DMA((7,)), pltpu.SemaphoreType.DMA],
        name=name)(shard)


def all_to_all(blocks, name):
    any_spec = pl.BlockSpec(memory_space=pl.ANY)

    def body(src_ref, out_ref, send_sems, recv_sems, local_sem):
        x, y, c = _me()
        me = 4 * x + 2 * y + c
        mine = pltpu.make_async_copy(src_ref.at[me], out_ref.at[me], local_sem)
        mine.start()
        copies = []
        for k in range(1, N_DEV):
            px, py, pc = x ^ ((k >> 2) & 1), y ^ ((k >> 1) & 1), c ^ (k & 1)
            peer = 4 * px + 2 * py + pc
            copies.append((pltpu.make_async_remote_copy(
                src_ref=src_ref.at[peer], dst_ref=out_ref.at[me], send_sem=send_sems.at[k - 1],
                recv_sem=recv_sems.at[k - 1], device_id=(px, py, pc), device_id_type=pl.DeviceIdType.MESH), peer))
        for cp, _ in copies:
            cp.start()
        for k, (cp, peer) in enumerate(copies):
            pltpu.make_async_remote_copy(
                src_ref=src_ref.at[peer], dst_ref=out_ref.at[peer], send_sem=send_sems.at[k],
                recv_sem=recv_sems.at[k], device_id=(x, y, c), device_id_type=pl.DeviceIdType.MESH).wait_recv()
        for cp, _ in copies:
            cp.wait_send()
        mine.wait()

    return pl.pallas_call(
        body, out_shape=jax.ShapeDtypeStruct(blocks.shape, blocks.dtype),
        in_specs=[any_spec], out_specs=any_spec,
        scratch_shapes=[pltpu.SemaphoreType.DMA((7,)), pltpu.SemaphoreType.DMA((7,)), pltpu.SemaphoreType.DMA],
        name=name)(blocks)


MATRICES = ("w_in", "w_attn_out", "w_conv_out", "pool_w", "w_o", "w_ffn_in", "w_ffn_out")
SHARD_INFO = {
    "w_in": ((DEPTH, D_MODEL, D_IN // N_DEV), 2),
    "w_attn_out": ((DEPTH, D_ATTN, D_MODEL // N_DEV), 2),
    "w_conv_out": ((DEPTH, D_CONV, D_MODEL // N_DEV), 2),
    "pool_w": ((DEPTH, 4, 64, 256 // N_DEV), 3),
    "w_o": ((DEPTH, D_MODEL // N_DEV, D_MODEL), 1),
    "w_ffn_in": ((DEPTH, D_MODEL, 2 * D_FF // N_DEV), 2),
    "w_ffn_out": ((DEPTH, D_FF // N_DEV, D_MODEL), 1),
}
VECTORS = ("norm_mix_g", "forget_b", "q_norm_g", "k_norm_g", "pool_scale", "norm_ffn_g")
VECTOR_SHAPES = {"norm_mix_g": (DEPTH, D_MODEL), "forget_b": (DEPTH, HEADS), "q_norm_g": (DEPTH, HEAD_DIM),
                 "k_norm_g": (DEPTH, HEAD_DIM), "pool_scale": (DEPTH, D_MODEL), "norm_ffn_g": (DEPTH, D_MODEL)}
CONV_W_FULL = (DEPTH, 3, D_CONV)


def _size(shape):
    n = 1
    for v in shape:
        n *= v
    return n


def _pack(arrays, rows, cols):
    flat = jnp.concatenate([a.reshape(-1) for a in arrays])
    return jnp.pad(flat, (0, rows * cols - flat.shape[0])).reshape(rows, cols)


def _unpack(packed, shapes):
    flat, out, off = packed.reshape(-1), [], 0
    for shp in shapes:
        out.append(flat[off:off + _size(shp)].reshape(shp))
        off += _size(shp)
    return out


def _join_shards(stacked, axis):
    moved = jnp.moveaxis(stacked, 0, axis)
    shp = list(moved.shape)
    shp[axis:axis + 2] = [shp[axis] * shp[axis + 1]]
    return moved.reshape(shp)


def _cut_shards(full, axis):
    shp = list(full.shape)
    shp[axis:axis + 1] = [N_DEV, shp[axis] // N_DEV]
    return jnp.moveaxis(full.reshape(shp), axis, 0)


def _regroup_w_in(w):
    pad = jnp.zeros((w.shape[0], N_FULL - N_MAIN - HEADS), w.dtype)
    return jnp.concatenate([w[:, 1544:2568], w[:, 2568:5640], w[:, 0:1536], w[:, 1536:1544], pad], axis=1)


def _ungroup_w_in(wp):
    return jnp.concatenate([wp[:, 4096:5632], wp[:, 5632:5640], wp[:, 0:1024], wp[:, 1024:4096]], axis=1)


def _interleave_ffn(w):
    d = w.shape[0]
    return jnp.stack([w[:, :D_FF].reshape(d, N_FF_BLKS, FF_BLK), w[:, D_FF:].reshape(d, N_FF_BLKS, FF_BLK)],
                     axis=2).reshape(d, 2 * D_FF)


def _deinterleave_ffn(wp):
    d = wp.shape[0]
    t = wp.reshape(d, N_FF_BLKS, 2, FF_BLK)
    return jnp.concatenate([t[:, :, 0].reshape(d, D_FF), t[:, :, 1].reshape(d, D_FF)], axis=1)


def _pool_block_diag(w):
    out = jnp.zeros((D_POOL, D_MODEL), w.dtype)
    for g in range(4):
        out = lax.dynamic_update_slice(out, w[g], (g * 64, g * 256))
    return out


def _pool_from_block_diag(wbd):
    return jnp.stack([wbd[g * 64:(g + 1) * 64, g * 256:(g + 1) * 256] for g in range(4)])


def _layer_weights(full, vec, conv_w, l):
    w_in = _regroup_w_in(full["w_in"][l])
    w_ffn_in = _interleave_ffn(full["w_ffn_in"][l])
    wp = _pool_block_diag(full["pool_w"][l])
    row = lambda v: v.reshape(1, -1)
    fb = jnp.zeros((1, 128), F32).at[0, :HEADS].set(vec["forget_b"][l])
    cw = jnp.zeros((8, D_CONV), F32).at[:3].set(conv_w[l])
    return dict(
        w_in=w_in, w_in_t=w_in.T, w_f=w_in[:, N_MAIN:], w_ffn_in=w_ffn_in, w_ffn_in_t=w_ffn_in.T,
        w_ffn_out=full["w_ffn_out"][l], w_ffn_out_t=full["w_ffn_out"][l].T,
        wa=full["w_attn_out"][l], wa_t=full["w_attn_out"][l].T, wc=full["w_conv_out"][l], wc_t=full["w_conv_out"][l].T,
        wp=wp, wp_t=wp.T, wo=full["w_o"][l], wo_t=full["w_o"][l].T,
        g_mix=row(vec["norm_mix_g"][l]), g_ffn=row(vec["norm_ffn_g"][l]), gq=row(vec["q_norm_g"][l]),
        gk=row(vec["k_norm_g"][l]), scale=row(vec["pool_scale"][l]), fb=fb, cw=cw)


def _heads(t):
    s = t.shape[0]
    return t.reshape(s, HEADS, HEAD_DIM).transpose(1, 0, 2)


def _layer_fwd(x, w, l):
    proj, h = norm_matmul(x, w["g_mix"], w["w_in"], N_MAIN, f"in_proj_{l}")
    z, c = forget_fwd(h, w["w_f"], w["fb"], f"forget_fwd_{l}")
    q, k, v = (_heads(proj[:, N_REST + i * D_ATTN:N_REST + (i + 1) * D_ATTN]) for i in range(3))
    ch = c[:, :HEADS].T
    ccol, crow = ch[:, :, None], ch[:, None, :]
    o, lse = attn_fwd(q, k, v, ccol, crow, w["gq"], w["gk"], f"attn_fwd_{l}")
    a = o.transpose(1, 0, 2).reshape(x.shape[0], D_ATTN)
    x1 = mix_fwd(proj, a, x, w["wa"], w["wc"], w["wp"], w["scale"], w["cw"], w["wo"], f"mix_fwd_{l}")
    gu, h2 = norm_matmul(x1, w["g_ffn"], w["w_ffn_in"], 2 * D_FF, f"ffn_in_{l}")
    x2 = swiglu_matmul(gu, w["w_ffn_out"], x1, f"ffn_out_{l}")
    saved = dict(x=x, proj=proj, h=h, z=z, q=q, k=k, v=v, ccol=ccol, crow=crow, o=o, lse=lse, a=a, x1=x1, gu=gu, h2=h2)
    return x2, saved


def _layer_bwd(dx2, sv, w, l):
    s = dx2.shape[0]
    g = {}
    dgu, act = swiglu_bwd(dx2, sv["gu"], w["w_ffn_out_t"], f"ffn_out_bwd_{l}")
    g["w_ffn_out"] = tn_matmul(act, dx2, f"dw_ffn_out_{l}")
    g["w_ffn_in"] = _deinterleave_ffn(tn_matmul(sv["h2"], dgu, f"dw_ffn_in_{l}"))
    dx1, dg = matmul_normbwd(dgu, w["w_ffn_in_t"], sv["x1"], w["g_ffn"], dx2, f"ffn_in_bwd_{l}")
    g["norm_ffn_g"] = dg[0]

    (dprest, da, merged, dya, dyc, dyp, uc, dd, dscale, dcw) = mix_bwd(
        sv["proj"], sv["a"], dx1, w["wa"], w["wc"], w["wp"], w["scale"], w["cw"],
        w["wo_t"], w["wa_t"], w["wc_t"], w["wp_t"], f"mix_bwd_{l}")
    g["w_o"] = tn_matmul(merged, dx1, f"dw_o_{l}")
    g["w_attn_out"] = tn_matmul(sv["a"], dya, f"dw_attn_out_{l}")
    g["w_conv_out"] = tn_matmul(uc, dyc, f"dw_conv_out_{l}")
    g["pool_w"] = _pool_from_block_diag(tn_matmul(dd, dyp, f"dw_pool_{l}"))
    g["pool_scale"] = dscale[0]
    g["conv_w"] = dcw[:3]

    do = _heads(da)
    att = (sv["q"], sv["k"], sv["v"], sv["o"], do, sv["lse"], sv["ccol"], sv["crow"], w["gq"], w["gk"])
    dq, dccol, dgq = attn_bwd_dq(*att, f"attn_dq_{l}")
    dk, dv, dcrow, dgk = attn_bwd_dkv(*att, f"attn_dkv_{l}")
    g["q_norm_g"], g["k_norm_g"] = dgq[0], dgk[0]
    dc = jnp.pad((dccol[:, :, 0] - dcrow[:, 0, :]).T, ((0, 0), (0, 128 - HEADS)))
    dz, db = forget_bwd(dc, sv["z"], f"forget_bwd_{l}")
    g["forget_b"] = db[0, :HEADS]

    tok = lambda t: t.transpose(1, 0, 2).reshape(s, D_ATTN)
    dproj = jnp.concatenate([dprest, tok(dq), tok(dk), tok(dv), dz], axis=1)
    g["w_in"] = _ungroup_w_in(tn_matmul(sv["h"], dproj, f"dw_in_{l}"))
    dx, dg = matmul_normbwd(dproj, w["w_in_t"], sv["x"], w["g_mix"], dx1, f"in_proj_bwd_{l}")
    g["norm_mix_g"] = dg[0]
    return dx, g


def _local_step(x, tgt, full, vec, conv_w):
    ws, saved = [], []
    for l in range(DEPTH):
        ws.append(_layer_weights(full, vec, conv_w, l))
        x, sv = _layer_fwd(x, ws[l], l)
        saved.append(sv)
    sq, dx = loss_kernel(x, tgt, "loss")
    grads = [None] * DEPTH
    for l in reversed(range(DEPTH)):
        dx, grads[l] = _layer_bwd(dx, saved[l], ws[l], l)
    stacked = {n: jnp.stack([grads[l][n] for l in range(DEPTH)]) for n in grads[0]}
    return sq[0, 0], dx, stacked


def kernel(x, norm_mix_g, w_in, forget_b, q_norm_g, k_norm_g, w_attn_out, conv_w, w_conv_out, pool_w, pool_scale, w_o, norm_ffn_g, w_ffn_in, w_ffn_out, loss_target, m_norm_mix_g, m_w_in, m_forget_b, m_q_norm_g, m_k_norm_g, m_w_attn_out, m_conv_w, m_w_conv_out, m_pool_w, m_pool_scale, m_w_o, m_norm_ffn_g, m_w_ffn_in, m_w_ffn_out, v_norm_mix_g, v_w_in, v_forget_b, v_q_norm_g, v_k_norm_g, v_w_attn_out, v_conv_w, v_w_conv_out, v_pool_w, v_pool_scale, v_w_o, v_norm_ffn_g, v_w_ffn_in, v_w_ffn_out):
    w = dict(norm_mix_g=norm_mix_g, w_in=w_in, forget_b=forget_b, q_norm_g=q_norm_g, k_norm_g=k_norm_g,
             w_attn_out=w_attn_out, conv_w=conv_w, w_conv_out=w_conv_out, pool_w=pool_w, pool_scale=pool_scale,
             w_o=w_o, norm_ffn_g=norm_ffn_g, w_ffn_in=w_ffn_in, w_ffn_out=w_ffn_out)
    m = dict(norm_mix_g=m_norm_mix_g, w_in=m_w_in, forget_b=m_forget_b, q_norm_g=m_q_norm_g, k_norm_g=m_k_norm_g,
             w_attn_out=m_w_attn_out, conv_w=m_conv_w, w_conv_out=m_w_conv_out, pool_w=m_pool_w,
             pool_scale=m_pool_scale, w_o=m_w_o, norm_ffn_g=m_norm_ffn_g, w_ffn_in=m_w_ffn_in, w_ffn_out=m_w_ffn_out)
    v = dict(norm_mix_g=v_norm_mix_g, w_in=v_w_in, forget_b=v_forget_b, q_norm_g=v_q_norm_g, k_norm_g=v_k_norm_g,
             w_attn_out=v_w_attn_out, conv_w=v_conv_w, w_conv_out=v_w_conv_out, pool_w=v_pool_w,
             pool_scale=v_pool_scale, w_o=v_w_o, norm_ffn_g=v_norm_ffn_g, w_ffn_in=v_w_ffn_in, w_ffn_out=v_w_ffn_out)
    me = 4 * lax.axis_index("x") + 2 * lax.axis_index("y") + lax.axis_index("c")
    shard_shapes = [SHARD_INFO[n][0] for n in MATRICES]

    gathered = all_gather(_pack([w[n].astype(BF16) for n in MATRICES], PACK_ROWS, PACK_COLS), "gather_matrices")
    per_dev = [_unpack(gathered[i], shard_shapes) for i in range(N_DEV)]
    full = {n: _join_shards(jnp.stack([per_dev[i][j] for i in range(N_DEV)]), SHARD_INFO[n][1])
            for j, n in enumerate(MATRICES)}
    conv_g = all_gather(_pack([conv_w], 8, 128), "gather_conv_w")
    conv_full = _join_shards(jnp.stack([_unpack(conv_g[i], [conv_w.shape])[0] for i in range(N_DEV)]), 2)
    vec = {n: w[n] for n in VECTORS}

    sq, dx, g = _local_step(x[0], loss_target[0], full, vec, conv_full)
    loss = lax.psum(0.5 * sq / D_MODEL, ("x", "y", "c"))

    cut = [_cut_shards(g[n], SHARD_INFO[n][1]) for n in MATRICES]
    blocks = jnp.stack([_pack([c[i].astype(BF16) for c in cut], PACK_ROWS, PACK_COLS) for i in range(N_DEV)])
    parts = all_to_all(blocks, "exchange_matrix_grads")
    packed = [_pack([d[n] for n in MATRICES], PACK_ROWS, PACK_COLS) for d in (w, m, v)]
    big = [_unpack(t, shard_shapes) for t in adamw_sum(parts, *packed, "adamw_matrices")]

    small_shapes = [VECTOR_SHAPES[n] for n in VECTORS] + [CONV_W_FULL]
    sparts = all_gather(_pack([g[n] for n in VECTORS] + [g["conv_w"]], SMALL_ROWS, 128), "gather_vector_grads")
    col0 = me * (D_CONV // N_DEV)
    place = lambda t: lax.dynamic_update_slice(jnp.zeros(CONV_W_FULL, F32), t, (0, 0, col0))
    spacked = [_pack([d[n] for n in VECTORS] + [place(d["conv_w"])], SMALL_ROWS, 128) for d in (w, m, v)]
    small = [_unpack(t, small_shapes) for t in adamw_sum(sparts, *spacked, "adamw_vectors")]

    def result(kind):
        out = {n: big[kind][j] for j, n in enumerate(MATRICES)}
        out.update({n: small[kind][j] for j, n in enumerate(VECTORS)})
        out["conv_w"] = lax.dynamic_slice(small[kind][len(VECTORS)], (0, 0, col0), conv_w.shape)
        return [out[n] for n in w]

    return (loss, dx[None], *result(0), *result(1), *result(2), *result(3))
```

```python
import functools

import jax
import jax.numpy as jnp
from jax import lax
from jax.experimental import pallas as pl
from jax.experimental.pallas import tpu as pltpu

F32 = jnp.float32
BF16 = jnp.bfloat16

N_DEV = 8
DEPTH = 4
D_MODEL = 1024
HEAD_DIM = 64
HEADS = 8
D_ATTN = 512
D_CONV = 256
D_POOL = 256
D_FF = 2816
D_IN = 5640
EPS = 1e-6
ATTN_SCALE = HEAD_DIM ** -0.5

N_REST = 4096
N_MAIN = 5632
N_FULL = 5760
DPROJ_TAIL = 2048
DPROJ_COLS = N_REST + DPROJ_TAIL
FF_BLK = 256
N_FF_BLKS = D_FF // FF_BLK
HALO = 16

ADAM_LR = 0.001
ADAM_B1 = 0.9
ADAM_B2 = 0.999
ADAM_EPS = 1e-08
ADAM_WD = 0.01
ADAM_STEP = 10

PACK_COLS = 1024
PACK_ROWS = 8192
SMALL_ROWS = 128

VMEM_LIMIT = 48 * 2 ** 20


def _cparams(sem, vmem=None):
    return pltpu.CompilerParams(dimension_semantics=sem, vmem_limit_bytes=vmem or VMEM_LIMIT)


def _pick(n, cands):
    for c in cands:
        if n % c == 0:
            return c
    raise ValueError(f"no tile for {n}")


def _sigmoid(v):
    return 1.0 / (1.0 + jnp.exp(-v))


def _rstd(v):
    return lax.rsqrt(jnp.mean(v * v, axis=-1, keepdims=True) + EPS)


def _dot(a, b):
    return jnp.dot(a, b, preferred_element_type=F32)


def _dot_tn(a, b):
    return lax.dot_general(a, b, (((0,), (0,)), ((), ())), preferred_element_type=F32)


def _dot_nt(a, b):
    return lax.dot_general(a, b, (((1,), (1,)), ((), ())), preferred_element_type=F32)


def norm_matmul(x, g, w, n_cols, name):
    s, d = x.shape
    tm, tn = min(512, s), 512

    def body(x_ref, g_ref, w_ref, o_ref, h_ref):
        @pl.when(pl.program_id(1) == 0)
        def _():
            xv = x_ref[...]
            h_ref[...] = (xv * _rstd(xv) * g_ref[...]).astype(BF16)

        o_ref[...] = _dot(h_ref[...], w_ref[...]).astype(BF16)

    return pl.pallas_call(
        body, grid=(s // tm, n_cols // tn),
        in_specs=[pl.BlockSpec((tm, d), lambda i, j: (i, 0)), pl.BlockSpec((1, d), lambda i, j: (0, 0)),
                  pl.BlockSpec((d, tn), lambda i, j: (0, j))],
        out_specs=[pl.BlockSpec((tm, tn), lambda i, j: (i, j)), pl.BlockSpec((tm, d), lambda i, j: (i, 0))],
        out_shape=[jax.ShapeDtypeStruct((s, n_cols), BF16), jax.ShapeDtypeStruct((s, d), BF16)],
        compiler_params=_cparams(("parallel", "arbitrary")), name=name)(x, g, w)


def tn_matmul(a, b, name, n_cols=None):
    t, m = a.shape
    n = n_cols or b.shape[1]
    tk = min(512, t)
    tmm = _pick(m, (1024, 1408, 512, 256))
    tn = _pick(n, (1152, 1024, 512, 128))

    nk = t // tk

    def body(a_ref, b_ref, o_ref, acc_ref):
        @pl.when(pl.program_id(2) == 0)
        def _():
            acc_ref[...] = jnp.zeros_like(acc_ref)

        acc_ref[...] += _dot_tn(a_ref[...].astype(BF16), b_ref[...].astype(BF16))

        @pl.when(pl.program_id(2) == nk - 1)
        def _():
            o_ref[...] = acc_ref[...].astype(BF16)

    return pl.pallas_call(
        body, grid=(m // tmm, n // tn, nk),
        in_specs=[pl.BlockSpec((tk, tmm), lambda i, j, k: (k, i)), pl.BlockSpec((tk, tn), lambda i, j, k: (k, j))],
        out_specs=pl.BlockSpec((tmm, tn), lambda i, j, k: (i, j)),
        out_shape=jax.ShapeDtypeStruct((m, n), BF16), scratch_shapes=[pltpu.VMEM((tmm, tn), F32)],
        compiler_params=_cparams(("parallel", "parallel", "arbitrary")), name=name)(a, b)


def matmul_normbwd(a, w, x, g, dres, name, k=None):
    s = a.shape[0]
    k = k or a.shape[1]
    d = w.shape[0]
    tm = min(512, s)
    tk = _pick(k, (1152, 512))
    nk = k // tk

    def body(a_ref, w_ref, x_ref, g_ref, r_ref, dx_ref, dg_ref, acc_ref):
        i, kk = pl.program_id(0), pl.program_id(1)

        @pl.when(kk == 0)
        def _():
            acc_ref[...] = jnp.zeros_like(acc_ref)

        @pl.when((i == 0) & (kk == 0))
        def _():
            dg_ref[...] = jnp.zeros_like(dg_ref)

        acc_ref[...] += _dot_nt(a_ref[...], w_ref[...])

        @pl.when(kk == nk - 1)
        def _():
            xv = x_ref[...]
            r = _rstd(xv)
            y = xv * r
            dh = acc_ref[...]
            dy = dh * g_ref[...]
            dx_ref[...] = r_ref[...] + r * (dy - y * jnp.mean(dy * y, axis=-1, keepdims=True))
            dg_ref[...] += jnp.sum(dh * y, axis=0, keepdims=True)

    return pl.pallas_call(
        body, grid=(s // tm, nk),
        in_specs=[pl.BlockSpec((tm, tk), lambda i, kk: (i, kk)), pl.BlockSpec((d, tk), lambda i, kk: (0, kk)),
                  pl.BlockSpec((tm, d), lambda i, kk: (i, 0)), pl.BlockSpec((1, d), lambda i, kk: (0, 0)),
                  pl.BlockSpec((tm, d), lambda i, kk: (i, 0))],
        out_specs=[pl.BlockSpec((tm, d), lambda i, kk: (i, 0)), pl.BlockSpec((1, d), lambda i, kk: (0, 0))],
        out_shape=[jax.ShapeDtypeStruct((s, d), F32), jax.ShapeDtypeStruct((1, d), F32)],
        scratch_shapes=[pltpu.VMEM((tm, d), F32)],
        compiler_params=_cparams(("arbitrary", "arbitrary")), name=name)(a, w, x, g, dres)


def swiglu_matmul(gu, w, x1, name):
    s = gu.shape[0]
    d = w.shape[1]
    tm = min(512, s)

    def body(gu_ref, w_ref, x_ref, o_ref):
        @pl.when(pl.program_id(1) == 0)
        def _():
            o_ref[...] = x_ref[...]

        gt = gu_ref[:, :FF_BLK].astype(F32)
        up = gu_ref[:, FF_BLK:].astype(F32)
        act = (gt * _sigmoid(gt) * up).astype(BF16)
        o_ref[...] += _dot(act, w_ref[...])

    return pl.pallas_call(
        body, grid=(s // tm, N_FF_BLKS),
        in_specs=[pl.BlockSpec((tm, 2 * FF_BLK), lambda i, j: (i, j)), pl.BlockSpec((FF_BLK, d), lambda i, j: (j, 0)),
                  pl.BlockSpec((tm, d), lambda i, j: (i, 0))],
        out_specs=pl.BlockSpec((tm, d), lambda i, j: (i, 0)),
        out_shape=jax.ShapeDtypeStruct((s, d), F32),
        compiler_params=_cparams(("parallel", "arbitrary")), name=name)(gu, w, x1)


def swiglu_bwd(dx2, gu, w, name):
    s, d = dx2.shape
    tm = min(512, s)

    def body(dx_ref, gu_ref, w_ref, dgu_ref, act_ref):
        dact = _dot_nt(dx_ref[...].astype(BF16), w_ref[...])
        gt = gu_ref[:, :FF_BLK].astype(F32)
        up = gu_ref[:, FF_BLK:].astype(F32)
        sg = _sigmoid(gt)
        act_ref[...] = (gt * sg * up).astype(BF16)
        dgu_ref[:, :FF_BLK] = (dact * up * (sg * (1.0 + gt * (1.0 - sg)))).astype(BF16)
        dgu_ref[:, FF_BLK:] = (dact * gt * sg).astype(BF16)

    return pl.pallas_call(
        body, grid=(s // tm, N_FF_BLKS),
        in_specs=[pl.BlockSpec((tm, d), lambda i, j: (i, 0)), pl.BlockSpec((tm, 2 * FF_BLK), lambda i, j: (i, j)),
                  pl.BlockSpec((FF_BLK, d), lambda i, j: (j, 0))],
        out_specs=[pl.BlockSpec((tm, 2 * FF_BLK), lambda i, j: (i, j)), pl.BlockSpec((tm, FF_BLK), lambda i, j: (i, j))],
        out_shape=[jax.ShapeDtypeStruct((s, 2 * D_FF), BF16), jax.ShapeDtypeStruct((s, D_FF), BF16)],
        compiler_params=_cparams(("parallel", "arbitrary")), name=name)(dx2, gu, w)


def loss_kernel(y, tgt, name):
    s, d = y.shape
    tm = min(512, s)

    def body(y_ref, t_ref, l_ref, dy_ref):
        @pl.when(pl.program_id(0) == 0)
        def _():
            l_ref[...] = jnp.zeros_like(l_ref)

        err = y_ref[...] - t_ref[...]
        dy_ref[...] = err * (1.0 / d)
        l_ref[...] += jnp.sum(jnp.sum(err * err, axis=1, keepdims=True), axis=0, keepdims=True)

    return pl.pallas_call(
        body, grid=(s // tm,),
        in_specs=[pl.BlockSpec((tm, d), lambda i: (i, 0)), pl.BlockSpec((tm, d), lambda i: (i, 0))],
        out_specs=[pl.BlockSpec((8, 128), lambda i: (0, 0)), pl.BlockSpec((tm, d), lambda i: (i, 0))],
        out_shape=[jax.ShapeDtypeStruct((8, 128), F32), jax.ShapeDtypeStruct((s, d), F32)],
        compiler_params=_cparams(("arbitrary",)), name=name)(y, tgt)


def _split3(v):
    a1 = v.astype(BF16)
    r1 = v - a1.astype(F32)
    a2 = r1.astype(BF16)
    a3 = (r1 - a2.astype(F32)).astype(BF16)
    return a1, a2, a3


def forget_fwd(h, wf, b, name):
    s, d = h.shape
    tm = min(512, s)

    def body(h_ref, w_ref, b_ref, z_ref, c_ref, carry_ref):
        @pl.when(pl.program_id(0) == 0)
        def _():
            carry_ref[...] = jnp.zeros_like(carry_ref)

        z = _dot(h_ref[...], w_ref[...]) + b_ref[...]
        z_ref[...] = z
        logf = jnp.minimum(z, 0.0) - jnp.log(1.0 + jnp.exp(-jnp.abs(z)))
        row = lax.broadcasted_iota(jnp.int32, (tm, tm), 0)
        col = lax.broadcasted_iota(jnp.int32, (tm, tm), 1)
        tri = (row >= col).astype(BF16)
        a1, a2, a3 = _split3(logf)
        c = _dot(tri, a1) + _dot(tri, a2) + _dot(tri, a3) + carry_ref[...]
        c_ref[...] = c
        carry_ref[...] = c[tm - 1:tm, :]

    return pl.pallas_call(
        body, grid=(s // tm,),
        in_specs=[pl.BlockSpec((tm, d), lambda i: (i, 0)), pl.BlockSpec((d, 128), lambda i: (0, 0)),
                  pl.BlockSpec((1, 128), lambda i: (0, 0))],
        out_specs=[pl.BlockSpec((tm, 128), lambda i: (i, 0)), pl.BlockSpec((tm, 128), lambda i: (i, 0))],
        out_shape=[jax.ShapeDtypeStruct((s, 128), F32), jax.ShapeDtypeStruct((s, 128), F32)],
        scratch_shapes=[pltpu.VMEM((1, 128), F32)],
        compiler_params=_cparams(("arbitrary",)), name=name)(h, wf, b)


def forget_bwd(dc, z, dproj, name):
    s = dc.shape[0]
    tm = min(512, s)
    nt = s // tm

    def body(dc_ref, z_ref, dp_ref, dz_ref, db_ref, carry_ref):
        @pl.when(pl.program_id(0) == 0)
        def _():
            carry_ref[...] = jnp.zeros_like(carry_ref)
            db_ref[...] = jnp.zeros_like(db_ref)

        row = lax.broadcasted_iota(jnp.int32, (tm, tm), 0)
        col = lax.broadcasted_iota(jnp.int32, (tm, tm), 1)
        tri = (col >= row).astype(BF16)
        a1, a2, a3 = _split3(dc_ref[...])
        dlogf = _dot(tri, a1) + _dot(tri, a2) + _dot(tri, a3) + carry_ref[...]
        carry_ref[...] = dlogf[0:1, :]
        dz = dlogf * (1.0 - _sigmoid(z_ref[...]))
        dz_ref[...] = dz.astype(BF16)
        db_ref[...] += jnp.sum(dz, axis=0, keepdims=True)

    return pl.pallas_call(
        body, grid=(nt,),
        in_specs=[pl.BlockSpec((tm, 128), lambda i: (nt - 1 - i, 0)), pl.BlockSpec((tm, 128), lambda i: (nt - 1 - i, 0)),
                  pl.BlockSpec(memory_space=pl.ANY)],
        out_specs=[pl.BlockSpec((tm, 128), lambda i: (nt - 1 - i, N_MAIN // 128)), pl.BlockSpec((1, 128), lambda i: (0, 0))],
        out_shape=[jax.ShapeDtypeStruct(dproj.shape, BF16), jax.ShapeDtypeStruct((1, 128), F32)],
        scratch_shapes=[pltpu.VMEM((1, 128), F32)], input_output_aliases={2: 0},
        compiler_params=_cparams(("arbitrary",)), name=name)(dc, z, dproj)


HEAD_GROUP = 2
LANE_C = 64
LANE_ONE = 67


def _lanes():
    lane = lax.broadcasted_iota(jnp.int32, (1, 128), 1)
    return lane, lane < HEAD_DIM


def _half_mean(t, lo):
    s_lo = jnp.sum(jnp.where(lo, t, 0.0), axis=-1, keepdims=True)
    s_hi = jnp.sum(jnp.where(lo, 0.0, t), axis=-1, keepdims=True)
    return jnp.where(lo, s_lo, s_hi) * (1.0 / HEAD_DIM)


def _lane_col(t, lane, idx):
    return jnp.sum(jnp.where(lane == idx, t, 0.0), axis=-1, keepdims=True)


def _swap_halves(t):
    return pltpu.roll(t, HEAD_DIM, 1)


def _causal(s_blk, tq, tk):
    row = lax.broadcasted_iota(jnp.int32, (tq, tk), 0)
    col = lax.broadcasted_iota(jnp.int32, (tq, tk), 1)
    return jnp.where(row >= col, s_blk, -jnp.inf)


def attn_prep(proj, c, gq2, gk2, name):
    s = proj.shape[0]
    tm = min(512, s)
    first = N_REST // 128

    def body(q_ref, k_ref, v_ref, c_ref, gq_ref, gk_ref, qa_ref, ka_ref, va_ref):
        j = pl.program_id(1)
        lane, lo = _lanes()

        def normed(ref, g):
            t = ref[...].astype(F32)
            return t * lax.rsqrt(_half_mean(t * t, lo) + EPS) * g

        qn = normed(q_ref, gq_ref[...] * ATTN_SCALE)
        kn = normed(k_ref, gk_ref[...])
        vv = v_ref[...].astype(F32)
        cv = c_ref[...]
        one_q = jnp.where((lane >= LANE_ONE) & (lane < LANE_ONE + 3), 1.0, 0.0)
        one_k = jnp.where((lane >= LANE_C) & (lane < LANE_C + 3), 1.0, 0.0)
        one_v = jnp.where(lane == LANE_C, 1.0, 0.0)
        for e in range(2):
            pick = (lambda t: t) if e == 0 else _swap_halves
            pieces = [p.astype(F32) for p in _split3(_lane_col(cv, lane, 2 * j + e))]
            ext_q, ext_k = one_q, one_k
            for i, p in enumerate(pieces):
                ext_q = jnp.where(lane == LANE_C + i, p, ext_q)
                ext_k = jnp.where(lane == LANE_ONE + i, -p, ext_k)
            qa_ref[e] = jnp.where(lo, pick(qn), ext_q).astype(BF16)
            ka_ref[e] = jnp.where(lo, pick(kn), ext_k).astype(BF16)
            va_ref[e] = jnp.where(lo, pick(vv), one_v).astype(BF16)

    tile = lambda base: pl.BlockSpec((tm, 128), lambda i, j: (i, base + j))
    vec = pl.BlockSpec((1, 128), lambda i, j: (0, 0))
    out = pl.BlockSpec((2, tm, 128), lambda i, j: (j, i, 0))
    return pl.pallas_call(
        body, grid=(s // tm, HEADS // 2),
        in_specs=[tile(first), tile(first + 4), tile(first + 8), pl.BlockSpec((tm, 128), lambda i, j: (i, 0)), vec, vec],
        out_specs=[out, out, out], out_shape=[jax.ShapeDtypeStruct((HEADS, s, 128), BF16)] * 3,
        compiler_params=_cparams(("parallel", "arbitrary")), name=name)(proj, proj, proj, c, gq2, gk2)


def attn_fwd(q, k, v, ccol, crow, gq, gk, name):
    hh, s, hd = q.shape
    tq = tk = min(512, s)
    nq = s // tq

    def body(q_ref, k_ref, v_ref, cc_ref, cr_ref, gq_ref, gk_ref, o_ref, lse_ref, qn_ref, m_ref, l_ref, acc_ref):
        qi, ki = pl.program_id(1), pl.program_id(2)

        @pl.when(ki == 0)
        def _():
            qn_ref[...] = _qk_hat(q_ref, gq_ref, ATTN_SCALE)
            m_ref[...] = jnp.full_like(m_ref, -jnp.inf)
            l_ref[...] = jnp.zeros_like(l_ref)
            acc_ref[...] = jnp.zeros_like(acc_ref)

        @pl.when(ki <= qi)
        def _():
            kn = _qk_hat(k_ref, gk_ref, 1.0)
            sb = _dot_nt(qn_ref[...], kn) + (cc_ref[...] - cr_ref[...])
            sb = _causal(sb, qi, ki, tq, tk)
            m_new = jnp.maximum(m_ref[...], jnp.max(sb, axis=-1, keepdims=True))
            alpha = jnp.exp(m_ref[...] - m_new)
            p = jnp.exp(sb - m_new)
            l_ref[...] = alpha * l_ref[...] + jnp.sum(p, axis=-1, keepdims=True)
            acc_ref[...] = alpha * acc_ref[...] + _dot(p.astype(BF16), v_ref[...])
            m_ref[...] = m_new

        @pl.when(ki == qi)
        def _():
            o_ref[...] = (acc_ref[...] / l_ref[...]).astype(BF16)
            lse_ref[...] = m_ref[...] + jnp.log(l_ref[...])

    qspec = pl.BlockSpec((None, tq, hd), lambda h, i, j: (h, i, 0))
    kspec = pl.BlockSpec((None, tk, hd), lambda h, i, j: (h, jnp.minimum(i, j), 0))
    gspec = pl.BlockSpec((1, hd), lambda h, i, j: (0, 0))
    return pl.pallas_call(
        body, grid=(hh, nq, nq),
        in_specs=[qspec, kspec, kspec,
                  pl.BlockSpec((None, tq, 1), lambda h, i, j: (h, i, 0)),
                  pl.BlockSpec((None, 1, tk), lambda h, i, j: (h, 0, jnp.minimum(i, j))), gspec, gspec],
        out_specs=[qspec, pl.BlockSpec((None, tq, 1), lambda h, i, j: (h, i, 0))],
        out_shape=[jax.ShapeDtypeStruct((hh, s, hd), BF16), jax.ShapeDtypeStruct((hh, s, 1), F32)],
        scratch_shapes=[pltpu.VMEM((tq, hd), BF16), pltpu.VMEM((tq, 1), F32), pltpu.VMEM((tq, 1), F32),
                        pltpu.VMEM((tq, hd), F32)],
        compiler_params=_cparams(("parallel", "parallel", "arbitrary")), name=name)(q, k, v, ccol, crow, gq, gk)


def attn_bwd_dq(q, k, v, o, do, lse, ccol, crow, gq, gk, name):
    hh, s, hd = q.shape
    tq = tk = min(512, s)
    nq = s // tq

    def body(q_ref, k_ref, v_ref, o_ref, do_ref, lse_ref, cc_ref, cr_ref, gq_ref, gk_ref,
             dq_ref, dcc_ref, dg_ref, qn_ref, dl_ref, acc_ref, dca_ref):
        h, qi, ki = pl.program_id(0), pl.program_id(1), pl.program_id(2)

        @pl.when((h == 0) & (qi == 0) & (ki == 0))
        def _():
            dg_ref[...] = jnp.zeros_like(dg_ref)

        @pl.when(ki == 0)
        def _():
            qn_ref[...] = _qk_hat(q_ref, gq_ref, ATTN_SCALE)
            dl_ref[...] = jnp.sum(do_ref[...].astype(F32) * o_ref[...].astype(F32), axis=-1, keepdims=True)
            acc_ref[...] = jnp.zeros_like(acc_ref)
            dca_ref[...] = jnp.zeros_like(dca_ref)

        @pl.when(ki <= qi)
        def _():
            kn = _qk_hat(k_ref, gk_ref, 1.0)
            sb = _dot_nt(qn_ref[...], kn) + (cc_ref[...] - cr_ref[...])
            p = jnp.exp(_causal(sb, qi, ki, tq, tk) - lse_ref[...])
            dp = _dot_nt(do_ref[...], v_ref[...])
            ds = p * (dp - dl_ref[...])
            acc_ref[...] += _dot(ds.astype(BF16), kn)
            dca_ref[...] += jnp.sum(ds, axis=-1, keepdims=True)

        @pl.when(ki == qi)
        def _():
            dq, dg = _norm_bwd(q_ref[...].astype(F32), gq_ref[...], acc_ref[...], ATTN_SCALE)
            dq_ref[...] = dq.astype(BF16)
            dcc_ref[...] = dca_ref[...]
            dg_ref[...] += dg

    qspec = pl.BlockSpec((None, tq, hd), lambda h, i, j: (h, i, 0))
    kspec = pl.BlockSpec((None, tk, hd), lambda h, i, j: (h, jnp.minimum(i, j), 0))
    cspec = pl.BlockSpec((None, tq, 1), lambda h, i, j: (h, i, 0))
    gspec = pl.BlockSpec((1, hd), lambda h, i, j: (0, 0))
    return pl.pallas_call(
        body, grid=(hh, nq, nq),
        in_specs=[qspec, kspec, kspec, qspec, qspec, cspec, cspec,
                  pl.BlockSpec((None, 1, tk), lambda h, i, j: (h, 0, jnp.minimum(i, j))), gspec, gspec],
        out_specs=[qspec, cspec, gspec],
        out_shape=[jax.ShapeDtypeStruct((hh, s, hd), BF16), jax.ShapeDtypeStruct((hh, s, 1), F32),
                   jax.ShapeDtypeStruct((1, hd), F32)],
        scratch_shapes=[pltpu.VMEM((tq, hd), BF16), pltpu.VMEM((tq, 1), F32), pltpu.VMEM((tq, hd), F32),
                        pltpu.VMEM((tq, 1), F32)],
        compiler_params=_cparams(("arbitrary", "arbitrary", "arbitrary")), name=name)(
            q, k, v, o, do, lse, ccol, crow, gq, gk)


def attn_bwd_dkv(q, k, v, o, do, lse, ccol, crow, gq, gk, name):
    hh, s, hd = q.shape
    tq = tk = min(512, s)
    nq = s // tq

    def body(q_ref, k_ref, v_ref, o_ref, do_ref, lse_ref, cc_ref, cr_ref, gq_ref, gk_ref,
             dk_ref, dv_ref, dcr_ref, dg_ref, kn_ref, dka_ref, dva_ref, dca_ref):
        h, ki, qi = pl.program_id(0), pl.program_id(1), pl.program_id(2)

        @pl.when((h == 0) & (ki == 0) & (qi == 0))
        def _():
            dg_ref[...] = jnp.zeros_like(dg_ref)

        @pl.when(qi == 0)
        def _():
            kn_ref[...] = _qk_hat(k_ref, gk_ref, 1.0)
            dka_ref[...] = jnp.zeros_like(dka_ref)
            dva_ref[...] = jnp.zeros_like(dva_ref)
            dca_ref[...] = jnp.zeros_like(dca_ref)

        @pl.when(qi >= ki)
        def _():
            qn = _qk_hat(q_ref, gq_ref, ATTN_SCALE)
            do = do_ref[...]
            delta = jnp.sum(do.astype(F32) * o_ref[...].astype(F32), axis=-1, keepdims=True)
            sb = _dot_nt(qn, kn_ref[...]) + (cc_ref[...] - cr_ref[...])
            p = jnp.exp(_causal(sb, qi, ki, tq, tk) - lse_ref[...])
            dva_ref[...] += _dot_tn(p.astype(BF16), do)
            ds = p * (_dot_nt(do, v_ref[...]) - delta)
            dka_ref[...] += _dot_tn(ds.astype(BF16), qn)
            dca_ref[...] += jnp.sum(ds, axis=0, keepdims=True)

        @pl.when(qi == nq - 1)
        def _():
            dk, dg = _norm_bwd(k_ref[...].astype(F32), gk_ref[...], dka_ref[...], 1.0)
            dk_ref[...] = dk.astype(BF16)
            dv_ref[...] = dva_ref[...].astype(BF16)
            dcr_ref[...] = dca_ref[...]
            dg_ref[...] += dg

    kspec = pl.BlockSpec((None, tk, hd), lambda h, j, i: (h, j, 0))
    qspec = pl.BlockSpec((None, tq, hd), lambda h, j, i: (h, jnp.maximum(i, j), 0))
    cspec = pl.BlockSpec((None, tq, 1), lambda h, j, i: (h, jnp.maximum(i, j), 0))
    rspec = pl.BlockSpec((None, 1, tk), lambda h, j, i: (h, 0, j))
    gspec = pl.BlockSpec((1, hd), lambda h, j, i: (0, 0))
    return pl.pallas_call(
        body, grid=(hh, nq, nq),
        in_specs=[qspec, kspec, kspec, qspec, qspec, cspec, cspec, rspec, gspec, gspec],
        out_specs=[kspec, kspec, rspec, gspec],
        out_shape=[jax.ShapeDtypeStruct((hh, s, hd), BF16), jax.ShapeDtypeStruct((hh, s, hd), BF16),
                   jax.ShapeDtypeStruct((hh, 1, s), F32), jax.ShapeDtypeStruct((1, hd), F32)],
        scratch_shapes=[pltpu.VMEM((tk, hd), BF16), pltpu.VMEM((tk, hd), F32), pltpu.VMEM((tk, hd), F32),
                        pltpu.VMEM((1, tk), F32)],
        compiler_params=_cparams(("arbitrary", "arbitrary", "arbitrary")), name=name)(
            q, k, v, o, do, lse, ccol, crow, gq, gk)


def attn_forward(qa, ka, va, name):
    hh, s, _ = qa.shape
    tq = tk = min(512, s)
    nq = s // tq
    grp = HEAD_GROUP

    def body(q_ref, k_ref, v_ref, o_ref, lse_ref, m_ref, acc_ref):
        qi, ki = pl.program_id(1), pl.program_id(2)
        lane, _ = _lanes()

        @pl.when(ki == 0)
        def _():
            m_ref[...] = jnp.full_like(m_ref, -jnp.inf)
            acc_ref[...] = jnp.zeros_like(acc_ref)

        def step(masked):
            for g in range(grp):
                sb = _dot_nt(q_ref[g], k_ref[g])
                if masked:
                    sb = _causal(sb, tq, tk)
                m_old = m_ref[g]
                m_new = jnp.maximum(m_old, jnp.max(sb, axis=-1, keepdims=True))
                p = jnp.exp(sb - m_new)
                acc_ref[g] = jnp.exp(m_old - m_new) * acc_ref[g] + _dot(p.astype(BF16), v_ref[g])
                m_ref[g] = m_new

        @pl.when(ki < qi)
        def _():
            step(False)

        @pl.when(ki == qi)
        def _():
            step(True)
            for g in range(grp):
                acc = acc_ref[g]
                denom = _lane_col(acc, lane, LANE_C)
                o_ref[g] = (acc / denom).astype(BF16)
                lse_ref[g] = m_ref[g] + jnp.log(denom)

    qspec = pl.BlockSpec((grp, tq, 128), lambda h, i, j: (h, i, 0))
    kspec = pl.BlockSpec((grp, tk, 128), lambda h, i, j: (h, jnp.minimum(i, j), 0))
    lspec = pl.BlockSpec((grp, tq, 1), lambda h, i, j: (h, i, 0))
    return pl.pallas_call(
        body, grid=(hh // grp, nq, nq), in_specs=[qspec, kspec, kspec], out_specs=[qspec, lspec],
        out_shape=[jax.ShapeDtypeStruct((hh, s, 128), BF16), jax.ShapeDtypeStruct((hh, s, 1), F32)],
        scratch_shapes=[pltpu.VMEM((grp, tq, 1), F32), pltpu.VMEM((grp, tq, 128), F32)],
        compiler_params=_cparams(("parallel", "parallel", "arbitrary")), name=name)(qa, ka, va)


def attn_backward(qa, ka, va, oa, doa, lse, name):
    hh, s, _ = qa.shape
    tq = tk = min(512, s)
    nq = s // tq
    grp = HEAD_GROUP

    def body(q_ref, k_ref, v_ref, o_ref, do_ref, lse_ref, dq_ref, dk_ref, dv_ref, dka_ref, dva_ref):
        ki, qi = pl.program_id(1), pl.program_id(2)

        @pl.when((ki == 0) & (qi == 0))
        def _():
            dq_ref[...] = jnp.zeros_like(dq_ref)

        @pl.when(qi == 0)
        def _():
            dka_ref[...] = jnp.zeros_like(dka_ref)
            dva_ref[...] = jnp.zeros_like(dva_ref)

        def step(masked):
            rows = pl.ds(pl.multiple_of(qi * tq, tq), tq)
            for g in range(grp):
                q, k, do = q_ref[g], k_ref[g], do_ref[g]
                sb = _dot_nt(q, k)
                if masked:
                    sb = _causal(sb, tq, tk)
                p = jnp.exp(sb - lse_ref[g])
                delta = jnp.sum(do.astype(F32) * o_ref[g].astype(F32), axis=-1, keepdims=True)
                ds = (p * (_dot_nt(do, v_ref[g]) - delta)).astype(BF16)
                dva_ref[g] += _dot_tn(p.astype(BF16), do)
                dka_ref[g] += _dot_tn(ds, q)
                dq_ref[g, rows, :] += _dot(ds, k)

        @pl.when(qi > ki)
        def _():
            step(False)

        @pl.when(qi == ki)
        def _():
            step(True)

        @pl.when(qi == nq - 1)
        def _():
            dk_ref[...] = dka_ref[...]
            dv_ref[...] = dva_ref[...].astype(BF16)

    qspec = pl.BlockSpec((grp, tq, 128), lambda h, j, i: (h, jnp.maximum(i, j), 0))
    lspec = pl.BlockSpec((grp, tq, 1), lambda h, j, i: (h, jnp.maximum(i, j), 0))
    kspec = pl.BlockSpec((grp, tk, 128), lambda h, j, i: (h, j, 0))
    return pl.pallas_call(
        body, grid=(hh // grp, nq, nq), in_specs=[qspec, kspec, kspec, qspec, qspec, lspec],
        out_specs=[pl.BlockSpec((grp, s, 128), lambda h, j, i: (h, 0, 0)), kspec, kspec],
        out_shape=[jax.ShapeDtypeStruct((hh, s, 128), F32), jax.ShapeDtypeStruct((hh, s, 128), F32),
                   jax.ShapeDtypeStruct((hh, s, 128), BF16)],
        scratch_shapes=[pltpu.VMEM((grp, tk, 128), F32), pltpu.VMEM((grp, tk, 128), F32)],
        compiler_params=_cparams(("arbitrary", "arbitrary", "arbitrary")), name=name)(qa, ka, va, oa, doa, lse)


def attn_post(dqa, dka, dva, proj, gq2, gk2, dproj, name):
    s = proj.shape[0]
    tm = min(256, s)

    def body(dq_ref, dk_ref, dv_ref, q_ref, k_ref, gq_ref, gk_ref, dp_any, dp_ref, dc_ref, dgq_ref, dgk_ref):
        lane, lo = _lanes()

        @pl.when(pl.program_id(0) == 0)
        def _():
            dgq_ref[...] = jnp.zeros_like(dgq_ref)
            dgk_ref[...] = jnp.zeros_like(dgk_ref)

        def pair(ref, j):
            return jnp.where(lo, ref[2 * j].astype(F32), _swap_halves(ref[2 * j + 1].astype(F32)))

        def norm_bwd(raw, g, dhat, scale):
            r = lax.rsqrt(_half_mean(raw * raw, lo) + EPS)
            y = raw * r
            dy = dhat * (g * scale)
            return r * (dy - y * _half_mean(dy * y, lo)), jnp.sum(dhat * y, axis=0, keepdims=True) * scale

        dc = jnp.zeros((tm, 128), F32)
        for j in range(HEADS // 2):
            cols = slice(128 * j, 128 * (j + 1))
            dq, dgq = norm_bwd(q_ref[:, cols].astype(F32), gq_ref[...], pair(dq_ref, j), ATTN_SCALE)
            dk, dgk = norm_bwd(k_ref[:, cols].astype(F32), gk_ref[...], pair(dk_ref, j), 1.0)
            dgq_ref[...] += dgq
            dgk_ref[...] += dgk
            dp_ref[:, cols] = dq.astype(BF16)
            dp_ref[:, D_ATTN + 128 * j:D_ATTN + 128 * (j + 1)] = dk.astype(BF16)
            dp_ref[:, 2 * D_ATTN + 128 * j:2 * D_ATTN + 128 * (j + 1)] = pair(dv_ref, j).astype(BF16)
            for e in range(2):
                h = 2 * j + e
                col = _lane_col(dq_ref[h], lane, LANE_C) - _lane_col(dk_ref[h], lane, LANE_ONE)
                dc = jnp.where(lane == h, col, dc)
        dp_ref[:, 3 * D_ATTN:] = jnp.zeros((tm, DPROJ_TAIL - 3 * D_ATTN), BF16)
        dc_ref[...] = dc

    heads = lambda: pl.BlockSpec((HEADS, tm, 128), lambda i: (0, i, 0))
    vec = pl.BlockSpec((1, 128), lambda i: (0, 0))
    first = N_REST // D_ATTN
    return pl.pallas_call(
        body, grid=(s // tm,),
        in_specs=[heads(), heads(), heads(), pl.BlockSpec((tm, D_ATTN), lambda i: (i, first)),
                  pl.BlockSpec((tm, D_ATTN), lambda i: (i, first + 1)), vec, vec, pl.BlockSpec(memory_space=pl.ANY)],
        out_specs=[pl.BlockSpec((tm, DPROJ_TAIL), lambda i: (i, N_REST // DPROJ_TAIL)),
                   pl.BlockSpec((tm, 128), lambda i: (i, 0)), vec, vec],
        out_shape=[jax.ShapeDtypeStruct(dproj.shape, BF16), jax.ShapeDtypeStruct((s, 128), F32),
                   jax.ShapeDtypeStruct((1, 128), F32), jax.ShapeDtypeStruct((1, 128), F32)],
        input_output_aliases={7: 0},
        compiler_params=_cparams(("arbitrary",)), name=name)(dqa, dka, dva, proj, proj, gq2, gk2, dproj)


def _pool_groups(tm):
    gid = lax.broadcasted_iota(jnp.int32, (1, D_POOL), 1) // (D_POOL // 4)
    win = jnp.where(gid == 0, 2.0, jnp.where(gid == 1, 4.0, jnp.where(gid == 2, 8.0, 16.0)))
    return gid, win


def _by_group(gid, v2, v4, v8, v16):
    return jnp.where(gid == 0, v2, jnp.where(gid == 1, v4, jnp.where(gid == 2, v8, v16)))


def _branches(rest_ref, halo_ref, a_ref, wa_ref, wc_ref, wp_ref, sc_ref, cw_ref, ti, tm):
    f = lambda v: v.astype(F32)
    cx, cb, cc, px = f(rest_ref[:, 0:256]), f(rest_ref[:, 256:512]), f(rest_ref[:, 512:768]), f(rest_ref[:, 768:1024])
    live = jnp.where(ti > 0, 1.0, 0.0)
    hz = f(halo_ref[:, 0:256]) * f(halo_ref[:, 512:768]) * live
    hp = f(halo_ref[:, 768:1024]) * live
    z = cc * cx
    zf = jnp.concatenate([hz, z], axis=0)
    z1 = pltpu.roll(zf, 1, 0)[HALO:]
    z2 = pltpu.roll(zf, 2, 0)[HALO:]
    cw = cw_ref[...]
    conv = cw[2:3] * z + cw[1:2] * z1 + cw[0:1] * z2
    uc = cb * conv
    pf = jnp.concatenate([hp, px], axis=0)
    s2 = pf + pltpu.roll(pf, 1, 0)
    s4 = s2 + pltpu.roll(s2, 2, 0)
    s8 = s4 + pltpu.roll(s4, 4, 0)
    s16 = s8 + pltpu.roll(s8, 8, 0)
    gid, win = _pool_groups(tm)
    t = (ti * tm + lax.broadcasted_iota(jnp.int32, (tm, 1), 0)).astype(F32)
    inv = 1.0 / jnp.minimum(t + 1.0, win)
    dpool = _by_group(gid, s2[HALO:], s4[HALO:], s8[HALO:], s16[HALO:]) * inv - px
    _, lo = _lanes()
    a_tok = [jnp.where(lo, f(a_ref[2 * j]), _swap_halves(f(a_ref[2 * j + 1]))).astype(BF16) for j in range(HEADS // 2)]
    y_attn = _dot(a_tok[0], wa_ref[0:128, :])
    for j in range(1, HEADS // 2):
        y_attn += _dot(a_tok[j], wa_ref[128 * j:128 * (j + 1), :])
    y_conv = _dot(uc.astype(BF16), wc_ref[...])
    y_pool_raw = _dot(dpool.astype(BF16), wp_ref[...])
    sg = [_sigmoid(f(rest_ref[:, 1024 + i * D_MODEL:1024 + (i + 1) * D_MODEL])) for i in range(3)]
    return dict(cx=cx, cb=cb, cc=cc, z=z, z1=z1, z2=z2, conv=conv, uc=uc, dpool=dpool, inv=inv, gid=gid, a_tok=a_tok,
                y_attn=y_attn, y_conv=y_conv, y_pool_raw=y_pool_raw, sg=sg, cw=cw)


def _mix_specs(tm, ti_of):
    blocks_per_tile = tm // HALO
    return [
        pl.BlockSpec((tm, N_REST), lambda i: (ti_of(i), 0)),
        pl.BlockSpec((HALO, 1024), lambda i: (jnp.maximum(ti_of(i) * blocks_per_tile - 1, 0), 0)),
        pl.BlockSpec((HEADS, tm, 128), lambda i: (0, ti_of(i), 0)),
        pl.BlockSpec((D_ATTN, D_MODEL), lambda i: (0, 0)),
        pl.BlockSpec((D_CONV, D_MODEL), lambda i: (0, 0)),
        pl.BlockSpec((D_POOL, D_MODEL), lambda i: (0, 0)),
        pl.BlockSpec((1, D_MODEL), lambda i: (0, 0)),
        pl.BlockSpec((8, D_CONV), lambda i: (0, 0)),
    ]


def mix_fwd(proj, a, x, wa, wc, wp, scale, cw, wo, name):
    s = x.shape[0]
    tm = min(256, s)

    def body(rest_ref, halo_ref, a_ref, wa_ref, wc_ref, wp_ref, sc_ref, cw_ref, wo_ref, x_ref, o_ref):
        b = _branches(rest_ref, halo_ref, a_ref, wa_ref, wc_ref, wp_ref, sc_ref, cw_ref, pl.program_id(0), tm)
        merged = b["sg"][0] * b["y_attn"] + b["sg"][1] * b["y_conv"] + b["sg"][2] * (b["y_pool_raw"] * sc_ref[...])
        o_ref[...] = x_ref[...] + _dot(merged.astype(BF16), wo_ref[...])

    return pl.pallas_call(
        body, grid=(s // tm,),
        in_specs=_mix_specs(tm, lambda i: i) + [pl.BlockSpec((D_MODEL, D_MODEL), lambda i: (0, 0)),
                                                 pl.BlockSpec((tm, D_MODEL), lambda i: (i, 0))],
        out_specs=pl.BlockSpec((tm, D_MODEL), lambda i: (i, 0)),
        out_shape=jax.ShapeDtypeStruct((s, D_MODEL), F32),
        compiler_params=_cparams(("parallel",)), name=name)(proj, proj, a, wa, wc, wp, scale, cw, wo, x)


def mix_bwd(proj, a, dx1, wa, wc, wp, scale, cw, wo, name):
    s = dx1.shape[0]
    tm = min(256, s)
    nt = s // tm
    ti_of = lambda i: nt - 1 - i
    n = tm + HALO

    def body(rest_ref, halo_ref, a_ref, wa_ref, wc_ref, wp_ref, sc_ref, cw_ref, wo_ref,
             dx_ref, dp_ref, da_ref, at_ref, mg_ref, dya_ref, dyc_ref, dyp_ref, uc_ref, dd_ref, dsc_ref, dcw_ref,
             cdc_ref, cde_ref):
        i = pl.program_id(0)
        ti = ti_of(i)

        @pl.when(i == 0)
        def _():
            cdc_ref[...] = jnp.zeros_like(cdc_ref)
            cde_ref[...] = jnp.zeros_like(cde_ref)
            dsc_ref[...] = jnp.zeros_like(dsc_ref)
            dcw_ref[...] = jnp.zeros_like(dcw_ref)

        b = _branches(rest_ref, halo_ref, a_ref, wa_ref, wc_ref, wp_ref, sc_ref, cw_ref, ti, tm)
        sg, sc = b["sg"], sc_ref[...]
        y_pool = b["y_pool_raw"] * sc
        merged = sg[0] * b["y_attn"] + sg[1] * b["y_conv"] + sg[2] * y_pool
        mg_ref[...] = merged.astype(BF16)
        dm = _dot_nt(dx_ref[...].astype(BF16), wo_ref[...])
        for j, y in enumerate((b["y_attn"], b["y_conv"], y_pool)):
            dp_ref[:, 1024 + j * D_MODEL:1024 + (j + 1) * D_MODEL] = (dm * y * sg[j] * (1.0 - sg[j])).astype(BF16)
        dya = (dm * sg[0]).astype(BF16)
        dya_ref[...] = dya
        _, lo = _lanes()
        for j in range(HEADS // 2):
            at_ref[:, 128 * j:128 * (j + 1)] = b["a_tok"][j]
            da = _dot_nt(dya, wa_ref[128 * j:128 * (j + 1), :])
            da_ref[2 * j] = jnp.where(lo, da, 0.0).astype(BF16)
            da_ref[2 * j + 1] = jnp.where(lo, _swap_halves(da), 0.0).astype(BF16)
        dyc = (dm * sg[1]).astype(BF16)
        dyc_ref[...] = dyc
        duc = _dot_nt(dyc, wc_ref[...])
        dyp = dm * sg[2]
        dsc_ref[...] += jnp.sum(dyp * b["y_pool_raw"], axis=0, keepdims=True)
        dypr = (dyp * sc).astype(BF16)
        dyp_ref[...] = dypr
        ddp = _dot_nt(dypr, wp_ref[...])
        uc_ref[...] = b["uc"].astype(BF16)
        dd_ref[...] = b["dpool"].astype(BF16)

        dconv = duc * b["cb"]
        dp_ref[:, 256:512] = (duc * b["conv"]).astype(BF16)
        dcf = jnp.concatenate([dconv, cdc_ref[...]], axis=0)
        cw = b["cw"]
        dz = cw[2:3] * dconv + cw[1:2] * pltpu.roll(dcf, n - 1, 0)[:tm] + cw[0:1] * pltpu.roll(dcf, n - 2, 0)[:tm]
        dp_ref[:, 0:256] = (dz * b["cc"]).astype(BF16)
        dp_ref[:, 512:768] = (dz * b["cx"]).astype(BF16)
        dcw_ref[0:1, :] += jnp.sum(dconv * b["z2"], axis=0, keepdims=True)
        dcw_ref[1:2, :] += jnp.sum(dconv * b["z1"], axis=0, keepdims=True)
        dcw_ref[2:3, :] += jnp.sum(dconv * b["z"], axis=0, keepdims=True)
        cdc_ref[...] = dconv[:HALO]

        e = ddp * b["inv"]
        ef = jnp.concatenate([e, cde_ref[...]], axis=0)
        r2 = ef + pltpu.roll(ef, n - 1, 0)
        r4 = r2 + pltpu.roll(r2, n - 2, 0)
        r8 = r4 + pltpu.roll(r4, n - 4, 0)
        r16 = r8 + pltpu.roll(r8, n - 8, 0)
        dp_ref[:, 768:1024] = (_by_group(b["gid"], r2[:tm], r4[:tm], r8[:tm], r16[:tm]) - ddp).astype(BF16)
        cde_ref[...] = e[:HALO]

    tile = lambda w: pl.BlockSpec((tm, w), lambda i: (ti_of(i), 0))
    whole = lambda r, c: pl.BlockSpec((r, c), lambda i: (0, 0))
    bf = lambda w: jax.ShapeDtypeStruct((s, w), BF16)
    return pl.pallas_call(
        body, grid=(nt,),
        in_specs=_mix_specs(tm, ti_of) + [whole(D_MODEL, D_MODEL), tile(D_MODEL)],
        out_specs=[tile(N_REST), pl.BlockSpec((HEADS, tm, 128), lambda i: (0, ti_of(i), 0)), tile(D_ATTN),
                   tile(D_MODEL), tile(D_MODEL), tile(D_MODEL), tile(D_MODEL),
                   tile(D_CONV), tile(D_POOL), whole(1, D_MODEL), whole(8, D_CONV)],
        out_shape=[bf(DPROJ_COLS), jax.ShapeDtypeStruct((HEADS, s, 128), BF16), bf(D_ATTN),
                   bf(D_MODEL), bf(D_MODEL), bf(D_MODEL), bf(D_MODEL), bf(D_CONV), bf(D_POOL),
                   jax.ShapeDtypeStruct((1, D_MODEL), F32), jax.ShapeDtypeStruct((8, D_CONV), F32)],
        scratch_shapes=[pltpu.VMEM((HALO, D_CONV), F32), pltpu.VMEM((HALO, D_POOL), F32)],
        compiler_params=_cparams(("arbitrary",)), name=name)(proj, proj, a, wa, wc, wp, scale, cw, wo, dx1)


def _adamw_math(w, g, m, v):
    m = ADAM_B1 * m + (1.0 - ADAM_B1) * g
    v = ADAM_B2 * v + (1.0 - ADAM_B2) * (g * g)
    m_hat = m / (1.0 - ADAM_B1 ** ADAM_STEP)
    v_hat = v / (1.0 - ADAM_B2 ** ADAM_STEP)
    delta = -ADAM_LR * (m_hat / (jnp.sqrt(v_hat) + ADAM_EPS) + ADAM_WD * w)
    return delta, m, v


ADAMW_PARTS_BLOCK_BYTES = 4 * 2 ** 20


def adamw_sum(parts, w, m, v, name):
    layers, rows, cols = w.shape
    row_bytes = N_DEV * (-(-cols // 128) * 128) * parts.dtype.itemsize
    fits = [t for t in range(16, rows + 1, 16) if rows % t == 0 and t * row_bytes <= ADAMW_PARTS_BLOCK_BYTES]
    tr = max(fits) if fits else rows

    def body(p_ref, w_ref, m_ref, v_ref, g_ref, d_ref, nm_ref, nv_ref):
        g = p_ref[0].astype(F32)
        for i in range(1, N_DEV):
            g = g + p_ref[i].astype(F32)
        g_ref[...] = g
        d_ref[...], nm_ref[...], nv_ref[...] = _adamw_math(w_ref[...], g, m_ref[...], v_ref[...])

    spec = pl.BlockSpec((None, tr, cols), lambda l, i: (l, i, 0))
    return pl.pallas_call(
        body, grid=(layers, rows // tr),
        in_specs=[pl.BlockSpec((None, N_DEV, tr, cols), lambda l, i: (l, 0, i, 0)), spec, spec, spec],
        out_specs=[spec] * 4, out_shape=[jax.ShapeDtypeStruct((layers, rows, cols), F32)] * 4,
        compiler_params=_cparams(("parallel", "parallel")), name=name)(parts, w, m, v)


def _me():
    return lax.axis_index("x"), lax.axis_index("y"), lax.axis_index("c")


N_PEERS = N_DEV - 1


def all_gather(shards, name):
    n = len(shards)
    any_spec = pl.BlockSpec(memory_space=pl.ANY)

    def body(*refs):
        x_refs, out_refs = refs[:n], refs[n:2 * n]
        send_sems, recv_sems, local_sems = refs[2 * n:]
        x, y, c = _me()
        me, sibling = (x, y, c), (x, y, 1 - c)
        chips = [(1 - x, y), (x, 1 - y), (1 - x, 1 - y)]

        def copy(t, k, block, to, from_input=False):
            slot = out_refs[t].at[4 * block[0] + 2 * block[1] + block[2]]
            return pltpu.make_async_remote_copy(
                src_ref=x_refs[t] if from_input else slot, dst_ref=slot, send_sem=send_sems.at[N_PEERS * t + k],
                recv_sem=recv_sems.at[N_PEERS * t + k], device_id=to, device_id_type=pl.DeviceIdType.MESH)

        mine = [pltpu.make_async_copy(x_refs[t], out_refs[t].at[4 * x + 2 * y + c], local_sems.at[t]) for t in range(n)]
        started = []
        for t in range(n):
            mine[t].start()
            started.append(copy(t, 0, me, sibling, from_input=True))
            started += [copy(t, 1 + j, me, (*chip, c), from_input=True) for j, chip in enumerate(chips)]
        for cp in started:
            cp.start()
        for j, chip in enumerate(chips):
            for t in range(n):
                copy(t, 1 + j, (*chip, c), me).wait_recv()
                fwd = copy(t, 4 + j, (*chip, c), sibling)
                fwd.start()
                started.append(fwd)
        for t in range(n):
            copy(t, 0, sibling, me).wait_recv()
            for j, chip in enumerate(chips):
                copy(t, 4 + j, (*chip, 1 - c), me).wait_recv()
        for cp in started:
            cp.wait_send()
        for cp in mine:
            cp.wait()

    return pl.pallas_call(
        body, out_shape=[jax.ShapeDtypeStruct((N_DEV,) + s.shape, s.dtype) for s in shards],
        in_specs=[any_spec] * n, out_specs=[any_spec] * n,
        scratch_shapes=[pltpu.SemaphoreType.DMA((N_PEERS * n,)), pltpu.SemaphoreType.DMA((N_PEERS * n,)),
                        pltpu.SemaphoreType.DMA((n,))],
        name=name)(*shards)


def all_to_all(blocks, bufs, layer, name):
    n = len(blocks)
    any_spec = pl.BlockSpec(memory_space=pl.ANY)

    def body(*refs):
        src_refs, out_refs = refs[:n], refs[2 * n:3 * n]
        send_sems, recv_sems, local_sems = refs[3 * n:]
        x, y, c = _me()
        me = 4 * x + 2 * y + c
        peers = [(x ^ ((k >> 2) & 1), y ^ ((k >> 1) & 1), c ^ (k & 1)) for k in range(1, N_DEV)]

        def copy(t, k, src_block, dst_block, to):
            return pltpu.make_async_remote_copy(
                src_ref=src_refs[t].at[src_block], dst_ref=out_refs[t].at[layer, dst_block],
                send_sem=send_sems.at[N_PEERS * t + k], recv_sem=recv_sems.at[N_PEERS * t + k],
                device_id=to, device_id_type=pl.DeviceIdType.MESH)

        mine = [pltpu.make_async_copy(src_refs[t].at[me], out_refs[t].at[layer, me], local_sems.at[t]) for t in range(n)]
        sends = []
        for t in range(n):
            mine[t].start()
            for k, (px, py, pc) in enumerate(peers):
                sends.append(copy(t, k, 4 * px + 2 * py + pc, me, (px, py, pc)))
        for cp in sends:
            cp.start()
        for t in range(n):
            for k, (px, py, pc) in enumerate(peers):
                peer = 4 * px + 2 * py + pc
                copy(t, k, peer, peer, (x, y, c)).wait_recv()
        for cp in sends:
            cp.wait_send()
        for cp in mine:
            cp.wait()

    return pl.pallas_call(
        body, out_shape=[jax.ShapeDtypeStruct(b.shape, b.dtype) for b in bufs],
        in_specs=[any_spec] * (2 * n), out_specs=[any_spec] * n,
        input_output_aliases={n + t: t for t in range(n)},
        scratch_shapes=[pltpu.SemaphoreType.DMA((N_PEERS * n,)), pltpu.SemaphoreType.DMA((N_PEERS * n,)),
                        pltpu.SemaphoreType.DMA((n,))],
        name=name)(*blocks, *bufs)


MATRICES = ("w_in", "w_attn_out", "w_conv_out", "pool_w", "w_o", "w_ffn_in", "w_ffn_out")
SHARD_INFO = {
    "w_in": ((DEPTH, D_MODEL, D_IN // N_DEV), 2),
    "w_attn_out": ((DEPTH, D_ATTN, D_MODEL // N_DEV), 2),
    "w_conv_out": ((DEPTH, D_CONV, D_MODEL // N_DEV), 2),
    "pool_w": ((DEPTH, 4, 64, 256 // N_DEV), 3),
    "w_o": ((DEPTH, D_MODEL // N_DEV, D_MODEL), 1),
    "w_ffn_in": ((DEPTH, D_MODEL, 2 * D_FF // N_DEV), 2),
    "w_ffn_out": ((DEPTH, D_FF // N_DEV, D_MODEL), 1),
}
VECTORS = ("norm_mix_g", "forget_b", "q_norm_g", "k_norm_g", "pool_scale", "norm_ffn_g")
VECTOR_SHAPES = {"norm_mix_g": (DEPTH, D_MODEL), "forget_b": (DEPTH, HEADS), "q_norm_g": (DEPTH, HEAD_DIM),
                 "k_norm_g": (DEPTH, HEAD_DIM), "pool_scale": (DEPTH, D_MODEL), "norm_ffn_g": (DEPTH, D_MODEL)}
CONV_W_FULL = (DEPTH, 3, D_CONV)


def _size(shape):
    n = 1
    for v in shape:
        n *= v
    return n


def _pack(arrays, rows, cols):
    flat = jnp.concatenate([a.reshape(-1) for a in arrays])
    return jnp.pad(flat, (0, rows * cols - flat.shape[0])).reshape(rows, cols)


def _unpack(packed, shapes):
    flat, out, off = packed.reshape(-1), [], 0
    for shp in shapes:
        out.append(flat[off:off + _size(shp)].reshape(shp))
        off += _size(shp)
    return out


def _join_shards(stacked, axis):
    moved = jnp.moveaxis(stacked, 0, axis)
    shp = list(moved.shape)
    shp[axis:axis + 2] = [shp[axis] * shp[axis + 1]]
    return moved.reshape(shp)


def _cut_shards(full, axis):
    shp = list(full.shape)
    shp[axis:axis + 1] = [N_DEV, shp[axis] // N_DEV]
    return jnp.moveaxis(full.reshape(shp), axis, 0)


def _regroup_w_in(w):
    pad = jnp.zeros((w.shape[0], N_FULL - N_MAIN - HEADS), w.dtype)
    return jnp.concatenate([w[:, 1544:2568], w[:, 2568:5640], w[:, 0:1536], w[:, 1536:1544], pad], axis=1)


def _ungroup_w_in(wp):
    return jnp.concatenate([wp[:, 4096:5632], wp[:, 5632:5640], wp[:, 0:1024], wp[:, 1024:4096]], axis=1)


def _interleave_ffn(w):
    d = w.shape[0]
    return jnp.stack([w[:, :D_FF].reshape(d, N_FF_BLKS, FF_BLK), w[:, D_FF:].reshape(d, N_FF_BLKS, FF_BLK)],
                     axis=2).reshape(d, 2 * D_FF)


def _deinterleave_ffn(wp):
    d = wp.shape[0]
    t = wp.reshape(d, N_FF_BLKS, 2, FF_BLK)
    return jnp.concatenate([t[:, :, 0].reshape(d, D_FF), t[:, :, 1].reshape(d, D_FF)], axis=1)


def _pool_block_diag(w):
    out = jnp.zeros((D_POOL, D_MODEL), w.dtype)
    for g in range(4):
        out = lax.dynamic_update_slice(out, w[g], (g * 64, g * 256))
    return out


def _pool_from_block_diag(wbd):
    return jnp.stack([wbd[g * 64:(g + 1) * 64, g * 256:(g + 1) * 256] for g in range(4)])


def _layer_weights(mats, vec, conv_w, l):
    w_in = _regroup_w_in(mats["w_in"])
    w_ffn_in = _interleave_ffn(mats["w_ffn_in"])
    wp = _pool_block_diag(mats["pool_w"])
    row = lambda v: v.reshape(1, -1)
    fb = jnp.zeros((1, 128), F32).at[0, :HEADS].set(vec["forget_b"][l])
    cw = jnp.zeros((8, D_CONV), F32).at[:3].set(conv_w[l])
    twice = lambda v: jnp.tile(v.reshape(1, -1), (1, 2))
    return dict(
        w_in=w_in, w_f=w_in[:, N_MAIN:], w_ffn_in=w_ffn_in, w_ffn_out=mats["w_ffn_out"],
        wa=mats["w_attn_out"], wc=mats["w_conv_out"], wp=wp, wo=mats["w_o"],
        g_mix=row(vec["norm_mix_g"][l]), g_ffn=row(vec["norm_ffn_g"][l]), gq2=twice(vec["q_norm_g"][l]),
        gk2=twice(vec["k_norm_g"][l]), scale=row(vec["pool_scale"][l]), fb=fb, cw=cw)


def _layer_fwd(x, w, l):
    proj, h = norm_matmul(x, w["g_mix"], w["w_in"], N_MAIN, f"in_proj_{l}")
    z, c = forget_fwd(h, w["w_f"], w["fb"], f"forget_fwd_{l}")
    qa, ka, va = attn_prep(proj, c, w["gq2"], w["gk2"], f"attn_prep_{l}")
    oa, lse = attn_forward(qa, ka, va, f"attn_fwd_{l}")
    x1 = mix_fwd(proj, oa, x, w["wa"], w["wc"], w["wp"], w["scale"], w["cw"], w["wo"], f"mix_fwd_{l}")
    gu, h2 = norm_matmul(x1, w["g_ffn"], w["w_ffn_in"], 2 * D_FF, f"ffn_in_{l}")
    x2 = swiglu_matmul(gu, w["w_ffn_out"], x1, f"ffn_out_{l}")
    saved = dict(x=x, proj=proj, h=h, z=z, qa=qa, ka=ka, va=va, oa=oa, lse=lse, x1=x1, gu=gu, h2=h2)
    return x2, saved


def _layer_bwd(dx2, sv, w, l):
    g = {}
    dgu, act = swiglu_bwd(dx2, sv["gu"], w["w_ffn_out"], f"ffn_out_bwd_{l}")
    g["w_ffn_out"] = tn_matmul(act, dx2, f"dw_ffn_out_{l}")
    g["w_ffn_in"] = _deinterleave_ffn(tn_matmul(sv["h2"], dgu, f"dw_ffn_in_{l}"))
    dx1, dg = matmul_normbwd(dgu, w["w_ffn_in"], sv["x1"], w["g_ffn"], dx2, f"ffn_in_bwd_{l}")
    g["norm_ffn_g"] = dg[0]

    (dproj, doa, a_tok, merged, dya, dyc, dyp, uc, dd, dscale, dcw) = mix_bwd(
        sv["proj"], sv["oa"], dx1, w["wa"], w["wc"], w["wp"], w["scale"], w["cw"], w["wo"], f"mix_bwd_{l}")
    g["w_o"] = tn_matmul(merged, dx1, f"dw_o_{l}")
    g["w_attn_out"] = tn_matmul(a_tok, dya, f"dw_attn_out_{l}")
    g["w_conv_out"] = tn_matmul(uc, dyc, f"dw_conv_out_{l}")
    g["pool_w"] = _pool_from_block_diag(tn_matmul(dd, dyp, f"dw_pool_{l}"))
    g["pool_scale"] = dscale[0]
    g["conv_w"] = dcw[:3]

    dqa, dka, dva = attn_backward(sv["qa"], sv["ka"], sv["va"], sv["oa"], doa, sv["lse"], f"attn_bwd_{l}")
    dproj, dc, dgq, dgk = attn_post(dqa, dka, dva, sv["proj"], w["gq2"], w["gk2"], dproj, f"attn_post_{l}")
    g["q_norm_g"] = dgq[0, :HEAD_DIM] + dgq[0, HEAD_DIM:]
    g["k_norm_g"] = dgk[0, :HEAD_DIM] + dgk[0, HEAD_DIM:]
    dproj, db = forget_bwd(dc, sv["z"], dproj, f"forget_bwd_{l}")
    g["forget_b"] = db[0, :HEADS]

    g["w_in"] = _ungroup_w_in(tn_matmul(sv["h"], dproj, f"dw_in_{l}", n_cols=N_FULL))
    dx, dg = matmul_normbwd(dproj, w["w_in"], sv["x"], w["g_mix"], dx1, f"in_proj_bwd_{l}", k=N_FULL)
    g["norm_mix_g"] = dg[0]
    return dx, g


def _local_step(x, tgt, weights_of, on_grads):
    ws, saved = [], []
    for l in range(DEPTH):
        ws.append(weights_of(l))
        x, sv = _layer_fwd(x, ws[l], l)
        saved.append(sv)
    sq, dx = loss_kernel(x, tgt, "loss")
    for l in reversed(range(DEPTH)):
        dx, g = _layer_bwd(dx, saved[l], ws[l], l)
        on_grads(l, g)
    return sq[0, 0], dx


def kernel(x, norm_mix_g, w_in, forget_b, q_norm_g, k_norm_g, w_attn_out, conv_w, w_conv_out, pool_w, pool_scale, w_o, norm_ffn_g, w_ffn_in, w_ffn_out, loss_target, m_norm_mix_g, m_w_in, m_forget_b, m_q_norm_g, m_k_norm_g, m_w_attn_out, m_conv_w, m_w_conv_out, m_pool_w, m_pool_scale, m_w_o, m_norm_ffn_g, m_w_ffn_in, m_w_ffn_out, v_norm_mix_g, v_w_in, v_forget_b, v_q_norm_g, v_k_norm_g, v_w_attn_out, v_conv_w, v_w_conv_out, v_pool_w, v_pool_scale, v_w_o, v_norm_ffn_g, v_w_ffn_in, v_w_ffn_out):
    w = dict(norm_mix_g=norm_mix_g, w_in=w_in, forget_b=forget_b, q_norm_g=q_norm_g, k_norm_g=k_norm_g,
             w_attn_out=w_attn_out, conv_w=conv_w, w_conv_out=w_conv_out, pool_w=pool_w, pool_scale=pool_scale,
             w_o=w_o, norm_ffn_g=norm_ffn_g, w_ffn_in=w_ffn_in, w_ffn_out=w_ffn_out)
    m = dict(norm_mix_g=m_norm_mix_g, w_in=m_w_in, forget_b=m_forget_b, q_norm_g=m_q_norm_g, k_norm_g=m_k_norm_g,
             w_attn_out=m_w_attn_out, conv_w=m_conv_w, w_conv_out=m_w_conv_out, pool_w=m_pool_w,
             pool_scale=m_pool_scale, w_o=m_w_o, norm_ffn_g=m_norm_ffn_g, w_ffn_in=m_w_ffn_in, w_ffn_out=m_w_ffn_out)
    v = dict(norm_mix_g=v_norm_mix_g, w_in=v_w_in, forget_b=v_forget_b, q_norm_g=v_q_norm_g, k_norm_g=v_k_norm_g,
             w_attn_out=v_w_attn_out, conv_w=v_conv_w, w_conv_out=v_w_conv_out, pool_w=v_pool_w,
             pool_scale=v_pool_scale, w_o=v_w_o, norm_ffn_g=v_norm_ffn_g, w_ffn_in=v_w_ffn_in, w_ffn_out=v_w_ffn_out)
    me = 4 * lax.axis_index("x") + 2 * lax.axis_index("y") + lax.axis_index("c")
    layer_shard = {n: SHARD_INFO[n][0][1:] for n in MATRICES}
    cut_axis = {n: SHARD_INFO[n][1] - 1 for n in MATRICES}

    conv_g = all_gather([_pack([conv_w], 8, 128)], "gather_conv_w")[0]
    conv_full = _join_shards(jnp.stack([_unpack(conv_g[i], [conv_w.shape])[0] for i in range(N_DEV)]), 2)
    vec = {n: w[n] for n in VECTORS}

    def weights_of(l):
        gathered = all_gather([w[n][l].astype(BF16) for n in MATRICES], f"gather_{l}")
        mats = {n: _join_shards(t, cut_axis[n]) for n, t in zip(MATRICES, gathered)}
        return _layer_weights(mats, vec, conv_full, l)

    received = {"bufs": [lax.empty((DEPTH, N_DEV) + layer_shard[n], BF16) for n in MATRICES]}
    small_g = [None] * DEPTH

    def on_grads(l, g):
        blocks = [_cut_shards(g[n], cut_axis[n]) for n in MATRICES]
        received["bufs"] = all_to_all(blocks, received["bufs"], l, f"exchange_{l}")
        small_g[l] = g

    sq, dx = _local_step(x[0], loss_target[0], weights_of, on_grads)
    loss = lax.psum(0.5 * sq / D_MODEL, ("x", "y", "c"))

    big = {}
    for n, parts in zip(MATRICES, received["bufs"]):
        rc = (_size(layer_shard[n][:-1]), layer_shard[n][-1])
        outs = adamw_sum(parts.reshape((DEPTH, N_DEV) + rc), *[d[n].reshape((DEPTH,) + rc) for d in (w, m, v)], f"adamw_{n}")
        big[n] = [t.reshape(w[n].shape) for t in outs]

    small_shapes = [VECTOR_SHAPES[n] for n in VECTORS] + [CONV_W_FULL]
    stacked = [jnp.stack([small_g[l][n] for l in range(DEPTH)]) for n in VECTORS + ("conv_w",)]
    sparts = all_gather([_pack(stacked, SMALL_ROWS, 128)], "gather_vector_grads")[0]
    col0 = me * (D_CONV // N_DEV)
    place = lambda t: lax.dynamic_update_slice(jnp.zeros(CONV_W_FULL, F32), t, (0, 0, col0))
    spacked = [_pack([d[n] for n in VECTORS] + [place(d["conv_w"])], SMALL_ROWS, 128)[None] for d in (w, m, v)]
    small = [_unpack(t[0], small_shapes) for t in adamw_sum(sparts[None], *spacked, "adamw_vectors")]

    def result(kind):
        out = {n: big[n][kind] for n in MATRICES}
        out.update({n: small[kind][j] for j, n in enumerate(VECTORS)})
        out["conv_w"] = lax.dynamic_slice(small[kind][len(VECTORS)], (0, 0, col0), conv_w.shape)
        return [out[n] for n in w]

    return (loss, dx[None], *result(0), *result(1), *result(2), *result(3))
```

```python
import functools

import jax
import jax.numpy as jnp
from jax import lax
from jax.experimental import pallas as pl
from jax.experimental.pallas import tpu as pltpu

F32 = jnp.float32
BF16 = jnp.bfloat16

N_DEV = 8
DEPTH = 4
D_MODEL = 1024
HEAD_DIM = 64
HEADS = 8
D_ATTN = 512
D_CONV = 256
D_POOL = 256
D_FF = 2816
D_IN = 5640
EPS = 1e-6
ATTN_SCALE = HEAD_DIM ** -0.5

N_REST = 4096
N_MAIN = 5632
N_FULL = 5760
DPROJ_TAIL = 2048
DPROJ_COLS = N_REST + DPROJ_TAIL
FF_BLK = 256
N_FF_BLKS = D_FF // FF_BLK
HALO = 16

ADAM_LR = 0.001
ADAM_B1 = 0.9
ADAM_B2 = 0.999
ADAM_EPS = 1e-08
ADAM_WD = 0.01
ADAM_STEP = 10

PACK_COLS = 1024
PACK_ROWS = 8192
SMALL_ROWS = 128

VMEM_LIMIT = 48 * 2 ** 20


def _cparams(sem, vmem=None):
    return pltpu.CompilerParams(dimension_semantics=sem, vmem_limit_bytes=vmem or VMEM_LIMIT)


def _pick(n, cands):
    for c in cands:
        if n % c == 0:
            return c
    raise ValueError(f"no tile for {n}")


def _sigmoid(v):
    return 1.0 / (1.0 + jnp.exp(-v))


def _rstd(v):
    return lax.rsqrt(jnp.mean(v * v, axis=-1, keepdims=True) + EPS)


def _dot(a, b):
    return jnp.dot(a, b, preferred_element_type=F32)


def _dot_tn(a, b):
    return lax.dot_general(a, b, (((0,), (0,)), ((), ())), preferred_element_type=F32)


def _dot_nt(a, b):
    return lax.dot_general(a, b, (((1,), (1,)), ((), ())), preferred_element_type=F32)


def norm_matmul(x, g, w, n_cols, name):
    s, d = x.shape
    tm, tn = min(512, s), 512

    def body(x_ref, g_ref, w_ref, o_ref, h_ref):
        @pl.when(pl.program_id(1) == 0)
        def _():
            xv = x_ref[...]
            h_ref[...] = (xv * _rstd(xv) * g_ref[...]).astype(BF16)

        o_ref[...] = _dot(h_ref[...], w_ref[...]).astype(BF16)

    return pl.pallas_call(
        body, grid=(s // tm, n_cols // tn),
        in_specs=[pl.BlockSpec((tm, d), lambda i, j: (i, 0)), pl.BlockSpec((1, d), lambda i, j: (0, 0)),
                  pl.BlockSpec((d, tn), lambda i, j: (0, j))],
        out_specs=[pl.BlockSpec((tm, tn), lambda i, j: (i, j)), pl.BlockSpec((tm, d), lambda i, j: (i, 0))],
        out_shape=[jax.ShapeDtypeStruct((s, n_cols), BF16), jax.ShapeDtypeStruct((s, d), BF16)],
        compiler_params=_cparams(("parallel", "arbitrary")), name=name)(x, g, w)


def tn_matmul(a, b, name, n_cols=None):
    t, m = a.shape
    n = n_cols or b.shape[1]
    tk = min(512, t)
    tmm = _pick(m, (1024, 1408, 512, 256))
    tn = _pick(n, (1152, 1024, 512, 128))

    nk = t // tk

    def body(a_ref, b_ref, o_ref, acc_ref):
        @pl.when(pl.program_id(2) == 0)
        def _():
            acc_ref[...] = jnp.zeros_like(acc_ref)

        acc_ref[...] += _dot_tn(a_ref[...].astype(BF16), b_ref[...].astype(BF16))

        @pl.when(pl.program_id(2) == nk - 1)
        def _():
            o_ref[...] = acc_ref[...].astype(BF16)

    return pl.pallas_call(
        body, grid=(m // tmm, n // tn, nk),
        in_specs=[pl.BlockSpec((tk, tmm), lambda i, j, k: (k, i)), pl.BlockSpec((tk, tn), lambda i, j, k: (k, j))],
        out_specs=pl.BlockSpec((tmm, tn), lambda i, j, k: (i, j)),
        out_shape=jax.ShapeDtypeStruct((m, n), BF16), scratch_shapes=[pltpu.VMEM((tmm, tn), F32)],
        compiler_params=_cparams(("parallel", "parallel", "arbitrary")), name=name)(a, b)


def matmul_normbwd(a, w, x, g, dres, name, k=None):
    s = a.shape[0]
    k = k or a.shape[1]
    d = w.shape[0]
    tm = min(512, s)
    tk = _pick(k, (1152, 512))
    nk = k // tk

    def body(a_ref, w_ref, x_ref, g_ref, r_ref, dx_ref, dg_ref, acc_ref):
        i, kk = pl.program_id(0), pl.program_id(1)

        @pl.when(kk == 0)
        def _():
            acc_ref[...] = jnp.zeros_like(acc_ref)

        @pl.when((i == 0) & (kk == 0))
        def _():
            dg_ref[...] = jnp.zeros_like(dg_ref)

        acc_ref[...] += _dot_nt(a_ref[...], w_ref[...])

        @pl.when(kk == nk - 1)
        def _():
            xv = x_ref[...]
            r = _rstd(xv)
            y = xv * r
            dh = acc_ref[...]
            dy = dh * g_ref[...]
            dx_ref[...] = r_ref[...] + r * (dy - y * jnp.mean(dy * y, axis=-1, keepdims=True))
            dg_ref[...] += jnp.sum(dh * y, axis=0, keepdims=True)

    return pl.pallas_call(
        body, grid=(s // tm, nk),
        in_specs=[pl.BlockSpec((tm, tk), lambda i, kk: (i, kk)), pl.BlockSpec((d, tk), lambda i, kk: (0, kk)),
                  pl.BlockSpec((tm, d), lambda i, kk: (i, 0)), pl.BlockSpec((1, d), lambda i, kk: (0, 0)),
                  pl.BlockSpec((tm, d), lambda i, kk: (i, 0))],
        out_specs=[pl.BlockSpec((tm, d), lambda i, kk: (i, 0)), pl.BlockSpec((1, d), lambda i, kk: (0, 0))],
        out_shape=[jax.ShapeDtypeStruct((s, d), F32), jax.ShapeDtypeStruct((1, d), F32)],
        scratch_shapes=[pltpu.VMEM((tm, d), F32)],
        compiler_params=_cparams(("arbitrary", "arbitrary")), name=name)(a, w, x, g, dres)


def swiglu_matmul(gu, w, x1, name):
    s = gu.shape[0]
    d = w.shape[1]
    tm = min(512, s)

    def body(gu_ref, w_ref, x_ref, o_ref):
        @pl.when(pl.program_id(1) == 0)
        def _():
            o_ref[...] = x_ref[...]

        gt = gu_ref[:, :FF_BLK].astype(F32)
        up = gu_ref[:, FF_BLK:].astype(F32)
        act = (gt * _sigmoid(gt) * up).astype(BF16)
        o_ref[...] += _dot(act, w_ref[...])

    return pl.pallas_call(
        body, grid=(s // tm, N_FF_BLKS),
        in_specs=[pl.BlockSpec((tm, 2 * FF_BLK), lambda i, j: (i, j)), pl.BlockSpec((FF_BLK, d), lambda i, j: (j, 0)),
                  pl.BlockSpec((tm, d), lambda i, j: (i, 0))],
        out_specs=pl.BlockSpec((tm, d), lambda i, j: (i, 0)),
        out_shape=jax.ShapeDtypeStruct((s, d), F32),
        compiler_params=_cparams(("parallel", "arbitrary")), name=name)(gu, w, x1)


def swiglu_bwd(dx2, gu, w, name):
    s, d = dx2.shape
    tm = min(512, s)

    def body(dx_ref, gu_ref, w_ref, dgu_ref, act_ref):
        dact = _dot_nt(dx_ref[...].astype(BF16), w_ref[...])
        gt = gu_ref[:, :FF_BLK].astype(F32)
        up = gu_ref[:, FF_BLK:].astype(F32)
        sg = _sigmoid(gt)
        act_ref[...] = (gt * sg * up).astype(BF16)
        dgu_ref[:, :FF_BLK] = (dact * up * (sg * (1.0 + gt * (1.0 - sg)))).astype(BF16)
        dgu_ref[:, FF_BLK:] = (dact * gt * sg).astype(BF16)

    return pl.pallas_call(
        body, grid=(s // tm, N_FF_BLKS),
        in_specs=[pl.BlockSpec((tm, d), lambda i, j: (i, 0)), pl.BlockSpec((tm, 2 * FF_BLK), lambda i, j: (i, j)),
                  pl.BlockSpec((FF_BLK, d), lambda i, j: (j, 0))],
        out_specs=[pl.BlockSpec((tm, 2 * FF_BLK), lambda i, j: (i, j)), pl.BlockSpec((tm, FF_BLK), lambda i, j: (i, j))],
        out_shape=[jax.ShapeDtypeStruct((s, 2 * D_FF), BF16), jax.ShapeDtypeStruct((s, D_FF), BF16)],
        compiler_params=_cparams(("parallel", "arbitrary")), name=name)(dx2, gu, w)


def loss_kernel(y, tgt, name):
    s, d = y.shape
    tm = min(512, s)

    def body(y_ref, t_ref, l_ref, dy_ref):
        @pl.when(pl.program_id(0) == 0)
        def _():
            l_ref[...] = jnp.zeros_like(l_ref)

        err = y_ref[...] - t_ref[...]
        dy_ref[...] = err * (1.0 / d)
        l_ref[...] += jnp.sum(jnp.sum(err * err, axis=1, keepdims=True), axis=0, keepdims=True)

    return pl.pallas_call(
        body, grid=(s // tm,),
        in_specs=[pl.BlockSpec((tm, d), lambda i: (i, 0)), pl.BlockSpec((tm, d), lambda i: (i, 0))],
        out_specs=[pl.BlockSpec((8, 128), lambda i: (0, 0)), pl.BlockSpec((tm, d), lambda i: (i, 0))],
        out_shape=[jax.ShapeDtypeStruct((8, 128), F32), jax.ShapeDtypeStruct((s, d), F32)],
        compiler_params=_cparams(("arbitrary",)), name=name)(y, tgt)


def _split3(v):
    a1 = v.astype(BF16)
    r1 = v - a1.astype(F32)
    a2 = r1.astype(BF16)
    a3 = (r1 - a2.astype(F32)).astype(BF16)
    return a1, a2, a3


def forget_fwd(h, wf, b, name):
    s, d = h.shape
    tm = min(512, s)

    def body(h_ref, w_ref, b_ref, z_ref, c_ref, carry_ref):
        @pl.when(pl.program_id(0) == 0)
        def _():
            carry_ref[...] = jnp.zeros_like(carry_ref)

        z = _dot(h_ref[...], w_ref[...]) + b_ref[...]
        z_ref[...] = z
        logf = jnp.minimum(z, 0.0) - jnp.log(1.0 + jnp.exp(-jnp.abs(z)))
        row = lax.broadcasted_iota(jnp.int32, (tm, tm), 0)
        col = lax.broadcasted_iota(jnp.int32, (tm, tm), 1)
        tri = (row >= col).astype(BF16)
        a1, a2, a3 = _split3(logf)
        c = _dot(tri, a1) + _dot(tri, a2) + _dot(tri, a3) + carry_ref[...]
        c_ref[...] = c
        carry_ref[...] = c[tm - 1:tm, :]

    return pl.pallas_call(
        body, grid=(s // tm,),
        in_specs=[pl.BlockSpec((tm, d), lambda i: (i, 0)), pl.BlockSpec((d, 128), lambda i: (0, 0)),
                  pl.BlockSpec((1, 128), lambda i: (0, 0))],
        out_specs=[pl.BlockSpec((tm, 128), lambda i: (i, 0)), pl.BlockSpec((tm, 128), lambda i: (i, 0))],
        out_shape=[jax.ShapeDtypeStruct((s, 128), F32), jax.ShapeDtypeStruct((s, 128), F32)],
        scratch_shapes=[pltpu.VMEM((1, 128), F32)],
        compiler_params=_cparams(("arbitrary",)), name=name)(h, wf, b)


def forget_bwd(dc, z, dproj, name):
    s = dc.shape[0]
    tm = min(512, s)
    nt = s // tm

    def body(dc_ref, z_ref, dp_ref, dz_ref, db_ref, carry_ref):
        @pl.when(pl.program_id(0) == 0)
        def _():
            carry_ref[...] = jnp.zeros_like(carry_ref)
            db_ref[...] = jnp.zeros_like(db_ref)

        row = lax.broadcasted_iota(jnp.int32, (tm, tm), 0)
        col = lax.broadcasted_iota(jnp.int32, (tm, tm), 1)
        tri = (col >= row).astype(BF16)
        a1, a2, a3 = _split3(dc_ref[...])
        dlogf = _dot(tri, a1) + _dot(tri, a2) + _dot(tri, a3) + carry_ref[...]
        carry_ref[...] = dlogf[0:1, :]
        dz = dlogf * (1.0 - _sigmoid(z_ref[...]))
        dz_ref[...] = dz.astype(BF16)
        db_ref[...] += jnp.sum(dz, axis=0, keepdims=True)

    return pl.pallas_call(
        body, grid=(nt,),
        in_specs=[pl.BlockSpec((tm, 128), lambda i: (nt - 1 - i, 0)), pl.BlockSpec((tm, 128), lambda i: (nt - 1 - i, 0)),
                  pl.BlockSpec(memory_space=pl.ANY)],
        out_specs=[pl.BlockSpec((tm, 128), lambda i: (nt - 1 - i, N_MAIN // 128)), pl.BlockSpec((1, 128), lambda i: (0, 0))],
        out_shape=[jax.ShapeDtypeStruct(dproj.shape, BF16), jax.ShapeDtypeStruct((1, 128), F32)],
        scratch_shapes=[pltpu.VMEM((1, 128), F32)], input_output_aliases={2: 0},
        compiler_params=_cparams(("arbitrary",)), name=name)(dc, z, dproj)


HEAD_GROUP = 2
LANE_C = 64
LANE_ONE = 67


def _lanes():
    lane = lax.broadcasted_iota(jnp.int32, (1, 128), 1)
    return lane, lane < HEAD_DIM


def _half_mean(t, lo):
    s_lo = jnp.sum(jnp.where(lo, t, 0.0), axis=-1, keepdims=True)
    s_hi = jnp.sum(jnp.where(lo, 0.0, t), axis=-1, keepdims=True)
    return jnp.where(lo, s_lo, s_hi) * (1.0 / HEAD_DIM)


def _lane_col(t, lane, idx):
    return jnp.sum(jnp.where(lane == idx, t, 0.0), axis=-1, keepdims=True)


def _swap_halves(t):
    return pltpu.roll(t, HEAD_DIM, 1)


def _causal(s_blk, tq, tk):
    row = lax.broadcasted_iota(jnp.int32, (tq, tk), 0)
    col = lax.broadcasted_iota(jnp.int32, (tq, tk), 1)
    return jnp.where(row >= col, s_blk, -jnp.inf)


def attn_prep(proj, c, gq2, gk2, name):
    s = proj.shape[0]
    tm = min(512, s)
    first = N_REST // 128

    def body(q_ref, k_ref, v_ref, c_ref, gq_ref, gk_ref, qa_ref, ka_ref, va_ref):
        j = pl.program_id(1)
        lane, lo = _lanes()

        def normed(ref, g):
            t = ref[...].astype(F32)
            return t * lax.rsqrt(_half_mean(t * t, lo) + EPS) * g

        qn = normed(q_ref, gq_ref[...] * ATTN_SCALE)
        kn = normed(k_ref, gk_ref[...])
        vv = v_ref[...].astype(F32)
        cv = c_ref[...]
        one_q = jnp.where((lane >= LANE_ONE) & (lane < LANE_ONE + 3), 1.0, 0.0)
        one_k = jnp.where((lane >= LANE_C) & (lane < LANE_C + 3), 1.0, 0.0)
        one_v = jnp.where(lane == LANE_C, 1.0, 0.0)
        for e in range(2):
            pick = (lambda t: t) if e == 0 else _swap_halves
            pieces = [p.astype(F32) for p in _split3(_lane_col(cv, lane, 2 * j + e))]
            ext_q, ext_k = one_q, one_k
            for i, p in enumerate(pieces):
                ext_q = jnp.where(lane == LANE_C + i, p, ext_q)
                ext_k = jnp.where(lane == LANE_ONE + i, -p, ext_k)
            qa_ref[e] = jnp.where(lo, pick(qn), ext_q).astype(BF16)
            ka_ref[e] = jnp.where(lo, pick(kn), ext_k).astype(BF16)
            va_ref[e] = jnp.where(lo, pick(vv), one_v).astype(BF16)

    tile = lambda base: pl.BlockSpec((tm, 128), lambda i, j: (i, base + j))
    vec = pl.BlockSpec((1, 128), lambda i, j: (0, 0))
    out = pl.BlockSpec((2, tm, 128), lambda i, j: (j, i, 0))
    return pl.pallas_call(
        body, grid=(s // tm, HEADS // 2),
        in_specs=[tile(first), tile(first + 4), tile(first + 8), pl.BlockSpec((tm, 128), lambda i, j: (i, 0)), vec, vec],
        out_specs=[out, out, out], out_shape=[jax.ShapeDtypeStruct((HEADS, s, 128), BF16)] * 3,
        compiler_params=_cparams(("parallel", "arbitrary")), name=name)(proj, proj, proj, c, gq2, gk2)


def attn_fwd(q, k, v, ccol, crow, gq, gk, name):
    hh, s, hd = q.shape
    tq = tk = min(512, s)
    nq = s // tq

    def body(q_ref, k_ref, v_ref, cc_ref, cr_ref, gq_ref, gk_ref, o_ref, lse_ref, qn_ref, m_ref, l_ref, acc_ref):
        qi, ki = pl.program_id(1), pl.program_id(2)

        @pl.when(ki == 0)
        def _():
            qn_ref[...] = _qk_hat(q_ref, gq_ref, ATTN_SCALE)
            m_ref[...] = jnp.full_like(m_ref, -jnp.inf)
            l_ref[...] = jnp.zeros_like(l_ref)
            acc_ref[...] = jnp.zeros_like(acc_ref)

        @pl.when(ki <= qi)
        def _():
            kn = _qk_hat(k_ref, gk_ref, 1.0)
            sb = _dot_nt(qn_ref[...], kn) + (cc_ref[...] - cr_ref[...])
            sb = _causal(sb, qi, ki, tq, tk)
            m_new = jnp.maximum(m_ref[...], jnp.max(sb, axis=-1, keepdims=True))
            alpha = jnp.exp(m_ref[...] - m_new)
            p = jnp.exp(sb - m_new)
            l_ref[...] = alpha * l_ref[...] + jnp.sum(p, axis=-1, keepdims=True)
            acc_ref[...] = alpha * acc_ref[...] + _dot(p.astype(BF16), v_ref[...])
            m_ref[...] = m_new

        @pl.when(ki == qi)
        def _():
            o_ref[...] = (acc_ref[...] / l_ref[...]).astype(BF16)
            lse_ref[...] = m_ref[...] + jnp.log(l_ref[...])

    qspec = pl.BlockSpec((None, tq, hd), lambda h, i, j: (h, i, 0))
    kspec = pl.BlockSpec((None, tk, hd), lambda h, i, j: (h, jnp.minimum(i, j), 0))
    gspec = pl.BlockSpec((1, hd), lambda h, i, j: (0, 0))
    return pl.pallas_call(
        body, grid=(hh, nq, nq),
        in_specs=[qspec, kspec, kspec,
                  pl.BlockSpec((None, tq, 1), lambda h, i, j: (h, i, 0)),
                  pl.BlockSpec((None, 1, tk), lambda h, i, j: (h, 0, jnp.minimum(i, j))), gspec, gspec],
        out_specs=[qspec, pl.BlockSpec((None, tq, 1), lambda h, i, j: (h, i, 0))],
        out_shape=[jax.ShapeDtypeStruct((hh, s, hd), BF16), jax.ShapeDtypeStruct((hh, s, 1), F32)],
        scratch_shapes=[pltpu.VMEM((tq, hd), BF16), pltpu.VMEM((tq, 1), F32), pltpu.VMEM((tq, 1), F32),
                        pltpu.VMEM((tq, hd), F32)],
        compiler_params=_cparams(("parallel", "parallel", "arbitrary")), name=name)(q, k, v, ccol, crow, gq, gk)


def attn_bwd_dq(q, k, v, o, do, lse, ccol, crow, gq, gk, name):
    hh, s, hd = q.shape
    tq = tk = min(512, s)
    nq = s // tq

    def body(q_ref, k_ref, v_ref, o_ref, do_ref, lse_ref, cc_ref, cr_ref, gq_ref, gk_ref,
             dq_ref, dcc_ref, dg_ref, qn_ref, dl_ref, acc_ref, dca_ref):
        h, qi, ki = pl.program_id(0), pl.program_id(1), pl.program_id(2)

        @pl.when((h == 0) & (qi == 0) & (ki == 0))
        def _():
            dg_ref[...] = jnp.zeros_like(dg_ref)

        @pl.when(ki == 0)
        def _():
            qn_ref[...] = _qk_hat(q_ref, gq_ref, ATTN_SCALE)
            dl_ref[...] = jnp.sum(do_ref[...].astype(F32) * o_ref[...].astype(F32), axis=-1, keepdims=True)
            acc_ref[...] = jnp.zeros_like(acc_ref)
            dca_ref[...] = jnp.zeros_like(dca_ref)

        @pl.when(ki <= qi)
        def _():
            kn = _qk_hat(k_ref, gk_ref, 1.0)
            sb = _dot_nt(qn_ref[...], kn) + (cc_ref[...] - cr_ref[...])
            p = jnp.exp(_causal(sb, qi, ki, tq, tk) - lse_ref[...])
            dp = _dot_nt(do_ref[...], v_ref[...])
            ds = p * (dp - dl_ref[...])
            acc_ref[...] += _dot(ds.astype(BF16), kn)
            dca_ref[...] += jnp.sum(ds, axis=-1, keepdims=True)

        @pl.when(ki == qi)
        def _():
            dq, dg = _norm_bwd(q_ref[...].astype(F32), gq_ref[...], acc_ref[...], ATTN_SCALE)
            dq_ref[...] = dq.astype(BF16)
            dcc_ref[...] = dca_ref[...]
            dg_ref[...] += dg

    qspec = pl.BlockSpec((None, tq, hd), lambda h, i, j: (h, i, 0))
    kspec = pl.BlockSpec((None, tk, hd), lambda h, i, j: (h, jnp.minimum(i, j), 0))
    cspec = pl.BlockSpec((None, tq, 1), lambda h, i, j: (h, i, 0))
    gspec = pl.BlockSpec((1, hd), lambda h, i, j: (0, 0))
    return pl.pallas_call(
        body, grid=(hh, nq, nq),
        in_specs=[qspec, kspec, kspec, qspec, qspec, cspec, cspec,
                  pl.BlockSpec((None, 1, tk), lambda h, i, j: (h, 0, jnp.minimum(i, j))), gspec, gspec],
        out_specs=[qspec, cspec, gspec],
        out_shape=[jax.ShapeDtypeStruct((hh, s, hd), BF16), jax.ShapeDtypeStruct((hh, s, 1), F32),
                   jax.ShapeDtypeStruct((1, hd), F32)],
        scratch_shapes=[pltpu.VMEM((tq, hd), BF16), pltpu.VMEM((tq, 1), F32), pltpu.VMEM((tq, hd), F32),
                        pltpu.VMEM((tq, 1), F32)],
        compiler_params=_cparams(("arbitrary", "arbitrary", "arbitrary")), name=name)(
            q, k, v, o, do, lse, ccol, crow, gq, gk)


def attn_bwd_dkv(q, k, v, o, do, lse, ccol, crow, gq, gk, name):
    hh, s, hd = q.shape
    tq = tk = min(512, s)
    nq = s // tq

    def body(q_ref, k_ref, v_ref, o_ref, do_ref, lse_ref, cc_ref, cr_ref, gq_ref, gk_ref,
             dk_ref, dv_ref, dcr_ref, dg_ref, kn_ref, dka_ref, dva_ref, dca_ref):
        h, ki, qi = pl.program_id(0), pl.program_id(1), pl.program_id(2)

        @pl.when((h == 0) & (ki == 0) & (qi == 0))
        def _():
            dg_ref[...] = jnp.zeros_like(dg_ref)

        @pl.when(qi == 0)
        def _():
            kn_ref[...] = _qk_hat(k_ref, gk_ref, 1.0)
            dka_ref[...] = jnp.zeros_like(dka_ref)
            dva_ref[...] = jnp.zeros_like(dva_ref)
            dca_ref[...] = jnp.zeros_like(dca_ref)

        @pl.when(qi >= ki)
        def _():
            qn = _qk_hat(q_ref, gq_ref, ATTN_SCALE)
            do = do_ref[...]
            delta = jnp.sum(do.astype(F32) * o_ref[...].astype(F32), axis=-1, keepdims=True)
            sb = _dot_nt(qn, kn_ref[...]) + (cc_ref[...] - cr_ref[...])
            p = jnp.exp(_causal(sb, qi, ki, tq, tk) - lse_ref[...])
            dva_ref[...] += _dot_tn(p.astype(BF16), do)
            ds = p * (_dot_nt(do, v_ref[...]) - delta)
            dka_ref[...] += _dot_tn(ds.astype(BF16), qn)
            dca_ref[...] += jnp.sum(ds, axis=0, keepdims=True)

        @pl.when(qi == nq - 1)
        def _():
            dk, dg = _norm_bwd(k_ref[...].astype(F32), gk_ref[...], dka_ref[...], 1.0)
            dk_ref[...] = dk.astype(BF16)
            dv_ref[...] = dva_ref[...].astype(BF16)
            dcr_ref[...] = dca_ref[...]
            dg_ref[...] += dg

    kspec = pl.BlockSpec((None, tk, hd), lambda h, j, i: (h, j, 0))
    qspec = pl.BlockSpec((None, tq, hd), lambda h, j, i: (h, jnp.maximum(i, j), 0))
    cspec = pl.BlockSpec((None, tq, 1), lambda h, j, i: (h, jnp.maximum(i, j), 0))
    rspec = pl.BlockSpec((None, 1, tk), lambda h, j, i: (h, 0, j))
    gspec = pl.BlockSpec((1, hd), lambda h, j, i: (0, 0))
    return pl.pallas_call(
        body, grid=(hh, nq, nq),
        in_specs=[qspec, kspec, kspec, qspec, qspec, cspec, cspec, rspec, gspec, gspec],
        out_specs=[kspec, kspec, rspec, gspec],
        out_shape=[jax.ShapeDtypeStruct((hh, s, hd), BF16), jax.ShapeDtypeStruct((hh, s, hd), BF16),
                   jax.ShapeDtypeStruct((hh, 1, s), F32), jax.ShapeDtypeStruct((1, hd), F32)],
        scratch_shapes=[pltpu.VMEM((tk, hd), BF16), pltpu.VMEM((tk, hd), F32), pltpu.VMEM((tk, hd), F32),
                        pltpu.VMEM((1, tk), F32)],
        compiler_params=_cparams(("arbitrary", "arbitrary", "arbitrary")), name=name)(
            q, k, v, o, do, lse, ccol, crow, gq, gk)


def _carry(ex, n_in, n_out, n_scratch, grid):
    n_xin, n_xout = (len(ex.inputs), len(ex.out_shapes)) if ex else (0, 0)

    def split(refs):
        ins, xins = refs[:n_in], refs[n_in:n_in + n_xin]
        rest = refs[n_in + n_xin:]
        outs, xouts = rest[:n_out], rest[n_out:n_out + n_xout]
        rest = rest[n_out + n_xout:]
        return ins + outs + rest[:n_scratch], (xins, xouts, rest[n_scratch:])

    def first():
        return functools.reduce(lambda a, b: a & b, [pl.program_id(d) == 0 for d in range(len(grid))])

    def last():
        return functools.reduce(lambda a, b: a & b, [pl.program_id(d) == grid[d] - 1 for d in range(len(grid))])

    return split, first, last


def _carried_call(body, ex, grid, in_specs, out_specs, out_shape, scratch, sem, name, operands):
    any_spec = pl.BlockSpec(memory_space=pl.ANY)
    split, first, last = _carry(ex, len(in_specs), len(out_specs), len(scratch), grid)

    def carried(*refs):
        own, xrefs = split(refs)
        if ex:
            @pl.when(first())
            def _():
                ex.start(*xrefs)

        body(*own)
        if ex:
            @pl.when(last())
            def _():
                ex.drain(*xrefs)

    n_xin = len(ex.inputs) if ex else 0
    results = pl.pallas_call(
        carried, grid=grid, in_specs=list(in_specs) + [any_spec] * n_xin,
        out_specs=list(out_specs) + [any_spec] * (len(ex.out_shapes) if ex else 0),
        out_shape=list(out_shape) + (list(ex.out_shapes) if ex else []),
        input_output_aliases={len(in_specs) + i: len(out_specs) + o for i, o in ex.aliases.items()} if ex else {},
        scratch_shapes=list(scratch) + (ex.scratch if ex else []),
        compiler_params=_cparams(sem), name=name)(*operands, *(ex.inputs if ex else []))
    return results[:len(out_specs)], results[len(out_specs):]


def attn_forward(qa, ka, va, name, ex=None):
    hh, s, _ = qa.shape
    tq = tk = min(512, s)
    nq = s // tq
    grp = HEAD_GROUP

    def body(q_ref, k_ref, v_ref, o_ref, lse_ref, m_ref, acc_ref):
        qi, ki = pl.program_id(1), pl.program_id(2)
        lane, _ = _lanes()

        @pl.when(ki == 0)
        def _():
            m_ref[...] = jnp.full_like(m_ref, -jnp.inf)
            acc_ref[...] = jnp.zeros_like(acc_ref)

        def step(masked):
            for g in range(grp):
                sb = _dot_nt(q_ref[g], k_ref[g])
                if masked:
                    sb = _causal(sb, tq, tk)
                m_old = m_ref[g]
                m_new = jnp.maximum(m_old, jnp.max(sb, axis=-1, keepdims=True))
                p = jnp.exp(sb - m_new)
                acc_ref[g] = jnp.exp(m_old - m_new) * acc_ref[g] + _dot(p.astype(BF16), v_ref[g])
                m_ref[g] = m_new

        @pl.when(ki < qi)
        def _():
            step(False)

        @pl.when(ki == qi)
        def _():
            step(True)
            for g in range(grp):
                acc = acc_ref[g]
                denom = _lane_col(acc, lane, LANE_C)
                o_ref[g] = (acc / denom).astype(BF16)
                lse_ref[g] = m_ref[g] + jnp.log(denom)

    qspec = pl.BlockSpec((grp, tq, 128), lambda h, i, j: (h, i, 0))
    kspec = pl.BlockSpec((grp, tk, 128), lambda h, i, j: (h, jnp.minimum(i, j), 0))
    lspec = pl.BlockSpec((grp, tq, 1), lambda h, i, j: (h, i, 0))
    return _carried_call(
        body, ex, (hh // grp, nq, nq), [qspec, kspec, kspec], [qspec, lspec],
        [jax.ShapeDtypeStruct((hh, s, 128), BF16), jax.ShapeDtypeStruct((hh, s, 1), F32)],
        [pltpu.VMEM((grp, tq, 1), F32), pltpu.VMEM((grp, tq, 128), F32)],
        ("arbitrary", "arbitrary", "arbitrary"), name, (qa, ka, va))


def attn_backward(qa, ka, va, oa, doa, lse, name, ex=None):
    hh, s, _ = qa.shape
    tq = tk = min(512, s)
    nq = s // tq
    grp = HEAD_GROUP

    def body(q_ref, k_ref, v_ref, o_ref, do_ref, lse_ref, dq_ref, dk_ref, dv_ref, dka_ref, dva_ref):
        ki, qi = pl.program_id(1), pl.program_id(2)

        @pl.when((ki == 0) & (qi == 0))
        def _():
            dq_ref[...] = jnp.zeros_like(dq_ref)

        @pl.when(qi == 0)
        def _():
            dka_ref[...] = jnp.zeros_like(dka_ref)
            dva_ref[...] = jnp.zeros_like(dva_ref)

        def step(masked):
            rows = pl.ds(pl.multiple_of(qi * tq, tq), tq)
            for g in range(grp):
                q, k, do = q_ref[g], k_ref[g], do_ref[g]
                sb = _dot_nt(q, k)
                if masked:
                    sb = _causal(sb, tq, tk)
                p = jnp.exp(sb - lse_ref[g])
                delta = jnp.sum(do.astype(F32) * o_ref[g].astype(F32), axis=-1, keepdims=True)
                ds = (p * (_dot_nt(do, v_ref[g]) - delta)).astype(BF16)
                dva_ref[g] += _dot_tn(p.astype(BF16), do)
                dka_ref[g] += _dot_tn(ds, q)
                dq_ref[g, rows, :] += _dot(ds, k)

        @pl.when(qi > ki)
        def _():
            step(False)

        @pl.when(qi == ki)
        def _():
            step(True)

        @pl.when(qi == nq - 1)
        def _():
            dk_ref[...] = dka_ref[...]
            dv_ref[...] = dva_ref[...].astype(BF16)

    qspec = pl.BlockSpec((grp, tq, 128), lambda h, j, i: (h, jnp.maximum(i, j), 0))
    lspec = pl.BlockSpec((grp, tq, 1), lambda h, j, i: (h, jnp.maximum(i, j), 0))
    kspec = pl.BlockSpec((grp, tk, 128), lambda h, j, i: (h, j, 0))
    return _carried_call(
        body, ex, (hh // grp, nq, nq), [qspec, kspec, kspec, qspec, qspec, lspec],
        [pl.BlockSpec((grp, s, 128), lambda h, j, i: (h, 0, 0)), kspec, kspec],
        [jax.ShapeDtypeStruct((hh, s, 128), F32), jax.ShapeDtypeStruct((hh, s, 128), F32),
         jax.ShapeDtypeStruct((hh, s, 128), BF16)],
        [pltpu.VMEM((grp, tk, 128), F32), pltpu.VMEM((grp, tk, 128), F32)],
        ("arbitrary", "arbitrary", "arbitrary"), name, (qa, ka, va, oa, doa, lse))


def attn_post(dqa, dka, dva, proj, gq2, gk2, dproj, name):
    s = proj.shape[0]
    tm = min(256, s)

    def body(dq_ref, dk_ref, dv_ref, q_ref, k_ref, gq_ref, gk_ref, dp_any, dp_ref, dc_ref, dgq_ref, dgk_ref):
        lane, lo = _lanes()

        @pl.when(pl.program_id(0) == 0)
        def _():
            dgq_ref[...] = jnp.zeros_like(dgq_ref)
            dgk_ref[...] = jnp.zeros_like(dgk_ref)

        def pair(ref, j):
            return jnp.where(lo, ref[2 * j].astype(F32), _swap_halves(ref[2 * j + 1].astype(F32)))

        def norm_bwd(raw, g, dhat, scale):
            r = lax.rsqrt(_half_mean(raw * raw, lo) + EPS)
            y = raw * r
            dy = dhat * (g * scale)
            return r * (dy - y * _half_mean(dy * y, lo)), jnp.sum(dhat * y, axis=0, keepdims=True) * scale

        dc = jnp.zeros((tm, 128), F32)
        for j in range(HEADS // 2):
            cols = slice(128 * j, 128 * (j + 1))
            dq, dgq = norm_bwd(q_ref[:, cols].astype(F32), gq_ref[...], pair(dq_ref, j), ATTN_SCALE)
            dk, dgk = norm_bwd(k_ref[:, cols].astype(F32), gk_ref[...], pair(dk_ref, j), 1.0)
            dgq_ref[...] += dgq
            dgk_ref[...] += dgk
            dp_ref[:, cols] = dq.astype(BF16)
            dp_ref[:, D_ATTN + 128 * j:D_ATTN + 128 * (j + 1)] = dk.astype(BF16)
            dp_ref[:, 2 * D_ATTN + 128 * j:2 * D_ATTN + 128 * (j + 1)] = pair(dv_ref, j).astype(BF16)
            for e in range(2):
                h = 2 * j + e
                col = _lane_col(dq_ref[h], lane, LANE_C) - _lane_col(dk_ref[h], lane, LANE_ONE)
                dc = jnp.where(lane == h, col, dc)
        dp_ref[:, 3 * D_ATTN:] = jnp.zeros((tm, DPROJ_TAIL - 3 * D_ATTN), BF16)
        dc_ref[...] = dc

    heads = lambda: pl.BlockSpec((HEADS, tm, 128), lambda i: (0, i, 0))
    vec = pl.BlockSpec((1, 128), lambda i: (0, 0))
    first = N_REST // D_ATTN
    return pl.pallas_call(
        body, grid=(s // tm,),
        in_specs=[heads(), heads(), heads(), pl.BlockSpec((tm, D_ATTN), lambda i: (i, first)),
                  pl.BlockSpec((tm, D_ATTN), lambda i: (i, first + 1)), vec, vec, pl.BlockSpec(memory_space=pl.ANY)],
        out_specs=[pl.BlockSpec((tm, DPROJ_TAIL), lambda i: (i, N_REST // DPROJ_TAIL)),
                   pl.BlockSpec((tm, 128), lambda i: (i, 0)), vec, vec],
        out_shape=[jax.ShapeDtypeStruct(dproj.shape, BF16), jax.ShapeDtypeStruct((s, 128), F32),
                   jax.ShapeDtypeStruct((1, 128), F32), jax.ShapeDtypeStruct((1, 128), F32)],
        input_output_aliases={7: 0},
        compiler_params=_cparams(("arbitrary",)), name=name)(dqa, dka, dva, proj, proj, gq2, gk2, dproj)


def _pool_groups(tm):
    gid = lax.broadcasted_iota(jnp.int32, (1, D_POOL), 1) // (D_POOL // 4)
    win = jnp.where(gid == 0, 2.0, jnp.where(gid == 1, 4.0, jnp.where(gid == 2, 8.0, 16.0)))
    return gid, win


def _by_group(gid, v2, v4, v8, v16):
    return jnp.where(gid == 0, v2, jnp.where(gid == 1, v4, jnp.where(gid == 2, v8, v16)))


def _branches(rest_ref, halo_ref, a_ref, wa_ref, wc_ref, wp_ref, sc_ref, cw_ref, ti, tm):
    f = lambda v: v.astype(F32)
    cx, cb, cc, px = f(rest_ref[:, 0:256]), f(rest_ref[:, 256:512]), f(rest_ref[:, 512:768]), f(rest_ref[:, 768:1024])
    live = jnp.where(ti > 0, 1.0, 0.0)
    hz = f(halo_ref[:, 0:256]) * f(halo_ref[:, 512:768]) * live
    hp = f(halo_ref[:, 768:1024]) * live
    z = cc * cx
    zf = jnp.concatenate([hz, z], axis=0)
    z1 = pltpu.roll(zf, 1, 0)[HALO:]
    z2 = pltpu.roll(zf, 2, 0)[HALO:]
    cw = cw_ref[...]
    conv = cw[2:3] * z + cw[1:2] * z1 + cw[0:1] * z2
    uc = cb * conv
    pf = jnp.concatenate([hp, px], axis=0)
    s2 = pf + pltpu.roll(pf, 1, 0)
    s4 = s2 + pltpu.roll(s2, 2, 0)
    s8 = s4 + pltpu.roll(s4, 4, 0)
    s16 = s8 + pltpu.roll(s8, 8, 0)
    gid, win = _pool_groups(tm)
    t = (ti * tm + lax.broadcasted_iota(jnp.int32, (tm, 1), 0)).astype(F32)
    inv = 1.0 / jnp.minimum(t + 1.0, win)
    dpool = _by_group(gid, s2[HALO:], s4[HALO:], s8[HALO:], s16[HALO:]) * inv - px
    _, lo = _lanes()
    a_tok = [jnp.where(lo, f(a_ref[2 * j]), _swap_halves(f(a_ref[2 * j + 1]))).astype(BF16) for j in range(HEADS // 2)]
    y_attn = _dot(a_tok[0], wa_ref[0:128, :])
    for j in range(1, HEADS // 2):
        y_attn += _dot(a_tok[j], wa_ref[128 * j:128 * (j + 1), :])
    y_conv = _dot(uc.astype(BF16), wc_ref[...])
    y_pool_raw = _dot(dpool.astype(BF16), wp_ref[...])
    sg = [_sigmoid(f(rest_ref[:, 1024 + i * D_MODEL:1024 + (i + 1) * D_MODEL])) for i in range(3)]
    return dict(cx=cx, cb=cb, cc=cc, z=z, z1=z1, z2=z2, conv=conv, uc=uc, dpool=dpool, inv=inv, gid=gid, a_tok=a_tok,
                y_attn=y_attn, y_conv=y_conv, y_pool_raw=y_pool_raw, sg=sg, cw=cw)


def _mix_specs(tm, ti_of):
    blocks_per_tile = tm // HALO
    return [
        pl.BlockSpec((tm, N_REST), lambda i: (ti_of(i), 0)),
        pl.BlockSpec((HALO, 1024), lambda i: (jnp.maximum(ti_of(i) * blocks_per_tile - 1, 0), 0)),
        pl.BlockSpec((HEADS, tm, 128), lambda i: (0, ti_of(i), 0)),
        pl.BlockSpec((D_ATTN, D_MODEL), lambda i: (0, 0)),
        pl.BlockSpec((D_CONV, D_MODEL), lambda i: (0, 0)),
        pl.BlockSpec((D_POOL, D_MODEL), lambda i: (0, 0)),
        pl.BlockSpec((1, D_MODEL), lambda i: (0, 0)),
        pl.BlockSpec((8, D_CONV), lambda i: (0, 0)),
    ]


def mix_fwd(proj, a, x, wa, wc, wp, scale, cw, wo, name):
    s = x.shape[0]
    tm = min(256, s)

    def body(rest_ref, halo_ref, a_ref, wa_ref, wc_ref, wp_ref, sc_ref, cw_ref, wo_ref, x_ref, o_ref):
        b = _branches(rest_ref, halo_ref, a_ref, wa_ref, wc_ref, wp_ref, sc_ref, cw_ref, pl.program_id(0), tm)
        merged = b["sg"][0] * b["y_attn"] + b["sg"][1] * b["y_conv"] + b["sg"][2] * (b["y_pool_raw"] * sc_ref[...])
        o_ref[...] = x_ref[...] + _dot(merged.astype(BF16), wo_ref[...])

    return pl.pallas_call(
        body, grid=(s // tm,),
        in_specs=_mix_specs(tm, lambda i: i) + [pl.BlockSpec((D_MODEL, D_MODEL), lambda i: (0, 0)),
                                                 pl.BlockSpec((tm, D_MODEL), lambda i: (i, 0))],
        out_specs=pl.BlockSpec((tm, D_MODEL), lambda i: (i, 0)),
        out_shape=jax.ShapeDtypeStruct((s, D_MODEL), F32),
        compiler_params=_cparams(("parallel",)), name=name)(proj, proj, a, wa, wc, wp, scale, cw, wo, x)


def mix_bwd(proj, a, dx1, wa, wc, wp, scale, cw, wo, name):
    s = dx1.shape[0]
    tm = min(256, s)
    nt = s // tm
    ti_of = lambda i: nt - 1 - i
    n = tm + HALO

    def body(rest_ref, halo_ref, a_ref, wa_ref, wc_ref, wp_ref, sc_ref, cw_ref, wo_ref,
             dx_ref, dp_ref, da_ref, at_ref, mg_ref, dya_ref, dyc_ref, dyp_ref, uc_ref, dd_ref, dsc_ref, dcw_ref,
             cdc_ref, cde_ref):
        i = pl.program_id(0)
        ti = ti_of(i)

        @pl.when(i == 0)
        def _():
            cdc_ref[...] = jnp.zeros_like(cdc_ref)
            cde_ref[...] = jnp.zeros_like(cde_ref)
            dsc_ref[...] = jnp.zeros_like(dsc_ref)
            dcw_ref[...] = jnp.zeros_like(dcw_ref)

        b = _branches(rest_ref, halo_ref, a_ref, wa_ref, wc_ref, wp_ref, sc_ref, cw_ref, ti, tm)
        sg, sc = b["sg"], sc_ref[...]
        y_pool = b["y_pool_raw"] * sc
        merged = sg[0] * b["y_attn"] + sg[1] * b["y_conv"] + sg[2] * y_pool
        mg_ref[...] = merged.astype(BF16)
        dm = _dot_nt(dx_ref[...].astype(BF16), wo_ref[...])
        for j, y in enumerate((b["y_attn"], b["y_conv"], y_pool)):
            dp_ref[:, 1024 + j * D_MODEL:1024 + (j + 1) * D_MODEL] = (dm * y * sg[j] * (1.0 - sg[j])).astype(BF16)
        dya = (dm * sg[0]).astype(BF16)
        dya_ref[...] = dya
        _, lo = _lanes()
        for j in range(HEADS // 2):
            at_ref[:, 128 * j:128 * (j + 1)] = b["a_tok"][j]
            da = _dot_nt(dya, wa_ref[128 * j:128 * (j + 1), :])
            da_ref[2 * j] = jnp.where(lo, da, 0.0).astype(BF16)
            da_ref[2 * j + 1] = jnp.where(lo, _swap_halves(da), 0.0).astype(BF16)
        dyc = (dm * sg[1]).astype(BF16)
        dyc_ref[...] = dyc
        duc = _dot_nt(dyc, wc_ref[...])
        dyp = dm * sg[2]
        dsc_ref[...] += jnp.sum(dyp * b["y_pool_raw"], axis=0, keepdims=True)
        dypr = (dyp * sc).astype(BF16)
        dyp_ref[...] = dypr
        ddp = _dot_nt(dypr, wp_ref[...])
        uc_ref[...] = b["uc"].astype(BF16)
        dd_ref[...] = b["dpool"].astype(BF16)

        dconv = duc * b["cb"]
        dp_ref[:, 256:512] = (duc * b["conv"]).astype(BF16)
        dcf = jnp.concatenate([dconv, cdc_ref[...]], axis=0)
        cw = b["cw"]
        dz = cw[2:3] * dconv + cw[1:2] * pltpu.roll(dcf, n - 1, 0)[:tm] + cw[0:1] * pltpu.roll(dcf, n - 2, 0)[:tm]
        dp_ref[:, 0:256] = (dz * b["cc"]).astype(BF16)
        dp_ref[:, 512:768] = (dz * b["cx"]).astype(BF16)
        dcw_ref[0:1, :] += jnp.sum(dconv * b["z2"], axis=0, keepdims=True)
        dcw_ref[1:2, :] += jnp.sum(dconv * b["z1"], axis=0, keepdims=True)
        dcw_ref[2:3, :] += jnp.sum(dconv * b["z"], axis=0, keepdims=True)
        cdc_ref[...] = dconv[:HALO]

        e = ddp * b["inv"]
        ef = jnp.concatenate([e, cde_ref[...]], axis=0)
        r2 = ef + pltpu.roll(ef, n - 1, 0)
        r4 = r2 + pltpu.roll(r2, n - 2, 0)
        r8 = r4 + pltpu.roll(r4, n - 4, 0)
        r16 = r8 + pltpu.roll(r8, n - 8, 0)
        dp_ref[:, 768:1024] = (_by_group(b["gid"], r2[:tm], r4[:tm], r8[:tm], r16[:tm]) - ddp).astype(BF16)
        cde_ref[...] = e[:HALO]

    tile = lambda w: pl.BlockSpec((tm, w), lambda i: (ti_of(i), 0))
    whole = lambda r, c: pl.BlockSpec((r, c), lambda i: (0, 0))
    bf = lambda w: jax.ShapeDtypeStruct((s, w), BF16)
    return pl.pallas_call(
        body, grid=(nt,),
        in_specs=_mix_specs(tm, ti_of) + [whole(D_MODEL, D_MODEL), tile(D_MODEL)],
        out_specs=[tile(N_REST), pl.BlockSpec((HEADS, tm, 128), lambda i: (0, ti_of(i), 0)), tile(D_ATTN),
                   tile(D_MODEL), tile(D_MODEL), tile(D_MODEL), tile(D_MODEL),
                   tile(D_CONV), tile(D_POOL), whole(1, D_MODEL), whole(8, D_CONV)],
        out_shape=[bf(DPROJ_COLS), jax.ShapeDtypeStruct((HEADS, s, 128), BF16), bf(D_ATTN),
                   bf(D_MODEL), bf(D_MODEL), bf(D_MODEL), bf(D_MODEL), bf(D_CONV), bf(D_POOL),
                   jax.ShapeDtypeStruct((1, D_MODEL), F32), jax.ShapeDtypeStruct((8, D_CONV), F32)],
        scratch_shapes=[pltpu.VMEM((HALO, D_CONV), F32), pltpu.VMEM((HALO, D_POOL), F32)],
        compiler_params=_cparams(("arbitrary",)), name=name)(proj, proj, a, wa, wc, wp, scale, cw, wo, dx1)


def _adamw_math(w, g, m, v):
    m = ADAM_B1 * m + (1.0 - ADAM_B1) * g
    v = ADAM_B2 * v + (1.0 - ADAM_B2) * (g * g)
    m_hat = m / (1.0 - ADAM_B1 ** ADAM_STEP)
    v_hat = v / (1.0 - ADAM_B2 ** ADAM_STEP)
    delta = -ADAM_LR * (m_hat / (jnp.sqrt(v_hat) + ADAM_EPS) + ADAM_WD * w)
    return delta, m, v


ADAMW_PARTS_BLOCK_BYTES = 4 * 2 ** 20


def adamw_sum(parts, w, m, v, name):
    layers, rows, cols = w.shape
    row_bytes = N_DEV * (-(-cols // 128) * 128) * parts.dtype.itemsize
    fits = [t for t in range(16, rows + 1, 16) if rows % t == 0 and t * row_bytes <= ADAMW_PARTS_BLOCK_BYTES]
    tr = max(fits) if fits else rows

    def body(p_ref, w_ref, m_ref, v_ref, g_ref, d_ref, nm_ref, nv_ref):
        g = p_ref[0].astype(F32)
        for i in range(1, N_DEV):
            g = g + p_ref[i].astype(F32)
        g_ref[...] = g
        d_ref[...], nm_ref[...], nv_ref[...] = _adamw_math(w_ref[...], g, m_ref[...], v_ref[...])

    spec = pl.BlockSpec((None, tr, cols), lambda l, i: (l, i, 0))
    return pl.pallas_call(
        body, grid=(layers, rows // tr),
        in_specs=[pl.BlockSpec((None, N_DEV, tr, cols), lambda l, i: (l, 0, i, 0)), spec, spec, spec],
        out_specs=[spec] * 4, out_shape=[jax.ShapeDtypeStruct((layers, rows, cols), F32)] * 4,
        compiler_params=_cparams(("parallel", "parallel")), name=name)(parts, w, m, v)


def _me():
    return lax.axis_index("x"), lax.axis_index("y"), lax.axis_index("c")


N_PEERS = N_DEV - 1


def all_gather(shards, name):
    n = len(shards)
    any_spec = pl.BlockSpec(memory_space=pl.ANY)

    def body(*refs):
        x_refs, out_refs = refs[:n], refs[n:2 * n]
        send_sems, recv_sems, local_sems = refs[2 * n:]
        x, y, c = _me()
        me, sibling = (x, y, c), (x, y, 1 - c)
        chips = [(1 - x, y), (x, 1 - y), (1 - x, 1 - y)]

        def copy(t, k, block, to, from_input=False):
            slot = out_refs[t].at[4 * block[0] + 2 * block[1] + block[2]]
            return pltpu.make_async_remote_copy(
                src_ref=x_refs[t] if from_input else slot, dst_ref=slot, send_sem=send_sems.at[N_PEERS * t + k],
                recv_sem=recv_sems.at[N_PEERS * t + k], device_id=to, device_id_type=pl.DeviceIdType.MESH)

        mine = [pltpu.make_async_copy(x_refs[t], out_refs[t].at[4 * x + 2 * y + c], local_sems.at[t]) for t in range(n)]
        started = []
        for t in range(n):
            mine[t].start()
            started.append(copy(t, 0, me, sibling, from_input=True))
            started += [copy(t, 1 + j, me, (*chip, c), from_input=True) for j, chip in enumerate(chips)]
        for cp in started:
            cp.start()
        for j, chip in enumerate(chips):
            for t in range(n):
                copy(t, 1 + j, (*chip, c), me).wait_recv()
                fwd = copy(t, 4 + j, (*chip, c), sibling)
                fwd.start()
                started.append(fwd)
        for t in range(n):
            copy(t, 0, sibling, me).wait_recv()
            for j, chip in enumerate(chips):
                copy(t, 4 + j, (*chip, 1 - c), me).wait_recv()
        for cp in started:
            cp.wait_send()
        for cp in mine:
            cp.wait()

    return pl.pallas_call(
        body, out_shape=[jax.ShapeDtypeStruct((N_DEV,) + s.shape, s.dtype) for s in shards],
        in_specs=[any_spec] * n, out_specs=[any_spec] * n,
        scratch_shapes=[pltpu.SemaphoreType.DMA((N_PEERS * n,)), pltpu.SemaphoreType.DMA((N_PEERS * n,)),
                        pltpu.SemaphoreType.DMA((n,))],
        name=name)(*shards)


class Exchange:
    def __init__(self, inputs, out_shapes, aliases, n, src_of, dst_of):
        self.inputs, self.out_shapes, self.aliases, self.n = inputs, out_shapes, aliases, n
        self._src_of, self._dst_of = src_of, dst_of
        self.scratch = [pltpu.SemaphoreType.DMA((N_PEERS * n,)), pltpu.SemaphoreType.DMA((N_PEERS * n,)),
                        pltpu.SemaphoreType.DMA((n,))]

    def _copies(self, ins, outs, sems):
        send_sems, recv_sems, local_sems = sems
        x, y, c = _me()
        me = 4 * x + 2 * y + c
        local, sends, recvs = [], [], []
        for t in range(self.n):
            local.append(functools.partial(
                pltpu.make_async_copy, self._src_of(ins, t, me), self._dst_of(outs, t, me), local_sems.at[t]))
            for k in range(1, N_DEV):
                px, py, pc = x ^ ((k >> 2) & 1), y ^ ((k >> 1) & 1), c ^ (k & 1)
                peer = 4 * px + 2 * py + pc
                pair = dict(send_sem=send_sems.at[N_PEERS * t + k - 1], recv_sem=recv_sems.at[N_PEERS * t + k - 1],
                            device_id_type=pl.DeviceIdType.MESH)
                sends.append(functools.partial(
                    pltpu.make_async_remote_copy, src_ref=self._src_of(ins, t, peer), dst_ref=self._dst_of(outs, t, me),
                    device_id=(px, py, pc), **pair))
                recvs.append(functools.partial(
                    pltpu.make_async_remote_copy, src_ref=self._src_of(ins, t, peer), dst_ref=self._dst_of(outs, t, peer),
                    device_id=(x, y, c), **pair))
        return local, sends, recvs

    def start(self, ins, outs, sems):
        local, sends, _ = self._copies(ins, outs, sems)
        for make in local + sends:
            make().start()

    def drain(self, ins, outs, sems):
        local, sends, recvs = self._copies(ins, outs, sems)
        for make in recvs:
            make().wait_recv()
        for make in sends:
            make().wait_send()
        for make in local:
            make().wait()


def gather_exchange(shards):
    n = len(shards)
    return Exchange(list(shards), [jax.ShapeDtypeStruct((N_DEV,) + s.shape, s.dtype) for s in shards], {}, n,
                    src_of=lambda ins, t, block: ins[t], dst_of=lambda outs, t, block: outs[t].at[block])


def scatter_exchange(blocks, bufs, layer):
    n = len(blocks)
    return Exchange(list(blocks) + list(bufs), [jax.ShapeDtypeStruct(b.shape, b.dtype) for b in bufs],
                    {n + t: t for t in range(n)}, n,
                    src_of=lambda ins, t, block: ins[t].at[block], dst_of=lambda outs, t, block: outs[t].at[layer, block])


def run_exchange(ex, name):
    any_spec = pl.BlockSpec(memory_space=pl.ANY)
    n_in, n_out = len(ex.inputs), len(ex.out_shapes)

    def body(*refs):
        ins, outs, sems = refs[:n_in], refs[n_in:n_in + n_out], refs[n_in + n_out:]
        ex.start(ins, outs, sems)
        ex.drain(ins, outs, sems)

    return pl.pallas_call(
        body, out_shape=ex.out_shapes, in_specs=[any_spec] * n_in, out_specs=[any_spec] * n_out,
        input_output_aliases=ex.aliases, scratch_shapes=ex.scratch, name=name)(*ex.inputs)


MATRICES = ("w_in", "w_attn_out", "w_conv_out", "pool_w", "w_o", "w_ffn_in", "w_ffn_out")
SHARD_INFO = {
    "w_in": ((DEPTH, D_MODEL, D_IN // N_DEV), 2),
    "w_attn_out": ((DEPTH, D_ATTN, D_MODEL // N_DEV), 2),
    "w_conv_out": ((DEPTH, D_CONV, D_MODEL // N_DEV), 2),
    "pool_w": ((DEPTH, 4, 64, 256 // N_DEV), 3),
    "w_o": ((DEPTH, D_MODEL // N_DEV, D_MODEL), 1),
    "w_ffn_in": ((DEPTH, D_MODEL, 2 * D_FF // N_DEV), 2),
    "w_ffn_out": ((DEPTH, D_FF // N_DEV, D_MODEL), 1),
}
VECTORS = ("norm_mix_g", "forget_b", "q_norm_g", "k_norm_g", "pool_scale", "norm_ffn_g")
VECTOR_SHAPES = {"norm_mix_g": (DEPTH, D_MODEL), "forget_b": (DEPTH, HEADS), "q_norm_g": (DEPTH, HEAD_DIM),
                 "k_norm_g": (DEPTH, HEAD_DIM), "pool_scale": (DEPTH, D_MODEL), "norm_ffn_g": (DEPTH, D_MODEL)}
CONV_W_FULL = (DEPTH, 3, D_CONV)


def _size(shape):
    n = 1
    for v in shape:
        n *= v
    return n


def _pack(arrays, rows, cols):
    flat = jnp.concatenate([a.reshape(-1) for a in arrays])
    return jnp.pad(flat, (0, rows * cols - flat.shape[0])).reshape(rows, cols)


def _unpack(packed, shapes):
    flat, out, off = packed.reshape(-1), [], 0
    for shp in shapes:
        out.append(flat[off:off + _size(shp)].reshape(shp))
        off += _size(shp)
    return out


def _join_shards(stacked, axis):
    moved = jnp.moveaxis(stacked, 0, axis)
    shp = list(moved.shape)
    shp[axis:axis + 2] = [shp[axis] * shp[axis + 1]]
    return moved.reshape(shp)


def _cut_shards(full, axis):
    shp = list(full.shape)
    shp[axis:axis + 1] = [N_DEV, shp[axis] // N_DEV]
    return jnp.moveaxis(full.reshape(shp), axis, 0)


def _regroup_w_in(w):
    pad = jnp.zeros((w.shape[0], N_FULL - N_MAIN - HEADS), w.dtype)
    return jnp.concatenate([w[:, 1544:2568], w[:, 2568:5640], w[:, 0:1536], w[:, 1536:1544], pad], axis=1)


def _ungroup_w_in(wp):
    return jnp.concatenate([wp[:, 4096:5632], wp[:, 5632:5640], wp[:, 0:1024], wp[:, 1024:4096]], axis=1)


def _interleave_ffn(w):
    d = w.shape[0]
    return jnp.stack([w[:, :D_FF].reshape(d, N_FF_BLKS, FF_BLK), w[:, D_FF:].reshape(d, N_FF_BLKS, FF_BLK)],
                     axis=2).reshape(d, 2 * D_FF)


def _deinterleave_ffn(wp):
    d = wp.shape[0]
    t = wp.reshape(d, N_FF_BLKS, 2, FF_BLK)
    return jnp.concatenate([t[:, :, 0].reshape(d, D_FF), t[:, :, 1].reshape(d, D_FF)], axis=1)


def _pool_block_diag(w):
    out = jnp.zeros((D_POOL, D_MODEL), w.dtype)
    for g in range(4):
        out = lax.dynamic_update_slice(out, w[g], (g * 64, g * 256))
    return out


def _pool_from_block_diag(wbd):
    return jnp.stack([wbd[g * 64:(g + 1) * 64, g * 256:(g + 1) * 256] for g in range(4)])


def _layer_weights(mats, vec, conv_w, l):
    w_in = _regroup_w_in(mats["w_in"])
    w_ffn_in = _interleave_ffn(mats["w_ffn_in"])
    wp = _pool_block_diag(mats["pool_w"])
    row = lambda v: v.reshape(1, -1)
    fb = jnp.zeros((1, 128), F32).at[0, :HEADS].set(vec["forget_b"][l])
    cw = jnp.zeros((8, D_CONV), F32).at[:3].set(conv_w[l])
    twice = lambda v: jnp.tile(v.reshape(1, -1), (1, 2))
    return dict(
        w_in=w_in, w_f=w_in[:, N_MAIN:], w_ffn_in=w_ffn_in, w_ffn_out=mats["w_ffn_out"],
        wa=mats["w_attn_out"], wc=mats["w_conv_out"], wp=wp, wo=mats["w_o"],
        g_mix=row(vec["norm_mix_g"][l]), g_ffn=row(vec["norm_ffn_g"][l]), gq2=twice(vec["q_norm_g"][l]),
        gk2=twice(vec["k_norm_g"][l]), scale=row(vec["pool_scale"][l]), fb=fb, cw=cw)


def _layer_fwd(x, w, l, ex):
    proj, h = norm_matmul(x, w["g_mix"], w["w_in"], N_MAIN, f"in_proj_{l}")
    z, c = forget_fwd(h, w["w_f"], w["fb"], f"forget_fwd_{l}")
    qa, ka, va = attn_prep(proj, c, w["gq2"], w["gk2"], f"attn_prep_{l}")
    (oa, lse), carried = attn_forward(qa, ka, va, f"attn_fwd_{l}", ex)
    x1 = mix_fwd(proj, oa, x, w["wa"], w["wc"], w["wp"], w["scale"], w["cw"], w["wo"], f"mix_fwd_{l}")
    gu, h2 = norm_matmul(x1, w["g_ffn"], w["w_ffn_in"], 2 * D_FF, f"ffn_in_{l}")
    x2 = swiglu_matmul(gu, w["w_ffn_out"], x1, f"ffn_out_{l}")
    saved = dict(x=x, proj=proj, h=h, z=z, qa=qa, ka=ka, va=va, oa=oa, lse=lse, x1=x1, gu=gu, h2=h2)
    return x2, saved, carried


def _layer_bwd(dx2, sv, w, l, ex):
    g = {}
    dgu, act = swiglu_bwd(dx2, sv["gu"], w["w_ffn_out"], f"ffn_out_bwd_{l}")
    g["w_ffn_out"] = tn_matmul(act, dx2, f"dw_ffn_out_{l}")
    g["w_ffn_in"] = _deinterleave_ffn(tn_matmul(sv["h2"], dgu, f"dw_ffn_in_{l}"))
    dx1, dg = matmul_normbwd(dgu, w["w_ffn_in"], sv["x1"], w["g_ffn"], dx2, f"ffn_in_bwd_{l}")
    g["norm_ffn_g"] = dg[0]

    (dproj, doa, a_tok, merged, dya, dyc, dyp, uc, dd, dscale, dcw) = mix_bwd(
        sv["proj"], sv["oa"], dx1, w["wa"], w["wc"], w["wp"], w["scale"], w["cw"], w["wo"], f"mix_bwd_{l}")
    g["w_o"] = tn_matmul(merged, dx1, f"dw_o_{l}")
    g["w_attn_out"] = tn_matmul(a_tok, dya, f"dw_attn_out_{l}")
    g["w_conv_out"] = tn_matmul(uc, dyc, f"dw_conv_out_{l}")
    g["pool_w"] = _pool_from_block_diag(tn_matmul(dd, dyp, f"dw_pool_{l}"))
    g["pool_scale"] = dscale[0]
    g["conv_w"] = dcw[:3]

    (dqa, dka, dva), carried = attn_backward(sv["qa"], sv["ka"], sv["va"], sv["oa"], doa, sv["lse"], f"attn_bwd_{l}", ex)
    dproj, dc, dgq, dgk = attn_post(dqa, dka, dva, sv["proj"], w["gq2"], w["gk2"], dproj, f"attn_post_{l}")
    g["q_norm_g"] = dgq[0, :HEAD_DIM] + dgq[0, HEAD_DIM:]
    g["k_norm_g"] = dgk[0, :HEAD_DIM] + dgk[0, HEAD_DIM:]
    dproj, db = forget_bwd(dc, sv["z"], dproj, f"forget_bwd_{l}")
    g["forget_b"] = db[0, :HEADS]

    g["w_in"] = _ungroup_w_in(tn_matmul(sv["h"], dproj, f"dw_in_{l}", n_cols=N_FULL))
    dx, dg = matmul_normbwd(dproj, w["w_in"], sv["x"], w["g_mix"], dx1, f"in_proj_bwd_{l}", k=N_FULL)
    g["norm_mix_g"] = dg[0]
    return dx, g, carried


def _local_step(x, tgt, hooks):
    ws, saved = [], []
    w = hooks.weights(0, None)
    for l in range(DEPTH):
        ws.append(w)
        ex = hooks.gather(l + 1) if l + 1 < DEPTH else None
        x, sv, got = _layer_fwd(x, w, l, ex)
        saved.append(sv)
        if l + 1 < DEPTH:
            w = hooks.weights(l + 1, got)
    sq, dx = loss_kernel(x, tgt, "loss")
    pending = None
    for l in reversed(range(DEPTH)):
        dx, g, got = _layer_bwd(dx, saved[l], ws[l], l, pending)
        if pending:
            hooks.scattered(got)
        pending = hooks.scatter(l, g)
    if pending:
        hooks.scattered(run_exchange(pending, "exchange_0"))
    return sq[0, 0], dx


def kernel(x, norm_mix_g, w_in, forget_b, q_norm_g, k_norm_g, w_attn_out, conv_w, w_conv_out, pool_w, pool_scale, w_o, norm_ffn_g, w_ffn_in, w_ffn_out, loss_target, m_norm_mix_g, m_w_in, m_forget_b, m_q_norm_g, m_k_norm_g, m_w_attn_out, m_conv_w, m_w_conv_out, m_pool_w, m_pool_scale, m_w_o, m_norm_ffn_g, m_w_ffn_in, m_w_ffn_out, v_norm_mix_g, v_w_in, v_forget_b, v_q_norm_g, v_k_norm_g, v_w_attn_out, v_conv_w, v_w_conv_out, v_pool_w, v_pool_scale, v_w_o, v_norm_ffn_g, v_w_ffn_in, v_w_ffn_out):
    w = dict(norm_mix_g=norm_mix_g, w_in=w_in, forget_b=forget_b, q_norm_g=q_norm_g, k_norm_g=k_norm_g,
             w_attn_out=w_attn_out, conv_w=conv_w, w_conv_out=w_conv_out, pool_w=pool_w, pool_scale=pool_scale,
             w_o=w_o, norm_ffn_g=norm_ffn_g, w_ffn_in=w_ffn_in, w_ffn_out=w_ffn_out)
    m = dict(norm_mix_g=m_norm_mix_g, w_in=m_w_in, forget_b=m_forget_b, q_norm_g=m_q_norm_g, k_norm_g=m_k_norm_g,
             w_attn_out=m_w_attn_out, conv_w=m_conv_w, w_conv_out=m_w_conv_out, pool_w=m_pool_w,
             pool_scale=m_pool_scale, w_o=m_w_o, norm_ffn_g=m_norm_ffn_g, w_ffn_in=m_w_ffn_in, w_ffn_out=m_w_ffn_out)
    v = dict(norm_mix_g=v_norm_mix_g, w_in=v_w_in, forget_b=v_forget_b, q_norm_g=v_q_norm_g, k_norm_g=v_k_norm_g,
             w_attn_out=v_w_attn_out, conv_w=v_conv_w, w_conv_out=v_w_conv_out, pool_w=v_pool_w,
             pool_scale=v_pool_scale, w_o=v_w_o, norm_ffn_g=v_norm_ffn_g, w_ffn_in=v_w_ffn_in, w_ffn_out=v_w_ffn_out)
    me = 4 * lax.axis_index("x") + 2 * lax.axis_index("y") + lax.axis_index("c")
    layer_shard = {n: SHARD_INFO[n][0][1:] for n in MATRICES}
    cut_axis = {n: SHARD_INFO[n][1] - 1 for n in MATRICES}

    conv_g = all_gather([_pack([conv_w], 8, 128)], "gather_conv_w")[0]
    conv_full = _join_shards(jnp.stack([_unpack(conv_g[i], [conv_w.shape])[0] for i in range(N_DEV)]), 2)
    vec = {n: w[n] for n in VECTORS}

    class Hooks:
        bufs = [lax.empty((DEPTH, N_DEV) + layer_shard[n], BF16) for n in MATRICES]
        small_g = [None] * DEPTH

        @staticmethod
        def shards(l):
            return [w[n][l].astype(BF16) for n in MATRICES]

        @staticmethod
        def gather(l):
            return gather_exchange(Hooks.shards(l))

        @staticmethod
        def weights(l, gathered):
            if l == 0:
                gathered = all_gather(Hooks.shards(0), "gather_0")
            mats = {n: _join_shards(t, cut_axis[n]) for n, t in zip(MATRICES, gathered)}
            return _layer_weights(mats, vec, conv_full, l)

        @staticmethod
        def scatter(l, g):
            Hooks.small_g[l] = g
            return scatter_exchange([_cut_shards(g[n], cut_axis[n]) for n in MATRICES], Hooks.bufs, l)

        @staticmethod
        def scattered(results):
            Hooks.bufs = list(results)

    small_g, received = Hooks.small_g, Hooks
    sq, dx = _local_step(x[0], loss_target[0], Hooks)
    loss = lax.psum(0.5 * sq / D_MODEL, ("x", "y", "c"))

    big = {}
    for n, parts in zip(MATRICES, received.bufs):
        rc = (_size(layer_shard[n][:-1]), layer_shard[n][-1])
        outs = adamw_sum(parts.reshape((DEPTH, N_DEV) + rc), *[d[n].reshape((DEPTH,) + rc) for d in (w, m, v)], f"adamw_{n}")
        big[n] = [t.reshape(w[n].shape) for t in outs]

    small_shapes = [VECTOR_SHAPES[n] for n in VECTORS] + [CONV_W_FULL]
    stacked = [jnp.stack([small_g[l][n] for l in range(DEPTH)]) for n in VECTORS + ("conv_w",)]
    sparts = all_gather([_pack(stacked, SMALL_ROWS, 128)], "gather_vector_grads")[0]
    col0 = me * (D_CONV // N_DEV)
    place = lambda t: lax.dynamic_update_slice(jnp.zeros(CONV_W_FULL, F32), t, (0, 0, col0))
    spacked = [_pack([d[n] for n in VECTORS] + [place(d["conv_w"])], SMALL_ROWS, 128)[None] for d in (w, m, v)]
    small = [_unpack(t[0], small_shapes) for t in adamw_sum(sparts[None], *spacked, "adamw_vectors")]

    def result(kind):
        out = {n: big[n][kind] for n in MATRICES}
        out.update({n: small[kind][j] for j, n in enumerate(VECTORS)})
        out["conv_w"] = lax.dynamic_slice(small[kind][len(VECTORS)], (0, 0, col0), conv_w.shape)
        return [out[n] for n in w]

    return (loss, dx[None], *result(0), *result(1), *result(2), *result(3))
```

```python
import functools

import jax
import jax.numpy as jnp
from jax import lax
from jax.experimental import pallas as pl
from jax.experimental.pallas import tpu as pltpu

F32 = jnp.float32
BF16 = jnp.bfloat16

N_DEV = 8
DEPTH = 4
D_MODEL = 1024
HEAD_DIM = 64
HEADS = 8
D_ATTN = 512
D_CONV = 256
D_POOL = 256
D_FF = 2816
D_IN = 5640
EPS = 1e-6
ATTN_SCALE = HEAD_DIM ** -0.5

N_REST = 4096
N_MAIN = 5632
N_FULL = 5760
DPROJ_TAIL = 2048
DPROJ_COLS = N_REST + DPROJ_TAIL
FF_BLK = 256
N_FF_BLKS = D_FF // FF_BLK
HALO = 16

ADAM_LR = 0.001
ADAM_B1 = 0.9
ADAM_B2 = 0.999
ADAM_EPS = 1e-08
ADAM_WD = 0.01
ADAM_STEP = 10

PACK_COLS = 1024
PACK_ROWS = 8192
SMALL_ROWS = 128

VMEM_LIMIT = 48 * 2 ** 20


def _cparams(sem, vmem=None):
    return pltpu.CompilerParams(dimension_semantics=sem, vmem_limit_bytes=vmem or VMEM_LIMIT)


def _pick(n, cands):
    for c in cands:
        if n % c == 0:
            return c
    raise ValueError(f"no tile for {n}")


def _sigmoid(v):
    return 1.0 / (1.0 + jnp.exp(-v))


def _rstd(v):
    return lax.rsqrt(jnp.mean(v * v, axis=-1, keepdims=True) + EPS)


def _dot(a, b):
    return jnp.dot(a, b, preferred_element_type=F32)


def _dot_tn(a, b):
    return lax.dot_general(a, b, (((0,), (0,)), ((), ())), preferred_element_type=F32)


def _dot_nt(a, b):
    return lax.dot_general(a, b, (((1,), (1,)), ((), ())), preferred_element_type=F32)


def norm_matmul(x, g, w, n_cols, name):
    s, d = x.shape
    tm, tn = min(512, s), _pick(n_cols, (1408, 512))

    def body(x_ref, g_ref, w_ref, o_ref, h_ref):
        @pl.when(pl.program_id(1) == 0)
        def _():
            xv = x_ref[...]
            h_ref[...] = (xv * _rstd(xv) * g_ref[...]).astype(BF16)

        o_ref[...] = _dot(h_ref[...], w_ref[...]).astype(BF16)

    return pl.pallas_call(
        body, grid=(s // tm, n_cols // tn),
        in_specs=[pl.BlockSpec((tm, d), lambda i, j: (i, 0)), pl.BlockSpec((1, d), lambda i, j: (0, 0)),
                  pl.BlockSpec((d, tn), lambda i, j: (0, j))],
        out_specs=[pl.BlockSpec((tm, tn), lambda i, j: (i, j)), pl.BlockSpec((tm, d), lambda i, j: (i, 0))],
        out_shape=[jax.ShapeDtypeStruct((s, n_cols), BF16), jax.ShapeDtypeStruct((s, d), BF16)],
        compiler_params=_cparams(("parallel", "arbitrary")), name=name)(x, g, w)


def tn_matmul(a, b, name, n_cols=None):
    t, m = a.shape
    n = n_cols or b.shape[1]
    tk = min(512, t)
    tmm = _pick(m, (1024, 1408, 512, 256))
    tn = _pick(n, (1408, 1152, 1024, 512, 128))
    nk = t // tk

    def body(a_ref, b_ref, o_ref, acc_ref):
        @pl.when(pl.program_id(2) == 0)
        def _():
            acc_ref[...] = jnp.zeros_like(acc_ref)

        acc_ref[...] += _dot_tn(a_ref[...].astype(BF16), b_ref[...].astype(BF16))

        @pl.when(pl.program_id(2) == nk - 1)
        def _():
            o_ref[...] = acc_ref[...].astype(BF16)

    return pl.pallas_call(
        body, grid=(m // tmm, n // tn, nk),
        in_specs=[pl.BlockSpec((tk, tmm), lambda i, j, k: (k, i)), pl.BlockSpec((tk, tn), lambda i, j, k: (k, j))],
        out_specs=pl.BlockSpec((tmm, tn), lambda i, j, k: (i, j)),
        out_shape=jax.ShapeDtypeStruct((m, n), BF16), scratch_shapes=[pltpu.VMEM((tmm, tn), F32)],
        compiler_params=_cparams(("parallel", "parallel", "arbitrary")), name=name)(a, b)


def matmul_normbwd(a, w, x, g, dres, name, k=None):
    s = a.shape[0]
    k = k or a.shape[1]
    d = w.shape[0]
    tm = min(512, s)
    tk = _pick(k, (1408, 1152, 512))
    nk = k // tk

    def body(a_ref, w_ref, x_ref, g_ref, r_ref, dx_ref, dg_ref, acc_ref):
        i, kk = pl.program_id(0), pl.program_id(1)

        @pl.when(kk == 0)
        def _():
            acc_ref[...] = jnp.zeros_like(acc_ref)

        @pl.when((i == 0) & (kk == 0))
        def _():
            dg_ref[...] = jnp.zeros_like(dg_ref)

        acc_ref[...] += _dot_nt(a_ref[...], w_ref[...])

        @pl.when(kk == nk - 1)
        def _():
            xv = x_ref[...]
            r = _rstd(xv)
            y = xv * r
            dh = acc_ref[...]
            dy = dh * g_ref[...]
            dx_ref[...] = r_ref[...] + r * (dy - y * jnp.mean(dy * y, axis=-1, keepdims=True))
            dg_ref[...] += jnp.sum(dh * y, axis=0, keepdims=True)

    return pl.pallas_call(
        body, grid=(s // tm, nk),
        in_specs=[pl.BlockSpec((tm, tk), lambda i, kk: (i, kk)), pl.BlockSpec((d, tk), lambda i, kk: (0, kk)),
                  pl.BlockSpec((tm, d), lambda i, kk: (i, 0)), pl.BlockSpec((1, d), lambda i, kk: (0, 0)),
                  pl.BlockSpec((tm, d), lambda i, kk: (i, 0))],
        out_specs=[pl.BlockSpec((tm, d), lambda i, kk: (i, 0)), pl.BlockSpec((1, d), lambda i, kk: (0, 0))],
        out_shape=[jax.ShapeDtypeStruct((s, d), F32), jax.ShapeDtypeStruct((1, d), F32)],
        scratch_shapes=[pltpu.VMEM((tm, d), F32)],
        compiler_params=_cparams(("arbitrary", "arbitrary")), name=name)(a, w, x, g, dres)


def swiglu_matmul(gu, w, x1, name):
    s = gu.shape[0]
    d = w.shape[1]
    tm = min(512, s)

    def body(gu_ref, w_ref, x_ref, o_ref):
        acc = x_ref[...]
        for j in range(N_FF_BLKS):
            gt = gu_ref[:, 2 * j * FF_BLK:(2 * j + 1) * FF_BLK].astype(F32)
            up = gu_ref[:, (2 * j + 1) * FF_BLK:(2 * j + 2) * FF_BLK].astype(F32)
            act = (gt * _sigmoid(gt) * up).astype(BF16)
            acc += _dot(act, w_ref[j * FF_BLK:(j + 1) * FF_BLK, :])
        o_ref[...] = acc

    return pl.pallas_call(
        body, grid=(s // tm,),
        in_specs=[pl.BlockSpec((tm, 2 * D_FF), lambda i: (i, 0)), pl.BlockSpec((D_FF, d), lambda i: (0, 0)),
                  pl.BlockSpec((tm, d), lambda i: (i, 0))],
        out_specs=pl.BlockSpec((tm, d), lambda i: (i, 0)),
        out_shape=jax.ShapeDtypeStruct((s, d), F32),
        compiler_params=_cparams(("parallel",)), name=name)(gu, w, x1)


def swiglu_bwd(dx2, gu, w, name):
    s, d = dx2.shape
    tm = min(256, s)

    def body(dx_ref, gu_ref, w_ref, dgu_ref, act_ref):
        dx = dx_ref[...].astype(BF16)
        for j in range(N_FF_BLKS):
            g_cols = slice(2 * j * FF_BLK, (2 * j + 1) * FF_BLK)
            u_cols = slice((2 * j + 1) * FF_BLK, (2 * j + 2) * FF_BLK)
            dact = _dot_nt(dx, w_ref[j * FF_BLK:(j + 1) * FF_BLK, :])
            gt = gu_ref[:, g_cols].astype(F32)
            up = gu_ref[:, u_cols].astype(F32)
            sg = _sigmoid(gt)
            act_ref[:, j * FF_BLK:(j + 1) * FF_BLK] = (gt * sg * up).astype(BF16)
            dgu_ref[:, g_cols] = (dact * up * (sg * (1.0 + gt * (1.0 - sg)))).astype(BF16)
            dgu_ref[:, u_cols] = (dact * gt * sg).astype(BF16)

    return pl.pallas_call(
        body, grid=(s // tm,),
        in_specs=[pl.BlockSpec((tm, d), lambda i: (i, 0)), pl.BlockSpec((tm, 2 * D_FF), lambda i: (i, 0)),
                  pl.BlockSpec((D_FF, d), lambda i: (0, 0))],
        out_specs=[pl.BlockSpec((tm, 2 * D_FF), lambda i: (i, 0)), pl.BlockSpec((tm, D_FF), lambda i: (i, 0))],
        out_shape=[jax.ShapeDtypeStruct((s, 2 * D_FF), BF16), jax.ShapeDtypeStruct((s, D_FF), BF16)],
        compiler_params=_cparams(("parallel",)), name=name)(dx2, gu, w)


def loss_kernel(y, tgt, name):
    s, d = y.shape
    tm = min(512, s)

    def body(y_ref, t_ref, l_ref, dy_ref):
        @pl.when(pl.program_id(0) == 0)
        def _():
            l_ref[...] = jnp.zeros_like(l_ref)

        err = y_ref[...] - t_ref[...]
        dy_ref[...] = err * (1.0 / d)
        l_ref[...] += jnp.sum(jnp.sum(err * err, axis=1, keepdims=True), axis=0, keepdims=True)

    return pl.pallas_call(
        body, grid=(s // tm,),
        in_specs=[pl.BlockSpec((tm, d), lambda i: (i, 0)), pl.BlockSpec((tm, d), lambda i: (i, 0))],
        out_specs=[pl.BlockSpec((8, 128), lambda i: (0, 0)), pl.BlockSpec((tm, d), lambda i: (i, 0))],
        out_shape=[jax.ShapeDtypeStruct((8, 128), F32), jax.ShapeDtypeStruct((s, d), F32)],
        compiler_params=_cparams(("arbitrary",)), name=name)(y, tgt)


def _split3(v):
    a1 = v.astype(BF16)
    r1 = v - a1.astype(F32)
    a2 = r1.astype(BF16)
    a3 = (r1 - a2.astype(F32)).astype(BF16)
    return a1, a2, a3


def forget_fwd(h, wf, b, name):
    s, d = h.shape
    tm = min(512, s)

    def body(h_ref, w_ref, b_ref, z_ref, c_ref, carry_ref):
        @pl.when(pl.program_id(0) == 0)
        def _():
            carry_ref[...] = jnp.zeros_like(carry_ref)

        z = _dot(h_ref[...], w_ref[...]) + b_ref[...]
        z_ref[...] = z
        logf = jnp.minimum(z, 0.0) - jnp.log(1.0 + jnp.exp(-jnp.abs(z)))
        row = lax.broadcasted_iota(jnp.int32, (tm, tm), 0)
        col = lax.broadcasted_iota(jnp.int32, (tm, tm), 1)
        tri = (row >= col).astype(BF16)
        a1, a2, a3 = _split3(logf)
        c = _dot(tri, a1) + _dot(tri, a2) + _dot(tri, a3) + carry_ref[...]
        c_ref[...] = c
        carry_ref[...] = c[tm - 1:tm, :]

    return pl.pallas_call(
        body, grid=(s // tm,),
        in_specs=[pl.BlockSpec((tm, d), lambda i: (i, 0)), pl.BlockSpec((d, 128), lambda i: (0, 0)),
                  pl.BlockSpec((1, 128), lambda i: (0, 0))],
        out_specs=[pl.BlockSpec((tm, 128), lambda i: (i, 0)), pl.BlockSpec((tm, 128), lambda i: (i, 0))],
        out_shape=[jax.ShapeDtypeStruct((s, 128), F32), jax.ShapeDtypeStruct((s, 128), F32)],
        scratch_shapes=[pltpu.VMEM((1, 128), F32)],
        compiler_params=_cparams(("arbitrary",)), name=name)(h, wf, b)


def forget_bwd(dc, z, dproj, name):
    s = dc.shape[0]
    tm = min(512, s)
    nt = s // tm

    def body(dc_ref, z_ref, dp_ref, dz_ref, db_ref, carry_ref):
        @pl.when(pl.program_id(0) == 0)
        def _():
            carry_ref[...] = jnp.zeros_like(carry_ref)
            db_ref[...] = jnp.zeros_like(db_ref)

        row = lax.broadcasted_iota(jnp.int32, (tm, tm), 0)
        col = lax.broadcasted_iota(jnp.int32, (tm, tm), 1)
        tri = (col >= row).astype(BF16)
        a1, a2, a3 = _split3(dc_ref[...])
        dlogf = _dot(tri, a1) + _dot(tri, a2) + _dot(tri, a3) + carry_ref[...]
        carry_ref[...] = dlogf[0:1, :]
        dz = dlogf * (1.0 - _sigmoid(z_ref[...]))
        dz_ref[...] = dz.astype(BF16)
        db_ref[...] += jnp.sum(dz, axis=0, keepdims=True)

    return pl.pallas_call(
        body, grid=(nt,),
        in_specs=[pl.BlockSpec((tm, 128), lambda i: (nt - 1 - i, 0)), pl.BlockSpec((tm, 128), lambda i: (nt - 1 - i, 0)),
                  pl.BlockSpec(memory_space=pl.ANY)],
        out_specs=[pl.BlockSpec((tm, 128), lambda i: (nt - 1 - i, N_MAIN // 128)), pl.BlockSpec((1, 128), lambda i: (0, 0))],
        out_shape=[jax.ShapeDtypeStruct(dproj.shape, BF16), jax.ShapeDtypeStruct((1, 128), F32)],
        scratch_shapes=[pltpu.VMEM((1, 128), F32)], input_output_aliases={2: 0},
        compiler_params=_cparams(("arbitrary",)), name=name)(dc, z, dproj)


HEAD_GROUP = 2
LANE_C = 64
LANE_ONE = 67


def _lanes():
    lane = lax.broadcasted_iota(jnp.int32, (1, 128), 1)
    return lane, lane < HEAD_DIM


def _half_mean(t, lo):
    s_lo = jnp.sum(jnp.where(lo, t, 0.0), axis=-1, keepdims=True)
    s_hi = jnp.sum(jnp.where(lo, 0.0, t), axis=-1, keepdims=True)
    return jnp.where(lo, s_lo, s_hi) * (1.0 / HEAD_DIM)


def _lane_col(t, lane, idx):
    return jnp.sum(jnp.where(lane == idx, t, 0.0), axis=-1, keepdims=True)


def _swap_halves(t):
    return pltpu.roll(t, HEAD_DIM, 1)


def _causal(s_blk, tq, tk):
    row = lax.broadcasted_iota(jnp.int32, (tq, tk), 0)
    col = lax.broadcasted_iota(jnp.int32, (tq, tk), 1)
    return jnp.where(row >= col, s_blk, -jnp.inf)


def attn_prep(proj, c, gq2, gk2, name):
    s = proj.shape[0]
    tm = min(512, s)
    first = N_REST // 128

    def body(q_ref, k_ref, v_ref, c_ref, gq_ref, gk_ref, qa_ref, ka_ref, va_ref, vt_ref):
        j = pl.program_id(1)
        lane, lo = _lanes()

        def normed(ref, g):
            t = ref[...].astype(F32)
            return t * lax.rsqrt(_half_mean(t * t, lo) + EPS) * g

        qn = normed(q_ref, gq_ref[...] * ATTN_SCALE)
        kn = normed(k_ref, gk_ref[...])
        vv = v_ref[...].astype(F32)
        cv = c_ref[...]
        one_q = jnp.where((lane >= LANE_ONE) & (lane < LANE_ONE + 3), 1.0, 0.0)
        one_k = jnp.where((lane >= LANE_C) & (lane < LANE_C + 3), 1.0, 0.0)
        one_v = jnp.where(lane == LANE_C, 1.0, 0.0)
        for e in range(2):
            pick = (lambda t: t) if e == 0 else _swap_halves
            pieces = [p.astype(F32) for p in _split3(_lane_col(cv, lane, 2 * j + e))]
            ext_q, ext_k = one_q, one_k
            for i, p in enumerate(pieces):
                ext_q = jnp.where(lane == LANE_C + i, p, ext_q)
                ext_k = jnp.where(lane == LANE_ONE + i, -p, ext_k)
            qa_ref[e] = jnp.where(lo, pick(qn), ext_q).astype(BF16)
            ka_ref[e] = jnp.where(lo, pick(kn), ext_k).astype(BF16)
            va = jnp.where(lo, pick(vv), one_v)
            va_ref[e] = va.astype(BF16)
            vt_ref[e] = va.T.astype(BF16)

    tile = lambda base: pl.BlockSpec((tm, 128), lambda i, j: (i, base + j))
    vec = pl.BlockSpec((1, 128), lambda i, j: (0, 0))
    out = pl.BlockSpec((2, tm, 128), lambda i, j: (j, i, 0))
    return pl.pallas_call(
        body, grid=(s // tm, HEADS // 2),
        in_specs=[tile(first), tile(first + 4), tile(first + 8), pl.BlockSpec((tm, 128), lambda i, j: (i, 0)), vec, vec],
        out_specs=[out, out, out, pl.BlockSpec((2, 128, tm), lambda i, j: (j, 0, i))],
        out_shape=[jax.ShapeDtypeStruct((HEADS, s, 128), BF16)] * 3 + [jax.ShapeDtypeStruct((HEADS, 128, s), BF16)],
        compiler_params=_cparams(("parallel", "arbitrary")), name=name)(proj, proj, proj, c, gq2, gk2)


def attn_fwd(q, k, v, ccol, crow, gq, gk, name):
    hh, s, hd = q.shape
    tq = tk = min(512, s)
    nq = s // tq

    def body(q_ref, k_ref, v_ref, cc_ref, cr_ref, gq_ref, gk_ref, o_ref, lse_ref, qn_ref, m_ref, l_ref, acc_ref):
        qi, ki = pl.program_id(1), pl.program_id(2)

        @pl.when(ki == 0)
        def _():
            qn_ref[...] = _qk_hat(q_ref, gq_ref, ATTN_SCALE)
            m_ref[...] = jnp.full_like(m_ref, -jnp.inf)
            l_ref[...] = jnp.zeros_like(l_ref)
            acc_ref[...] = jnp.zeros_like(acc_ref)

        @pl.when(ki <= qi)
        def _():
            kn = _qk_hat(k_ref, gk_ref, 1.0)
            sb = _dot_nt(qn_ref[...], kn) + (cc_ref[...] - cr_ref[...])
            sb = _causal(sb, qi, ki, tq, tk)
            m_new = jnp.maximum(m_ref[...], jnp.max(sb, axis=-1, keepdims=True))
            alpha = jnp.exp(m_ref[...] - m_new)
            p = jnp.exp(sb - m_new)
            l_ref[...] = alpha * l_ref[...] + jnp.sum(p, axis=-1, keepdims=True)
            acc_ref[...] = alpha * acc_ref[...] + _dot(p.astype(BF16), v_ref[...])
            m_ref[...] = m_new

        @pl.when(ki == qi)
        def _():
            o_ref[...] = (acc_ref[...] / l_ref[...]).astype(BF16)
            lse_ref[...] = m_ref[...] + jnp.log(l_ref[...])

    qspec = pl.BlockSpec((None, tq, hd), lambda h, i, j: (h, i, 0))
    kspec = pl.BlockSpec((None, tk, hd), lambda h, i, j: (h, jnp.minimum(i, j), 0))
    gspec = pl.BlockSpec((1, hd), lambda h, i, j: (0, 0))
    return pl.pallas_call(
        body, grid=(hh, nq, nq),
        in_specs=[qspec, kspec, kspec,
                  pl.BlockSpec((None, tq, 1), lambda h, i, j: (h, i, 0)),
                  pl.BlockSpec((None, 1, tk), lambda h, i, j: (h, 0, jnp.minimum(i, j))), gspec, gspec],
        out_specs=[qspec, pl.BlockSpec((None, tq, 1), lambda h, i, j: (h, i, 0))],
        out_shape=[jax.ShapeDtypeStruct((hh, s, hd), BF16), jax.ShapeDtypeStruct((hh, s, 1), F32)],
        scratch_shapes=[pltpu.VMEM((tq, hd), BF16), pltpu.VMEM((tq, 1), F32), pltpu.VMEM((tq, 1), F32),
                        pltpu.VMEM((tq, hd), F32)],
        compiler_params=_cparams(("parallel", "parallel", "arbitrary")), name=name)(q, k, v, ccol, crow, gq, gk)


def attn_bwd_dq(q, k, v, o, do, lse, ccol, crow, gq, gk, name):
    hh, s, hd = q.shape
    tq = tk = min(512, s)
    nq = s // tq

    def body(q_ref, k_ref, v_ref, o_ref, do_ref, lse_ref, cc_ref, cr_ref, gq_ref, gk_ref,
             dq_ref, dcc_ref, dg_ref, qn_ref, dl_ref, acc_ref, dca_ref):
        h, qi, ki = pl.program_id(0), pl.program_id(1), pl.program_id(2)

        @pl.when((h == 0) & (qi == 0) & (ki == 0))
        def _():
            dg_ref[...] = jnp.zeros_like(dg_ref)

        @pl.when(ki == 0)
        def _():
            qn_ref[...] = _qk_hat(q_ref, gq_ref, ATTN_SCALE)
            dl_ref[...] = jnp.sum(do_ref[...].astype(F32) * o_ref[...].astype(F32), axis=-1, keepdims=True)
            acc_ref[...] = jnp.zeros_like(acc_ref)
            dca_ref[...] = jnp.zeros_like(dca_ref)

        @pl.when(ki <= qi)
        def _():
            kn = _qk_hat(k_ref, gk_ref, 1.0)
            sb = _dot_nt(qn_ref[...], kn) + (cc_ref[...] - cr_ref[...])
            p = jnp.exp(_causal(sb, qi, ki, tq, tk) - lse_ref[...])
            dp = _dot_nt(do_ref[...], v_ref[...])
            ds = p * (dp - dl_ref[...])
            acc_ref[...] += _dot(ds.astype(BF16), kn)
            dca_ref[...] += jnp.sum(ds, axis=-1, keepdims=True)

        @pl.when(ki == qi)
        def _():
            dq, dg = _norm_bwd(q_ref[...].astype(F32), gq_ref[...], acc_ref[...], ATTN_SCALE)
            dq_ref[...] = dq.astype(BF16)
            dcc_ref[...] = dca_ref[...]
            dg_ref[...] += dg

    qspec = pl.BlockSpec((None, tq, hd), lambda h, i, j: (h, i, 0))
    kspec = pl.BlockSpec((None, tk, hd), lambda h, i, j: (h, jnp.minimum(i, j), 0))
    cspec = pl.BlockSpec((None, tq, 1), lambda h, i, j: (h, i, 0))
    gspec = pl.BlockSpec((1, hd), lambda h, i, j: (0, 0))
    return pl.pallas_call(
        body, grid=(hh, nq, nq),
        in_specs=[qspec, kspec, kspec, qspec, qspec, cspec, cspec,
                  pl.BlockSpec((None, 1, tk), lambda h, i, j: (h, 0, jnp.minimum(i, j))), gspec, gspec],
        out_specs=[qspec, cspec, gspec],
        out_shape=[jax.ShapeDtypeStruct((hh, s, hd), BF16), jax.ShapeDtypeStruct((hh, s, 1), F32),
                   jax.ShapeDtypeStruct((1, hd), F32)],
        scratch_shapes=[pltpu.VMEM((tq, hd), BF16), pltpu.VMEM((tq, 1), F32), pltpu.VMEM((tq, hd), F32),
                        pltpu.VMEM((tq, 1), F32)],
        compiler_params=_cparams(("arbitrary", "arbitrary", "arbitrary")), name=name)(
            q, k, v, o, do, lse, ccol, crow, gq, gk)


def attn_bwd_dkv(q, k, v, o, do, lse, ccol, crow, gq, gk, name):
    hh, s, hd = q.shape
    tq = tk = min(512, s)
    nq = s // tq

    def body(q_ref, k_ref, v_ref, o_ref, do_ref, lse_ref, cc_ref, cr_ref, gq_ref, gk_ref,
             dk_ref, dv_ref, dcr_ref, dg_ref, kn_ref, dka_ref, dva_ref, dca_ref):
        h, ki, qi = pl.program_id(0), pl.program_id(1), pl.program_id(2)

        @pl.when((h == 0) & (ki == 0) & (qi == 0))
        def _():
            dg_ref[...] = jnp.zeros_like(dg_ref)

        @pl.when(qi == 0)
        def _():
            kn_ref[...] = _qk_hat(k_ref, gk_ref, 1.0)
            dka_ref[...] = jnp.zeros_like(dka_ref)
            dva_ref[...] = jnp.zeros_like(dva_ref)
            dca_ref[...] = jnp.zeros_like(dca_ref)

        @pl.when(qi >= ki)
        def _():
            qn = _qk_hat(q_ref, gq_ref, ATTN_SCALE)
            do = do_ref[...]
            delta = jnp.sum(do.astype(F32) * o_ref[...].astype(F32), axis=-1, keepdims=True)
            sb = _dot_nt(qn, kn_ref[...]) + (cc_ref[...] - cr_ref[...])
            p = jnp.exp(_causal(sb, qi, ki, tq, tk) - lse_ref[...])
            dva_ref[...] += _dot_tn(p.astype(BF16), do)
            ds = p * (_dot_nt(do, v_ref[...]) - delta)
            dka_ref[...] += _dot_tn(ds.astype(BF16), qn)
            dca_ref[...] += jnp.sum(ds, axis=0, keepdims=True)

        @pl.when(qi == nq - 1)
        def _():
            dk, dg = _norm_bwd(k_ref[...].astype(F32), gk_ref[...], dka_ref[...], 1.0)
            dk_ref[...] = dk.astype(BF16)
            dv_ref[...] = dva_ref[...].astype(BF16)
            dcr_ref[...] = dca_ref[...]
            dg_ref[...] += dg

    kspec = pl.BlockSpec((None, tk, hd), lambda h, j, i: (h, j, 0))
    qspec = pl.BlockSpec((None, tq, hd), lambda h, j, i: (h, jnp.maximum(i, j), 0))
    cspec = pl.BlockSpec((None, tq, 1), lambda h, j, i: (h, jnp.maximum(i, j), 0))
    rspec = pl.BlockSpec((None, 1, tk), lambda h, j, i: (h, 0, j))
    gspec = pl.BlockSpec((1, hd), lambda h, j, i: (0, 0))
    return pl.pallas_call(
        body, grid=(hh, nq, nq),
        in_specs=[qspec, kspec, kspec, qspec, qspec, cspec, cspec, rspec, gspec, gspec],
        out_specs=[kspec, kspec, rspec, gspec],
        out_shape=[jax.ShapeDtypeStruct((hh, s, hd), BF16), jax.ShapeDtypeStruct((hh, s, hd), BF16),
                   jax.ShapeDtypeStruct((hh, 1, s), F32), jax.ShapeDtypeStruct((1, hd), F32)],
        scratch_shapes=[pltpu.VMEM((tk, hd), BF16), pltpu.VMEM((tk, hd), F32), pltpu.VMEM((tk, hd), F32),
                        pltpu.VMEM((1, tk), F32)],
        compiler_params=_cparams(("arbitrary", "arbitrary", "arbitrary")), name=name)(
            q, k, v, o, do, lse, ccol, crow, gq, gk)


def _carry(ex, n_in, n_out, n_scratch, grid):
    n_xin, n_xout = (len(ex.inputs), len(ex.out_shapes)) if ex else (0, 0)

    def split(refs):
        ins, xins = refs[:n_in], refs[n_in:n_in + n_xin]
        rest = refs[n_in + n_xin:]
        outs, xouts = rest[:n_out], rest[n_out:n_out + n_xout]
        rest = rest[n_out + n_xout:]
        return ins + outs + rest[:n_scratch], (xins, xouts, rest[n_scratch:])

    def first():
        return functools.reduce(lambda a, b: a & b, [pl.program_id(d) == 0 for d in range(len(grid))])

    def last():
        return functools.reduce(lambda a, b: a & b, [pl.program_id(d) == grid[d] - 1 for d in range(len(grid))])

    return split, first, last


def _carried_call(body, ex, grid, in_specs, out_specs, out_shape, scratch, sem, name, operands):
    any_spec = pl.BlockSpec(memory_space=pl.ANY)
    split, first, last = _carry(ex, len(in_specs), len(out_specs), len(scratch), grid)

    def carried(*refs):
        own, xrefs = split(refs)
        if ex:
            @pl.when(first())
            def _():
                ex.start(*xrefs)

        body(*own)
        if ex:
            @pl.when(last())
            def _():
                ex.drain(*xrefs)

    n_xin = len(ex.inputs) if ex else 0
    results = pl.pallas_call(
        carried, grid=grid, in_specs=list(in_specs) + [any_spec] * n_xin,
        out_specs=list(out_specs) + [any_spec] * (len(ex.out_shapes) if ex else 0),
        out_shape=list(out_shape) + (list(ex.out_shapes) if ex else []),
        input_output_aliases={len(in_specs) + i: len(out_specs) + o for i, o in ex.aliases.items()} if ex else {},
        scratch_shapes=list(scratch) + (ex.scratch if ex else []),
        compiler_params=_cparams(sem), name=name)(*operands, *(ex.inputs if ex else []))
    return results[:len(out_specs)], results[len(out_specs):]


def _tri_rows(t, n):
    qi = sum(jnp.where(t >= r * (r + 1) // 2, 1, 0) for r in range(1, n))
    return qi, t - qi * (qi + 1) // 2


def _tri_cols(t, n):
    ki = sum(jnp.where(t >= r * n - r * (r - 1) // 2, 1, 0) for r in range(1, n))
    return ki, ki + t - (ki * n - ki * (ki - 1) // 2)


def _causal_t(st_blk, tk, tq):
    key = lax.broadcasted_iota(jnp.int32, (tk, tq), 0)
    qry = lax.broadcasted_iota(jnp.int32, (tk, tq), 1)
    return jnp.where(qry >= key, st_blk, -jnp.inf)


def attn_forward(qa, ka, vt, name, ex=None):
    hh, s, _ = qa.shape
    tq = tk = min(512, s)
    nq = s // tq
    grp = HEAD_GROUP

    def body(q_ref, k_ref, vt_ref, o_ref, lse_ref, m_ref, acc_ref):
        qi, ki = _tri_rows(pl.program_id(1), nq)

        @pl.when(ki == 0)
        def _():
            m_ref[...] = jnp.full_like(m_ref, -jnp.inf)
            acc_ref[...] = jnp.zeros_like(acc_ref)

        def step(masked):
            nxt = _dot_nt(k_ref[0], q_ref[0])
            for g in range(grp):
                st = nxt
                if g + 1 < grp:
                    nxt = _dot_nt(k_ref[g + 1], q_ref[g + 1])
                if masked:
                    st = _causal_t(st, tk, tq)
                m_old = m_ref[g]
                m_new = jnp.maximum(m_old, jnp.max(st, axis=0, keepdims=True))
                pt = jnp.exp(st - m_new).astype(BF16)
                acc_ref[g] = jnp.exp(m_old - m_new) * acc_ref[g] + _dot(vt_ref[g], pt)
                m_ref[g] = m_new

        @pl.when(ki < qi)
        def _():
            step(False)

        @pl.when(ki == qi)
        def _():
            step(True)
            for g in range(grp):
                acc = acc_ref[g]
                denom = acc[LANE_C:LANE_C + 1, :]
                o_ref[g] = (acc / denom).T.astype(BF16)
                lse_ref[g] = m_ref[g] + jnp.log(denom)

    qspec = pl.BlockSpec((grp, tq, 128), lambda h, t: (h, _tri_rows(t, nq)[0], 0))
    kspec = pl.BlockSpec((grp, tk, 128), lambda h, t: (h, _tri_rows(t, nq)[1], 0))
    vspec = pl.BlockSpec((grp, 128, tk), lambda h, t: (h, 0, _tri_rows(t, nq)[1]))
    lspec = pl.BlockSpec((grp, 1, tq), lambda h, t: (h, 0, _tri_rows(t, nq)[0]))
    return _carried_call(
        body, ex, (hh // grp, nq * (nq + 1) // 2), [qspec, kspec, vspec], [qspec, lspec],
        [jax.ShapeDtypeStruct((hh, s, 128), BF16), jax.ShapeDtypeStruct((hh, 1, s), F32)],
        [pltpu.VMEM((grp, 1, tq), F32), pltpu.VMEM((grp, 128, tq), F32)],
        ("arbitrary", "arbitrary"), name, (qa, ka, vt))


def attn_backward(qa, ka, va, oa, doa, lse, name, ex=None):
    hh, s, _ = qa.shape
    tq = tk = min(512, s)
    nq = s // tq
    grp = HEAD_GROUP

    def body(q_ref, k_ref, v_ref, o_ref, do_ref, lse_ref, dq_ref, dk_ref, dv_ref, dka_ref, dva_ref):
        ki, qi = _tri_cols(pl.program_id(1), nq)

        @pl.when(pl.program_id(1) == 0)
        def _():
            dq_ref[...] = jnp.zeros_like(dq_ref)

        @pl.when(qi == ki)
        def _():
            dka_ref[...] = jnp.zeros_like(dka_ref)
            dva_ref[...] = jnp.zeros_like(dva_ref)

        def step(masked):
            rows = pl.ds(pl.multiple_of(qi * tq, tq), tq)
            products = lambda g: (_dot_nt(k_ref[g], q_ref[g]), _dot_nt(v_ref[g], do_ref[g]))
            nxt = products(0)
            for g in range(grp):
                st, dpt = nxt
                if g + 1 < grp:
                    nxt = products(g + 1)
                q, k, do = q_ref[g], k_ref[g], do_ref[g]
                if masked:
                    st = _causal_t(st, tk, tq)
                pt = jnp.exp(st - lse_ref[g])
                delta = jnp.sum((do.astype(F32) * o_ref[g].astype(F32)).T, axis=0, keepdims=True)
                dst = (pt * (dpt - delta)).astype(BF16)
                dva_ref[g] += _dot(pt.astype(BF16), do)
                dka_ref[g] += _dot(dst, q)
                dq_ref[g, rows, :] += _dot_tn(dst, k)

        @pl.when(qi > ki)
        def _():
            step(False)

        @pl.when(qi == ki)
        def _():
            step(True)

        @pl.when(qi == nq - 1)
        def _():
            dk_ref[...] = dka_ref[...]
            dv_ref[...] = dva_ref[...].astype(BF16)

    qspec = pl.BlockSpec((grp, tq, 128), lambda h, t: (h, _tri_cols(t, nq)[1], 0))
    lspec = pl.BlockSpec((grp, 1, tq), lambda h, t: (h, 0, _tri_cols(t, nq)[1]))
    kspec = pl.BlockSpec((grp, tk, 128), lambda h, t: (h, _tri_cols(t, nq)[0], 0))
    return _carried_call(
        body, ex, (hh // grp, nq * (nq + 1) // 2), [qspec, kspec, kspec, qspec, qspec, lspec],
        [pl.BlockSpec((grp, s, 128), lambda h, t: (h, 0, 0)), kspec, kspec],
        [jax.ShapeDtypeStruct((hh, s, 128), F32), jax.ShapeDtypeStruct((hh, s, 128), F32),
         jax.ShapeDtypeStruct((hh, s, 128), BF16)],
        [pltpu.VMEM((grp, tk, 128), F32), pltpu.VMEM((grp, tk, 128), F32)],
        ("arbitrary", "arbitrary"), name, (qa, ka, va, oa, doa, lse))


def attn_post(dqa, dka, dva, proj, gq2, gk2, dproj, name):
    s = proj.shape[0]
    tm = min(256, s)

    def body(dq_ref, dk_ref, dv_ref, q_ref, k_ref, gq_ref, gk_ref, dp_any, dp_ref, dc_ref, dgq_ref, dgk_ref):
        lane, lo = _lanes()

        @pl.when(pl.program_id(0) == 0)
        def _():
            dgq_ref[...] = jnp.zeros_like(dgq_ref)
            dgk_ref[...] = jnp.zeros_like(dgk_ref)

        def pair(ref, j):
            return jnp.where(lo, ref[2 * j].astype(F32), _swap_halves(ref[2 * j + 1].astype(F32)))

        def norm_bwd(raw, g, dhat, scale):
            r = lax.rsqrt(_half_mean(raw * raw, lo) + EPS)
            y = raw * r
            dy = dhat * (g * scale)
            return r * (dy - y * _half_mean(dy * y, lo)), jnp.sum(dhat * y, axis=0, keepdims=True) * scale

        dc = jnp.zeros((tm, 128), F32)
        for j in range(HEADS // 2):
            cols = slice(128 * j, 128 * (j + 1))
            dq, dgq = norm_bwd(q_ref[:, cols].astype(F32), gq_ref[...], pair(dq_ref, j), ATTN_SCALE)
            dk, dgk = norm_bwd(k_ref[:, cols].astype(F32), gk_ref[...], pair(dk_ref, j), 1.0)
            dgq_ref[...] += dgq
            dgk_ref[...] += dgk
            dp_ref[:, cols] = dq.astype(BF16)
            dp_ref[:, D_ATTN + 128 * j:D_ATTN + 128 * (j + 1)] = dk.astype(BF16)
            dp_ref[:, 2 * D_ATTN + 128 * j:2 * D_ATTN + 128 * (j + 1)] = pair(dv_ref, j).astype(BF16)
            for e in range(2):
                h = 2 * j + e
                col = _lane_col(dq_ref[h], lane, LANE_C) - _lane_col(dk_ref[h], lane, LANE_ONE)
                dc = jnp.where(lane == h, col, dc)
        dp_ref[:, 3 * D_ATTN:] = jnp.zeros((tm, DPROJ_TAIL - 3 * D_ATTN), BF16)
        dc_ref[...] = dc

    heads = lambda: pl.BlockSpec((HEADS, tm, 128), lambda i: (0, i, 0))
    vec = pl.BlockSpec((1, 128), lambda i: (0, 0))
    first = N_REST // D_ATTN
    return pl.pallas_call(
        body, grid=(s // tm,),
        in_specs=[heads(), heads(), heads(), pl.BlockSpec((tm, D_ATTN), lambda i: (i, first)),
                  pl.BlockSpec((tm, D_ATTN), lambda i: (i, first + 1)), vec, vec, pl.BlockSpec(memory_space=pl.ANY)],
        out_specs=[pl.BlockSpec((tm, DPROJ_TAIL), lambda i: (i, N_REST // DPROJ_TAIL)),
                   pl.BlockSpec((tm, 128), lambda i: (i, 0)), vec, vec],
        out_shape=[jax.ShapeDtypeStruct(dproj.shape, BF16), jax.ShapeDtypeStruct((s, 128), F32),
                   jax.ShapeDtypeStruct((1, 128), F32), jax.ShapeDtypeStruct((1, 128), F32)],
        input_output_aliases={7: 0},
        compiler_params=_cparams(("arbitrary",)), name=name)(dqa, dka, dva, proj, proj, gq2, gk2, dproj)


def _pool_groups(tm):
    gid = lax.broadcasted_iota(jnp.int32, (1, D_POOL), 1) // (D_POOL // 4)
    win = jnp.where(gid == 0, 2.0, jnp.where(gid == 1, 4.0, jnp.where(gid == 2, 8.0, 16.0)))
    return gid, win


def _by_group(gid, v2, v4, v8, v16):
    return jnp.where(gid == 0, v2, jnp.where(gid == 1, v4, jnp.where(gid == 2, v8, v16)))


def _branches(rest_ref, halo_ref, a_ref, wa_ref, wc_ref, wp_ref, sc_ref, cw_ref, ti, tm):
    f = lambda v: v.astype(F32)
    cx, cb, cc, px = f(rest_ref[:, 0:256]), f(rest_ref[:, 256:512]), f(rest_ref[:, 512:768]), f(rest_ref[:, 768:1024])
    live = jnp.where(ti > 0, 1.0, 0.0)
    hz = f(halo_ref[:, 0:256]) * f(halo_ref[:, 512:768]) * live
    hp = f(halo_ref[:, 768:1024]) * live
    z = cc * cx
    zf = jnp.concatenate([hz, z], axis=0)
    z1 = pltpu.roll(zf, 1, 0)[HALO:]
    z2 = pltpu.roll(zf, 2, 0)[HALO:]
    cw = cw_ref[...]
    conv = cw[2:3] * z + cw[1:2] * z1 + cw[0:1] * z2
    uc = cb * conv
    pf = jnp.concatenate([hp, px], axis=0)
    s2 = pf + pltpu.roll(pf, 1, 0)
    s4 = s2 + pltpu.roll(s2, 2, 0)
    s8 = s4 + pltpu.roll(s4, 4, 0)
    s16 = s8 + pltpu.roll(s8, 8, 0)
    gid, win = _pool_groups(tm)
    t = (ti * tm + lax.broadcasted_iota(jnp.int32, (tm, 1), 0)).astype(F32)
    inv = 1.0 / jnp.minimum(t + 1.0, win)
    dpool = _by_group(gid, s2[HALO:], s4[HALO:], s8[HALO:], s16[HALO:]) * inv - px
    _, lo = _lanes()
    a_tok = [jnp.where(lo, f(a_ref[2 * j]), _swap_halves(f(a_ref[2 * j + 1]))).astype(BF16) for j in range(HEADS // 2)]
    y_attn = _dot(a_tok[0], wa_ref[0:128, :])
    for j in range(1, HEADS // 2):
        y_attn += _dot(a_tok[j], wa_ref[128 * j:128 * (j + 1), :])
    y_conv = _dot(uc.astype(BF16), wc_ref[...])
    y_pool_raw = _dot(dpool.astype(BF16), wp_ref[...])
    sg = [_sigmoid(f(rest_ref[:, 1024 + i * D_MODEL:1024 + (i + 1) * D_MODEL])) for i in range(3)]
    return dict(cx=cx, cb=cb, cc=cc, z=z, z1=z1, z2=z2, conv=conv, uc=uc, dpool=dpool, inv=inv, gid=gid, a_tok=a_tok,
                y_attn=y_attn, y_conv=y_conv, y_pool_raw=y_pool_raw, sg=sg, cw=cw)


def _mix_specs(tm, ti_of):
    blocks_per_tile = tm // HALO
    return [
        pl.BlockSpec((tm, N_REST), lambda i: (ti_of(i), 0)),
        pl.BlockSpec((HALO, 1024), lambda i: (jnp.maximum(ti_of(i) * blocks_per_tile - 1, 0), 0)),
        pl.BlockSpec((HEADS, tm, 128), lambda i: (0, ti_of(i), 0)),
        pl.BlockSpec((D_ATTN, D_MODEL), lambda i: (0, 0)),
        pl.BlockSpec((D_CONV, D_MODEL), lambda i: (0, 0)),
        pl.BlockSpec((D_POOL, D_MODEL), lambda i: (0, 0)),
        pl.BlockSpec((1, D_MODEL), lambda i: (0, 0)),
        pl.BlockSpec((8, D_CONV), lambda i: (0, 0)),
    ]


def mix_fwd(proj, a, x, wa, wc, wp, scale, cw, wo, name):
    s = x.shape[0]
    tm = min(256, s)

    def body(rest_ref, halo_ref, a_ref, wa_ref, wc_ref, wp_ref, sc_ref, cw_ref, wo_ref, x_ref, o_ref):
        b = _branches(rest_ref, halo_ref, a_ref, wa_ref, wc_ref, wp_ref, sc_ref, cw_ref, pl.program_id(0), tm)
        merged = b["sg"][0] * b["y_attn"] + b["sg"][1] * b["y_conv"] + b["sg"][2] * (b["y_pool_raw"] * sc_ref[...])
        o_ref[...] = x_ref[...] + _dot(merged.astype(BF16), wo_ref[...])

    return pl.pallas_call(
        body, grid=(s // tm,),
        in_specs=_mix_specs(tm, lambda i: i) + [pl.BlockSpec((D_MODEL, D_MODEL), lambda i: (0, 0)),
                                                 pl.BlockSpec((tm, D_MODEL), lambda i: (i, 0))],
        out_specs=pl.BlockSpec((tm, D_MODEL), lambda i: (i, 0)),
        out_shape=jax.ShapeDtypeStruct((s, D_MODEL), F32),
        compiler_params=_cparams(("parallel",)), name=name)(proj, proj, a, wa, wc, wp, scale, cw, wo, x)


def mix_bwd(proj, a, dx1, wa, wc, wp, scale, cw, wo, name):
    s = dx1.shape[0]
    tm = min(256, s)
    nt = s // tm
    ti_of = lambda i: nt - 1 - i
    n = tm + HALO

    def body(rest_ref, halo_ref, a_ref, wa_ref, wc_ref, wp_ref, sc_ref, cw_ref, wo_ref,
             dx_ref, dp_ref, da_ref, at_ref, mg_ref, dya_ref, dyc_ref, dyp_ref, uc_ref, dd_ref, dsc_ref, dcw_ref,
             cdc_ref, cde_ref):
        i = pl.program_id(0)
        ti = ti_of(i)

        @pl.when(i == 0)
        def _():
            cdc_ref[...] = jnp.zeros_like(cdc_ref)
            cde_ref[...] = jnp.zeros_like(cde_ref)
            dsc_ref[...] = jnp.zeros_like(dsc_ref)
            dcw_ref[...] = jnp.zeros_like(dcw_ref)

        b = _branches(rest_ref, halo_ref, a_ref, wa_ref, wc_ref, wp_ref, sc_ref, cw_ref, ti, tm)
        sg, sc = b["sg"], sc_ref[...]
        y_pool = b["y_pool_raw"] * sc
        merged = sg[0] * b["y_attn"] + sg[1] * b["y_conv"] + sg[2] * y_pool
        mg_ref[...] = merged.astype(BF16)
        dm = _dot_nt(dx_ref[...].astype(BF16), wo_ref[...])
        for j, y in enumerate((b["y_attn"], b["y_conv"], y_pool)):
            dp_ref[:, 1024 + j * D_MODEL:1024 + (j + 1) * D_MODEL] = (dm * y * sg[j] * (1.0 - sg[j])).astype(BF16)
        dya = (dm * sg[0]).astype(BF16)
        dya_ref[...] = dya
        _, lo = _lanes()
        for j in range(HEADS // 2):
            at_ref[:, 128 * j:128 * (j + 1)] = b["a_tok"][j]
            da = _dot_nt(dya, wa_ref[128 * j:128 * (j + 1), :])
            da_ref[2 * j] = jnp.where(lo, da, 0.0).astype(BF16)
            da_ref[2 * j + 1] = jnp.where(lo, _swap_halves(da), 0.0).astype(BF16)
        dyc = (dm * sg[1]).astype(BF16)
        dyc_ref[...] = dyc
        duc = _dot_nt(dyc, wc_ref[...])
        dyp = dm * sg[2]
        dsc_ref[...] += jnp.sum(dyp * b["y_pool_raw"], axis=0, keepdims=True)
        dypr = (dyp * sc).astype(BF16)
        dyp_ref[...] = dypr
        ddp = _dot_nt(dypr, wp_ref[...])
        uc_ref[...] = b["uc"].astype(BF16)
        dd_ref[...] = b["dpool"].astype(BF16)

        dconv = duc * b["cb"]
        dp_ref[:, 256:512] = (duc * b["conv"]).astype(BF16)
        dcf = jnp.concatenate([dconv, cdc_ref[...]], axis=0)
        cw = b["cw"]
        dz = cw[2:3] * dconv + cw[1:2] * pltpu.roll(dcf, n - 1, 0)[:tm] + cw[0:1] * pltpu.roll(dcf, n - 2, 0)[:tm]
        dp_ref[:, 0:256] = (dz * b["cc"]).astype(BF16)
        dp_ref[:, 512:768] = (dz * b["cx"]).astype(BF16)
        dcw_ref[0:1, :] += jnp.sum(dconv * b["z2"], axis=0, keepdims=True)
        dcw_ref[1:2, :] += jnp.sum(dconv * b["z1"], axis=0, keepdims=True)
        dcw_ref[2:3, :] += jnp.sum(dconv * b["z"], axis=0, keepdims=True)
        cdc_ref[...] = dconv[:HALO]

        e = ddp * b["inv"]
        ef = jnp.concatenate([e, cde_ref[...]], axis=0)
        r2 = ef + pltpu.roll(ef, n - 1, 0)
        r4 = r2 + pltpu.roll(r2, n - 2, 0)
        r8 = r4 + pltpu.roll(r4, n - 4, 0)
        r16 = r8 + pltpu.roll(r8, n - 8, 0)
        dp_ref[:, 768:1024] = (_by_group(b["gid"], r2[:tm], r4[:tm], r8[:tm], r16[:tm]) - ddp).astype(BF16)
        cde_ref[...] = e[:HALO]

    tile = lambda w: pl.BlockSpec((tm, w), lambda i: (ti_of(i), 0))
    whole = lambda r, c: pl.BlockSpec((r, c), lambda i: (0, 0))
    bf = lambda w: jax.ShapeDtypeStruct((s, w), BF16)
    return pl.pallas_call(
        body, grid=(nt,),
        in_specs=_mix_specs(tm, ti_of) + [whole(D_MODEL, D_MODEL), tile(D_MODEL)],
        out_specs=[tile(N_REST), pl.BlockSpec((HEADS, tm, 128), lambda i: (0, ti_of(i), 0)), tile(D_ATTN),
                   tile(D_MODEL), tile(D_MODEL), tile(D_MODEL), tile(D_MODEL),
                   tile(D_CONV), tile(D_POOL), whole(1, D_MODEL), whole(8, D_CONV)],
        out_shape=[bf(DPROJ_COLS), jax.ShapeDtypeStruct((HEADS, s, 128), BF16), bf(D_ATTN),
                   bf(D_MODEL), bf(D_MODEL), bf(D_MODEL), bf(D_MODEL), bf(D_CONV), bf(D_POOL),
                   jax.ShapeDtypeStruct((1, D_MODEL), F32), jax.ShapeDtypeStruct((8, D_CONV), F32)],
        scratch_shapes=[pltpu.VMEM((HALO, D_CONV), F32), pltpu.VMEM((HALO, D_POOL), F32)],
        compiler_params=_cparams(("arbitrary",)), name=name)(proj, proj, a, wa, wc, wp, scale, cw, wo, dx1)


def _adamw_math(w, g, m, v):
    m = ADAM_B1 * m + (1.0 - ADAM_B1) * g
    v = ADAM_B2 * v + (1.0 - ADAM_B2) * (g * g)
    m_hat = m / (1.0 - ADAM_B1 ** ADAM_STEP)
    v_hat = v / (1.0 - ADAM_B2 ** ADAM_STEP)
    delta = -ADAM_LR * (m_hat / (jnp.sqrt(v_hat) + ADAM_EPS) + ADAM_WD * w)
    return delta, m, v


ADAMW_PARTS_BLOCK_BYTES = 4 * 2 ** 20


def adamw_sum(parts, w, m, v, name):
    layers, rows, cols = w.shape
    row_bytes = N_DEV * (-(-cols // 128) * 128) * parts.dtype.itemsize
    fits = [t for t in range(16, rows + 1, 16) if rows % t == 0 and t * row_bytes <= ADAMW_PARTS_BLOCK_BYTES]
    tr = max(fits) if fits else rows

    def body(p_ref, w_ref, m_ref, v_ref, g_ref, d_ref, nm_ref, nv_ref):
        g = p_ref[0].astype(F32)
        for i in range(1, N_DEV):
            g = g + p_ref[i].astype(F32)
        g_ref[...] = g
        d_ref[...], nm_ref[...], nv_ref[...] = _adamw_math(w_ref[...], g, m_ref[...], v_ref[...])

    spec = pl.BlockSpec((None, tr, cols), lambda l, i: (l, i, 0))
    return pl.pallas_call(
        body, grid=(layers, rows // tr),
        in_specs=[pl.BlockSpec((None, N_DEV, tr, cols), lambda l, i: (l, 0, i, 0)), spec, spec, spec],
        out_specs=[spec] * 4, out_shape=[jax.ShapeDtypeStruct((layers, rows, cols), F32)] * 4,
        compiler_params=_cparams(("parallel", "parallel")), name=name)(parts, w, m, v)


def _me():
    return lax.axis_index("x"), lax.axis_index("y"), lax.axis_index("c")


N_PEERS = N_DEV - 1


def all_gather(shards, name):
    n = len(shards)
    any_spec = pl.BlockSpec(memory_space=pl.ANY)

    def body(*refs):
        x_refs, out_refs = refs[:n], refs[n:2 * n]
        send_sems, recv_sems, local_sems = refs[2 * n:]
        x, y, c = _me()
        me, sibling = (x, y, c), (x, y, 1 - c)
        chips = [(1 - x, y), (x, 1 - y), (1 - x, 1 - y)]

        def copy(t, k, block, to, from_input=False):
            slot = out_refs[t].at[4 * block[0] + 2 * block[1] + block[2]]
            return pltpu.make_async_remote_copy(
                src_ref=x_refs[t] if from_input else slot, dst_ref=slot, send_sem=send_sems.at[N_PEERS * t + k],
                recv_sem=recv_sems.at[N_PEERS * t + k], device_id=to, device_id_type=pl.DeviceIdType.MESH)

        mine = [pltpu.make_async_copy(x_refs[t], out_refs[t].at[4 * x + 2 * y + c], local_sems.at[t]) for t in range(n)]
        started = []
        for t in range(n):
            mine[t].start()
            started.append(copy(t, 0, me, sibling, from_input=True))
            started += [copy(t, 1 + j, me, (*chip, c), from_input=True) for j, chip in enumerate(chips)]
        for cp in started:
            cp.start()
        for j, chip in enumerate(chips):
            for t in range(n):
                copy(t, 1 + j, (*chip, c), me).wait_recv()
                fwd = copy(t, 4 + j, (*chip, c), sibling)
                fwd.start()
                started.append(fwd)
        for t in range(n):
            copy(t, 0, sibling, me).wait_recv()
            for j, chip in enumerate(chips):
                copy(t, 4 + j, (*chip, 1 - c), me).wait_recv()
        for cp in started:
            cp.wait_send()
        for cp in mine:
            cp.wait()

    return pl.pallas_call(
        body, out_shape=[jax.ShapeDtypeStruct((N_DEV,) + s.shape, s.dtype) for s in shards],
        in_specs=[any_spec] * n, out_specs=[any_spec] * n,
        scratch_shapes=[pltpu.SemaphoreType.DMA((N_PEERS * n,)), pltpu.SemaphoreType.DMA((N_PEERS * n,)),
                        pltpu.SemaphoreType.DMA((n,))],
        name=name)(*shards)


class Exchange:
    def __init__(self, inputs, out_shapes, aliases, n, src_of, dst_of):
        self.inputs, self.out_shapes, self.aliases, self.n = inputs, out_shapes, aliases, n
        self._src_of, self._dst_of = src_of, dst_of
        self.scratch = [pltpu.SemaphoreType.DMA((N_PEERS * n,)), pltpu.SemaphoreType.DMA((N_PEERS * n,)),
                        pltpu.SemaphoreType.DMA((n,))]

    def _copies(self, ins, outs, sems):
        send_sems, recv_sems, local_sems = sems
        x, y, c = _me()
        me = 4 * x + 2 * y + c
        local, sends, recvs = [], [], []
        for t in range(self.n):
            local.append(functools.partial(
                pltpu.make_async_copy, self._src_of(ins, t, me), self._dst_of(outs, t, me), local_sems.at[t]))
            for k in range(1, N_DEV):
                px, py, pc = x ^ ((k >> 2) & 1), y ^ ((k >> 1) & 1), c ^ (k & 1)
                peer = 4 * px + 2 * py + pc
                pair = dict(send_sem=send_sems.at[N_PEERS * t + k - 1], recv_sem=recv_sems.at[N_PEERS * t + k - 1],
                            device_id_type=pl.DeviceIdType.MESH)
                sends.append(functools.partial(
                    pltpu.make_async_remote_copy, src_ref=self._src_of(ins, t, peer), dst_ref=self._dst_of(outs, t, me),
                    device_id=(px, py, pc), **pair))
                recvs.append(functools.partial(
                    pltpu.make_async_remote_copy, src_ref=self._src_of(ins, t, peer), dst_ref=self._dst_of(outs, t, peer),
                    device_id=(x, y, c), **pair))
        return local, sends, recvs

    def start(self, ins, outs, sems):
        local, sends, _ = self._copies(ins, outs, sems)
        for make in local + sends:
            make().start()

    def drain(self, ins, outs, sems):
        local, sends, recvs = self._copies(ins, outs, sems)
        for make in recvs:
            make().wait_recv()
        for make in sends:
            make().wait_send()
        for make in local:
            make().wait()


def gather_exchange(shards):
    n = len(shards)
    return Exchange(list(shards), [jax.ShapeDtypeStruct((N_DEV,) + s.shape, s.dtype) for s in shards], {}, n,
                    src_of=lambda ins, t, block: ins[t], dst_of=lambda outs, t, block: outs[t].at[block])


def scatter_exchange(blocks, bufs, layer):
    n = len(blocks)
    return Exchange(list(blocks) + list(bufs), [jax.ShapeDtypeStruct(b.shape, b.dtype) for b in bufs],
                    {n + t: t for t in range(n)}, n,
                    src_of=lambda ins, t, block: ins[t].at[block], dst_of=lambda outs, t, block: outs[t].at[layer, block])


def run_exchange(ex, name):
    any_spec = pl.BlockSpec(memory_space=pl.ANY)
    n_in, n_out = len(ex.inputs), len(ex.out_shapes)

    def body(*refs):
        ins, outs, sems = refs[:n_in], refs[n_in:n_in + n_out], refs[n_in + n_out:]
        ex.start(ins, outs, sems)
        ex.drain(ins, outs, sems)

    return pl.pallas_call(
        body, out_shape=ex.out_shapes, in_specs=[any_spec] * n_in, out_specs=[any_spec] * n_out,
        input_output_aliases=ex.aliases, scratch_shapes=ex.scratch, name=name)(*ex.inputs)


MATRICES = ("w_in", "w_attn_out", "w_conv_out", "pool_w", "w_o", "w_ffn_in", "w_ffn_out")
SHARD_INFO = {
    "w_in": ((DEPTH, D_MODEL, D_IN // N_DEV), 2),
    "w_attn_out": ((DEPTH, D_ATTN, D_MODEL // N_DEV), 2),
    "w_conv_out": ((DEPTH, D_CONV, D_MODEL // N_DEV), 2),
    "pool_w": ((DEPTH, 4, 64, 256 // N_DEV), 3),
    "w_o": ((DEPTH, D_MODEL // N_DEV, D_MODEL), 1),
    "w_ffn_in": ((DEPTH, D_MODEL, 2 * D_FF // N_DEV), 2),
    "w_ffn_out": ((DEPTH, D_FF // N_DEV, D_MODEL), 1),
}
VECTORS = ("norm_mix_g", "forget_b", "q_norm_g", "k_norm_g", "pool_scale", "norm_ffn_g")
VECTOR_SHAPES = {"norm_mix_g": (DEPTH, D_MODEL), "forget_b": (DEPTH, HEADS), "q_norm_g": (DEPTH, HEAD_DIM),
                 "k_norm_g": (DEPTH, HEAD_DIM), "pool_scale": (DEPTH, D_MODEL), "norm_ffn_g": (DEPTH, D_MODEL)}
CONV_W_FULL = (DEPTH, 3, D_CONV)


def _size(shape):
    n = 1
    for v in shape:
        n *= v
    return n


def _pack(arrays, rows, cols):
    flat = jnp.concatenate([a.reshape(-1) for a in arrays])
    return jnp.pad(flat, (0, rows * cols - flat.shape[0])).reshape(rows, cols)


def _unpack(packed, shapes):
    flat, out, off = packed.reshape(-1), [], 0
    for shp in shapes:
        out.append(flat[off:off + _size(shp)].reshape(shp))
        off += _size(shp)
    return out


def _join_shards(stacked, axis):
    moved = jnp.moveaxis(stacked, 0, axis)
    shp = list(moved.shape)
    shp[axis:axis + 2] = [shp[axis] * shp[axis + 1]]
    return moved.reshape(shp)


def _cut_shards(full, axis):
    shp = list(full.shape)
    shp[axis:axis + 1] = [N_DEV, shp[axis] // N_DEV]
    return jnp.moveaxis(full.reshape(shp), axis, 0)


def _regroup_w_in(w):
    pad = jnp.zeros((w.shape[0], N_FULL - N_MAIN - HEADS), w.dtype)
    return jnp.concatenate([w[:, 1544:2568], w[:, 2568:5640], w[:, 0:1536], w[:, 1536:1544], pad], axis=1)


def _ungroup_w_in(wp):
    return jnp.concatenate([wp[:, 4096:5632], wp[:, 5632:5640], wp[:, 0:1024], wp[:, 1024:4096]], axis=1)


def _interleave_ffn(w):
    d = w.shape[0]
    return jnp.stack([w[:, :D_FF].reshape(d, N_FF_BLKS, FF_BLK), w[:, D_FF:].reshape(d, N_FF_BLKS, FF_BLK)],
                     axis=2).reshape(d, 2 * D_FF)


def _deinterleave_ffn(wp):
    d = wp.shape[0]
    t = wp.reshape(d, N_FF_BLKS, 2, FF_BLK)
    return jnp.concatenate([t[:, :, 0].reshape(d, D_FF), t[:, :, 1].reshape(d, D_FF)], axis=1)


def _pool_block_diag(w):
    out = jnp.zeros((D_POOL, D_MODEL), w.dtype)
    for g in range(4):
        out = lax.dynamic_update_slice(out, w[g], (g * 64, g * 256))
    return out


def _pool_from_block_diag(wbd):
    return jnp.stack([wbd[g * 64:(g + 1) * 64, g * 256:(g + 1) * 256] for g in range(4)])


def _layer_weights(mats, vec, conv_w, l):
    w_in = _regroup_w_in(mats["w_in"])
    w_ffn_in = _interleave_ffn(mats["w_ffn_in"])
    wp = _pool_block_diag(mats["pool_w"])
    row = lambda v: v.reshape(1, -1)
    fb = jnp.zeros((1, 128), F32).at[0, :HEADS].set(vec["forget_b"][l])
    cw = jnp.zeros((8, D_CONV), F32).at[:3].set(conv_w[l])
    twice = lambda v: jnp.tile(v.reshape(1, -1), (1, 2))
    return dict(
        w_in=w_in, w_f=w_in[:, N_MAIN:], w_ffn_in=w_ffn_in, w_ffn_out=mats["w_ffn_out"],
        wa=mats["w_attn_out"], wc=mats["w_conv_out"], wp=wp, wo=mats["w_o"],
        g_mix=row(vec["norm_mix_g"][l]), g_ffn=row(vec["norm_ffn_g"][l]), gq2=twice(vec["q_norm_g"][l]),
        gk2=twice(vec["k_norm_g"][l]), scale=row(vec["pool_scale"][l]), fb=fb, cw=cw)


def _layer_fwd(x, w, l, ex):
    proj, h = norm_matmul(x, w["g_mix"], w["w_in"], N_MAIN, f"in_proj_{l}")
    z, c = forget_fwd(h, w["w_f"], w["fb"], f"forget_fwd_{l}")
    qa, ka, va, vt = attn_prep(proj, c, w["gq2"], w["gk2"], f"attn_prep_{l}")
    (oa, lse), carried = attn_forward(qa, ka, vt, f"attn_fwd_{l}", ex)
    x1 = mix_fwd(proj, oa, x, w["wa"], w["wc"], w["wp"], w["scale"], w["cw"], w["wo"], f"mix_fwd_{l}")
    gu, h2 = norm_matmul(x1, w["g_ffn"], w["w_ffn_in"], 2 * D_FF, f"ffn_in_{l}")
    x2 = swiglu_matmul(gu, w["w_ffn_out"], x1, f"ffn_out_{l}")
    saved = dict(x=x, proj=proj, h=h, z=z, qa=qa, ka=ka, va=va, oa=oa, lse=lse, x1=x1, gu=gu, h2=h2)
    return x2, saved, carried


def _layer_bwd(dx2, sv, w, l, ex):
    g = {}
    dgu, act = swiglu_bwd(dx2, sv["gu"], w["w_ffn_out"], f"ffn_out_bwd_{l}")
    g["w_ffn_out"] = tn_matmul(act, dx2, f"dw_ffn_out_{l}")
    g["w_ffn_in"] = _deinterleave_ffn(tn_matmul(sv["h2"], dgu, f"dw_ffn_in_{l}"))
    dx1, dg = matmul_normbwd(dgu, w["w_ffn_in"], sv["x1"], w["g_ffn"], dx2, f"ffn_in_bwd_{l}")
    g["norm_ffn_g"] = dg[0]

    (dproj, doa, a_tok, merged, dya, dyc, dyp, uc, dd, dscale, dcw) = mix_bwd(
        sv["proj"], sv["oa"], dx1, w["wa"], w["wc"], w["wp"], w["scale"], w["cw"], w["wo"], f"mix_bwd_{l}")
    g["w_o"] = tn_matmul(merged, dx1, f"dw_o_{l}")
    g["w_attn_out"] = tn_matmul(a_tok, dya, f"dw_attn_out_{l}")
    g["w_conv_out"] = tn_matmul(uc, dyc, f"dw_conv_out_{l}")
    g["pool_w"] = _pool_from_block_diag(tn_matmul(dd, dyp, f"dw_pool_{l}"))
    g["pool_scale"] = dscale[0]
    g["conv_w"] = dcw[:3]

    (dqa, dka, dva), carried = attn_backward(sv["qa"], sv["ka"], sv["va"], sv["oa"], doa, sv["lse"], f"attn_bwd_{l}", ex)
    dproj, dc, dgq, dgk = attn_post(dqa, dka, dva, sv["proj"], w["gq2"], w["gk2"], dproj, f"attn_post_{l}")
    g["q_norm_g"] = dgq[0, :HEAD_DIM] + dgq[0, HEAD_DIM:]
    g["k_norm_g"] = dgk[0, :HEAD_DIM] + dgk[0, HEAD_DIM:]
    dproj, db = forget_bwd(dc, sv["z"], dproj, f"forget_bwd_{l}")
    g["forget_b"] = db[0, :HEADS]

    g["w_in"] = _ungroup_w_in(tn_matmul(sv["h"], dproj, f"dw_in_{l}", n_cols=N_FULL))
    dx, dg = matmul_normbwd(dproj, w["w_in"], sv["x"], w["g_mix"], dx1, f"in_proj_bwd_{l}", k=N_FULL)
    g["norm_mix_g"] = dg[0]
    return dx, g, carried


def _local_step(x, tgt, hooks):
    ws, saved = [], []
    w = hooks.weights(0, None)
    for l in range(DEPTH):
        ws.append(w)
        ex = hooks.gather(l + 1) if l + 1 < DEPTH else None
        x, sv, got = _layer_fwd(x, w, l, ex)
        saved.append(sv)
        if l + 1 < DEPTH:
            w = hooks.weights(l + 1, got)
    sq, dx = loss_kernel(x, tgt, "loss")
    pending = None
    for l in reversed(range(DEPTH)):
        dx, g, got = _layer_bwd(dx, saved[l], ws[l], l, pending)
        if pending:
            hooks.scattered(got)
        pending = hooks.scatter(l, g)
    if pending:
        hooks.scattered(run_exchange(pending, "exchange_0"))
    return sq[0, 0], dx


def kernel(x, norm_mix_g, w_in, forget_b, q_norm_g, k_norm_g, w_attn_out, conv_w, w_conv_out, pool_w, pool_scale, w_o, norm_ffn_g, w_ffn_in, w_ffn_out, loss_target, m_norm_mix_g, m_w_in, m_forget_b, m_q_norm_g, m_k_norm_g, m_w_attn_out, m_conv_w, m_w_conv_out, m_pool_w, m_pool_scale, m_w_o, m_norm_ffn_g, m_w_ffn_in, m_w_ffn_out, v_norm_mix_g, v_w_in, v_forget_b, v_q_norm_g, v_k_norm_g, v_w_attn_out, v_conv_w, v_w_conv_out, v_pool_w, v_pool_scale, v_w_o, v_norm_ffn_g, v_w_ffn_in, v_w_ffn_out):
    w = dict(norm_mix_g=norm_mix_g, w_in=w_in, forget_b=forget_b, q_norm_g=q_norm_g, k_norm_g=k_norm_g,
             w_attn_out=w_attn_out, conv_w=conv_w, w_conv_out=w_conv_out, pool_w=pool_w, pool_scale=pool_scale,
             w_o=w_o, norm_ffn_g=norm_ffn_g, w_ffn_in=w_ffn_in, w_ffn_out=w_ffn_out)
    m = dict(norm_mix_g=m_norm_mix_g, w_in=m_w_in, forget_b=m_forget_b, q_norm_g=m_q_norm_g, k_norm_g=m_k_norm_g,
             w_attn_out=m_w_attn_out, conv_w=m_conv_w, w_conv_out=m_w_conv_out, pool_w=m_pool_w,
             pool_scale=m_pool_scale, w_o=m_w_o, norm_ffn_g=m_norm_ffn_g, w_ffn_in=m_w_ffn_in, w_ffn_out=m_w_ffn_out)
    v = dict(norm_mix_g=v_norm_mix_g, w_in=v_w_in, forget_b=v_forget_b, q_norm_g=v_q_norm_g, k_norm_g=v_k_norm_g,
             w_attn_out=v_w_attn_out, conv_w=v_conv_w, w_conv_out=v_w_conv_out, pool_w=v_pool_w,
             pool_scale=v_pool_scale, w_o=v_w_o, norm_ffn_g=v_norm_ffn_g, w_ffn_in=v_w_ffn_in, w_ffn_out=v_w_ffn_out)
    me = 4 * lax.axis_index("x") + 2 * lax.axis_index("y") + lax.axis_index("c")
    layer_shard = {n: SHARD_INFO[n][0][1:] for n in MATRICES}
    cut_axis = {n: SHARD_INFO[n][1] - 1 for n in MATRICES}

    conv_g = all_gather([_pack([conv_w], 8, 128)], "gather_conv_w")[0]
    conv_full = _join_shards(jnp.stack([_unpack(conv_g[i], [conv_w.shape])[0] for i in range(N_DEV)]), 2)
    vec = {n: w[n] for n in VECTORS}

    class Hooks:
        bufs = [lax.empty((DEPTH, N_DEV) + layer_shard[n], BF16) for n in MATRICES]
        small_g = [None] * DEPTH

        @staticmethod
        def shards(l):
            return [w[n][l].astype(BF16) for n in MATRICES]

        @staticmethod
        def gather(l):
            return gather_exchange(Hooks.shards(l))

        @staticmethod
        def weights(l, gathered):
            if l == 0:
                gathered = all_gather(Hooks.shards(0), "gather_0")
            mats = {n: _join_shards(t, cut_axis[n]) for n, t in zip(MATRICES, gathered)}
            return _layer_weights(mats, vec, conv_full, l)

        @staticmethod
        def scatter(l, g):
            Hooks.small_g[l] = g
            return scatter_exchange([_cut_shards(g[n], cut_axis[n]) for n in MATRICES], Hooks.bufs, l)

        @staticmethod
        def scattered(results):
            Hooks.bufs = list(results)

    small_g, received = Hooks.small_g, Hooks
    sq, dx = _local_step(x[0], loss_target[0], Hooks)
    loss = lax.psum(0.5 * sq / D_MODEL, ("x", "y", "c"))

    big = {}
    for n, parts in zip(MATRICES, received.bufs):
        rc = (_size(layer_shard[n][:-1]), layer_shard[n][-1])
        outs = adamw_sum(parts.reshape((DEPTH, N_DEV) + rc), *[d[n].reshape((DEPTH,) + rc) for d in (w, m, v)], f"adamw_{n}")
        big[n] = [t.reshape(w[n].shape) for t in outs]

    small_shapes = [VECTOR_SHAPES[n] for n in VECTORS] + [CONV_W_FULL]
    stacked = [jnp.stack([small_g[l][n] for l in range(DEPTH)]) for n in VECTORS + ("conv_w",)]
    sparts = all_gather([_pack(stacked, SMALL_ROWS, 128)], "gather_vector_grads")[0]
    col0 = me * (D_CONV // N_DEV)
    place = lambda t: lax.dynamic_update_slice(jnp.zeros(CONV_W_FULL, F32), t, (0, 0, col0))
    spacked = [_pack([d[n] for n in VECTORS] + [place(d["conv_w"])], SMALL_ROWS, 128)[None] for d in (w, m, v)]
    small = [_unpack(t[0], small_shapes) for t in adamw_sum(sparts[None], *spacked, "adamw_vectors")]

    def result(kind):
        out = {n: big[n][kind] for n in MATRICES}
        out.update({n: small[kind][j] for j, n in enumerate(VECTORS)})
        out["conv_w"] = lax.dynamic_slice(small[kind][len(VECTORS)], (0, 0, col0), conv_w.shape)
        return [out[n] for n in w]

    return (loss, dx[None], *result(0), *result(1), *result(2), *result(3))
```

```python
import functools

import jax
import jax.numpy as jnp
from jax import lax
from jax.experimental import pallas as pl
from jax.experimental.pallas import tpu as pltpu

F32 = jnp.float32
BF16 = jnp.bfloat16

N_DEV = 8
DEPTH = 4
D_MODEL = 1024
HEAD_DIM = 64
HEADS = 8
D_ATTN = 512
D_CONV = 256
D_POOL = 256
D_FF = 2816
D_IN = 5640
EPS = 1e-6
ATTN_SCALE = HEAD_DIM ** -0.5

N_REST = 4096
N_MAIN = 5632
N_FULL = 5760
DPROJ_TAIL = 2048
DPROJ_COLS = N_REST + DPROJ_TAIL
FF_BLK = 256
N_FF_BLKS = D_FF // FF_BLK
HALO = 16

ADAM_LR = 0.001
ADAM_B1 = 0.9
ADAM_B2 = 0.999
ADAM_EPS = 1e-08
ADAM_WD = 0.01
ADAM_STEP = 10

PACK_COLS = 1024
PACK_ROWS = 8192
SMALL_ROWS = 128

VMEM_LIMIT = 48 * 2 ** 20


def _cparams(sem, vmem=None):
    return pltpu.CompilerParams(dimension_semantics=sem, vmem_limit_bytes=vmem or VMEM_LIMIT)


def _pick(n, cands):
    for c in cands:
        if n % c == 0:
            return c
    raise ValueError(f"no tile for {n}")


def _sigmoid(v):
    return 1.0 / (1.0 + jnp.exp(-v))


def _rstd(v):
    return lax.rsqrt(jnp.mean(v * v, axis=-1, keepdims=True) + EPS)


def _dot(a, b):
    return jnp.dot(a, b, preferred_element_type=F32)


def _dot_tn(a, b):
    return lax.dot_general(a, b, (((0,), (0,)), ((), ())), preferred_element_type=F32)


def _dot_nt(a, b):
    return lax.dot_general(a, b, (((1,), (1,)), ((), ())), preferred_element_type=F32)


def norm_matmul(x, g, w, n_cols, name, ex=None):
    s, d = x.shape
    tm, tn = min(512, s), _pick(n_cols, (1408, 512))

    def body(x_ref, g_ref, w_ref, o_ref, h_ref):
        @pl.when(pl.program_id(1) == 0)
        def _():
            xv = x_ref[...]
            h_ref[...] = (xv * _rstd(xv) * g_ref[...]).astype(BF16)

        o_ref[...] = _dot(h_ref[...], w_ref[...]).astype(BF16)

    return _carried_call(
        body, ex, (s // tm, n_cols // tn),
        [pl.BlockSpec((tm, d), lambda i, j: (i, 0)), pl.BlockSpec((1, d), lambda i, j: (0, 0)),
         pl.BlockSpec((d, tn), lambda i, j: (0, j))],
        [pl.BlockSpec((tm, tn), lambda i, j: (i, j)), pl.BlockSpec((tm, d), lambda i, j: (i, 0))],
        [jax.ShapeDtypeStruct((s, n_cols), BF16), jax.ShapeDtypeStruct((s, d), BF16)], [],
        ("arbitrary", "arbitrary"), name, (x, g, w))


def tn_matmul(a, b, name, n_cols=None):
    t, m = a.shape
    n = n_cols or b.shape[1]
    tk = min(512, t)
    tmm = _pick(m, (1024, 1408, 512, 256))
    tn = _pick(n, (1408, 1152, 1024, 512, 128))
    nk = t // tk

    def body(a_ref, b_ref, o_ref, acc_ref):
        @pl.when(pl.program_id(2) == 0)
        def _():
            acc_ref[...] = jnp.zeros_like(acc_ref)

        acc_ref[...] += _dot_tn(a_ref[...].astype(BF16), b_ref[...].astype(BF16))

        @pl.when(pl.program_id(2) == nk - 1)
        def _():
            o_ref[...] = acc_ref[...].astype(BF16)

    return pl.pallas_call(
        body, grid=(m // tmm, n // tn, nk),
        in_specs=[pl.BlockSpec((tk, tmm), lambda i, j, k: (k, i)), pl.BlockSpec((tk, tn), lambda i, j, k: (k, j))],
        out_specs=pl.BlockSpec((tmm, tn), lambda i, j, k: (i, j)),
        out_shape=jax.ShapeDtypeStruct((m, n), BF16), scratch_shapes=[pltpu.VMEM((tmm, tn), F32)],
        compiler_params=_cparams(("parallel", "parallel", "arbitrary")), name=name)(a, b)


def matmul_normbwd(a, w, x, g, dres, name, k=None):
    s = a.shape[0]
    k = k or a.shape[1]
    d = w.shape[0]
    tm = min(512, s)
    tk = _pick(k, (1408, 1152, 512))
    nk = k // tk

    def body(a_ref, w_ref, x_ref, g_ref, r_ref, dx_ref, dg_ref, acc_ref):
        i, kk = pl.program_id(0), pl.program_id(1)

        @pl.when(kk == 0)
        def _():
            acc_ref[...] = jnp.zeros_like(acc_ref)

        @pl.when((i == 0) & (kk == 0))
        def _():
            dg_ref[...] = jnp.zeros_like(dg_ref)

        acc_ref[...] += _dot_nt(a_ref[...], w_ref[...])

        @pl.when(kk == nk - 1)
        def _():
            xv = x_ref[...]
            r = _rstd(xv)
            y = xv * r
            dh = acc_ref[...]
            dy = dh * g_ref[...]
            dx_ref[...] = r_ref[...] + r * (dy - y * jnp.mean(dy * y, axis=-1, keepdims=True))
            dg_ref[...] += jnp.sum(dh * y, axis=0, keepdims=True)

    return pl.pallas_call(
        body, grid=(s // tm, nk),
        in_specs=[pl.BlockSpec((tm, tk), lambda i, kk: (i, kk)), pl.BlockSpec((d, tk), lambda i, kk: (0, kk)),
                  pl.BlockSpec((tm, d), lambda i, kk: (i, 0)), pl.BlockSpec((1, d), lambda i, kk: (0, 0)),
                  pl.BlockSpec((tm, d), lambda i, kk: (i, 0))],
        out_specs=[pl.BlockSpec((tm, d), lambda i, kk: (i, 0)), pl.BlockSpec((1, d), lambda i, kk: (0, 0))],
        out_shape=[jax.ShapeDtypeStruct((s, d), F32), jax.ShapeDtypeStruct((1, d), F32)],
        scratch_shapes=[pltpu.VMEM((tm, d), F32)],
        compiler_params=_cparams(("arbitrary", "arbitrary")), name=name)(a, w, x, g, dres)


def swiglu_matmul(gu, w, x1, name):
    s = gu.shape[0]
    d = w.shape[1]
    tm = min(512, s)

    def body(gu_ref, w_ref, x_ref, o_ref):
        acc = x_ref[...]
        for j in range(N_FF_BLKS):
            gt = gu_ref[:, 2 * j * FF_BLK:(2 * j + 1) * FF_BLK].astype(F32)
            up = gu_ref[:, (2 * j + 1) * FF_BLK:(2 * j + 2) * FF_BLK].astype(F32)
            act = (gt * _sigmoid(gt) * up).astype(BF16)
            acc += _dot(act, w_ref[j * FF_BLK:(j + 1) * FF_BLK, :])
        o_ref[...] = acc

    return pl.pallas_call(
        body, grid=(s // tm,),
        in_specs=[pl.BlockSpec((tm, 2 * D_FF), lambda i: (i, 0)), pl.BlockSpec((D_FF, d), lambda i: (0, 0)),
                  pl.BlockSpec((tm, d), lambda i: (i, 0))],
        out_specs=pl.BlockSpec((tm, d), lambda i: (i, 0)),
        out_shape=jax.ShapeDtypeStruct((s, d), F32),
        compiler_params=_cparams(("parallel",)), name=name)(gu, w, x1)


def swiglu_bwd(dx2, gu, w, name, ex=None):
    s, d = dx2.shape
    tm = min(256, s)

    def body(dx_ref, gu_ref, w_ref, dgu_ref, act_ref):
        dx = dx_ref[...].astype(BF16)
        for j in range(N_FF_BLKS):
            g_cols = slice(2 * j * FF_BLK, (2 * j + 1) * FF_BLK)
            u_cols = slice((2 * j + 1) * FF_BLK, (2 * j + 2) * FF_BLK)
            dact = _dot_nt(dx, w_ref[j * FF_BLK:(j + 1) * FF_BLK, :])
            gt = gu_ref[:, g_cols].astype(F32)
            up = gu_ref[:, u_cols].astype(F32)
            sg = _sigmoid(gt)
            act_ref[:, j * FF_BLK:(j + 1) * FF_BLK] = (gt * sg * up).astype(BF16)
            dgu_ref[:, g_cols] = (dact * up * (sg * (1.0 + gt * (1.0 - sg)))).astype(BF16)
            dgu_ref[:, u_cols] = (dact * gt * sg).astype(BF16)

    return _carried_call(
        body, ex, (s // tm,),
        [pl.BlockSpec((tm, d), lambda i: (i, 0)), pl.BlockSpec((tm, 2 * D_FF), lambda i: (i, 0)),
         pl.BlockSpec((D_FF, d), lambda i: (0, 0))],
        [pl.BlockSpec((tm, 2 * D_FF), lambda i: (i, 0)), pl.BlockSpec((tm, D_FF), lambda i: (i, 0))],
        [jax.ShapeDtypeStruct((s, 2 * D_FF), BF16), jax.ShapeDtypeStruct((s, D_FF), BF16)], [],
        ("arbitrary",), name, (dx2, gu, w))


def loss_kernel(y, tgt, name):
    s, d = y.shape
    tm = min(512, s)

    def body(y_ref, t_ref, l_ref, dy_ref):
        @pl.when(pl.program_id(0) == 0)
        def _():
            l_ref[...] = jnp.zeros_like(l_ref)

        err = y_ref[...] - t_ref[...]
        dy_ref[...] = err * (1.0 / d)
        l_ref[...] += jnp.sum(jnp.sum(err * err, axis=1, keepdims=True), axis=0, keepdims=True)

    return pl.pallas_call(
        body, grid=(s // tm,),
        in_specs=[pl.BlockSpec((tm, d), lambda i: (i, 0)), pl.BlockSpec((tm, d), lambda i: (i, 0))],
        out_specs=[pl.BlockSpec((8, 128), lambda i: (0, 0)), pl.BlockSpec((tm, d), lambda i: (i, 0))],
        out_shape=[jax.ShapeDtypeStruct((8, 128), F32), jax.ShapeDtypeStruct((s, d), F32)],
        compiler_params=_cparams(("arbitrary",)), name=name)(y, tgt)


def _split3(v):
    a1 = v.astype(BF16)
    r1 = v - a1.astype(F32)
    a2 = r1.astype(BF16)
    a3 = (r1 - a2.astype(F32)).astype(BF16)
    return a1, a2, a3


def forget_fwd(h, wf, b, name):
    s, d = h.shape
    tm = min(512, s)

    def body(h_ref, w_ref, b_ref, z_ref, c_ref, carry_ref):
        @pl.when(pl.program_id(0) == 0)
        def _():
            carry_ref[...] = jnp.zeros_like(carry_ref)

        z = _dot(h_ref[...], w_ref[...]) + b_ref[...]
        z_ref[...] = z
        logf = jnp.minimum(z, 0.0) - jnp.log(1.0 + jnp.exp(-jnp.abs(z)))
        row = lax.broadcasted_iota(jnp.int32, (tm, tm), 0)
        col = lax.broadcasted_iota(jnp.int32, (tm, tm), 1)
        tri = (row >= col).astype(BF16)
        a1, a2, a3 = _split3(logf)
        c = _dot(tri, a1) + _dot(tri, a2) + _dot(tri, a3) + carry_ref[...]
        c_ref[...] = c
        carry_ref[...] = c[tm - 1:tm, :]

    return pl.pallas_call(
        body, grid=(s // tm,),
        in_specs=[pl.BlockSpec((tm, d), lambda i: (i, 0)), pl.BlockSpec((d, 128), lambda i: (0, 0)),
                  pl.BlockSpec((1, 128), lambda i: (0, 0))],
        out_specs=[pl.BlockSpec((tm, 128), lambda i: (i, 0)), pl.BlockSpec((tm, 128), lambda i: (i, 0))],
        out_shape=[jax.ShapeDtypeStruct((s, 128), F32), jax.ShapeDtypeStruct((s, 128), F32)],
        scratch_shapes=[pltpu.VMEM((1, 128), F32)],
        compiler_params=_cparams(("arbitrary",)), name=name)(h, wf, b)


def forget_bwd(dc, z, dproj, name):
    s = dc.shape[0]
    tm = min(512, s)
    nt = s // tm

    def body(dc_ref, z_ref, dp_ref, dz_ref, db_ref, carry_ref):
        @pl.when(pl.program_id(0) == 0)
        def _():
            carry_ref[...] = jnp.zeros_like(carry_ref)
            db_ref[...] = jnp.zeros_like(db_ref)

        row = lax.broadcasted_iota(jnp.int32, (tm, tm), 0)
        col = lax.broadcasted_iota(jnp.int32, (tm, tm), 1)
        tri = (col >= row).astype(BF16)
        a1, a2, a3 = _split3(dc_ref[...])
        dlogf = _dot(tri, a1) + _dot(tri, a2) + _dot(tri, a3) + carry_ref[...]
        carry_ref[...] = dlogf[0:1, :]
        dz = dlogf * (1.0 - _sigmoid(z_ref[...]))
        dz_ref[...] = dz.astype(BF16)
        db_ref[...] += jnp.sum(dz, axis=0, keepdims=True)

    return pl.pallas_call(
        body, grid=(nt,),
        in_specs=[pl.BlockSpec((tm, 128), lambda i: (nt - 1 - i, 0)), pl.BlockSpec((tm, 128), lambda i: (nt - 1 - i, 0)),
                  pl.BlockSpec(memory_space=pl.ANY)],
        out_specs=[pl.BlockSpec((tm, 128), lambda i: (nt - 1 - i, N_MAIN // 128)), pl.BlockSpec((1, 128), lambda i: (0, 0))],
        out_shape=[jax.ShapeDtypeStruct(dproj.shape, BF16), jax.ShapeDtypeStruct((1, 128), F32)],
        scratch_shapes=[pltpu.VMEM((1, 128), F32)], input_output_aliases={2: 0},
        compiler_params=_cparams(("arbitrary",)), name=name)(dc, z, dproj)


HEAD_GROUP = 2
LANE_C = 64
LANE_ONE = 67


def _lanes():
    lane = lax.broadcasted_iota(jnp.int32, (1, 128), 1)
    return lane, lane < HEAD_DIM


def _half_mean(t, lo):
    s_lo = jnp.sum(jnp.where(lo, t, 0.0), axis=-1, keepdims=True)
    s_hi = jnp.sum(jnp.where(lo, 0.0, t), axis=-1, keepdims=True)
    return jnp.where(lo, s_lo, s_hi) * (1.0 / HEAD_DIM)


def _lane_col(t, lane, idx):
    return jnp.sum(jnp.where(lane == idx, t, 0.0), axis=-1, keepdims=True)


def _swap_halves(t):
    return pltpu.roll(t, HEAD_DIM, 1)


def _causal(s_blk, tq, tk):
    row = lax.broadcasted_iota(jnp.int32, (tq, tk), 0)
    col = lax.broadcasted_iota(jnp.int32, (tq, tk), 1)
    return jnp.where(row >= col, s_blk, -jnp.inf)


def attn_prep(proj, c, gq2, gk2, name):
    s = proj.shape[0]
    tm = min(512, s)
    first = N_REST // 128

    def body(q_ref, k_ref, v_ref, c_ref, gq_ref, gk_ref, qa_ref, ka_ref, va_ref, vt_ref):
        j = pl.program_id(1)
        lane, lo = _lanes()

        def normed(ref, g):
            t = ref[...].astype(F32)
            return t * lax.rsqrt(_half_mean(t * t, lo) + EPS) * g

        qn = normed(q_ref, gq_ref[...] * ATTN_SCALE)
        kn = normed(k_ref, gk_ref[...])
        vv = v_ref[...].astype(F32)
        cv = c_ref[...]
        one_q = jnp.where((lane >= LANE_ONE) & (lane < LANE_ONE + 3), 1.0, 0.0)
        one_k = jnp.where((lane >= LANE_C) & (lane < LANE_C + 3), 1.0, 0.0)
        one_v = jnp.where(lane == LANE_C, 1.0, 0.0)
        for e in range(2):
            pick = (lambda t: t) if e == 0 else _swap_halves
            pieces = [p.astype(F32) for p in _split3(_lane_col(cv, lane, 2 * j + e))]
            ext_q, ext_k = one_q, one_k
            for i, p in enumerate(pieces):
                ext_q = jnp.where(lane == LANE_C + i, p, ext_q)
                ext_k = jnp.where(lane == LANE_ONE + i, -p, ext_k)
            qa_ref[e] = jnp.where(lo, pick(qn), ext_q).astype(BF16)
            ka_ref[e] = jnp.where(lo, pick(kn), ext_k).astype(BF16)
            va = jnp.where(lo, pick(vv), one_v)
            va_ref[e] = va.astype(BF16)
            vt_ref[e] = va.T.astype(BF16)

    tile = lambda base: pl.BlockSpec((tm, 128), lambda i, j: (i, base + j))
    vec = pl.BlockSpec((1, 128), lambda i, j: (0, 0))
    out = pl.BlockSpec((2, tm, 128), lambda i, j: (j, i, 0))
    return pl.pallas_call(
        body, grid=(s // tm, HEADS // 2),
        in_specs=[tile(first), tile(first + 4), tile(first + 8), pl.BlockSpec((tm, 128), lambda i, j: (i, 0)), vec, vec],
        out_specs=[out, out, out, pl.BlockSpec((2, 128, tm), lambda i, j: (j, 0, i))],
        out_shape=[jax.ShapeDtypeStruct((HEADS, s, 128), BF16)] * 3 + [jax.ShapeDtypeStruct((HEADS, 128, s), BF16)],
        compiler_params=_cparams(("parallel", "arbitrary")), name=name)(proj, proj, proj, c, gq2, gk2)


def attn_fwd(q, k, v, ccol, crow, gq, gk, name):
    hh, s, hd = q.shape
    tq = tk = min(512, s)
    nq = s // tq

    def body(q_ref, k_ref, v_ref, cc_ref, cr_ref, gq_ref, gk_ref, o_ref, lse_ref, qn_ref, m_ref, l_ref, acc_ref):
        qi, ki = pl.program_id(1), pl.program_id(2)

        @pl.when(ki == 0)
        def _():
            qn_ref[...] = _qk_hat(q_ref, gq_ref, ATTN_SCALE)
            m_ref[...] = jnp.full_like(m_ref, -jnp.inf)
            l_ref[...] = jnp.zeros_like(l_ref)
            acc_ref[...] = jnp.zeros_like(acc_ref)

        @pl.when(ki <= qi)
        def _():
            kn = _qk_hat(k_ref, gk_ref, 1.0)
            sb = _dot_nt(qn_ref[...], kn) + (cc_ref[...] - cr_ref[...])
            sb = _causal(sb, qi, ki, tq, tk)
            m_new = jnp.maximum(m_ref[...], jnp.max(sb, axis=-1, keepdims=True))
            alpha = jnp.exp(m_ref[...] - m_new)
            p = jnp.exp(sb - m_new)
            l_ref[...] = alpha * l_ref[...] + jnp.sum(p, axis=-1, keepdims=True)
            acc_ref[...] = alpha * acc_ref[...] + _dot(p.astype(BF16), v_ref[...])
            m_ref[...] = m_new

        @pl.when(ki == qi)
        def _():
            o_ref[...] = (acc_ref[...] / l_ref[...]).astype(BF16)
            lse_ref[...] = m_ref[...] + jnp.log(l_ref[...])

    qspec = pl.BlockSpec((None, tq, hd), lambda h, i, j: (h, i, 0))
    kspec = pl.BlockSpec((None, tk, hd), lambda h, i, j: (h, jnp.minimum(i, j), 0))
    gspec = pl.BlockSpec((1, hd), lambda h, i, j: (0, 0))
    return pl.pallas_call(
        body, grid=(hh, nq, nq),
        in_specs=[qspec, kspec, kspec,
                  pl.BlockSpec((None, tq, 1), lambda h, i, j: (h, i, 0)),
                  pl.BlockSpec((None, 1, tk), lambda h, i, j: (h, 0, jnp.minimum(i, j))), gspec, gspec],
        out_specs=[qspec, pl.BlockSpec((None, tq, 1), lambda h, i, j: (h, i, 0))],
        out_shape=[jax.ShapeDtypeStruct((hh, s, hd), BF16), jax.ShapeDtypeStruct((hh, s, 1), F32)],
        scratch_shapes=[pltpu.VMEM((tq, hd), BF16), pltpu.VMEM((tq, 1), F32), pltpu.VMEM((tq, 1), F32),
                        pltpu.VMEM((tq, hd), F32)],
        compiler_params=_cparams(("parallel", "parallel", "arbitrary")), name=name)(q, k, v, ccol, crow, gq, gk)


def attn_bwd_dq(q, k, v, o, do, lse, ccol, crow, gq, gk, name):
    hh, s, hd = q.shape
    tq = tk = min(512, s)
    nq = s // tq

    def body(q_ref, k_ref, v_ref, o_ref, do_ref, lse_ref, cc_ref, cr_ref, gq_ref, gk_ref,
             dq_ref, dcc_ref, dg_ref, qn_ref, dl_ref, acc_ref, dca_ref):
        h, qi, ki = pl.program_id(0), pl.program_id(1), pl.program_id(2)

        @pl.when((h == 0) & (qi == 0) & (ki == 0))
        def _():
            dg_ref[...] = jnp.zeros_like(dg_ref)

        @pl.when(ki == 0)
        def _():
            qn_ref[...] = _qk_hat(q_ref, gq_ref, ATTN_SCALE)
            dl_ref[...] = jnp.sum(do_ref[...].astype(F32) * o_ref[...].astype(F32), axis=-1, keepdims=True)
            acc_ref[...] = jnp.zeros_like(acc_ref)
            dca_ref[...] = jnp.zeros_like(dca_ref)

        @pl.when(ki <= qi)
        def _():
            kn = _qk_hat(k_ref, gk_ref, 1.0)
            sb = _dot_nt(qn_ref[...], kn) + (cc_ref[...] - cr_ref[...])
            p = jnp.exp(_causal(sb, qi, ki, tq, tk) - lse_ref[...])
            dp = _dot_nt(do_ref[...], v_ref[...])
            ds = p * (dp - dl_ref[...])
            acc_ref[...] += _dot(ds.astype(BF16), kn)
            dca_ref[...] += jnp.sum(ds, axis=-1, keepdims=True)

        @pl.when(ki == qi)
        def _():
            dq, dg = _norm_bwd(q_ref[...].astype(F32), gq_ref[...], acc_ref[...], ATTN_SCALE)
            dq_ref[...] = dq.astype(BF16)
            dcc_ref[...] = dca_ref[...]
            dg_ref[...] += dg

    qspec = pl.BlockSpec((None, tq, hd), lambda h, i, j: (h, i, 0))
    kspec = pl.BlockSpec((None, tk, hd), lambda h, i, j: (h, jnp.minimum(i, j), 0))
    cspec = pl.BlockSpec((None, tq, 1), lambda h, i, j: (h, i, 0))
    gspec = pl.BlockSpec((1, hd), lambda h, i, j: (0, 0))
    return pl.pallas_call(
        body, grid=(hh, nq, nq),
        in_specs=[qspec, kspec, kspec, qspec, qspec, cspec, cspec,
                  pl.BlockSpec((None, 1, tk), lambda h, i, j: (h, 0, jnp.minimum(i, j))), gspec, gspec],
        out_specs=[qspec, cspec, gspec],
        out_shape=[jax.ShapeDtypeStruct((hh, s, hd), BF16), jax.ShapeDtypeStruct((hh, s, 1), F32),
                   jax.ShapeDtypeStruct((1, hd), F32)],
        scratch_shapes=[pltpu.VMEM((tq, hd), BF16), pltpu.VMEM((tq, 1), F32), pltpu.VMEM((tq, hd), F32),
                        pltpu.VMEM((tq, 1), F32)],
        compiler_params=_cparams(("arbitrary", "arbitrary", "arbitrary")), name=name)(
            q, k, v, o, do, lse, ccol, crow, gq, gk)


def attn_bwd_dkv(q, k, v, o, do, lse, ccol, crow, gq, gk, name):
    hh, s, hd = q.shape
    tq = tk = min(512, s)
    nq = s // tq

    def body(q_ref, k_ref, v_ref, o_ref, do_ref, lse_ref, cc_ref, cr_ref, gq_ref, gk_ref,
             dk_ref, dv_ref, dcr_ref, dg_ref, kn_ref, dka_ref, dva_ref, dca_ref):
        h, ki, qi = pl.program_id(0), pl.program_id(1), pl.program_id(2)

        @pl.when((h == 0) & (ki == 0) & (qi == 0))
        def _():
            dg_ref[...] = jnp.zeros_like(dg_ref)

        @pl.when(qi == 0)
        def _():
            kn_ref[...] = _qk_hat(k_ref, gk_ref, 1.0)
            dka_ref[...] = jnp.zeros_like(dka_ref)
            dva_ref[...] = jnp.zeros_like(dva_ref)
            dca_ref[...] = jnp.zeros_like(dca_ref)

        @pl.when(qi >= ki)
        def _():
            qn = _qk_hat(q_ref, gq_ref, ATTN_SCALE)
            do = do_ref[...]
            delta = jnp.sum(do.astype(F32) * o_ref[...].astype(F32), axis=-1, keepdims=True)
            sb = _dot_nt(qn, kn_ref[...]) + (cc_ref[...] - cr_ref[...])
            p = jnp.exp(_causal(sb, qi, ki, tq, tk) - lse_ref[...])
            dva_ref[...] += _dot_tn(p.astype(BF16), do)
            ds = p * (_dot_nt(do, v_ref[...]) - delta)
            dka_ref[...] += _dot_tn(ds.astype(BF16), qn)
            dca_ref[...] += jnp.sum(ds, axis=0, keepdims=True)

        @pl.when(qi == nq - 1)
        def _():
            dk, dg = _norm_bwd(k_ref[...].astype(F32), gk_ref[...], dka_ref[...], 1.0)
            dk_ref[...] = dk.astype(BF16)
            dv_ref[...] = dva_ref[...].astype(BF16)
            dcr_ref[...] = dca_ref[...]
            dg_ref[...] += dg

    kspec = pl.BlockSpec((None, tk, hd), lambda h, j, i: (h, j, 0))
    qspec = pl.BlockSpec((None, tq, hd), lambda h, j, i: (h, jnp.maximum(i, j), 0))
    cspec = pl.BlockSpec((None, tq, 1), lambda h, j, i: (h, jnp.maximum(i, j), 0))
    rspec = pl.BlockSpec((None, 1, tk), lambda h, j, i: (h, 0, j))
    gspec = pl.BlockSpec((1, hd), lambda h, j, i: (0, 0))
    return pl.pallas_call(
        body, grid=(hh, nq, nq),
        in_specs=[qspec, kspec, kspec, qspec, qspec, cspec, cspec, rspec, gspec, gspec],
        out_specs=[kspec, kspec, rspec, gspec],
        out_shape=[jax.ShapeDtypeStruct((hh, s, hd), BF16), jax.ShapeDtypeStruct((hh, s, hd), BF16),
                   jax.ShapeDtypeStruct((hh, 1, s), F32), jax.ShapeDtypeStruct((1, hd), F32)],
        scratch_shapes=[pltpu.VMEM((tk, hd), BF16), pltpu.VMEM((tk, hd), F32), pltpu.VMEM((tk, hd), F32),
                        pltpu.VMEM((1, tk), F32)],
        compiler_params=_cparams(("arbitrary", "arbitrary", "arbitrary")), name=name)(
            q, k, v, o, do, lse, ccol, crow, gq, gk)


def _carry(ex, n_in, n_out, n_scratch, grid):
    n_xin, n_xout = (len(ex.inputs), len(ex.out_shapes)) if ex else (0, 0)

    def split(refs):
        ins, xins = refs[:n_in], refs[n_in:n_in + n_xin]
        rest = refs[n_in + n_xin:]
        outs, xouts = rest[:n_out], rest[n_out:n_out + n_xout]
        rest = rest[n_out + n_xout:]
        return ins + outs + rest[:n_scratch], (xins, xouts, rest[n_scratch:])

    def first():
        return functools.reduce(lambda a, b: a & b, [pl.program_id(d) == 0 for d in range(len(grid))])

    def last():
        return functools.reduce(lambda a, b: a & b, [pl.program_id(d) == grid[d] - 1 for d in range(len(grid))])

    return split, first, last


def _carried_call(body, ex, grid, in_specs, out_specs, out_shape, scratch, sem, name, operands):
    any_spec = pl.BlockSpec(memory_space=pl.ANY)
    split, first, last = _carry(ex, len(in_specs), len(out_specs), len(scratch), grid)

    def carried(*refs):
        own, xrefs = split(refs)
        if ex:
            @pl.when(first())
            def _():
                ex.start(*xrefs)

        body(*own)
        if ex:
            @pl.when(last())
            def _():
                ex.drain(*xrefs)

    n_xin = len(ex.inputs) if ex else 0
    results = pl.pallas_call(
        carried, grid=grid, in_specs=list(in_specs) + [any_spec] * n_xin,
        out_specs=list(out_specs) + [any_spec] * (len(ex.out_shapes) if ex else 0),
        out_shape=list(out_shape) + (list(ex.out_shapes) if ex else []),
        input_output_aliases={len(in_specs) + i: len(out_specs) + o for i, o in ex.aliases.items()} if ex else {},
        scratch_shapes=list(scratch) + (ex.scratch if ex else []),
        compiler_params=_cparams(sem), name=name)(*operands, *(ex.inputs if ex else []))
    return results[:len(out_specs)], results[len(out_specs):]


def _tri_rows(t, n):
    qi = sum(jnp.where(t >= r * (r + 1) // 2, 1, 0) for r in range(1, n))
    return qi, t - qi * (qi + 1) // 2


def _tri_cols(t, n):
    ki = sum(jnp.where(t >= r * n - r * (r - 1) // 2, 1, 0) for r in range(1, n))
    return ki, ki + t - (ki * n - ki * (ki - 1) // 2)


def _causal_t(st_blk, tk, tq):
    key = lax.broadcasted_iota(jnp.int32, (tk, tq), 0)
    qry = lax.broadcasted_iota(jnp.int32, (tk, tq), 1)
    return jnp.where(qry >= key, st_blk, -jnp.inf)


def attn_forward(qa, ka, vt, name, ex=None):
    hh, s, _ = qa.shape
    tq = tk = min(512, s)
    nq = s // tq
    grp = HEAD_GROUP

    def body(q_ref, k_ref, vt_ref, o_ref, lse_ref, m_ref, acc_ref):
        qi, ki = _tri_rows(pl.program_id(1), nq)

        @pl.when(ki == 0)
        def _():
            m_ref[...] = jnp.full_like(m_ref, -jnp.inf)
            acc_ref[...] = jnp.zeros_like(acc_ref)

        def step(masked):
            nxt = _dot_nt(k_ref[0], q_ref[0])
            for g in range(grp):
                st = nxt
                if g + 1 < grp:
                    nxt = _dot_nt(k_ref[g + 1], q_ref[g + 1])
                if masked:
                    st = _causal_t(st, tk, tq)
                m_old = m_ref[g]
                m_new = jnp.maximum(m_old, jnp.max(st, axis=0, keepdims=True))
                pt = jnp.exp(st - m_new).astype(BF16)
                acc_ref[g] = jnp.exp(m_old - m_new) * acc_ref[g] + _dot(vt_ref[g], pt)
                m_ref[g] = m_new

        @pl.when(ki < qi)
        def _():
            step(False)

        @pl.when(ki == qi)
        def _():
            step(True)
            for g in range(grp):
                acc = acc_ref[g]
                denom = acc[LANE_C:LANE_C + 1, :]
                o_ref[g] = (acc / denom).T.astype(BF16)
                lse_ref[g] = m_ref[g] + jnp.log(denom)

    qspec = pl.BlockSpec((grp, tq, 128), lambda h, t: (h, _tri_rows(t, nq)[0], 0))
    kspec = pl.BlockSpec((grp, tk, 128), lambda h, t: (h, _tri_rows(t, nq)[1], 0))
    vspec = pl.BlockSpec((grp, 128, tk), lambda h, t: (h, 0, _tri_rows(t, nq)[1]))
    lspec = pl.BlockSpec((grp, 1, tq), lambda h, t: (h, 0, _tri_rows(t, nq)[0]))
    return _carried_call(
        body, ex, (hh // grp, nq * (nq + 1) // 2), [qspec, kspec, vspec], [qspec, lspec],
        [jax.ShapeDtypeStruct((hh, s, 128), BF16), jax.ShapeDtypeStruct((hh, 1, s), F32)],
        [pltpu.VMEM((grp, 1, tq), F32), pltpu.VMEM((grp, 128, tq), F32)],
        ("arbitrary", "arbitrary"), name, (qa, ka, vt))


def attn_backward(qa, ka, va, oa, doa, lse, name, ex=None):
    hh, s, _ = qa.shape
    tq = tk = min(512, s)
    nq = s // tq
    grp = HEAD_GROUP

    def body(q_ref, k_ref, v_ref, o_ref, do_ref, lse_ref, dq_ref, dk_ref, dv_ref, dka_ref, dva_ref):
        ki, qi = _tri_cols(pl.program_id(1), nq)

        @pl.when(pl.program_id(1) == 0)
        def _():
            dq_ref[...] = jnp.zeros_like(dq_ref)

        @pl.when(qi == ki)
        def _():
            dka_ref[...] = jnp.zeros_like(dka_ref)
            dva_ref[...] = jnp.zeros_like(dva_ref)

        def step(masked):
            rows = pl.ds(pl.multiple_of(qi * tq, tq), tq)
            products = lambda g: (_dot_nt(k_ref[g], q_ref[g]), _dot_nt(v_ref[g], do_ref[g]))
            nxt = products(0)
            for g in range(grp):
                st, dpt = nxt
                if g + 1 < grp:
                    nxt = products(g + 1)
                q, k, do = q_ref[g], k_ref[g], do_ref[g]
                if masked:
                    st = _causal_t(st, tk, tq)
                pt = jnp.exp(st - lse_ref[g])
                delta = jnp.sum((do.astype(F32) * o_ref[g].astype(F32)).T, axis=0, keepdims=True)
                dst = (pt * (dpt - delta)).astype(BF16)
                dva_ref[g] += _dot(pt.astype(BF16), do)
                dka_ref[g] += _dot(dst, q)
                dq_ref[g, rows, :] += _dot_tn(dst, k)

        @pl.when(qi > ki)
        def _():
            step(False)

        @pl.when(qi == ki)
        def _():
            step(True)

        @pl.when(qi == nq - 1)
        def _():
            dk_ref[...] = dka_ref[...]
            dv_ref[...] = dva_ref[...].astype(BF16)

    qspec = pl.BlockSpec((grp, tq, 128), lambda h, t: (h, _tri_cols(t, nq)[1], 0))
    lspec = pl.BlockSpec((grp, 1, tq), lambda h, t: (h, 0, _tri_cols(t, nq)[1]))
    kspec = pl.BlockSpec((grp, tk, 128), lambda h, t: (h, _tri_cols(t, nq)[0], 0))
    return _carried_call(
        body, ex, (hh // grp, nq * (nq + 1) // 2), [qspec, kspec, kspec, qspec, qspec, lspec],
        [pl.BlockSpec((grp, s, 128), lambda h, t: (h, 0, 0)), kspec, kspec],
        [jax.ShapeDtypeStruct((hh, s, 128), F32), jax.ShapeDtypeStruct((hh, s, 128), F32),
         jax.ShapeDtypeStruct((hh, s, 128), BF16)],
        [pltpu.VMEM((grp, tk, 128), F32), pltpu.VMEM((grp, tk, 128), F32)],
        ("arbitrary", "arbitrary"), name, (qa, ka, va, oa, doa, lse))


def attn_post(dqa, dka, dva, proj, gq2, gk2, dproj, name):
    s = proj.shape[0]
    tm = min(256, s)

    def body(dq_ref, dk_ref, dv_ref, q_ref, k_ref, gq_ref, gk_ref, dp_any, dp_ref, dc_ref, dgq_ref, dgk_ref):
        lane, lo = _lanes()

        @pl.when(pl.program_id(0) == 0)
        def _():
            dgq_ref[...] = jnp.zeros_like(dgq_ref)
            dgk_ref[...] = jnp.zeros_like(dgk_ref)

        def pair(ref, j):
            return jnp.where(lo, ref[2 * j].astype(F32), _swap_halves(ref[2 * j + 1].astype(F32)))

        def norm_bwd(raw, g, dhat, scale):
            r = lax.rsqrt(_half_mean(raw * raw, lo) + EPS)
            y = raw * r
            dy = dhat * (g * scale)
            return r * (dy - y * _half_mean(dy * y, lo)), jnp.sum(dhat * y, axis=0, keepdims=True) * scale

        dc = jnp.zeros((tm, 128), F32)
        for j in range(HEADS // 2):
            cols = slice(128 * j, 128 * (j + 1))
            dq, dgq = norm_bwd(q_ref[:, cols].astype(F32), gq_ref[...], pair(dq_ref, j), ATTN_SCALE)
            dk, dgk = norm_bwd(k_ref[:, cols].astype(F32), gk_ref[...], pair(dk_ref, j), 1.0)
            dgq_ref[...] += dgq
            dgk_ref[...] += dgk
            dp_ref[:, cols] = dq.astype(BF16)
            dp_ref[:, D_ATTN + 128 * j:D_ATTN + 128 * (j + 1)] = dk.astype(BF16)
            dp_ref[:, 2 * D_ATTN + 128 * j:2 * D_ATTN + 128 * (j + 1)] = pair(dv_ref, j).astype(BF16)
            for e in range(2):
                h = 2 * j + e
                col = _lane_col(dq_ref[h], lane, LANE_C) - _lane_col(dk_ref[h], lane, LANE_ONE)
                dc = jnp.where(lane == h, col, dc)
        dp_ref[:, 3 * D_ATTN:] = jnp.zeros((tm, DPROJ_TAIL - 3 * D_ATTN), BF16)
        dc_ref[...] = dc

    heads = lambda: pl.BlockSpec((HEADS, tm, 128), lambda i: (0, i, 0))
    vec = pl.BlockSpec((1, 128), lambda i: (0, 0))
    first = N_REST // D_ATTN
    return pl.pallas_call(
        body, grid=(s // tm,),
        in_specs=[heads(), heads(), heads(), pl.BlockSpec((tm, D_ATTN), lambda i: (i, first)),
                  pl.BlockSpec((tm, D_ATTN), lambda i: (i, first + 1)), vec, vec, pl.BlockSpec(memory_space=pl.ANY)],
        out_specs=[pl.BlockSpec((tm, DPROJ_TAIL), lambda i: (i, N_REST // DPROJ_TAIL)),
                   pl.BlockSpec((tm, 128), lambda i: (i, 0)), vec, vec],
        out_shape=[jax.ShapeDtypeStruct(dproj.shape, BF16), jax.ShapeDtypeStruct((s, 128), F32),
                   jax.ShapeDtypeStruct((1, 128), F32), jax.ShapeDtypeStruct((1, 128), F32)],
        input_output_aliases={7: 0},
        compiler_params=_cparams(("arbitrary",)), name=name)(dqa, dka, dva, proj, proj, gq2, gk2, dproj)


def _pool_groups(tm):
    gid = lax.broadcasted_iota(jnp.int32, (1, D_POOL), 1) // (D_POOL // 4)
    win = jnp.where(gid == 0, 2.0, jnp.where(gid == 1, 4.0, jnp.where(gid == 2, 8.0, 16.0)))
    return gid, win


def _by_group(gid, v2, v4, v8, v16):
    return jnp.where(gid == 0, v2, jnp.where(gid == 1, v4, jnp.where(gid == 2, v8, v16)))


def _branches(rest_ref, halo_ref, a_ref, wa_ref, wc_ref, wp_ref, sc_ref, cw_ref, ti, tm):
    f = lambda v: v.astype(F32)
    cx, cb, cc, px = f(rest_ref[:, 0:256]), f(rest_ref[:, 256:512]), f(rest_ref[:, 512:768]), f(rest_ref[:, 768:1024])
    live = jnp.where(ti > 0, 1.0, 0.0)
    hz = f(halo_ref[:, 0:256]) * f(halo_ref[:, 512:768]) * live
    hp = f(halo_ref[:, 768:1024]) * live
    z = cc * cx
    zf = jnp.concatenate([hz, z], axis=0)
    z1 = pltpu.roll(zf, 1, 0)[HALO:]
    z2 = pltpu.roll(zf, 2, 0)[HALO:]
    cw = cw_ref[...]
    conv = cw[2:3] * z + cw[1:2] * z1 + cw[0:1] * z2
    uc = cb * conv
    pf = jnp.concatenate([hp, px], axis=0)
    s2 = pf + pltpu.roll(pf, 1, 0)
    s4 = s2 + pltpu.roll(s2, 2, 0)
    s8 = s4 + pltpu.roll(s4, 4, 0)
    s16 = s8 + pltpu.roll(s8, 8, 0)
    gid, win = _pool_groups(tm)
    t = (ti * tm + lax.broadcasted_iota(jnp.int32, (tm, 1), 0)).astype(F32)
    inv = 1.0 / jnp.minimum(t + 1.0, win)
    dpool = _by_group(gid, s2[HALO:], s4[HALO:], s8[HALO:], s16[HALO:]) * inv - px
    _, lo = _lanes()
    a_tok = [jnp.where(lo, f(a_ref[2 * j]), _swap_halves(f(a_ref[2 * j + 1]))).astype(BF16) for j in range(HEADS // 2)]
    y_attn = _dot(a_tok[0], wa_ref[0:128, :])
    for j in range(1, HEADS // 2):
        y_attn += _dot(a_tok[j], wa_ref[128 * j:128 * (j + 1), :])
    y_conv = _dot(uc.astype(BF16), wc_ref[...])
    y_pool_raw = _dot(dpool.astype(BF16), wp_ref[...])
    sg = [_sigmoid(f(rest_ref[:, 1024 + i * D_MODEL:1024 + (i + 1) * D_MODEL])) for i in range(3)]
    return dict(cx=cx, cb=cb, cc=cc, z=z, z1=z1, z2=z2, conv=conv, uc=uc, dpool=dpool, inv=inv, gid=gid, a_tok=a_tok,
                y_attn=y_attn, y_conv=y_conv, y_pool_raw=y_pool_raw, sg=sg, cw=cw)


def _mix_specs(tm, ti_of):
    blocks_per_tile = tm // HALO
    return [
        pl.BlockSpec((tm, N_REST), lambda i: (ti_of(i), 0)),
        pl.BlockSpec((HALO, 1024), lambda i: (jnp.maximum(ti_of(i) * blocks_per_tile - 1, 0), 0)),
        pl.BlockSpec((HEADS, tm, 128), lambda i: (0, ti_of(i), 0)),
        pl.BlockSpec((D_ATTN, D_MODEL), lambda i: (0, 0)),
        pl.BlockSpec((D_CONV, D_MODEL), lambda i: (0, 0)),
        pl.BlockSpec((D_POOL, D_MODEL), lambda i: (0, 0)),
        pl.BlockSpec((1, D_MODEL), lambda i: (0, 0)),
        pl.BlockSpec((8, D_CONV), lambda i: (0, 0)),
    ]


def mix_fwd(proj, a, x, wa, wc, wp, scale, cw, wo, name):
    s = x.shape[0]
    tm = min(256, s)

    def body(rest_ref, halo_ref, a_ref, wa_ref, wc_ref, wp_ref, sc_ref, cw_ref, wo_ref, x_ref, o_ref):
        b = _branches(rest_ref, halo_ref, a_ref, wa_ref, wc_ref, wp_ref, sc_ref, cw_ref, pl.program_id(0), tm)
        merged = b["sg"][0] * b["y_attn"] + b["sg"][1] * b["y_conv"] + b["sg"][2] * (b["y_pool_raw"] * sc_ref[...])
        o_ref[...] = x_ref[...] + _dot(merged.astype(BF16), wo_ref[...])

    return pl.pallas_call(
        body, grid=(s // tm,),
        in_specs=_mix_specs(tm, lambda i: i) + [pl.BlockSpec((D_MODEL, D_MODEL), lambda i: (0, 0)),
                                                 pl.BlockSpec((tm, D_MODEL), lambda i: (i, 0))],
        out_specs=pl.BlockSpec((tm, D_MODEL), lambda i: (i, 0)),
        out_shape=jax.ShapeDtypeStruct((s, D_MODEL), F32),
        compiler_params=_cparams(("parallel",)), name=name)(proj, proj, a, wa, wc, wp, scale, cw, wo, x)


def mix_bwd(proj, a, dx1, wa, wc, wp, scale, cw, wo, name):
    s = dx1.shape[0]
    tm = min(256, s)
    nt = s // tm
    ti_of = lambda i: nt - 1 - i
    n = tm + HALO

    def body(rest_ref, halo_ref, a_ref, wa_ref, wc_ref, wp_ref, sc_ref, cw_ref, wo_ref,
             dx_ref, dp_ref, da_ref, at_ref, mg_ref, dya_ref, dyc_ref, dyp_ref, uc_ref, dd_ref, dsc_ref, dcw_ref,
             cdc_ref, cde_ref):
        i = pl.program_id(0)
        ti = ti_of(i)

        @pl.when(i == 0)
        def _():
            cdc_ref[...] = jnp.zeros_like(cdc_ref)
            cde_ref[...] = jnp.zeros_like(cde_ref)
            dsc_ref[...] = jnp.zeros_like(dsc_ref)
            dcw_ref[...] = jnp.zeros_like(dcw_ref)

        b = _branches(rest_ref, halo_ref, a_ref, wa_ref, wc_ref, wp_ref, sc_ref, cw_ref, ti, tm)
        sg, sc = b["sg"], sc_ref[...]
        y_pool = b["y_pool_raw"] * sc
        merged = sg[0] * b["y_attn"] + sg[1] * b["y_conv"] + sg[2] * y_pool
        mg_ref[...] = merged.astype(BF16)
        dm = _dot_nt(dx_ref[...].astype(BF16), wo_ref[...])
        for j, y in enumerate((b["y_attn"], b["y_conv"], y_pool)):
            dp_ref[:, 1024 + j * D_MODEL:1024 + (j + 1) * D_MODEL] = (dm * y * sg[j] * (1.0 - sg[j])).astype(BF16)
        dya = (dm * sg[0]).astype(BF16)
        dya_ref[...] = dya
        _, lo = _lanes()
        for j in range(HEADS // 2):
            at_ref[:, 128 * j:128 * (j + 1)] = b["a_tok"][j]
            da = _dot_nt(dya, wa_ref[128 * j:128 * (j + 1), :])
            da_ref[2 * j] = jnp.where(lo, da, 0.0).astype(BF16)
            da_ref[2 * j + 1] = jnp.where(lo, _swap_halves(da), 0.0).astype(BF16)
        dyc = (dm * sg[1]).astype(BF16)
        dyc_ref[...] = dyc
        duc = _dot_nt(dyc, wc_ref[...])
        dyp = dm * sg[2]
        dsc_ref[...] += jnp.sum(dyp * b["y_pool_raw"], axis=0, keepdims=True)
        dypr = (dyp * sc).astype(BF16)
        dyp_ref[...] = dypr
        ddp = _dot_nt(dypr, wp_ref[...])
        uc_ref[...] = b["uc"].astype(BF16)
        dd_ref[...] = b["dpool"].astype(BF16)

        dconv = duc * b["cb"]
        dp_ref[:, 256:512] = (duc * b["conv"]).astype(BF16)
        dcf = jnp.concatenate([dconv, cdc_ref[...]], axis=0)
        cw = b["cw"]
        dz = cw[2:3] * dconv + cw[1:2] * pltpu.roll(dcf, n - 1, 0)[:tm] + cw[0:1] * pltpu.roll(dcf, n - 2, 0)[:tm]
        dp_ref[:, 0:256] = (dz * b["cc"]).astype(BF16)
        dp_ref[:, 512:768] = (dz * b["cx"]).astype(BF16)
        dcw_ref[0:1, :] += jnp.sum(dconv * b["z2"], axis=0, keepdims=True)
        dcw_ref[1:2, :] += jnp.sum(dconv * b["z1"], axis=0, keepdims=True)
        dcw_ref[2:3, :] += jnp.sum(dconv * b["z"], axis=0, keepdims=True)
        cdc_ref[...] = dconv[:HALO]

        e = ddp * b["inv"]
        ef = jnp.concatenate([e, cde_ref[...]], axis=0)
        r2 = ef + pltpu.roll(ef, n - 1, 0)
        r4 = r2 + pltpu.roll(r2, n - 2, 0)
        r8 = r4 + pltpu.roll(r4, n - 4, 0)
        r16 = r8 + pltpu.roll(r8, n - 8, 0)
        dp_ref[:, 768:1024] = (_by_group(b["gid"], r2[:tm], r4[:tm], r8[:tm], r16[:tm]) - ddp).astype(BF16)
        cde_ref[...] = e[:HALO]

    tile = lambda w: pl.BlockSpec((tm, w), lambda i: (ti_of(i), 0))
    whole = lambda r, c: pl.BlockSpec((r, c), lambda i: (0, 0))
    bf = lambda w: jax.ShapeDtypeStruct((s, w), BF16)
    return pl.pallas_call(
        body, grid=(nt,),
        in_specs=_mix_specs(tm, ti_of) + [whole(D_MODEL, D_MODEL), tile(D_MODEL)],
        out_specs=[tile(N_REST), pl.BlockSpec((HEADS, tm, 128), lambda i: (0, ti_of(i), 0)), tile(D_ATTN),
                   tile(D_MODEL), tile(D_MODEL), tile(D_MODEL), tile(D_MODEL),
                   tile(D_CONV), tile(D_POOL), whole(1, D_MODEL), whole(8, D_CONV)],
        out_shape=[bf(DPROJ_COLS), jax.ShapeDtypeStruct((HEADS, s, 128), BF16), bf(D_ATTN),
                   bf(D_MODEL), bf(D_MODEL), bf(D_MODEL), bf(D_MODEL), bf(D_CONV), bf(D_POOL),
                   jax.ShapeDtypeStruct((1, D_MODEL), F32), jax.ShapeDtypeStruct((8, D_CONV), F32)],
        scratch_shapes=[pltpu.VMEM((HALO, D_CONV), F32), pltpu.VMEM((HALO, D_POOL), F32)],
        compiler_params=_cparams(("arbitrary",)), name=name)(proj, proj, a, wa, wc, wp, scale, cw, wo, dx1)


def _adamw_math(w, g, m, v):
    m = ADAM_B1 * m + (1.0 - ADAM_B1) * g
    v = ADAM_B2 * v + (1.0 - ADAM_B2) * (g * g)
    m_hat = m / (1.0 - ADAM_B1 ** ADAM_STEP)
    v_hat = v / (1.0 - ADAM_B2 ** ADAM_STEP)
    delta = -ADAM_LR * (m_hat / (jnp.sqrt(v_hat) + ADAM_EPS) + ADAM_WD * w)
    return delta, m, v


ADAMW_PARTS_BLOCK_BYTES = 4 * 2 ** 20


def _row_tile(rows, cols, copies, itemsize):
    row_bytes = copies * (-(-cols // 128) * 128) * itemsize
    fits = [t for t in range(16, rows + 1, 16) if rows % t == 0 and t * row_bytes <= ADAMW_PARTS_BLOCK_BYTES]
    return max(fits) if fits else rows


def pair_sum(blocks, stage, me, name):
    n_slots, rows, cols = stage.shape
    tr = _row_tile(rows, cols, 1, 4)

    def body(me_ref, a_ref, b_ref, o_ref):
        o_ref[...] = (a_ref[...].astype(F32) + b_ref[...].astype(F32)).astype(BF16)

    slot = pl.BlockSpec((None, tr, cols), lambda i, r, me_ref: (i, r, 0))
    return pl.pallas_call(
        body, out_shape=jax.ShapeDtypeStruct(stage.shape, BF16),
        grid_spec=pltpu.PrefetchScalarGridSpec(
            num_scalar_prefetch=1, grid=(n_slots, rows // tr),
            in_specs=[pl.BlockSpec((None, tr, cols), lambda i, r, me_ref: (me_ref[0] ^ (2 * i), r, 0)), slot],
            out_specs=slot),
        compiler_params=_cparams(("parallel", "parallel")), name=name)(me.reshape(1), blocks, stage)


def adamw_sum(parts, w, m, v, name):
    layers, rows, cols = w.shape
    n_parts = parts.shape[1]
    tr = _row_tile(rows, cols, n_parts, parts.dtype.itemsize)

    def body(p_ref, w_ref, m_ref, v_ref, g_ref, d_ref, nm_ref, nv_ref):
        g = p_ref[0].astype(F32)
        for i in range(1, n_parts):
            g = g + p_ref[i].astype(F32)
        g_ref[...] = g
        d_ref[...], nm_ref[...], nv_ref[...] = _adamw_math(w_ref[...], g, m_ref[...], v_ref[...])

    spec = pl.BlockSpec((None, tr, cols), lambda l, i: (l, i, 0))
    return pl.pallas_call(
        body, grid=(layers, rows // tr),
        in_specs=[pl.BlockSpec((None, n_parts, tr, cols), lambda l, i: (l, 0, i, 0)), spec, spec, spec],
        out_specs=[spec] * 4, out_shape=[jax.ShapeDtypeStruct((layers, rows, cols), F32)] * 4,
        compiler_params=_cparams(("parallel", "parallel")), name=name)(parts, w, m, v)


def _me():
    return lax.axis_index("x"), lax.axis_index("y"), lax.axis_index("c")


N_PEERS = N_DEV - 1


def all_gather(shards, name):
    n = len(shards)
    any_spec = pl.BlockSpec(memory_space=pl.ANY)

    def body(*refs):
        x_refs, out_refs = refs[:n], refs[n:2 * n]
        send_sems, recv_sems, local_sems = refs[2 * n:]
        x, y, c = _me()
        me, sibling = (x, y, c), (x, y, 1 - c)
        chips = [(1 - x, y), (x, 1 - y), (1 - x, 1 - y)]

        def copy(t, k, block, to, from_input=False):
            slot = out_refs[t].at[4 * block[0] + 2 * block[1] + block[2]]
            return pltpu.make_async_remote_copy(
                src_ref=x_refs[t] if from_input else slot, dst_ref=slot, send_sem=send_sems.at[N_PEERS * t + k],
                recv_sem=recv_sems.at[N_PEERS * t + k], device_id=to, device_id_type=pl.DeviceIdType.MESH)

        mine = [pltpu.make_async_copy(x_refs[t], out_refs[t].at[4 * x + 2 * y + c], local_sems.at[t]) for t in range(n)]
        started = []
        for t in range(n):
            mine[t].start()
            started.append(copy(t, 0, me, sibling, from_input=True))
            started += [copy(t, 1 + j, me, (*chip, c), from_input=True) for j, chip in enumerate(chips)]
        for cp in started:
            cp.start()
        for j, chip in enumerate(chips):
            for t in range(n):
                copy(t, 1 + j, (*chip, c), me).wait_recv()
                fwd = copy(t, 4 + j, (*chip, c), sibling)
                fwd.start()
                started.append(fwd)
        for t in range(n):
            copy(t, 0, sibling, me).wait_recv()
            for j, chip in enumerate(chips):
                copy(t, 4 + j, (*chip, 1 - c), me).wait_recv()
        for cp in started:
            cp.wait_send()
        for cp in mine:
            cp.wait()

    return pl.pallas_call(
        body, out_shape=[jax.ShapeDtypeStruct((N_DEV,) + s.shape, s.dtype) for s in shards],
        in_specs=[any_spec] * n, out_specs=[any_spec] * n,
        scratch_shapes=[pltpu.SemaphoreType.DMA((N_PEERS * n,)), pltpu.SemaphoreType.DMA((N_PEERS * n,)),
                        pltpu.SemaphoreType.DMA((n,))],
        name=name)(*shards)


SIBLING = 1
OTHER_CHIPS = (2, 4, 6)
SAME_CORE = (0,) + OTHER_CHIPS


class Exchange:
    def __init__(self, inputs, out_shapes, aliases, copies, local=()):
        self.inputs, self.out_shapes, self.aliases = list(inputs), list(out_shapes), aliases
        self._copies, self._local = list(copies), list(local)
        self.scratch = [pltpu.SemaphoreType.DMA((len(self._copies),)), pltpu.SemaphoreType.DMA((len(self._copies),)),
                        pltpu.SemaphoreType.DMA((max(len(self._local), 1),))]

    def _build(self, ins, outs, sems):
        send_sems, recv_sems, local_sems = sems
        x, y, c = _me()
        me = 4 * x + 2 * y + c
        local = [functools.partial(pltpu.make_async_copy, src(ins, outs, me), dst(outs, me), local_sems.at[i])
                 for i, (src, dst) in enumerate(self._local)]
        sends, recvs = [], []
        for i, (mask, src, dst) in enumerate(self._copies):
            px, py, pc = x ^ ((mask >> 2) & 1), y ^ ((mask >> 1) & 1), c ^ (mask & 1)
            pair = dict(send_sem=send_sems.at[i], recv_sem=recv_sems.at[i], device_id_type=pl.DeviceIdType.MESH)
            sends.append(functools.partial(
                pltpu.make_async_remote_copy, src_ref=src(ins, outs, me), dst_ref=dst(outs, me), device_id=(px, py, pc), **pair))
            recvs.append(functools.partial(
                pltpu.make_async_remote_copy, src_ref=src(ins, outs, me), dst_ref=dst(outs, me ^ mask), device_id=(x, y, c), **pair))
        return local, sends, recvs

    def start(self, ins, outs, sems):
        local, sends, _ = self._build(ins, outs, sems)
        for make in local + sends:
            make().start()

    def drain(self, ins, outs, sems):
        local, sends, recvs = self._build(ins, outs, sems)
        for make in recvs:
            make().wait_recv()
        for make in sends:
            make().wait_send()
        for make in local:
            make().wait()


def _bind(fn, *args):
    return functools.partial(fn, *args)


def gather_over_ici(shards):
    copies = [(mask, _bind(lambda t, ins, outs, me: ins[t], t), _bind(lambda t, outs, sender: outs[t].at[sender], t))
              for t in range(len(shards)) for mask in OTHER_CHIPS]
    local = [(_bind(lambda t, ins, outs, me: ins[t], t), _bind(lambda t, outs, me: outs[t].at[me], t))
             for t in range(len(shards))]
    return Exchange(shards, [jax.ShapeDtypeStruct((N_DEV,) + s.shape, s.dtype) for s in shards], {}, copies, local)


def gather_over_d2d(gathered):
    copies = [(SIBLING, _bind(lambda t, m, ins, outs, me: outs[t].at[me ^ m], t, m),
               _bind(lambda t, m, outs, sender: outs[t].at[sender ^ m], t, m))
              for t in range(len(gathered)) for m in SAME_CORE]
    return Exchange(gathered, [jax.ShapeDtypeStruct(g.shape, g.dtype) for g in gathered],
                    {t: t for t in range(len(gathered))}, copies)


def scatter_over_d2d(blocks):
    copies = [(SIBLING, _bind(lambda t, m, ins, outs, me: ins[t].at[me ^ SIBLING ^ m], t, m),
               _bind(lambda t, i, outs, sender: outs[t].at[i], t, i))
              for t in range(len(blocks)) for i, m in enumerate(SAME_CORE)]
    return Exchange(blocks, [jax.ShapeDtypeStruct((len(SAME_CORE),) + b.shape[1:], b.dtype) for b in blocks], {}, copies)


def scatter_over_ici(pair_sums, bufs, layer):
    n = len(pair_sums)
    copies = [(m, _bind(lambda t, i, ins, outs, me: ins[t].at[i], t, i),
               _bind(lambda t, i, outs, sender: outs[t].at[layer, i], t, i))
              for t in range(n) for i, m in enumerate(SAME_CORE) if m]
    local = [(_bind(lambda t, ins, outs, me: ins[t].at[0], t), _bind(lambda t, outs, me: outs[t].at[layer, 0], t))
             for t in range(n)]
    return Exchange(list(pair_sums) + list(bufs), [jax.ShapeDtypeStruct(b.shape, b.dtype) for b in bufs],
                    {n + t: t for t in range(n)}, copies, local)


def run_exchange(ex, name):
    any_spec = pl.BlockSpec(memory_space=pl.ANY)
    n_in, n_out = len(ex.inputs), len(ex.out_shapes)

    def body(*refs):
        ins, outs, sems = refs[:n_in], refs[n_in:n_in + n_out], refs[n_in + n_out:]
        ex.start(ins, outs, sems)
        ex.drain(ins, outs, sems)

    return pl.pallas_call(
        body, out_shape=ex.out_shapes, in_specs=[any_spec] * n_in, out_specs=[any_spec] * n_out,
        input_output_aliases=ex.aliases, scratch_shapes=ex.scratch, name=name)(*ex.inputs)


MATRICES = ("w_in", "w_attn_out", "w_conv_out", "pool_w", "w_o", "w_ffn_in", "w_ffn_out")
SHARD_INFO = {
    "w_in": ((DEPTH, D_MODEL, D_IN // N_DEV), 2),
    "w_attn_out": ((DEPTH, D_ATTN, D_MODEL // N_DEV), 2),
    "w_conv_out": ((DEPTH, D_CONV, D_MODEL // N_DEV), 2),
    "pool_w": ((DEPTH, 4, 64, 256 // N_DEV), 3),
    "w_o": ((DEPTH, D_MODEL // N_DEV, D_MODEL), 1),
    "w_ffn_in": ((DEPTH, D_MODEL, 2 * D_FF // N_DEV), 2),
    "w_ffn_out": ((DEPTH, D_FF // N_DEV, D_MODEL), 1),
}
VECTORS = ("norm_mix_g", "forget_b", "q_norm_g", "k_norm_g", "pool_scale", "norm_ffn_g")
VECTOR_SHAPES = {"norm_mix_g": (DEPTH, D_MODEL), "forget_b": (DEPTH, HEADS), "q_norm_g": (DEPTH, HEAD_DIM),
                 "k_norm_g": (DEPTH, HEAD_DIM), "pool_scale": (DEPTH, D_MODEL), "norm_ffn_g": (DEPTH, D_MODEL)}
CONV_W_FULL = (DEPTH, 3, D_CONV)


def _size(shape):
    n = 1
    for v in shape:
        n *= v
    return n


def _pack(arrays, rows, cols):
    flat = jnp.concatenate([a.reshape(-1) for a in arrays])
    return jnp.pad(flat, (0, rows * cols - flat.shape[0])).reshape(rows, cols)


def _unpack(packed, shapes):
    flat, out, off = packed.reshape(-1), [], 0
    for shp in shapes:
        out.append(flat[off:off + _size(shp)].reshape(shp))
        off += _size(shp)
    return out


def _join_shards(stacked, axis):
    moved = jnp.moveaxis(stacked, 0, axis)
    shp = list(moved.shape)
    shp[axis:axis + 2] = [shp[axis] * shp[axis + 1]]
    return moved.reshape(shp)


def _cut_shards(full, axis):
    shp = list(full.shape)
    shp[axis:axis + 1] = [N_DEV, shp[axis] // N_DEV]
    return jnp.moveaxis(full.reshape(shp), axis, 0)


def _regroup_w_in(w):
    pad = jnp.zeros((w.shape[0], N_FULL - N_MAIN - HEADS), w.dtype)
    return jnp.concatenate([w[:, 1544:2568], w[:, 2568:5640], w[:, 0:1536], w[:, 1536:1544], pad], axis=1)


def _ungroup_w_in(wp):
    return jnp.concatenate([wp[:, 4096:5632], wp[:, 5632:5640], wp[:, 0:1024], wp[:, 1024:4096]], axis=1)


def _interleave_ffn(w):
    d = w.shape[0]
    return jnp.stack([w[:, :D_FF].reshape(d, N_FF_BLKS, FF_BLK), w[:, D_FF:].reshape(d, N_FF_BLKS, FF_BLK)],
                     axis=2).reshape(d, 2 * D_FF)


def _deinterleave_ffn(wp):
    d = wp.shape[0]
    t = wp.reshape(d, N_FF_BLKS, 2, FF_BLK)
    return jnp.concatenate([t[:, :, 0].reshape(d, D_FF), t[:, :, 1].reshape(d, D_FF)], axis=1)


def _pool_block_diag(w):
    out = jnp.zeros((D_POOL, D_MODEL), w.dtype)
    for g in range(4):
        out = lax.dynamic_update_slice(out, w[g], (g * 64, g * 256))
    return out


def _pool_from_block_diag(wbd):
    return jnp.stack([wbd[g * 64:(g + 1) * 64, g * 256:(g + 1) * 256] for g in range(4)])


def _layer_weights(mats, vec, conv_w, l):
    w_in = _regroup_w_in(mats["w_in"])
    w_ffn_in = _interleave_ffn(mats["w_ffn_in"])
    wp = _pool_block_diag(mats["pool_w"])
    row = lambda v: v.reshape(1, -1)
    fb = jnp.zeros((1, 128), F32).at[0, :HEADS].set(vec["forget_b"][l])
    cw = jnp.zeros((8, D_CONV), F32).at[:3].set(conv_w[l])
    twice = lambda v: jnp.tile(v.reshape(1, -1), (1, 2))
    return dict(
        w_in=w_in, w_f=w_in[:, N_MAIN:], w_ffn_in=w_ffn_in, w_ffn_out=mats["w_ffn_out"],
        wa=mats["w_attn_out"], wc=mats["w_conv_out"], wp=wp, wo=mats["w_o"],
        g_mix=row(vec["norm_mix_g"][l]), g_ffn=row(vec["norm_ffn_g"][l]), gq2=twice(vec["q_norm_g"][l]),
        gk2=twice(vec["k_norm_g"][l]), scale=row(vec["pool_scale"][l]), fb=fb, cw=cw)


def _layer_fwd(x, w, l, comm):
    (proj, h), _ = norm_matmul(x, w["g_mix"], w["w_in"], N_MAIN, f"in_proj_{l}")
    z, c = forget_fwd(h, w["w_f"], w["fb"], f"forget_fwd_{l}")
    qa, ka, va, vt = attn_prep(proj, c, w["gq2"], w["gk2"], f"attn_prep_{l}")
    (oa, lse), half = attn_forward(qa, ka, vt, f"attn_fwd_{l}", comm.gather_ici(l + 1))
    x1 = mix_fwd(proj, oa, x, w["wa"], w["wc"], w["wp"], w["scale"], w["cw"], w["wo"], f"mix_fwd_{l}")
    (gu, h2), gathered = norm_matmul(x1, w["g_ffn"], w["w_ffn_in"], 2 * D_FF, f"ffn_in_{l}", comm.gather_d2d(l + 1, half))
    x2 = swiglu_matmul(gu, w["w_ffn_out"], x1, f"ffn_out_{l}")
    saved = dict(x=x, proj=proj, h=h, z=z, qa=qa, ka=ka, va=va, oa=oa, lse=lse, x1=x1, gu=gu, h2=h2)
    return x2, saved, gathered


def _layer_bwd(dx2, sv, w, l, comm):
    g = {}
    (dgu, act), stage = swiglu_bwd(dx2, sv["gu"], w["w_ffn_out"], f"ffn_out_bwd_{l}", comm.scatter_d2d(l + 1))
    sums = comm.pair_sums(l + 1, stage)
    g["w_ffn_out"] = tn_matmul(act, dx2, f"dw_ffn_out_{l}")
    g["w_ffn_in"] = _deinterleave_ffn(tn_matmul(sv["h2"], dgu, f"dw_ffn_in_{l}"))
    dx1, dg = matmul_normbwd(dgu, w["w_ffn_in"], sv["x1"], w["g_ffn"], dx2, f"ffn_in_bwd_{l}")
    g["norm_ffn_g"] = dg[0]

    (dproj, doa, a_tok, merged, dya, dyc, dyp, uc, dd, dscale, dcw) = mix_bwd(
        sv["proj"], sv["oa"], dx1, w["wa"], w["wc"], w["wp"], w["scale"], w["cw"], w["wo"], f"mix_bwd_{l}")
    g["w_o"] = tn_matmul(merged, dx1, f"dw_o_{l}")
    g["w_attn_out"] = tn_matmul(a_tok, dya, f"dw_attn_out_{l}")
    g["w_conv_out"] = tn_matmul(uc, dyc, f"dw_conv_out_{l}")
    g["pool_w"] = _pool_from_block_diag(tn_matmul(dd, dyp, f"dw_pool_{l}"))
    g["pool_scale"] = dscale[0]
    g["conv_w"] = dcw[:3]

    (dqa, dka, dva), got = attn_backward(sv["qa"], sv["ka"], sv["va"], sv["oa"], doa, sv["lse"], f"attn_bwd_{l}",
                                         comm.scatter_ici(l + 1, sums))
    comm.scattered(got)
    dproj, dc, dgq, dgk = attn_post(dqa, dka, dva, sv["proj"], w["gq2"], w["gk2"], dproj, f"attn_post_{l}")
    g["q_norm_g"] = dgq[0, :HEAD_DIM] + dgq[0, HEAD_DIM:]
    g["k_norm_g"] = dgk[0, :HEAD_DIM] + dgk[0, HEAD_DIM:]
    dproj, db = forget_bwd(dc, sv["z"], dproj, f"forget_bwd_{l}")
    g["forget_b"] = db[0, :HEADS]

    g["w_in"] = _ungroup_w_in(tn_matmul(sv["h"], dproj, f"dw_in_{l}", n_cols=N_FULL))
    dx, dg = matmul_normbwd(dproj, w["w_in"], sv["x"], w["g_mix"], dx1, f"in_proj_bwd_{l}", k=N_FULL)
    g["norm_mix_g"] = dg[0]
    comm.grads(l, g)
    return dx


def _local_step(x, tgt, comm):
    ws, saved = [], []
    w = comm.weights(0, None)
    for l in range(DEPTH):
        ws.append(w)
        x, sv, gathered = _layer_fwd(x, w, l, comm)
        saved.append(sv)
        if l + 1 < DEPTH:
            w = comm.weights(l + 1, gathered)
    sq, dx = loss_kernel(x, tgt, "loss")
    for l in reversed(range(DEPTH)):
        dx = _layer_bwd(dx, saved[l], ws[l], l, comm)
    comm.finish()
    return sq[0, 0], dx


def kernel(x, norm_mix_g, w_in, forget_b, q_norm_g, k_norm_g, w_attn_out, conv_w, w_conv_out, pool_w, pool_scale, w_o, norm_ffn_g, w_ffn_in, w_ffn_out, loss_target, m_norm_mix_g, m_w_in, m_forget_b, m_q_norm_g, m_k_norm_g, m_w_attn_out, m_conv_w, m_w_conv_out, m_pool_w, m_pool_scale, m_w_o, m_norm_ffn_g, m_w_ffn_in, m_w_ffn_out, v_norm_mix_g, v_w_in, v_forget_b, v_q_norm_g, v_k_norm_g, v_w_attn_out, v_conv_w, v_w_conv_out, v_pool_w, v_pool_scale, v_w_o, v_norm_ffn_g, v_w_ffn_in, v_w_ffn_out):
    w = dict(norm_mix_g=norm_mix_g, w_in=w_in, forget_b=forget_b, q_norm_g=q_norm_g, k_norm_g=k_norm_g,
             w_attn_out=w_attn_out, conv_w=conv_w, w_conv_out=w_conv_out, pool_w=pool_w, pool_scale=pool_scale,
             w_o=w_o, norm_ffn_g=norm_ffn_g, w_ffn_in=w_ffn_in, w_ffn_out=w_ffn_out)
    m = dict(norm_mix_g=m_norm_mix_g, w_in=m_w_in, forget_b=m_forget_b, q_norm_g=m_q_norm_g, k_norm_g=m_k_norm_g,
             w_attn_out=m_w_attn_out, conv_w=m_conv_w, w_conv_out=m_w_conv_out, pool_w=m_pool_w,
             pool_scale=m_pool_scale, w_o=m_w_o, norm_ffn_g=m_norm_ffn_g, w_ffn_in=m_w_ffn_in, w_ffn_out=m_w_ffn_out)
    v = dict(norm_mix_g=v_norm_mix_g, w_in=v_w_in, forget_b=v_forget_b, q_norm_g=v_q_norm_g, k_norm_g=v_k_norm_g,
             w_attn_out=v_w_attn_out, conv_w=v_conv_w, w_conv_out=v_w_conv_out, pool_w=v_pool_w,
             pool_scale=v_pool_scale, w_o=v_w_o, norm_ffn_g=v_norm_ffn_g, w_ffn_in=v_w_ffn_in, w_ffn_out=v_w_ffn_out)
    me = 4 * lax.axis_index("x") + 2 * lax.axis_index("y") + lax.axis_index("c")
    layer_shard = {n: SHARD_INFO[n][0][1:] for n in MATRICES}
    cut_axis = {n: SHARD_INFO[n][1] - 1 for n in MATRICES}

    conv_g = all_gather([_pack([conv_w], 8, 128)], "gather_conv_w")[0]
    conv_full = _join_shards(jnp.stack([_unpack(conv_g[i], [conv_w.shape])[0] for i in range(N_DEV)]), 2)
    vec = {n: w[n] for n in VECTORS}

    rc = {n: (_size(layer_shard[n][:-1]), layer_shard[n][-1]) for n in MATRICES}

    class Comm:
        bufs = [lax.empty((DEPTH, len(SAME_CORE)) + layer_shard[n], BF16) for n in MATRICES]
        blocks = [None] * DEPTH
        small_g = [None] * DEPTH

        @staticmethod
        def shards(l):
            return [w[n][l].astype(BF16) for n in MATRICES]

        @staticmethod
        def gather_ici(l):
            return gather_over_ici(Comm.shards(l)) if l < DEPTH else None

        @staticmethod
        def gather_d2d(l, half):
            return gather_over_d2d(half) if l < DEPTH else None

        @staticmethod
        def weights(l, gathered):
            if l == 0:
                gathered = all_gather(Comm.shards(0), "gather_0")
            mats = {n: _join_shards(t, cut_axis[n]) for n, t in zip(MATRICES, gathered)}
            return _layer_weights(mats, vec, conv_full, l)

        @staticmethod
        def grads(l, g):
            Comm.small_g[l] = g
            Comm.blocks[l] = [_cut_shards(g[n], cut_axis[n]) for n in MATRICES]

        @staticmethod
        def scatter_d2d(l):
            return scatter_over_d2d(Comm.blocks[l]) if l < DEPTH else None

        @staticmethod
        def pair_sums(l, stage):
            if l >= DEPTH:
                return None
            return [pair_sum(b.reshape((N_DEV,) + rc[n]), s.reshape((len(SAME_CORE),) + rc[n]), me,
                             f"pair_sum_{n}_{l}").reshape(s.shape) for n, b, s in zip(MATRICES, Comm.blocks[l], stage)]

        @staticmethod
        def scatter_ici(l, sums):
            return scatter_over_ici(sums, Comm.bufs, l) if l < DEPTH else None

        @staticmethod
        def scattered(results):
            if results:
                Comm.bufs = list(results)

        @staticmethod
        def finish():
            stage = run_exchange(Comm.scatter_d2d(0), "scatter_d2d_0")
            Comm.scattered(run_exchange(Comm.scatter_ici(0, Comm.pair_sums(0, stage)), "scatter_ici_0"))

    small_g, received = Comm.small_g, Comm
    sq, dx = _local_step(x[0], loss_target[0], Comm)
    loss = lax.psum(0.5 * sq / D_MODEL, ("x", "y", "c"))

    big = {}
    for n, parts in zip(MATRICES, received.bufs):
        outs = adamw_sum(parts.reshape((DEPTH, len(SAME_CORE)) + rc[n]),
                         *[d[n].reshape((DEPTH,) + rc[n]) for d in (w, m, v)], f"adamw_{n}")
        big[n] = [t.reshape(w[n].shape) for t in outs]

    small_shapes = [VECTOR_SHAPES[n] for n in VECTORS] + [CONV_W_FULL]
    stacked = [jnp.stack([small_g[l][n] for l in range(DEPTH)]) for n in VECTORS + ("conv_w",)]
    sparts = all_gather([_pack(stacked, SMALL_ROWS, 128)], "gather_vector_grads")[0]
    col0 = me * (D_CONV // N_DEV)
    place = lambda t: lax.dynamic_update_slice(jnp.zeros(CONV_W_FULL, F32), t, (0, 0, col0))
    spacked = [_pack([d[n] for n in VECTORS] + [place(d["conv_w"])], SMALL_ROWS, 128)[None] for d in (w, m, v)]
    small = [_unpack(t[0], small_shapes) for t in adamw_sum(sparts[None], *spacked, "adamw_vectors")]

    def result(kind):
        out = {n: big[n][kind] for n in MATRICES}
        out.update({n: small[kind][j] for j, n in enumerate(VECTORS)})
        out["conv_w"] = lax.dynamic_slice(small[kind][len(VECTORS)], (0, 0, col0), conv_w.shape)
        return [out[n] for n in w]

    return (loss, dx[None], *result(0), *result(1), *result(2), *result(3))
```

```python
import functools

import jax
import jax.numpy as jnp
from jax import lax
from jax.experimental import pallas as pl
from jax.experimental.pallas import tpu as pltpu

F32 = jnp.float32
BF16 = jnp.bfloat16

N_DEV = 8
DEPTH = 4
D_MODEL = 1024
HEAD_DIM = 64
HEADS = 8
D_ATTN = 512
D_CONV = 256
D_POOL = 256
D_FF = 2816
D_IN = 5640
EPS = 1e-6
ATTN_SCALE = HEAD_DIM ** -0.5

N_REST = 4096
N_MAIN = 5632
N_FULL = 5760
DPROJ_TAIL = 2048
DPROJ_COLS = N_REST + DPROJ_TAIL
FF_BLK = 256
N_FF_BLKS = D_FF // FF_BLK
HALO = 16

ADAM_LR = 0.001
ADAM_B1 = 0.9
ADAM_B2 = 0.999
ADAM_EPS = 1e-08
ADAM_WD = 0.01
ADAM_STEP = 10

PACK_COLS = 1024
PACK_ROWS = 8192
SMALL_ROWS = 128

VMEM_LIMIT = 48 * 2 ** 20


def _cparams(sem, vmem=None):
    return pltpu.CompilerParams(dimension_semantics=sem, vmem_limit_bytes=vmem or VMEM_LIMIT)


def _pick(n, cands):
    for c in cands:
        if n % c == 0:
            return c
    raise ValueError(f"no tile for {n}")


def _sigmoid(v):
    return 1.0 / (1.0 + jnp.exp(-v))


def _rstd(v):
    return lax.rsqrt(jnp.mean(v * v, axis=-1, keepdims=True) + EPS)


def _dot(a, b):
    return jnp.dot(a, b, preferred_element_type=F32)


def _dot_tn(a, b):
    return lax.dot_general(a, b, (((0,), (0,)), ((), ())), preferred_element_type=F32)


def _dot_nt(a, b):
    return lax.dot_general(a, b, (((1,), (1,)), ((), ())), preferred_element_type=F32)


def norm_matmul(x, g, wt, n_cols, name, ex=None):
    s, d = x.shape
    tm, tn = min(512, s), _pick(n_cols, (1408, 512))

    def body(x_ref, g_ref, w_ref, o_ref, h_ref):
        @pl.when(pl.program_id(1) == 0)
        def _():
            xv = x_ref[...]
            h_ref[...] = (xv * _rstd(xv) * g_ref[...]).astype(BF16)

        o_ref[...] = _dot_nt(h_ref[...], w_ref[...]).astype(BF16)

    return _carried_call(
        body, ex, (s // tm, n_cols // tn),
        [pl.BlockSpec((tm, d), lambda i, j: (i, 0)), pl.BlockSpec((1, d), lambda i, j: (0, 0)),
         pl.BlockSpec((tn, d), lambda i, j: (j, 0))],
        [pl.BlockSpec((tm, tn), lambda i, j: (i, j)), pl.BlockSpec((tm, d), lambda i, j: (i, 0))],
        [jax.ShapeDtypeStruct((s, n_cols), BF16), jax.ShapeDtypeStruct((s, d), BF16)], [],
        ("arbitrary", "arbitrary"), name, (x, g, wt))


def tn_matmul(a, b, name, m_cols=None):
    t = a.shape[0]
    m = m_cols or a.shape[1]
    n = b.shape[1]
    tk = min(512, t)
    tmm = _pick(m, (1408, 1152, 1024, 512, 256))
    tn = _pick(n, (1408, 1152, 1024, 512, 128))
    nk = t // tk

    def body(a_ref, b_ref, o_ref, acc_ref):
        @pl.when(pl.program_id(2) == 0)
        def _():
            acc_ref[...] = jnp.zeros_like(acc_ref)

        acc_ref[...] += _dot_tn(a_ref[...].astype(BF16), b_ref[...].astype(BF16))

        @pl.when(pl.program_id(2) == nk - 1)
        def _():
            o_ref[...] = acc_ref[...].astype(BF16)

    return pl.pallas_call(
        body, grid=(m // tmm, n // tn, nk),
        in_specs=[pl.BlockSpec((tk, tmm), lambda i, j, k: (k, i)), pl.BlockSpec((tk, tn), lambda i, j, k: (k, j))],
        out_specs=pl.BlockSpec((tmm, tn), lambda i, j, k: (i, j)),
        out_shape=jax.ShapeDtypeStruct((m, n), BF16), scratch_shapes=[pltpu.VMEM((tmm, tn), F32)],
        compiler_params=_cparams(("parallel", "parallel", "arbitrary")), name=name)(a, b)


def matmul_normbwd(a, wt, x, g, dres, name, k=None):
    s = a.shape[0]
    k = k or a.shape[1]
    d = wt.shape[1]
    tm = min(512, s)
    tk = _pick(k, (1408, 1152, 512))
    nk = k // tk

    def body(a_ref, w_ref, x_ref, g_ref, r_ref, dx_ref, dg_ref, acc_ref):
        i, kk = pl.program_id(0), pl.program_id(1)

        @pl.when(kk == 0)
        def _():
            acc_ref[...] = jnp.zeros_like(acc_ref)

        @pl.when((i == 0) & (kk == 0))
        def _():
            dg_ref[...] = jnp.zeros_like(dg_ref)

        acc_ref[...] += _dot(a_ref[...], w_ref[...])

        @pl.when(kk == nk - 1)
        def _():
            xv = x_ref[...]
            r = _rstd(xv)
            y = xv * r
            dh = acc_ref[...]
            dy = dh * g_ref[...]
            dx_ref[...] = r_ref[...] + r * (dy - y * jnp.mean(dy * y, axis=-1, keepdims=True))
            dg_ref[...] += jnp.sum(dh * y, axis=0, keepdims=True)

    return pl.pallas_call(
        body, grid=(s // tm, nk),
        in_specs=[pl.BlockSpec((tm, tk), lambda i, kk: (i, kk)), pl.BlockSpec((tk, d), lambda i, kk: (kk, 0)),
                  pl.BlockSpec((tm, d), lambda i, kk: (i, 0)), pl.BlockSpec((1, d), lambda i, kk: (0, 0)),
                  pl.BlockSpec((tm, d), lambda i, kk: (i, 0))],
        out_specs=[pl.BlockSpec((tm, d), lambda i, kk: (i, 0)), pl.BlockSpec((1, d), lambda i, kk: (0, 0))],
        out_shape=[jax.ShapeDtypeStruct((s, d), F32), jax.ShapeDtypeStruct((1, d), F32)],
        scratch_shapes=[pltpu.VMEM((tm, d), F32)],
        compiler_params=_cparams(("arbitrary", "arbitrary")), name=name)(a, wt, x, g, dres)


def swiglu_matmul(gu, w, x1, name):
    s = gu.shape[0]
    d = w.shape[1]
    tm = min(512, s)

    def body(gu_ref, w_ref, x_ref, o_ref):
        acc = x_ref[...]
        for j in range(N_FF_BLKS):
            gt = gu_ref[:, j * FF_BLK:(j + 1) * FF_BLK].astype(F32)
            up = gu_ref[:, D_FF + j * FF_BLK:D_FF + (j + 1) * FF_BLK].astype(F32)
            act = (gt * _sigmoid(gt) * up).astype(BF16)
            acc += _dot(act, w_ref[j * FF_BLK:(j + 1) * FF_BLK, :])
        o_ref[...] = acc

    return pl.pallas_call(
        body, grid=(s // tm,),
        in_specs=[pl.BlockSpec((tm, 2 * D_FF), lambda i: (i, 0)), pl.BlockSpec((D_FF, d), lambda i: (0, 0)),
                  pl.BlockSpec((tm, d), lambda i: (i, 0))],
        out_specs=pl.BlockSpec((tm, d), lambda i: (i, 0)),
        out_shape=jax.ShapeDtypeStruct((s, d), F32),
        compiler_params=_cparams(("parallel",)), name=name)(gu, w, x1)


def swiglu_bwd(dx2, gu, w, name, ex=None):
    s, d = dx2.shape
    tm = min(256, s)

    def body(dx_ref, gu_ref, w_ref, dgu_ref, act_ref):
        dx = dx_ref[...].astype(BF16)
        for j in range(N_FF_BLKS):
            g_cols = slice(j * FF_BLK, (j + 1) * FF_BLK)
            u_cols = slice(D_FF + j * FF_BLK, D_FF + (j + 1) * FF_BLK)
            dact = _dot_nt(dx, w_ref[j * FF_BLK:(j + 1) * FF_BLK, :])
            gt = gu_ref[:, g_cols].astype(F32)
            up = gu_ref[:, u_cols].astype(F32)
            sg = _sigmoid(gt)
            act_ref[:, j * FF_BLK:(j + 1) * FF_BLK] = (gt * sg * up).astype(BF16)
            dgu_ref[:, g_cols] = (dact * up * (sg * (1.0 + gt * (1.0 - sg)))).astype(BF16)
            dgu_ref[:, u_cols] = (dact * gt * sg).astype(BF16)

    return _carried_call(
        body, ex, (s // tm,),
        [pl.BlockSpec((tm, d), lambda i: (i, 0)), pl.BlockSpec((tm, 2 * D_FF), lambda i: (i, 0)),
         pl.BlockSpec((D_FF, d), lambda i: (0, 0))],
        [pl.BlockSpec((tm, 2 * D_FF), lambda i: (i, 0)), pl.BlockSpec((tm, D_FF), lambda i: (i, 0))],
        [jax.ShapeDtypeStruct((s, 2 * D_FF), BF16), jax.ShapeDtypeStruct((s, D_FF), BF16)], [],
        ("arbitrary",), name, (dx2, gu, w))


def loss_kernel(y, tgt, name):
    s, d = y.shape
    tm = min(512, s)

    def body(y_ref, t_ref, l_ref, dy_ref):
        @pl.when(pl.program_id(0) == 0)
        def _():
            l_ref[...] = jnp.zeros_like(l_ref)

        err = y_ref[...] - t_ref[...]
        dy_ref[...] = err * (1.0 / d)
        l_ref[...] += jnp.sum(jnp.sum(err * err, axis=1, keepdims=True), axis=0, keepdims=True)

    return pl.pallas_call(
        body, grid=(s // tm,),
        in_specs=[pl.BlockSpec((tm, d), lambda i: (i, 0)), pl.BlockSpec((tm, d), lambda i: (i, 0))],
        out_specs=[pl.BlockSpec((8, 128), lambda i: (0, 0)), pl.BlockSpec((tm, d), lambda i: (i, 0))],
        out_shape=[jax.ShapeDtypeStruct((8, 128), F32), jax.ShapeDtypeStruct((s, d), F32)],
        compiler_params=_cparams(("arbitrary",)), name=name)(y, tgt)


def _split3(v):
    a1 = v.astype(BF16)
    r1 = v - a1.astype(F32)
    a2 = r1.astype(BF16)
    a3 = (r1 - a2.astype(F32)).astype(BF16)
    return a1, a2, a3


def forget_fwd(h, wt_in, b, name):
    s, d = h.shape
    tm = min(512, s)

    def body(h_ref, w_ref, b_ref, z_ref, c_ref, carry_ref):
        @pl.when(pl.program_id(0) == 0)
        def _():
            carry_ref[...] = jnp.zeros_like(carry_ref)

        z = _dot_nt(h_ref[...], w_ref[...]) + b_ref[...]
        z_ref[...] = z
        logf = jnp.minimum(z, 0.0) - jnp.log(1.0 + jnp.exp(-jnp.abs(z)))
        row = lax.broadcasted_iota(jnp.int32, (tm, tm), 0)
        col = lax.broadcasted_iota(jnp.int32, (tm, tm), 1)
        tri = (row >= col).astype(BF16)
        a1, a2, a3 = _split3(logf)
        c = _dot(tri, a1) + _dot(tri, a2) + _dot(tri, a3) + carry_ref[...]
        c_ref[...] = c
        carry_ref[...] = c[tm - 1:tm, :]

    return pl.pallas_call(
        body, grid=(s // tm,),
        in_specs=[pl.BlockSpec((tm, d), lambda i: (i, 0)), pl.BlockSpec((128, d), lambda i: (N_MAIN // 128, 0)),
                  pl.BlockSpec((1, 128), lambda i: (0, 0))],
        out_specs=[pl.BlockSpec((tm, 128), lambda i: (i, 0)), pl.BlockSpec((tm, 128), lambda i: (i, 0))],
        out_shape=[jax.ShapeDtypeStruct((s, 128), F32), jax.ShapeDtypeStruct((s, 128), F32)],
        scratch_shapes=[pltpu.VMEM((1, 128), F32)],
        compiler_params=_cparams(("arbitrary",)), name=name)(h, wt_in, b)


def forget_bwd(dc, z, dproj, name):
    s = dc.shape[0]
    tm = min(512, s)
    nt = s // tm

    def body(dc_ref, z_ref, dp_ref, dz_ref, db_ref, carry_ref):
        @pl.when(pl.program_id(0) == 0)
        def _():
            carry_ref[...] = jnp.zeros_like(carry_ref)
            db_ref[...] = jnp.zeros_like(db_ref)

        row = lax.broadcasted_iota(jnp.int32, (tm, tm), 0)
        col = lax.broadcasted_iota(jnp.int32, (tm, tm), 1)
        tri = (col >= row).astype(BF16)
        a1, a2, a3 = _split3(dc_ref[...])
        dlogf = _dot(tri, a1) + _dot(tri, a2) + _dot(tri, a3) + carry_ref[...]
        carry_ref[...] = dlogf[0:1, :]
        dz = dlogf * (1.0 - _sigmoid(z_ref[...]))
        dz_ref[...] = dz.astype(BF16)
        db_ref[...] += jnp.sum(dz, axis=0, keepdims=True)

    return pl.pallas_call(
        body, grid=(nt,),
        in_specs=[pl.BlockSpec((tm, 128), lambda i: (nt - 1 - i, 0)), pl.BlockSpec((tm, 128), lambda i: (nt - 1 - i, 0)),
                  pl.BlockSpec(memory_space=pl.ANY)],
        out_specs=[pl.BlockSpec((tm, 128), lambda i: (nt - 1 - i, N_MAIN // 128)), pl.BlockSpec((1, 128), lambda i: (0, 0))],
        out_shape=[jax.ShapeDtypeStruct(dproj.shape, BF16), jax.ShapeDtypeStruct((1, 128), F32)],
        scratch_shapes=[pltpu.VMEM((1, 128), F32)], input_output_aliases={2: 0},
        compiler_params=_cparams(("arbitrary",)), name=name)(dc, z, dproj)


HEAD_GROUP = 2
LANE_C = 64
LANE_ONE = 67


def _lanes():
    lane = lax.broadcasted_iota(jnp.int32, (1, 128), 1)
    return lane, lane < HEAD_DIM


def _half_mean(t, lo):
    s_lo = jnp.sum(jnp.where(lo, t, 0.0), axis=-1, keepdims=True)
    s_hi = jnp.sum(jnp.where(lo, 0.0, t), axis=-1, keepdims=True)
    return jnp.where(lo, s_lo, s_hi) * (1.0 / HEAD_DIM)


def _lane_col(t, lane, idx):
    return jnp.sum(jnp.where(lane == idx, t, 0.0), axis=-1, keepdims=True)


def _swap_halves(t):
    return pltpu.roll(t, HEAD_DIM, 1)


def _causal(s_blk, tq, tk):
    row = lax.broadcasted_iota(jnp.int32, (tq, tk), 0)
    col = lax.broadcasted_iota(jnp.int32, (tq, tk), 1)
    return jnp.where(row >= col, s_blk, -jnp.inf)


def attn_prep(proj, c, gq2, gk2, name):
    s = proj.shape[0]
    tm = min(512, s)
    first = N_REST // 128

    def body(q_ref, k_ref, v_ref, c_ref, gq_ref, gk_ref, qa_ref, ka_ref, va_ref, vt_ref):
        j = pl.program_id(1)
        lane, lo = _lanes()

        def normed(ref, g):
            t = ref[...].astype(F32)
            return t * lax.rsqrt(_half_mean(t * t, lo) + EPS) * g

        qn = normed(q_ref, gq_ref[...] * ATTN_SCALE)
        kn = normed(k_ref, gk_ref[...])
        vv = v_ref[...].astype(F32)
        cv = c_ref[...]
        one_q = jnp.where((lane >= LANE_ONE) & (lane < LANE_ONE + 3), 1.0, 0.0)
        one_k = jnp.where((lane >= LANE_C) & (lane < LANE_C + 3), 1.0, 0.0)
        one_v = jnp.where(lane == LANE_C, 1.0, 0.0)
        for e in range(2):
            pick = (lambda t: t) if e == 0 else _swap_halves
            pieces = [p.astype(F32) for p in _split3(_lane_col(cv, lane, 2 * j + e))]
            ext_q, ext_k = one_q, one_k
            for i, p in enumerate(pieces):
                ext_q = jnp.where(lane == LANE_C + i, p, ext_q)
                ext_k = jnp.where(lane == LANE_ONE + i, -p, ext_k)
            qa_ref[e] = jnp.where(lo, pick(qn), ext_q).astype(BF16)
            ka_ref[e] = jnp.where(lo, pick(kn), ext_k).astype(BF16)
            va = jnp.where(lo, pick(vv), one_v)
            va_ref[e] = va.astype(BF16)
            vt_ref[e] = va.T.astype(BF16)

    tile = lambda base: pl.BlockSpec((tm, 128), lambda i, j: (i, base + j))
    vec = pl.BlockSpec((1, 128), lambda i, j: (0, 0))
    out = pl.BlockSpec((2, tm, 128), lambda i, j: (j, i, 0))
    return pl.pallas_call(
        body, grid=(s // tm, HEADS // 2),
        in_specs=[tile(first), tile(first + 4), tile(first + 8), pl.BlockSpec((tm, 128), lambda i, j: (i, 0)), vec, vec],
        out_specs=[out, out, out, pl.BlockSpec((2, 128, tm), lambda i, j: (j, 0, i))],
        out_shape=[jax.ShapeDtypeStruct((HEADS, s, 128), BF16)] * 3 + [jax.ShapeDtypeStruct((HEADS, 128, s), BF16)],
        compiler_params=_cparams(("parallel", "arbitrary")), name=name)(proj, proj, proj, c, gq2, gk2)


def attn_fwd(q, k, v, ccol, crow, gq, gk, name):
    hh, s, hd = q.shape
    tq = tk = min(512, s)
    nq = s // tq

    def body(q_ref, k_ref, v_ref, cc_ref, cr_ref, gq_ref, gk_ref, o_ref, lse_ref, qn_ref, m_ref, l_ref, acc_ref):
        qi, ki = pl.program_id(1), pl.program_id(2)

        @pl.when(ki == 0)
        def _():
            qn_ref[...] = _qk_hat(q_ref, gq_ref, ATTN_SCALE)
            m_ref[...] = jnp.full_like(m_ref, -jnp.inf)
            l_ref[...] = jnp.zeros_like(l_ref)
            acc_ref[...] = jnp.zeros_like(acc_ref)

        @pl.when(ki <= qi)
        def _():
            kn = _qk_hat(k_ref, gk_ref, 1.0)
            sb = _dot_nt(qn_ref[...], kn) + (cc_ref[...] - cr_ref[...])
            sb = _causal(sb, qi, ki, tq, tk)
            m_new = jnp.maximum(m_ref[...], jnp.max(sb, axis=-1, keepdims=True))
            alpha = jnp.exp(m_ref[...] - m_new)
            p = jnp.exp(sb - m_new)
            l_ref[...] = alpha * l_ref[...] + jnp.sum(p, axis=-1, keepdims=True)
            acc_ref[...] = alpha * acc_ref[...] + _dot(p.astype(BF16), v_ref[...])
            m_ref[...] = m_new

        @pl.when(ki == qi)
        def _():
            o_ref[...] = (acc_ref[...] / l_ref[...]).astype(BF16)
            lse_ref[...] = m_ref[...] + jnp.log(l_ref[...])

    qspec = pl.BlockSpec((None, tq, hd), lambda h, i, j: (h, i, 0))
    kspec = pl.BlockSpec((None, tk, hd), lambda h, i, j: (h, jnp.minimum(i, j), 0))
    gspec = pl.BlockSpec((1, hd), lambda h, i, j: (0, 0))
    return pl.pallas_call(
        body, grid=(hh, nq, nq),
        in_specs=[qspec, kspec, kspec,
                  pl.BlockSpec((None, tq, 1), lambda h, i, j: (h, i, 0)),
                  pl.BlockSpec((None, 1, tk), lambda h, i, j: (h, 0, jnp.minimum(i, j))), gspec, gspec],
        out_specs=[qspec, pl.BlockSpec((None, tq, 1), lambda h, i, j: (h, i, 0))],
        out_shape=[jax.ShapeDtypeStruct((hh, s, hd), BF16), jax.ShapeDtypeStruct((hh, s, 1), F32)],
        scratch_shapes=[pltpu.VMEM((tq, hd), BF16), pltpu.VMEM((tq, 1), F32), pltpu.VMEM((tq, 1), F32),
                        pltpu.VMEM((tq, hd), F32)],
        compiler_params=_cparams(("parallel", "parallel", "arbitrary")), name=name)(q, k, v, ccol, crow, gq, gk)


def attn_bwd_dq(q, k, v, o, do, lse, ccol, crow, gq, gk, name):
    hh, s, hd = q.shape
    tq = tk = min(512, s)
    nq = s // tq

    def body(q_ref, k_ref, v_ref, o_ref, do_ref, lse_ref, cc_ref, cr_ref, gq_ref, gk_ref,
             dq_ref, dcc_ref, dg_ref, qn_ref, dl_ref, acc_ref, dca_ref):
        h, qi, ki = pl.program_id(0), pl.program_id(1), pl.program_id(2)

        @pl.when((h == 0) & (qi == 0) & (ki == 0))
        def _():
            dg_ref[...] = jnp.zeros_like(dg_ref)

        @pl.when(ki == 0)
        def _():
            qn_ref[...] = _qk_hat(q_ref, gq_ref, ATTN_SCALE)
            dl_ref[...] = jnp.sum(do_ref[...].astype(F32) * o_ref[...].astype(F32), axis=-1, keepdims=True)
            acc_ref[...] = jnp.zeros_like(acc_ref)
            dca_ref[...] = jnp.zeros_like(dca_ref)

        @pl.when(ki <= qi)
        def _():
            kn = _qk_hat(k_ref, gk_ref, 1.0)
            sb = _dot_nt(qn_ref[...], kn) + (cc_ref[...] - cr_ref[...])
            p = jnp.exp(_causal(sb, qi, ki, tq, tk) - lse_ref[...])
            dp = _dot_nt(do_ref[...], v_ref[...])
            ds = p * (dp - dl_ref[...])
            acc_ref[...] += _dot(ds.astype(BF16), kn)
            dca_ref[...] += jnp.sum(ds, axis=-1, keepdims=True)

        @pl.when(ki == qi)
        def _():
            dq, dg = _norm_bwd(q_ref[...].astype(F32), gq_ref[...], acc_ref[...], ATTN_SCALE)
            dq_ref[...] = dq.astype(BF16)
            dcc_ref[...] = dca_ref[...]
            dg_ref[...] += dg

    qspec = pl.BlockSpec((None, tq, hd), lambda h, i, j: (h, i, 0))
    kspec = pl.BlockSpec((None, tk, hd), lambda h, i, j: (h, jnp.minimum(i, j), 0))
    cspec = pl.BlockSpec((None, tq, 1), lambda h, i, j: (h, i, 0))
    gspec = pl.BlockSpec((1, hd), lambda h, i, j: (0, 0))
    return pl.pallas_call(
        body, grid=(hh, nq, nq),
        in_specs=[qspec, kspec, kspec, qspec, qspec, cspec, cspec,
                  pl.BlockSpec((None, 1, tk), lambda h, i, j: (h, 0, jnp.minimum(i, j))), gspec, gspec],
        out_specs=[qspec, cspec, gspec],
        out_shape=[jax.ShapeDtypeStruct((hh, s, hd), BF16), jax.ShapeDtypeStruct((hh, s, 1), F32),
                   jax.ShapeDtypeStruct((1, hd), F32)],
        scratch_shapes=[pltpu.VMEM((tq, hd), BF16), pltpu.VMEM((tq, 1), F32), pltpu.VMEM((tq, hd), F32),
                        pltpu.VMEM((tq, 1), F32)],
        compiler_params=_cparams(("arbitrary", "arbitrary", "arbitrary")), name=name)(
            q, k, v, o, do, lse, ccol, crow, gq, gk)


def attn_bwd_dkv(q, k, v, o, do, lse, ccol, crow, gq, gk, name):
    hh, s, hd = q.shape
    tq = tk = min(512, s)
    nq = s // tq

    def body(q_ref, k_ref, v_ref, o_ref, do_ref, lse_ref, cc_ref, cr_ref, gq_ref, gk_ref,
             dk_ref, dv_ref, dcr_ref, dg_ref, kn_ref, dka_ref, dva_ref, dca_ref):
        h, ki, qi = pl.program_id(0), pl.program_id(1), pl.program_id(2)

        @pl.when((h == 0) & (ki == 0) & (qi == 0))
        def _():
            dg_ref[...] = jnp.zeros_like(dg_ref)

        @pl.when(qi == 0)
        def _():
            kn_ref[...] = _qk_hat(k_ref, gk_ref, 1.0)
            dka_ref[...] = jnp.zeros_like(dka_ref)
            dva_ref[...] = jnp.zeros_like(dva_ref)
            dca_ref[...] = jnp.zeros_like(dca_ref)

        @pl.when(qi >= ki)
        def _():
            qn = _qk_hat(q_ref, gq_ref, ATTN_SCALE)
            do = do_ref[...]
            delta = jnp.sum(do.astype(F32) * o_ref[...].astype(F32), axis=-1, keepdims=True)
            sb = _dot_nt(qn, kn_ref[...]) + (cc_ref[...] - cr_ref[...])
            p = jnp.exp(_causal(sb, qi, ki, tq, tk) - lse_ref[...])
            dva_ref[...] += _dot_tn(p.astype(BF16), do)
            ds = p * (_dot_nt(do, v_ref[...]) - delta)
            dka_ref[...] += _dot_tn(ds.astype(BF16), qn)
            dca_ref[...] += jnp.sum(ds, axis=0, keepdims=True)

        @pl.when(qi == nq - 1)
        def _():
            dk, dg = _norm_bwd(k_ref[...].astype(F32), gk_ref[...], dka_ref[...], 1.0)
            dk_ref[...] = dk.astype(BF16)
            dv_ref[...] = dva_ref[...].astype(BF16)
            dcr_ref[...] = dca_ref[...]
            dg_ref[...] += dg

    kspec = pl.BlockSpec((None, tk, hd), lambda h, j, i: (h, j, 0))
    qspec = pl.BlockSpec((None, tq, hd), lambda h, j, i: (h, jnp.maximum(i, j), 0))
    cspec = pl.BlockSpec((None, tq, 1), lambda h, j, i: (h, jnp.maximum(i, j), 0))
    rspec = pl.BlockSpec((None, 1, tk), lambda h, j, i: (h, 0, j))
    gspec = pl.BlockSpec((1, hd), lambda h, j, i: (0, 0))
    return pl.pallas_call(
        body, grid=(hh, nq, nq),
        in_specs=[qspec, kspec, kspec, qspec, qspec, cspec, cspec, rspec, gspec, gspec],
        out_specs=[kspec, kspec, rspec, gspec],
        out_shape=[jax.ShapeDtypeStruct((hh, s, hd), BF16), jax.ShapeDtypeStruct((hh, s, hd), BF16),
                   jax.ShapeDtypeStruct((hh, 1, s), F32), jax.ShapeDtypeStruct((1, hd), F32)],
        scratch_shapes=[pltpu.VMEM((tk, hd), BF16), pltpu.VMEM((tk, hd), F32), pltpu.VMEM((tk, hd), F32),
                        pltpu.VMEM((1, tk), F32)],
        compiler_params=_cparams(("arbitrary", "arbitrary", "arbitrary")), name=name)(
            q, k, v, o, do, lse, ccol, crow, gq, gk)


def _carry(ex, n_in, n_out, n_scratch, grid):
    n_xin, n_xout = (len(ex.inputs), len(ex.out_shapes)) if ex else (0, 0)

    def split(refs):
        ins, xins = refs[:n_in], refs[n_in:n_in + n_xin]
        rest = refs[n_in + n_xin:]
        outs, xouts = rest[:n_out], rest[n_out:n_out + n_xout]
        rest = rest[n_out + n_xout:]
        return ins + outs + rest[:n_scratch], (xins, xouts, rest[n_scratch:])

    def first():
        return functools.reduce(lambda a, b: a & b, [pl.program_id(d) == 0 for d in range(len(grid))])

    def last():
        return functools.reduce(lambda a, b: a & b, [pl.program_id(d) == grid[d] - 1 for d in range(len(grid))])

    return split, first, last


def _carried_call(body, ex, grid, in_specs, out_specs, out_shape, scratch, sem, name, operands):
    any_spec = pl.BlockSpec(memory_space=pl.ANY)
    split, first, last = _carry(ex, len(in_specs), len(out_specs), len(scratch), grid)

    def carried(*refs):
        own, xrefs = split(refs)
        if ex:
            @pl.when(first())
            def _():
                ex.start(*xrefs)

        body(*own)
        if ex:
            @pl.when(last())
            def _():
                ex.drain(*xrefs)

    n_xin = len(ex.inputs) if ex else 0
    results = pl.pallas_call(
        carried, grid=grid, in_specs=list(in_specs) + [any_spec] * n_xin,
        out_specs=list(out_specs) + [any_spec] * (len(ex.out_shapes) if ex else 0),
        out_shape=list(out_shape) + (list(ex.out_shapes) if ex else []),
        input_output_aliases={len(in_specs) + i: len(out_specs) + o for i, o in ex.aliases.items()} if ex else {},
        scratch_shapes=list(scratch) + (ex.scratch if ex else []),
        compiler_params=_cparams(sem), name=name)(*operands, *(ex.inputs if ex else []))
    return results[:len(out_specs)], results[len(out_specs):]


def _tri_rows(t, n):
    qi = sum(jnp.where(t >= r * (r + 1) // 2, 1, 0) for r in range(1, n))
    return qi, t - qi * (qi + 1) // 2


def _tri_cols(t, n):
    ki = sum(jnp.where(t >= r * n - r * (r - 1) // 2, 1, 0) for r in range(1, n))
    return ki, ki + t - (ki * n - ki * (ki - 1) // 2)


def _causal_t(st_blk, tk, tq):
    key = lax.broadcasted_iota(jnp.int32, (tk, tq), 0)
    qry = lax.broadcasted_iota(jnp.int32, (tk, tq), 1)
    return jnp.where(qry >= key, st_blk, -jnp.inf)


def attn_forward(qa, ka, vt, name, ex=None):
    hh, s, _ = qa.shape
    tq = tk = min(512, s)
    nq = s // tq
    grp = HEAD_GROUP

    def body(q_ref, k_ref, vt_ref, o_ref, lse_ref, m_ref, acc_ref):
        qi, ki = _tri_rows(pl.program_id(1), nq)

        @pl.when(ki == 0)
        def _():
            m_ref[...] = jnp.full_like(m_ref, -jnp.inf)
            acc_ref[...] = jnp.zeros_like(acc_ref)

        def step(masked):
            nxt = _dot_nt(k_ref[0], q_ref[0])
            for g in range(grp):
                st = nxt
                if g + 1 < grp:
                    nxt = _dot_nt(k_ref[g + 1], q_ref[g + 1])
                if masked:
                    st = _causal_t(st, tk, tq)
                m_old = m_ref[g]
                m_new = jnp.maximum(m_old, jnp.max(st, axis=0, keepdims=True))
                pt = jnp.exp(st - m_new).astype(BF16)
                acc_ref[g] = jnp.exp(m_old - m_new) * acc_ref[g] + _dot(vt_ref[g], pt)
                m_ref[g] = m_new

        @pl.when(ki < qi)
        def _():
            step(False)

        @pl.when(ki == qi)
        def _():
            step(True)
            for g in range(grp):
                acc = acc_ref[g]
                denom = acc[LANE_C:LANE_C + 1, :]
                o_ref[g] = (acc / denom).T.astype(BF16)
                lse_ref[g] = m_ref[g] + jnp.log(denom)

    qspec = pl.BlockSpec((grp, tq, 128), lambda h, t: (h, _tri_rows(t, nq)[0], 0))
    kspec = pl.BlockSpec((grp, tk, 128), lambda h, t: (h, _tri_rows(t, nq)[1], 0))
    vspec = pl.BlockSpec((grp, 128, tk), lambda h, t: (h, 0, _tri_rows(t, nq)[1]))
    lspec = pl.BlockSpec((grp, 1, tq), lambda h, t: (h, 0, _tri_rows(t, nq)[0]))
    return _carried_call(
        body, ex, (hh // grp, nq * (nq + 1) // 2), [qspec, kspec, vspec], [qspec, lspec],
        [jax.ShapeDtypeStruct((hh, s, 128), BF16), jax.ShapeDtypeStruct((hh, 1, s), F32)],
        [pltpu.VMEM((grp, 1, tq), F32), pltpu.VMEM((grp, 128, tq), F32)],
        ("arbitrary", "arbitrary"), name, (qa, ka, vt))


def attn_backward(qa, ka, va, oa, doa, lse, name, ex=None):
    hh, s, _ = qa.shape
    tq = tk = min(512, s)
    nq = s // tq
    grp = HEAD_GROUP

    def body(q_ref, k_ref, v_ref, o_ref, do_ref, lse_ref, dq_ref, dk_ref, dv_ref, dka_ref, dva_ref):
        ki, qi = _tri_cols(pl.program_id(1), nq)

        @pl.when(pl.program_id(1) == 0)
        def _():
            dq_ref[...] = jnp.zeros_like(dq_ref)

        @pl.when(qi == ki)
        def _():
            dka_ref[...] = jnp.zeros_like(dka_ref)
            dva_ref[...] = jnp.zeros_like(dva_ref)

        def step(masked):
            rows = pl.ds(pl.multiple_of(qi * tq, tq), tq)
            products = lambda g: (_dot_nt(k_ref[g], q_ref[g]), _dot_nt(v_ref[g], do_ref[g]))
            nxt = products(0)
            for g in range(grp):
                st, dpt = nxt
                if g + 1 < grp:
                    nxt = products(g + 1)
                q, k, do = q_ref[g], k_ref[g], do_ref[g]
                if masked:
                    st = _causal_t(st, tk, tq)
                pt = jnp.exp(st - lse_ref[g])
                delta = jnp.sum((do.astype(F32) * o_ref[g].astype(F32)).T, axis=0, keepdims=True)
                dst = (pt * (dpt - delta)).astype(BF16)
                dva_ref[g] += _dot(pt.astype(BF16), do)
                dka_ref[g] += _dot(dst, q)
                dq_ref[g, rows, :] += _dot_tn(dst, k)

        @pl.when(qi > ki)
        def _():
            step(False)

        @pl.when(qi == ki)
        def _():
            step(True)

        @pl.when(qi == nq - 1)
        def _():
            dk_ref[...] = dka_ref[...]
            dv_ref[...] = dva_ref[...].astype(BF16)

    qspec = pl.BlockSpec((grp, tq, 128), lambda h, t: (h, _tri_cols(t, nq)[1], 0))
    lspec = pl.BlockSpec((grp, 1, tq), lambda h, t: (h, 0, _tri_cols(t, nq)[1]))
    kspec = pl.BlockSpec((grp, tk, 128), lambda h, t: (h, _tri_cols(t, nq)[0], 0))
    return _carried_call(
        body, ex, (hh // grp, nq * (nq + 1) // 2), [qspec, kspec, kspec, qspec, qspec, lspec],
        [pl.BlockSpec((grp, s, 128), lambda h, t: (h, 0, 0)), kspec, kspec],
        [jax.ShapeDtypeStruct((hh, s, 128), F32), jax.ShapeDtypeStruct((hh, s, 128), F32),
         jax.ShapeDtypeStruct((hh, s, 128), BF16)],
        [pltpu.VMEM((grp, tk, 128), F32), pltpu.VMEM((grp, tk, 128), F32)],
        ("arbitrary", "arbitrary"), name, (qa, ka, va, oa, doa, lse))


def attn_post(dqa, dka, dva, proj, gq2, gk2, dproj, name):
    s = proj.shape[0]
    tm = min(256, s)

    def body(dq_ref, dk_ref, dv_ref, q_ref, k_ref, gq_ref, gk_ref, dp_any, dp_ref, dc_ref, dgq_ref, dgk_ref):
        lane, lo = _lanes()

        @pl.when(pl.program_id(0) == 0)
        def _():
            dgq_ref[...] = jnp.zeros_like(dgq_ref)
            dgk_ref[...] = jnp.zeros_like(dgk_ref)

        def pair(ref, j):
            return jnp.where(lo, ref[2 * j].astype(F32), _swap_halves(ref[2 * j + 1].astype(F32)))

        def norm_bwd(raw, g, dhat, scale):
            r = lax.rsqrt(_half_mean(raw * raw, lo) + EPS)
            y = raw * r
            dy = dhat * (g * scale)
            return r * (dy - y * _half_mean(dy * y, lo)), jnp.sum(dhat * y, axis=0, keepdims=True) * scale

        dc = jnp.zeros((tm, 128), F32)
        for j in range(HEADS // 2):
            cols = slice(128 * j, 128 * (j + 1))
            dq, dgq = norm_bwd(q_ref[:, cols].astype(F32), gq_ref[...], pair(dq_ref, j), ATTN_SCALE)
            dk, dgk = norm_bwd(k_ref[:, cols].astype(F32), gk_ref[...], pair(dk_ref, j), 1.0)
            dgq_ref[...] += dgq
            dgk_ref[...] += dgk
            dp_ref[:, cols] = dq.astype(BF16)
            dp_ref[:, D_ATTN + 128 * j:D_ATTN + 128 * (j + 1)] = dk.astype(BF16)
            dp_ref[:, 2 * D_ATTN + 128 * j:2 * D_ATTN + 128 * (j + 1)] = pair(dv_ref, j).astype(BF16)
            for e in range(2):
                h = 2 * j + e
                col = _lane_col(dq_ref[h], lane, LANE_C) - _lane_col(dk_ref[h], lane, LANE_ONE)
                dc = jnp.where(lane == h, col, dc)
        dp_ref[:, 3 * D_ATTN:] = jnp.zeros((tm, DPROJ_TAIL - 3 * D_ATTN), BF16)
        dc_ref[...] = dc

    heads = lambda: pl.BlockSpec((HEADS, tm, 128), lambda i: (0, i, 0))
    vec = pl.BlockSpec((1, 128), lambda i: (0, 0))
    first = N_REST // D_ATTN
    return pl.pallas_call(
        body, grid=(s // tm,),
        in_specs=[heads(), heads(), heads(), pl.BlockSpec((tm, D_ATTN), lambda i: (i, first)),
                  pl.BlockSpec((tm, D_ATTN), lambda i: (i, first + 1)), vec, vec, pl.BlockSpec(memory_space=pl.ANY)],
        out_specs=[pl.BlockSpec((tm, DPROJ_TAIL), lambda i: (i, N_REST // DPROJ_TAIL)),
                   pl.BlockSpec((tm, 128), lambda i: (i, 0)), vec, vec],
        out_shape=[jax.ShapeDtypeStruct(dproj.shape, BF16), jax.ShapeDtypeStruct((s, 128), F32),
                   jax.ShapeDtypeStruct((1, 128), F32), jax.ShapeDtypeStruct((1, 128), F32)],
        input_output_aliases={7: 0},
        compiler_params=_cparams(("arbitrary",)), name=name)(dqa, dka, dva, proj, proj, gq2, gk2, dproj)


def _pool_groups(tm):
    gid = lax.broadcasted_iota(jnp.int32, (1, D_POOL), 1) // (D_POOL // 4)
    win = jnp.where(gid == 0, 2.0, jnp.where(gid == 1, 4.0, jnp.where(gid == 2, 8.0, 16.0)))
    return gid, win


def _by_group(gid, v2, v4, v8, v16):
    return jnp.where(gid == 0, v2, jnp.where(gid == 1, v4, jnp.where(gid == 2, v8, v16)))


def _branches(rest_ref, halo_ref, a_ref, wa_ref, wc_ref, wp_ref, sc_ref, cw_ref, ti, tm):
    f = lambda v: v.astype(F32)
    cx, cb, cc, px = f(rest_ref[:, 0:256]), f(rest_ref[:, 256:512]), f(rest_ref[:, 512:768]), f(rest_ref[:, 768:1024])
    live = jnp.where(ti > 0, 1.0, 0.0)
    hz = f(halo_ref[:, 0:256]) * f(halo_ref[:, 512:768]) * live
    hp = f(halo_ref[:, 768:1024]) * live
    z = cc * cx
    zf = jnp.concatenate([hz, z], axis=0)
    z1 = pltpu.roll(zf, 1, 0)[HALO:]
    z2 = pltpu.roll(zf, 2, 0)[HALO:]
    cw = cw_ref[...]
    conv = cw[2:3] * z + cw[1:2] * z1 + cw[0:1] * z2
    uc = cb * conv
    pf = jnp.concatenate([hp, px], axis=0)
    s2 = pf + pltpu.roll(pf, 1, 0)
    s4 = s2 + pltpu.roll(s2, 2, 0)
    s8 = s4 + pltpu.roll(s4, 4, 0)
    s16 = s8 + pltpu.roll(s8, 8, 0)
    gid, win = _pool_groups(tm)
    t = (ti * tm + lax.broadcasted_iota(jnp.int32, (tm, 1), 0)).astype(F32)
    inv = 1.0 / jnp.minimum(t + 1.0, win)
    dpool = _by_group(gid, s2[HALO:], s4[HALO:], s8[HALO:], s16[HALO:]) * inv - px
    _, lo = _lanes()
    a_tok = [jnp.where(lo, f(a_ref[2 * j]), _swap_halves(f(a_ref[2 * j + 1]))).astype(BF16) for j in range(HEADS // 2)]
    y_attn = _dot(a_tok[0], wa_ref[0:128, :])
    for j in range(1, HEADS // 2):
        y_attn += _dot(a_tok[j], wa_ref[128 * j:128 * (j + 1), :])
    y_conv = _dot(uc.astype(BF16), wc_ref[...])
    y_pool_raw = _dot(dpool.astype(BF16), wp_ref[...])
    sg = [_sigmoid(f(rest_ref[:, 1024 + i * D_MODEL:1024 + (i + 1) * D_MODEL])) for i in range(3)]
    return dict(cx=cx, cb=cb, cc=cc, z=z, z1=z1, z2=z2, conv=conv, uc=uc, dpool=dpool, inv=inv, gid=gid, a_tok=a_tok,
                y_attn=y_attn, y_conv=y_conv, y_pool_raw=y_pool_raw, sg=sg, cw=cw)


def _mix_specs(tm, ti_of):
    blocks_per_tile = tm // HALO
    return [
        pl.BlockSpec((tm, N_REST), lambda i: (ti_of(i), 0)),
        pl.BlockSpec((HALO, 1024), lambda i: (jnp.maximum(ti_of(i) * blocks_per_tile - 1, 0), 0)),
        pl.BlockSpec((HEADS, tm, 128), lambda i: (0, ti_of(i), 0)),
        pl.BlockSpec((D_ATTN, D_MODEL), lambda i: (0, 0)),
        pl.BlockSpec((D_CONV, D_MODEL), lambda i: (0, 0)),
        pl.BlockSpec((D_POOL, D_MODEL), lambda i: (0, 0)),
        pl.BlockSpec((1, D_MODEL), lambda i: (0, 0)),
        pl.BlockSpec((8, D_CONV), lambda i: (0, 0)),
    ]


def mix_fwd(proj, a, x, wa, wc, wp, scale, cw, wo, name):
    s = x.shape[0]
    tm = min(256, s)

    def body(rest_ref, halo_ref, a_ref, wa_ref, wc_ref, wp_ref, sc_ref, cw_ref, wo_ref, x_ref, o_ref):
        b = _branches(rest_ref, halo_ref, a_ref, wa_ref, wc_ref, wp_ref, sc_ref, cw_ref, pl.program_id(0), tm)
        merged = b["sg"][0] * b["y_attn"] + b["sg"][1] * b["y_conv"] + b["sg"][2] * (b["y_pool_raw"] * sc_ref[...])
        o_ref[...] = x_ref[...] + _dot(merged.astype(BF16), wo_ref[...])

    return pl.pallas_call(
        body, grid=(s // tm,),
        in_specs=_mix_specs(tm, lambda i: i) + [pl.BlockSpec((D_MODEL, D_MODEL), lambda i: (0, 0)),
                                                 pl.BlockSpec((tm, D_MODEL), lambda i: (i, 0))],
        out_specs=pl.BlockSpec((tm, D_MODEL), lambda i: (i, 0)),
        out_shape=jax.ShapeDtypeStruct((s, D_MODEL), F32),
        compiler_params=_cparams(("parallel",)), name=name)(proj, proj, a, wa, wc, wp, scale, cw, wo, x)


def mix_bwd(proj, a, dx1, wa, wc, wp, scale, cw, wo, name):
    s = dx1.shape[0]
    tm = min(256, s)
    nt = s // tm
    ti_of = lambda i: nt - 1 - i
    n = tm + HALO

    def body(rest_ref, halo_ref, a_ref, wa_ref, wc_ref, wp_ref, sc_ref, cw_ref, wo_ref,
             dx_ref, dp_ref, da_ref, at_ref, mg_ref, dya_ref, dyc_ref, dyp_ref, uc_ref, dd_ref, dsc_ref, dcw_ref,
             cdc_ref, cde_ref):
        i = pl.program_id(0)
        ti = ti_of(i)

        @pl.when(i == 0)
        def _():
            cdc_ref[...] = jnp.zeros_like(cdc_ref)
            cde_ref[...] = jnp.zeros_like(cde_ref)
            dsc_ref[...] = jnp.zeros_like(dsc_ref)
            dcw_ref[...] = jnp.zeros_like(dcw_ref)

        b = _branches(rest_ref, halo_ref, a_ref, wa_ref, wc_ref, wp_ref, sc_ref, cw_ref, ti, tm)
        sg, sc = b["sg"], sc_ref[...]
        y_pool = b["y_pool_raw"] * sc
        merged = sg[0] * b["y_attn"] + sg[1] * b["y_conv"] + sg[2] * y_pool
        mg_ref[...] = merged.astype(BF16)
        dm = _dot_nt(dx_ref[...].astype(BF16), wo_ref[...])
        for j, y in enumerate((b["y_attn"], b["y_conv"], y_pool)):
            dp_ref[:, 1024 + j * D_MODEL:1024 + (j + 1) * D_MODEL] = (dm * y * sg[j] * (1.0 - sg[j])).astype(BF16)
        dya = (dm * sg[0]).astype(BF16)
        dya_ref[...] = dya
        _, lo = _lanes()
        for j in range(HEADS // 2):
            at_ref[:, 128 * j:128 * (j + 1)] = b["a_tok"][j]
            da = _dot_nt(dya, wa_ref[128 * j:128 * (j + 1), :])
            da_ref[2 * j] = jnp.where(lo, da, 0.0).astype(BF16)
            da_ref[2 * j + 1] = jnp.where(lo, _swap_halves(da), 0.0).astype(BF16)
        dyc = (dm * sg[1]).astype(BF16)
        dyc_ref[...] = dyc
        duc = _dot_nt(dyc, wc_ref[...])
        dyp = dm * sg[2]
        dsc_ref[...] += jnp.sum(dyp * b["y_pool_raw"], axis=0, keepdims=True)
        dypr = (dyp * sc).astype(BF16)
        dyp_ref[...] = dypr
        ddp = _dot_nt(dypr, wp_ref[...])
        uc_ref[...] = b["uc"].astype(BF16)
        dd_ref[...] = b["dpool"].astype(BF16)

        dconv = duc * b["cb"]
        dp_ref[:, 256:512] = (duc * b["conv"]).astype(BF16)
        dcf = jnp.concatenate([dconv, cdc_ref[...]], axis=0)
        cw = b["cw"]
        dz = cw[2:3] * dconv + cw[1:2] * pltpu.roll(dcf, n - 1, 0)[:tm] + cw[0:1] * pltpu.roll(dcf, n - 2, 0)[:tm]
        dp_ref[:, 0:256] = (dz * b["cc"]).astype(BF16)
        dp_ref[:, 512:768] = (dz * b["cx"]).astype(BF16)
        dcw_ref[0:1, :] += jnp.sum(dconv * b["z2"], axis=0, keepdims=True)
        dcw_ref[1:2, :] += jnp.sum(dconv * b["z1"], axis=0, keepdims=True)
        dcw_ref[2:3, :] += jnp.sum(dconv * b["z"], axis=0, keepdims=True)
        cdc_ref[...] = dconv[:HALO]

        e = ddp * b["inv"]
        ef = jnp.concatenate([e, cde_ref[...]], axis=0)
        r2 = ef + pltpu.roll(ef, n - 1, 0)
        r4 = r2 + pltpu.roll(r2, n - 2, 0)
        r8 = r4 + pltpu.roll(r4, n - 4, 0)
        r16 = r8 + pltpu.roll(r8, n - 8, 0)
        dp_ref[:, 768:1024] = (_by_group(b["gid"], r2[:tm], r4[:tm], r8[:tm], r16[:tm]) - ddp).astype(BF16)
        cde_ref[...] = e[:HALO]

    tile = lambda w: pl.BlockSpec((tm, w), lambda i: (ti_of(i), 0))
    whole = lambda r, c: pl.BlockSpec((r, c), lambda i: (0, 0))
    bf = lambda w: jax.ShapeDtypeStruct((s, w), BF16)
    return pl.pallas_call(
        body, grid=(nt,),
        in_specs=_mix_specs(tm, ti_of) + [whole(D_MODEL, D_MODEL), tile(D_MODEL)],
        out_specs=[tile(N_REST), pl.BlockSpec((HEADS, tm, 128), lambda i: (0, ti_of(i), 0)), tile(D_ATTN),
                   tile(D_MODEL), tile(D_MODEL), tile(D_MODEL), tile(D_MODEL),
                   tile(D_CONV), tile(D_POOL), whole(1, D_MODEL), whole(8, D_CONV)],
        out_shape=[bf(DPROJ_COLS), jax.ShapeDtypeStruct((HEADS, s, 128), BF16), bf(D_ATTN),
                   bf(D_MODEL), bf(D_MODEL), bf(D_MODEL), bf(D_MODEL), bf(D_CONV), bf(D_POOL),
                   jax.ShapeDtypeStruct((1, D_MODEL), F32), jax.ShapeDtypeStruct((8, D_CONV), F32)],
        scratch_shapes=[pltpu.VMEM((HALO, D_CONV), F32), pltpu.VMEM((HALO, D_POOL), F32)],
        compiler_params=_cparams(("arbitrary",)), name=name)(proj, proj, a, wa, wc, wp, scale, cw, wo, dx1)


def _adamw_math(w, g, m, v):
    m = ADAM_B1 * m + (1.0 - ADAM_B1) * g
    v = ADAM_B2 * v + (1.0 - ADAM_B2) * (g * g)
    m_hat = m / (1.0 - ADAM_B1 ** ADAM_STEP)
    v_hat = v / (1.0 - ADAM_B2 ** ADAM_STEP)
    delta = -ADAM_LR * (m_hat / (jnp.sqrt(v_hat) + ADAM_EPS) + ADAM_WD * w)
    return delta, m, v


ADAMW_PARTS_BLOCK_BYTES = 4 * 2 ** 20


def _row_tile(rows, cols, copies, itemsize):
    row_bytes = copies * (-(-cols // 128) * 128) * itemsize
    fits = [t for t in range(16, rows + 1, 16) if rows % t == 0 and t * row_bytes <= ADAMW_PARTS_BLOCK_BYTES]
    return max(fits) if fits else rows


def pair_sum(blocks, stage, me, name):
    n_slots, rows, cols = stage.shape
    tr = _row_tile(rows, cols, 1, 4)

    def body(me_ref, a_ref, b_ref, o_ref):
        o_ref[...] = (a_ref[...].astype(F32) + b_ref[...].astype(F32)).astype(BF16)

    slot = pl.BlockSpec((None, tr, cols), lambda i, r, me_ref: (i, r, 0))
    return pl.pallas_call(
        body, out_shape=jax.ShapeDtypeStruct(stage.shape, BF16),
        grid_spec=pltpu.PrefetchScalarGridSpec(
            num_scalar_prefetch=1, grid=(n_slots, rows // tr),
            in_specs=[pl.BlockSpec((None, tr, cols), lambda i, r, me_ref: (me_ref[0] ^ (2 * i), r, 0)), slot],
            out_specs=slot),
        compiler_params=_cparams(("parallel", "parallel")), name=name)(me.reshape(1), blocks, stage)


def adamw_sum(parts, w, m, v, name):
    layers, rows, cols = w.shape
    n_parts = parts.shape[1]
    if rows % 16 == 0:
        tr, tc = _row_tile(rows, cols, n_parts, parts.dtype.itemsize), cols
    else:
        tr, tc = rows, _pick(cols, (256, 128))

    def body(p_ref, w_ref, m_ref, v_ref, g_ref, d_ref, nm_ref, nv_ref):
        g = p_ref[0].astype(F32)
        for i in range(1, n_parts):
            g = g + p_ref[i].astype(F32)
        g_ref[...] = g
        d_ref[...], nm_ref[...], nv_ref[...] = _adamw_math(w_ref[...], g, m_ref[...], v_ref[...])

    spec = pl.BlockSpec((None, tr, tc), lambda l, i, j: (l, i, j))
    return pl.pallas_call(
        body, grid=(layers, rows // tr, cols // tc),
        in_specs=[pl.BlockSpec((None, n_parts, tr, tc), lambda l, i, j: (l, 0, i, j)), spec, spec, spec],
        out_specs=[spec] * 4, out_shape=[jax.ShapeDtypeStruct((layers, rows, cols), F32)] * 4,
        compiler_params=_cparams(("parallel", "parallel", "parallel")), name=name)(parts, w, m, v)


def _me():
    return lax.axis_index("x"), lax.axis_index("y"), lax.axis_index("c")


N_PEERS = N_DEV - 1


def all_gather(shards, name):
    n = len(shards)
    any_spec = pl.BlockSpec(memory_space=pl.ANY)

    def body(*refs):
        x_refs, out_refs = refs[:n], refs[n:2 * n]
        send_sems, recv_sems, local_sems = refs[2 * n:]
        x, y, c = _me()
        me, sibling = (x, y, c), (x, y, 1 - c)
        chips = [(1 - x, y), (x, 1 - y), (1 - x, 1 - y)]

        def copy(t, k, block, to, from_input=False):
            slot = out_refs[t].at[4 * block[0] + 2 * block[1] + block[2]]
            return pltpu.make_async_remote_copy(
                src_ref=x_refs[t] if from_input else slot, dst_ref=slot, send_sem=send_sems.at[N_PEERS * t + k],
                recv_sem=recv_sems.at[N_PEERS * t + k], device_id=to, device_id_type=pl.DeviceIdType.MESH)

        mine = [pltpu.make_async_copy(x_refs[t], out_refs[t].at[4 * x + 2 * y + c], local_sems.at[t]) for t in range(n)]
        started = []
        for t in range(n):
            mine[t].start()
            started.append(copy(t, 0, me, sibling, from_input=True))
            started += [copy(t, 1 + j, me, (*chip, c), from_input=True) for j, chip in enumerate(chips)]
        for cp in started:
            cp.start()
        for j, chip in enumerate(chips):
            for t in range(n):
                copy(t, 1 + j, (*chip, c), me).wait_recv()
                fwd = copy(t, 4 + j, (*chip, c), sibling)
                fwd.start()
                started.append(fwd)
        for t in range(n):
            copy(t, 0, sibling, me).wait_recv()
            for j, chip in enumerate(chips):
                copy(t, 4 + j, (*chip, 1 - c), me).wait_recv()
        for cp in started:
            cp.wait_send()
        for cp in mine:
            cp.wait()

    return pl.pallas_call(
        body, out_shape=[jax.ShapeDtypeStruct((N_DEV,) + s.shape, s.dtype) for s in shards],
        in_specs=[any_spec] * n, out_specs=[any_spec] * n,
        scratch_shapes=[pltpu.SemaphoreType.DMA((N_PEERS * n,)), pltpu.SemaphoreType.DMA((N_PEERS * n,)),
                        pltpu.SemaphoreType.DMA((n,))],
        name=name)(*shards)


SIBLING = 1
OTHER_CHIPS = (2, 4, 6)
SAME_CORE = (0,) + OTHER_CHIPS


class Exchange:
    def __init__(self, inputs, out_shapes, aliases, copies, local=()):
        self.inputs, self.out_shapes, self.aliases = list(inputs), list(out_shapes), aliases
        self._copies, self._local = list(copies), list(local)
        self.scratch = [pltpu.SemaphoreType.DMA((len(self._copies),)), pltpu.SemaphoreType.DMA((len(self._copies),)),
                        pltpu.SemaphoreType.DMA((max(len(self._local), 1),))]

    def _build(self, ins, outs, sems):
        send_sems, recv_sems, local_sems = sems
        x, y, c = _me()
        me = 4 * x + 2 * y + c
        local = [functools.partial(pltpu.make_async_copy, src(ins, outs, me), dst(outs, me), local_sems.at[i])
                 for i, (src, dst) in enumerate(self._local)]
        sends, recvs = [], []
        for i, (mask, src, dst) in enumerate(self._copies):
            px, py, pc = x ^ ((mask >> 2) & 1), y ^ ((mask >> 1) & 1), c ^ (mask & 1)
            pair = dict(send_sem=send_sems.at[i], recv_sem=recv_sems.at[i], device_id_type=pl.DeviceIdType.MESH)
            sends.append(functools.partial(
                pltpu.make_async_remote_copy, src_ref=src(ins, outs, me), dst_ref=dst(outs, me), device_id=(px, py, pc), **pair))
            recvs.append(functools.partial(
                pltpu.make_async_remote_copy, src_ref=src(ins, outs, me), dst_ref=dst(outs, me ^ mask), device_id=(x, y, c), **pair))
        return local, sends, recvs

    def start(self, ins, outs, sems):
        local, sends, _ = self._build(ins, outs, sems)
        for make in local + sends:
            make().start()

    def drain(self, ins, outs, sems):
        local, sends, recvs = self._build(ins, outs, sems)
        for make in recvs:
            make().wait_recv()
        for make in sends:
            make().wait_send()
        for make in local:
            make().wait()


def _bind(fn, *args):
    return functools.partial(fn, *args)


def gather_over_ici(shards):
    copies = [(mask, _bind(lambda t, ins, outs, me: ins[t], t), _bind(lambda t, outs, sender: outs[t].at[sender], t))
              for t in range(len(shards)) for mask in OTHER_CHIPS]
    local = [(_bind(lambda t, ins, outs, me: ins[t], t), _bind(lambda t, outs, me: outs[t].at[me], t))
             for t in range(len(shards))]
    return Exchange(shards, [jax.ShapeDtypeStruct((N_DEV,) + s.shape, s.dtype) for s in shards], {}, copies, local)


def gather_over_d2d(gathered):
    copies = [(SIBLING, _bind(lambda t, m, ins, outs, me: outs[t].at[me ^ m], t, m),
               _bind(lambda t, m, outs, sender: outs[t].at[sender ^ m], t, m))
              for t in range(len(gathered)) for m in SAME_CORE]
    return Exchange(gathered, [jax.ShapeDtypeStruct(g.shape, g.dtype) for g in gathered],
                    {t: t for t in range(len(gathered))}, copies)


def scatter_over_d2d(blocks):
    copies = [(SIBLING, _bind(lambda t, m, ins, outs, me: ins[t].at[me ^ SIBLING ^ m], t, m),
               _bind(lambda t, i, outs, sender: outs[t].at[i], t, i))
              for t in range(len(blocks)) for i, m in enumerate(SAME_CORE)]
    return Exchange(blocks, [jax.ShapeDtypeStruct((len(SAME_CORE),) + b.shape[1:], b.dtype) for b in blocks], {}, copies)


def scatter_over_ici(pair_sums, bufs, layer):
    n = len(pair_sums)
    copies = [(m, _bind(lambda t, i, ins, outs, me: ins[t].at[i], t, i),
               _bind(lambda t, i, outs, sender: outs[t].at[layer, i], t, i))
              for t in range(n) for i, m in enumerate(SAME_CORE) if m]
    local = [(_bind(lambda t, ins, outs, me: ins[t].at[0], t), _bind(lambda t, outs, me: outs[t].at[layer, 0], t))
             for t in range(n)]
    return Exchange(list(pair_sums) + list(bufs), [jax.ShapeDtypeStruct(b.shape, b.dtype) for b in bufs],
                    {n + t: t for t in range(n)}, copies, local)


def run_exchange(ex, name):
    any_spec = pl.BlockSpec(memory_space=pl.ANY)
    n_in, n_out = len(ex.inputs), len(ex.out_shapes)

    def body(*refs):
        ins, outs, sems = refs[:n_in], refs[n_in:n_in + n_out], refs[n_in + n_out:]
        ex.start(ins, outs, sems)
        ex.drain(ins, outs, sems)

    return pl.pallas_call(
        body, out_shape=ex.out_shapes, in_specs=[any_spec] * n_in, out_specs=[any_spec] * n_out,
        input_output_aliases=ex.aliases, scratch_shapes=ex.scratch, name=name)(*ex.inputs)


MATRICES = ("w_in", "w_attn_out", "w_conv_out", "pool_w", "w_o", "w_ffn_in", "w_ffn_out")
TRANSPOSED = ("w_in", "w_ffn_in")
SHARD_INFO = {
    "w_in": ((DEPTH, D_IN // N_DEV, D_MODEL), 1),
    "w_attn_out": ((DEPTH, D_ATTN, D_MODEL // N_DEV), 2),
    "w_conv_out": ((DEPTH, D_CONV, D_MODEL // N_DEV), 2),
    "pool_w": ((DEPTH, 4, 64, 256 // N_DEV), 3),
    "w_o": ((DEPTH, D_MODEL // N_DEV, D_MODEL), 1),
    "w_ffn_in": ((DEPTH, 2 * D_FF // N_DEV, D_MODEL), 1),
    "w_ffn_out": ((DEPTH, D_FF // N_DEV, D_MODEL), 1),
}


def _handled(name, t):
    return jnp.transpose(t, (0, 2, 1)) if name in TRANSPOSED else t
VECTORS = ("norm_mix_g", "forget_b", "q_norm_g", "k_norm_g", "pool_scale", "norm_ffn_g")
VECTOR_SHAPES = {"norm_mix_g": (DEPTH, D_MODEL), "forget_b": (DEPTH, HEADS), "q_norm_g": (DEPTH, HEAD_DIM),
                 "k_norm_g": (DEPTH, HEAD_DIM), "pool_scale": (DEPTH, D_MODEL), "norm_ffn_g": (DEPTH, D_MODEL)}
CONV_W_FULL = (DEPTH, 3, D_CONV)


def _size(shape):
    n = 1
    for v in shape:
        n *= v
    return n


def _pack(arrays, rows, cols):
    flat = jnp.concatenate([a.reshape(-1) for a in arrays])
    return jnp.pad(flat, (0, rows * cols - flat.shape[0])).reshape(rows, cols)


def _unpack(packed, shapes):
    flat, out, off = packed.reshape(-1), [], 0
    for shp in shapes:
        out.append(flat[off:off + _size(shp)].reshape(shp))
        off += _size(shp)
    return out


def _join_shards(stacked, axis):
    moved = jnp.moveaxis(stacked, 0, axis)
    shp = list(moved.shape)
    shp[axis:axis + 2] = [shp[axis] * shp[axis + 1]]
    return moved.reshape(shp)


def _cut_shards(full, axis):
    shp = list(full.shape)
    shp[axis:axis + 1] = [N_DEV, shp[axis] // N_DEV]
    return jnp.moveaxis(full.reshape(shp), axis, 0)


def _regroup_w_in(wt):
    pad = jnp.zeros((N_FULL - N_MAIN - HEADS, wt.shape[1]), wt.dtype)
    return jnp.concatenate([wt[1544:2568], wt[2568:5640], wt[0:1536], wt[1536:1544], pad], axis=0)


def _ungroup_w_in(wpt):
    return jnp.concatenate([wpt[4096:5632], wpt[5632:5640], wpt[0:1024], wpt[1024:4096]], axis=0)


def _pool_block_diag(w):
    out = jnp.zeros((D_POOL, D_MODEL), w.dtype)
    for g in range(4):
        out = lax.dynamic_update_slice(out, w[g], (g * 64, g * 256))
    return out


def _pool_from_block_diag(wbd):
    return jnp.stack([wbd[g * 64:(g + 1) * 64, g * 256:(g + 1) * 256] for g in range(4)])


def _layer_weights(mats, vec, conv_w, l):
    wp = _pool_block_diag(mats["pool_w"])
    row = lambda v: v.reshape(1, -1)
    fb = jnp.zeros((1, 128), F32).at[0, :HEADS].set(vec["forget_b"][l])
    cw = jnp.zeros((8, D_CONV), F32).at[:3].set(conv_w[l])
    twice = lambda v: jnp.tile(v.reshape(1, -1), (1, 2))
    return dict(
        wt_in=_regroup_w_in(mats["w_in"]), wt_ffn_in=mats["w_ffn_in"], w_ffn_out=mats["w_ffn_out"],
        wa=mats["w_attn_out"], wc=mats["w_conv_out"], wp=wp, wo=mats["w_o"],
        g_mix=row(vec["norm_mix_g"][l]), g_ffn=row(vec["norm_ffn_g"][l]), gq2=twice(vec["q_norm_g"][l]),
        gk2=twice(vec["k_norm_g"][l]), scale=row(vec["pool_scale"][l]), fb=fb, cw=cw)


def _layer_fwd(x, w, l, comm):
    (proj, h), _ = norm_matmul(x, w["g_mix"], w["wt_in"], N_MAIN, f"in_proj_{l}")
    z, c = forget_fwd(h, w["wt_in"], w["fb"], f"forget_fwd_{l}")
    qa, ka, va, vt = attn_prep(proj, c, w["gq2"], w["gk2"], f"attn_prep_{l}")
    (oa, lse), half = attn_forward(qa, ka, vt, f"attn_fwd_{l}", comm.gather_ici(l + 1))
    x1 = mix_fwd(proj, oa, x, w["wa"], w["wc"], w["wp"], w["scale"], w["cw"], w["wo"], f"mix_fwd_{l}")
    (gu, h2), gathered = norm_matmul(x1, w["g_ffn"], w["wt_ffn_in"], 2 * D_FF, f"ffn_in_{l}", comm.gather_d2d(l + 1, half))
    x2 = swiglu_matmul(gu, w["w_ffn_out"], x1, f"ffn_out_{l}")
    saved = dict(x=x, proj=proj, h=h, z=z, qa=qa, ka=ka, va=va, oa=oa, lse=lse, x1=x1, gu=gu, h2=h2)
    return x2, saved, gathered


def _layer_bwd(dx2, sv, w, l, comm):
    g = {}
    (dgu, act), stage = swiglu_bwd(dx2, sv["gu"], w["w_ffn_out"], f"ffn_out_bwd_{l}", comm.scatter_d2d(l + 1))
    sums = comm.pair_sums(l + 1, stage)
    g["w_ffn_out"] = tn_matmul(act, dx2, f"dw_ffn_out_{l}")
    g["w_ffn_in"] = tn_matmul(dgu, sv["h2"], f"dw_ffn_in_{l}")
    dx1, dg = matmul_normbwd(dgu, w["wt_ffn_in"], sv["x1"], w["g_ffn"], dx2, f"ffn_in_bwd_{l}")
    g["norm_ffn_g"] = dg[0]

    (dproj, doa, a_tok, merged, dya, dyc, dyp, uc, dd, dscale, dcw) = mix_bwd(
        sv["proj"], sv["oa"], dx1, w["wa"], w["wc"], w["wp"], w["scale"], w["cw"], w["wo"], f"mix_bwd_{l}")
    g["w_o"] = tn_matmul(merged, dx1, f"dw_o_{l}")
    g["w_attn_out"] = tn_matmul(a_tok, dya, f"dw_attn_out_{l}")
    g["w_conv_out"] = tn_matmul(uc, dyc, f"dw_conv_out_{l}")
    g["pool_w"] = _pool_from_block_diag(tn_matmul(dd, dyp, f"dw_pool_{l}"))
    g["pool_scale"] = dscale[0]
    g["conv_w"] = dcw[:3]

    (dqa, dka, dva), got = attn_backward(sv["qa"], sv["ka"], sv["va"], sv["oa"], doa, sv["lse"], f"attn_bwd_{l}",
                                         comm.scatter_ici(l + 1, sums))
    comm.scattered(got)
    dproj, dc, dgq, dgk = attn_post(dqa, dka, dva, sv["proj"], w["gq2"], w["gk2"], dproj, f"attn_post_{l}")
    g["q_norm_g"] = dgq[0, :HEAD_DIM] + dgq[0, HEAD_DIM:]
    g["k_norm_g"] = dgk[0, :HEAD_DIM] + dgk[0, HEAD_DIM:]
    dproj, db = forget_bwd(dc, sv["z"], dproj, f"forget_bwd_{l}")
    g["forget_b"] = db[0, :HEADS]

    g["w_in"] = _ungroup_w_in(tn_matmul(dproj, sv["h"], f"dw_in_{l}", m_cols=N_FULL))
    dx, dg = matmul_normbwd(dproj, w["wt_in"], sv["x"], w["g_mix"], dx1, f"in_proj_bwd_{l}", k=N_FULL)
    g["norm_mix_g"] = dg[0]
    comm.grads(l, g)
    return dx


def _local_step(x, tgt, comm):
    ws, saved = [], []
    w = comm.weights(0, None)
    for l in range(DEPTH):
        ws.append(w)
        x, sv, gathered = _layer_fwd(x, w, l, comm)
        saved.append(sv)
        if l + 1 < DEPTH:
            w = comm.weights(l + 1, gathered)
    sq, dx = loss_kernel(x, tgt, "loss")
    for l in reversed(range(DEPTH)):
        dx = _layer_bwd(dx, saved[l], ws[l], l, comm)
    comm.finish()
    return sq[0, 0], dx


def kernel(x, norm_mix_g, w_in, forget_b, q_norm_g, k_norm_g, w_attn_out, conv_w, w_conv_out, pool_w, pool_scale, w_o, norm_ffn_g, w_ffn_in, w_ffn_out, loss_target, m_norm_mix_g, m_w_in, m_forget_b, m_q_norm_g, m_k_norm_g, m_w_attn_out, m_conv_w, m_w_conv_out, m_pool_w, m_pool_scale, m_w_o, m_norm_ffn_g, m_w_ffn_in, m_w_ffn_out, v_norm_mix_g, v_w_in, v_forget_b, v_q_norm_g, v_k_norm_g, v_w_attn_out, v_conv_w, v_w_conv_out, v_pool_w, v_pool_scale, v_w_o, v_norm_ffn_g, v_w_ffn_in, v_w_ffn_out):
    w = dict(norm_mix_g=norm_mix_g, w_in=w_in, forget_b=forget_b, q_norm_g=q_norm_g, k_norm_g=k_norm_g,
             w_attn_out=w_attn_out, conv_w=conv_w, w_conv_out=w_conv_out, pool_w=pool_w, pool_scale=pool_scale,
             w_o=w_o, norm_ffn_g=norm_ffn_g, w_ffn_in=w_ffn_in, w_ffn_out=w_ffn_out)
    m = dict(norm_mix_g=m_norm_mix_g, w_in=m_w_in, forget_b=m_forget_b, q_norm_g=m_q_norm_g, k_norm_g=m_k_norm_g,
             w_attn_out=m_w_attn_out, conv_w=m_conv_w, w_conv_out=m_w_conv_out, pool_w=m_pool_w,
             pool_scale=m_pool_scale, w_o=m_w_o, norm_ffn_g=m_norm_ffn_g, w_ffn_in=m_w_ffn_in, w_ffn_out=m_w_ffn_out)
    v = dict(norm_mix_g=v_norm_mix_g, w_in=v_w_in, forget_b=v_forget_b, q_norm_g=v_q_norm_g, k_norm_g=v_k_norm_g,
             w_attn_out=v_w_attn_out, conv_w=v_conv_w, w_conv_out=v_w_conv_out, pool_w=v_pool_w,
             pool_scale=v_pool_scale, w_o=v_w_o, norm_ffn_g=v_norm_ffn_g, w_ffn_in=v_w_ffn_in, w_ffn_out=v_w_ffn_out)
    me = 4 * lax.axis_index("x") + 2 * lax.axis_index("y") + lax.axis_index("c")
    layer_shard = {n: SHARD_INFO[n][0][1:] for n in MATRICES}
    cut_axis = {n: SHARD_INFO[n][1] - 1 for n in MATRICES}

    conv_g = all_gather([_pack([conv_w], 8, 128)], "gather_conv_w")[0]
    conv_full = _join_shards(jnp.stack([_unpack(conv_g[i], [conv_w.shape])[0] for i in range(N_DEV)]), 2)
    vec = {n: w[n] for n in VECTORS}

    rc = {n: (_size(layer_shard[n][:-1]), layer_shard[n][-1]) for n in MATRICES}

    class Comm:
        bufs = [lax.empty((DEPTH, len(SAME_CORE)) + layer_shard[n], BF16) for n in MATRICES]
        blocks = [None] * DEPTH
        small_g = [None] * DEPTH

        @staticmethod
        def shards(l):
            return [_handled(n, w[n])[l].astype(BF16) for n in MATRICES]

        @staticmethod
        def gather_ici(l):
            return gather_over_ici(Comm.shards(l)) if l < DEPTH else None

        @staticmethod
        def gather_d2d(l, half):
            return gather_over_d2d(half) if l < DEPTH else None

        @staticmethod
        def weights(l, gathered):
            if l == 0:
                gathered = all_gather(Comm.shards(0), "gather_0")
            mats = {n: _join_shards(t, cut_axis[n]) for n, t in zip(MATRICES, gathered)}
            return _layer_weights(mats, vec, conv_full, l)

        @staticmethod
        def grads(l, g):
            Comm.small_g[l] = g
            Comm.blocks[l] = [_cut_shards(g[n], cut_axis[n]) for n in MATRICES]

        @staticmethod
        def scatter_d2d(l):
            return scatter_over_d2d(Comm.blocks[l]) if l < DEPTH else None

        @staticmethod
        def pair_sums(l, stage):
            if l >= DEPTH:
                return None
            return [pair_sum(b.reshape((N_DEV,) + rc[n]), s.reshape((len(SAME_CORE),) + rc[n]), me,
                             f"pair_sum_{n}_{l}").reshape(s.shape) for n, b, s in zip(MATRICES, Comm.blocks[l], stage)]

        @staticmethod
        def scatter_ici(l, sums):
            return scatter_over_ici(sums, Comm.bufs, l) if l < DEPTH else None

        @staticmethod
        def scattered(results):
            if results:
                Comm.bufs = list(results)

        @staticmethod
        def finish():
            stage = run_exchange(Comm.scatter_d2d(0), "scatter_d2d_0")
            Comm.scattered(run_exchange(Comm.scatter_ici(0, Comm.pair_sums(0, stage)), "scatter_ici_0"))

    small_g, received = Comm.small_g, Comm
    sq, dx = _local_step(x[0], loss_target[0], Comm)
    loss = lax.psum(0.5 * sq / D_MODEL, ("x", "y", "c"))

    big = {}
    for n, parts in zip(MATRICES, received.bufs):
        outs = adamw_sum(parts.reshape((DEPTH, len(SAME_CORE)) + rc[n]),
                         *[_handled(n, d[n]).reshape((DEPTH,) + rc[n]) for d in (w, m, v)], f"adamw_{n}")
        big[n] = [_handled(n, t.reshape((DEPTH,) + layer_shard[n])) for t in outs]

    small_shapes = [VECTOR_SHAPES[n] for n in VECTORS] + [CONV_W_FULL]
    stacked = [jnp.stack([small_g[l][n] for l in range(DEPTH)]) for n in VECTORS + ("conv_w",)]
    sparts = all_gather([_pack(stacked, SMALL_ROWS, 128)], "gather_vector_grads")[0]
    col0 = me * (D_CONV // N_DEV)
    place = lambda t: lax.dynamic_update_slice(jnp.zeros(CONV_W_FULL, F32), t, (0, 0, col0))
    spacked = [_pack([d[n] for n in VECTORS] + [place(d["conv_w"])], SMALL_ROWS, 128)[None] for d in (w, m, v)]
    small = [_unpack(t[0], small_shapes) for t in adamw_sum(sparts[None], *spacked, "adamw_vectors")]

    def result(kind):
        out = {n: big[n][kind] for n in MATRICES}
        out.update({n: small[kind][j] for j, n in enumerate(VECTORS)})
        out["conv_w"] = lax.dynamic_slice(small[kind][len(VECTORS)], (0, 0, col0), conv_w.shape)
        return [out[n] for n in w]

    return (loss, dx[None], *result(0), *result(1), *result(2), *result(3))
```

```python
import functools

import jax
import jax.numpy as jnp
from jax import lax
from jax.experimental import pallas as pl
from jax.experimental.pallas import tpu as pltpu

F32 = jnp.float32
BF16 = jnp.bfloat16

N_DEV = 8
DEPTH = 4
D_MODEL = 1024
HEAD_DIM = 64
HEADS = 8
D_ATTN = 512
D_CONV = 256
D_POOL = 256
D_FF = 2816
D_IN = 5640
EPS = 1e-6
ATTN_SCALE = HEAD_DIM ** -0.5

N_REST = 4096
N_MAIN = 5632
N_FULL = 5760
DPROJ_TAIL = 2048
DPROJ_COLS = N_REST + DPROJ_TAIL
FF_BLK = 256
N_FF_BLKS = D_FF // FF_BLK
HALO = 16

ADAM_LR = 0.001
ADAM_B1 = 0.9
ADAM_B2 = 0.999
ADAM_EPS = 1e-08
ADAM_WD = 0.01
ADAM_STEP = 10

PACK_COLS = 1024
PACK_ROWS = 8192
SMALL_ROWS = 128

VMEM_LIMIT = 48 * 2 ** 20


def _cparams(sem, vmem=None):
    return pltpu.CompilerParams(dimension_semantics=sem, vmem_limit_bytes=vmem or VMEM_LIMIT)


def _pick(n, cands):
    for c in cands:
        if n % c == 0:
            return c
    raise ValueError(f"no tile for {n}")


def _sigmoid(v):
    return 1.0 / (1.0 + jnp.exp(-v))


def _rstd(v):
    return lax.rsqrt(jnp.mean(v * v, axis=-1, keepdims=True) + EPS)


def _dot(a, b):
    return jnp.dot(a, b, preferred_element_type=F32)


def _dot_tn(a, b):
    return lax.dot_general(a, b, (((0,), (0,)), ((), ())), preferred_element_type=F32)


def _dot_nt(a, b):
    return lax.dot_general(a, b, (((1,), (1,)), ((), ())), preferred_element_type=F32)


def norm_matmul(x, g, wt, n_cols, name, ex=None):
    s, d = x.shape
    tm, tn = min(1024, s), _pick(n_cols, (1408, 512))

    def body(x_ref, g_ref, w_ref, o_ref, h_ref):
        @pl.when(pl.program_id(1) == 0)
        def _():
            xv = x_ref[...]
            h_ref[...] = (xv * _rstd(xv) * g_ref[...]).astype(BF16)

        o_ref[...] = _dot_nt(h_ref[...], w_ref[...]).astype(BF16)

    return _carried_call(
        body, ex, (s // tm, n_cols // tn),
        [pl.BlockSpec((tm, d), lambda i, j: (i, 0)), pl.BlockSpec((1, d), lambda i, j: (0, 0)),
         pl.BlockSpec((tn, d), lambda i, j: (j, 0))],
        [pl.BlockSpec((tm, tn), lambda i, j: (i, j)), pl.BlockSpec((tm, d), lambda i, j: (i, 0))],
        [jax.ShapeDtypeStruct((s, n_cols), BF16), jax.ShapeDtypeStruct((s, d), BF16)], [],
        ("arbitrary", "arbitrary"), name, (x, g, wt))


def tn_matmul(a, b, name, m_cols=None):
    t = a.shape[0]
    m = m_cols or a.shape[1]
    n = b.shape[1]
    tk = min(1024, t)
    tmm = _pick(m, (1408, 1152, 1024, 512, 256))
    tn = _pick(n, (1408, 1152, 1024, 512, 128))
    nk = t // tk

    def body(a_ref, b_ref, o_ref, acc_ref):
        @pl.when(pl.program_id(2) == 0)
        def _():
            acc_ref[...] = jnp.zeros_like(acc_ref)

        acc_ref[...] += _dot_tn(a_ref[...].astype(BF16), b_ref[...].astype(BF16))

        @pl.when(pl.program_id(2) == nk - 1)
        def _():
            o_ref[...] = acc_ref[...].astype(BF16)

    return pl.pallas_call(
        body, grid=(m // tmm, n // tn, nk),
        in_specs=[pl.BlockSpec((tk, tmm), lambda i, j, k: (k, i)), pl.BlockSpec((tk, tn), lambda i, j, k: (k, j))],
        out_specs=pl.BlockSpec((tmm, tn), lambda i, j, k: (i, j)),
        out_shape=jax.ShapeDtypeStruct((m, n), BF16), scratch_shapes=[pltpu.VMEM((tmm, tn), F32)],
        compiler_params=_cparams(("parallel", "parallel", "arbitrary")), name=name)(a, b)


def matmul_normbwd(a, wt, x, g, dres, name, k=None):
    s = a.shape[0]
    k = k or a.shape[1]
    d = wt.shape[1]
    tm = min(512, s)
    tk = _pick(k, (1408, 1152, 512))
    nk = k // tk

    def body(a_ref, w_ref, x_ref, g_ref, r_ref, dx_ref, dg_ref, acc_ref):
        i, kk = pl.program_id(0), pl.program_id(1)

        @pl.when(kk == 0)
        def _():
            acc_ref[...] = jnp.zeros_like(acc_ref)

        @pl.when((i == 0) & (kk == 0))
        def _():
            dg_ref[...] = jnp.zeros_like(dg_ref)

        acc_ref[...] += _dot(a_ref[...], w_ref[...])

        @pl.when(kk == nk - 1)
        def _():
            xv = x_ref[...]
            r = _rstd(xv)
            y = xv * r
            dh = acc_ref[...]
            dy = dh * g_ref[...]
            dx_ref[...] = r_ref[...] + r * (dy - y * jnp.mean(dy * y, axis=-1, keepdims=True))
            dg_ref[...] += jnp.sum(dh * y, axis=0, keepdims=True)

    return pl.pallas_call(
        body, grid=(s // tm, nk),
        in_specs=[pl.BlockSpec((tm, tk), lambda i, kk: (i, kk)), pl.BlockSpec((tk, d), lambda i, kk: (kk, 0)),
                  pl.BlockSpec((tm, d), lambda i, kk: (i, 0)), pl.BlockSpec((1, d), lambda i, kk: (0, 0)),
                  pl.BlockSpec((tm, d), lambda i, kk: (i, 0))],
        out_specs=[pl.BlockSpec((tm, d), lambda i, kk: (i, 0)), pl.BlockSpec((1, d), lambda i, kk: (0, 0))],
        out_shape=[jax.ShapeDtypeStruct((s, d), F32), jax.ShapeDtypeStruct((1, d), F32)],
        scratch_shapes=[pltpu.VMEM((tm, d), F32)],
        compiler_params=_cparams(("arbitrary", "arbitrary")), name=name)(a, wt, x, g, dres)


def swiglu_matmul(gu, w, x1, name):
    s = gu.shape[0]
    d = w.shape[1]
    tm = min(512, s)

    def body(gu_ref, w_ref, x_ref, o_ref):
        acc = x_ref[...]
        for j in range(N_FF_BLKS):
            gt = gu_ref[:, j * FF_BLK:(j + 1) * FF_BLK].astype(F32)
            up = gu_ref[:, D_FF + j * FF_BLK:D_FF + (j + 1) * FF_BLK].astype(F32)
            act = (gt * _sigmoid(gt) * up).astype(BF16)
            acc += _dot(act, w_ref[j * FF_BLK:(j + 1) * FF_BLK, :])
        o_ref[...] = acc

    return pl.pallas_call(
        body, grid=(s // tm,),
        in_specs=[pl.BlockSpec((tm, 2 * D_FF), lambda i: (i, 0)), pl.BlockSpec((D_FF, d), lambda i: (0, 0)),
                  pl.BlockSpec((tm, d), lambda i: (i, 0))],
        out_specs=pl.BlockSpec((tm, d), lambda i: (i, 0)),
        out_shape=jax.ShapeDtypeStruct((s, d), F32),
        compiler_params=_cparams(("parallel",)), name=name)(gu, w, x1)


def swiglu_bwd(dx2, gu, w, name, ex=None):
    s, d = dx2.shape
    tm = min(256, s)

    def body(dx_ref, gu_ref, w_ref, dgu_ref, act_ref):
        dx = dx_ref[...].astype(BF16)
        for j in range(N_FF_BLKS):
            g_cols = slice(j * FF_BLK, (j + 1) * FF_BLK)
            u_cols = slice(D_FF + j * FF_BLK, D_FF + (j + 1) * FF_BLK)
            dact = _dot_nt(dx, w_ref[j * FF_BLK:(j + 1) * FF_BLK, :])
            gt = gu_ref[:, g_cols].astype(F32)
            up = gu_ref[:, u_cols].astype(F32)
            sg = _sigmoid(gt)
            act_ref[:, j * FF_BLK:(j + 1) * FF_BLK] = (gt * sg * up).astype(BF16)
            dgu_ref[:, g_cols] = (dact * up * (sg * (1.0 + gt * (1.0 - sg)))).astype(BF16)
            dgu_ref[:, u_cols] = (dact * gt * sg).astype(BF16)

    return _carried_call(
        body, ex, (s // tm,),
        [pl.BlockSpec((tm, d), lambda i: (i, 0)), pl.BlockSpec((tm, 2 * D_FF), lambda i: (i, 0)),
         pl.BlockSpec((D_FF, d), lambda i: (0, 0))],
        [pl.BlockSpec((tm, 2 * D_FF), lambda i: (i, 0)), pl.BlockSpec((tm, D_FF), lambda i: (i, 0))],
        [jax.ShapeDtypeStruct((s, 2 * D_FF), BF16), jax.ShapeDtypeStruct((s, D_FF), BF16)], [],
        ("arbitrary",), name, (dx2, gu, w))


def loss_kernel(y, tgt, name):
    s, d = y.shape
    tm = min(512, s)

    def body(y_ref, t_ref, l_ref, dy_ref):
        @pl.when(pl.program_id(0) == 0)
        def _():
            l_ref[...] = jnp.zeros_like(l_ref)

        err = y_ref[...] - t_ref[...]
        dy_ref[...] = err * (1.0 / d)
        l_ref[...] += jnp.sum(jnp.sum(err * err, axis=1, keepdims=True), axis=0, keepdims=True)

    return pl.pallas_call(
        body, grid=(s // tm,),
        in_specs=[pl.BlockSpec((tm, d), lambda i: (i, 0)), pl.BlockSpec((tm, d), lambda i: (i, 0))],
        out_specs=[pl.BlockSpec((8, 128), lambda i: (0, 0)), pl.BlockSpec((tm, d), lambda i: (i, 0))],
        out_shape=[jax.ShapeDtypeStruct((8, 128), F32), jax.ShapeDtypeStruct((s, d), F32)],
        compiler_params=_cparams(("arbitrary",)), name=name)(y, tgt)


def _split3(v):
    a1 = v.astype(BF16)
    r1 = v - a1.astype(F32)
    a2 = r1.astype(BF16)
    a3 = (r1 - a2.astype(F32)).astype(BF16)
    return a1, a2, a3


def forget_fwd(h, wt_in, b, name):
    s, d = h.shape
    tm = min(512, s)

    def body(h_ref, w_ref, b_ref, z_ref, c_ref, carry_ref):
        @pl.when(pl.program_id(0) == 0)
        def _():
            carry_ref[...] = jnp.zeros_like(carry_ref)

        z = _dot_nt(h_ref[...], w_ref[...]) + b_ref[...]
        z_ref[...] = z
        logf = jnp.minimum(z, 0.0) - jnp.log(1.0 + jnp.exp(-jnp.abs(z)))
        row = lax.broadcasted_iota(jnp.int32, (tm, tm), 0)
        col = lax.broadcasted_iota(jnp.int32, (tm, tm), 1)
        tri = (row >= col).astype(BF16)
        a1, a2, a3 = _split3(logf)
        c = _dot(tri, a1) + _dot(tri, a2) + _dot(tri, a3) + carry_ref[...]
        c_ref[...] = c
        carry_ref[...] = c[tm - 1:tm, :]

    return pl.pallas_call(
        body, grid=(s // tm,),
        in_specs=[pl.BlockSpec((tm, d), lambda i: (i, 0)), pl.BlockSpec((128, d), lambda i: (N_MAIN // 128, 0)),
                  pl.BlockSpec((1, 128), lambda i: (0, 0))],
        out_specs=[pl.BlockSpec((tm, 128), lambda i: (i, 0)), pl.BlockSpec((tm, 128), lambda i: (i, 0))],
        out_shape=[jax.ShapeDtypeStruct((s, 128), F32), jax.ShapeDtypeStruct((s, 128), F32)],
        scratch_shapes=[pltpu.VMEM((1, 128), F32)],
        compiler_params=_cparams(("arbitrary",)), name=name)(h, wt_in, b)


def forget_bwd(dc, z, dproj, name):
    s = dc.shape[0]
    tm = min(512, s)
    nt = s // tm

    def body(dc_ref, z_ref, dp_ref, dz_ref, db_ref, carry_ref):
        @pl.when(pl.program_id(0) == 0)
        def _():
            carry_ref[...] = jnp.zeros_like(carry_ref)
            db_ref[...] = jnp.zeros_like(db_ref)

        row = lax.broadcasted_iota(jnp.int32, (tm, tm), 0)
        col = lax.broadcasted_iota(jnp.int32, (tm, tm), 1)
        tri = (col >= row).astype(BF16)
        a1, a2, a3 = _split3(dc_ref[...])
        dlogf = _dot(tri, a1) + _dot(tri, a2) + _dot(tri, a3) + carry_ref[...]
        carry_ref[...] = dlogf[0:1, :]
        dz = dlogf * (1.0 - _sigmoid(z_ref[...]))
        dz_ref[...] = dz.astype(BF16)
        db_ref[...] += jnp.sum(dz, axis=0, keepdims=True)

    return pl.pallas_call(
        body, grid=(nt,),
        in_specs=[pl.BlockSpec((tm, 128), lambda i: (nt - 1 - i, 0)), pl.BlockSpec((tm, 128), lambda i: (nt - 1 - i, 0)),
                  pl.BlockSpec(memory_space=pl.ANY)],
        out_specs=[pl.BlockSpec((tm, 128), lambda i: (nt - 1 - i, N_MAIN // 128)), pl.BlockSpec((1, 128), lambda i: (0, 0))],
        out_shape=[jax.ShapeDtypeStruct(dproj.shape, BF16), jax.ShapeDtypeStruct((1, 128), F32)],
        scratch_shapes=[pltpu.VMEM((1, 128), F32)], input_output_aliases={2: 0},
        compiler_params=_cparams(("arbitrary",)), name=name)(dc, z, dproj)


HEAD_GROUP = 4
LANE_C = 64
LANE_ONE = 67


def _lanes():
    lane = lax.broadcasted_iota(jnp.int32, (1, 128), 1)
    return lane, lane < HEAD_DIM


def _half_mean(t, lo):
    s_lo = jnp.sum(jnp.where(lo, t, 0.0), axis=-1, keepdims=True)
    s_hi = jnp.sum(jnp.where(lo, 0.0, t), axis=-1, keepdims=True)
    return jnp.where(lo, s_lo, s_hi) * (1.0 / HEAD_DIM)


def _lane_col(t, lane, idx):
    return jnp.sum(jnp.where(lane == idx, t, 0.0), axis=-1, keepdims=True)


def _swap_halves(t):
    return pltpu.roll(t, HEAD_DIM, 1)


def _causal(s_blk, tq, tk):
    row = lax.broadcasted_iota(jnp.int32, (tq, tk), 0)
    col = lax.broadcasted_iota(jnp.int32, (tq, tk), 1)
    return jnp.where(row >= col, s_blk, -jnp.inf)


def attn_prep(proj, c, gq2, gk2, name):
    s = proj.shape[0]
    tm = min(512, s)
    first = N_REST // 128

    def body(q_ref, k_ref, v_ref, c_ref, gq_ref, gk_ref, qa_ref, ka_ref, va_ref, vt_ref):
        j = pl.program_id(1)
        lane, lo = _lanes()

        def normed(ref, g):
            t = ref[...].astype(F32)
            return t * lax.rsqrt(_half_mean(t * t, lo) + EPS) * g

        qn = normed(q_ref, gq_ref[...] * ATTN_SCALE)
        kn = normed(k_ref, gk_ref[...])
        vv = v_ref[...].astype(F32)
        cv = c_ref[...]
        one_q = jnp.where((lane >= LANE_ONE) & (lane < LANE_ONE + 3), 1.0, 0.0)
        one_k = jnp.where((lane >= LANE_C) & (lane < LANE_C + 3), 1.0, 0.0)
        one_v = jnp.where(lane == LANE_C, 1.0, 0.0)
        for e in range(2):
            pick = (lambda t: t) if e == 0 else _swap_halves
            pieces = [p.astype(F32) for p in _split3(_lane_col(cv, lane, 2 * j + e))]
            ext_q, ext_k = one_q, one_k
            for i, p in enumerate(pieces):
                ext_q = jnp.where(lane == LANE_C + i, p, ext_q)
                ext_k = jnp.where(lane == LANE_ONE + i, -p, ext_k)
            qa_ref[e] = jnp.where(lo, pick(qn), ext_q).astype(BF16)
            ka_ref[e] = jnp.where(lo, pick(kn), ext_k).astype(BF16)
            va = jnp.where(lo, pick(vv), one_v)
            va_ref[e] = va.astype(BF16)
            vt_ref[e] = va.T.astype(BF16)

    tile = lambda base: pl.BlockSpec((tm, 128), lambda i, j: (i, base + j))
    vec = pl.BlockSpec((1, 128), lambda i, j: (0, 0))
    out = pl.BlockSpec((2, tm, 128), lambda i, j: (j, i, 0))
    return pl.pallas_call(
        body, grid=(s // tm, HEADS // 2),
        in_specs=[tile(first), tile(first + 4), tile(first + 8), pl.BlockSpec((tm, 128), lambda i, j: (i, 0)), vec, vec],
        out_specs=[out, out, out, pl.BlockSpec((2, 128, tm), lambda i, j: (j, 0, i))],
        out_shape=[jax.ShapeDtypeStruct((HEADS, s, 128), BF16)] * 3 + [jax.ShapeDtypeStruct((HEADS, 128, s), BF16)],
        compiler_params=_cparams(("parallel", "arbitrary")), name=name)(proj, proj, proj, c, gq2, gk2)


def attn_fwd(q, k, v, ccol, crow, gq, gk, name):
    hh, s, hd = q.shape
    tq = tk = min(512, s)
    nq = s // tq

    def body(q_ref, k_ref, v_ref, cc_ref, cr_ref, gq_ref, gk_ref, o_ref, lse_ref, qn_ref, m_ref, l_ref, acc_ref):
        qi, ki = pl.program_id(1), pl.program_id(2)

        @pl.when(ki == 0)
        def _():
            qn_ref[...] = _qk_hat(q_ref, gq_ref, ATTN_SCALE)
            m_ref[...] = jnp.full_like(m_ref, -jnp.inf)
            l_ref[...] = jnp.zeros_like(l_ref)
            acc_ref[...] = jnp.zeros_like(acc_ref)

        @pl.when(ki <= qi)
        def _():
            kn = _qk_hat(k_ref, gk_ref, 1.0)
            sb = _dot_nt(qn_ref[...], kn) + (cc_ref[...] - cr_ref[...])
            sb = _causal(sb, qi, ki, tq, tk)
            m_new = jnp.maximum(m_ref[...], jnp.max(sb, axis=-1, keepdims=True))
            alpha = jnp.exp(m_ref[...] - m_new)
            p = jnp.exp(sb - m_new)
            l_ref[...] = alpha * l_ref[...] + jnp.sum(p, axis=-1, keepdims=True)
            acc_ref[...] = alpha * acc_ref[...] + _dot(p.astype(BF16), v_ref[...])
            m_ref[...] = m_new

        @pl.when(ki == qi)
        def _():
            o_ref[...] = (acc_ref[...] / l_ref[...]).astype(BF16)
            lse_ref[...] = m_ref[...] + jnp.log(l_ref[...])

    qspec = pl.BlockSpec((None, tq, hd), lambda h, i, j: (h, i, 0))
    kspec = pl.BlockSpec((None, tk, hd), lambda h, i, j: (h, jnp.minimum(i, j), 0))
    gspec = pl.BlockSpec((1, hd), lambda h, i, j: (0, 0))
    return pl.pallas_call(
        body, grid=(hh, nq, nq),
        in_specs=[qspec, kspec, kspec,
                  pl.BlockSpec((None, tq, 1), lambda h, i, j: (h, i, 0)),
                  pl.BlockSpec((None, 1, tk), lambda h, i, j: (h, 0, jnp.minimum(i, j))), gspec, gspec],
        out_specs=[qspec, pl.BlockSpec((None, tq, 1), lambda h, i, j: (h, i, 0))],
        out_shape=[jax.ShapeDtypeStruct((hh, s, hd), BF16), jax.ShapeDtypeStruct((hh, s, 1), F32)],
        scratch_shapes=[pltpu.VMEM((tq, hd), BF16), pltpu.VMEM((tq, 1), F32), pltpu.VMEM((tq, 1), F32),
                        pltpu.VMEM((tq, hd), F32)],
        compiler_params=_cparams(("parallel", "parallel", "arbitrary")), name=name)(q, k, v, ccol, crow, gq, gk)


def attn_bwd_dq(q, k, v, o, do, lse, ccol, crow, gq, gk, name):
    hh, s, hd = q.shape
    tq = tk = min(512, s)
    nq = s // tq

    def body(q_ref, k_ref, v_ref, o_ref, do_ref, lse_ref, cc_ref, cr_ref, gq_ref, gk_ref,
             dq_ref, dcc_ref, dg_ref, qn_ref, dl_ref, acc_ref, dca_ref):
        h, qi, ki = pl.program_id(0), pl.program_id(1), pl.program_id(2)

        @pl.when((h == 0) & (qi == 0) & (ki == 0))
        def _():
            dg_ref[...] = jnp.zeros_like(dg_ref)

        @pl.when(ki == 0)
        def _():
            qn_ref[...] = _qk_hat(q_ref, gq_ref, ATTN_SCALE)
            dl_ref[...] = jnp.sum(do_ref[...].astype(F32) * o_ref[...].astype(F32), axis=-1, keepdims=True)
            acc_ref[...] = jnp.zeros_like(acc_ref)
            dca_ref[...] = jnp.zeros_like(dca_ref)

        @pl.when(ki <= qi)
        def _():
            kn = _qk_hat(k_ref, gk_ref, 1.0)
            sb = _dot_nt(qn_ref[...], kn) + (cc_ref[...] - cr_ref[...])
            p = jnp.exp(_causal(sb, qi, ki, tq, tk) - lse_ref[...])
            dp = _dot_nt(do_ref[...], v_ref[...])
            ds = p * (dp - dl_ref[...])
            acc_ref[...] += _dot(ds.astype(BF16), kn)
            dca_ref[...] += jnp.sum(ds, axis=-1, keepdims=True)

        @pl.when(ki == qi)
        def _():
            dq, dg = _norm_bwd(q_ref[...].astype(F32), gq_ref[...], acc_ref[...], ATTN_SCALE)
            dq_ref[...] = dq.astype(BF16)
            dcc_ref[...] = dca_ref[...]
            dg_ref[...] += dg

    qspec = pl.BlockSpec((None, tq, hd), lambda h, i, j: (h, i, 0))
    kspec = pl.BlockSpec((None, tk, hd), lambda h, i, j: (h, jnp.minimum(i, j), 0))
    cspec = pl.BlockSpec((None, tq, 1), lambda h, i, j: (h, i, 0))
    gspec = pl.BlockSpec((1, hd), lambda h, i, j: (0, 0))
    return pl.pallas_call(
        body, grid=(hh, nq, nq),
        in_specs=[qspec, kspec, kspec, qspec, qspec, cspec, cspec,
                  pl.BlockSpec((None, 1, tk), lambda h, i, j: (h, 0, jnp.minimum(i, j))), gspec, gspec],
        out_specs=[qspec, cspec, gspec],
        out_shape=[jax.ShapeDtypeStruct((hh, s, hd), BF16), jax.ShapeDtypeStruct((hh, s, 1), F32),
                   jax.ShapeDtypeStruct((1, hd), F32)],
        scratch_shapes=[pltpu.VMEM((tq, hd), BF16), pltpu.VMEM((tq, 1), F32), pltpu.VMEM((tq, hd), F32),
                        pltpu.VMEM((tq, 1), F32)],
        compiler_params=_cparams(("arbitrary", "arbitrary", "arbitrary")), name=name)(
            q, k, v, o, do, lse, ccol, crow, gq, gk)


def attn_bwd_dkv(q, k, v, o, do, lse, ccol, crow, gq, gk, name):
    hh, s, hd = q.shape
    tq = tk = min(512, s)
    nq = s // tq

    def body(q_ref, k_ref, v_ref, o_ref, do_ref, lse_ref, cc_ref, cr_ref, gq_ref, gk_ref,
             dk_ref, dv_ref, dcr_ref, dg_ref, kn_ref, dka_ref, dva_ref, dca_ref):
        h, ki, qi = pl.program_id(0), pl.program_id(1), pl.program_id(2)

        @pl.when((h == 0) & (ki == 0) & (qi == 0))
        def _():
            dg_ref[...] = jnp.zeros_like(dg_ref)

        @pl.when(qi == 0)
        def _():
            kn_ref[...] = _qk_hat(k_ref, gk_ref, 1.0)
            dka_ref[...] = jnp.zeros_like(dka_ref)
            dva_ref[...] = jnp.zeros_like(dva_ref)
            dca_ref[...] = jnp.zeros_like(dca_ref)

        @pl.when(qi >= ki)
        def _():
            qn = _qk_hat(q_ref, gq_ref, ATTN_SCALE)
            do = do_ref[...]
            delta = jnp.sum(do.astype(F32) * o_ref[...].astype(F32), axis=-1, keepdims=True)
            sb = _dot_nt(qn, kn_ref[...]) + (cc_ref[...] - cr_ref[...])
            p = jnp.exp(_causal(sb, qi, ki, tq, tk) - lse_ref[...])
            dva_ref[...] += _dot_tn(p.astype(BF16), do)
            ds = p * (_dot_nt(do, v_ref[...]) - delta)
            dka_ref[...] += _dot_tn(ds.astype(BF16), qn)
            dca_ref[...] += jnp.sum(ds, axis=0, keepdims=True)

        @pl.when(qi == nq - 1)
        def _():
            dk, dg = _norm_bwd(k_ref[...].astype(F32), gk_ref[...], dka_ref[...], 1.0)
            dk_ref[...] = dk.astype(BF16)
            dv_ref[...] = dva_ref[...].astype(BF16)
            dcr_ref[...] = dca_ref[...]
            dg_ref[...] += dg

    kspec = pl.BlockSpec((None, tk, hd), lambda h, j, i: (h, j, 0))
    qspec = pl.BlockSpec((None, tq, hd), lambda h, j, i: (h, jnp.maximum(i, j), 0))
    cspec = pl.BlockSpec((None, tq, 1), lambda h, j, i: (h, jnp.maximum(i, j), 0))
    rspec = pl.BlockSpec((None, 1, tk), lambda h, j, i: (h, 0, j))
    gspec = pl.BlockSpec((1, hd), lambda h, j, i: (0, 0))
    return pl.pallas_call(
        body, grid=(hh, nq, nq),
        in_specs=[qspec, kspec, kspec, qspec, qspec, cspec, cspec, rspec, gspec, gspec],
        out_specs=[kspec, kspec, rspec, gspec],
        out_shape=[jax.ShapeDtypeStruct((hh, s, hd), BF16), jax.ShapeDtypeStruct((hh, s, hd), BF16),
                   jax.ShapeDtypeStruct((hh, 1, s), F32), jax.ShapeDtypeStruct((1, hd), F32)],
        scratch_shapes=[pltpu.VMEM((tk, hd), BF16), pltpu.VMEM((tk, hd), F32), pltpu.VMEM((tk, hd), F32),
                        pltpu.VMEM((1, tk), F32)],
        compiler_params=_cparams(("arbitrary", "arbitrary", "arbitrary")), name=name)(
            q, k, v, o, do, lse, ccol, crow, gq, gk)


def _carry(ex, n_in, n_out, n_scratch, grid):
    n_xin, n_xout = (len(ex.inputs), len(ex.out_shapes)) if ex else (0, 0)

    def split(refs):
        ins, xins = refs[:n_in], refs[n_in:n_in + n_xin]
        rest = refs[n_in + n_xin:]
        outs, xouts = rest[:n_out], rest[n_out:n_out + n_xout]
        rest = rest[n_out + n_xout:]
        return ins + outs + rest[:n_scratch], (xins, xouts, rest[n_scratch:])

    def first():
        return functools.reduce(lambda a, b: a & b, [pl.program_id(d) == 0 for d in range(len(grid))])

    def last():
        return functools.reduce(lambda a, b: a & b, [pl.program_id(d) == grid[d] - 1 for d in range(len(grid))])

    return split, first, last


def _carried_call(body, ex, grid, in_specs, out_specs, out_shape, scratch, sem, name, operands):
    any_spec = pl.BlockSpec(memory_space=pl.ANY)
    split, first, last = _carry(ex, len(in_specs), len(out_specs), len(scratch), grid)

    def carried(*refs):
        own, xrefs = split(refs)
        if ex:
            @pl.when(first())
            def _():
                ex.start(*xrefs)

        body(*own)
        if ex:
            @pl.when(last())
            def _():
                ex.drain(*xrefs)

    n_xin = len(ex.inputs) if ex else 0
    results = pl.pallas_call(
        carried, grid=grid, in_specs=list(in_specs) + [any_spec] * n_xin,
        out_specs=list(out_specs) + [any_spec] * (len(ex.out_shapes) if ex else 0),
        out_shape=list(out_shape) + (list(ex.out_shapes) if ex else []),
        input_output_aliases={len(in_specs) + i: len(out_specs) + o for i, o in ex.aliases.items()} if ex else {},
        scratch_shapes=list(scratch) + (ex.scratch if ex else []),
        compiler_params=_cparams(sem), name=name)(*operands, *(ex.inputs if ex else []))
    return results[:len(out_specs)], results[len(out_specs):]


def _tri_rows(t, n):
    qi = sum(jnp.where(t >= r * (r + 1) // 2, 1, 0) for r in range(1, n))
    return qi, t - qi * (qi + 1) // 2


def _tri_cols(t, n):
    ki = sum(jnp.where(t >= r * n - r * (r - 1) // 2, 1, 0) for r in range(1, n))
    return ki, ki + t - (ki * n - ki * (ki - 1) // 2)


def _causal_t(st_blk, tk, tq):
    key = lax.broadcasted_iota(jnp.int32, (tk, tq), 0)
    qry = lax.broadcasted_iota(jnp.int32, (tk, tq), 1)
    return jnp.where(qry >= key, st_blk, -jnp.inf)


def attn_forward(qa, ka, vt, name, ex=None):
    hh, s, _ = qa.shape
    tq = tk = min(512, s)
    nq = s // tq
    grp = HEAD_GROUP

    def body(q_ref, k_ref, vt_ref, o_ref, lse_ref, m_ref, acc_ref):
        qi, ki = _tri_rows(pl.program_id(1), nq)

        @pl.when(ki == 0)
        def _():
            m_ref[...] = jnp.full_like(m_ref, -jnp.inf)
            acc_ref[...] = jnp.zeros_like(acc_ref)

        def step(masked):
            nxt = _dot_nt(k_ref[0], q_ref[0])
            for g in range(grp):
                st = nxt
                if g + 1 < grp:
                    nxt = _dot_nt(k_ref[g + 1], q_ref[g + 1])
                if masked:
                    st = _causal_t(st, tk, tq)
                m_old = m_ref[g]
                m_new = jnp.maximum(m_old, jnp.max(st, axis=0, keepdims=True))
                pt = jnp.exp(st - m_new).astype(BF16)
                acc_ref[g] = jnp.exp(m_old - m_new) * acc_ref[g] + _dot(vt_ref[g], pt)
                m_ref[g] = m_new

        @pl.when(ki < qi)
        def _():
            step(False)

        @pl.when(ki == qi)
        def _():
            step(True)
            for g in range(grp):
                acc = acc_ref[g]
                denom = acc[LANE_C:LANE_C + 1, :]
                o_ref[g] = (acc / denom).T.astype(BF16)
                lse_ref[g] = m_ref[g] + jnp.log(denom)

    qspec = pl.BlockSpec((grp, tq, 128), lambda h, t: (h, _tri_rows(t, nq)[0], 0))
    kspec = pl.BlockSpec((grp, tk, 128), lambda h, t: (h, _tri_rows(t, nq)[1], 0))
    vspec = pl.BlockSpec((grp, 128, tk), lambda h, t: (h, 0, _tri_rows(t, nq)[1]))
    lspec = pl.BlockSpec((grp, 1, tq), lambda h, t: (h, 0, _tri_rows(t, nq)[0]))
    return _carried_call(
        body, ex, (hh // grp, nq * (nq + 1) // 2), [qspec, kspec, vspec], [qspec, lspec],
        [jax.ShapeDtypeStruct((hh, s, 128), BF16), jax.ShapeDtypeStruct((hh, 1, s), F32)],
        [pltpu.VMEM((grp, 1, tq), F32), pltpu.VMEM((grp, 128, tq), F32)],
        ("arbitrary", "arbitrary"), name, (qa, ka, vt))


def attn_backward(qa, ka, va, oa, doa, lse, name, ex=None):
    hh, s, _ = qa.shape
    tq = tk = min(512, s)
    nq = s // tq
    grp = HEAD_GROUP

    def body(q_ref, k_ref, v_ref, o_ref, do_ref, lse_ref, dq_ref, dk_ref, dv_ref, dka_ref, dva_ref):
        ki, qi = _tri_cols(pl.program_id(1), nq)

        @pl.when(pl.program_id(1) == 0)
        def _():
            dq_ref[...] = jnp.zeros_like(dq_ref)

        @pl.when(qi == ki)
        def _():
            dka_ref[...] = jnp.zeros_like(dka_ref)
            dva_ref[...] = jnp.zeros_like(dva_ref)

        def step(masked):
            rows = pl.ds(pl.multiple_of(qi * tq, tq), tq)
            products = lambda g: (_dot_nt(k_ref[g], q_ref[g]), _dot_nt(v_ref[g], do_ref[g]))
            nxt = products(0)
            for g in range(grp):
                st, dpt = nxt
                if g + 1 < grp:
                    nxt = products(g + 1)
                q, k, do = q_ref[g], k_ref[g], do_ref[g]
                if masked:
                    st = _causal_t(st, tk, tq)
                pt = jnp.exp(st - lse_ref[g])
                delta = jnp.sum((do.astype(F32) * o_ref[g].astype(F32)).T, axis=0, keepdims=True)
                dst = (pt * (dpt - delta)).astype(BF16)
                dva_ref[g] += _dot(pt.astype(BF16), do)
                dka_ref[g] += _dot(dst, q)
                dq_ref[g, rows, :] += _dot_tn(dst, k)

        @pl.when(qi > ki)
        def _():
            step(False)

        @pl.when(qi == ki)
        def _():
            step(True)

        @pl.when(qi == nq - 1)
        def _():
            dk_ref[...] = dka_ref[...]
            dv_ref[...] = dva_ref[...].astype(BF16)

    qspec = pl.BlockSpec((grp, tq, 128), lambda h, t: (h, _tri_cols(t, nq)[1], 0))
    lspec = pl.BlockSpec((grp, 1, tq), lambda h, t: (h, 0, _tri_cols(t, nq)[1]))
    kspec = pl.BlockSpec((grp, tk, 128), lambda h, t: (h, _tri_cols(t, nq)[0], 0))
    return _carried_call(
        body, ex, (hh // grp, nq * (nq + 1) // 2), [qspec, kspec, kspec, qspec, qspec, lspec],
        [pl.BlockSpec((grp, s, 128), lambda h, t: (h, 0, 0)), kspec, kspec],
        [jax.ShapeDtypeStruct((hh, s, 128), F32), jax.ShapeDtypeStruct((hh, s, 128), F32),
         jax.ShapeDtypeStruct((hh, s, 128), BF16)],
        [pltpu.VMEM((grp, tk, 128), F32), pltpu.VMEM((grp, tk, 128), F32)],
        ("arbitrary", "arbitrary"), name, (qa, ka, va, oa, doa, lse))


def attn_post(dqa, dka, dva, proj, gq2, gk2, dproj, name):
    s = proj.shape[0]
    tm = min(256, s)

    def body(dq_ref, dk_ref, dv_ref, q_ref, k_ref, gq_ref, gk_ref, dp_any, dp_ref, dc_ref, dgq_ref, dgk_ref):
        lane, lo = _lanes()

        @pl.when(pl.program_id(0) == 0)
        def _():
            dgq_ref[...] = jnp.zeros_like(dgq_ref)
            dgk_ref[...] = jnp.zeros_like(dgk_ref)

        def pair(ref, j):
            return jnp.where(lo, ref[2 * j].astype(F32), _swap_halves(ref[2 * j + 1].astype(F32)))

        def norm_bwd(raw, g, dhat, scale):
            r = lax.rsqrt(_half_mean(raw * raw, lo) + EPS)
            y = raw * r
            dy = dhat * (g * scale)
            return r * (dy - y * _half_mean(dy * y, lo)), jnp.sum(dhat * y, axis=0, keepdims=True) * scale

        dc = jnp.zeros((tm, 128), F32)
        for j in range(HEADS // 2):
            cols = slice(128 * j, 128 * (j + 1))
            dq, dgq = norm_bwd(q_ref[:, cols].astype(F32), gq_ref[...], pair(dq_ref, j), ATTN_SCALE)
            dk, dgk = norm_bwd(k_ref[:, cols].astype(F32), gk_ref[...], pair(dk_ref, j), 1.0)
            dgq_ref[...] += dgq
            dgk_ref[...] += dgk
            dp_ref[:, cols] = dq.astype(BF16)
            dp_ref[:, D_ATTN + 128 * j:D_ATTN + 128 * (j + 1)] = dk.astype(BF16)
            dp_ref[:, 2 * D_ATTN + 128 * j:2 * D_ATTN + 128 * (j + 1)] = pair(dv_ref, j).astype(BF16)
            for e in range(2):
                h = 2 * j + e
                col = _lane_col(dq_ref[h], lane, LANE_C) - _lane_col(dk_ref[h], lane, LANE_ONE)
                dc = jnp.where(lane == h, col, dc)
        dp_ref[:, 3 * D_ATTN:] = jnp.zeros((tm, DPROJ_TAIL - 3 * D_ATTN), BF16)
        dc_ref[...] = dc

    heads = lambda: pl.BlockSpec((HEADS, tm, 128), lambda i: (0, i, 0))
    vec = pl.BlockSpec((1, 128), lambda i: (0, 0))
    first = N_REST // D_ATTN
    return pl.pallas_call(
        body, grid=(s // tm,),
        in_specs=[heads(), heads(), heads(), pl.BlockSpec((tm, D_ATTN), lambda i: (i, first)),
                  pl.BlockSpec((tm, D_ATTN), lambda i: (i, first + 1)), vec, vec, pl.BlockSpec(memory_space=pl.ANY)],
        out_specs=[pl.BlockSpec((tm, DPROJ_TAIL), lambda i: (i, N_REST // DPROJ_TAIL)),
                   pl.BlockSpec((tm, 128), lambda i: (i, 0)), vec, vec],
        out_shape=[jax.ShapeDtypeStruct(dproj.shape, BF16), jax.ShapeDtypeStruct((s, 128), F32),
                   jax.ShapeDtypeStruct((1, 128), F32), jax.ShapeDtypeStruct((1, 128), F32)],
        input_output_aliases={7: 0},
        compiler_params=_cparams(("arbitrary",)), name=name)(dqa, dka, dva, proj, proj, gq2, gk2, dproj)


def _pool_groups(tm):
    gid = lax.broadcasted_iota(jnp.int32, (1, D_POOL), 1) // (D_POOL // 4)
    win = jnp.where(gid == 0, 2.0, jnp.where(gid == 1, 4.0, jnp.where(gid == 2, 8.0, 16.0)))
    return gid, win


def _by_group(gid, v2, v4, v8, v16):
    return jnp.where(gid == 0, v2, jnp.where(gid == 1, v4, jnp.where(gid == 2, v8, v16)))


def _branches(rest_ref, halo_ref, a_ref, wa_ref, wc_ref, wp_ref, sc_ref, cw_ref, ti, tm):
    f = lambda v: v.astype(F32)
    cx, cb, cc, px = f(rest_ref[:, 0:256]), f(rest_ref[:, 256:512]), f(rest_ref[:, 512:768]), f(rest_ref[:, 768:1024])
    live = jnp.where(ti > 0, 1.0, 0.0)
    hz = f(halo_ref[:, 0:256]) * f(halo_ref[:, 512:768]) * live
    hp = f(halo_ref[:, 768:1024]) * live
    z = cc * cx
    zf = jnp.concatenate([hz, z], axis=0)
    z1 = pltpu.roll(zf, 1, 0)[HALO:]
    z2 = pltpu.roll(zf, 2, 0)[HALO:]
    cw = cw_ref[...]
    conv = cw[2:3] * z + cw[1:2] * z1 + cw[0:1] * z2
    uc = cb * conv
    pf = jnp.concatenate([hp, px], axis=0)
    s2 = pf + pltpu.roll(pf, 1, 0)
    s4 = s2 + pltpu.roll(s2, 2, 0)
    s8 = s4 + pltpu.roll(s4, 4, 0)
    s16 = s8 + pltpu.roll(s8, 8, 0)
    gid, win = _pool_groups(tm)
    t = (ti * tm + lax.broadcasted_iota(jnp.int32, (tm, 1), 0)).astype(F32)
    inv = 1.0 / jnp.minimum(t + 1.0, win)
    dpool = _by_group(gid, s2[HALO:], s4[HALO:], s8[HALO:], s16[HALO:]) * inv - px
    _, lo = _lanes()
    a_tok = [jnp.where(lo, f(a_ref[2 * j]), _swap_halves(f(a_ref[2 * j + 1]))).astype(BF16) for j in range(HEADS // 2)]
    y_attn = _dot(a_tok[0], wa_ref[0:128, :])
    for j in range(1, HEADS // 2):
        y_attn += _dot(a_tok[j], wa_ref[128 * j:128 * (j + 1), :])
    y_conv = _dot(uc.astype(BF16), wc_ref[...])
    y_pool_raw = _dot(dpool.astype(BF16), wp_ref[...])
    sg = [_sigmoid(f(rest_ref[:, 1024 + i * D_MODEL:1024 + (i + 1) * D_MODEL])) for i in range(3)]
    return dict(cx=cx, cb=cb, cc=cc, z=z, z1=z1, z2=z2, conv=conv, uc=uc, dpool=dpool, inv=inv, gid=gid, a_tok=a_tok,
                y_attn=y_attn, y_conv=y_conv, y_pool_raw=y_pool_raw, sg=sg, cw=cw)


def _mix_specs(tm, ti_of):
    blocks_per_tile = tm // HALO
    return [
        pl.BlockSpec((tm, N_REST), lambda i: (ti_of(i), 0)),
        pl.BlockSpec((HALO, 1024), lambda i: (jnp.maximum(ti_of(i) * blocks_per_tile - 1, 0), 0)),
        pl.BlockSpec((HEADS, tm, 128), lambda i: (0, ti_of(i), 0)),
        pl.BlockSpec((D_ATTN, D_MODEL), lambda i: (0, 0)),
        pl.BlockSpec((D_CONV, D_MODEL), lambda i: (0, 0)),
        pl.BlockSpec((D_POOL, D_MODEL), lambda i: (0, 0)),
        pl.BlockSpec((1, D_MODEL), lambda i: (0, 0)),
        pl.BlockSpec((8, D_CONV), lambda i: (0, 0)),
    ]


def mix_fwd(proj, a, x, wa, wc, wp, scale, cw, wo, name):
    s = x.shape[0]
    tm = min(256, s)

    def body(rest_ref, halo_ref, a_ref, wa_ref, wc_ref, wp_ref, sc_ref, cw_ref, wo_ref, x_ref, o_ref):
        b = _branches(rest_ref, halo_ref, a_ref, wa_ref, wc_ref, wp_ref, sc_ref, cw_ref, pl.program_id(0), tm)
        merged = b["sg"][0] * b["y_attn"] + b["sg"][1] * b["y_conv"] + b["sg"][2] * (b["y_pool_raw"] * sc_ref[...])
        o_ref[...] = x_ref[...] + _dot(merged.astype(BF16), wo_ref[...])

    return pl.pallas_call(
        body, grid=(s // tm,),
        in_specs=_mix_specs(tm, lambda i: i) + [pl.BlockSpec((D_MODEL, D_MODEL), lambda i: (0, 0)),
                                                 pl.BlockSpec((tm, D_MODEL), lambda i: (i, 0))],
        out_specs=pl.BlockSpec((tm, D_MODEL), lambda i: (i, 0)),
        out_shape=jax.ShapeDtypeStruct((s, D_MODEL), F32),
        compiler_params=_cparams(("parallel",)), name=name)(proj, proj, a, wa, wc, wp, scale, cw, wo, x)


def mix_bwd(proj, a, dx1, wa, wc, wp, scale, cw, wo, name):
    s = dx1.shape[0]
    tm = min(256, s)
    nt = s // tm
    ti_of = lambda i: nt - 1 - i
    n = tm + HALO

    def body(rest_ref, halo_ref, a_ref, wa_ref, wc_ref, wp_ref, sc_ref, cw_ref, wo_ref,
             dx_ref, dp_ref, da_ref, at_ref, mg_ref, dya_ref, dyc_ref, dyp_ref, uc_ref, dd_ref, dsc_ref, dcw_ref,
             cdc_ref, cde_ref):
        i = pl.program_id(0)
        ti = ti_of(i)

        @pl.when(i == 0)
        def _():
            cdc_ref[...] = jnp.zeros_like(cdc_ref)
            cde_ref[...] = jnp.zeros_like(cde_ref)
            dsc_ref[...] = jnp.zeros_like(dsc_ref)
            dcw_ref[...] = jnp.zeros_like(dcw_ref)

        b = _branches(rest_ref, halo_ref, a_ref, wa_ref, wc_ref, wp_ref, sc_ref, cw_ref, ti, tm)
        sg, sc = b["sg"], sc_ref[...]
        y_pool = b["y_pool_raw"] * sc
        merged = sg[0] * b["y_attn"] + sg[1] * b["y_conv"] + sg[2] * y_pool
        mg_ref[...] = merged.astype(BF16)
        dm = _dot_nt(dx_ref[...].astype(BF16), wo_ref[...])
        for j, y in enumerate((b["y_attn"], b["y_conv"], y_pool)):
            dp_ref[:, 1024 + j * D_MODEL:1024 + (j + 1) * D_MODEL] = (dm * y * sg[j] * (1.0 - sg[j])).astype(BF16)
        dya = (dm * sg[0]).astype(BF16)
        dya_ref[...] = dya
        _, lo = _lanes()
        for j in range(HEADS // 2):
            at_ref[:, 128 * j:128 * (j + 1)] = b["a_tok"][j]
            da = _dot_nt(dya, wa_ref[128 * j:128 * (j + 1), :])
            da_ref[2 * j] = jnp.where(lo, da, 0.0).astype(BF16)
            da_ref[2 * j + 1] = jnp.where(lo, _swap_halves(da), 0.0).astype(BF16)
        dyc = (dm * sg[1]).astype(BF16)
        dyc_ref[...] = dyc
        duc = _dot_nt(dyc, wc_ref[...])
        dyp = dm * sg[2]
        dsc_ref[...] += jnp.sum(dyp * b["y_pool_raw"], axis=0, keepdims=True)
        dypr = (dyp * sc).astype(BF16)
        dyp_ref[...] = dypr
        ddp = _dot_nt(dypr, wp_ref[...])
        uc_ref[...] = b["uc"].astype(BF16)
        dd_ref[...] = b["dpool"].astype(BF16)

        dconv = duc * b["cb"]
        dp_ref[:, 256:512] = (duc * b["conv"]).astype(BF16)
        dcf = jnp.concatenate([dconv, cdc_ref[...]], axis=0)
        cw = b["cw"]
        dz = cw[2:3] * dconv + cw[1:2] * pltpu.roll(dcf, n - 1, 0)[:tm] + cw[0:1] * pltpu.roll(dcf, n - 2, 0)[:tm]
        dp_ref[:, 0:256] = (dz * b["cc"]).astype(BF16)
        dp_ref[:, 512:768] = (dz * b["cx"]).astype(BF16)
        dcw_ref[0:1, :] += jnp.sum(dconv * b["z2"], axis=0, keepdims=True)
        dcw_ref[1:2, :] += jnp.sum(dconv * b["z1"], axis=0, keepdims=True)
        dcw_ref[2:3, :] += jnp.sum(dconv * b["z"], axis=0, keepdims=True)
        cdc_ref[...] = dconv[:HALO]

        e = ddp * b["inv"]
        ef = jnp.concatenate([e, cde_ref[...]], axis=0)
        r2 = ef + pltpu.roll(ef, n - 1, 0)
        r4 = r2 + pltpu.roll(r2, n - 2, 0)
        r8 = r4 + pltpu.roll(r4, n - 4, 0)
        r16 = r8 + pltpu.roll(r8, n - 8, 0)
        dp_ref[:, 768:1024] = (_by_group(b["gid"], r2[:tm], r4[:tm], r8[:tm], r16[:tm]) - ddp).astype(BF16)
        cde_ref[...] = e[:HALO]

    tile = lambda w: pl.BlockSpec((tm, w), lambda i: (ti_of(i), 0))
    whole = lambda r, c: pl.BlockSpec((r, c), lambda i: (0, 0))
    bf = lambda w: jax.ShapeDtypeStruct((s, w), BF16)
    return pl.pallas_call(
        body, grid=(nt,),
        in_specs=_mix_specs(tm, ti_of) + [whole(D_MODEL, D_MODEL), tile(D_MODEL)],
        out_specs=[tile(N_REST), pl.BlockSpec((HEADS, tm, 128), lambda i: (0, ti_of(i), 0)), tile(D_ATTN),
                   tile(D_MODEL), tile(D_MODEL), tile(D_MODEL), tile(D_MODEL),
                   tile(D_CONV), tile(D_POOL), whole(1, D_MODEL), whole(8, D_CONV)],
        out_shape=[bf(DPROJ_COLS), jax.ShapeDtypeStruct((HEADS, s, 128), BF16), bf(D_ATTN),
                   bf(D_MODEL), bf(D_MODEL), bf(D_MODEL), bf(D_MODEL), bf(D_CONV), bf(D_POOL),
                   jax.ShapeDtypeStruct((1, D_MODEL), F32), jax.ShapeDtypeStruct((8, D_CONV), F32)],
        scratch_shapes=[pltpu.VMEM((HALO, D_CONV), F32), pltpu.VMEM((HALO, D_POOL), F32)],
        compiler_params=_cparams(("arbitrary",)), name=name)(proj, proj, a, wa, wc, wp, scale, cw, wo, dx1)


def _adamw_math(w, g, m, v):
    m = ADAM_B1 * m + (1.0 - ADAM_B1) * g
    v = ADAM_B2 * v + (1.0 - ADAM_B2) * (g * g)
    m_hat = m / (1.0 - ADAM_B1 ** ADAM_STEP)
    v_hat = v / (1.0 - ADAM_B2 ** ADAM_STEP)
    delta = -ADAM_LR * (m_hat / (jnp.sqrt(v_hat) + ADAM_EPS) + ADAM_WD * w)
    return delta, m, v


ADAMW_PARTS_BLOCK_BYTES = 4 * 2 ** 20


def _row_tile(rows, cols, copies, itemsize):
    row_bytes = copies * (-(-cols // 128) * 128) * itemsize
    fits = [t for t in range(16, rows + 1, 16) if rows % t == 0 and t * row_bytes <= ADAMW_PARTS_BLOCK_BYTES]
    return max(fits) if fits else rows


def pair_sum(blocks, stage, me, name):
    n_slots, rows, cols = stage.shape
    tr = _row_tile(rows, cols, 1, 4)

    def body(me_ref, a_ref, b_ref, o_ref):
        o_ref[...] = (a_ref[...].astype(F32) + b_ref[...].astype(F32)).astype(BF16)

    slot = pl.BlockSpec((None, tr, cols), lambda i, r, me_ref: (i, r, 0))
    return pl.pallas_call(
        body, out_shape=jax.ShapeDtypeStruct(stage.shape, BF16),
        grid_spec=pltpu.PrefetchScalarGridSpec(
            num_scalar_prefetch=1, grid=(n_slots, rows // tr),
            in_specs=[pl.BlockSpec((None, tr, cols), lambda i, r, me_ref: (me_ref[0] ^ (2 * i), r, 0)), slot],
            out_specs=slot),
        compiler_params=_cparams(("parallel", "parallel")), name=name)(me.reshape(1), blocks, stage)


def adamw_sum(parts, w, m, v, name):
    layers, rows, cols = w.shape
    n_parts = parts.shape[1]
    if rows % 16 == 0:
        tr, tc = _row_tile(rows, cols, n_parts, parts.dtype.itemsize), cols
    else:
        tr, tc = rows, _pick(cols, (256, 128))

    def body(p_ref, w_ref, m_ref, v_ref, g_ref, d_ref, nm_ref, nv_ref):
        g = p_ref[0].astype(F32)
        for i in range(1, n_parts):
            g = g + p_ref[i].astype(F32)
        g_ref[...] = g
        d_ref[...], nm_ref[...], nv_ref[...] = _adamw_math(w_ref[...], g, m_ref[...], v_ref[...])

    spec = pl.BlockSpec((None, tr, tc), lambda l, i, j: (l, i, j))
    return pl.pallas_call(
        body, grid=(layers, rows // tr, cols // tc),
        in_specs=[pl.BlockSpec((None, n_parts, tr, tc), lambda l, i, j: (l, 0, i, j)), spec, spec, spec],
        out_specs=[spec] * 4, out_shape=[jax.ShapeDtypeStruct((layers, rows, cols), F32)] * 4,
        compiler_params=_cparams(("parallel", "parallel", "parallel")), name=name)(parts, w, m, v)


def _me():
    return lax.axis_index("x"), lax.axis_index("y"), lax.axis_index("c")


N_PEERS = N_DEV - 1


def all_gather(shards, name):
    n = len(shards)
    any_spec = pl.BlockSpec(memory_space=pl.ANY)

    def body(*refs):
        x_refs, out_refs = refs[:n], refs[n:2 * n]
        send_sems, recv_sems, local_sems = refs[2 * n:]
        x, y, c = _me()
        me, sibling = (x, y, c), (x, y, 1 - c)
        chips = [(1 - x, y), (x, 1 - y), (1 - x, 1 - y)]

        def copy(t, k, block, to, from_input=False):
            slot = out_refs[t].at[4 * block[0] + 2 * block[1] + block[2]]
            return pltpu.make_async_remote_copy(
                src_ref=x_refs[t] if from_input else slot, dst_ref=slot, send_sem=send_sems.at[N_PEERS * t + k],
                recv_sem=recv_sems.at[N_PEERS * t + k], device_id=to, device_id_type=pl.DeviceIdType.MESH)

        mine = [pltpu.make_async_copy(x_refs[t], out_refs[t].at[4 * x + 2 * y + c], local_sems.at[t]) for t in range(n)]
        started = []
        for t in range(n):
            mine[t].start()
            started.append(copy(t, 0, me, sibling, from_input=True))
            started += [copy(t, 1 + j, me, (*chip, c), from_input=True) for j, chip in enumerate(chips)]
        for cp in started:
            cp.start()
        for j, chip in enumerate(chips):
            for t in range(n):
                copy(t, 1 + j, (*chip, c), me).wait_recv()
                fwd = copy(t, 4 + j, (*chip, c), sibling)
                fwd.start()
                started.append(fwd)
        for t in range(n):
            copy(t, 0, sibling, me).wait_recv()
            for j, chip in enumerate(chips):
                copy(t, 4 + j, (*chip, 1 - c), me).wait_recv()
        for cp in started:
            cp.wait_send()
        for cp in mine:
            cp.wait()

    return pl.pallas_call(
        body, out_shape=[jax.ShapeDtypeStruct((N_DEV,) + s.shape, s.dtype) for s in shards],
        in_specs=[any_spec] * n, out_specs=[any_spec] * n,
        scratch_shapes=[pltpu.SemaphoreType.DMA((N_PEERS * n,)), pltpu.SemaphoreType.DMA((N_PEERS * n,)),
                        pltpu.SemaphoreType.DMA((n,))],
        name=name)(*shards)


SIBLING = 1
OTHER_CHIPS = (2, 4, 6)
SAME_CORE = (0,) + OTHER_CHIPS


class Exchange:
    def __init__(self, inputs, out_shapes, aliases, copies, local=()):
        self.inputs, self.out_shapes, self.aliases = list(inputs), list(out_shapes), aliases
        self._copies, self._local = list(copies), list(local)
        self.scratch = [pltpu.SemaphoreType.DMA((len(self._copies),)), pltpu.SemaphoreType.DMA((len(self._copies),)),
                        pltpu.SemaphoreType.DMA((max(len(self._local), 1),))]

    def _build(self, ins, outs, sems):
        send_sems, recv_sems, local_sems = sems
        x, y, c = _me()
        me = 4 * x + 2 * y + c
        local = [functools.partial(pltpu.make_async_copy, src(ins, outs, me), dst(outs, me), local_sems.at[i])
                 for i, (src, dst) in enumerate(self._local)]
        sends, recvs = [], []
        for i, (mask, src, dst) in enumerate(self._copies):
            px, py, pc = x ^ ((mask >> 2) & 1), y ^ ((mask >> 1) & 1), c ^ (mask & 1)
            pair = dict(send_sem=send_sems.at[i], recv_sem=recv_sems.at[i], device_id_type=pl.DeviceIdType.MESH)
            sends.append(functools.partial(
                pltpu.make_async_remote_copy, src_ref=src(ins, outs, me), dst_ref=dst(outs, me), device_id=(px, py, pc), **pair))
            recvs.append(functools.partial(
                pltpu.make_async_remote_copy, src_ref=src(ins, outs, me), dst_ref=dst(outs, me ^ mask), device_id=(x, y, c), **pair))
        return local, sends, recvs

    def start(self, ins, outs, sems):
        local, sends, _ = self._build(ins, outs, sems)
        for make in local + sends:
            make().start()

    def drain(self, ins, outs, sems):
        local, sends, recvs = self._build(ins, outs, sems)
        for make in recvs:
            make().wait_recv()
        for make in sends:
            make().wait_send()
        for make in local:
            make().wait()


def _bind(fn, *args):
    return functools.partial(fn, *args)


def gather_over_ici(shards):
    copies = [(mask, _bind(lambda t, ins, outs, me: ins[t], t), _bind(lambda t, outs, sender: outs[t].at[sender], t))
              for t in range(len(shards)) for mask in OTHER_CHIPS]
    local = [(_bind(lambda t, ins, outs, me: ins[t], t), _bind(lambda t, outs, me: outs[t].at[me], t))
             for t in range(len(shards))]
    return Exchange(shards, [jax.ShapeDtypeStruct((N_DEV,) + s.shape, s.dtype) for s in shards], {}, copies, local)


def gather_over_d2d(gathered):
    copies = [(SIBLING, _bind(lambda t, m, ins, outs, me: outs[t].at[me ^ m], t, m),
               _bind(lambda t, m, outs, sender: outs[t].at[sender ^ m], t, m))
              for t in range(len(gathered)) for m in SAME_CORE]
    return Exchange(gathered, [jax.ShapeDtypeStruct(g.shape, g.dtype) for g in gathered],
                    {t: t for t in range(len(gathered))}, copies)


def scatter_over_d2d(blocks):
    copies = [(SIBLING, _bind(lambda t, m, ins, outs, me: ins[t].at[me ^ SIBLING ^ m], t, m),
               _bind(lambda t, i, outs, sender: outs[t].at[i], t, i))
              for t in range(len(blocks)) for i, m in enumerate(SAME_CORE)]
    return Exchange(blocks, [jax.ShapeDtypeStruct((len(SAME_CORE),) + b.shape[1:], b.dtype) for b in blocks], {}, copies)


def scatter_over_ici(pair_sums, bufs, layer):
    n = len(pair_sums)
    copies = [(m, _bind(lambda t, i, ins, outs, me: ins[t].at[i], t, i),
               _bind(lambda t, i, outs, sender: outs[t].at[layer, i], t, i))
              for t in range(n) for i, m in enumerate(SAME_CORE) if m]
    local = [(_bind(lambda t, ins, outs, me: ins[t].at[0], t), _bind(lambda t, outs, me: outs[t].at[layer, 0], t))
             for t in range(n)]
    return Exchange(list(pair_sums) + list(bufs), [jax.ShapeDtypeStruct(b.shape, b.dtype) for b in bufs],
                    {n + t: t for t in range(n)}, copies, local)


def run_exchange(ex, name):
    any_spec = pl.BlockSpec(memory_space=pl.ANY)
    n_in, n_out = len(ex.inputs), len(ex.out_shapes)

    def body(*refs):
        ins, outs, sems = refs[:n_in], refs[n_in:n_in + n_out], refs[n_in + n_out:]
        ex.start(ins, outs, sems)
        ex.drain(ins, outs, sems)

    return pl.pallas_call(
        body, out_shape=ex.out_shapes, in_specs=[any_spec] * n_in, out_specs=[any_spec] * n_out,
        input_output_aliases=ex.aliases, scratch_shapes=ex.scratch, name=name)(*ex.inputs)


MATRICES = ("w_in", "w_attn_out", "w_conv_out", "pool_w", "w_o", "w_ffn_in", "w_ffn_out")
TRANSPOSED = ("w_in", "w_ffn_in")
MIXER_PART, FFN_PART = slice(0, 5), slice(5, 7)
SHARD_INFO = {
    "w_in": ((DEPTH, D_IN // N_DEV, D_MODEL), 1),
    "w_attn_out": ((DEPTH, D_ATTN, D_MODEL // N_DEV), 2),
    "w_conv_out": ((DEPTH, D_CONV, D_MODEL // N_DEV), 2),
    "pool_w": ((DEPTH, 4, 64, 256 // N_DEV), 3),
    "w_o": ((DEPTH, D_MODEL // N_DEV, D_MODEL), 1),
    "w_ffn_in": ((DEPTH, 2 * D_FF // N_DEV, D_MODEL), 1),
    "w_ffn_out": ((DEPTH, D_FF // N_DEV, D_MODEL), 1),
}


def _handled(name, t):
    return jnp.transpose(t, (0, 2, 1)) if name in TRANSPOSED else t
VECTORS = ("norm_mix_g", "forget_b", "q_norm_g", "k_norm_g", "pool_scale", "norm_ffn_g")
VECTOR_SHAPES = {"norm_mix_g": (DEPTH, D_MODEL), "forget_b": (DEPTH, HEADS), "q_norm_g": (DEPTH, HEAD_DIM),
                 "k_norm_g": (DEPTH, HEAD_DIM), "pool_scale": (DEPTH, D_MODEL), "norm_ffn_g": (DEPTH, D_MODEL)}
CONV_W_FULL = (DEPTH, 3, D_CONV)


def _size(shape):
    n = 1
    for v in shape:
        n *= v
    return n


def _pack(arrays, rows, cols):
    flat = jnp.concatenate([a.reshape(-1) for a in arrays])
    return jnp.pad(flat, (0, rows * cols - flat.shape[0])).reshape(rows, cols)


def _unpack(packed, shapes):
    flat, out, off = packed.reshape(-1), [], 0
    for shp in shapes:
        out.append(flat[off:off + _size(shp)].reshape(shp))
        off += _size(shp)
    return out


def _join_shards(stacked, axis):
    moved = jnp.moveaxis(stacked, 0, axis)
    shp = list(moved.shape)
    shp[axis:axis + 2] = [shp[axis] * shp[axis + 1]]
    return moved.reshape(shp)


def _cut_shards(full, axis):
    shp = list(full.shape)
    shp[axis:axis + 1] = [N_DEV, shp[axis] // N_DEV]
    return jnp.moveaxis(full.reshape(shp), axis, 0)


def _regroup_w_in(wt):
    pad = jnp.zeros((N_FULL - N_MAIN - HEADS, wt.shape[1]), wt.dtype)
    return jnp.concatenate([wt[1544:2568], wt[2568:5640], wt[0:1536], wt[1536:1544], pad], axis=0)


def _ungroup_w_in(wpt):
    return jnp.concatenate([wpt[4096:5632], wpt[5632:5640], wpt[0:1024], wpt[1024:4096]], axis=0)


def _pool_block_diag(w):
    out = jnp.zeros((D_POOL, D_MODEL), w.dtype)
    for g in range(4):
        out = lax.dynamic_update_slice(out, w[g], (g * 64, g * 256))
    return out


def _pool_from_block_diag(wbd):
    return jnp.stack([wbd[g * 64:(g + 1) * 64, g * 256:(g + 1) * 256] for g in range(4)])


def _layer_weights(mats, vec, conv_w, l):
    wp = _pool_block_diag(mats["pool_w"])
    row = lambda v: v.reshape(1, -1)
    fb = jnp.zeros((1, 128), F32).at[0, :HEADS].set(vec["forget_b"][l])
    cw = jnp.zeros((8, D_CONV), F32).at[:3].set(conv_w[l])
    twice = lambda v: jnp.tile(v.reshape(1, -1), (1, 2))
    return dict(
        wt_in=_regroup_w_in(mats["w_in"]), wt_ffn_in=mats["w_ffn_in"], w_ffn_out=mats["w_ffn_out"],
        wa=mats["w_attn_out"], wc=mats["w_conv_out"], wp=wp, wo=mats["w_o"],
        g_mix=row(vec["norm_mix_g"][l]), g_ffn=row(vec["norm_ffn_g"][l]), gq2=twice(vec["q_norm_g"][l]),
        gk2=twice(vec["k_norm_g"][l]), scale=row(vec["pool_scale"][l]), fb=fb, cw=cw)


def _layer_fwd(x, w, l, comm):
    (proj, h), half_mix = norm_matmul(x, w["g_mix"], w["wt_in"], N_MAIN, f"in_proj_{l}", comm.gather_ici(l + 1, MIXER_PART))
    z, c = forget_fwd(h, w["wt_in"], w["fb"], f"forget_fwd_{l}")
    qa, ka, va, vt = attn_prep(proj, c, w["gq2"], w["gk2"], f"attn_prep_{l}")
    (oa, lse), half_ffn = attn_forward(qa, ka, vt, f"attn_fwd_{l}", comm.gather_ici(l + 1, FFN_PART))
    half = list(half_mix) + list(half_ffn)
    x1 = mix_fwd(proj, oa, x, w["wa"], w["wc"], w["wp"], w["scale"], w["cw"], w["wo"], f"mix_fwd_{l}")
    (gu, h2), gathered = norm_matmul(x1, w["g_ffn"], w["wt_ffn_in"], 2 * D_FF, f"ffn_in_{l}", comm.gather_d2d(l + 1, half))
    x2 = swiglu_matmul(gu, w["w_ffn_out"], x1, f"ffn_out_{l}")
    saved = dict(x=x, proj=proj, h=h, z=z, qa=qa, ka=ka, va=va, oa=oa, lse=lse, x1=x1, gu=gu, h2=h2)
    return x2, saved, gathered


def _layer_bwd(dx2, sv, w, l, comm):
    g = {}
    (dgu, act), stage = swiglu_bwd(dx2, sv["gu"], w["w_ffn_out"], f"ffn_out_bwd_{l}", comm.scatter_d2d(l + 1))
    sums = comm.pair_sums(l + 1, stage)
    g["w_ffn_out"] = tn_matmul(act, dx2, f"dw_ffn_out_{l}")
    g["w_ffn_in"] = tn_matmul(dgu, sv["h2"], f"dw_ffn_in_{l}")
    dx1, dg = matmul_normbwd(dgu, w["wt_ffn_in"], sv["x1"], w["g_ffn"], dx2, f"ffn_in_bwd_{l}")
    g["norm_ffn_g"] = dg[0]

    (dproj, doa, a_tok, merged, dya, dyc, dyp, uc, dd, dscale, dcw) = mix_bwd(
        sv["proj"], sv["oa"], dx1, w["wa"], w["wc"], w["wp"], w["scale"], w["cw"], w["wo"], f"mix_bwd_{l}")
    g["w_o"] = tn_matmul(merged, dx1, f"dw_o_{l}")
    g["w_attn_out"] = tn_matmul(a_tok, dya, f"dw_attn_out_{l}")
    g["w_conv_out"] = tn_matmul(uc, dyc, f"dw_conv_out_{l}")
    g["pool_w"] = _pool_from_block_diag(tn_matmul(dd, dyp, f"dw_pool_{l}"))
    g["pool_scale"] = dscale[0]
    g["conv_w"] = dcw[:3]

    (dqa, dka, dva), got = attn_backward(sv["qa"], sv["ka"], sv["va"], sv["oa"], doa, sv["lse"], f"attn_bwd_{l}",
                                         comm.scatter_ici(l + 1, sums))
    comm.scattered(got)
    dproj, dc, dgq, dgk = attn_post(dqa, dka, dva, sv["proj"], w["gq2"], w["gk2"], dproj, f"attn_post_{l}")
    g["q_norm_g"] = dgq[0, :HEAD_DIM] + dgq[0, HEAD_DIM:]
    g["k_norm_g"] = dgk[0, :HEAD_DIM] + dgk[0, HEAD_DIM:]
    dproj, db = forget_bwd(dc, sv["z"], dproj, f"forget_bwd_{l}")
    g["forget_b"] = db[0, :HEADS]

    g["w_in"] = _ungroup_w_in(tn_matmul(dproj, sv["h"], f"dw_in_{l}", m_cols=N_FULL))
    dx, dg = matmul_normbwd(dproj, w["wt_in"], sv["x"], w["g_mix"], dx1, f"in_proj_bwd_{l}", k=N_FULL)
    g["norm_mix_g"] = dg[0]
    comm.grads(l, g)
    return dx


def _local_step(x, tgt, comm):
    ws, saved = [], []
    w = comm.weights(0, None)
    for l in range(DEPTH):
        ws.append(w)
        x, sv, gathered = _layer_fwd(x, w, l, comm)
        saved.append(sv)
        if l + 1 < DEPTH:
            w = comm.weights(l + 1, gathered)
    sq, dx = loss_kernel(x, tgt, "loss")
    for l in reversed(range(DEPTH)):
        dx = _layer_bwd(dx, saved[l], ws[l], l, comm)
    comm.finish()
    return sq[0, 0], dx


def kernel(x, norm_mix_g, w_in, forget_b, q_norm_g, k_norm_g, w_attn_out, conv_w, w_conv_out, pool_w, pool_scale, w_o, norm_ffn_g, w_ffn_in, w_ffn_out, loss_target, m_norm_mix_g, m_w_in, m_forget_b, m_q_norm_g, m_k_norm_g, m_w_attn_out, m_conv_w, m_w_conv_out, m_pool_w, m_pool_scale, m_w_o, m_norm_ffn_g, m_w_ffn_in, m_w_ffn_out, v_norm_mix_g, v_w_in, v_forget_b, v_q_norm_g, v_k_norm_g, v_w_attn_out, v_conv_w, v_w_conv_out, v_pool_w, v_pool_scale, v_w_o, v_norm_ffn_g, v_w_ffn_in, v_w_ffn_out):
    w = dict(norm_mix_g=norm_mix_g, w_in=w_in, forget_b=forget_b, q_norm_g=q_norm_g, k_norm_g=k_norm_g,
             w_attn_out=w_attn_out, conv_w=conv_w, w_conv_out=w_conv_out, pool_w=pool_w, pool_scale=pool_scale,
             w_o=w_o, norm_ffn_g=norm_ffn_g, w_ffn_in=w_ffn_in, w_ffn_out=w_ffn_out)
    m = dict(norm_mix_g=m_norm_mix_g, w_in=m_w_in, forget_b=m_forget_b, q_norm_g=m_q_norm_g, k_norm_g=m_k_norm_g,
             w_attn_out=m_w_attn_out, conv_w=m_conv_w, w_conv_out=m_w_conv_out, pool_w=m_pool_w,
             pool_scale=m_pool_scale, w_o=m_w_o, norm_ffn_g=m_norm_ffn_g, w_ffn_in=m_w_ffn_in, w_ffn_out=m_w_ffn_out)
    v = dict(norm_mix_g=v_norm_mix_g, w_in=v_w_in, forget_b=v_forget_b, q_norm_g=v_q_norm_g, k_norm_g=v_k_norm_g,
             w_attn_out=v_w_attn_out, conv_w=v_conv_w, w_conv_out=v_w_conv_out, pool_w=v_pool_w,
             pool_scale=v_pool_scale, w_o=v_w_o, norm_ffn_g=v_norm_ffn_g, w_ffn_in=v_w_ffn_in, w_ffn_out=v_w_ffn_out)
    me = 4 * lax.axis_index("x") + 2 * lax.axis_index("y") + lax.axis_index("c")
    layer_shard = {n: SHARD_INFO[n][0][1:] for n in MATRICES}
    cut_axis = {n: SHARD_INFO[n][1] - 1 for n in MATRICES}

    conv_g = all_gather([_pack([conv_w], 8, 128)], "gather_conv_w")[0]
    conv_full = _join_shards(jnp.stack([_unpack(conv_g[i], [conv_w.shape])[0] for i in range(N_DEV)]), 2)
    vec = {n: w[n] for n in VECTORS}

    rc = {n: (_size(layer_shard[n][:-1]), layer_shard[n][-1]) for n in MATRICES}

    class Comm:
        bufs = [lax.empty((DEPTH, len(SAME_CORE)) + layer_shard[n], BF16) for n in MATRICES]
        blocks = [None] * DEPTH
        small_g = [None] * DEPTH

        @staticmethod
        def shards(l):
            return [_handled(n, w[n])[l].astype(BF16) for n in MATRICES]

        @staticmethod
        def gather_ici(l, part):
            return gather_over_ici(Comm.shards(l)[part]) if l < DEPTH else None

        @staticmethod
        def gather_d2d(l, half):
            return gather_over_d2d(half) if l < DEPTH else None

        @staticmethod
        def weights(l, gathered):
            if l == 0:
                gathered = all_gather(Comm.shards(0), "gather_0")
            mats = {n: _join_shards(t, cut_axis[n]) for n, t in zip(MATRICES, gathered)}
            return _layer_weights(mats, vec, conv_full, l)

        @staticmethod
        def grads(l, g):
            Comm.small_g[l] = g
            Comm.blocks[l] = [_cut_shards(g[n], cut_axis[n]) for n in MATRICES]

        @staticmethod
        def scatter_d2d(l):
            return scatter_over_d2d(Comm.blocks[l]) if l < DEPTH else None

        @staticmethod
        def pair_sums(l, stage):
            if l >= DEPTH:
                return None
            return [pair_sum(b.reshape((N_DEV,) + rc[n]), s.reshape((len(SAME_CORE),) + rc[n]), me,
                             f"pair_sum_{n}_{l}").reshape(s.shape) for n, b, s in zip(MATRICES, Comm.blocks[l], stage)]

        @staticmethod
        def scatter_ici(l, sums):
            return scatter_over_ici(sums, Comm.bufs, l) if l < DEPTH else None

        @staticmethod
        def scattered(results):
            if results:
                Comm.bufs = list(results)

        @staticmethod
        def finish():
            stage = run_exchange(Comm.scatter_d2d(0), "scatter_d2d_0")
            Comm.scattered(run_exchange(Comm.scatter_ici(0, Comm.pair_sums(0, stage)), "scatter_ici_0"))

    small_g, received = Comm.small_g, Comm
    sq, dx = _local_step(x[0], loss_target[0], Comm)
    loss = lax.psum(0.5 * sq / D_MODEL, ("x", "y", "c"))

    big = {}
    for n, parts in zip(MATRICES, received.bufs):
        outs = adamw_sum(parts.reshape((DEPTH, len(SAME_CORE)) + rc[n]),
                         *[_handled(n, d[n]).reshape((DEPTH,) + rc[n]) for d in (w, m, v)], f"adamw_{n}")
        big[n] = [_handled(n, t.reshape((DEPTH,) + layer_shard[n])) for t in outs]

    small_shapes = [VECTOR_SHAPES[n] for n in VECTORS] + [CONV_W_FULL]
    stacked = [jnp.stack([small_g[l][n] for l in range(DEPTH)]) for n in VECTORS + ("conv_w",)]
    sparts = all_gather([_pack(stacked, SMALL_ROWS, 128)], "gather_vector_grads")[0]
    col0 = me * (D_CONV // N_DEV)
    place = lambda t: lax.dynamic_update_slice(jnp.zeros(CONV_W_FULL, F32), t, (0, 0, col0))
    spacked = [_pack([d[n] for n in VECTORS] + [place(d["conv_w"])], SMALL_ROWS, 128)[None] for d in (w, m, v)]
    small = [_unpack(t[0], small_shapes) for t in adamw_sum(sparts[None], *spacked, "adamw_vectors")]

    def result(kind):
        out = {n: big[n][kind] for n in MATRICES}
        out.update({n: small[kind][j] for j, n in enumerate(VECTORS)})
        out["conv_w"] = lax.dynamic_slice(small[kind][len(VECTORS)], (0, 0, col0), conv_w.shape)
        return [out[n] for n in w]

    return (loss, dx[None], *result(0), *result(1), *result(2), *result(3))
```

```python
import functools

import jax
import jax.numpy as jnp
from jax import lax
from jax.experimental import pallas as pl
from jax.experimental.pallas import tpu as pltpu

F32 = jnp.float32
BF16 = jnp.bfloat16

N_DEV = 8
DEPTH = 4
D_MODEL = 1024
HEAD_DIM = 64
HEADS = 8
D_ATTN = 512
D_CONV = 256
D_POOL = 256
D_FF = 2816
D_IN = 5640
EPS = 1e-6
ATTN_SCALE = HEAD_DIM ** -0.5

N_REST = 4096
N_MAIN = 5632
N_FULL = 5760
DPROJ_TAIL = 2048
DPROJ_COLS = N_REST + DPROJ_TAIL
FF_BLK = 256
N_FF_BLKS = D_FF // FF_BLK
HALO = 16

ADAM_LR = 0.001
ADAM_B1 = 0.9
ADAM_B2 = 0.999
ADAM_EPS = 1e-08
ADAM_WD = 0.01
ADAM_STEP = 10

PACK_COLS = 1024
PACK_ROWS = 8192
SMALL_ROWS = 128

VMEM_LIMIT = 48 * 2 ** 20


def _cparams(sem, vmem=None):
    return pltpu.CompilerParams(dimension_semantics=sem, vmem_limit_bytes=vmem or VMEM_LIMIT)


def _pick(n, cands):
    for c in cands:
        if n % c == 0:
            return c
    raise ValueError(f"no tile for {n}")


def _sigmoid(v):
    return 1.0 / (1.0 + jnp.exp(-v))


def _rstd(v):
    return lax.rsqrt(jnp.mean(v * v, axis=-1, keepdims=True) + EPS)


def _dot(a, b):
    return jnp.dot(a, b, preferred_element_type=F32)


def _dot_tn(a, b):
    return lax.dot_general(a, b, (((0,), (0,)), ((), ())), preferred_element_type=F32)


def _dot_nt(a, b):
    return lax.dot_general(a, b, (((1,), (1,)), ((), ())), preferred_element_type=F32)


def norm_matmul(x, g, wt, n_cols, name, ex=None):
    s, d = x.shape
    tm, tn = min(1024, s), _pick(n_cols, (1408, 512))

    def body(x_ref, g_ref, w_ref, o_ref, h_ref):
        @pl.when(pl.program_id(1) == 0)
        def _():
            xv = x_ref[...]
            h_ref[...] = (xv * _rstd(xv) * g_ref[...]).astype(BF16)

        o_ref[...] = _dot_nt(h_ref[...], w_ref[...]).astype(BF16)

    return _carried_call(
        body, ex, (s // tm, n_cols // tn),
        [pl.BlockSpec((tm, d), lambda i, j: (i, 0)), pl.BlockSpec((1, d), lambda i, j: (0, 0)),
         pl.BlockSpec((tn, d), lambda i, j: (j, 0))],
        [pl.BlockSpec((tm, tn), lambda i, j: (i, j)), pl.BlockSpec((tm, d), lambda i, j: (i, 0))],
        [jax.ShapeDtypeStruct((s, n_cols), BF16), jax.ShapeDtypeStruct((s, d), BF16)], [],
        ("arbitrary", "arbitrary"), name, (x, g, wt))


def tn_matmul(a, b, name, m_cols=None):
    t = a.shape[0]
    m = m_cols or a.shape[1]
    n = b.shape[1]
    tk = min(1024, t)
    tmm = _pick(m, (1408, 1152, 1024, 512, 256))
    tn = _pick(n, (1408, 1152, 1024, 512, 128))
    nk = t // tk

    def body(a_ref, b_ref, o_ref, acc_ref):
        @pl.when(pl.program_id(2) == 0)
        def _():
            acc_ref[...] = jnp.zeros_like(acc_ref)

        acc_ref[...] += _dot_tn(a_ref[...].astype(BF16), b_ref[...].astype(BF16))

        @pl.when(pl.program_id(2) == nk - 1)
        def _():
            o_ref[...] = acc_ref[...].astype(BF16)

    return pl.pallas_call(
        body, grid=(m // tmm, n // tn, nk),
        in_specs=[pl.BlockSpec((tk, tmm), lambda i, j, k: (k, i)), pl.BlockSpec((tk, tn), lambda i, j, k: (k, j))],
        out_specs=pl.BlockSpec((tmm, tn), lambda i, j, k: (i, j)),
        out_shape=jax.ShapeDtypeStruct((m, n), BF16), scratch_shapes=[pltpu.VMEM((tmm, tn), F32)],
        compiler_params=_cparams(("parallel", "parallel", "arbitrary")), name=name)(a, b)


def matmul_normbwd(a, wt, x, g, dres, name, k=None, ex=None):
    s = a.shape[0]
    k = k or a.shape[1]
    d = wt.shape[1]
    tm = min(1024, s)
    tk = _pick(k, (1408, 1152, 512))
    nk = k // tk

    def body(a_ref, w_ref, x_ref, g_ref, r_ref, dx_ref, dg_ref, acc_ref):
        i, kk = pl.program_id(0), pl.program_id(1)

        @pl.when(kk == 0)
        def _():
            acc_ref[...] = jnp.zeros_like(acc_ref)

        @pl.when((i == 0) & (kk == 0))
        def _():
            dg_ref[...] = jnp.zeros_like(dg_ref)

        acc_ref[...] += _dot(a_ref[...], w_ref[...])

        @pl.when(kk == nk - 1)
        def _():
            xv = x_ref[...]
            r = _rstd(xv)
            y = xv * r
            dh = acc_ref[...]
            dy = dh * g_ref[...]
            dx_ref[...] = r_ref[...] + r * (dy - y * jnp.mean(dy * y, axis=-1, keepdims=True))
            dg_ref[...] += jnp.sum(dh * y, axis=0, keepdims=True)

    return _carried_call(
        body, ex, (s // tm, nk),
        [pl.BlockSpec((tm, tk), lambda i, kk: (i, kk)), pl.BlockSpec((tk, d), lambda i, kk: (kk, 0)),
         pl.BlockSpec((tm, d), lambda i, kk: (i, 0)), pl.BlockSpec((1, d), lambda i, kk: (0, 0)),
         pl.BlockSpec((tm, d), lambda i, kk: (i, 0))],
        [pl.BlockSpec((tm, d), lambda i, kk: (i, 0)), pl.BlockSpec((1, d), lambda i, kk: (0, 0))],
        [jax.ShapeDtypeStruct((s, d), F32), jax.ShapeDtypeStruct((1, d), F32)],
        [pltpu.VMEM((tm, d), F32)], ("arbitrary", "arbitrary"), name, (a, wt, x, g, dres), vmem=56 * 2 ** 20)


def swiglu_matmul(gu, w, x1, name):
    s = gu.shape[0]
    d = w.shape[1]
    tm = min(512, s)

    def body(gu_ref, w_ref, x_ref, o_ref):
        acc = x_ref[...]
        for j in range(N_FF_BLKS):
            gt = gu_ref[:, j * FF_BLK:(j + 1) * FF_BLK].astype(F32)
            up = gu_ref[:, D_FF + j * FF_BLK:D_FF + (j + 1) * FF_BLK].astype(F32)
            act = (gt * _sigmoid(gt) * up).astype(BF16)
            acc += _dot(act, w_ref[j * FF_BLK:(j + 1) * FF_BLK, :])
        o_ref[...] = acc

    return pl.pallas_call(
        body, grid=(s // tm,),
        in_specs=[pl.BlockSpec((tm, 2 * D_FF), lambda i: (i, 0)), pl.BlockSpec((D_FF, d), lambda i: (0, 0)),
                  pl.BlockSpec((tm, d), lambda i: (i, 0))],
        out_specs=pl.BlockSpec((tm, d), lambda i: (i, 0)),
        out_shape=jax.ShapeDtypeStruct((s, d), F32),
        compiler_params=_cparams(("parallel",)), name=name)(gu, w, x1)


def swiglu_bwd(dx2, gu, w, name, ex=None):
    s, d = dx2.shape
    tm = min(256, s)

    def body(dx_ref, gu_ref, w_ref, dgu_ref, act_ref):
        dx = dx_ref[...].astype(BF16)
        for j in range(N_FF_BLKS):
            g_cols = slice(j * FF_BLK, (j + 1) * FF_BLK)
            u_cols = slice(D_FF + j * FF_BLK, D_FF + (j + 1) * FF_BLK)
            dact = _dot_nt(dx, w_ref[j * FF_BLK:(j + 1) * FF_BLK, :])
            gt = gu_ref[:, g_cols].astype(F32)
            up = gu_ref[:, u_cols].astype(F32)
            sg = _sigmoid(gt)
            act_ref[:, j * FF_BLK:(j + 1) * FF_BLK] = (gt * sg * up).astype(BF16)
            dgu_ref[:, g_cols] = (dact * up * (sg * (1.0 + gt * (1.0 - sg)))).astype(BF16)
            dgu_ref[:, u_cols] = (dact * gt * sg).astype(BF16)

    return _carried_call(
        body, ex, (s // tm,),
        [pl.BlockSpec((tm, d), lambda i: (i, 0)), pl.BlockSpec((tm, 2 * D_FF), lambda i: (i, 0)),
         pl.BlockSpec((D_FF, d), lambda i: (0, 0))],
        [pl.BlockSpec((tm, 2 * D_FF), lambda i: (i, 0)), pl.BlockSpec((tm, D_FF), lambda i: (i, 0))],
        [jax.ShapeDtypeStruct((s, 2 * D_FF), BF16), jax.ShapeDtypeStruct((s, D_FF), BF16)], [],
        ("arbitrary",), name, (dx2, gu, w))


def loss_kernel(y, tgt, name):
    s, d = y.shape
    tm = min(512, s)

    def body(y_ref, t_ref, l_ref, dy_ref):
        @pl.when(pl.program_id(0) == 0)
        def _():
            l_ref[...] = jnp.zeros_like(l_ref)

        err = y_ref[...] - t_ref[...]
        dy_ref[...] = err * (1.0 / d)
        l_ref[...] += jnp.sum(jnp.sum(err * err, axis=1, keepdims=True), axis=0, keepdims=True)

    return pl.pallas_call(
        body, grid=(s // tm,),
        in_specs=[pl.BlockSpec((tm, d), lambda i: (i, 0)), pl.BlockSpec((tm, d), lambda i: (i, 0))],
        out_specs=[pl.BlockSpec((8, 128), lambda i: (0, 0)), pl.BlockSpec((tm, d), lambda i: (i, 0))],
        out_shape=[jax.ShapeDtypeStruct((8, 128), F32), jax.ShapeDtypeStruct((s, d), F32)],
        compiler_params=_cparams(("arbitrary",)), name=name)(y, tgt)


def _split3(v):
    a1 = v.astype(BF16)
    r1 = v - a1.astype(F32)
    a2 = r1.astype(BF16)
    a3 = (r1 - a2.astype(F32)).astype(BF16)
    return a1, a2, a3


def forget_fwd(h, wt_in, b, name):
    s, d = h.shape
    tm = min(512, s)

    def body(h_ref, w_ref, b_ref, z_ref, c_ref, carry_ref):
        @pl.when(pl.program_id(0) == 0)
        def _():
            carry_ref[...] = jnp.zeros_like(carry_ref)

        z = _dot_nt(h_ref[...], w_ref[...]) + b_ref[...]
        z_ref[...] = z
        logf = jnp.minimum(z, 0.0) - jnp.log(1.0 + jnp.exp(-jnp.abs(z)))
        row = lax.broadcasted_iota(jnp.int32, (tm, tm), 0)
        col = lax.broadcasted_iota(jnp.int32, (tm, tm), 1)
        tri = (row >= col).astype(BF16)
        a1, a2, a3 = _split3(logf)
        c = _dot(tri, a1) + _dot(tri, a2) + _dot(tri, a3) + carry_ref[...]
        c_ref[...] = c
        carry_ref[...] = c[tm - 1:tm, :]

    return pl.pallas_call(
        body, grid=(s // tm,),
        in_specs=[pl.BlockSpec((tm, d), lambda i: (i, 0)), pl.BlockSpec((128, d), lambda i: (N_MAIN // 128, 0)),
                  pl.BlockSpec((1, 128), lambda i: (0, 0))],
        out_specs=[pl.BlockSpec((tm, 128), lambda i: (i, 0)), pl.BlockSpec((tm, 128), lambda i: (i, 0))],
        out_shape=[jax.ShapeDtypeStruct((s, 128), F32), jax.ShapeDtypeStruct((s, 128), F32)],
        scratch_shapes=[pltpu.VMEM((1, 128), F32)],
        compiler_params=_cparams(("arbitrary",)), name=name)(h, wt_in, b)


def forget_bwd(dc, z, dproj, name):
    s = dc.shape[0]
    tm = min(512, s)
    nt = s // tm

    def body(dc_ref, z_ref, dp_ref, dz_ref, db_ref, carry_ref):
        @pl.when(pl.program_id(0) == 0)
        def _():
            carry_ref[...] = jnp.zeros_like(carry_ref)
            db_ref[...] = jnp.zeros_like(db_ref)

        row = lax.broadcasted_iota(jnp.int32, (tm, tm), 0)
        col = lax.broadcasted_iota(jnp.int32, (tm, tm), 1)
        tri = (col >= row).astype(BF16)
        a1, a2, a3 = _split3(dc_ref[...])
        dlogf = _dot(tri, a1) + _dot(tri, a2) + _dot(tri, a3) + carry_ref[...]
        carry_ref[...] = dlogf[0:1, :]
        dz = dlogf * (1.0 - _sigmoid(z_ref[...]))
        dz_ref[...] = dz.astype(BF16)
        db_ref[...] += jnp.sum(dz, axis=0, keepdims=True)

    return pl.pallas_call(
        body, grid=(nt,),
        in_specs=[pl.BlockSpec((tm, 128), lambda i: (nt - 1 - i, 0)), pl.BlockSpec((tm, 128), lambda i: (nt - 1 - i, 0)),
                  pl.BlockSpec(memory_space=pl.ANY)],
        out_specs=[pl.BlockSpec((tm, 128), lambda i: (nt - 1 - i, N_MAIN // 128)), pl.BlockSpec((1, 128), lambda i: (0, 0))],
        out_shape=[jax.ShapeDtypeStruct(dproj.shape, BF16), jax.ShapeDtypeStruct((1, 128), F32)],
        scratch_shapes=[pltpu.VMEM((1, 128), F32)], input_output_aliases={2: 0},
        compiler_params=_cparams(("arbitrary",)), name=name)(dc, z, dproj)


HEAD_GROUP = 4
LANE_C = 64
LANE_ONE = 67


def _lanes():
    lane = lax.broadcasted_iota(jnp.int32, (1, 128), 1)
    return lane, lane < HEAD_DIM


def _half_mean(t, lo):
    s_lo = jnp.sum(jnp.where(lo, t, 0.0), axis=-1, keepdims=True)
    s_hi = jnp.sum(jnp.where(lo, 0.0, t), axis=-1, keepdims=True)
    return jnp.where(lo, s_lo, s_hi) * (1.0 / HEAD_DIM)


def _lane_col(t, lane, idx):
    return jnp.sum(jnp.where(lane == idx, t, 0.0), axis=-1, keepdims=True)


def _swap_halves(t):
    return pltpu.roll(t, HEAD_DIM, 1)


def _causal(s_blk, tq, tk):
    row = lax.broadcasted_iota(jnp.int32, (tq, tk), 0)
    col = lax.broadcasted_iota(jnp.int32, (tq, tk), 1)
    return jnp.where(row >= col, s_blk, -jnp.inf)


def attn_prep(proj, c, gq2, gk2, name):
    s = proj.shape[0]
    tm = min(512, s)
    first = N_REST // 128

    def body(q_ref, k_ref, v_ref, c_ref, gq_ref, gk_ref, qa_ref, ka_ref, va_ref, vt_ref):
        j = pl.program_id(1)
        lane, lo = _lanes()

        def normed(ref, g):
            t = ref[...].astype(F32)
            return t * lax.rsqrt(_half_mean(t * t, lo) + EPS) * g

        qn = normed(q_ref, gq_ref[...] * ATTN_SCALE)
        kn = normed(k_ref, gk_ref[...])
        vv = v_ref[...].astype(F32)
        cv = c_ref[...]
        one_q = jnp.where((lane >= LANE_ONE) & (lane < LANE_ONE + 3), 1.0, 0.0)
        one_k = jnp.where((lane >= LANE_C) & (lane < LANE_C + 3), 1.0, 0.0)
        one_v = jnp.where(lane == LANE_C, 1.0, 0.0)
        for e in range(2):
            pick = (lambda t: t) if e == 0 else _swap_halves
            pieces = [p.astype(F32) for p in _split3(_lane_col(cv, lane, 2 * j + e))]
            ext_q, ext_k = one_q, one_k
            for i, p in enumerate(pieces):
                ext_q = jnp.where(lane == LANE_C + i, p, ext_q)
                ext_k = jnp.where(lane == LANE_ONE + i, -p, ext_k)
            qa_ref[e] = jnp.where(lo, pick(qn), ext_q).astype(BF16)
            ka_ref[e] = jnp.where(lo, pick(kn), ext_k).astype(BF16)
            va = jnp.where(lo, pick(vv), one_v)
            va_ref[e] = va.astype(BF16)
            vt_ref[e] = va.T.astype(BF16)

    tile = lambda base: pl.BlockSpec((tm, 128), lambda i, j: (i, base + j))
    vec = pl.BlockSpec((1, 128), lambda i, j: (0, 0))
    out = pl.BlockSpec((2, tm, 128), lambda i, j: (j, i, 0))
    return pl.pallas_call(
        body, grid=(s // tm, HEADS // 2),
        in_specs=[tile(first), tile(first + 4), tile(first + 8), pl.BlockSpec((tm, 128), lambda i, j: (i, 0)), vec, vec],
        out_specs=[out, out, out, pl.BlockSpec((2, 128, tm), lambda i, j: (j, 0, i))],
        out_shape=[jax.ShapeDtypeStruct((HEADS, s, 128), BF16)] * 3 + [jax.ShapeDtypeStruct((HEADS, 128, s), BF16)],
        compiler_params=_cparams(("parallel", "arbitrary")), name=name)(proj, proj, proj, c, gq2, gk2)


def attn_fwd(q, k, v, ccol, crow, gq, gk, name):
    hh, s, hd = q.shape
    tq = tk = min(512, s)
    nq = s // tq

    def body(q_ref, k_ref, v_ref, cc_ref, cr_ref, gq_ref, gk_ref, o_ref, lse_ref, qn_ref, m_ref, l_ref, acc_ref):
        qi, ki = pl.program_id(1), pl.program_id(2)

        @pl.when(ki == 0)
        def _():
            qn_ref[...] = _qk_hat(q_ref, gq_ref, ATTN_SCALE)
            m_ref[...] = jnp.full_like(m_ref, -jnp.inf)
            l_ref[...] = jnp.zeros_like(l_ref)
            acc_ref[...] = jnp.zeros_like(acc_ref)

        @pl.when(ki <= qi)
        def _():
            kn = _qk_hat(k_ref, gk_ref, 1.0)
            sb = _dot_nt(qn_ref[...], kn) + (cc_ref[...] - cr_ref[...])
            sb = _causal(sb, qi, ki, tq, tk)
            m_new = jnp.maximum(m_ref[...], jnp.max(sb, axis=-1, keepdims=True))
            alpha = jnp.exp(m_ref[...] - m_new)
            p = jnp.exp(sb - m_new)
            l_ref[...] = alpha * l_ref[...] + jnp.sum(p, axis=-1, keepdims=True)
            acc_ref[...] = alpha * acc_ref[...] + _dot(p.astype(BF16), v_ref[...])
            m_ref[...] = m_new

        @pl.when(ki == qi)
        def _():
            o_ref[...] = (acc_ref[...] / l_ref[...]).astype(BF16)
            lse_ref[...] = m_ref[...] + jnp.log(l_ref[...])

    qspec = pl.BlockSpec((None, tq, hd), lambda h, i, j: (h, i, 0))
    kspec = pl.BlockSpec((None, tk, hd), lambda h, i, j: (h, jnp.minimum(i, j), 0))
    gspec = pl.BlockSpec((1, hd), lambda h, i, j: (0, 0))
    return pl.pallas_call(
        body, grid=(hh, nq, nq),
        in_specs=[qspec, kspec, kspec,
                  pl.BlockSpec((None, tq, 1), lambda h, i, j: (h, i, 0)),
                  pl.BlockSpec((None, 1, tk), lambda h, i, j: (h, 0, jnp.minimum(i, j))), gspec, gspec],
        out_specs=[qspec, pl.BlockSpec((None, tq, 1), lambda h, i, j: (h, i, 0))],
        out_shape=[jax.ShapeDtypeStruct((hh, s, hd), BF16), jax.ShapeDtypeStruct((hh, s, 1), F32)],
        scratch_shapes=[pltpu.VMEM((tq, hd), BF16), pltpu.VMEM((tq, 1), F32), pltpu.VMEM((tq, 1), F32),
                        pltpu.VMEM((tq, hd), F32)],
        compiler_params=_cparams(("parallel", "parallel", "arbitrary")), name=name)(q, k, v, ccol, crow, gq, gk)


def attn_bwd_dq(q, k, v, o, do, lse, ccol, crow, gq, gk, name):
    hh, s, hd = q.shape
    tq = tk = min(512, s)
    nq = s // tq

    def body(q_ref, k_ref, v_ref, o_ref, do_ref, lse_ref, cc_ref, cr_ref, gq_ref, gk_ref,
             dq_ref, dcc_ref, dg_ref, qn_ref, dl_ref, acc_ref, dca_ref):
        h, qi, ki = pl.program_id(0), pl.program_id(1), pl.program_id(2)

        @pl.when((h == 0) & (qi == 0) & (ki == 0))
        def _():
            dg_ref[...] = jnp.zeros_like(dg_ref)

        @pl.when(ki == 0)
        def _():
            qn_ref[...] = _qk_hat(q_ref, gq_ref, ATTN_SCALE)
            dl_ref[...] = jnp.sum(do_ref[...].astype(F32) * o_ref[...].astype(F32), axis=-1, keepdims=True)
            acc_ref[...] = jnp.zeros_like(acc_ref)
            dca_ref[...] = jnp.zeros_like(dca_ref)

        @pl.when(ki <= qi)
        def _():
            kn = _qk_hat(k_ref, gk_ref, 1.0)
            sb = _dot_nt(qn_ref[...], kn) + (cc_ref[...] - cr_ref[...])
            p = jnp.exp(_causal(sb, qi, ki, tq, tk) - lse_ref[...])
            dp = _dot_nt(do_ref[...], v_ref[...])
            ds = p * (dp - dl_ref[...])
            acc_ref[...] += _dot(ds.astype(BF16), kn)
            dca_ref[...] += jnp.sum(ds, axis=-1, keepdims=True)

        @pl.when(ki == qi)
        def _():
            dq, dg = _norm_bwd(q_ref[...].astype(F32), gq_ref[...], acc_ref[...], ATTN_SCALE)
            dq_ref[...] = dq.astype(BF16)
            dcc_ref[...] = dca_ref[...]
            dg_ref[...] += dg

    qspec = pl.BlockSpec((None, tq, hd), lambda h, i, j: (h, i, 0))
    kspec = pl.BlockSpec((None, tk, hd), lambda h, i, j: (h, jnp.minimum(i, j), 0))
    cspec = pl.BlockSpec((None, tq, 1), lambda h, i, j: (h, i, 0))
    gspec = pl.BlockSpec((1, hd), lambda h, i, j: (0, 0))
    return pl.pallas_call(
        body, grid=(hh, nq, nq),
        in_specs=[qspec, kspec, kspec, qspec, qspec, cspec, cspec,
                  pl.BlockSpec((None, 1, tk), lambda h, i, j: (h, 0, jnp.minimum(i, j))), gspec, gspec],
        out_specs=[qspec, cspec, gspec],
        out_shape=[jax.ShapeDtypeStruct((hh, s, hd), BF16), jax.ShapeDtypeStruct((hh, s, 1), F32),
                   jax.ShapeDtypeStruct((1, hd), F32)],
        scratch_shapes=[pltpu.VMEM((tq, hd), BF16), pltpu.VMEM((tq, 1), F32), pltpu.VMEM((tq, hd), F32),
                        pltpu.VMEM((tq, 1), F32)],
        compiler_params=_cparams(("arbitrary", "arbitrary", "arbitrary")), name=name)(
            q, k, v, o, do, lse, ccol, crow, gq, gk)


def attn_bwd_dkv(q, k, v, o, do, lse, ccol, crow, gq, gk, name):
    hh, s, hd = q.shape
    tq = tk = min(512, s)
    nq = s // tq

    def body(q_ref, k_ref, v_ref, o_ref, do_ref, lse_ref, cc_ref, cr_ref, gq_ref, gk_ref,
             dk_ref, dv_ref, dcr_ref, dg_ref, kn_ref, dka_ref, dva_ref, dca_ref):
        h, ki, qi = pl.program_id(0), pl.program_id(1), pl.program_id(2)

        @pl.when((h == 0) & (ki == 0) & (qi == 0))
        def _():
            dg_ref[...] = jnp.zeros_like(dg_ref)

        @pl.when(qi == 0)
        def _():
            kn_ref[...] = _qk_hat(k_ref, gk_ref, 1.0)
            dka_ref[...] = jnp.zeros_like(dka_ref)
            dva_ref[...] = jnp.zeros_like(dva_ref)
            dca_ref[...] = jnp.zeros_like(dca_ref)

        @pl.when(qi >= ki)
        def _():
            qn = _qk_hat(q_ref, gq_ref, ATTN_SCALE)
            do = do_ref[...]
            delta = jnp.sum(do.astype(F32) * o_ref[...].astype(F32), axis=-1, keepdims=True)
            sb = _dot_nt(qn, kn_ref[...]) + (cc_ref[...] - cr_ref[...])
            p = jnp.exp(_causal(sb, qi, ki, tq, tk) - lse_ref[...])
            dva_ref[...] += _dot_tn(p.astype(BF16), do)
            ds = p * (_dot_nt(do, v_ref[...]) - delta)
            dka_ref[...] += _dot_tn(ds.astype(BF16), qn)
            dca_ref[...] += jnp.sum(ds, axis=0, keepdims=True)

        @pl.when(qi == nq - 1)
        def _():
            dk, dg = _norm_bwd(k_ref[...].astype(F32), gk_ref[...], dka_ref[...], 1.0)
            dk_ref[...] = dk.astype(BF16)
            dv_ref[...] = dva_ref[...].astype(BF16)
            dcr_ref[...] = dca_ref[...]
            dg_ref[...] += dg

    kspec = pl.BlockSpec((None, tk, hd), lambda h, j, i: (h, j, 0))
    qspec = pl.BlockSpec((None, tq, hd), lambda h, j, i: (h, jnp.maximum(i, j), 0))
    cspec = pl.BlockSpec((None, tq, 1), lambda h, j, i: (h, jnp.maximum(i, j), 0))
    rspec = pl.BlockSpec((None, 1, tk), lambda h, j, i: (h, 0, j))
    gspec = pl.BlockSpec((1, hd), lambda h, j, i: (0, 0))
    return pl.pallas_call(
        body, grid=(hh, nq, nq),
        in_specs=[qspec, kspec, kspec, qspec, qspec, cspec, cspec, rspec, gspec, gspec],
        out_specs=[kspec, kspec, rspec, gspec],
        out_shape=[jax.ShapeDtypeStruct((hh, s, hd), BF16), jax.ShapeDtypeStruct((hh, s, hd), BF16),
                   jax.ShapeDtypeStruct((hh, 1, s), F32), jax.ShapeDtypeStruct((1, hd), F32)],
        scratch_shapes=[pltpu.VMEM((tk, hd), BF16), pltpu.VMEM((tk, hd), F32), pltpu.VMEM((tk, hd), F32),
                        pltpu.VMEM((1, tk), F32)],
        compiler_params=_cparams(("arbitrary", "arbitrary", "arbitrary")), name=name)(
            q, k, v, o, do, lse, ccol, crow, gq, gk)


def _carry(ex, n_in, n_out, n_scratch, grid):
    n_xin, n_xout = (len(ex.inputs), len(ex.out_shapes)) if ex else (0, 0)

    def split(refs):
        ins, xins = refs[:n_in], refs[n_in:n_in + n_xin]
        rest = refs[n_in + n_xin:]
        outs, xouts = rest[:n_out], rest[n_out:n_out + n_xout]
        rest = rest[n_out + n_xout:]
        return ins + outs + rest[:n_scratch], (xins, xouts, rest[n_scratch:])

    def first():
        return functools.reduce(lambda a, b: a & b, [pl.program_id(d) == 0 for d in range(len(grid))])

    def last():
        return functools.reduce(lambda a, b: a & b, [pl.program_id(d) == grid[d] - 1 for d in range(len(grid))])

    return split, first, last


def _carried_call(body, ex, grid, in_specs, out_specs, out_shape, scratch, sem, name, operands, vmem=None):
    any_spec = pl.BlockSpec(memory_space=pl.ANY)
    split, first, last = _carry(ex, len(in_specs), len(out_specs), len(scratch), grid)

    def carried(*refs):
        own, xrefs = split(refs)
        if ex:
            @pl.when(first())
            def _():
                ex.start(*xrefs)

        body(*own)
        if ex:
            @pl.when(last())
            def _():
                ex.drain(*xrefs)

    n_xin = len(ex.inputs) if ex else 0
    results = pl.pallas_call(
        carried, grid=grid, in_specs=list(in_specs) + [any_spec] * n_xin,
        out_specs=list(out_specs) + [any_spec] * (len(ex.out_shapes) if ex else 0),
        out_shape=list(out_shape) + (list(ex.out_shapes) if ex else []),
        input_output_aliases={len(in_specs) + i: len(out_specs) + o for i, o in ex.aliases.items()} if ex else {},
        scratch_shapes=list(scratch) + (ex.scratch if ex else []),
        compiler_params=_cparams(sem, vmem), name=name)(*operands, *(ex.inputs if ex else []))
    return results[:len(out_specs)], results[len(out_specs):]


def _tri_rows(t, n):
    qi = sum(jnp.where(t >= r * (r + 1) // 2, 1, 0) for r in range(1, n))
    return qi, t - qi * (qi + 1) // 2


def _tri_cols(t, n):
    ki = sum(jnp.where(t >= r * n - r * (r - 1) // 2, 1, 0) for r in range(1, n))
    return ki, ki + t - (ki * n - ki * (ki - 1) // 2)


def _causal_t(st_blk, tk, tq):
    key = lax.broadcasted_iota(jnp.int32, (tk, tq), 0)
    qry = lax.broadcasted_iota(jnp.int32, (tk, tq), 1)
    return jnp.where(qry >= key, st_blk, -jnp.inf)


def attn_forward(qa, ka, vt, name, ex=None):
    hh, s, _ = qa.shape
    tq = tk = min(512, s)
    nq = s // tq
    grp = HEAD_GROUP

    def body(q_ref, k_ref, vt_ref, o_ref, lse_ref, m_ref, acc_ref):
        qi, ki = _tri_rows(pl.program_id(1), nq)

        @pl.when(ki == 0)
        def _():
            m_ref[...] = jnp.full_like(m_ref, -jnp.inf)
            acc_ref[...] = jnp.zeros_like(acc_ref)

        def step(masked):
            nxt = _dot_nt(k_ref[0], q_ref[0])
            for g in range(grp):
                st = nxt
                if g + 1 < grp:
                    nxt = _dot_nt(k_ref[g + 1], q_ref[g + 1])
                if masked:
                    st = _causal_t(st, tk, tq)
                m_old = m_ref[g]
                m_new = jnp.maximum(m_old, jnp.max(st, axis=0, keepdims=True))
                pt = jnp.exp(st - m_new).astype(BF16)
                acc_ref[g] = jnp.exp(m_old - m_new) * acc_ref[g] + _dot(vt_ref[g], pt)
                m_ref[g] = m_new

        @pl.when(ki < qi)
        def _():
            step(False)

        @pl.when(ki == qi)
        def _():
            step(True)
            for g in range(grp):
                acc = acc_ref[g]
                denom = acc[LANE_C:LANE_C + 1, :]
                o_ref[g] = (acc / denom).T.astype(BF16)
                lse_ref[g] = m_ref[g] + jnp.log(denom)

    qspec = pl.BlockSpec((grp, tq, 128), lambda h, t: (h, _tri_rows(t, nq)[0], 0))
    kspec = pl.BlockSpec((grp, tk, 128), lambda h, t: (h, _tri_rows(t, nq)[1], 0))
    vspec = pl.BlockSpec((grp, 128, tk), lambda h, t: (h, 0, _tri_rows(t, nq)[1]))
    lspec = pl.BlockSpec((grp, 1, tq), lambda h, t: (h, 0, _tri_rows(t, nq)[0]))
    return _carried_call(
        body, ex, (hh // grp, nq * (nq + 1) // 2), [qspec, kspec, vspec], [qspec, lspec],
        [jax.ShapeDtypeStruct((hh, s, 128), BF16), jax.ShapeDtypeStruct((hh, 1, s), F32)],
        [pltpu.VMEM((grp, 1, tq), F32), pltpu.VMEM((grp, 128, tq), F32)],
        ("arbitrary", "arbitrary"), name, (qa, ka, vt))


def attn_backward(qa, ka, va, oa, doa, lse, name, ex=None):
    hh, s, _ = qa.shape
    tq = tk = min(512, s)
    nq = s // tq
    grp = HEAD_GROUP

    def body(q_ref, k_ref, v_ref, o_ref, do_ref, lse_ref, dq_ref, dk_ref, dv_ref, dka_ref, dva_ref):
        ki, qi = _tri_cols(pl.program_id(1), nq)

        @pl.when(pl.program_id(1) == 0)
        def _():
            dq_ref[...] = jnp.zeros_like(dq_ref)

        @pl.when(qi == ki)
        def _():
            dka_ref[...] = jnp.zeros_like(dka_ref)
            dva_ref[...] = jnp.zeros_like(dva_ref)

        def step(masked):
            rows = pl.ds(pl.multiple_of(qi * tq, tq), tq)
            products = lambda g: (_dot_nt(k_ref[g], q_ref[g]), _dot_nt(v_ref[g], do_ref[g]))
            nxt = products(0)
            for g in range(grp):
                st, dpt = nxt
                if g + 1 < grp:
                    nxt = products(g + 1)
                q, k, do = q_ref[g], k_ref[g], do_ref[g]
                if masked:
                    st = _causal_t(st, tk, tq)
                pt = jnp.exp(st - lse_ref[g])
                delta = jnp.sum((do.astype(F32) * o_ref[g].astype(F32)).T, axis=0, keepdims=True)
                dst = (pt * (dpt - delta)).astype(BF16)
                dva_ref[g] += _dot(pt.astype(BF16), do)
                dka_ref[g] += _dot(dst, q)
                dq_ref[g, rows, :] += _dot_tn(dst, k)

        @pl.when(qi > ki)
        def _():
            step(False)

        @pl.when(qi == ki)
        def _():
            step(True)

        @pl.when(qi == nq - 1)
        def _():
            dk_ref[...] = dka_ref[...]
            dv_ref[...] = dva_ref[...].astype(BF16)

    qspec = pl.BlockSpec((grp, tq, 128), lambda h, t: (h, _tri_cols(t, nq)[1], 0))
    lspec = pl.BlockSpec((grp, 1, tq), lambda h, t: (h, 0, _tri_cols(t, nq)[1]))
    kspec = pl.BlockSpec((grp, tk, 128), lambda h, t: (h, _tri_cols(t, nq)[0], 0))
    return _carried_call(
        body, ex, (hh // grp, nq * (nq + 1) // 2), [qspec, kspec, kspec, qspec, qspec, lspec],
        [pl.BlockSpec((grp, s, 128), lambda h, t: (h, 0, 0)), kspec, kspec],
        [jax.ShapeDtypeStruct((hh, s, 128), F32), jax.ShapeDtypeStruct((hh, s, 128), F32),
         jax.ShapeDtypeStruct((hh, s, 128), BF16)],
        [pltpu.VMEM((grp, tk, 128), F32), pltpu.VMEM((grp, tk, 128), F32)],
        ("arbitrary", "arbitrary"), name, (qa, ka, va, oa, doa, lse))


def attn_post(dqa, dka, dva, proj, gq2, gk2, dproj, name):
    s = proj.shape[0]
    tm = min(256, s)

    def body(dq_ref, dk_ref, dv_ref, q_ref, k_ref, gq_ref, gk_ref, dp_any, dp_ref, dc_ref, dgq_ref, dgk_ref):
        lane, lo = _lanes()

        @pl.when(pl.program_id(0) == 0)
        def _():
            dgq_ref[...] = jnp.zeros_like(dgq_ref)
            dgk_ref[...] = jnp.zeros_like(dgk_ref)

        def pair(ref, j):
            return jnp.where(lo, ref[2 * j].astype(F32), _swap_halves(ref[2 * j + 1].astype(F32)))

        def norm_bwd(raw, g, dhat, scale):
            r = lax.rsqrt(_half_mean(raw * raw, lo) + EPS)
            y = raw * r
            dy = dhat * (g * scale)
            return r * (dy - y * _half_mean(dy * y, lo)), jnp.sum(dhat * y, axis=0, keepdims=True) * scale

        dc = jnp.zeros((tm, 128), F32)
        for j in range(HEADS // 2):
            cols = slice(128 * j, 128 * (j + 1))
            dq, dgq = norm_bwd(q_ref[:, cols].astype(F32), gq_ref[...], pair(dq_ref, j), ATTN_SCALE)
            dk, dgk = norm_bwd(k_ref[:, cols].astype(F32), gk_ref[...], pair(dk_ref, j), 1.0)
            dgq_ref[...] += dgq
            dgk_ref[...] += dgk
            dp_ref[:, cols] = dq.astype(BF16)
            dp_ref[:, D_ATTN + 128 * j:D_ATTN + 128 * (j + 1)] = dk.astype(BF16)
            dp_ref[:, 2 * D_ATTN + 128 * j:2 * D_ATTN + 128 * (j + 1)] = pair(dv_ref, j).astype(BF16)
            for e in range(2):
                h = 2 * j + e
                col = _lane_col(dq_ref[h], lane, LANE_C) - _lane_col(dk_ref[h], lane, LANE_ONE)
                dc = jnp.where(lane == h, col, dc)
        dp_ref[:, 3 * D_ATTN:] = jnp.zeros((tm, DPROJ_TAIL - 3 * D_ATTN), BF16)
        dc_ref[...] = dc

    heads = lambda: pl.BlockSpec((HEADS, tm, 128), lambda i: (0, i, 0))
    vec = pl.BlockSpec((1, 128), lambda i: (0, 0))
    first = N_REST // D_ATTN
    return pl.pallas_call(
        body, grid=(s // tm,),
        in_specs=[heads(), heads(), heads(), pl.BlockSpec((tm, D_ATTN), lambda i: (i, first)),
                  pl.BlockSpec((tm, D_ATTN), lambda i: (i, first + 1)), vec, vec, pl.BlockSpec(memory_space=pl.ANY)],
        out_specs=[pl.BlockSpec((tm, DPROJ_TAIL), lambda i: (i, N_REST // DPROJ_TAIL)),
                   pl.BlockSpec((tm, 128), lambda i: (i, 0)), vec, vec],
        out_shape=[jax.ShapeDtypeStruct(dproj.shape, BF16), jax.ShapeDtypeStruct((s, 128), F32),
                   jax.ShapeDtypeStruct((1, 128), F32), jax.ShapeDtypeStruct((1, 128), F32)],
        input_output_aliases={7: 0},
        compiler_params=_cparams(("arbitrary",)), name=name)(dqa, dka, dva, proj, proj, gq2, gk2, dproj)


def _pool_groups(tm):
    gid = lax.broadcasted_iota(jnp.int32, (1, D_POOL), 1) // (D_POOL // 4)
    win = jnp.where(gid == 0, 2.0, jnp.where(gid == 1, 4.0, jnp.where(gid == 2, 8.0, 16.0)))
    return gid, win


def _by_group(gid, v2, v4, v8, v16):
    return jnp.where(gid == 0, v2, jnp.where(gid == 1, v4, jnp.where(gid == 2, v8, v16)))


def _branches(rest_ref, halo_ref, a_ref, wa_ref, wc_ref, wp_ref, sc_ref, cw_ref, ti, tm):
    f = lambda v: v.astype(F32)
    cx, cb, cc, px = f(rest_ref[:, 0:256]), f(rest_ref[:, 256:512]), f(rest_ref[:, 512:768]), f(rest_ref[:, 768:1024])
    live = jnp.where(ti > 0, 1.0, 0.0)
    hz = f(halo_ref[:, 0:256]) * f(halo_ref[:, 512:768]) * live
    hp = f(halo_ref[:, 768:1024]) * live
    z = cc * cx
    zf = jnp.concatenate([hz, z], axis=0)
    z1 = pltpu.roll(zf, 1, 0)[HALO:]
    z2 = pltpu.roll(zf, 2, 0)[HALO:]
    cw = cw_ref[...]
    conv = cw[2:3] * z + cw[1:2] * z1 + cw[0:1] * z2
    uc = cb * conv
    pf = jnp.concatenate([hp, px], axis=0)
    s2 = pf + pltpu.roll(pf, 1, 0)
    s4 = s2 + pltpu.roll(s2, 2, 0)
    s8 = s4 + pltpu.roll(s4, 4, 0)
    s16 = s8 + pltpu.roll(s8, 8, 0)
    gid, win = _pool_groups(tm)
    t = (ti * tm + lax.broadcasted_iota(jnp.int32, (tm, 1), 0)).astype(F32)
    inv = 1.0 / jnp.minimum(t + 1.0, win)
    dpool = _by_group(gid, s2[HALO:], s4[HALO:], s8[HALO:], s16[HALO:]) * inv - px
    _, lo = _lanes()
    a_tok = [jnp.where(lo, f(a_ref[2 * j]), _swap_halves(f(a_ref[2 * j + 1]))).astype(BF16) for j in range(HEADS // 2)]
    y_attn = _dot(a_tok[0], wa_ref[0:128, :])
    for j in range(1, HEADS // 2):
        y_attn += _dot(a_tok[j], wa_ref[128 * j:128 * (j + 1), :])
    y_conv = _dot(uc.astype(BF16), wc_ref[...])
    y_pool_raw = _dot(dpool.astype(BF16), wp_ref[...])
    sg = [_sigmoid(f(rest_ref[:, 1024 + i * D_MODEL:1024 + (i + 1) * D_MODEL])) for i in range(3)]
    return dict(cx=cx, cb=cb, cc=cc, z=z, z1=z1, z2=z2, conv=conv, uc=uc, dpool=dpool, inv=inv, gid=gid, a_tok=a_tok,
                y_attn=y_attn, y_conv=y_conv, y_pool_raw=y_pool_raw, sg=sg, cw=cw)


def _mix_specs(tm, ti_of):
    blocks_per_tile = tm // HALO
    return [
        pl.BlockSpec((tm, N_REST), lambda i: (ti_of(i), 0)),
        pl.BlockSpec((HALO, 1024), lambda i: (jnp.maximum(ti_of(i) * blocks_per_tile - 1, 0), 0)),
        pl.BlockSpec((HEADS, tm, 128), lambda i: (0, ti_of(i), 0)),
        pl.BlockSpec((D_ATTN, D_MODEL), lambda i: (0, 0)),
        pl.BlockSpec((D_CONV, D_MODEL), lambda i: (0, 0)),
        pl.BlockSpec((D_POOL, D_MODEL), lambda i: (0, 0)),
        pl.BlockSpec((1, D_MODEL), lambda i: (0, 0)),
        pl.BlockSpec((8, D_CONV), lambda i: (0, 0)),
    ]


def mix_fwd(proj, a, x, wa, wc, wp, scale, cw, wo, name):
    s = x.shape[0]
    tm = min(256, s)

    def body(rest_ref, halo_ref, a_ref, wa_ref, wc_ref, wp_ref, sc_ref, cw_ref, wo_ref, x_ref, o_ref):
        b = _branches(rest_ref, halo_ref, a_ref, wa_ref, wc_ref, wp_ref, sc_ref, cw_ref, pl.program_id(0), tm)
        merged = b["sg"][0] * b["y_attn"] + b["sg"][1] * b["y_conv"] + b["sg"][2] * (b["y_pool_raw"] * sc_ref[...])
        o_ref[...] = x_ref[...] + _dot(merged.astype(BF16), wo_ref[...])

    return pl.pallas_call(
        body, grid=(s // tm,),
        in_specs=_mix_specs(tm, lambda i: i) + [pl.BlockSpec((D_MODEL, D_MODEL), lambda i: (0, 0)),
                                                 pl.BlockSpec((tm, D_MODEL), lambda i: (i, 0))],
        out_specs=pl.BlockSpec((tm, D_MODEL), lambda i: (i, 0)),
        out_shape=jax.ShapeDtypeStruct((s, D_MODEL), F32),
        compiler_params=_cparams(("parallel",)), name=name)(proj, proj, a, wa, wc, wp, scale, cw, wo, x)


def mix_bwd(proj, a, dx1, wa, wc, wp, scale, cw, wo, name):
    s = dx1.shape[0]
    tm = min(256, s)
    nt = s // tm
    ti_of = lambda i: nt - 1 - i
    n = tm + HALO

    def body(rest_ref, halo_ref, a_ref, wa_ref, wc_ref, wp_ref, sc_ref, cw_ref, wo_ref,
             dx_ref, dp_ref, da_ref, at_ref, mg_ref, dya_ref, dyc_ref, dyp_ref, uc_ref, dd_ref, dsc_ref, dcw_ref,
             cdc_ref, cde_ref):
        i = pl.program_id(0)
        ti = ti_of(i)

        @pl.when(i == 0)
        def _():
            cdc_ref[...] = jnp.zeros_like(cdc_ref)
            cde_ref[...] = jnp.zeros_like(cde_ref)
            dsc_ref[...] = jnp.zeros_like(dsc_ref)
            dcw_ref[...] = jnp.zeros_like(dcw_ref)

        b = _branches(rest_ref, halo_ref, a_ref, wa_ref, wc_ref, wp_ref, sc_ref, cw_ref, ti, tm)
        sg, sc = b["sg"], sc_ref[...]
        y_pool = b["y_pool_raw"] * sc
        merged = sg[0] * b["y_attn"] + sg[1] * b["y_conv"] + sg[2] * y_pool
        mg_ref[...] = merged.astype(BF16)
        dm = _dot_nt(dx_ref[...].astype(BF16), wo_ref[...])
        for j, y in enumerate((b["y_attn"], b["y_conv"], y_pool)):
            dp_ref[:, 1024 + j * D_MODEL:1024 + (j + 1) * D_MODEL] = (dm * y * sg[j] * (1.0 - sg[j])).astype(BF16)
        dya = (dm * sg[0]).astype(BF16)
        dya_ref[...] = dya
        _, lo = _lanes()
        for j in range(HEADS // 2):
            at_ref[:, 128 * j:128 * (j + 1)] = b["a_tok"][j]
            da = _dot_nt(dya, wa_ref[128 * j:128 * (j + 1), :])
            da_ref[2 * j] = jnp.where(lo, da, 0.0).astype(BF16)
            da_ref[2 * j + 1] = jnp.where(lo, _swap_halves(da), 0.0).astype(BF16)
        dyc = (dm * sg[1]).astype(BF16)
        dyc_ref[...] = dyc
        duc = _dot_nt(dyc, wc_ref[...])
        dyp = dm * sg[2]
        dsc_ref[...] += jnp.sum(dyp * b["y_pool_raw"], axis=0, keepdims=True)
        dypr = (dyp * sc).astype(BF16)
        dyp_ref[...] = dypr
        ddp = _dot_nt(dypr, wp_ref[...])
        uc_ref[...] = b["uc"].astype(BF16)
        dd_ref[...] = b["dpool"].astype(BF16)

        dconv = duc * b["cb"]
        dp_ref[:, 256:512] = (duc * b["conv"]).astype(BF16)
        dcf = jnp.concatenate([dconv, cdc_ref[...]], axis=0)
        cw = b["cw"]
        dz = cw[2:3] * dconv + cw[1:2] * pltpu.roll(dcf, n - 1, 0)[:tm] + cw[0:1] * pltpu.roll(dcf, n - 2, 0)[:tm]
        dp_ref[:, 0:256] = (dz * b["cc"]).astype(BF16)
        dp_ref[:, 512:768] = (dz * b["cx"]).astype(BF16)
        dcw_ref[0:1, :] += jnp.sum(dconv * b["z2"], axis=0, keepdims=True)
        dcw_ref[1:2, :] += jnp.sum(dconv * b["z1"], axis=0, keepdims=True)
        dcw_ref[2:3, :] += jnp.sum(dconv * b["z"], axis=0, keepdims=True)
        cdc_ref[...] = dconv[:HALO]

        e = ddp * b["inv"]
        ef = jnp.concatenate([e, cde_ref[...]], axis=0)
        r2 = ef + pltpu.roll(ef, n - 1, 0)
        r4 = r2 + pltpu.roll(r2, n - 2, 0)
        r8 = r4 + pltpu.roll(r4, n - 4, 0)
        r16 = r8 + pltpu.roll(r8, n - 8, 0)
        dp_ref[:, 768:1024] = (_by_group(b["gid"], r2[:tm], r4[:tm], r8[:tm], r16[:tm]) - ddp).astype(BF16)
        cde_ref[...] = e[:HALO]

    tile = lambda w: pl.BlockSpec((tm, w), lambda i: (ti_of(i), 0))
    whole = lambda r, c: pl.BlockSpec((r, c), lambda i: (0, 0))
    bf = lambda w: jax.ShapeDtypeStruct((s, w), BF16)
    return pl.pallas_call(
        body, grid=(nt,),
        in_specs=_mix_specs(tm, ti_of) + [whole(D_MODEL, D_MODEL), tile(D_MODEL)],
        out_specs=[tile(N_REST), pl.BlockSpec((HEADS, tm, 128), lambda i: (0, ti_of(i), 0)), tile(D_ATTN),
                   tile(D_MODEL), tile(D_MODEL), tile(D_MODEL), tile(D_MODEL),
                   tile(D_CONV), tile(D_POOL), whole(1, D_MODEL), whole(8, D_CONV)],
        out_shape=[bf(DPROJ_COLS), jax.ShapeDtypeStruct((HEADS, s, 128), BF16), bf(D_ATTN),
                   bf(D_MODEL), bf(D_MODEL), bf(D_MODEL), bf(D_MODEL), bf(D_CONV), bf(D_POOL),
                   jax.ShapeDtypeStruct((1, D_MODEL), F32), jax.ShapeDtypeStruct((8, D_CONV), F32)],
        scratch_shapes=[pltpu.VMEM((HALO, D_CONV), F32), pltpu.VMEM((HALO, D_POOL), F32)],
        compiler_params=_cparams(("arbitrary",)), name=name)(proj, proj, a, wa, wc, wp, scale, cw, wo, dx1)


def _adamw_math(w, g, m, v):
    m = ADAM_B1 * m + (1.0 - ADAM_B1) * g
    v = ADAM_B2 * v + (1.0 - ADAM_B2) * (g * g)
    m_hat = m / (1.0 - ADAM_B1 ** ADAM_STEP)
    v_hat = v / (1.0 - ADAM_B2 ** ADAM_STEP)
    delta = -ADAM_LR * (m_hat / (jnp.sqrt(v_hat) + ADAM_EPS) + ADAM_WD * w)
    return delta, m, v


ADAMW_PARTS_BLOCK_BYTES = 4 * 2 ** 20


def _row_tile(rows, cols, copies, itemsize):
    row_bytes = copies * (-(-cols // 128) * 128) * itemsize
    fits = [t for t in range(16, rows + 1, 16) if rows % t == 0 and t * row_bytes <= ADAMW_PARTS_BLOCK_BYTES]
    return max(fits) if fits else rows


def pair_sum(blocks, stage, me, name):
    n_slots, rows, cols = stage.shape
    tr = _row_tile(rows, cols, 1, 4)

    def body(me_ref, a_ref, b_ref, o_ref):
        o_ref[...] = (a_ref[...].astype(F32) + b_ref[...].astype(F32)).astype(BF16)

    slot = pl.BlockSpec((None, tr, cols), lambda i, r, me_ref: (i, r, 0))
    return pl.pallas_call(
        body, out_shape=jax.ShapeDtypeStruct(stage.shape, BF16),
        grid_spec=pltpu.PrefetchScalarGridSpec(
            num_scalar_prefetch=1, grid=(n_slots, rows // tr),
            in_specs=[pl.BlockSpec((None, tr, cols), lambda i, r, me_ref: (me_ref[0] ^ (2 * i), r, 0)), slot],
            out_specs=slot),
        compiler_params=_cparams(("parallel", "parallel")), name=name)(me.reshape(1), blocks, stage)


def adamw_sum(parts, w, m, v, name):
    layers, rows, cols = w.shape
    n_parts = parts.shape[1]
    if rows % 16 == 0:
        tr, tc = _row_tile(rows, cols, n_parts, parts.dtype.itemsize), cols
    else:
        tr, tc = rows, _pick(cols, (256, 128))

    def body(p_ref, w_ref, m_ref, v_ref, g_ref, d_ref, nm_ref, nv_ref):
        g = p_ref[0].astype(F32)
        for i in range(1, n_parts):
            g = g + p_ref[i].astype(F32)
        g_ref[...] = g
        d_ref[...], nm_ref[...], nv_ref[...] = _adamw_math(w_ref[...], g, m_ref[...], v_ref[...])

    spec = pl.BlockSpec((None, tr, tc), lambda l, i, j: (l, i, j))
    return pl.pallas_call(
        body, grid=(layers, rows // tr, cols // tc),
        in_specs=[pl.BlockSpec((None, n_parts, tr, tc), lambda l, i, j: (l, 0, i, j)), spec, spec, spec],
        out_specs=[spec] * 4, out_shape=[jax.ShapeDtypeStruct((layers, rows, cols), F32)] * 4,
        compiler_params=_cparams(("parallel", "parallel", "parallel")), name=name)(parts, w, m, v)


def _me():
    return lax.axis_index("x"), lax.axis_index("y"), lax.axis_index("c")


N_PEERS = N_DEV - 1


def all_gather(shards, name):
    n = len(shards)
    any_spec = pl.BlockSpec(memory_space=pl.ANY)

    def body(*refs):
        x_refs, out_refs = refs[:n], refs[n:2 * n]
        send_sems, recv_sems, local_sems = refs[2 * n:]
        x, y, c = _me()
        me, sibling = (x, y, c), (x, y, 1 - c)
        chips = [(1 - x, y), (x, 1 - y), (1 - x, 1 - y)]

        def copy(t, k, block, to, from_input=False):
            slot = out_refs[t].at[4 * block[0] + 2 * block[1] + block[2]]
            return pltpu.make_async_remote_copy(
                src_ref=x_refs[t] if from_input else slot, dst_ref=slot, send_sem=send_sems.at[N_PEERS * t + k],
                recv_sem=recv_sems.at[N_PEERS * t + k], device_id=to, device_id_type=pl.DeviceIdType.MESH)

        mine = [pltpu.make_async_copy(x_refs[t], out_refs[t].at[4 * x + 2 * y + c], local_sems.at[t]) for t in range(n)]
        started = []
        for t in range(n):
            mine[t].start()
            started.append(copy(t, 0, me, sibling, from_input=True))
            started += [copy(t, 1 + j, me, (*chip, c), from_input=True) for j, chip in enumerate(chips)]
        for cp in started:
            cp.start()
        for j, chip in enumerate(chips):
            for t in range(n):
                copy(t, 1 + j, (*chip, c), me).wait_recv()
                fwd = copy(t, 4 + j, (*chip, c), sibling)
                fwd.start()
                started.append(fwd)
        for t in range(n):
            copy(t, 0, sibling, me).wait_recv()
            for j, chip in enumerate(chips):
                copy(t, 4 + j, (*chip, 1 - c), me).wait_recv()
        for cp in started:
            cp.wait_send()
        for cp in mine:
            cp.wait()

    return pl.pallas_call(
        body, out_shape=[jax.ShapeDtypeStruct((N_DEV,) + s.shape, s.dtype) for s in shards],
        in_specs=[any_spec] * n, out_specs=[any_spec] * n,
        scratch_shapes=[pltpu.SemaphoreType.DMA((N_PEERS * n,)), pltpu.SemaphoreType.DMA((N_PEERS * n,)),
                        pltpu.SemaphoreType.DMA((n,))],
        name=name)(*shards)


SIBLING = 1
OTHER_CHIPS = (2, 4, 6)
SAME_CORE = (0,) + OTHER_CHIPS


class Exchange:
    def __init__(self, inputs, out_shapes, aliases, copies, local=()):
        self.inputs, self.out_shapes, self.aliases = list(inputs), list(out_shapes), aliases
        self._copies, self._local = list(copies), list(local)
        self.scratch = [pltpu.SemaphoreType.DMA((len(self._copies),)), pltpu.SemaphoreType.DMA((len(self._copies),)),
                        pltpu.SemaphoreType.DMA((max(len(self._local), 1),))]

    def _build(self, ins, outs, sems):
        send_sems, recv_sems, local_sems = sems
        x, y, c = _me()
        me = 4 * x + 2 * y + c
        local = [functools.partial(pltpu.make_async_copy, src(ins, outs, me), dst(outs, me), local_sems.at[i])
                 for i, (src, dst) in enumerate(self._local)]
        sends, recvs = [], []
        for i, (mask, src, dst) in enumerate(self._copies):
            px, py, pc = x ^ ((mask >> 2) & 1), y ^ ((mask >> 1) & 1), c ^ (mask & 1)
            pair = dict(send_sem=send_sems.at[i], recv_sem=recv_sems.at[i], device_id_type=pl.DeviceIdType.MESH)
            sends.append(functools.partial(
                pltpu.make_async_remote_copy, src_ref=src(ins, outs, me), dst_ref=dst(outs, me), device_id=(px, py, pc), **pair))
            recvs.append(functools.partial(
                pltpu.make_async_remote_copy, src_ref=src(ins, outs, me), dst_ref=dst(outs, me ^ mask), device_id=(x, y, c), **pair))
        return local, sends, recvs

    def start(self, ins, outs, sems):
        local, sends, _ = self._build(ins, outs, sems)
        for make in local + sends:
            make().start()

    def drain(self, ins, outs, sems):
        local, sends, recvs = self._build(ins, outs, sems)
        for make in recvs:
            make().wait_recv()
        for make in sends:
            make().wait_send()
        for make in local:
            make().wait()


def _bind(fn, *args):
    return functools.partial(fn, *args)


def join_exchanges(a, b):
    if a is None or b is None:
        return a or b
    na_in, na_out = len(a.inputs), len(a.out_shapes)

    def src_a(fn):
        return lambda ins, outs, me: fn(ins[:na_in], outs[:na_out], me)

    def dst_a(fn):
        return lambda outs, who: fn(outs[:na_out], who)

    def src_b(fn):
        return lambda ins, outs, me: fn(ins[na_in:], outs[na_out:], me)

    def dst_b(fn):
        return lambda outs, who: fn(outs[na_out:], who)

    copies = [(m, src_a(s), dst_a(d)) for m, s, d in a._copies] + [(m, src_b(s), dst_b(d)) for m, s, d in b._copies]
    local = [(src_a(s), dst_a(d)) for s, d in a._local] + [(src_b(s), dst_b(d)) for s, d in b._local]
    aliases = dict(a.aliases)
    aliases.update({na_in + i: na_out + o for i, o in b.aliases.items()})
    return Exchange(a.inputs + b.inputs, a.out_shapes + b.out_shapes, aliases, copies, local)


def gather_over_ici(shards):
    copies = [(mask, _bind(lambda t, ins, outs, me: ins[t], t), _bind(lambda t, outs, sender: outs[t].at[sender], t))
              for t in range(len(shards)) for mask in OTHER_CHIPS]
    local = [(_bind(lambda t, ins, outs, me: ins[t], t), _bind(lambda t, outs, me: outs[t].at[me], t))
             for t in range(len(shards))]
    return Exchange(shards, [jax.ShapeDtypeStruct((N_DEV,) + s.shape, s.dtype) for s in shards], {}, copies, local)


def gather_over_d2d(gathered):
    copies = [(SIBLING, _bind(lambda t, m, ins, outs, me: outs[t].at[me ^ m], t, m),
               _bind(lambda t, m, outs, sender: outs[t].at[sender ^ m], t, m))
              for t in range(len(gathered)) for m in SAME_CORE]
    return Exchange(gathered, [jax.ShapeDtypeStruct(g.shape, g.dtype) for g in gathered],
                    {t: t for t in range(len(gathered))}, copies)


def scatter_over_d2d(blocks):
    copies = [(SIBLING, _bind(lambda t, m, ins, outs, me: ins[t].at[me ^ SIBLING ^ m], t, m),
               _bind(lambda t, i, outs, sender: outs[t].at[i], t, i))
              for t in range(len(blocks)) for i, m in enumerate(SAME_CORE)]
    return Exchange(blocks, [jax.ShapeDtypeStruct((len(SAME_CORE),) + b.shape[1:], b.dtype) for b in blocks], {}, copies)


def scatter_over_ici(pair_sums, bufs, layer):
    n = len(pair_sums)
    copies = [(m, _bind(lambda t, i, ins, outs, me: ins[t].at[i], t, i),
               _bind(lambda t, i, outs, sender: outs[t].at[layer, i], t, i))
              for t in range(n) for i, m in enumerate(SAME_CORE) if m]
    local = [(_bind(lambda t, ins, outs, me: ins[t].at[0], t), _bind(lambda t, outs, me: outs[t].at[layer, 0], t))
             for t in range(n)]
    return Exchange(list(pair_sums) + list(bufs), [jax.ShapeDtypeStruct(b.shape, b.dtype) for b in bufs],
                    {n + t: t for t in range(n)}, copies, local)


def run_exchange(ex, name):
    any_spec = pl.BlockSpec(memory_space=pl.ANY)
    n_in, n_out = len(ex.inputs), len(ex.out_shapes)

    def body(*refs):
        ins, outs, sems = refs[:n_in], refs[n_in:n_in + n_out], refs[n_in + n_out:]
        ex.start(ins, outs, sems)
        ex.drain(ins, outs, sems)

    return pl.pallas_call(
        body, out_shape=ex.out_shapes, in_specs=[any_spec] * n_in, out_specs=[any_spec] * n_out,
        input_output_aliases=ex.aliases, scratch_shapes=ex.scratch, name=name)(*ex.inputs)


MATRICES = ("w_in", "w_attn_out", "w_conv_out", "pool_w", "w_o", "w_ffn_in", "w_ffn_out")
TRANSPOSED = ("w_in", "w_ffn_in")
MIXER_PART, FFN_PART = slice(0, 5), slice(5, 7)
EVERY = tuple(range(len(MATRICES)))
LATE = (0,)
EARLY = EVERY[1:]
SHARD_INFO = {
    "w_in": ((DEPTH, D_IN // N_DEV, D_MODEL), 1),
    "w_attn_out": ((DEPTH, D_ATTN, D_MODEL // N_DEV), 2),
    "w_conv_out": ((DEPTH, D_CONV, D_MODEL // N_DEV), 2),
    "pool_w": ((DEPTH, 4, 64, 256 // N_DEV), 3),
    "w_o": ((DEPTH, D_MODEL // N_DEV, D_MODEL), 1),
    "w_ffn_in": ((DEPTH, 2 * D_FF // N_DEV, D_MODEL), 1),
    "w_ffn_out": ((DEPTH, D_FF // N_DEV, D_MODEL), 1),
}


def _handled(name, t):
    return jnp.transpose(t, (0, 2, 1)) if name in TRANSPOSED else t
VECTORS = ("norm_mix_g", "forget_b", "q_norm_g", "k_norm_g", "pool_scale", "norm_ffn_g")
VECTOR_SHAPES = {"norm_mix_g": (DEPTH, D_MODEL), "forget_b": (DEPTH, HEADS), "q_norm_g": (DEPTH, HEAD_DIM),
                 "k_norm_g": (DEPTH, HEAD_DIM), "pool_scale": (DEPTH, D_MODEL), "norm_ffn_g": (DEPTH, D_MODEL)}
CONV_W_FULL = (DEPTH, 3, D_CONV)


def _size(shape):
    n = 1
    for v in shape:
        n *= v
    return n


def _pack(arrays, rows, cols):
    flat = jnp.concatenate([a.reshape(-1) for a in arrays])
    return jnp.pad(flat, (0, rows * cols - flat.shape[0])).reshape(rows, cols)


def _unpack(packed, shapes):
    flat, out, off = packed.reshape(-1), [], 0
    for shp in shapes:
        out.append(flat[off:off + _size(shp)].reshape(shp))
        off += _size(shp)
    return out


def _join_shards(stacked, axis):
    moved = jnp.moveaxis(stacked, 0, axis)
    shp = list(moved.shape)
    shp[axis:axis + 2] = [shp[axis] * shp[axis + 1]]
    return moved.reshape(shp)


def _cut_shards(full, axis):
    shp = list(full.shape)
    shp[axis:axis + 1] = [N_DEV, shp[axis] // N_DEV]
    return jnp.moveaxis(full.reshape(shp), axis, 0)


N_MOVED = 1544
SHARD_ROWS = D_IN // N_DEV


def _regroup_w_in(shards):
    wt = shards.reshape(D_IN, shards.shape[2])
    pad = jnp.zeros((N_FULL - D_IN, wt.shape[1]), wt.dtype)
    return jnp.concatenate([wt[N_MOVED:], wt[:N_MOVED], pad], axis=0)


def _ungroup_w_in(wpt):
    def kernel_rows(a, b):
        if b <= N_MOVED:
            return [wpt[a + D_IN - N_MOVED:b + D_IN - N_MOVED]]
        if a >= N_MOVED:
            return [wpt[a - N_MOVED:b - N_MOVED]]
        return kernel_rows(a, N_MOVED) + kernel_rows(N_MOVED, b)

    return jnp.stack([jnp.concatenate(kernel_rows(s * SHARD_ROWS, (s + 1) * SHARD_ROWS), axis=0) for s in range(N_DEV)])


def _pool_block_diag(w):
    out = jnp.zeros((D_POOL, D_MODEL), w.dtype)
    for g in range(4):
        out = lax.dynamic_update_slice(out, w[g], (g * 64, g * 256))
    return out


def _pool_from_block_diag(wbd):
    return jnp.stack([wbd[g * 64:(g + 1) * 64, g * 256:(g + 1) * 256] for g in range(4)])


def _layer_weights(mats, vec, conv_w, l):
    wp = _pool_block_diag(mats["pool_w"])
    row = lambda v: v.reshape(1, -1)
    fb = jnp.zeros((1, 128), F32).at[0, :HEADS].set(vec["forget_b"][l])
    cw = jnp.zeros((8, D_CONV), F32).at[:3].set(conv_w[l])
    twice = lambda v: jnp.tile(v.reshape(1, -1), (1, 2))
    return dict(
        wt_in=_regroup_w_in(mats["w_in"]), wt_ffn_in=mats["w_ffn_in"], w_ffn_out=mats["w_ffn_out"],
        wa=mats["w_attn_out"], wc=mats["w_conv_out"], wp=wp, wo=mats["w_o"],
        g_mix=row(vec["norm_mix_g"][l]), g_ffn=row(vec["norm_ffn_g"][l]), gq2=twice(vec["q_norm_g"][l]),
        gk2=twice(vec["k_norm_g"][l]), scale=row(vec["pool_scale"][l]), fb=fb, cw=cw)


def _layer_fwd(x, w, l, comm):
    (proj, h), half_mix = norm_matmul(x, w["g_mix"], w["wt_in"], N_MAIN, f"in_proj_{l}", comm.gather_ici(l + 1, MIXER_PART))
    z, c = forget_fwd(h, w["wt_in"], w["fb"], f"forget_fwd_{l}")
    qa, ka, va, vt = attn_prep(proj, c, w["gq2"], w["gk2"], f"attn_prep_{l}")
    (oa, lse), half_ffn = attn_forward(qa, ka, vt, f"attn_fwd_{l}", comm.gather_ici(l + 1, FFN_PART))
    half = list(half_mix) + list(half_ffn)
    x1 = mix_fwd(proj, oa, x, w["wa"], w["wc"], w["wp"], w["scale"], w["cw"], w["wo"], f"mix_fwd_{l}")
    (gu, h2), gathered = norm_matmul(x1, w["g_ffn"], w["wt_ffn_in"], 2 * D_FF, f"ffn_in_{l}", comm.gather_d2d(l + 1, half))
    x2 = swiglu_matmul(gu, w["w_ffn_out"], x1, f"ffn_out_{l}")
    saved = dict(x=x, proj=proj, h=h, z=z, qa=qa, ka=ka, va=va, oa=oa, lse=lse, x1=x1, gu=gu, h2=h2)
    return x2, saved, gathered


def _layer_bwd(dx2, sv, w, l, comm):
    g = {}
    (dgu, act), stage = swiglu_bwd(dx2, sv["gu"], w["w_ffn_out"], f"ffn_out_bwd_{l}", comm.scatter_d2d(l + 1))
    sums = comm.pair_sums(l + 1, stage)
    g["w_ffn_out"] = tn_matmul(act, dx2, f"dw_ffn_out_{l}")
    g["w_ffn_in"] = tn_matmul(dgu, sv["h2"], f"dw_ffn_in_{l}")
    (dx1, dg), _ = matmul_normbwd(dgu, w["wt_ffn_in"], sv["x1"], w["g_ffn"], dx2, f"ffn_in_bwd_{l}")
    g["norm_ffn_g"] = dg[0]

    (dproj, doa, a_tok, merged, dya, dyc, dyp, uc, dd, dscale, dcw) = mix_bwd(
        sv["proj"], sv["oa"], dx1, w["wa"], w["wc"], w["wp"], w["scale"], w["cw"], w["wo"], f"mix_bwd_{l}")
    g["w_o"] = tn_matmul(merged, dx1, f"dw_o_{l}")
    g["w_attn_out"] = tn_matmul(a_tok, dya, f"dw_attn_out_{l}")
    g["w_conv_out"] = tn_matmul(uc, dyc, f"dw_conv_out_{l}")
    g["pool_w"] = _pool_from_block_diag(tn_matmul(dd, dyp, f"dw_pool_{l}"))
    g["pool_scale"] = dscale[0]
    g["conv_w"] = dcw[:3]

    early = comm.early(l)
    comm.grads(l, g)
    above = comm.scatter_ici(l + 1, sums)
    (dqa, dka, dva), got = attn_backward(sv["qa"], sv["ka"], sv["va"], sv["oa"], doa, sv["lse"], f"attn_bwd_{l}",
                                         join_exchanges(above, comm.scatter_d2d(l, early) if early else None))
    n_above = len(above.out_shapes) if above else 0
    comm.scattered(got[:n_above])
    dproj, dc, dgq, dgk = attn_post(dqa, dka, dva, sv["proj"], w["gq2"], w["gk2"], dproj, f"attn_post_{l}")
    g["q_norm_g"] = dgq[0, :HEAD_DIM] + dgq[0, HEAD_DIM:]
    g["k_norm_g"] = dgk[0, :HEAD_DIM] + dgk[0, HEAD_DIM:]
    dproj, db = forget_bwd(dc, sv["z"], dproj, f"forget_bwd_{l}")
    g["forget_b"] = db[0, :HEADS]

    g["w_in"] = _ungroup_w_in(tn_matmul(dproj, sv["h"], f"dw_in_{l}", m_cols=N_FULL))
    early_ici = comm.scatter_ici(l, comm.pair_sums(l, got[n_above:], early), early) if early else None
    (dx, dg), got = matmul_normbwd(dproj, w["wt_in"], sv["x"], w["g_mix"], dx1, f"in_proj_bwd_{l}", k=N_FULL, ex=early_ici)
    comm.scattered(got, early)
    g["norm_mix_g"] = dg[0]
    comm.grads(l, g)
    return dx


def _local_step(x, tgt, comm):
    ws, saved = [], []
    w = comm.weights(0, None)
    for l in range(DEPTH):
        ws.append(w)
        x, sv, gathered = _layer_fwd(x, w, l, comm)
        saved.append(sv)
        if l + 1 < DEPTH:
            w = comm.weights(l + 1, gathered)
    sq, dx = loss_kernel(x, tgt, "loss")
    for l in reversed(range(DEPTH)):
        dx = _layer_bwd(dx, saved[l], ws[l], l, comm)
    comm.finish()
    return sq[0, 0], dx


def kernel(x, norm_mix_g, w_in, forget_b, q_norm_g, k_norm_g, w_attn_out, conv_w, w_conv_out, pool_w, pool_scale, w_o, norm_ffn_g, w_ffn_in, w_ffn_out, loss_target, m_norm_mix_g, m_w_in, m_forget_b, m_q_norm_g, m_k_norm_g, m_w_attn_out, m_conv_w, m_w_conv_out, m_pool_w, m_pool_scale, m_w_o, m_norm_ffn_g, m_w_ffn_in, m_w_ffn_out, v_norm_mix_g, v_w_in, v_forget_b, v_q_norm_g, v_k_norm_g, v_w_attn_out, v_conv_w, v_w_conv_out, v_pool_w, v_pool_scale, v_w_o, v_norm_ffn_g, v_w_ffn_in, v_w_ffn_out):
    w = dict(norm_mix_g=norm_mix_g, w_in=w_in, forget_b=forget_b, q_norm_g=q_norm_g, k_norm_g=k_norm_g,
             w_attn_out=w_attn_out, conv_w=conv_w, w_conv_out=w_conv_out, pool_w=pool_w, pool_scale=pool_scale,
             w_o=w_o, norm_ffn_g=norm_ffn_g, w_ffn_in=w_ffn_in, w_ffn_out=w_ffn_out)
    m = dict(norm_mix_g=m_norm_mix_g, w_in=m_w_in, forget_b=m_forget_b, q_norm_g=m_q_norm_g, k_norm_g=m_k_norm_g,
             w_attn_out=m_w_attn_out, conv_w=m_conv_w, w_conv_out=m_w_conv_out, pool_w=m_pool_w,
             pool_scale=m_pool_scale, w_o=m_w_o, norm_ffn_g=m_norm_ffn_g, w_ffn_in=m_w_ffn_in, w_ffn_out=m_w_ffn_out)
    v = dict(norm_mix_g=v_norm_mix_g, w_in=v_w_in, forget_b=v_forget_b, q_norm_g=v_q_norm_g, k_norm_g=v_k_norm_g,
             w_attn_out=v_w_attn_out, conv_w=v_conv_w, w_conv_out=v_w_conv_out, pool_w=v_pool_w,
             pool_scale=v_pool_scale, w_o=v_w_o, norm_ffn_g=v_norm_ffn_g, w_ffn_in=v_w_ffn_in, w_ffn_out=v_w_ffn_out)
    me = 4 * lax.axis_index("x") + 2 * lax.axis_index("y") + lax.axis_index("c")
    layer_shard = {n: SHARD_INFO[n][0][1:] for n in MATRICES}
    cut_axis = {n: SHARD_INFO[n][1] - 1 for n in MATRICES}

    conv_g = all_gather([_pack([conv_w], 8, 128)], "gather_conv_w")[0]
    conv_full = _join_shards(jnp.stack([_unpack(conv_g[i], [conv_w.shape])[0] for i in range(N_DEV)]), 2)
    vec = {n: w[n] for n in VECTORS}

    rc = {n: (_size(layer_shard[n][:-1]), layer_shard[n][-1]) for n in MATRICES}

    class Comm:
        bufs = [lax.empty((DEPTH, len(SAME_CORE)) + layer_shard[n], BF16) for n in MATRICES]
        blocks = [None] * DEPTH
        small_g = [None] * DEPTH

        @staticmethod
        def shards(l):
            return [_handled(n, w[n])[l].astype(BF16) for n in MATRICES]

        @staticmethod
        def gather_ici(l, part):
            return gather_over_ici(Comm.shards(l)[part]) if l < DEPTH else None

        @staticmethod
        def gather_d2d(l, half):
            return gather_over_d2d(half) if l < DEPTH else None

        @staticmethod
        def weights(l, gathered):
            if l == 0:
                gathered = all_gather(Comm.shards(0), "gather_0")
            mats = {n: t if n == "w_in" else _join_shards(t, cut_axis[n]) for n, t in zip(MATRICES, gathered)}
            return _layer_weights(mats, vec, conv_full, l)

        @staticmethod
        def grads(l, g):
            Comm.small_g[l] = g
            Comm.blocks[l] = [None if n not in g else g[n] if n == "w_in" else _cut_shards(g[n], cut_axis[n])
                              for n in MATRICES]

        @staticmethod
        def early(l):
            return EARLY if l == 0 else None

        @staticmethod
        def scatter_d2d(l, which=EVERY):
            return scatter_over_d2d([Comm.blocks[l][t] for t in which]) if l < DEPTH else None

        @staticmethod
        def pair_sums(l, stage, which=EVERY):
            if l >= DEPTH:
                return None
            return [pair_sum(Comm.blocks[l][t].reshape((N_DEV,) + rc[MATRICES[t]]),
                             s.reshape((len(SAME_CORE),) + rc[MATRICES[t]]), me,
                             f"pair_sum_{MATRICES[t]}_{l}").reshape(s.shape) for t, s in zip(which, stage)]

        @staticmethod
        def scatter_ici(l, sums, which=EVERY):
            return scatter_over_ici(sums, [Comm.bufs[t] for t in which], l) if l < DEPTH else None

        @staticmethod
        def scattered(results, which=EVERY):
            for t, r in zip(which or (), results):
                Comm.bufs[t] = r

        @staticmethod
        def finish():
            stage = run_exchange(Comm.scatter_d2d(0, LATE), "scatter_d2d_0")
            Comm.scattered(run_exchange(Comm.scatter_ici(0, Comm.pair_sums(0, stage, LATE), LATE), "scatter_ici_0"), LATE)

    small_g, received = Comm.small_g, Comm
    sq, dx = _local_step(x[0], loss_target[0], Comm)
    loss = lax.psum(0.5 * sq / D_MODEL, ("x", "y", "c"))

    big = {}
    for n, parts in zip(MATRICES, received.bufs):
        outs = adamw_sum(parts.reshape((DEPTH, len(SAME_CORE)) + rc[n]),
                         *[_handled(n, d[n]).reshape((DEPTH,) + rc[n]) for d in (w, m, v)], f"adamw_{n}")
        big[n] = [_handled(n, t.reshape((DEPTH,) + layer_shard[n])) for t in outs]

    small_shapes = [VECTOR_SHAPES[n] for n in VECTORS] + [CONV_W_FULL]
    stacked = [jnp.stack([small_g[l][n] for l in range(DEPTH)]) for n in VECTORS + ("conv_w",)]
    sparts = all_gather([_pack(stacked, SMALL_ROWS, 128)], "gather_vector_grads")[0]
    col0 = me * (D_CONV // N_DEV)
    place = lambda t: lax.dynamic_update_slice(jnp.zeros(CONV_W_FULL, F32), t, (0, 0, col0))
    spacked = [_pack([d[n] for n in VECTORS] + [place(d["conv_w"])], SMALL_ROWS, 128)[None] for d in (w, m, v)]
    small = [_unpack(t[0], small_shapes) for t in adamw_sum(sparts[None], *spacked, "adamw_vectors")]

    def result(kind):
        out = {n: big[n][kind] for n in MATRICES}
        out.update({n: small[kind][j] for j, n in enumerate(VECTORS)})
        out["conv_w"] = lax.dynamic_slice(small[kind][len(VECTORS)], (0, 0, col0), conv_w.shape)
        return [out[n] for n in w]

    return (loss, dx[None], *result(0), *result(1), *result(2), *result(3))
```

```python
import functools

import jax
import jax.numpy as jnp
from jax import lax
from jax.experimental import pallas as pl
from jax.experimental.pallas import tpu as pltpu

F32 = jnp.float32
BF16 = jnp.bfloat16

N_DEV = 8
DEPTH = 4
D_MODEL = 1024
HEAD_DIM = 64
HEADS = 8
D_ATTN = 512
D_CONV = 256
D_POOL = 256
D_FF = 2816
D_IN = 5640
EPS = 1e-6
ATTN_SCALE = HEAD_DIM ** -0.5

N_REST = 4096
N_MAIN = 5632
N_FULL = 5760
DPROJ_TAIL = 2048
DPROJ_COLS = N_REST + DPROJ_TAIL
FF_BLK = 256
N_FF_BLKS = D_FF // FF_BLK
HALO = 16

ADAM_LR = 0.001
ADAM_B1 = 0.9
ADAM_B2 = 0.999
ADAM_EPS = 1e-08
ADAM_WD = 0.01
ADAM_STEP = 10

PACK_COLS = 1024
PACK_ROWS = 8192
SMALL_ROWS = 128

VMEM_LIMIT = 48 * 2 ** 20


def _cparams(sem, vmem=None):
    return pltpu.CompilerParams(dimension_semantics=sem, vmem_limit_bytes=vmem or VMEM_LIMIT)


def _pick(n, cands):
    for c in cands:
        if n % c == 0:
            return c
    raise ValueError(f"no tile for {n}")


def _sigmoid(v):
    return 1.0 / (1.0 + jnp.exp(-v))


def _rstd(v):
    return lax.rsqrt(jnp.mean(v * v, axis=-1, keepdims=True) + EPS)


def _dot(a, b):
    return jnp.dot(a, b, preferred_element_type=F32)


def _dot_tn(a, b):
    return lax.dot_general(a, b, (((0,), (0,)), ((), ())), preferred_element_type=F32)


def _dot_nt(a, b):
    return lax.dot_general(a, b, (((1,), (1,)), ((), ())), preferred_element_type=F32)


def norm_matmul(x, g, wt, n_cols, name, ex=None):
    s, d = x.shape
    tm, tn = min(1024, s), _pick(n_cols, (2816, 1408, 512))

    def body(x_ref, g_ref, w_ref, o_ref, h_ref):
        @pl.when(pl.program_id(1) == 0)
        def _():
            xv = x_ref[...]
            h_ref[...] = (xv * _rstd(xv) * g_ref[...]).astype(BF16)

        o_ref[...] = _dot_nt(h_ref[...], w_ref[...]).astype(BF16)

    return _carried_call(
        body, ex, (s // tm, n_cols // tn),
        [pl.BlockSpec((tm, d), lambda i, j: (i, 0)), pl.BlockSpec((1, d), lambda i, j: (0, 0)),
         pl.BlockSpec((tn, d), lambda i, j: (j, 0))],
        [pl.BlockSpec((tm, tn), lambda i, j: (i, j)), pl.BlockSpec((tm, d), lambda i, j: (i, 0))],
        [jax.ShapeDtypeStruct((s, n_cols), BF16), jax.ShapeDtypeStruct((s, d), BF16)], [],
        ("arbitrary", "arbitrary"), name, (x, g, wt))


def tn_matmul(a, b, name, m_cols=None):
    t = a.shape[0]
    m = m_cols or a.shape[1]
    n = b.shape[1]
    tk = min(1024, t)
    tmm = _pick(m, (1408, 1152, 1024, 512, 256))
    tn = _pick(n, (1408, 1152, 1024, 512, 128))
    nk = t // tk

    def body(a_ref, b_ref, o_ref, acc_ref):
        @pl.when(pl.program_id(2) == 0)
        def _():
            acc_ref[...] = jnp.zeros_like(acc_ref)

        acc_ref[...] += _dot_tn(a_ref[...].astype(BF16), b_ref[...].astype(BF16))

        @pl.when(pl.program_id(2) == nk - 1)
        def _():
            o_ref[...] = acc_ref[...].astype(BF16)

    return pl.pallas_call(
        body, grid=(m // tmm, n // tn, nk),
        in_specs=[pl.BlockSpec((tk, tmm), lambda i, j, k: (k, i)), pl.BlockSpec((tk, tn), lambda i, j, k: (k, j))],
        out_specs=pl.BlockSpec((tmm, tn), lambda i, j, k: (i, j)),
        out_shape=jax.ShapeDtypeStruct((m, n), BF16), scratch_shapes=[pltpu.VMEM((tmm, tn), F32)],
        compiler_params=_cparams(("parallel", "parallel", "arbitrary")), name=name)(a, b)


def matmul_normbwd(a, wt, x, g, dres, name, k=None, ex=None):
    s = a.shape[0]
    k = k or a.shape[1]
    d = wt.shape[1]
    tm = min(1024, s)
    tk = _pick(k, (1408, 1152, 512))
    nk = k // tk

    def body(a_ref, w_ref, x_ref, g_ref, r_ref, dx_ref, dg_ref, acc_ref):
        i, kk = pl.program_id(0), pl.program_id(1)

        @pl.when(kk == 0)
        def _():
            acc_ref[...] = jnp.zeros_like(acc_ref)

        @pl.when((i == 0) & (kk == 0))
        def _():
            dg_ref[...] = jnp.zeros_like(dg_ref)

        acc_ref[...] += _dot(a_ref[...], w_ref[...])

        @pl.when(kk == nk - 1)
        def _():
            xv = x_ref[...]
            r = _rstd(xv)
            y = xv * r
            dh = acc_ref[...]
            dy = dh * g_ref[...]
            dx_ref[...] = r_ref[...] + r * (dy - y * jnp.mean(dy * y, axis=-1, keepdims=True))
            dg_ref[...] += jnp.sum(dh * y, axis=0, keepdims=True)

    return _carried_call(
        body, ex, (s // tm, nk),
        [pl.BlockSpec((tm, tk), lambda i, kk: (i, kk)), pl.BlockSpec((tk, d), lambda i, kk: (kk, 0)),
         pl.BlockSpec((tm, d), lambda i, kk: (i, 0)), pl.BlockSpec((1, d), lambda i, kk: (0, 0)),
         pl.BlockSpec((tm, d), lambda i, kk: (i, 0))],
        [pl.BlockSpec((tm, d), lambda i, kk: (i, 0)), pl.BlockSpec((1, d), lambda i, kk: (0, 0))],
        [jax.ShapeDtypeStruct((s, d), F32), jax.ShapeDtypeStruct((1, d), F32)],
        [pltpu.VMEM((tm, d), F32)], ("arbitrary", "arbitrary"), name, (a, wt, x, g, dres), vmem=56 * 2 ** 20)


def swiglu_matmul(gu, w, x1, name):
    s = gu.shape[0]
    d = w.shape[1]
    tm = min(512, s)

    def body(gu_ref, w_ref, x_ref, o_ref):
        acc = x_ref[...]
        for j in range(N_FF_BLKS):
            gt = gu_ref[:, j * FF_BLK:(j + 1) * FF_BLK].astype(F32)
            up = gu_ref[:, D_FF + j * FF_BLK:D_FF + (j + 1) * FF_BLK].astype(F32)
            act = (gt * _sigmoid(gt) * up).astype(BF16)
            acc += _dot(act, w_ref[j * FF_BLK:(j + 1) * FF_BLK, :])
        o_ref[...] = acc

    return pl.pallas_call(
        body, grid=(s // tm,),
        in_specs=[pl.BlockSpec((tm, 2 * D_FF), lambda i: (i, 0)), pl.BlockSpec((D_FF, d), lambda i: (0, 0)),
                  pl.BlockSpec((tm, d), lambda i: (i, 0))],
        out_specs=pl.BlockSpec((tm, d), lambda i: (i, 0)),
        out_shape=jax.ShapeDtypeStruct((s, d), F32),
        compiler_params=_cparams(("parallel",)), name=name)(gu, w, x1)


def swiglu_bwd(dx2, gu, w, name, ex=None):
    s, d = dx2.shape
    tm = min(512, s)

    def body(dx_ref, gu_ref, w_ref, dgu_ref, act_ref):
        dx = dx_ref[...].astype(BF16)
        for j in range(N_FF_BLKS):
            g_cols = slice(j * FF_BLK, (j + 1) * FF_BLK)
            u_cols = slice(D_FF + j * FF_BLK, D_FF + (j + 1) * FF_BLK)
            dact = _dot_nt(dx, w_ref[j * FF_BLK:(j + 1) * FF_BLK, :])
            gt = gu_ref[:, g_cols].astype(F32)
            up = gu_ref[:, u_cols].astype(F32)
            sg = _sigmoid(gt)
            act_ref[:, j * FF_BLK:(j + 1) * FF_BLK] = (gt * sg * up).astype(BF16)
            dgu_ref[:, g_cols] = (dact * up * (sg * (1.0 + gt * (1.0 - sg)))).astype(BF16)
            dgu_ref[:, u_cols] = (dact * gt * sg).astype(BF16)

    return _carried_call(
        body, ex, (s // tm,),
        [pl.BlockSpec((tm, d), lambda i: (i, 0)), pl.BlockSpec((tm, 2 * D_FF), lambda i: (i, 0)),
         pl.BlockSpec((D_FF, d), lambda i: (0, 0), pipeline_mode=pl.Buffered(1))],
        [pl.BlockSpec((tm, 2 * D_FF), lambda i: (i, 0)), pl.BlockSpec((tm, D_FF), lambda i: (i, 0))],
        [jax.ShapeDtypeStruct((s, 2 * D_FF), BF16), jax.ShapeDtypeStruct((s, D_FF), BF16)], [],
        ("arbitrary",), name, (dx2, gu, w), vmem=56 * 2 ** 20)


def loss_kernel(y, tgt, name):
    s, d = y.shape
    tm = min(512, s)

    def body(y_ref, t_ref, l_ref, dy_ref):
        @pl.when(pl.program_id(0) == 0)
        def _():
            l_ref[...] = jnp.zeros_like(l_ref)

        err = y_ref[...] - t_ref[...]
        dy_ref[...] = err * (1.0 / d)
        l_ref[...] += jnp.sum(jnp.sum(err * err, axis=1, keepdims=True), axis=0, keepdims=True)

    return pl.pallas_call(
        body, grid=(s // tm,),
        in_specs=[pl.BlockSpec((tm, d), lambda i: (i, 0)), pl.BlockSpec((tm, d), lambda i: (i, 0))],
        out_specs=[pl.BlockSpec((8, 128), lambda i: (0, 0)), pl.BlockSpec((tm, d), lambda i: (i, 0))],
        out_shape=[jax.ShapeDtypeStruct((8, 128), F32), jax.ShapeDtypeStruct((s, d), F32)],
        compiler_params=_cparams(("arbitrary",)), name=name)(y, tgt)


def _split3(v):
    a1 = v.astype(BF16)
    r1 = v - a1.astype(F32)
    a2 = r1.astype(BF16)
    a3 = (r1 - a2.astype(F32)).astype(BF16)
    return a1, a2, a3


def forget_fwd(h, wt_in, b, name):
    s, d = h.shape
    tm = min(512, s)

    def body(h_ref, w_ref, b_ref, z_ref, c_ref, carry_ref):
        @pl.when(pl.program_id(0) == 0)
        def _():
            carry_ref[...] = jnp.zeros_like(carry_ref)

        z = _dot_nt(h_ref[...], w_ref[...]) + b_ref[...]
        z_ref[...] = z
        logf = jnp.minimum(z, 0.0) - jnp.log(1.0 + jnp.exp(-jnp.abs(z)))
        row = lax.broadcasted_iota(jnp.int32, (tm, tm), 0)
        col = lax.broadcasted_iota(jnp.int32, (tm, tm), 1)
        tri = (row >= col).astype(BF16)
        a1, a2, a3 = _split3(logf)
        c = _dot(tri, a1) + _dot(tri, a2) + _dot(tri, a3) + carry_ref[...]
        c_ref[...] = c
        carry_ref[...] = c[tm - 1:tm, :]

    return pl.pallas_call(
        body, grid=(s // tm,),
        in_specs=[pl.BlockSpec((tm, d), lambda i: (i, 0)), pl.BlockSpec((128, d), lambda i: (N_MAIN // 128, 0)),
                  pl.BlockSpec((1, 128), lambda i: (0, 0))],
        out_specs=[pl.BlockSpec((tm, 128), lambda i: (i, 0)), pl.BlockSpec((tm, 128), lambda i: (i, 0))],
        out_shape=[jax.ShapeDtypeStruct((s, 128), F32), jax.ShapeDtypeStruct((s, 128), F32)],
        scratch_shapes=[pltpu.VMEM((1, 128), F32)],
        compiler_params=_cparams(("arbitrary",)), name=name)(h, wt_in, b)


def forget_bwd(dc, z, dproj, name):
    s = dc.shape[0]
    tm = min(512, s)
    nt = s // tm

    def body(dc_ref, z_ref, dp_ref, dz_ref, db_ref, carry_ref):
        @pl.when(pl.program_id(0) == 0)
        def _():
            carry_ref[...] = jnp.zeros_like(carry_ref)
            db_ref[...] = jnp.zeros_like(db_ref)

        row = lax.broadcasted_iota(jnp.int32, (tm, tm), 0)
        col = lax.broadcasted_iota(jnp.int32, (tm, tm), 1)
        tri = (col >= row).astype(BF16)
        a1, a2, a3 = _split3(dc_ref[...])
        dlogf = _dot(tri, a1) + _dot(tri, a2) + _dot(tri, a3) + carry_ref[...]
        carry_ref[...] = dlogf[0:1, :]
        dz = dlogf * (1.0 - _sigmoid(z_ref[...]))
        dz_ref[...] = dz.astype(BF16)
        db_ref[...] += jnp.sum(dz, axis=0, keepdims=True)

    return pl.pallas_call(
        body, grid=(nt,),
        in_specs=[pl.BlockSpec((tm, 128), lambda i: (nt - 1 - i, 0)), pl.BlockSpec((tm, 128), lambda i: (nt - 1 - i, 0)),
                  pl.BlockSpec(memory_space=pl.ANY)],
        out_specs=[pl.BlockSpec((tm, 128), lambda i: (nt - 1 - i, N_MAIN // 128)), pl.BlockSpec((1, 128), lambda i: (0, 0))],
        out_shape=[jax.ShapeDtypeStruct(dproj.shape, BF16), jax.ShapeDtypeStruct((1, 128), F32)],
        scratch_shapes=[pltpu.VMEM((1, 128), F32)], input_output_aliases={2: 0},
        compiler_params=_cparams(("arbitrary",)), name=name)(dc, z, dproj)


HEAD_GROUP = 4
LANE_C = 64
LANE_ONE = 67


def _lanes():
    lane = lax.broadcasted_iota(jnp.int32, (1, 128), 1)
    return lane, lane < HEAD_DIM


def _half_mean(t, lo):
    s_lo = jnp.sum(jnp.where(lo, t, 0.0), axis=-1, keepdims=True)
    s_hi = jnp.sum(jnp.where(lo, 0.0, t), axis=-1, keepdims=True)
    return jnp.where(lo, s_lo, s_hi) * (1.0 / HEAD_DIM)


def _lane_col(t, lane, idx):
    return jnp.sum(jnp.where(lane == idx, t, 0.0), axis=-1, keepdims=True)


def _swap_halves(t):
    return pltpu.roll(t, HEAD_DIM, 1)


def _causal(s_blk, tq, tk):
    row = lax.broadcasted_iota(jnp.int32, (tq, tk), 0)
    col = lax.broadcasted_iota(jnp.int32, (tq, tk), 1)
    return jnp.where(row >= col, s_blk, -jnp.inf)


def attn_prep(proj, c, gq2, gk2, name):
    s = proj.shape[0]
    tm = min(512, s)
    first = N_REST // 128

    def body(q_ref, k_ref, v_ref, c_ref, gq_ref, gk_ref, qa_ref, ka_ref, va_ref, vt_ref):
        j = pl.program_id(1)
        lane, lo = _lanes()

        def normed(ref, g):
            t = ref[...].astype(F32)
            return t * lax.rsqrt(_half_mean(t * t, lo) + EPS) * g

        qn = normed(q_ref, gq_ref[...] * ATTN_SCALE)
        kn = normed(k_ref, gk_ref[...])
        vv = v_ref[...].astype(F32)
        cv = c_ref[...]
        one_q = jnp.where((lane >= LANE_ONE) & (lane < LANE_ONE + 3), 1.0, 0.0)
        one_k = jnp.where((lane >= LANE_C) & (lane < LANE_C + 3), 1.0, 0.0)
        one_v = jnp.where(lane == LANE_C, 1.0, 0.0)
        for e in range(2):
            pick = (lambda t: t) if e == 0 else _swap_halves
            pieces = [p.astype(F32) for p in _split3(_lane_col(cv, lane, 2 * j + e))]
            ext_q, ext_k = one_q, one_k
            for i, p in enumerate(pieces):
                ext_q = jnp.where(lane == LANE_C + i, p, ext_q)
                ext_k = jnp.where(lane == LANE_ONE + i, -p, ext_k)
            qa_ref[e] = jnp.where(lo, pick(qn), ext_q).astype(BF16)
            ka_ref[e] = jnp.where(lo, pick(kn), ext_k).astype(BF16)
            va = jnp.where(lo, pick(vv), one_v)
            va_ref[e] = va.astype(BF16)
            vt_ref[e] = va.T.astype(BF16)

    tile = lambda base: pl.BlockSpec((tm, 128), lambda i, j: (i, base + j))
    vec = pl.BlockSpec((1, 128), lambda i, j: (0, 0))
    out = pl.BlockSpec((2, tm, 128), lambda i, j: (j, i, 0))
    return pl.pallas_call(
        body, grid=(s // tm, HEADS // 2),
        in_specs=[tile(first), tile(first + 4), tile(first + 8), pl.BlockSpec((tm, 128), lambda i, j: (i, 0)), vec, vec],
        out_specs=[out, out, out, pl.BlockSpec((2, 128, tm), lambda i, j: (j, 0, i))],
        out_shape=[jax.ShapeDtypeStruct((HEADS, s, 128), BF16)] * 3 + [jax.ShapeDtypeStruct((HEADS, 128, s), BF16)],
        compiler_params=_cparams(("parallel", "arbitrary")), name=name)(proj, proj, proj, c, gq2, gk2)


def attn_fwd(q, k, v, ccol, crow, gq, gk, name):
    hh, s, hd = q.shape
    tq = tk = min(512, s)
    nq = s // tq

    def body(q_ref, k_ref, v_ref, cc_ref, cr_ref, gq_ref, gk_ref, o_ref, lse_ref, qn_ref, m_ref, l_ref, acc_ref):
        qi, ki = pl.program_id(1), pl.program_id(2)

        @pl.when(ki == 0)
        def _():
            qn_ref[...] = _qk_hat(q_ref, gq_ref, ATTN_SCALE)
            m_ref[...] = jnp.full_like(m_ref, -jnp.inf)
            l_ref[...] = jnp.zeros_like(l_ref)
            acc_ref[...] = jnp.zeros_like(acc_ref)

        @pl.when(ki <= qi)
        def _():
            kn = _qk_hat(k_ref, gk_ref, 1.0)
            sb = _dot_nt(qn_ref[...], kn) + (cc_ref[...] - cr_ref[...])
            sb = _causal(sb, qi, ki, tq, tk)
            m_new = jnp.maximum(m_ref[...], jnp.max(sb, axis=-1, keepdims=True))
            alpha = jnp.exp(m_ref[...] - m_new)
            p = jnp.exp(sb - m_new)
            l_ref[...] = alpha * l_ref[...] + jnp.sum(p, axis=-1, keepdims=True)
            acc_ref[...] = alpha * acc_ref[...] + _dot(p.astype(BF16), v_ref[...])
            m_ref[...] = m_new

        @pl.when(ki == qi)
        def _():
            o_ref[...] = (acc_ref[...] / l_ref[...]).astype(BF16)
            lse_ref[...] = m_ref[...] + jnp.log(l_ref[...])

    qspec = pl.BlockSpec((None, tq, hd), lambda h, i, j: (h, i, 0))
    kspec = pl.BlockSpec((None, tk, hd), lambda h, i, j: (h, jnp.minimum(i, j), 0))
    gspec = pl.BlockSpec((1, hd), lambda h, i, j: (0, 0))
    return pl.pallas_call(
        body, grid=(hh, nq, nq),
        in_specs=[qspec, kspec, kspec,
                  pl.BlockSpec((None, tq, 1), lambda h, i, j: (h, i, 0)),
                  pl.BlockSpec((None, 1, tk), lambda h, i, j: (h, 0, jnp.minimum(i, j))), gspec, gspec],
        out_specs=[qspec, pl.BlockSpec((None, tq, 1), lambda h, i, j: (h, i, 0))],
        out_shape=[jax.ShapeDtypeStruct((hh, s, hd), BF16), jax.ShapeDtypeStruct((hh, s, 1), F32)],
        scratch_shapes=[pltpu.VMEM((tq, hd), BF16), pltpu.VMEM((tq, 1), F32), pltpu.VMEM((tq, 1), F32),
                        pltpu.VMEM((tq, hd), F32)],
        compiler_params=_cparams(("parallel", "parallel", "arbitrary")), name=name)(q, k, v, ccol, crow, gq, gk)


def attn_bwd_dq(q, k, v, o, do, lse, ccol, crow, gq, gk, name):
    hh, s, hd = q.shape
    tq = tk = min(512, s)
    nq = s // tq

    def body(q_ref, k_ref, v_ref, o_ref, do_ref, lse_ref, cc_ref, cr_ref, gq_ref, gk_ref,
             dq_ref, dcc_ref, dg_ref, qn_ref, dl_ref, acc_ref, dca_ref):
        h, qi, ki = pl.program_id(0), pl.program_id(1), pl.program_id(2)

        @pl.when((h == 0) & (qi == 0) & (ki == 0))
        def _():
            dg_ref[...] = jnp.zeros_like(dg_ref)

        @pl.when(ki == 0)
        def _():
            qn_ref[...] = _qk_hat(q_ref, gq_ref, ATTN_SCALE)
            dl_ref[...] = jnp.sum(do_ref[...].astype(F32) * o_ref[...].astype(F32), axis=-1, keepdims=True)
            acc_ref[...] = jnp.zeros_like(acc_ref)
            dca_ref[...] = jnp.zeros_like(dca_ref)

        @pl.when(ki <= qi)
        def _():
            kn = _qk_hat(k_ref, gk_ref, 1.0)
            sb = _dot_nt(qn_ref[...], kn) + (cc_ref[...] - cr_ref[...])
            p = jnp.exp(_causal(sb, qi, ki, tq, tk) - lse_ref[...])
            dp = _dot_nt(do_ref[...], v_ref[...])
            ds = p * (dp - dl_ref[...])
            acc_ref[...] += _dot(ds.astype(BF16), kn)
            dca_ref[...] += jnp.sum(ds, axis=-1, keepdims=True)

        @pl.when(ki == qi)
        def _():
            dq, dg = _norm_bwd(q_ref[...].astype(F32), gq_ref[...], acc_ref[...], ATTN_SCALE)
            dq_ref[...] = dq.astype(BF16)
            dcc_ref[...] = dca_ref[...]
            dg_ref[...] += dg

    qspec = pl.BlockSpec((None, tq, hd), lambda h, i, j: (h, i, 0))
    kspec = pl.BlockSpec((None, tk, hd), lambda h, i, j: (h, jnp.minimum(i, j), 0))
    cspec = pl.BlockSpec((None, tq, 1), lambda h, i, j: (h, i, 0))
    gspec = pl.BlockSpec((1, hd), lambda h, i, j: (0, 0))
    return pl.pallas_call(
        body, grid=(hh, nq, nq),
        in_specs=[qspec, kspec, kspec, qspec, qspec, cspec, cspec,
                  pl.BlockSpec((None, 1, tk), lambda h, i, j: (h, 0, jnp.minimum(i, j))), gspec, gspec],
        out_specs=[qspec, cspec, gspec],
        out_shape=[jax.ShapeDtypeStruct((hh, s, hd), BF16), jax.ShapeDtypeStruct((hh, s, 1), F32),
                   jax.ShapeDtypeStruct((1, hd), F32)],
        scratch_shapes=[pltpu.VMEM((tq, hd), BF16), pltpu.VMEM((tq, 1), F32), pltpu.VMEM((tq, hd), F32),
                        pltpu.VMEM((tq, 1), F32)],
        compiler_params=_cparams(("arbitrary", "arbitrary", "arbitrary")), name=name)(
            q, k, v, o, do, lse, ccol, crow, gq, gk)


def attn_bwd_dkv(q, k, v, o, do, lse, ccol, crow, gq, gk, name):
    hh, s, hd = q.shape
    tq = tk = min(512, s)
    nq = s // tq

    def body(q_ref, k_ref, v_ref, o_ref, do_ref, lse_ref, cc_ref, cr_ref, gq_ref, gk_ref,
             dk_ref, dv_ref, dcr_ref, dg_ref, kn_ref, dka_ref, dva_ref, dca_ref):
        h, ki, qi = pl.program_id(0), pl.program_id(1), pl.program_id(2)

        @pl.when((h == 0) & (ki == 0) & (qi == 0))
        def _():
            dg_ref[...] = jnp.zeros_like(dg_ref)

        @pl.when(qi == 0)
        def _():
            kn_ref[...] = _qk_hat(k_ref, gk_ref, 1.0)
            dka_ref[...] = jnp.zeros_like(dka_ref)
            dva_ref[...] = jnp.zeros_like(dva_ref)
            dca_ref[...] = jnp.zeros_like(dca_ref)

        @pl.when(qi >= ki)
        def _():
            qn = _qk_hat(q_ref, gq_ref, ATTN_SCALE)
            do = do_ref[...]
            delta = jnp.sum(do.astype(F32) * o_ref[...].astype(F32), axis=-1, keepdims=True)
            sb = _dot_nt(qn, kn_ref[...]) + (cc_ref[...] - cr_ref[...])
            p = jnp.exp(_causal(sb, qi, ki, tq, tk) - lse_ref[...])
            dva_ref[...] += _dot_tn(p.astype(BF16), do)
            ds = p * (_dot_nt(do, v_ref[...]) - delta)
            dka_ref[...] += _dot_tn(ds.astype(BF16), qn)
            dca_ref[...] += jnp.sum(ds, axis=0, keepdims=True)

        @pl.when(qi == nq - 1)
        def _():
            dk, dg = _norm_bwd(k_ref[...].astype(F32), gk_ref[...], dka_ref[...], 1.0)
            dk_ref[...] = dk.astype(BF16)
            dv_ref[...] = dva_ref[...].astype(BF16)
            dcr_ref[...] = dca_ref[...]
            dg_ref[...] += dg

    kspec = pl.BlockSpec((None, tk, hd), lambda h, j, i: (h, j, 0))
    qspec = pl.BlockSpec((None, tq, hd), lambda h, j, i: (h, jnp.maximum(i, j), 0))
    cspec = pl.BlockSpec((None, tq, 1), lambda h, j, i: (h, jnp.maximum(i, j), 0))
    rspec = pl.BlockSpec((None, 1, tk), lambda h, j, i: (h, 0, j))
    gspec = pl.BlockSpec((1, hd), lambda h, j, i: (0, 0))
    return pl.pallas_call(
        body, grid=(hh, nq, nq),
        in_specs=[qspec, kspec, kspec, qspec, qspec, cspec, cspec, rspec, gspec, gspec],
        out_specs=[kspec, kspec, rspec, gspec],
        out_shape=[jax.ShapeDtypeStruct((hh, s, hd), BF16), jax.ShapeDtypeStruct((hh, s, hd), BF16),
                   jax.ShapeDtypeStruct((hh, 1, s), F32), jax.ShapeDtypeStruct((1, hd), F32)],
        scratch_shapes=[pltpu.VMEM((tk, hd), BF16), pltpu.VMEM((tk, hd), F32), pltpu.VMEM((tk, hd), F32),
                        pltpu.VMEM((1, tk), F32)],
        compiler_params=_cparams(("arbitrary", "arbitrary", "arbitrary")), name=name)(
            q, k, v, o, do, lse, ccol, crow, gq, gk)


def _carry(ex, n_in, n_out, n_scratch, grid):
    n_xin, n_xout = (len(ex.inputs), len(ex.out_shapes)) if ex else (0, 0)

    def split(refs):
        ins, xins = refs[:n_in], refs[n_in:n_in + n_xin]
        rest = refs[n_in + n_xin:]
        outs, xouts = rest[:n_out], rest[n_out:n_out + n_xout]
        rest = rest[n_out + n_xout:]
        return ins + outs + rest[:n_scratch], (xins, xouts, rest[n_scratch:])

    def first():
        return functools.reduce(lambda a, b: a & b, [pl.program_id(d) == 0 for d in range(len(grid))])

    def last():
        return functools.reduce(lambda a, b: a & b, [pl.program_id(d) == grid[d] - 1 for d in range(len(grid))])

    return split, first, last


def _carried_call(body, ex, grid, in_specs, out_specs, out_shape, scratch, sem, name, operands, vmem=None):
    any_spec = pl.BlockSpec(memory_space=pl.ANY)
    split, first, last = _carry(ex, len(in_specs), len(out_specs), len(scratch), grid)

    def carried(*refs):
        own, xrefs = split(refs)
        if ex:
            @pl.when(first())
            def _():
                ex.start(*xrefs)

        body(*own)
        if ex:
            @pl.when(last())
            def _():
                ex.drain(*xrefs)

    n_xin = len(ex.inputs) if ex else 0
    results = pl.pallas_call(
        carried, grid=grid, in_specs=list(in_specs) + [any_spec] * n_xin,
        out_specs=list(out_specs) + [any_spec] * (len(ex.out_shapes) if ex else 0),
        out_shape=list(out_shape) + (list(ex.out_shapes) if ex else []),
        input_output_aliases={len(in_specs) + i: len(out_specs) + o for i, o in ex.aliases.items()} if ex else {},
        scratch_shapes=list(scratch) + (ex.scratch if ex else []),
        compiler_params=_cparams(sem, vmem), name=name)(*operands, *(ex.inputs if ex else []))
    return results[:len(out_specs)], results[len(out_specs):]


def _tri_rows(t, n):
    qi = sum(jnp.where(t >= r * (r + 1) // 2, 1, 0) for r in range(1, n))
    return qi, t - qi * (qi + 1) // 2


def _tri_cols(t, n):
    ki = sum(jnp.where(t >= r * n - r * (r - 1) // 2, 1, 0) for r in range(1, n))
    return ki, ki + t - (ki * n - ki * (ki - 1) // 2)


def _causal_t(st_blk, tk, tq):
    key = lax.broadcasted_iota(jnp.int32, (tk, tq), 0)
    qry = lax.broadcasted_iota(jnp.int32, (tk, tq), 1)
    return jnp.where(qry >= key, st_blk, -jnp.inf)


def attn_forward(qa, ka, vt, name, ex=None):
    hh, s, _ = qa.shape
    tq = tk = min(512, s)
    nq = s // tq
    grp = HEAD_GROUP

    def body(q_ref, k_ref, vt_ref, o_ref, lse_ref, m_ref, acc_ref):
        qi, ki = _tri_rows(pl.program_id(1), nq)

        @pl.when(ki == 0)
        def _():
            m_ref[...] = jnp.full_like(m_ref, -jnp.inf)
            acc_ref[...] = jnp.zeros_like(acc_ref)

        def step(masked):
            nxt = _dot_nt(k_ref[0], q_ref[0])
            for g in range(grp):
                st = nxt
                if g + 1 < grp:
                    nxt = _dot_nt(k_ref[g + 1], q_ref[g + 1])
                if masked:
                    st = _causal_t(st, tk, tq)
                m_old = m_ref[g]
                m_new = jnp.maximum(m_old, jnp.max(st, axis=0, keepdims=True))
                pt = jnp.exp(st - m_new).astype(BF16)
                acc_ref[g] = jnp.exp(m_old - m_new) * acc_ref[g] + _dot(vt_ref[g], pt)
                m_ref[g] = m_new

        @pl.when(ki < qi)
        def _():
            step(False)

        @pl.when(ki == qi)
        def _():
            step(True)
            for g in range(grp):
                acc = acc_ref[g]
                denom = acc[LANE_C:LANE_C + 1, :]
                o_ref[g] = (acc / denom).T.astype(BF16)
                lse_ref[g] = m_ref[g] + jnp.log(denom)

    qspec = pl.BlockSpec((grp, tq, 128), lambda h, t: (h, _tri_rows(t, nq)[0], 0))
    kspec = pl.BlockSpec((grp, tk, 128), lambda h, t: (h, _tri_rows(t, nq)[1], 0))
    vspec = pl.BlockSpec((grp, 128, tk), lambda h, t: (h, 0, _tri_rows(t, nq)[1]))
    lspec = pl.BlockSpec((grp, 1, tq), lambda h, t: (h, 0, _tri_rows(t, nq)[0]))
    return _carried_call(
        body, ex, (hh // grp, nq * (nq + 1) // 2), [qspec, kspec, vspec], [qspec, lspec],
        [jax.ShapeDtypeStruct((hh, s, 128), BF16), jax.ShapeDtypeStruct((hh, 1, s), F32)],
        [pltpu.VMEM((grp, 1, tq), F32), pltpu.VMEM((grp, 128, tq), F32)],
        ("arbitrary", "arbitrary"), name, (qa, ka, vt))


def attn_backward(qa, ka, va, oa, doa, lse, name, ex=None):
    hh, s, _ = qa.shape
    tq = tk = min(512, s)
    nq = s // tq
    grp = HEAD_GROUP

    def body(q_ref, k_ref, v_ref, o_ref, do_ref, lse_ref, dq_ref, dk_ref, dv_ref, dka_ref, dva_ref):
        ki, qi = _tri_cols(pl.program_id(1), nq)

        @pl.when(pl.program_id(1) == 0)
        def _():
            dq_ref[...] = jnp.zeros_like(dq_ref)

        @pl.when(qi == ki)
        def _():
            dka_ref[...] = jnp.zeros_like(dka_ref)
            dva_ref[...] = jnp.zeros_like(dva_ref)

        def step(masked):
            rows = pl.ds(pl.multiple_of(qi * tq, tq), tq)
            products = lambda g: (_dot_nt(k_ref[g], q_ref[g]), _dot_nt(v_ref[g], do_ref[g]))
            nxt = products(0)
            for g in range(grp):
                st, dpt = nxt
                if g + 1 < grp:
                    nxt = products(g + 1)
                q, k, do = q_ref[g], k_ref[g], do_ref[g]
                if masked:
                    st = _causal_t(st, tk, tq)
                pt = jnp.exp(st - lse_ref[g])
                delta = jnp.sum((do.astype(F32) * o_ref[g].astype(F32)).T, axis=0, keepdims=True)
                dst = (pt * (dpt - delta)).astype(BF16)
                dva_ref[g] += _dot(pt.astype(BF16), do)
                dka_ref[g] += _dot(dst, q)
                dq_ref[g, rows, :] += _dot_tn(dst, k)

        @pl.when(qi > ki)
        def _():
            step(False)

        @pl.when(qi == ki)
        def _():
            step(True)

        @pl.when(qi == nq - 1)
        def _():
            dk_ref[...] = dka_ref[...]
            dv_ref[...] = dva_ref[...].astype(BF16)

    qspec = pl.BlockSpec((grp, tq, 128), lambda h, t: (h, _tri_cols(t, nq)[1], 0))
    lspec = pl.BlockSpec((grp, 1, tq), lambda h, t: (h, 0, _tri_cols(t, nq)[1]))
    kspec = pl.BlockSpec((grp, tk, 128), lambda h, t: (h, _tri_cols(t, nq)[0], 0))
    return _carried_call(
        body, ex, (hh // grp, nq * (nq + 1) // 2), [qspec, kspec, kspec, qspec, qspec, lspec],
        [pl.BlockSpec((grp, s, 128), lambda h, t: (h, 0, 0)), kspec, kspec],
        [jax.ShapeDtypeStruct((hh, s, 128), F32), jax.ShapeDtypeStruct((hh, s, 128), F32),
         jax.ShapeDtypeStruct((hh, s, 128), BF16)],
        [pltpu.VMEM((grp, tk, 128), F32), pltpu.VMEM((grp, tk, 128), F32)],
        ("arbitrary", "arbitrary"), name, (qa, ka, va, oa, doa, lse))


def attn_post(dqa, dka, dva, proj, gq2, gk2, dproj, name):
    s = proj.shape[0]
    tm = min(256, s)

    def body(dq_ref, dk_ref, dv_ref, q_ref, k_ref, gq_ref, gk_ref, dp_any, dp_ref, dc_ref, dgq_ref, dgk_ref):
        lane, lo = _lanes()

        @pl.when(pl.program_id(0) == 0)
        def _():
            dgq_ref[...] = jnp.zeros_like(dgq_ref)
            dgk_ref[...] = jnp.zeros_like(dgk_ref)

        def pair(ref, j):
            return jnp.where(lo, ref[2 * j].astype(F32), _swap_halves(ref[2 * j + 1].astype(F32)))

        def norm_bwd(raw, g, dhat, scale):
            r = lax.rsqrt(_half_mean(raw * raw, lo) + EPS)
            y = raw * r
            dy = dhat * (g * scale)
            return r * (dy - y * _half_mean(dy * y, lo)), jnp.sum(dhat * y, axis=0, keepdims=True) * scale

        dc = jnp.zeros((tm, 128), F32)
        for j in range(HEADS // 2):
            cols = slice(128 * j, 128 * (j + 1))
            dq, dgq = norm_bwd(q_ref[:, cols].astype(F32), gq_ref[...], pair(dq_ref, j), ATTN_SCALE)
            dk, dgk = norm_bwd(k_ref[:, cols].astype(F32), gk_ref[...], pair(dk_ref, j), 1.0)
            dgq_ref[...] += dgq
            dgk_ref[...] += dgk
            dp_ref[:, cols] = dq.astype(BF16)
            dp_ref[:, D_ATTN + 128 * j:D_ATTN + 128 * (j + 1)] = dk.astype(BF16)
            dp_ref[:, 2 * D_ATTN + 128 * j:2 * D_ATTN + 128 * (j + 1)] = pair(dv_ref, j).astype(BF16)
            for e in range(2):
                h = 2 * j + e
                col = _lane_col(dq_ref[h], lane, LANE_C) - _lane_col(dk_ref[h], lane, LANE_ONE)
                dc = jnp.where(lane == h, col, dc)
        dp_ref[:, 3 * D_ATTN:] = jnp.zeros((tm, DPROJ_TAIL - 3 * D_ATTN), BF16)
        dc_ref[...] = dc

    heads = lambda: pl.BlockSpec((HEADS, tm, 128), lambda i: (0, i, 0))
    vec = pl.BlockSpec((1, 128), lambda i: (0, 0))
    first = N_REST // D_ATTN
    return pl.pallas_call(
        body, grid=(s // tm,),
        in_specs=[heads(), heads(), heads(), pl.BlockSpec((tm, D_ATTN), lambda i: (i, first)),
                  pl.BlockSpec((tm, D_ATTN), lambda i: (i, first + 1)), vec, vec, pl.BlockSpec(memory_space=pl.ANY)],
        out_specs=[pl.BlockSpec((tm, DPROJ_TAIL), lambda i: (i, N_REST // DPROJ_TAIL)),
                   pl.BlockSpec((tm, 128), lambda i: (i, 0)), vec, vec],
        out_shape=[jax.ShapeDtypeStruct(dproj.shape, BF16), jax.ShapeDtypeStruct((s, 128), F32),
                   jax.ShapeDtypeStruct((1, 128), F32), jax.ShapeDtypeStruct((1, 128), F32)],
        input_output_aliases={7: 0},
        compiler_params=_cparams(("arbitrary",)), name=name)(dqa, dka, dva, proj, proj, gq2, gk2, dproj)


def _pool_groups(tm):
    gid = lax.broadcasted_iota(jnp.int32, (1, D_POOL), 1) // (D_POOL // 4)
    win = jnp.where(gid == 0, 2.0, jnp.where(gid == 1, 4.0, jnp.where(gid == 2, 8.0, 16.0)))
    return gid, win


def _by_group(gid, v2, v4, v8, v16):
    return jnp.where(gid == 0, v2, jnp.where(gid == 1, v4, jnp.where(gid == 2, v8, v16)))


def _branches(rest_ref, halo_ref, a_ref, wa_ref, wc_ref, wp_ref, sc_ref, cw_ref, ti, tm):
    f = lambda v: v.astype(F32)
    cx, cb, cc, px = f(rest_ref[:, 0:256]), f(rest_ref[:, 256:512]), f(rest_ref[:, 512:768]), f(rest_ref[:, 768:1024])
    live = jnp.where(ti > 0, 1.0, 0.0)
    hz = f(halo_ref[:, 0:256]) * f(halo_ref[:, 512:768]) * live
    hp = f(halo_ref[:, 768:1024]) * live
    z = cc * cx
    zf = jnp.concatenate([hz, z], axis=0)
    z1 = pltpu.roll(zf, 1, 0)[HALO:]
    z2 = pltpu.roll(zf, 2, 0)[HALO:]
    cw = cw_ref[...]
    conv = cw[2:3] * z + cw[1:2] * z1 + cw[0:1] * z2
    uc = cb * conv
    pf = jnp.concatenate([hp, px], axis=0)
    s2 = pf + pltpu.roll(pf, 1, 0)
    s4 = s2 + pltpu.roll(s2, 2, 0)
    s8 = s4 + pltpu.roll(s4, 4, 0)
    s16 = s8 + pltpu.roll(s8, 8, 0)
    gid, win = _pool_groups(tm)
    t = (ti * tm + lax.broadcasted_iota(jnp.int32, (tm, 1), 0)).astype(F32)
    inv = 1.0 / jnp.minimum(t + 1.0, win)
    dpool = _by_group(gid, s2[HALO:], s4[HALO:], s8[HALO:], s16[HALO:]) * inv - px
    _, lo = _lanes()
    a_tok = [jnp.where(lo, f(a_ref[2 * j]), _swap_halves(f(a_ref[2 * j + 1]))).astype(BF16) for j in range(HEADS // 2)]
    y_attn = _dot(a_tok[0], wa_ref[0:128, :])
    for j in range(1, HEADS // 2):
        y_attn += _dot(a_tok[j], wa_ref[128 * j:128 * (j + 1), :])
    y_conv = _dot(uc.astype(BF16), wc_ref[...])
    y_pool_raw = _dot(dpool.astype(BF16), wp_ref[...])
    sg = [_sigmoid(f(rest_ref[:, 1024 + i * D_MODEL:1024 + (i + 1) * D_MODEL])) for i in range(3)]
    return dict(cx=cx, cb=cb, cc=cc, z=z, z1=z1, z2=z2, conv=conv, uc=uc, dpool=dpool, inv=inv, gid=gid, a_tok=a_tok,
                y_attn=y_attn, y_conv=y_conv, y_pool_raw=y_pool_raw, sg=sg, cw=cw)


def _mix_specs(tm, ti_of):
    blocks_per_tile = tm // HALO
    return [
        pl.BlockSpec((tm, N_REST), lambda i: (ti_of(i), 0)),
        pl.BlockSpec((HALO, 1024), lambda i: (jnp.maximum(ti_of(i) * blocks_per_tile - 1, 0), 0)),
        pl.BlockSpec((HEADS, tm, 128), lambda i: (0, ti_of(i), 0)),
        pl.BlockSpec((D_ATTN, D_MODEL), lambda i: (0, 0)),
        pl.BlockSpec((D_CONV, D_MODEL), lambda i: (0, 0)),
        pl.BlockSpec((D_POOL, D_MODEL), lambda i: (0, 0)),
        pl.BlockSpec((1, D_MODEL), lambda i: (0, 0)),
        pl.BlockSpec((8, D_CONV), lambda i: (0, 0)),
    ]


def mix_fwd(proj, a, x, wa, wc, wp, scale, cw, wo, name):
    s = x.shape[0]
    tm = min(256, s)

    def body(rest_ref, halo_ref, a_ref, wa_ref, wc_ref, wp_ref, sc_ref, cw_ref, wo_ref, x_ref, o_ref):
        b = _branches(rest_ref, halo_ref, a_ref, wa_ref, wc_ref, wp_ref, sc_ref, cw_ref, pl.program_id(0), tm)
        merged = b["sg"][0] * b["y_attn"] + b["sg"][1] * b["y_conv"] + b["sg"][2] * (b["y_pool_raw"] * sc_ref[...])
        o_ref[...] = x_ref[...] + _dot(merged.astype(BF16), wo_ref[...])

    return pl.pallas_call(
        body, grid=(s // tm,),
        in_specs=_mix_specs(tm, lambda i: i) + [pl.BlockSpec((D_MODEL, D_MODEL), lambda i: (0, 0)),
                                                 pl.BlockSpec((tm, D_MODEL), lambda i: (i, 0))],
        out_specs=pl.BlockSpec((tm, D_MODEL), lambda i: (i, 0)),
        out_shape=jax.ShapeDtypeStruct((s, D_MODEL), F32),
        compiler_params=_cparams(("parallel",)), name=name)(proj, proj, a, wa, wc, wp, scale, cw, wo, x)


def mix_bwd(proj, a, dx1, wa, wc, wp, scale, cw, wo, name):
    s = dx1.shape[0]
    tm = min(256, s)
    nt = s // tm
    ti_of = lambda i: nt - 1 - i
    n = tm + HALO

    def body(rest_ref, halo_ref, a_ref, wa_ref, wc_ref, wp_ref, sc_ref, cw_ref, wo_ref,
             dx_ref, dp_ref, da_ref, at_ref, mg_ref, dya_ref, dyc_ref, dyp_ref, uc_ref, dd_ref, dsc_ref, dcw_ref,
             cdc_ref, cde_ref):
        i = pl.program_id(0)
        ti = ti_of(i)

        @pl.when(i == 0)
        def _():
            cdc_ref[...] = jnp.zeros_like(cdc_ref)
            cde_ref[...] = jnp.zeros_like(cde_ref)
            dsc_ref[...] = jnp.zeros_like(dsc_ref)
            dcw_ref[...] = jnp.zeros_like(dcw_ref)

        b = _branches(rest_ref, halo_ref, a_ref, wa_ref, wc_ref, wp_ref, sc_ref, cw_ref, ti, tm)
        sg, sc = b["sg"], sc_ref[...]
        y_pool = b["y_pool_raw"] * sc
        merged = sg[0] * b["y_attn"] + sg[1] * b["y_conv"] + sg[2] * y_pool
        mg_ref[...] = merged.astype(BF16)
        dm = _dot_nt(dx_ref[...].astype(BF16), wo_ref[...])
        for j, y in enumerate((b["y_attn"], b["y_conv"], y_pool)):
            dp_ref[:, 1024 + j * D_MODEL:1024 + (j + 1) * D_MODEL] = (dm * y * sg[j] * (1.0 - sg[j])).astype(BF16)
        dya = (dm * sg[0]).astype(BF16)
        dya_ref[...] = dya
        _, lo = _lanes()
        for j in range(HEADS // 2):
            at_ref[:, 128 * j:128 * (j + 1)] = b["a_tok"][j]
            da = _dot_nt(dya, wa_ref[128 * j:128 * (j + 1), :])
            da_ref[2 * j] = jnp.where(lo, da, 0.0).astype(BF16)
            da_ref[2 * j + 1] = jnp.where(lo, _swap_halves(da), 0.0).astype(BF16)
        dyc = (dm * sg[1]).astype(BF16)
        dyc_ref[...] = dyc
        duc = _dot_nt(dyc, wc_ref[...])
        dyp = dm * sg[2]
        dsc_ref[...] += jnp.sum(dyp * b["y_pool_raw"], axis=0, keepdims=True)
        dypr = (dyp * sc).astype(BF16)
        dyp_ref[...] = dypr
        ddp = _dot_nt(dypr, wp_ref[...])
        uc_ref[...] = b["uc"].astype(BF16)
        dd_ref[...] = b["dpool"].astype(BF16)

        dconv = duc * b["cb"]
        dp_ref[:, 256:512] = (duc * b["conv"]).astype(BF16)
        dcf = jnp.concatenate([dconv, cdc_ref[...]], axis=0)
        cw = b["cw"]
        dz = cw[2:3] * dconv + cw[1:2] * pltpu.roll(dcf, n - 1, 0)[:tm] + cw[0:1] * pltpu.roll(dcf, n - 2, 0)[:tm]
        dp_ref[:, 0:256] = (dz * b["cc"]).astype(BF16)
        dp_ref[:, 512:768] = (dz * b["cx"]).astype(BF16)
        dcw_ref[0:1, :] += jnp.sum(dconv * b["z2"], axis=0, keepdims=True)
        dcw_ref[1:2, :] += jnp.sum(dconv * b["z1"], axis=0, keepdims=True)
        dcw_ref[2:3, :] += jnp.sum(dconv * b["z"], axis=0, keepdims=True)
        cdc_ref[...] = dconv[:HALO]

        e = ddp * b["inv"]
        ef = jnp.concatenate([e, cde_ref[...]], axis=0)
        r2 = ef + pltpu.roll(ef, n - 1, 0)
        r4 = r2 + pltpu.roll(r2, n - 2, 0)
        r8 = r4 + pltpu.roll(r4, n - 4, 0)
        r16 = r8 + pltpu.roll(r8, n - 8, 0)
        dp_ref[:, 768:1024] = (_by_group(b["gid"], r2[:tm], r4[:tm], r8[:tm], r16[:tm]) - ddp).astype(BF16)
        cde_ref[...] = e[:HALO]

    tile = lambda w: pl.BlockSpec((tm, w), lambda i: (ti_of(i), 0))
    whole = lambda r, c: pl.BlockSpec((r, c), lambda i: (0, 0))
    bf = lambda w: jax.ShapeDtypeStruct((s, w), BF16)
    return pl.pallas_call(
        body, grid=(nt,),
        in_specs=_mix_specs(tm, ti_of) + [whole(D_MODEL, D_MODEL), tile(D_MODEL)],
        out_specs=[tile(N_REST), pl.BlockSpec((HEADS, tm, 128), lambda i: (0, ti_of(i), 0)), tile(D_ATTN),
                   tile(D_MODEL), tile(D_MODEL), tile(D_MODEL), tile(D_MODEL),
                   tile(D_CONV), tile(D_POOL), whole(1, D_MODEL), whole(8, D_CONV)],
        out_shape=[bf(DPROJ_COLS), jax.ShapeDtypeStruct((HEADS, s, 128), BF16), bf(D_ATTN),
                   bf(D_MODEL), bf(D_MODEL), bf(D_MODEL), bf(D_MODEL), bf(D_CONV), bf(D_POOL),
                   jax.ShapeDtypeStruct((1, D_MODEL), F32), jax.ShapeDtypeStruct((8, D_CONV), F32)],
        scratch_shapes=[pltpu.VMEM((HALO, D_CONV), F32), pltpu.VMEM((HALO, D_POOL), F32)],
        compiler_params=_cparams(("arbitrary",)), name=name)(proj, proj, a, wa, wc, wp, scale, cw, wo, dx1)


def _adamw_math(w, g, m, v):
    m = ADAM_B1 * m + (1.0 - ADAM_B1) * g
    v = ADAM_B2 * v + (1.0 - ADAM_B2) * (g * g)
    m_hat = m / (1.0 - ADAM_B1 ** ADAM_STEP)
    v_hat = v / (1.0 - ADAM_B2 ** ADAM_STEP)
    delta = -ADAM_LR * (m_hat / (jnp.sqrt(v_hat) + ADAM_EPS) + ADAM_WD * w)
    return delta, m, v


ADAMW_PARTS_BLOCK_BYTES = 4 * 2 ** 20


def _row_tile(rows, cols, copies, itemsize):
    row_bytes = copies * (-(-cols // 128) * 128) * itemsize
    fits = [t for t in range(16, rows + 1, 16) if rows % t == 0 and t * row_bytes <= ADAMW_PARTS_BLOCK_BYTES]
    return max(fits) if fits else rows


def pair_sum(blocks, stage, me, name):
    n_slots, rows, cols = stage.shape
    tr = _row_tile(rows, cols, 1, 4)

    def body(me_ref, a_ref, b_ref, o_ref):
        o_ref[...] = (a_ref[...].astype(F32) + b_ref[...].astype(F32)).astype(BF16)

    slot = pl.BlockSpec((None, tr, cols), lambda i, r, me_ref: (i, r, 0))
    return pl.pallas_call(
        body, out_shape=jax.ShapeDtypeStruct(stage.shape, BF16),
        grid_spec=pltpu.PrefetchScalarGridSpec(
            num_scalar_prefetch=1, grid=(n_slots, rows // tr),
            in_specs=[pl.BlockSpec((None, tr, cols), lambda i, r, me_ref: (me_ref[0] ^ (2 * i), r, 0)), slot],
            out_specs=slot),
        compiler_params=_cparams(("parallel", "parallel")), name=name)(me.reshape(1), blocks, stage)


def adamw_sum(parts, w, m, v, name):
    layers, rows, cols = w.shape
    n_parts = parts.shape[1]
    if rows % 16 == 0:
        tr, tc = _row_tile(rows, cols, n_parts, parts.dtype.itemsize), cols
    else:
        tr, tc = rows, _pick(cols, (256, 128))

    def body(p_ref, w_ref, m_ref, v_ref, g_ref, d_ref, nm_ref, nv_ref):
        g = p_ref[0].astype(F32)
        for i in range(1, n_parts):
            g = g + p_ref[i].astype(F32)
        g_ref[...] = g
        d_ref[...], nm_ref[...], nv_ref[...] = _adamw_math(w_ref[...], g, m_ref[...], v_ref[...])

    spec = pl.BlockSpec((None, tr, tc), lambda l, i, j: (l, i, j))
    return pl.pallas_call(
        body, grid=(layers, rows // tr, cols // tc),
        in_specs=[pl.BlockSpec((None, n_parts, tr, tc), lambda l, i, j: (l, 0, i, j)), spec, spec, spec],
        out_specs=[spec] * 4, out_shape=[jax.ShapeDtypeStruct((layers, rows, cols), F32)] * 4,
        compiler_params=_cparams(("parallel", "parallel", "parallel")), name=name)(parts, w, m, v)


def _me():
    return lax.axis_index("x"), lax.axis_index("y"), lax.axis_index("c")


N_PEERS = N_DEV - 1


def all_gather(shards, name):
    n = len(shards)
    any_spec = pl.BlockSpec(memory_space=pl.ANY)

    def body(*refs):
        x_refs, out_refs = refs[:n], refs[n:2 * n]
        send_sems, recv_sems, local_sems = refs[2 * n:]
        x, y, c = _me()
        me, sibling = (x, y, c), (x, y, 1 - c)
        chips = [(1 - x, y), (x, 1 - y), (1 - x, 1 - y)]

        def copy(t, k, block, to, from_input=False):
            slot = out_refs[t].at[4 * block[0] + 2 * block[1] + block[2]]
            return pltpu.make_async_remote_copy(
                src_ref=x_refs[t] if from_input else slot, dst_ref=slot, send_sem=send_sems.at[N_PEERS * t + k],
                recv_sem=recv_sems.at[N_PEERS * t + k], device_id=to, device_id_type=pl.DeviceIdType.MESH)

        mine = [pltpu.make_async_copy(x_refs[t], out_refs[t].at[4 * x + 2 * y + c], local_sems.at[t]) for t in range(n)]
        started = []
        for t in range(n):
            mine[t].start()
            started.append(copy(t, 0, me, sibling, from_input=True))
            started += [copy(t, 1 + j, me, (*chip, c), from_input=True) for j, chip in enumerate(chips)]
        for cp in started:
            cp.start()
        for j, chip in enumerate(chips):
            for t in range(n):
                copy(t, 1 + j, (*chip, c), me).wait_recv()
                fwd = copy(t, 4 + j, (*chip, c), sibling)
                fwd.start()
                started.append(fwd)
        for t in range(n):
            copy(t, 0, sibling, me).wait_recv()
            for j, chip in enumerate(chips):
                copy(t, 4 + j, (*chip, 1 - c), me).wait_recv()
        for cp in started:
            cp.wait_send()
        for cp in mine:
            cp.wait()

    return pl.pallas_call(
        body, out_shape=[jax.ShapeDtypeStruct((N_DEV,) + s.shape, s.dtype) for s in shards],
        in_specs=[any_spec] * n, out_specs=[any_spec] * n,
        scratch_shapes=[pltpu.SemaphoreType.DMA((N_PEERS * n,)), pltpu.SemaphoreType.DMA((N_PEERS * n,)),
                        pltpu.SemaphoreType.DMA((n,))],
        name=name)(*shards)


SIBLING = 1
OTHER_CHIPS = (2, 4, 6)
SAME_CORE = (0,) + OTHER_CHIPS


class Exchange:
    def __init__(self, inputs, out_shapes, aliases, copies, local=()):
        self.inputs, self.out_shapes, self.aliases = list(inputs), list(out_shapes), aliases
        self._copies, self._local = list(copies), list(local)
        self.scratch = [pltpu.SemaphoreType.DMA((len(self._copies),)), pltpu.SemaphoreType.DMA((len(self._copies),)),
                        pltpu.SemaphoreType.DMA((max(len(self._local), 1),))]

    def _build(self, ins, outs, sems):
        send_sems, recv_sems, local_sems = sems
        x, y, c = _me()
        me = 4 * x + 2 * y + c
        local = [functools.partial(pltpu.make_async_copy, src(ins, outs, me), dst(outs, me), local_sems.at[i])
                 for i, (src, dst) in enumerate(self._local)]
        sends, recvs = [], []
        for i, (mask, src, dst) in enumerate(self._copies):
            px, py, pc = x ^ ((mask >> 2) & 1), y ^ ((mask >> 1) & 1), c ^ (mask & 1)
            pair = dict(send_sem=send_sems.at[i], recv_sem=recv_sems.at[i], device_id_type=pl.DeviceIdType.MESH)
            sends.append(functools.partial(
                pltpu.make_async_remote_copy, src_ref=src(ins, outs, me), dst_ref=dst(outs, me), device_id=(px, py, pc), **pair))
            recvs.append(functools.partial(
                pltpu.make_async_remote_copy, src_ref=src(ins, outs, me), dst_ref=dst(outs, me ^ mask), device_id=(x, y, c), **pair))
        return local, sends, recvs

    def start(self, ins, outs, sems):
        local, sends, _ = self._build(ins, outs, sems)
        for make in local + sends:
            make().start()

    def drain(self, ins, outs, sems):
        local, sends, recvs = self._build(ins, outs, sems)
        for make in recvs:
            make().wait_recv()
        for make in sends:
            make().wait_send()
        for make in local:
            make().wait()


def _bind(fn, *args):
    return functools.partial(fn, *args)


def join_exchanges(a, b):
    if a is None or b is None:
        return a or b
    na_in, na_out = len(a.inputs), len(a.out_shapes)

    def src_a(fn):
        return lambda ins, outs, me: fn(ins[:na_in], outs[:na_out], me)

    def dst_a(fn):
        return lambda outs, who: fn(outs[:na_out], who)

    def src_b(fn):
        return lambda ins, outs, me: fn(ins[na_in:], outs[na_out:], me)

    def dst_b(fn):
        return lambda outs, who: fn(outs[na_out:], who)

    copies = [(m, src_a(s), dst_a(d)) for m, s, d in a._copies] + [(m, src_b(s), dst_b(d)) for m, s, d in b._copies]
    local = [(src_a(s), dst_a(d)) for s, d in a._local] + [(src_b(s), dst_b(d)) for s, d in b._local]
    aliases = dict(a.aliases)
    aliases.update({na_in + i: na_out + o for i, o in b.aliases.items()})
    return Exchange(a.inputs + b.inputs, a.out_shapes + b.out_shapes, aliases, copies, local)


def gather_over_ici(shards):
    copies = [(mask, _bind(lambda t, ins, outs, me: ins[t], t), _bind(lambda t, outs, sender: outs[t].at[sender], t))
              for t in range(len(shards)) for mask in OTHER_CHIPS]
    local = [(_bind(lambda t, ins, outs, me: ins[t], t), _bind(lambda t, outs, me: outs[t].at[me], t))
             for t in range(len(shards))]
    return Exchange(shards, [jax.ShapeDtypeStruct((N_DEV,) + s.shape, s.dtype) for s in shards], {}, copies, local)


def gather_over_d2d(gathered):
    copies = [(SIBLING, _bind(lambda t, m, ins, outs, me: outs[t].at[me ^ m], t, m),
               _bind(lambda t, m, outs, sender: outs[t].at[sender ^ m], t, m))
              for t in range(len(gathered)) for m in SAME_CORE]
    return Exchange(gathered, [jax.ShapeDtypeStruct(g.shape, g.dtype) for g in gathered],
                    {t: t for t in range(len(gathered))}, copies)


def scatter_over_d2d(blocks):
    copies = [(SIBLING, _bind(lambda t, m, ins, outs, me: ins[t].at[me ^ SIBLING ^ m], t, m),
               _bind(lambda t, i, outs, sender: outs[t].at[i], t, i))
              for t in range(len(blocks)) for i, m in enumerate(SAME_CORE)]
    return Exchange(blocks, [jax.ShapeDtypeStruct((len(SAME_CORE),) + b.shape[1:], b.dtype) for b in blocks], {}, copies)


def scatter_over_ici(pair_sums, bufs, layer):
    n = len(pair_sums)
    copies = [(m, _bind(lambda t, i, ins, outs, me: ins[t].at[i], t, i),
               _bind(lambda t, i, outs, sender: outs[t].at[layer, i], t, i))
              for t in range(n) for i, m in enumerate(SAME_CORE) if m]
    local = [(_bind(lambda t, ins, outs, me: ins[t].at[0], t), _bind(lambda t, outs, me: outs[t].at[layer, 0], t))
             for t in range(n)]
    return Exchange(list(pair_sums) + list(bufs), [jax.ShapeDtypeStruct(b.shape, b.dtype) for b in bufs],
                    {n + t: t for t in range(n)}, copies, local)


def run_exchange(ex, name):
    any_spec = pl.BlockSpec(memory_space=pl.ANY)
    n_in, n_out = len(ex.inputs), len(ex.out_shapes)

    def body(*refs):
        ins, outs, sems = refs[:n_in], refs[n_in:n_in + n_out], refs[n_in + n_out:]
        ex.start(ins, outs, sems)
        ex.drain(ins, outs, sems)

    return pl.pallas_call(
        body, out_shape=ex.out_shapes, in_specs=[any_spec] * n_in, out_specs=[any_spec] * n_out,
        input_output_aliases=ex.aliases, scratch_shapes=ex.scratch, name=name)(*ex.inputs)


MATRICES = ("w_in", "w_attn_out", "w_conv_out", "pool_w", "w_o", "w_ffn_in", "w_ffn_out")
TRANSPOSED = ("w_in", "w_ffn_in")
MIXER_PART, FFN_PART = slice(0, 1), slice(1, 7)
EVERY = tuple(range(len(MATRICES)))
LATE = (0,)
EARLY = EVERY[1:]
SHARD_INFO = {
    "w_in": ((DEPTH, D_IN // N_DEV, D_MODEL), 1),
    "w_attn_out": ((DEPTH, D_ATTN, D_MODEL // N_DEV), 2),
    "w_conv_out": ((DEPTH, D_CONV, D_MODEL // N_DEV), 2),
    "pool_w": ((DEPTH, 4, 64, 256 // N_DEV), 3),
    "w_o": ((DEPTH, D_MODEL // N_DEV, D_MODEL), 1),
    "w_ffn_in": ((DEPTH, 2 * D_FF // N_DEV, D_MODEL), 1),
    "w_ffn_out": ((DEPTH, D_FF // N_DEV, D_MODEL), 1),
}


def _handled(name, t):
    return jnp.transpose(t, (0, 2, 1)) if name in TRANSPOSED else t
VECTORS = ("norm_mix_g", "forget_b", "q_norm_g", "k_norm_g", "pool_scale", "norm_ffn_g")
VECTOR_SHAPES = {"norm_mix_g": (DEPTH, D_MODEL), "forget_b": (DEPTH, HEADS), "q_norm_g": (DEPTH, HEAD_DIM),
                 "k_norm_g": (DEPTH, HEAD_DIM), "pool_scale": (DEPTH, D_MODEL), "norm_ffn_g": (DEPTH, D_MODEL)}
CONV_W_FULL = (DEPTH, 3, D_CONV)


def _size(shape):
    n = 1
    for v in shape:
        n *= v
    return n


def _pack(arrays, rows, cols):
    flat = jnp.concatenate([a.reshape(-1) for a in arrays])
    return jnp.pad(flat, (0, rows * cols - flat.shape[0])).reshape(rows, cols)


def _unpack(packed, shapes):
    flat, out, off = packed.reshape(-1), [], 0
    for shp in shapes:
        out.append(flat[off:off + _size(shp)].reshape(shp))
        off += _size(shp)
    return out


def _join_shards(stacked, axis):
    moved = jnp.moveaxis(stacked, 0, axis)
    shp = list(moved.shape)
    shp[axis:axis + 2] = [shp[axis] * shp[axis + 1]]
    return moved.reshape(shp)


def _cut_shards(full, axis):
    shp = list(full.shape)
    shp[axis:axis + 1] = [N_DEV, shp[axis] // N_DEV]
    return jnp.moveaxis(full.reshape(shp), axis, 0)


N_MOVED = 1544
SHARD_ROWS = D_IN // N_DEV


def _regroup_w_in(shards):
    wt = shards.reshape(D_IN, shards.shape[2])
    pad = jnp.zeros((N_FULL - D_IN, wt.shape[1]), wt.dtype)
    return jnp.concatenate([wt[N_MOVED:], wt[:N_MOVED], pad], axis=0)


def _ungroup_w_in(wpt):
    def kernel_rows(a, b):
        if b <= N_MOVED:
            return [wpt[a + D_IN - N_MOVED:b + D_IN - N_MOVED]]
        if a >= N_MOVED:
            return [wpt[a - N_MOVED:b - N_MOVED]]
        return kernel_rows(a, N_MOVED) + kernel_rows(N_MOVED, b)

    return jnp.stack([jnp.concatenate(kernel_rows(s * SHARD_ROWS, (s + 1) * SHARD_ROWS), axis=0) for s in range(N_DEV)])


def _pool_block_diag(w):
    out = jnp.zeros((D_POOL, D_MODEL), w.dtype)
    for g in range(4):
        out = lax.dynamic_update_slice(out, w[g], (g * 64, g * 256))
    return out


def _pool_from_block_diag(wbd):
    return jnp.stack([wbd[g * 64:(g + 1) * 64, g * 256:(g + 1) * 256] for g in range(4)])


def _layer_weights(mats, vec, conv_w, l):
    wp = _pool_block_diag(mats["pool_w"])
    row = lambda v: v.reshape(1, -1)
    fb = jnp.zeros((1, 128), F32).at[0, :HEADS].set(vec["forget_b"][l])
    cw = jnp.zeros((8, D_CONV), F32).at[:3].set(conv_w[l])
    twice = lambda v: jnp.tile(v.reshape(1, -1), (1, 2))
    return dict(
        wt_in=_regroup_w_in(mats["w_in"]), wt_ffn_in=mats["w_ffn_in"], w_ffn_out=mats["w_ffn_out"],
        wa=mats["w_attn_out"], wc=mats["w_conv_out"], wp=wp, wo=mats["w_o"],
        g_mix=row(vec["norm_mix_g"][l]), g_ffn=row(vec["norm_ffn_g"][l]), gq2=twice(vec["q_norm_g"][l]),
        gk2=twice(vec["k_norm_g"][l]), scale=row(vec["pool_scale"][l]), fb=fb, cw=cw)


def _layer_fwd(x, w, l, comm):
    (proj, h), half_mix = norm_matmul(x, w["g_mix"], w["wt_in"], N_MAIN, f"in_proj_{l}", comm.gather_ici(l + 1, MIXER_PART))
    z, c = forget_fwd(h, w["wt_in"], w["fb"], f"forget_fwd_{l}")
    qa, ka, va, vt = attn_prep(proj, c, w["gq2"], w["gk2"], f"attn_prep_{l}")
    (oa, lse), half_ffn = attn_forward(qa, ka, vt, f"attn_fwd_{l}", comm.gather_ici(l + 1, FFN_PART))
    half = list(half_mix) + list(half_ffn)
    x1 = mix_fwd(proj, oa, x, w["wa"], w["wc"], w["wp"], w["scale"], w["cw"], w["wo"], f"mix_fwd_{l}")
    (gu, h2), gathered = norm_matmul(x1, w["g_ffn"], w["wt_ffn_in"], 2 * D_FF, f"ffn_in_{l}", comm.gather_d2d(l + 1, half))
    x2 = swiglu_matmul(gu, w["w_ffn_out"], x1, f"ffn_out_{l}")
    saved = dict(x=x, proj=proj, h=h, z=z, qa=qa, ka=ka, va=va, oa=oa, lse=lse, x1=x1, gu=gu, h2=h2)
    return x2, saved, gathered


def _layer_bwd(dx2, sv, w, l, comm):
    g = {}
    (dgu, act), stage = swiglu_bwd(dx2, sv["gu"], w["w_ffn_out"], f"ffn_out_bwd_{l}", comm.scatter_d2d(l + 1))
    sums = comm.pair_sums(l + 1, stage)
    g["w_ffn_out"] = tn_matmul(act, dx2, f"dw_ffn_out_{l}")
    g["w_ffn_in"] = tn_matmul(dgu, sv["h2"], f"dw_ffn_in_{l}")
    (dx1, dg), _ = matmul_normbwd(dgu, w["wt_ffn_in"], sv["x1"], w["g_ffn"], dx2, f"ffn_in_bwd_{l}")
    g["norm_ffn_g"] = dg[0]

    (dproj, doa, a_tok, merged, dya, dyc, dyp, uc, dd, dscale, dcw) = mix_bwd(
        sv["proj"], sv["oa"], dx1, w["wa"], w["wc"], w["wp"], w["scale"], w["cw"], w["wo"], f"mix_bwd_{l}")
    g["w_o"] = tn_matmul(merged, dx1, f"dw_o_{l}")
    g["w_attn_out"] = tn_matmul(a_tok, dya, f"dw_attn_out_{l}")
    g["w_conv_out"] = tn_matmul(uc, dyc, f"dw_conv_out_{l}")
    g["pool_w"] = _pool_from_block_diag(tn_matmul(dd, dyp, f"dw_pool_{l}"))
    g["pool_scale"] = dscale[0]
    g["conv_w"] = dcw[:3]

    early = comm.early(l)
    comm.grads(l, g)
    above = comm.scatter_ici(l + 1, sums)
    (dqa, dka, dva), got = attn_backward(sv["qa"], sv["ka"], sv["va"], sv["oa"], doa, sv["lse"], f"attn_bwd_{l}",
                                         join_exchanges(above, comm.scatter_d2d(l, early) if early else None))
    n_above = len(above.out_shapes) if above else 0
    comm.scattered(got[:n_above])
    dproj, dc, dgq, dgk = attn_post(dqa, dka, dva, sv["proj"], w["gq2"], w["gk2"], dproj, f"attn_post_{l}")
    g["q_norm_g"] = dgq[0, :HEAD_DIM] + dgq[0, HEAD_DIM:]
    g["k_norm_g"] = dgk[0, :HEAD_DIM] + dgk[0, HEAD_DIM:]
    dproj, db = forget_bwd(dc, sv["z"], dproj, f"forget_bwd_{l}")
    g["forget_b"] = db[0, :HEADS]

    g["w_in"] = _ungroup_w_in(tn_matmul(dproj, sv["h"], f"dw_in_{l}", m_cols=N_FULL))
    early_ici = comm.scatter_ici(l, comm.pair_sums(l, got[n_above:], early), early) if early else None
    (dx, dg), got = matmul_normbwd(dproj, w["wt_in"], sv["x"], w["g_mix"], dx1, f"in_proj_bwd_{l}", k=N_FULL, ex=early_ici)
    comm.scattered(got, early)
    g["norm_mix_g"] = dg[0]
    comm.grads(l, g)
    return dx


def _local_step(x, tgt, comm):
    ws, saved = [], []
    w = comm.weights(0, None)
    for l in range(DEPTH):
        ws.append(w)
        x, sv, gathered = _layer_fwd(x, w, l, comm)
        saved.append(sv)
        if l + 1 < DEPTH:
            w = comm.weights(l + 1, gathered)
    sq, dx = loss_kernel(x, tgt, "loss")
    for l in reversed(range(DEPTH)):
        dx = _layer_bwd(dx, saved[l], ws[l], l, comm)
    comm.finish()
    return sq[0, 0], dx


def kernel(x, norm_mix_g, w_in, forget_b, q_norm_g, k_norm_g, w_attn_out, conv_w, w_conv_out, pool_w, pool_scale, w_o, norm_ffn_g, w_ffn_in, w_ffn_out, loss_target, m_norm_mix_g, m_w_in, m_forget_b, m_q_norm_g, m_k_norm_g, m_w_attn_out, m_conv_w, m_w_conv_out, m_pool_w, m_pool_scale, m_w_o, m_norm_ffn_g, m_w_ffn_in, m_w_ffn_out, v_norm_mix_g, v_w_in, v_forget_b, v_q_norm_g, v_k_norm_g, v_w_attn_out, v_conv_w, v_w_conv_out, v_pool_w, v_pool_scale, v_w_o, v_norm_ffn_g, v_w_ffn_in, v_w_ffn_out):
    w = dict(norm_mix_g=norm_mix_g, w_in=w_in, forget_b=forget_b, q_norm_g=q_norm_g, k_norm_g=k_norm_g,
             w_attn_out=w_attn_out, conv_w=conv_w, w_conv_out=w_conv_out, pool_w=pool_w, pool_scale=pool_scale,
             w_o=w_o, norm_ffn_g=norm_ffn_g, w_ffn_in=w_ffn_in, w_ffn_out=w_ffn_out)
    m = dict(norm_mix_g=m_norm_mix_g, w_in=m_w_in, forget_b=m_forget_b, q_norm_g=m_q_norm_g, k_norm_g=m_k_norm_g,
             w_attn_out=m_w_attn_out, conv_w=m_conv_w, w_conv_out=m_w_conv_out, pool_w=m_pool_w,
             pool_scale=m_pool_scale, w_o=m_w_o, norm_ffn_g=m_norm_ffn_g, w_ffn_in=m_w_ffn_in, w_ffn_out=m_w_ffn_out)
    v = dict(norm_mix_g=v_norm_mix_g, w_in=v_w_in, forget_b=v_forget_b, q_norm_g=v_q_norm_g, k_norm_g=v_k_norm_g,
             w_attn_out=v_w_attn_out, conv_w=v_conv_w, w_conv_out=v_w_conv_out, pool_w=v_pool_w,
             pool_scale=v_pool_scale, w_o=v_w_o, norm_ffn_g=v_norm_ffn_g, w_ffn_in=v_w_ffn_in, w_ffn_out=v_w_ffn_out)
    me = 4 * lax.axis_index("x") + 2 * lax.axis_index("y") + lax.axis_index("c")
    layer_shard = {n: SHARD_INFO[n][0][1:] for n in MATRICES}
    cut_axis = {n: SHARD_INFO[n][1] - 1 for n in MATRICES}

    conv_g = all_gather([_pack([conv_w], 8, 128)], "gather_conv_w")[0]
    conv_full = _join_shards(jnp.stack([_unpack(conv_g[i], [conv_w.shape])[0] for i in range(N_DEV)]), 2)
    vec = {n: w[n] for n in VECTORS}

    rc = {n: (_size(layer_shard[n][:-1]), layer_shard[n][-1]) for n in MATRICES}

    class Comm:
        bufs = [lax.empty((DEPTH, len(SAME_CORE)) + layer_shard[n], BF16) for n in MATRICES]
        blocks = [None] * DEPTH
        small_g = [None] * DEPTH

        @staticmethod
        def shards(l):
            return [_handled(n, w[n])[l].astype(BF16) for n in MATRICES]

        @staticmethod
        def gather_ici(l, part):
            return gather_over_ici(Comm.shards(l)[part]) if l < DEPTH else None

        @staticmethod
        def gather_d2d(l, half):
            return gather_over_d2d(half) if l < DEPTH else None

        @staticmethod
        def weights(l, gathered):
            if l == 0:
                gathered = all_gather(Comm.shards(0), "gather_0")
            mats = {n: t if n == "w_in" else _join_shards(t, cut_axis[n]) for n, t in zip(MATRICES, gathered)}
            return _layer_weights(mats, vec, conv_full, l)

        @staticmethod
        def grads(l, g):
            Comm.small_g[l] = g
            Comm.blocks[l] = [None if n not in g else g[n] if n == "w_in" else _cut_shards(g[n], cut_axis[n])
                              for n in MATRICES]

        @staticmethod
        def early(l):
            return EARLY if l == 0 else None

        @staticmethod
        def scatter_d2d(l, which=EVERY):
            return scatter_over_d2d([Comm.blocks[l][t] for t in which]) if l < DEPTH else None

        @staticmethod
        def pair_sums(l, stage, which=EVERY):
            if l >= DEPTH:
                return None
            return [pair_sum(Comm.blocks[l][t].reshape((N_DEV,) + rc[MATRICES[t]]),
                             s.reshape((len(SAME_CORE),) + rc[MATRICES[t]]), me,
                             f"pair_sum_{MATRICES[t]}_{l}").reshape(s.shape) for t, s in zip(which, stage)]

        @staticmethod
        def scatter_ici(l, sums, which=EVERY):
            return scatter_over_ici(sums, [Comm.bufs[t] for t in which], l) if l < DEPTH else None

        @staticmethod
        def scattered(results, which=EVERY):
            for t, r in zip(which or (), results):
                Comm.bufs[t] = r

        @staticmethod
        def finish():
            stage = run_exchange(Comm.scatter_d2d(0, LATE), "scatter_d2d_0")
            Comm.scattered(run_exchange(Comm.scatter_ici(0, Comm.pair_sums(0, stage, LATE), LATE), "scatter_ici_0"), LATE)

    small_g, received = Comm.small_g, Comm
    sq, dx = _local_step(x[0], loss_target[0], Comm)
    loss = lax.psum(0.5 * sq / D_MODEL, ("x", "y", "c"))

    big = {}
    for n, parts in zip(MATRICES, received.bufs):
        outs = adamw_sum(parts.reshape((DEPTH, len(SAME_CORE)) + rc[n]),
                         *[_handled(n, d[n]).reshape((DEPTH,) + rc[n]) for d in (w, m, v)], f"adamw_{n}")
        big[n] = [_handled(n, t.reshape((DEPTH,) + layer_shard[n])) for t in outs]

    small_shapes = [VECTOR_SHAPES[n] for n in VECTORS] + [CONV_W_FULL]
    stacked = [jnp.stack([small_g[l][n] for l in range(DEPTH)]) for n in VECTORS + ("conv_w",)]
    sparts = all_gather([_pack(stacked, SMALL_ROWS, 128)], "gather_vector_grads")[0]
    col0 = me * (D_CONV // N_DEV)
    place = lambda t: lax.dynamic_update_slice(jnp.zeros(CONV_W_FULL, F32), t, (0, 0, col0))
    spacked = [_pack([d[n] for n in VECTORS] + [place(d["conv_w"])], SMALL_ROWS, 128)[None] for d in (w, m, v)]
    small = [_unpack(t[0], small_shapes) for t in adamw_sum(sparts[None], *spacked, "adamw_vectors")]

    def result(kind):
        out = {n: big[n][kind] for n in MATRICES}
        out.update({n: small[kind][j] for j, n in enumerate(VECTORS)})
        out["conv_w"] = lax.dynamic_slice(small[kind][len(VECTORS)], (0, 0, col0), conv_w.shape)
        return [out[n] for n in w]

    return (loss, dx[None], *result(0), *result(1), *result(2), *result(3))
```

```python
import functools

import jax
import jax.numpy as jnp
from jax import lax
from jax.experimental import pallas as pl
from jax.experimental.pallas import tpu as pltpu

F32 = jnp.float32
BF16 = jnp.bfloat16

N_DEV = 8
DEPTH = 4
D_MODEL = 1024
HEAD_DIM = 64
HEADS = 8
D_ATTN = 512
D_CONV = 256
D_POOL = 256
D_FF = 2816
D_IN = 5640
EPS = 1e-6
ATTN_SCALE = HEAD_DIM ** -0.5

N_REST = 4096
N_MAIN = 5632
N_FULL = 5760
DPROJ_TAIL = 2048
DPROJ_COLS = N_REST + DPROJ_TAIL
FF_BLK = 256
N_FF_BLKS = D_FF // FF_BLK
HALO = 16

ADAM_LR = 0.001
ADAM_B1 = 0.9
ADAM_B2 = 0.999
ADAM_EPS = 1e-08
ADAM_WD = 0.01
ADAM_STEP = 10

PACK_COLS = 1024
PACK_ROWS = 8192
SMALL_ROWS = 128

VMEM_LIMIT = 48 * 2 ** 20


def _cparams(sem, vmem=None):
    return pltpu.CompilerParams(dimension_semantics=sem, vmem_limit_bytes=vmem or VMEM_LIMIT)


def _pick(n, cands):
    for c in cands:
        if n % c == 0:
            return c
    raise ValueError(f"no tile for {n}")


def _sigmoid(v):
    return 1.0 / (1.0 + jnp.exp(-v))


def _rstd(v):
    return lax.rsqrt(jnp.mean(v * v, axis=-1, keepdims=True) + EPS)


def _dot(a, b):
    return jnp.dot(a, b, preferred_element_type=F32)


def _dot_tn(a, b):
    return lax.dot_general(a, b, (((0,), (0,)), ((), ())), preferred_element_type=F32)


def _dot_nt(a, b):
    return lax.dot_general(a, b, (((1,), (1,)), ((), ())), preferred_element_type=F32)


def norm_matmul(x, g, wt, n_cols, name, ex=None):
    s, d = x.shape
    tm, tn = min(1024, s), _pick(n_cols, (2816, 1408, 512))

    def body(x_ref, g_ref, w_ref, o_ref, h_ref):
        @pl.when(pl.program_id(1) == 0)
        def _():
            xv = x_ref[...]
            h_ref[...] = (xv * _rstd(xv) * g_ref[...]).astype(BF16)

        o_ref[...] = _dot_nt(h_ref[...], w_ref[...]).astype(BF16)

    return _carried_call(
        body, ex, (s // tm, n_cols // tn),
        [pl.BlockSpec((tm, d), lambda i, j: (i, 0)), pl.BlockSpec((1, d), lambda i, j: (0, 0)),
         pl.BlockSpec((tn, d), lambda i, j: (j, 0))],
        [pl.BlockSpec((tm, tn), lambda i, j: (i, j)), pl.BlockSpec((tm, d), lambda i, j: (i, 0))],
        [jax.ShapeDtypeStruct((s, n_cols), BF16), jax.ShapeDtypeStruct((s, d), BF16)], [],
        ("arbitrary", "arbitrary"), name, (x, g, wt))


def tn_matmul(a, b, name, m_cols=None, ex=None):
    t = a.shape[0]
    m = m_cols or a.shape[1]
    n = b.shape[1]
    tk = min(1024, t)
    tmm = _pick(m, (1408, 1152, 1024, 512, 256))
    tn = _pick(n, (1408, 1152, 1024, 512, 128))
    nk = t // tk

    def body(a_ref, b_ref, o_ref, acc_ref):
        @pl.when(pl.program_id(2) == 0)
        def _():
            acc_ref[...] = jnp.zeros_like(acc_ref)

        acc_ref[...] += _dot_tn(a_ref[...].astype(BF16), b_ref[...].astype(BF16))

        @pl.when(pl.program_id(2) == nk - 1)
        def _():
            o_ref[...] = acc_ref[...].astype(BF16)

    if ex is None:
        return pl.pallas_call(
            body, grid=(m // tmm, n // tn, nk),
            in_specs=[pl.BlockSpec((tk, tmm), lambda i, j, k: (k, i)), pl.BlockSpec((tk, tn), lambda i, j, k: (k, j))],
            out_specs=pl.BlockSpec((tmm, tn), lambda i, j, k: (i, j)),
            out_shape=jax.ShapeDtypeStruct((m, n), BF16), scratch_shapes=[pltpu.VMEM((tmm, tn), F32)],
            compiler_params=_cparams(("parallel", "parallel", "arbitrary")), name=name)(a, b)
    (out,), carried = _carried_call(
        body, ex, (m // tmm, n // tn, nk),
        [pl.BlockSpec((tk, tmm), lambda i, j, k: (k, i)), pl.BlockSpec((tk, tn), lambda i, j, k: (k, j))],
        [pl.BlockSpec((tmm, tn), lambda i, j, k: (i, j))], [jax.ShapeDtypeStruct((m, n), BF16)],
        [pltpu.VMEM((tmm, tn), F32)], ("arbitrary", "arbitrary", "arbitrary"), name, (a, b))
    return out, carried


def matmul_normbwd(a, wt, x, g, dres, name, k=None, ex=None):
    s = a.shape[0]
    k = k or a.shape[1]
    d = wt.shape[1]
    tm = min(1024, s)
    tk = _pick(k, (1408, 1152, 512))
    nk = k // tk

    def body(a_ref, w_ref, x_ref, g_ref, r_ref, dx_ref, dg_ref, acc_ref):
        i, kk = pl.program_id(0), pl.program_id(1)

        @pl.when(kk == 0)
        def _():
            acc_ref[...] = jnp.zeros_like(acc_ref)

        @pl.when((i == 0) & (kk == 0))
        def _():
            dg_ref[...] = jnp.zeros_like(dg_ref)

        acc_ref[...] += _dot(a_ref[...], w_ref[...])

        @pl.when(kk == nk - 1)
        def _():
            xv = x_ref[...]
            r = _rstd(xv)
            y = xv * r
            dh = acc_ref[...]
            dy = dh * g_ref[...]
            dx_ref[...] = r_ref[...] + r * (dy - y * jnp.mean(dy * y, axis=-1, keepdims=True))
            dg_ref[...] += jnp.sum(dh * y, axis=0, keepdims=True)

    return _carried_call(
        body, ex, (s // tm, nk),
        [pl.BlockSpec((tm, tk), lambda i, kk: (i, kk)), pl.BlockSpec((tk, d), lambda i, kk: (kk, 0)),
         pl.BlockSpec((tm, d), lambda i, kk: (i, 0)), pl.BlockSpec((1, d), lambda i, kk: (0, 0)),
         pl.BlockSpec((tm, d), lambda i, kk: (i, 0))],
        [pl.BlockSpec((tm, d), lambda i, kk: (i, 0)), pl.BlockSpec((1, d), lambda i, kk: (0, 0))],
        [jax.ShapeDtypeStruct((s, d), F32), jax.ShapeDtypeStruct((1, d), F32)],
        [pltpu.VMEM((tm, d), F32)], ("arbitrary", "arbitrary"), name, (a, wt, x, g, dres), vmem=56 * 2 ** 20)


def swiglu_matmul(gu, w, x1, name):
    s = gu.shape[0]
    d = w.shape[1]
    tm = min(512, s)

    def body(gu_ref, w_ref, x_ref, o_ref):
        acc = x_ref[...]
        for j in range(N_FF_BLKS):
            gt = gu_ref[:, j * FF_BLK:(j + 1) * FF_BLK].astype(F32)
            up = gu_ref[:, D_FF + j * FF_BLK:D_FF + (j + 1) * FF_BLK].astype(F32)
            act = (gt * _sigmoid(gt) * up).astype(BF16)
            acc += _dot(act, w_ref[j * FF_BLK:(j + 1) * FF_BLK, :])
        o_ref[...] = acc

    return pl.pallas_call(
        body, grid=(s // tm,),
        in_specs=[pl.BlockSpec((tm, 2 * D_FF), lambda i: (i, 0)), pl.BlockSpec((D_FF, d), lambda i: (0, 0)),
                  pl.BlockSpec((tm, d), lambda i: (i, 0))],
        out_specs=pl.BlockSpec((tm, d), lambda i: (i, 0)),
        out_shape=jax.ShapeDtypeStruct((s, d), F32),
        compiler_params=_cparams(("parallel",)), name=name)(gu, w, x1)


def swiglu_bwd(dx2, gu, w, name, ex=None):
    s, d = dx2.shape
    tm = min(512, s)

    def body(dx_ref, gu_ref, w_ref, dgu_ref, act_ref):
        dx = dx_ref[...].astype(BF16)
        for j in range(N_FF_BLKS):
            g_cols = slice(j * FF_BLK, (j + 1) * FF_BLK)
            u_cols = slice(D_FF + j * FF_BLK, D_FF + (j + 1) * FF_BLK)
            dact = _dot_nt(dx, w_ref[j * FF_BLK:(j + 1) * FF_BLK, :])
            gt = gu_ref[:, g_cols].astype(F32)
            up = gu_ref[:, u_cols].astype(F32)
            sg = _sigmoid(gt)
            act_ref[:, j * FF_BLK:(j + 1) * FF_BLK] = (gt * sg * up).astype(BF16)
            dgu_ref[:, g_cols] = (dact * up * (sg * (1.0 + gt * (1.0 - sg)))).astype(BF16)
            dgu_ref[:, u_cols] = (dact * gt * sg).astype(BF16)

    return _carried_call(
        body, ex, (s // tm,),
        [pl.BlockSpec((tm, d), lambda i: (i, 0)), pl.BlockSpec((tm, 2 * D_FF), lambda i: (i, 0)),
         pl.BlockSpec((D_FF, d), lambda i: (0, 0), pipeline_mode=pl.Buffered(1))],
        [pl.BlockSpec((tm, 2 * D_FF), lambda i: (i, 0)), pl.BlockSpec((tm, D_FF), lambda i: (i, 0))],
        [jax.ShapeDtypeStruct((s, 2 * D_FF), BF16), jax.ShapeDtypeStruct((s, D_FF), BF16)], [],
        ("arbitrary",), name, (dx2, gu, w), vmem=56 * 2 ** 20)


def loss_kernel(y, tgt, name):
    s, d = y.shape
    tm = min(512, s)

    def body(y_ref, t_ref, l_ref, dy_ref):
        @pl.when(pl.program_id(0) == 0)
        def _():
            l_ref[...] = jnp.zeros_like(l_ref)

        err = y_ref[...] - t_ref[...]
        dy_ref[...] = err * (1.0 / d)
        l_ref[...] += jnp.sum(jnp.sum(err * err, axis=1, keepdims=True), axis=0, keepdims=True)

    return pl.pallas_call(
        body, grid=(s // tm,),
        in_specs=[pl.BlockSpec((tm, d), lambda i: (i, 0)), pl.BlockSpec((tm, d), lambda i: (i, 0))],
        out_specs=[pl.BlockSpec((8, 128), lambda i: (0, 0)), pl.BlockSpec((tm, d), lambda i: (i, 0))],
        out_shape=[jax.ShapeDtypeStruct((8, 128), F32), jax.ShapeDtypeStruct((s, d), F32)],
        compiler_params=_cparams(("arbitrary",)), name=name)(y, tgt)


def _split3(v):
    a1 = v.astype(BF16)
    r1 = v - a1.astype(F32)
    a2 = r1.astype(BF16)
    a3 = (r1 - a2.astype(F32)).astype(BF16)
    return a1, a2, a3


def forget_fwd(h, wt_in, b, name):
    s, d = h.shape
    tm = min(512, s)

    def body(h_ref, w_ref, b_ref, z_ref, c_ref, carry_ref):
        @pl.when(pl.program_id(0) == 0)
        def _():
            carry_ref[...] = jnp.zeros_like(carry_ref)

        z = _dot_nt(h_ref[...], w_ref[...]) + b_ref[...]
        z_ref[...] = z
        logf = jnp.minimum(z, 0.0) - jnp.log(1.0 + jnp.exp(-jnp.abs(z)))
        row = lax.broadcasted_iota(jnp.int32, (tm, tm), 0)
        col = lax.broadcasted_iota(jnp.int32, (tm, tm), 1)
        tri = (row >= col).astype(BF16)
        a1, a2, a3 = _split3(logf)
        c = _dot(tri, a1) + _dot(tri, a2) + _dot(tri, a3) + carry_ref[...]
        c_ref[...] = c
        carry_ref[...] = c[tm - 1:tm, :]

    return pl.pallas_call(
        body, grid=(s // tm,),
        in_specs=[pl.BlockSpec((tm, d), lambda i: (i, 0)), pl.BlockSpec((128, d), lambda i: (N_MAIN // 128, 0)),
                  pl.BlockSpec((1, 128), lambda i: (0, 0))],
        out_specs=[pl.BlockSpec((tm, 128), lambda i: (i, 0)), pl.BlockSpec((tm, 128), lambda i: (i, 0))],
        out_shape=[jax.ShapeDtypeStruct((s, 128), F32), jax.ShapeDtypeStruct((s, 128), F32)],
        scratch_shapes=[pltpu.VMEM((1, 128), F32)],
        compiler_params=_cparams(("arbitrary",)), name=name)(h, wt_in, b)


def forget_bwd(dc, z, dproj, name):
    s = dc.shape[0]
    tm = min(512, s)
    nt = s // tm

    def body(dc_ref, z_ref, dp_ref, dz_ref, db_ref, carry_ref):
        @pl.when(pl.program_id(0) == 0)
        def _():
            carry_ref[...] = jnp.zeros_like(carry_ref)
            db_ref[...] = jnp.zeros_like(db_ref)

        row = lax.broadcasted_iota(jnp.int32, (tm, tm), 0)
        col = lax.broadcasted_iota(jnp.int32, (tm, tm), 1)
        tri = (col >= row).astype(BF16)
        a1, a2, a3 = _split3(dc_ref[...])
        dlogf = _dot(tri, a1) + _dot(tri, a2) + _dot(tri, a3) + carry_ref[...]
        carry_ref[...] = dlogf[0:1, :]
        dz = dlogf * (1.0 - _sigmoid(z_ref[...]))
        dz_ref[...] = dz.astype(BF16)
        db_ref[...] += jnp.sum(dz, axis=0, keepdims=True)

    return pl.pallas_call(
        body, grid=(nt,),
        in_specs=[pl.BlockSpec((tm, 128), lambda i: (nt - 1 - i, 0)), pl.BlockSpec((tm, 128), lambda i: (nt - 1 - i, 0)),
                  pl.BlockSpec(memory_space=pl.ANY)],
        out_specs=[pl.BlockSpec((tm, 128), lambda i: (nt - 1 - i, N_MAIN // 128)), pl.BlockSpec((1, 128), lambda i: (0, 0))],
        out_shape=[jax.ShapeDtypeStruct(dproj.shape, BF16), jax.ShapeDtypeStruct((1, 128), F32)],
        scratch_shapes=[pltpu.VMEM((1, 128), F32)], input_output_aliases={2: 0},
        compiler_params=_cparams(("arbitrary",)), name=name)(dc, z, dproj)


HEAD_GROUP = 4
LANE_C = 64
LANE_ONE = 67


def _lanes():
    lane = lax.broadcasted_iota(jnp.int32, (1, 128), 1)
    return lane, lane < HEAD_DIM


def _half_mean(t, lo):
    s_lo = jnp.sum(jnp.where(lo, t, 0.0), axis=-1, keepdims=True)
    s_hi = jnp.sum(jnp.where(lo, 0.0, t), axis=-1, keepdims=True)
    return jnp.where(lo, s_lo, s_hi) * (1.0 / HEAD_DIM)


def _lane_col(t, lane, idx):
    return jnp.sum(jnp.where(lane == idx, t, 0.0), axis=-1, keepdims=True)


def _swap_halves(t):
    return pltpu.roll(t, HEAD_DIM, 1)


def _causal(s_blk, tq, tk):
    row = lax.broadcasted_iota(jnp.int32, (tq, tk), 0)
    col = lax.broadcasted_iota(jnp.int32, (tq, tk), 1)
    return jnp.where(row >= col, s_blk, -jnp.inf)


def attn_prep(proj, c, gq2, gk2, name):
    s = proj.shape[0]
    tm = min(512, s)
    first = N_REST // 128

    def body(q_ref, k_ref, v_ref, c_ref, gq_ref, gk_ref, qa_ref, ka_ref, va_ref, vt_ref):
        j = pl.program_id(1)
        lane, lo = _lanes()

        def normed(ref, g):
            t = ref[...].astype(F32)
            return t * lax.rsqrt(_half_mean(t * t, lo) + EPS) * g

        qn = normed(q_ref, gq_ref[...] * ATTN_SCALE)
        kn = normed(k_ref, gk_ref[...])
        vv = v_ref[...].astype(F32)
        cv = c_ref[...]
        one_q = jnp.where((lane >= LANE_ONE) & (lane < LANE_ONE + 3), 1.0, 0.0)
        one_k = jnp.where((lane >= LANE_C) & (lane < LANE_C + 3), 1.0, 0.0)
        one_v = jnp.where(lane == LANE_C, 1.0, 0.0)
        for e in range(2):
            pick = (lambda t: t) if e == 0 else _swap_halves
            pieces = [p.astype(F32) for p in _split3(_lane_col(cv, lane, 2 * j + e))]
            ext_q, ext_k = one_q, one_k
            for i, p in enumerate(pieces):
                ext_q = jnp.where(lane == LANE_C + i, p, ext_q)
                ext_k = jnp.where(lane == LANE_ONE + i, -p, ext_k)
            qa_ref[e] = jnp.where(lo, pick(qn), ext_q).astype(BF16)
            ka_ref[e] = jnp.where(lo, pick(kn), ext_k).astype(BF16)
            va = jnp.where(lo, pick(vv), one_v)
            va_ref[e] = va.astype(BF16)
            vt_ref[e] = va.T.astype(BF16)

    tile = lambda base: pl.BlockSpec((tm, 128), lambda i, j: (i, base + j))
    vec = pl.BlockSpec((1, 128), lambda i, j: (0, 0))
    out = pl.BlockSpec((2, tm, 128), lambda i, j: (j, i, 0))
    return pl.pallas_call(
        body, grid=(s // tm, HEADS // 2),
        in_specs=[tile(first), tile(first + 4), tile(first + 8), pl.BlockSpec((tm, 128), lambda i, j: (i, 0)), vec, vec],
        out_specs=[out, out, out, pl.BlockSpec((2, 128, tm), lambda i, j: (j, 0, i))],
        out_shape=[jax.ShapeDtypeStruct((HEADS, s, 128), BF16)] * 3 + [jax.ShapeDtypeStruct((HEADS, 128, s), BF16)],
        compiler_params=_cparams(("parallel", "arbitrary")), name=name)(proj, proj, proj, c, gq2, gk2)


def attn_fwd(q, k, v, ccol, crow, gq, gk, name):
    hh, s, hd = q.shape
    tq = tk = min(512, s)
    nq = s // tq

    def body(q_ref, k_ref, v_ref, cc_ref, cr_ref, gq_ref, gk_ref, o_ref, lse_ref, qn_ref, m_ref, l_ref, acc_ref):
        qi, ki = pl.program_id(1), pl.program_id(2)

        @pl.when(ki == 0)
        def _():
            qn_ref[...] = _qk_hat(q_ref, gq_ref, ATTN_SCALE)
            m_ref[...] = jnp.full_like(m_ref, -jnp.inf)
            l_ref[...] = jnp.zeros_like(l_ref)
            acc_ref[...] = jnp.zeros_like(acc_ref)

        @pl.when(ki <= qi)
        def _():
            kn = _qk_hat(k_ref, gk_ref, 1.0)
            sb = _dot_nt(qn_ref[...], kn) + (cc_ref[...] - cr_ref[...])
            sb = _causal(sb, qi, ki, tq, tk)
            m_new = jnp.maximum(m_ref[...], jnp.max(sb, axis=-1, keepdims=True))
            alpha = jnp.exp(m_ref[...] - m_new)
            p = jnp.exp(sb - m_new)
            l_ref[...] = alpha * l_ref[...] + jnp.sum(p, axis=-1, keepdims=True)
            acc_ref[...] = alpha * acc_ref[...] + _dot(p.astype(BF16), v_ref[...])
            m_ref[...] = m_new

        @pl.when(ki == qi)
        def _():
            o_ref[...] = (acc_ref[...] / l_ref[...]).astype(BF16)
            lse_ref[...] = m_ref[...] + jnp.log(l_ref[...])

    qspec = pl.BlockSpec((None, tq, hd), lambda h, i, j: (h, i, 0))
    kspec = pl.BlockSpec((None, tk, hd), lambda h, i, j: (h, jnp.minimum(i, j), 0))
    gspec = pl.BlockSpec((1, hd), lambda h, i, j: (0, 0))
    return pl.pallas_call(
        body, grid=(hh, nq, nq),
        in_specs=[qspec, kspec, kspec,
                  pl.BlockSpec((None, tq, 1), lambda h, i, j: (h, i, 0)),
                  pl.BlockSpec((None, 1, tk), lambda h, i, j: (h, 0, jnp.minimum(i, j))), gspec, gspec],
        out_specs=[qspec, pl.BlockSpec((None, tq, 1), lambda h, i, j: (h, i, 0))],
        out_shape=[jax.ShapeDtypeStruct((hh, s, hd), BF16), jax.ShapeDtypeStruct((hh, s, 1), F32)],
        scratch_shapes=[pltpu.VMEM((tq, hd), BF16), pltpu.VMEM((tq, 1), F32), pltpu.VMEM((tq, 1), F32),
                        pltpu.VMEM((tq, hd), F32)],
        compiler_params=_cparams(("parallel", "parallel", "arbitrary")), name=name)(q, k, v, ccol, crow, gq, gk)


def attn_bwd_dq(q, k, v, o, do, lse, ccol, crow, gq, gk, name):
    hh, s, hd = q.shape
    tq = tk = min(512, s)
    nq = s // tq

    def body(q_ref, k_ref, v_ref, o_ref, do_ref, lse_ref, cc_ref, cr_ref, gq_ref, gk_ref,
             dq_ref, dcc_ref, dg_ref, qn_ref, dl_ref, acc_ref, dca_ref):
        h, qi, ki = pl.program_id(0), pl.program_id(1), pl.program_id(2)

        @pl.when((h == 0) & (qi == 0) & (ki == 0))
        def _():
            dg_ref[...] = jnp.zeros_like(dg_ref)

        @pl.when(ki == 0)
        def _():
            qn_ref[...] = _qk_hat(q_ref, gq_ref, ATTN_SCALE)
            dl_ref[...] = jnp.sum(do_ref[...].astype(F32) * o_ref[...].astype(F32), axis=-1, keepdims=True)
            acc_ref[...] = jnp.zeros_like(acc_ref)
            dca_ref[...] = jnp.zeros_like(dca_ref)

        @pl.when(ki <= qi)
        def _():
            kn = _qk_hat(k_ref, gk_ref, 1.0)
            sb = _dot_nt(qn_ref[...], kn) + (cc_ref[...] - cr_ref[...])
            p = jnp.exp(_causal(sb, qi, ki, tq, tk) - lse_ref[...])
            dp = _dot_nt(do_ref[...], v_ref[...])
            ds = p * (dp - dl_ref[...])
            acc_ref[...] += _dot(ds.astype(BF16), kn)
            dca_ref[...] += jnp.sum(ds, axis=-1, keepdims=True)

        @pl.when(ki == qi)
        def _():
            dq, dg = _norm_bwd(q_ref[...].astype(F32), gq_ref[...], acc_ref[...], ATTN_SCALE)
            dq_ref[...] = dq.astype(BF16)
            dcc_ref[...] = dca_ref[...]
            dg_ref[...] += dg

    qspec = pl.BlockSpec((None, tq, hd), lambda h, i, j: (h, i, 0))
    kspec = pl.BlockSpec((None, tk, hd), lambda h, i, j: (h, jnp.minimum(i, j), 0))
    cspec = pl.BlockSpec((None, tq, 1), lambda h, i, j: (h, i, 0))
    gspec = pl.BlockSpec((1, hd), lambda h, i, j: (0, 0))
    return pl.pallas_call(
        body, grid=(hh, nq, nq),
        in_specs=[qspec, kspec, kspec, qspec, qspec, cspec, cspec,
                  pl.BlockSpec((None, 1, tk), lambda h, i, j: (h, 0, jnp.minimum(i, j))), gspec, gspec],
        out_specs=[qspec, cspec, gspec],
        out_shape=[jax.ShapeDtypeStruct((hh, s, hd), BF16), jax.ShapeDtypeStruct((hh, s, 1), F32),
                   jax.ShapeDtypeStruct((1, hd), F32)],
        scratch_shapes=[pltpu.VMEM((tq, hd), BF16), pltpu.VMEM((tq, 1), F32), pltpu.VMEM((tq, hd), F32),
                        pltpu.VMEM((tq, 1), F32)],
        compiler_params=_cparams(("arbitrary", "arbitrary", "arbitrary")), name=name)(
            q, k, v, o, do, lse, ccol, crow, gq, gk)


def attn_bwd_dkv(q, k, v, o, do, lse, ccol, crow, gq, gk, name):
    hh, s, hd = q.shape
    tq = tk = min(512, s)
    nq = s // tq

    def body(q_ref, k_ref, v_ref, o_ref, do_ref, lse_ref, cc_ref, cr_ref, gq_ref, gk_ref,
             dk_ref, dv_ref, dcr_ref, dg_ref, kn_ref, dka_ref, dva_ref, dca_ref):
        h, ki, qi = pl.program_id(0), pl.program_id(1), pl.program_id(2)

        @pl.when((h == 0) & (ki == 0) & (qi == 0))
        def _():
            dg_ref[...] = jnp.zeros_like(dg_ref)

        @pl.when(qi == 0)
        def _():
            kn_ref[...] = _qk_hat(k_ref, gk_ref, 1.0)
            dka_ref[...] = jnp.zeros_like(dka_ref)
            dva_ref[...] = jnp.zeros_like(dva_ref)
            dca_ref[...] = jnp.zeros_like(dca_ref)

        @pl.when(qi >= ki)
        def _():
            qn = _qk_hat(q_ref, gq_ref, ATTN_SCALE)
            do = do_ref[...]
            delta = jnp.sum(do.astype(F32) * o_ref[...].astype(F32), axis=-1, keepdims=True)
            sb = _dot_nt(qn, kn_ref[...]) + (cc_ref[...] - cr_ref[...])
            p = jnp.exp(_causal(sb, qi, ki, tq, tk) - lse_ref[...])
            dva_ref[...] += _dot_tn(p.astype(BF16), do)
            ds = p * (_dot_nt(do, v_ref[...]) - delta)
            dka_ref[...] += _dot_tn(ds.astype(BF16), qn)
            dca_ref[...] += jnp.sum(ds, axis=0, keepdims=True)

        @pl.when(qi == nq - 1)
        def _():
            dk, dg = _norm_bwd(k_ref[...].astype(F32), gk_ref[...], dka_ref[...], 1.0)
            dk_ref[...] = dk.astype(BF16)
            dv_ref[...] = dva_ref[...].astype(BF16)
            dcr_ref[...] = dca_ref[...]
            dg_ref[...] += dg

    kspec = pl.BlockSpec((None, tk, hd), lambda h, j, i: (h, j, 0))
    qspec = pl.BlockSpec((None, tq, hd), lambda h, j, i: (h, jnp.maximum(i, j), 0))
    cspec = pl.BlockSpec((None, tq, 1), lambda h, j, i: (h, jnp.maximum(i, j), 0))
    rspec = pl.BlockSpec((None, 1, tk), lambda h, j, i: (h, 0, j))
    gspec = pl.BlockSpec((1, hd), lambda h, j, i: (0, 0))
    return pl.pallas_call(
        body, grid=(hh, nq, nq),
        in_specs=[qspec, kspec, kspec, qspec, qspec, cspec, cspec, rspec, gspec, gspec],
        out_specs=[kspec, kspec, rspec, gspec],
        out_shape=[jax.ShapeDtypeStruct((hh, s, hd), BF16), jax.ShapeDtypeStruct((hh, s, hd), BF16),
                   jax.ShapeDtypeStruct((hh, 1, s), F32), jax.ShapeDtypeStruct((1, hd), F32)],
        scratch_shapes=[pltpu.VMEM((tk, hd), BF16), pltpu.VMEM((tk, hd), F32), pltpu.VMEM((tk, hd), F32),
                        pltpu.VMEM((1, tk), F32)],
        compiler_params=_cparams(("arbitrary", "arbitrary", "arbitrary")), name=name)(
            q, k, v, o, do, lse, ccol, crow, gq, gk)


def _carry(ex, n_in, n_out, n_scratch, grid):
    n_xin, n_xout = (len(ex.inputs), len(ex.out_shapes)) if ex else (0, 0)

    def split(refs):
        ins, xins = refs[:n_in], refs[n_in:n_in + n_xin]
        rest = refs[n_in + n_xin:]
        outs, xouts = rest[:n_out], rest[n_out:n_out + n_xout]
        rest = rest[n_out + n_xout:]
        return ins + outs + rest[:n_scratch], (xins, xouts, rest[n_scratch:])

    def first():
        return functools.reduce(lambda a, b: a & b, [pl.program_id(d) == 0 for d in range(len(grid))])

    def last():
        return functools.reduce(lambda a, b: a & b, [pl.program_id(d) == grid[d] - 1 for d in range(len(grid))])

    return split, first, last


def _carried_call(body, ex, grid, in_specs, out_specs, out_shape, scratch, sem, name, operands, vmem=None):
    any_spec = pl.BlockSpec(memory_space=pl.ANY)
    split, first, last = _carry(ex, len(in_specs), len(out_specs), len(scratch), grid)

    def carried(*refs):
        own, xrefs = split(refs)
        if ex:
            @pl.when(first())
            def _():
                ex.start(*xrefs)

        body(*own)
        if ex:
            @pl.when(last())
            def _():
                ex.drain(*xrefs)

    n_xin = len(ex.inputs) if ex else 0
    results = pl.pallas_call(
        carried, grid=grid, in_specs=list(in_specs) + [any_spec] * n_xin,
        out_specs=list(out_specs) + [any_spec] * (len(ex.out_shapes) if ex else 0),
        out_shape=list(out_shape) + (list(ex.out_shapes) if ex else []),
        input_output_aliases={len(in_specs) + i: len(out_specs) + o for i, o in ex.aliases.items()} if ex else {},
        scratch_shapes=list(scratch) + (ex.scratch if ex else []),
        compiler_params=_cparams(sem, vmem), name=name)(*operands, *(ex.inputs if ex else []))
    return results[:len(out_specs)], results[len(out_specs):]


def _tri_rows(t, n):
    qi = sum(jnp.where(t >= r * (r + 1) // 2, 1, 0) for r in range(1, n))
    return qi, t - qi * (qi + 1) // 2


def _tri_cols(t, n):
    ki = sum(jnp.where(t >= r * n - r * (r - 1) // 2, 1, 0) for r in range(1, n))
    return ki, ki + t - (ki * n - ki * (ki - 1) // 2)


def _causal_t(st_blk, tk, tq):
    key = lax.broadcasted_iota(jnp.int32, (tk, tq), 0)
    qry = lax.broadcasted_iota(jnp.int32, (tk, tq), 1)
    return jnp.where(qry >= key, st_blk, -jnp.inf)


def attn_forward(qa, ka, vt, name, ex=None):
    hh, s, _ = qa.shape
    tq = tk = min(512, s)
    nq = s // tq
    grp = HEAD_GROUP

    def body(q_ref, k_ref, vt_ref, o_ref, lse_ref, m_ref, acc_ref):
        qi, ki = _tri_rows(pl.program_id(1), nq)

        @pl.when(ki == 0)
        def _():
            m_ref[...] = jnp.full_like(m_ref, -jnp.inf)
            acc_ref[...] = jnp.zeros_like(acc_ref)

        def step(masked):
            nxt = _dot_nt(k_ref[0], q_ref[0])
            for g in range(grp):
                st = nxt
                if g + 1 < grp:
                    nxt = _dot_nt(k_ref[g + 1], q_ref[g + 1])
                if masked:
                    st = _causal_t(st, tk, tq)
                m_old = m_ref[g]
                m_new = jnp.maximum(m_old, jnp.max(st, axis=0, keepdims=True))
                pt = jnp.exp(st - m_new).astype(BF16)
                acc_ref[g] = jnp.exp(m_old - m_new) * acc_ref[g] + _dot(vt_ref[g], pt)
                m_ref[g] = m_new

        @pl.when(ki < qi)
        def _():
            step(False)

        @pl.when(ki == qi)
        def _():
            step(True)
            for g in range(grp):
                acc = acc_ref[g]
                denom = acc[LANE_C:LANE_C + 1, :]
                o_ref[g] = (acc / denom).T.astype(BF16)
                lse_ref[g] = m_ref[g] + jnp.log(denom)

    qspec = pl.BlockSpec((grp, tq, 128), lambda h, t: (h, _tri_rows(t, nq)[0], 0))
    kspec = pl.BlockSpec((grp, tk, 128), lambda h, t: (h, _tri_rows(t, nq)[1], 0))
    vspec = pl.BlockSpec((grp, 128, tk), lambda h, t: (h, 0, _tri_rows(t, nq)[1]))
    lspec = pl.BlockSpec((grp, 1, tq), lambda h, t: (h, 0, _tri_rows(t, nq)[0]))
    return _carried_call(
        body, ex, (hh // grp, nq * (nq + 1) // 2), [qspec, kspec, vspec], [qspec, lspec],
        [jax.ShapeDtypeStruct((hh, s, 128), BF16), jax.ShapeDtypeStruct((hh, 1, s), F32)],
        [pltpu.VMEM((grp, 1, tq), F32), pltpu.VMEM((grp, 128, tq), F32)],
        ("arbitrary", "arbitrary"), name, (qa, ka, vt))


def attn_backward(qa, ka, va, oa, doa, lse, name, ex=None):
    hh, s, _ = qa.shape
    tq = tk = min(512, s)
    nq = s // tq
    grp = HEAD_GROUP

    def body(q_ref, k_ref, v_ref, o_ref, do_ref, lse_ref, dq_ref, dk_ref, dv_ref, dka_ref, dva_ref):
        ki, qi = _tri_cols(pl.program_id(1), nq)

        @pl.when(pl.program_id(1) == 0)
        def _():
            dq_ref[...] = jnp.zeros_like(dq_ref)

        @pl.when(qi == ki)
        def _():
            dka_ref[...] = jnp.zeros_like(dka_ref)
            dva_ref[...] = jnp.zeros_like(dva_ref)

        def step(masked):
            rows = pl.ds(pl.multiple_of(qi * tq, tq), tq)
            products = lambda g: (_dot_nt(k_ref[g], q_ref[g]), _dot_nt(v_ref[g], do_ref[g]))
            nxt = products(0)
            for g in range(grp):
                st, dpt = nxt
                if g + 1 < grp:
                    nxt = products(g + 1)
                q, k, do = q_ref[g], k_ref[g], do_ref[g]
                if masked:
                    st = _causal_t(st, tk, tq)
                pt = jnp.exp(st - lse_ref[g])
                delta = jnp.sum((do.astype(F32) * o_ref[g].astype(F32)).T, axis=0, keepdims=True)
                dst = (pt * (dpt - delta)).astype(BF16)
                dva_ref[g] += _dot(pt.astype(BF16), do)
                dka_ref[g] += _dot(dst, q)
                dq_ref[g, rows, :] += _dot_tn(dst, k)

        @pl.when(qi > ki)
        def _():
            step(False)

        @pl.when(qi == ki)
        def _():
            step(True)

        @pl.when(qi == nq - 1)
        def _():
            dk_ref[...] = dka_ref[...]
            dv_ref[...] = dva_ref[...].astype(BF16)

    qspec = pl.BlockSpec((grp, tq, 128), lambda h, t: (h, _tri_cols(t, nq)[1], 0))
    lspec = pl.BlockSpec((grp, 1, tq), lambda h, t: (h, 0, _tri_cols(t, nq)[1]))
    kspec = pl.BlockSpec((grp, tk, 128), lambda h, t: (h, _tri_cols(t, nq)[0], 0))
    return _carried_call(
        body, ex, (hh // grp, nq * (nq + 1) // 2), [qspec, kspec, kspec, qspec, qspec, lspec],
        [pl.BlockSpec((grp, s, 128), lambda h, t: (h, 0, 0)), kspec, kspec],
        [jax.ShapeDtypeStruct((hh, s, 128), F32), jax.ShapeDtypeStruct((hh, s, 128), F32),
         jax.ShapeDtypeStruct((hh, s, 128), BF16)],
        [pltpu.VMEM((grp, tk, 128), F32), pltpu.VMEM((grp, tk, 128), F32)],
        ("arbitrary", "arbitrary"), name, (qa, ka, va, oa, doa, lse))


def attn_post(dqa, dka, dva, proj, gq2, gk2, dproj, name):
    s = proj.shape[0]
    tm = min(256, s)

    def body(dq_ref, dk_ref, dv_ref, q_ref, k_ref, gq_ref, gk_ref, dp_any, dp_ref, dc_ref, dgq_ref, dgk_ref):
        lane, lo = _lanes()

        @pl.when(pl.program_id(0) == 0)
        def _():
            dgq_ref[...] = jnp.zeros_like(dgq_ref)
            dgk_ref[...] = jnp.zeros_like(dgk_ref)

        def pair(ref, j):
            return jnp.where(lo, ref[2 * j].astype(F32), _swap_halves(ref[2 * j + 1].astype(F32)))

        def norm_bwd(raw, g, dhat, scale):
            r = lax.rsqrt(_half_mean(raw * raw, lo) + EPS)
            y = raw * r
            dy = dhat * (g * scale)
            return r * (dy - y * _half_mean(dy * y, lo)), jnp.sum(dhat * y, axis=0, keepdims=True) * scale

        dc = jnp.zeros((tm, 128), F32)
        for j in range(HEADS // 2):
            cols = slice(128 * j, 128 * (j + 1))
            dq, dgq = norm_bwd(q_ref[:, cols].astype(F32), gq_ref[...], pair(dq_ref, j), ATTN_SCALE)
            dk, dgk = norm_bwd(k_ref[:, cols].astype(F32), gk_ref[...], pair(dk_ref, j), 1.0)
            dgq_ref[...] += dgq
            dgk_ref[...] += dgk
            dp_ref[:, cols] = dq.astype(BF16)
            dp_ref[:, D_ATTN + 128 * j:D_ATTN + 128 * (j + 1)] = dk.astype(BF16)
            dp_ref[:, 2 * D_ATTN + 128 * j:2 * D_ATTN + 128 * (j + 1)] = pair(dv_ref, j).astype(BF16)
            for e in range(2):
                h = 2 * j + e
                col = _lane_col(dq_ref[h], lane, LANE_C) - _lane_col(dk_ref[h], lane, LANE_ONE)
                dc = jnp.where(lane == h, col, dc)
        dp_ref[:, 3 * D_ATTN:] = jnp.zeros((tm, DPROJ_TAIL - 3 * D_ATTN), BF16)
        dc_ref[...] = dc

    heads = lambda: pl.BlockSpec((HEADS, tm, 128), lambda i: (0, i, 0))
    vec = pl.BlockSpec((1, 128), lambda i: (0, 0))
    first = N_REST // D_ATTN
    return pl.pallas_call(
        body, grid=(s // tm,),
        in_specs=[heads(), heads(), heads(), pl.BlockSpec((tm, D_ATTN), lambda i: (i, first)),
                  pl.BlockSpec((tm, D_ATTN), lambda i: (i, first + 1)), vec, vec, pl.BlockSpec(memory_space=pl.ANY)],
        out_specs=[pl.BlockSpec((tm, DPROJ_TAIL), lambda i: (i, N_REST // DPROJ_TAIL)),
                   pl.BlockSpec((tm, 128), lambda i: (i, 0)), vec, vec],
        out_shape=[jax.ShapeDtypeStruct(dproj.shape, BF16), jax.ShapeDtypeStruct((s, 128), F32),
                   jax.ShapeDtypeStruct((1, 128), F32), jax.ShapeDtypeStruct((1, 128), F32)],
        input_output_aliases={7: 0},
        compiler_params=_cparams(("arbitrary",)), name=name)(dqa, dka, dva, proj, proj, gq2, gk2, dproj)


def _pool_groups(tm):
    gid = lax.broadcasted_iota(jnp.int32, (1, D_POOL), 1) // (D_POOL // 4)
    win = jnp.where(gid == 0, 2.0, jnp.where(gid == 1, 4.0, jnp.where(gid == 2, 8.0, 16.0)))
    return gid, win


def _by_group(gid, v2, v4, v8, v16):
    return jnp.where(gid == 0, v2, jnp.where(gid == 1, v4, jnp.where(gid == 2, v8, v16)))


def _branches(rest_ref, halo_ref, a_ref, wa_ref, wc_ref, wp_ref, sc_ref, cw_ref, ti, tm):
    f = lambda v: v.astype(F32)
    cx, cb, cc, px = f(rest_ref[:, 0:256]), f(rest_ref[:, 256:512]), f(rest_ref[:, 512:768]), f(rest_ref[:, 768:1024])
    live = jnp.where(ti > 0, 1.0, 0.0)
    hz = f(halo_ref[:, 0:256]) * f(halo_ref[:, 512:768]) * live
    hp = f(halo_ref[:, 768:1024]) * live
    z = cc * cx
    zf = jnp.concatenate([hz, z], axis=0)
    z1 = pltpu.roll(zf, 1, 0)[HALO:]
    z2 = pltpu.roll(zf, 2, 0)[HALO:]
    cw = cw_ref[...]
    conv = cw[2:3] * z + cw[1:2] * z1 + cw[0:1] * z2
    uc = cb * conv
    pf = jnp.concatenate([hp, px], axis=0)
    s2 = pf + pltpu.roll(pf, 1, 0)
    s4 = s2 + pltpu.roll(s2, 2, 0)
    s8 = s4 + pltpu.roll(s4, 4, 0)
    s16 = s8 + pltpu.roll(s8, 8, 0)
    gid, win = _pool_groups(tm)
    t = (ti * tm + lax.broadcasted_iota(jnp.int32, (tm, 1), 0)).astype(F32)
    inv = 1.0 / jnp.minimum(t + 1.0, win)
    dpool = _by_group(gid, s2[HALO:], s4[HALO:], s8[HALO:], s16[HALO:]) * inv - px
    _, lo = _lanes()
    a_tok = [jnp.where(lo, f(a_ref[2 * j]), _swap_halves(f(a_ref[2 * j + 1]))).astype(BF16) for j in range(HEADS // 2)]
    y_attn = _dot(a_tok[0], wa_ref[0:128, :])
    for j in range(1, HEADS // 2):
        y_attn += _dot(a_tok[j], wa_ref[128 * j:128 * (j + 1), :])
    y_conv = _dot(uc.astype(BF16), wc_ref[...])
    y_pool_raw = _dot(dpool.astype(BF16), wp_ref[...])
    sg = [_sigmoid(f(rest_ref[:, 1024 + i * D_MODEL:1024 + (i + 1) * D_MODEL])) for i in range(3)]
    return dict(cx=cx, cb=cb, cc=cc, z=z, z1=z1, z2=z2, conv=conv, uc=uc, dpool=dpool, inv=inv, gid=gid, a_tok=a_tok,
                y_attn=y_attn, y_conv=y_conv, y_pool_raw=y_pool_raw, sg=sg, cw=cw)


def _mix_specs(tm, ti_of):
    blocks_per_tile = tm // HALO
    return [
        pl.BlockSpec((tm, N_REST), lambda i: (ti_of(i), 0)),
        pl.BlockSpec((HALO, 1024), lambda i: (jnp.maximum(ti_of(i) * blocks_per_tile - 1, 0), 0)),
        pl.BlockSpec((HEADS, tm, 128), lambda i: (0, ti_of(i), 0)),
        pl.BlockSpec((D_ATTN, D_MODEL), lambda i: (0, 0)),
        pl.BlockSpec((D_CONV, D_MODEL), lambda i: (0, 0)),
        pl.BlockSpec((D_POOL, D_MODEL), lambda i: (0, 0)),
        pl.BlockSpec((1, D_MODEL), lambda i: (0, 0)),
        pl.BlockSpec((8, D_CONV), lambda i: (0, 0)),
    ]


def mix_fwd(proj, a, x, wa, wc, wp, scale, cw, wo, name, ex=None):
    s = x.shape[0]
    tm = min(256, s)

    def body(rest_ref, halo_ref, a_ref, wa_ref, wc_ref, wp_ref, sc_ref, cw_ref, wo_ref, x_ref, o_ref):
        b = _branches(rest_ref, halo_ref, a_ref, wa_ref, wc_ref, wp_ref, sc_ref, cw_ref, pl.program_id(0), tm)
        merged = b["sg"][0] * b["y_attn"] + b["sg"][1] * b["y_conv"] + b["sg"][2] * (b["y_pool_raw"] * sc_ref[...])
        o_ref[...] = x_ref[...] + _dot(merged.astype(BF16), wo_ref[...])

    (x1,), carried = _carried_call(
        body, ex, (s // tm,),
        _mix_specs(tm, lambda i: i) + [pl.BlockSpec((D_MODEL, D_MODEL), lambda i: (0, 0)),
                                       pl.BlockSpec((tm, D_MODEL), lambda i: (i, 0))],
        [pl.BlockSpec((tm, D_MODEL), lambda i: (i, 0))], [jax.ShapeDtypeStruct((s, D_MODEL), F32)], [],
        ("arbitrary",), name, (proj, proj, a, wa, wc, wp, scale, cw, wo, x))
    return x1, carried


def mix_bwd(proj, a, dx1, wa, wc, wp, scale, cw, wo, name):
    s = dx1.shape[0]
    tm = min(256, s)
    nt = s // tm
    ti_of = lambda i: nt - 1 - i
    n = tm + HALO

    def body(rest_ref, halo_ref, a_ref, wa_ref, wc_ref, wp_ref, sc_ref, cw_ref, wo_ref,
             dx_ref, dp_ref, da_ref, at_ref, mg_ref, dya_ref, dyc_ref, dyp_ref, uc_ref, dd_ref, dsc_ref, dcw_ref,
             cdc_ref, cde_ref):
        i = pl.program_id(0)
        ti = ti_of(i)

        @pl.when(i == 0)
        def _():
            cdc_ref[...] = jnp.zeros_like(cdc_ref)
            cde_ref[...] = jnp.zeros_like(cde_ref)
            dsc_ref[...] = jnp.zeros_like(dsc_ref)
            dcw_ref[...] = jnp.zeros_like(dcw_ref)

        b = _branches(rest_ref, halo_ref, a_ref, wa_ref, wc_ref, wp_ref, sc_ref, cw_ref, ti, tm)
        sg, sc = b["sg"], sc_ref[...]
        y_pool = b["y_pool_raw"] * sc
        merged = sg[0] * b["y_attn"] + sg[1] * b["y_conv"] + sg[2] * y_pool
        mg_ref[...] = merged.astype(BF16)
        dm = _dot_nt(dx_ref[...].astype(BF16), wo_ref[...])
        for j, y in enumerate((b["y_attn"], b["y_conv"], y_pool)):
            dp_ref[:, 1024 + j * D_MODEL:1024 + (j + 1) * D_MODEL] = (dm * y * sg[j] * (1.0 - sg[j])).astype(BF16)
        dya = (dm * sg[0]).astype(BF16)
        dya_ref[...] = dya
        _, lo = _lanes()
        for j in range(HEADS // 2):
            at_ref[:, 128 * j:128 * (j + 1)] = b["a_tok"][j]
            da = _dot_nt(dya, wa_ref[128 * j:128 * (j + 1), :])
            da_ref[2 * j] = jnp.where(lo, da, 0.0).astype(BF16)
            da_ref[2 * j + 1] = jnp.where(lo, _swap_halves(da), 0.0).astype(BF16)
        dyc = (dm * sg[1]).astype(BF16)
        dyc_ref[...] = dyc
        duc = _dot_nt(dyc, wc_ref[...])
        dyp = dm * sg[2]
        dsc_ref[...] += jnp.sum(dyp * b["y_pool_raw"], axis=0, keepdims=True)
        dypr = (dyp * sc).astype(BF16)
        dyp_ref[...] = dypr
        ddp = _dot_nt(dypr, wp_ref[...])
        uc_ref[...] = b["uc"].astype(BF16)
        dd_ref[...] = b["dpool"].astype(BF16)

        dconv = duc * b["cb"]
        dp_ref[:, 256:512] = (duc * b["conv"]).astype(BF16)
        dcf = jnp.concatenate([dconv, cdc_ref[...]], axis=0)
        cw = b["cw"]
        dz = cw[2:3] * dconv + cw[1:2] * pltpu.roll(dcf, n - 1, 0)[:tm] + cw[0:1] * pltpu.roll(dcf, n - 2, 0)[:tm]
        dp_ref[:, 0:256] = (dz * b["cc"]).astype(BF16)
        dp_ref[:, 512:768] = (dz * b["cx"]).astype(BF16)
        dcw_ref[0:1, :] += jnp.sum(dconv * b["z2"], axis=0, keepdims=True)
        dcw_ref[1:2, :] += jnp.sum(dconv * b["z1"], axis=0, keepdims=True)
        dcw_ref[2:3, :] += jnp.sum(dconv * b["z"], axis=0, keepdims=True)
        cdc_ref[...] = dconv[:HALO]

        e = ddp * b["inv"]
        ef = jnp.concatenate([e, cde_ref[...]], axis=0)
        r2 = ef + pltpu.roll(ef, n - 1, 0)
        r4 = r2 + pltpu.roll(r2, n - 2, 0)
        r8 = r4 + pltpu.roll(r4, n - 4, 0)
        r16 = r8 + pltpu.roll(r8, n - 8, 0)
        dp_ref[:, 768:1024] = (_by_group(b["gid"], r2[:tm], r4[:tm], r8[:tm], r16[:tm]) - ddp).astype(BF16)
        cde_ref[...] = e[:HALO]

    tile = lambda w: pl.BlockSpec((tm, w), lambda i: (ti_of(i), 0))
    whole = lambda r, c: pl.BlockSpec((r, c), lambda i: (0, 0))
    bf = lambda w: jax.ShapeDtypeStruct((s, w), BF16)
    return pl.pallas_call(
        body, grid=(nt,),
        in_specs=_mix_specs(tm, ti_of) + [whole(D_MODEL, D_MODEL), tile(D_MODEL)],
        out_specs=[tile(N_REST), pl.BlockSpec((HEADS, tm, 128), lambda i: (0, ti_of(i), 0)), tile(D_ATTN),
                   tile(D_MODEL), tile(D_MODEL), tile(D_MODEL), tile(D_MODEL),
                   tile(D_CONV), tile(D_POOL), whole(1, D_MODEL), whole(8, D_CONV)],
        out_shape=[bf(DPROJ_COLS), jax.ShapeDtypeStruct((HEADS, s, 128), BF16), bf(D_ATTN),
                   bf(D_MODEL), bf(D_MODEL), bf(D_MODEL), bf(D_MODEL), bf(D_CONV), bf(D_POOL),
                   jax.ShapeDtypeStruct((1, D_MODEL), F32), jax.ShapeDtypeStruct((8, D_CONV), F32)],
        scratch_shapes=[pltpu.VMEM((HALO, D_CONV), F32), pltpu.VMEM((HALO, D_POOL), F32)],
        compiler_params=_cparams(("arbitrary",)), name=name)(proj, proj, a, wa, wc, wp, scale, cw, wo, dx1)


def _adamw_math(w, g, m, v):
    m = ADAM_B1 * m + (1.0 - ADAM_B1) * g
    v = ADAM_B2 * v + (1.0 - ADAM_B2) * (g * g)
    m_hat = m / (1.0 - ADAM_B1 ** ADAM_STEP)
    v_hat = v / (1.0 - ADAM_B2 ** ADAM_STEP)
    delta = -ADAM_LR * (m_hat / (jnp.sqrt(v_hat) + ADAM_EPS) + ADAM_WD * w)
    return delta, m, v


ADAMW_PARTS_BLOCK_BYTES = 4 * 2 ** 20


def _row_tile(rows, cols, copies, itemsize):
    row_bytes = copies * (-(-cols // 128) * 128) * itemsize
    fits = [t for t in range(16, rows + 1, 16) if rows % t == 0 and t * row_bytes <= ADAMW_PARTS_BLOCK_BYTES]
    return max(fits) if fits else rows


def pair_sum(blocks, stage, me, name):
    n_slots, rows, cols = stage.shape
    tr = _row_tile(rows, cols, 1, 4)

    def body(me_ref, a_ref, b_ref, o_ref):
        o_ref[...] = (a_ref[...].astype(F32) + b_ref[...].astype(F32)).astype(BF16)

    slot = pl.BlockSpec((None, tr, cols), lambda i, r, me_ref: (i, r, 0))
    return pl.pallas_call(
        body, out_shape=jax.ShapeDtypeStruct(stage.shape, BF16),
        grid_spec=pltpu.PrefetchScalarGridSpec(
            num_scalar_prefetch=1, grid=(n_slots, rows // tr),
            in_specs=[pl.BlockSpec((None, tr, cols), lambda i, r, me_ref: (me_ref[0] ^ (2 * i), r, 0)), slot],
            out_specs=slot),
        compiler_params=_cparams(("parallel", "parallel")), name=name)(me.reshape(1), blocks, stage)


def adamw_sum(parts, w, m, v, name):
    layers, rows, cols = w.shape
    n_parts = parts.shape[1]
    if rows % 16 == 0:
        tr, tc = _row_tile(rows, cols, n_parts, parts.dtype.itemsize), cols
    else:
        tr, tc = rows, _pick(cols, (256, 128))

    def body(p_ref, w_ref, m_ref, v_ref, g_ref, d_ref, nm_ref, nv_ref):
        g = p_ref[0].astype(F32)
        for i in range(1, n_parts):
            g = g + p_ref[i].astype(F32)
        g_ref[...] = g
        d_ref[...], nm_ref[...], nv_ref[...] = _adamw_math(w_ref[...], g, m_ref[...], v_ref[...])

    spec = pl.BlockSpec((None, tr, tc), lambda l, i, j: (l, i, j))
    return pl.pallas_call(
        body, grid=(layers, rows // tr, cols // tc),
        in_specs=[pl.BlockSpec((None, n_parts, tr, tc), lambda l, i, j: (l, 0, i, j)), spec, spec, spec],
        out_specs=[spec] * 4, out_shape=[jax.ShapeDtypeStruct((layers, rows, cols), F32)] * 4,
        compiler_params=_cparams(("parallel", "parallel", "parallel")), name=name)(parts, w, m, v)


def _me():
    return lax.axis_index("x"), lax.axis_index("y"), lax.axis_index("c")


N_PEERS = N_DEV - 1


def all_gather(shards, name):
    n = len(shards)
    any_spec = pl.BlockSpec(memory_space=pl.ANY)

    def body(*refs):
        x_refs, out_refs = refs[:n], refs[n:2 * n]
        send_sems, recv_sems, local_sems = refs[2 * n:]
        x, y, c = _me()
        me, sibling = (x, y, c), (x, y, 1 - c)
        chips = [(1 - x, y), (x, 1 - y), (1 - x, 1 - y)]

        def copy(t, k, block, to, from_input=False):
            slot = out_refs[t].at[4 * block[0] + 2 * block[1] + block[2]]
            return pltpu.make_async_remote_copy(
                src_ref=x_refs[t] if from_input else slot, dst_ref=slot, send_sem=send_sems.at[N_PEERS * t + k],
                recv_sem=recv_sems.at[N_PEERS * t + k], device_id=to, device_id_type=pl.DeviceIdType.MESH)

        mine = [pltpu.make_async_copy(x_refs[t], out_refs[t].at[4 * x + 2 * y + c], local_sems.at[t]) for t in range(n)]
        started = []
        for t in range(n):
            mine[t].start()
            started.append(copy(t, 0, me, sibling, from_input=True))
            started += [copy(t, 1 + j, me, (*chip, c), from_input=True) for j, chip in enumerate(chips)]
        for cp in started:
            cp.start()
        for j, chip in enumerate(chips):
            for t in range(n):
                copy(t, 1 + j, (*chip, c), me).wait_recv()
                fwd = copy(t, 4 + j, (*chip, c), sibling)
                fwd.start()
                started.append(fwd)
        for t in range(n):
            copy(t, 0, sibling, me).wait_recv()
            for j, chip in enumerate(chips):
                copy(t, 4 + j, (*chip, 1 - c), me).wait_recv()
        for cp in started:
            cp.wait_send()
        for cp in mine:
            cp.wait()

    return pl.pallas_call(
        body, out_shape=[jax.ShapeDtypeStruct((N_DEV,) + s.shape, s.dtype) for s in shards],
        in_specs=[any_spec] * n, out_specs=[any_spec] * n,
        scratch_shapes=[pltpu.SemaphoreType.DMA((N_PEERS * n,)), pltpu.SemaphoreType.DMA((N_PEERS * n,)),
                        pltpu.SemaphoreType.DMA((n,))],
        name=name)(*shards)


SIBLING = 1
OTHER_CHIPS = (2, 4, 6)
SAME_CORE = (0,) + OTHER_CHIPS


class Exchange:
    def __init__(self, inputs, out_shapes, aliases, copies, local=()):
        self.inputs, self.out_shapes, self.aliases = list(inputs), list(out_shapes), aliases
        self._copies, self._local = list(copies), list(local)
        self.scratch = [pltpu.SemaphoreType.DMA((len(self._copies),)), pltpu.SemaphoreType.DMA((len(self._copies),)),
                        pltpu.SemaphoreType.DMA((max(len(self._local), 1),))]

    def _build(self, ins, outs, sems):
        send_sems, recv_sems, local_sems = sems
        x, y, c = _me()
        me = 4 * x + 2 * y + c
        local = [functools.partial(pltpu.make_async_copy, src(ins, outs, me), dst(outs, me), local_sems.at[i])
                 for i, (src, dst) in enumerate(self._local)]
        sends, recvs = [], []
        for i, (mask, src, dst) in enumerate(self._copies):
            px, py, pc = x ^ ((mask >> 2) & 1), y ^ ((mask >> 1) & 1), c ^ (mask & 1)
            pair = dict(send_sem=send_sems.at[i], recv_sem=recv_sems.at[i], device_id_type=pl.DeviceIdType.MESH)
            sends.append(functools.partial(
                pltpu.make_async_remote_copy, src_ref=src(ins, outs, me), dst_ref=dst(outs, me), device_id=(px, py, pc), **pair))
            recvs.append(functools.partial(
                pltpu.make_async_remote_copy, src_ref=src(ins, outs, me), dst_ref=dst(outs, me ^ mask), device_id=(x, y, c), **pair))
        return local, sends, recvs

    def start(self, ins, outs, sems):
        local, sends, _ = self._build(ins, outs, sems)
        for make in local + sends:
            make().start()

    def drain(self, ins, outs, sems):
        local, sends, recvs = self._build(ins, outs, sems)
        for make in recvs:
            make().wait_recv()
        for make in sends:
            make().wait_send()
        for make in local:
            make().wait()


def _bind(fn, *args):
    return functools.partial(fn, *args)


def join_exchanges(a, b):
    if a is None or b is None:
        return a or b
    na_in, na_out = len(a.inputs), len(a.out_shapes)

    def src_a(fn):
        return lambda ins, outs, me: fn(ins[:na_in], outs[:na_out], me)

    def dst_a(fn):
        return lambda outs, who: fn(outs[:na_out], who)

    def src_b(fn):
        return lambda ins, outs, me: fn(ins[na_in:], outs[na_out:], me)

    def dst_b(fn):
        return lambda outs, who: fn(outs[na_out:], who)

    copies = [(m, src_a(s), dst_a(d)) for m, s, d in a._copies] + [(m, src_b(s), dst_b(d)) for m, s, d in b._copies]
    local = [(src_a(s), dst_a(d)) for s, d in a._local] + [(src_b(s), dst_b(d)) for s, d in b._local]
    aliases = dict(a.aliases)
    aliases.update({na_in + i: na_out + o for i, o in b.aliases.items()})
    return Exchange(a.inputs + b.inputs, a.out_shapes + b.out_shapes, aliases, copies, local)


def gather_over_ici(shards):
    copies = [(mask, _bind(lambda t, ins, outs, me: ins[t], t), _bind(lambda t, outs, sender: outs[t].at[sender], t))
              for t in range(len(shards)) for mask in OTHER_CHIPS]
    local = [(_bind(lambda t, ins, outs, me: ins[t], t), _bind(lambda t, outs, me: outs[t].at[me], t))
             for t in range(len(shards))]
    return Exchange(shards, [jax.ShapeDtypeStruct((N_DEV,) + s.shape, s.dtype) for s in shards], {}, copies, local)


def gather_over_d2d(gathered):
    copies = [(SIBLING, _bind(lambda t, m, ins, outs, me: outs[t].at[me ^ m], t, m),
               _bind(lambda t, m, outs, sender: outs[t].at[sender ^ m], t, m))
              for t in range(len(gathered)) for m in SAME_CORE]
    return Exchange(gathered, [jax.ShapeDtypeStruct(g.shape, g.dtype) for g in gathered],
                    {t: t for t in range(len(gathered))}, copies)


def scatter_over_d2d(blocks):
    copies = [(SIBLING, _bind(lambda t, m, ins, outs, me: ins[t].at[me ^ SIBLING ^ m], t, m),
               _bind(lambda t, i, outs, sender: outs[t].at[i], t, i))
              for t in range(len(blocks)) for i, m in enumerate(SAME_CORE)]
    return Exchange(blocks, [jax.ShapeDtypeStruct((len(SAME_CORE),) + b.shape[1:], b.dtype) for b in blocks], {}, copies)


def scatter_over_ici(pair_sums, bufs, layer):
    n = len(pair_sums)
    copies = [(m, _bind(lambda t, i, ins, outs, me: ins[t].at[i], t, i),
               _bind(lambda t, i, outs, sender: outs[t].at[layer, i], t, i))
              for t in range(n) for i, m in enumerate(SAME_CORE) if m]
    local = [(_bind(lambda t, ins, outs, me: ins[t].at[0], t), _bind(lambda t, outs, me: outs[t].at[layer, 0], t))
             for t in range(n)]
    return Exchange(list(pair_sums) + list(bufs), [jax.ShapeDtypeStruct(b.shape, b.dtype) for b in bufs],
                    {n + t: t for t in range(n)}, copies, local)


def run_exchange(ex, name):
    any_spec = pl.BlockSpec(memory_space=pl.ANY)
    n_in, n_out = len(ex.inputs), len(ex.out_shapes)

    def body(*refs):
        ins, outs, sems = refs[:n_in], refs[n_in:n_in + n_out], refs[n_in + n_out:]
        ex.start(ins, outs, sems)
        ex.drain(ins, outs, sems)

    return pl.pallas_call(
        body, out_shape=ex.out_shapes, in_specs=[any_spec] * n_in, out_specs=[any_spec] * n_out,
        input_output_aliases=ex.aliases, scratch_shapes=ex.scratch, name=name)(*ex.inputs)


MATRICES = ("w_in", "w_attn_out", "w_conv_out", "pool_w", "w_o", "w_ffn_in", "w_ffn_out")
TRANSPOSED = ("w_in", "w_ffn_in")
EVERY = tuple(range(len(MATRICES)))
IN_PROJ_PART, ATTN_PART, MIX_PART = (0,), (1, 2, 3, 4, 5), (6,)
LATE = (0,)
EARLY = EVERY[1:]
EARLY_FIRST, EARLY_SECOND = (4, 6), (1, 2, 3, 5)
SHARD_INFO = {
    "w_in": ((DEPTH, D_IN // N_DEV, D_MODEL), 1),
    "w_attn_out": ((DEPTH, D_ATTN, D_MODEL // N_DEV), 2),
    "w_conv_out": ((DEPTH, D_CONV, D_MODEL // N_DEV), 2),
    "pool_w": ((DEPTH, 4, 64, 256 // N_DEV), 3),
    "w_o": ((DEPTH, D_MODEL // N_DEV, D_MODEL), 1),
    "w_ffn_in": ((DEPTH, 2 * D_FF // N_DEV, D_MODEL), 1),
    "w_ffn_out": ((DEPTH, D_FF // N_DEV, D_MODEL), 1),
}


def _handled(name, t):
    return jnp.transpose(t, (0, 2, 1)) if name in TRANSPOSED else t
VECTORS = ("norm_mix_g", "forget_b", "q_norm_g", "k_norm_g", "pool_scale", "norm_ffn_g")
VECTOR_SHAPES = {"norm_mix_g": (DEPTH, D_MODEL), "forget_b": (DEPTH, HEADS), "q_norm_g": (DEPTH, HEAD_DIM),
                 "k_norm_g": (DEPTH, HEAD_DIM), "pool_scale": (DEPTH, D_MODEL), "norm_ffn_g": (DEPTH, D_MODEL)}
CONV_W_FULL = (DEPTH, 3, D_CONV)


def _size(shape):
    n = 1
    for v in shape:
        n *= v
    return n


def _pack(arrays, rows, cols):
    flat = jnp.concatenate([a.reshape(-1) for a in arrays])
    return jnp.pad(flat, (0, rows * cols - flat.shape[0])).reshape(rows, cols)


def _unpack(packed, shapes):
    flat, out, off = packed.reshape(-1), [], 0
    for shp in shapes:
        out.append(flat[off:off + _size(shp)].reshape(shp))
        off += _size(shp)
    return out


def _join_shards(stacked, axis):
    moved = jnp.moveaxis(stacked, 0, axis)
    shp = list(moved.shape)
    shp[axis:axis + 2] = [shp[axis] * shp[axis + 1]]
    return moved.reshape(shp)


def _cut_shards(full, axis):
    shp = list(full.shape)
    shp[axis:axis + 1] = [N_DEV, shp[axis] // N_DEV]
    return jnp.moveaxis(full.reshape(shp), axis, 0)


N_MOVED = 1544
SHARD_ROWS = D_IN // N_DEV


def _regroup_w_in(shards):
    wt = shards.reshape(D_IN, shards.shape[2])
    pad = jnp.zeros((N_FULL - D_IN, wt.shape[1]), wt.dtype)
    return jnp.concatenate([wt[N_MOVED:], wt[:N_MOVED], pad], axis=0)


def _ungroup_w_in(wpt):
    def kernel_rows(a, b):
        if b <= N_MOVED:
            return [wpt[a + D_IN - N_MOVED:b + D_IN - N_MOVED]]
        if a >= N_MOVED:
            return [wpt[a - N_MOVED:b - N_MOVED]]
        return kernel_rows(a, N_MOVED) + kernel_rows(N_MOVED, b)

    return jnp.stack([jnp.concatenate(kernel_rows(s * SHARD_ROWS, (s + 1) * SHARD_ROWS), axis=0) for s in range(N_DEV)])


def _pool_block_diag(w):
    out = jnp.zeros((D_POOL, D_MODEL), w.dtype)
    for g in range(4):
        out = lax.dynamic_update_slice(out, w[g], (g * 64, g * 256))
    return out


def _pool_from_block_diag(wbd):
    return jnp.stack([wbd[g * 64:(g + 1) * 64, g * 256:(g + 1) * 256] for g in range(4)])


def _layer_weights(mats, vec, conv_w, l):
    wp = _pool_block_diag(mats["pool_w"])
    row = lambda v: v.reshape(1, -1)
    fb = jnp.zeros((1, 128), F32).at[0, :HEADS].set(vec["forget_b"][l])
    cw = jnp.zeros((8, D_CONV), F32).at[:3].set(conv_w[l])
    twice = lambda v: jnp.tile(v.reshape(1, -1), (1, 2))
    return dict(
        wt_in=_regroup_w_in(mats["w_in"]), wt_ffn_in=mats["w_ffn_in"], w_ffn_out=mats["w_ffn_out"],
        wa=mats["w_attn_out"], wc=mats["w_conv_out"], wp=wp, wo=mats["w_o"],
        g_mix=row(vec["norm_mix_g"][l]), g_ffn=row(vec["norm_ffn_g"][l]), gq2=twice(vec["q_norm_g"][l]),
        gk2=twice(vec["k_norm_g"][l]), scale=row(vec["pool_scale"][l]), fb=fb, cw=cw)


def _layer_fwd(x, w, l, comm):
    (proj, h), half_a = norm_matmul(x, w["g_mix"], w["wt_in"], N_MAIN, f"in_proj_{l}", comm.gather_ici(l + 1, IN_PROJ_PART))
    z, c = forget_fwd(h, w["wt_in"], w["fb"], f"forget_fwd_{l}")
    qa, ka, va, vt = attn_prep(proj, c, w["gq2"], w["gk2"], f"attn_prep_{l}")
    (oa, lse), half_b = attn_forward(qa, ka, vt, f"attn_fwd_{l}", comm.gather_ici(l + 1, ATTN_PART))
    x1, half_c = mix_fwd(proj, oa, x, w["wa"], w["wc"], w["wp"], w["scale"], w["cw"], w["wo"], f"mix_fwd_{l}",
                         comm.gather_ici(l + 1, MIX_PART))
    half = list(half_a) + list(half_b) + list(half_c)
    (gu, h2), gathered = norm_matmul(x1, w["g_ffn"], w["wt_ffn_in"], 2 * D_FF, f"ffn_in_{l}", comm.gather_d2d(l + 1, half))
    x2 = swiglu_matmul(gu, w["w_ffn_out"], x1, f"ffn_out_{l}")
    saved = dict(x=x, proj=proj, h=h, z=z, qa=qa, ka=ka, va=va, oa=oa, lse=lse, x1=x1, gu=gu, h2=h2)
    return x2, saved, gathered


def _layer_bwd(dx2, sv, w, l, comm):
    g = {}
    (dgu, act), stage = swiglu_bwd(dx2, sv["gu"], w["w_ffn_out"], f"ffn_out_bwd_{l}", comm.scatter_d2d(l + 1))
    sums = comm.pair_sums(l + 1, stage)
    g["w_ffn_out"] = tn_matmul(act, dx2, f"dw_ffn_out_{l}")
    g["w_ffn_in"] = tn_matmul(dgu, sv["h2"], f"dw_ffn_in_{l}")
    (dx1, dg), _ = matmul_normbwd(dgu, w["wt_ffn_in"], sv["x1"], w["g_ffn"], dx2, f"ffn_in_bwd_{l}")
    g["norm_ffn_g"] = dg[0]

    (dproj, doa, a_tok, merged, dya, dyc, dyp, uc, dd, dscale, dcw) = mix_bwd(
        sv["proj"], sv["oa"], dx1, w["wa"], w["wc"], w["wp"], w["scale"], w["cw"], w["wo"], f"mix_bwd_{l}")
    g["w_o"] = tn_matmul(merged, dx1, f"dw_o_{l}")
    g["w_attn_out"] = tn_matmul(a_tok, dya, f"dw_attn_out_{l}")
    g["w_conv_out"] = tn_matmul(uc, dyc, f"dw_conv_out_{l}")
    g["pool_w"] = _pool_from_block_diag(tn_matmul(dd, dyp, f"dw_pool_{l}"))
    g["pool_scale"] = dscale[0]
    g["conv_w"] = dcw[:3]

    early = comm.early(l)
    comm.grads(l, g)
    above = comm.scatter_ici(l + 1, sums)
    (dqa, dka, dva), got = attn_backward(sv["qa"], sv["ka"], sv["va"], sv["oa"], doa, sv["lse"], f"attn_bwd_{l}",
                                         join_exchanges(above, comm.scatter_d2d(l, early) if early else None))
    n_above = len(above.out_shapes) if above else 0
    comm.scattered(got[:n_above])
    early_sums = dict(zip(early, comm.pair_sums(l, got[n_above:], early))) if early else {}
    early_ici = lambda which: comm.scatter_ici(l, [early_sums[t] for t in which], which) if early else None
    dproj, dc, dgq, dgk = attn_post(dqa, dka, dva, sv["proj"], w["gq2"], w["gk2"], dproj, f"attn_post_{l}")
    g["q_norm_g"] = dgq[0, :HEAD_DIM] + dgq[0, HEAD_DIM:]
    g["k_norm_g"] = dgk[0, :HEAD_DIM] + dgk[0, HEAD_DIM:]
    dproj, db = forget_bwd(dc, sv["z"], dproj, f"forget_bwd_{l}")
    g["forget_b"] = db[0, :HEADS]

    dw_in = tn_matmul(dproj, sv["h"], f"dw_in_{l}", m_cols=N_FULL, ex=early_ici(EARLY_FIRST))
    if early:
        dw_in, got = dw_in
        comm.scattered(got, EARLY_FIRST)
    g["w_in"] = _ungroup_w_in(dw_in)
    (dx, dg), got = matmul_normbwd(dproj, w["wt_in"], sv["x"], w["g_mix"], dx1, f"in_proj_bwd_{l}", k=N_FULL,
                                   ex=early_ici(EARLY_SECOND))
    comm.scattered(got, EARLY_SECOND if early else None)
    g["norm_mix_g"] = dg[0]
    comm.grads(l, g)
    return dx


def _local_step(x, tgt, comm):
    ws, saved = [], []
    w = comm.weights(0, None)
    for l in range(DEPTH):
        ws.append(w)
        x, sv, gathered = _layer_fwd(x, w, l, comm)
        saved.append(sv)
        if l + 1 < DEPTH:
            w = comm.weights(l + 1, gathered)
    sq, dx = loss_kernel(x, tgt, "loss")
    for l in reversed(range(DEPTH)):
        dx = _layer_bwd(dx, saved[l], ws[l], l, comm)
    comm.finish()
    return sq[0, 0], dx


def kernel(x, norm_mix_g, w_in, forget_b, q_norm_g, k_norm_g, w_attn_out, conv_w, w_conv_out, pool_w, pool_scale, w_o, norm_ffn_g, w_ffn_in, w_ffn_out, loss_target, m_norm_mix_g, m_w_in, m_forget_b, m_q_norm_g, m_k_norm_g, m_w_attn_out, m_conv_w, m_w_conv_out, m_pool_w, m_pool_scale, m_w_o, m_norm_ffn_g, m_w_ffn_in, m_w_ffn_out, v_norm_mix_g, v_w_in, v_forget_b, v_q_norm_g, v_k_norm_g, v_w_attn_out, v_conv_w, v_w_conv_out, v_pool_w, v_pool_scale, v_w_o, v_norm_ffn_g, v_w_ffn_in, v_w_ffn_out):
    w = dict(norm_mix_g=norm_mix_g, w_in=w_in, forget_b=forget_b, q_norm_g=q_norm_g, k_norm_g=k_norm_g,
             w_attn_out=w_attn_out, conv_w=conv_w, w_conv_out=w_conv_out, pool_w=pool_w, pool_scale=pool_scale,
             w_o=w_o, norm_ffn_g=norm_ffn_g, w_ffn_in=w_ffn_in, w_ffn_out=w_ffn_out)
    m = dict(norm_mix_g=m_norm_mix_g, w_in=m_w_in, forget_b=m_forget_b, q_norm_g=m_q_norm_g, k_norm_g=m_k_norm_g,
             w_attn_out=m_w_attn_out, conv_w=m_conv_w, w_conv_out=m_w_conv_out, pool_w=m_pool_w,
             pool_scale=m_pool_scale, w_o=m_w_o, norm_ffn_g=m_norm_ffn_g, w_ffn_in=m_w_ffn_in, w_ffn_out=m_w_ffn_out)
    v = dict(norm_mix_g=v_norm_mix_g, w_in=v_w_in, forget_b=v_forget_b, q_norm_g=v_q_norm_g, k_norm_g=v_k_norm_g,
             w_attn_out=v_w_attn_out, conv_w=v_conv_w, w_conv_out=v_w_conv_out, pool_w=v_pool_w,
             pool_scale=v_pool_scale, w_o=v_w_o, norm_ffn_g=v_norm_ffn_g, w_ffn_in=v_w_ffn_in, w_ffn_out=v_w_ffn_out)
    me = 4 * lax.axis_index("x") + 2 * lax.axis_index("y") + lax.axis_index("c")
    layer_shard = {n: SHARD_INFO[n][0][1:] for n in MATRICES}
    cut_axis = {n: SHARD_INFO[n][1] - 1 for n in MATRICES}

    conv_g = all_gather([_pack([conv_w], 8, 128)], "gather_conv_w")[0]
    conv_full = _join_shards(jnp.stack([_unpack(conv_g[i], [conv_w.shape])[0] for i in range(N_DEV)]), 2)
    vec = {n: w[n] for n in VECTORS}

    rc = {n: (_size(layer_shard[n][:-1]), layer_shard[n][-1]) for n in MATRICES}

    class Comm:
        bufs = [lax.empty((DEPTH, len(SAME_CORE)) + layer_shard[n], BF16) for n in MATRICES]
        blocks = [None] * DEPTH
        small_g = [None] * DEPTH

        @staticmethod
        def shards(l):
            return [_handled(n, w[n])[l].astype(BF16) for n in MATRICES]

        @staticmethod
        def gather_ici(l, part):
            return gather_over_ici([Comm.shards(l)[t] for t in part]) if l < DEPTH else None

        @staticmethod
        def gather_d2d(l, half):
            return gather_over_d2d(half) if l < DEPTH else None

        @staticmethod
        def weights(l, gathered):
            if l == 0:
                gathered = all_gather(Comm.shards(0), "gather_0")
            mats = {n: t if n == "w_in" else _join_shards(t, cut_axis[n]) for n, t in zip(MATRICES, gathered)}
            return _layer_weights(mats, vec, conv_full, l)

        @staticmethod
        def grads(l, g):
            Comm.small_g[l] = g
            Comm.blocks[l] = [None if n not in g else g[n] if n == "w_in" else _cut_shards(g[n], cut_axis[n])
                              for n in MATRICES]

        @staticmethod
        def early(l):
            return EARLY if l == 0 else None

        @staticmethod
        def scatter_d2d(l, which=EVERY):
            return scatter_over_d2d([Comm.blocks[l][t] for t in which]) if l < DEPTH else None

        @staticmethod
        def pair_sums(l, stage, which=EVERY):
            if l >= DEPTH:
                return None
            return [pair_sum(Comm.blocks[l][t].reshape((N_DEV,) + rc[MATRICES[t]]),
                             s.reshape((len(SAME_CORE),) + rc[MATRICES[t]]), me,
                             f"pair_sum_{MATRICES[t]}_{l}").reshape(s.shape) for t, s in zip(which, stage)]

        @staticmethod
        def scatter_ici(l, sums, which=EVERY):
            return scatter_over_ici(sums, [Comm.bufs[t] for t in which], l) if l < DEPTH else None

        @staticmethod
        def scattered(results, which=EVERY):
            for t, r in zip(which or (), results):
                Comm.bufs[t] = r

        @staticmethod
        def finish():
            stage = run_exchange(Comm.scatter_d2d(0, LATE), "scatter_d2d_0")
            Comm.scattered(run_exchange(Comm.scatter_ici(0, Comm.pair_sums(0, stage, LATE), LATE), "scatter_ici_0"), LATE)

    small_g, received = Comm.small_g, Comm
    sq, dx = _local_step(x[0], loss_target[0], Comm)
    loss = lax.psum(0.5 * sq / D_MODEL, ("x", "y", "c"))

    big = {}
    for n, parts in zip(MATRICES, received.bufs):
        outs = adamw_sum(parts.reshape((DEPTH, len(SAME_CORE)) + rc[n]),
                         *[_handled(n, d[n]).reshape((DEPTH,) + rc[n]) for d in (w, m, v)], f"adamw_{n}")
        big[n] = [_handled(n, t.reshape((DEPTH,) + layer_shard[n])) for t in outs]

    small_shapes = [VECTOR_SHAPES[n] for n in VECTORS] + [CONV_W_FULL]
    stacked = [jnp.stack([small_g[l][n] for l in range(DEPTH)]) for n in VECTORS + ("conv_w",)]
    sparts = all_gather([_pack(stacked, SMALL_ROWS, 128)], "gather_vector_grads")[0]
    col0 = me * (D_CONV // N_DEV)
    place = lambda t: lax.dynamic_update_slice(jnp.zeros(CONV_W_FULL, F32), t, (0, 0, col0))
    spacked = [_pack([d[n] for n in VECTORS] + [place(d["conv_w"])], SMALL_ROWS, 128)[None] for d in (w, m, v)]
    small = [_unpack(t[0], small_shapes) for t in adamw_sum(sparts[None], *spacked, "adamw_vectors")]

    def result(kind):
        out = {n: big[n][kind] for n in MATRICES}
        out.update({n: small[kind][j] for j, n in enumerate(VECTORS)})
        out["conv_w"] = lax.dynamic_slice(small[kind][len(VECTORS)], (0, 0, col0), conv_w.shape)
        return [out[n] for n in w]

    return (loss, dx[None], *result(0), *result(1), *result(2), *result(3))
```

```python
import functools

import jax
import jax.numpy as jnp
from jax import lax
from jax.experimental import pallas as pl
from jax.experimental.pallas import tpu as pltpu

F32 = jnp.float32
BF16 = jnp.bfloat16

N_DEV = 8
DEPTH = 4
D_MODEL = 1024
HEAD_DIM = 64
HEADS = 8
D_ATTN = 512
D_CONV = 256
D_POOL = 256
D_FF = 2816
D_IN = 5640
EPS = 1e-6
ATTN_SCALE = HEAD_DIM ** -0.5

N_REST = 4096
N_MAIN = 5632
N_FULL = 5760
DPROJ_TAIL = 2048
DPROJ_COLS = N_REST + DPROJ_TAIL
FF_BLK = 256
N_FF_BLKS = D_FF // FF_BLK
HALO = 16

ADAM_LR = 0.001
ADAM_B1 = 0.9
ADAM_B2 = 0.999
ADAM_EPS = 1e-08
ADAM_WD = 0.01
ADAM_STEP = 10

SMALL_ROWS = 128

VMEM_LIMIT = 48 * 2 ** 20


def _cparams(sem, vmem=None):
    return pltpu.CompilerParams(dimension_semantics=sem, vmem_limit_bytes=vmem or VMEM_LIMIT)


def _pick(n, cands):
    for c in cands:
        if n % c == 0:
            return c
    raise ValueError(f"no tile for {n}")


def _sigmoid(v):
    return 1.0 / (1.0 + jnp.exp(-v))


def _rstd(v):
    return lax.rsqrt(jnp.mean(v * v, axis=-1, keepdims=True) + EPS)


def _dot(a, b):
    return jnp.dot(a, b, preferred_element_type=F32)


def _dot_tn(a, b):
    return lax.dot_general(a, b, (((0,), (0,)), ((), ())), preferred_element_type=F32)


def _dot_nt(a, b):
    return lax.dot_general(a, b, (((1,), (1,)), ((), ())), preferred_element_type=F32)


def norm_matmul(x, g, wt, n_cols, name, ex=None):
    s, d = x.shape
    tm, tn = min(1024, s), _pick(n_cols, (2816, 1408, 512))

    def body(x_ref, g_ref, w_ref, o_ref, h_ref):
        @pl.when(pl.program_id(1) == 0)
        def _():
            xv = x_ref[...]
            h_ref[...] = (xv * _rstd(xv) * g_ref[...]).astype(BF16)

        o_ref[...] = _dot_nt(h_ref[...], w_ref[...]).astype(BF16)

    return _carried_call(
        body, ex, (s // tm, n_cols // tn),
        [pl.BlockSpec((tm, d), lambda i, j: (i, 0)), pl.BlockSpec((1, d), lambda i, j: (0, 0)),
         pl.BlockSpec((tn, d), lambda i, j: (j, 0))],
        [pl.BlockSpec((tm, tn), lambda i, j: (i, j)), pl.BlockSpec((tm, d), lambda i, j: (i, 0))],
        [jax.ShapeDtypeStruct((s, n_cols), BF16), jax.ShapeDtypeStruct((s, d), BF16)], [],
        ("arbitrary", "arbitrary"), name, (x, g, wt))


def tn_matmul(a, b, name, m_cols=None, ex=None):
    t = a.shape[0]
    m = m_cols or a.shape[1]
    n = b.shape[1]
    tk = min(1024, t)
    tmm = _pick(m, (1408, 1152, 1024, 512, 256))
    tn = _pick(n, (1408, 1152, 1024, 512, 128))
    nk = t // tk

    def body(a_ref, b_ref, o_ref, acc_ref):
        @pl.when(pl.program_id(2) == 0)
        def _():
            acc_ref[...] = jnp.zeros_like(acc_ref)

        acc_ref[...] += _dot_tn(a_ref[...].astype(BF16), b_ref[...].astype(BF16))

        @pl.when(pl.program_id(2) == nk - 1)
        def _():
            o_ref[...] = acc_ref[...].astype(BF16)

    if ex is None:
        return pl.pallas_call(
            body, grid=(m // tmm, n // tn, nk),
            in_specs=[pl.BlockSpec((tk, tmm), lambda i, j, k: (k, i)), pl.BlockSpec((tk, tn), lambda i, j, k: (k, j))],
            out_specs=pl.BlockSpec((tmm, tn), lambda i, j, k: (i, j)),
            out_shape=jax.ShapeDtypeStruct((m, n), BF16), scratch_shapes=[pltpu.VMEM((tmm, tn), F32)],
            compiler_params=_cparams(("parallel", "parallel", "arbitrary")), name=name)(a, b)
    (out,), carried = _carried_call(
        body, ex, (m // tmm, n // tn, nk),
        [pl.BlockSpec((tk, tmm), lambda i, j, k: (k, i)), pl.BlockSpec((tk, tn), lambda i, j, k: (k, j))],
        [pl.BlockSpec((tmm, tn), lambda i, j, k: (i, j))], [jax.ShapeDtypeStruct((m, n), BF16)],
        [pltpu.VMEM((tmm, tn), F32)], ("arbitrary", "arbitrary", "arbitrary"), name, (a, b))
    return out, carried


def tn_matmuls(pairs, name):
    t = pairs[0][0].shape[0]
    tk = min(1024, t)
    nk = t // tk
    n = len(pairs)

    def body(*refs):
        ins, outs, accs = refs[:2 * n], refs[2 * n:3 * n], refs[3 * n:]

        @pl.when(pl.program_id(0) == 0)
        def _():
            for acc in accs:
                acc[...] = jnp.zeros_like(acc)

        for i in range(n):
            accs[i][...] += _dot_tn(ins[2 * i][...], ins[2 * i + 1][...])

        @pl.when(pl.program_id(0) == nk - 1)
        def _():
            for out, acc in zip(outs, accs):
                out[...] = acc[...].astype(BF16)

    shapes = [(a.shape[1], b.shape[1]) for a, b in pairs]
    return pl.pallas_call(
        body, grid=(nk,),
        in_specs=[pl.BlockSpec((tk, t_.shape[1]), lambda k: (k, 0)) for pair in pairs for t_ in pair],
        out_specs=[pl.BlockSpec(shp, lambda k: (0, 0)) for shp in shapes],
        out_shape=[jax.ShapeDtypeStruct(shp, BF16) for shp in shapes],
        scratch_shapes=[pltpu.VMEM(shp, F32) for shp in shapes],
        compiler_params=_cparams(("arbitrary",)), name=name)(*[t_ for pair in pairs for t_ in pair])


def matmul_normbwd(a, wt, x, g, dres, name, k=None, ex=None):
    s = a.shape[0]
    k = k or a.shape[1]
    d = wt.shape[1]
    tm = min(1024, s)
    tk = _pick(k, (1408, 1152, 512))
    nk = k // tk

    def body(a_ref, w_ref, x_ref, g_ref, r_ref, dx_ref, dg_ref, acc_ref):
        i, kk = pl.program_id(0), pl.program_id(1)

        @pl.when(kk == 0)
        def _():
            acc_ref[...] = jnp.zeros_like(acc_ref)

        @pl.when((i == 0) & (kk == 0))
        def _():
            dg_ref[...] = jnp.zeros_like(dg_ref)

        acc_ref[...] += _dot(a_ref[...], w_ref[...])

        @pl.when(kk == nk - 1)
        def _():
            xv = x_ref[...]
            r = _rstd(xv)
            y = xv * r
            dh = acc_ref[...]
            dy = dh * g_ref[...]
            dx_ref[...] = r_ref[...] + r * (dy - y * jnp.mean(dy * y, axis=-1, keepdims=True))
            dg_ref[...] += jnp.sum(dh * y, axis=0, keepdims=True)

    return _carried_call(
        body, ex, (s // tm, nk),
        [pl.BlockSpec((tm, tk), lambda i, kk: (i, kk)), pl.BlockSpec((tk, d), lambda i, kk: (kk, 0)),
         pl.BlockSpec((tm, d), lambda i, kk: (i, 0)), pl.BlockSpec((1, d), lambda i, kk: (0, 0)),
         pl.BlockSpec((tm, d), lambda i, kk: (i, 0))],
        [pl.BlockSpec((tm, d), lambda i, kk: (i, 0)), pl.BlockSpec((1, d), lambda i, kk: (0, 0))],
        [jax.ShapeDtypeStruct((s, d), F32), jax.ShapeDtypeStruct((1, d), F32)],
        [pltpu.VMEM((tm, d), F32)], ("arbitrary", "arbitrary"), name, (a, wt, x, g, dres), vmem=56 * 2 ** 20)


def swiglu_matmul(gu, w, x1, name):
    s = gu.shape[0]
    d = w.shape[1]
    tm = min(512, s)

    def body(gu_ref, w_ref, x_ref, o_ref):
        acc = x_ref[...]
        for j in range(N_FF_BLKS):
            gt = gu_ref[:, j * FF_BLK:(j + 1) * FF_BLK].astype(F32)
            up = gu_ref[:, D_FF + j * FF_BLK:D_FF + (j + 1) * FF_BLK].astype(F32)
            act = (gt * _sigmoid(gt) * up).astype(BF16)
            acc += _dot(act, w_ref[j * FF_BLK:(j + 1) * FF_BLK, :])
        o_ref[...] = acc

    return pl.pallas_call(
        body, grid=(s // tm,),
        in_specs=[pl.BlockSpec((tm, 2 * D_FF), lambda i: (i, 0)), pl.BlockSpec((D_FF, d), lambda i: (0, 0)),
                  pl.BlockSpec((tm, d), lambda i: (i, 0))],
        out_specs=pl.BlockSpec((tm, d), lambda i: (i, 0)),
        out_shape=jax.ShapeDtypeStruct((s, d), F32),
        compiler_params=_cparams(("parallel",)), name=name)(gu, w, x1)


def swiglu_bwd(dx2, gu, w, name, ex=None):
    s, d = dx2.shape
    tm = min(512, s)

    def body(dx_ref, gu_ref, w_ref, dgu_ref, act_ref):
        dx = dx_ref[...].astype(BF16)
        for j in range(N_FF_BLKS):
            g_cols = slice(j * FF_BLK, (j + 1) * FF_BLK)
            u_cols = slice(D_FF + j * FF_BLK, D_FF + (j + 1) * FF_BLK)
            dact = _dot_nt(dx, w_ref[j * FF_BLK:(j + 1) * FF_BLK, :])
            gt = gu_ref[:, g_cols].astype(F32)
            up = gu_ref[:, u_cols].astype(F32)
            sg = _sigmoid(gt)
            act_ref[:, j * FF_BLK:(j + 1) * FF_BLK] = (gt * sg * up).astype(BF16)
            dgu_ref[:, g_cols] = (dact * up * (sg * (1.0 + gt * (1.0 - sg)))).astype(BF16)
            dgu_ref[:, u_cols] = (dact * gt * sg).astype(BF16)

    return _carried_call(
        body, ex, (s // tm,),
        [pl.BlockSpec((tm, d), lambda i: (i, 0)), pl.BlockSpec((tm, 2 * D_FF), lambda i: (i, 0)),
         pl.BlockSpec((D_FF, d), lambda i: (0, 0), pipeline_mode=pl.Buffered(1))],
        [pl.BlockSpec((tm, 2 * D_FF), lambda i: (i, 0)), pl.BlockSpec((tm, D_FF), lambda i: (i, 0))],
        [jax.ShapeDtypeStruct((s, 2 * D_FF), BF16), jax.ShapeDtypeStruct((s, D_FF), BF16)], [],
        ("arbitrary",), name, (dx2, gu, w), vmem=56 * 2 ** 20)


def loss_kernel(y, tgt, name):
    s, d = y.shape
    tm = min(512, s)

    def body(y_ref, t_ref, l_ref, dy_ref):
        @pl.when(pl.program_id(0) == 0)
        def _():
            l_ref[...] = jnp.zeros_like(l_ref)

        err = y_ref[...] - t_ref[...]
        dy_ref[...] = err * (1.0 / d)
        l_ref[...] += jnp.sum(jnp.sum(err * err, axis=1, keepdims=True), axis=0, keepdims=True)

    return pl.pallas_call(
        body, grid=(s // tm,),
        in_specs=[pl.BlockSpec((tm, d), lambda i: (i, 0)), pl.BlockSpec((tm, d), lambda i: (i, 0))],
        out_specs=[pl.BlockSpec((8, 128), lambda i: (0, 0)), pl.BlockSpec((tm, d), lambda i: (i, 0))],
        out_shape=[jax.ShapeDtypeStruct((8, 128), F32), jax.ShapeDtypeStruct((s, d), F32)],
        compiler_params=_cparams(("arbitrary",)), name=name)(y, tgt)


def _split3(v):
    a1 = v.astype(BF16)
    r1 = v - a1.astype(F32)
    a2 = r1.astype(BF16)
    a3 = (r1 - a2.astype(F32)).astype(BF16)
    return a1, a2, a3


def forget_fwd(h, wt_in, b, name):
    s, d = h.shape
    tm = min(512, s)

    def body(h_ref, w_ref, b_ref, z_ref, c_ref, carry_ref):
        @pl.when(pl.program_id(0) == 0)
        def _():
            carry_ref[...] = jnp.zeros_like(carry_ref)

        z = _dot_nt(h_ref[...], w_ref[...]) + b_ref[...]
        z_ref[...] = z
        logf = jnp.minimum(z, 0.0) - jnp.log(1.0 + jnp.exp(-jnp.abs(z)))
        row = lax.broadcasted_iota(jnp.int32, (tm, tm), 0)
        col = lax.broadcasted_iota(jnp.int32, (tm, tm), 1)
        tri = (row >= col).astype(BF16)
        a1, a2, a3 = _split3(logf)
        c = _dot(tri, a1) + _dot(tri, a2) + _dot(tri, a3) + carry_ref[...]
        c_ref[...] = c
        carry_ref[...] = c[tm - 1:tm, :]

    return pl.pallas_call(
        body, grid=(s // tm,),
        in_specs=[pl.BlockSpec((tm, d), lambda i: (i, 0)), pl.BlockSpec((128, d), lambda i: (N_MAIN // 128, 0)),
                  pl.BlockSpec((1, 128), lambda i: (0, 0))],
        out_specs=[pl.BlockSpec((tm, 128), lambda i: (i, 0)), pl.BlockSpec((tm, 128), lambda i: (i, 0))],
        out_shape=[jax.ShapeDtypeStruct((s, 128), F32), jax.ShapeDtypeStruct((s, 128), F32)],
        scratch_shapes=[pltpu.VMEM((1, 128), F32)],
        compiler_params=_cparams(("arbitrary",)), name=name)(h, wt_in, b)


def forget_bwd(dc, z, dproj, name):
    s = dc.shape[0]
    tm = min(512, s)
    nt = s // tm

    def body(dc_ref, z_ref, dp_ref, dz_ref, db_ref, carry_ref):
        @pl.when(pl.program_id(0) == 0)
        def _():
            carry_ref[...] = jnp.zeros_like(carry_ref)
            db_ref[...] = jnp.zeros_like(db_ref)

        row = lax.broadcasted_iota(jnp.int32, (tm, tm), 0)
        col = lax.broadcasted_iota(jnp.int32, (tm, tm), 1)
        tri = (col >= row).astype(BF16)
        a1, a2, a3 = _split3(dc_ref[...])
        dlogf = _dot(tri, a1) + _dot(tri, a2) + _dot(tri, a3) + carry_ref[...]
        carry_ref[...] = dlogf[0:1, :]
        dz = dlogf * (1.0 - _sigmoid(z_ref[...]))
        dz_ref[...] = dz.astype(BF16)
        db_ref[...] += jnp.sum(dz, axis=0, keepdims=True)

    return pl.pallas_call(
        body, grid=(nt,),
        in_specs=[pl.BlockSpec((tm, 128), lambda i: (nt - 1 - i, 0)), pl.BlockSpec((tm, 128), lambda i: (nt - 1 - i, 0)),
                  pl.BlockSpec(memory_space=pl.ANY)],
        out_specs=[pl.BlockSpec((tm, 128), lambda i: (nt - 1 - i, N_MAIN // 128)), pl.BlockSpec((1, 128), lambda i: (0, 0))],
        out_shape=[jax.ShapeDtypeStruct(dproj.shape, BF16), jax.ShapeDtypeStruct((1, 128), F32)],
        scratch_shapes=[pltpu.VMEM((1, 128), F32)], input_output_aliases={2: 0},
        compiler_params=_cparams(("arbitrary",)), name=name)(dc, z, dproj)


HEAD_GROUP = 4
LANE_C = 64
LANE_ONE = 67


def _lanes():
    lane = lax.broadcasted_iota(jnp.int32, (1, 128), 1)
    return lane, lane < HEAD_DIM


def _half_mean(t, lo):
    s_lo = jnp.sum(jnp.where(lo, t, 0.0), axis=-1, keepdims=True)
    s_hi = jnp.sum(jnp.where(lo, 0.0, t), axis=-1, keepdims=True)
    return jnp.where(lo, s_lo, s_hi) * (1.0 / HEAD_DIM)


def _lane_col(t, lane, idx):
    return jnp.sum(jnp.where(lane == idx, t, 0.0), axis=-1, keepdims=True)


def _swap_halves(t):
    return pltpu.roll(t, HEAD_DIM, 1)


def attn_prep(proj, c, gq2, gk2, name):
    s = proj.shape[0]
    tm = min(512, s)
    first = N_REST // 128

    def body(q_ref, k_ref, v_ref, c_ref, gq_ref, gk_ref, qa_ref, ka_ref, va_ref, vt_ref):
        j = pl.program_id(1)
        lane, lo = _lanes()

        def normed(ref, g):
            t = ref[...].astype(F32)
            return t * lax.rsqrt(_half_mean(t * t, lo) + EPS) * g

        qn = normed(q_ref, gq_ref[...] * ATTN_SCALE)
        kn = normed(k_ref, gk_ref[...])
        vv = v_ref[...].astype(F32)
        cv = c_ref[...]
        one_q = jnp.where((lane >= LANE_ONE) & (lane < LANE_ONE + 3), 1.0, 0.0)
        one_k = jnp.where((lane >= LANE_C) & (lane < LANE_C + 3), 1.0, 0.0)
        one_v = jnp.where(lane == LANE_C, 1.0, 0.0)
        for e in range(2):
            pick = (lambda t: t) if e == 0 else _swap_halves
            pieces = [p.astype(F32) for p in _split3(_lane_col(cv, lane, 2 * j + e))]
            ext_q, ext_k = one_q, one_k
            for i, p in enumerate(pieces):
                ext_q = jnp.where(lane == LANE_C + i, p, ext_q)
                ext_k = jnp.where(lane == LANE_ONE + i, -p, ext_k)
            qa_ref[e] = jnp.where(lo, pick(qn), ext_q).astype(BF16)
            ka_ref[e] = jnp.where(lo, pick(kn), ext_k).astype(BF16)
            va = jnp.where(lo, pick(vv), one_v)
            va_ref[e] = va.astype(BF16)
            vt_ref[e] = va.T.astype(BF16)

    tile = lambda base: pl.BlockSpec((tm, 128), lambda i, j: (i, base + j))
    vec = pl.BlockSpec((1, 128), lambda i, j: (0, 0))
    out = pl.BlockSpec((2, tm, 128), lambda i, j: (j, i, 0))
    return pl.pallas_call(
        body, grid=(s // tm, HEADS // 2),
        in_specs=[tile(first), tile(first + 4), tile(first + 8), pl.BlockSpec((tm, 128), lambda i, j: (i, 0)), vec, vec],
        out_specs=[out, out, out, pl.BlockSpec((2, 128, tm), lambda i, j: (j, 0, i))],
        out_shape=[jax.ShapeDtypeStruct((HEADS, s, 128), BF16)] * 3 + [jax.ShapeDtypeStruct((HEADS, 128, s), BF16)],
        compiler_params=_cparams(("parallel", "arbitrary")), name=name)(proj, proj, proj, c, gq2, gk2)


def _carry(ex, n_in, n_out, n_scratch, grid):
    n_xin, n_xout = (len(ex.inputs), len(ex.out_shapes)) if ex else (0, 0)

    def split(refs):
        ins, xins = refs[:n_in], refs[n_in:n_in + n_xin]
        rest = refs[n_in + n_xin:]
        outs, xouts = rest[:n_out], rest[n_out:n_out + n_xout]
        rest = rest[n_out + n_xout:]
        return ins + outs + rest[:n_scratch], (xins, xouts, rest[n_scratch:])

    def first():
        return functools.reduce(lambda a, b: a & b, [pl.program_id(d) == 0 for d in range(len(grid))])

    def last():
        return functools.reduce(lambda a, b: a & b, [pl.program_id(d) == grid[d] - 1 for d in range(len(grid))])

    return split, first, last


def _carried_call(body, ex, grid, in_specs, out_specs, out_shape, scratch, sem, name, operands, vmem=None):
    any_spec = pl.BlockSpec(memory_space=pl.ANY)
    split, first, last = _carry(ex, len(in_specs), len(out_specs), len(scratch), grid)

    def carried(*refs):
        own, xrefs = split(refs)
        if ex:
            @pl.when(first())
            def _():
                ex.start(*xrefs)

        body(*own)
        if ex:
            @pl.when(last())
            def _():
                ex.drain(*xrefs)

    n_xin = len(ex.inputs) if ex else 0
    results = pl.pallas_call(
        carried, grid=grid, in_specs=list(in_specs) + [any_spec] * n_xin,
        out_specs=list(out_specs) + [any_spec] * (len(ex.out_shapes) if ex else 0),
        out_shape=list(out_shape) + (list(ex.out_shapes) if ex else []),
        input_output_aliases={len(in_specs) + i: len(out_specs) + o for i, o in ex.aliases.items()} if ex else {},
        scratch_shapes=list(scratch) + (ex.scratch if ex else []),
        compiler_params=_cparams(sem, vmem), name=name)(*operands, *(ex.inputs if ex else []))
    return results[:len(out_specs)], results[len(out_specs):]


def _tri_rows(t, n):
    qi = sum(jnp.where(t >= r * (r + 1) // 2, 1, 0) for r in range(1, n))
    return qi, t - qi * (qi + 1) // 2


def _tri_cols(t, n):
    ki = sum(jnp.where(t >= r * n - r * (r - 1) // 2, 1, 0) for r in range(1, n))
    return ki, ki + t - (ki * n - ki * (ki - 1) // 2)


def _causal_t(st_blk, tk, tq):
    key = lax.broadcasted_iota(jnp.int32, (tk, tq), 0)
    qry = lax.broadcasted_iota(jnp.int32, (tk, tq), 1)
    return jnp.where(qry >= key, st_blk, -jnp.inf)


def attn_forward(qa, ka, vt, name, ex=None):
    hh, s, _ = qa.shape
    tq = tk = min(512, s)
    nq = s // tq
    grp = HEAD_GROUP

    def body(q_ref, k_ref, vt_ref, o_ref, lse_ref, m_ref, acc_ref):
        qi, ki = _tri_rows(pl.program_id(1), nq)

        @pl.when(ki == 0)
        def _():
            m_ref[...] = jnp.full_like(m_ref, -jnp.inf)
            acc_ref[...] = jnp.zeros_like(acc_ref)

        def step(masked):
            nxt = _dot_nt(k_ref[0], q_ref[0])
            for g in range(grp):
                st = nxt
                if g + 1 < grp:
                    nxt = _dot_nt(k_ref[g + 1], q_ref[g + 1])
                if masked:
                    st = _causal_t(st, tk, tq)
                m_old = m_ref[g]
                m_new = jnp.maximum(m_old, jnp.max(st, axis=0, keepdims=True))
                pt = jnp.exp(st - m_new).astype(BF16)
                acc_ref[g] = jnp.exp(m_old - m_new) * acc_ref[g] + _dot(vt_ref[g], pt)
                m_ref[g] = m_new

        @pl.when(ki < qi)
        def _():
            step(False)

        @pl.when(ki == qi)
        def _():
            step(True)
            for g in range(grp):
                acc = acc_ref[g]
                denom = acc[LANE_C:LANE_C + 1, :]
                o_ref[g] = (acc / denom).T.astype(BF16)
                lse_ref[g] = m_ref[g] + jnp.log(denom)

    qspec = pl.BlockSpec((grp, tq, 128), lambda h, t: (h, _tri_rows(t, nq)[0], 0))
    kspec = pl.BlockSpec((grp, tk, 128), lambda h, t: (h, _tri_rows(t, nq)[1], 0))
    vspec = pl.BlockSpec((grp, 128, tk), lambda h, t: (h, 0, _tri_rows(t, nq)[1]))
    lspec = pl.BlockSpec((grp, 1, tq), lambda h, t: (h, 0, _tri_rows(t, nq)[0]))
    return _carried_call(
        body, ex, (hh // grp, nq * (nq + 1) // 2), [qspec, kspec, vspec], [qspec, lspec],
        [jax.ShapeDtypeStruct((hh, s, 128), BF16), jax.ShapeDtypeStruct((hh, 1, s), F32)],
        [pltpu.VMEM((grp, 1, tq), F32), pltpu.VMEM((grp, 128, tq), F32)],
        ("arbitrary", "arbitrary"), name, (qa, ka, vt))


def attn_backward(qa, ka, va, oa, doa, lse, name, ex=None):
    hh, s, _ = qa.shape
    tq = tk = min(512, s)
    nq = s // tq
    grp = HEAD_GROUP

    def body(q_ref, k_ref, v_ref, o_ref, do_ref, lse_ref, dq_ref, dk_ref, dv_ref, dka_ref, dva_ref):
        ki, qi = _tri_cols(pl.program_id(1), nq)

        @pl.when(pl.program_id(1) == 0)
        def _():
            dq_ref[...] = jnp.zeros_like(dq_ref)

        @pl.when(qi == ki)
        def _():
            dka_ref[...] = jnp.zeros_like(dka_ref)
            dva_ref[...] = jnp.zeros_like(dva_ref)

        def step(masked):
            rows = pl.ds(pl.multiple_of(qi * tq, tq), tq)
            products = lambda g: (_dot_nt(k_ref[g], q_ref[g]), _dot_nt(v_ref[g], do_ref[g]))
            nxt = products(0)
            for g in range(grp):
                st, dpt = nxt
                if g + 1 < grp:
                    nxt = products(g + 1)
                q, k, do = q_ref[g], k_ref[g], do_ref[g]
                if masked:
                    st = _causal_t(st, tk, tq)
                pt = jnp.exp(st - lse_ref[g])
                delta = jnp.sum((do.astype(F32) * o_ref[g].astype(F32)).T, axis=0, keepdims=True)
                dst = (pt * (dpt - delta)).astype(BF16)
                dva_ref[g] += _dot(pt.astype(BF16), do)
                dka_ref[g] += _dot(dst, q)
                dq_ref[g, rows, :] += _dot_tn(dst, k)

        @pl.when(qi > ki)
        def _():
            step(False)

        @pl.when(qi == ki)
        def _():
            step(True)

        @pl.when(qi == nq - 1)
        def _():
            dk_ref[...] = dka_ref[...]
            dv_ref[...] = dva_ref[...].astype(BF16)

    qspec = pl.BlockSpec((grp, tq, 128), lambda h, t: (h, _tri_cols(t, nq)[1], 0))
    lspec = pl.BlockSpec((grp, 1, tq), lambda h, t: (h, 0, _tri_cols(t, nq)[1]))
    kspec = pl.BlockSpec((grp, tk, 128), lambda h, t: (h, _tri_cols(t, nq)[0], 0))
    return _carried_call(
        body, ex, (hh // grp, nq * (nq + 1) // 2), [qspec, kspec, kspec, qspec, qspec, lspec],
        [pl.BlockSpec((grp, s, 128), lambda h, t: (h, 0, 0)), kspec, kspec],
        [jax.ShapeDtypeStruct((hh, s, 128), F32), jax.ShapeDtypeStruct((hh, s, 128), F32),
         jax.ShapeDtypeStruct((hh, s, 128), BF16)],
        [pltpu.VMEM((grp, tk, 128), F32), pltpu.VMEM((grp, tk, 128), F32)],
        ("arbitrary", "arbitrary"), name, (qa, ka, va, oa, doa, lse))


def attn_post(dqa, dka, dva, proj, gq2, gk2, dproj, name):
    s = proj.shape[0]
    tm = min(256, s)

    def body(dq_ref, dk_ref, dv_ref, q_ref, k_ref, gq_ref, gk_ref, dp_any, dp_ref, dc_ref, dgq_ref, dgk_ref):
        lane, lo = _lanes()

        @pl.when(pl.program_id(0) == 0)
        def _():
            dgq_ref[...] = jnp.zeros_like(dgq_ref)
            dgk_ref[...] = jnp.zeros_like(dgk_ref)

        def pair(ref, j):
            return jnp.where(lo, ref[2 * j].astype(F32), _swap_halves(ref[2 * j + 1].astype(F32)))

        def norm_bwd(raw, g, dhat, scale):
            r = lax.rsqrt(_half_mean(raw * raw, lo) + EPS)
            y = raw * r
            dy = dhat * (g * scale)
            return r * (dy - y * _half_mean(dy * y, lo)), jnp.sum(dhat * y, axis=0, keepdims=True) * scale

        dc = jnp.zeros((tm, 128), F32)
        for j in range(HEADS // 2):
            cols = slice(128 * j, 128 * (j + 1))
            dq, dgq = norm_bwd(q_ref[:, cols].astype(F32), gq_ref[...], pair(dq_ref, j), ATTN_SCALE)
            dk, dgk = norm_bwd(k_ref[:, cols].astype(F32), gk_ref[...], pair(dk_ref, j), 1.0)
            dgq_ref[...] += dgq
            dgk_ref[...] += dgk
            dp_ref[:, cols] = dq.astype(BF16)
            dp_ref[:, D_ATTN + 128 * j:D_ATTN + 128 * (j + 1)] = dk.astype(BF16)
            dp_ref[:, 2 * D_ATTN + 128 * j:2 * D_ATTN + 128 * (j + 1)] = pair(dv_ref, j).astype(BF16)
            for e in range(2):
                h = 2 * j + e
                col = _lane_col(dq_ref[h], lane, LANE_C) - _lane_col(dk_ref[h], lane, LANE_ONE)
                dc = jnp.where(lane == h, col, dc)
        dp_ref[:, 3 * D_ATTN:] = jnp.zeros((tm, DPROJ_TAIL - 3 * D_ATTN), BF16)
        dc_ref[...] = dc

    heads = lambda: pl.BlockSpec((HEADS, tm, 128), lambda i: (0, i, 0))
    vec = pl.BlockSpec((1, 128), lambda i: (0, 0))
    first = N_REST // D_ATTN
    return pl.pallas_call(
        body, grid=(s // tm,),
        in_specs=[heads(), heads(), heads(), pl.BlockSpec((tm, D_ATTN), lambda i: (i, first)),
                  pl.BlockSpec((tm, D_ATTN), lambda i: (i, first + 1)), vec, vec, pl.BlockSpec(memory_space=pl.ANY)],
        out_specs=[pl.BlockSpec((tm, DPROJ_TAIL), lambda i: (i, N_REST // DPROJ_TAIL)),
                   pl.BlockSpec((tm, 128), lambda i: (i, 0)), vec, vec],
        out_shape=[jax.ShapeDtypeStruct(dproj.shape, BF16), jax.ShapeDtypeStruct((s, 128), F32),
                   jax.ShapeDtypeStruct((1, 128), F32), jax.ShapeDtypeStruct((1, 128), F32)],
        input_output_aliases={7: 0},
        compiler_params=_cparams(("arbitrary",)), name=name)(dqa, dka, dva, proj, proj, gq2, gk2, dproj)


def _pool_groups(tm):
    gid = lax.broadcasted_iota(jnp.int32, (1, D_POOL), 1) // (D_POOL // 4)
    win = jnp.where(gid == 0, 2.0, jnp.where(gid == 1, 4.0, jnp.where(gid == 2, 8.0, 16.0)))
    return gid, win


def _by_group(gid, v2, v4, v8, v16):
    return jnp.where(gid == 0, v2, jnp.where(gid == 1, v4, jnp.where(gid == 2, v8, v16)))


def _branches(rest_ref, halo_ref, a_ref, wa_ref, wc_ref, wp_ref, sc_ref, cw_ref, ti, tm):
    f = lambda v: v.astype(F32)
    cx, cb, cc, px = f(rest_ref[:, 0:256]), f(rest_ref[:, 256:512]), f(rest_ref[:, 512:768]), f(rest_ref[:, 768:1024])
    live = jnp.where(ti > 0, 1.0, 0.0)
    hz = f(halo_ref[:, 0:256]) * f(halo_ref[:, 512:768]) * live
    hp = f(halo_ref[:, 768:1024]) * live
    z = cc * cx
    zf = jnp.concatenate([hz, z], axis=0)
    z1 = pltpu.roll(zf, 1, 0)[HALO:]
    z2 = pltpu.roll(zf, 2, 0)[HALO:]
    cw = cw_ref[...]
    conv = cw[2:3] * z + cw[1:2] * z1 + cw[0:1] * z2
    uc = cb * conv
    pf = jnp.concatenate([hp, px], axis=0)
    s2 = pf + pltpu.roll(pf, 1, 0)
    s4 = s2 + pltpu.roll(s2, 2, 0)
    s8 = s4 + pltpu.roll(s4, 4, 0)
    s16 = s8 + pltpu.roll(s8, 8, 0)
    gid, win = _pool_groups(tm)
    t = (ti * tm + lax.broadcasted_iota(jnp.int32, (tm, 1), 0)).astype(F32)
    inv = 1.0 / jnp.minimum(t + 1.0, win)
    dpool = _by_group(gid, s2[HALO:], s4[HALO:], s8[HALO:], s16[HALO:]) * inv - px
    _, lo = _lanes()
    a_tok = [jnp.where(lo, f(a_ref[2 * j]), _swap_halves(f(a_ref[2 * j + 1]))).astype(BF16) for j in range(HEADS // 2)]
    y_attn = _dot(a_tok[0], wa_ref[0:128, :])
    for j in range(1, HEADS // 2):
        y_attn += _dot(a_tok[j], wa_ref[128 * j:128 * (j + 1), :])
    y_conv = _dot(uc.astype(BF16), wc_ref[...])
    y_pool_raw = _dot(dpool.astype(BF16), wp_ref[...])
    sg = [_sigmoid(f(rest_ref[:, 1024 + i * D_MODEL:1024 + (i + 1) * D_MODEL])) for i in range(3)]
    return dict(cx=cx, cb=cb, cc=cc, z=z, z1=z1, z2=z2, conv=conv, uc=uc, dpool=dpool, inv=inv, gid=gid, a_tok=a_tok,
                y_attn=y_attn, y_conv=y_conv, y_pool_raw=y_pool_raw, sg=sg, cw=cw)


def _mix_specs(tm, ti_of):
    blocks_per_tile = tm // HALO
    return [
        pl.BlockSpec((tm, N_REST), lambda i: (ti_of(i), 0)),
        pl.BlockSpec((HALO, 1024), lambda i: (jnp.maximum(ti_of(i) * blocks_per_tile - 1, 0), 0)),
        pl.BlockSpec((HEADS, tm, 128), lambda i: (0, ti_of(i), 0)),
        pl.BlockSpec((D_ATTN, D_MODEL), lambda i: (0, 0)),
        pl.BlockSpec((D_CONV, D_MODEL), lambda i: (0, 0)),
        pl.BlockSpec((D_POOL, D_MODEL), lambda i: (0, 0)),
        pl.BlockSpec((1, D_MODEL), lambda i: (0, 0)),
        pl.BlockSpec((8, D_CONV), lambda i: (0, 0)),
    ]


def mix_fwd(proj, a, x, wa, wc, wp, scale, cw, wo, name, ex=None):
    s = x.shape[0]
    tm = min(256, s)

    def body(rest_ref, halo_ref, a_ref, wa_ref, wc_ref, wp_ref, sc_ref, cw_ref, wo_ref, x_ref, o_ref):
        b = _branches(rest_ref, halo_ref, a_ref, wa_ref, wc_ref, wp_ref, sc_ref, cw_ref, pl.program_id(0), tm)
        merged = b["sg"][0] * b["y_attn"] + b["sg"][1] * b["y_conv"] + b["sg"][2] * (b["y_pool_raw"] * sc_ref[...])
        o_ref[...] = x_ref[...] + _dot(merged.astype(BF16), wo_ref[...])

    (x1,), carried = _carried_call(
        body, ex, (s // tm,),
        _mix_specs(tm, lambda i: i) + [pl.BlockSpec((D_MODEL, D_MODEL), lambda i: (0, 0)),
                                       pl.BlockSpec((tm, D_MODEL), lambda i: (i, 0))],
        [pl.BlockSpec((tm, D_MODEL), lambda i: (i, 0))], [jax.ShapeDtypeStruct((s, D_MODEL), F32)], [],
        ("arbitrary",), name, (proj, proj, a, wa, wc, wp, scale, cw, wo, x))
    return x1, carried


def mix_bwd(proj, a, dx1, wa, wc, wp, scale, cw, wo, name):
    s = dx1.shape[0]
    tm = min(256, s)
    nt = s // tm
    ti_of = lambda i: nt - 1 - i
    n = tm + HALO

    def body(rest_ref, halo_ref, a_ref, wa_ref, wc_ref, wp_ref, sc_ref, cw_ref, wo_ref,
             dx_ref, dp_ref, da_ref, at_ref, mg_ref, dya_ref, dyc_ref, dyp_ref, uc_ref, dd_ref, dsc_ref, dcw_ref,
             cdc_ref, cde_ref):
        i = pl.program_id(0)
        ti = ti_of(i)

        @pl.when(i == 0)
        def _():
            cdc_ref[...] = jnp.zeros_like(cdc_ref)
            cde_ref[...] = jnp.zeros_like(cde_ref)
            dsc_ref[...] = jnp.zeros_like(dsc_ref)
            dcw_ref[...] = jnp.zeros_like(dcw_ref)

        b = _branches(rest_ref, halo_ref, a_ref, wa_ref, wc_ref, wp_ref, sc_ref, cw_ref, ti, tm)
        sg, sc = b["sg"], sc_ref[...]
        y_pool = b["y_pool_raw"] * sc
        merged = sg[0] * b["y_attn"] + sg[1] * b["y_conv"] + sg[2] * y_pool
        mg_ref[...] = merged.astype(BF16)
        dm = _dot_nt(dx_ref[...].astype(BF16), wo_ref[...])
        for j, y in enumerate((b["y_attn"], b["y_conv"], y_pool)):
            dp_ref[:, 1024 + j * D_MODEL:1024 + (j + 1) * D_MODEL] = (dm * y * sg[j] * (1.0 - sg[j])).astype(BF16)
        dya = (dm * sg[0]).astype(BF16)
        dya_ref[...] = dya
        _, lo = _lanes()
        for j in range(HEADS // 2):
            at_ref[:, 128 * j:128 * (j + 1)] = b["a_tok"][j]
            da = _dot_nt(dya, wa_ref[128 * j:128 * (j + 1), :])
            da_ref[2 * j] = jnp.where(lo, da, 0.0).astype(BF16)
            da_ref[2 * j + 1] = jnp.where(lo, _swap_halves(da), 0.0).astype(BF16)
        dyc = (dm * sg[1]).astype(BF16)
        dyc_ref[...] = dyc
        duc = _dot_nt(dyc, wc_ref[...])
        dyp = dm * sg[2]
        dsc_ref[...] += jnp.sum(dyp * b["y_pool_raw"], axis=0, keepdims=True)
        dypr = (dyp * sc).astype(BF16)
        dyp_ref[...] = dypr
        ddp = _dot_nt(dypr, wp_ref[...])
        uc_ref[...] = b["uc"].astype(BF16)
        dd_ref[...] = b["dpool"].astype(BF16)

        dconv = duc * b["cb"]
        dp_ref[:, 256:512] = (duc * b["conv"]).astype(BF16)
        dcf = jnp.concatenate([dconv, cdc_ref[...]], axis=0)
        cw = b["cw"]
        dz = cw[2:3] * dconv + cw[1:2] * pltpu.roll(dcf, n - 1, 0)[:tm] + cw[0:1] * pltpu.roll(dcf, n - 2, 0)[:tm]
        dp_ref[:, 0:256] = (dz * b["cc"]).astype(BF16)
        dp_ref[:, 512:768] = (dz * b["cx"]).astype(BF16)
        dcw_ref[0:1, :] += jnp.sum(dconv * b["z2"], axis=0, keepdims=True)
        dcw_ref[1:2, :] += jnp.sum(dconv * b["z1"], axis=0, keepdims=True)
        dcw_ref[2:3, :] += jnp.sum(dconv * b["z"], axis=0, keepdims=True)
        cdc_ref[...] = dconv[:HALO]

        e = ddp * b["inv"]
        ef = jnp.concatenate([e, cde_ref[...]], axis=0)
        r2 = ef + pltpu.roll(ef, n - 1, 0)
        r4 = r2 + pltpu.roll(r2, n - 2, 0)
        r8 = r4 + pltpu.roll(r4, n - 4, 0)
        r16 = r8 + pltpu.roll(r8, n - 8, 0)
        dp_ref[:, 768:1024] = (_by_group(b["gid"], r2[:tm], r4[:tm], r8[:tm], r16[:tm]) - ddp).astype(BF16)
        cde_ref[...] = e[:HALO]

    tile = lambda w: pl.BlockSpec((tm, w), lambda i: (ti_of(i), 0))
    whole = lambda r, c: pl.BlockSpec((r, c), lambda i: (0, 0))
    bf = lambda w: jax.ShapeDtypeStruct((s, w), BF16)
    return pl.pallas_call(
        body, grid=(nt,),
        in_specs=_mix_specs(tm, ti_of) + [whole(D_MODEL, D_MODEL), tile(D_MODEL)],
        out_specs=[tile(N_REST), pl.BlockSpec((HEADS, tm, 128), lambda i: (0, ti_of(i), 0)), tile(D_ATTN),
                   tile(D_MODEL), tile(D_MODEL), tile(D_MODEL), tile(D_MODEL),
                   tile(D_CONV), tile(D_POOL), whole(1, D_MODEL), whole(8, D_CONV)],
        out_shape=[bf(DPROJ_COLS), jax.ShapeDtypeStruct((HEADS, s, 128), BF16), bf(D_ATTN),
                   bf(D_MODEL), bf(D_MODEL), bf(D_MODEL), bf(D_MODEL), bf(D_CONV), bf(D_POOL),
                   jax.ShapeDtypeStruct((1, D_MODEL), F32), jax.ShapeDtypeStruct((8, D_CONV), F32)],
        scratch_shapes=[pltpu.VMEM((HALO, D_CONV), F32), pltpu.VMEM((HALO, D_POOL), F32)],
        compiler_params=_cparams(("arbitrary",)), name=name)(proj, proj, a, wa, wc, wp, scale, cw, wo, dx1)


def _adamw_math(w, g, m, v):
    m = ADAM_B1 * m + (1.0 - ADAM_B1) * g
    v = ADAM_B2 * v + (1.0 - ADAM_B2) * (g * g)
    m_hat = m / (1.0 - ADAM_B1 ** ADAM_STEP)
    v_hat = v / (1.0 - ADAM_B2 ** ADAM_STEP)
    delta = -ADAM_LR * (m_hat / (jnp.sqrt(v_hat) + ADAM_EPS) + ADAM_WD * w)
    return delta, m, v


ADAMW_PARTS_BLOCK_BYTES = 4 * 2 ** 20


def _row_tile(rows, cols, copies, itemsize):
    row_bytes = copies * (-(-cols // 128) * 128) * itemsize
    fits = [t for t in range(16, rows + 1, 16) if rows % t == 0 and t * row_bytes <= ADAMW_PARTS_BLOCK_BYTES]
    return max(fits) if fits else rows


def pair_sum(blocks, stage, me, name):
    n_slots, rows, cols = stage.shape
    tr = _row_tile(rows, cols, 1, 4)

    def body(me_ref, a_ref, b_ref, o_ref):
        o_ref[...] = (a_ref[...].astype(F32) + b_ref[...].astype(F32)).astype(BF16)

    slot = pl.BlockSpec((None, tr, cols), lambda i, r, me_ref: (i, r, 0))
    return pl.pallas_call(
        body, out_shape=jax.ShapeDtypeStruct(stage.shape, BF16),
        grid_spec=pltpu.PrefetchScalarGridSpec(
            num_scalar_prefetch=1, grid=(n_slots, rows // tr),
            in_specs=[pl.BlockSpec((None, tr, cols), lambda i, r, me_ref: (me_ref[0] ^ (2 * i), r, 0)), slot],
            out_specs=slot),
        compiler_params=_cparams(("parallel", "parallel")), name=name)(me.reshape(1), blocks, stage)


def adamw_sum(parts, w, m, v, name):
    layers, rows, cols = w.shape
    n_parts = parts.shape[1]
    if rows % 16 == 0:
        tr, tc = _row_tile(rows, cols, n_parts, parts.dtype.itemsize), cols
    else:
        tr, tc = rows, _pick(cols, (256, 128))

    def body(p_ref, w_ref, m_ref, v_ref, g_ref, d_ref, nm_ref, nv_ref):
        g = p_ref[0].astype(F32)
        for i in range(1, n_parts):
            g = g + p_ref[i].astype(F32)
        g_ref[...] = g
        d_ref[...], nm_ref[...], nv_ref[...] = _adamw_math(w_ref[...], g, m_ref[...], v_ref[...])

    spec = pl.BlockSpec((None, tr, tc), lambda l, i, j: (l, i, j))
    return pl.pallas_call(
        body, grid=(layers, rows // tr, cols // tc),
        in_specs=[pl.BlockSpec((None, n_parts, tr, tc), lambda l, i, j: (l, 0, i, j)), spec, spec, spec],
        out_specs=[spec] * 4, out_shape=[jax.ShapeDtypeStruct((layers, rows, cols), F32)] * 4,
        compiler_params=_cparams(("parallel", "parallel", "parallel")), name=name)(parts, w, m, v)


def _me():
    return lax.axis_index("x"), lax.axis_index("y"), lax.axis_index("c")


N_PEERS = N_DEV - 1


def all_gather(shards, name):
    n = len(shards)
    any_spec = pl.BlockSpec(memory_space=pl.ANY)

    def body(*refs):
        x_refs, out_refs = refs[:n], refs[n:2 * n]
        send_sems, recv_sems, local_sems = refs[2 * n:]
        x, y, c = _me()
        me, sibling = (x, y, c), (x, y, 1 - c)
        chips = [(1 - x, y), (x, 1 - y), (1 - x, 1 - y)]

        def copy(t, k, block, to, from_input=False):
            slot = out_refs[t].at[4 * block[0] + 2 * block[1] + block[2]]
            return pltpu.make_async_remote_copy(
                src_ref=x_refs[t] if from_input else slot, dst_ref=slot, send_sem=send_sems.at[N_PEERS * t + k],
                recv_sem=recv_sems.at[N_PEERS * t + k], device_id=to, device_id_type=pl.DeviceIdType.MESH)

        mine = [pltpu.make_async_copy(x_refs[t], out_refs[t].at[4 * x + 2 * y + c], local_sems.at[t]) for t in range(n)]
        started = []
        for t in range(n):
            mine[t].start()
            started.append(copy(t, 0, me, sibling, from_input=True))
            started += [copy(t, 1 + j, me, (*chip, c), from_input=True) for j, chip in enumerate(chips)]
        for cp in started:
            cp.start()
        for j, chip in enumerate(chips):
            for t in range(n):
                copy(t, 1 + j, (*chip, c), me).wait_recv()
                fwd = copy(t, 4 + j, (*chip, c), sibling)
                fwd.start()
                started.append(fwd)
        for t in range(n):
            copy(t, 0, sibling, me).wait_recv()
            for j, chip in enumerate(chips):
                copy(t, 4 + j, (*chip, 1 - c), me).wait_recv()
        for cp in started:
            cp.wait_send()
        for cp in mine:
            cp.wait()

    return pl.pallas_call(
        body, out_shape=[jax.ShapeDtypeStruct((N_DEV,) + s.shape, s.dtype) for s in shards],
        in_specs=[any_spec] * n, out_specs=[any_spec] * n,
        scratch_shapes=[pltpu.SemaphoreType.DMA((N_PEERS * n,)), pltpu.SemaphoreType.DMA((N_PEERS * n,)),
                        pltpu.SemaphoreType.DMA((n,))],
        name=name)(*shards)


SIBLING = 1
OTHER_CHIPS = (2, 4, 6)
SAME_CORE = (0,) + OTHER_CHIPS


class Exchange:
    def __init__(self, inputs, out_shapes, aliases, copies, local=()):
        self.inputs, self.out_shapes, self.aliases = list(inputs), list(out_shapes), aliases
        self._copies, self._local = list(copies), list(local)
        self.scratch = [pltpu.SemaphoreType.DMA((len(self._copies),)), pltpu.SemaphoreType.DMA((len(self._copies),)),
                        pltpu.SemaphoreType.DMA((max(len(self._local), 1),))]

    def _build(self, ins, outs, sems):
        send_sems, recv_sems, local_sems = sems
        x, y, c = _me()
        me = 4 * x + 2 * y + c
        local = [functools.partial(pltpu.make_async_copy, src(ins, outs, me), dst(outs, me), local_sems.at[i])
                 for i, (src, dst) in enumerate(self._local)]
        sends, recvs = [], []
        for i, (mask, src, dst) in enumerate(self._copies):
            px, py, pc = x ^ ((mask >> 2) & 1), y ^ ((mask >> 1) & 1), c ^ (mask & 1)
            pair = dict(send_sem=send_sems.at[i], recv_sem=recv_sems.at[i], device_id_type=pl.DeviceIdType.MESH)
            sends.append(functools.partial(
                pltpu.make_async_remote_copy, src_ref=src(ins, outs, me), dst_ref=dst(outs, me), device_id=(px, py, pc), **pair))
            recvs.append(functools.partial(
                pltpu.make_async_remote_copy, src_ref=src(ins, outs, me), dst_ref=dst(outs, me ^ mask), device_id=(x, y, c), **pair))
        return local, sends, recvs

    def start(self, ins, outs, sems):
        local, sends, _ = self._build(ins, outs, sems)
        for make in local + sends:
            make().start()

    def drain(self, ins, outs, sems):
        local, sends, recvs = self._build(ins, outs, sems)
        for make in recvs:
            make().wait_recv()
        for make in sends:
            make().wait_send()
        for make in local:
            make().wait()


def _bind(fn, *args):
    return functools.partial(fn, *args)


def join_exchanges(a, b):
    if a is None or b is None:
        return a or b
    na_in, na_out = len(a.inputs), len(a.out_shapes)

    def src_a(fn):
        return lambda ins, outs, me: fn(ins[:na_in], outs[:na_out], me)

    def dst_a(fn):
        return lambda outs, who: fn(outs[:na_out], who)

    def src_b(fn):
        return lambda ins, outs, me: fn(ins[na_in:], outs[na_out:], me)

    def dst_b(fn):
        return lambda outs, who: fn(outs[na_out:], who)

    copies = [(m, src_a(s), dst_a(d)) for m, s, d in a._copies] + [(m, src_b(s), dst_b(d)) for m, s, d in b._copies]
    local = [(src_a(s), dst_a(d)) for s, d in a._local] + [(src_b(s), dst_b(d)) for s, d in b._local]
    aliases = dict(a.aliases)
    aliases.update({na_in + i: na_out + o for i, o in b.aliases.items()})
    return Exchange(a.inputs + b.inputs, a.out_shapes + b.out_shapes, aliases, copies, local)


def gather_over_ici(shards):
    copies = [(mask, _bind(lambda t, ins, outs, me: ins[t], t), _bind(lambda t, outs, sender: outs[t].at[sender], t))
              for t in range(len(shards)) for mask in OTHER_CHIPS]
    local = [(_bind(lambda t, ins, outs, me: ins[t], t), _bind(lambda t, outs, me: outs[t].at[me], t))
             for t in range(len(shards))]
    return Exchange(shards, [jax.ShapeDtypeStruct((N_DEV,) + s.shape, s.dtype) for s in shards], {}, copies, local)


def gather_over_d2d(gathered):
    copies = [(SIBLING, _bind(lambda t, m, ins, outs, me: outs[t].at[me ^ m], t, m),
               _bind(lambda t, m, outs, sender: outs[t].at[sender ^ m], t, m))
              for t in range(len(gathered)) for m in SAME_CORE]
    return Exchange(gathered, [jax.ShapeDtypeStruct(g.shape, g.dtype) for g in gathered],
                    {t: t for t in range(len(gathered))}, copies)


def scatter_over_d2d(blocks):
    copies = [(SIBLING, _bind(lambda t, m, ins, outs, me: ins[t].at[me ^ SIBLING ^ m], t, m),
               _bind(lambda t, i, outs, sender: outs[t].at[i], t, i))
              for t in range(len(blocks)) for i, m in enumerate(SAME_CORE)]
    return Exchange(blocks, [jax.ShapeDtypeStruct((len(SAME_CORE),) + b.shape[1:], b.dtype) for b in blocks], {}, copies)


def scatter_over_ici(pair_sums, bufs, layer):
    n = len(pair_sums)
    copies = [(m, _bind(lambda t, i, ins, outs, me: ins[t].at[i], t, i),
               _bind(lambda t, i, outs, sender: outs[t].at[layer, i], t, i))
              for t in range(n) for i, m in enumerate(SAME_CORE) if m]
    local = [(_bind(lambda t, ins, outs, me: ins[t].at[0], t), _bind(lambda t, outs, me: outs[t].at[layer, 0], t))
             for t in range(n)]
    return Exchange(list(pair_sums) + list(bufs), [jax.ShapeDtypeStruct(b.shape, b.dtype) for b in bufs],
                    {n + t: t for t in range(n)}, copies, local)


def run_exchange(ex, name):
    any_spec = pl.BlockSpec(memory_space=pl.ANY)
    n_in, n_out = len(ex.inputs), len(ex.out_shapes)

    def body(*refs):
        ins, outs, sems = refs[:n_in], refs[n_in:n_in + n_out], refs[n_in + n_out:]
        ex.start(ins, outs, sems)
        ex.drain(ins, outs, sems)

    return pl.pallas_call(
        body, out_shape=ex.out_shapes, in_specs=[any_spec] * n_in, out_specs=[any_spec] * n_out,
        input_output_aliases=ex.aliases, scratch_shapes=ex.scratch, name=name)(*ex.inputs)


MATRICES = ("w_in", "w_attn_out", "w_conv_out", "pool_w", "w_o", "w_ffn_in", "w_ffn_out")
TRANSPOSED = ("w_in", "w_ffn_in")
EVERY = tuple(range(len(MATRICES)))
IN_PROJ_PART, ATTN_PART, MIX_PART = (0,), (1, 2, 3, 4, 5), (6,)
LATE = (0,)
EARLY = EVERY[1:]
EARLY_FIRST, EARLY_SECOND = (4, 6), (1, 2, 3, 5)
SHARD_INFO = {
    "w_in": ((DEPTH, D_IN // N_DEV, D_MODEL), 1),
    "w_attn_out": ((DEPTH, D_ATTN, D_MODEL // N_DEV), 2),
    "w_conv_out": ((DEPTH, D_CONV, D_MODEL // N_DEV), 2),
    "pool_w": ((DEPTH, 4, 64, 256 // N_DEV), 3),
    "w_o": ((DEPTH, D_MODEL // N_DEV, D_MODEL), 1),
    "w_ffn_in": ((DEPTH, 2 * D_FF // N_DEV, D_MODEL), 1),
    "w_ffn_out": ((DEPTH, D_FF // N_DEV, D_MODEL), 1),
}


def _handled(name, t):
    return jnp.transpose(t, (0, 2, 1)) if name in TRANSPOSED else t
VECTORS = ("norm_mix_g", "forget_b", "q_norm_g", "k_norm_g", "pool_scale", "norm_ffn_g")
VECTOR_SHAPES = {"norm_mix_g": (DEPTH, D_MODEL), "forget_b": (DEPTH, HEADS), "q_norm_g": (DEPTH, HEAD_DIM),
                 "k_norm_g": (DEPTH, HEAD_DIM), "pool_scale": (DEPTH, D_MODEL), "norm_ffn_g": (DEPTH, D_MODEL)}
CONV_W_FULL = (DEPTH, 3, D_CONV)


def _size(shape):
    n = 1
    for v in shape:
        n *= v
    return n


def _pack(arrays, rows, cols):
    flat = jnp.concatenate([a.reshape(-1) for a in arrays])
    return jnp.pad(flat, (0, rows * cols - flat.shape[0])).reshape(rows, cols)


def _unpack(packed, shapes):
    flat, out, off = packed.reshape(-1), [], 0
    for shp in shapes:
        out.append(flat[off:off + _size(shp)].reshape(shp))
        off += _size(shp)
    return out


def _join_shards(stacked, axis):
    moved = jnp.moveaxis(stacked, 0, axis)
    shp = list(moved.shape)
    shp[axis:axis + 2] = [shp[axis] * shp[axis + 1]]
    return moved.reshape(shp)


def _cut_shards(full, axis):
    shp = list(full.shape)
    shp[axis:axis + 1] = [N_DEV, shp[axis] // N_DEV]
    return jnp.moveaxis(full.reshape(shp), axis, 0)


N_MOVED = 1544
SHARD_ROWS = D_IN // N_DEV


def _regroup_w_in(shards):
    wt = shards.reshape(D_IN, shards.shape[2])
    pad = jnp.zeros((N_FULL - D_IN, wt.shape[1]), wt.dtype)
    return jnp.concatenate([wt[N_MOVED:], wt[:N_MOVED], pad], axis=0)


def _ungroup_w_in(wpt):
    def kernel_rows(a, b):
        if b <= N_MOVED:
            return [wpt[a + D_IN - N_MOVED:b + D_IN - N_MOVED]]
        if a >= N_MOVED:
            return [wpt[a - N_MOVED:b - N_MOVED]]
        return kernel_rows(a, N_MOVED) + kernel_rows(N_MOVED, b)

    return jnp.stack([jnp.concatenate(kernel_rows(s * SHARD_ROWS, (s + 1) * SHARD_ROWS), axis=0) for s in range(N_DEV)])


def _pool_block_diag(w):
    out = jnp.zeros((D_POOL, D_MODEL), w.dtype)
    for g in range(4):
        out = lax.dynamic_update_slice(out, w[g], (g * 64, g * 256))
    return out


def _pool_from_block_diag(wbd):
    return jnp.stack([wbd[g * 64:(g + 1) * 64, g * 256:(g + 1) * 256] for g in range(4)])


def _layer_weights(mats, vec, conv_w, l):
    wp = _pool_block_diag(mats["pool_w"])
    row = lambda v: v.reshape(1, -1)
    fb = jnp.zeros((1, 128), F32).at[0, :HEADS].set(vec["forget_b"][l])
    cw = jnp.zeros((8, D_CONV), F32).at[:3].set(conv_w[l])
    twice = lambda v: jnp.tile(v.reshape(1, -1), (1, 2))
    return dict(
        wt_in=_regroup_w_in(mats["w_in"]), wt_ffn_in=mats["w_ffn_in"], w_ffn_out=mats["w_ffn_out"],
        wa=mats["w_attn_out"], wc=mats["w_conv_out"], wp=wp, wo=mats["w_o"],
        g_mix=row(vec["norm_mix_g"][l]), g_ffn=row(vec["norm_ffn_g"][l]), gq2=twice(vec["q_norm_g"][l]),
        gk2=twice(vec["k_norm_g"][l]), scale=row(vec["pool_scale"][l]), fb=fb, cw=cw)


def _layer_fwd(x, w, l, comm):
    (proj, h), half_a = norm_matmul(x, w["g_mix"], w["wt_in"], N_MAIN, f"in_proj_{l}", comm.gather_ici(l + 1, IN_PROJ_PART))
    z, c = forget_fwd(h, w["wt_in"], w["fb"], f"forget_fwd_{l}")
    qa, ka, va, vt = attn_prep(proj, c, w["gq2"], w["gk2"], f"attn_prep_{l}")
    (oa, lse), half_b = attn_forward(qa, ka, vt, f"attn_fwd_{l}", comm.gather_ici(l + 1, ATTN_PART))
    x1, half_c = mix_fwd(proj, oa, x, w["wa"], w["wc"], w["wp"], w["scale"], w["cw"], w["wo"], f"mix_fwd_{l}",
                         comm.gather_ici(l + 1, MIX_PART))
    half = list(half_a) + list(half_b) + list(half_c)
    (gu, h2), gathered = norm_matmul(x1, w["g_ffn"], w["wt_ffn_in"], 2 * D_FF, f"ffn_in_{l}", comm.gather_d2d(l + 1, half))
    x2 = swiglu_matmul(gu, w["w_ffn_out"], x1, f"ffn_out_{l}")
    saved = dict(x=x, proj=proj, h=h, z=z, qa=qa, ka=ka, va=va, oa=oa, lse=lse, x1=x1, gu=gu, h2=h2)
    return x2, saved, gathered


def _layer_bwd(dx2, sv, w, l, comm):
    g = {}
    (dgu, act), stage = swiglu_bwd(dx2, sv["gu"], w["w_ffn_out"], f"ffn_out_bwd_{l}", comm.scatter_d2d(l + 1))
    sums = comm.pair_sums(l + 1, stage)
    g["w_ffn_out"] = tn_matmul(act, dx2, f"dw_ffn_out_{l}")
    g["w_ffn_in"] = tn_matmul(dgu, sv["h2"], f"dw_ffn_in_{l}")
    (dx1, dg), _ = matmul_normbwd(dgu, w["wt_ffn_in"], sv["x1"], w["g_ffn"], dx2, f"ffn_in_bwd_{l}")
    g["norm_ffn_g"] = dg[0]

    (dproj, doa, a_tok, merged, dya, dyc, dyp, uc, dd, dscale, dcw) = mix_bwd(
        sv["proj"], sv["oa"], dx1, w["wa"], w["wc"], w["wp"], w["scale"], w["cw"], w["wo"], f"mix_bwd_{l}")
    g["w_o"] = tn_matmul(merged, dx1, f"dw_o_{l}")
    g["w_attn_out"], g["w_conv_out"], dwp = tn_matmuls([(a_tok, dya), (uc, dyc), (dd, dyp)], f"dw_branches_{l}")
    g["pool_w"] = _pool_from_block_diag(dwp)
    g["pool_scale"] = dscale[0]
    g["conv_w"] = dcw[:3]

    early = comm.early(l)
    comm.grads(l, g)
    above = comm.scatter_ici(l + 1, sums)
    (dqa, dka, dva), got = attn_backward(sv["qa"], sv["ka"], sv["va"], sv["oa"], doa, sv["lse"], f"attn_bwd_{l}",
                                         join_exchanges(above, comm.scatter_d2d(l, early) if early else None))
    n_above = len(above.out_shapes) if above else 0
    comm.scattered(got[:n_above])
    early_sums = dict(zip(early, comm.pair_sums(l, got[n_above:], early))) if early else {}
    early_ici = lambda which: comm.scatter_ici(l, [early_sums[t] for t in which], which) if early else None
    dproj, dc, dgq, dgk = attn_post(dqa, dka, dva, sv["proj"], w["gq2"], w["gk2"], dproj, f"attn_post_{l}")
    g["q_norm_g"] = dgq[0, :HEAD_DIM] + dgq[0, HEAD_DIM:]
    g["k_norm_g"] = dgk[0, :HEAD_DIM] + dgk[0, HEAD_DIM:]
    dproj, db = forget_bwd(dc, sv["z"], dproj, f"forget_bwd_{l}")
    g["forget_b"] = db[0, :HEADS]

    dw_in = tn_matmul(dproj, sv["h"], f"dw_in_{l}", m_cols=N_FULL, ex=early_ici(EARLY_FIRST))
    if early:
        dw_in, got = dw_in
        comm.scattered(got, EARLY_FIRST)
    g["w_in"] = _ungroup_w_in(dw_in)
    (dx, dg), got = matmul_normbwd(dproj, w["wt_in"], sv["x"], w["g_mix"], dx1, f"in_proj_bwd_{l}", k=N_FULL,
                                   ex=early_ici(EARLY_SECOND))
    comm.scattered(got, EARLY_SECOND if early else None)
    g["norm_mix_g"] = dg[0]
    comm.grads(l, g)
    return dx


def _local_step(x, tgt, comm):
    ws, saved = [], []
    w = comm.weights(0, None)
    for l in range(DEPTH):
        ws.append(w)
        x, sv, gathered = _layer_fwd(x, w, l, comm)
        saved.append(sv)
        if l + 1 < DEPTH:
            w = comm.weights(l + 1, gathered)
    sq, dx = loss_kernel(x, tgt, "loss")
    for l in reversed(range(DEPTH)):
        dx = _layer_bwd(dx, saved[l], ws[l], l, comm)
    comm.finish()
    return sq[0, 0], dx


def kernel(x, norm_mix_g, w_in, forget_b, q_norm_g, k_norm_g, w_attn_out, conv_w, w_conv_out, pool_w, pool_scale, w_o, norm_ffn_g, w_ffn_in, w_ffn_out, loss_target, m_norm_mix_g, m_w_in, m_forget_b, m_q_norm_g, m_k_norm_g, m_w_attn_out, m_conv_w, m_w_conv_out, m_pool_w, m_pool_scale, m_w_o, m_norm_ffn_g, m_w_ffn_in, m_w_ffn_out, v_norm_mix_g, v_w_in, v_forget_b, v_q_norm_g, v_k_norm_g, v_w_attn_out, v_conv_w, v_w_conv_out, v_pool_w, v_pool_scale, v_w_o, v_norm_ffn_g, v_w_ffn_in, v_w_ffn_out):
    w = dict(norm_mix_g=norm_mix_g, w_in=w_in, forget_b=forget_b, q_norm_g=q_norm_g, k_norm_g=k_norm_g,
             w_attn_out=w_attn_out, conv_w=conv_w, w_conv_out=w_conv_out, pool_w=pool_w, pool_scale=pool_scale,
             w_o=w_o, norm_ffn_g=norm_ffn_g, w_ffn_in=w_ffn_in, w_ffn_out=w_ffn_out)
    m = dict(norm_mix_g=m_norm_mix_g, w_in=m_w_in, forget_b=m_forget_b, q_norm_g=m_q_norm_g, k_norm_g=m_k_norm_g,
             w_attn_out=m_w_attn_out, conv_w=m_conv_w, w_conv_out=m_w_conv_out, pool_w=m_pool_w,
             pool_scale=m_pool_scale, w_o=m_w_o, norm_ffn_g=m_norm_ffn_g, w_ffn_in=m_w_ffn_in, w_ffn_out=m_w_ffn_out)
    v = dict(norm_mix_g=v_norm_mix_g, w_in=v_w_in, forget_b=v_forget_b, q_norm_g=v_q_norm_g, k_norm_g=v_k_norm_g,
             w_attn_out=v_w_attn_out, conv_w=v_conv_w, w_conv_out=v_w_conv_out, pool_w=v_pool_w,
             pool_scale=v_pool_scale, w_o=v_w_o, norm_ffn_g=v_norm_ffn_g, w_ffn_in=v_w_ffn_in, w_ffn_out=v_w_ffn_out)
    me = 4 * lax.axis_index("x") + 2 * lax.axis_index("y") + lax.axis_index("c")
    layer_shard = {n: SHARD_INFO[n][0][1:] for n in MATRICES}
    cut_axis = {n: SHARD_INFO[n][1] - 1 for n in MATRICES}

    vec = {n: w[n] for n in VECTORS}
    rc = {n: (_size(layer_shard[n][:-1]), layer_shard[n][-1]) for n in MATRICES}

    class Comm:
        bufs = [lax.empty((DEPTH, len(SAME_CORE)) + layer_shard[n], BF16) for n in MATRICES]
        blocks = [None] * DEPTH
        small_g = [None] * DEPTH
        conv_full = None

        @staticmethod
        def shards(l):
            return [_handled(n, w[n])[l].astype(BF16) for n in MATRICES]

        @staticmethod
        def gather_ici(l, part):
            return gather_over_ici([Comm.shards(l)[t] for t in part]) if l < DEPTH else None

        @staticmethod
        def gather_d2d(l, half):
            return gather_over_d2d(half) if l < DEPTH else None

        @staticmethod
        def weights(l, gathered):
            if l == 0:
                *gathered, conv_g = all_gather(Comm.shards(0) + [_pack([conv_w], 8, 128)], "gather_0")
                Comm.conv_full = _join_shards(jnp.stack([_unpack(conv_g[i], [conv_w.shape])[0] for i in range(N_DEV)]), 2)
            mats = {n: t if n == "w_in" else _join_shards(t, cut_axis[n]) for n, t in zip(MATRICES, gathered)}
            return _layer_weights(mats, vec, Comm.conv_full, l)

        @staticmethod
        def grads(l, g):
            Comm.small_g[l] = g
            Comm.blocks[l] = [None if n not in g else g[n] if n == "w_in" else _cut_shards(g[n], cut_axis[n])
                              for n in MATRICES]

        @staticmethod
        def early(l):
            return EARLY if l == 0 else None

        @staticmethod
        def scatter_d2d(l, which=EVERY):
            return scatter_over_d2d([Comm.blocks[l][t] for t in which]) if l < DEPTH else None

        @staticmethod
        def pair_sums(l, stage, which=EVERY):
            if l >= DEPTH:
                return None
            return [pair_sum(Comm.blocks[l][t].reshape((N_DEV,) + rc[MATRICES[t]]),
                             s.reshape((len(SAME_CORE),) + rc[MATRICES[t]]), me,
                             f"pair_sum_{MATRICES[t]}_{l}").reshape(s.shape) for t, s in zip(which, stage)]

        @staticmethod
        def scatter_ici(l, sums, which=EVERY):
            return scatter_over_ici(sums, [Comm.bufs[t] for t in which], l) if l < DEPTH else None

        @staticmethod
        def scattered(results, which=EVERY):
            for t, r in zip(which or (), results):
                Comm.bufs[t] = r

        @staticmethod
        def finish():
            stage = run_exchange(Comm.scatter_d2d(0, LATE), "scatter_d2d_0")
            Comm.scattered(run_exchange(Comm.scatter_ici(0, Comm.pair_sums(0, stage, LATE), LATE), "scatter_ici_0"), LATE)

    small_g, received = Comm.small_g, Comm
    sq, dx = _local_step(x[0], loss_target[0], Comm)
    loss = lax.psum(0.5 * sq / D_MODEL, ("x", "y", "c"))

    big = {}
    for n, parts in zip(MATRICES, received.bufs):
        outs = adamw_sum(parts.reshape((DEPTH, len(SAME_CORE)) + rc[n]),
                         *[_handled(n, d[n]).reshape((DEPTH,) + rc[n]) for d in (w, m, v)], f"adamw_{n}")
        big[n] = [_handled(n, t.reshape((DEPTH,) + layer_shard[n])) for t in outs]

    small_shapes = [VECTOR_SHAPES[n] for n in VECTORS] + [CONV_W_FULL]
    stacked = [jnp.stack([small_g[l][n] for l in range(DEPTH)]) for n in VECTORS + ("conv_w",)]
    sparts = all_gather([_pack(stacked, SMALL_ROWS, 128)], "gather_vector_grads")[0]
    col0 = me * (D_CONV // N_DEV)
    place = lambda t: lax.dynamic_update_slice(jnp.zeros(CONV_W_FULL, F32), t, (0, 0, col0))
    spacked = [_pack([d[n] for n in VECTORS] + [place(d["conv_w"])], SMALL_ROWS, 128)[None] for d in (w, m, v)]
    small = [_unpack(t[0], small_shapes) for t in adamw_sum(sparts[None], *spacked, "adamw_vectors")]

    def result(kind):
        out = {n: big[n][kind] for n in MATRICES}
        out.update({n: small[kind][j] for j, n in enumerate(VECTORS)})
        out["conv_w"] = lax.dynamic_slice(small[kind][len(VECTORS)], (0, 0, col0), conv_w.shape)
        return [out[n] for n in w]

    return (loss, dx[None], *result(0), *result(1), *result(2), *result(3))
```

```python
import functools

import jax
import jax.numpy as jnp
from jax import lax
from jax.experimental import pallas as pl
from jax.experimental.pallas import tpu as pltpu

F32 = jnp.float32
BF16 = jnp.bfloat16

N_DEV = 8
DEPTH = 4
D_MODEL = 1024
HEAD_DIM = 64
HEADS = 8
D_ATTN = 512
D_CONV = 256
D_POOL = 256
D_FF = 2816
D_IN = 5640
EPS = 1e-6
ATTN_SCALE = HEAD_DIM ** -0.5

N_REST = 4096
N_MAIN = 5632
N_FULL = 5760
DPROJ_TAIL = 2048
DPROJ_COLS = N_REST + DPROJ_TAIL
FF_BLK = 256
N_FF_BLKS = D_FF // FF_BLK
HALO = 16

ADAM_LR = 0.001
ADAM_B1 = 0.9
ADAM_B2 = 0.999
ADAM_EPS = 1e-08
ADAM_WD = 0.01
ADAM_STEP = 10

SMALL_ROWS = 128

VMEM_LIMIT = 48 * 2 ** 20


def _cparams(sem, vmem=None):
    return pltpu.CompilerParams(dimension_semantics=sem, vmem_limit_bytes=vmem or VMEM_LIMIT)


def _pick(n, cands):
    for c in cands:
        if n % c == 0:
            return c
    raise ValueError(f"no tile for {n}")


def _tile(n, cap):
    t = min(cap, n)
    assert n % t == 0, (n, cap)
    return t


def _sigmoid(v):
    return 1.0 / (1.0 + jnp.exp(-v))


def _rstd(v):
    return lax.rsqrt(jnp.mean(v * v, axis=-1, keepdims=True) + EPS)


def _dot(a, b):
    return jnp.dot(a, b, preferred_element_type=F32)


def _dot_tn(a, b):
    return lax.dot_general(a, b, (((0,), (0,)), ((), ())), preferred_element_type=F32)


def _dot_nt(a, b):
    return lax.dot_general(a, b, (((1,), (1,)), ((), ())), preferred_element_type=F32)


def norm_matmul(x, g, wt, n_cols, name, ex=None):
    s, d = x.shape
    tm, tn = _tile(s, 1024), _pick(n_cols, (2816, 1408, 512))

    def body(x_ref, g_ref, w_ref, o_ref, h_ref):
        @pl.when(pl.program_id(1) == 0)
        def _():
            xv = x_ref[...]
            h_ref[...] = (xv * _rstd(xv) * g_ref[...]).astype(BF16)

        o_ref[...] = _dot_nt(h_ref[...], w_ref[...]).astype(BF16)

    return _carried_call(
        body, ex, (s // tm, n_cols // tn),
        [pl.BlockSpec((tm, d), lambda i, j: (i, 0)), pl.BlockSpec((1, d), lambda i, j: (0, 0)),
         pl.BlockSpec((tn, d), lambda i, j: (j, 0))],
        [pl.BlockSpec((tm, tn), lambda i, j: (i, j)), pl.BlockSpec((tm, d), lambda i, j: (i, 0))],
        [jax.ShapeDtypeStruct((s, n_cols), BF16), jax.ShapeDtypeStruct((s, d), BF16)], [],
        ("arbitrary", "arbitrary"), name, (x, g, wt))


def tn_matmul(a, b, name, m_cols=None, ex=None):
    t = a.shape[0]
    m = m_cols or a.shape[1]
    n = b.shape[1]
    tk = _tile(t, 1024)
    tmm = _pick(m, (1408, 1152, 1024, 512, 256))
    tn = _pick(n, (1408, 1152, 1024, 512, 128))
    nk = t // tk

    def body(a_ref, b_ref, o_ref, acc_ref):
        @pl.when(pl.program_id(2) == 0)
        def _():
            acc_ref[...] = jnp.zeros_like(acc_ref)

        acc_ref[...] += _dot_tn(a_ref[...].astype(BF16), b_ref[...].astype(BF16))

        @pl.when(pl.program_id(2) == nk - 1)
        def _():
            o_ref[...] = acc_ref[...].astype(BF16)

    if ex is None:
        return pl.pallas_call(
            body, grid=(m // tmm, n // tn, nk),
            in_specs=[pl.BlockSpec((tk, tmm), lambda i, j, k: (k, i)), pl.BlockSpec((tk, tn), lambda i, j, k: (k, j))],
            out_specs=pl.BlockSpec((tmm, tn), lambda i, j, k: (i, j)),
            out_shape=jax.ShapeDtypeStruct((m, n), BF16), scratch_shapes=[pltpu.VMEM((tmm, tn), F32)],
            compiler_params=_cparams(("parallel", "parallel", "arbitrary")), name=name)(a, b)
    (out,), carried = _carried_call(
        body, ex, (m // tmm, n // tn, nk),
        [pl.BlockSpec((tk, tmm), lambda i, j, k: (k, i)), pl.BlockSpec((tk, tn), lambda i, j, k: (k, j))],
        [pl.BlockSpec((tmm, tn), lambda i, j, k: (i, j))], [jax.ShapeDtypeStruct((m, n), BF16)],
        [pltpu.VMEM((tmm, tn), F32)], ("arbitrary", "arbitrary", "arbitrary"), name, (a, b))
    return out, carried


def tn_matmuls(pairs, name):
    t = pairs[0][0].shape[0]
    tk = _tile(t, 1024)
    nk = t // tk
    n = len(pairs)

    def body(*refs):
        ins, outs, accs = refs[:2 * n], refs[2 * n:3 * n], refs[3 * n:]

        @pl.when(pl.program_id(0) == 0)
        def _():
            for acc in accs:
                acc[...] = jnp.zeros_like(acc)

        for i in range(n):
            accs[i][...] += _dot_tn(ins[2 * i][...], ins[2 * i + 1][...])

        @pl.when(pl.program_id(0) == nk - 1)
        def _():
            for out, acc in zip(outs, accs):
                out[...] = acc[...].astype(BF16)

    shapes = [(a.shape[1], b.shape[1]) for a, b in pairs]
    return pl.pallas_call(
        body, grid=(nk,),
        in_specs=[pl.BlockSpec((tk, t_.shape[1]), lambda k: (k, 0)) for pair in pairs for t_ in pair],
        out_specs=[pl.BlockSpec(shp, lambda k: (0, 0)) for shp in shapes],
        out_shape=[jax.ShapeDtypeStruct(shp, BF16) for shp in shapes],
        scratch_shapes=[pltpu.VMEM(shp, F32) for shp in shapes],
        compiler_params=_cparams(("arbitrary",)), name=name)(*[t_ for pair in pairs for t_ in pair])


def matmul_normbwd(a, wt, x, g, dres, name, k=None, ex=None):
    s = a.shape[0]
    k = k or a.shape[1]
    d = wt.shape[1]
    tm = _tile(s, 1024)
    tk = _pick(k, (1408, 1152, 512))
    nk = k // tk

    def body(a_ref, w_ref, x_ref, g_ref, r_ref, dx_ref, dg_ref, acc_ref):
        i, kk = pl.program_id(0), pl.program_id(1)

        @pl.when(kk == 0)
        def _():
            acc_ref[...] = jnp.zeros_like(acc_ref)

        @pl.when((i == 0) & (kk == 0))
        def _():
            dg_ref[...] = jnp.zeros_like(dg_ref)

        acc_ref[...] += _dot(a_ref[...], w_ref[...])

        @pl.when(kk == nk - 1)
        def _():
            xv = x_ref[...]
            r = _rstd(xv)
            y = xv * r
            dh = acc_ref[...]
            dy = dh * g_ref[...]
            dx_ref[...] = r_ref[...] + r * (dy - y * jnp.mean(dy * y, axis=-1, keepdims=True))
            dg_ref[...] += jnp.sum(dh * y, axis=0, keepdims=True)

    return _carried_call(
        body, ex, (s // tm, nk),
        [pl.BlockSpec((tm, tk), lambda i, kk: (i, kk)), pl.BlockSpec((tk, d), lambda i, kk: (kk, 0)),
         pl.BlockSpec((tm, d), lambda i, kk: (i, 0)), pl.BlockSpec((1, d), lambda i, kk: (0, 0)),
         pl.BlockSpec((tm, d), lambda i, kk: (i, 0))],
        [pl.BlockSpec((tm, d), lambda i, kk: (i, 0)), pl.BlockSpec((1, d), lambda i, kk: (0, 0))],
        [jax.ShapeDtypeStruct((s, d), F32), jax.ShapeDtypeStruct((1, d), F32)],
        [pltpu.VMEM((tm, d), F32)], ("arbitrary", "arbitrary"), name, (a, wt, x, g, dres), vmem=56 * 2 ** 20)


def swiglu_matmul(gu, w, x1, name):
    s = gu.shape[0]
    d = w.shape[1]
    tm = _tile(s, 512)

    def body(gu_ref, w_ref, x_ref, o_ref):
        acc = x_ref[...]
        for j in range(N_FF_BLKS):
            gt = gu_ref[:, j * FF_BLK:(j + 1) * FF_BLK].astype(F32)
            up = gu_ref[:, D_FF + j * FF_BLK:D_FF + (j + 1) * FF_BLK].astype(F32)
            act = (gt * _sigmoid(gt) * up).astype(BF16)
            acc += _dot(act, w_ref[j * FF_BLK:(j + 1) * FF_BLK, :])
        o_ref[...] = acc

    return pl.pallas_call(
        body, grid=(s // tm,),
        in_specs=[pl.BlockSpec((tm, 2 * D_FF), lambda i: (i, 0)), pl.BlockSpec((D_FF, d), lambda i: (0, 0)),
                  pl.BlockSpec((tm, d), lambda i: (i, 0))],
        out_specs=pl.BlockSpec((tm, d), lambda i: (i, 0)),
        out_shape=jax.ShapeDtypeStruct((s, d), F32),
        compiler_params=_cparams(("parallel",)), name=name)(gu, w, x1)


def swiglu_bwd(dx2, gu, w, name, ex=None):
    s, d = dx2.shape
    tm = _tile(s, 512)

    def body(dx_ref, gu_ref, w_ref, dgu_ref, act_ref):
        dx = dx_ref[...].astype(BF16)
        for j in range(N_FF_BLKS):
            g_cols = slice(j * FF_BLK, (j + 1) * FF_BLK)
            u_cols = slice(D_FF + j * FF_BLK, D_FF + (j + 1) * FF_BLK)
            dact = _dot_nt(dx, w_ref[j * FF_BLK:(j + 1) * FF_BLK, :])
            gt = gu_ref[:, g_cols].astype(F32)
            up = gu_ref[:, u_cols].astype(F32)
            sg = _sigmoid(gt)
            silu = gt * sg
            act_ref[:, j * FF_BLK:(j + 1) * FF_BLK] = (silu * up).astype(BF16)
            dgu_ref[:, g_cols] = (dact * up * (sg + silu * (1.0 - sg))).astype(BF16)
            dgu_ref[:, u_cols] = (dact * silu).astype(BF16)

    return _carried_call(
        body, ex, (s // tm,),
        [pl.BlockSpec((tm, d), lambda i: (i, 0)), pl.BlockSpec((tm, 2 * D_FF), lambda i: (i, 0)),
         pl.BlockSpec((D_FF, d), lambda i: (0, 0), pipeline_mode=pl.Buffered(1))],
        [pl.BlockSpec((tm, 2 * D_FF), lambda i: (i, 0)), pl.BlockSpec((tm, D_FF), lambda i: (i, 0))],
        [jax.ShapeDtypeStruct((s, 2 * D_FF), BF16), jax.ShapeDtypeStruct((s, D_FF), BF16)], [],
        ("arbitrary",), name, (dx2, gu, w), vmem=56 * 2 ** 20)


def loss_kernel(y, tgt, name):
    s, d = y.shape
    tm = _tile(s, 512)

    def body(y_ref, t_ref, l_ref, dy_ref):
        @pl.when(pl.program_id(0) == 0)
        def _():
            l_ref[...] = jnp.zeros_like(l_ref)

        err = y_ref[...] - t_ref[...]
        dy_ref[...] = err * (1.0 / d)
        l_ref[...] += jnp.sum(jnp.sum(err * err, axis=1, keepdims=True), axis=0, keepdims=True)

    return pl.pallas_call(
        body, grid=(s // tm,),
        in_specs=[pl.BlockSpec((tm, d), lambda i: (i, 0)), pl.BlockSpec((tm, d), lambda i: (i, 0))],
        out_specs=[pl.BlockSpec((8, 128), lambda i: (0, 0)), pl.BlockSpec((tm, d), lambda i: (i, 0))],
        out_shape=[jax.ShapeDtypeStruct((8, 128), F32), jax.ShapeDtypeStruct((s, d), F32)],
        compiler_params=_cparams(("arbitrary",)), name=name)(y, tgt)


def _split3(v):
    a1 = v.astype(BF16)
    r1 = v - a1.astype(F32)
    a2 = r1.astype(BF16)
    a3 = (r1 - a2.astype(F32)).astype(BF16)
    return a1, a2, a3


def forget_fwd(h, wt_in, b, name):
    s, d = h.shape
    tm = _tile(s, 512)

    def body(h_ref, w_ref, b_ref, z_ref, c_ref, carry_ref):
        @pl.when(pl.program_id(0) == 0)
        def _():
            carry_ref[...] = jnp.zeros_like(carry_ref)

        z = _dot_nt(h_ref[...], w_ref[...]) + b_ref[...]
        z_ref[...] = z
        logf = jnp.minimum(z, 0.0) - jnp.log(1.0 + jnp.exp(-jnp.abs(z)))
        row = lax.broadcasted_iota(jnp.int32, (tm, tm), 0)
        col = lax.broadcasted_iota(jnp.int32, (tm, tm), 1)
        tri = (row >= col).astype(BF16)
        a1, a2, a3 = _split3(logf)
        c = _dot(tri, a1) + _dot(tri, a2) + _dot(tri, a3) + carry_ref[...]
        c_ref[...] = c
        carry_ref[...] = c[tm - 1:tm, :]

    return pl.pallas_call(
        body, grid=(s // tm,),
        in_specs=[pl.BlockSpec((tm, d), lambda i: (i, 0)), pl.BlockSpec((128, d), lambda i: (N_MAIN // 128, 0)),
                  pl.BlockSpec((1, 128), lambda i: (0, 0))],
        out_specs=[pl.BlockSpec((tm, 128), lambda i: (i, 0)), pl.BlockSpec((tm, 128), lambda i: (i, 0))],
        out_shape=[jax.ShapeDtypeStruct((s, 128), F32), jax.ShapeDtypeStruct((s, 128), F32)],
        scratch_shapes=[pltpu.VMEM((1, 128), F32)],
        compiler_params=_cparams(("arbitrary",)), name=name)(h, wt_in, b)


def forget_bwd(dc, z, dproj, name):
    s = dc.shape[0]
    tm = _tile(s, 512)
    nt = s // tm

    def body(dc_ref, z_ref, dp_ref, dz_ref, db_ref, carry_ref):
        @pl.when(pl.program_id(0) == 0)
        def _():
            carry_ref[...] = jnp.zeros_like(carry_ref)
            db_ref[...] = jnp.zeros_like(db_ref)

        row = lax.broadcasted_iota(jnp.int32, (tm, tm), 0)
        col = lax.broadcasted_iota(jnp.int32, (tm, tm), 1)
        tri = (col >= row).astype(BF16)
        a1, a2, a3 = _split3(dc_ref[...])
        dlogf = _dot(tri, a1) + _dot(tri, a2) + _dot(tri, a3) + carry_ref[...]
        carry_ref[...] = dlogf[0:1, :]
        dz = dlogf * (1.0 - _sigmoid(z_ref[...]))
        dz_ref[...] = dz.astype(BF16)
        db_ref[...] += jnp.sum(dz, axis=0, keepdims=True)

    return pl.pallas_call(
        body, grid=(nt,),
        in_specs=[pl.BlockSpec((tm, 128), lambda i: (nt - 1 - i, 0)), pl.BlockSpec((tm, 128), lambda i: (nt - 1 - i, 0)),
                  pl.BlockSpec(memory_space=pl.ANY)],
        out_specs=[pl.BlockSpec((tm, 128), lambda i: (nt - 1 - i, N_MAIN // 128)), pl.BlockSpec((1, 128), lambda i: (0, 0))],
        out_shape=[jax.ShapeDtypeStruct(dproj.shape, BF16), jax.ShapeDtypeStruct((1, 128), F32)],
        scratch_shapes=[pltpu.VMEM((1, 128), F32)], input_output_aliases={2: 0},
        compiler_params=_cparams(("arbitrary",)), name=name)(dc, z, dproj)


HEAD_GROUP = 4
LANE_C = 64
LANE_ONE = 67


def _lanes():
    lane = lax.broadcasted_iota(jnp.int32, (1, 128), 1)
    return lane, lane < HEAD_DIM


def _half_mean(t, lo):
    s_lo = jnp.sum(jnp.where(lo, t, 0.0), axis=-1, keepdims=True)
    s_hi = jnp.sum(jnp.where(lo, 0.0, t), axis=-1, keepdims=True)
    return jnp.where(lo, s_lo, s_hi) * (1.0 / HEAD_DIM)


def _lane_col(t, lane, idx):
    return jnp.sum(jnp.where(lane == idx, t, 0.0), axis=-1, keepdims=True)


def _swap_halves(t):
    return pltpu.roll(t, HEAD_DIM, 1)


def attn_prep(proj, c, gq2, gk2, name):
    s = proj.shape[0]
    tm = _tile(s, 512)
    first = N_REST // 128

    def body(q_ref, k_ref, v_ref, c_ref, gq_ref, gk_ref, qa_ref, ka_ref, va_ref, vt_ref):
        j = pl.program_id(1)
        lane, lo = _lanes()

        def normed(ref, g):
            t = ref[...].astype(F32)
            return t * lax.rsqrt(_half_mean(t * t, lo) + EPS) * g

        qn = normed(q_ref, gq_ref[...] * ATTN_SCALE)
        kn = normed(k_ref, gk_ref[...])
        vv = v_ref[...].astype(F32)
        cv = c_ref[...]
        one_q = jnp.where((lane >= LANE_ONE) & (lane < LANE_ONE + 3), 1.0, 0.0)
        one_k = jnp.where((lane >= LANE_C) & (lane < LANE_C + 3), 1.0, 0.0)
        one_v = jnp.where(lane == LANE_C, 1.0, 0.0)
        for e in range(2):
            pick = (lambda t: t) if e == 0 else _swap_halves
            pieces = [p.astype(F32) for p in _split3(_lane_col(cv, lane, 2 * j + e))]
            ext_q, ext_k = one_q, one_k
            for i, p in enumerate(pieces):
                ext_q = jnp.where(lane == LANE_C + i, p, ext_q)
                ext_k = jnp.where(lane == LANE_ONE + i, -p, ext_k)
            qa_ref[e] = jnp.where(lo, pick(qn), ext_q).astype(BF16)
            ka_ref[e] = jnp.where(lo, pick(kn), ext_k).astype(BF16)
            va = jnp.where(lo, pick(vv), one_v)
            va_ref[e] = va.astype(BF16)
            vt_ref[e] = va.T.astype(BF16)

    tile = lambda base: pl.BlockSpec((tm, 128), lambda i, j: (i, base + j))
    vec = pl.BlockSpec((1, 128), lambda i, j: (0, 0))
    out = pl.BlockSpec((2, tm, 128), lambda i, j: (j, i, 0))
    return pl.pallas_call(
        body, grid=(s // tm, HEADS // 2),
        in_specs=[tile(first), tile(first + 4), tile(first + 8), pl.BlockSpec((tm, 128), lambda i, j: (i, 0)), vec, vec],
        out_specs=[out, out, out, pl.BlockSpec((2, 128, tm), lambda i, j: (j, 0, i))],
        out_shape=[jax.ShapeDtypeStruct((HEADS, s, 128), BF16)] * 3 + [jax.ShapeDtypeStruct((HEADS, 128, s), BF16)],
        compiler_params=_cparams(("parallel", "arbitrary")), name=name)(proj, proj, proj, c, gq2, gk2)


def _carry(ex, n_in, n_out, n_scratch, grid):
    n_xin, n_xout = (len(ex.inputs), len(ex.out_shapes)) if ex else (0, 0)

    def split(refs):
        ins, xins = refs[:n_in], refs[n_in:n_in + n_xin]
        rest = refs[n_in + n_xin:]
        outs, xouts = rest[:n_out], rest[n_out:n_out + n_xout]
        rest = rest[n_out + n_xout:]
        return ins + outs + rest[:n_scratch], (xins, xouts, rest[n_scratch:])

    def first():
        return functools.reduce(lambda a, b: a & b, [pl.program_id(d) == 0 for d in range(len(grid))])

    def last():
        return functools.reduce(lambda a, b: a & b, [pl.program_id(d) == grid[d] - 1 for d in range(len(grid))])

    return split, first, last


def _carried_call(body, ex, grid, in_specs, out_specs, out_shape, scratch, sem, name, operands, vmem=None):
    any_spec = pl.BlockSpec(memory_space=pl.ANY)
    split, first, last = _carry(ex, len(in_specs), len(out_specs), len(scratch), grid)

    def carried(*refs):
        own, xrefs = split(refs)
        if ex:
            @pl.when(first())
            def _():
                ex.start(*xrefs)

        body(*own)
        if ex:
            @pl.when(last())
            def _():
                ex.drain(*xrefs)

    n_xin = len(ex.inputs) if ex else 0
    results = pl.pallas_call(
        carried, grid=grid, in_specs=list(in_specs) + [any_spec] * n_xin,
        out_specs=list(out_specs) + [any_spec] * (len(ex.out_shapes) if ex else 0),
        out_shape=list(out_shape) + (list(ex.out_shapes) if ex else []),
        input_output_aliases={len(in_specs) + i: len(out_specs) + o for i, o in ex.aliases.items()} if ex else {},
        scratch_shapes=list(scratch) + (ex.scratch if ex else []),
        compiler_params=_cparams(sem, vmem), name=name)(*operands, *(ex.inputs if ex else []))
    return results[:len(out_specs)], results[len(out_specs):]


def _tri_rows(t, n):
    qi = sum(jnp.where(t >= r * (r + 1) // 2, 1, 0) for r in range(1, n))
    return qi, t - qi * (qi + 1) // 2


def _tri_cols(t, n):
    ki = sum(jnp.where(t >= r * n - r * (r - 1) // 2, 1, 0) for r in range(1, n))
    return ki, ki + t - (ki * n - ki * (ki - 1) // 2)


def _causal_t(st_blk, tk, tq):
    key = lax.broadcasted_iota(jnp.int32, (tk, tq), 0)
    qry = lax.broadcasted_iota(jnp.int32, (tk, tq), 1)
    return jnp.where(qry >= key, st_blk, -jnp.inf)


def attn_forward(qa, ka, vt, name, ex=None):
    hh, s, _ = qa.shape
    tq = tk = _tile(s, 512)
    nq = s // tq
    grp = HEAD_GROUP

    def body(q_ref, k_ref, vt_ref, o_ref, lse_ref, m_ref, acc_ref):
        qi, ki = _tri_rows(pl.program_id(1), nq)

        @pl.when(ki == 0)
        def _():
            m_ref[...] = jnp.full_like(m_ref, -jnp.inf)
            acc_ref[...] = jnp.zeros_like(acc_ref)

        def step(masked):
            nxt = _dot_nt(k_ref[0], q_ref[0])
            for g in range(grp):
                st = nxt
                if g + 1 < grp:
                    nxt = _dot_nt(k_ref[g + 1], q_ref[g + 1])
                if masked:
                    st = _causal_t(st, tk, tq)
                m_old = m_ref[g]
                m_new = jnp.maximum(m_old, jnp.max(st, axis=0, keepdims=True))
                pt = jnp.exp(st - m_new).astype(BF16)
                acc_ref[g] = jnp.exp(m_old - m_new) * acc_ref[g] + _dot(vt_ref[g], pt)
                m_ref[g] = m_new

        @pl.when(ki < qi)
        def _():
            step(False)

        @pl.when(ki == qi)
        def _():
            step(True)
            for g in range(grp):
                acc = acc_ref[g]
                denom = acc[LANE_C:LANE_C + 1, :]
                o_ref[g] = (acc / denom).T.astype(BF16)
                lse_ref[g] = m_ref[g] + jnp.log(denom)

    qspec = pl.BlockSpec((grp, tq, 128), lambda h, t: (h, _tri_rows(t, nq)[0], 0))
    kspec = pl.BlockSpec((grp, tk, 128), lambda h, t: (h, _tri_rows(t, nq)[1], 0))
    vspec = pl.BlockSpec((grp, 128, tk), lambda h, t: (h, 0, _tri_rows(t, nq)[1]))
    lspec = pl.BlockSpec((grp, 1, tq), lambda h, t: (h, 0, _tri_rows(t, nq)[0]))
    return _carried_call(
        body, ex, (hh // grp, nq * (nq + 1) // 2), [qspec, kspec, vspec], [qspec, lspec],
        [jax.ShapeDtypeStruct((hh, s, 128), BF16), jax.ShapeDtypeStruct((hh, 1, s), F32)],
        [pltpu.VMEM((grp, 1, tq), F32), pltpu.VMEM((grp, 128, tq), F32)],
        ("arbitrary", "arbitrary"), name, (qa, ka, vt))


def attn_backward(qa, ka, va, oa, doa, lse, name, ex=None):
    hh, s, _ = qa.shape
    tq = tk = _tile(s, 512)
    nq = s // tq
    grp = HEAD_GROUP

    def body(q_ref, k_ref, v_ref, o_ref, do_ref, lse_ref, dq_ref, dk_ref, dv_ref, dka_ref, dva_ref):
        ki, qi = _tri_cols(pl.program_id(1), nq)

        @pl.when(pl.program_id(1) == 0)
        def _():
            dq_ref[...] = jnp.zeros_like(dq_ref)

        @pl.when(qi == ki)
        def _():
            dka_ref[...] = jnp.zeros_like(dka_ref)
            dva_ref[...] = jnp.zeros_like(dva_ref)

        def step(masked):
            rows = pl.ds(pl.multiple_of(qi * tq, tq), tq)
            products = lambda g: (_dot_nt(k_ref[g], q_ref[g]), _dot_nt(v_ref[g], do_ref[g]))
            nxt = products(0)
            for g in range(grp):
                st, dpt = nxt
                if g + 1 < grp:
                    nxt = products(g + 1)
                q, k, do = q_ref[g], k_ref[g], do_ref[g]
                if masked:
                    st = _causal_t(st, tk, tq)
                pt = jnp.exp(st - lse_ref[g])
                delta = jnp.sum((do.astype(F32) * o_ref[g].astype(F32)).T, axis=0, keepdims=True)
                dst = (pt * (dpt - delta)).astype(BF16)
                dva_ref[g] += _dot(pt.astype(BF16), do)
                dka_ref[g] += _dot(dst, q)
                dq_ref[g, rows, :] += _dot_tn(dst, k)

        @pl.when(qi > ki)
        def _():
            step(False)

        @pl.when(qi == ki)
        def _():
            step(True)

        @pl.when(qi == nq - 1)
        def _():
            dk_ref[...] = dka_ref[...]
            dv_ref[...] = dva_ref[...].astype(BF16)

    qspec = pl.BlockSpec((grp, tq, 128), lambda h, t: (h, _tri_cols(t, nq)[1], 0))
    lspec = pl.BlockSpec((grp, 1, tq), lambda h, t: (h, 0, _tri_cols(t, nq)[1]))
    kspec = pl.BlockSpec((grp, tk, 128), lambda h, t: (h, _tri_cols(t, nq)[0], 0))
    return _carried_call(
        body, ex, (hh // grp, nq * (nq + 1) // 2), [qspec, kspec, kspec, qspec, qspec, lspec],
        [pl.BlockSpec((grp, s, 128), lambda h, t: (h, 0, 0)), kspec, kspec],
        [jax.ShapeDtypeStruct((hh, s, 128), F32), jax.ShapeDtypeStruct((hh, s, 128), F32),
         jax.ShapeDtypeStruct((hh, s, 128), BF16)],
        [pltpu.VMEM((grp, tk, 128), F32), pltpu.VMEM((grp, tk, 128), F32)],
        ("arbitrary", "arbitrary"), name, (qa, ka, va, oa, doa, lse))


def attn_post(dqa, dka, dva, proj, gq2, gk2, dproj, name):
    s = proj.shape[0]
    tm = _tile(s, 256)

    def body(dq_ref, dk_ref, dv_ref, q_ref, k_ref, gq_ref, gk_ref, dp_any, dp_ref, dc_ref, dgq_ref, dgk_ref):
        lane, lo = _lanes()

        @pl.when(pl.program_id(0) == 0)
        def _():
            dgq_ref[...] = jnp.zeros_like(dgq_ref)
            dgk_ref[...] = jnp.zeros_like(dgk_ref)

        def pair(ref, j):
            return jnp.where(lo, ref[2 * j].astype(F32), _swap_halves(ref[2 * j + 1].astype(F32)))

        def norm_bwd(raw, g, dhat, scale):
            r = lax.rsqrt(_half_mean(raw * raw, lo) + EPS)
            y = raw * r
            dy = dhat * (g * scale)
            return r * (dy - y * _half_mean(dy * y, lo)), jnp.sum(dhat * y, axis=0, keepdims=True) * scale

        dc = jnp.zeros((tm, 128), F32)
        for j in range(HEADS // 2):
            cols = slice(128 * j, 128 * (j + 1))
            dq, dgq = norm_bwd(q_ref[:, cols].astype(F32), gq_ref[...], pair(dq_ref, j), ATTN_SCALE)
            dk, dgk = norm_bwd(k_ref[:, cols].astype(F32), gk_ref[...], pair(dk_ref, j), 1.0)
            dgq_ref[...] += dgq
            dgk_ref[...] += dgk
            dp_ref[:, cols] = dq.astype(BF16)
            dp_ref[:, D_ATTN + 128 * j:D_ATTN + 128 * (j + 1)] = dk.astype(BF16)
            dp_ref[:, 2 * D_ATTN + 128 * j:2 * D_ATTN + 128 * (j + 1)] = pair(dv_ref, j).astype(BF16)
            for e in range(2):
                h = 2 * j + e
                both = jnp.where(lane == LANE_C, dq_ref[h], 0.0) - jnp.where(lane == LANE_ONE, dk_ref[h], 0.0)
                dc = jnp.where(lane == h, jnp.sum(both, axis=-1, keepdims=True), dc)
        dp_ref[:, 3 * D_ATTN:] = jnp.zeros((tm, DPROJ_TAIL - 3 * D_ATTN), BF16)
        dc_ref[...] = dc

    heads = lambda: pl.BlockSpec((HEADS, tm, 128), lambda i: (0, i, 0))
    vec = pl.BlockSpec((1, 128), lambda i: (0, 0))
    first = N_REST // D_ATTN
    return pl.pallas_call(
        body, grid=(s // tm,),
        in_specs=[heads(), heads(), heads(), pl.BlockSpec((tm, D_ATTN), lambda i: (i, first)),
                  pl.BlockSpec((tm, D_ATTN), lambda i: (i, first + 1)), vec, vec, pl.BlockSpec(memory_space=pl.ANY)],
        out_specs=[pl.BlockSpec((tm, DPROJ_TAIL), lambda i: (i, N_REST // DPROJ_TAIL)),
                   pl.BlockSpec((tm, 128), lambda i: (i, 0)), vec, vec],
        out_shape=[jax.ShapeDtypeStruct(dproj.shape, BF16), jax.ShapeDtypeStruct((s, 128), F32),
                   jax.ShapeDtypeStruct((1, 128), F32), jax.ShapeDtypeStruct((1, 128), F32)],
        input_output_aliases={7: 0},
        compiler_params=_cparams(("arbitrary",)), name=name)(dqa, dka, dva, proj, proj, gq2, gk2, dproj)


def _pool_groups(tm):
    gid = lax.broadcasted_iota(jnp.int32, (1, D_POOL), 1) // (D_POOL // 4)
    win = jnp.where(gid == 0, 2.0, jnp.where(gid == 1, 4.0, jnp.where(gid == 2, 8.0, 16.0)))
    return gid, win


def _by_group(gid, v2, v4, v8, v16):
    return jnp.where(gid == 0, v2, jnp.where(gid == 1, v4, jnp.where(gid == 2, v8, v16)))


def _branches(rest_ref, halo_ref, a_ref, wa_ref, wc_ref, wp_ref, sc_ref, cw_ref, ti, tm):
    f = lambda v: v.astype(F32)
    cx, cb, cc, px = f(rest_ref[:, 0:256]), f(rest_ref[:, 256:512]), f(rest_ref[:, 512:768]), f(rest_ref[:, 768:1024])
    live = jnp.where(ti > 0, 1.0, 0.0)
    hz = f(halo_ref[:, 0:256]) * f(halo_ref[:, 512:768]) * live
    hp = f(halo_ref[:, 768:1024]) * live
    z = cc * cx
    zf = jnp.concatenate([hz, z], axis=0)
    z1 = pltpu.roll(zf, 1, 0)[HALO:]
    z2 = pltpu.roll(zf, 2, 0)[HALO:]
    cw = cw_ref[...]
    conv = cw[2:3] * z + cw[1:2] * z1 + cw[0:1] * z2
    uc = cb * conv
    pf = jnp.concatenate([hp, px], axis=0)
    s2 = pf + pltpu.roll(pf, 1, 0)
    s4 = s2 + pltpu.roll(s2, 2, 0)
    s8 = s4 + pltpu.roll(s4, 4, 0)
    s16 = s8 + pltpu.roll(s8, 8, 0)
    gid, win = _pool_groups(tm)
    t = (ti * tm + lax.broadcasted_iota(jnp.int32, (tm, 1), 0)).astype(F32)
    inv = 1.0 / jnp.minimum(t + 1.0, win)
    dpool = _by_group(gid, s2[HALO:], s4[HALO:], s8[HALO:], s16[HALO:]) * inv - px
    _, lo = _lanes()
    a_tok = [jnp.where(lo, f(a_ref[2 * j]), _swap_halves(f(a_ref[2 * j + 1]))).astype(BF16) for j in range(HEADS // 2)]
    y_attn = _dot(a_tok[0], wa_ref[0:128, :])
    for j in range(1, HEADS // 2):
        y_attn += _dot(a_tok[j], wa_ref[128 * j:128 * (j + 1), :])
    y_conv = _dot(uc.astype(BF16), wc_ref[...])
    y_pool_raw = _dot(dpool.astype(BF16), wp_ref[...])
    sg = [_sigmoid(f(rest_ref[:, 1024 + i * D_MODEL:1024 + (i + 1) * D_MODEL])) for i in range(3)]
    return dict(cx=cx, cb=cb, cc=cc, z=z, z1=z1, z2=z2, conv=conv, uc=uc, dpool=dpool, inv=inv, gid=gid, a_tok=a_tok,
                y_attn=y_attn, y_conv=y_conv, y_pool_raw=y_pool_raw, sg=sg, cw=cw)


def _mix_specs(tm, ti_of):
    blocks_per_tile = tm // HALO
    return [
        pl.BlockSpec((tm, N_REST), lambda i: (ti_of(i), 0)),
        pl.BlockSpec((HALO, 1024), lambda i: (jnp.maximum(ti_of(i) * blocks_per_tile - 1, 0), 0)),
        pl.BlockSpec((HEADS, tm, 128), lambda i: (0, ti_of(i), 0)),
        pl.BlockSpec((D_ATTN, D_MODEL), lambda i: (0, 0)),
        pl.BlockSpec((D_CONV, D_MODEL), lambda i: (0, 0)),
        pl.BlockSpec((D_POOL, D_MODEL), lambda i: (0, 0)),
        pl.BlockSpec((1, D_MODEL), lambda i: (0, 0)),
        pl.BlockSpec((8, D_CONV), lambda i: (0, 0)),
    ]


def mix_fwd(proj, a, x, wa, wc, wp, scale, cw, wo, name, ex=None):
    s = x.shape[0]
    tm = _tile(s, 256)

    def body(rest_ref, halo_ref, a_ref, wa_ref, wc_ref, wp_ref, sc_ref, cw_ref, wo_ref, x_ref, o_ref):
        b = _branches(rest_ref, halo_ref, a_ref, wa_ref, wc_ref, wp_ref, sc_ref, cw_ref, pl.program_id(0), tm)
        merged = b["sg"][0] * b["y_attn"] + b["sg"][1] * b["y_conv"] + b["sg"][2] * (b["y_pool_raw"] * sc_ref[...])
        o_ref[...] = x_ref[...] + _dot(merged.astype(BF16), wo_ref[...])

    (x1,), carried = _carried_call(
        body, ex, (s // tm,),
        _mix_specs(tm, lambda i: i) + [pl.BlockSpec((D_MODEL, D_MODEL), lambda i: (0, 0)),
                                       pl.BlockSpec((tm, D_MODEL), lambda i: (i, 0))],
        [pl.BlockSpec((tm, D_MODEL), lambda i: (i, 0))], [jax.ShapeDtypeStruct((s, D_MODEL), F32)], [],
        ("arbitrary",), name, (proj, proj, a, wa, wc, wp, scale, cw, wo, x))
    return x1, carried


def mix_bwd(proj, a, dx1, wa, wc, wp, scale, cw, wo, name):
    s = dx1.shape[0]
    tm = _tile(s, 256)
    nt = s // tm
    ti_of = lambda i: nt - 1 - i
    n = tm + HALO

    def body(rest_ref, halo_ref, a_ref, wa_ref, wc_ref, wp_ref, sc_ref, cw_ref, wo_ref,
             dx_ref, dp_ref, da_ref, at_ref, mg_ref, dya_ref, dyc_ref, dyp_ref, uc_ref, dd_ref, dsc_ref, dcw_ref,
             cdc_ref, cde_ref):
        i = pl.program_id(0)
        ti = ti_of(i)

        @pl.when(i == 0)
        def _():
            cdc_ref[...] = jnp.zeros_like(cdc_ref)
            cde_ref[...] = jnp.zeros_like(cde_ref)
            dsc_ref[...] = jnp.zeros_like(dsc_ref)
            dcw_ref[...] = jnp.zeros_like(dcw_ref)

        b = _branches(rest_ref, halo_ref, a_ref, wa_ref, wc_ref, wp_ref, sc_ref, cw_ref, ti, tm)
        sg, sc = b["sg"], sc_ref[...]
        y_pool = b["y_pool_raw"] * sc
        merged = sg[0] * b["y_attn"] + sg[1] * b["y_conv"] + sg[2] * y_pool
        mg_ref[...] = merged.astype(BF16)
        dm = _dot_nt(dx_ref[...].astype(BF16), wo_ref[...])
        dys = [dm * sg[j] for j in range(3)]
        for j, y in enumerate((b["y_attn"], b["y_conv"], y_pool)):
            dp_ref[:, 1024 + j * D_MODEL:1024 + (j + 1) * D_MODEL] = (dys[j] * y * (1.0 - sg[j])).astype(BF16)
        dya = dys[0].astype(BF16)
        dya_ref[...] = dya
        _, lo = _lanes()
        for j in range(HEADS // 2):
            at_ref[:, 128 * j:128 * (j + 1)] = b["a_tok"][j]
            da = _dot_nt(dya, wa_ref[128 * j:128 * (j + 1), :])
            da_ref[2 * j] = jnp.where(lo, da, 0.0).astype(BF16)
            da_ref[2 * j + 1] = jnp.where(lo, _swap_halves(da), 0.0).astype(BF16)
        dyc = dys[1].astype(BF16)
        dyc_ref[...] = dyc
        duc = _dot_nt(dyc, wc_ref[...])
        dyp = dys[2]
        dsc_ref[...] += jnp.sum(dyp * b["y_pool_raw"], axis=0, keepdims=True)
        dypr = (dyp * sc).astype(BF16)
        dyp_ref[...] = dypr
        ddp = _dot_nt(dypr, wp_ref[...])
        uc_ref[...] = b["uc"].astype(BF16)
        dd_ref[...] = b["dpool"].astype(BF16)

        dconv = duc * b["cb"]
        dp_ref[:, 256:512] = (duc * b["conv"]).astype(BF16)
        dcf = jnp.concatenate([dconv, cdc_ref[...]], axis=0)
        cw = b["cw"]
        dz = cw[2:3] * dconv + cw[1:2] * pltpu.roll(dcf, n - 1, 0)[:tm] + cw[0:1] * pltpu.roll(dcf, n - 2, 0)[:tm]
        dp_ref[:, 0:256] = (dz * b["cc"]).astype(BF16)
        dp_ref[:, 512:768] = (dz * b["cx"]).astype(BF16)
        dcw_ref[0:1, :] += jnp.sum(dconv * b["z2"], axis=0, keepdims=True)
        dcw_ref[1:2, :] += jnp.sum(dconv * b["z1"], axis=0, keepdims=True)
        dcw_ref[2:3, :] += jnp.sum(dconv * b["z"], axis=0, keepdims=True)
        cdc_ref[...] = dconv[:HALO]

        e = ddp * b["inv"]
        ef = jnp.concatenate([e, cde_ref[...]], axis=0)
        r2 = ef + pltpu.roll(ef, n - 1, 0)
        r4 = r2 + pltpu.roll(r2, n - 2, 0)
        r8 = r4 + pltpu.roll(r4, n - 4, 0)
        r16 = r8 + pltpu.roll(r8, n - 8, 0)
        dp_ref[:, 768:1024] = (_by_group(b["gid"], r2[:tm], r4[:tm], r8[:tm], r16[:tm]) - ddp).astype(BF16)
        cde_ref[...] = e[:HALO]

    tile = lambda w: pl.BlockSpec((tm, w), lambda i: (ti_of(i), 0))
    whole = lambda r, c: pl.BlockSpec((r, c), lambda i: (0, 0))
    bf = lambda w: jax.ShapeDtypeStruct((s, w), BF16)
    return pl.pallas_call(
        body, grid=(nt,),
        in_specs=_mix_specs(tm, ti_of) + [whole(D_MODEL, D_MODEL), tile(D_MODEL)],
        out_specs=[tile(N_REST), pl.BlockSpec((HEADS, tm, 128), lambda i: (0, ti_of(i), 0)), tile(D_ATTN),
                   tile(D_MODEL), tile(D_MODEL), tile(D_MODEL), tile(D_MODEL),
                   tile(D_CONV), tile(D_POOL), whole(1, D_MODEL), whole(8, D_CONV)],
        out_shape=[bf(DPROJ_COLS), jax.ShapeDtypeStruct((HEADS, s, 128), BF16), bf(D_ATTN),
                   bf(D_MODEL), bf(D_MODEL), bf(D_MODEL), bf(D_MODEL), bf(D_CONV), bf(D_POOL),
                   jax.ShapeDtypeStruct((1, D_MODEL), F32), jax.ShapeDtypeStruct((8, D_CONV), F32)],
        scratch_shapes=[pltpu.VMEM((HALO, D_CONV), F32), pltpu.VMEM((HALO, D_POOL), F32)],
        compiler_params=_cparams(("arbitrary",)), name=name)(proj, proj, a, wa, wc, wp, scale, cw, wo, dx1)


def _adamw_math(w, g, m, v):
    m = ADAM_B1 * m + (1.0 - ADAM_B1) * g
    v = ADAM_B2 * v + (1.0 - ADAM_B2) * (g * g)
    m_hat = m / (1.0 - ADAM_B1 ** ADAM_STEP)
    v_hat = v / (1.0 - ADAM_B2 ** ADAM_STEP)
    delta = -ADAM_LR * (m_hat / (jnp.sqrt(v_hat) + ADAM_EPS) + ADAM_WD * w)
    return delta, m, v


ADAMW_PARTS_BLOCK_BYTES = 4 * 2 ** 20


def _row_tile(rows, cols, copies, itemsize):
    row_bytes = copies * (-(-cols // 128) * 128) * itemsize
    fits = [t for t in range(16, rows + 1, 16) if rows % t == 0 and t * row_bytes <= ADAMW_PARTS_BLOCK_BYTES]
    return max(fits) if fits else rows


def pair_sum(blocks, stage, me, name):
    n_slots, rows, cols = stage.shape
    tr = _row_tile(rows, cols, 1, 4)

    def body(me_ref, a_ref, b_ref, o_ref):
        o_ref[...] = (a_ref[...].astype(F32) + b_ref[...].astype(F32)).astype(BF16)

    slot = pl.BlockSpec((None, tr, cols), lambda i, r, me_ref: (i, r, 0))
    return pl.pallas_call(
        body, out_shape=jax.ShapeDtypeStruct(stage.shape, BF16),
        grid_spec=pltpu.PrefetchScalarGridSpec(
            num_scalar_prefetch=1, grid=(n_slots, rows // tr),
            in_specs=[pl.BlockSpec((None, tr, cols), lambda i, r, me_ref: (me_ref[0] ^ (2 * i), r, 0)), slot],
            out_specs=slot),
        compiler_params=_cparams(("parallel", "parallel")), name=name)(me.reshape(1), blocks, stage)


def adamw_sum(parts, w, m, v, name):
    layers, rows, cols = w.shape
    n_parts = parts.shape[1]
    if rows % 16 == 0:
        tr, tc = _row_tile(rows, cols, n_parts, parts.dtype.itemsize), cols
    else:
        tr, tc = rows, _pick(cols, (256, 128))

    def body(p_ref, w_ref, m_ref, v_ref, g_ref, d_ref, nm_ref, nv_ref):
        g = p_ref[0].astype(F32)
        for i in range(1, n_parts):
            g = g + p_ref[i].astype(F32)
        g_ref[...] = g
        d_ref[...], nm_ref[...], nv_ref[...] = _adamw_math(w_ref[...], g, m_ref[...], v_ref[...])

    spec = pl.BlockSpec((None, tr, tc), lambda l, i, j: (l, i, j))
    return pl.pallas_call(
        body, grid=(layers, rows // tr, cols // tc),
        in_specs=[pl.BlockSpec((None, n_parts, tr, tc), lambda l, i, j: (l, 0, i, j)), spec, spec, spec],
        out_specs=[spec] * 4, out_shape=[jax.ShapeDtypeStruct((layers, rows, cols), F32)] * 4,
        compiler_params=_cparams(("parallel", "parallel", "parallel")), name=name)(parts, w, m, v)


def _me():
    return lax.axis_index("x"), lax.axis_index("y"), lax.axis_index("c")


N_PEERS = N_DEV - 1


def all_gather(shards, name):
    n = len(shards)
    any_spec = pl.BlockSpec(memory_space=pl.ANY)

    def body(*refs):
        x_refs, out_refs = refs[:n], refs[n:2 * n]
        send_sems, recv_sems, local_sems = refs[2 * n:]
        x, y, c = _me()
        me, sibling = (x, y, c), (x, y, 1 - c)
        chips = [(1 - x, y), (x, 1 - y), (1 - x, 1 - y)]

        def copy(t, k, block, to, from_input=False):
            slot = out_refs[t].at[4 * block[0] + 2 * block[1] + block[2]]
            return pltpu.make_async_remote_copy(
                src_ref=x_refs[t] if from_input else slot, dst_ref=slot, send_sem=send_sems.at[N_PEERS * t + k],
                recv_sem=recv_sems.at[N_PEERS * t + k], device_id=to, device_id_type=pl.DeviceIdType.MESH)

        mine = [pltpu.make_async_copy(x_refs[t], out_refs[t].at[4 * x + 2 * y + c], local_sems.at[t]) for t in range(n)]
        started = []
        for t in range(n):
            mine[t].start()
            started.append(copy(t, 0, me, sibling, from_input=True))
            started += [copy(t, 1 + j, me, (*chip, c), from_input=True) for j, chip in enumerate(chips)]
        for cp in started:
            cp.start()
        for j, chip in enumerate(chips):
            for t in range(n):
                copy(t, 1 + j, (*chip, c), me).wait_recv()
                fwd = copy(t, 4 + j, (*chip, c), sibling)
                fwd.start()
                started.append(fwd)
        for t in range(n):
            copy(t, 0, sibling, me).wait_recv()
            for j, chip in enumerate(chips):
                copy(t, 4 + j, (*chip, 1 - c), me).wait_recv()
        for cp in started:
            cp.wait_send()
        for cp in mine:
            cp.wait()

    return pl.pallas_call(
        body, out_shape=[jax.ShapeDtypeStruct((N_DEV,) + s.shape, s.dtype) for s in shards],
        in_specs=[any_spec] * n, out_specs=[any_spec] * n,
        scratch_shapes=[pltpu.SemaphoreType.DMA((N_PEERS * n,)), pltpu.SemaphoreType.DMA((N_PEERS * n,)),
                        pltpu.SemaphoreType.DMA((n,))],
        name=name)(*shards)


SIBLING = 1
OTHER_CHIPS = (2, 4, 6)
SAME_CORE = (0,) + OTHER_CHIPS


class Exchange:
    def __init__(self, inputs, out_shapes, aliases, copies, local=()):
        self.inputs, self.out_shapes, self.aliases = list(inputs), list(out_shapes), aliases
        self._copies, self._local = list(copies), list(local)
        self.scratch = [pltpu.SemaphoreType.DMA((len(self._copies),)), pltpu.SemaphoreType.DMA((len(self._copies),)),
                        pltpu.SemaphoreType.DMA((max(len(self._local), 1),))]

    def _build(self, ins, outs, sems):
        send_sems, recv_sems, local_sems = sems
        x, y, c = _me()
        me = 4 * x + 2 * y + c
        local = [functools.partial(pltpu.make_async_copy, src(ins, outs, me), dst(outs, me), local_sems.at[i])
                 for i, (src, dst) in enumerate(self._local)]
        sends, recvs = [], []
        for i, (mask, src, dst) in enumerate(self._copies):
            px, py, pc = x ^ ((mask >> 2) & 1), y ^ ((mask >> 1) & 1), c ^ (mask & 1)
            pair = dict(send_sem=send_sems.at[i], recv_sem=recv_sems.at[i], device_id_type=pl.DeviceIdType.MESH)
            sends.append(functools.partial(
                pltpu.make_async_remote_copy, src_ref=src(ins, outs, me), dst_ref=dst(outs, me), device_id=(px, py, pc), **pair))
            recvs.append(functools.partial(
                pltpu.make_async_remote_copy, src_ref=src(ins, outs, me), dst_ref=dst(outs, me ^ mask), device_id=(x, y, c), **pair))
        return local, sends, recvs

    def start(self, ins, outs, sems):
        local, sends, _ = self._build(ins, outs, sems)
        for make in local + sends:
            make().start()

    def drain(self, ins, outs, sems):
        local, sends, recvs = self._build(ins, outs, sems)
        for make in recvs:
            make().wait_recv()
        for make in sends:
            make().wait_send()
        for make in local:
            make().wait()


def _bind(fn, *args):
    return functools.partial(fn, *args)


def join_exchanges(a, b):
    if a is None or b is None:
        return a or b
    na_in, na_out = len(a.inputs), len(a.out_shapes)

    def src_a(fn):
        return lambda ins, outs, me: fn(ins[:na_in], outs[:na_out], me)

    def dst_a(fn):
        return lambda outs, who: fn(outs[:na_out], who)

    def src_b(fn):
        return lambda ins, outs, me: fn(ins[na_in:], outs[na_out:], me)

    def dst_b(fn):
        return lambda outs, who: fn(outs[na_out:], who)

    copies = [(m, src_a(s), dst_a(d)) for m, s, d in a._copies] + [(m, src_b(s), dst_b(d)) for m, s, d in b._copies]
    local = [(src_a(s), dst_a(d)) for s, d in a._local] + [(src_b(s), dst_b(d)) for s, d in b._local]
    aliases = dict(a.aliases)
    aliases.update({na_in + i: na_out + o for i, o in b.aliases.items()})
    return Exchange(a.inputs + b.inputs, a.out_shapes + b.out_shapes, aliases, copies, local)


def gather_over_ici(shards):
    copies = [(mask, _bind(lambda t, ins, outs, me: ins[t], t), _bind(lambda t, outs, sender: outs[t].at[sender], t))
              for t in range(len(shards)) for mask in OTHER_CHIPS]
    local = [(_bind(lambda t, ins, outs, me: ins[t], t), _bind(lambda t, outs, me: outs[t].at[me], t))
             for t in range(len(shards))]
    return Exchange(shards, [jax.ShapeDtypeStruct((N_DEV,) + s.shape, s.dtype) for s in shards], {}, copies, local)


def gather_over_d2d(gathered):
    copies = [(SIBLING, _bind(lambda t, m, ins, outs, me: outs[t].at[me ^ m], t, m),
               _bind(lambda t, m, outs, sender: outs[t].at[sender ^ m], t, m))
              for t in range(len(gathered)) for m in SAME_CORE]
    return Exchange(gathered, [jax.ShapeDtypeStruct(g.shape, g.dtype) for g in gathered],
                    {t: t for t in range(len(gathered))}, copies)


def scatter_over_d2d(blocks):
    copies = [(SIBLING, _bind(lambda t, m, ins, outs, me: ins[t].at[me ^ SIBLING ^ m], t, m),
               _bind(lambda t, i, outs, sender: outs[t].at[i], t, i))
              for t in range(len(blocks)) for i, m in enumerate(SAME_CORE)]
    return Exchange(blocks, [jax.ShapeDtypeStruct((len(SAME_CORE),) + b.shape[1:], b.dtype) for b in blocks], {}, copies)


def scatter_over_ici(pair_sums, bufs, layer):
    n = len(pair_sums)
    copies = [(m, _bind(lambda t, i, ins, outs, me: ins[t].at[i], t, i),
               _bind(lambda t, i, outs, sender: outs[t].at[layer, i], t, i))
              for t in range(n) for i, m in enumerate(SAME_CORE) if m]
    local = [(_bind(lambda t, ins, outs, me: ins[t].at[0], t), _bind(lambda t, outs, me: outs[t].at[layer, 0], t))
             for t in range(n)]
    return Exchange(list(pair_sums) + list(bufs), [jax.ShapeDtypeStruct(b.shape, b.dtype) for b in bufs],
                    {n + t: t for t in range(n)}, copies, local)


def run_exchange(ex, name):
    any_spec = pl.BlockSpec(memory_space=pl.ANY)
    n_in, n_out = len(ex.inputs), len(ex.out_shapes)

    def body(*refs):
        ins, outs, sems = refs[:n_in], refs[n_in:n_in + n_out], refs[n_in + n_out:]
        ex.start(ins, outs, sems)
        ex.drain(ins, outs, sems)

    return pl.pallas_call(
        body, out_shape=ex.out_shapes, in_specs=[any_spec] * n_in, out_specs=[any_spec] * n_out,
        input_output_aliases=ex.aliases, scratch_shapes=ex.scratch, name=name)(*ex.inputs)


MATRICES = ("w_in", "w_attn_out", "w_conv_out", "pool_w", "w_o", "w_ffn_in", "w_ffn_out")
TRANSPOSED = ("w_in", "w_ffn_in")
EVERY = tuple(range(len(MATRICES)))
IN_PROJ_PART, ATTN_PART, MIX_PART = (0,), (1, 2, 3, 4, 5), (6,)
LATE = (0,)
EARLY = EVERY[1:]
EARLY_FIRST, EARLY_SECOND = (4, 6), (1, 2, 3, 5)
SHARD_INFO = {
    "w_in": ((DEPTH, D_IN // N_DEV, D_MODEL), 1),
    "w_attn_out": ((DEPTH, D_ATTN, D_MODEL // N_DEV), 2),
    "w_conv_out": ((DEPTH, D_CONV, D_MODEL // N_DEV), 2),
    "pool_w": ((DEPTH, 4, 64, 256 // N_DEV), 3),
    "w_o": ((DEPTH, D_MODEL // N_DEV, D_MODEL), 1),
    "w_ffn_in": ((DEPTH, 2 * D_FF // N_DEV, D_MODEL), 1),
    "w_ffn_out": ((DEPTH, D_FF // N_DEV, D_MODEL), 1),
}


def _handled(name, t):
    return jnp.transpose(t, (0, 2, 1)) if name in TRANSPOSED else t
VECTORS = ("norm_mix_g", "forget_b", "q_norm_g", "k_norm_g", "pool_scale", "norm_ffn_g")
VECTOR_SHAPES = {"norm_mix_g": (DEPTH, D_MODEL), "forget_b": (DEPTH, HEADS), "q_norm_g": (DEPTH, HEAD_DIM),
                 "k_norm_g": (DEPTH, HEAD_DIM), "pool_scale": (DEPTH, D_MODEL), "norm_ffn_g": (DEPTH, D_MODEL)}
CONV_W_FULL = (DEPTH, 3, D_CONV)


def _size(shape):
    n = 1
    for v in shape:
        n *= v
    return n


def _pack(arrays, rows, cols):
    flat = jnp.concatenate([a.reshape(-1) for a in arrays])
    return jnp.pad(flat, (0, rows * cols - flat.shape[0])).reshape(rows, cols)


def _unpack(packed, shapes):
    flat, out, off = packed.reshape(-1), [], 0
    for shp in shapes:
        out.append(flat[off:off + _size(shp)].reshape(shp))
        off += _size(shp)
    return out


def _join_shards(stacked, axis):
    moved = jnp.moveaxis(stacked, 0, axis)
    shp = list(moved.shape)
    shp[axis:axis + 2] = [shp[axis] * shp[axis + 1]]
    return moved.reshape(shp)


def _cut_shards(full, axis):
    shp = list(full.shape)
    shp[axis:axis + 1] = [N_DEV, shp[axis] // N_DEV]
    return jnp.moveaxis(full.reshape(shp), axis, 0)


N_MOVED = 1544
SHARD_ROWS = D_IN // N_DEV


def _regroup_w_in(shards):
    wt = shards.reshape(D_IN, shards.shape[2])
    pad = jnp.zeros((N_FULL - D_IN, wt.shape[1]), wt.dtype)
    return jnp.concatenate([wt[N_MOVED:], wt[:N_MOVED], pad], axis=0)


def _ungroup_w_in(wpt):
    def kernel_rows(a, b):
        if b <= N_MOVED:
            return [wpt[a + D_IN - N_MOVED:b + D_IN - N_MOVED]]
        if a >= N_MOVED:
            return [wpt[a - N_MOVED:b - N_MOVED]]
        return kernel_rows(a, N_MOVED) + kernel_rows(N_MOVED, b)

    return jnp.stack([jnp.concatenate(kernel_rows(s * SHARD_ROWS, (s + 1) * SHARD_ROWS), axis=0) for s in range(N_DEV)])


def _pool_block_diag(w):
    out = jnp.zeros((D_POOL, D_MODEL), w.dtype)
    for g in range(4):
        out = lax.dynamic_update_slice(out, w[g], (g * 64, g * 256))
    return out


def _pool_from_block_diag(wbd):
    return jnp.stack([wbd[g * 64:(g + 1) * 64, g * 256:(g + 1) * 256] for g in range(4)])


def _layer_weights(mats, vec, conv_w, l):
    wp = _pool_block_diag(mats["pool_w"])
    row = lambda v: v.reshape(1, -1)
    fb = jnp.zeros((1, 128), F32).at[0, :HEADS].set(vec["forget_b"][l])
    cw = jnp.zeros((8, D_CONV), F32).at[:3].set(conv_w[l])
    twice = lambda v: jnp.tile(v.reshape(1, -1), (1, 2))
    return dict(
        wt_in=_regroup_w_in(mats["w_in"]), wt_ffn_in=mats["w_ffn_in"], w_ffn_out=mats["w_ffn_out"],
        wa=mats["w_attn_out"], wc=mats["w_conv_out"], wp=wp, wo=mats["w_o"],
        g_mix=row(vec["norm_mix_g"][l]), g_ffn=row(vec["norm_ffn_g"][l]), gq2=twice(vec["q_norm_g"][l]),
        gk2=twice(vec["k_norm_g"][l]), scale=row(vec["pool_scale"][l]), fb=fb, cw=cw)


def _layer_fwd(x, w, l, comm):
    (proj, h), half_a = norm_matmul(x, w["g_mix"], w["wt_in"], N_MAIN, f"in_proj_{l}", comm.gather_ici(l + 1, IN_PROJ_PART))
    z, c = forget_fwd(h, w["wt_in"], w["fb"], f"forget_fwd_{l}")
    qa, ka, va, vt = attn_prep(proj, c, w["gq2"], w["gk2"], f"attn_prep_{l}")
    (oa, lse), half_b = attn_forward(qa, ka, vt, f"attn_fwd_{l}", comm.gather_ici(l + 1, ATTN_PART))
    x1, half_c = mix_fwd(proj, oa, x, w["wa"], w["wc"], w["wp"], w["scale"], w["cw"], w["wo"], f"mix_fwd_{l}",
                         comm.gather_ici(l + 1, MIX_PART))
    half = list(half_a) + list(half_b) + list(half_c)
    (gu, h2), gathered = norm_matmul(x1, w["g_ffn"], w["wt_ffn_in"], 2 * D_FF, f"ffn_in_{l}", comm.gather_d2d(l + 1, half))
    x2 = swiglu_matmul(gu, w["w_ffn_out"], x1, f"ffn_out_{l}")
    saved = dict(x=x, proj=proj, h=h, z=z, qa=qa, ka=ka, va=va, oa=oa, lse=lse, x1=x1, gu=gu, h2=h2)
    return x2, saved, gathered


def _layer_bwd(dx2, sv, w, l, comm):
    g = {}
    (dgu, act), stage = swiglu_bwd(dx2, sv["gu"], w["w_ffn_out"], f"ffn_out_bwd_{l}", comm.scatter_d2d(l + 1))
    sums = comm.pair_sums(l + 1, stage)
    g["w_ffn_out"] = tn_matmul(act, dx2, f"dw_ffn_out_{l}")
    g["w_ffn_in"] = tn_matmul(dgu, sv["h2"], f"dw_ffn_in_{l}")
    (dx1, dg), _ = matmul_normbwd(dgu, w["wt_ffn_in"], sv["x1"], w["g_ffn"], dx2, f"ffn_in_bwd_{l}")
    g["norm_ffn_g"] = dg[0]

    (dproj, doa, a_tok, merged, dya, dyc, dyp, uc, dd, dscale, dcw) = mix_bwd(
        sv["proj"], sv["oa"], dx1, w["wa"], w["wc"], w["wp"], w["scale"], w["cw"], w["wo"], f"mix_bwd_{l}")
    g["w_o"] = tn_matmul(merged, dx1, f"dw_o_{l}")
    g["w_attn_out"], g["w_conv_out"], dwp = tn_matmuls([(a_tok, dya), (uc, dyc), (dd, dyp)], f"dw_branches_{l}")
    g["pool_w"] = _pool_from_block_diag(dwp)
    g["pool_scale"] = dscale[0]
    g["conv_w"] = dcw[:3]

    early = comm.early(l)
    comm.grads(l, g)
    above = comm.scatter_ici(l + 1, sums)
    (dqa, dka, dva), got = attn_backward(sv["qa"], sv["ka"], sv["va"], sv["oa"], doa, sv["lse"], f"attn_bwd_{l}",
                                         join_exchanges(above, comm.scatter_d2d(l, early) if early else None))
    n_above = len(above.out_shapes) if above else 0
    comm.scattered(got[:n_above])
    early_sums = dict(zip(early, comm.pair_sums(l, got[n_above:], early))) if early else {}
    early_ici = lambda which: comm.scatter_ici(l, [early_sums[t] for t in which], which) if early else None
    dproj, dc, dgq, dgk = attn_post(dqa, dka, dva, sv["proj"], w["gq2"], w["gk2"], dproj, f"attn_post_{l}")
    g["q_norm_g"] = dgq[0, :HEAD_DIM] + dgq[0, HEAD_DIM:]
    g["k_norm_g"] = dgk[0, :HEAD_DIM] + dgk[0, HEAD_DIM:]
    dproj, db = forget_bwd(dc, sv["z"], dproj, f"forget_bwd_{l}")
    g["forget_b"] = db[0, :HEADS]

    dw_in = tn_matmul(dproj, sv["h"], f"dw_in_{l}", m_cols=N_FULL, ex=early_ici(EARLY_FIRST))
    if early:
        dw_in, got = dw_in
        comm.scattered(got, EARLY_FIRST)
    g["w_in"] = _ungroup_w_in(dw_in)
    (dx, dg), got = matmul_normbwd(dproj, w["wt_in"], sv["x"], w["g_mix"], dx1, f"in_proj_bwd_{l}", k=N_FULL,
                                   ex=early_ici(EARLY_SECOND))
    comm.scattered(got, EARLY_SECOND if early else None)
    g["norm_mix_g"] = dg[0]
    comm.grads(l, g)
    return dx


def _local_step(x, tgt, comm):
    ws, saved = [], []
    w = comm.weights(0, None)
    for l in range(DEPTH):
        ws.append(w)
        x, sv, gathered = _layer_fwd(x, w, l, comm)
        saved.append(sv)
        if l + 1 < DEPTH:
            w = comm.weights(l + 1, gathered)
    sq, dx = loss_kernel(x, tgt, "loss")
    for l in reversed(range(DEPTH)):
        dx = _layer_bwd(dx, saved[l], ws[l], l, comm)
    comm.finish()
    return sq[0, 0], dx


def kernel(x, norm_mix_g, w_in, forget_b, q_norm_g, k_norm_g, w_attn_out, conv_w, w_conv_out, pool_w, pool_scale, w_o, norm_ffn_g, w_ffn_in, w_ffn_out, loss_target, m_norm_mix_g, m_w_in, m_forget_b, m_q_norm_g, m_k_norm_g, m_w_attn_out, m_conv_w, m_w_conv_out, m_pool_w, m_pool_scale, m_w_o, m_norm_ffn_g, m_w_ffn_in, m_w_ffn_out, v_norm_mix_g, v_w_in, v_forget_b, v_q_norm_g, v_k_norm_g, v_w_attn_out, v_conv_w, v_w_conv_out, v_pool_w, v_pool_scale, v_w_o, v_norm_ffn_g, v_w_ffn_in, v_w_ffn_out):
    w = dict(norm_mix_g=norm_mix_g, w_in=w_in, forget_b=forget_b, q_norm_g=q_norm_g, k_norm_g=k_norm_g,
             w_attn_out=w_attn_out, conv_w=conv_w, w_conv_out=w_conv_out, pool_w=pool_w, pool_scale=pool_scale,
             w_o=w_o, norm_ffn_g=norm_ffn_g, w_ffn_in=w_ffn_in, w_ffn_out=w_ffn_out)
    m = dict(norm_mix_g=m_norm_mix_g, w_in=m_w_in, forget_b=m_forget_b, q_norm_g=m_q_norm_g, k_norm_g=m_k_norm_g,
             w_attn_out=m_w_attn_out, conv_w=m_conv_w, w_conv_out=m_w_conv_out, pool_w=m_pool_w,
             pool_scale=m_pool_scale, w_o=m_w_o, norm_ffn_g=m_norm_ffn_g, w_ffn_in=m_w_ffn_in, w_ffn_out=m_w_ffn_out)
    v = dict(norm_mix_g=v_norm_mix_g, w_in=v_w_in, forget_b=v_forget_b, q_norm_g=v_q_norm_g, k_norm_g=v_k_norm_g,
             w_attn_out=v_w_attn_out, conv_w=v_conv_w, w_conv_out=v_w_conv_out, pool_w=v_pool_w,
             pool_scale=v_pool_scale, w_o=v_w_o, norm_ffn_g=v_norm_ffn_g, w_ffn_in=v_w_ffn_in, w_ffn_out=v_w_ffn_out)
    me = 4 * lax.axis_index("x") + 2 * lax.axis_index("y") + lax.axis_index("c")
    layer_shard = {n: SHARD_INFO[n][0][1:] for n in MATRICES}
    cut_axis = {n: SHARD_INFO[n][1] - 1 for n in MATRICES}

    vec = {n: w[n] for n in VECTORS}
    rc = {n: (_size(layer_shard[n][:-1]), layer_shard[n][-1]) for n in MATRICES}

    class Comm:
        bufs = [lax.empty((DEPTH, len(SAME_CORE)) + layer_shard[n], BF16) for n in MATRICES]
        blocks = [None] * DEPTH
        small_g = [None] * DEPTH
        conv_full = None

        @staticmethod
        def shards(l):
            return [_handled(n, w[n])[l].astype(BF16) for n in MATRICES]

        @staticmethod
        def gather_ici(l, part):
            return gather_over_ici([Comm.shards(l)[t] for t in part]) if l < DEPTH else None

        @staticmethod
        def gather_d2d(l, half):
            return gather_over_d2d(half) if l < DEPTH else None

        @staticmethod
        def weights(l, gathered):
            if l == 0:
                *gathered, conv_g = all_gather(Comm.shards(0) + [_pack([conv_w], 8, 128)], "gather_0")
                Comm.conv_full = _join_shards(jnp.stack([_unpack(conv_g[i], [conv_w.shape])[0] for i in range(N_DEV)]), 2)
            mats = {n: t if n == "w_in" else _join_shards(t, cut_axis[n]) for n, t in zip(MATRICES, gathered)}
            return _layer_weights(mats, vec, Comm.conv_full, l)

        @staticmethod
        def grads(l, g):
            Comm.small_g[l] = g
            Comm.blocks[l] = [None if n not in g else g[n] if n == "w_in" else _cut_shards(g[n], cut_axis[n])
                              for n in MATRICES]

        @staticmethod
        def early(l):
            return EARLY if l == 0 else None

        @staticmethod
        def scatter_d2d(l, which=EVERY):
            return scatter_over_d2d([Comm.blocks[l][t] for t in which]) if l < DEPTH else None

        @staticmethod
        def pair_sums(l, stage, which=EVERY):
            if l >= DEPTH:
                return None
            return [pair_sum(Comm.blocks[l][t].reshape((N_DEV,) + rc[MATRICES[t]]),
                             s.reshape((len(SAME_CORE),) + rc[MATRICES[t]]), me,
                             f"pair_sum_{MATRICES[t]}_{l}").reshape(s.shape) for t, s in zip(which, stage)]

        @staticmethod
        def scatter_ici(l, sums, which=EVERY):
            return scatter_over_ici(sums, [Comm.bufs[t] for t in which], l) if l < DEPTH else None

        @staticmethod
        def scattered(results, which=EVERY):
            for t, r in zip(which or (), results):
                Comm.bufs[t] = r

        @staticmethod
        def finish():
            stage = run_exchange(Comm.scatter_d2d(0, LATE), "scatter_d2d_0")
            Comm.scattered(run_exchange(Comm.scatter_ici(0, Comm.pair_sums(0, stage, LATE), LATE), "scatter_ici_0"), LATE)

    small_g, received = Comm.small_g, Comm
    sq, dx = _local_step(x[0], loss_target[0], Comm)
    loss = lax.psum(0.5 * sq / D_MODEL, ("x", "y", "c"))

    big = {}
    for n, parts in zip(MATRICES, received.bufs):
        outs = adamw_sum(parts.reshape((DEPTH, len(SAME_CORE)) + rc[n]),
                         *[_handled(n, d[n]).reshape((DEPTH,) + rc[n]) for d in (w, m, v)], f"adamw_{n}")
        big[n] = [_handled(n, t.reshape((DEPTH,) + layer_shard[n])) for t in outs]

    small_shapes = [VECTOR_SHAPES[n] for n in VECTORS] + [CONV_W_FULL]
    stacked = [jnp.stack([small_g[l][n] for l in range(DEPTH)]) for n in VECTORS + ("conv_w",)]
    sparts = all_gather([_pack(stacked, SMALL_ROWS, 128)], "gather_vector_grads")[0]
    col0 = me * (D_CONV // N_DEV)
    place = lambda t: lax.dynamic_update_slice(jnp.zeros(CONV_W_FULL, F32), t, (0, 0, col0))
    spacked = [_pack([d[n] for n in VECTORS] + [place(d["conv_w"])], SMALL_ROWS, 128)[None] for d in (w, m, v)]
    small = [_unpack(t[0], small_shapes) for t in adamw_sum(sparts[None], *spacked, "adamw_vectors")]

    def result(kind):
        out = {n: big[n][kind] for n in MATRICES}
        out.update({n: small[kind][j] for j, n in enumerate(VECTORS)})
        out["conv_w"] = lax.dynamic_slice(small[kind][len(VECTORS)], (0, 0, col0), conv_w.shape)
        return [out[n] for n in w]

    return (loss, dx[None], *result(0), *result(1), *result(2), *result(3))
```

```python
import functools

import jax
import jax.numpy as jnp
from jax import lax
from jax.experimental import pallas as pl
from jax.experimental.pallas import tpu as pltpu

F32 = jnp.float32
BF16 = jnp.bfloat16

N_DEV = 8
DEPTH = 4
D_MODEL = 1024
HEAD_DIM = 64
HEADS = 8
D_ATTN = 512
D_CONV = 256
D_POOL = 256
D_FF = 2816
D_IN = 5640
EPS = 1e-6
ATTN_SCALE = HEAD_DIM ** -0.5

N_REST = 4096
N_MAIN = 5632
N_FULL = 5760
DPROJ_TAIL = 2048
DPROJ_COLS = N_REST + DPROJ_TAIL
FF_BLK = 256
N_FF_BLKS = D_FF // FF_BLK
HALO = 16

ADAM_LR = 0.001
ADAM_B1 = 0.9
ADAM_B2 = 0.999
ADAM_EPS = 1e-08
ADAM_WD = 0.01
ADAM_STEP = 10

SMALL_ROWS = 128

VMEM_LIMIT = 48 * 2 ** 20


def _cparams(sem, vmem=None):
    return pltpu.CompilerParams(dimension_semantics=sem, vmem_limit_bytes=vmem or VMEM_LIMIT)


def _pick(n, cands):
    for c in cands:
        if n % c == 0:
            return c
    raise ValueError(f"no tile for {n}")


def _tile(n, cap):
    t = min(cap, n)
    assert n % t == 0, (n, cap)
    return t


def _sigmoid(v):
    return 1.0 / (1.0 + jnp.exp(-v))


def _rstd(v):
    return lax.rsqrt(jnp.mean(v * v, axis=-1, keepdims=True) + EPS)


def _dot(a, b):
    return jnp.dot(a, b, preferred_element_type=F32)


def _dot_tn(a, b):
    return lax.dot_general(a, b, (((0,), (0,)), ((), ())), preferred_element_type=F32)


def _dot_nt(a, b):
    return lax.dot_general(a, b, (((1,), (1,)), ((), ())), preferred_element_type=F32)


def norm_matmul(x, g, wt, n_cols, name, ex=None):
    s, d = x.shape
    tm, tn = _tile(s, 1024), _pick(n_cols, (2816, 1408, 512))

    def body(x_ref, g_ref, w_ref, o_ref, h_ref):
        @pl.when(pl.program_id(1) == 0)
        def _():
            xv = x_ref[...]
            h_ref[...] = (xv * _rstd(xv) * g_ref[...]).astype(BF16)

        o_ref[...] = _dot_nt(h_ref[...], w_ref[...]).astype(BF16)

    return _carried_call(
        body, ex, (s // tm, n_cols // tn),
        [pl.BlockSpec((tm, d), lambda i, j: (i, 0)), pl.BlockSpec((1, d), lambda i, j: (0, 0)),
         pl.BlockSpec((tn, d), lambda i, j: (j, 0))],
        [pl.BlockSpec((tm, tn), lambda i, j: (i, j)), pl.BlockSpec((tm, d), lambda i, j: (i, 0))],
        [jax.ShapeDtypeStruct((s, n_cols), BF16), jax.ShapeDtypeStruct((s, d), BF16)], [],
        ("arbitrary", "arbitrary"), name, (x, g, wt))


def tn_matmul(a, b, name, m_cols=None, ex=None):
    t = a.shape[0]
    m = m_cols or a.shape[1]
    n = b.shape[1]
    tk = _tile(t, 1024)
    tmm = _pick(m, (1408, 1152, 1024, 512, 256))
    tn = _pick(n, (1408, 1152, 1024, 512, 128))
    nk = t // tk

    def body(a_ref, b_ref, o_ref, acc_ref):
        @pl.when(pl.program_id(2) == 0)
        def _():
            acc_ref[...] = jnp.zeros_like(acc_ref)

        acc_ref[...] += _dot_tn(a_ref[...].astype(BF16), b_ref[...].astype(BF16))

        @pl.when(pl.program_id(2) == nk - 1)
        def _():
            o_ref[...] = acc_ref[...].astype(BF16)

    if ex is None:
        return pl.pallas_call(
            body, grid=(m // tmm, n // tn, nk),
            in_specs=[pl.BlockSpec((tk, tmm), lambda i, j, k: (k, i)), pl.BlockSpec((tk, tn), lambda i, j, k: (k, j))],
            out_specs=pl.BlockSpec((tmm, tn), lambda i, j, k: (i, j)),
            out_shape=jax.ShapeDtypeStruct((m, n), BF16), scratch_shapes=[pltpu.VMEM((tmm, tn), F32)],
            compiler_params=_cparams(("parallel", "parallel", "arbitrary")), name=name)(a, b)
    (out,), carried = _carried_call(
        body, ex, (m // tmm, n // tn, nk),
        [pl.BlockSpec((tk, tmm), lambda i, j, k: (k, i)), pl.BlockSpec((tk, tn), lambda i, j, k: (k, j))],
        [pl.BlockSpec((tmm, tn), lambda i, j, k: (i, j))], [jax.ShapeDtypeStruct((m, n), BF16)],
        [pltpu.VMEM((tmm, tn), F32)], ("arbitrary", "arbitrary", "arbitrary"), name, (a, b))
    return out, carried


def tn_matmuls(pairs, name):
    t = pairs[0][0].shape[0]
    tk = _tile(t, 1024)
    nk = t // tk
    n = len(pairs)

    def body(*refs):
        ins, outs, accs = refs[:2 * n], refs[2 * n:3 * n], refs[3 * n:]

        @pl.when(pl.program_id(0) == 0)
        def _():
            for acc in accs:
                acc[...] = jnp.zeros_like(acc)

        for i in range(n):
            accs[i][...] += _dot_tn(ins[2 * i][...], ins[2 * i + 1][...])

        @pl.when(pl.program_id(0) == nk - 1)
        def _():
            for out, acc in zip(outs, accs):
                out[...] = acc[...].astype(BF16)

    shapes = [(a.shape[1], b.shape[1]) for a, b in pairs]
    return pl.pallas_call(
        body, grid=(nk,),
        in_specs=[pl.BlockSpec((tk, t_.shape[1]), lambda k: (k, 0)) for pair in pairs for t_ in pair],
        out_specs=[pl.BlockSpec(shp, lambda k: (0, 0)) for shp in shapes],
        out_shape=[jax.ShapeDtypeStruct(shp, BF16) for shp in shapes],
        scratch_shapes=[pltpu.VMEM(shp, F32) for shp in shapes],
        compiler_params=_cparams(("arbitrary",)), name=name)(*[t_ for pair in pairs for t_ in pair])


def matmul_normbwd(a, wt, x, g, dres, name, k=None, ex=None):
    s = a.shape[0]
    k = k or a.shape[1]
    d = wt.shape[1]
    tm = _tile(s, 1024)
    tk = _pick(k, (1408, 1152, 512))
    nk = k // tk

    def body(a_ref, w_ref, x_ref, g_ref, r_ref, dx_ref, dg_ref, acc_ref):
        i, kk = pl.program_id(0), pl.program_id(1)

        @pl.when(kk == 0)
        def _():
            acc_ref[...] = jnp.zeros_like(acc_ref)

        @pl.when((i == 0) & (kk == 0))
        def _():
            dg_ref[...] = jnp.zeros_like(dg_ref)

        acc_ref[...] += _dot(a_ref[...], w_ref[...])

        @pl.when(kk == nk - 1)
        def _():
            xv = x_ref[...]
            r = _rstd(xv)
            y = xv * r
            dh = acc_ref[...]
            dy = dh * g_ref[...]
            dx_ref[...] = r_ref[...] + r * (dy - y * jnp.mean(dy * y, axis=-1, keepdims=True))
            dg_ref[...] += jnp.sum(dh * y, axis=0, keepdims=True)

    return _carried_call(
        body, ex, (s // tm, nk),
        [pl.BlockSpec((tm, tk), lambda i, kk: (i, kk)), pl.BlockSpec((tk, d), lambda i, kk: (kk, 0)),
         pl.BlockSpec((tm, d), lambda i, kk: (i, 0)), pl.BlockSpec((1, d), lambda i, kk: (0, 0)),
         pl.BlockSpec((tm, d), lambda i, kk: (i, 0))],
        [pl.BlockSpec((tm, d), lambda i, kk: (i, 0)), pl.BlockSpec((1, d), lambda i, kk: (0, 0))],
        [jax.ShapeDtypeStruct((s, d), F32), jax.ShapeDtypeStruct((1, d), F32)],
        [pltpu.VMEM((tm, d), F32)], ("arbitrary", "arbitrary"), name, (a, wt, x, g, dres), vmem=56 * 2 ** 20)


def swiglu_matmul(gu, w, x1, name):
    s = gu.shape[0]
    d = w.shape[1]
    tm = _tile(s, 512)

    def body(gu_ref, w_ref, x_ref, o_ref):
        acc = x_ref[...]
        for j in range(N_FF_BLKS):
            gt = gu_ref[:, j * FF_BLK:(j + 1) * FF_BLK].astype(F32)
            up = gu_ref[:, D_FF + j * FF_BLK:D_FF + (j + 1) * FF_BLK].astype(F32)
            act = (gt * _sigmoid(gt) * up).astype(BF16)
            acc += _dot(act, w_ref[j * FF_BLK:(j + 1) * FF_BLK, :])
        o_ref[...] = acc

    return pl.pallas_call(
        body, grid=(s // tm,),
        in_specs=[pl.BlockSpec((tm, 2 * D_FF), lambda i: (i, 0)), pl.BlockSpec((D_FF, d), lambda i: (0, 0)),
                  pl.BlockSpec((tm, d), lambda i: (i, 0))],
        out_specs=pl.BlockSpec((tm, d), lambda i: (i, 0)),
        out_shape=jax.ShapeDtypeStruct((s, d), F32),
        compiler_params=_cparams(("parallel",)), name=name)(gu, w, x1)


def swiglu_bwd(dx2, gu, w, name, ex=None):
    s, d = dx2.shape
    tm = _tile(s, 512)

    def body(dx_ref, gu_ref, w_ref, dgu_ref, act_ref):
        dx = dx_ref[...].astype(BF16)
        for j in range(N_FF_BLKS):
            g_cols = slice(j * FF_BLK, (j + 1) * FF_BLK)
            u_cols = slice(D_FF + j * FF_BLK, D_FF + (j + 1) * FF_BLK)
            dact = _dot_nt(dx, w_ref[j * FF_BLK:(j + 1) * FF_BLK, :])
            gt = gu_ref[:, g_cols].astype(F32)
            up = gu_ref[:, u_cols].astype(F32)
            sg = _sigmoid(gt)
            silu = gt * sg
            act_ref[:, j * FF_BLK:(j + 1) * FF_BLK] = (silu * up).astype(BF16)
            dgu_ref[:, g_cols] = (dact * up * (sg + silu * (1.0 - sg))).astype(BF16)
            dgu_ref[:, u_cols] = (dact * silu).astype(BF16)

    return _carried_call(
        body, ex, (s // tm,),
        [pl.BlockSpec((tm, d), lambda i: (i, 0)), pl.BlockSpec((tm, 2 * D_FF), lambda i: (i, 0)),
         pl.BlockSpec((D_FF, d), lambda i: (0, 0), pipeline_mode=pl.Buffered(1))],
        [pl.BlockSpec((tm, 2 * D_FF), lambda i: (i, 0)), pl.BlockSpec((tm, D_FF), lambda i: (i, 0))],
        [jax.ShapeDtypeStruct((s, 2 * D_FF), BF16), jax.ShapeDtypeStruct((s, D_FF), BF16)], [],
        ("arbitrary",), name, (dx2, gu, w), vmem=56 * 2 ** 20)


def loss_kernel(y, tgt, name):
    s, d = y.shape
    tm = _tile(s, 512)

    def body(y_ref, t_ref, l_ref, dy_ref):
        @pl.when(pl.program_id(0) == 0)
        def _():
            l_ref[...] = jnp.zeros_like(l_ref)

        err = y_ref[...] - t_ref[...]
        dy_ref[...] = err * (1.0 / d)
        l_ref[...] += jnp.sum(jnp.sum(err * err, axis=1, keepdims=True), axis=0, keepdims=True)

    return pl.pallas_call(
        body, grid=(s // tm,),
        in_specs=[pl.BlockSpec((tm, d), lambda i: (i, 0)), pl.BlockSpec((tm, d), lambda i: (i, 0))],
        out_specs=[pl.BlockSpec((8, 128), lambda i: (0, 0)), pl.BlockSpec((tm, d), lambda i: (i, 0))],
        out_shape=[jax.ShapeDtypeStruct((8, 128), F32), jax.ShapeDtypeStruct((s, d), F32)],
        compiler_params=_cparams(("arbitrary",)), name=name)(y, tgt)


def _split3(v):
    a1 = v.astype(BF16)
    r1 = v - a1.astype(F32)
    a2 = r1.astype(BF16)
    a3 = (r1 - a2.astype(F32)).astype(BF16)
    return a1, a2, a3


def forget_fwd(h, wt_in, b, name):
    s, d = h.shape
    tm = _tile(s, 512)

    def body(h_ref, w_ref, b_ref, z_ref, c_ref, carry_ref):
        @pl.when(pl.program_id(0) == 0)
        def _():
            carry_ref[...] = jnp.zeros_like(carry_ref)

        z = _dot_nt(h_ref[...], w_ref[...]) + b_ref[...]
        z_ref[...] = z
        logf = jnp.minimum(z, 0.0) - jnp.log(1.0 + jnp.exp(-jnp.abs(z)))
        row = lax.broadcasted_iota(jnp.int32, (tm, tm), 0)
        col = lax.broadcasted_iota(jnp.int32, (tm, tm), 1)
        tri = (row >= col).astype(BF16)
        a1, a2, a3 = _split3(logf)
        c = _dot(tri, a1) + _dot(tri, a2) + _dot(tri, a3) + carry_ref[...]
        c_ref[...] = c
        carry_ref[...] = c[tm - 1:tm, :]

    return pl.pallas_call(
        body, grid=(s // tm,),
        in_specs=[pl.BlockSpec((tm, d), lambda i: (i, 0)), pl.BlockSpec((128, d), lambda i: (N_MAIN // 128, 0)),
                  pl.BlockSpec((1, 128), lambda i: (0, 0))],
        out_specs=[pl.BlockSpec((tm, 128), lambda i: (i, 0)), pl.BlockSpec((tm, 128), lambda i: (i, 0))],
        out_shape=[jax.ShapeDtypeStruct((s, 128), F32), jax.ShapeDtypeStruct((s, 128), F32)],
        scratch_shapes=[pltpu.VMEM((1, 128), F32)],
        compiler_params=_cparams(("arbitrary",)), name=name)(h, wt_in, b)


def forget_bwd(dc, z, dproj, name):
    s = dc.shape[0]
    tm = _tile(s, 512)
    nt = s // tm

    def body(dc_ref, z_ref, dp_ref, dz_ref, db_ref, carry_ref):
        @pl.when(pl.program_id(0) == 0)
        def _():
            carry_ref[...] = jnp.zeros_like(carry_ref)
            db_ref[...] = jnp.zeros_like(db_ref)

        row = lax.broadcasted_iota(jnp.int32, (tm, tm), 0)
        col = lax.broadcasted_iota(jnp.int32, (tm, tm), 1)
        tri = (col >= row).astype(BF16)
        a1, a2, a3 = _split3(dc_ref[...])
        dlogf = _dot(tri, a1) + _dot(tri, a2) + _dot(tri, a3) + carry_ref[...]
        carry_ref[...] = dlogf[0:1, :]
        dz = dlogf * (1.0 - _sigmoid(z_ref[...]))
        dz_ref[...] = dz.astype(BF16)
        db_ref[...] += jnp.sum(dz, axis=0, keepdims=True)

    return pl.pallas_call(
        body, grid=(nt,),
        in_specs=[pl.BlockSpec((tm, 128), lambda i: (nt - 1 - i, 0)), pl.BlockSpec((tm, 128), lambda i: (nt - 1 - i, 0)),
                  pl.BlockSpec(memory_space=pl.ANY)],
        out_specs=[pl.BlockSpec((tm, 128), lambda i: (nt - 1 - i, N_MAIN // 128)), pl.BlockSpec((1, 128), lambda i: (0, 0))],
        out_shape=[jax.ShapeDtypeStruct(dproj.shape, BF16), jax.ShapeDtypeStruct((1, 128), F32)],
        scratch_shapes=[pltpu.VMEM((1, 128), F32)], input_output_aliases={2: 0},
        compiler_params=_cparams(("arbitrary",)), name=name)(dc, z, dproj)


HEAD_GROUP = 4
HEAD_GROUP_FWD = 8
LANE_C = 64
LANE_ONE = 67


def _lanes():
    lane = lax.broadcasted_iota(jnp.int32, (1, 128), 1)
    return lane, lane < HEAD_DIM


def _half_mean(t, lo):
    s_lo = jnp.sum(jnp.where(lo, t, 0.0), axis=-1, keepdims=True)
    s_hi = jnp.sum(jnp.where(lo, 0.0, t), axis=-1, keepdims=True)
    return jnp.where(lo, s_lo, s_hi) * (1.0 / HEAD_DIM)


def _lane_col(t, lane, idx):
    return jnp.sum(jnp.where(lane == idx, t, 0.0), axis=-1, keepdims=True)


def _swap_halves(t):
    return pltpu.roll(t, HEAD_DIM, 1)


def attn_prep(proj, c, gq2, gk2, name):
    s = proj.shape[0]
    tm = _tile(s, 512)
    first = N_REST // 128

    def body(q_ref, k_ref, v_ref, c_ref, gq_ref, gk_ref, qa_ref, ka_ref, va_ref, vt_ref):
        j = pl.program_id(1)
        lane, lo = _lanes()

        def normed(ref, g):
            t = ref[...].astype(F32)
            return t * lax.rsqrt(_half_mean(t * t, lo) + EPS) * g

        qn = normed(q_ref, gq_ref[...] * ATTN_SCALE)
        kn = normed(k_ref, gk_ref[...])
        vv = v_ref[...].astype(F32)
        cv = c_ref[...]
        one_q = jnp.where((lane >= LANE_ONE) & (lane < LANE_ONE + 3), 1.0, 0.0)
        one_k = jnp.where((lane >= LANE_C) & (lane < LANE_C + 3), 1.0, 0.0)
        one_v = jnp.where(lane == LANE_C, 1.0, 0.0)
        for e in range(2):
            pick = (lambda t: t) if e == 0 else _swap_halves
            pieces = [p.astype(F32) for p in _split3(_lane_col(cv, lane, 2 * j + e))]
            ext_q, ext_k = one_q, one_k
            for i, p in enumerate(pieces):
                ext_q = jnp.where(lane == LANE_C + i, p, ext_q)
                ext_k = jnp.where(lane == LANE_ONE + i, -p, ext_k)
            qa_ref[e] = jnp.where(lo, pick(qn), ext_q).astype(BF16)
            ka_ref[e] = jnp.where(lo, pick(kn), ext_k).astype(BF16)
            va = jnp.where(lo, pick(vv), one_v)
            va_ref[e] = va.astype(BF16)
            vt_ref[e] = va.T.astype(BF16)

    tile = lambda base: pl.BlockSpec((tm, 128), lambda i, j: (i, base + j))
    vec = pl.BlockSpec((1, 128), lambda i, j: (0, 0))
    out = pl.BlockSpec((2, tm, 128), lambda i, j: (j, i, 0))
    return pl.pallas_call(
        body, grid=(s // tm, HEADS // 2),
        in_specs=[tile(first), tile(first + 4), tile(first + 8), pl.BlockSpec((tm, 128), lambda i, j: (i, 0)), vec, vec],
        out_specs=[out, out, out, pl.BlockSpec((2, 128, tm), lambda i, j: (j, 0, i))],
        out_shape=[jax.ShapeDtypeStruct((HEADS, s, 128), BF16)] * 3 + [jax.ShapeDtypeStruct((HEADS, 128, s), BF16)],
        compiler_params=_cparams(("parallel", "arbitrary")), name=name)(proj, proj, proj, c, gq2, gk2)


def _carry(ex, n_in, n_out, n_scratch, grid):
    n_xin, n_xout = (len(ex.inputs), len(ex.out_shapes)) if ex else (0, 0)

    def split(refs):
        ins, xins = refs[:n_in], refs[n_in:n_in + n_xin]
        rest = refs[n_in + n_xin:]
        outs, xouts = rest[:n_out], rest[n_out:n_out + n_xout]
        rest = rest[n_out + n_xout:]
        return ins + outs + rest[:n_scratch], (xins, xouts, rest[n_scratch:])

    def first():
        return functools.reduce(lambda a, b: a & b, [pl.program_id(d) == 0 for d in range(len(grid))])

    def last():
        return functools.reduce(lambda a, b: a & b, [pl.program_id(d) == grid[d] - 1 for d in range(len(grid))])

    return split, first, last


def _carried_call(body, ex, grid, in_specs, out_specs, out_shape, scratch, sem, name, operands, vmem=None):
    any_spec = pl.BlockSpec(memory_space=pl.ANY)
    split, first, last = _carry(ex, len(in_specs), len(out_specs), len(scratch), grid)

    def carried(*refs):
        own, xrefs = split(refs)
        if ex:
            @pl.when(first())
            def _():
                ex.start(*xrefs)

        body(*own)
        if ex:
            @pl.when(last())
            def _():
                ex.drain(*xrefs)

    n_xin = len(ex.inputs) if ex else 0
    results = pl.pallas_call(
        carried, grid=grid, in_specs=list(in_specs) + [any_spec] * n_xin,
        out_specs=list(out_specs) + [any_spec] * (len(ex.out_shapes) if ex else 0),
        out_shape=list(out_shape) + (list(ex.out_shapes) if ex else []),
        input_output_aliases={len(in_specs) + i: len(out_specs) + o for i, o in ex.aliases.items()} if ex else {},
        scratch_shapes=list(scratch) + (ex.scratch if ex else []),
        compiler_params=_cparams(sem, vmem), name=name)(*operands, *(ex.inputs if ex else []))
    return results[:len(out_specs)], results[len(out_specs):]


def _tri_rows(t, n):
    qi = sum(jnp.where(t >= r * (r + 1) // 2, 1, 0) for r in range(1, n))
    return qi, t - qi * (qi + 1) // 2


def _tri_cols(t, n):
    ki = sum(jnp.where(t >= r * n - r * (r - 1) // 2, 1, 0) for r in range(1, n))
    return ki, ki + t - (ki * n - ki * (ki - 1) // 2)


def _causal_t(st_blk, tk, tq):
    key = lax.broadcasted_iota(jnp.int32, (tk, tq), 0)
    qry = lax.broadcasted_iota(jnp.int32, (tk, tq), 1)
    return jnp.where(qry >= key, st_blk, -jnp.inf)


def attn_forward(qa, ka, vt, name, ex=None):
    hh, s, _ = qa.shape
    tq = tk = _tile(s, 512)
    nq = s // tq
    grp = HEAD_GROUP_FWD

    def body(q_ref, k_ref, vt_ref, o_ref, lse_ref, m_ref, acc_ref):
        qi, ki = _tri_rows(pl.program_id(1), nq)

        @pl.when(ki == 0)
        def _():
            m_ref[...] = jnp.full_like(m_ref, -jnp.inf)
            acc_ref[...] = jnp.zeros_like(acc_ref)

        def step(masked):
            nxt = _dot_nt(k_ref[0], q_ref[0])
            for g in range(grp):
                st = nxt
                if g + 1 < grp:
                    nxt = _dot_nt(k_ref[g + 1], q_ref[g + 1])
                if masked:
                    st = _causal_t(st, tk, tq)
                m_old = m_ref[g]
                m_new = jnp.maximum(m_old, jnp.max(st, axis=0, keepdims=True))
                pt = jnp.exp(st - m_new).astype(BF16)
                acc_ref[g] = jnp.exp(m_old - m_new) * acc_ref[g] + _dot(vt_ref[g], pt)
                m_ref[g] = m_new

        @pl.when(ki < qi)
        def _():
            step(False)

        @pl.when(ki == qi)
        def _():
            step(True)
            for g in range(grp):
                acc = acc_ref[g]
                denom = acc[LANE_C:LANE_C + 1, :]
                o_ref[g] = (acc / denom).T.astype(BF16)
                lse_ref[g] = m_ref[g] + jnp.log(denom)

    qspec = pl.BlockSpec((grp, tq, 128), lambda h, t: (h, _tri_rows(t, nq)[0], 0))
    kspec = pl.BlockSpec((grp, tk, 128), lambda h, t: (h, _tri_rows(t, nq)[1], 0))
    vspec = pl.BlockSpec((grp, 128, tk), lambda h, t: (h, 0, _tri_rows(t, nq)[1]))
    lspec = pl.BlockSpec((grp, 1, tq), lambda h, t: (h, 0, _tri_rows(t, nq)[0]))
    return _carried_call(
        body, ex, (hh // grp, nq * (nq + 1) // 2), [qspec, kspec, vspec], [qspec, lspec],
        [jax.ShapeDtypeStruct((hh, s, 128), BF16), jax.ShapeDtypeStruct((hh, 1, s), F32)],
        [pltpu.VMEM((grp, 1, tq), F32), pltpu.VMEM((grp, 128, tq), F32)],
        ("arbitrary", "arbitrary"), name, (qa, ka, vt))


def attn_backward(qa, ka, va, oa, doa, lse, name, ex=None):
    hh, s, _ = qa.shape
    tq = tk = _tile(s, 512)
    nq = s // tq
    grp = HEAD_GROUP

    def body(q_ref, k_ref, v_ref, o_ref, do_ref, lse_ref, dq_ref, dk_ref, dv_ref, dka_ref, dva_ref):
        ki, qi = _tri_cols(pl.program_id(1), nq)

        @pl.when(pl.program_id(1) == 0)
        def _():
            dq_ref[...] = jnp.zeros_like(dq_ref)

        @pl.when(qi == ki)
        def _():
            dka_ref[...] = jnp.zeros_like(dka_ref)
            dva_ref[...] = jnp.zeros_like(dva_ref)

        def step(masked):
            rows = pl.ds(pl.multiple_of(qi * tq, tq), tq)
            products = lambda g: (_dot_nt(k_ref[g], q_ref[g]), _dot_nt(v_ref[g], do_ref[g]))
            nxt = products(0)
            for g in range(grp):
                st, dpt = nxt
                if g + 1 < grp:
                    nxt = products(g + 1)
                q, k, do = q_ref[g], k_ref[g], do_ref[g]
                if masked:
                    st = _causal_t(st, tk, tq)
                pt = jnp.exp(st - lse_ref[g])
                delta = jnp.sum((do.astype(F32) * o_ref[g].astype(F32)).T, axis=0, keepdims=True)
                dst = (pt * (dpt - delta)).astype(BF16)
                dva_ref[g] += _dot(pt.astype(BF16), do)
                dka_ref[g] += _dot(dst, q)
                dq_ref[g, rows, :] += _dot_tn(dst, k)

        @pl.when(qi > ki)
        def _():
            step(False)

        @pl.when(qi == ki)
        def _():
            step(True)

        @pl.when(qi == nq - 1)
        def _():
            dk_ref[...] = dka_ref[...]
            dv_ref[...] = dva_ref[...].astype(BF16)

    qspec = pl.BlockSpec((grp, tq, 128), lambda h, t: (h, _tri_cols(t, nq)[1], 0))
    lspec = pl.BlockSpec((grp, 1, tq), lambda h, t: (h, 0, _tri_cols(t, nq)[1]))
    kspec = pl.BlockSpec((grp, tk, 128), lambda h, t: (h, _tri_cols(t, nq)[0], 0))
    return _carried_call(
        body, ex, (hh // grp, nq * (nq + 1) // 2), [qspec, kspec, kspec, qspec, qspec, lspec],
        [pl.BlockSpec((grp, s, 128), lambda h, t: (h, 0, 0)), kspec, kspec],
        [jax.ShapeDtypeStruct((hh, s, 128), F32), jax.ShapeDtypeStruct((hh, s, 128), F32),
         jax.ShapeDtypeStruct((hh, s, 128), BF16)],
        [pltpu.VMEM((grp, tk, 128), F32), pltpu.VMEM((grp, tk, 128), F32)],
        ("arbitrary", "arbitrary"), name, (qa, ka, va, oa, doa, lse))


def attn_post(dqa, dka, dva, proj, gq2, gk2, dproj, name):
    s = proj.shape[0]
    tm = _tile(s, 256)

    def body(dq_ref, dk_ref, dv_ref, q_ref, k_ref, gq_ref, gk_ref, dp_any, dp_ref, dc_ref, dgq_ref, dgk_ref):
        lane, lo = _lanes()

        @pl.when(pl.program_id(0) == 0)
        def _():
            dgq_ref[...] = jnp.zeros_like(dgq_ref)
            dgk_ref[...] = jnp.zeros_like(dgk_ref)

        def pair(ref, j):
            return jnp.where(lo, ref[2 * j].astype(F32), _swap_halves(ref[2 * j + 1].astype(F32)))

        def norm_bwd(raw, g, dhat, scale):
            r = lax.rsqrt(_half_mean(raw * raw, lo) + EPS)
            y = raw * r
            dy = dhat * (g * scale)
            return r * (dy - y * _half_mean(dy * y, lo)), jnp.sum(dhat * y, axis=0, keepdims=True) * scale

        dc = jnp.zeros((tm, 128), F32)
        for j in range(HEADS // 2):
            cols = slice(128 * j, 128 * (j + 1))
            dq, dgq = norm_bwd(q_ref[:, cols].astype(F32), gq_ref[...], pair(dq_ref, j), ATTN_SCALE)
            dk, dgk = norm_bwd(k_ref[:, cols].astype(F32), gk_ref[...], pair(dk_ref, j), 1.0)
            dgq_ref[...] += dgq
            dgk_ref[...] += dgk
            dp_ref[:, cols] = dq.astype(BF16)
            dp_ref[:, D_ATTN + 128 * j:D_ATTN + 128 * (j + 1)] = dk.astype(BF16)
            dp_ref[:, 2 * D_ATTN + 128 * j:2 * D_ATTN + 128 * (j + 1)] = pair(dv_ref, j).astype(BF16)
            for e in range(2):
                h = 2 * j + e
                both = jnp.where(lane == LANE_C, dq_ref[h], 0.0) - jnp.where(lane == LANE_ONE, dk_ref[h], 0.0)
                dc = jnp.where(lane == h, jnp.sum(both, axis=-1, keepdims=True), dc)
        dp_ref[:, 3 * D_ATTN:] = jnp.zeros((tm, DPROJ_TAIL - 3 * D_ATTN), BF16)
        dc_ref[...] = dc

    heads = lambda: pl.BlockSpec((HEADS, tm, 128), lambda i: (0, i, 0))
    vec = pl.BlockSpec((1, 128), lambda i: (0, 0))
    first = N_REST // D_ATTN
    return pl.pallas_call(
        body, grid=(s // tm,),
        in_specs=[heads(), heads(), heads(), pl.BlockSpec((tm, D_ATTN), lambda i: (i, first)),
                  pl.BlockSpec((tm, D_ATTN), lambda i: (i, first + 1)), vec, vec, pl.BlockSpec(memory_space=pl.ANY)],
        out_specs=[pl.BlockSpec((tm, DPROJ_TAIL), lambda i: (i, N_REST // DPROJ_TAIL)),
                   pl.BlockSpec((tm, 128), lambda i: (i, 0)), vec, vec],
        out_shape=[jax.ShapeDtypeStruct(dproj.shape, BF16), jax.ShapeDtypeStruct((s, 128), F32),
                   jax.ShapeDtypeStruct((1, 128), F32), jax.ShapeDtypeStruct((1, 128), F32)],
        input_output_aliases={7: 0},
        compiler_params=_cparams(("arbitrary",)), name=name)(dqa, dka, dva, proj, proj, gq2, gk2, dproj)


def _pool_groups(tm):
    gid = lax.broadcasted_iota(jnp.int32, (1, D_POOL), 1) // (D_POOL // 4)
    win = jnp.where(gid == 0, 2.0, jnp.where(gid == 1, 4.0, jnp.where(gid == 2, 8.0, 16.0)))
    return gid, win


def _by_group(gid, v2, v4, v8, v16):
    return jnp.where(gid == 0, v2, jnp.where(gid == 1, v4, jnp.where(gid == 2, v8, v16)))


def _branches(rest_ref, halo_ref, a_ref, wa_ref, wc_ref, wp_ref, sc_ref, cw_ref, ti, tm):
    f = lambda v: v.astype(F32)
    cx, cb, cc, px = f(rest_ref[:, 0:256]), f(rest_ref[:, 256:512]), f(rest_ref[:, 512:768]), f(rest_ref[:, 768:1024])
    live = jnp.where(ti > 0, 1.0, 0.0)
    hz = f(halo_ref[:, 0:256]) * f(halo_ref[:, 512:768]) * live
    hp = f(halo_ref[:, 768:1024]) * live
    z = cc * cx
    zf = jnp.concatenate([hz, z], axis=0)
    z1 = pltpu.roll(zf, 1, 0)[HALO:]
    z2 = pltpu.roll(zf, 2, 0)[HALO:]
    cw = cw_ref[...]
    conv = cw[2:3] * z + cw[1:2] * z1 + cw[0:1] * z2
    uc = cb * conv
    pf = jnp.concatenate([hp, px], axis=0)
    s2 = pf + pltpu.roll(pf, 1, 0)
    s4 = s2 + pltpu.roll(s2, 2, 0)
    s8 = s4 + pltpu.roll(s4, 4, 0)
    s16 = s8 + pltpu.roll(s8, 8, 0)
    gid, win = _pool_groups(tm)
    t = (ti * tm + lax.broadcasted_iota(jnp.int32, (tm, 1), 0)).astype(F32)
    inv = 1.0 / jnp.minimum(t + 1.0, win)
    dpool = _by_group(gid, s2[HALO:], s4[HALO:], s8[HALO:], s16[HALO:]) * inv - px
    _, lo = _lanes()
    a_tok = [jnp.where(lo, f(a_ref[2 * j]), _swap_halves(f(a_ref[2 * j + 1]))).astype(BF16) for j in range(HEADS // 2)]
    y_attn = _dot(a_tok[0], wa_ref[0:128, :])
    for j in range(1, HEADS // 2):
        y_attn += _dot(a_tok[j], wa_ref[128 * j:128 * (j + 1), :])
    y_conv = _dot(uc.astype(BF16), wc_ref[...])
    y_pool_raw = _dot(dpool.astype(BF16), wp_ref[...])
    sg = [_sigmoid(f(rest_ref[:, 1024 + i * D_MODEL:1024 + (i + 1) * D_MODEL])) for i in range(3)]
    return dict(cx=cx, cb=cb, cc=cc, z=z, z1=z1, z2=z2, conv=conv, uc=uc, dpool=dpool, inv=inv, gid=gid, a_tok=a_tok,
                y_attn=y_attn, y_conv=y_conv, y_pool_raw=y_pool_raw, sg=sg, cw=cw)


def _mix_specs(tm, ti_of):
    blocks_per_tile = tm // HALO
    return [
        pl.BlockSpec((tm, N_REST), lambda i: (ti_of(i), 0)),
        pl.BlockSpec((HALO, 1024), lambda i: (jnp.maximum(ti_of(i) * blocks_per_tile - 1, 0), 0)),
        pl.BlockSpec((HEADS, tm, 128), lambda i: (0, ti_of(i), 0)),
        pl.BlockSpec((D_ATTN, D_MODEL), lambda i: (0, 0)),
        pl.BlockSpec((D_CONV, D_MODEL), lambda i: (0, 0)),
        pl.BlockSpec((D_POOL, D_MODEL), lambda i: (0, 0)),
        pl.BlockSpec((1, D_MODEL), lambda i: (0, 0)),
        pl.BlockSpec((8, D_CONV), lambda i: (0, 0)),
    ]


def mix_fwd(proj, a, x, wa, wc, wp, scale, cw, wo, name, ex=None):
    s = x.shape[0]
    tm = _tile(s, 256)

    def body(rest_ref, halo_ref, a_ref, wa_ref, wc_ref, wp_ref, sc_ref, cw_ref, wo_ref, x_ref, o_ref):
        b = _branches(rest_ref, halo_ref, a_ref, wa_ref, wc_ref, wp_ref, sc_ref, cw_ref, pl.program_id(0), tm)
        merged = b["sg"][0] * b["y_attn"] + b["sg"][1] * b["y_conv"] + b["sg"][2] * (b["y_pool_raw"] * sc_ref[...])
        o_ref[...] = x_ref[...] + _dot(merged.astype(BF16), wo_ref[...])

    (x1,), carried = _carried_call(
        body, ex, (s // tm,),
        _mix_specs(tm, lambda i: i) + [pl.BlockSpec((D_MODEL, D_MODEL), lambda i: (0, 0)),
                                       pl.BlockSpec((tm, D_MODEL), lambda i: (i, 0))],
        [pl.BlockSpec((tm, D_MODEL), lambda i: (i, 0))], [jax.ShapeDtypeStruct((s, D_MODEL), F32)], [],
        ("arbitrary",), name, (proj, proj, a, wa, wc, wp, scale, cw, wo, x))
    return x1, carried


def mix_bwd(proj, a, dx1, wa, wc, wp, scale, cw, wo, name):
    s = dx1.shape[0]
    tm = _tile(s, 256)
    nt = s // tm
    ti_of = lambda i: nt - 1 - i
    n = tm + HALO

    def body(rest_ref, halo_ref, a_ref, wa_ref, wc_ref, wp_ref, sc_ref, cw_ref, wo_ref,
             dx_ref, dp_ref, da_ref, at_ref, mg_ref, dya_ref, dyc_ref, dyp_ref, uc_ref, dd_ref, dsc_ref, dcw_ref,
             cdc_ref, cde_ref):
        i = pl.program_id(0)
        ti = ti_of(i)

        @pl.when(i == 0)
        def _():
            cdc_ref[...] = jnp.zeros_like(cdc_ref)
            cde_ref[...] = jnp.zeros_like(cde_ref)
            dsc_ref[...] = jnp.zeros_like(dsc_ref)
            dcw_ref[...] = jnp.zeros_like(dcw_ref)

        b = _branches(rest_ref, halo_ref, a_ref, wa_ref, wc_ref, wp_ref, sc_ref, cw_ref, ti, tm)
        sg, sc = b["sg"], sc_ref[...]
        y_pool = b["y_pool_raw"] * sc
        merged = sg[0] * b["y_attn"] + sg[1] * b["y_conv"] + sg[2] * y_pool
        mg_ref[...] = merged.astype(BF16)
        dm = _dot_nt(dx_ref[...].astype(BF16), wo_ref[...])
        dys = [dm * sg[j] for j in range(3)]
        for j, y in enumerate((b["y_attn"], b["y_conv"], y_pool)):
            dp_ref[:, 1024 + j * D_MODEL:1024 + (j + 1) * D_MODEL] = (dys[j] * y * (1.0 - sg[j])).astype(BF16)
        dya = dys[0].astype(BF16)
        dya_ref[...] = dya
        _, lo = _lanes()
        for j in range(HEADS // 2):
            at_ref[:, 128 * j:128 * (j + 1)] = b["a_tok"][j]
            da = _dot_nt(dya, wa_ref[128 * j:128 * (j + 1), :])
            da_ref[2 * j] = jnp.where(lo, da, 0.0).astype(BF16)
            da_ref[2 * j + 1] = jnp.where(lo, _swap_halves(da), 0.0).astype(BF16)
        dyc = dys[1].astype(BF16)
        dyc_ref[...] = dyc
        duc = _dot_nt(dyc, wc_ref[...])
        dyp = dys[2]
        dsc_ref[...] += jnp.sum(dyp * b["y_pool_raw"], axis=0, keepdims=True)
        dypr = (dyp * sc).astype(BF16)
        dyp_ref[...] = dypr
        ddp = _dot_nt(dypr, wp_ref[...])
        uc_ref[...] = b["uc"].astype(BF16)
        dd_ref[...] = b["dpool"].astype(BF16)

        dconv = duc * b["cb"]
        dp_ref[:, 256:512] = (duc * b["conv"]).astype(BF16)
        dcf = jnp.concatenate([dconv, cdc_ref[...]], axis=0)
        cw = b["cw"]
        dz = cw[2:3] * dconv + cw[1:2] * pltpu.roll(dcf, n - 1, 0)[:tm] + cw[0:1] * pltpu.roll(dcf, n - 2, 0)[:tm]
        dp_ref[:, 0:256] = (dz * b["cc"]).astype(BF16)
        dp_ref[:, 512:768] = (dz * b["cx"]).astype(BF16)
        dcw_ref[0:1, :] += jnp.sum(dconv * b["z2"], axis=0, keepdims=True)
        dcw_ref[1:2, :] += jnp.sum(dconv * b["z1"], axis=0, keepdims=True)
        dcw_ref[2:3, :] += jnp.sum(dconv * b["z"], axis=0, keepdims=True)
        cdc_ref[...] = dconv[:HALO]

        e = ddp * b["inv"]
        ef = jnp.concatenate([e, cde_ref[...]], axis=0)
        r2 = ef + pltpu.roll(ef, n - 1, 0)
        r4 = r2 + pltpu.roll(r2, n - 2, 0)
        r8 = r4 + pltpu.roll(r4, n - 4, 0)
        r16 = r8 + pltpu.roll(r8, n - 8, 0)
        dp_ref[:, 768:1024] = (_by_group(b["gid"], r2[:tm], r4[:tm], r8[:tm], r16[:tm]) - ddp).astype(BF16)
        cde_ref[...] = e[:HALO]

    tile = lambda w: pl.BlockSpec((tm, w), lambda i: (ti_of(i), 0))
    whole = lambda r, c: pl.BlockSpec((r, c), lambda i: (0, 0))
    bf = lambda w: jax.ShapeDtypeStruct((s, w), BF16)
    return pl.pallas_call(
        body, grid=(nt,),
        in_specs=_mix_specs(tm, ti_of) + [whole(D_MODEL, D_MODEL), tile(D_MODEL)],
        out_specs=[tile(N_REST), pl.BlockSpec((HEADS, tm, 128), lambda i: (0, ti_of(i), 0)), tile(D_ATTN),
                   tile(D_MODEL), tile(D_MODEL), tile(D_MODEL), tile(D_MODEL),
                   tile(D_CONV), tile(D_POOL), whole(1, D_MODEL), whole(8, D_CONV)],
        out_shape=[bf(DPROJ_COLS), jax.ShapeDtypeStruct((HEADS, s, 128), BF16), bf(D_ATTN),
                   bf(D_MODEL), bf(D_MODEL), bf(D_MODEL), bf(D_MODEL), bf(D_CONV), bf(D_POOL),
                   jax.ShapeDtypeStruct((1, D_MODEL), F32), jax.ShapeDtypeStruct((8, D_CONV), F32)],
        scratch_shapes=[pltpu.VMEM((HALO, D_CONV), F32), pltpu.VMEM((HALO, D_POOL), F32)],
        compiler_params=_cparams(("arbitrary",)), name=name)(proj, proj, a, wa, wc, wp, scale, cw, wo, dx1)


def _adamw_math(w, g, m, v):
    m = ADAM_B1 * m + (1.0 - ADAM_B1) * g
    v = ADAM_B2 * v + (1.0 - ADAM_B2) * (g * g)
    m_hat = m / (1.0 - ADAM_B1 ** ADAM_STEP)
    v_hat = v / (1.0 - ADAM_B2 ** ADAM_STEP)
    delta = -ADAM_LR * (m_hat / (jnp.sqrt(v_hat) + ADAM_EPS) + ADAM_WD * w)
    return delta, m, v


ADAMW_PARTS_BLOCK_BYTES = 4 * 2 ** 20


def _row_tile(rows, cols, copies, itemsize):
    row_bytes = copies * (-(-cols // 128) * 128) * itemsize
    fits = [t for t in range(16, rows + 1, 16) if rows % t == 0 and t * row_bytes <= ADAMW_PARTS_BLOCK_BYTES]
    return max(fits) if fits else rows


def pair_sum(blocks, stage, me, name):
    n_slots, rows, cols = stage.shape
    tr = _row_tile(rows, cols, 1, 4)

    def body(me_ref, a_ref, b_ref, o_ref):
        o_ref[...] = (a_ref[...].astype(F32) + b_ref[...].astype(F32)).astype(BF16)

    slot = pl.BlockSpec((None, tr, cols), lambda i, r, me_ref: (i, r, 0))
    return pl.pallas_call(
        body, out_shape=jax.ShapeDtypeStruct(stage.shape, BF16),
        grid_spec=pltpu.PrefetchScalarGridSpec(
            num_scalar_prefetch=1, grid=(n_slots, rows // tr),
            in_specs=[pl.BlockSpec((None, tr, cols), lambda i, r, me_ref: (me_ref[0] ^ (2 * i), r, 0)), slot],
            out_specs=slot),
        compiler_params=_cparams(("parallel", "parallel")), name=name)(me.reshape(1), blocks, stage)


def adamw_sum(parts, w, m, v, name):
    layers, rows, cols = w.shape
    n_parts = parts.shape[1]
    if rows % 16 == 0:
        tr, tc = _row_tile(rows, cols, n_parts, parts.dtype.itemsize), cols
    else:
        tr, tc = rows, _pick(cols, (256, 128))

    def body(p_ref, w_ref, m_ref, v_ref, g_ref, d_ref, nm_ref, nv_ref):
        g = p_ref[0].astype(F32)
        for i in range(1, n_parts):
            g = g + p_ref[i].astype(F32)
        g_ref[...] = g
        d_ref[...], nm_ref[...], nv_ref[...] = _adamw_math(w_ref[...], g, m_ref[...], v_ref[...])

    spec = pl.BlockSpec((None, tr, tc), lambda l, i, j: (l, i, j))
    return pl.pallas_call(
        body, grid=(layers, rows // tr, cols // tc),
        in_specs=[pl.BlockSpec((None, n_parts, tr, tc), lambda l, i, j: (l, 0, i, j)), spec, spec, spec],
        out_specs=[spec] * 4, out_shape=[jax.ShapeDtypeStruct((layers, rows, cols), F32)] * 4,
        compiler_params=_cparams(("parallel", "parallel", "parallel")), name=name)(parts, w, m, v)


def _me():
    return lax.axis_index("x"), lax.axis_index("y"), lax.axis_index("c")


N_PEERS = N_DEV - 1


def all_gather(shards, name):
    n = len(shards)
    any_spec = pl.BlockSpec(memory_space=pl.ANY)

    def body(*refs):
        x_refs, out_refs = refs[:n], refs[n:2 * n]
        send_sems, recv_sems, local_sems = refs[2 * n:]
        x, y, c = _me()
        me, sibling = (x, y, c), (x, y, 1 - c)
        chips = [(1 - x, y), (x, 1 - y), (1 - x, 1 - y)]

        def copy(t, k, block, to, from_input=False):
            slot = out_refs[t].at[4 * block[0] + 2 * block[1] + block[2]]
            return pltpu.make_async_remote_copy(
                src_ref=x_refs[t] if from_input else slot, dst_ref=slot, send_sem=send_sems.at[N_PEERS * t + k],
                recv_sem=recv_sems.at[N_PEERS * t + k], device_id=to, device_id_type=pl.DeviceIdType.MESH)

        mine = [pltpu.make_async_copy(x_refs[t], out_refs[t].at[4 * x + 2 * y + c], local_sems.at[t]) for t in range(n)]
        started = []
        for t in range(n):
            mine[t].start()
            started.append(copy(t, 0, me, sibling, from_input=True))
            started += [copy(t, 1 + j, me, (*chip, c), from_input=True) for j, chip in enumerate(chips)]
        for cp in started:
            cp.start()
        for j, chip in enumerate(chips):
            for t in range(n):
                copy(t, 1 + j, (*chip, c), me).wait_recv()
                fwd = copy(t, 4 + j, (*chip, c), sibling)
                fwd.start()
                started.append(fwd)
        for t in range(n):
            copy(t, 0, sibling, me).wait_recv()
            for j, chip in enumerate(chips):
                copy(t, 4 + j, (*chip, 1 - c), me).wait_recv()
        for cp in started:
            cp.wait_send()
        for cp in mine:
            cp.wait()

    return pl.pallas_call(
        body, out_shape=[jax.ShapeDtypeStruct((N_DEV,) + s.shape, s.dtype) for s in shards],
        in_specs=[any_spec] * n, out_specs=[any_spec] * n,
        scratch_shapes=[pltpu.SemaphoreType.DMA((N_PEERS * n,)), pltpu.SemaphoreType.DMA((N_PEERS * n,)),
                        pltpu.SemaphoreType.DMA((n,))],
        name=name)(*shards)


SIBLING = 1
OTHER_CHIPS = (2, 4, 6)
SAME_CORE = (0,) + OTHER_CHIPS


class Exchange:
    def __init__(self, inputs, out_shapes, aliases, copies, local=()):
        self.inputs, self.out_shapes, self.aliases = list(inputs), list(out_shapes), aliases
        self._copies, self._local = list(copies), list(local)
        self.scratch = [pltpu.SemaphoreType.DMA((len(self._copies),)), pltpu.SemaphoreType.DMA((len(self._copies),)),
                        pltpu.SemaphoreType.DMA((max(len(self._local), 1),))]

    def _build(self, ins, outs, sems):
        send_sems, recv_sems, local_sems = sems
        x, y, c = _me()
        me = 4 * x + 2 * y + c
        local = [functools.partial(pltpu.make_async_copy, src(ins, outs, me), dst(outs, me), local_sems.at[i])
                 for i, (src, dst) in enumerate(self._local)]
        sends, recvs = [], []
        for i, (mask, src, dst) in enumerate(self._copies):
            px, py, pc = x ^ ((mask >> 2) & 1), y ^ ((mask >> 1) & 1), c ^ (mask & 1)
            pair = dict(send_sem=send_sems.at[i], recv_sem=recv_sems.at[i], device_id_type=pl.DeviceIdType.MESH)
            sends.append(functools.partial(
                pltpu.make_async_remote_copy, src_ref=src(ins, outs, me), dst_ref=dst(outs, me), device_id=(px, py, pc), **pair))
            recvs.append(functools.partial(
                pltpu.make_async_remote_copy, src_ref=src(ins, outs, me), dst_ref=dst(outs, me ^ mask), device_id=(x, y, c), **pair))
        return local, sends, recvs

    def start(self, ins, outs, sems):
        local, sends, _ = self._build(ins, outs, sems)
        for make in local + sends:
            make().start()

    def drain(self, ins, outs, sems):
        local, sends, recvs = self._build(ins, outs, sems)
        for make in recvs:
            make().wait_recv()
        for make in sends:
            make().wait_send()
        for make in local:
            make().wait()


def _bind(fn, *args):
    return functools.partial(fn, *args)


def join_exchanges(a, b):
    if a is None or b is None:
        return a or b
    na_in, na_out = len(a.inputs), len(a.out_shapes)

    def src_a(fn):
        return lambda ins, outs, me: fn(ins[:na_in], outs[:na_out], me)

    def dst_a(fn):
        return lambda outs, who: fn(outs[:na_out], who)

    def src_b(fn):
        return lambda ins, outs, me: fn(ins[na_in:], outs[na_out:], me)

    def dst_b(fn):
        return lambda outs, who: fn(outs[na_out:], who)

    copies = [(m, src_a(s), dst_a(d)) for m, s, d in a._copies] + [(m, src_b(s), dst_b(d)) for m, s, d in b._copies]
    local = [(src_a(s), dst_a(d)) for s, d in a._local] + [(src_b(s), dst_b(d)) for s, d in b._local]
    aliases = dict(a.aliases)
    aliases.update({na_in + i: na_out + o for i, o in b.aliases.items()})
    return Exchange(a.inputs + b.inputs, a.out_shapes + b.out_shapes, aliases, copies, local)


def gather_over_ici(shards):
    copies = [(mask, _bind(lambda t, ins, outs, me: ins[t], t), _bind(lambda t, outs, sender: outs[t].at[sender], t))
              for t in range(len(shards)) for mask in OTHER_CHIPS]
    local = [(_bind(lambda t, ins, outs, me: ins[t], t), _bind(lambda t, outs, me: outs[t].at[me], t))
             for t in range(len(shards))]
    return Exchange(shards, [jax.ShapeDtypeStruct((N_DEV,) + s.shape, s.dtype) for s in shards], {}, copies, local)


def gather_over_d2d(gathered):
    copies = [(SIBLING, _bind(lambda t, m, ins, outs, me: outs[t].at[me ^ m], t, m),
               _bind(lambda t, m, outs, sender: outs[t].at[sender ^ m], t, m))
              for t in range(len(gathered)) for m in SAME_CORE]
    return Exchange(gathered, [jax.ShapeDtypeStruct(g.shape, g.dtype) for g in gathered],
                    {t: t for t in range(len(gathered))}, copies)


def scatter_over_d2d(blocks):
    copies = [(SIBLING, _bind(lambda t, m, ins, outs, me: ins[t].at[me ^ SIBLING ^ m], t, m),
               _bind(lambda t, i, outs, sender: outs[t].at[i], t, i))
              for t in range(len(blocks)) for i, m in enumerate(SAME_CORE)]
    return Exchange(blocks, [jax.ShapeDtypeStruct((len(SAME_CORE),) + b.shape[1:], b.dtype) for b in blocks], {}, copies)


def scatter_over_ici(pair_sums, bufs, layer):
    n = len(pair_sums)
    copies = [(m, _bind(lambda t, i, ins, outs, me: ins[t].at[i], t, i),
               _bind(lambda t, i, outs, sender: outs[t].at[layer, i], t, i))
              for t in range(n) for i, m in enumerate(SAME_CORE) if m]
    local = [(_bind(lambda t, ins, outs, me: ins[t].at[0], t), _bind(lambda t, outs, me: outs[t].at[layer, 0], t))
             for t in range(n)]
    return Exchange(list(pair_sums) + list(bufs), [jax.ShapeDtypeStruct(b.shape, b.dtype) for b in bufs],
                    {n + t: t for t in range(n)}, copies, local)


def run_exchange(ex, name):
    any_spec = pl.BlockSpec(memory_space=pl.ANY)
    n_in, n_out = len(ex.inputs), len(ex.out_shapes)

    def body(*refs):
        ins, outs, sems = refs[:n_in], refs[n_in:n_in + n_out], refs[n_in + n_out:]
        ex.start(ins, outs, sems)
        ex.drain(ins, outs, sems)

    return pl.pallas_call(
        body, out_shape=ex.out_shapes, in_specs=[any_spec] * n_in, out_specs=[any_spec] * n_out,
        input_output_aliases=ex.aliases, scratch_shapes=ex.scratch, name=name)(*ex.inputs)


MATRICES = ("w_in", "w_attn_out", "w_conv_out", "pool_w", "w_o", "w_ffn_in", "w_ffn_out")
TRANSPOSED = ("w_in", "w_ffn_in")
EVERY = tuple(range(len(MATRICES)))
IN_PROJ_PART, ATTN_PART, MIX_PART = (0,), (1, 2, 3, 4, 5), (6,)
LATE = (0,)
EARLY = EVERY[1:]
EARLY_FIRST, EARLY_SECOND = (4, 6), (1, 2, 3, 5)
SHARD_INFO = {
    "w_in": ((DEPTH, D_IN // N_DEV, D_MODEL), 1),
    "w_attn_out": ((DEPTH, D_ATTN, D_MODEL // N_DEV), 2),
    "w_conv_out": ((DEPTH, D_CONV, D_MODEL // N_DEV), 2),
    "pool_w": ((DEPTH, 4, 64, 256 // N_DEV), 3),
    "w_o": ((DEPTH, D_MODEL // N_DEV, D_MODEL), 1),
    "w_ffn_in": ((DEPTH, 2 * D_FF // N_DEV, D_MODEL), 1),
    "w_ffn_out": ((DEPTH, D_FF // N_DEV, D_MODEL), 1),
}


def _handled(name, t):
    return jnp.transpose(t, (0, 2, 1)) if name in TRANSPOSED else t
VECTORS = ("norm_mix_g", "forget_b", "q_norm_g", "k_norm_g", "pool_scale", "norm_ffn_g")
VECTOR_SHAPES = {"norm_mix_g": (DEPTH, D_MODEL), "forget_b": (DEPTH, HEADS), "q_norm_g": (DEPTH, HEAD_DIM),
                 "k_norm_g": (DEPTH, HEAD_DIM), "pool_scale": (DEPTH, D_MODEL), "norm_ffn_g": (DEPTH, D_MODEL)}
CONV_W_FULL = (DEPTH, 3, D_CONV)


def _size(shape):
    n = 1
    for v in shape:
        n *= v
    return n


def _pack(arrays, rows, cols):
    flat = jnp.concatenate([a.reshape(-1) for a in arrays])
    return jnp.pad(flat, (0, rows * cols - flat.shape[0])).reshape(rows, cols)


def _unpack(packed, shapes):
    flat, out, off = packed.reshape(-1), [], 0
    for shp in shapes:
        out.append(flat[off:off + _size(shp)].reshape(shp))
        off += _size(shp)
    return out


def _join_shards(stacked, axis):
    moved = jnp.moveaxis(stacked, 0, axis)
    shp = list(moved.shape)
    shp[axis:axis + 2] = [shp[axis] * shp[axis + 1]]
    return moved.reshape(shp)


def _cut_shards(full, axis):
    shp = list(full.shape)
    shp[axis:axis + 1] = [N_DEV, shp[axis] // N_DEV]
    return jnp.moveaxis(full.reshape(shp), axis, 0)


N_MOVED = 1544
SHARD_ROWS = D_IN // N_DEV


def _regroup_w_in(shards):
    wt = shards.reshape(D_IN, shards.shape[2])
    pad = jnp.zeros((N_FULL - D_IN, wt.shape[1]), wt.dtype)
    return jnp.concatenate([wt[N_MOVED:], wt[:N_MOVED], pad], axis=0)


def _ungroup_w_in(wpt):
    def kernel_rows(a, b):
        if b <= N_MOVED:
            return [wpt[a + D_IN - N_MOVED:b + D_IN - N_MOVED]]
        if a >= N_MOVED:
            return [wpt[a - N_MOVED:b - N_MOVED]]
        return kernel_rows(a, N_MOVED) + kernel_rows(N_MOVED, b)

    return jnp.stack([jnp.concatenate(kernel_rows(s * SHARD_ROWS, (s + 1) * SHARD_ROWS), axis=0) for s in range(N_DEV)])


def _pool_block_diag(w):
    out = jnp.zeros((D_POOL, D_MODEL), w.dtype)
    for g in range(4):
        out = lax.dynamic_update_slice(out, w[g], (g * 64, g * 256))
    return out


def _pool_from_block_diag(wbd):
    return jnp.stack([wbd[g * 64:(g + 1) * 64, g * 256:(g + 1) * 256] for g in range(4)])


def _layer_weights(mats, vec, conv_w, l):
    wp = _pool_block_diag(mats["pool_w"])
    row = lambda v: v.reshape(1, -1)
    fb = jnp.zeros((1, 128), F32).at[0, :HEADS].set(vec["forget_b"][l])
    cw = jnp.zeros((8, D_CONV), F32).at[:3].set(conv_w[l])
    twice = lambda v: jnp.tile(v.reshape(1, -1), (1, 2))
    return dict(
        wt_in=_regroup_w_in(mats["w_in"]), wt_ffn_in=mats["w_ffn_in"], w_ffn_out=mats["w_ffn_out"],
        wa=mats["w_attn_out"], wc=mats["w_conv_out"], wp=wp, wo=mats["w_o"],
        g_mix=row(vec["norm_mix_g"][l]), g_ffn=row(vec["norm_ffn_g"][l]), gq2=twice(vec["q_norm_g"][l]),
        gk2=twice(vec["k_norm_g"][l]), scale=row(vec["pool_scale"][l]), fb=fb, cw=cw)


def _layer_fwd(x, w, l, comm):
    (proj, h), half_a = norm_matmul(x, w["g_mix"], w["wt_in"], N_MAIN, f"in_proj_{l}", comm.gather_ici(l + 1, IN_PROJ_PART))
    z, c = forget_fwd(h, w["wt_in"], w["fb"], f"forget_fwd_{l}")
    qa, ka, va, vt = attn_prep(proj, c, w["gq2"], w["gk2"], f"attn_prep_{l}")
    (oa, lse), half_b = attn_forward(qa, ka, vt, f"attn_fwd_{l}", comm.gather_ici(l + 1, ATTN_PART))
    x1, half_c = mix_fwd(proj, oa, x, w["wa"], w["wc"], w["wp"], w["scale"], w["cw"], w["wo"], f"mix_fwd_{l}",
                         comm.gather_ici(l + 1, MIX_PART))
    half = list(half_a) + list(half_b) + list(half_c)
    (gu, h2), gathered = norm_matmul(x1, w["g_ffn"], w["wt_ffn_in"], 2 * D_FF, f"ffn_in_{l}", comm.gather_d2d(l + 1, half))
    x2 = swiglu_matmul(gu, w["w_ffn_out"], x1, f"ffn_out_{l}")
    saved = dict(x=x, proj=proj, h=h, z=z, qa=qa, ka=ka, va=va, oa=oa, lse=lse, x1=x1, gu=gu, h2=h2)
    return x2, saved, gathered


def _layer_bwd(dx2, sv, w, l, comm):
    g = {}
    (dgu, act), stage = swiglu_bwd(dx2, sv["gu"], w["w_ffn_out"], f"ffn_out_bwd_{l}", comm.scatter_d2d(l + 1))
    sums = comm.pair_sums(l + 1, stage)
    g["w_ffn_out"] = tn_matmul(act, dx2, f"dw_ffn_out_{l}")
    g["w_ffn_in"] = tn_matmul(dgu, sv["h2"], f"dw_ffn_in_{l}")
    (dx1, dg), _ = matmul_normbwd(dgu, w["wt_ffn_in"], sv["x1"], w["g_ffn"], dx2, f"ffn_in_bwd_{l}")
    g["norm_ffn_g"] = dg[0]

    (dproj, doa, a_tok, merged, dya, dyc, dyp, uc, dd, dscale, dcw) = mix_bwd(
        sv["proj"], sv["oa"], dx1, w["wa"], w["wc"], w["wp"], w["scale"], w["cw"], w["wo"], f"mix_bwd_{l}")
    g["w_o"] = tn_matmul(merged, dx1, f"dw_o_{l}")
    g["w_attn_out"], g["w_conv_out"], dwp = tn_matmuls([(a_tok, dya), (uc, dyc), (dd, dyp)], f"dw_branches_{l}")
    g["pool_w"] = _pool_from_block_diag(dwp)
    g["pool_scale"] = dscale[0]
    g["conv_w"] = dcw[:3]

    early = comm.early(l)
    comm.grads(l, g)
    above = comm.scatter_ici(l + 1, sums)
    (dqa, dka, dva), got = attn_backward(sv["qa"], sv["ka"], sv["va"], sv["oa"], doa, sv["lse"], f"attn_bwd_{l}",
                                         join_exchanges(above, comm.scatter_d2d(l, early) if early else None))
    n_above = len(above.out_shapes) if above else 0
    comm.scattered(got[:n_above])
    early_sums = dict(zip(early, comm.pair_sums(l, got[n_above:], early))) if early else {}
    early_ici = lambda which: comm.scatter_ici(l, [early_sums[t] for t in which], which) if early else None
    dproj, dc, dgq, dgk = attn_post(dqa, dka, dva, sv["proj"], w["gq2"], w["gk2"], dproj, f"attn_post_{l}")
    g["q_norm_g"] = dgq[0, :HEAD_DIM] + dgq[0, HEAD_DIM:]
    g["k_norm_g"] = dgk[0, :HEAD_DIM] + dgk[0, HEAD_DIM:]
    dproj, db = forget_bwd(dc, sv["z"], dproj, f"forget_bwd_{l}")
    g["forget_b"] = db[0, :HEADS]

    dw_in = tn_matmul(dproj, sv["h"], f"dw_in_{l}", m_cols=N_FULL, ex=early_ici(EARLY_FIRST))
    if early:
        dw_in, got = dw_in
        comm.scattered(got, EARLY_FIRST)
    g["w_in"] = _ungroup_w_in(dw_in)
    (dx, dg), got = matmul_normbwd(dproj, w["wt_in"], sv["x"], w["g_mix"], dx1, f"in_proj_bwd_{l}", k=N_FULL,
                                   ex=early_ici(EARLY_SECOND))
    comm.scattered(got, EARLY_SECOND if early else None)
    g["norm_mix_g"] = dg[0]
    comm.grads(l, g)
    return dx


def _local_step(x, tgt, comm):
    ws, saved = [], []
    w = comm.weights(0, None)
    for l in range(DEPTH):
        ws.append(w)
        x, sv, gathered = _layer_fwd(x, w, l, comm)
        saved.append(sv)
        if l + 1 < DEPTH:
            w = comm.weights(l + 1, gathered)
    sq, dx = loss_kernel(x, tgt, "loss")
    for l in reversed(range(DEPTH)):
        dx = _layer_bwd(dx, saved[l], ws[l], l, comm)
    comm.finish()
    return sq[0, 0], dx


def kernel(x, norm_mix_g, w_in, forget_b, q_norm_g, k_norm_g, w_attn_out, conv_w, w_conv_out, pool_w, pool_scale, w_o, norm_ffn_g, w_ffn_in, w_ffn_out, loss_target, m_norm_mix_g, m_w_in, m_forget_b, m_q_norm_g, m_k_norm_g, m_w_attn_out, m_conv_w, m_w_conv_out, m_pool_w, m_pool_scale, m_w_o, m_norm_ffn_g, m_w_ffn_in, m_w_ffn_out, v_norm_mix_g, v_w_in, v_forget_b, v_q_norm_g, v_k_norm_g, v_w_attn_out, v_conv_w, v_w_conv_out, v_pool_w, v_pool_scale, v_w_o, v_norm_ffn_g, v_w_ffn_in, v_w_ffn_out):
    w = dict(norm_mix_g=norm_mix_g, w_in=w_in, forget_b=forget_b, q_norm_g=q_norm_g, k_norm_g=k_norm_g,
             w_attn_out=w_attn_out, conv_w=conv_w, w_conv_out=w_conv_out, pool_w=pool_w, pool_scale=pool_scale,
             w_o=w_o, norm_ffn_g=norm_ffn_g, w_ffn_in=w_ffn_in, w_ffn_out=w_ffn_out)
    m = dict(norm_mix_g=m_norm_mix_g, w_in=m_w_in, forget_b=m_forget_b, q_norm_g=m_q_norm_g, k_norm_g=m_k_norm_g,
             w_attn_out=m_w_attn_out, conv_w=m_conv_w, w_conv_out=m_w_conv_out, pool_w=m_pool_w,
             pool_scale=m_pool_scale, w_o=m_w_o, norm_ffn_g=m_norm_ffn_g, w_ffn_in=m_w_ffn_in, w_ffn_out=m_w_ffn_out)
    v = dict(norm_mix_g=v_norm_mix_g, w_in=v_w_in, forget_b=v_forget_b, q_norm_g=v_q_norm_g, k_norm_g=v_k_norm_g,
             w_attn_out=v_w_attn_out, conv_w=v_conv_w, w_conv_out=v_w_conv_out, pool_w=v_pool_w,
             pool_scale=v_pool_scale, w_o=v_w_o, norm_ffn_g=v_norm_ffn_g, w_ffn_in=v_w_ffn_in, w_ffn_out=v_w_ffn_out)
    me = 4 * lax.axis_index("x") + 2 * lax.axis_index("y") + lax.axis_index("c")
    layer_shard = {n: SHARD_INFO[n][0][1:] for n in MATRICES}
    cut_axis = {n: SHARD_INFO[n][1] - 1 for n in MATRICES}

    vec = {n: w[n] for n in VECTORS}
    rc = {n: (_size(layer_shard[n][:-1]), layer_shard[n][-1]) for n in MATRICES}

    class Comm:
        bufs = [lax.empty((DEPTH, len(SAME_CORE)) + layer_shard[n], BF16) for n in MATRICES]
        blocks = [None] * DEPTH
        small_g = [None] * DEPTH
        conv_full = None

        @staticmethod
        def shards(l):
            return [_handled(n, w[n])[l].astype(BF16) for n in MATRICES]

        @staticmethod
        def gather_ici(l, part):
            return gather_over_ici([Comm.shards(l)[t] for t in part]) if l < DEPTH else None

        @staticmethod
        def gather_d2d(l, half):
            return gather_over_d2d(half) if l < DEPTH else None

        @staticmethod
        def weights(l, gathered):
            if l == 0:
                *gathered, conv_g = all_gather(Comm.shards(0) + [_pack([conv_w], 8, 128)], "gather_0")
                Comm.conv_full = _join_shards(jnp.stack([_unpack(conv_g[i], [conv_w.shape])[0] for i in range(N_DEV)]), 2)
            mats = {n: t if n == "w_in" else _join_shards(t, cut_axis[n]) for n, t in zip(MATRICES, gathered)}
            return _layer_weights(mats, vec, Comm.conv_full, l)

        @staticmethod
        def grads(l, g):
            Comm.small_g[l] = g
            Comm.blocks[l] = [None if n not in g else g[n] if n == "w_in" else _cut_shards(g[n], cut_axis[n])
                              for n in MATRICES]

        @staticmethod
        def early(l):
            return EARLY if l == 0 else None

        @staticmethod
        def scatter_d2d(l, which=EVERY):
            return scatter_over_d2d([Comm.blocks[l][t] for t in which]) if l < DEPTH else None

        @staticmethod
        def pair_sums(l, stage, which=EVERY):
            if l >= DEPTH:
                return None
            return [pair_sum(Comm.blocks[l][t].reshape((N_DEV,) + rc[MATRICES[t]]),
                             s.reshape((len(SAME_CORE),) + rc[MATRICES[t]]), me,
                             f"pair_sum_{MATRICES[t]}_{l}").reshape(s.shape) for t, s in zip(which, stage)]

        @staticmethod
        def scatter_ici(l, sums, which=EVERY):
            return scatter_over_ici(sums, [Comm.bufs[t] for t in which], l) if l < DEPTH else None

        @staticmethod
        def scattered(results, which=EVERY):
            for t, r in zip(which or (), results):
                Comm.bufs[t] = r

        @staticmethod
        def finish():
            stage = run_exchange(Comm.scatter_d2d(0, LATE), "scatter_d2d_0")
            Comm.scattered(run_exchange(Comm.scatter_ici(0, Comm.pair_sums(0, stage, LATE), LATE), "scatter_ici_0"), LATE)

    small_g, received = Comm.small_g, Comm
    sq, dx = _local_step(x[0], loss_target[0], Comm)
    loss = lax.psum(0.5 * sq / D_MODEL, ("x", "y", "c"))

    big = {}
    for n, parts in zip(MATRICES, received.bufs):
        outs = adamw_sum(parts.reshape((DEPTH, len(SAME_CORE)) + rc[n]),
                         *[_handled(n, d[n]).reshape((DEPTH,) + rc[n]) for d in (w, m, v)], f"adamw_{n}")
        big[n] = [_handled(n, t.reshape((DEPTH,) + layer_shard[n])) for t in outs]

    small_shapes = [VECTOR_SHAPES[n] for n in VECTORS] + [CONV_W_FULL]
    stacked = [jnp.stack([small_g[l][n] for l in range(DEPTH)]) for n in VECTORS + ("conv_w",)]
    sparts = all_gather([_pack(stacked, SMALL_ROWS, 128)], "gather_vector_grads")[0]
    col0 = me * (D_CONV // N_DEV)
    place = lambda t: lax.dynamic_update_slice(jnp.zeros(CONV_W_FULL, F32), t, (0, 0, col0))
    spacked = [_pack([d[n] for n in VECTORS] + [place(d["conv_w"])], SMALL_ROWS, 128)[None] for d in (w, m, v)]
    small = [_unpack(t[0], small_shapes) for t in adamw_sum(sparts[None], *spacked, "adamw_vectors")]

    def result(kind):
        out = {n: big[n][kind] for n in MATRICES}
        out.update({n: small[kind][j] for j, n in enumerate(VECTORS)})
        out["conv_w"] = lax.dynamic_slice(small[kind][len(VECTORS)], (0, 0, col0), conv_w.shape)
        return [out[n] for n in w]

    return (loss, dx[None], *result(0), *result(1), *result(2), *result(3))
```

```python
import functools

import jax
import jax.numpy as jnp
from jax import lax
from jax.experimental import pallas as pl
from jax.experimental.pallas import tpu as pltpu

F32 = jnp.float32
BF16 = jnp.bfloat16

N_DEV = 8
DEPTH = 4
D_MODEL = 1024
HEAD_DIM = 64
HEADS = 8
D_ATTN = 512
D_CONV = 256
D_POOL = 256
D_FF = 2816
D_IN = 5640
EPS = 1e-6
ATTN_SCALE = HEAD_DIM ** -0.5

N_REST = 4096
N_MAIN = 5632
N_FULL = 5760
DPROJ_TAIL = 2048
DPROJ_COLS = N_REST + DPROJ_TAIL
FF_BLK = 256
N_FF_BLKS = D_FF // FF_BLK
HALO = 16

ADAM_LR = 0.001
ADAM_B1 = 0.9
ADAM_B2 = 0.999
ADAM_EPS = 1e-08
ADAM_WD = 0.01
ADAM_STEP = 10

SMALL_ROWS = 128

VMEM_LIMIT = 48 * 2 ** 20


def _cparams(sem, vmem=None):
    return pltpu.CompilerParams(dimension_semantics=sem, vmem_limit_bytes=vmem or VMEM_LIMIT)


def _pick(n, cands):
    for c in cands:
        if n % c == 0:
            return c
    raise ValueError(f"no tile for {n}")


def _tile(n, cap):
    t = min(cap, n)
    assert n % t == 0, (n, cap)
    return t


def _sigmoid(v):
    return 1.0 / (1.0 + jnp.exp(-v))


def _rstd(v):
    return lax.rsqrt(jnp.mean(v * v, axis=-1, keepdims=True) + EPS)


def _dot(a, b):
    return jnp.dot(a, b, preferred_element_type=F32)


def _dot_tn(a, b):
    return lax.dot_general(a, b, (((0,), (0,)), ((), ())), preferred_element_type=F32)


def _dot_nt(a, b):
    return lax.dot_general(a, b, (((1,), (1,)), ((), ())), preferred_element_type=F32)


def norm_matmul(x, g, wt, n_cols, name, ex=None):
    s, d = x.shape
    tm, tn = _tile(s, 1024), _pick(n_cols, (2816, 1408, 512))

    def body(x_ref, g_ref, w_ref, o_ref, h_ref):
        @pl.when(pl.program_id(1) == 0)
        def _():
            xv = x_ref[...]
            h_ref[...] = (xv * _rstd(xv) * g_ref[...]).astype(BF16)

        o_ref[...] = _dot_nt(h_ref[...], w_ref[...]).astype(BF16)

    return _carried_call(
        body, ex, (s // tm, n_cols // tn),
        [pl.BlockSpec((tm, d), lambda i, j: (i, 0)), pl.BlockSpec((1, d), lambda i, j: (0, 0)),
         pl.BlockSpec((tn, d), lambda i, j: (j, 0))],
        [pl.BlockSpec((tm, tn), lambda i, j: (i, j)), pl.BlockSpec((tm, d), lambda i, j: (i, 0))],
        [jax.ShapeDtypeStruct((s, n_cols), BF16), jax.ShapeDtypeStruct((s, d), BF16)], [],
        ("arbitrary", "arbitrary"), name, (x, g, wt))


def tn_matmul(a, b, name, m_cols=None, ex=None):
    t = a.shape[0]
    m = m_cols or a.shape[1]
    n = b.shape[1]
    tk = _tile(t, 1024)
    tmm = _pick(m, (1408, 1152, 1024, 512, 256))
    tn = _pick(n, (1408, 1152, 1024, 512, 128))
    nk = t // tk

    def body(a_ref, b_ref, o_ref, acc_ref):
        @pl.when(pl.program_id(2) == 0)
        def _():
            acc_ref[...] = jnp.zeros_like(acc_ref)

        acc_ref[...] += _dot_tn(a_ref[...].astype(BF16), b_ref[...].astype(BF16))

        @pl.when(pl.program_id(2) == nk - 1)
        def _():
            o_ref[...] = acc_ref[...].astype(BF16)

    if ex is None:
        return pl.pallas_call(
            body, grid=(m // tmm, n // tn, nk),
            in_specs=[pl.BlockSpec((tk, tmm), lambda i, j, k: (k, i)), pl.BlockSpec((tk, tn), lambda i, j, k: (k, j))],
            out_specs=pl.BlockSpec((tmm, tn), lambda i, j, k: (i, j)),
            out_shape=jax.ShapeDtypeStruct((m, n), BF16), scratch_shapes=[pltpu.VMEM((tmm, tn), F32)],
            compiler_params=_cparams(("parallel", "parallel", "arbitrary")), name=name)(a, b)
    (out,), carried = _carried_call(
        body, ex, (m // tmm, n // tn, nk),
        [pl.BlockSpec((tk, tmm), lambda i, j, k: (k, i)), pl.BlockSpec((tk, tn), lambda i, j, k: (k, j))],
        [pl.BlockSpec((tmm, tn), lambda i, j, k: (i, j))], [jax.ShapeDtypeStruct((m, n), BF16)],
        [pltpu.VMEM((tmm, tn), F32)], ("arbitrary", "arbitrary", "arbitrary"), name, (a, b))
    return out, carried


def tn_matmuls(pairs, name):
    t = pairs[0][0].shape[0]
    tk = _tile(t, 1024)
    nk = t // tk
    n = len(pairs)

    def body(*refs):
        ins, outs, accs = refs[:2 * n], refs[2 * n:3 * n], refs[3 * n:]

        @pl.when(pl.program_id(0) == 0)
        def _():
            for acc in accs:
                acc[...] = jnp.zeros_like(acc)

        for i in range(n):
            accs[i][...] += _dot_tn(ins[2 * i][...], ins[2 * i + 1][...])

        @pl.when(pl.program_id(0) == nk - 1)
        def _():
            for out, acc in zip(outs, accs):
                out[...] = acc[...].astype(BF16)

    shapes = [(a.shape[1], b.shape[1]) for a, b in pairs]
    return pl.pallas_call(
        body, grid=(nk,),
        in_specs=[pl.BlockSpec((tk, t_.shape[1]), lambda k: (k, 0)) for pair in pairs for t_ in pair],
        out_specs=[pl.BlockSpec(shp, lambda k: (0, 0)) for shp in shapes],
        out_shape=[jax.ShapeDtypeStruct(shp, BF16) for shp in shapes],
        scratch_shapes=[pltpu.VMEM(shp, F32) for shp in shapes],
        compiler_params=_cparams(("arbitrary",)), name=name)(*[t_ for pair in pairs for t_ in pair])


def matmul_normbwd(a, wt, x, g, dres, name, k=None, ex=None):
    s = a.shape[0]
    k = k or a.shape[1]
    d = wt.shape[1]
    tm = _tile(s, 1024)
    tk = _pick(k, (1408, 1152, 512))
    nk = k // tk

    def body(a_ref, w_ref, x_ref, g_ref, r_ref, dx_ref, dg_ref, acc_ref):
        i, kk = pl.program_id(0), pl.program_id(1)

        @pl.when(kk == 0)
        def _():
            acc_ref[...] = jnp.zeros_like(acc_ref)

        @pl.when((i == 0) & (kk == 0))
        def _():
            dg_ref[...] = jnp.zeros_like(dg_ref)

        acc_ref[...] += _dot(a_ref[...], w_ref[...])

        @pl.when(kk == nk - 1)
        def _():
            xv = x_ref[...]
            r = _rstd(xv)
            y = xv * r
            dh = acc_ref[...]
            dy = dh * g_ref[...]
            dx_ref[...] = r_ref[...] + r * (dy - y * jnp.mean(dy * y, axis=-1, keepdims=True))
            dg_ref[...] += jnp.sum(dh * y, axis=0, keepdims=True)

    return _carried_call(
        body, ex, (s // tm, nk),
        [pl.BlockSpec((tm, tk), lambda i, kk: (i, kk)), pl.BlockSpec((tk, d), lambda i, kk: (kk, 0)),
         pl.BlockSpec((tm, d), lambda i, kk: (i, 0)), pl.BlockSpec((1, d), lambda i, kk: (0, 0)),
         pl.BlockSpec((tm, d), lambda i, kk: (i, 0))],
        [pl.BlockSpec((tm, d), lambda i, kk: (i, 0)), pl.BlockSpec((1, d), lambda i, kk: (0, 0))],
        [jax.ShapeDtypeStruct((s, d), F32), jax.ShapeDtypeStruct((1, d), F32)],
        [pltpu.VMEM((tm, d), F32)], ("arbitrary", "arbitrary"), name, (a, wt, x, g, dres), vmem=56 * 2 ** 20)


def swiglu_matmul(gu, w, x1, name):
    s = gu.shape[0]
    d = w.shape[1]
    tm = _tile(s, 512)

    def body(gu_ref, w_ref, x_ref, o_ref):
        acc = x_ref[...]
        for j in range(N_FF_BLKS):
            gt = gu_ref[:, j * FF_BLK:(j + 1) * FF_BLK].astype(F32)
            up = gu_ref[:, D_FF + j * FF_BLK:D_FF + (j + 1) * FF_BLK].astype(F32)
            act = (gt * _sigmoid(gt) * up).astype(BF16)
            acc += _dot(act, w_ref[j * FF_BLK:(j + 1) * FF_BLK, :])
        o_ref[...] = acc

    return pl.pallas_call(
        body, grid=(s // tm,),
        in_specs=[pl.BlockSpec((tm, 2 * D_FF), lambda i: (i, 0)), pl.BlockSpec((D_FF, d), lambda i: (0, 0)),
                  pl.BlockSpec((tm, d), lambda i: (i, 0))],
        out_specs=pl.BlockSpec((tm, d), lambda i: (i, 0)),
        out_shape=jax.ShapeDtypeStruct((s, d), F32),
        compiler_params=_cparams(("parallel",)), name=name)(gu, w, x1)


def swiglu_bwd(dx2, gu, w, name, ex=None):
    s, d = dx2.shape
    tm = _tile(s, 512)

    def body(dx_ref, gu_ref, w_ref, dgu_ref, act_ref):
        dx = dx_ref[...].astype(BF16)
        for j in range(N_FF_BLKS):
            g_cols = slice(j * FF_BLK, (j + 1) * FF_BLK)
            u_cols = slice(D_FF + j * FF_BLK, D_FF + (j + 1) * FF_BLK)
            dact = _dot_nt(dx, w_ref[j * FF_BLK:(j + 1) * FF_BLK, :])
            gt = gu_ref[:, g_cols].astype(F32)
            up = gu_ref[:, u_cols].astype(F32)
            sg = _sigmoid(gt)
            silu = gt * sg
            act_ref[:, j * FF_BLK:(j + 1) * FF_BLK] = (silu * up).astype(BF16)
            dgu_ref[:, g_cols] = (dact * up * (sg + silu * (1.0 - sg))).astype(BF16)
            dgu_ref[:, u_cols] = (dact * silu).astype(BF16)

    return _carried_call(
        body, ex, (s // tm,),
        [pl.BlockSpec((tm, d), lambda i: (i, 0)), pl.BlockSpec((tm, 2 * D_FF), lambda i: (i, 0)),
         pl.BlockSpec((D_FF, d), lambda i: (0, 0), pipeline_mode=pl.Buffered(1))],
        [pl.BlockSpec((tm, 2 * D_FF), lambda i: (i, 0)), pl.BlockSpec((tm, D_FF), lambda i: (i, 0))],
        [jax.ShapeDtypeStruct((s, 2 * D_FF), BF16), jax.ShapeDtypeStruct((s, D_FF), BF16)], [],
        ("arbitrary",), name, (dx2, gu, w), vmem=56 * 2 ** 20)


def loss_kernel(y, tgt, name):
    s, d = y.shape
    tm = _tile(s, 512)

    def body(y_ref, t_ref, l_ref, dy_ref):
        @pl.when(pl.program_id(0) == 0)
        def _():
            l_ref[...] = jnp.zeros_like(l_ref)

        err = y_ref[...] - t_ref[...]
        dy_ref[...] = err * (1.0 / d)
        l_ref[...] += jnp.sum(jnp.sum(err * err, axis=1, keepdims=True), axis=0, keepdims=True)

    return pl.pallas_call(
        body, grid=(s // tm,),
        in_specs=[pl.BlockSpec((tm, d), lambda i: (i, 0)), pl.BlockSpec((tm, d), lambda i: (i, 0))],
        out_specs=[pl.BlockSpec((8, 128), lambda i: (0, 0)), pl.BlockSpec((tm, d), lambda i: (i, 0))],
        out_shape=[jax.ShapeDtypeStruct((8, 128), F32), jax.ShapeDtypeStruct((s, d), F32)],
        compiler_params=_cparams(("arbitrary",)), name=name)(y, tgt)


def _split3(v):
    a1 = v.astype(BF16)
    r1 = v - a1.astype(F32)
    a2 = r1.astype(BF16)
    a3 = (r1 - a2.astype(F32)).astype(BF16)
    return a1, a2, a3


def forget_fwd(h, wt_in, b, name):
    s, d = h.shape
    tm = _tile(s, 512)

    def body(h_ref, w_ref, b_ref, z_ref, c_ref, carry_ref):
        @pl.when(pl.program_id(0) == 0)
        def _():
            carry_ref[...] = jnp.zeros_like(carry_ref)

        z = _dot_nt(h_ref[...], w_ref[...]) + b_ref[...]
        z_ref[...] = z
        logf = jnp.minimum(z, 0.0) - jnp.log(1.0 + jnp.exp(-jnp.abs(z)))
        row = lax.broadcasted_iota(jnp.int32, (tm, tm), 0)
        col = lax.broadcasted_iota(jnp.int32, (tm, tm), 1)
        tri = (row >= col).astype(BF16)
        a1, a2, a3 = _split3(logf)
        c = _dot(tri, a1) + _dot(tri, a2) + _dot(tri, a3) + carry_ref[...]
        c_ref[...] = c
        carry_ref[...] = c[tm - 1:tm, :]

    return pl.pallas_call(
        body, grid=(s // tm,),
        in_specs=[pl.BlockSpec((tm, d), lambda i: (i, 0)), pl.BlockSpec((128, d), lambda i: (N_MAIN // 128, 0)),
                  pl.BlockSpec((1, 128), lambda i: (0, 0))],
        out_specs=[pl.BlockSpec((tm, 128), lambda i: (i, 0)), pl.BlockSpec((tm, 128), lambda i: (i, 0))],
        out_shape=[jax.ShapeDtypeStruct((s, 128), F32), jax.ShapeDtypeStruct((s, 128), F32)],
        scratch_shapes=[pltpu.VMEM((1, 128), F32)],
        compiler_params=_cparams(("arbitrary",)), name=name)(h, wt_in, b)


def forget_bwd(dc, z, dproj, name):
    s = dc.shape[0]
    tm = _tile(s, 512)
    nt = s // tm

    def body(dc_ref, z_ref, dp_ref, dz_ref, db_ref, carry_ref):
        @pl.when(pl.program_id(0) == 0)
        def _():
            carry_ref[...] = jnp.zeros_like(carry_ref)
            db_ref[...] = jnp.zeros_like(db_ref)

        row = lax.broadcasted_iota(jnp.int32, (tm, tm), 0)
        col = lax.broadcasted_iota(jnp.int32, (tm, tm), 1)
        tri = (col >= row).astype(BF16)
        a1, a2, a3 = _split3(dc_ref[...])
        dlogf = _dot(tri, a1) + _dot(tri, a2) + _dot(tri, a3) + carry_ref[...]
        carry_ref[...] = dlogf[0:1, :]
        dz = dlogf * (1.0 - _sigmoid(z_ref[...]))
        dz_ref[...] = dz.astype(BF16)
        db_ref[...] += jnp.sum(dz, axis=0, keepdims=True)

    return pl.pallas_call(
        body, grid=(nt,),
        in_specs=[pl.BlockSpec((tm, 128), lambda i: (nt - 1 - i, 0)), pl.BlockSpec((tm, 128), lambda i: (nt - 1 - i, 0)),
                  pl.BlockSpec(memory_space=pl.ANY)],
        out_specs=[pl.BlockSpec((tm, 128), lambda i: (nt - 1 - i, N_MAIN // 128)), pl.BlockSpec((1, 128), lambda i: (0, 0))],
        out_shape=[jax.ShapeDtypeStruct(dproj.shape, BF16), jax.ShapeDtypeStruct((1, 128), F32)],
        scratch_shapes=[pltpu.VMEM((1, 128), F32)], input_output_aliases={2: 0},
        compiler_params=_cparams(("arbitrary",)), name=name)(dc, z, dproj)


HEAD_GROUP_FWD = 8
HEAD_GROUP_BWD = 8
LANE_C = 64
LANE_ONE = 67


def _lanes():
    lane = lax.broadcasted_iota(jnp.int32, (1, 128), 1)
    return lane, lane < HEAD_DIM


def _half_mean(t, lo):
    s_lo = jnp.sum(jnp.where(lo, t, 0.0), axis=-1, keepdims=True)
    s_hi = jnp.sum(jnp.where(lo, 0.0, t), axis=-1, keepdims=True)
    return jnp.where(lo, s_lo, s_hi) * (1.0 / HEAD_DIM)


def _lane_col(t, lane, idx):
    return jnp.sum(jnp.where(lane == idx, t, 0.0), axis=-1, keepdims=True)


def _swap_halves(t):
    return pltpu.roll(t, HEAD_DIM, 1)


def attn_prep(proj, c, gq2, gk2, name):
    s = proj.shape[0]
    tm = _tile(s, 512)
    first = N_REST // 128

    def body(q_ref, k_ref, v_ref, c_ref, gq_ref, gk_ref, qa_ref, ka_ref, va_ref, vt_ref):
        j = pl.program_id(1)
        lane, lo = _lanes()

        def normed(ref, g):
            t = ref[...].astype(F32)
            return t * lax.rsqrt(_half_mean(t * t, lo) + EPS) * g

        qn = normed(q_ref, gq_ref[...] * ATTN_SCALE)
        kn = normed(k_ref, gk_ref[...])
        vv = v_ref[...].astype(F32)
        cv = c_ref[...]
        one_q = jnp.where((lane >= LANE_ONE) & (lane < LANE_ONE + 3), 1.0, 0.0)
        one_k = jnp.where((lane >= LANE_C) & (lane < LANE_C + 3), 1.0, 0.0)
        one_v = jnp.where(lane == LANE_C, 1.0, 0.0)
        for e in range(2):
            pick = (lambda t: t) if e == 0 else _swap_halves
            pieces = [p.astype(F32) for p in _split3(_lane_col(cv, lane, 2 * j + e))]
            ext_q, ext_k = one_q, one_k
            for i, p in enumerate(pieces):
                ext_q = jnp.where(lane == LANE_C + i, p, ext_q)
                ext_k = jnp.where(lane == LANE_ONE + i, -p, ext_k)
            qa_ref[e] = jnp.where(lo, pick(qn), ext_q).astype(BF16)
            ka_ref[e] = jnp.where(lo, pick(kn), ext_k).astype(BF16)
            va = jnp.where(lo, pick(vv), one_v)
            va_ref[e] = va.astype(BF16)
            vt_ref[e] = va.T.astype(BF16)

    tile = lambda base: pl.BlockSpec((tm, 128), lambda i, j: (i, base + j))
    vec = pl.BlockSpec((1, 128), lambda i, j: (0, 0))
    out = pl.BlockSpec((2, tm, 128), lambda i, j: (j, i, 0))
    return pl.pallas_call(
        body, grid=(s // tm, HEADS // 2),
        in_specs=[tile(first), tile(first + 4), tile(first + 8), pl.BlockSpec((tm, 128), lambda i, j: (i, 0)), vec, vec],
        out_specs=[out, out, out, pl.BlockSpec((2, 128, tm), lambda i, j: (j, 0, i))],
        out_shape=[jax.ShapeDtypeStruct((HEADS, s, 128), BF16)] * 3 + [jax.ShapeDtypeStruct((HEADS, 128, s), BF16)],
        compiler_params=_cparams(("parallel", "arbitrary")), name=name)(proj, proj, proj, c, gq2, gk2)


def _carry(ex, n_in, n_out, n_scratch, grid):
    n_xin, n_xout = (len(ex.inputs), len(ex.out_shapes)) if ex else (0, 0)

    def split(refs):
        ins, xins = refs[:n_in], refs[n_in:n_in + n_xin]
        rest = refs[n_in + n_xin:]
        outs, xouts = rest[:n_out], rest[n_out:n_out + n_xout]
        rest = rest[n_out + n_xout:]
        return ins + outs + rest[:n_scratch], (xins, xouts, rest[n_scratch:])

    def first():
        return functools.reduce(lambda a, b: a & b, [pl.program_id(d) == 0 for d in range(len(grid))])

    def last():
        return functools.reduce(lambda a, b: a & b, [pl.program_id(d) == grid[d] - 1 for d in range(len(grid))])

    return split, first, last


def _carried_call(body, ex, grid, in_specs, out_specs, out_shape, scratch, sem, name, operands, vmem=None):
    any_spec = pl.BlockSpec(memory_space=pl.ANY)
    split, first, last = _carry(ex, len(in_specs), len(out_specs), len(scratch), grid)

    def carried(*refs):
        own, xrefs = split(refs)
        if ex:
            @pl.when(first())
            def _():
                ex.start(*xrefs)

        body(*own)
        if ex:
            @pl.when(last())
            def _():
                ex.drain(*xrefs)

    n_xin = len(ex.inputs) if ex else 0
    results = pl.pallas_call(
        carried, grid=grid, in_specs=list(in_specs) + [any_spec] * n_xin,
        out_specs=list(out_specs) + [any_spec] * (len(ex.out_shapes) if ex else 0),
        out_shape=list(out_shape) + (list(ex.out_shapes) if ex else []),
        input_output_aliases={len(in_specs) + i: len(out_specs) + o for i, o in ex.aliases.items()} if ex else {},
        scratch_shapes=list(scratch) + (ex.scratch if ex else []),
        compiler_params=_cparams(sem, vmem), name=name)(*operands, *(ex.inputs if ex else []))
    return results[:len(out_specs)], results[len(out_specs):]


def _tri_rows(t, n):
    qi = sum(jnp.where(t >= r * (r + 1) // 2, 1, 0) for r in range(1, n))
    return qi, t - qi * (qi + 1) // 2


def _tri_cols(t, n):
    ki = sum(jnp.where(t >= r * n - r * (r - 1) // 2, 1, 0) for r in range(1, n))
    return ki, ki + t - (ki * n - ki * (ki - 1) // 2)


def _causal_t(st_blk, tk, tq):
    key = lax.broadcasted_iota(jnp.int32, (tk, tq), 0)
    qry = lax.broadcasted_iota(jnp.int32, (tk, tq), 1)
    return jnp.where(qry >= key, st_blk, -jnp.inf)


def attn_forward(qa, ka, vt, name, ex=None):
    hh, s, _ = qa.shape
    tq = tk = _tile(s, 512)
    nq = s // tq
    grp = HEAD_GROUP_FWD

    def body(q_ref, k_ref, vt_ref, o_ref, lse_ref, m_ref, acc_ref):
        qi, ki = _tri_rows(pl.program_id(1), nq)

        @pl.when(ki == 0)
        def _():
            m_ref[...] = jnp.full_like(m_ref, -jnp.inf)
            acc_ref[...] = jnp.zeros_like(acc_ref)

        def step(masked):
            nxt = _dot_nt(k_ref[0], q_ref[0])
            for g in range(grp):
                st = nxt
                if g + 1 < grp:
                    nxt = _dot_nt(k_ref[g + 1], q_ref[g + 1])
                if masked:
                    st = _causal_t(st, tk, tq)
                m_old = m_ref[g]
                m_new = jnp.maximum(m_old, jnp.max(st, axis=0, keepdims=True))
                pt = jnp.exp(st - m_new).astype(BF16)
                acc_ref[g] = jnp.exp(m_old - m_new) * acc_ref[g] + _dot(vt_ref[g], pt)
                m_ref[g] = m_new

        @pl.when(ki < qi)
        def _():
            step(False)

        @pl.when(ki == qi)
        def _():
            step(True)
            for g in range(grp):
                acc = acc_ref[g]
                denom = acc[LANE_C:LANE_C + 1, :]
                o_ref[g] = (acc / denom).T.astype(BF16)
                lse_ref[g] = m_ref[g] + jnp.log(denom)

    qspec = pl.BlockSpec((grp, tq, 128), lambda h, t: (h, _tri_rows(t, nq)[0], 0))
    kspec = pl.BlockSpec((grp, tk, 128), lambda h, t: (h, _tri_rows(t, nq)[1], 0))
    vspec = pl.BlockSpec((grp, 128, tk), lambda h, t: (h, 0, _tri_rows(t, nq)[1]))
    lspec = pl.BlockSpec((grp, 1, tq), lambda h, t: (h, 0, _tri_rows(t, nq)[0]))
    return _carried_call(
        body, ex, (hh // grp, nq * (nq + 1) // 2), [qspec, kspec, vspec], [qspec, lspec],
        [jax.ShapeDtypeStruct((hh, s, 128), BF16), jax.ShapeDtypeStruct((hh, 1, s), F32)],
        [pltpu.VMEM((grp, 1, tq), F32), pltpu.VMEM((grp, 128, tq), F32)],
        ("arbitrary", "arbitrary"), name, (qa, ka, vt))


def attn_backward(qa, ka, va, oa, doa, lse, name, ex=None):
    hh, s, _ = qa.shape
    tq = tk = _tile(s, 512)
    nq = s // tq
    grp = HEAD_GROUP_BWD

    def body(q_ref, k_ref, v_ref, o_ref, do_ref, lse_ref, dq_ref, dk_ref, dv_ref, dka_ref, dva_ref):
        ki, qi = _tri_cols(pl.program_id(1), nq)

        @pl.when(pl.program_id(1) == 0)
        def _():
            dq_ref[...] = jnp.zeros_like(dq_ref)

        @pl.when(qi == ki)
        def _():
            dka_ref[...] = jnp.zeros_like(dka_ref)
            dva_ref[...] = jnp.zeros_like(dva_ref)

        def step(masked):
            rows = pl.ds(pl.multiple_of(qi * tq, tq), tq)
            products = lambda g: (_dot_nt(k_ref[g], q_ref[g]), _dot_nt(v_ref[g], do_ref[g]))
            nxt = products(0)
            for g in range(grp):
                st, dpt = nxt
                if g + 1 < grp:
                    nxt = products(g + 1)
                q, k, do = q_ref[g], k_ref[g], do_ref[g]
                if masked:
                    st = _causal_t(st, tk, tq)
                pt = jnp.exp(st - lse_ref[g])
                delta = jnp.sum((do.astype(F32) * o_ref[g].astype(F32)).T, axis=0, keepdims=True)
                dst = (pt * (dpt - delta)).astype(BF16)
                dva_ref[g] += _dot(pt.astype(BF16), do)
                dka_ref[g] += _dot(dst, q)
                dq_ref[g, rows, :] += _dot_tn(dst, k)

        @pl.when(qi > ki)
        def _():
            step(False)

        @pl.when(qi == ki)
        def _():
            step(True)

        @pl.when(qi == nq - 1)
        def _():
            dk_ref[...] = dka_ref[...]
            dv_ref[...] = dva_ref[...].astype(BF16)

    qspec = pl.BlockSpec((grp, tq, 128), lambda h, t: (h, _tri_cols(t, nq)[1], 0))
    lspec = pl.BlockSpec((grp, 1, tq), lambda h, t: (h, 0, _tri_cols(t, nq)[1]))
    kspec = pl.BlockSpec((grp, tk, 128), lambda h, t: (h, _tri_cols(t, nq)[0], 0))
    return _carried_call(
        body, ex, (hh // grp, nq * (nq + 1) // 2), [qspec, kspec, kspec, qspec, qspec, lspec],
        [pl.BlockSpec((grp, s, 128), lambda h, t: (h, 0, 0), pipeline_mode=pl.Buffered(1)), kspec, kspec],
        [jax.ShapeDtypeStruct((hh, s, 128), F32), jax.ShapeDtypeStruct((hh, s, 128), F32),
         jax.ShapeDtypeStruct((hh, s, 128), BF16)],
        [pltpu.VMEM((grp, tk, 128), F32), pltpu.VMEM((grp, tk, 128), F32)],
        ("arbitrary", "arbitrary"), name, (qa, ka, va, oa, doa, lse), vmem=58 * 2 ** 20)


def attn_post(dqa, dka, dva, proj, gq2, gk2, dproj, name):
    s = proj.shape[0]
    tm = _tile(s, 256)

    def body(dq_ref, dk_ref, dv_ref, q_ref, k_ref, gq_ref, gk_ref, dp_any, dp_ref, dc_ref, dgq_ref, dgk_ref):
        lane, lo = _lanes()

        @pl.when(pl.program_id(0) == 0)
        def _():
            dgq_ref[...] = jnp.zeros_like(dgq_ref)
            dgk_ref[...] = jnp.zeros_like(dgk_ref)

        def pair(ref, j):
            return jnp.where(lo, ref[2 * j].astype(F32), _swap_halves(ref[2 * j + 1].astype(F32)))

        def norm_bwd(raw, g, dhat, scale):
            r = lax.rsqrt(_half_mean(raw * raw, lo) + EPS)
            y = raw * r
            dy = dhat * (g * scale)
            return r * (dy - y * _half_mean(dy * y, lo)), jnp.sum(dhat * y, axis=0, keepdims=True) * scale

        dc = jnp.zeros((tm, 128), F32)
        for j in range(HEADS // 2):
            cols = slice(128 * j, 128 * (j + 1))
            dq, dgq = norm_bwd(q_ref[:, cols].astype(F32), gq_ref[...], pair(dq_ref, j), ATTN_SCALE)
            dk, dgk = norm_bwd(k_ref[:, cols].astype(F32), gk_ref[...], pair(dk_ref, j), 1.0)
            dgq_ref[...] += dgq
            dgk_ref[...] += dgk
            dp_ref[:, cols] = dq.astype(BF16)
            dp_ref[:, D_ATTN + 128 * j:D_ATTN + 128 * (j + 1)] = dk.astype(BF16)
            dp_ref[:, 2 * D_ATTN + 128 * j:2 * D_ATTN + 128 * (j + 1)] = pair(dv_ref, j).astype(BF16)
            for e in range(2):
                h = 2 * j + e
                both = jnp.where(lane == LANE_C, dq_ref[h], 0.0) - jnp.where(lane == LANE_ONE, dk_ref[h], 0.0)
                dc = jnp.where(lane == h, jnp.sum(both, axis=-1, keepdims=True), dc)
        dp_ref[:, 3 * D_ATTN:] = jnp.zeros((tm, DPROJ_TAIL - 3 * D_ATTN), BF16)
        dc_ref[...] = dc

    heads = lambda: pl.BlockSpec((HEADS, tm, 128), lambda i: (0, i, 0))
    vec = pl.BlockSpec((1, 128), lambda i: (0, 0))
    first = N_REST // D_ATTN
    return pl.pallas_call(
        body, grid=(s // tm,),
        in_specs=[heads(), heads(), heads(), pl.BlockSpec((tm, D_ATTN), lambda i: (i, first)),
                  pl.BlockSpec((tm, D_ATTN), lambda i: (i, first + 1)), vec, vec, pl.BlockSpec(memory_space=pl.ANY)],
        out_specs=[pl.BlockSpec((tm, DPROJ_TAIL), lambda i: (i, N_REST // DPROJ_TAIL)),
                   pl.BlockSpec((tm, 128), lambda i: (i, 0)), vec, vec],
        out_shape=[jax.ShapeDtypeStruct(dproj.shape, BF16), jax.ShapeDtypeStruct((s, 128), F32),
                   jax.ShapeDtypeStruct((1, 128), F32), jax.ShapeDtypeStruct((1, 128), F32)],
        input_output_aliases={7: 0},
        compiler_params=_cparams(("arbitrary",)), name=name)(dqa, dka, dva, proj, proj, gq2, gk2, dproj)


def _pool_groups(tm):
    gid = lax.broadcasted_iota(jnp.int32, (1, D_POOL), 1) // (D_POOL // 4)
    win = jnp.where(gid == 0, 2.0, jnp.where(gid == 1, 4.0, jnp.where(gid == 2, 8.0, 16.0)))
    return gid, win


def _by_group(gid, v2, v4, v8, v16):
    return jnp.where(gid == 0, v2, jnp.where(gid == 1, v4, jnp.where(gid == 2, v8, v16)))


def _branches(rest_ref, halo_ref, a_ref, wa_ref, wc_ref, wp_ref, sc_ref, cw_ref, ti, tm):
    f = lambda v: v.astype(F32)
    cx, cb, cc, px = f(rest_ref[:, 0:256]), f(rest_ref[:, 256:512]), f(rest_ref[:, 512:768]), f(rest_ref[:, 768:1024])
    live = jnp.where(ti > 0, 1.0, 0.0)
    hz = f(halo_ref[:, 0:256]) * f(halo_ref[:, 512:768]) * live
    hp = f(halo_ref[:, 768:1024]) * live
    z = cc * cx
    zf = jnp.concatenate([hz, z], axis=0)
    z1 = pltpu.roll(zf, 1, 0)[HALO:]
    z2 = pltpu.roll(zf, 2, 0)[HALO:]
    cw = cw_ref[...]
    conv = cw[2:3] * z + cw[1:2] * z1 + cw[0:1] * z2
    uc = cb * conv
    pf = jnp.concatenate([hp, px], axis=0)
    s2 = pf + pltpu.roll(pf, 1, 0)
    s4 = s2 + pltpu.roll(s2, 2, 0)
    s8 = s4 + pltpu.roll(s4, 4, 0)
    s16 = s8 + pltpu.roll(s8, 8, 0)
    gid, win = _pool_groups(tm)
    t = (ti * tm + lax.broadcasted_iota(jnp.int32, (tm, 1), 0)).astype(F32)
    inv = 1.0 / jnp.minimum(t + 1.0, win)
    dpool = _by_group(gid, s2[HALO:], s4[HALO:], s8[HALO:], s16[HALO:]) * inv - px
    _, lo = _lanes()
    a_tok = [jnp.where(lo, f(a_ref[2 * j]), _swap_halves(f(a_ref[2 * j + 1]))).astype(BF16) for j in range(HEADS // 2)]
    y_attn = _dot(a_tok[0], wa_ref[0:128, :])
    for j in range(1, HEADS // 2):
        y_attn += _dot(a_tok[j], wa_ref[128 * j:128 * (j + 1), :])
    y_conv = _dot(uc.astype(BF16), wc_ref[...])
    y_pool_raw = _dot(dpool.astype(BF16), wp_ref[...])
    sg = [_sigmoid(f(rest_ref[:, 1024 + i * D_MODEL:1024 + (i + 1) * D_MODEL])) for i in range(3)]
    return dict(cx=cx, cb=cb, cc=cc, z=z, z1=z1, z2=z2, conv=conv, uc=uc, dpool=dpool, inv=inv, gid=gid, a_tok=a_tok,
                y_attn=y_attn, y_conv=y_conv, y_pool_raw=y_pool_raw, sg=sg, cw=cw)


def _mix_specs(tm, ti_of):
    blocks_per_tile = tm // HALO
    return [
        pl.BlockSpec((tm, N_REST), lambda i: (ti_of(i), 0)),
        pl.BlockSpec((HALO, 1024), lambda i: (jnp.maximum(ti_of(i) * blocks_per_tile - 1, 0), 0)),
        pl.BlockSpec((HEADS, tm, 128), lambda i: (0, ti_of(i), 0)),
        pl.BlockSpec((D_ATTN, D_MODEL), lambda i: (0, 0)),
        pl.BlockSpec((D_CONV, D_MODEL), lambda i: (0, 0)),
        pl.BlockSpec((D_POOL, D_MODEL), lambda i: (0, 0)),
        pl.BlockSpec((1, D_MODEL), lambda i: (0, 0)),
        pl.BlockSpec((8, D_CONV), lambda i: (0, 0)),
    ]


def mix_fwd(proj, a, x, wa, wc, wp, scale, cw, wo, name, ex=None):
    s = x.shape[0]
    tm = _tile(s, 256)

    def body(rest_ref, halo_ref, a_ref, wa_ref, wc_ref, wp_ref, sc_ref, cw_ref, wo_ref, x_ref, o_ref):
        b = _branches(rest_ref, halo_ref, a_ref, wa_ref, wc_ref, wp_ref, sc_ref, cw_ref, pl.program_id(0), tm)
        merged = b["sg"][0] * b["y_attn"] + b["sg"][1] * b["y_conv"] + b["sg"][2] * (b["y_pool_raw"] * sc_ref[...])
        o_ref[...] = x_ref[...] + _dot(merged.astype(BF16), wo_ref[...])

    (x1,), carried = _carried_call(
        body, ex, (s // tm,),
        _mix_specs(tm, lambda i: i) + [pl.BlockSpec((D_MODEL, D_MODEL), lambda i: (0, 0)),
                                       pl.BlockSpec((tm, D_MODEL), lambda i: (i, 0))],
        [pl.BlockSpec((tm, D_MODEL), lambda i: (i, 0))], [jax.ShapeDtypeStruct((s, D_MODEL), F32)], [],
        ("arbitrary",), name, (proj, proj, a, wa, wc, wp, scale, cw, wo, x))
    return x1, carried


def mix_bwd(proj, a, dx1, wa, wc, wp, scale, cw, wo, name):
    s = dx1.shape[0]
    tm = _tile(s, 256)
    nt = s // tm
    ti_of = lambda i: nt - 1 - i
    n = tm + HALO

    def body(rest_ref, halo_ref, a_ref, wa_ref, wc_ref, wp_ref, sc_ref, cw_ref, wo_ref,
             dx_ref, dp_ref, da_ref, at_ref, mg_ref, dya_ref, dyc_ref, dyp_ref, uc_ref, dd_ref, dsc_ref, dcw_ref,
             cdc_ref, cde_ref):
        i = pl.program_id(0)
        ti = ti_of(i)

        @pl.when(i == 0)
        def _():
            cdc_ref[...] = jnp.zeros_like(cdc_ref)
            cde_ref[...] = jnp.zeros_like(cde_ref)
            dsc_ref[...] = jnp.zeros_like(dsc_ref)
            dcw_ref[...] = jnp.zeros_like(dcw_ref)

        b = _branches(rest_ref, halo_ref, a_ref, wa_ref, wc_ref, wp_ref, sc_ref, cw_ref, ti, tm)
        sg, sc = b["sg"], sc_ref[...]
        y_pool = b["y_pool_raw"] * sc
        merged = sg[0] * b["y_attn"] + sg[1] * b["y_conv"] + sg[2] * y_pool
        mg_ref[...] = merged.astype(BF16)
        dm = _dot_nt(dx_ref[...].astype(BF16), wo_ref[...])
        dys = [dm * sg[j] for j in range(3)]
        for j, y in enumerate((b["y_attn"], b["y_conv"], y_pool)):
            dp_ref[:, 1024 + j * D_MODEL:1024 + (j + 1) * D_MODEL] = (dys[j] * y * (1.0 - sg[j])).astype(BF16)
        dya = dys[0].astype(BF16)
        dya_ref[...] = dya
        _, lo = _lanes()
        for j in range(HEADS // 2):
            at_ref[:, 128 * j:128 * (j + 1)] = b["a_tok"][j]
            da = _dot_nt(dya, wa_ref[128 * j:128 * (j + 1), :])
            da_ref[2 * j] = jnp.where(lo, da, 0.0).astype(BF16)
            da_ref[2 * j + 1] = jnp.where(lo, _swap_halves(da), 0.0).astype(BF16)
        dyc = dys[1].astype(BF16)
        dyc_ref[...] = dyc
        duc = _dot_nt(dyc, wc_ref[...])
        dyp = dys[2]
        dsc_ref[...] += jnp.sum(dyp * b["y_pool_raw"], axis=0, keepdims=True)
        dypr = (dyp * sc).astype(BF16)
        dyp_ref[...] = dypr
        ddp = _dot_nt(dypr, wp_ref[...])
        uc_ref[...] = b["uc"].astype(BF16)
        dd_ref[...] = b["dpool"].astype(BF16)

        dconv = duc * b["cb"]
        dp_ref[:, 256:512] = (duc * b["conv"]).astype(BF16)
        dcf = jnp.concatenate([dconv, cdc_ref[...]], axis=0)
        cw = b["cw"]
        dz = cw[2:3] * dconv + cw[1:2] * pltpu.roll(dcf, n - 1, 0)[:tm] + cw[0:1] * pltpu.roll(dcf, n - 2, 0)[:tm]
        dp_ref[:, 0:256] = (dz * b["cc"]).astype(BF16)
        dp_ref[:, 512:768] = (dz * b["cx"]).astype(BF16)
        dcw_ref[0:1, :] += jnp.sum(dconv * b["z2"], axis=0, keepdims=True)
        dcw_ref[1:2, :] += jnp.sum(dconv * b["z1"], axis=0, keepdims=True)
        dcw_ref[2:3, :] += jnp.sum(dconv * b["z"], axis=0, keepdims=True)
        cdc_ref[...] = dconv[:HALO]

        e = ddp * b["inv"]
        ef = jnp.concatenate([e, cde_ref[...]], axis=0)
        r2 = ef + pltpu.roll(ef, n - 1, 0)
        r4 = r2 + pltpu.roll(r2, n - 2, 0)
        r8 = r4 + pltpu.roll(r4, n - 4, 0)
        r16 = r8 + pltpu.roll(r8, n - 8, 0)
        dp_ref[:, 768:1024] = (_by_group(b["gid"], r2[:tm], r4[:tm], r8[:tm], r16[:tm]) - ddp).astype(BF16)
        cde_ref[...] = e[:HALO]

    tile = lambda w: pl.BlockSpec((tm, w), lambda i: (ti_of(i), 0))
    whole = lambda r, c: pl.BlockSpec((r, c), lambda i: (0, 0))
    bf = lambda w: jax.ShapeDtypeStruct((s, w), BF16)
    return pl.pallas_call(
        body, grid=(nt,),
        in_specs=_mix_specs(tm, ti_of) + [whole(D_MODEL, D_MODEL), tile(D_MODEL)],
        out_specs=[tile(N_REST), pl.BlockSpec((HEADS, tm, 128), lambda i: (0, ti_of(i), 0)), tile(D_ATTN),
                   tile(D_MODEL), tile(D_MODEL), tile(D_MODEL), tile(D_MODEL),
                   tile(D_CONV), tile(D_POOL), whole(1, D_MODEL), whole(8, D_CONV)],
        out_shape=[bf(DPROJ_COLS), jax.ShapeDtypeStruct((HEADS, s, 128), BF16), bf(D_ATTN),
                   bf(D_MODEL), bf(D_MODEL), bf(D_MODEL), bf(D_MODEL), bf(D_CONV), bf(D_POOL),
                   jax.ShapeDtypeStruct((1, D_MODEL), F32), jax.ShapeDtypeStruct((8, D_CONV), F32)],
        scratch_shapes=[pltpu.VMEM((HALO, D_CONV), F32), pltpu.VMEM((HALO, D_POOL), F32)],
        compiler_params=_cparams(("arbitrary",)), name=name)(proj, proj, a, wa, wc, wp, scale, cw, wo, dx1)


def _adamw_math(w, g, m, v):
    m = ADAM_B1 * m + (1.0 - ADAM_B1) * g
    v = ADAM_B2 * v + (1.0 - ADAM_B2) * (g * g)
    m_hat = m / (1.0 - ADAM_B1 ** ADAM_STEP)
    v_hat = v / (1.0 - ADAM_B2 ** ADAM_STEP)
    delta = -ADAM_LR * (m_hat / (jnp.sqrt(v_hat) + ADAM_EPS) + ADAM_WD * w)
    return delta, m, v


ADAMW_PARTS_BLOCK_BYTES = 4 * 2 ** 20


def _row_tile(rows, cols, copies, itemsize):
    row_bytes = copies * (-(-cols // 128) * 128) * itemsize
    fits = [t for t in range(16, rows + 1, 16) if rows % t == 0 and t * row_bytes <= ADAMW_PARTS_BLOCK_BYTES]
    return max(fits) if fits else rows


def pair_sum(blocks, stage, me, name):
    n_slots, rows, cols = stage.shape
    tr = _row_tile(rows, cols, 1, 4)

    def body(me_ref, a_ref, b_ref, o_ref):
        o_ref[...] = (a_ref[...].astype(F32) + b_ref[...].astype(F32)).astype(BF16)

    slot = pl.BlockSpec((None, tr, cols), lambda i, r, me_ref: (i, r, 0))
    return pl.pallas_call(
        body, out_shape=jax.ShapeDtypeStruct(stage.shape, BF16),
        grid_spec=pltpu.PrefetchScalarGridSpec(
            num_scalar_prefetch=1, grid=(n_slots, rows // tr),
            in_specs=[pl.BlockSpec((None, tr, cols), lambda i, r, me_ref: (me_ref[0] ^ (2 * i), r, 0)), slot],
            out_specs=slot),
        compiler_params=_cparams(("parallel", "parallel")), name=name)(me.reshape(1), blocks, stage)


def adamw_sum(parts, w, m, v, name):
    layers, rows, cols = w.shape
    n_parts = parts.shape[1]
    if rows % 16 == 0:
        tr, tc = _row_tile(rows, cols, n_parts, parts.dtype.itemsize), cols
    else:
        tr, tc = rows, _pick(cols, (256, 128))

    def body(p_ref, w_ref, m_ref, v_ref, g_ref, d_ref, nm_ref, nv_ref):
        g = p_ref[0].astype(F32)
        for i in range(1, n_parts):
            g = g + p_ref[i].astype(F32)
        g_ref[...] = g
        d_ref[...], nm_ref[...], nv_ref[...] = _adamw_math(w_ref[...], g, m_ref[...], v_ref[...])

    spec = pl.BlockSpec((None, tr, tc), lambda l, i, j: (l, i, j))
    return pl.pallas_call(
        body, grid=(layers, rows // tr, cols // tc),
        in_specs=[pl.BlockSpec((None, n_parts, tr, tc), lambda l, i, j: (l, 0, i, j)), spec, spec, spec],
        out_specs=[spec] * 4, out_shape=[jax.ShapeDtypeStruct((layers, rows, cols), F32)] * 4,
        compiler_params=_cparams(("parallel", "parallel", "parallel")), name=name)(parts, w, m, v)


def _me():
    return lax.axis_index("x"), lax.axis_index("y"), lax.axis_index("c")


N_PEERS = N_DEV - 1


def all_gather(shards, name):
    n = len(shards)
    any_spec = pl.BlockSpec(memory_space=pl.ANY)

    def body(*refs):
        x_refs, out_refs = refs[:n], refs[n:2 * n]
        send_sems, recv_sems, local_sems = refs[2 * n:]
        x, y, c = _me()
        me, sibling = (x, y, c), (x, y, 1 - c)
        chips = [(1 - x, y), (x, 1 - y), (1 - x, 1 - y)]

        def copy(t, k, block, to, from_input=False):
            slot = out_refs[t].at[4 * block[0] + 2 * block[1] + block[2]]
            return pltpu.make_async_remote_copy(
                src_ref=x_refs[t] if from_input else slot, dst_ref=slot, send_sem=send_sems.at[N_PEERS * t + k],
                recv_sem=recv_sems.at[N_PEERS * t + k], device_id=to, device_id_type=pl.DeviceIdType.MESH)

        mine = [pltpu.make_async_copy(x_refs[t], out_refs[t].at[4 * x + 2 * y + c], local_sems.at[t]) for t in range(n)]
        started = []
        for t in range(n):
            mine[t].start()
            started.append(copy(t, 0, me, sibling, from_input=True))
            started += [copy(t, 1 + j, me, (*chip, c), from_input=True) for j, chip in enumerate(chips)]
        for cp in started:
            cp.start()
        for j, chip in enumerate(chips):
            for t in range(n):
                copy(t, 1 + j, (*chip, c), me).wait_recv()
                fwd = copy(t, 4 + j, (*chip, c), sibling)
                fwd.start()
                started.append(fwd)
        for t in range(n):
            copy(t, 0, sibling, me).wait_recv()
            for j, chip in enumerate(chips):
                copy(t, 4 + j, (*chip, 1 - c), me).wait_recv()
        for cp in started:
            cp.wait_send()
        for cp in mine:
            cp.wait()

    return pl.pallas_call(
        body, out_shape=[jax.ShapeDtypeStruct((N_DEV,) + s.shape, s.dtype) for s in shards],
        in_specs=[any_spec] * n, out_specs=[any_spec] * n,
        scratch_shapes=[pltpu.SemaphoreType.DMA((N_PEERS * n,)), pltpu.SemaphoreType.DMA((N_PEERS * n,)),
                        pltpu.SemaphoreType.DMA((n,))],
        name=name)(*shards)


SIBLING = 1
OTHER_CHIPS = (2, 4, 6)
SAME_CORE = (0,) + OTHER_CHIPS


class Exchange:
    def __init__(self, inputs, out_shapes, aliases, copies, local=()):
        self.inputs, self.out_shapes, self.aliases = list(inputs), list(out_shapes), aliases
        self._copies, self._local = list(copies), list(local)
        self.scratch = [pltpu.SemaphoreType.DMA((len(self._copies),)), pltpu.SemaphoreType.DMA((len(self._copies),)),
                        pltpu.SemaphoreType.DMA((max(len(self._local), 1),))]

    def _build(self, ins, outs, sems):
        send_sems, recv_sems, local_sems = sems
        x, y, c = _me()
        me = 4 * x + 2 * y + c
        local = [functools.partial(pltpu.make_async_copy, src(ins, outs, me), dst(outs, me), local_sems.at[i])
                 for i, (src, dst) in enumerate(self._local)]
        sends, recvs = [], []
        for i, (mask, src, dst) in enumerate(self._copies):
            px, py, pc = x ^ ((mask >> 2) & 1), y ^ ((mask >> 1) & 1), c ^ (mask & 1)
            pair = dict(send_sem=send_sems.at[i], recv_sem=recv_sems.at[i], device_id_type=pl.DeviceIdType.MESH)
            sends.append(functools.partial(
                pltpu.make_async_remote_copy, src_ref=src(ins, outs, me), dst_ref=dst(outs, me), device_id=(px, py, pc), **pair))
            recvs.append(functools.partial(
                pltpu.make_async_remote_copy, src_ref=src(ins, outs, me), dst_ref=dst(outs, me ^ mask), device_id=(x, y, c), **pair))
        return local, sends, recvs

    def start(self, ins, outs, sems):
        local, sends, _ = self._build(ins, outs, sems)
        for make in local + sends:
            make().start()

    def drain(self, ins, outs, sems):
        local, sends, recvs = self._build(ins, outs, sems)
        for make in recvs:
            make().wait_recv()
        for make in sends:
            make().wait_send()
        for make in local:
            make().wait()


def _bind(fn, *args):
    return functools.partial(fn, *args)


def join_exchanges(a, b):
    if a is None or b is None:
        return a or b
    na_in, na_out = len(a.inputs), len(a.out_shapes)

    def src_a(fn):
        return lambda ins, outs, me: fn(ins[:na_in], outs[:na_out], me)

    def dst_a(fn):
        return lambda outs, who: fn(outs[:na_out], who)

    def src_b(fn):
        return lambda ins, outs, me: fn(ins[na_in:], outs[na_out:], me)

    def dst_b(fn):
        return lambda outs, who: fn(outs[na_out:], who)

    copies = [(m, src_a(s), dst_a(d)) for m, s, d in a._copies] + [(m, src_b(s), dst_b(d)) for m, s, d in b._copies]
    local = [(src_a(s), dst_a(d)) for s, d in a._local] + [(src_b(s), dst_b(d)) for s, d in b._local]
    aliases = dict(a.aliases)
    aliases.update({na_in + i: na_out + o for i, o in b.aliases.items()})
    return Exchange(a.inputs + b.inputs, a.out_shapes + b.out_shapes, aliases, copies, local)


def gather_over_ici(shards):
    copies = [(mask, _bind(lambda t, ins, outs, me: ins[t], t), _bind(lambda t, outs, sender: outs[t].at[sender], t))
              for t in range(len(shards)) for mask in OTHER_CHIPS]
    local = [(_bind(lambda t, ins, outs, me: ins[t], t), _bind(lambda t, outs, me: outs[t].at[me], t))
             for t in range(len(shards))]
    return Exchange(shards, [jax.ShapeDtypeStruct((N_DEV,) + s.shape, s.dtype) for s in shards], {}, copies, local)


def gather_over_d2d(gathered):
    copies = [(SIBLING, _bind(lambda t, m, ins, outs, me: outs[t].at[me ^ m], t, m),
               _bind(lambda t, m, outs, sender: outs[t].at[sender ^ m], t, m))
              for t in range(len(gathered)) for m in SAME_CORE]
    return Exchange(gathered, [jax.ShapeDtypeStruct(g.shape, g.dtype) for g in gathered],
                    {t: t for t in range(len(gathered))}, copies)


def scatter_over_d2d(blocks):
    copies = [(SIBLING, _bind(lambda t, m, ins, outs, me: ins[t].at[me ^ SIBLING ^ m], t, m),
               _bind(lambda t, i, outs, sender: outs[t].at[i], t, i))
              for t in range(len(blocks)) for i, m in enumerate(SAME_CORE)]
    return Exchange(blocks, [jax.ShapeDtypeStruct((len(SAME_CORE),) + b.shape[1:], b.dtype) for b in blocks], {}, copies)


def scatter_over_ici(pair_sums, bufs, layer):
    n = len(pair_sums)
    copies = [(m, _bind(lambda t, i, ins, outs, me: ins[t].at[i], t, i),
               _bind(lambda t, i, outs, sender: outs[t].at[layer, i], t, i))
              for t in range(n) for i, m in enumerate(SAME_CORE) if m]
    local = [(_bind(lambda t, ins, outs, me: ins[t].at[0], t), _bind(lambda t, outs, me: outs[t].at[layer, 0], t))
             for t in range(n)]
    return Exchange(list(pair_sums) + list(bufs), [jax.ShapeDtypeStruct(b.shape, b.dtype) for b in bufs],
                    {n + t: t for t in range(n)}, copies, local)


def run_exchange(ex, name):
    any_spec = pl.BlockSpec(memory_space=pl.ANY)
    n_in, n_out = len(ex.inputs), len(ex.out_shapes)

    def body(*refs):
        ins, outs, sems = refs[:n_in], refs[n_in:n_in + n_out], refs[n_in + n_out:]
        ex.start(ins, outs, sems)
        ex.drain(ins, outs, sems)

    return pl.pallas_call(
        body, out_shape=ex.out_shapes, in_specs=[any_spec] * n_in, out_specs=[any_spec] * n_out,
        input_output_aliases=ex.aliases, scratch_shapes=ex.scratch, name=name)(*ex.inputs)


MATRICES = ("w_in", "w_attn_out", "w_conv_out", "pool_w", "w_o", "w_ffn_in", "w_ffn_out")
TRANSPOSED = ("w_in", "w_ffn_in")
EVERY = tuple(range(len(MATRICES)))
IN_PROJ_PART, ATTN_PART, MIX_PART = (0,), (1, 2, 3, 4, 5), (6,)
LATE = (0,)
EARLY = EVERY[1:]
EARLY_FIRST, EARLY_SECOND = (4, 6), (1, 2, 3, 5)
SHARD_INFO = {
    "w_in": ((DEPTH, D_IN // N_DEV, D_MODEL), 1),
    "w_attn_out": ((DEPTH, D_ATTN, D_MODEL // N_DEV), 2),
    "w_conv_out": ((DEPTH, D_CONV, D_MODEL // N_DEV), 2),
    "pool_w": ((DEPTH, 4, 64, 256 // N_DEV), 3),
    "w_o": ((DEPTH, D_MODEL // N_DEV, D_MODEL), 1),
    "w_ffn_in": ((DEPTH, 2 * D_FF // N_DEV, D_MODEL), 1),
    "w_ffn_out": ((DEPTH, D_FF // N_DEV, D_MODEL), 1),
}


def _handled(name, t):
    return jnp.transpose(t, (0, 2, 1)) if name in TRANSPOSED else t
VECTORS = ("norm_mix_g", "forget_b", "q_norm_g", "k_norm_g", "pool_scale", "norm_ffn_g")
VECTOR_SHAPES = {"norm_mix_g": (DEPTH, D_MODEL), "forget_b": (DEPTH, HEADS), "q_norm_g": (DEPTH, HEAD_DIM),
                 "k_norm_g": (DEPTH, HEAD_DIM), "pool_scale": (DEPTH, D_MODEL), "norm_ffn_g": (DEPTH, D_MODEL)}
CONV_W_FULL = (DEPTH, 3, D_CONV)


def _size(shape):
    n = 1
    for v in shape:
        n *= v
    return n


def _pack(arrays, rows, cols):
    flat = jnp.concatenate([a.reshape(-1) for a in arrays])
    return jnp.pad(flat, (0, rows * cols - flat.shape[0])).reshape(rows, cols)


def _unpack(packed, shapes):
    flat, out, off = packed.reshape(-1), [], 0
    for shp in shapes:
        out.append(flat[off:off + _size(shp)].reshape(shp))
        off += _size(shp)
    return out


def _join_shards(stacked, axis):
    moved = jnp.moveaxis(stacked, 0, axis)
    shp = list(moved.shape)
    shp[axis:axis + 2] = [shp[axis] * shp[axis + 1]]
    return moved.reshape(shp)


def _cut_shards(full, axis):
    shp = list(full.shape)
    shp[axis:axis + 1] = [N_DEV, shp[axis] // N_DEV]
    return jnp.moveaxis(full.reshape(shp), axis, 0)


N_MOVED = 1544
SHARD_ROWS = D_IN // N_DEV


def _regroup_w_in(shards):
    wt = shards.reshape(D_IN, shards.shape[2])
    pad = jnp.zeros((N_FULL - D_IN, wt.shape[1]), wt.dtype)
    return jnp.concatenate([wt[N_MOVED:], wt[:N_MOVED], pad], axis=0)


def _ungroup_w_in(wpt):
    def kernel_rows(a, b):
        if b <= N_MOVED:
            return [wpt[a + D_IN - N_MOVED:b + D_IN - N_MOVED]]
        if a >= N_MOVED:
            return [wpt[a - N_MOVED:b - N_MOVED]]
        return kernel_rows(a, N_MOVED) + kernel_rows(N_MOVED, b)

    return jnp.stack([jnp.concatenate(kernel_rows(s * SHARD_ROWS, (s + 1) * SHARD_ROWS), axis=0) for s in range(N_DEV)])


def _pool_block_diag(w):
    out = jnp.zeros((D_POOL, D_MODEL), w.dtype)
    for g in range(4):
        out = lax.dynamic_update_slice(out, w[g], (g * 64, g * 256))
    return out


def _pool_from_block_diag(wbd):
    return jnp.stack([wbd[g * 64:(g + 1) * 64, g * 256:(g + 1) * 256] for g in range(4)])


def _layer_weights(mats, vec, conv_w, l):
    wp = _pool_block_diag(mats["pool_w"])
    row = lambda v: v.reshape(1, -1)
    fb = jnp.zeros((1, 128), F32).at[0, :HEADS].set(vec["forget_b"][l])
    cw = jnp.zeros((8, D_CONV), F32).at[:3].set(conv_w[l])
    twice = lambda v: jnp.tile(v.reshape(1, -1), (1, 2))
    return dict(
        wt_in=_regroup_w_in(mats["w_in"]), wt_ffn_in=mats["w_ffn_in"], w_ffn_out=mats["w_ffn_out"],
        wa=mats["w_attn_out"], wc=mats["w_conv_out"], wp=wp, wo=mats["w_o"],
        g_mix=row(vec["norm_mix_g"][l]), g_ffn=row(vec["norm_ffn_g"][l]), gq2=twice(vec["q_norm_g"][l]),
        gk2=twice(vec["k_norm_g"][l]), scale=row(vec["pool_scale"][l]), fb=fb, cw=cw)


def _layer_fwd(x, w, l, comm):
    (proj, h), half_a = norm_matmul(x, w["g_mix"], w["wt_in"], N_MAIN, f"in_proj_{l}", comm.gather_ici(l + 1, IN_PROJ_PART))
    z, c = forget_fwd(h, w["wt_in"], w["fb"], f"forget_fwd_{l}")
    qa, ka, va, vt = attn_prep(proj, c, w["gq2"], w["gk2"], f"attn_prep_{l}")
    (oa, lse), half_b = attn_forward(qa, ka, vt, f"attn_fwd_{l}", comm.gather_ici(l + 1, ATTN_PART))
    x1, half_c = mix_fwd(proj, oa, x, w["wa"], w["wc"], w["wp"], w["scale"], w["cw"], w["wo"], f"mix_fwd_{l}",
                         comm.gather_ici(l + 1, MIX_PART))
    half = list(half_a) + list(half_b) + list(half_c)
    (gu, h2), gathered = norm_matmul(x1, w["g_ffn"], w["wt_ffn_in"], 2 * D_FF, f"ffn_in_{l}", comm.gather_d2d(l + 1, half))
    x2 = swiglu_matmul(gu, w["w_ffn_out"], x1, f"ffn_out_{l}")
    saved = dict(x=x, proj=proj, h=h, z=z, qa=qa, ka=ka, va=va, oa=oa, lse=lse, x1=x1, gu=gu, h2=h2)
    return x2, saved, gathered


def _layer_bwd(dx2, sv, w, l, comm):
    g = {}
    (dgu, act), stage = swiglu_bwd(dx2, sv["gu"], w["w_ffn_out"], f"ffn_out_bwd_{l}", comm.scatter_d2d(l + 1))
    sums = comm.pair_sums(l + 1, stage)
    g["w_ffn_out"] = tn_matmul(act, dx2, f"dw_ffn_out_{l}")
    g["w_ffn_in"] = tn_matmul(dgu, sv["h2"], f"dw_ffn_in_{l}")
    (dx1, dg), _ = matmul_normbwd(dgu, w["wt_ffn_in"], sv["x1"], w["g_ffn"], dx2, f"ffn_in_bwd_{l}")
    g["norm_ffn_g"] = dg[0]

    (dproj, doa, a_tok, merged, dya, dyc, dyp, uc, dd, dscale, dcw) = mix_bwd(
        sv["proj"], sv["oa"], dx1, w["wa"], w["wc"], w["wp"], w["scale"], w["cw"], w["wo"], f"mix_bwd_{l}")
    g["w_o"] = tn_matmul(merged, dx1, f"dw_o_{l}")
    g["w_attn_out"], g["w_conv_out"], dwp = tn_matmuls([(a_tok, dya), (uc, dyc), (dd, dyp)], f"dw_branches_{l}")
    g["pool_w"] = _pool_from_block_diag(dwp)
    g["pool_scale"] = dscale[0]
    g["conv_w"] = dcw[:3]

    early = comm.early(l)
    comm.grads(l, g)
    above = comm.scatter_ici(l + 1, sums)
    (dqa, dka, dva), got = attn_backward(sv["qa"], sv["ka"], sv["va"], sv["oa"], doa, sv["lse"], f"attn_bwd_{l}",
                                         join_exchanges(above, comm.scatter_d2d(l, early) if early else None))
    n_above = len(above.out_shapes) if above else 0
    comm.scattered(got[:n_above])
    early_sums = dict(zip(early, comm.pair_sums(l, got[n_above:], early))) if early else {}
    early_ici = lambda which: comm.scatter_ici(l, [early_sums[t] for t in which], which) if early else None
    dproj, dc, dgq, dgk = attn_post(dqa, dka, dva, sv["proj"], w["gq2"], w["gk2"], dproj, f"attn_post_{l}")
    g["q_norm_g"] = dgq[0, :HEAD_DIM] + dgq[0, HEAD_DIM:]
    g["k_norm_g"] = dgk[0, :HEAD_DIM] + dgk[0, HEAD_DIM:]
    dproj, db = forget_bwd(dc, sv["z"], dproj, f"forget_bwd_{l}")
    g["forget_b"] = db[0, :HEADS]

    dw_in = tn_matmul(dproj, sv["h"], f"dw_in_{l}", m_cols=N_FULL, ex=early_ici(EARLY_FIRST))
    if early:
        dw_in, got = dw_in
        comm.scattered(got, EARLY_FIRST)
    g["w_in"] = _ungroup_w_in(dw_in)
    (dx, dg), got = matmul_normbwd(dproj, w["wt_in"], sv["x"], w["g_mix"], dx1, f"in_proj_bwd_{l}", k=N_FULL,
                                   ex=early_ici(EARLY_SECOND))
    comm.scattered(got, EARLY_SECOND if early else None)
    g["norm_mix_g"] = dg[0]
    comm.grads(l, g)
    return dx


def _local_step(x, tgt, comm):
    ws, saved = [], []
    w = comm.weights(0, None)
    for l in range(DEPTH):
        ws.append(w)
        x, sv, gathered = _layer_fwd(x, w, l, comm)
        saved.append(sv)
        if l + 1 < DEPTH:
            w = comm.weights(l + 1, gathered)
    sq, dx = loss_kernel(x, tgt, "loss")
    for l in reversed(range(DEPTH)):
        dx = _layer_bwd(dx, saved[l], ws[l], l, comm)
    comm.finish()
    return sq[0, 0], dx


def kernel(x, norm_mix_g, w_in, forget_b, q_norm_g, k_norm_g, w_attn_out, conv_w, w_conv_out, pool_w, pool_scale, w_o, norm_ffn_g, w_ffn_in, w_ffn_out, loss_target, m_norm_mix_g, m_w_in, m_forget_b, m_q_norm_g, m_k_norm_g, m_w_attn_out, m_conv_w, m_w_conv_out, m_pool_w, m_pool_scale, m_w_o, m_norm_ffn_g, m_w_ffn_in, m_w_ffn_out, v_norm_mix_g, v_w_in, v_forget_b, v_q_norm_g, v_k_norm_g, v_w_attn_out, v_conv_w, v_w_conv_out, v_pool_w, v_pool_scale, v_w_o, v_norm_ffn_g, v_w_ffn_in, v_w_ffn_out):
    w = dict(norm_mix_g=norm_mix_g, w_in=w_in, forget_b=forget_b, q_norm_g=q_norm_g, k_norm_g=k_norm_g,
             w_attn_out=w_attn_out, conv_w=conv_w, w_conv_out=w_conv_out, pool_w=pool_w, pool_scale=pool_scale,
             w_o=w_o, norm_ffn_g=norm_ffn_g, w_ffn_in=w_ffn_in, w_ffn_out=w_ffn_out)
    m = dict(norm_mix_g=m_norm_mix_g, w_in=m_w_in, forget_b=m_forget_b, q_norm_g=m_q_norm_g, k_norm_g=m_k_norm_g,
             w_attn_out=m_w_attn_out, conv_w=m_conv_w, w_conv_out=m_w_conv_out, pool_w=m_pool_w,
             pool_scale=m_pool_scale, w_o=m_w_o, norm_ffn_g=m_norm_ffn_g, w_ffn_in=m_w_ffn_in, w_ffn_out=m_w_ffn_out)
    v = dict(norm_mix_g=v_norm_mix_g, w_in=v_w_in, forget_b=v_forget_b, q_norm_g=v_q_norm_g, k_norm_g=v_k_norm_g,
             w_attn_out=v_w_attn_out, conv_w=v_conv_w, w_conv_out=v_w_conv_out, pool_w=v_pool_w,
             pool_scale=v_pool_scale, w_o=v_w_o, norm_ffn_g=v_norm_ffn_g, w_ffn_in=v_w_ffn_in, w_ffn_out=v_w_ffn_out)
    me = 4 * lax.axis_index("x") + 2 * lax.axis_index("y") + lax.axis_index("c")
    layer_shard = {n: SHARD_INFO[n][0][1:] for n in MATRICES}
    cut_axis = {n: SHARD_INFO[n][1] - 1 for n in MATRICES}

    vec = {n: w[n] for n in VECTORS}
    rc = {n: (_size(layer_shard[n][:-1]), layer_shard[n][-1]) for n in MATRICES}

    class Comm:
        bufs = [lax.empty((DEPTH, len(SAME_CORE)) + layer_shard[n], BF16) for n in MATRICES]
        blocks = [None] * DEPTH
        small_g = [None] * DEPTH
        conv_full = None

        @staticmethod
        def shards(l):
            return [_handled(n, w[n])[l].astype(BF16) for n in MATRICES]

        @staticmethod
        def gather_ici(l, part):
            return gather_over_ici([Comm.shards(l)[t] for t in part]) if l < DEPTH else None

        @staticmethod
        def gather_d2d(l, half):
            return gather_over_d2d(half) if l < DEPTH else None

        @staticmethod
        def weights(l, gathered):
            if l == 0:
                *gathered, conv_g = all_gather(Comm.shards(0) + [_pack([conv_w], 8, 128)], "gather_0")
                Comm.conv_full = _join_shards(jnp.stack([_unpack(conv_g[i], [conv_w.shape])[0] for i in range(N_DEV)]), 2)
            mats = {n: t if n == "w_in" else _join_shards(t, cut_axis[n]) for n, t in zip(MATRICES, gathered)}
            return _layer_weights(mats, vec, Comm.conv_full, l)

        @staticmethod
        def grads(l, g):
            Comm.small_g[l] = g
            Comm.blocks[l] = [None if n not in g else g[n] if n == "w_in" else _cut_shards(g[n], cut_axis[n])
                              for n in MATRICES]

        @staticmethod
        def early(l):
            return EARLY if l == 0 else None

        @staticmethod
        def scatter_d2d(l, which=EVERY):
            return scatter_over_d2d([Comm.blocks[l][t] for t in which]) if l < DEPTH else None

        @staticmethod
        def pair_sums(l, stage, which=EVERY):
            if l >= DEPTH:
                return None
            return [pair_sum(Comm.blocks[l][t].reshape((N_DEV,) + rc[MATRICES[t]]),
                             s.reshape((len(SAME_CORE),) + rc[MATRICES[t]]), me,
                             f"pair_sum_{MATRICES[t]}_{l}").reshape(s.shape) for t, s in zip(which, stage)]

        @staticmethod
        def scatter_ici(l, sums, which=EVERY):
            return scatter_over_ici(sums, [Comm.bufs[t] for t in which], l) if l < DEPTH else None

        @staticmethod
        def scattered(results, which=EVERY):
            for t, r in zip(which or (), results):
                Comm.bufs[t] = r

        @staticmethod
        def finish():
            stage = run_exchange(Comm.scatter_d2d(0, LATE), "scatter_d2d_0")
            Comm.scattered(run_exchange(Comm.scatter_ici(0, Comm.pair_sums(0, stage, LATE), LATE), "scatter_ici_0"), LATE)

    small_g, received = Comm.small_g, Comm
    sq, dx = _local_step(x[0], loss_target[0], Comm)
    loss = lax.psum(0.5 * sq / D_MODEL, ("x", "y", "c"))

    big = {}
    for n, parts in zip(MATRICES, received.bufs):
        outs = adamw_sum(parts.reshape((DEPTH, len(SAME_CORE)) + rc[n]),
                         *[_handled(n, d[n]).reshape((DEPTH,) + rc[n]) for d in (w, m, v)], f"adamw_{n}")
        big[n] = [_handled(n, t.reshape((DEPTH,) + layer_shard[n])) for t in outs]

    small_shapes = [VECTOR_SHAPES[n] for n in VECTORS] + [CONV_W_FULL]
    stacked = [jnp.stack([small_g[l][n] for l in range(DEPTH)]) for n in VECTORS + ("conv_w",)]
    sparts = all_gather([_pack(stacked, SMALL_ROWS, 128)], "gather_vector_grads")[0]
    col0 = me * (D_CONV // N_DEV)
    place = lambda t: lax.dynamic_update_slice(jnp.zeros(CONV_W_FULL, F32), t, (0, 0, col0))
    spacked = [_pack([d[n] for n in VECTORS] + [place(d["conv_w"])], SMALL_ROWS, 128)[None] for d in (w, m, v)]
    small = [_unpack(t[0], small_shapes) for t in adamw_sum(sparts[None], *spacked, "adamw_vectors")]

    def result(kind):
        out = {n: big[n][kind] for n in MATRICES}
        out.update({n: small[kind][j] for j, n in enumerate(VECTORS)})
        out["conv_w"] = lax.dynamic_slice(small[kind][len(VECTORS)], (0, 0, col0), conv_w.shape)
        return [out[n] for n in w]

    return (loss, dx[None], *result(0), *result(1), *result(2), *result(3))
```

```python
import functools

import jax
import jax.numpy as jnp
from jax import lax
from jax.experimental import pallas as pl
from jax.experimental.pallas import tpu as pltpu

F32 = jnp.float32
BF16 = jnp.bfloat16

N_DEV = 8
DEPTH = 4
D_MODEL = 1024
HEAD_DIM = 64
HEADS = 8
D_ATTN = 512
D_CONV = 256
D_POOL = 256
D_FF = 2816
D_IN = 5640
EPS = 1e-6
ATTN_SCALE = HEAD_DIM ** -0.5

N_REST = 4096
N_MAIN = 5632
N_FULL = 5760
DPROJ_TAIL = 2048
DPROJ_COLS = N_REST + DPROJ_TAIL
FF_BLK = 256
N_FF_BLKS = D_FF // FF_BLK
HALO = 16

ADAM_LR = 0.001
ADAM_B1 = 0.9
ADAM_B2 = 0.999
ADAM_EPS = 1e-08
ADAM_WD = 0.01
ADAM_STEP = 10

SMALL_ROWS = 128

VMEM_LIMIT = 48 * 2 ** 20


def _cparams(sem, vmem=None):
    return pltpu.CompilerParams(dimension_semantics=sem, vmem_limit_bytes=vmem or VMEM_LIMIT)


def _pick(n, cands):
    for c in cands:
        if n % c == 0:
            return c
    raise ValueError(f"no tile for {n}")


def _tile(n, cap):
    t = min(cap, n)
    assert n % t == 0, (n, cap)
    return t


def _sigmoid(v):
    return 1.0 / (1.0 + jnp.exp(-v))


def _rstd(v):
    return lax.rsqrt(jnp.mean(v * v, axis=-1, keepdims=True) + EPS)


def _dot(a, b):
    return jnp.dot(a, b, preferred_element_type=F32)


def _dot_tn(a, b):
    return lax.dot_general(a, b, (((0,), (0,)), ((), ())), preferred_element_type=F32)


def _dot_nt(a, b):
    return lax.dot_general(a, b, (((1,), (1,)), ((), ())), preferred_element_type=F32)


def norm_matmul(x, g, wt, n_cols, name, ex=None):
    s, d = x.shape
    tm, tn = _tile(s, 1024), _pick(n_cols, (2816, 1408, 512))

    def body(x_ref, g_ref, w_ref, o_ref, h_ref):
        @pl.when(pl.program_id(1) == 0)
        def _():
            xv = x_ref[...]
            h_ref[...] = (xv * _rstd(xv) * g_ref[...]).astype(BF16)

        o_ref[...] = _dot_nt(h_ref[...], w_ref[...]).astype(BF16)

    return _carried_call(
        body, ex, (s // tm, n_cols // tn),
        [pl.BlockSpec((tm, d), lambda i, j: (i, 0)), pl.BlockSpec((1, d), lambda i, j: (0, 0)),
         pl.BlockSpec((tn, d), lambda i, j: (j, 0))],
        [pl.BlockSpec((tm, tn), lambda i, j: (i, j)), pl.BlockSpec((tm, d), lambda i, j: (i, 0))],
        [jax.ShapeDtypeStruct((s, n_cols), BF16), jax.ShapeDtypeStruct((s, d), BF16)], [],
        ("arbitrary", "arbitrary"), name, (x, g, wt))


def tn_matmul(a, b, name, m_cols=None, ex=None):
    t = a.shape[0]
    m = m_cols or a.shape[1]
    n = b.shape[1]
    tk = _tile(t, 1024)
    tmm = _pick(m, (1408, 1152, 1024, 512, 256))
    tn = _pick(n, (1408, 1152, 1024, 512, 128))
    nk = t // tk

    def body(a_ref, b_ref, o_ref, acc_ref):
        @pl.when(pl.program_id(2) == 0)
        def _():
            acc_ref[...] = jnp.zeros_like(acc_ref)

        acc_ref[...] += _dot_tn(a_ref[...].astype(BF16), b_ref[...].astype(BF16))

        @pl.when(pl.program_id(2) == nk - 1)
        def _():
            o_ref[...] = acc_ref[...].astype(BF16)

    if ex is None:
        return pl.pallas_call(
            body, grid=(m // tmm, n // tn, nk),
            in_specs=[pl.BlockSpec((tk, tmm), lambda i, j, k: (k, i)), pl.BlockSpec((tk, tn), lambda i, j, k: (k, j))],
            out_specs=pl.BlockSpec((tmm, tn), lambda i, j, k: (i, j)),
            out_shape=jax.ShapeDtypeStruct((m, n), BF16), scratch_shapes=[pltpu.VMEM((tmm, tn), F32)],
            compiler_params=_cparams(("parallel", "parallel", "arbitrary")), name=name)(a, b)
    (out,), carried = _carried_call(
        body, ex, (m // tmm, n // tn, nk),
        [pl.BlockSpec((tk, tmm), lambda i, j, k: (k, i)), pl.BlockSpec((tk, tn), lambda i, j, k: (k, j))],
        [pl.BlockSpec((tmm, tn), lambda i, j, k: (i, j))], [jax.ShapeDtypeStruct((m, n), BF16)],
        [pltpu.VMEM((tmm, tn), F32)], ("arbitrary", "arbitrary", "arbitrary"), name, (a, b))
    return out, carried


def tn_matmuls(pairs, name):
    t = pairs[0][0].shape[0]
    tk = _tile(t, 1024)
    nk = t // tk
    n = len(pairs)

    def body(*refs):
        ins, outs, accs = refs[:2 * n], refs[2 * n:3 * n], refs[3 * n:]

        @pl.when(pl.program_id(0) == 0)
        def _():
            for acc in accs:
                acc[...] = jnp.zeros_like(acc)

        for i in range(n):
            accs[i][...] += _dot_tn(ins[2 * i][...], ins[2 * i + 1][...])

        @pl.when(pl.program_id(0) == nk - 1)
        def _():
            for out, acc in zip(outs, accs):
                out[...] = acc[...].astype(BF16)

    shapes = [(a.shape[1], b.shape[1]) for a, b in pairs]
    return pl.pallas_call(
        body, grid=(nk,),
        in_specs=[pl.BlockSpec((tk, t_.shape[1]), lambda k: (k, 0)) for pair in pairs for t_ in pair],
        out_specs=[pl.BlockSpec(shp, lambda k: (0, 0)) for shp in shapes],
        out_shape=[jax.ShapeDtypeStruct(shp, BF16) for shp in shapes],
        scratch_shapes=[pltpu.VMEM(shp, F32) for shp in shapes],
        compiler_params=_cparams(("arbitrary",)), name=name)(*[t_ for pair in pairs for t_ in pair])


def matmul_normbwd(a, wt, x, g, dres, name, k=None, ex=None):
    s = a.shape[0]
    k = k or a.shape[1]
    d = wt.shape[1]
    tm = _tile(s, 1024)
    tk = _pick(k, (1408, 1152, 512))
    nk = k // tk

    def body(a_ref, w_ref, x_ref, g_ref, r_ref, dx_ref, dg_ref, acc_ref):
        i, kk = pl.program_id(0), pl.program_id(1)

        @pl.when(kk == 0)
        def _():
            acc_ref[...] = jnp.zeros_like(acc_ref)

        @pl.when((i == 0) & (kk == 0))
        def _():
            dg_ref[...] = jnp.zeros_like(dg_ref)

        acc_ref[...] += _dot(a_ref[...], w_ref[...])

        @pl.when(kk == nk - 1)
        def _():
            xv = x_ref[...]
            r = _rstd(xv)
            y = xv * r
            dh = acc_ref[...]
            dy = dh * g_ref[...]
            dx_ref[...] = r_ref[...] + r * (dy - y * jnp.mean(dy * y, axis=-1, keepdims=True))
            dg_ref[...] += jnp.sum(dh * y, axis=0, keepdims=True)

    return _carried_call(
        body, ex, (s // tm, nk),
        [pl.BlockSpec((tm, tk), lambda i, kk: (i, kk)), pl.BlockSpec((tk, d), lambda i, kk: (kk, 0)),
         pl.BlockSpec((tm, d), lambda i, kk: (i, 0)), pl.BlockSpec((1, d), lambda i, kk: (0, 0)),
         pl.BlockSpec((tm, d), lambda i, kk: (i, 0))],
        [pl.BlockSpec((tm, d), lambda i, kk: (i, 0)), pl.BlockSpec((1, d), lambda i, kk: (0, 0))],
        [jax.ShapeDtypeStruct((s, d), F32), jax.ShapeDtypeStruct((1, d), F32)],
        [pltpu.VMEM((tm, d), F32)], ("arbitrary", "arbitrary"), name, (a, wt, x, g, dres), vmem=56 * 2 ** 20)


def swiglu_matmul(gu, w, x1, name):
    s = gu.shape[0]
    d = w.shape[1]
    tm = _tile(s, 512)

    def body(gu_ref, w_ref, x_ref, o_ref):
        acc = x_ref[...]
        for j in range(N_FF_BLKS):
            gt = gu_ref[:, j * FF_BLK:(j + 1) * FF_BLK].astype(F32)
            up = gu_ref[:, D_FF + j * FF_BLK:D_FF + (j + 1) * FF_BLK].astype(F32)
            act = (gt * _sigmoid(gt) * up).astype(BF16)
            acc += _dot(act, w_ref[j * FF_BLK:(j + 1) * FF_BLK, :])
        o_ref[...] = acc

    return pl.pallas_call(
        body, grid=(s // tm,),
        in_specs=[pl.BlockSpec((tm, 2 * D_FF), lambda i: (i, 0)), pl.BlockSpec((D_FF, d), lambda i: (0, 0)),
                  pl.BlockSpec((tm, d), lambda i: (i, 0))],
        out_specs=pl.BlockSpec((tm, d), lambda i: (i, 0)),
        out_shape=jax.ShapeDtypeStruct((s, d), F32),
        compiler_params=_cparams(("parallel",)), name=name)(gu, w, x1)


def swiglu_bwd(dx2, gu, w, name, ex=None):
    s, d = dx2.shape
    tm = _tile(s, 512)

    def body(dx_ref, gu_ref, w_ref, dgu_ref, act_ref):
        dx = dx_ref[...].astype(BF16)
        for j in range(N_FF_BLKS):
            g_cols = slice(j * FF_BLK, (j + 1) * FF_BLK)
            u_cols = slice(D_FF + j * FF_BLK, D_FF + (j + 1) * FF_BLK)
            dact = _dot_nt(dx, w_ref[j * FF_BLK:(j + 1) * FF_BLK, :])
            gt = gu_ref[:, g_cols].astype(F32)
            up = gu_ref[:, u_cols].astype(F32)
            sg = _sigmoid(gt)
            silu = gt * sg
            act_ref[:, j * FF_BLK:(j + 1) * FF_BLK] = (silu * up).astype(BF16)
            dgu_ref[:, g_cols] = (dact * up * (sg + silu * (1.0 - sg))).astype(BF16)
            dgu_ref[:, u_cols] = (dact * silu).astype(BF16)

    return _carried_call(
        body, ex, (s // tm,),
        [pl.BlockSpec((tm, d), lambda i: (i, 0)), pl.BlockSpec((tm, 2 * D_FF), lambda i: (i, 0)),
         pl.BlockSpec((D_FF, d), lambda i: (0, 0), pipeline_mode=pl.Buffered(1))],
        [pl.BlockSpec((tm, 2 * D_FF), lambda i: (i, 0)), pl.BlockSpec((tm, D_FF), lambda i: (i, 0))],
        [jax.ShapeDtypeStruct((s, 2 * D_FF), BF16), jax.ShapeDtypeStruct((s, D_FF), BF16)], [],
        ("arbitrary",), name, (dx2, gu, w), vmem=56 * 2 ** 20)


def loss_kernel(y, tgt, name):
    s, d = y.shape
    tm = _tile(s, 512)

    def body(y_ref, t_ref, l_ref, dy_ref):
        @pl.when(pl.program_id(0) == 0)
        def _():
            l_ref[...] = jnp.zeros_like(l_ref)

        err = y_ref[...] - t_ref[...]
        dy_ref[...] = err * (1.0 / d)
        l_ref[...] += jnp.sum(jnp.sum(err * err, axis=1, keepdims=True), axis=0, keepdims=True)

    return pl.pallas_call(
        body, grid=(s // tm,),
        in_specs=[pl.BlockSpec((tm, d), lambda i: (i, 0)), pl.BlockSpec((tm, d), lambda i: (i, 0))],
        out_specs=[pl.BlockSpec((8, 128), lambda i: (0, 0)), pl.BlockSpec((tm, d), lambda i: (i, 0))],
        out_shape=[jax.ShapeDtypeStruct((8, 128), F32), jax.ShapeDtypeStruct((s, d), F32)],
        compiler_params=_cparams(("arbitrary",)), name=name)(y, tgt)


def _split3(v):
    a1 = v.astype(BF16)
    r1 = v - a1.astype(F32)
    a2 = r1.astype(BF16)
    a3 = (r1 - a2.astype(F32)).astype(BF16)
    return a1, a2, a3


def forget_fwd(h, wt_in, b, name):
    s, d = h.shape
    tm = _tile(s, 512)

    def body(h_ref, w_ref, b_ref, z_ref, c_ref, carry_ref):
        @pl.when(pl.program_id(0) == 0)
        def _():
            carry_ref[...] = jnp.zeros_like(carry_ref)

        z = _dot_nt(h_ref[...], w_ref[...]) + b_ref[...]
        z_ref[...] = z
        logf = jnp.minimum(z, 0.0) - jnp.log(1.0 + jnp.exp(-jnp.abs(z)))
        row = lax.broadcasted_iota(jnp.int32, (tm, tm), 0)
        col = lax.broadcasted_iota(jnp.int32, (tm, tm), 1)
        tri = (row >= col).astype(BF16)
        a1, a2, a3 = _split3(logf)
        c = _dot(tri, a1) + _dot(tri, a2) + _dot(tri, a3) + carry_ref[...]
        c_ref[...] = c
        carry_ref[...] = c[tm - 1:tm, :]

    return pl.pallas_call(
        body, grid=(s // tm,),
        in_specs=[pl.BlockSpec((tm, d), lambda i: (i, 0)), pl.BlockSpec((128, d), lambda i: (N_MAIN // 128, 0)),
                  pl.BlockSpec((1, 128), lambda i: (0, 0))],
        out_specs=[pl.BlockSpec((tm, 128), lambda i: (i, 0)), pl.BlockSpec((tm, 128), lambda i: (i, 0))],
        out_shape=[jax.ShapeDtypeStruct((s, 128), F32), jax.ShapeDtypeStruct((s, 128), F32)],
        scratch_shapes=[pltpu.VMEM((1, 128), F32)],
        compiler_params=_cparams(("arbitrary",)), name=name)(h, wt_in, b)


def forget_bwd(dc, z, dproj, name):
    s = dc.shape[0]
    tm = _tile(s, 512)
    nt = s // tm

    def body(dc_ref, z_ref, dp_ref, dz_ref, db_ref, carry_ref):
        @pl.when(pl.program_id(0) == 0)
        def _():
            carry_ref[...] = jnp.zeros_like(carry_ref)
            db_ref[...] = jnp.zeros_like(db_ref)

        row = lax.broadcasted_iota(jnp.int32, (tm, tm), 0)
        col = lax.broadcasted_iota(jnp.int32, (tm, tm), 1)
        tri = (col >= row).astype(BF16)
        a1, a2, a3 = _split3(dc_ref[...])
        dlogf = _dot(tri, a1) + _dot(tri, a2) + _dot(tri, a3) + carry_ref[...]
        carry_ref[...] = dlogf[0:1, :]
        dz = dlogf * (1.0 - _sigmoid(z_ref[...]))
        dz_ref[...] = dz.astype(BF16)
        db_ref[...] += jnp.sum(dz, axis=0, keepdims=True)

    return pl.pallas_call(
        body, grid=(nt,),
        in_specs=[pl.BlockSpec((tm, 128), lambda i: (nt - 1 - i, 0)), pl.BlockSpec((tm, 128), lambda i: (nt - 1 - i, 0)),
                  pl.BlockSpec(memory_space=pl.ANY)],
        out_specs=[pl.BlockSpec((tm, 128), lambda i: (nt - 1 - i, N_MAIN // 128)), pl.BlockSpec((1, 128), lambda i: (0, 0))],
        out_shape=[jax.ShapeDtypeStruct(dproj.shape, BF16), jax.ShapeDtypeStruct((1, 128), F32)],
        scratch_shapes=[pltpu.VMEM((1, 128), F32)], input_output_aliases={2: 0},
        compiler_params=_cparams(("arbitrary",)), name=name)(dc, z, dproj)


HEAD_GROUP_FWD = 8
HEAD_GROUP_BWD = 8
LANE_C = 64
LANE_ONE = 67


def _lanes():
    lane = lax.broadcasted_iota(jnp.int32, (1, 128), 1)
    return lane, lane < HEAD_DIM


def _half_mean(t, lo):
    s_lo = jnp.sum(jnp.where(lo, t, 0.0), axis=-1, keepdims=True)
    s_hi = jnp.sum(jnp.where(lo, 0.0, t), axis=-1, keepdims=True)
    return jnp.where(lo, s_lo, s_hi) * (1.0 / HEAD_DIM)


def _lane_col(t, lane, idx):
    return jnp.sum(jnp.where(lane == idx, t, 0.0), axis=-1, keepdims=True)


def _swap_halves(t):
    return pltpu.roll(t, HEAD_DIM, 1)


def attn_prep(proj, c, gq2, gk2, name):
    s = proj.shape[0]
    tm = _tile(s, 512)
    first = N_REST // D_ATTN

    def body(q_ref, k_ref, v_ref, c_ref, gq_ref, gk_ref, qa_ref, ka_ref, va_ref, vt_ref):
        lane, lo = _lanes()

        def normed(t, g):
            t = t.astype(F32)
            return t * lax.rsqrt(_half_mean(t * t, lo) + EPS) * g

        cv = c_ref[...]
        one_q = jnp.where((lane >= LANE_ONE) & (lane < LANE_ONE + 3), 1.0, 0.0)
        one_k = jnp.where((lane >= LANE_C) & (lane < LANE_C + 3), 1.0, 0.0)
        one_v = jnp.where(lane == LANE_C, 1.0, 0.0)
        for j in range(HEADS // 2):
            cols = slice(128 * j, 128 * (j + 1))
            qn = normed(q_ref[:, cols], gq_ref[...] * ATTN_SCALE)
            kn = normed(k_ref[:, cols], gk_ref[...])
            vv = v_ref[:, cols].astype(F32)
            for e in range(2):
                h = 2 * j + e
                pick = (lambda t: t) if e == 0 else _swap_halves
                pieces = [p.astype(F32) for p in _split3(_lane_col(cv, lane, h))]
                ext_q, ext_k = one_q, one_k
                for i, p in enumerate(pieces):
                    ext_q = jnp.where(lane == LANE_C + i, p, ext_q)
                    ext_k = jnp.where(lane == LANE_ONE + i, -p, ext_k)
                qa_ref[h] = jnp.where(lo, pick(qn), ext_q).astype(BF16)
                ka_ref[h] = jnp.where(lo, pick(kn), ext_k).astype(BF16)
                va = jnp.where(lo, pick(vv), one_v)
                va_ref[h] = va.astype(BF16)
                vt_ref[h] = va.T.astype(BF16)

    tile = lambda blk: pl.BlockSpec((tm, D_ATTN), lambda i: (i, blk))
    vec = pl.BlockSpec((1, 128), lambda i: (0, 0))
    out = pl.BlockSpec((HEADS, tm, 128), lambda i: (0, i, 0))
    return pl.pallas_call(
        body, grid=(s // tm,),
        in_specs=[tile(first), tile(first + 1), tile(first + 2), pl.BlockSpec((tm, 128), lambda i: (i, 0)), vec, vec],
        out_specs=[out, out, out, pl.BlockSpec((HEADS, 128, tm), lambda i: (0, 0, i))],
        out_shape=[jax.ShapeDtypeStruct((HEADS, s, 128), BF16)] * 3 + [jax.ShapeDtypeStruct((HEADS, 128, s), BF16)],
        compiler_params=_cparams(("parallel",)), name=name)(proj, proj, proj, c, gq2, gk2)


def _carry(ex, n_in, n_out, n_scratch, grid):
    n_xin, n_xout = (len(ex.inputs), len(ex.out_shapes)) if ex else (0, 0)

    def split(refs):
        ins, xins = refs[:n_in], refs[n_in:n_in + n_xin]
        rest = refs[n_in + n_xin:]
        outs, xouts = rest[:n_out], rest[n_out:n_out + n_xout]
        rest = rest[n_out + n_xout:]
        return ins + outs + rest[:n_scratch], (xins, xouts, rest[n_scratch:])

    def first():
        return functools.reduce(lambda a, b: a & b, [pl.program_id(d) == 0 for d in range(len(grid))])

    def last():
        return functools.reduce(lambda a, b: a & b, [pl.program_id(d) == grid[d] - 1 for d in range(len(grid))])

    return split, first, last


def _carried_call(body, ex, grid, in_specs, out_specs, out_shape, scratch, sem, name, operands, vmem=None):
    any_spec = pl.BlockSpec(memory_space=pl.ANY)
    split, first, last = _carry(ex, len(in_specs), len(out_specs), len(scratch), grid)

    def carried(*refs):
        own, xrefs = split(refs)
        if ex:
            @pl.when(first())
            def _():
                ex.start(*xrefs)

        body(*own)
        if ex:
            @pl.when(last())
            def _():
                ex.drain(*xrefs)

    n_xin = len(ex.inputs) if ex else 0
    results = pl.pallas_call(
        carried, grid=grid, in_specs=list(in_specs) + [any_spec] * n_xin,
        out_specs=list(out_specs) + [any_spec] * (len(ex.out_shapes) if ex else 0),
        out_shape=list(out_shape) + (list(ex.out_shapes) if ex else []),
        input_output_aliases={len(in_specs) + i: len(out_specs) + o for i, o in ex.aliases.items()} if ex else {},
        scratch_shapes=list(scratch) + (ex.scratch if ex else []),
        compiler_params=_cparams(sem, vmem), name=name)(*operands, *(ex.inputs if ex else []))
    return results[:len(out_specs)], results[len(out_specs):]


def _tri_rows(t, n):
    qi = sum(jnp.where(t >= r * (r + 1) // 2, 1, 0) for r in range(1, n))
    return qi, t - qi * (qi + 1) // 2


def _tri_cols(t, n):
    ki = sum(jnp.where(t >= r * n - r * (r - 1) // 2, 1, 0) for r in range(1, n))
    return ki, ki + t - (ki * n - ki * (ki - 1) // 2)


def _causal_t(st_blk, tk, tq):
    key = lax.broadcasted_iota(jnp.int32, (tk, tq), 0)
    qry = lax.broadcasted_iota(jnp.int32, (tk, tq), 1)
    return jnp.where(qry >= key, st_blk, -jnp.inf)


def attn_forward(qa, ka, vt, name, ex=None):
    hh, s, _ = qa.shape
    tq = tk = _tile(s, 512)
    nq = s // tq
    grp = HEAD_GROUP_FWD

    def body(q_ref, k_ref, vt_ref, o_ref, lse_ref, m_ref, acc_ref):
        qi, ki = _tri_rows(pl.program_id(1), nq)

        @pl.when(ki == 0)
        def _():
            m_ref[...] = jnp.full_like(m_ref, -jnp.inf)
            acc_ref[...] = jnp.zeros_like(acc_ref)

        def step(masked):
            nxt = _dot_nt(k_ref[0], q_ref[0])
            for g in range(grp):
                st = nxt
                if g + 1 < grp:
                    nxt = _dot_nt(k_ref[g + 1], q_ref[g + 1])
                if masked:
                    st = _causal_t(st, tk, tq)
                m_old = m_ref[g]
                m_new = jnp.maximum(m_old, jnp.max(st, axis=0, keepdims=True))
                pt = jnp.exp(st - m_new).astype(BF16)
                acc_ref[g] = jnp.exp(m_old - m_new) * acc_ref[g] + _dot(vt_ref[g], pt)
                m_ref[g] = m_new

        @pl.when(ki < qi)
        def _():
            step(False)

        @pl.when(ki == qi)
        def _():
            step(True)
            for g in range(grp):
                acc = acc_ref[g]
                denom = acc[LANE_C:LANE_C + 1, :]
                o_ref[g] = (acc / denom).T.astype(BF16)
                lse_ref[g] = m_ref[g] + jnp.log(denom)

    qspec = pl.BlockSpec((grp, tq, 128), lambda h, t: (h, _tri_rows(t, nq)[0], 0))
    kspec = pl.BlockSpec((grp, tk, 128), lambda h, t: (h, _tri_rows(t, nq)[1], 0))
    vspec = pl.BlockSpec((grp, 128, tk), lambda h, t: (h, 0, _tri_rows(t, nq)[1]))
    lspec = pl.BlockSpec((grp, 1, tq), lambda h, t: (h, 0, _tri_rows(t, nq)[0]))
    return _carried_call(
        body, ex, (hh // grp, nq * (nq + 1) // 2), [qspec, kspec, vspec], [qspec, lspec],
        [jax.ShapeDtypeStruct((hh, s, 128), BF16), jax.ShapeDtypeStruct((hh, 1, s), F32)],
        [pltpu.VMEM((grp, 1, tq), F32), pltpu.VMEM((grp, 128, tq), F32)],
        ("arbitrary", "arbitrary"), name, (qa, ka, vt))


def attn_backward(qa, ka, va, oa, doa, lse, name, ex=None):
    hh, s, _ = qa.shape
    tq = tk = _tile(s, 512)
    nq = s // tq
    grp = HEAD_GROUP_BWD

    def body(q_ref, k_ref, v_ref, o_ref, do_ref, lse_ref, dq_ref, dk_ref, dv_ref, dka_ref, dva_ref):
        ki, qi = _tri_cols(pl.program_id(1), nq)

        @pl.when(pl.program_id(1) == 0)
        def _():
            dq_ref[...] = jnp.zeros_like(dq_ref)

        @pl.when(qi == ki)
        def _():
            dka_ref[...] = jnp.zeros_like(dka_ref)
            dva_ref[...] = jnp.zeros_like(dva_ref)

        def step(masked):
            rows = pl.ds(pl.multiple_of(qi * tq, tq), tq)
            products = lambda g: (_dot_nt(k_ref[g], q_ref[g]), _dot_nt(v_ref[g], do_ref[g]))
            nxt = products(0)
            for g in range(grp):
                st, dpt = nxt
                if g + 1 < grp:
                    nxt = products(g + 1)
                q, k, do = q_ref[g], k_ref[g], do_ref[g]
                if masked:
                    st = _causal_t(st, tk, tq)
                pt = jnp.exp(st - lse_ref[g])
                delta = jnp.sum((do.astype(F32) * o_ref[g].astype(F32)).T, axis=0, keepdims=True)
                dst = (pt * (dpt - delta)).astype(BF16)
                dva_ref[g] += _dot(pt.astype(BF16), do)
                dka_ref[g] += _dot(dst, q)
                dq_ref[g, rows, :] += _dot_tn(dst, k)

        @pl.when(qi > ki)
        def _():
            step(False)

        @pl.when(qi == ki)
        def _():
            step(True)

        @pl.when(qi == nq - 1)
        def _():
            dk_ref[...] = dka_ref[...]
            dv_ref[...] = dva_ref[...].astype(BF16)

    qspec = pl.BlockSpec((grp, tq, 128), lambda h, t: (h, _tri_cols(t, nq)[1], 0))
    lspec = pl.BlockSpec((grp, 1, tq), lambda h, t: (h, 0, _tri_cols(t, nq)[1]))
    kspec = pl.BlockSpec((grp, tk, 128), lambda h, t: (h, _tri_cols(t, nq)[0], 0))
    return _carried_call(
        body, ex, (hh // grp, nq * (nq + 1) // 2), [qspec, kspec, kspec, qspec, qspec, lspec],
        [pl.BlockSpec((grp, s, 128), lambda h, t: (h, 0, 0), pipeline_mode=pl.Buffered(1)), kspec, kspec],
        [jax.ShapeDtypeStruct((hh, s, 128), F32), jax.ShapeDtypeStruct((hh, s, 128), F32),
         jax.ShapeDtypeStruct((hh, s, 128), BF16)],
        [pltpu.VMEM((grp, tk, 128), F32), pltpu.VMEM((grp, tk, 128), F32)],
        ("arbitrary", "arbitrary"), name, (qa, ka, va, oa, doa, lse), vmem=58 * 2 ** 20)


def attn_post(dqa, dka, dva, proj, gq2, gk2, dproj, name):
    s = proj.shape[0]
    tm = _tile(s, 512)

    def body(dq_ref, dk_ref, dv_ref, q_ref, k_ref, gq_ref, gk_ref, dp_any, dp_ref, dc_ref, dgq_ref, dgk_ref):
        lane, lo = _lanes()

        @pl.when(pl.program_id(0) == 0)
        def _():
            dgq_ref[...] = jnp.zeros_like(dgq_ref)
            dgk_ref[...] = jnp.zeros_like(dgk_ref)

        def pair(ref, j):
            return jnp.where(lo, ref[2 * j].astype(F32), _swap_halves(ref[2 * j + 1].astype(F32)))

        def norm_bwd(raw, g, dhat, scale):
            r = lax.rsqrt(_half_mean(raw * raw, lo) + EPS)
            y = raw * r
            dy = dhat * (g * scale)
            return r * (dy - y * _half_mean(dy * y, lo)), jnp.sum(dhat * y, axis=0, keepdims=True) * scale

        dc = jnp.zeros((tm, 128), F32)
        for j in range(HEADS // 2):
            cols = slice(128 * j, 128 * (j + 1))
            dq, dgq = norm_bwd(q_ref[:, cols].astype(F32), gq_ref[...], pair(dq_ref, j), ATTN_SCALE)
            dk, dgk = norm_bwd(k_ref[:, cols].astype(F32), gk_ref[...], pair(dk_ref, j), 1.0)
            dgq_ref[...] += dgq
            dgk_ref[...] += dgk
            dp_ref[:, cols] = dq.astype(BF16)
            dp_ref[:, D_ATTN + 128 * j:D_ATTN + 128 * (j + 1)] = dk.astype(BF16)
            dp_ref[:, 2 * D_ATTN + 128 * j:2 * D_ATTN + 128 * (j + 1)] = pair(dv_ref, j).astype(BF16)
            for e in range(2):
                h = 2 * j + e
                both = jnp.where(lane == LANE_C, dq_ref[h], 0.0) - jnp.where(lane == LANE_ONE, dk_ref[h], 0.0)
                dc = jnp.where(lane == h, jnp.sum(both, axis=-1, keepdims=True), dc)
        dp_ref[:, 3 * D_ATTN:] = jnp.zeros((tm, DPROJ_TAIL - 3 * D_ATTN), BF16)
        dc_ref[...] = dc

    heads = lambda: pl.BlockSpec((HEADS, tm, 128), lambda i: (0, i, 0))
    vec = pl.BlockSpec((1, 128), lambda i: (0, 0))
    first = N_REST // D_ATTN
    return pl.pallas_call(
        body, grid=(s // tm,),
        in_specs=[heads(), heads(), heads(), pl.BlockSpec((tm, D_ATTN), lambda i: (i, first)),
                  pl.BlockSpec((tm, D_ATTN), lambda i: (i, first + 1)), vec, vec, pl.BlockSpec(memory_space=pl.ANY)],
        out_specs=[pl.BlockSpec((tm, DPROJ_TAIL), lambda i: (i, N_REST // DPROJ_TAIL)),
                   pl.BlockSpec((tm, 128), lambda i: (i, 0)), vec, vec],
        out_shape=[jax.ShapeDtypeStruct(dproj.shape, BF16), jax.ShapeDtypeStruct((s, 128), F32),
                   jax.ShapeDtypeStruct((1, 128), F32), jax.ShapeDtypeStruct((1, 128), F32)],
        input_output_aliases={7: 0},
        compiler_params=_cparams(("arbitrary",)), name=name)(dqa, dka, dva, proj, proj, gq2, gk2, dproj)


def _pool_groups(tm):
    gid = lax.broadcasted_iota(jnp.int32, (1, D_POOL), 1) // (D_POOL // 4)
    win = jnp.where(gid == 0, 2.0, jnp.where(gid == 1, 4.0, jnp.where(gid == 2, 8.0, 16.0)))
    return gid, win


def _by_group(gid, v2, v4, v8, v16):
    return jnp.where(gid == 0, v2, jnp.where(gid == 1, v4, jnp.where(gid == 2, v8, v16)))


def _branches(rest_ref, halo_ref, a_ref, wa_ref, wc_ref, wp_ref, sc_ref, cw_ref, ti, tm):
    f = lambda v: v.astype(F32)
    cx, cb, cc, px = f(rest_ref[:, 0:256]), f(rest_ref[:, 256:512]), f(rest_ref[:, 512:768]), f(rest_ref[:, 768:1024])
    live = jnp.where(ti > 0, 1.0, 0.0)
    hz = f(halo_ref[:, 0:256]) * f(halo_ref[:, 512:768]) * live
    hp = f(halo_ref[:, 768:1024]) * live
    z = cc * cx
    zf = jnp.concatenate([hz, z], axis=0)
    z1 = pltpu.roll(zf, 1, 0)[HALO:]
    z2 = pltpu.roll(zf, 2, 0)[HALO:]
    cw = cw_ref[...]
    conv = cw[2:3] * z + cw[1:2] * z1 + cw[0:1] * z2
    uc = cb * conv
    pf = jnp.concatenate([hp, px], axis=0)
    s2 = pf + pltpu.roll(pf, 1, 0)
    s4 = s2 + pltpu.roll(s2, 2, 0)
    s8 = s4 + pltpu.roll(s4, 4, 0)
    s16 = s8 + pltpu.roll(s8, 8, 0)
    gid, win = _pool_groups(tm)
    t = (ti * tm + lax.broadcasted_iota(jnp.int32, (tm, 1), 0)).astype(F32)
    inv = 1.0 / jnp.minimum(t + 1.0, win)
    dpool = _by_group(gid, s2[HALO:], s4[HALO:], s8[HALO:], s16[HALO:]) * inv - px
    _, lo = _lanes()
    a_tok = [jnp.where(lo, f(a_ref[2 * j]), _swap_halves(f(a_ref[2 * j + 1]))).astype(BF16) for j in range(HEADS // 2)]
    y_attn = _dot(a_tok[0], wa_ref[0:128, :])
    for j in range(1, HEADS // 2):
        y_attn += _dot(a_tok[j], wa_ref[128 * j:128 * (j + 1), :])
    y_conv = _dot(uc.astype(BF16), wc_ref[...])
    y_pool_raw = _dot(dpool.astype(BF16), wp_ref[...])
    sg = [_sigmoid(f(rest_ref[:, 1024 + i * D_MODEL:1024 + (i + 1) * D_MODEL])) for i in range(3)]
    return dict(cx=cx, cb=cb, cc=cc, z=z, z1=z1, z2=z2, conv=conv, uc=uc, dpool=dpool, inv=inv, gid=gid, a_tok=a_tok,
                y_attn=y_attn, y_conv=y_conv, y_pool_raw=y_pool_raw, sg=sg, cw=cw)


def _mix_specs(tm, ti_of):
    blocks_per_tile = tm // HALO
    return [
        pl.BlockSpec((tm, N_REST), lambda i: (ti_of(i), 0)),
        pl.BlockSpec((HALO, 1024), lambda i: (jnp.maximum(ti_of(i) * blocks_per_tile - 1, 0), 0)),
        pl.BlockSpec((HEADS, tm, 128), lambda i: (0, ti_of(i), 0)),
        pl.BlockSpec((D_ATTN, D_MODEL), lambda i: (0, 0)),
        pl.BlockSpec((D_CONV, D_MODEL), lambda i: (0, 0)),
        pl.BlockSpec((D_POOL, D_MODEL), lambda i: (0, 0)),
        pl.BlockSpec((1, D_MODEL), lambda i: (0, 0)),
        pl.BlockSpec((8, D_CONV), lambda i: (0, 0)),
    ]


def mix_fwd(proj, a, x, wa, wc, wp, scale, cw, wo, name, ex=None):
    s = x.shape[0]
    tm = _tile(s, 256)

    def body(rest_ref, halo_ref, a_ref, wa_ref, wc_ref, wp_ref, sc_ref, cw_ref, wo_ref, x_ref, o_ref):
        b = _branches(rest_ref, halo_ref, a_ref, wa_ref, wc_ref, wp_ref, sc_ref, cw_ref, pl.program_id(0), tm)
        merged = b["sg"][0] * b["y_attn"] + b["sg"][1] * b["y_conv"] + b["sg"][2] * (b["y_pool_raw"] * sc_ref[...])
        o_ref[...] = x_ref[...] + _dot(merged.astype(BF16), wo_ref[...])

    (x1,), carried = _carried_call(
        body, ex, (s // tm,),
        _mix_specs(tm, lambda i: i) + [pl.BlockSpec((D_MODEL, D_MODEL), lambda i: (0, 0)),
                                       pl.BlockSpec((tm, D_MODEL), lambda i: (i, 0))],
        [pl.BlockSpec((tm, D_MODEL), lambda i: (i, 0))], [jax.ShapeDtypeStruct((s, D_MODEL), F32)], [],
        ("arbitrary",), name, (proj, proj, a, wa, wc, wp, scale, cw, wo, x))
    return x1, carried


def mix_bwd(proj, a, dx1, wa, wc, wp, scale, cw, wo, name):
    s = dx1.shape[0]
    tm = _tile(s, 256)
    nt = s // tm
    ti_of = lambda i: nt - 1 - i
    n = tm + HALO

    def body(rest_ref, halo_ref, a_ref, wa_ref, wc_ref, wp_ref, sc_ref, cw_ref, wo_ref,
             dx_ref, dp_ref, da_ref, at_ref, mg_ref, dya_ref, dyc_ref, dyp_ref, uc_ref, dd_ref, dsc_ref, dcw_ref,
             cdc_ref, cde_ref):
        i = pl.program_id(0)
        ti = ti_of(i)

        @pl.when(i == 0)
        def _():
            cdc_ref[...] = jnp.zeros_like(cdc_ref)
            cde_ref[...] = jnp.zeros_like(cde_ref)
            dsc_ref[...] = jnp.zeros_like(dsc_ref)
            dcw_ref[...] = jnp.zeros_like(dcw_ref)

        b = _branches(rest_ref, halo_ref, a_ref, wa_ref, wc_ref, wp_ref, sc_ref, cw_ref, ti, tm)
        sg, sc = b["sg"], sc_ref[...]
        y_pool = b["y_pool_raw"] * sc
        merged = sg[0] * b["y_attn"] + sg[1] * b["y_conv"] + sg[2] * y_pool
        mg_ref[...] = merged.astype(BF16)
        dm = _dot_nt(dx_ref[...].astype(BF16), wo_ref[...])
        dys = [dm * sg[j] for j in range(3)]
        for j, y in enumerate((b["y_attn"], b["y_conv"], y_pool)):
            dp_ref[:, 1024 + j * D_MODEL:1024 + (j + 1) * D_MODEL] = (dys[j] * y * (1.0 - sg[j])).astype(BF16)
        dya = dys[0].astype(BF16)
        dya_ref[...] = dya
        _, lo = _lanes()
        for j in range(HEADS // 2):
            at_ref[:, 128 * j:128 * (j + 1)] = b["a_tok"][j]
            da = _dot_nt(dya, wa_ref[128 * j:128 * (j + 1), :])
            da_ref[2 * j] = jnp.where(lo, da, 0.0).astype(BF16)
            da_ref[2 * j + 1] = jnp.where(lo, _swap_halves(da), 0.0).astype(BF16)
        dyc = dys[1].astype(BF16)
        dyc_ref[...] = dyc
        duc = _dot_nt(dyc, wc_ref[...])
        dyp = dys[2]
        dsc_ref[...] += jnp.sum(dyp * b["y_pool_raw"], axis=0, keepdims=True)
        dypr = (dyp * sc).astype(BF16)
        dyp_ref[...] = dypr
        ddp = _dot_nt(dypr, wp_ref[...])
        uc_ref[...] = b["uc"].astype(BF16)
        dd_ref[...] = b["dpool"].astype(BF16)

        dconv = duc * b["cb"]
        dp_ref[:, 256:512] = (duc * b["conv"]).astype(BF16)
        dcf = jnp.concatenate([dconv, cdc_ref[...]], axis=0)
        cw = b["cw"]
        dz = cw[2:3] * dconv + cw[1:2] * pltpu.roll(dcf, n - 1, 0)[:tm] + cw[0:1] * pltpu.roll(dcf, n - 2, 0)[:tm]
        dp_ref[:, 0:256] = (dz * b["cc"]).astype(BF16)
        dp_ref[:, 512:768] = (dz * b["cx"]).astype(BF16)
        dcw_ref[0:1, :] += jnp.sum(dconv * b["z2"], axis=0, keepdims=True)
        dcw_ref[1:2, :] += jnp.sum(dconv * b["z1"], axis=0, keepdims=True)
        dcw_ref[2:3, :] += jnp.sum(dconv * b["z"], axis=0, keepdims=True)
        cdc_ref[...] = dconv[:HALO]

        e = ddp * b["inv"]
        ef = jnp.concatenate([e, cde_ref[...]], axis=0)
        r2 = ef + pltpu.roll(ef, n - 1, 0)
        r4 = r2 + pltpu.roll(r2, n - 2, 0)
        r8 = r4 + pltpu.roll(r4, n - 4, 0)
        r16 = r8 + pltpu.roll(r8, n - 8, 0)
        dp_ref[:, 768:1024] = (_by_group(b["gid"], r2[:tm], r4[:tm], r8[:tm], r16[:tm]) - ddp).astype(BF16)
        cde_ref[...] = e[:HALO]

    tile = lambda w: pl.BlockSpec((tm, w), lambda i: (ti_of(i), 0))
    whole = lambda r, c: pl.BlockSpec((r, c), lambda i: (0, 0))
    bf = lambda w: jax.ShapeDtypeStruct((s, w), BF16)
    return pl.pallas_call(
        body, grid=(nt,),
        in_specs=_mix_specs(tm, ti_of) + [whole(D_MODEL, D_MODEL), tile(D_MODEL)],
        out_specs=[tile(N_REST), pl.BlockSpec((HEADS, tm, 128), lambda i: (0, ti_of(i), 0)), tile(D_ATTN),
                   tile(D_MODEL), tile(D_MODEL), tile(D_MODEL), tile(D_MODEL),
                   tile(D_CONV), tile(D_POOL), whole(1, D_MODEL), whole(8, D_CONV)],
        out_shape=[bf(DPROJ_COLS), jax.ShapeDtypeStruct((HEADS, s, 128), BF16), bf(D_ATTN),
                   bf(D_MODEL), bf(D_MODEL), bf(D_MODEL), bf(D_MODEL), bf(D_CONV), bf(D_POOL),
                   jax.ShapeDtypeStruct((1, D_MODEL), F32), jax.ShapeDtypeStruct((8, D_CONV), F32)],
        scratch_shapes=[pltpu.VMEM((HALO, D_CONV), F32), pltpu.VMEM((HALO, D_POOL), F32)],
        compiler_params=_cparams(("arbitrary",)), name=name)(proj, proj, a, wa, wc, wp, scale, cw, wo, dx1)


def _adamw_math(w, g, m, v):
    m = ADAM_B1 * m + (1.0 - ADAM_B1) * g
    v = ADAM_B2 * v + (1.0 - ADAM_B2) * (g * g)
    m_hat = m / (1.0 - ADAM_B1 ** ADAM_STEP)
    v_hat = v / (1.0 - ADAM_B2 ** ADAM_STEP)
    delta = -ADAM_LR * (m_hat / (jnp.sqrt(v_hat) + ADAM_EPS) + ADAM_WD * w)
    return delta, m, v


ADAMW_PARTS_BLOCK_BYTES = 4 * 2 ** 20


def _row_tile(rows, cols, copies, itemsize):
    row_bytes = copies * (-(-cols // 128) * 128) * itemsize
    fits = [t for t in range(16, rows + 1, 16) if rows % t == 0 and t * row_bytes <= ADAMW_PARTS_BLOCK_BYTES]
    return max(fits) if fits else rows


def pair_sum(blocks, stage, me, name):
    n_slots, rows, cols = stage.shape
    tr = _row_tile(rows, cols, 1, 4)

    def body(me_ref, a_ref, b_ref, o_ref):
        o_ref[...] = (a_ref[...].astype(F32) + b_ref[...].astype(F32)).astype(BF16)

    slot = pl.BlockSpec((None, tr, cols), lambda i, r, me_ref: (i, r, 0))
    return pl.pallas_call(
        body, out_shape=jax.ShapeDtypeStruct(stage.shape, BF16),
        grid_spec=pltpu.PrefetchScalarGridSpec(
            num_scalar_prefetch=1, grid=(n_slots, rows // tr),
            in_specs=[pl.BlockSpec((None, tr, cols), lambda i, r, me_ref: (me_ref[0] ^ (2 * i), r, 0)), slot],
            out_specs=slot),
        compiler_params=_cparams(("parallel", "parallel")), name=name)(me.reshape(1), blocks, stage)


def adamw_sum(parts, w, m, v, name):
    layers, rows, cols = w.shape
    n_parts = parts.shape[1]
    if rows % 16 == 0:
        tr, tc = _row_tile(rows, cols, n_parts, parts.dtype.itemsize), cols
    else:
        tr, tc = rows, _pick(cols, (256, 128))

    def body(p_ref, w_ref, m_ref, v_ref, g_ref, d_ref, nm_ref, nv_ref):
        g = p_ref[0].astype(F32)
        for i in range(1, n_parts):
            g = g + p_ref[i].astype(F32)
        g_ref[...] = g
        d_ref[...], nm_ref[...], nv_ref[...] = _adamw_math(w_ref[...], g, m_ref[...], v_ref[...])

    spec = pl.BlockSpec((None, tr, tc), lambda l, i, j: (l, i, j))
    return pl.pallas_call(
        body, grid=(layers, rows // tr, cols // tc),
        in_specs=[pl.BlockSpec((None, n_parts, tr, tc), lambda l, i, j: (l, 0, i, j)), spec, spec, spec],
        out_specs=[spec] * 4, out_shape=[jax.ShapeDtypeStruct((layers, rows, cols), F32)] * 4,
        compiler_params=_cparams(("parallel", "parallel", "parallel")), name=name)(parts, w, m, v)


def _me():
    return lax.axis_index("x"), lax.axis_index("y"), lax.axis_index("c")


N_PEERS = N_DEV - 1


def all_gather(shards, name):
    n = len(shards)
    any_spec = pl.BlockSpec(memory_space=pl.ANY)

    def body(*refs):
        x_refs, out_refs = refs[:n], refs[n:2 * n]
        send_sems, recv_sems, local_sems = refs[2 * n:]
        x, y, c = _me()
        me, sibling = (x, y, c), (x, y, 1 - c)
        chips = [(1 - x, y), (x, 1 - y), (1 - x, 1 - y)]

        def copy(t, k, block, to, from_input=False):
            slot = out_refs[t].at[4 * block[0] + 2 * block[1] + block[2]]
            return pltpu.make_async_remote_copy(
                src_ref=x_refs[t] if from_input else slot, dst_ref=slot, send_sem=send_sems.at[N_PEERS * t + k],
                recv_sem=recv_sems.at[N_PEERS * t + k], device_id=to, device_id_type=pl.DeviceIdType.MESH)

        mine = [pltpu.make_async_copy(x_refs[t], out_refs[t].at[4 * x + 2 * y + c], local_sems.at[t]) for t in range(n)]
        started = []
        for t in range(n):
            mine[t].start()
            started.append(copy(t, 0, me, sibling, from_input=True))
            started += [copy(t, 1 + j, me, (*chip, c), from_input=True) for j, chip in enumerate(chips)]
        for cp in started:
            cp.start()
        for j, chip in enumerate(chips):
            for t in range(n):
                copy(t, 1 + j, (*chip, c), me).wait_recv()
                fwd = copy(t, 4 + j, (*chip, c), sibling)
                fwd.start()
                started.append(fwd)
        for t in range(n):
            copy(t, 0, sibling, me).wait_recv()
            for j, chip in enumerate(chips):
                copy(t, 4 + j, (*chip, 1 - c), me).wait_recv()
        for cp in started:
            cp.wait_send()
        for cp in mine:
            cp.wait()

    return pl.pallas_call(
        body, out_shape=[jax.ShapeDtypeStruct((N_DEV,) + s.shape, s.dtype) for s in shards],
        in_specs=[any_spec] * n, out_specs=[any_spec] * n,
        scratch_shapes=[pltpu.SemaphoreType.DMA((N_PEERS * n,)), pltpu.SemaphoreType.DMA((N_PEERS * n,)),
                        pltpu.SemaphoreType.DMA((n,))],
        name=name)(*shards)


SIBLING = 1
OTHER_CHIPS = (2, 4, 6)
SAME_CORE = (0,) + OTHER_CHIPS


class Exchange:
    def __init__(self, inputs, out_shapes, aliases, copies, local=()):
        self.inputs, self.out_shapes, self.aliases = list(inputs), list(out_shapes), aliases
        self._copies, self._local = list(copies), list(local)
        self.scratch = [pltpu.SemaphoreType.DMA((len(self._copies),)), pltpu.SemaphoreType.DMA((len(self._copies),)),
                        pltpu.SemaphoreType.DMA((max(len(self._local), 1),))]

    def _build(self, ins, outs, sems):
        send_sems, recv_sems, local_sems = sems
        x, y, c = _me()
        me = 4 * x + 2 * y + c
        local = [functools.partial(pltpu.make_async_copy, src(ins, outs, me), dst(outs, me), local_sems.at[i])
                 for i, (src, dst) in enumerate(self._local)]
        sends, recvs = [], []
        for i, (mask, src, dst) in enumerate(self._copies):
            px, py, pc = x ^ ((mask >> 2) & 1), y ^ ((mask >> 1) & 1), c ^ (mask & 1)
            pair = dict(send_sem=send_sems.at[i], recv_sem=recv_sems.at[i], device_id_type=pl.DeviceIdType.MESH)
            sends.append(functools.partial(
                pltpu.make_async_remote_copy, src_ref=src(ins, outs, me), dst_ref=dst(outs, me), device_id=(px, py, pc), **pair))
            recvs.append(functools.partial(
                pltpu.make_async_remote_copy, src_ref=src(ins, outs, me), dst_ref=dst(outs, me ^ mask), device_id=(x, y, c), **pair))
        return local, sends, recvs

    def start(self, ins, outs, sems):
        local, sends, _ = self._build(ins, outs, sems)
        for make in local + sends:
            make().start()

    def drain(self, ins, outs, sems):
        local, sends, recvs = self._build(ins, outs, sems)
        for make in recvs:
            make().wait_recv()
        for make in sends:
            make().wait_send()
        for make in local:
            make().wait()


def _bind(fn, *args):
    return functools.partial(fn, *args)


def join_exchanges(a, b):
    if a is None or b is None:
        return a or b
    na_in, na_out = len(a.inputs), len(a.out_shapes)

    def src_a(fn):
        return lambda ins, outs, me: fn(ins[:na_in], outs[:na_out], me)

    def dst_a(fn):
        return lambda outs, who: fn(outs[:na_out], who)

    def src_b(fn):
        return lambda ins, outs, me: fn(ins[na_in:], outs[na_out:], me)

    def dst_b(fn):
        return lambda outs, who: fn(outs[na_out:], who)

    copies = [(m, src_a(s), dst_a(d)) for m, s, d in a._copies] + [(m, src_b(s), dst_b(d)) for m, s, d in b._copies]
    local = [(src_a(s), dst_a(d)) for s, d in a._local] + [(src_b(s), dst_b(d)) for s, d in b._local]
    aliases = dict(a.aliases)
    aliases.update({na_in + i: na_out + o for i, o in b.aliases.items()})
    return Exchange(a.inputs + b.inputs, a.out_shapes + b.out_shapes, aliases, copies, local)


def gather_over_ici(shards):
    copies = [(mask, _bind(lambda t, ins, outs, me: ins[t], t), _bind(lambda t, outs, sender: outs[t].at[sender], t))
              for t in range(len(shards)) for mask in OTHER_CHIPS]
    local = [(_bind(lambda t, ins, outs, me: ins[t], t), _bind(lambda t, outs, me: outs[t].at[me], t))
             for t in range(len(shards))]
    return Exchange(shards, [jax.ShapeDtypeStruct((N_DEV,) + s.shape, s.dtype) for s in shards], {}, copies, local)


def gather_over_d2d(gathered):
    copies = [(SIBLING, _bind(lambda t, m, ins, outs, me: outs[t].at[me ^ m], t, m),
               _bind(lambda t, m, outs, sender: outs[t].at[sender ^ m], t, m))
              for t in range(len(gathered)) for m in SAME_CORE]
    return Exchange(gathered, [jax.ShapeDtypeStruct(g.shape, g.dtype) for g in gathered],
                    {t: t for t in range(len(gathered))}, copies)


def scatter_over_d2d(blocks):
    copies = [(SIBLING, _bind(lambda t, m, ins, outs, me: ins[t].at[me ^ SIBLING ^ m], t, m),
               _bind(lambda t, i, outs, sender: outs[t].at[i], t, i))
              for t in range(len(blocks)) for i, m in enumerate(SAME_CORE)]
    return Exchange(blocks, [jax.ShapeDtypeStruct((len(SAME_CORE),) + b.shape[1:], b.dtype) for b in blocks], {}, copies)


def scatter_over_ici(pair_sums, bufs, layer):
    n = len(pair_sums)
    copies = [(m, _bind(lambda t, i, ins, outs, me: ins[t].at[i], t, i),
               _bind(lambda t, i, outs, sender: outs[t].at[layer, i], t, i))
              for t in range(n) for i, m in enumerate(SAME_CORE) if m]
    local = [(_bind(lambda t, ins, outs, me: ins[t].at[0], t), _bind(lambda t, outs, me: outs[t].at[layer, 0], t))
             for t in range(n)]
    return Exchange(list(pair_sums) + list(bufs), [jax.ShapeDtypeStruct(b.shape, b.dtype) for b in bufs],
                    {n + t: t for t in range(n)}, copies, local)


def run_exchange(ex, name):
    any_spec = pl.BlockSpec(memory_space=pl.ANY)
    n_in, n_out = len(ex.inputs), len(ex.out_shapes)

    def body(*refs):
        ins, outs, sems = refs[:n_in], refs[n_in:n_in + n_out], refs[n_in + n_out:]
        ex.start(ins, outs, sems)
        ex.drain(ins, outs, sems)

    return pl.pallas_call(
        body, out_shape=ex.out_shapes, in_specs=[any_spec] * n_in, out_specs=[any_spec] * n_out,
        input_output_aliases=ex.aliases, scratch_shapes=ex.scratch, name=name)(*ex.inputs)


MATRICES = ("w_in", "w_attn_out", "w_conv_out", "pool_w", "w_o", "w_ffn_in", "w_ffn_out")
TRANSPOSED = ("w_in", "w_ffn_in")
EVERY = tuple(range(len(MATRICES)))
IN_PROJ_PART, ATTN_PART, MIX_PART = (0,), (1, 2, 3, 4, 5), (6,)
LATE = (0,)
EARLY = EVERY[1:]
EARLY_FIRST, EARLY_SECOND = (4, 6), (1, 2, 3, 5)
SHARD_INFO = {
    "w_in": ((DEPTH, D_IN // N_DEV, D_MODEL), 1),
    "w_attn_out": ((DEPTH, D_ATTN, D_MODEL // N_DEV), 2),
    "w_conv_out": ((DEPTH, D_CONV, D_MODEL // N_DEV), 2),
    "pool_w": ((DEPTH, 4, 64, 256 // N_DEV), 3),
    "w_o": ((DEPTH, D_MODEL // N_DEV, D_MODEL), 1),
    "w_ffn_in": ((DEPTH, 2 * D_FF // N_DEV, D_MODEL), 1),
    "w_ffn_out": ((DEPTH, D_FF // N_DEV, D_MODEL), 1),
}


def _handled(name, t):
    return jnp.transpose(t, (0, 2, 1)) if name in TRANSPOSED else t
VECTORS = ("norm_mix_g", "forget_b", "q_norm_g", "k_norm_g", "pool_scale", "norm_ffn_g")
VECTOR_SHAPES = {"norm_mix_g": (DEPTH, D_MODEL), "forget_b": (DEPTH, HEADS), "q_norm_g": (DEPTH, HEAD_DIM),
                 "k_norm_g": (DEPTH, HEAD_DIM), "pool_scale": (DEPTH, D_MODEL), "norm_ffn_g": (DEPTH, D_MODEL)}
CONV_W_FULL = (DEPTH, 3, D_CONV)


def _size(shape):
    n = 1
    for v in shape:
        n *= v
    return n


def _pack(arrays, rows, cols):
    flat = jnp.concatenate([a.reshape(-1) for a in arrays])
    return jnp.pad(flat, (0, rows * cols - flat.shape[0])).reshape(rows, cols)


def _unpack(packed, shapes):
    flat, out, off = packed.reshape(-1), [], 0
    for shp in shapes:
        out.append(flat[off:off + _size(shp)].reshape(shp))
        off += _size(shp)
    return out


def _join_shards(stacked, axis):
    moved = jnp.moveaxis(stacked, 0, axis)
    shp = list(moved.shape)
    shp[axis:axis + 2] = [shp[axis] * shp[axis + 1]]
    return moved.reshape(shp)


def _cut_shards(full, axis):
    shp = list(full.shape)
    shp[axis:axis + 1] = [N_DEV, shp[axis] // N_DEV]
    return jnp.moveaxis(full.reshape(shp), axis, 0)


N_MOVED = 1544
SHARD_ROWS = D_IN // N_DEV


def _regroup_w_in(shards):
    wt = shards.reshape(D_IN, shards.shape[2])
    pad = jnp.zeros((N_FULL - D_IN, wt.shape[1]), wt.dtype)
    return jnp.concatenate([wt[N_MOVED:], wt[:N_MOVED], pad], axis=0)


def _ungroup_w_in(wpt):
    def kernel_rows(a, b):
        if b <= N_MOVED:
            return [wpt[a + D_IN - N_MOVED:b + D_IN - N_MOVED]]
        if a >= N_MOVED:
            return [wpt[a - N_MOVED:b - N_MOVED]]
        return kernel_rows(a, N_MOVED) + kernel_rows(N_MOVED, b)

    return jnp.stack([jnp.concatenate(kernel_rows(s * SHARD_ROWS, (s + 1) * SHARD_ROWS), axis=0) for s in range(N_DEV)])


def _pool_block_diag(w):
    out = jnp.zeros((D_POOL, D_MODEL), w.dtype)
    for g in range(4):
        out = lax.dynamic_update_slice(out, w[g], (g * 64, g * 256))
    return out


def _pool_from_block_diag(wbd):
    return jnp.stack([wbd[g * 64:(g + 1) * 64, g * 256:(g + 1) * 256] for g in range(4)])


def _layer_weights(mats, vec, conv_w, l):
    wp = _pool_block_diag(mats["pool_w"])
    row = lambda v: v.reshape(1, -1)
    fb = jnp.zeros((1, 128), F32).at[0, :HEADS].set(vec["forget_b"][l])
    cw = jnp.zeros((8, D_CONV), F32).at[:3].set(conv_w[l])
    twice = lambda v: jnp.tile(v.reshape(1, -1), (1, 2))
    return dict(
        wt_in=_regroup_w_in(mats["w_in"]), wt_ffn_in=mats["w_ffn_in"], w_ffn_out=mats["w_ffn_out"],
        wa=mats["w_attn_out"], wc=mats["w_conv_out"], wp=wp, wo=mats["w_o"],
        g_mix=row(vec["norm_mix_g"][l]), g_ffn=row(vec["norm_ffn_g"][l]), gq2=twice(vec["q_norm_g"][l]),
        gk2=twice(vec["k_norm_g"][l]), scale=row(vec["pool_scale"][l]), fb=fb, cw=cw)


def _layer_fwd(x, w, l, comm):
    (proj, h), half_a = norm_matmul(x, w["g_mix"], w["wt_in"], N_MAIN, f"in_proj_{l}", comm.gather_ici(l + 1, IN_PROJ_PART))
    z, c = forget_fwd(h, w["wt_in"], w["fb"], f"forget_fwd_{l}")
    qa, ka, va, vt = attn_prep(proj, c, w["gq2"], w["gk2"], f"attn_prep_{l}")
    (oa, lse), half_b = attn_forward(qa, ka, vt, f"attn_fwd_{l}", comm.gather_ici(l + 1, ATTN_PART))
    x1, half_c = mix_fwd(proj, oa, x, w["wa"], w["wc"], w["wp"], w["scale"], w["cw"], w["wo"], f"mix_fwd_{l}",
                         comm.gather_ici(l + 1, MIX_PART))
    half = list(half_a) + list(half_b) + list(half_c)
    (gu, h2), gathered = norm_matmul(x1, w["g_ffn"], w["wt_ffn_in"], 2 * D_FF, f"ffn_in_{l}", comm.gather_d2d(l + 1, half))
    x2 = swiglu_matmul(gu, w["w_ffn_out"], x1, f"ffn_out_{l}")
    saved = dict(x=x, proj=proj, h=h, z=z, qa=qa, ka=ka, va=va, oa=oa, lse=lse, x1=x1, gu=gu, h2=h2)
    return x2, saved, gathered


def _layer_bwd(dx2, sv, w, l, comm):
    g = {}
    (dgu, act), stage = swiglu_bwd(dx2, sv["gu"], w["w_ffn_out"], f"ffn_out_bwd_{l}", comm.scatter_d2d(l + 1))
    sums = comm.pair_sums(l + 1, stage)
    g["w_ffn_out"] = tn_matmul(act, dx2, f"dw_ffn_out_{l}")
    g["w_ffn_in"] = tn_matmul(dgu, sv["h2"], f"dw_ffn_in_{l}")
    (dx1, dg), _ = matmul_normbwd(dgu, w["wt_ffn_in"], sv["x1"], w["g_ffn"], dx2, f"ffn_in_bwd_{l}")
    g["norm_ffn_g"] = dg[0]

    (dproj, doa, a_tok, merged, dya, dyc, dyp, uc, dd, dscale, dcw) = mix_bwd(
        sv["proj"], sv["oa"], dx1, w["wa"], w["wc"], w["wp"], w["scale"], w["cw"], w["wo"], f"mix_bwd_{l}")
    g["w_o"] = tn_matmul(merged, dx1, f"dw_o_{l}")
    g["w_attn_out"], g["w_conv_out"], dwp = tn_matmuls([(a_tok, dya), (uc, dyc), (dd, dyp)], f"dw_branches_{l}")
    g["pool_w"] = _pool_from_block_diag(dwp)
    g["pool_scale"] = dscale[0]
    g["conv_w"] = dcw[:3]

    early = comm.early(l)
    comm.grads(l, g)
    above = comm.scatter_ici(l + 1, sums)
    (dqa, dka, dva), got = attn_backward(sv["qa"], sv["ka"], sv["va"], sv["oa"], doa, sv["lse"], f"attn_bwd_{l}",
                                         join_exchanges(above, comm.scatter_d2d(l, early) if early else None))
    n_above = len(above.out_shapes) if above else 0
    comm.scattered(got[:n_above])
    early_sums = dict(zip(early, comm.pair_sums(l, got[n_above:], early))) if early else {}
    early_ici = lambda which: comm.scatter_ici(l, [early_sums[t] for t in which], which) if early else None
    dproj, dc, dgq, dgk = attn_post(dqa, dka, dva, sv["proj"], w["gq2"], w["gk2"], dproj, f"attn_post_{l}")
    g["q_norm_g"] = dgq[0, :HEAD_DIM] + dgq[0, HEAD_DIM:]
    g["k_norm_g"] = dgk[0, :HEAD_DIM] + dgk[0, HEAD_DIM:]
    dproj, db = forget_bwd(dc, sv["z"], dproj, f"forget_bwd_{l}")
    g["forget_b"] = db[0, :HEADS]

    dw_in = tn_matmul(dproj, sv["h"], f"dw_in_{l}", m_cols=N_FULL, ex=early_ici(EARLY_FIRST))
    if early:
        dw_in, got = dw_in
        comm.scattered(got, EARLY_FIRST)
    g["w_in"] = _ungroup_w_in(dw_in)
    (dx, dg), got = matmul_normbwd(dproj, w["wt_in"], sv["x"], w["g_mix"], dx1, f"in_proj_bwd_{l}", k=N_FULL,
                                   ex=early_ici(EARLY_SECOND))
    comm.scattered(got, EARLY_SECOND if early else None)
    g["norm_mix_g"] = dg[0]
    comm.grads(l, g)
    return dx


def _local_step(x, tgt, comm):
    ws, saved = [], []
    w = comm.weights(0, None)
    for l in range(DEPTH):
        ws.append(w)
        x, sv, gathered = _layer_fwd(x, w, l, comm)
        saved.append(sv)
        if l + 1 < DEPTH:
            w = comm.weights(l + 1, gathered)
    sq, dx = loss_kernel(x, tgt, "loss")
    for l in reversed(range(DEPTH)):
        dx = _layer_bwd(dx, saved[l], ws[l], l, comm)
    comm.finish()
    return sq[0, 0], dx


def kernel(x, norm_mix_g, w_in, forget_b, q_norm_g, k_norm_g, w_attn_out, conv_w, w_conv_out, pool_w, pool_scale, w_o, norm_ffn_g, w_ffn_in, w_ffn_out, loss_target, m_norm_mix_g, m_w_in, m_forget_b, m_q_norm_g, m_k_norm_g, m_w_attn_out, m_conv_w, m_w_conv_out, m_pool_w, m_pool_scale, m_w_o, m_norm_ffn_g, m_w_ffn_in, m_w_ffn_out, v_norm_mix_g, v_w_in, v_forget_b, v_q_norm_g, v_k_norm_g, v_w_attn_out, v_conv_w, v_w_conv_out, v_pool_w, v_pool_scale, v_w_o, v_norm_ffn_g, v_w_ffn_in, v_w_ffn_out):
    w = dict(norm_mix_g=norm_mix_g, w_in=w_in, forget_b=forget_b, q_norm_g=q_norm_g, k_norm_g=k_norm_g,
             w_attn_out=w_attn_out, conv_w=conv_w, w_conv_out=w_conv_out, pool_w=pool_w, pool_scale=pool_scale,
             w_o=w_o, norm_ffn_g=norm_ffn_g, w_ffn_in=w_ffn_in, w_ffn_out=w_ffn_out)
    m = dict(norm_mix_g=m_norm_mix_g, w_in=m_w_in, forget_b=m_forget_b, q_norm_g=m_q_norm_g, k_norm_g=m_k_norm_g,
             w_attn_out=m_w_attn_out, conv_w=m_conv_w, w_conv_out=m_w_conv_out, pool_w=m_pool_w,
             pool_scale=m_pool_scale, w_o=m_w_o, norm_ffn_g=m_norm_ffn_g, w_ffn_in=m_w_ffn_in, w_ffn_out=m_w_ffn_out)
    v = dict(norm_mix_g=v_norm_mix_g, w_in=v_w_in, forget_b=v_forget_b, q_norm_g=v_q_norm_g, k_norm_g=v_k_norm_g,
             w_attn_out=v_w_attn_out, conv_w=v_conv_w, w_conv_out=v_w_conv_out, pool_w=v_pool_w,
             pool_scale=v_pool_scale, w_o=v_w_o, norm_ffn_g=v_norm_ffn_g, w_ffn_in=v_w_ffn_in, w_ffn_out=v_w_ffn_out)
    me = 4 * lax.axis_index("x") + 2 * lax.axis_index("y") + lax.axis_index("c")
    layer_shard = {n: SHARD_INFO[n][0][1:] for n in MATRICES}
    cut_axis = {n: SHARD_INFO[n][1] - 1 for n in MATRICES}

    vec = {n: w[n] for n in VECTORS}
    rc = {n: (_size(layer_shard[n][:-1]), layer_shard[n][-1]) for n in MATRICES}

    class Comm:
        bufs = [lax.empty((DEPTH, len(SAME_CORE)) + layer_shard[n], BF16) for n in MATRICES]
        blocks = [None] * DEPTH
        small_g = [None] * DEPTH
        conv_full = None

        @staticmethod
        def shards(l):
            return [_handled(n, w[n])[l].astype(BF16) for n in MATRICES]

        @staticmethod
        def gather_ici(l, part):
            return gather_over_ici([Comm.shards(l)[t] for t in part]) if l < DEPTH else None

        @staticmethod
        def gather_d2d(l, half):
            return gather_over_d2d(half) if l < DEPTH else None

        @staticmethod
        def weights(l, gathered):
            if l == 0:
                *gathered, conv_g = all_gather(Comm.shards(0) + [_pack([conv_w], 8, 128)], "gather_0")
                Comm.conv_full = _join_shards(jnp.stack([_unpack(conv_g[i], [conv_w.shape])[0] for i in range(N_DEV)]), 2)
            mats = {n: t if n == "w_in" else _join_shards(t, cut_axis[n]) for n, t in zip(MATRICES, gathered)}
            return _layer_weights(mats, vec, Comm.conv_full, l)

        @staticmethod
        def grads(l, g):
            Comm.small_g[l] = g
            Comm.blocks[l] = [None if n not in g else g[n] if n == "w_in" else _cut_shards(g[n], cut_axis[n])
                              for n in MATRICES]

        @staticmethod
        def early(l):
            return EARLY if l == 0 else None

        @staticmethod
        def scatter_d2d(l, which=EVERY):
            return scatter_over_d2d([Comm.blocks[l][t] for t in which]) if l < DEPTH else None

        @staticmethod
        def pair_sums(l, stage, which=EVERY):
            if l >= DEPTH:
                return None
            return [pair_sum(Comm.blocks[l][t].reshape((N_DEV,) + rc[MATRICES[t]]),
                             s.reshape((len(SAME_CORE),) + rc[MATRICES[t]]), me,
                             f"pair_sum_{MATRICES[t]}_{l}").reshape(s.shape) for t, s in zip(which, stage)]

        @staticmethod
        def scatter_ici(l, sums, which=EVERY):
            return scatter_over_ici(sums, [Comm.bufs[t] for t in which], l) if l < DEPTH else None

        @staticmethod
        def scattered(results, which=EVERY):
            for t, r in zip(which or (), results):
                Comm.bufs[t] = r

        @staticmethod
        def finish():
            stage = run_exchange(Comm.scatter_d2d(0, LATE), "scatter_d2d_0")
            Comm.scattered(run_exchange(Comm.scatter_ici(0, Comm.pair_sums(0, stage, LATE), LATE), "scatter_ici_0"), LATE)

    small_g, received = Comm.small_g, Comm
    sq, dx = _local_step(x[0], loss_target[0], Comm)

    big = {}
    for n, parts in zip(MATRICES, received.bufs):
        outs = adamw_sum(parts.reshape((DEPTH, len(SAME_CORE)) + rc[n]),
                         *[_handled(n, d[n]).reshape((DEPTH,) + rc[n]) for d in (w, m, v)], f"adamw_{n}")
        big[n] = [_handled(n, t.reshape((DEPTH,) + layer_shard[n])) for t in outs]

    small_shapes = [VECTOR_SHAPES[n] for n in VECTORS] + [CONV_W_FULL, (1,)]
    stacked = [jnp.stack([small_g[l][n] for l in range(DEPTH)]) for n in VECTORS + ("conv_w",)] + [sq.reshape(1)]
    sparts = all_gather([_pack(stacked, SMALL_ROWS, 128)], "gather_vector_grads")[0]
    col0 = me * (D_CONV // N_DEV)
    place = lambda t: lax.dynamic_update_slice(jnp.zeros(CONV_W_FULL, F32), t, (0, 0, col0))
    spacked = [_pack([d[n] for n in VECTORS] + [place(d["conv_w"]), jnp.zeros((1,), F32)], SMALL_ROWS, 128)[None]
               for d in (w, m, v)]
    small = [_unpack(t[0], small_shapes) for t in adamw_sum(sparts[None], *spacked, "adamw_vectors")]
    loss = (0.5 / D_MODEL) * small[0][-1][0]

    def result(kind):
        out = {n: big[n][kind] for n in MATRICES}
        out.update({n: small[kind][j] for j, n in enumerate(VECTORS)})
        out["conv_w"] = lax.dynamic_slice(small[kind][len(VECTORS)], (0, 0, col0), conv_w.shape)
        return [out[n] for n in w]

    return (loss, dx[None], *result(0), *result(1), *result(2), *result(3))
```

```python
import functools

import jax
import jax.numpy as jnp
from jax import lax
from jax.experimental import pallas as pl
from jax.experimental.pallas import tpu as pltpu

F32 = jnp.float32
BF16 = jnp.bfloat16

N_DEV = 8
DEPTH = 4
D_MODEL = 1024
HEAD_DIM = 64
HEADS = 8
D_ATTN = 512
D_CONV = 256
D_POOL = 256
D_FF = 2816
D_IN = 5640
EPS = 1e-6
ATTN_SCALE = HEAD_DIM ** -0.5

N_REST = 4096
N_MAIN = 5632
N_FULL = 5760
DPROJ_TAIL = 2048
DPROJ_COLS = N_REST + DPROJ_TAIL
FF_BLK = 256
N_FF_BLKS = D_FF // FF_BLK
HALO = 16

ADAM_LR = 0.001
ADAM_B1 = 0.9
ADAM_B2 = 0.999
ADAM_EPS = 1e-08
ADAM_WD = 0.01
ADAM_STEP = 10

SMALL_ROWS = 128

VMEM_LIMIT = 48 * 2 ** 20


def _cparams(sem, vmem=None):
    return pltpu.CompilerParams(dimension_semantics=sem, vmem_limit_bytes=vmem or VMEM_LIMIT)


def _pick(n, cands):
    for c in cands:
        if n % c == 0:
            return c
    raise ValueError(f"no tile for {n}")


def _tile(n, cap):
    t = min(cap, n)
    assert n % t == 0, (n, cap)
    return t


def _sigmoid(v):
    return 1.0 / (1.0 + jnp.exp(-v))


def _rstd(v):
    return lax.rsqrt(jnp.mean(v * v, axis=-1, keepdims=True) + EPS)


def _dot(a, b):
    return jnp.dot(a, b, preferred_element_type=F32)


def _dot_tn(a, b):
    return lax.dot_general(a, b, (((0,), (0,)), ((), ())), preferred_element_type=F32)


def _dot_nt(a, b):
    return lax.dot_general(a, b, (((1,), (1,)), ((), ())), preferred_element_type=F32)


def norm_matmul(x, g, wt, n_cols, name, ex=None):
    s, d = x.shape
    tm, tn = _tile(s, 1024), _pick(n_cols, (2816, 1408, 512))

    def body(x_ref, g_ref, w_ref, o_ref, h_ref):
        @pl.when(pl.program_id(1) == 0)
        def _():
            xv = x_ref[...]
            h_ref[...] = (xv * _rstd(xv) * g_ref[...]).astype(BF16)

        o_ref[...] = _dot_nt(h_ref[...], w_ref[...]).astype(BF16)

    return _carried_call(
        body, ex, (s // tm, n_cols // tn),
        [pl.BlockSpec((tm, d), lambda i, j: (i, 0)), pl.BlockSpec((1, d), lambda i, j: (0, 0)),
         pl.BlockSpec((tn, d), lambda i, j: (j, 0))],
        [pl.BlockSpec((tm, tn), lambda i, j: (i, j)), pl.BlockSpec((tm, d), lambda i, j: (i, 0))],
        [jax.ShapeDtypeStruct((s, n_cols), BF16), jax.ShapeDtypeStruct((s, d), BF16)], [],
        ("arbitrary", "arbitrary"), name, (x, g, wt))


def tn_matmul(a, b, name, m_cols=None, ex=None):
    t = a.shape[0]
    m = m_cols or a.shape[1]
    n = b.shape[1]
    tk = _tile(t, 1024)
    tmm = _pick(m, (1408, 1152, 1024, 512, 256))
    tn = _pick(n, (1408, 1152, 1024, 512, 128))
    nk = t // tk

    def body(a_ref, b_ref, o_ref, acc_ref):
        @pl.when(pl.program_id(2) == 0)
        def _():
            acc_ref[...] = jnp.zeros_like(acc_ref)

        acc_ref[...] += _dot_tn(a_ref[...].astype(BF16), b_ref[...].astype(BF16))

        @pl.when(pl.program_id(2) == nk - 1)
        def _():
            o_ref[...] = acc_ref[...].astype(BF16)

    if ex is None:
        return pl.pallas_call(
            body, grid=(m // tmm, n // tn, nk),
            in_specs=[pl.BlockSpec((tk, tmm), lambda i, j, k: (k, i)), pl.BlockSpec((tk, tn), lambda i, j, k: (k, j))],
            out_specs=pl.BlockSpec((tmm, tn), lambda i, j, k: (i, j)),
            out_shape=jax.ShapeDtypeStruct((m, n), BF16), scratch_shapes=[pltpu.VMEM((tmm, tn), F32)],
            compiler_params=_cparams(("parallel", "parallel", "arbitrary")), name=name)(a, b)
    (out,), carried = _carried_call(
        body, ex, (m // tmm, n // tn, nk),
        [pl.BlockSpec((tk, tmm), lambda i, j, k: (k, i)), pl.BlockSpec((tk, tn), lambda i, j, k: (k, j))],
        [pl.BlockSpec((tmm, tn), lambda i, j, k: (i, j))], [jax.ShapeDtypeStruct((m, n), BF16)],
        [pltpu.VMEM((tmm, tn), F32)], ("arbitrary", "arbitrary", "arbitrary"), name, (a, b))
    return out, carried


def tn_matmuls(pairs, name):
    t = pairs[0][0].shape[0]
    tk = _tile(t, 1024)
    nk = t // tk
    n = len(pairs)

    def body(*refs):
        ins, outs, accs = refs[:2 * n], refs[2 * n:3 * n], refs[3 * n:]

        @pl.when(pl.program_id(0) == 0)
        def _():
            for acc in accs:
                acc[...] = jnp.zeros_like(acc)

        for i in range(n):
            accs[i][...] += _dot_tn(ins[2 * i][...], ins[2 * i + 1][...])

        @pl.when(pl.program_id(0) == nk - 1)
        def _():
            for out, acc in zip(outs, accs):
                out[...] = acc[...].astype(BF16)

    shapes = [(a.shape[1], b.shape[1]) for a, b in pairs]
    return pl.pallas_call(
        body, grid=(nk,),
        in_specs=[pl.BlockSpec((tk, t_.shape[1]), lambda k: (k, 0)) for pair in pairs for t_ in pair],
        out_specs=[pl.BlockSpec(shp, lambda k: (0, 0)) for shp in shapes],
        out_shape=[jax.ShapeDtypeStruct(shp, BF16) for shp in shapes],
        scratch_shapes=[pltpu.VMEM(shp, F32) for shp in shapes],
        compiler_params=_cparams(("arbitrary",)), name=name)(*[t_ for pair in pairs for t_ in pair])


def matmul_normbwd(a, wt, x, g, dres, name, k=None, ex=None):
    s = a.shape[0]
    k = k or a.shape[1]
    d = wt.shape[1]
    tm = _tile(s, 1024)
    tk = _pick(k, (1408, 1152, 512))
    nk = k // tk

    def body(a_ref, w_ref, x_ref, g_ref, r_ref, dx_ref, dg_ref, acc_ref):
        i, kk = pl.program_id(0), pl.program_id(1)

        @pl.when(kk == 0)
        def _():
            acc_ref[...] = jnp.zeros_like(acc_ref)

        @pl.when((i == 0) & (kk == 0))
        def _():
            dg_ref[...] = jnp.zeros_like(dg_ref)

        acc_ref[...] += _dot(a_ref[...], w_ref[...])

        @pl.when(kk == nk - 1)
        def _():
            xv = x_ref[...]
            r = _rstd(xv)
            y = xv * r
            dh = acc_ref[...]
            dy = dh * g_ref[...]
            dx_ref[...] = r_ref[...] + r * (dy - y * jnp.mean(dy * y, axis=-1, keepdims=True))
            dg_ref[...] += jnp.sum(dh * y, axis=0, keepdims=True)

    return _carried_call(
        body, ex, (s // tm, nk),
        [pl.BlockSpec((tm, tk), lambda i, kk: (i, kk)), pl.BlockSpec((tk, d), lambda i, kk: (kk, 0)),
         pl.BlockSpec((tm, d), lambda i, kk: (i, 0)), pl.BlockSpec((1, d), lambda i, kk: (0, 0)),
         pl.BlockSpec((tm, d), lambda i, kk: (i, 0))],
        [pl.BlockSpec((tm, d), lambda i, kk: (i, 0)), pl.BlockSpec((1, d), lambda i, kk: (0, 0))],
        [jax.ShapeDtypeStruct((s, d), F32), jax.ShapeDtypeStruct((1, d), F32)],
        [pltpu.VMEM((tm, d), F32)], ("arbitrary", "arbitrary"), name, (a, wt, x, g, dres), vmem=56 * 2 ** 20)


def swiglu_matmul(gu, w, x1, name):
    s = gu.shape[0]
    d = w.shape[1]
    tm = _tile(s, 512)

    def body(gu_ref, w_ref, x_ref, o_ref):
        acc = x_ref[...]
        for j in range(N_FF_BLKS):
            gt = gu_ref[:, j * FF_BLK:(j + 1) * FF_BLK].astype(F32)
            up = gu_ref[:, D_FF + j * FF_BLK:D_FF + (j + 1) * FF_BLK].astype(F32)
            act = (gt * _sigmoid(gt) * up).astype(BF16)
            acc += _dot(act, w_ref[j * FF_BLK:(j + 1) * FF_BLK, :])
        o_ref[...] = acc

    return pl.pallas_call(
        body, grid=(s // tm,),
        in_specs=[pl.BlockSpec((tm, 2 * D_FF), lambda i: (i, 0)), pl.BlockSpec((D_FF, d), lambda i: (0, 0)),
                  pl.BlockSpec((tm, d), lambda i: (i, 0))],
        out_specs=pl.BlockSpec((tm, d), lambda i: (i, 0)),
        out_shape=jax.ShapeDtypeStruct((s, d), F32),
        compiler_params=_cparams(("parallel",)), name=name)(gu, w, x1)


def swiglu_bwd(dx2, gu, w, name, ex=None):
    s, d = dx2.shape
    tm = _tile(s, 512)

    def body(dx_ref, gu_ref, w_ref, dgu_ref, act_ref):
        dx = dx_ref[...].astype(BF16)
        for j in range(N_FF_BLKS):
            g_cols = slice(j * FF_BLK, (j + 1) * FF_BLK)
            u_cols = slice(D_FF + j * FF_BLK, D_FF + (j + 1) * FF_BLK)
            dact = _dot_nt(dx, w_ref[j * FF_BLK:(j + 1) * FF_BLK, :])
            gt = gu_ref[:, g_cols].astype(F32)
            up = gu_ref[:, u_cols].astype(F32)
            sg = _sigmoid(gt)
            silu = gt * sg
            act_ref[:, j * FF_BLK:(j + 1) * FF_BLK] = (silu * up).astype(BF16)
            dgu_ref[:, g_cols] = (dact * up * (sg + silu * (1.0 - sg))).astype(BF16)
            dgu_ref[:, u_cols] = (dact * silu).astype(BF16)

    return _carried_call(
        body, ex, (s // tm,),
        [pl.BlockSpec((tm, d), lambda i: (i, 0)), pl.BlockSpec((tm, 2 * D_FF), lambda i: (i, 0)),
         pl.BlockSpec((D_FF, d), lambda i: (0, 0), pipeline_mode=pl.Buffered(1))],
        [pl.BlockSpec((tm, 2 * D_FF), lambda i: (i, 0)), pl.BlockSpec((tm, D_FF), lambda i: (i, 0))],
        [jax.ShapeDtypeStruct((s, 2 * D_FF), BF16), jax.ShapeDtypeStruct((s, D_FF), BF16)], [],
        ("arbitrary",), name, (dx2, gu, w), vmem=56 * 2 ** 20)


def loss_kernel(y, tgt, name):
    s, d = y.shape
    tm = _tile(s, 512)

    def body(y_ref, t_ref, l_ref, dy_ref):
        @pl.when(pl.program_id(0) == 0)
        def _():
            l_ref[...] = jnp.zeros_like(l_ref)

        err = y_ref[...] - t_ref[...]
        dy_ref[...] = err * (1.0 / d)
        l_ref[...] += jnp.sum(jnp.sum(err * err, axis=1, keepdims=True), axis=0, keepdims=True)

    return pl.pallas_call(
        body, grid=(s // tm,),
        in_specs=[pl.BlockSpec((tm, d), lambda i: (i, 0)), pl.BlockSpec((tm, d), lambda i: (i, 0))],
        out_specs=[pl.BlockSpec((8, 128), lambda i: (0, 0)), pl.BlockSpec((tm, d), lambda i: (i, 0))],
        out_shape=[jax.ShapeDtypeStruct((8, 128), F32), jax.ShapeDtypeStruct((s, d), F32)],
        compiler_params=_cparams(("arbitrary",)), name=name)(y, tgt)


def _split3(v):
    a1 = v.astype(BF16)
    r1 = v - a1.astype(F32)
    a2 = r1.astype(BF16)
    a3 = (r1 - a2.astype(F32)).astype(BF16)
    return a1, a2, a3


def _running_sum(v, carry_ref, reverse):
    tm = v.shape[0]
    row = lax.broadcasted_iota(jnp.int32, (tm, tm), 0)
    col = lax.broadcasted_iota(jnp.int32, (tm, tm), 1)
    tri = ((col >= row) if reverse else (row >= col)).astype(BF16)
    a1, a2, a3 = _split3(v)
    out = _dot(tri, a1) + _dot(tri, a2) + _dot(tri, a3) + carry_ref[...]
    carry_ref[...] = out[0:1, :] if reverse else out[tm - 1:tm, :]
    return out


HEAD_GROUP_FWD = 8
HEAD_GROUP_BWD = 8
LANE_C = 64
LANE_ONE = 67


def _lanes():
    lane = lax.broadcasted_iota(jnp.int32, (1, 128), 1)
    return lane, lane < HEAD_DIM


def _half_mean(t, lo):
    s_lo = jnp.sum(jnp.where(lo, t, 0.0), axis=-1, keepdims=True)
    s_hi = jnp.sum(jnp.where(lo, 0.0, t), axis=-1, keepdims=True)
    return jnp.where(lo, s_lo, s_hi) * (1.0 / HEAD_DIM)


def _lane_col(t, lane, idx):
    return jnp.sum(jnp.where(lane == idx, t, 0.0), axis=-1, keepdims=True)


def _swap_halves(t):
    return pltpu.roll(t, HEAD_DIM, 1)


def attn_prep(proj, h, wt_in, fb, gq2, gk2, name):
    s, d = h.shape
    tm = _tile(s, 512)
    first = N_REST // D_ATTN

    def body(q_ref, k_ref, v_ref, h_ref, wf_ref, fb_ref, gq_ref, gk_ref, qa_ref, ka_ref, va_ref, vt_ref, z_ref, carry_ref):
        lane, lo = _lanes()

        @pl.when(pl.program_id(0) == 0)
        def _():
            carry_ref[...] = jnp.zeros_like(carry_ref)

        z = _dot_nt(h_ref[...], wf_ref[...]) + fb_ref[...]
        z_ref[...] = z
        cv = _running_sum(jnp.minimum(z, 0.0) - jnp.log(1.0 + jnp.exp(-jnp.abs(z))), carry_ref, reverse=False)

        def normed(t, g):
            t = t.astype(F32)
            return t * lax.rsqrt(_half_mean(t * t, lo) + EPS) * g

        one_q = jnp.where((lane >= LANE_ONE) & (lane < LANE_ONE + 3), 1.0, 0.0)
        one_k = jnp.where((lane >= LANE_C) & (lane < LANE_C + 3), 1.0, 0.0)
        one_v = jnp.where(lane == LANE_C, 1.0, 0.0)
        for j in range(HEADS // 2):
            cols = slice(128 * j, 128 * (j + 1))
            qn = normed(q_ref[:, cols], gq_ref[...] * ATTN_SCALE)
            kn = normed(k_ref[:, cols], gk_ref[...])
            vv = v_ref[:, cols].astype(F32)
            for e in range(2):
                h = 2 * j + e
                pick = (lambda t: t) if e == 0 else _swap_halves
                pieces = [p.astype(F32) for p in _split3(_lane_col(cv, lane, h))]
                ext_q, ext_k = one_q, one_k
                for i, p in enumerate(pieces):
                    ext_q = jnp.where(lane == LANE_C + i, p, ext_q)
                    ext_k = jnp.where(lane == LANE_ONE + i, -p, ext_k)
                qa_ref[h] = jnp.where(lo, pick(qn), ext_q).astype(BF16)
                ka_ref[h] = jnp.where(lo, pick(kn), ext_k).astype(BF16)
                va = jnp.where(lo, pick(vv), one_v)
                va_ref[h] = va.astype(BF16)
                vt_ref[h] = va.T.astype(BF16)

    tile = lambda blk: pl.BlockSpec((tm, D_ATTN), lambda i: (i, blk))
    vec = pl.BlockSpec((1, 128), lambda i: (0, 0))
    out = pl.BlockSpec((HEADS, tm, 128), lambda i: (0, i, 0))
    return pl.pallas_call(
        body, grid=(s // tm,),
        in_specs=[tile(first), tile(first + 1), tile(first + 2), pl.BlockSpec((tm, d), lambda i: (i, 0)),
                  pl.BlockSpec((128, d), lambda i: (N_MAIN // 128, 0)), vec, vec, vec],
        out_specs=[out, out, out, pl.BlockSpec((HEADS, 128, tm), lambda i: (0, 0, i)),
                   pl.BlockSpec((tm, 128), lambda i: (i, 0))],
        out_shape=[jax.ShapeDtypeStruct((HEADS, s, 128), BF16)] * 3 + [jax.ShapeDtypeStruct((HEADS, 128, s), BF16),
                                                                       jax.ShapeDtypeStruct((s, 128), F32)],
        scratch_shapes=[pltpu.VMEM((1, 128), F32)],
        compiler_params=_cparams(("arbitrary",)), name=name)(proj, proj, proj, h, wt_in, fb, gq2, gk2)


def _carry(ex, n_in, n_out, n_scratch, grid):
    n_xin, n_xout = (len(ex.inputs), len(ex.out_shapes)) if ex else (0, 0)

    def split(refs):
        ins, xins = refs[:n_in], refs[n_in:n_in + n_xin]
        rest = refs[n_in + n_xin:]
        outs, xouts = rest[:n_out], rest[n_out:n_out + n_xout]
        rest = rest[n_out + n_xout:]
        return ins + outs + rest[:n_scratch], (xins, xouts, rest[n_scratch:])

    def first():
        return functools.reduce(lambda a, b: a & b, [pl.program_id(d) == 0 for d in range(len(grid))])

    def last():
        return functools.reduce(lambda a, b: a & b, [pl.program_id(d) == grid[d] - 1 for d in range(len(grid))])

    return split, first, last


def _carried_call(body, ex, grid, in_specs, out_specs, out_shape, scratch, sem, name, operands, vmem=None):
    any_spec = pl.BlockSpec(memory_space=pl.ANY)
    split, first, last = _carry(ex, len(in_specs), len(out_specs), len(scratch), grid)

    def carried(*refs):
        own, xrefs = split(refs)
        if ex:
            @pl.when(first())
            def _():
                ex.start(*xrefs)

        body(*own)
        if ex:
            @pl.when(last())
            def _():
                ex.drain(*xrefs)

    n_xin = len(ex.inputs) if ex else 0
    results = pl.pallas_call(
        carried, grid=grid, in_specs=list(in_specs) + [any_spec] * n_xin,
        out_specs=list(out_specs) + [any_spec] * (len(ex.out_shapes) if ex else 0),
        out_shape=list(out_shape) + (list(ex.out_shapes) if ex else []),
        input_output_aliases={len(in_specs) + i: len(out_specs) + o for i, o in ex.aliases.items()} if ex else {},
        scratch_shapes=list(scratch) + (ex.scratch if ex else []),
        compiler_params=_cparams(sem, vmem), name=name)(*operands, *(ex.inputs if ex else []))
    return results[:len(out_specs)], results[len(out_specs):]


def _tri_rows(t, n):
    qi = sum(jnp.where(t >= r * (r + 1) // 2, 1, 0) for r in range(1, n))
    return qi, t - qi * (qi + 1) // 2


def _tri_cols(t, n):
    ki = sum(jnp.where(t >= r * n - r * (r - 1) // 2, 1, 0) for r in range(1, n))
    return ki, ki + t - (ki * n - ki * (ki - 1) // 2)


def _causal_t(st_blk, tk, tq):
    key = lax.broadcasted_iota(jnp.int32, (tk, tq), 0)
    qry = lax.broadcasted_iota(jnp.int32, (tk, tq), 1)
    return jnp.where(qry >= key, st_blk, -jnp.inf)


def attn_forward(qa, ka, vt, name, ex=None):
    hh, s, _ = qa.shape
    tq = tk = _tile(s, 512)
    nq = s // tq
    grp = HEAD_GROUP_FWD

    def body(q_ref, k_ref, vt_ref, o_ref, lse_ref, m_ref, acc_ref):
        qi, ki = _tri_rows(pl.program_id(1), nq)

        @pl.when(ki == 0)
        def _():
            m_ref[...] = jnp.full_like(m_ref, -jnp.inf)
            acc_ref[...] = jnp.zeros_like(acc_ref)

        def step(masked):
            nxt = _dot_nt(k_ref[0], q_ref[0])
            for g in range(grp):
                st = nxt
                if g + 1 < grp:
                    nxt = _dot_nt(k_ref[g + 1], q_ref[g + 1])
                if masked:
                    st = _causal_t(st, tk, tq)
                m_old = m_ref[g]
                m_new = jnp.maximum(m_old, jnp.max(st, axis=0, keepdims=True))
                pt = jnp.exp(st - m_new).astype(BF16)
                acc_ref[g] = jnp.exp(m_old - m_new) * acc_ref[g] + _dot(vt_ref[g], pt)
                m_ref[g] = m_new

        @pl.when(ki < qi)
        def _():
            step(False)

        @pl.when(ki == qi)
        def _():
            step(True)
            for g in range(grp):
                acc = acc_ref[g]
                denom = acc[LANE_C:LANE_C + 1, :]
                o_ref[g] = (acc / denom).T.astype(BF16)
                lse_ref[g] = m_ref[g] + jnp.log(denom)

    qspec = pl.BlockSpec((grp, tq, 128), lambda h, t: (h, _tri_rows(t, nq)[0], 0))
    kspec = pl.BlockSpec((grp, tk, 128), lambda h, t: (h, _tri_rows(t, nq)[1], 0))
    vspec = pl.BlockSpec((grp, 128, tk), lambda h, t: (h, 0, _tri_rows(t, nq)[1]))
    lspec = pl.BlockSpec((grp, 1, tq), lambda h, t: (h, 0, _tri_rows(t, nq)[0]))
    return _carried_call(
        body, ex, (hh // grp, nq * (nq + 1) // 2), [qspec, kspec, vspec], [qspec, lspec],
        [jax.ShapeDtypeStruct((hh, s, 128), BF16), jax.ShapeDtypeStruct((hh, 1, s), F32)],
        [pltpu.VMEM((grp, 1, tq), F32), pltpu.VMEM((grp, 128, tq), F32)],
        ("arbitrary", "arbitrary"), name, (qa, ka, vt))


def attn_backward(qa, ka, va, oa, doa, lse, name, ex=None):
    hh, s, _ = qa.shape
    tq = tk = _tile(s, 512)
    nq = s // tq
    grp = HEAD_GROUP_BWD

    def body(q_ref, k_ref, v_ref, o_ref, do_ref, lse_ref, dq_ref, dk_ref, dv_ref, dka_ref, dva_ref):
        ki, qi = _tri_cols(pl.program_id(1), nq)

        @pl.when(pl.program_id(1) == 0)
        def _():
            dq_ref[...] = jnp.zeros_like(dq_ref)

        @pl.when(qi == ki)
        def _():
            dka_ref[...] = jnp.zeros_like(dka_ref)
            dva_ref[...] = jnp.zeros_like(dva_ref)

        def step(masked):
            rows = pl.ds(pl.multiple_of(qi * tq, tq), tq)
            products = lambda g: (_dot_nt(k_ref[g], q_ref[g]), _dot_nt(v_ref[g], do_ref[g]))
            nxt = products(0)
            for g in range(grp):
                st, dpt = nxt
                if g + 1 < grp:
                    nxt = products(g + 1)
                q, k, do = q_ref[g], k_ref[g], do_ref[g]
                if masked:
                    st = _causal_t(st, tk, tq)
                pt = jnp.exp(st - lse_ref[g])
                delta = jnp.sum((do.astype(F32) * o_ref[g].astype(F32)).T, axis=0, keepdims=True)
                dst = (pt * (dpt - delta)).astype(BF16)
                dva_ref[g] += _dot(pt.astype(BF16), do)
                dka_ref[g] += _dot(dst, q)
                dq_ref[g, rows, :] += _dot_tn(dst, k)

        @pl.when(qi > ki)
        def _():
            step(False)

        @pl.when(qi == ki)
        def _():
            step(True)

        @pl.when(qi == nq - 1)
        def _():
            dk_ref[...] = dka_ref[...]
            dv_ref[...] = dva_ref[...].astype(BF16)

    qspec = pl.BlockSpec((grp, tq, 128), lambda h, t: (h, _tri_cols(t, nq)[1], 0))
    lspec = pl.BlockSpec((grp, 1, tq), lambda h, t: (h, 0, _tri_cols(t, nq)[1]))
    kspec = pl.BlockSpec((grp, tk, 128), lambda h, t: (h, _tri_cols(t, nq)[0], 0))
    return _carried_call(
        body, ex, (hh // grp, nq * (nq + 1) // 2), [qspec, kspec, kspec, qspec, qspec, lspec],
        [pl.BlockSpec((grp, s, 128), lambda h, t: (h, 0, 0), pipeline_mode=pl.Buffered(1)), kspec, kspec],
        [jax.ShapeDtypeStruct((hh, s, 128), F32), jax.ShapeDtypeStruct((hh, s, 128), F32),
         jax.ShapeDtypeStruct((hh, s, 128), BF16)],
        [pltpu.VMEM((grp, tk, 128), F32), pltpu.VMEM((grp, tk, 128), F32)],
        ("arbitrary", "arbitrary"), name, (qa, ka, va, oa, doa, lse), vmem=58 * 2 ** 20)


def attn_post(dqa, dka, dva, proj, z, gq2, gk2, dproj, name):
    s = proj.shape[0]
    tm = _tile(s, 512)
    nt = s // tm

    def body(dq_ref, dk_ref, dv_ref, q_ref, k_ref, z_ref, gq_ref, gk_ref, dp_any, dp_ref, dgq_ref, dgk_ref, db_ref,
             carry_ref):
        lane, lo = _lanes()

        @pl.when(pl.program_id(0) == 0)
        def _():
            dgq_ref[...] = jnp.zeros_like(dgq_ref)
            dgk_ref[...] = jnp.zeros_like(dgk_ref)
            db_ref[...] = jnp.zeros_like(db_ref)
            carry_ref[...] = jnp.zeros_like(carry_ref)

        def pair(ref, j):
            return jnp.where(lo, ref[2 * j].astype(F32), _swap_halves(ref[2 * j + 1].astype(F32)))

        def norm_bwd(raw, g, dhat, scale):
            r = lax.rsqrt(_half_mean(raw * raw, lo) + EPS)
            y = raw * r
            dy = dhat * (g * scale)
            return r * (dy - y * _half_mean(dy * y, lo)), jnp.sum(dhat * y, axis=0, keepdims=True) * scale

        dc = jnp.zeros((tm, 128), F32)
        for j in range(HEADS // 2):
            cols = slice(128 * j, 128 * (j + 1))
            dq, dgq = norm_bwd(q_ref[:, cols].astype(F32), gq_ref[...], pair(dq_ref, j), ATTN_SCALE)
            dk, dgk = norm_bwd(k_ref[:, cols].astype(F32), gk_ref[...], pair(dk_ref, j), 1.0)
            dgq_ref[...] += dgq
            dgk_ref[...] += dgk
            dp_ref[:, cols] = dq.astype(BF16)
            dp_ref[:, D_ATTN + 128 * j:D_ATTN + 128 * (j + 1)] = dk.astype(BF16)
            dp_ref[:, 2 * D_ATTN + 128 * j:2 * D_ATTN + 128 * (j + 1)] = pair(dv_ref, j).astype(BF16)
            for e in range(2):
                h = 2 * j + e
                both = jnp.where(lane == LANE_C, dq_ref[h], 0.0) - jnp.where(lane == LANE_ONE, dk_ref[h], 0.0)
                dc = jnp.where(lane == h, jnp.sum(both, axis=-1, keepdims=True), dc)
        dz = _running_sum(dc, carry_ref, reverse=True) * (1.0 - _sigmoid(z_ref[...]))
        db_ref[...] += jnp.sum(dz, axis=0, keepdims=True)
        dp_ref[:, 3 * D_ATTN:3 * D_ATTN + 128] = dz.astype(BF16)
        dp_ref[:, 3 * D_ATTN + 128:] = jnp.zeros((tm, DPROJ_TAIL - 3 * D_ATTN - 128), BF16)

    heads = lambda: pl.BlockSpec((HEADS, tm, 128), lambda i: (0, nt - 1 - i, 0))
    vec = pl.BlockSpec((1, 128), lambda i: (0, 0))
    first = N_REST // D_ATTN
    return pl.pallas_call(
        body, grid=(nt,),
        in_specs=[heads(), heads(), heads(), pl.BlockSpec((tm, D_ATTN), lambda i: (nt - 1 - i, first)),
                  pl.BlockSpec((tm, D_ATTN), lambda i: (nt - 1 - i, first + 1)),
                  pl.BlockSpec((tm, 128), lambda i: (nt - 1 - i, 0)), vec, vec, pl.BlockSpec(memory_space=pl.ANY)],
        out_specs=[pl.BlockSpec((tm, DPROJ_TAIL), lambda i: (nt - 1 - i, N_REST // DPROJ_TAIL)), vec, vec, vec],
        out_shape=[jax.ShapeDtypeStruct(dproj.shape, BF16), jax.ShapeDtypeStruct((1, 128), F32),
                   jax.ShapeDtypeStruct((1, 128), F32), jax.ShapeDtypeStruct((1, 128), F32)],
        scratch_shapes=[pltpu.VMEM((1, 128), F32)], input_output_aliases={8: 0},
        compiler_params=_cparams(("arbitrary",)), name=name)(dqa, dka, dva, proj, proj, z, gq2, gk2, dproj)


def _pool_groups(tm):
    gid = lax.broadcasted_iota(jnp.int32, (1, D_POOL), 1) // (D_POOL // 4)
    win = jnp.where(gid == 0, 2.0, jnp.where(gid == 1, 4.0, jnp.where(gid == 2, 8.0, 16.0)))
    return gid, win


def _by_group(gid, v2, v4, v8, v16):
    return jnp.where(gid == 0, v2, jnp.where(gid == 1, v4, jnp.where(gid == 2, v8, v16)))


def _branches(rest_ref, halo_ref, a_ref, wa_ref, wc_ref, wp_ref, sc_ref, cw_ref, ti, tm):
    f = lambda v: v.astype(F32)
    cx, cb, cc, px = f(rest_ref[:, 0:256]), f(rest_ref[:, 256:512]), f(rest_ref[:, 512:768]), f(rest_ref[:, 768:1024])
    live = jnp.where(ti > 0, 1.0, 0.0)
    hz = f(halo_ref[:, 0:256]) * f(halo_ref[:, 512:768]) * live
    hp = f(halo_ref[:, 768:1024]) * live
    z = cc * cx
    zf = jnp.concatenate([hz, z], axis=0)
    z1 = pltpu.roll(zf, 1, 0)[HALO:]
    z2 = pltpu.roll(zf, 2, 0)[HALO:]
    cw = cw_ref[...]
    conv = cw[2:3] * z + cw[1:2] * z1 + cw[0:1] * z2
    uc = cb * conv
    pf = jnp.concatenate([hp, px], axis=0)
    s2 = pf + pltpu.roll(pf, 1, 0)
    s4 = s2 + pltpu.roll(s2, 2, 0)
    s8 = s4 + pltpu.roll(s4, 4, 0)
    s16 = s8 + pltpu.roll(s8, 8, 0)
    gid, win = _pool_groups(tm)
    t = (ti * tm + lax.broadcasted_iota(jnp.int32, (tm, 1), 0)).astype(F32)
    inv = 1.0 / jnp.minimum(t + 1.0, win)
    dpool = _by_group(gid, s2[HALO:], s4[HALO:], s8[HALO:], s16[HALO:]) * inv - px
    _, lo = _lanes()
    a_tok = [jnp.where(lo, f(a_ref[2 * j]), _swap_halves(f(a_ref[2 * j + 1]))).astype(BF16) for j in range(HEADS // 2)]
    y_attn = _dot(a_tok[0], wa_ref[0:128, :])
    for j in range(1, HEADS // 2):
        y_attn += _dot(a_tok[j], wa_ref[128 * j:128 * (j + 1), :])
    y_conv = _dot(uc.astype(BF16), wc_ref[...])
    y_pool_raw = _dot(dpool.astype(BF16), wp_ref[...])
    sg = [_sigmoid(f(rest_ref[:, 1024 + i * D_MODEL:1024 + (i + 1) * D_MODEL])) for i in range(3)]
    return dict(cx=cx, cb=cb, cc=cc, z=z, z1=z1, z2=z2, conv=conv, uc=uc, dpool=dpool, inv=inv, gid=gid, a_tok=a_tok,
                y_attn=y_attn, y_conv=y_conv, y_pool_raw=y_pool_raw, sg=sg, cw=cw)


def _mix_specs(tm, ti_of):
    blocks_per_tile = tm // HALO
    return [
        pl.BlockSpec((tm, N_REST), lambda i: (ti_of(i), 0)),
        pl.BlockSpec((HALO, 1024), lambda i: (jnp.maximum(ti_of(i) * blocks_per_tile - 1, 0), 0)),
        pl.BlockSpec((HEADS, tm, 128), lambda i: (0, ti_of(i), 0)),
        pl.BlockSpec((D_ATTN, D_MODEL), lambda i: (0, 0)),
        pl.BlockSpec((D_CONV, D_MODEL), lambda i: (0, 0)),
        pl.BlockSpec((D_POOL, D_MODEL), lambda i: (0, 0)),
        pl.BlockSpec((1, D_MODEL), lambda i: (0, 0)),
        pl.BlockSpec((8, D_CONV), lambda i: (0, 0)),
    ]


def mix_fwd(proj, a, x, wa, wc, wp, scale, cw, wo, name, ex=None):
    s = x.shape[0]
    tm = _tile(s, 256)

    def body(rest_ref, halo_ref, a_ref, wa_ref, wc_ref, wp_ref, sc_ref, cw_ref, wo_ref, x_ref, o_ref):
        b = _branches(rest_ref, halo_ref, a_ref, wa_ref, wc_ref, wp_ref, sc_ref, cw_ref, pl.program_id(0), tm)
        merged = b["sg"][0] * b["y_attn"] + b["sg"][1] * b["y_conv"] + b["sg"][2] * (b["y_pool_raw"] * sc_ref[...])
        o_ref[...] = x_ref[...] + _dot(merged.astype(BF16), wo_ref[...])

    (x1,), carried = _carried_call(
        body, ex, (s // tm,),
        _mix_specs(tm, lambda i: i) + [pl.BlockSpec((D_MODEL, D_MODEL), lambda i: (0, 0)),
                                       pl.BlockSpec((tm, D_MODEL), lambda i: (i, 0))],
        [pl.BlockSpec((tm, D_MODEL), lambda i: (i, 0))], [jax.ShapeDtypeStruct((s, D_MODEL), F32)], [],
        ("arbitrary",), name, (proj, proj, a, wa, wc, wp, scale, cw, wo, x))
    return x1, carried


def mix_bwd(proj, a, dx1, wa, wc, wp, scale, cw, wo, name):
    s = dx1.shape[0]
    tm = _tile(s, 256)
    nt = s // tm
    ti_of = lambda i: nt - 1 - i
    n = tm + HALO

    def body(rest_ref, halo_ref, a_ref, wa_ref, wc_ref, wp_ref, sc_ref, cw_ref, wo_ref,
             dx_ref, dp_ref, da_ref, at_ref, mg_ref, dya_ref, dyc_ref, dyp_ref, uc_ref, dd_ref, dsc_ref, dcw_ref,
             cdc_ref, cde_ref):
        i = pl.program_id(0)
        ti = ti_of(i)

        @pl.when(i == 0)
        def _():
            cdc_ref[...] = jnp.zeros_like(cdc_ref)
            cde_ref[...] = jnp.zeros_like(cde_ref)
            dsc_ref[...] = jnp.zeros_like(dsc_ref)
            dcw_ref[...] = jnp.zeros_like(dcw_ref)

        b = _branches(rest_ref, halo_ref, a_ref, wa_ref, wc_ref, wp_ref, sc_ref, cw_ref, ti, tm)
        sg, sc = b["sg"], sc_ref[...]
        y_pool = b["y_pool_raw"] * sc
        merged = sg[0] * b["y_attn"] + sg[1] * b["y_conv"] + sg[2] * y_pool
        mg_ref[...] = merged.astype(BF16)
        dm = _dot_nt(dx_ref[...].astype(BF16), wo_ref[...])
        dys = [dm * sg[j] for j in range(3)]
        for j, y in enumerate((b["y_attn"], b["y_conv"], y_pool)):
            dp_ref[:, 1024 + j * D_MODEL:1024 + (j + 1) * D_MODEL] = (dys[j] * y * (1.0 - sg[j])).astype(BF16)
        dya = dys[0].astype(BF16)
        dya_ref[...] = dya
        _, lo = _lanes()
        for j in range(HEADS // 2):
            at_ref[:, 128 * j:128 * (j + 1)] = b["a_tok"][j]
            da = _dot_nt(dya, wa_ref[128 * j:128 * (j + 1), :])
            da_ref[2 * j] = jnp.where(lo, da, 0.0).astype(BF16)
            da_ref[2 * j + 1] = jnp.where(lo, _swap_halves(da), 0.0).astype(BF16)
        dyc = dys[1].astype(BF16)
        dyc_ref[...] = dyc
        duc = _dot_nt(dyc, wc_ref[...])
        dyp = dys[2]
        dsc_ref[...] += jnp.sum(dyp * b["y_pool_raw"], axis=0, keepdims=True)
        dypr = (dyp * sc).astype(BF16)
        dyp_ref[...] = dypr
        ddp = _dot_nt(dypr, wp_ref[...])
        uc_ref[...] = b["uc"].astype(BF16)
        dd_ref[...] = b["dpool"].astype(BF16)

        dconv = duc * b["cb"]
        dp_ref[:, 256:512] = (duc * b["conv"]).astype(BF16)
        dcf = jnp.concatenate([dconv, cdc_ref[...]], axis=0)
        cw = b["cw"]
        dz = cw[2:3] * dconv + cw[1:2] * pltpu.roll(dcf, n - 1, 0)[:tm] + cw[0:1] * pltpu.roll(dcf, n - 2, 0)[:tm]
        dp_ref[:, 0:256] = (dz * b["cc"]).astype(BF16)
        dp_ref[:, 512:768] = (dz * b["cx"]).astype(BF16)
        dcw_ref[0:1, :] += jnp.sum(dconv * b["z2"], axis=0, keepdims=True)
        dcw_ref[1:2, :] += jnp.sum(dconv * b["z1"], axis=0, keepdims=True)
        dcw_ref[2:3, :] += jnp.sum(dconv * b["z"], axis=0, keepdims=True)
        cdc_ref[...] = dconv[:HALO]

        e = ddp * b["inv"]
        ef = jnp.concatenate([e, cde_ref[...]], axis=0)
        r2 = ef + pltpu.roll(ef, n - 1, 0)
        r4 = r2 + pltpu.roll(r2, n - 2, 0)
        r8 = r4 + pltpu.roll(r4, n - 4, 0)
        r16 = r8 + pltpu.roll(r8, n - 8, 0)
        dp_ref[:, 768:1024] = (_by_group(b["gid"], r2[:tm], r4[:tm], r8[:tm], r16[:tm]) - ddp).astype(BF16)
        cde_ref[...] = e[:HALO]

    tile = lambda w: pl.BlockSpec((tm, w), lambda i: (ti_of(i), 0))
    whole = lambda r, c: pl.BlockSpec((r, c), lambda i: (0, 0))
    bf = lambda w: jax.ShapeDtypeStruct((s, w), BF16)
    return pl.pallas_call(
        body, grid=(nt,),
        in_specs=_mix_specs(tm, ti_of) + [whole(D_MODEL, D_MODEL), tile(D_MODEL)],
        out_specs=[tile(N_REST), pl.BlockSpec((HEADS, tm, 128), lambda i: (0, ti_of(i), 0)), tile(D_ATTN),
                   tile(D_MODEL), tile(D_MODEL), tile(D_MODEL), tile(D_MODEL),
                   tile(D_CONV), tile(D_POOL), whole(1, D_MODEL), whole(8, D_CONV)],
        out_shape=[bf(DPROJ_COLS), jax.ShapeDtypeStruct((HEADS, s, 128), BF16), bf(D_ATTN),
                   bf(D_MODEL), bf(D_MODEL), bf(D_MODEL), bf(D_MODEL), bf(D_CONV), bf(D_POOL),
                   jax.ShapeDtypeStruct((1, D_MODEL), F32), jax.ShapeDtypeStruct((8, D_CONV), F32)],
        scratch_shapes=[pltpu.VMEM((HALO, D_CONV), F32), pltpu.VMEM((HALO, D_POOL), F32)],
        compiler_params=_cparams(("arbitrary",)), name=name)(proj, proj, a, wa, wc, wp, scale, cw, wo, dx1)


def _adamw_math(w, g, m, v):
    m = ADAM_B1 * m + (1.0 - ADAM_B1) * g
    v = ADAM_B2 * v + (1.0 - ADAM_B2) * (g * g)
    m_hat = m / (1.0 - ADAM_B1 ** ADAM_STEP)
    v_hat = v / (1.0 - ADAM_B2 ** ADAM_STEP)
    delta = -ADAM_LR * (m_hat / (jnp.sqrt(v_hat) + ADAM_EPS) + ADAM_WD * w)
    return delta, m, v


ADAMW_PARTS_BLOCK_BYTES = 4 * 2 ** 20


def _row_tile(rows, cols, copies, itemsize):
    row_bytes = copies * (-(-cols // 128) * 128) * itemsize
    fits = [t for t in range(16, rows + 1, 16) if rows % t == 0 and t * row_bytes <= ADAMW_PARTS_BLOCK_BYTES]
    return max(fits) if fits else rows


def pair_sum(blocks, stage, me, name):
    n_slots, rows, cols = stage.shape
    tr = _row_tile(rows, cols, 1, 4)

    def body(me_ref, a_ref, b_ref, o_ref):
        o_ref[...] = (a_ref[...].astype(F32) + b_ref[...].astype(F32)).astype(BF16)

    slot = pl.BlockSpec((None, tr, cols), lambda i, r, me_ref: (i, r, 0))
    return pl.pallas_call(
        body, out_shape=jax.ShapeDtypeStruct(stage.shape, BF16),
        grid_spec=pltpu.PrefetchScalarGridSpec(
            num_scalar_prefetch=1, grid=(n_slots, rows // tr),
            in_specs=[pl.BlockSpec((None, tr, cols), lambda i, r, me_ref: (me_ref[0] ^ (2 * i), r, 0)), slot],
            out_specs=slot),
        compiler_params=_cparams(("parallel", "parallel")), name=name)(me.reshape(1), blocks, stage)


def adamw_sum(parts, w, m, v, name):
    layers, rows, cols = w.shape
    n_parts = parts.shape[1]
    if rows % 16 == 0:
        tr, tc = _row_tile(rows, cols, n_parts, parts.dtype.itemsize), cols
    else:
        tr, tc = rows, _pick(cols, (256, 128))

    def body(p_ref, w_ref, m_ref, v_ref, g_ref, d_ref, nm_ref, nv_ref):
        g = p_ref[0].astype(F32)
        for i in range(1, n_parts):
            g = g + p_ref[i].astype(F32)
        g_ref[...] = g
        d_ref[...], nm_ref[...], nv_ref[...] = _adamw_math(w_ref[...], g, m_ref[...], v_ref[...])

    spec = pl.BlockSpec((None, tr, tc), lambda l, i, j: (l, i, j))
    return pl.pallas_call(
        body, grid=(layers, rows // tr, cols // tc),
        in_specs=[pl.BlockSpec((None, n_parts, tr, tc), lambda l, i, j: (l, 0, i, j)), spec, spec, spec],
        out_specs=[spec] * 4, out_shape=[jax.ShapeDtypeStruct((layers, rows, cols), F32)] * 4,
        compiler_params=_cparams(("parallel", "parallel", "parallel")), name=name)(parts, w, m, v)


def _me():
    return lax.axis_index("x"), lax.axis_index("y"), lax.axis_index("c")


N_PEERS = N_DEV - 1


def all_gather(shards, name):
    n = len(shards)
    any_spec = pl.BlockSpec(memory_space=pl.ANY)

    def body(*refs):
        x_refs, out_refs = refs[:n], refs[n:2 * n]
        send_sems, recv_sems, local_sems = refs[2 * n:]
        x, y, c = _me()
        me, sibling = (x, y, c), (x, y, 1 - c)
        chips = [(1 - x, y), (x, 1 - y), (1 - x, 1 - y)]

        def copy(t, k, block, to, from_input=False):
            slot = out_refs[t].at[4 * block[0] + 2 * block[1] + block[2]]
            return pltpu.make_async_remote_copy(
                src_ref=x_refs[t] if from_input else slot, dst_ref=slot, send_sem=send_sems.at[N_PEERS * t + k],
                recv_sem=recv_sems.at[N_PEERS * t + k], device_id=to, device_id_type=pl.DeviceIdType.MESH)

        mine = [pltpu.make_async_copy(x_refs[t], out_refs[t].at[4 * x + 2 * y + c], local_sems.at[t]) for t in range(n)]
        started = []
        for t in range(n):
            mine[t].start()
            started.append(copy(t, 0, me, sibling, from_input=True))
            started += [copy(t, 1 + j, me, (*chip, c), from_input=True) for j, chip in enumerate(chips)]
        for cp in started:
            cp.start()
        for j, chip in enumerate(chips):
            for t in range(n):
                copy(t, 1 + j, (*chip, c), me).wait_recv()
                fwd = copy(t, 4 + j, (*chip, c), sibling)
                fwd.start()
                started.append(fwd)
        for t in range(n):
            copy(t, 0, sibling, me).wait_recv()
            for j, chip in enumerate(chips):
                copy(t, 4 + j, (*chip, 1 - c), me).wait_recv()
        for cp in started:
            cp.wait_send()
        for cp in mine:
            cp.wait()

    return pl.pallas_call(
        body, out_shape=[jax.ShapeDtypeStruct((N_DEV,) + s.shape, s.dtype) for s in shards],
        in_specs=[any_spec] * n, out_specs=[any_spec] * n,
        scratch_shapes=[pltpu.SemaphoreType.DMA((N_PEERS * n,)), pltpu.SemaphoreType.DMA((N_PEERS * n,)),
                        pltpu.SemaphoreType.DMA((n,))],
        name=name)(*shards)


SIBLING = 1
OTHER_CHIPS = (2, 4, 6)
SAME_CORE = (0,) + OTHER_CHIPS


class Exchange:
    def __init__(self, inputs, out_shapes, aliases, copies, local=()):
        self.inputs, self.out_shapes, self.aliases = list(inputs), list(out_shapes), aliases
        self._copies, self._local = list(copies), list(local)
        self.scratch = [pltpu.SemaphoreType.DMA((len(self._copies),)), pltpu.SemaphoreType.DMA((len(self._copies),)),
                        pltpu.SemaphoreType.DMA((max(len(self._local), 1),))]

    def _build(self, ins, outs, sems):
        send_sems, recv_sems, local_sems = sems
        x, y, c = _me()
        me = 4 * x + 2 * y + c
        local = [functools.partial(pltpu.make_async_copy, src(ins, outs, me), dst(outs, me), local_sems.at[i])
                 for i, (src, dst) in enumerate(self._local)]
        sends, recvs = [], []
        for i, (mask, src, dst) in enumerate(self._copies):
            px, py, pc = x ^ ((mask >> 2) & 1), y ^ ((mask >> 1) & 1), c ^ (mask & 1)
            pair = dict(send_sem=send_sems.at[i], recv_sem=recv_sems.at[i], device_id_type=pl.DeviceIdType.MESH)
            sends.append(functools.partial(
                pltpu.make_async_remote_copy, src_ref=src(ins, outs, me), dst_ref=dst(outs, me), device_id=(px, py, pc), **pair))
            recvs.append(functools.partial(
                pltpu.make_async_remote_copy, src_ref=src(ins, outs, me), dst_ref=dst(outs, me ^ mask), device_id=(x, y, c), **pair))
        return local, sends, recvs

    def start(self, ins, outs, sems):
        local, sends, _ = self._build(ins, outs, sems)
        for make in local + sends:
            make().start()

    def drain(self, ins, outs, sems):
        local, sends, recvs = self._build(ins, outs, sems)
        for make in recvs:
            make().wait_recv()
        for make in sends:
            make().wait_send()
        for make in local:
            make().wait()


def _bind(fn, *args):
    return functools.partial(fn, *args)


def join_exchanges(a, b):
    if a is None or b is None:
        return a or b
    na_in, na_out = len(a.inputs), len(a.out_shapes)

    def src_a(fn):
        return lambda ins, outs, me: fn(ins[:na_in], outs[:na_out], me)

    def dst_a(fn):
        return lambda outs, who: fn(outs[:na_out], who)

    def src_b(fn):
        return lambda ins, outs, me: fn(ins[na_in:], outs[na_out:], me)

    def dst_b(fn):
        return lambda outs, who: fn(outs[na_out:], who)

    copies = [(m, src_a(s), dst_a(d)) for m, s, d in a._copies] + [(m, src_b(s), dst_b(d)) for m, s, d in b._copies]
    local = [(src_a(s), dst_a(d)) for s, d in a._local] + [(src_b(s), dst_b(d)) for s, d in b._local]
    aliases = dict(a.aliases)
    aliases.update({na_in + i: na_out + o for i, o in b.aliases.items()})
    return Exchange(a.inputs + b.inputs, a.out_shapes + b.out_shapes, aliases, copies, local)


def gather_over_ici(shards):
    copies = [(mask, _bind(lambda t, ins, outs, me: ins[t], t), _bind(lambda t, outs, sender: outs[t].at[sender], t))
              for t in range(len(shards)) for mask in OTHER_CHIPS]
    local = [(_bind(lambda t, ins, outs, me: ins[t], t), _bind(lambda t, outs, me: outs[t].at[me], t))
             for t in range(len(shards))]
    return Exchange(shards, [jax.ShapeDtypeStruct((N_DEV,) + s.shape, s.dtype) for s in shards], {}, copies, local)


def gather_over_d2d(gathered):
    copies = [(SIBLING, _bind(lambda t, m, ins, outs, me: outs[t].at[me ^ m], t, m),
               _bind(lambda t, m, outs, sender: outs[t].at[sender ^ m], t, m))
              for t in range(len(gathered)) for m in SAME_CORE]
    return Exchange(gathered, [jax.ShapeDtypeStruct(g.shape, g.dtype) for g in gathered],
                    {t: t for t in range(len(gathered))}, copies)


def scatter_over_d2d(blocks):
    copies = [(SIBLING, _bind(lambda t, m, ins, outs, me: ins[t].at[me ^ SIBLING ^ m], t, m),
               _bind(lambda t, i, outs, sender: outs[t].at[i], t, i))
              for t in range(len(blocks)) for i, m in enumerate(SAME_CORE)]
    return Exchange(blocks, [jax.ShapeDtypeStruct((len(SAME_CORE),) + b.shape[1:], b.dtype) for b in blocks], {}, copies)


def scatter_over_ici(pair_sums, bufs, layer):
    n = len(pair_sums)
    copies = [(m, _bind(lambda t, i, ins, outs, me: ins[t].at[i], t, i),
               _bind(lambda t, i, outs, sender: outs[t].at[layer, i], t, i))
              for t in range(n) for i, m in enumerate(SAME_CORE) if m]
    local = [(_bind(lambda t, ins, outs, me: ins[t].at[0], t), _bind(lambda t, outs, me: outs[t].at[layer, 0], t))
             for t in range(n)]
    return Exchange(list(pair_sums) + list(bufs), [jax.ShapeDtypeStruct(b.shape, b.dtype) for b in bufs],
                    {n + t: t for t in range(n)}, copies, local)


def run_exchange(ex, name):
    any_spec = pl.BlockSpec(memory_space=pl.ANY)
    n_in, n_out = len(ex.inputs), len(ex.out_shapes)

    def body(*refs):
        ins, outs, sems = refs[:n_in], refs[n_in:n_in + n_out], refs[n_in + n_out:]
        ex.start(ins, outs, sems)
        ex.drain(ins, outs, sems)

    return pl.pallas_call(
        body, out_shape=ex.out_shapes, in_specs=[any_spec] * n_in, out_specs=[any_spec] * n_out,
        input_output_aliases=ex.aliases, scratch_shapes=ex.scratch, name=name)(*ex.inputs)


MATRICES = ("w_in", "w_attn_out", "w_conv_out", "pool_w", "w_o", "w_ffn_in", "w_ffn_out")
TRANSPOSED = ("w_in", "w_ffn_in")
EVERY = tuple(range(len(MATRICES)))
IN_PROJ_PART, ATTN_PART, MIX_PART = (0,), (1, 2, 3, 4, 5), (6,)
LATE = (0,)
EARLY = EVERY[1:]
EARLY_FIRST, EARLY_SECOND = (4, 6), (1, 2, 3, 5)
SHARD_INFO = {
    "w_in": ((DEPTH, D_IN // N_DEV, D_MODEL), 1),
    "w_attn_out": ((DEPTH, D_ATTN, D_MODEL // N_DEV), 2),
    "w_conv_out": ((DEPTH, D_CONV, D_MODEL // N_DEV), 2),
    "pool_w": ((DEPTH, 4, 64, 256 // N_DEV), 3),
    "w_o": ((DEPTH, D_MODEL // N_DEV, D_MODEL), 1),
    "w_ffn_in": ((DEPTH, 2 * D_FF // N_DEV, D_MODEL), 1),
    "w_ffn_out": ((DEPTH, D_FF // N_DEV, D_MODEL), 1),
}


def _handled(name, t):
    return jnp.transpose(t, (0, 2, 1)) if name in TRANSPOSED else t
VECTORS = ("norm_mix_g", "forget_b", "q_norm_g", "k_norm_g", "pool_scale", "norm_ffn_g")
VECTOR_SHAPES = {"norm_mix_g": (DEPTH, D_MODEL), "forget_b": (DEPTH, HEADS), "q_norm_g": (DEPTH, HEAD_DIM),
                 "k_norm_g": (DEPTH, HEAD_DIM), "pool_scale": (DEPTH, D_MODEL), "norm_ffn_g": (DEPTH, D_MODEL)}
CONV_W_FULL = (DEPTH, 3, D_CONV)


def _size(shape):
    n = 1
    for v in shape:
        n *= v
    return n


def _pack(arrays, rows, cols):
    flat = jnp.concatenate([a.reshape(-1) for a in arrays])
    return jnp.pad(flat, (0, rows * cols - flat.shape[0])).reshape(rows, cols)


def _unpack(packed, shapes):
    flat, out, off = packed.reshape(-1), [], 0
    for shp in shapes:
        out.append(flat[off:off + _size(shp)].reshape(shp))
        off += _size(shp)
    return out


def _join_shards(stacked, axis):
    moved = jnp.moveaxis(stacked, 0, axis)
    shp = list(moved.shape)
    shp[axis:axis + 2] = [shp[axis] * shp[axis + 1]]
    return moved.reshape(shp)


def _cut_shards(full, axis):
    shp = list(full.shape)
    shp[axis:axis + 1] = [N_DEV, shp[axis] // N_DEV]
    return jnp.moveaxis(full.reshape(shp), axis, 0)


N_MOVED = 1544
SHARD_ROWS = D_IN // N_DEV


def _regroup_w_in(shards):
    wt = shards.reshape(D_IN, shards.shape[2])
    pad = jnp.zeros((N_FULL - D_IN, wt.shape[1]), wt.dtype)
    return jnp.concatenate([wt[N_MOVED:], wt[:N_MOVED], pad], axis=0)


def _ungroup_w_in(wpt):
    def kernel_rows(a, b):
        if b <= N_MOVED:
            return [wpt[a + D_IN - N_MOVED:b + D_IN - N_MOVED]]
        if a >= N_MOVED:
            return [wpt[a - N_MOVED:b - N_MOVED]]
        return kernel_rows(a, N_MOVED) + kernel_rows(N_MOVED, b)

    return jnp.stack([jnp.concatenate(kernel_rows(s * SHARD_ROWS, (s + 1) * SHARD_ROWS), axis=0) for s in range(N_DEV)])


def _pool_block_diag(w):
    out = jnp.zeros((D_POOL, D_MODEL), w.dtype)
    for g in range(4):
        out = lax.dynamic_update_slice(out, w[g], (g * 64, g * 256))
    return out


def _pool_from_block_diag(wbd):
    return jnp.stack([wbd[g * 64:(g + 1) * 64, g * 256:(g + 1) * 256] for g in range(4)])


def _layer_weights(mats, vec, conv_w, l):
    wp = _pool_block_diag(mats["pool_w"])
    row = lambda v: v.reshape(1, -1)
    fb = jnp.zeros((1, 128), F32).at[0, :HEADS].set(vec["forget_b"][l])
    cw = jnp.zeros((8, D_CONV), F32).at[:3].set(conv_w[l])
    twice = lambda v: jnp.tile(v.reshape(1, -1), (1, 2))
    return dict(
        wt_in=_regroup_w_in(mats["w_in"]), wt_ffn_in=mats["w_ffn_in"], w_ffn_out=mats["w_ffn_out"],
        wa=mats["w_attn_out"], wc=mats["w_conv_out"], wp=wp, wo=mats["w_o"],
        g_mix=row(vec["norm_mix_g"][l]), g_ffn=row(vec["norm_ffn_g"][l]), gq2=twice(vec["q_norm_g"][l]),
        gk2=twice(vec["k_norm_g"][l]), scale=row(vec["pool_scale"][l]), fb=fb, cw=cw)


def _layer_fwd(x, w, l, comm):
    (proj, h), half_a = norm_matmul(x, w["g_mix"], w["wt_in"], N_MAIN, f"in_proj_{l}", comm.gather_ici(l + 1, IN_PROJ_PART))
    qa, ka, va, vt, z = attn_prep(proj, h, w["wt_in"], w["fb"], w["gq2"], w["gk2"], f"attn_prep_{l}")
    (oa, lse), half_b = attn_forward(qa, ka, vt, f"attn_fwd_{l}", comm.gather_ici(l + 1, ATTN_PART))
    x1, half_c = mix_fwd(proj, oa, x, w["wa"], w["wc"], w["wp"], w["scale"], w["cw"], w["wo"], f"mix_fwd_{l}",
                         comm.gather_ici(l + 1, MIX_PART))
    half = list(half_a) + list(half_b) + list(half_c)
    (gu, h2), gathered = norm_matmul(x1, w["g_ffn"], w["wt_ffn_in"], 2 * D_FF, f"ffn_in_{l}", comm.gather_d2d(l + 1, half))
    x2 = swiglu_matmul(gu, w["w_ffn_out"], x1, f"ffn_out_{l}")
    saved = dict(x=x, proj=proj, h=h, z=z, qa=qa, ka=ka, va=va, oa=oa, lse=lse, x1=x1, gu=gu, h2=h2)
    return x2, saved, gathered


def _layer_bwd(dx2, sv, w, l, comm):
    g = {}
    (dgu, act), stage = swiglu_bwd(dx2, sv["gu"], w["w_ffn_out"], f"ffn_out_bwd_{l}", comm.scatter_d2d(l + 1))
    sums = comm.pair_sums(l + 1, stage)
    g["w_ffn_out"] = tn_matmul(act, dx2, f"dw_ffn_out_{l}")
    g["w_ffn_in"] = tn_matmul(dgu, sv["h2"], f"dw_ffn_in_{l}")
    (dx1, dg), _ = matmul_normbwd(dgu, w["wt_ffn_in"], sv["x1"], w["g_ffn"], dx2, f"ffn_in_bwd_{l}")
    g["norm_ffn_g"] = dg[0]

    (dproj, doa, a_tok, merged, dya, dyc, dyp, uc, dd, dscale, dcw) = mix_bwd(
        sv["proj"], sv["oa"], dx1, w["wa"], w["wc"], w["wp"], w["scale"], w["cw"], w["wo"], f"mix_bwd_{l}")
    g["w_o"] = tn_matmul(merged, dx1, f"dw_o_{l}")
    g["w_attn_out"], g["w_conv_out"], dwp = tn_matmuls([(a_tok, dya), (uc, dyc), (dd, dyp)], f"dw_branches_{l}")
    g["pool_w"] = _pool_from_block_diag(dwp)
    g["pool_scale"] = dscale[0]
    g["conv_w"] = dcw[:3]

    early = comm.early(l)
    comm.grads(l, g)
    above = comm.scatter_ici(l + 1, sums)
    (dqa, dka, dva), got = attn_backward(sv["qa"], sv["ka"], sv["va"], sv["oa"], doa, sv["lse"], f"attn_bwd_{l}",
                                         join_exchanges(above, comm.scatter_d2d(l, early) if early else None))
    n_above = len(above.out_shapes) if above else 0
    comm.scattered(got[:n_above])
    early_sums = dict(zip(early, comm.pair_sums(l, got[n_above:], early))) if early else {}
    early_ici = lambda which: comm.scatter_ici(l, [early_sums[t] for t in which], which) if early else None
    dproj, dgq, dgk, db = attn_post(dqa, dka, dva, sv["proj"], sv["z"], w["gq2"], w["gk2"], dproj, f"attn_post_{l}")
    g["q_norm_g"] = dgq[0, :HEAD_DIM] + dgq[0, HEAD_DIM:]
    g["k_norm_g"] = dgk[0, :HEAD_DIM] + dgk[0, HEAD_DIM:]
    g["forget_b"] = db[0, :HEADS]

    dw_in = tn_matmul(dproj, sv["h"], f"dw_in_{l}", m_cols=N_FULL, ex=early_ici(EARLY_FIRST))
    if early:
        dw_in, got = dw_in
        comm.scattered(got, EARLY_FIRST)
    g["w_in"] = _ungroup_w_in(dw_in)
    (dx, dg), got = matmul_normbwd(dproj, w["wt_in"], sv["x"], w["g_mix"], dx1, f"in_proj_bwd_{l}", k=N_FULL,
                                   ex=early_ici(EARLY_SECOND))
    comm.scattered(got, EARLY_SECOND if early else None)
    g["norm_mix_g"] = dg[0]
    comm.grads(l, g)
    return dx


def _local_step(x, tgt, comm):
    ws, saved = [], []
    w = comm.weights(0, None)
    for l in range(DEPTH):
        ws.append(w)
        x, sv, gathered = _layer_fwd(x, w, l, comm)
        saved.append(sv)
        if l + 1 < DEPTH:
            w = comm.weights(l + 1, gathered)
    sq, dx = loss_kernel(x, tgt, "loss")
    for l in reversed(range(DEPTH)):
        dx = _layer_bwd(dx, saved[l], ws[l], l, comm)
    comm.finish()
    return sq[0, 0], dx


def kernel(x, norm_mix_g, w_in, forget_b, q_norm_g, k_norm_g, w_attn_out, conv_w, w_conv_out, pool_w, pool_scale, w_o, norm_ffn_g, w_ffn_in, w_ffn_out, loss_target, m_norm_mix_g, m_w_in, m_forget_b, m_q_norm_g, m_k_norm_g, m_w_attn_out, m_conv_w, m_w_conv_out, m_pool_w, m_pool_scale, m_w_o, m_norm_ffn_g, m_w_ffn_in, m_w_ffn_out, v_norm_mix_g, v_w_in, v_forget_b, v_q_norm_g, v_k_norm_g, v_w_attn_out, v_conv_w, v_w_conv_out, v_pool_w, v_pool_scale, v_w_o, v_norm_ffn_g, v_w_ffn_in, v_w_ffn_out):
    w = dict(norm_mix_g=norm_mix_g, w_in=w_in, forget_b=forget_b, q_norm_g=q_norm_g, k_norm_g=k_norm_g,
             w_attn_out=w_attn_out, conv_w=conv_w, w_conv_out=w_conv_out, pool_w=pool_w, pool_scale=pool_scale,
             w_o=w_o, norm_ffn_g=norm_ffn_g, w_ffn_in=w_ffn_in, w_ffn_out=w_ffn_out)
    m = dict(norm_mix_g=m_norm_mix_g, w_in=m_w_in, forget_b=m_forget_b, q_norm_g=m_q_norm_g, k_norm_g=m_k_norm_g,
             w_attn_out=m_w_attn_out, conv_w=m_conv_w, w_conv_out=m_w_conv_out, pool_w=m_pool_w,
             pool_scale=m_pool_scale, w_o=m_w_o, norm_ffn_g=m_norm_ffn_g, w_ffn_in=m_w_ffn_in, w_ffn_out=m_w_ffn_out)
    v = dict(norm_mix_g=v_norm_mix_g, w_in=v_w_in, forget_b=v_forget_b, q_norm_g=v_q_norm_g, k_norm_g=v_k_norm_g,
             w_attn_out=v_w_attn_out, conv_w=v_conv_w, w_conv_out=v_w_conv_out, pool_w=v_pool_w,
             pool_scale=v_pool_scale, w_o=v_w_o, norm_ffn_g=v_norm_ffn_g, w_ffn_in=v_w_ffn_in, w_ffn_out=v_w_ffn_out)
    me = 4 * lax.axis_index("x") + 2 * lax.axis_index("y") + lax.axis_index("c")
    layer_shard = {n: SHARD_INFO[n][0][1:] for n in MATRICES}
    cut_axis = {n: SHARD_INFO[n][1] - 1 for n in MATRICES}

    vec = {n: w[n] for n in VECTORS}
    rc = {n: (_size(layer_shard[n][:-1]), layer_shard[n][-1]) for n in MATRICES}

    class Comm:
        bufs = [lax.empty((DEPTH, len(SAME_CORE)) + layer_shard[n], BF16) for n in MATRICES]
        blocks = [None] * DEPTH
        small_g = [None] * DEPTH
        conv_full = None

        @staticmethod
        def shards(l):
            return [_handled(n, w[n])[l].astype(BF16) for n in MATRICES]

        @staticmethod
        def gather_ici(l, part):
            return gather_over_ici([Comm.shards(l)[t] for t in part]) if l < DEPTH else None

        @staticmethod
        def gather_d2d(l, half):
            return gather_over_d2d(half) if l < DEPTH else None

        @staticmethod
        def weights(l, gathered):
            if l == 0:
                *gathered, conv_g = all_gather(Comm.shards(0) + [_pack([conv_w], 8, 128)], "gather_0")
                Comm.conv_full = _join_shards(jnp.stack([_unpack(conv_g[i], [conv_w.shape])[0] for i in range(N_DEV)]), 2)
            mats = {n: t if n == "w_in" else _join_shards(t, cut_axis[n]) for n, t in zip(MATRICES, gathered)}
            return _layer_weights(mats, vec, Comm.conv_full, l)

        @staticmethod
        def grads(l, g):
            Comm.small_g[l] = g
            Comm.blocks[l] = [None if n not in g else g[n] if n == "w_in" else _cut_shards(g[n], cut_axis[n])
                              for n in MATRICES]

        @staticmethod
        def early(l):
            return EARLY if l == 0 else None

        @staticmethod
        def scatter_d2d(l, which=EVERY):
            return scatter_over_d2d([Comm.blocks[l][t] for t in which]) if l < DEPTH else None

        @staticmethod
        def pair_sums(l, stage, which=EVERY):
            if l >= DEPTH:
                return None
            return [pair_sum(Comm.blocks[l][t].reshape((N_DEV,) + rc[MATRICES[t]]),
                             s.reshape((len(SAME_CORE),) + rc[MATRICES[t]]), me,
                             f"pair_sum_{MATRICES[t]}_{l}").reshape(s.shape) for t, s in zip(which, stage)]

        @staticmethod
        def scatter_ici(l, sums, which=EVERY):
            return scatter_over_ici(sums, [Comm.bufs[t] for t in which], l) if l < DEPTH else None

        @staticmethod
        def scattered(results, which=EVERY):
            for t, r in zip(which or (), results):
                Comm.bufs[t] = r

        @staticmethod
        def finish():
            stage = run_exchange(Comm.scatter_d2d(0, LATE), "scatter_d2d_0")
            Comm.scattered(run_exchange(Comm.scatter_ici(0, Comm.pair_sums(0, stage, LATE), LATE), "scatter_ici_0"), LATE)

    small_g, received = Comm.small_g, Comm
    sq, dx = _local_step(x[0], loss_target[0], Comm)

    big = {}
    for n, parts in zip(MATRICES, received.bufs):
        outs = adamw_sum(parts.reshape((DEPTH, len(SAME_CORE)) + rc[n]),
                         *[_handled(n, d[n]).reshape((DEPTH,) + rc[n]) for d in (w, m, v)], f"adamw_{n}")
        big[n] = [_handled(n, t.reshape((DEPTH,) + layer_shard[n])) for t in outs]

    small_shapes = [VECTOR_SHAPES[n] for n in VECTORS] + [CONV_W_FULL, (1,)]
    stacked = [jnp.stack([small_g[l][n] for l in range(DEPTH)]) for n in VECTORS + ("conv_w",)] + [sq.reshape(1)]
    sparts = all_gather([_pack(stacked, SMALL_ROWS, 128)], "gather_vector_grads")[0]
    col0 = me * (D_CONV // N_DEV)
    place = lambda t: lax.dynamic_update_slice(jnp.zeros(CONV_W_FULL, F32), t, (0, 0, col0))
    spacked = [_pack([d[n] for n in VECTORS] + [place(d["conv_w"]), jnp.zeros((1,), F32)], SMALL_ROWS, 128)[None]
               for d in (w, m, v)]
    small = [_unpack(t[0], small_shapes) for t in adamw_sum(sparts[None], *spacked, "adamw_vectors")]
    loss = (0.5 / D_MODEL) * small[0][-1][0]

    def result(kind):
        out = {n: big[n][kind] for n in MATRICES}
        out.update({n: small[kind][j] for j, n in enumerate(VECTORS)})
        out["conv_w"] = lax.dynamic_slice(small[kind][len(VECTORS)], (0, 0, col0), conv_w.shape)
        return [out[n] for n in w]

    return (loss, dx[None], *result(0), *result(1), *result(2), *result(3))
```

```python
import functools

import jax
import jax.numpy as jnp
from jax import lax
from jax.experimental import pallas as pl
from jax.experimental.pallas import tpu as pltpu

F32 = jnp.float32
BF16 = jnp.bfloat16

N_DEV = 8
DEPTH = 4
D_MODEL = 1024
HEAD_DIM = 64
HEADS = 8
D_ATTN = 512
D_CONV = 256
D_POOL = 256
D_FF = 2816
D_IN = 5640
EPS = 1e-6
ATTN_SCALE = HEAD_DIM ** -0.5

N_REST = 4096
N_MAIN = 5632
N_FULL = 5760
DPROJ_TAIL = 2048
DPROJ_COLS = N_REST + DPROJ_TAIL
FF_BLK = 256
N_FF_BLKS = D_FF // FF_BLK
HALO = 16

ADAM_LR = 0.001
ADAM_B1 = 0.9
ADAM_B2 = 0.999
ADAM_EPS = 1e-08
ADAM_WD = 0.01
ADAM_STEP = 10

SMALL_ROWS = 128

VMEM_LIMIT = 48 * 2 ** 20


def _cparams(sem, vmem=None):
    return pltpu.CompilerParams(dimension_semantics=sem, vmem_limit_bytes=vmem or VMEM_LIMIT)


def _pick(n, cands):
    for c in cands:
        if n % c == 0:
            return c
    raise ValueError(f"no tile for {n}")


def _tile(n, cap):
    t = min(cap, n)
    assert n % t == 0, (n, cap)
    return t


def _sigmoid(v):
    return 1.0 / (1.0 + jnp.exp(-v))


def _rstd(v):
    return lax.rsqrt(jnp.mean(v * v, axis=-1, keepdims=True) + EPS)


def _dot(a, b):
    return jnp.dot(a, b, preferred_element_type=F32)


def _dot_tn(a, b):
    return lax.dot_general(a, b, (((0,), (0,)), ((), ())), preferred_element_type=F32)


def _dot_nt(a, b):
    return lax.dot_general(a, b, (((1,), (1,)), ((), ())), preferred_element_type=F32)


def norm_matmul(x, g, wt, n_cols, name, ex=None):
    s, d = x.shape
    tm, tn = _tile(s, 1024), _pick(n_cols, (2816, 1408, 512))

    def body(x_ref, g_ref, w_ref, o_ref, h_ref):
        @pl.when(pl.program_id(1) == 0)
        def _():
            xv = x_ref[...]
            h_ref[...] = (xv * _rstd(xv) * g_ref[...]).astype(BF16)

        o_ref[...] = _dot_nt(h_ref[...], w_ref[...]).astype(BF16)

    return _carried_call(
        body, ex, (s // tm, n_cols // tn),
        [pl.BlockSpec((tm, d), lambda i, j: (i, 0)), pl.BlockSpec((1, d), lambda i, j: (0, 0)),
         pl.BlockSpec((tn, d), lambda i, j: (j, 0))],
        [pl.BlockSpec((tm, tn), lambda i, j: (i, j)), pl.BlockSpec((tm, d), lambda i, j: (i, 0))],
        [jax.ShapeDtypeStruct((s, n_cols), BF16), jax.ShapeDtypeStruct((s, d), BF16)], [],
        ("arbitrary", "arbitrary"), name, (x, g, wt))


def tn_matmul(a, b, name, m_cols=None, ex=None):
    t = a.shape[0]
    m = m_cols or a.shape[1]
    n = b.shape[1]
    tk = _tile(t, 1024)
    tmm = _pick(m, (1408, 1152, 1024, 512, 256))
    tn = _pick(n, (1408, 1152, 1024, 512, 128))
    nk = t // tk

    def body(a_ref, b_ref, o_ref, acc_ref):
        @pl.when(pl.program_id(2) == 0)
        def _():
            acc_ref[...] = jnp.zeros_like(acc_ref)

        acc_ref[...] += _dot_tn(a_ref[...].astype(BF16), b_ref[...].astype(BF16))

        @pl.when(pl.program_id(2) == nk - 1)
        def _():
            o_ref[...] = acc_ref[...].astype(BF16)

    if ex is None:
        return pl.pallas_call(
            body, grid=(m // tmm, n // tn, nk),
            in_specs=[pl.BlockSpec((tk, tmm), lambda i, j, k: (k, i)), pl.BlockSpec((tk, tn), lambda i, j, k: (k, j))],
            out_specs=pl.BlockSpec((tmm, tn), lambda i, j, k: (i, j)),
            out_shape=jax.ShapeDtypeStruct((m, n), BF16), scratch_shapes=[pltpu.VMEM((tmm, tn), F32)],
            compiler_params=_cparams(("parallel", "parallel", "arbitrary")), name=name)(a, b)
    (out,), carried = _carried_call(
        body, ex, (m // tmm, n // tn, nk),
        [pl.BlockSpec((tk, tmm), lambda i, j, k: (k, i)), pl.BlockSpec((tk, tn), lambda i, j, k: (k, j))],
        [pl.BlockSpec((tmm, tn), lambda i, j, k: (i, j))], [jax.ShapeDtypeStruct((m, n), BF16)],
        [pltpu.VMEM((tmm, tn), F32)], ("arbitrary", "arbitrary", "arbitrary"), name, (a, b))
    return out, carried


def tn_matmuls(pairs, name):
    t = pairs[0][0].shape[0]
    tk = _tile(t, 1024)
    nk = t // tk
    n = len(pairs)

    def body(*refs):
        ins, outs, accs = refs[:2 * n], refs[2 * n:3 * n], refs[3 * n:]

        @pl.when(pl.program_id(0) == 0)
        def _():
            for acc in accs:
                acc[...] = jnp.zeros_like(acc)

        for i in range(n):
            accs[i][...] += _dot_tn(ins[2 * i][...], ins[2 * i + 1][...])

        @pl.when(pl.program_id(0) == nk - 1)
        def _():
            for out, acc in zip(outs, accs):
                out[...] = acc[...].astype(BF16)

    shapes = [(a.shape[1], b.shape[1]) for a, b in pairs]
    return pl.pallas_call(
        body, grid=(nk,),
        in_specs=[pl.BlockSpec((tk, t_.shape[1]), lambda k: (k, 0)) for pair in pairs for t_ in pair],
        out_specs=[pl.BlockSpec(shp, lambda k: (0, 0)) for shp in shapes],
        out_shape=[jax.ShapeDtypeStruct(shp, BF16) for shp in shapes],
        scratch_shapes=[pltpu.VMEM(shp, F32) for shp in shapes],
        compiler_params=_cparams(("arbitrary",)), name=name)(*[t_ for pair in pairs for t_ in pair])


def matmul_normbwd(a, wt, x, g, dres, name, k=None, ex=None):
    s = a.shape[0]
    k = k or a.shape[1]
    d = wt.shape[1]
    tm = _tile(s, 1024)
    tk = _pick(k, (1408, 1152, 512))
    nk = k // tk

    def body(a_ref, w_ref, x_ref, g_ref, r_ref, dx_ref, dg_ref, acc_ref):
        i, kk = pl.program_id(0), pl.program_id(1)

        @pl.when(kk == 0)
        def _():
            acc_ref[...] = jnp.zeros_like(acc_ref)

        @pl.when((i == 0) & (kk == 0))
        def _():
            dg_ref[...] = jnp.zeros_like(dg_ref)

        acc_ref[...] += _dot(a_ref[...], w_ref[...])

        @pl.when(kk == nk - 1)
        def _():
            xv = x_ref[...]
            r = _rstd(xv)
            y = xv * r
            dh = acc_ref[...]
            dy = dh * g_ref[...]
            dx_ref[...] = r_ref[...] + r * (dy - y * jnp.mean(dy * y, axis=-1, keepdims=True))
            dg_ref[...] += jnp.sum(dh * y, axis=0, keepdims=True)

    return _carried_call(
        body, ex, (s // tm, nk),
        [pl.BlockSpec((tm, tk), lambda i, kk: (i, kk)), pl.BlockSpec((tk, d), lambda i, kk: (kk, 0)),
         pl.BlockSpec((tm, d), lambda i, kk: (i, 0)), pl.BlockSpec((1, d), lambda i, kk: (0, 0)),
         pl.BlockSpec((tm, d), lambda i, kk: (i, 0))],
        [pl.BlockSpec((tm, d), lambda i, kk: (i, 0)), pl.BlockSpec((1, d), lambda i, kk: (0, 0))],
        [jax.ShapeDtypeStruct((s, d), F32), jax.ShapeDtypeStruct((1, d), F32)],
        [pltpu.VMEM((tm, d), F32)], ("arbitrary", "arbitrary"), name, (a, wt, x, g, dres), vmem=56 * 2 ** 20)


def swiglu_matmul(gu, w, x1, name):
    s = gu.shape[0]
    d = w.shape[1]
    tm = _tile(s, 512)

    def body(gu_ref, w_ref, x_ref, o_ref):
        acc = x_ref[...]
        for j in range(N_FF_BLKS):
            gt = gu_ref[:, j * FF_BLK:(j + 1) * FF_BLK].astype(F32)
            up = gu_ref[:, D_FF + j * FF_BLK:D_FF + (j + 1) * FF_BLK].astype(F32)
            act = (gt * _sigmoid(gt) * up).astype(BF16)
            acc += _dot(act, w_ref[j * FF_BLK:(j + 1) * FF_BLK, :])
        o_ref[...] = acc

    return pl.pallas_call(
        body, grid=(s // tm,),
        in_specs=[pl.BlockSpec((tm, 2 * D_FF), lambda i: (i, 0)), pl.BlockSpec((D_FF, d), lambda i: (0, 0)),
                  pl.BlockSpec((tm, d), lambda i: (i, 0))],
        out_specs=pl.BlockSpec((tm, d), lambda i: (i, 0)),
        out_shape=jax.ShapeDtypeStruct((s, d), F32),
        compiler_params=_cparams(("parallel",)), name=name)(gu, w, x1)


def swiglu_bwd(dx2, gu, w, name, ex=None):
    s, d = dx2.shape
    tm = _tile(s, 512)

    def body(dx_ref, gu_ref, w_ref, dgu_ref, act_ref):
        dx = dx_ref[...].astype(BF16)
        for j in range(N_FF_BLKS):
            g_cols = slice(j * FF_BLK, (j + 1) * FF_BLK)
            u_cols = slice(D_FF + j * FF_BLK, D_FF + (j + 1) * FF_BLK)
            dact = _dot_nt(dx, w_ref[j * FF_BLK:(j + 1) * FF_BLK, :])
            gt = gu_ref[:, g_cols].astype(F32)
            up = gu_ref[:, u_cols].astype(F32)
            sg = _sigmoid(gt)
            silu = gt * sg
            act_ref[:, j * FF_BLK:(j + 1) * FF_BLK] = (silu * up).astype(BF16)
            dgu_ref[:, g_cols] = (dact * up * (sg + silu * (1.0 - sg))).astype(BF16)
            dgu_ref[:, u_cols] = (dact * silu).astype(BF16)

    return _carried_call(
        body, ex, (s // tm,),
        [pl.BlockSpec((tm, d), lambda i: (i, 0)), pl.BlockSpec((tm, 2 * D_FF), lambda i: (i, 0)),
         pl.BlockSpec((D_FF, d), lambda i: (0, 0), pipeline_mode=pl.Buffered(1))],
        [pl.BlockSpec((tm, 2 * D_FF), lambda i: (i, 0)), pl.BlockSpec((tm, D_FF), lambda i: (i, 0))],
        [jax.ShapeDtypeStruct((s, 2 * D_FF), BF16), jax.ShapeDtypeStruct((s, D_FF), BF16)], [],
        ("arbitrary",), name, (dx2, gu, w), vmem=56 * 2 ** 20)


def loss_kernel(y, tgt, name):
    s, d = y.shape
    tm = _tile(s, 512)

    def body(y_ref, t_ref, l_ref, dy_ref):
        @pl.when(pl.program_id(0) == 0)
        def _():
            l_ref[...] = jnp.zeros_like(l_ref)

        err = y_ref[...] - t_ref[...]
        dy_ref[...] = err * (1.0 / d)
        l_ref[...] += jnp.sum(jnp.sum(err * err, axis=1, keepdims=True), axis=0, keepdims=True)

    return pl.pallas_call(
        body, grid=(s // tm,),
        in_specs=[pl.BlockSpec((tm, d), lambda i: (i, 0)), pl.BlockSpec((tm, d), lambda i: (i, 0))],
        out_specs=[pl.BlockSpec((8, 128), lambda i: (0, 0)), pl.BlockSpec((tm, d), lambda i: (i, 0))],
        out_shape=[jax.ShapeDtypeStruct((8, 128), F32), jax.ShapeDtypeStruct((s, d), F32)],
        compiler_params=_cparams(("arbitrary",)), name=name)(y, tgt)


def _split3(v):
    a1 = v.astype(BF16)
    r1 = v - a1.astype(F32)
    a2 = r1.astype(BF16)
    a3 = (r1 - a2.astype(F32)).astype(BF16)
    return a1, a2, a3


def _running_sum(v, carry_ref, reverse):
    tm = v.shape[0]
    row = lax.broadcasted_iota(jnp.int32, (tm, tm), 0)
    col = lax.broadcasted_iota(jnp.int32, (tm, tm), 1)
    tri = ((col >= row) if reverse else (row >= col)).astype(BF16)
    a1, a2, a3 = _split3(v)
    out = _dot(tri, a1) + _dot(tri, a2) + _dot(tri, a3) + carry_ref[...]
    carry_ref[...] = out[0:1, :] if reverse else out[tm - 1:tm, :]
    return out


HEAD_GROUP_FWD = 8
HEAD_GROUP_BWD = 8
LANE_C = 64
LANE_ONE = 67


def _lanes():
    lane = lax.broadcasted_iota(jnp.int32, (1, 128), 1)
    return lane, lane < HEAD_DIM


def _half_mean(t, lo):
    s_lo = jnp.sum(jnp.where(lo, t, 0.0), axis=-1, keepdims=True)
    s_hi = jnp.sum(jnp.where(lo, 0.0, t), axis=-1, keepdims=True)
    return jnp.where(lo, s_lo, s_hi) * (1.0 / HEAD_DIM)


def _lane_col(t, lane, idx):
    return jnp.sum(jnp.where(lane == idx, t, 0.0), axis=-1, keepdims=True)


def _swap_halves(t):
    return pltpu.roll(t, HEAD_DIM, 1)


def attn_prep(proj, h, wt_in, fb, gq2, gk2, name):
    s, d = h.shape
    tm = _tile(s, 512)
    first = N_REST // D_ATTN

    def body(q_ref, k_ref, v_ref, h_ref, wf_ref, fb_ref, gq_ref, gk_ref, qa_ref, ka_ref, va_ref, vt_ref, z_ref, carry_ref):
        lane, lo = _lanes()

        @pl.when(pl.program_id(0) == 0)
        def _():
            carry_ref[...] = jnp.zeros_like(carry_ref)

        z = _dot_nt(h_ref[...], wf_ref[...]) + fb_ref[...]
        z_ref[...] = z
        cv = _running_sum(jnp.minimum(z, 0.0) - jnp.log(1.0 + jnp.exp(-jnp.abs(z))), carry_ref, reverse=False)

        def normed(t, g):
            t = t.astype(F32)
            return t * lax.rsqrt(_half_mean(t * t, lo) + EPS) * g

        one_q = jnp.where((lane >= LANE_ONE) & (lane < LANE_ONE + 3), 1.0, 0.0)
        one_k = jnp.where((lane >= LANE_C) & (lane < LANE_C + 3), 1.0, 0.0)
        one_v = jnp.where(lane == LANE_C, 1.0, 0.0)
        for j in range(HEADS // 2):
            cols = slice(128 * j, 128 * (j + 1))
            qn = normed(q_ref[:, cols], gq_ref[...] * ATTN_SCALE)
            kn = normed(k_ref[:, cols], gk_ref[...])
            vv = v_ref[:, cols].astype(F32)
            for e in range(2):
                h = 2 * j + e
                pick = (lambda t: t) if e == 0 else _swap_halves
                pieces = [p.astype(F32) for p in _split3(_lane_col(cv, lane, h))]
                ext_q, ext_k = one_q, one_k
                for i, p in enumerate(pieces):
                    ext_q = jnp.where(lane == LANE_C + i, p, ext_q)
                    ext_k = jnp.where(lane == LANE_ONE + i, -p, ext_k)
                qa_ref[h] = jnp.where(lo, pick(qn), ext_q).astype(BF16)
                ka_ref[h] = jnp.where(lo, pick(kn), ext_k).astype(BF16)
                va = jnp.where(lo, pick(vv), one_v)
                va_ref[h] = va.astype(BF16)
                vt_ref[h] = va.T.astype(BF16)

    tile = lambda blk: pl.BlockSpec((tm, D_ATTN), lambda i: (i, blk))
    vec = pl.BlockSpec((1, 128), lambda i: (0, 0))
    out = pl.BlockSpec((HEADS, tm, 128), lambda i: (0, i, 0))
    return pl.pallas_call(
        body, grid=(s // tm,),
        in_specs=[tile(first), tile(first + 1), tile(first + 2), pl.BlockSpec((tm, d), lambda i: (i, 0)),
                  pl.BlockSpec((128, d), lambda i: (N_MAIN // 128, 0)), vec, vec, vec],
        out_specs=[out, out, out, pl.BlockSpec((HEADS, 128, tm), lambda i: (0, 0, i)),
                   pl.BlockSpec((tm, 128), lambda i: (i, 0))],
        out_shape=[jax.ShapeDtypeStruct((HEADS, s, 128), BF16)] * 3 + [jax.ShapeDtypeStruct((HEADS, 128, s), BF16),
                                                                       jax.ShapeDtypeStruct((s, 128), F32)],
        scratch_shapes=[pltpu.VMEM((1, 128), F32)],
        compiler_params=_cparams(("arbitrary",)), name=name)(proj, proj, proj, h, wt_in, fb, gq2, gk2)


def _carry(ex, n_in, n_out, n_scratch, grid):
    n_xin, n_xout = (len(ex.inputs), len(ex.out_shapes)) if ex else (0, 0)

    def split(refs):
        ins, xins = refs[:n_in], refs[n_in:n_in + n_xin]
        rest = refs[n_in + n_xin:]
        outs, xouts = rest[:n_out], rest[n_out:n_out + n_xout]
        rest = rest[n_out + n_xout:]
        return ins + outs + rest[:n_scratch], (xins, xouts, rest[n_scratch:])

    def first():
        return functools.reduce(lambda a, b: a & b, [pl.program_id(d) == 0 for d in range(len(grid))])

    def last():
        return functools.reduce(lambda a, b: a & b, [pl.program_id(d) == grid[d] - 1 for d in range(len(grid))])

    return split, first, last


def _carried_call(body, ex, grid, in_specs, out_specs, out_shape, scratch, sem, name, operands, vmem=None):
    any_spec = pl.BlockSpec(memory_space=pl.ANY)
    split, first, last = _carry(ex, len(in_specs), len(out_specs), len(scratch), grid)

    def carried(*refs):
        own, xrefs = split(refs)
        if ex:
            @pl.when(first())
            def _():
                ex.start(*xrefs)

        body(*own)
        if ex:
            @pl.when(last())
            def _():
                ex.drain(*xrefs)

    n_xin = len(ex.inputs) if ex else 0
    results = pl.pallas_call(
        carried, grid=grid, in_specs=list(in_specs) + [any_spec] * n_xin,
        out_specs=list(out_specs) + [any_spec] * (len(ex.out_shapes) if ex else 0),
        out_shape=list(out_shape) + (list(ex.out_shapes) if ex else []),
        input_output_aliases={len(in_specs) + i: len(out_specs) + o for i, o in ex.aliases.items()} if ex else {},
        scratch_shapes=list(scratch) + (ex.scratch if ex else []),
        compiler_params=_cparams(sem, vmem), name=name)(*operands, *(ex.inputs if ex else []))
    return results[:len(out_specs)], results[len(out_specs):]


def _tri_rows(t, n):
    qi = sum(jnp.where(t >= r * (r + 1) // 2, 1, 0) for r in range(1, n))
    return qi, t - qi * (qi + 1) // 2


def _tri_cols(t, n):
    ki = sum(jnp.where(t >= r * n - r * (r - 1) // 2, 1, 0) for r in range(1, n))
    return ki, ki + t - (ki * n - ki * (ki - 1) // 2)


def _causal_t(st_blk, tk, tq):
    key = lax.broadcasted_iota(jnp.int32, (tk, tq), 0)
    qry = lax.broadcasted_iota(jnp.int32, (tk, tq), 1)
    return jnp.where(qry >= key, st_blk, -jnp.inf)


def attn_forward(qa, ka, vt, name, ex=None):
    hh, s, _ = qa.shape
    tq = tk = _tile(s, 512)
    nq = s // tq
    grp = HEAD_GROUP_FWD

    def body(q_ref, k_ref, vt_ref, o_ref, lse_ref, m_ref, acc_ref):
        qi, ki = _tri_rows(pl.program_id(1), nq)

        @pl.when(ki == 0)
        def _():
            m_ref[...] = jnp.full_like(m_ref, -jnp.inf)
            acc_ref[...] = jnp.zeros_like(acc_ref)

        def step(masked):
            nxt = _dot_nt(k_ref[0], q_ref[0])
            for g in range(grp):
                st = nxt
                if g + 1 < grp:
                    nxt = _dot_nt(k_ref[g + 1], q_ref[g + 1])
                if masked:
                    st = _causal_t(st, tk, tq)
                m_old = m_ref[g]
                m_new = jnp.maximum(m_old, jnp.max(st, axis=0, keepdims=True))
                pt = jnp.exp(st - m_new).astype(BF16)
                acc_ref[g] = jnp.exp(m_old - m_new) * acc_ref[g] + _dot(vt_ref[g], pt)
                m_ref[g] = m_new

        @pl.when(ki < qi)
        def _():
            step(False)

        @pl.when(ki == qi)
        def _():
            step(True)
            for g in range(grp):
                acc = acc_ref[g]
                denom = acc[LANE_C:LANE_C + 1, :]
                o_ref[g] = (acc / denom).T.astype(BF16)
                lse_ref[g] = m_ref[g] + jnp.log(denom)

    qspec = pl.BlockSpec((grp, tq, 128), lambda h, t: (h, _tri_rows(t, nq)[0], 0))
    kspec = pl.BlockSpec((grp, tk, 128), lambda h, t: (h, _tri_rows(t, nq)[1], 0))
    vspec = pl.BlockSpec((grp, 128, tk), lambda h, t: (h, 0, _tri_rows(t, nq)[1]))
    lspec = pl.BlockSpec((grp, 1, tq), lambda h, t: (h, 0, _tri_rows(t, nq)[0]))
    return _carried_call(
        body, ex, (hh // grp, nq * (nq + 1) // 2), [qspec, kspec, vspec], [qspec, lspec],
        [jax.ShapeDtypeStruct((hh, s, 128), BF16), jax.ShapeDtypeStruct((hh, 1, s), F32)],
        [pltpu.VMEM((grp, 1, tq), F32), pltpu.VMEM((grp, 128, tq), F32)],
        ("arbitrary", "arbitrary"), name, (qa, ka, vt))


def attn_backward(qa, ka, va, oa, doa, lse, name, ex=None):
    hh, s, _ = qa.shape
    tq = tk = _tile(s, 512)
    nq = s // tq
    grp = HEAD_GROUP_BWD

    def body(q_ref, k_ref, v_ref, o_ref, do_ref, lse_ref, dq_ref, dk_ref, dv_ref, dka_ref, dva_ref):
        ki, qi = _tri_cols(pl.program_id(1), nq)

        @pl.when(pl.program_id(1) == 0)
        def _():
            dq_ref[...] = jnp.zeros_like(dq_ref)

        @pl.when(qi == ki)
        def _():
            dka_ref[...] = jnp.zeros_like(dka_ref)
            dva_ref[...] = jnp.zeros_like(dva_ref)

        def step(masked):
            rows = pl.ds(pl.multiple_of(qi * tq, tq), tq)
            products = lambda g: (_dot_nt(k_ref[g], q_ref[g]), _dot_nt(v_ref[g], do_ref[g]))
            nxt = products(0)
            for g in range(grp):
                st, dpt = nxt
                if g + 1 < grp:
                    nxt = products(g + 1)
                q, k, do = q_ref[g], k_ref[g], do_ref[g]
                if masked:
                    st = _causal_t(st, tk, tq)
                pt = jnp.exp(st - lse_ref[g])
                delta = jnp.sum((do.astype(F32) * o_ref[g].astype(F32)).T, axis=0, keepdims=True)
                dst = (pt * (dpt - delta)).astype(BF16)
                dva_ref[g] += _dot(pt.astype(BF16), do)
                dka_ref[g] += _dot(dst, q)
                dq_ref[g, rows, :] += _dot_tn(dst, k)

        @pl.when(qi > ki)
        def _():
            step(False)

        @pl.when(qi == ki)
        def _():
            step(True)

        @pl.when(qi == nq - 1)
        def _():
            dk_ref[...] = dka_ref[...]
            dv_ref[...] = dva_ref[...].astype(BF16)

    qspec = pl.BlockSpec((grp, tq, 128), lambda h, t: (h, _tri_cols(t, nq)[1], 0))
    lspec = pl.BlockSpec((grp, 1, tq), lambda h, t: (h, 0, _tri_cols(t, nq)[1]))
    kspec = pl.BlockSpec((grp, tk, 128), lambda h, t: (h, _tri_cols(t, nq)[0], 0))
    return _carried_call(
        body, ex, (hh // grp, nq * (nq + 1) // 2), [qspec, kspec, kspec, qspec, qspec, lspec],
        [pl.BlockSpec((grp, s, 128), lambda h, t: (h, 0, 0), pipeline_mode=pl.Buffered(1)), kspec, kspec],
        [jax.ShapeDtypeStruct((hh, s, 128), F32), jax.ShapeDtypeStruct((hh, s, 128), F32),
         jax.ShapeDtypeStruct((hh, s, 128), BF16)],
        [pltpu.VMEM((grp, tk, 128), F32), pltpu.VMEM((grp, tk, 128), F32)],
        ("arbitrary", "arbitrary"), name, (qa, ka, va, oa, doa, lse), vmem=58 * 2 ** 20)


def attn_post(dqa, dka, dva, proj, z, gq2, gk2, dproj, name):
    s = proj.shape[0]
    tm = _tile(s, 512)
    nt = s // tm

    def body(dq_ref, dk_ref, dv_ref, q_ref, k_ref, z_ref, gq_ref, gk_ref, dp_any, dp_ref, dgq_ref, dgk_ref, db_ref,
             carry_ref):
        lane, lo = _lanes()

        @pl.when(pl.program_id(0) == 0)
        def _():
            dgq_ref[...] = jnp.zeros_like(dgq_ref)
            dgk_ref[...] = jnp.zeros_like(dgk_ref)
            db_ref[...] = jnp.zeros_like(db_ref)
            carry_ref[...] = jnp.zeros_like(carry_ref)

        def pair(ref, j):
            return jnp.where(lo, ref[2 * j].astype(F32), _swap_halves(ref[2 * j + 1].astype(F32)))

        def norm_bwd(raw, g, dhat, scale):
            r = lax.rsqrt(_half_mean(raw * raw, lo) + EPS)
            y = raw * r
            dy = dhat * (g * scale)
            return r * (dy - y * _half_mean(dy * y, lo)), jnp.sum(dhat * y, axis=0, keepdims=True) * scale

        dc = jnp.zeros((tm, 128), F32)
        for j in range(HEADS // 2):
            cols = slice(128 * j, 128 * (j + 1))
            dq, dgq = norm_bwd(q_ref[:, cols].astype(F32), gq_ref[...], pair(dq_ref, j), ATTN_SCALE)
            dk, dgk = norm_bwd(k_ref[:, cols].astype(F32), gk_ref[...], pair(dk_ref, j), 1.0)
            dgq_ref[...] += dgq
            dgk_ref[...] += dgk
            dp_ref[:, cols] = dq.astype(BF16)
            dp_ref[:, D_ATTN + 128 * j:D_ATTN + 128 * (j + 1)] = dk.astype(BF16)
            dp_ref[:, 2 * D_ATTN + 128 * j:2 * D_ATTN + 128 * (j + 1)] = pair(dv_ref, j).astype(BF16)
            for e in range(2):
                h = 2 * j + e
                both = jnp.where(lane == LANE_C, dq_ref[h], 0.0) - jnp.where(lane == LANE_ONE, dk_ref[h], 0.0)
                dc = jnp.where(lane == h, jnp.sum(both, axis=-1, keepdims=True), dc)
        dz = _running_sum(dc, carry_ref, reverse=True) * (1.0 - _sigmoid(z_ref[...]))
        db_ref[...] += jnp.sum(dz, axis=0, keepdims=True)
        dp_ref[:, 3 * D_ATTN:3 * D_ATTN + 128] = dz.astype(BF16)
        dp_ref[:, 3 * D_ATTN + 128:] = jnp.zeros((tm, DPROJ_TAIL - 3 * D_ATTN - 128), BF16)

    heads = lambda: pl.BlockSpec((HEADS, tm, 128), lambda i: (0, nt - 1 - i, 0))
    vec = pl.BlockSpec((1, 128), lambda i: (0, 0))
    first = N_REST // D_ATTN
    return pl.pallas_call(
        body, grid=(nt,),
        in_specs=[heads(), heads(), heads(), pl.BlockSpec((tm, D_ATTN), lambda i: (nt - 1 - i, first)),
                  pl.BlockSpec((tm, D_ATTN), lambda i: (nt - 1 - i, first + 1)),
                  pl.BlockSpec((tm, 128), lambda i: (nt - 1 - i, 0)), vec, vec, pl.BlockSpec(memory_space=pl.ANY)],
        out_specs=[pl.BlockSpec((tm, DPROJ_TAIL), lambda i: (nt - 1 - i, N_REST // DPROJ_TAIL)), vec, vec, vec],
        out_shape=[jax.ShapeDtypeStruct(dproj.shape, BF16), jax.ShapeDtypeStruct((1, 128), F32),
                   jax.ShapeDtypeStruct((1, 128), F32), jax.ShapeDtypeStruct((1, 128), F32)],
        scratch_shapes=[pltpu.VMEM((1, 128), F32)], input_output_aliases={8: 0},
        compiler_params=_cparams(("arbitrary",)), name=name)(dqa, dka, dva, proj, proj, z, gq2, gk2, dproj)


def _pool_groups(tm):
    gid = lax.broadcasted_iota(jnp.int32, (1, D_POOL), 1) // (D_POOL // 4)
    win = jnp.where(gid == 0, 2.0, jnp.where(gid == 1, 4.0, jnp.where(gid == 2, 8.0, 16.0)))
    return gid, win


def _by_group(gid, v2, v4, v8, v16):
    return jnp.where(gid == 0, v2, jnp.where(gid == 1, v4, jnp.where(gid == 2, v8, v16)))


def _branches(rest_ref, halo_ref, a_ref, wa_ref, wc_ref, wp_ref, sc_ref, cw_ref, ti, tm):
    f = lambda v: v.astype(F32)
    cx, cb, cc, px = f(rest_ref[:, 0:256]), f(rest_ref[:, 256:512]), f(rest_ref[:, 512:768]), f(rest_ref[:, 768:1024])
    live = jnp.where(ti > 0, 1.0, 0.0)
    hz = f(halo_ref[:, 0:256]) * f(halo_ref[:, 512:768]) * live
    hp = f(halo_ref[:, 768:1024]) * live
    z = cc * cx
    zf = jnp.concatenate([hz, z], axis=0)
    z1 = pltpu.roll(zf, 1, 0)[HALO:]
    z2 = pltpu.roll(zf, 2, 0)[HALO:]
    cw = cw_ref[...]
    conv = cw[2:3] * z + cw[1:2] * z1 + cw[0:1] * z2
    uc = cb * conv
    pf = jnp.concatenate([hp, px], axis=0)
    s2 = pf + pltpu.roll(pf, 1, 0)
    s4 = s2 + pltpu.roll(s2, 2, 0)
    s8 = s4 + pltpu.roll(s4, 4, 0)
    s16 = s8 + pltpu.roll(s8, 8, 0)
    gid, win = _pool_groups(tm)
    t = (ti * tm + lax.broadcasted_iota(jnp.int32, (tm, 1), 0)).astype(F32)
    inv = 1.0 / jnp.minimum(t + 1.0, win)
    dpool = _by_group(gid, s2[HALO:], s4[HALO:], s8[HALO:], s16[HALO:]) * inv - px
    _, lo = _lanes()
    a_tok = [jnp.where(lo, f(a_ref[2 * j]), _swap_halves(f(a_ref[2 * j + 1]))).astype(BF16) for j in range(HEADS // 2)]
    y_attn = _dot(a_tok[0], wa_ref[0:128, :])
    for j in range(1, HEADS // 2):
        y_attn += _dot(a_tok[j], wa_ref[128 * j:128 * (j + 1), :])
    y_conv = _dot(uc.astype(BF16), wc_ref[...])
    y_pool_raw = _dot(dpool.astype(BF16), wp_ref[...])
    sg = [_sigmoid(f(rest_ref[:, 1024 + i * D_MODEL:1024 + (i + 1) * D_MODEL])) for i in range(3)]
    return dict(cx=cx, cb=cb, cc=cc, z=z, z1=z1, z2=z2, conv=conv, uc=uc, dpool=dpool, inv=inv, gid=gid, a_tok=a_tok,
                y_attn=y_attn, y_conv=y_conv, y_pool_raw=y_pool_raw, sg=sg, cw=cw)


def _mix_specs(tm, ti_of):
    blocks_per_tile = tm // HALO
    return [
        pl.BlockSpec((tm, N_REST), lambda i: (ti_of(i), 0)),
        pl.BlockSpec((HALO, 1024), lambda i: (jnp.maximum(ti_of(i) * blocks_per_tile - 1, 0), 0)),
        pl.BlockSpec((HEADS, tm, 128), lambda i: (0, ti_of(i), 0)),
        pl.BlockSpec((D_ATTN, D_MODEL), lambda i: (0, 0), pipeline_mode=pl.Buffered(1)),
        pl.BlockSpec((D_CONV, D_MODEL), lambda i: (0, 0), pipeline_mode=pl.Buffered(1)),
        pl.BlockSpec((D_POOL, D_MODEL), lambda i: (0, 0), pipeline_mode=pl.Buffered(1)),
        pl.BlockSpec((1, D_MODEL), lambda i: (0, 0)),
        pl.BlockSpec((8, D_CONV), lambda i: (0, 0)),
    ]


def mix_fwd(proj, a, x, wa, wc, wp, scale, cw, wo, name, ex=None):
    s = x.shape[0]
    tm = _tile(s, 512)

    def body(rest_ref, halo_ref, a_ref, wa_ref, wc_ref, wp_ref, sc_ref, cw_ref, wo_ref, x_ref, o_ref):
        b = _branches(rest_ref, halo_ref, a_ref, wa_ref, wc_ref, wp_ref, sc_ref, cw_ref, pl.program_id(0), tm)
        merged = b["sg"][0] * b["y_attn"] + b["sg"][1] * b["y_conv"] + b["sg"][2] * (b["y_pool_raw"] * sc_ref[...])
        o_ref[...] = x_ref[...] + _dot(merged.astype(BF16), wo_ref[...])

    (x1,), carried = _carried_call(
        body, ex, (s // tm,),
        _mix_specs(tm, lambda i: i) + [pl.BlockSpec((D_MODEL, D_MODEL), lambda i: (0, 0), pipeline_mode=pl.Buffered(1)),
                                       pl.BlockSpec((tm, D_MODEL), lambda i: (i, 0))],
        [pl.BlockSpec((tm, D_MODEL), lambda i: (i, 0))], [jax.ShapeDtypeStruct((s, D_MODEL), F32)], [],
        ("arbitrary",), name, (proj, proj, a, wa, wc, wp, scale, cw, wo, x), vmem=58 * 2 ** 20)
    return x1, carried


def mix_bwd(proj, a, dx1, wa, wc, wp, scale, cw, wo, name):
    s = dx1.shape[0]
    tm = _tile(s, 512)
    nt = s // tm
    ti_of = lambda i: nt - 1 - i
    n = tm + HALO

    def body(rest_ref, halo_ref, a_ref, wa_ref, wc_ref, wp_ref, sc_ref, cw_ref, wo_ref,
             dx_ref, dp_ref, da_ref, at_ref, mg_ref, dya_ref, dyc_ref, dyp_ref, uc_ref, dd_ref, dsc_ref, dcw_ref,
             cdc_ref, cde_ref):
        i = pl.program_id(0)
        ti = ti_of(i)

        @pl.when(i == 0)
        def _():
            cdc_ref[...] = jnp.zeros_like(cdc_ref)
            cde_ref[...] = jnp.zeros_like(cde_ref)
            dsc_ref[...] = jnp.zeros_like(dsc_ref)
            dcw_ref[...] = jnp.zeros_like(dcw_ref)

        b = _branches(rest_ref, halo_ref, a_ref, wa_ref, wc_ref, wp_ref, sc_ref, cw_ref, ti, tm)
        sg, sc = b["sg"], sc_ref[...]
        y_pool = b["y_pool_raw"] * sc
        merged = sg[0] * b["y_attn"] + sg[1] * b["y_conv"] + sg[2] * y_pool
        mg_ref[...] = merged.astype(BF16)
        dm = _dot_nt(dx_ref[...].astype(BF16), wo_ref[...])
        dys = [dm * sg[j] for j in range(3)]
        for j, y in enumerate((b["y_attn"], b["y_conv"], y_pool)):
            dp_ref[:, 1024 + j * D_MODEL:1024 + (j + 1) * D_MODEL] = (dys[j] * y * (1.0 - sg[j])).astype(BF16)
        dya = dys[0].astype(BF16)
        dya_ref[...] = dya
        _, lo = _lanes()
        for j in range(HEADS // 2):
            at_ref[:, 128 * j:128 * (j + 1)] = b["a_tok"][j]
            da = _dot_nt(dya, wa_ref[128 * j:128 * (j + 1), :])
            da_ref[2 * j] = jnp.where(lo, da, 0.0).astype(BF16)
            da_ref[2 * j + 1] = jnp.where(lo, _swap_halves(da), 0.0).astype(BF16)
        dyc = dys[1].astype(BF16)
        dyc_ref[...] = dyc
        duc = _dot_nt(dyc, wc_ref[...])
        dyp = dys[2]
        dsc_ref[...] += jnp.sum(dyp * b["y_pool_raw"], axis=0, keepdims=True)
        dypr = (dyp * sc).astype(BF16)
        dyp_ref[...] = dypr
        ddp = _dot_nt(dypr, wp_ref[...])
        uc_ref[...] = b["uc"].astype(BF16)
        dd_ref[...] = b["dpool"].astype(BF16)

        dconv = duc * b["cb"]
        dp_ref[:, 256:512] = (duc * b["conv"]).astype(BF16)
        dcf = jnp.concatenate([dconv, cdc_ref[...]], axis=0)
        cw = b["cw"]
        dz = cw[2:3] * dconv + cw[1:2] * pltpu.roll(dcf, n - 1, 0)[:tm] + cw[0:1] * pltpu.roll(dcf, n - 2, 0)[:tm]
        dp_ref[:, 0:256] = (dz * b["cc"]).astype(BF16)
        dp_ref[:, 512:768] = (dz * b["cx"]).astype(BF16)
        dcw_ref[0:1, :] += jnp.sum(dconv * b["z2"], axis=0, keepdims=True)
        dcw_ref[1:2, :] += jnp.sum(dconv * b["z1"], axis=0, keepdims=True)
        dcw_ref[2:3, :] += jnp.sum(dconv * b["z"], axis=0, keepdims=True)
        cdc_ref[...] = dconv[:HALO]

        e = ddp * b["inv"]
        ef = jnp.concatenate([e, cde_ref[...]], axis=0)
        r2 = ef + pltpu.roll(ef, n - 1, 0)
        r4 = r2 + pltpu.roll(r2, n - 2, 0)
        r8 = r4 + pltpu.roll(r4, n - 4, 0)
        r16 = r8 + pltpu.roll(r8, n - 8, 0)
        dp_ref[:, 768:1024] = (_by_group(b["gid"], r2[:tm], r4[:tm], r8[:tm], r16[:tm]) - ddp).astype(BF16)
        cde_ref[...] = e[:HALO]

    tile = lambda w: pl.BlockSpec((tm, w), lambda i: (ti_of(i), 0))
    whole = lambda r, c: pl.BlockSpec((r, c), lambda i: (0, 0))
    bf = lambda w: jax.ShapeDtypeStruct((s, w), BF16)
    return pl.pallas_call(
        body, grid=(nt,),
        in_specs=_mix_specs(tm, ti_of) + [pl.BlockSpec((D_MODEL, D_MODEL), lambda i: (0, 0), pipeline_mode=pl.Buffered(1)),
                                          tile(D_MODEL)],
        out_specs=[tile(N_REST), pl.BlockSpec((HEADS, tm, 128), lambda i: (0, ti_of(i), 0)), tile(D_ATTN),
                   tile(D_MODEL), tile(D_MODEL), tile(D_MODEL), tile(D_MODEL),
                   tile(D_CONV), tile(D_POOL), whole(1, D_MODEL), whole(8, D_CONV)],
        out_shape=[bf(DPROJ_COLS), jax.ShapeDtypeStruct((HEADS, s, 128), BF16), bf(D_ATTN),
                   bf(D_MODEL), bf(D_MODEL), bf(D_MODEL), bf(D_MODEL), bf(D_CONV), bf(D_POOL),
                   jax.ShapeDtypeStruct((1, D_MODEL), F32), jax.ShapeDtypeStruct((8, D_CONV), F32)],
        scratch_shapes=[pltpu.VMEM((HALO, D_CONV), F32), pltpu.VMEM((HALO, D_POOL), F32)],
        compiler_params=_cparams(("arbitrary",), 58 * 2 ** 20), name=name)(proj, proj, a, wa, wc, wp, scale, cw, wo, dx1)


def _adamw_math(w, g, m, v):
    m = ADAM_B1 * m + (1.0 - ADAM_B1) * g
    v = ADAM_B2 * v + (1.0 - ADAM_B2) * (g * g)
    m_hat = m / (1.0 - ADAM_B1 ** ADAM_STEP)
    v_hat = v / (1.0 - ADAM_B2 ** ADAM_STEP)
    delta = -ADAM_LR * (m_hat / (jnp.sqrt(v_hat) + ADAM_EPS) + ADAM_WD * w)
    return delta, m, v


ADAMW_PARTS_BLOCK_BYTES = 4 * 2 ** 20


def _row_tile(rows, cols, copies, itemsize):
    row_bytes = copies * (-(-cols // 128) * 128) * itemsize
    fits = [t for t in range(16, rows + 1, 16) if rows % t == 0 and t * row_bytes <= ADAMW_PARTS_BLOCK_BYTES]
    return max(fits) if fits else rows


def pair_sum(blocks, stage, me, name):
    n_slots, rows, cols = stage.shape
    tr = _row_tile(rows, cols, 1, 4)

    def body(me_ref, a_ref, b_ref, o_ref):
        o_ref[...] = (a_ref[...].astype(F32) + b_ref[...].astype(F32)).astype(BF16)

    slot = pl.BlockSpec((None, tr, cols), lambda i, r, me_ref: (i, r, 0))
    return pl.pallas_call(
        body, out_shape=jax.ShapeDtypeStruct(stage.shape, BF16),
        grid_spec=pltpu.PrefetchScalarGridSpec(
            num_scalar_prefetch=1, grid=(n_slots, rows // tr),
            in_specs=[pl.BlockSpec((None, tr, cols), lambda i, r, me_ref: (me_ref[0] ^ (2 * i), r, 0)), slot],
            out_specs=slot),
        compiler_params=_cparams(("parallel", "parallel")), name=name)(me.reshape(1), blocks, stage)


def adamw_sum(parts, w, m, v, name):
    layers, rows, cols = w.shape
    n_parts = parts.shape[1]
    if rows % 16 == 0:
        tr, tc = _row_tile(rows, cols, n_parts, parts.dtype.itemsize), cols
    else:
        tr, tc = rows, _pick(cols, (256, 128))

    def body(p_ref, w_ref, m_ref, v_ref, g_ref, d_ref, nm_ref, nv_ref):
        g = p_ref[0].astype(F32)
        for i in range(1, n_parts):
            g = g + p_ref[i].astype(F32)
        g_ref[...] = g
        d_ref[...], nm_ref[...], nv_ref[...] = _adamw_math(w_ref[...], g, m_ref[...], v_ref[...])

    spec = pl.BlockSpec((None, tr, tc), lambda l, i, j: (l, i, j))
    return pl.pallas_call(
        body, grid=(layers, rows // tr, cols // tc),
        in_specs=[pl.BlockSpec((None, n_parts, tr, tc), lambda l, i, j: (l, 0, i, j)), spec, spec, spec],
        out_specs=[spec] * 4, out_shape=[jax.ShapeDtypeStruct((layers, rows, cols), F32)] * 4,
        compiler_params=_cparams(("parallel", "parallel", "parallel")), name=name)(parts, w, m, v)


def _me():
    return lax.axis_index("x"), lax.axis_index("y"), lax.axis_index("c")


N_PEERS = N_DEV - 1


def all_gather(shards, name):
    n = len(shards)
    any_spec = pl.BlockSpec(memory_space=pl.ANY)

    def body(*refs):
        x_refs, out_refs = refs[:n], refs[n:2 * n]
        send_sems, recv_sems, local_sems = refs[2 * n:]
        x, y, c = _me()
        me, sibling = (x, y, c), (x, y, 1 - c)
        chips = [(1 - x, y), (x, 1 - y), (1 - x, 1 - y)]

        def copy(t, k, block, to, from_input=False):
            slot = out_refs[t].at[4 * block[0] + 2 * block[1] + block[2]]
            return pltpu.make_async_remote_copy(
                src_ref=x_refs[t] if from_input else slot, dst_ref=slot, send_sem=send_sems.at[N_PEERS * t + k],
                recv_sem=recv_sems.at[N_PEERS * t + k], device_id=to, device_id_type=pl.DeviceIdType.MESH)

        mine = [pltpu.make_async_copy(x_refs[t], out_refs[t].at[4 * x + 2 * y + c], local_sems.at[t]) for t in range(n)]
        started = []
        for t in range(n):
            mine[t].start()
            started.append(copy(t, 0, me, sibling, from_input=True))
            started += [copy(t, 1 + j, me, (*chip, c), from_input=True) for j, chip in enumerate(chips)]
        for cp in started:
            cp.start()
        for j, chip in enumerate(chips):
            for t in range(n):
                copy(t, 1 + j, (*chip, c), me).wait_recv()
                fwd = copy(t, 4 + j, (*chip, c), sibling)
                fwd.start()
                started.append(fwd)
        for t in range(n):
            copy(t, 0, sibling, me).wait_recv()
            for j, chip in enumerate(chips):
                copy(t, 4 + j, (*chip, 1 - c), me).wait_recv()
        for cp in started:
            cp.wait_send()
        for cp in mine:
            cp.wait()

    return pl.pallas_call(
        body, out_shape=[jax.ShapeDtypeStruct((N_DEV,) + s.shape, s.dtype) for s in shards],
        in_specs=[any_spec] * n, out_specs=[any_spec] * n,
        scratch_shapes=[pltpu.SemaphoreType.DMA((N_PEERS * n,)), pltpu.SemaphoreType.DMA((N_PEERS * n,)),
                        pltpu.SemaphoreType.DMA((n,))],
        name=name)(*shards)


SIBLING = 1
OTHER_CHIPS = (2, 4, 6)
SAME_CORE = (0,) + OTHER_CHIPS


class Exchange:
    def __init__(self, inputs, out_shapes, aliases, copies, local=()):
        self.inputs, self.out_shapes, self.aliases = list(inputs), list(out_shapes), aliases
        self._copies, self._local = list(copies), list(local)
        self.scratch = [pltpu.SemaphoreType.DMA((len(self._copies),)), pltpu.SemaphoreType.DMA((len(self._copies),)),
                        pltpu.SemaphoreType.DMA((max(len(self._local), 1),))]

    def _build(self, ins, outs, sems):
        send_sems, recv_sems, local_sems = sems
        x, y, c = _me()
        me = 4 * x + 2 * y + c
        local = [functools.partial(pltpu.make_async_copy, src(ins, outs, me), dst(outs, me), local_sems.at[i])
                 for i, (src, dst) in enumerate(self._local)]
        sends, recvs = [], []
        for i, (mask, src, dst) in enumerate(self._copies):
            px, py, pc = x ^ ((mask >> 2) & 1), y ^ ((mask >> 1) & 1), c ^ (mask & 1)
            pair = dict(send_sem=send_sems.at[i], recv_sem=recv_sems.at[i], device_id_type=pl.DeviceIdType.MESH)
            sends.append(functools.partial(
                pltpu.make_async_remote_copy, src_ref=src(ins, outs, me), dst_ref=dst(outs, me), device_id=(px, py, pc), **pair))
            recvs.append(functools.partial(
                pltpu.make_async_remote_copy, src_ref=src(ins, outs, me), dst_ref=dst(outs, me ^ mask), device_id=(x, y, c), **pair))
        return local, sends, recvs

    def start(self, ins, outs, sems):
        local, sends, _ = self._build(ins, outs, sems)
        for make in local + sends:
            make().start()

    def drain(self, ins, outs, sems):
        local, sends, recvs = self._build(ins, outs, sems)
        for make in recvs:
            make().wait_recv()
        for make in sends:
            make().wait_send()
        for make in local:
            make().wait()


def _bind(fn, *args):
    return functools.partial(fn, *args)


def join_exchanges(a, b):
    if a is None or b is None:
        return a or b
    na_in, na_out = len(a.inputs), len(a.out_shapes)

    def src_a(fn):
        return lambda ins, outs, me: fn(ins[:na_in], outs[:na_out], me)

    def dst_a(fn):
        return lambda outs, who: fn(outs[:na_out], who)

    def src_b(fn):
        return lambda ins, outs, me: fn(ins[na_in:], outs[na_out:], me)

    def dst_b(fn):
        return lambda outs, who: fn(outs[na_out:], who)

    copies = [(m, src_a(s), dst_a(d)) for m, s, d in a._copies] + [(m, src_b(s), dst_b(d)) for m, s, d in b._copies]
    local = [(src_a(s), dst_a(d)) for s, d in a._local] + [(src_b(s), dst_b(d)) for s, d in b._local]
    aliases = dict(a.aliases)
    aliases.update({na_in + i: na_out + o for i, o in b.aliases.items()})
    return Exchange(a.inputs + b.inputs, a.out_shapes + b.out_shapes, aliases, copies, local)


def gather_over_ici(shards):
    copies = [(mask, _bind(lambda t, ins, outs, me: ins[t], t), _bind(lambda t, outs, sender: outs[t].at[sender], t))
              for t in range(len(shards)) for mask in OTHER_CHIPS]
    local = [(_bind(lambda t, ins, outs, me: ins[t], t), _bind(lambda t, outs, me: outs[t].at[me], t))
             for t in range(len(shards))]
    return Exchange(shards, [jax.ShapeDtypeStruct((N_DEV,) + s.shape, s.dtype) for s in shards], {}, copies, local)


def gather_over_d2d(gathered):
    copies = [(SIBLING, _bind(lambda t, m, ins, outs, me: outs[t].at[me ^ m], t, m),
               _bind(lambda t, m, outs, sender: outs[t].at[sender ^ m], t, m))
              for t in range(len(gathered)) for m in SAME_CORE]
    return Exchange(gathered, [jax.ShapeDtypeStruct(g.shape, g.dtype) for g in gathered],
                    {t: t for t in range(len(gathered))}, copies)


def scatter_over_d2d(blocks):
    copies = [(SIBLING, _bind(lambda t, m, ins, outs, me: ins[t].at[me ^ SIBLING ^ m], t, m),
               _bind(lambda t, i, outs, sender: outs[t].at[i], t, i))
              for t in range(len(blocks)) for i, m in enumerate(SAME_CORE)]
    return Exchange(blocks, [jax.ShapeDtypeStruct((len(SAME_CORE),) + b.shape[1:], b.dtype) for b in blocks], {}, copies)


def scatter_over_ici(pair_sums, bufs, layer):
    n = len(pair_sums)
    copies = [(m, _bind(lambda t, i, ins, outs, me: ins[t].at[i], t, i),
               _bind(lambda t, i, outs, sender: outs[t].at[layer, i], t, i))
              for t in range(n) for i, m in enumerate(SAME_CORE) if m]
    local = [(_bind(lambda t, ins, outs, me: ins[t].at[0], t), _bind(lambda t, outs, me: outs[t].at[layer, 0], t))
             for t in range(n)]
    return Exchange(list(pair_sums) + list(bufs), [jax.ShapeDtypeStruct(b.shape, b.dtype) for b in bufs],
                    {n + t: t for t in range(n)}, copies, local)


def run_exchange(ex, name):
    any_spec = pl.BlockSpec(memory_space=pl.ANY)
    n_in, n_out = len(ex.inputs), len(ex.out_shapes)

    def body(*refs):
        ins, outs, sems = refs[:n_in], refs[n_in:n_in + n_out], refs[n_in + n_out:]
        ex.start(ins, outs, sems)
        ex.drain(ins, outs, sems)

    return pl.pallas_call(
        body, out_shape=ex.out_shapes, in_specs=[any_spec] * n_in, out_specs=[any_spec] * n_out,
        input_output_aliases=ex.aliases, scratch_shapes=ex.scratch, name=name)(*ex.inputs)


MATRICES = ("w_in", "w_attn_out", "w_conv_out", "pool_w", "w_o", "w_ffn_in", "w_ffn_out")
TRANSPOSED = ("w_in", "w_ffn_in")
EVERY = tuple(range(len(MATRICES)))
IN_PROJ_PART, ATTN_PART, MIX_PART = (0,), (1, 2, 3, 4, 5), (6,)
LATE = (0,)
EARLY = EVERY[1:]
EARLY_FIRST, EARLY_SECOND = (4, 6), (1, 2, 3, 5)
SHARD_INFO = {
    "w_in": ((DEPTH, D_IN // N_DEV, D_MODEL), 1),
    "w_attn_out": ((DEPTH, D_ATTN, D_MODEL // N_DEV), 2),
    "w_conv_out": ((DEPTH, D_CONV, D_MODEL // N_DEV), 2),
    "pool_w": ((DEPTH, 4, 64, 256 // N_DEV), 3),
    "w_o": ((DEPTH, D_MODEL // N_DEV, D_MODEL), 1),
    "w_ffn_in": ((DEPTH, 2 * D_FF // N_DEV, D_MODEL), 1),
    "w_ffn_out": ((DEPTH, D_FF // N_DEV, D_MODEL), 1),
}


def _handled(name, t):
    return jnp.transpose(t, (0, 2, 1)) if name in TRANSPOSED else t
VECTORS = ("norm_mix_g", "forget_b", "q_norm_g", "k_norm_g", "pool_scale", "norm_ffn_g")
VECTOR_SHAPES = {"norm_mix_g": (DEPTH, D_MODEL), "forget_b": (DEPTH, HEADS), "q_norm_g": (DEPTH, HEAD_DIM),
                 "k_norm_g": (DEPTH, HEAD_DIM), "pool_scale": (DEPTH, D_MODEL), "norm_ffn_g": (DEPTH, D_MODEL)}
CONV_W_FULL = (DEPTH, 3, D_CONV)


def _size(shape):
    n = 1
    for v in shape:
        n *= v
    return n


def _pack(arrays, rows, cols):
    flat = jnp.concatenate([a.reshape(-1) for a in arrays])
    return jnp.pad(flat, (0, rows * cols - flat.shape[0])).reshape(rows, cols)


def _unpack(packed, shapes):
    flat, out, off = packed.reshape(-1), [], 0
    for shp in shapes:
        out.append(flat[off:off + _size(shp)].reshape(shp))
        off += _size(shp)
    return out


def _join_shards(stacked, axis):
    moved = jnp.moveaxis(stacked, 0, axis)
    shp = list(moved.shape)
    shp[axis:axis + 2] = [shp[axis] * shp[axis + 1]]
    return moved.reshape(shp)


def _cut_shards(full, axis):
    shp = list(full.shape)
    shp[axis:axis + 1] = [N_DEV, shp[axis] // N_DEV]
    return jnp.moveaxis(full.reshape(shp), axis, 0)


N_MOVED = 1544
SHARD_ROWS = D_IN // N_DEV


def _regroup_w_in(shards):
    wt = shards.reshape(D_IN, shards.shape[2])
    pad = jnp.zeros((N_FULL - D_IN, wt.shape[1]), wt.dtype)
    return jnp.concatenate([wt[N_MOVED:], wt[:N_MOVED], pad], axis=0)


def _ungroup_w_in(wpt):
    def kernel_rows(a, b):
        if b <= N_MOVED:
            return [wpt[a + D_IN - N_MOVED:b + D_IN - N_MOVED]]
        if a >= N_MOVED:
            return [wpt[a - N_MOVED:b - N_MOVED]]
        return kernel_rows(a, N_MOVED) + kernel_rows(N_MOVED, b)

    return jnp.stack([jnp.concatenate(kernel_rows(s * SHARD_ROWS, (s + 1) * SHARD_ROWS), axis=0) for s in range(N_DEV)])


def _pool_block_diag(w):
    out = jnp.zeros((D_POOL, D_MODEL), w.dtype)
    for g in range(4):
        out = lax.dynamic_update_slice(out, w[g], (g * 64, g * 256))
    return out


def _pool_from_block_diag(wbd):
    return jnp.stack([wbd[g * 64:(g + 1) * 64, g * 256:(g + 1) * 256] for g in range(4)])


def _layer_weights(mats, vec, conv_w, l):
    wp = _pool_block_diag(mats["pool_w"])
    row = lambda v: v.reshape(1, -1)
    fb = jnp.zeros((1, 128), F32).at[0, :HEADS].set(vec["forget_b"][l])
    cw = jnp.zeros((8, D_CONV), F32).at[:3].set(conv_w[l])
    twice = lambda v: jnp.tile(v.reshape(1, -1), (1, 2))
    return dict(
        wt_in=_regroup_w_in(mats["w_in"]), wt_ffn_in=mats["w_ffn_in"], w_ffn_out=mats["w_ffn_out"],
        wa=mats["w_attn_out"], wc=mats["w_conv_out"], wp=wp, wo=mats["w_o"],
        g_mix=row(vec["norm_mix_g"][l]), g_ffn=row(vec["norm_ffn_g"][l]), gq2=twice(vec["q_norm_g"][l]),
        gk2=twice(vec["k_norm_g"][l]), scale=row(vec["pool_scale"][l]), fb=fb, cw=cw)


def _layer_fwd(x, w, l, comm):
    (proj, h), half_a = norm_matmul(x, w["g_mix"], w["wt_in"], N_MAIN, f"in_proj_{l}", comm.gather_ici(l + 1, IN_PROJ_PART))
    qa, ka, va, vt, z = attn_prep(proj, h, w["wt_in"], w["fb"], w["gq2"], w["gk2"], f"attn_prep_{l}")
    (oa, lse), half_b = attn_forward(qa, ka, vt, f"attn_fwd_{l}", comm.gather_ici(l + 1, ATTN_PART))
    x1, half_c = mix_fwd(proj, oa, x, w["wa"], w["wc"], w["wp"], w["scale"], w["cw"], w["wo"], f"mix_fwd_{l}",
                         comm.gather_ici(l + 1, MIX_PART))
    half = list(half_a) + list(half_b) + list(half_c)
    (gu, h2), gathered = norm_matmul(x1, w["g_ffn"], w["wt_ffn_in"], 2 * D_FF, f"ffn_in_{l}", comm.gather_d2d(l + 1, half))
    x2 = swiglu_matmul(gu, w["w_ffn_out"], x1, f"ffn_out_{l}")
    saved = dict(x=x, proj=proj, h=h, z=z, qa=qa, ka=ka, va=va, oa=oa, lse=lse, x1=x1, gu=gu, h2=h2)
    return x2, saved, gathered


def _layer_bwd(dx2, sv, w, l, comm):
    g = {}
    (dgu, act), stage = swiglu_bwd(dx2, sv["gu"], w["w_ffn_out"], f"ffn_out_bwd_{l}", comm.scatter_d2d(l + 1))
    sums = comm.pair_sums(l + 1, stage)
    g["w_ffn_out"] = tn_matmul(act, dx2, f"dw_ffn_out_{l}")
    g["w_ffn_in"] = tn_matmul(dgu, sv["h2"], f"dw_ffn_in_{l}")
    (dx1, dg), _ = matmul_normbwd(dgu, w["wt_ffn_in"], sv["x1"], w["g_ffn"], dx2, f"ffn_in_bwd_{l}")
    g["norm_ffn_g"] = dg[0]

    (dproj, doa, a_tok, merged, dya, dyc, dyp, uc, dd, dscale, dcw) = mix_bwd(
        sv["proj"], sv["oa"], dx1, w["wa"], w["wc"], w["wp"], w["scale"], w["cw"], w["wo"], f"mix_bwd_{l}")
    g["w_o"] = tn_matmul(merged, dx1, f"dw_o_{l}")
    g["w_attn_out"], g["w_conv_out"], dwp = tn_matmuls([(a_tok, dya), (uc, dyc), (dd, dyp)], f"dw_branches_{l}")
    g["pool_w"] = _pool_from_block_diag(dwp)
    g["pool_scale"] = dscale[0]
    g["conv_w"] = dcw[:3]

    early = comm.early(l)
    comm.grads(l, g)
    above = comm.scatter_ici(l + 1, sums)
    (dqa, dka, dva), got = attn_backward(sv["qa"], sv["ka"], sv["va"], sv["oa"], doa, sv["lse"], f"attn_bwd_{l}",
                                         join_exchanges(above, comm.scatter_d2d(l, early) if early else None))
    n_above = len(above.out_shapes) if above else 0
    comm.scattered(got[:n_above])
    early_sums = dict(zip(early, comm.pair_sums(l, got[n_above:], early))) if early else {}
    early_ici = lambda which: comm.scatter_ici(l, [early_sums[t] for t in which], which) if early else None
    dproj, dgq, dgk, db = attn_post(dqa, dka, dva, sv["proj"], sv["z"], w["gq2"], w["gk2"], dproj, f"attn_post_{l}")
    g["q_norm_g"] = dgq[0, :HEAD_DIM] + dgq[0, HEAD_DIM:]
    g["k_norm_g"] = dgk[0, :HEAD_DIM] + dgk[0, HEAD_DIM:]
    g["forget_b"] = db[0, :HEADS]

    dw_in = tn_matmul(dproj, sv["h"], f"dw_in_{l}", m_cols=N_FULL, ex=early_ici(EARLY_FIRST))
    if early:
        dw_in, got = dw_in
        comm.scattered(got, EARLY_FIRST)
    g["w_in"] = _ungroup_w_in(dw_in)
    (dx, dg), got = matmul_normbwd(dproj, w["wt_in"], sv["x"], w["g_mix"], dx1, f"in_proj_bwd_{l}", k=N_FULL,
                                   ex=early_ici(EARLY_SECOND))
    comm.scattered(got, EARLY_SECOND if early else None)
    g["norm_mix_g"] = dg[0]
    comm.grads(l, g)
    return dx


def _local_step(x, tgt, comm):
    ws, saved = [], []
    w = comm.weights(0, None)
    for l in range(DEPTH):
        ws.append(w)
        x, sv, gathered = _layer_fwd(x, w, l, comm)
        saved.append(sv)
        if l + 1 < DEPTH:
            w = comm.weights(l + 1, gathered)
    sq, dx = loss_kernel(x, tgt, "loss")
    for l in reversed(range(DEPTH)):
        dx = _layer_bwd(dx, saved[l], ws[l], l, comm)
    comm.finish()
    return sq[0, 0], dx


def kernel(x, norm_mix_g, w_in, forget_b, q_norm_g, k_norm_g, w_attn_out, conv_w, w_conv_out, pool_w, pool_scale, w_o, norm_ffn_g, w_ffn_in, w_ffn_out, loss_target, m_norm_mix_g, m_w_in, m_forget_b, m_q_norm_g, m_k_norm_g, m_w_attn_out, m_conv_w, m_w_conv_out, m_pool_w, m_pool_scale, m_w_o, m_norm_ffn_g, m_w_ffn_in, m_w_ffn_out, v_norm_mix_g, v_w_in, v_forget_b, v_q_norm_g, v_k_norm_g, v_w_attn_out, v_conv_w, v_w_conv_out, v_pool_w, v_pool_scale, v_w_o, v_norm_ffn_g, v_w_ffn_in, v_w_ffn_out):
    w = dict(norm_mix_g=norm_mix_g, w_in=w_in, forget_b=forget_b, q_norm_g=q_norm_g, k_norm_g=k_norm_g,
             w_attn_out=w_attn_out, conv_w=conv_w, w_conv_out=w_conv_out, pool_w=pool_w, pool_scale=pool_scale,
             w_o=w_o, norm_ffn_g=norm_ffn_g, w_ffn_in=w_ffn_in, w_ffn_out=w_ffn_out)
    m = dict(norm_mix_g=m_norm_mix_g, w_in=m_w_in, forget_b=m_forget_b, q_norm_g=m_q_norm_g, k_norm_g=m_k_norm_g,
             w_attn_out=m_w_attn_out, conv_w=m_conv_w, w_conv_out=m_w_conv_out, pool_w=m_pool_w,
             pool_scale=m_pool_scale, w_o=m_w_o, norm_ffn_g=m_norm_ffn_g, w_ffn_in=m_w_ffn_in, w_ffn_out=m_w_ffn_out)
    v = dict(norm_mix_g=v_norm_mix_g, w_in=v_w_in, forget_b=v_forget_b, q_norm_g=v_q_norm_g, k_norm_g=v_k_norm_g,
             w_attn_out=v_w_attn_out, conv_w=v_conv_w, w_conv_out=v_w_conv_out, pool_w=v_pool_w,
             pool_scale=v_pool_scale, w_o=v_w_o, norm_ffn_g=v_norm_ffn_g, w_ffn_in=v_w_ffn_in, w_ffn_out=v_w_ffn_out)
    me = 4 * lax.axis_index("x") + 2 * lax.axis_index("y") + lax.axis_index("c")
    layer_shard = {n: SHARD_INFO[n][0][1:] for n in MATRICES}
    cut_axis = {n: SHARD_INFO[n][1] - 1 for n in MATRICES}

    vec = {n: w[n] for n in VECTORS}
    rc = {n: (_size(layer_shard[n][:-1]), layer_shard[n][-1]) for n in MATRICES}

    class Comm:
        bufs = [lax.empty((DEPTH, len(SAME_CORE)) + layer_shard[n], BF16) for n in MATRICES]
        blocks = [None] * DEPTH
        small_g = [None] * DEPTH
        conv_full = None

        @staticmethod
        def shards(l):
            return [_handled(n, w[n])[l].astype(BF16) for n in MATRICES]

        @staticmethod
        def gather_ici(l, part):
            return gather_over_ici([Comm.shards(l)[t] for t in part]) if l < DEPTH else None

        @staticmethod
        def gather_d2d(l, half):
            return gather_over_d2d(half) if l < DEPTH else None

        @staticmethod
        def weights(l, gathered):
            if l == 0:
                *gathered, conv_g = all_gather(Comm.shards(0) + [_pack([conv_w], 8, 128)], "gather_0")
                Comm.conv_full = _join_shards(jnp.stack([_unpack(conv_g[i], [conv_w.shape])[0] for i in range(N_DEV)]), 2)
            mats = {n: t if n == "w_in" else _join_shards(t, cut_axis[n]) for n, t in zip(MATRICES, gathered)}
            return _layer_weights(mats, vec, Comm.conv_full, l)

        @staticmethod
        def grads(l, g):
            Comm.small_g[l] = g
            Comm.blocks[l] = [None if n not in g else g[n] if n == "w_in" else _cut_shards(g[n], cut_axis[n])
                              for n in MATRICES]

        @staticmethod
        def early(l):
            return EARLY if l == 0 else None

        @staticmethod
        def scatter_d2d(l, which=EVERY):
            return scatter_over_d2d([Comm.blocks[l][t] for t in which]) if l < DEPTH else None

        @staticmethod
        def pair_sums(l, stage, which=EVERY):
            if l >= DEPTH:
                return None
            return [pair_sum(Comm.blocks[l][t].reshape((N_DEV,) + rc[MATRICES[t]]),
                             s.reshape((len(SAME_CORE),) + rc[MATRICES[t]]), me,
                             f"pair_sum_{MATRICES[t]}_{l}").reshape(s.shape) for t, s in zip(which, stage)]

        @staticmethod
        def scatter_ici(l, sums, which=EVERY):
            return scatter_over_ici(sums, [Comm.bufs[t] for t in which], l) if l < DEPTH else None

        @staticmethod
        def scattered(results, which=EVERY):
            for t, r in zip(which or (), results):
                Comm.bufs[t] = r

        @staticmethod
        def finish():
            stage = run_exchange(Comm.scatter_d2d(0, LATE), "scatter_d2d_0")
            Comm.scattered(run_exchange(Comm.scatter_ici(0, Comm.pair_sums(0, stage, LATE), LATE), "scatter_ici_0"), LATE)

    small_g, received = Comm.small_g, Comm
    sq, dx = _local_step(x[0], loss_target[0], Comm)

    big = {}
    for n, parts in zip(MATRICES, received.bufs):
        outs = adamw_sum(parts.reshape((DEPTH, len(SAME_CORE)) + rc[n]),
                         *[_handled(n, d[n]).reshape((DEPTH,) + rc[n]) for d in (w, m, v)], f"adamw_{n}")
        big[n] = [_handled(n, t.reshape((DEPTH,) + layer_shard[n])) for t in outs]

    small_shapes = [VECTOR_SHAPES[n] for n in VECTORS] + [CONV_W_FULL, (1,)]
    stacked = [jnp.stack([small_g[l][n] for l in range(DEPTH)]) for n in VECTORS + ("conv_w",)] + [sq.reshape(1)]
    sparts = all_gather([_pack(stacked, SMALL_ROWS, 128)], "gather_vector_grads")[0]
    col0 = me * (D_CONV // N_DEV)
    place = lambda t: lax.dynamic_update_slice(jnp.zeros(CONV_W_FULL, F32), t, (0, 0, col0))
    spacked = [_pack([d[n] for n in VECTORS] + [place(d["conv_w"]), jnp.zeros((1,), F32)], SMALL_ROWS, 128)[None]
               for d in (w, m, v)]
    small = [_unpack(t[0], small_shapes) for t in adamw_sum(sparts[None], *spacked, "adamw_vectors")]
    loss = (0.5 / D_MODEL) * small[0][-1][0]

    def result(kind):
        out = {n: big[n][kind] for n in MATRICES}
        out.update({n: small[kind][j] for j, n in enumerate(VECTORS)})
        out["conv_w"] = lax.dynamic_slice(small[kind][len(VECTORS)], (0, 0, col0), conv_w.shape)
        return [out[n] for n in w]

    return (loss, dx[None], *result(0), *result(1), *result(2), *result(3))
```

```python
import functools

import jax
import jax.numpy as jnp
from jax import lax
from jax.experimental import pallas as pl
from jax.experimental.pallas import tpu as pltpu

F32 = jnp.float32
BF16 = jnp.bfloat16

N_DEV = 8
DEPTH = 4
D_MODEL = 1024
HEAD_DIM = 64
HEADS = 8
D_ATTN = 512
D_CONV = 256
D_POOL = 256
D_FF = 2816
D_IN = 5640
EPS = 1e-6
ATTN_SCALE = HEAD_DIM ** -0.5

N_REST = 4096
N_MAIN = 5632
N_FULL = 5760
DPROJ_TAIL = 2048
DPROJ_COLS = N_REST + DPROJ_TAIL
FF_BLK = 256
N_FF_BLKS = D_FF // FF_BLK
HALO = 16

ADAM_LR = 0.001
ADAM_B1 = 0.9
ADAM_B2 = 0.999
ADAM_EPS = 1e-08
ADAM_WD = 0.01
ADAM_STEP = 10

SMALL_ROWS = 128

VMEM_LIMIT = 48 * 2 ** 20


def _cparams(sem, vmem=None):
    return pltpu.CompilerParams(dimension_semantics=sem, vmem_limit_bytes=vmem or VMEM_LIMIT)


def _pick(n, cands):
    for c in cands:
        if n % c == 0:
            return c
    raise ValueError(f"no tile for {n}")


def _tile(n, cap):
    t = min(cap, n)
    assert n % t == 0, (n, cap)
    return t


def _sigmoid(v):
    return 1.0 / (1.0 + jnp.exp(-v))


def _rstd(v):
    return lax.rsqrt(jnp.mean(v * v, axis=-1, keepdims=True) + EPS)


def _dot(a, b):
    return jnp.dot(a, b, preferred_element_type=F32)


def _dot_tn(a, b):
    return lax.dot_general(a, b, (((0,), (0,)), ((), ())), preferred_element_type=F32)


def _dot_nt(a, b):
    return lax.dot_general(a, b, (((1,), (1,)), ((), ())), preferred_element_type=F32)


def norm_matmul(x, g, wt, n_cols, name, ex=None):
    s, d = x.shape
    tm, tn = _tile(s, 1024), _pick(n_cols, (2816, 1408, 512))

    def body(x_ref, g_ref, w_ref, o_ref, h_ref):
        @pl.when(pl.program_id(1) == 0)
        def _():
            xv = x_ref[...]
            h_ref[...] = (xv * _rstd(xv) * g_ref[...]).astype(BF16)

        o_ref[...] = _dot_nt(h_ref[...], w_ref[...]).astype(BF16)

    return _carried_call(
        body, ex, (s // tm, n_cols // tn),
        [pl.BlockSpec((tm, d), lambda i, j: (i, 0)), pl.BlockSpec((1, d), lambda i, j: (0, 0)),
         pl.BlockSpec((tn, d), lambda i, j: (j, 0))],
        [pl.BlockSpec((tm, tn), lambda i, j: (i, j)), pl.BlockSpec((tm, d), lambda i, j: (i, 0))],
        [jax.ShapeDtypeStruct((s, n_cols), BF16), jax.ShapeDtypeStruct((s, d), BF16)], [],
        ("arbitrary", "arbitrary"), name, (x, g, wt))


def tn_matmul(a, b, name, m_cols=None, ex=None):
    t = a.shape[0]
    m = m_cols or a.shape[1]
    n = b.shape[1]
    tk = _tile(t, 1024)
    tmm = _pick(m, (1408, 1152, 1024, 512, 256))
    tn = _pick(n, (1408, 1152, 1024, 512, 128))
    nk = t // tk

    def body(a_ref, b_ref, o_ref, acc_ref):
        @pl.when(pl.program_id(2) == 0)
        def _():
            acc_ref[...] = jnp.zeros_like(acc_ref)

        acc_ref[...] += _dot_tn(a_ref[...].astype(BF16), b_ref[...].astype(BF16))

        @pl.when(pl.program_id(2) == nk - 1)
        def _():
            o_ref[...] = acc_ref[...].astype(BF16)

    if ex is None:
        return pl.pallas_call(
            body, grid=(m // tmm, n // tn, nk),
            in_specs=[pl.BlockSpec((tk, tmm), lambda i, j, k: (k, i)), pl.BlockSpec((tk, tn), lambda i, j, k: (k, j))],
            out_specs=pl.BlockSpec((tmm, tn), lambda i, j, k: (i, j)),
            out_shape=jax.ShapeDtypeStruct((m, n), BF16), scratch_shapes=[pltpu.VMEM((tmm, tn), F32)],
            compiler_params=_cparams(("parallel", "parallel", "arbitrary")), name=name)(a, b)
    (out,), carried = _carried_call(
        body, ex, (m // tmm, n // tn, nk),
        [pl.BlockSpec((tk, tmm), lambda i, j, k: (k, i)), pl.BlockSpec((tk, tn), lambda i, j, k: (k, j))],
        [pl.BlockSpec((tmm, tn), lambda i, j, k: (i, j))], [jax.ShapeDtypeStruct((m, n), BF16)],
        [pltpu.VMEM((tmm, tn), F32)], ("arbitrary", "arbitrary", "arbitrary"), name, (a, b))
    return out, carried


def tn_matmuls(pairs, name):
    t = pairs[0][0].shape[0]
    tk = _tile(t, 1024)
    nk = t // tk
    n = len(pairs)

    def body(*refs):
        ins, outs, accs = refs[:2 * n], refs[2 * n:3 * n], refs[3 * n:]

        @pl.when(pl.program_id(0) == 0)
        def _():
            for acc in accs:
                acc[...] = jnp.zeros_like(acc)

        for i in range(n):
            accs[i][...] += _dot_tn(ins[2 * i][...], ins[2 * i + 1][...])

        @pl.when(pl.program_id(0) == nk - 1)
        def _():
            for out, acc in zip(outs, accs):
                out[...] = acc[...].astype(BF16)

    shapes = [(a.shape[1], b.shape[1]) for a, b in pairs]
    return pl.pallas_call(
        body, grid=(nk,),
        in_specs=[pl.BlockSpec((tk, t_.shape[1]), lambda k: (k, 0)) for pair in pairs for t_ in pair],
        out_specs=[pl.BlockSpec(shp, lambda k: (0, 0)) for shp in shapes],
        out_shape=[jax.ShapeDtypeStruct(shp, BF16) for shp in shapes],
        scratch_shapes=[pltpu.VMEM(shp, F32) for shp in shapes],
        compiler_params=_cparams(("arbitrary",)), name=name)(*[t_ for pair in pairs for t_ in pair])


def matmul_normbwd(a, wt, x, g, dres, name, k=None, ex=None):
    s = a.shape[0]
    k = k or a.shape[1]
    d = wt.shape[1]
    tm = _tile(s, 1024)
    tk = _pick(k, (1408, 1152, 512))
    nk = k // tk

    def body(a_ref, w_ref, x_ref, g_ref, r_ref, dx_ref, dg_ref, acc_ref):
        i, kk = pl.program_id(0), pl.program_id(1)

        @pl.when(kk == 0)
        def _():
            acc_ref[...] = jnp.zeros_like(acc_ref)

        @pl.when((i == 0) & (kk == 0))
        def _():
            dg_ref[...] = jnp.zeros_like(dg_ref)

        acc_ref[...] += _dot(a_ref[...], w_ref[...])

        @pl.when(kk == nk - 1)
        def _():
            xv = x_ref[...]
            r = _rstd(xv)
            y = xv * r
            dh = acc_ref[...]
            dy = dh * g_ref[...]
            dx_ref[...] = r_ref[...] + r * (dy - y * jnp.mean(dy * y, axis=-1, keepdims=True))
            dg_ref[...] += jnp.sum(dh * y, axis=0, keepdims=True)

    return _carried_call(
        body, ex, (s // tm, nk),
        [pl.BlockSpec((tm, tk), lambda i, kk: (i, kk)), pl.BlockSpec((tk, d), lambda i, kk: (kk, 0)),
         pl.BlockSpec((tm, d), lambda i, kk: (i, 0)), pl.BlockSpec((1, d), lambda i, kk: (0, 0)),
         pl.BlockSpec((tm, d), lambda i, kk: (i, 0))],
        [pl.BlockSpec((tm, d), lambda i, kk: (i, 0)), pl.BlockSpec((1, d), lambda i, kk: (0, 0))],
        [jax.ShapeDtypeStruct((s, d), F32), jax.ShapeDtypeStruct((1, d), F32)],
        [pltpu.VMEM((tm, d), F32)], ("arbitrary", "arbitrary"), name, (a, wt, x, g, dres), vmem=56 * 2 ** 20)


def swiglu_matmul(gu, w, x1, name):
    s = gu.shape[0]
    d = w.shape[1]
    tm = _tile(s, 1024)

    def body(gu_ref, w_ref, x_ref, o_ref):
        acc = x_ref[...]
        for j in range(N_FF_BLKS):
            gt = gu_ref[:, j * FF_BLK:(j + 1) * FF_BLK].astype(F32)
            up = gu_ref[:, D_FF + j * FF_BLK:D_FF + (j + 1) * FF_BLK].astype(F32)
            act = (gt * _sigmoid(gt) * up).astype(BF16)
            acc += _dot(act, w_ref[j * FF_BLK:(j + 1) * FF_BLK, :])
        o_ref[...] = acc

    return pl.pallas_call(
        body, grid=(s // tm,),
        in_specs=[pl.BlockSpec((tm, 2 * D_FF), lambda i: (i, 0)),
                  pl.BlockSpec((D_FF, d), lambda i: (0, 0), pipeline_mode=pl.Buffered(1)),
                  pl.BlockSpec((tm, d), lambda i: (i, 0))],
        out_specs=pl.BlockSpec((tm, d), lambda i: (i, 0)),
        out_shape=jax.ShapeDtypeStruct((s, d), F32),
        compiler_params=_cparams(("parallel",), 58 * 2 ** 20), name=name)(gu, w, x1)


def swiglu_bwd(dx2, gu, w, name, ex=None):
    s, d = dx2.shape
    tm = _tile(s, 512)

    def body(dx_ref, gu_ref, w_ref, dgu_ref, act_ref):
        dx = dx_ref[...].astype(BF16)
        for j in range(N_FF_BLKS):
            g_cols = slice(j * FF_BLK, (j + 1) * FF_BLK)
            u_cols = slice(D_FF + j * FF_BLK, D_FF + (j + 1) * FF_BLK)
            dact = _dot_nt(dx, w_ref[j * FF_BLK:(j + 1) * FF_BLK, :])
            gt = gu_ref[:, g_cols].astype(F32)
            up = gu_ref[:, u_cols].astype(F32)
            sg = _sigmoid(gt)
            silu = gt * sg
            act_ref[:, j * FF_BLK:(j + 1) * FF_BLK] = (silu * up).astype(BF16)
            dgu_ref[:, g_cols] = (dact * up * (sg + silu * (1.0 - sg))).astype(BF16)
            dgu_ref[:, u_cols] = (dact * silu).astype(BF16)

    return _carried_call(
        body, ex, (s // tm,),
        [pl.BlockSpec((tm, d), lambda i: (i, 0)), pl.BlockSpec((tm, 2 * D_FF), lambda i: (i, 0)),
         pl.BlockSpec((D_FF, d), lambda i: (0, 0), pipeline_mode=pl.Buffered(1))],
        [pl.BlockSpec((tm, 2 * D_FF), lambda i: (i, 0)), pl.BlockSpec((tm, D_FF), lambda i: (i, 0))],
        [jax.ShapeDtypeStruct((s, 2 * D_FF), BF16), jax.ShapeDtypeStruct((s, D_FF), BF16)], [],
        ("arbitrary",), name, (dx2, gu, w), vmem=56 * 2 ** 20)


def loss_kernel(y, tgt, name):
    s, d = y.shape
    tm = _tile(s, 512)

    def body(y_ref, t_ref, l_ref, dy_ref):
        @pl.when(pl.program_id(0) == 0)
        def _():
            l_ref[...] = jnp.zeros_like(l_ref)

        err = y_ref[...] - t_ref[...]
        dy_ref[...] = err * (1.0 / d)
        l_ref[...] += jnp.sum(jnp.sum(err * err, axis=1, keepdims=True), axis=0, keepdims=True)

    return pl.pallas_call(
        body, grid=(s // tm,),
        in_specs=[pl.BlockSpec((tm, d), lambda i: (i, 0)), pl.BlockSpec((tm, d), lambda i: (i, 0))],
        out_specs=[pl.BlockSpec((8, 128), lambda i: (0, 0)), pl.BlockSpec((tm, d), lambda i: (i, 0))],
        out_shape=[jax.ShapeDtypeStruct((8, 128), F32), jax.ShapeDtypeStruct((s, d), F32)],
        compiler_params=_cparams(("arbitrary",)), name=name)(y, tgt)


def _split3(v):
    a1 = v.astype(BF16)
    r1 = v - a1.astype(F32)
    a2 = r1.astype(BF16)
    a3 = (r1 - a2.astype(F32)).astype(BF16)
    return a1, a2, a3


def _running_sum(v, carry_ref, reverse):
    tm = v.shape[0]
    row = lax.broadcasted_iota(jnp.int32, (tm, tm), 0)
    col = lax.broadcasted_iota(jnp.int32, (tm, tm), 1)
    tri = ((col >= row) if reverse else (row >= col)).astype(BF16)
    a1, a2, a3 = _split3(v)
    out = _dot(tri, a1) + _dot(tri, a2) + _dot(tri, a3) + carry_ref[...]
    carry_ref[...] = out[0:1, :] if reverse else out[tm - 1:tm, :]
    return out


HEAD_GROUP_FWD = 8
HEAD_GROUP_BWD = 8
LANE_C = 64
LANE_ONE = 67


def _lanes():
    lane = lax.broadcasted_iota(jnp.int32, (1, 128), 1)
    return lane, lane < HEAD_DIM


def _half_mean(t, lo):
    s_lo = jnp.sum(jnp.where(lo, t, 0.0), axis=-1, keepdims=True)
    s_hi = jnp.sum(jnp.where(lo, 0.0, t), axis=-1, keepdims=True)
    return jnp.where(lo, s_lo, s_hi) * (1.0 / HEAD_DIM)


def _lane_col(t, lane, idx):
    return jnp.sum(jnp.where(lane == idx, t, 0.0), axis=-1, keepdims=True)


def _swap_halves(t):
    return pltpu.roll(t, HEAD_DIM, 1)


def attn_prep(proj, h, wt_in, fb, gq2, gk2, name):
    s, d = h.shape
    tm = _tile(s, 512)
    first = N_REST // D_ATTN

    def body(q_ref, k_ref, v_ref, h_ref, wf_ref, fb_ref, gq_ref, gk_ref, qa_ref, ka_ref, va_ref, vt_ref, z_ref, carry_ref):
        lane, lo = _lanes()

        @pl.when(pl.program_id(0) == 0)
        def _():
            carry_ref[...] = jnp.zeros_like(carry_ref)

        z = _dot_nt(h_ref[...], wf_ref[...]) + fb_ref[...]
        z_ref[...] = z
        cv = _running_sum(jnp.minimum(z, 0.0) - jnp.log(1.0 + jnp.exp(-jnp.abs(z))), carry_ref, reverse=False)

        def normed(t, g):
            t = t.astype(F32)
            return t * lax.rsqrt(_half_mean(t * t, lo) + EPS) * g

        one_q = jnp.where((lane >= LANE_ONE) & (lane < LANE_ONE + 3), 1.0, 0.0)
        one_k = jnp.where((lane >= LANE_C) & (lane < LANE_C + 3), 1.0, 0.0)
        one_v = jnp.where(lane == LANE_C, 1.0, 0.0)
        for j in range(HEADS // 2):
            cols = slice(128 * j, 128 * (j + 1))
            qn = normed(q_ref[:, cols], gq_ref[...] * ATTN_SCALE)
            kn = normed(k_ref[:, cols], gk_ref[...])
            vv = v_ref[:, cols].astype(F32)
            for e in range(2):
                h = 2 * j + e
                pick = (lambda t: t) if e == 0 else _swap_halves
                pieces = [p.astype(F32) for p in _split3(_lane_col(cv, lane, h))]
                ext_q, ext_k = one_q, one_k
                for i, p in enumerate(pieces):
                    ext_q = jnp.where(lane == LANE_C + i, p, ext_q)
                    ext_k = jnp.where(lane == LANE_ONE + i, -p, ext_k)
                qa_ref[h] = jnp.where(lo, pick(qn), ext_q).astype(BF16)
                ka_ref[h] = jnp.where(lo, pick(kn), ext_k).astype(BF16)
                va = jnp.where(lo, pick(vv), one_v)
                va_ref[h] = va.astype(BF16)
                vt_ref[h] = va.T.astype(BF16)

    tile = lambda blk: pl.BlockSpec((tm, D_ATTN), lambda i: (i, blk))
    vec = pl.BlockSpec((1, 128), lambda i: (0, 0))
    out = pl.BlockSpec((HEADS, tm, 128), lambda i: (0, i, 0))
    return pl.pallas_call(
        body, grid=(s // tm,),
        in_specs=[tile(first), tile(first + 1), tile(first + 2), pl.BlockSpec((tm, d), lambda i: (i, 0)),
                  pl.BlockSpec((128, d), lambda i: (N_MAIN // 128, 0)), vec, vec, vec],
        out_specs=[out, out, out, pl.BlockSpec((HEADS, 128, tm), lambda i: (0, 0, i)),
                   pl.BlockSpec((tm, 128), lambda i: (i, 0))],
        out_shape=[jax.ShapeDtypeStruct((HEADS, s, 128), BF16)] * 3 + [jax.ShapeDtypeStruct((HEADS, 128, s), BF16),
                                                                       jax.ShapeDtypeStruct((s, 128), F32)],
        scratch_shapes=[pltpu.VMEM((1, 128), F32)],
        compiler_params=_cparams(("arbitrary",)), name=name)(proj, proj, proj, h, wt_in, fb, gq2, gk2)


def _carry(ex, n_in, n_out, n_scratch, grid):
    n_xin, n_xout = (len(ex.inputs), len(ex.out_shapes)) if ex else (0, 0)

    def split(refs):
        ins, xins = refs[:n_in], refs[n_in:n_in + n_xin]
        rest = refs[n_in + n_xin:]
        outs, xouts = rest[:n_out], rest[n_out:n_out + n_xout]
        rest = rest[n_out + n_xout:]
        return ins + outs + rest[:n_scratch], (xins, xouts, rest[n_scratch:])

    def first():
        return functools.reduce(lambda a, b: a & b, [pl.program_id(d) == 0 for d in range(len(grid))])

    def last():
        return functools.reduce(lambda a, b: a & b, [pl.program_id(d) == grid[d] - 1 for d in range(len(grid))])

    return split, first, last


def _carried_call(body, ex, grid, in_specs, out_specs, out_shape, scratch, sem, name, operands, vmem=None):
    any_spec = pl.BlockSpec(memory_space=pl.ANY)
    split, first, last = _carry(ex, len(in_specs), len(out_specs), len(scratch), grid)

    def carried(*refs):
        own, xrefs = split(refs)
        if ex:
            @pl.when(first())
            def _():
                ex.start(*xrefs)

        body(*own)
        if ex:
            @pl.when(last())
            def _():
                ex.drain(*xrefs)

    n_xin = len(ex.inputs) if ex else 0
    results = pl.pallas_call(
        carried, grid=grid, in_specs=list(in_specs) + [any_spec] * n_xin,
        out_specs=list(out_specs) + [any_spec] * (len(ex.out_shapes) if ex else 0),
        out_shape=list(out_shape) + (list(ex.out_shapes) if ex else []),
        input_output_aliases={len(in_specs) + i: len(out_specs) + o for i, o in ex.aliases.items()} if ex else {},
        scratch_shapes=list(scratch) + (ex.scratch if ex else []),
        compiler_params=_cparams(sem, vmem), name=name)(*operands, *(ex.inputs if ex else []))
    return results[:len(out_specs)], results[len(out_specs):]


def _tri_rows(t, n):
    qi = sum(jnp.where(t >= r * (r + 1) // 2, 1, 0) for r in range(1, n))
    return qi, t - qi * (qi + 1) // 2


def _tri_cols(t, n):
    ki = sum(jnp.where(t >= r * n - r * (r - 1) // 2, 1, 0) for r in range(1, n))
    return ki, ki + t - (ki * n - ki * (ki - 1) // 2)


def _causal_t(st_blk, tk, tq):
    key = lax.broadcasted_iota(jnp.int32, (tk, tq), 0)
    qry = lax.broadcasted_iota(jnp.int32, (tk, tq), 1)
    return jnp.where(qry >= key, st_blk, -jnp.inf)


def attn_forward(qa, ka, vt, name, ex=None):
    hh, s, _ = qa.shape
    tq = tk = _tile(s, 512)
    nq = s // tq
    grp = HEAD_GROUP_FWD

    def body(q_ref, k_ref, vt_ref, o_ref, lse_ref, m_ref, acc_ref):
        qi, ki = _tri_rows(pl.program_id(1), nq)

        @pl.when(ki == 0)
        def _():
            m_ref[...] = jnp.full_like(m_ref, -jnp.inf)
            acc_ref[...] = jnp.zeros_like(acc_ref)

        def step(masked):
            nxt = _dot_nt(k_ref[0], q_ref[0])
            for g in range(grp):
                st = nxt
                if g + 1 < grp:
                    nxt = _dot_nt(k_ref[g + 1], q_ref[g + 1])
                if masked:
                    st = _causal_t(st, tk, tq)
                m_old = m_ref[g]
                m_new = jnp.maximum(m_old, jnp.max(st, axis=0, keepdims=True))
                pt = jnp.exp(st - m_new).astype(BF16)
                acc_ref[g] = jnp.exp(m_old - m_new) * acc_ref[g] + _dot(vt_ref[g], pt)
                m_ref[g] = m_new

        @pl.when(ki < qi)
        def _():
            step(False)

        @pl.when(ki == qi)
        def _():
            step(True)
            for g in range(grp):
                acc = acc_ref[g]
                denom = acc[LANE_C:LANE_C + 1, :]
                o_ref[g] = (acc / denom).T.astype(BF16)
                lse_ref[g] = m_ref[g] + jnp.log(denom)

    qspec = pl.BlockSpec((grp, tq, 128), lambda h, t: (h, _tri_rows(t, nq)[0], 0))
    kspec = pl.BlockSpec((grp, tk, 128), lambda h, t: (h, _tri_rows(t, nq)[1], 0))
    vspec = pl.BlockSpec((grp, 128, tk), lambda h, t: (h, 0, _tri_rows(t, nq)[1]))
    lspec = pl.BlockSpec((grp, 1, tq), lambda h, t: (h, 0, _tri_rows(t, nq)[0]))
    return _carried_call(
        body, ex, (hh // grp, nq * (nq + 1) // 2), [qspec, kspec, vspec], [qspec, lspec],
        [jax.ShapeDtypeStruct((hh, s, 128), BF16), jax.ShapeDtypeStruct((hh, 1, s), F32)],
        [pltpu.VMEM((grp, 1, tq), F32), pltpu.VMEM((grp, 128, tq), F32)],
        ("arbitrary", "arbitrary"), name, (qa, ka, vt))


def attn_backward(qa, ka, va, oa, doa, lse, name, ex=None):
    hh, s, _ = qa.shape
    tq = tk = _tile(s, 512)
    nq = s // tq
    grp = HEAD_GROUP_BWD

    def body(q_ref, k_ref, v_ref, o_ref, do_ref, lse_ref, dq_ref, dk_ref, dv_ref, dka_ref, dva_ref):
        ki, qi = _tri_cols(pl.program_id(1), nq)

        @pl.when(pl.program_id(1) == 0)
        def _():
            dq_ref[...] = jnp.zeros_like(dq_ref)

        @pl.when(qi == ki)
        def _():
            dka_ref[...] = jnp.zeros_like(dka_ref)
            dva_ref[...] = jnp.zeros_like(dva_ref)

        def step(masked):
            rows = pl.ds(pl.multiple_of(qi * tq, tq), tq)
            products = lambda g: (_dot_nt(k_ref[g], q_ref[g]), _dot_nt(v_ref[g], do_ref[g]))
            nxt = products(0)
            for g in range(grp):
                st, dpt = nxt
                if g + 1 < grp:
                    nxt = products(g + 1)
                q, k, do = q_ref[g], k_ref[g], do_ref[g]
                if masked:
                    st = _causal_t(st, tk, tq)
                pt = jnp.exp(st - lse_ref[g])
                delta = jnp.sum((do.astype(F32) * o_ref[g].astype(F32)).T, axis=0, keepdims=True)
                dst = (pt * (dpt - delta)).astype(BF16)
                dva_ref[g] += _dot(pt.astype(BF16), do)
                dka_ref[g] += _dot(dst, q)
                dq_ref[g, rows, :] += _dot_tn(dst, k)

        @pl.when(qi > ki)
        def _():
            step(False)

        @pl.when(qi == ki)
        def _():
            step(True)

        @pl.when(qi == nq - 1)
        def _():
            dk_ref[...] = dka_ref[...]
            dv_ref[...] = dva_ref[...].astype(BF16)

    qspec = pl.BlockSpec((grp, tq, 128), lambda h, t: (h, _tri_cols(t, nq)[1], 0))
    lspec = pl.BlockSpec((grp, 1, tq), lambda h, t: (h, 0, _tri_cols(t, nq)[1]))
    kspec = pl.BlockSpec((grp, tk, 128), lambda h, t: (h, _tri_cols(t, nq)[0], 0))
    return _carried_call(
        body, ex, (hh // grp, nq * (nq + 1) // 2), [qspec, kspec, kspec, qspec, qspec, lspec],
        [pl.BlockSpec((grp, s, 128), lambda h, t: (h, 0, 0), pipeline_mode=pl.Buffered(1)), kspec, kspec],
        [jax.ShapeDtypeStruct((hh, s, 128), F32), jax.ShapeDtypeStruct((hh, s, 128), F32),
         jax.ShapeDtypeStruct((hh, s, 128), BF16)],
        [pltpu.VMEM((grp, tk, 128), F32), pltpu.VMEM((grp, tk, 128), F32)],
        ("arbitrary", "arbitrary"), name, (qa, ka, va, oa, doa, lse), vmem=58 * 2 ** 20)


def attn_post(dqa, dka, dva, proj, z, gq2, gk2, dproj, name):
    s = proj.shape[0]
    tm = _tile(s, 512)
    nt = s // tm

    def body(dq_ref, dk_ref, dv_ref, q_ref, k_ref, z_ref, gq_ref, gk_ref, dp_any, dp_ref, dgq_ref, dgk_ref, db_ref,
             carry_ref):
        lane, lo = _lanes()

        @pl.when(pl.program_id(0) == 0)
        def _():
            dgq_ref[...] = jnp.zeros_like(dgq_ref)
            dgk_ref[...] = jnp.zeros_like(dgk_ref)
            db_ref[...] = jnp.zeros_like(db_ref)
            carry_ref[...] = jnp.zeros_like(carry_ref)

        def pair(ref, j):
            return jnp.where(lo, ref[2 * j].astype(F32), _swap_halves(ref[2 * j + 1].astype(F32)))

        def norm_bwd(raw, g, dhat, scale):
            r = lax.rsqrt(_half_mean(raw * raw, lo) + EPS)
            y = raw * r
            dy = dhat * (g * scale)
            return r * (dy - y * _half_mean(dy * y, lo)), jnp.sum(dhat * y, axis=0, keepdims=True) * scale

        dc = jnp.zeros((tm, 128), F32)
        for j in range(HEADS // 2):
            cols = slice(128 * j, 128 * (j + 1))
            dq, dgq = norm_bwd(q_ref[:, cols].astype(F32), gq_ref[...], pair(dq_ref, j), ATTN_SCALE)
            dk, dgk = norm_bwd(k_ref[:, cols].astype(F32), gk_ref[...], pair(dk_ref, j), 1.0)
            dgq_ref[...] += dgq
            dgk_ref[...] += dgk
            dp_ref[:, cols] = dq.astype(BF16)
            dp_ref[:, D_ATTN + 128 * j:D_ATTN + 128 * (j + 1)] = dk.astype(BF16)
            dp_ref[:, 2 * D_ATTN + 128 * j:2 * D_ATTN + 128 * (j + 1)] = pair(dv_ref, j).astype(BF16)
            for e in range(2):
                h = 2 * j + e
                both = jnp.where(lane == LANE_C, dq_ref[h], 0.0) - jnp.where(lane == LANE_ONE, dk_ref[h], 0.0)
                dc = jnp.where(lane == h, jnp.sum(both, axis=-1, keepdims=True), dc)
        dz = _running_sum(dc, carry_ref, reverse=True) * (1.0 - _sigmoid(z_ref[...]))
        db_ref[...] += jnp.sum(dz, axis=0, keepdims=True)
        dp_ref[:, 3 * D_ATTN:3 * D_ATTN + 128] = dz.astype(BF16)
        dp_ref[:, 3 * D_ATTN + 128:] = jnp.zeros((tm, DPROJ_TAIL - 3 * D_ATTN - 128), BF16)

    heads = lambda: pl.BlockSpec((HEADS, tm, 128), lambda i: (0, nt - 1 - i, 0))
    vec = pl.BlockSpec((1, 128), lambda i: (0, 0))
    first = N_REST // D_ATTN
    return pl.pallas_call(
        body, grid=(nt,),
        in_specs=[heads(), heads(), heads(), pl.BlockSpec((tm, D_ATTN), lambda i: (nt - 1 - i, first)),
                  pl.BlockSpec((tm, D_ATTN), lambda i: (nt - 1 - i, first + 1)),
                  pl.BlockSpec((tm, 128), lambda i: (nt - 1 - i, 0)), vec, vec, pl.BlockSpec(memory_space=pl.ANY)],
        out_specs=[pl.BlockSpec((tm, DPROJ_TAIL), lambda i: (nt - 1 - i, N_REST // DPROJ_TAIL)), vec, vec, vec],
        out_shape=[jax.ShapeDtypeStruct(dproj.shape, BF16), jax.ShapeDtypeStruct((1, 128), F32),
                   jax.ShapeDtypeStruct((1, 128), F32), jax.ShapeDtypeStruct((1, 128), F32)],
        scratch_shapes=[pltpu.VMEM((1, 128), F32)], input_output_aliases={8: 0},
        compiler_params=_cparams(("arbitrary",)), name=name)(dqa, dka, dva, proj, proj, z, gq2, gk2, dproj)


def _pool_groups(tm):
    gid = lax.broadcasted_iota(jnp.int32, (1, D_POOL), 1) // (D_POOL // 4)
    win = jnp.where(gid == 0, 2.0, jnp.where(gid == 1, 4.0, jnp.where(gid == 2, 8.0, 16.0)))
    return gid, win


def _by_group(gid, v2, v4, v8, v16):
    return jnp.where(gid == 0, v2, jnp.where(gid == 1, v4, jnp.where(gid == 2, v8, v16)))


def _branches(rest_ref, halo_ref, a_ref, wa_ref, wc_ref, wp_ref, sc_ref, cw_ref, ti, tm):
    f = lambda v: v.astype(F32)
    cx, cb, cc, px = f(rest_ref[:, 0:256]), f(rest_ref[:, 256:512]), f(rest_ref[:, 512:768]), f(rest_ref[:, 768:1024])
    live = jnp.where(ti > 0, 1.0, 0.0)
    hz = f(halo_ref[:, 0:256]) * f(halo_ref[:, 512:768]) * live
    hp = f(halo_ref[:, 768:1024]) * live
    z = cc * cx
    zf = jnp.concatenate([hz, z], axis=0)
    z1 = pltpu.roll(zf, 1, 0)[HALO:]
    z2 = pltpu.roll(zf, 2, 0)[HALO:]
    cw = cw_ref[...]
    conv = cw[2:3] * z + cw[1:2] * z1 + cw[0:1] * z2
    uc = cb * conv
    pf = jnp.concatenate([hp, px], axis=0)
    s2 = pf + pltpu.roll(pf, 1, 0)
    s4 = s2 + pltpu.roll(s2, 2, 0)
    s8 = s4 + pltpu.roll(s4, 4, 0)
    s16 = s8 + pltpu.roll(s8, 8, 0)
    gid, win = _pool_groups(tm)
    t = (ti * tm + lax.broadcasted_iota(jnp.int32, (tm, 1), 0)).astype(F32)
    inv = 1.0 / jnp.minimum(t + 1.0, win)
    dpool = _by_group(gid, s2[HALO:], s4[HALO:], s8[HALO:], s16[HALO:]) * inv - px
    _, lo = _lanes()
    a_tok = [jnp.where(lo, f(a_ref[2 * j]), _swap_halves(f(a_ref[2 * j + 1]))).astype(BF16) for j in range(HEADS // 2)]
    y_attn = _dot(a_tok[0], wa_ref[0:128, :])
    for j in range(1, HEADS // 2):
        y_attn += _dot(a_tok[j], wa_ref[128 * j:128 * (j + 1), :])
    y_conv = _dot(uc.astype(BF16), wc_ref[...])
    y_pool_raw = _dot(dpool.astype(BF16), wp_ref[...])
    sg = [_sigmoid(f(rest_ref[:, 1024 + i * D_MODEL:1024 + (i + 1) * D_MODEL])) for i in range(3)]
    return dict(cx=cx, cb=cb, cc=cc, z=z, z1=z1, z2=z2, conv=conv, uc=uc, dpool=dpool, inv=inv, gid=gid, a_tok=a_tok,
                y_attn=y_attn, y_conv=y_conv, y_pool_raw=y_pool_raw, sg=sg, cw=cw)


def _mix_specs(tm, ti_of):
    blocks_per_tile = tm // HALO
    return [
        pl.BlockSpec((tm, N_REST), lambda i: (ti_of(i), 0)),
        pl.BlockSpec((HALO, 1024), lambda i: (jnp.maximum(ti_of(i) * blocks_per_tile - 1, 0), 0)),
        pl.BlockSpec((HEADS, tm, 128), lambda i: (0, ti_of(i), 0)),
        pl.BlockSpec((D_ATTN, D_MODEL), lambda i: (0, 0), pipeline_mode=pl.Buffered(1)),
        pl.BlockSpec((D_CONV, D_MODEL), lambda i: (0, 0), pipeline_mode=pl.Buffered(1)),
        pl.BlockSpec((D_POOL, D_MODEL), lambda i: (0, 0), pipeline_mode=pl.Buffered(1)),
        pl.BlockSpec((1, D_MODEL), lambda i: (0, 0)),
        pl.BlockSpec((8, D_CONV), lambda i: (0, 0)),
    ]


def mix_fwd(proj, a, x, wa, wc, wp, scale, cw, wo, name, ex=None):
    s = x.shape[0]
    tm = _tile(s, 512)

    def body(rest_ref, halo_ref, a_ref, wa_ref, wc_ref, wp_ref, sc_ref, cw_ref, wo_ref, x_ref, o_ref):
        b = _branches(rest_ref, halo_ref, a_ref, wa_ref, wc_ref, wp_ref, sc_ref, cw_ref, pl.program_id(0), tm)
        merged = b["sg"][0] * b["y_attn"] + b["sg"][1] * b["y_conv"] + b["sg"][2] * (b["y_pool_raw"] * sc_ref[...])
        o_ref[...] = x_ref[...] + _dot(merged.astype(BF16), wo_ref[...])

    (x1,), carried = _carried_call(
        body, ex, (s // tm,),
        _mix_specs(tm, lambda i: i) + [pl.BlockSpec((D_MODEL, D_MODEL), lambda i: (0, 0), pipeline_mode=pl.Buffered(1)),
                                       pl.BlockSpec((tm, D_MODEL), lambda i: (i, 0))],
        [pl.BlockSpec((tm, D_MODEL), lambda i: (i, 0))], [jax.ShapeDtypeStruct((s, D_MODEL), F32)], [],
        ("arbitrary",), name, (proj, proj, a, wa, wc, wp, scale, cw, wo, x), vmem=58 * 2 ** 20)
    return x1, carried


def mix_bwd(proj, a, dx1, wa, wc, wp, scale, cw, wo, name):
    s = dx1.shape[0]
    tm = _tile(s, 512)
    nt = s // tm
    ti_of = lambda i: nt - 1 - i
    n = tm + HALO

    def body(rest_ref, halo_ref, a_ref, wa_ref, wc_ref, wp_ref, sc_ref, cw_ref, wo_ref,
             dx_ref, dp_ref, da_ref, at_ref, mg_ref, dya_ref, dyc_ref, dyp_ref, uc_ref, dd_ref, dsc_ref, dcw_ref,
             cdc_ref, cde_ref):
        i = pl.program_id(0)
        ti = ti_of(i)

        @pl.when(i == 0)
        def _():
            cdc_ref[...] = jnp.zeros_like(cdc_ref)
            cde_ref[...] = jnp.zeros_like(cde_ref)
            dsc_ref[...] = jnp.zeros_like(dsc_ref)
            dcw_ref[...] = jnp.zeros_like(dcw_ref)

        b = _branches(rest_ref, halo_ref, a_ref, wa_ref, wc_ref, wp_ref, sc_ref, cw_ref, ti, tm)
        sg, sc = b["sg"], sc_ref[...]
        y_pool = b["y_pool_raw"] * sc
        merged = sg[0] * b["y_attn"] + sg[1] * b["y_conv"] + sg[2] * y_pool
        mg_ref[...] = merged.astype(BF16)
        dm = _dot_nt(dx_ref[...].astype(BF16), wo_ref[...])
        dys = [dm * sg[j] for j in range(3)]
        for j, y in enumerate((b["y_attn"], b["y_conv"], y_pool)):
            dp_ref[:, 1024 + j * D_MODEL:1024 + (j + 1) * D_MODEL] = (dys[j] * y * (1.0 - sg[j])).astype(BF16)
        dya = dys[0].astype(BF16)
        dya_ref[...] = dya
        _, lo = _lanes()
        for j in range(HEADS // 2):
            at_ref[:, 128 * j:128 * (j + 1)] = b["a_tok"][j]
            da = _dot_nt(dya, wa_ref[128 * j:128 * (j + 1), :])
            da_ref[2 * j] = jnp.where(lo, da, 0.0).astype(BF16)
            da_ref[2 * j + 1] = jnp.where(lo, _swap_halves(da), 0.0).astype(BF16)
        dyc = dys[1].astype(BF16)
        dyc_ref[...] = dyc
        duc = _dot_nt(dyc, wc_ref[...])
        dyp = dys[2]
        dsc_ref[...] += jnp.sum(dyp * b["y_pool_raw"], axis=0, keepdims=True)
        dypr = (dyp * sc).astype(BF16)
        dyp_ref[...] = dypr
        ddp = _dot_nt(dypr, wp_ref[...])
        uc_ref[...] = b["uc"].astype(BF16)
        dd_ref[...] = b["dpool"].astype(BF16)

        dconv = duc * b["cb"]
        dp_ref[:, 256:512] = (duc * b["conv"]).astype(BF16)
        dcf = jnp.concatenate([dconv, cdc_ref[...]], axis=0)
        cw = b["cw"]
        dz = cw[2:3] * dconv + cw[1:2] * pltpu.roll(dcf, n - 1, 0)[:tm] + cw[0:1] * pltpu.roll(dcf, n - 2, 0)[:tm]
        dp_ref[:, 0:256] = (dz * b["cc"]).astype(BF16)
        dp_ref[:, 512:768] = (dz * b["cx"]).astype(BF16)
        dcw_ref[0:1, :] += jnp.sum(dconv * b["z2"], axis=0, keepdims=True)
        dcw_ref[1:2, :] += jnp.sum(dconv * b["z1"], axis=0, keepdims=True)
        dcw_ref[2:3, :] += jnp.sum(dconv * b["z"], axis=0, keepdims=True)
        cdc_ref[...] = dconv[:HALO]

        e = ddp * b["inv"]
        ef = jnp.concatenate([e, cde_ref[...]], axis=0)
        r2 = ef + pltpu.roll(ef, n - 1, 0)
        r4 = r2 + pltpu.roll(r2, n - 2, 0)
        r8 = r4 + pltpu.roll(r4, n - 4, 0)
        r16 = r8 + pltpu.roll(r8, n - 8, 0)
        dp_ref[:, 768:1024] = (_by_group(b["gid"], r2[:tm], r4[:tm], r8[:tm], r16[:tm]) - ddp).astype(BF16)
        cde_ref[...] = e[:HALO]

    tile = lambda w: pl.BlockSpec((tm, w), lambda i: (ti_of(i), 0))
    whole = lambda r, c: pl.BlockSpec((r, c), lambda i: (0, 0))
    bf = lambda w: jax.ShapeDtypeStruct((s, w), BF16)
    return pl.pallas_call(
        body, grid=(nt,),
        in_specs=_mix_specs(tm, ti_of) + [pl.BlockSpec((D_MODEL, D_MODEL), lambda i: (0, 0), pipeline_mode=pl.Buffered(1)),
                                          tile(D_MODEL)],
        out_specs=[tile(N_REST), pl.BlockSpec((HEADS, tm, 128), lambda i: (0, ti_of(i), 0)), tile(D_ATTN),
                   tile(D_MODEL), tile(D_MODEL), tile(D_MODEL), tile(D_MODEL),
                   tile(D_CONV), tile(D_POOL), whole(1, D_MODEL), whole(8, D_CONV)],
        out_shape=[bf(DPROJ_COLS), jax.ShapeDtypeStruct((HEADS, s, 128), BF16), bf(D_ATTN),
                   bf(D_MODEL), bf(D_MODEL), bf(D_MODEL), bf(D_MODEL), bf(D_CONV), bf(D_POOL),
                   jax.ShapeDtypeStruct((1, D_MODEL), F32), jax.ShapeDtypeStruct((8, D_CONV), F32)],
        scratch_shapes=[pltpu.VMEM((HALO, D_CONV), F32), pltpu.VMEM((HALO, D_POOL), F32)],
        compiler_params=_cparams(("arbitrary",), 58 * 2 ** 20), name=name)(proj, proj, a, wa, wc, wp, scale, cw, wo, dx1)


def _adamw_math(w, g, m, v):
    m = ADAM_B1 * m + (1.0 - ADAM_B1) * g
    v = ADAM_B2 * v + (1.0 - ADAM_B2) * (g * g)
    m_hat = m / (1.0 - ADAM_B1 ** ADAM_STEP)
    v_hat = v / (1.0 - ADAM_B2 ** ADAM_STEP)
    delta = -ADAM_LR * (m_hat / (jnp.sqrt(v_hat) + ADAM_EPS) + ADAM_WD * w)
    return delta, m, v


ADAMW_PARTS_BLOCK_BYTES = 4 * 2 ** 20


def _row_tile(rows, cols, copies, itemsize):
    row_bytes = copies * (-(-cols // 128) * 128) * itemsize
    fits = [t for t in range(16, rows + 1, 16) if rows % t == 0 and t * row_bytes <= ADAMW_PARTS_BLOCK_BYTES]
    return max(fits) if fits else rows


def pair_sum(blocks, stage, me, name):
    n_slots, rows, cols = stage.shape
    tr = _row_tile(rows, cols, 1, 4)

    def body(me_ref, a_ref, b_ref, o_ref):
        o_ref[...] = (a_ref[...].astype(F32) + b_ref[...].astype(F32)).astype(BF16)

    slot = pl.BlockSpec((None, tr, cols), lambda i, r, me_ref: (i, r, 0))
    return pl.pallas_call(
        body, out_shape=jax.ShapeDtypeStruct(stage.shape, BF16),
        grid_spec=pltpu.PrefetchScalarGridSpec(
            num_scalar_prefetch=1, grid=(n_slots, rows // tr),
            in_specs=[pl.BlockSpec((None, tr, cols), lambda i, r, me_ref: (me_ref[0] ^ (2 * i), r, 0)), slot],
            out_specs=slot),
        compiler_params=_cparams(("parallel", "parallel")), name=name)(me.reshape(1), blocks, stage)


def adamw_sum(parts, w, m, v, name, ex=None):
    layers, rows, cols = w.shape
    n_parts = parts.shape[1]
    if rows % 16 == 0:
        tr, tc = _row_tile(rows, cols, n_parts, parts.dtype.itemsize), cols
    else:
        tr, tc = rows, _pick(cols, (256, 128))

    def body(p_ref, w_ref, m_ref, v_ref, g_ref, d_ref, nm_ref, nv_ref):
        g = p_ref[0].astype(F32)
        for i in range(1, n_parts):
            g = g + p_ref[i].astype(F32)
        g_ref[...] = g
        d_ref[...], nm_ref[...], nv_ref[...] = _adamw_math(w_ref[...], g, m_ref[...], v_ref[...])

    spec = pl.BlockSpec((None, tr, tc), lambda l, i, j: (l, i, j))
    return _carried_call(
        body, ex, (layers, rows // tr, cols // tc),
        [pl.BlockSpec((None, n_parts, tr, tc), lambda l, i, j: (l, 0, i, j)), spec, spec, spec],
        [spec] * 4, [jax.ShapeDtypeStruct((layers, rows, cols), F32)] * 4, [],
        ("arbitrary", "arbitrary", "arbitrary"), name, (parts, w, m, v))


def _me():
    return lax.axis_index("x"), lax.axis_index("y"), lax.axis_index("c")


N_PEERS = N_DEV - 1


def all_gather(shards, name):
    n = len(shards)
    any_spec = pl.BlockSpec(memory_space=pl.ANY)

    def body(*refs):
        x_refs, out_refs = refs[:n], refs[n:2 * n]
        send_sems, recv_sems, local_sems = refs[2 * n:]
        x, y, c = _me()
        me, sibling = (x, y, c), (x, y, 1 - c)
        chips = [(1 - x, y), (x, 1 - y), (1 - x, 1 - y)]

        def copy(t, k, block, to, from_input=False):
            slot = out_refs[t].at[4 * block[0] + 2 * block[1] + block[2]]
            return pltpu.make_async_remote_copy(
                src_ref=x_refs[t] if from_input else slot, dst_ref=slot, send_sem=send_sems.at[N_PEERS * t + k],
                recv_sem=recv_sems.at[N_PEERS * t + k], device_id=to, device_id_type=pl.DeviceIdType.MESH)

        mine = [pltpu.make_async_copy(x_refs[t], out_refs[t].at[4 * x + 2 * y + c], local_sems.at[t]) for t in range(n)]
        started = []
        for t in range(n):
            mine[t].start()
            started.append(copy(t, 0, me, sibling, from_input=True))
            started += [copy(t, 1 + j, me, (*chip, c), from_input=True) for j, chip in enumerate(chips)]
        for cp in started:
            cp.start()
        for j, chip in enumerate(chips):
            for t in range(n):
                copy(t, 1 + j, (*chip, c), me).wait_recv()
                fwd = copy(t, 4 + j, (*chip, c), sibling)
                fwd.start()
                started.append(fwd)
        for t in range(n):
            copy(t, 0, sibling, me).wait_recv()
            for j, chip in enumerate(chips):
                copy(t, 4 + j, (*chip, 1 - c), me).wait_recv()
        for cp in started:
            cp.wait_send()
        for cp in mine:
            cp.wait()

    return pl.pallas_call(
        body, out_shape=[jax.ShapeDtypeStruct((N_DEV,) + s.shape, s.dtype) for s in shards],
        in_specs=[any_spec] * n, out_specs=[any_spec] * n,
        scratch_shapes=[pltpu.SemaphoreType.DMA((N_PEERS * n,)), pltpu.SemaphoreType.DMA((N_PEERS * n,)),
                        pltpu.SemaphoreType.DMA((n,))],
        name=name)(*shards)


SIBLING = 1
OTHER_CHIPS = (2, 4, 6)
SAME_CORE = (0,) + OTHER_CHIPS


class Exchange:
    def __init__(self, inputs, out_shapes, aliases, copies, local=()):
        self.inputs, self.out_shapes, self.aliases = list(inputs), list(out_shapes), aliases
        self._copies, self._local = list(copies), list(local)
        self.scratch = [pltpu.SemaphoreType.DMA((len(self._copies),)), pltpu.SemaphoreType.DMA((len(self._copies),)),
                        pltpu.SemaphoreType.DMA((max(len(self._local), 1),))]

    def _build(self, ins, outs, sems):
        send_sems, recv_sems, local_sems = sems
        x, y, c = _me()
        me = 4 * x + 2 * y + c
        local = [functools.partial(pltpu.make_async_copy, src(ins, outs, me), dst(outs, me), local_sems.at[i])
                 for i, (src, dst) in enumerate(self._local)]
        sends, recvs = [], []
        for i, (mask, src, dst) in enumerate(self._copies):
            px, py, pc = x ^ ((mask >> 2) & 1), y ^ ((mask >> 1) & 1), c ^ (mask & 1)
            pair = dict(send_sem=send_sems.at[i], recv_sem=recv_sems.at[i], device_id_type=pl.DeviceIdType.MESH)
            sends.append(functools.partial(
                pltpu.make_async_remote_copy, src_ref=src(ins, outs, me), dst_ref=dst(outs, me), device_id=(px, py, pc), **pair))
            recvs.append(functools.partial(
                pltpu.make_async_remote_copy, src_ref=src(ins, outs, me), dst_ref=dst(outs, me ^ mask), device_id=(x, y, c), **pair))
        return local, sends, recvs

    def start(self, ins, outs, sems):
        local, sends, _ = self._build(ins, outs, sems)
        for make in local + sends:
            make().start()

    def drain(self, ins, outs, sems):
        local, sends, recvs = self._build(ins, outs, sems)
        for make in recvs:
            make().wait_recv()
        for make in sends:
            make().wait_send()
        for make in local:
            make().wait()


def _bind(fn, *args):
    return functools.partial(fn, *args)


def join_exchanges(a, b):
    if a is None or b is None:
        return a or b
    na_in, na_out = len(a.inputs), len(a.out_shapes)

    def src_a(fn):
        return lambda ins, outs, me: fn(ins[:na_in], outs[:na_out], me)

    def dst_a(fn):
        return lambda outs, who: fn(outs[:na_out], who)

    def src_b(fn):
        return lambda ins, outs, me: fn(ins[na_in:], outs[na_out:], me)

    def dst_b(fn):
        return lambda outs, who: fn(outs[na_out:], who)

    copies = [(m, src_a(s), dst_a(d)) for m, s, d in a._copies] + [(m, src_b(s), dst_b(d)) for m, s, d in b._copies]
    local = [(src_a(s), dst_a(d)) for s, d in a._local] + [(src_b(s), dst_b(d)) for s, d in b._local]
    aliases = dict(a.aliases)
    aliases.update({na_in + i: na_out + o for i, o in b.aliases.items()})
    return Exchange(a.inputs + b.inputs, a.out_shapes + b.out_shapes, aliases, copies, local)


def gather_over_ici(shards):
    copies = [(mask, _bind(lambda t, ins, outs, me: ins[t], t), _bind(lambda t, outs, sender: outs[t].at[sender], t))
              for t in range(len(shards)) for mask in OTHER_CHIPS]
    local = [(_bind(lambda t, ins, outs, me: ins[t], t), _bind(lambda t, outs, me: outs[t].at[me], t))
             for t in range(len(shards))]
    return Exchange(shards, [jax.ShapeDtypeStruct((N_DEV,) + s.shape, s.dtype) for s in shards], {}, copies, local)


def gather_over_d2d(gathered):
    copies = [(SIBLING, _bind(lambda t, m, ins, outs, me: outs[t].at[me ^ m], t, m),
               _bind(lambda t, m, outs, sender: outs[t].at[sender ^ m], t, m))
              for t in range(len(gathered)) for m in SAME_CORE]
    return Exchange(gathered, [jax.ShapeDtypeStruct(g.shape, g.dtype) for g in gathered],
                    {t: t for t in range(len(gathered))}, copies)


def scatter_over_d2d(blocks):
    copies = [(SIBLING, _bind(lambda t, m, ins, outs, me: ins[t].at[me ^ SIBLING ^ m], t, m),
               _bind(lambda t, i, outs, sender: outs[t].at[i], t, i))
              for t in range(len(blocks)) for i, m in enumerate(SAME_CORE)]
    return Exchange(blocks, [jax.ShapeDtypeStruct((len(SAME_CORE),) + b.shape[1:], b.dtype) for b in blocks], {}, copies)


def scatter_over_ici(pair_sums, bufs, layer):
    n = len(pair_sums)
    copies = [(m, _bind(lambda t, i, ins, outs, me: ins[t].at[i], t, i),
               _bind(lambda t, i, outs, sender: outs[t].at[layer, i], t, i))
              for t in range(n) for i, m in enumerate(SAME_CORE) if m]
    local = [(_bind(lambda t, ins, outs, me: ins[t].at[0], t), _bind(lambda t, outs, me: outs[t].at[layer, 0], t))
             for t in range(n)]
    return Exchange(list(pair_sums) + list(bufs), [jax.ShapeDtypeStruct(b.shape, b.dtype) for b in bufs],
                    {n + t: t for t in range(n)}, copies, local)


MATRICES = ("w_in", "w_attn_out", "w_conv_out", "pool_w", "w_o", "w_ffn_in", "w_ffn_out")
TRANSPOSED = ("w_in", "w_ffn_in")
EVERY = tuple(range(len(MATRICES)))
IN_PROJ_PART, ATTN_PART, MIX_PART = (0,), (1, 2, 3, 4, 5), (6,)
LATE = (0,)
EARLY = EVERY[1:]
EARLY_FIRST, EARLY_SECOND = (4, 6), (1, 2, 3, 5)
SHARD_INFO = {
    "w_in": ((DEPTH, D_IN // N_DEV, D_MODEL), 1),
    "w_attn_out": ((DEPTH, D_ATTN, D_MODEL // N_DEV), 2),
    "w_conv_out": ((DEPTH, D_CONV, D_MODEL // N_DEV), 2),
    "pool_w": ((DEPTH, 4, 64, 256 // N_DEV), 3),
    "w_o": ((DEPTH, D_MODEL // N_DEV, D_MODEL), 1),
    "w_ffn_in": ((DEPTH, 2 * D_FF // N_DEV, D_MODEL), 1),
    "w_ffn_out": ((DEPTH, D_FF // N_DEV, D_MODEL), 1),
}


def _handled(name, t):
    return jnp.transpose(t, (0, 2, 1)) if name in TRANSPOSED else t
VECTORS = ("norm_mix_g", "forget_b", "q_norm_g", "k_norm_g", "pool_scale", "norm_ffn_g")
VECTOR_SHAPES = {"norm_mix_g": (DEPTH, D_MODEL), "forget_b": (DEPTH, HEADS), "q_norm_g": (DEPTH, HEAD_DIM),
                 "k_norm_g": (DEPTH, HEAD_DIM), "pool_scale": (DEPTH, D_MODEL), "norm_ffn_g": (DEPTH, D_MODEL)}
CONV_W_FULL = (DEPTH, 3, D_CONV)


def _size(shape):
    n = 1
    for v in shape:
        n *= v
    return n


def _pack(arrays, rows, cols):
    flat = jnp.concatenate([a.reshape(-1) for a in arrays])
    return jnp.pad(flat, (0, rows * cols - flat.shape[0])).reshape(rows, cols)


def _unpack(packed, shapes):
    flat, out, off = packed.reshape(-1), [], 0
    for shp in shapes:
        out.append(flat[off:off + _size(shp)].reshape(shp))
        off += _size(shp)
    return out


def _join_shards(stacked, axis):
    moved = jnp.moveaxis(stacked, 0, axis)
    shp = list(moved.shape)
    shp[axis:axis + 2] = [shp[axis] * shp[axis + 1]]
    return moved.reshape(shp)


def _cut_shards(full, axis):
    shp = list(full.shape)
    shp[axis:axis + 1] = [N_DEV, shp[axis] // N_DEV]
    return jnp.moveaxis(full.reshape(shp), axis, 0)


N_MOVED = 1544
SHARD_ROWS = D_IN // N_DEV


def _regroup_w_in(shards):
    wt = shards.reshape(D_IN, shards.shape[2])
    pad = jnp.zeros((N_FULL - D_IN, wt.shape[1]), wt.dtype)
    return jnp.concatenate([wt[N_MOVED:], wt[:N_MOVED], pad], axis=0)


def _ungroup_w_in(wpt):
    def kernel_rows(a, b):
        if b <= N_MOVED:
            return [wpt[a + D_IN - N_MOVED:b + D_IN - N_MOVED]]
        if a >= N_MOVED:
            return [wpt[a - N_MOVED:b - N_MOVED]]
        return kernel_rows(a, N_MOVED) + kernel_rows(N_MOVED, b)

    return jnp.stack([jnp.concatenate(kernel_rows(s * SHARD_ROWS, (s + 1) * SHARD_ROWS), axis=0) for s in range(N_DEV)])


def _pool_block_diag(w):
    out = jnp.zeros((D_POOL, D_MODEL), w.dtype)
    for g in range(4):
        out = lax.dynamic_update_slice(out, w[g], (g * 64, g * 256))
    return out


def _pool_from_block_diag(wbd):
    return jnp.stack([wbd[g * 64:(g + 1) * 64, g * 256:(g + 1) * 256] for g in range(4)])


def _layer_weights(mats, vec, conv_w, l):
    wp = _pool_block_diag(mats["pool_w"])
    row = lambda v: v.reshape(1, -1)
    fb = jnp.zeros((1, 128), F32).at[0, :HEADS].set(vec["forget_b"][l])
    cw = jnp.zeros((8, D_CONV), F32).at[:3].set(conv_w[l])
    twice = lambda v: jnp.tile(v.reshape(1, -1), (1, 2))
    return dict(
        wt_in=_regroup_w_in(mats["w_in"]), wt_ffn_in=mats["w_ffn_in"], w_ffn_out=mats["w_ffn_out"],
        wa=mats["w_attn_out"], wc=mats["w_conv_out"], wp=wp, wo=mats["w_o"],
        g_mix=row(vec["norm_mix_g"][l]), g_ffn=row(vec["norm_ffn_g"][l]), gq2=twice(vec["q_norm_g"][l]),
        gk2=twice(vec["k_norm_g"][l]), scale=row(vec["pool_scale"][l]), fb=fb, cw=cw)


def _layer_fwd(x, w, l, comm):
    (proj, h), half_a = norm_matmul(x, w["g_mix"], w["wt_in"], N_MAIN, f"in_proj_{l}", comm.gather_ici(l + 1, IN_PROJ_PART))
    qa, ka, va, vt, z = attn_prep(proj, h, w["wt_in"], w["fb"], w["gq2"], w["gk2"], f"attn_prep_{l}")
    (oa, lse), half_b = attn_forward(qa, ka, vt, f"attn_fwd_{l}", comm.gather_ici(l + 1, ATTN_PART))
    x1, half_c = mix_fwd(proj, oa, x, w["wa"], w["wc"], w["wp"], w["scale"], w["cw"], w["wo"], f"mix_fwd_{l}",
                         comm.gather_ici(l + 1, MIX_PART))
    half = list(half_a) + list(half_b) + list(half_c)
    (gu, h2), gathered = norm_matmul(x1, w["g_ffn"], w["wt_ffn_in"], 2 * D_FF, f"ffn_in_{l}", comm.gather_d2d(l + 1, half))
    x2 = swiglu_matmul(gu, w["w_ffn_out"], x1, f"ffn_out_{l}")
    saved = dict(x=x, proj=proj, h=h, z=z, qa=qa, ka=ka, va=va, oa=oa, lse=lse, x1=x1, gu=gu, h2=h2)
    return x2, saved, gathered


def _layer_bwd(dx2, sv, w, l, comm):
    g = {}
    (dgu, act), stage = swiglu_bwd(dx2, sv["gu"], w["w_ffn_out"], f"ffn_out_bwd_{l}", comm.scatter_d2d(l + 1))
    sums = comm.pair_sums(l + 1, stage)
    g["w_ffn_out"] = tn_matmul(act, dx2, f"dw_ffn_out_{l}")
    g["w_ffn_in"] = tn_matmul(dgu, sv["h2"], f"dw_ffn_in_{l}")
    (dx1, dg), _ = matmul_normbwd(dgu, w["wt_ffn_in"], sv["x1"], w["g_ffn"], dx2, f"ffn_in_bwd_{l}")
    g["norm_ffn_g"] = dg[0]

    (dproj, doa, a_tok, merged, dya, dyc, dyp, uc, dd, dscale, dcw) = mix_bwd(
        sv["proj"], sv["oa"], dx1, w["wa"], w["wc"], w["wp"], w["scale"], w["cw"], w["wo"], f"mix_bwd_{l}")
    g["w_o"] = tn_matmul(merged, dx1, f"dw_o_{l}")
    g["w_attn_out"], g["w_conv_out"], dwp = tn_matmuls([(a_tok, dya), (uc, dyc), (dd, dyp)], f"dw_branches_{l}")
    g["pool_w"] = _pool_from_block_diag(dwp)
    g["pool_scale"] = dscale[0]
    g["conv_w"] = dcw[:3]

    early = comm.early(l)
    comm.grads(l, g)
    above = comm.scatter_ici(l + 1, sums)
    (dqa, dka, dva), got = attn_backward(sv["qa"], sv["ka"], sv["va"], sv["oa"], doa, sv["lse"], f"attn_bwd_{l}",
                                         join_exchanges(above, comm.scatter_d2d(l, early) if early else None))
    n_above = len(above.out_shapes) if above else 0
    comm.scattered(got[:n_above])
    early_sums = dict(zip(early, comm.pair_sums(l, got[n_above:], early))) if early else {}
    early_ici = lambda which: comm.scatter_ici(l, [early_sums[t] for t in which], which) if early else None
    dproj, dgq, dgk, db = attn_post(dqa, dka, dva, sv["proj"], sv["z"], w["gq2"], w["gk2"], dproj, f"attn_post_{l}")
    g["q_norm_g"] = dgq[0, :HEAD_DIM] + dgq[0, HEAD_DIM:]
    g["k_norm_g"] = dgk[0, :HEAD_DIM] + dgk[0, HEAD_DIM:]
    g["forget_b"] = db[0, :HEADS]

    dw_in = tn_matmul(dproj, sv["h"], f"dw_in_{l}", m_cols=N_FULL, ex=early_ici(EARLY_FIRST))
    if early:
        dw_in, got = dw_in
        comm.scattered(got, EARLY_FIRST)
    g["w_in"] = _ungroup_w_in(dw_in)
    comm.grads(l, g)
    (dx, dg), got = matmul_normbwd(
        dproj, w["wt_in"], sv["x"], w["g_mix"], dx1, f"in_proj_bwd_{l}", k=N_FULL,
        ex=join_exchanges(early_ici(EARLY_SECOND), comm.scatter_d2d(l, LATE) if early else None))
    if early:
        comm.scattered(got[:len(EARLY_SECOND)], EARLY_SECOND)
        comm.late_stage(got[len(EARLY_SECOND):])
    g["norm_mix_g"] = dg[0]
    comm.grads(l, g)
    return dx


def _local_step(x, tgt, comm):
    ws, saved = [], []
    w = comm.weights(0, None)
    for l in range(DEPTH):
        ws.append(w)
        x, sv, gathered = _layer_fwd(x, w, l, comm)
        saved.append(sv)
        if l + 1 < DEPTH:
            w = comm.weights(l + 1, gathered)
    sq, dx = loss_kernel(x, tgt, "loss")
    for l in reversed(range(DEPTH)):
        dx = _layer_bwd(dx, saved[l], ws[l], l, comm)
    return sq[0, 0], dx


def kernel(x, norm_mix_g, w_in, forget_b, q_norm_g, k_norm_g, w_attn_out, conv_w, w_conv_out, pool_w, pool_scale, w_o, norm_ffn_g, w_ffn_in, w_ffn_out, loss_target, m_norm_mix_g, m_w_in, m_forget_b, m_q_norm_g, m_k_norm_g, m_w_attn_out, m_conv_w, m_w_conv_out, m_pool_w, m_pool_scale, m_w_o, m_norm_ffn_g, m_w_ffn_in, m_w_ffn_out, v_norm_mix_g, v_w_in, v_forget_b, v_q_norm_g, v_k_norm_g, v_w_attn_out, v_conv_w, v_w_conv_out, v_pool_w, v_pool_scale, v_w_o, v_norm_ffn_g, v_w_ffn_in, v_w_ffn_out):
    w = dict(norm_mix_g=norm_mix_g, w_in=w_in, forget_b=forget_b, q_norm_g=q_norm_g, k_norm_g=k_norm_g,
             w_attn_out=w_attn_out, conv_w=conv_w, w_conv_out=w_conv_out, pool_w=pool_w, pool_scale=pool_scale,
             w_o=w_o, norm_ffn_g=norm_ffn_g, w_ffn_in=w_ffn_in, w_ffn_out=w_ffn_out)
    m = dict(norm_mix_g=m_norm_mix_g, w_in=m_w_in, forget_b=m_forget_b, q_norm_g=m_q_norm_g, k_norm_g=m_k_norm_g,
             w_attn_out=m_w_attn_out, conv_w=m_conv_w, w_conv_out=m_w_conv_out, pool_w=m_pool_w,
             pool_scale=m_pool_scale, w_o=m_w_o, norm_ffn_g=m_norm_ffn_g, w_ffn_in=m_w_ffn_in, w_ffn_out=m_w_ffn_out)
    v = dict(norm_mix_g=v_norm_mix_g, w_in=v_w_in, forget_b=v_forget_b, q_norm_g=v_q_norm_g, k_norm_g=v_k_norm_g,
             w_attn_out=v_w_attn_out, conv_w=v_conv_w, w_conv_out=v_w_conv_out, pool_w=v_pool_w,
             pool_scale=v_pool_scale, w_o=v_w_o, norm_ffn_g=v_norm_ffn_g, w_ffn_in=v_w_ffn_in, w_ffn_out=v_w_ffn_out)
    me = 4 * lax.axis_index("x") + 2 * lax.axis_index("y") + lax.axis_index("c")
    layer_shard = {n: SHARD_INFO[n][0][1:] for n in MATRICES}
    cut_axis = {n: SHARD_INFO[n][1] - 1 for n in MATRICES}

    vec = {n: w[n] for n in VECTORS}
    rc = {n: (_size(layer_shard[n][:-1]), layer_shard[n][-1]) for n in MATRICES}

    class Comm:
        bufs = [lax.empty((DEPTH, len(SAME_CORE)) + layer_shard[n], BF16) for n in MATRICES]
        blocks = [None] * DEPTH
        small_g = [None] * DEPTH
        conv_full = None

        @staticmethod
        def shards(l):
            return [_handled(n, w[n])[l].astype(BF16) for n in MATRICES]

        @staticmethod
        def gather_ici(l, part):
            return gather_over_ici([Comm.shards(l)[t] for t in part]) if l < DEPTH else None

        @staticmethod
        def gather_d2d(l, half):
            return gather_over_d2d(half) if l < DEPTH else None

        @staticmethod
        def weights(l, gathered):
            if l == 0:
                *gathered, conv_g = all_gather(Comm.shards(0) + [_pack([conv_w], 8, 128)], "gather_0")
                Comm.conv_full = _join_shards(jnp.stack([_unpack(conv_g[i], [conv_w.shape])[0] for i in range(N_DEV)]), 2)
            mats = {n: t if n == "w_in" else _join_shards(t, cut_axis[n]) for n, t in zip(MATRICES, gathered)}
            return _layer_weights(mats, vec, Comm.conv_full, l)

        @staticmethod
        def grads(l, g):
            Comm.small_g[l] = g
            Comm.blocks[l] = [None if n not in g else g[n] if n == "w_in" else _cut_shards(g[n], cut_axis[n])
                              for n in MATRICES]

        @staticmethod
        def early(l):
            return EARLY if l == 0 else None

        @staticmethod
        def scatter_d2d(l, which=EVERY):
            return scatter_over_d2d([Comm.blocks[l][t] for t in which]) if l < DEPTH else None

        @staticmethod
        def pair_sums(l, stage, which=EVERY):
            if l >= DEPTH:
                return None
            return [pair_sum(Comm.blocks[l][t].reshape((N_DEV,) + rc[MATRICES[t]]),
                             s.reshape((len(SAME_CORE),) + rc[MATRICES[t]]), me,
                             f"pair_sum_{MATRICES[t]}_{l}").reshape(s.shape) for t, s in zip(which, stage)]

        @staticmethod
        def scatter_ici(l, sums, which=EVERY):
            return scatter_over_ici(sums, [Comm.bufs[t] for t in which], l) if l < DEPTH else None

        @staticmethod
        def scattered(results, which=EVERY):
            for t, r in zip(which or (), results):
                Comm.bufs[t] = r

        @staticmethod
        def late_stage(stage):
            Comm.last_ici = Comm.scatter_ici(0, Comm.pair_sums(0, stage, LATE), LATE)

    small_g, received = Comm.small_g, Comm
    sq, dx = _local_step(x[0], loss_target[0], Comm)

    big = {}
    for n in ("w_ffn_in",) + tuple(t for t in MATRICES if t != "w_ffn_in"):
        parts = received.bufs[MATRICES.index(n)]
        outs, got = adamw_sum(parts.reshape((DEPTH, len(SAME_CORE)) + rc[n]),
                              *[_handled(n, d[n]).reshape((DEPTH,) + rc[n]) for d in (w, m, v)], f"adamw_{n}",
                              ex=Comm.last_ici if n == "w_ffn_in" else None)
        Comm.scattered(got, LATE)
        big[n] = [_handled(n, t.reshape((DEPTH,) + layer_shard[n])) for t in outs]

    small_shapes = [VECTOR_SHAPES[n] for n in VECTORS] + [CONV_W_FULL, (1,)]
    stacked = [jnp.stack([small_g[l][n] for l in range(DEPTH)]) for n in VECTORS + ("conv_w",)] + [sq.reshape(1)]
    sparts = all_gather([_pack(stacked, SMALL_ROWS, 128)], "gather_vector_grads")[0]
    col0 = me * (D_CONV // N_DEV)
    place = lambda t: lax.dynamic_update_slice(jnp.zeros(CONV_W_FULL, F32), t, (0, 0, col0))
    spacked = [_pack([d[n] for n in VECTORS] + [place(d["conv_w"]), jnp.zeros((1,), F32)], SMALL_ROWS, 128)[None]
               for d in (w, m, v)]
    small = [_unpack(t[0], small_shapes) for t in adamw_sum(sparts[None], *spacked, "adamw_vectors")[0]]
    loss = (0.5 / D_MODEL) * small[0][-1][0]

    def result(kind):
        out = {n: big[n][kind] for n in MATRICES}
        out.update({n: small[kind][j] for j, n in enumerate(VECTORS)})
        out["conv_w"] = lax.dynamic_slice(small[kind][len(VECTORS)], (0, 0, col0), conv_w.shape)
        return [out[n] for n in w]

    return (loss, dx[None], *result(0), *result(1), *result(2), *result(3))
```

```python
import functools

import jax
import jax.numpy as jnp
from jax import lax
from jax.experimental import pallas as pl
from jax.experimental.pallas import tpu as pltpu

F32 = jnp.float32
BF16 = jnp.bfloat16

N_DEV = 8
DEPTH = 4
D_MODEL = 1024
HEAD_DIM = 64
HEADS = 8
D_ATTN = 512
D_CONV = 256
D_POOL = 256
D_FF = 2816
D_IN = 5640
EPS = 1e-6
ATTN_SCALE = HEAD_DIM ** -0.5

N_REST = 4096
N_MAIN = 5632
N_FULL = 5760
DPROJ_TAIL = 2048
DPROJ_COLS = N_REST + DPROJ_TAIL
FF_BLK = 256
N_FF_BLKS = D_FF // FF_BLK
HALO = 16

ADAM_LR = 0.001
ADAM_B1 = 0.9
ADAM_B2 = 0.999
ADAM_EPS = 1e-08
ADAM_WD = 0.01
ADAM_STEP = 10

SMALL_ROWS = 128

VMEM_LIMIT = 48 * 2 ** 20


def _cparams(sem, vmem=None):
    return pltpu.CompilerParams(dimension_semantics=sem, vmem_limit_bytes=vmem or VMEM_LIMIT)


def _pick(n, cands):
    for c in cands:
        if n % c == 0:
            return c
    raise ValueError(f"no tile for {n}")


def _tile(n, cap):
    t = min(cap, n)
    assert n % t == 0, (n, cap)
    return t


def _sigmoid(v):
    return 1.0 / (1.0 + jnp.exp(-v))


def _rstd(v):
    return lax.rsqrt(jnp.mean(v * v, axis=-1, keepdims=True) + EPS)


def _dot(a, b):
    return jnp.dot(a, b, preferred_element_type=F32)


def _dot_tn(a, b):
    return lax.dot_general(a, b, (((0,), (0,)), ((), ())), preferred_element_type=F32)


def _dot_nt(a, b):
    return lax.dot_general(a, b, (((1,), (1,)), ((), ())), preferred_element_type=F32)


def norm_matmul(x, g, wt, n_cols, name, ex=None):
    s, d = x.shape
    tm, tn = _tile(s, 1024), _pick(n_cols, (2816, 1408, 512))

    def body(x_ref, g_ref, w_ref, o_ref, h_ref):
        @pl.when(pl.program_id(1) == 0)
        def _():
            xv = x_ref[...]
            h_ref[...] = (xv * _rstd(xv) * g_ref[...]).astype(BF16)

        o_ref[...] = _dot_nt(h_ref[...], w_ref[...]).astype(BF16)

    return _carried_call(
        body, ex, (s // tm, n_cols // tn),
        [pl.BlockSpec((tm, d), lambda i, j: (i, 0)), pl.BlockSpec((1, d), lambda i, j: (0, 0)),
         pl.BlockSpec((tn, d), lambda i, j: (j, 0))],
        [pl.BlockSpec((tm, tn), lambda i, j: (i, j)), pl.BlockSpec((tm, d), lambda i, j: (i, 0))],
        [jax.ShapeDtypeStruct((s, n_cols), BF16), jax.ShapeDtypeStruct((s, d), BF16)], [],
        ("arbitrary", "arbitrary"), name, (x, g, wt))


def tn_matmul(a, b, name, m_cols=None, ex=None):
    t = a.shape[0]
    m = m_cols or a.shape[1]
    n = b.shape[1]
    tk = _tile(t, 1024)
    tmm = _pick(m, (1408, 1152, 1024, 512, 256))
    tn = _pick(n, (1408, 1152, 1024, 512, 128))
    nk = t // tk

    def body(a_ref, b_ref, o_ref, acc_ref):
        @pl.when(pl.program_id(2) == 0)
        def _():
            acc_ref[...] = jnp.zeros_like(acc_ref)

        acc_ref[...] += _dot_tn(a_ref[...].astype(BF16), b_ref[...].astype(BF16))

        @pl.when(pl.program_id(2) == nk - 1)
        def _():
            o_ref[...] = acc_ref[...].astype(BF16)

    if ex is None:
        return pl.pallas_call(
            body, grid=(m // tmm, n // tn, nk),
            in_specs=[pl.BlockSpec((tk, tmm), lambda i, j, k: (k, i)), pl.BlockSpec((tk, tn), lambda i, j, k: (k, j))],
            out_specs=pl.BlockSpec((tmm, tn), lambda i, j, k: (i, j)),
            out_shape=jax.ShapeDtypeStruct((m, n), BF16), scratch_shapes=[pltpu.VMEM((tmm, tn), F32)],
            compiler_params=_cparams(("parallel", "parallel", "arbitrary")), name=name)(a, b)
    (out,), carried = _carried_call(
        body, ex, (m // tmm, n // tn, nk),
        [pl.BlockSpec((tk, tmm), lambda i, j, k: (k, i)), pl.BlockSpec((tk, tn), lambda i, j, k: (k, j))],
        [pl.BlockSpec((tmm, tn), lambda i, j, k: (i, j))], [jax.ShapeDtypeStruct((m, n), BF16)],
        [pltpu.VMEM((tmm, tn), F32)], ("arbitrary", "arbitrary", "arbitrary"), name, (a, b))
    return out, carried


def tn_matmuls(pairs, name):
    t = pairs[0][0].shape[0]
    tk = _tile(t, 1024)
    nk = t // tk
    n = len(pairs)

    def body(*refs):
        ins, outs, accs = refs[:2 * n], refs[2 * n:3 * n], refs[3 * n:]

        @pl.when(pl.program_id(0) == 0)
        def _():
            for acc in accs:
                acc[...] = jnp.zeros_like(acc)

        for i in range(n):
            accs[i][...] += _dot_tn(ins[2 * i][...], ins[2 * i + 1][...])

        @pl.when(pl.program_id(0) == nk - 1)
        def _():
            for out, acc in zip(outs, accs):
                out[...] = acc[...].astype(BF16)

    shapes = [(a.shape[1], b.shape[1]) for a, b in pairs]
    return pl.pallas_call(
        body, grid=(nk,),
        in_specs=[pl.BlockSpec((tk, t_.shape[1]), lambda k: (k, 0)) for pair in pairs for t_ in pair],
        out_specs=[pl.BlockSpec(shp, lambda k: (0, 0)) for shp in shapes],
        out_shape=[jax.ShapeDtypeStruct(shp, BF16) for shp in shapes],
        scratch_shapes=[pltpu.VMEM(shp, F32) for shp in shapes],
        compiler_params=_cparams(("arbitrary",)), name=name)(*[t_ for pair in pairs for t_ in pair])


def matmul_normbwd(a, wt, x, g, dres, name, k=None, ex=None):
    s = a.shape[0]
    k = k or a.shape[1]
    d = wt.shape[1]
    tm = _tile(s, 1024)
    tk = _pick(k, (1408, 1152, 512))
    nk = k // tk

    def body(a_ref, w_ref, x_ref, g_ref, r_ref, dx_ref, dg_ref, acc_ref):
        i, kk = pl.program_id(0), pl.program_id(1)

        @pl.when(kk == 0)
        def _():
            acc_ref[...] = jnp.zeros_like(acc_ref)

        @pl.when((i == 0) & (kk == 0))
        def _():
            dg_ref[...] = jnp.zeros_like(dg_ref)

        acc_ref[...] += _dot(a_ref[...], w_ref[...])

        @pl.when(kk == nk - 1)
        def _():
            xv = x_ref[...]
            r = _rstd(xv)
            y = xv * r
            dh = acc_ref[...]
            dy = dh * g_ref[...]
            dx_ref[...] = r_ref[...] + r * (dy - y * jnp.mean(dy * y, axis=-1, keepdims=True))
            dg_ref[...] += jnp.sum(dh * y, axis=0, keepdims=True)

    return _carried_call(
        body, ex, (s // tm, nk),
        [pl.BlockSpec((tm, tk), lambda i, kk: (i, kk)), pl.BlockSpec((tk, d), lambda i, kk: (kk, 0)),
         pl.BlockSpec((tm, d), lambda i, kk: (i, 0)), pl.BlockSpec((1, d), lambda i, kk: (0, 0)),
         pl.BlockSpec((tm, d), lambda i, kk: (i, 0))],
        [pl.BlockSpec((tm, d), lambda i, kk: (i, 0)), pl.BlockSpec((1, d), lambda i, kk: (0, 0))],
        [jax.ShapeDtypeStruct((s, d), F32), jax.ShapeDtypeStruct((1, d), F32)],
        [pltpu.VMEM((tm, d), F32)], ("arbitrary", "arbitrary"), name, (a, wt, x, g, dres), vmem=56 * 2 ** 20)


def swiglu_matmul(gu, w, x1, name):
    s = gu.shape[0]
    d = w.shape[1]
    tm = _tile(s, 512)

    def body(gu_ref, w_ref, x_ref, o_ref):
        acc = x_ref[...]
        for j in range(N_FF_BLKS):
            gt = gu_ref[:, j * FF_BLK:(j + 1) * FF_BLK].astype(F32)
            up = gu_ref[:, D_FF + j * FF_BLK:D_FF + (j + 1) * FF_BLK].astype(F32)
            act = (gt * _sigmoid(gt) * up).astype(BF16)
            acc += _dot(act, w_ref[j * FF_BLK:(j + 1) * FF_BLK, :])
        o_ref[...] = acc

    return pl.pallas_call(
        body, grid=(s // tm,),
        in_specs=[pl.BlockSpec((tm, 2 * D_FF), lambda i: (i, 0)), pl.BlockSpec((D_FF, d), lambda i: (0, 0)),
                  pl.BlockSpec((tm, d), lambda i: (i, 0))],
        out_specs=pl.BlockSpec((tm, d), lambda i: (i, 0)),
        out_shape=jax.ShapeDtypeStruct((s, d), F32),
        compiler_params=_cparams(("parallel",)), name=name)(gu, w, x1)


def swiglu_bwd(dx2, gu, w, name, ex=None):
    s, d = dx2.shape
    tm = _tile(s, 512)

    def body(dx_ref, gu_ref, w_ref, dgu_ref, act_ref):
        dx = dx_ref[...].astype(BF16)
        for j in range(N_FF_BLKS):
            g_cols = slice(j * FF_BLK, (j + 1) * FF_BLK)
            u_cols = slice(D_FF + j * FF_BLK, D_FF + (j + 1) * FF_BLK)
            dact = _dot_nt(dx, w_ref[j * FF_BLK:(j + 1) * FF_BLK, :])
            gt = gu_ref[:, g_cols].astype(F32)
            up = gu_ref[:, u_cols].astype(F32)
            sg = _sigmoid(gt)
            silu = gt * sg
            act_ref[:, j * FF_BLK:(j + 1) * FF_BLK] = (silu * up).astype(BF16)
            dgu_ref[:, g_cols] = (dact * up * (sg + silu * (1.0 - sg))).astype(BF16)
            dgu_ref[:, u_cols] = (dact * silu).astype(BF16)

    return _carried_call(
        body, ex, (s // tm,),
        [pl.BlockSpec((tm, d), lambda i: (i, 0)), pl.BlockSpec((tm, 2 * D_FF), lambda i: (i, 0)),
         pl.BlockSpec((D_FF, d), lambda i: (0, 0), pipeline_mode=pl.Buffered(1))],
        [pl.BlockSpec((tm, 2 * D_FF), lambda i: (i, 0)), pl.BlockSpec((tm, D_FF), lambda i: (i, 0))],
        [jax.ShapeDtypeStruct((s, 2 * D_FF), BF16), jax.ShapeDtypeStruct((s, D_FF), BF16)], [],
        ("arbitrary",), name, (dx2, gu, w), vmem=56 * 2 ** 20)


def loss_kernel(y, tgt, name):
    s, d = y.shape
    tm = _tile(s, 512)

    def body(y_ref, t_ref, l_ref, dy_ref):
        @pl.when(pl.program_id(0) == 0)
        def _():
            l_ref[...] = jnp.zeros_like(l_ref)

        err = y_ref[...] - t_ref[...]
        dy_ref[...] = err * (1.0 / d)
        l_ref[...] += jnp.sum(jnp.sum(err * err, axis=1, keepdims=True), axis=0, keepdims=True)

    return pl.pallas_call(
        body, grid=(s // tm,),
        in_specs=[pl.BlockSpec((tm, d), lambda i: (i, 0)), pl.BlockSpec((tm, d), lambda i: (i, 0))],
        out_specs=[pl.BlockSpec((8, 128), lambda i: (0, 0)), pl.BlockSpec((tm, d), lambda i: (i, 0))],
        out_shape=[jax.ShapeDtypeStruct((8, 128), F32), jax.ShapeDtypeStruct((s, d), F32)],
        compiler_params=_cparams(("arbitrary",)), name=name)(y, tgt)


def _split3(v):
    a1 = v.astype(BF16)
    r1 = v - a1.astype(F32)
    a2 = r1.astype(BF16)
    a3 = (r1 - a2.astype(F32)).astype(BF16)
    return a1, a2, a3


def _running_sum(v, carry_ref, reverse):
    tm = v.shape[0]
    row = lax.broadcasted_iota(jnp.int32, (tm, tm), 0)
    col = lax.broadcasted_iota(jnp.int32, (tm, tm), 1)
    tri = ((col >= row) if reverse else (row >= col)).astype(BF16)
    a1, a2, a3 = _split3(v)
    out = _dot(tri, a1) + _dot(tri, a2) + _dot(tri, a3) + carry_ref[...]
    carry_ref[...] = out[0:1, :] if reverse else out[tm - 1:tm, :]
    return out


HEAD_GROUP_FWD = 8
HEAD_GROUP_BWD = 8
LANE_C = 64
LANE_ONE = 67


def _lanes():
    lane = lax.broadcasted_iota(jnp.int32, (1, 128), 1)
    return lane, lane < HEAD_DIM


def _half_mean(t, lo):
    s_lo = jnp.sum(jnp.where(lo, t, 0.0), axis=-1, keepdims=True)
    s_hi = jnp.sum(jnp.where(lo, 0.0, t), axis=-1, keepdims=True)
    return jnp.where(lo, s_lo, s_hi) * (1.0 / HEAD_DIM)


def _lane_col(t, lane, idx):
    return jnp.sum(jnp.where(lane == idx, t, 0.0), axis=-1, keepdims=True)


def _swap_halves(t):
    return pltpu.roll(t, HEAD_DIM, 1)


def attn_prep(proj, h, wt_in, fb, gq2, gk2, name):
    s, d = h.shape
    tm = _tile(s, 512)
    first = N_REST // D_ATTN

    def body(q_ref, k_ref, v_ref, h_ref, wf_ref, fb_ref, gq_ref, gk_ref, qa_ref, ka_ref, va_ref, vt_ref, z_ref, carry_ref):
        lane, lo = _lanes()

        @pl.when(pl.program_id(0) == 0)
        def _():
            carry_ref[...] = jnp.zeros_like(carry_ref)

        z = _dot_nt(h_ref[...], wf_ref[...]) + fb_ref[...]
        z_ref[...] = z
        cv = _running_sum(jnp.minimum(z, 0.0) - jnp.log(1.0 + jnp.exp(-jnp.abs(z))), carry_ref, reverse=False)

        def normed(t, g):
            t = t.astype(F32)
            return t * lax.rsqrt(_half_mean(t * t, lo) + EPS) * g

        one_q = jnp.where((lane >= LANE_ONE) & (lane < LANE_ONE + 3), 1.0, 0.0)
        one_k = jnp.where((lane >= LANE_C) & (lane < LANE_C + 3), 1.0, 0.0)
        one_v = jnp.where(lane == LANE_C, 1.0, 0.0)
        for j in range(HEADS // 2):
            cols = slice(128 * j, 128 * (j + 1))
            qn = normed(q_ref[:, cols], gq_ref[...] * ATTN_SCALE)
            kn = normed(k_ref[:, cols], gk_ref[...])
            vv = v_ref[:, cols].astype(F32)
            for e in range(2):
                h = 2 * j + e
                pick = (lambda t: t) if e == 0 else _swap_halves
                pieces = [p.astype(F32) for p in _split3(_lane_col(cv, lane, h))]
                ext_q, ext_k = one_q, one_k
                for i, p in enumerate(pieces):
                    ext_q = jnp.where(lane == LANE_C + i, p, ext_q)
                    ext_k = jnp.where(lane == LANE_ONE + i, -p, ext_k)
                qa_ref[h] = jnp.where(lo, pick(qn), ext_q).astype(BF16)
                ka_ref[h] = jnp.where(lo, pick(kn), ext_k).astype(BF16)
                va = jnp.where(lo, pick(vv), one_v)
                va_ref[h] = va.astype(BF16)
                vt_ref[h] = va.T.astype(BF16)

    tile = lambda blk: pl.BlockSpec((tm, D_ATTN), lambda i: (i, blk))
    vec = pl.BlockSpec((1, 128), lambda i: (0, 0))
    out = pl.BlockSpec((HEADS, tm, 128), lambda i: (0, i, 0))
    return pl.pallas_call(
        body, grid=(s // tm,),
        in_specs=[tile(first), tile(first + 1), tile(first + 2), pl.BlockSpec((tm, d), lambda i: (i, 0)),
                  pl.BlockSpec((128, d), lambda i: (N_MAIN // 128, 0)), vec, vec, vec],
        out_specs=[out, out, out, pl.BlockSpec((HEADS, 128, tm), lambda i: (0, 0, i)),
                   pl.BlockSpec((tm, 128), lambda i: (i, 0))],
        out_shape=[jax.ShapeDtypeStruct((HEADS, s, 128), BF16)] * 3 + [jax.ShapeDtypeStruct((HEADS, 128, s), BF16),
                                                                       jax.ShapeDtypeStruct((s, 128), F32)],
        scratch_shapes=[pltpu.VMEM((1, 128), F32)],
        compiler_params=_cparams(("arbitrary",)), name=name)(proj, proj, proj, h, wt_in, fb, gq2, gk2)


def _carry(ex, n_in, n_out, n_scratch, grid):
    n_xin, n_xout = (len(ex.inputs), len(ex.out_shapes)) if ex else (0, 0)

    def split(refs):
        ins, xins = refs[:n_in], refs[n_in:n_in + n_xin]
        rest = refs[n_in + n_xin:]
        outs, xouts = rest[:n_out], rest[n_out:n_out + n_xout]
        rest = rest[n_out + n_xout:]
        return ins + outs + rest[:n_scratch], (xins, xouts, rest[n_scratch:])

    def first():
        return functools.reduce(lambda a, b: a & b, [pl.program_id(d) == 0 for d in range(len(grid))])

    def last():
        return functools.reduce(lambda a, b: a & b, [pl.program_id(d) == grid[d] - 1 for d in range(len(grid))])

    return split, first, last


def _carried_call(body, ex, grid, in_specs, out_specs, out_shape, scratch, sem, name, operands, vmem=None):
    any_spec = pl.BlockSpec(memory_space=pl.ANY)
    split, first, last = _carry(ex, len(in_specs), len(out_specs), len(scratch), grid)

    def carried(*refs):
        own, xrefs = split(refs)
        if ex:
            @pl.when(first())
            def _():
                ex.start(*xrefs)

        body(*own)
        if ex:
            @pl.when(last())
            def _():
                ex.drain(*xrefs)

    n_xin = len(ex.inputs) if ex else 0
    results = pl.pallas_call(
        carried, grid=grid, in_specs=list(in_specs) + [any_spec] * n_xin,
        out_specs=list(out_specs) + [any_spec] * (len(ex.out_shapes) if ex else 0),
        out_shape=list(out_shape) + (list(ex.out_shapes) if ex else []),
        input_output_aliases={len(in_specs) + i: len(out_specs) + o for i, o in ex.aliases.items()} if ex else {},
        scratch_shapes=list(scratch) + (ex.scratch if ex else []),
        compiler_params=_cparams(sem, vmem), name=name)(*operands, *(ex.inputs if ex else []))
    return results[:len(out_specs)], results[len(out_specs):]


def _tri_rows(t, n):
    qi = sum(jnp.where(t >= r * (r + 1) // 2, 1, 0) for r in range(1, n))
    return qi, t - qi * (qi + 1) // 2


def _tri_cols(t, n):
    ki = sum(jnp.where(t >= r * n - r * (r - 1) // 2, 1, 0) for r in range(1, n))
    return ki, ki + t - (ki * n - ki * (ki - 1) // 2)


def _causal_t(st_blk, tk, tq):
    key = lax.broadcasted_iota(jnp.int32, (tk, tq), 0)
    qry = lax.broadcasted_iota(jnp.int32, (tk, tq), 1)
    return jnp.where(qry >= key, st_blk, -jnp.inf)


def attn_forward(qa, ka, vt, name, ex=None):
    hh, s, _ = qa.shape
    tq = tk = _tile(s, 512)
    nq = s // tq
    grp = HEAD_GROUP_FWD

    def body(q_ref, k_ref, vt_ref, o_ref, lse_ref, m_ref, acc_ref):
        qi, ki = _tri_rows(pl.program_id(1), nq)

        @pl.when(ki == 0)
        def _():
            m_ref[...] = jnp.full_like(m_ref, -jnp.inf)
            acc_ref[...] = jnp.zeros_like(acc_ref)

        def step(masked):
            nxt = _dot_nt(k_ref[0], q_ref[0])
            for g in range(grp):
                st = nxt
                if g + 1 < grp:
                    nxt = _dot_nt(k_ref[g + 1], q_ref[g + 1])
                if masked:
                    st = _causal_t(st, tk, tq)
                m_old = m_ref[g]
                m_new = jnp.maximum(m_old, jnp.max(st, axis=0, keepdims=True))
                pt = jnp.exp(st - m_new).astype(BF16)
                acc_ref[g] = jnp.exp(m_old - m_new) * acc_ref[g] + _dot(vt_ref[g], pt)
                m_ref[g] = m_new

        @pl.when(ki < qi)
        def _():
            step(False)

        @pl.when(ki == qi)
        def _():
            step(True)
            for g in range(grp):
                acc = acc_ref[g]
                denom = acc[LANE_C:LANE_C + 1, :]
                o_ref[g] = (acc / denom).T.astype(BF16)
                lse_ref[g] = m_ref[g] + jnp.log(denom)

    qspec = pl.BlockSpec((grp, tq, 128), lambda h, t: (h, _tri_rows(t, nq)[0], 0))
    kspec = pl.BlockSpec((grp, tk, 128), lambda h, t: (h, _tri_rows(t, nq)[1], 0))
    vspec = pl.BlockSpec((grp, 128, tk), lambda h, t: (h, 0, _tri_rows(t, nq)[1]))
    lspec = pl.BlockSpec((grp, 1, tq), lambda h, t: (h, 0, _tri_rows(t, nq)[0]))
    return _carried_call(
        body, ex, (hh // grp, nq * (nq + 1) // 2), [qspec, kspec, vspec], [qspec, lspec],
        [jax.ShapeDtypeStruct((hh, s, 128), BF16), jax.ShapeDtypeStruct((hh, 1, s), F32)],
        [pltpu.VMEM((grp, 1, tq), F32), pltpu.VMEM((grp, 128, tq), F32)],
        ("arbitrary", "arbitrary"), name, (qa, ka, vt))


def attn_backward(qa, ka, va, oa, doa, lse, name, ex=None):
    hh, s, _ = qa.shape
    tq = tk = _tile(s, 512)
    nq = s // tq
    grp = HEAD_GROUP_BWD

    def body(q_ref, k_ref, v_ref, o_ref, do_ref, lse_ref, dq_ref, dk_ref, dv_ref, dka_ref, dva_ref):
        ki, qi = _tri_cols(pl.program_id(1), nq)

        @pl.when(pl.program_id(1) == 0)
        def _():
            dq_ref[...] = jnp.zeros_like(dq_ref)

        @pl.when(qi == ki)
        def _():
            dka_ref[...] = jnp.zeros_like(dka_ref)
            dva_ref[...] = jnp.zeros_like(dva_ref)

        def step(masked):
            rows = pl.ds(pl.multiple_of(qi * tq, tq), tq)
            products = lambda g: (_dot_nt(k_ref[g], q_ref[g]), _dot_nt(v_ref[g], do_ref[g]))
            nxt = products(0)
            for g in range(grp):
                st, dpt = nxt
                if g + 1 < grp:
                    nxt = products(g + 1)
                q, k, do = q_ref[g], k_ref[g], do_ref[g]
                if masked:
                    st = _causal_t(st, tk, tq)
                pt = jnp.exp(st - lse_ref[g])
                delta = jnp.sum((do.astype(F32) * o_ref[g].astype(F32)).T, axis=0, keepdims=True)
                dst = (pt * (dpt - delta)).astype(BF16)
                dva_ref[g] += _dot(pt.astype(BF16), do)
                dka_ref[g] += _dot(dst, q)
                dq_ref[g, rows, :] += _dot_tn(dst, k)

        @pl.when(qi > ki)
        def _():
            step(False)

        @pl.when(qi == ki)
        def _():
            step(True)

        @pl.when(qi == nq - 1)
        def _():
            dk_ref[...] = dka_ref[...]
            dv_ref[...] = dva_ref[...].astype(BF16)

    qspec = pl.BlockSpec((grp, tq, 128), lambda h, t: (h, _tri_cols(t, nq)[1], 0))
    lspec = pl.BlockSpec((grp, 1, tq), lambda h, t: (h, 0, _tri_cols(t, nq)[1]))
    kspec = pl.BlockSpec((grp, tk, 128), lambda h, t: (h, _tri_cols(t, nq)[0], 0))
    return _carried_call(
        body, ex, (hh // grp, nq * (nq + 1) // 2), [qspec, kspec, kspec, qspec, qspec, lspec],
        [pl.BlockSpec((grp, s, 128), lambda h, t: (h, 0, 0), pipeline_mode=pl.Buffered(1)), kspec, kspec],
        [jax.ShapeDtypeStruct((hh, s, 128), F32), jax.ShapeDtypeStruct((hh, s, 128), F32),
         jax.ShapeDtypeStruct((hh, s, 128), BF16)],
        [pltpu.VMEM((grp, tk, 128), F32), pltpu.VMEM((grp, tk, 128), F32)],
        ("arbitrary", "arbitrary"), name, (qa, ka, va, oa, doa, lse), vmem=58 * 2 ** 20)


def attn_post(dqa, dka, dva, proj, z, gq2, gk2, dproj, name):
    s = proj.shape[0]
    tm = _tile(s, 512)
    nt = s // tm

    def body(dq_ref, dk_ref, dv_ref, q_ref, k_ref, z_ref, gq_ref, gk_ref, dp_any, dp_ref, dgq_ref, dgk_ref, db_ref,
             carry_ref):
        lane, lo = _lanes()

        @pl.when(pl.program_id(0) == 0)
        def _():
            dgq_ref[...] = jnp.zeros_like(dgq_ref)
            dgk_ref[...] = jnp.zeros_like(dgk_ref)
            db_ref[...] = jnp.zeros_like(db_ref)
            carry_ref[...] = jnp.zeros_like(carry_ref)

        def pair(ref, j):
            return jnp.where(lo, ref[2 * j].astype(F32), _swap_halves(ref[2 * j + 1].astype(F32)))

        def norm_bwd(raw, g, dhat, scale):
            r = lax.rsqrt(_half_mean(raw * raw, lo) + EPS)
            y = raw * r
            dy = dhat * (g * scale)
            return r * (dy - y * _half_mean(dy * y, lo)), jnp.sum(dhat * y, axis=0, keepdims=True) * scale

        dc = jnp.zeros((tm, 128), F32)
        for j in range(HEADS // 2):
            cols = slice(128 * j, 128 * (j + 1))
            dq, dgq = norm_bwd(q_ref[:, cols].astype(F32), gq_ref[...], pair(dq_ref, j), ATTN_SCALE)
            dk, dgk = norm_bwd(k_ref[:, cols].astype(F32), gk_ref[...], pair(dk_ref, j), 1.0)
            dgq_ref[...] += dgq
            dgk_ref[...] += dgk
            dp_ref[:, cols] = dq.astype(BF16)
            dp_ref[:, D_ATTN + 128 * j:D_ATTN + 128 * (j + 1)] = dk.astype(BF16)
            dp_ref[:, 2 * D_ATTN + 128 * j:2 * D_ATTN + 128 * (j + 1)] = pair(dv_ref, j).astype(BF16)
            for e in range(2):
                h = 2 * j + e
                both = jnp.where(lane == LANE_C, dq_ref[h], 0.0) - jnp.where(lane == LANE_ONE, dk_ref[h], 0.0)
                dc = jnp.where(lane == h, jnp.sum(both, axis=-1, keepdims=True), dc)
        dz = _running_sum(dc, carry_ref, reverse=True) * (1.0 - _sigmoid(z_ref[...]))
        db_ref[...] += jnp.sum(dz, axis=0, keepdims=True)
        dp_ref[:, 3 * D_ATTN:3 * D_ATTN + 128] = dz.astype(BF16)
        dp_ref[:, 3 * D_ATTN + 128:] = jnp.zeros((tm, DPROJ_TAIL - 3 * D_ATTN - 128), BF16)

    heads = lambda: pl.BlockSpec((HEADS, tm, 128), lambda i: (0, nt - 1 - i, 0))
    vec = pl.BlockSpec((1, 128), lambda i: (0, 0))
    first = N_REST // D_ATTN
    return pl.pallas_call(
        body, grid=(nt,),
        in_specs=[heads(), heads(), heads(), pl.BlockSpec((tm, D_ATTN), lambda i: (nt - 1 - i, first)),
                  pl.BlockSpec((tm, D_ATTN), lambda i: (nt - 1 - i, first + 1)),
                  pl.BlockSpec((tm, 128), lambda i: (nt - 1 - i, 0)), vec, vec, pl.BlockSpec(memory_space=pl.ANY)],
        out_specs=[pl.BlockSpec((tm, DPROJ_TAIL), lambda i: (nt - 1 - i, N_REST // DPROJ_TAIL)), vec, vec, vec],
        out_shape=[jax.ShapeDtypeStruct(dproj.shape, BF16), jax.ShapeDtypeStruct((1, 128), F32),
                   jax.ShapeDtypeStruct((1, 128), F32), jax.ShapeDtypeStruct((1, 128), F32)],
        scratch_shapes=[pltpu.VMEM((1, 128), F32)], input_output_aliases={8: 0},
        compiler_params=_cparams(("arbitrary",)), name=name)(dqa, dka, dva, proj, proj, z, gq2, gk2, dproj)


def _pool_groups(tm):
    gid = lax.broadcasted_iota(jnp.int32, (1, D_POOL), 1) // (D_POOL // 4)
    win = jnp.where(gid == 0, 2.0, jnp.where(gid == 1, 4.0, jnp.where(gid == 2, 8.0, 16.0)))
    return gid, win


def _by_group(gid, v2, v4, v8, v16):
    return jnp.where(gid == 0, v2, jnp.where(gid == 1, v4, jnp.where(gid == 2, v8, v16)))


def _branches(rest_ref, halo_ref, a_ref, wa_ref, wc_ref, wp_ref, sc_ref, cw_ref, ti, tm):
    f = lambda v: v.astype(F32)
    cx, cb, cc, px = f(rest_ref[:, 0:256]), f(rest_ref[:, 256:512]), f(rest_ref[:, 512:768]), f(rest_ref[:, 768:1024])
    live = jnp.where(ti > 0, 1.0, 0.0)
    hz = f(halo_ref[:, 0:256]) * f(halo_ref[:, 512:768]) * live
    hp = f(halo_ref[:, 768:1024]) * live
    z = cc * cx
    zf = jnp.concatenate([hz, z], axis=0)
    z1 = pltpu.roll(zf, 1, 0)[HALO:]
    z2 = pltpu.roll(zf, 2, 0)[HALO:]
    cw = cw_ref[...]
    conv = cw[2:3] * z + cw[1:2] * z1 + cw[0:1] * z2
    uc = cb * conv
    pf = jnp.concatenate([hp, px], axis=0)
    s2 = pf + pltpu.roll(pf, 1, 0)
    s4 = s2 + pltpu.roll(s2, 2, 0)
    s8 = s4 + pltpu.roll(s4, 4, 0)
    s16 = s8 + pltpu.roll(s8, 8, 0)
    gid, win = _pool_groups(tm)
    t = (ti * tm + lax.broadcasted_iota(jnp.int32, (tm, 1), 0)).astype(F32)
    inv = 1.0 / jnp.minimum(t + 1.0, win)
    dpool = _by_group(gid, s2[HALO:], s4[HALO:], s8[HALO:], s16[HALO:]) * inv - px
    _, lo = _lanes()
    a_tok = [jnp.where(lo, f(a_ref[2 * j]), _swap_halves(f(a_ref[2 * j + 1]))).astype(BF16) for j in range(HEADS // 2)]
    y_attn = _dot(a_tok[0], wa_ref[0:128, :])
    for j in range(1, HEADS // 2):
        y_attn += _dot(a_tok[j], wa_ref[128 * j:128 * (j + 1), :])
    y_conv = _dot(uc.astype(BF16), wc_ref[...])
    y_pool_raw = _dot(dpool.astype(BF16), wp_ref[...])
    sg = [_sigmoid(f(rest_ref[:, 1024 + i * D_MODEL:1024 + (i + 1) * D_MODEL])) for i in range(3)]
    return dict(cx=cx, cb=cb, cc=cc, z=z, z1=z1, z2=z2, conv=conv, uc=uc, dpool=dpool, inv=inv, gid=gid, a_tok=a_tok,
                y_attn=y_attn, y_conv=y_conv, y_pool_raw=y_pool_raw, sg=sg, cw=cw)


def _mix_specs(tm, ti_of):
    blocks_per_tile = tm // HALO
    return [
        pl.BlockSpec((tm, N_REST), lambda i: (ti_of(i), 0)),
        pl.BlockSpec((HALO, 1024), lambda i: (jnp.maximum(ti_of(i) * blocks_per_tile - 1, 0), 0)),
        pl.BlockSpec((HEADS, tm, 128), lambda i: (0, ti_of(i), 0)),
        pl.BlockSpec((D_ATTN, D_MODEL), lambda i: (0, 0), pipeline_mode=pl.Buffered(1)),
        pl.BlockSpec((D_CONV, D_MODEL), lambda i: (0, 0), pipeline_mode=pl.Buffered(1)),
        pl.BlockSpec((D_POOL, D_MODEL), lambda i: (0, 0), pipeline_mode=pl.Buffered(1)),
        pl.BlockSpec((1, D_MODEL), lambda i: (0, 0)),
        pl.BlockSpec((8, D_CONV), lambda i: (0, 0)),
    ]


def mix_fwd(proj, a, x, wa, wc, wp, scale, cw, wo, name, ex=None):
    s = x.shape[0]
    tm = _tile(s, 512)

    def body(rest_ref, halo_ref, a_ref, wa_ref, wc_ref, wp_ref, sc_ref, cw_ref, wo_ref, x_ref, o_ref):
        b = _branches(rest_ref, halo_ref, a_ref, wa_ref, wc_ref, wp_ref, sc_ref, cw_ref, pl.program_id(0), tm)
        merged = b["sg"][0] * b["y_attn"] + b["sg"][1] * b["y_conv"] + b["sg"][2] * (b["y_pool_raw"] * sc_ref[...])
        o_ref[...] = x_ref[...] + _dot(merged.astype(BF16), wo_ref[...])

    (x1,), carried = _carried_call(
        body, ex, (s // tm,),
        _mix_specs(tm, lambda i: i) + [pl.BlockSpec((D_MODEL, D_MODEL), lambda i: (0, 0), pipeline_mode=pl.Buffered(1)),
                                       pl.BlockSpec((tm, D_MODEL), lambda i: (i, 0))],
        [pl.BlockSpec((tm, D_MODEL), lambda i: (i, 0))], [jax.ShapeDtypeStruct((s, D_MODEL), F32)], [],
        ("arbitrary",), name, (proj, proj, a, wa, wc, wp, scale, cw, wo, x), vmem=58 * 2 ** 20)
    return x1, carried


def mix_bwd(proj, a, dx1, wa, wc, wp, scale, cw, wo, name):
    s = dx1.shape[0]
    tm = _tile(s, 512)
    nt = s // tm
    ti_of = lambda i: nt - 1 - i
    n = tm + HALO

    def body(rest_ref, halo_ref, a_ref, wa_ref, wc_ref, wp_ref, sc_ref, cw_ref, wo_ref,
             dx_ref, dp_ref, da_ref, at_ref, mg_ref, dya_ref, dyc_ref, dyp_ref, uc_ref, dd_ref, dsc_ref, dcw_ref,
             cdc_ref, cde_ref):
        i = pl.program_id(0)
        ti = ti_of(i)

        @pl.when(i == 0)
        def _():
            cdc_ref[...] = jnp.zeros_like(cdc_ref)
            cde_ref[...] = jnp.zeros_like(cde_ref)
            dsc_ref[...] = jnp.zeros_like(dsc_ref)
            dcw_ref[...] = jnp.zeros_like(dcw_ref)

        b = _branches(rest_ref, halo_ref, a_ref, wa_ref, wc_ref, wp_ref, sc_ref, cw_ref, ti, tm)
        sg, sc = b["sg"], sc_ref[...]
        y_pool = b["y_pool_raw"] * sc
        merged = sg[0] * b["y_attn"] + sg[1] * b["y_conv"] + sg[2] * y_pool
        mg_ref[...] = merged.astype(BF16)
        dm = _dot_nt(dx_ref[...].astype(BF16), wo_ref[...])
        dys = [dm * sg[j] for j in range(3)]
        for j, y in enumerate((b["y_attn"], b["y_conv"], y_pool)):
            dp_ref[:, 1024 + j * D_MODEL:1024 + (j + 1) * D_MODEL] = (dys[j] * y * (1.0 - sg[j])).astype(BF16)
        dya = dys[0].astype(BF16)
        dya_ref[...] = dya
        _, lo = _lanes()
        for j in range(HEADS // 2):
            at_ref[:, 128 * j:128 * (j + 1)] = b["a_tok"][j]
            da = _dot_nt(dya, wa_ref[128 * j:128 * (j + 1), :])
            da_ref[2 * j] = jnp.where(lo, da, 0.0).astype(BF16)
            da_ref[2 * j + 1] = jnp.where(lo, _swap_halves(da), 0.0).astype(BF16)
        dyc = dys[1].astype(BF16)
        dyc_ref[...] = dyc
        duc = _dot_nt(dyc, wc_ref[...])
        dyp = dys[2]
        dsc_ref[...] += jnp.sum(dyp * b["y_pool_raw"], axis=0, keepdims=True)
        dypr = (dyp * sc).astype(BF16)
        dyp_ref[...] = dypr
        ddp = _dot_nt(dypr, wp_ref[...])
        uc_ref[...] = b["uc"].astype(BF16)
        dd_ref[...] = b["dpool"].astype(BF16)

        dconv = duc * b["cb"]
        dp_ref[:, 256:512] = (duc * b["conv"]).astype(BF16)
        dcf = jnp.concatenate([dconv, cdc_ref[...]], axis=0)
        cw = b["cw"]
        dz = cw[2:3] * dconv + cw[1:2] * pltpu.roll(dcf, n - 1, 0)[:tm] + cw[0:1] * pltpu.roll(dcf, n - 2, 0)[:tm]
        dp_ref[:, 0:256] = (dz * b["cc"]).astype(BF16)
        dp_ref[:, 512:768] = (dz * b["cx"]).astype(BF16)
        dcw_ref[0:1, :] += jnp.sum(dconv * b["z2"], axis=0, keepdims=True)
        dcw_ref[1:2, :] += jnp.sum(dconv * b["z1"], axis=0, keepdims=True)
        dcw_ref[2:3, :] += jnp.sum(dconv * b["z"], axis=0, keepdims=True)
        cdc_ref[...] = dconv[:HALO]

        e = ddp * b["inv"]
        ef = jnp.concatenate([e, cde_ref[...]], axis=0)
        r2 = ef + pltpu.roll(ef, n - 1, 0)
        r4 = r2 + pltpu.roll(r2, n - 2, 0)
        r8 = r4 + pltpu.roll(r4, n - 4, 0)
        r16 = r8 + pltpu.roll(r8, n - 8, 0)
        dp_ref[:, 768:1024] = (_by_group(b["gid"], r2[:tm], r4[:tm], r8[:tm], r16[:tm]) - ddp).astype(BF16)
        cde_ref[...] = e[:HALO]

    tile = lambda w: pl.BlockSpec((tm, w), lambda i: (ti_of(i), 0))
    whole = lambda r, c: pl.BlockSpec((r, c), lambda i: (0, 0))
    bf = lambda w: jax.ShapeDtypeStruct((s, w), BF16)
    return pl.pallas_call(
        body, grid=(nt,),
        in_specs=_mix_specs(tm, ti_of) + [pl.BlockSpec((D_MODEL, D_MODEL), lambda i: (0, 0), pipeline_mode=pl.Buffered(1)),
                                          tile(D_MODEL)],
        out_specs=[tile(N_REST), pl.BlockSpec((HEADS, tm, 128), lambda i: (0, ti_of(i), 0)), tile(D_ATTN),
                   tile(D_MODEL), tile(D_MODEL), tile(D_MODEL), tile(D_MODEL),
                   tile(D_CONV), tile(D_POOL), whole(1, D_MODEL), whole(8, D_CONV)],
        out_shape=[bf(DPROJ_COLS), jax.ShapeDtypeStruct((HEADS, s, 128), BF16), bf(D_ATTN),
                   bf(D_MODEL), bf(D_MODEL), bf(D_MODEL), bf(D_MODEL), bf(D_CONV), bf(D_POOL),
                   jax.ShapeDtypeStruct((1, D_MODEL), F32), jax.ShapeDtypeStruct((8, D_CONV), F32)],
        scratch_shapes=[pltpu.VMEM((HALO, D_CONV), F32), pltpu.VMEM((HALO, D_POOL), F32)],
        compiler_params=_cparams(("arbitrary",), 58 * 2 ** 20), name=name)(proj, proj, a, wa, wc, wp, scale, cw, wo, dx1)


def _adamw_math(w, g, m, v):
    m = ADAM_B1 * m + (1.0 - ADAM_B1) * g
    v = ADAM_B2 * v + (1.0 - ADAM_B2) * (g * g)
    m_hat = m / (1.0 - ADAM_B1 ** ADAM_STEP)
    v_hat = v / (1.0 - ADAM_B2 ** ADAM_STEP)
    delta = -ADAM_LR * (m_hat / (jnp.sqrt(v_hat) + ADAM_EPS) + ADAM_WD * w)
    return delta, m, v


ADAMW_PARTS_BLOCK_BYTES = 4 * 2 ** 20


def _row_tile(rows, cols, copies, itemsize):
    row_bytes = copies * (-(-cols // 128) * 128) * itemsize
    fits = [t for t in range(16, rows + 1, 16) if rows % t == 0 and t * row_bytes <= ADAMW_PARTS_BLOCK_BYTES]
    return max(fits) if fits else rows


def pair_sum(blocks, stage, me, name):
    n_slots, rows, cols = stage.shape
    tr = _row_tile(rows, cols, 1, 4)

    def body(me_ref, a_ref, b_ref, o_ref):
        o_ref[...] = (a_ref[...].astype(F32) + b_ref[...].astype(F32)).astype(BF16)

    slot = pl.BlockSpec((None, tr, cols), lambda i, r, me_ref: (i, r, 0))
    return pl.pallas_call(
        body, out_shape=jax.ShapeDtypeStruct(stage.shape, BF16),
        grid_spec=pltpu.PrefetchScalarGridSpec(
            num_scalar_prefetch=1, grid=(n_slots, rows // tr),
            in_specs=[pl.BlockSpec((None, tr, cols), lambda i, r, me_ref: (me_ref[0] ^ (2 * i), r, 0)), slot],
            out_specs=slot),
        compiler_params=_cparams(("parallel", "parallel")), name=name)(me.reshape(1), blocks, stage)


def adamw_sum(parts, w, m, v, name):
    layers, rows, cols = w.shape
    n_parts = parts.shape[1]
    if rows % 16 == 0:
        tr, tc = _row_tile(rows, cols, n_parts, parts.dtype.itemsize), cols
    else:
        tr, tc = rows, _pick(cols, (256, 128))

    def body(p_ref, w_ref, m_ref, v_ref, g_ref, d_ref, nm_ref, nv_ref):
        g = p_ref[0].astype(F32)
        for i in range(1, n_parts):
            g = g + p_ref[i].astype(F32)
        g_ref[...] = g
        d_ref[...], nm_ref[...], nv_ref[...] = _adamw_math(w_ref[...], g, m_ref[...], v_ref[...])

    spec = pl.BlockSpec((None, tr, tc), lambda l, i, j: (l, i, j))
    return pl.pallas_call(
        body, grid=(layers, rows // tr, cols // tc),
        in_specs=[pl.BlockSpec((None, n_parts, tr, tc), lambda l, i, j: (l, 0, i, j)), spec, spec, spec],
        out_specs=[spec] * 4, out_shape=[jax.ShapeDtypeStruct((layers, rows, cols), F32)] * 4,
        compiler_params=_cparams(("parallel", "parallel", "parallel")), name=name)(parts, w, m, v)


def _me():
    return lax.axis_index("x"), lax.axis_index("y"), lax.axis_index("c")


N_PEERS = N_DEV - 1


def all_gather(shards, name):
    n = len(shards)
    any_spec = pl.BlockSpec(memory_space=pl.ANY)

    def body(*refs):
        x_refs, out_refs = refs[:n], refs[n:2 * n]
        send_sems, recv_sems, local_sems = refs[2 * n:]
        x, y, c = _me()
        me, sibling = (x, y, c), (x, y, 1 - c)
        chips = [(1 - x, y), (x, 1 - y), (1 - x, 1 - y)]

        def copy(t, k, block, to, from_input=False):
            slot = out_refs[t].at[4 * block[0] + 2 * block[1] + block[2]]
            return pltpu.make_async_remote_copy(
                src_ref=x_refs[t] if from_input else slot, dst_ref=slot, send_sem=send_sems.at[N_PEERS * t + k],
                recv_sem=recv_sems.at[N_PEERS * t + k], device_id=to, device_id_type=pl.DeviceIdType.MESH)

        mine = [pltpu.make_async_copy(x_refs[t], out_refs[t].at[4 * x + 2 * y + c], local_sems.at[t]) for t in range(n)]
        started = []
        for t in range(n):
            mine[t].start()
            started.append(copy(t, 0, me, sibling, from_input=True))
            started += [copy(t, 1 + j, me, (*chip, c), from_input=True) for j, chip in enumerate(chips)]
        for cp in started:
            cp.start()
        for j, chip in enumerate(chips):
            for t in range(n):
                copy(t, 1 + j, (*chip, c), me).wait_recv()
                fwd = copy(t, 4 + j, (*chip, c), sibling)
                fwd.start()
                started.append(fwd)
        for t in range(n):
            copy(t, 0, sibling, me).wait_recv()
            for j, chip in enumerate(chips):
                copy(t, 4 + j, (*chip, 1 - c), me).wait_recv()
        for cp in started:
            cp.wait_send()
        for cp in mine:
            cp.wait()

    return pl.pallas_call(
        body, out_shape=[jax.ShapeDtypeStruct((N_DEV,) + s.shape, s.dtype) for s in shards],
        in_specs=[any_spec] * n, out_specs=[any_spec] * n,
        scratch_shapes=[pltpu.SemaphoreType.DMA((N_PEERS * n,)), pltpu.SemaphoreType.DMA((N_PEERS * n,)),
                        pltpu.SemaphoreType.DMA((n,))],
        name=name)(*shards)


SIBLING = 1
OTHER_CHIPS = (2, 4, 6)
SAME_CORE = (0,) + OTHER_CHIPS


class Exchange:
    def __init__(self, inputs, out_shapes, aliases, copies, local=()):
        self.inputs, self.out_shapes, self.aliases = list(inputs), list(out_shapes), aliases
        self._copies, self._local = list(copies), list(local)
        self.scratch = [pltpu.SemaphoreType.DMA((len(self._copies),)), pltpu.SemaphoreType.DMA((len(self._copies),)),
                        pltpu.SemaphoreType.DMA((max(len(self._local), 1),))]

    def _build(self, ins, outs, sems):
        send_sems, recv_sems, local_sems = sems
        x, y, c = _me()
        me = 4 * x + 2 * y + c
        local = [functools.partial(pltpu.make_async_copy, src(ins, outs, me), dst(outs, me), local_sems.at[i])
                 for i, (src, dst) in enumerate(self._local)]
        sends, recvs = [], []
        for i, (mask, src, dst) in enumerate(self._copies):
            px, py, pc = x ^ ((mask >> 2) & 1), y ^ ((mask >> 1) & 1), c ^ (mask & 1)
            pair = dict(send_sem=send_sems.at[i], recv_sem=recv_sems.at[i], device_id_type=pl.DeviceIdType.MESH)
            sends.append(functools.partial(
                pltpu.make_async_remote_copy, src_ref=src(ins, outs, me), dst_ref=dst(outs, me), device_id=(px, py, pc), **pair))
            recvs.append(functools.partial(
                pltpu.make_async_remote_copy, src_ref=src(ins, outs, me), dst_ref=dst(outs, me ^ mask), device_id=(x, y, c), **pair))
        return local, sends, recvs

    def start(self, ins, outs, sems):
        local, sends, _ = self._build(ins, outs, sems)
        for make in local + sends:
            make().start()

    def drain(self, ins, outs, sems):
        local, sends, recvs = self._build(ins, outs, sems)
        for make in recvs:
            make().wait_recv()
        for make in sends:
            make().wait_send()
        for make in local:
            make().wait()


def _bind(fn, *args):
    return functools.partial(fn, *args)


def join_exchanges(a, b):
    if a is None or b is None:
        return a or b
    na_in, na_out = len(a.inputs), len(a.out_shapes)

    def src_a(fn):
        return lambda ins, outs, me: fn(ins[:na_in], outs[:na_out], me)

    def dst_a(fn):
        return lambda outs, who: fn(outs[:na_out], who)

    def src_b(fn):
        return lambda ins, outs, me: fn(ins[na_in:], outs[na_out:], me)

    def dst_b(fn):
        return lambda outs, who: fn(outs[na_out:], who)

    copies = [(m, src_a(s), dst_a(d)) for m, s, d in a._copies] + [(m, src_b(s), dst_b(d)) for m, s, d in b._copies]
    local = [(src_a(s), dst_a(d)) for s, d in a._local] + [(src_b(s), dst_b(d)) for s, d in b._local]
    aliases = dict(a.aliases)
    aliases.update({na_in + i: na_out + o for i, o in b.aliases.items()})
    return Exchange(a.inputs + b.inputs, a.out_shapes + b.out_shapes, aliases, copies, local)


def gather_over_ici(shards):
    copies = [(mask, _bind(lambda t, ins, outs, me: ins[t], t), _bind(lambda t, outs, sender: outs[t].at[sender], t))
              for t in range(len(shards)) for mask in OTHER_CHIPS]
    local = [(_bind(lambda t, ins, outs, me: ins[t], t), _bind(lambda t, outs, me: outs[t].at[me], t))
             for t in range(len(shards))]
    return Exchange(shards, [jax.ShapeDtypeStruct((N_DEV,) + s.shape, s.dtype) for s in shards], {}, copies, local)


def gather_over_d2d(gathered):
    copies = [(SIBLING, _bind(lambda t, m, ins, outs, me: outs[t].at[me ^ m], t, m),
               _bind(lambda t, m, outs, sender: outs[t].at[sender ^ m], t, m))
              for t in range(len(gathered)) for m in SAME_CORE]
    return Exchange(gathered, [jax.ShapeDtypeStruct(g.shape, g.dtype) for g in gathered],
                    {t: t for t in range(len(gathered))}, copies)


def scatter_over_d2d(blocks):
    copies = [(SIBLING, _bind(lambda t, m, ins, outs, me: ins[t].at[me ^ SIBLING ^ m], t, m),
               _bind(lambda t, i, outs, sender: outs[t].at[i], t, i))
              for t in range(len(blocks)) for i, m in enumerate(SAME_CORE)]
    return Exchange(blocks, [jax.ShapeDtypeStruct((len(SAME_CORE),) + b.shape[1:], b.dtype) for b in blocks], {}, copies)


def scatter_over_ici(pair_sums, bufs, layer):
    n = len(pair_sums)
    copies = [(m, _bind(lambda t, i, ins, outs, me: ins[t].at[i], t, i),
               _bind(lambda t, i, outs, sender: outs[t].at[layer, i], t, i))
              for t in range(n) for i, m in enumerate(SAME_CORE) if m]
    local = [(_bind(lambda t, ins, outs, me: ins[t].at[0], t), _bind(lambda t, outs, me: outs[t].at[layer, 0], t))
             for t in range(n)]
    return Exchange(list(pair_sums) + list(bufs), [jax.ShapeDtypeStruct(b.shape, b.dtype) for b in bufs],
                    {n + t: t for t in range(n)}, copies, local)


def run_exchange(ex, name):
    any_spec = pl.BlockSpec(memory_space=pl.ANY)
    n_in, n_out = len(ex.inputs), len(ex.out_shapes)

    def body(*refs):
        ins, outs, sems = refs[:n_in], refs[n_in:n_in + n_out], refs[n_in + n_out:]
        ex.start(ins, outs, sems)
        ex.drain(ins, outs, sems)

    return pl.pallas_call(
        body, out_shape=ex.out_shapes, in_specs=[any_spec] * n_in, out_specs=[any_spec] * n_out,
        input_output_aliases=ex.aliases, scratch_shapes=ex.scratch, name=name)(*ex.inputs)


MATRICES = ("w_in", "w_attn_out", "w_conv_out", "pool_w", "w_o", "w_ffn_in", "w_ffn_out")
TRANSPOSED = ("w_in", "w_ffn_in")
EVERY = tuple(range(len(MATRICES)))
IN_PROJ_PART, ATTN_PART, MIX_PART = (0,), (1, 2, 3, 4, 5), (6,)
LATE = (0,)
EARLY = EVERY[1:]
EARLY_FIRST, EARLY_SECOND = (4, 6), (1, 2, 3, 5)
SHARD_INFO = {
    "w_in": ((DEPTH, D_IN // N_DEV, D_MODEL), 1),
    "w_attn_out": ((DEPTH, D_ATTN, D_MODEL // N_DEV), 2),
    "w_conv_out": ((DEPTH, D_CONV, D_MODEL // N_DEV), 2),
    "pool_w": ((DEPTH, 4, 64, 256 // N_DEV), 3),
    "w_o": ((DEPTH, D_MODEL // N_DEV, D_MODEL), 1),
    "w_ffn_in": ((DEPTH, 2 * D_FF // N_DEV, D_MODEL), 1),
    "w_ffn_out": ((DEPTH, D_FF // N_DEV, D_MODEL), 1),
}


def _handled(name, t):
    return jnp.transpose(t, (0, 2, 1)) if name in TRANSPOSED else t
VECTORS = ("norm_mix_g", "forget_b", "q_norm_g", "k_norm_g", "pool_scale", "norm_ffn_g")
VECTOR_SHAPES = {"norm_mix_g": (DEPTH, D_MODEL), "forget_b": (DEPTH, HEADS), "q_norm_g": (DEPTH, HEAD_DIM),
                 "k_norm_g": (DEPTH, HEAD_DIM), "pool_scale": (DEPTH, D_MODEL), "norm_ffn_g": (DEPTH, D_MODEL)}
CONV_W_FULL = (DEPTH, 3, D_CONV)


def _size(shape):
    n = 1
    for v in shape:
        n *= v
    return n


def _pack(arrays, rows, cols):
    flat = jnp.concatenate([a.reshape(-1) for a in arrays])
    return jnp.pad(flat, (0, rows * cols - flat.shape[0])).reshape(rows, cols)


def _unpack(packed, shapes):
    flat, out, off = packed.reshape(-1), [], 0
    for shp in shapes:
        out.append(flat[off:off + _size(shp)].reshape(shp))
        off += _size(shp)
    return out


def _join_shards(stacked, axis):
    moved = jnp.moveaxis(stacked, 0, axis)
    shp = list(moved.shape)
    shp[axis:axis + 2] = [shp[axis] * shp[axis + 1]]
    return moved.reshape(shp)


def _cut_shards(full, axis):
    shp = list(full.shape)
    shp[axis:axis + 1] = [N_DEV, shp[axis] // N_DEV]
    return jnp.moveaxis(full.reshape(shp), axis, 0)


N_MOVED = 1544
SHARD_ROWS = D_IN // N_DEV


def _regroup_w_in(shards):
    wt = shards.reshape(D_IN, shards.shape[2])
    pad = jnp.zeros((N_FULL - D_IN, wt.shape[1]), wt.dtype)
    return jnp.concatenate([wt[N_MOVED:], wt[:N_MOVED], pad], axis=0)


def _ungroup_w_in(wpt):
    def kernel_rows(a, b):
        if b <= N_MOVED:
            return [wpt[a + D_IN - N_MOVED:b + D_IN - N_MOVED]]
        if a >= N_MOVED:
            return [wpt[a - N_MOVED:b - N_MOVED]]
        return kernel_rows(a, N_MOVED) + kernel_rows(N_MOVED, b)

    return jnp.stack([jnp.concatenate(kernel_rows(s * SHARD_ROWS, (s + 1) * SHARD_ROWS), axis=0) for s in range(N_DEV)])


def _pool_block_diag(w):
    out = jnp.zeros((D_POOL, D_MODEL), w.dtype)
    for g in range(4):
        out = lax.dynamic_update_slice(out, w[g], (g * 64, g * 256))
    return out


def _pool_from_block_diag(wbd):
    return jnp.stack([wbd[g * 64:(g + 1) * 64, g * 256:(g + 1) * 256] for g in range(4)])


def _layer_weights(mats, vec, conv_w, l):
    wp = _pool_block_diag(mats["pool_w"])
    row = lambda v: v.reshape(1, -1)
    fb = jnp.zeros((1, 128), F32).at[0, :HEADS].set(vec["forget_b"][l])
    cw = jnp.zeros((8, D_CONV), F32).at[:3].set(conv_w[l])
    twice = lambda v: jnp.tile(v.reshape(1, -1), (1, 2))
    return dict(
        wt_in=_regroup_w_in(mats["w_in"]), wt_ffn_in=mats["w_ffn_in"], w_ffn_out=mats["w_ffn_out"],
        wa=mats["w_attn_out"], wc=mats["w_conv_out"], wp=wp, wo=mats["w_o"],
        g_mix=row(vec["norm_mix_g"][l]), g_ffn=row(vec["norm_ffn_g"][l]), gq2=twice(vec["q_norm_g"][l]),
        gk2=twice(vec["k_norm_g"][l]), scale=row(vec["pool_scale"][l]), fb=fb, cw=cw)


def _layer_fwd(x, w, l, comm):
    (proj, h), half_a = norm_matmul(x, w["g_mix"], w["wt_in"], N_MAIN, f"in_proj_{l}", comm.gather_ici(l + 1, IN_PROJ_PART))
    qa, ka, va, vt, z = attn_prep(proj, h, w["wt_in"], w["fb"], w["gq2"], w["gk2"], f"attn_prep_{l}")
    (oa, lse), half_b = attn_forward(qa, ka, vt, f"attn_fwd_{l}", comm.gather_ici(l + 1, ATTN_PART))
    x1, half_c = mix_fwd(proj, oa, x, w["wa"], w["wc"], w["wp"], w["scale"], w["cw"], w["wo"], f"mix_fwd_{l}",
                         comm.gather_ici(l + 1, MIX_PART))
    half = list(half_a) + list(half_b) + list(half_c)
    (gu, h2), gathered = norm_matmul(x1, w["g_ffn"], w["wt_ffn_in"], 2 * D_FF, f"ffn_in_{l}", comm.gather_d2d(l + 1, half))
    x2 = swiglu_matmul(gu, w["w_ffn_out"], x1, f"ffn_out_{l}")
    saved = dict(x=x, proj=proj, h=h, z=z, qa=qa, ka=ka, va=va, oa=oa, lse=lse, x1=x1, gu=gu, h2=h2)
    return x2, saved, gathered


def _layer_bwd(dx2, sv, w, l, comm):
    g = {}
    (dgu, act), stage = swiglu_bwd(dx2, sv["gu"], w["w_ffn_out"], f"ffn_out_bwd_{l}", comm.scatter_d2d(l + 1))
    sums = comm.pair_sums(l + 1, stage)
    g["w_ffn_out"] = tn_matmul(act, dx2, f"dw_ffn_out_{l}")
    g["w_ffn_in"] = tn_matmul(dgu, sv["h2"], f"dw_ffn_in_{l}")
    (dx1, dg), _ = matmul_normbwd(dgu, w["wt_ffn_in"], sv["x1"], w["g_ffn"], dx2, f"ffn_in_bwd_{l}")
    g["norm_ffn_g"] = dg[0]

    (dproj, doa, a_tok, merged, dya, dyc, dyp, uc, dd, dscale, dcw) = mix_bwd(
        sv["proj"], sv["oa"], dx1, w["wa"], w["wc"], w["wp"], w["scale"], w["cw"], w["wo"], f"mix_bwd_{l}")
    g["w_o"] = tn_matmul(merged, dx1, f"dw_o_{l}")
    g["w_attn_out"], g["w_conv_out"], dwp = tn_matmuls([(a_tok, dya), (uc, dyc), (dd, dyp)], f"dw_branches_{l}")
    g["pool_w"] = _pool_from_block_diag(dwp)
    g["pool_scale"] = dscale[0]
    g["conv_w"] = dcw[:3]

    early = comm.early(l)
    comm.grads(l, g)
    above = comm.scatter_ici(l + 1, sums)
    (dqa, dka, dva), got = attn_backward(sv["qa"], sv["ka"], sv["va"], sv["oa"], doa, sv["lse"], f"attn_bwd_{l}",
                                         join_exchanges(above, comm.scatter_d2d(l, early) if early else None))
    n_above = len(above.out_shapes) if above else 0
    comm.scattered(got[:n_above])
    early_sums = dict(zip(early, comm.pair_sums(l, got[n_above:], early))) if early else {}
    early_ici = lambda which: comm.scatter_ici(l, [early_sums[t] for t in which], which) if early else None
    dproj, dgq, dgk, db = attn_post(dqa, dka, dva, sv["proj"], sv["z"], w["gq2"], w["gk2"], dproj, f"attn_post_{l}")
    g["q_norm_g"] = dgq[0, :HEAD_DIM] + dgq[0, HEAD_DIM:]
    g["k_norm_g"] = dgk[0, :HEAD_DIM] + dgk[0, HEAD_DIM:]
    g["forget_b"] = db[0, :HEADS]

    dw_in = tn_matmul(dproj, sv["h"], f"dw_in_{l}", m_cols=N_FULL, ex=early_ici(EARLY_FIRST))
    if early:
        dw_in, got = dw_in
        comm.scattered(got, EARLY_FIRST)
    g["w_in"] = _ungroup_w_in(dw_in)
    comm.grads(l, g)
    (dx, dg), got = matmul_normbwd(
        dproj, w["wt_in"], sv["x"], w["g_mix"], dx1, f"in_proj_bwd_{l}", k=N_FULL,
        ex=join_exchanges(early_ici(EARLY_SECOND), comm.scatter_d2d(l, LATE) if early else None))
    if early:
        comm.scattered(got[:len(EARLY_SECOND)], EARLY_SECOND)
        comm.late_stage(got[len(EARLY_SECOND):])
    g["norm_mix_g"] = dg[0]
    comm.grads(l, g)
    return dx


def _local_step(x, tgt, comm):
    ws, saved = [], []
    w = comm.weights(0, None)
    for l in range(DEPTH):
        ws.append(w)
        x, sv, gathered = _layer_fwd(x, w, l, comm)
        saved.append(sv)
        if l + 1 < DEPTH:
            w = comm.weights(l + 1, gathered)
    sq, dx = loss_kernel(x, tgt, "loss")
    for l in reversed(range(DEPTH)):
        dx = _layer_bwd(dx, saved[l], ws[l], l, comm)
    return sq[0, 0], dx


def kernel(x, norm_mix_g, w_in, forget_b, q_norm_g, k_norm_g, w_attn_out, conv_w, w_conv_out, pool_w, pool_scale, w_o, norm_ffn_g, w_ffn_in, w_ffn_out, loss_target, m_norm_mix_g, m_w_in, m_forget_b, m_q_norm_g, m_k_norm_g, m_w_attn_out, m_conv_w, m_w_conv_out, m_pool_w, m_pool_scale, m_w_o, m_norm_ffn_g, m_w_ffn_in, m_w_ffn_out, v_norm_mix_g, v_w_in, v_forget_b, v_q_norm_g, v_k_norm_g, v_w_attn_out, v_conv_w, v_w_conv_out, v_pool_w, v_pool_scale, v_w_o, v_norm_ffn_g, v_w_ffn_in, v_w_ffn_out):
    w = dict(norm_mix_g=norm_mix_g, w_in=w_in, forget_b=forget_b, q_norm_g=q_norm_g, k_norm_g=k_norm_g,
             w_attn_out=w_attn_out, conv_w=conv_w, w_conv_out=w_conv_out, pool_w=pool_w, pool_scale=pool_scale,
             w_o=w_o, norm_ffn_g=norm_ffn_g, w_ffn_in=w_ffn_in, w_ffn_out=w_ffn_out)
    m = dict(norm_mix_g=m_norm_mix_g, w_in=m_w_in, forget_b=m_forget_b, q_norm_g=m_q_norm_g, k_norm_g=m_k_norm_g,
             w_attn_out=m_w_attn_out, conv_w=m_conv_w, w_conv_out=m_w_conv_out, pool_w=m_pool_w,
             pool_scale=m_pool_scale, w_o=m_w_o, norm_ffn_g=m_norm_ffn_g, w_ffn_in=m_w_ffn_in, w_ffn_out=m_w_ffn_out)
    v = dict(norm_mix_g=v_norm_mix_g, w_in=v_w_in, forget_b=v_forget_b, q_norm_g=v_q_norm_g, k_norm_g=v_k_norm_g,
             w_attn_out=v_w_attn_out, conv_w=v_conv_w, w_conv_out=v_w_conv_out, pool_w=v_pool_w,
             pool_scale=v_pool_scale, w_o=v_w_o, norm_ffn_g=v_norm_ffn_g, w_ffn_in=v_w_ffn_in, w_ffn_out=v_w_ffn_out)
    me = 4 * lax.axis_index("x") + 2 * lax.axis_index("y") + lax.axis_index("c")
    layer_shard = {n: SHARD_INFO[n][0][1:] for n in MATRICES}
    cut_axis = {n: SHARD_INFO[n][1] - 1 for n in MATRICES}

    vec = {n: w[n] for n in VECTORS}
    rc = {n: (_size(layer_shard[n][:-1]), layer_shard[n][-1]) for n in MATRICES}

    class Comm:
        bufs = [lax.empty((DEPTH, len(SAME_CORE)) + layer_shard[n], BF16) for n in MATRICES]
        blocks = [None] * DEPTH
        small_g = [None] * DEPTH
        conv_full = None

        @staticmethod
        def shards(l):
            return [_handled(n, w[n])[l].astype(BF16) for n in MATRICES]

        @staticmethod
        def gather_ici(l, part):
            return gather_over_ici([Comm.shards(l)[t] for t in part]) if l < DEPTH else None

        @staticmethod
        def gather_d2d(l, half):
            return gather_over_d2d(half) if l < DEPTH else None

        @staticmethod
        def weights(l, gathered):
            if l == 0:
                *gathered, conv_g = all_gather(Comm.shards(0) + [_pack([conv_w], 8, 128)], "gather_0")
                Comm.conv_full = _join_shards(jnp.stack([_unpack(conv_g[i], [conv_w.shape])[0] for i in range(N_DEV)]), 2)
            mats = {n: t if n == "w_in" else _join_shards(t, cut_axis[n]) for n, t in zip(MATRICES, gathered)}
            return _layer_weights(mats, vec, Comm.conv_full, l)

        @staticmethod
        def grads(l, g):
            Comm.small_g[l] = g
            Comm.blocks[l] = [None if n not in g else g[n] if n == "w_in" else _cut_shards(g[n], cut_axis[n])
                              for n in MATRICES]

        @staticmethod
        def early(l):
            return EARLY if l == 0 else None

        @staticmethod
        def scatter_d2d(l, which=EVERY):
            return scatter_over_d2d([Comm.blocks[l][t] for t in which]) if l < DEPTH else None

        @staticmethod
        def pair_sums(l, stage, which=EVERY):
            if l >= DEPTH:
                return None
            return [pair_sum(Comm.blocks[l][t].reshape((N_DEV,) + rc[MATRICES[t]]),
                             s.reshape((len(SAME_CORE),) + rc[MATRICES[t]]), me,
                             f"pair_sum_{MATRICES[t]}_{l}").reshape(s.shape) for t, s in zip(which, stage)]

        @staticmethod
        def scatter_ici(l, sums, which=EVERY):
            return scatter_over_ici(sums, [Comm.bufs[t] for t in which], l) if l < DEPTH else None

        @staticmethod
        def scattered(results, which=EVERY):
            for t, r in zip(which or (), results):
                Comm.bufs[t] = r

        @staticmethod
        def late_stage(stage):
            sums = Comm.pair_sums(0, stage, LATE)
            Comm.scattered(run_exchange(Comm.scatter_ici(0, sums, LATE), "scatter_ici_0"), LATE)

    small_g, received = Comm.small_g, Comm
    sq, dx = _local_step(x[0], loss_target[0], Comm)

    big = {}
    for n, parts in zip(MATRICES, received.bufs):
        outs = adamw_sum(parts.reshape((DEPTH, len(SAME_CORE)) + rc[n]),
                         *[_handled(n, d[n]).reshape((DEPTH,) + rc[n]) for d in (w, m, v)], f"adamw_{n}")
        big[n] = [_handled(n, t.reshape((DEPTH,) + layer_shard[n])) for t in outs]

    small_shapes = [VECTOR_SHAPES[n] for n in VECTORS] + [CONV_W_FULL, (1,)]
    stacked = [jnp.stack([small_g[l][n] for l in range(DEPTH)]) for n in VECTORS + ("conv_w",)] + [sq.reshape(1)]
    sparts = all_gather([_pack(stacked, SMALL_ROWS, 128)], "gather_vector_grads")[0]
    col0 = me * (D_CONV // N_DEV)
    place = lambda t: lax.dynamic_update_slice(jnp.zeros(CONV_W_FULL, F32), t, (0, 0, col0))
    spacked = [_pack([d[n] for n in VECTORS] + [place(d["conv_w"]), jnp.zeros((1,), F32)], SMALL_ROWS, 128)[None]
               for d in (w, m, v)]
    small = [_unpack(t[0], small_shapes) for t in adamw_sum(sparts[None], *spacked, "adamw_vectors")]
    loss = (0.5 / D_MODEL) * small[0][-1][0]

    def result(kind):
        out = {n: big[n][kind] for n in MATRICES}
        out.update({n: small[kind][j] for j, n in enumerate(VECTORS)})
        out["conv_w"] = lax.dynamic_slice(small[kind][len(VECTORS)], (0, 0, col0), conv_w.shape)
        return [out[n] for n in w]

    return (loss, dx[None], *result(0), *result(1), *result(2), *result(3))
```

```python
import functools

import jax
import jax.numpy as jnp
from jax import lax
from jax.experimental import pallas as pl
from jax.experimental.pallas import tpu as pltpu

F32 = jnp.float32
BF16 = jnp.bfloat16

N_DEV = 8
DEPTH = 4
D_MODEL = 1024
HEAD_DIM = 64
HEADS = 8
D_ATTN = 512
D_CONV = 256
D_POOL = 256
D_FF = 2816
D_IN = 5640
EPS = 1e-6
ATTN_SCALE = HEAD_DIM ** -0.5

N_REST = 4096
N_MAIN = 5632
N_FULL = 5760
DPROJ_TAIL = 2048
DPROJ_COLS = N_REST + DPROJ_TAIL
FF_BLK = 256
N_FF_BLKS = D_FF // FF_BLK
HALO = 16

ADAM_LR = 0.001
ADAM_B1 = 0.9
ADAM_B2 = 0.999
ADAM_EPS = 1e-08
ADAM_WD = 0.01
ADAM_STEP = 10

SMALL_ROWS = 128

VMEM_LIMIT = 48 * 2 ** 20


def _cparams(sem, vmem=None):
    return pltpu.CompilerParams(dimension_semantics=sem, vmem_limit_bytes=vmem or VMEM_LIMIT)


def _pick(n, cands):
    for c in cands:
        if n % c == 0:
            return c
    raise ValueError(f"no tile for {n}")


def _tile(n, cap):
    t = min(cap, n)
    assert n % t == 0, (n, cap)
    return t


def _sigmoid(v):
    return 1.0 / (1.0 + jnp.exp(-v))


def _rstd(v):
    return lax.rsqrt(jnp.mean(v * v, axis=-1, keepdims=True) + EPS)


def _dot(a, b):
    return jnp.dot(a, b, preferred_element_type=F32)


def _dot_tn(a, b):
    return lax.dot_general(a, b, (((0,), (0,)), ((), ())), preferred_element_type=F32)


def _dot_nt(a, b):
    return lax.dot_general(a, b, (((1,), (1,)), ((), ())), preferred_element_type=F32)


def norm_matmul(x, g, wt, n_cols, name, ex=None):
    s, d = x.shape
    tm, tn = _tile(s, 1024), _pick(n_cols, (2816, 1408, 512))

    def body(x_ref, g_ref, w_ref, o_ref, h_ref):
        @pl.when(pl.program_id(1) == 0)
        def _():
            xv = x_ref[...]
            h_ref[...] = (xv * _rstd(xv) * g_ref[...]).astype(BF16)

        o_ref[...] = _dot_nt(h_ref[...], w_ref[...]).astype(BF16)

    return _carried_call(
        body, ex, (s // tm, n_cols // tn),
        [pl.BlockSpec((tm, d), lambda i, j: (i, 0)), pl.BlockSpec((1, d), lambda i, j: (0, 0)),
         pl.BlockSpec((tn, d), lambda i, j: (j, 0))],
        [pl.BlockSpec((tm, tn), lambda i, j: (i, j)), pl.BlockSpec((tm, d), lambda i, j: (i, 0))],
        [jax.ShapeDtypeStruct((s, n_cols), BF16), jax.ShapeDtypeStruct((s, d), BF16)], [],
        ("arbitrary", "arbitrary"), name, (x, g, wt))


def tn_matmul(a, b, name, m_cols=None, ex=None):
    t = a.shape[0]
    m = m_cols or a.shape[1]
    n = b.shape[1]
    tk = _tile(t, 1024)
    tmm = _pick(m, (1408, 1152, 1024, 512, 256))
    tn = _pick(n, (1408, 1152, 1024, 512, 128))
    nk = t // tk

    def body(a_ref, b_ref, o_ref, acc_ref):
        @pl.when(pl.program_id(2) == 0)
        def _():
            acc_ref[...] = jnp.zeros_like(acc_ref)

        acc_ref[...] += _dot_tn(a_ref[...].astype(BF16), b_ref[...].astype(BF16))

        @pl.when(pl.program_id(2) == nk - 1)
        def _():
            o_ref[...] = acc_ref[...].astype(BF16)

    if ex is None:
        return pl.pallas_call(
            body, grid=(m // tmm, n // tn, nk),
            in_specs=[pl.BlockSpec((tk, tmm), lambda i, j, k: (k, i)), pl.BlockSpec((tk, tn), lambda i, j, k: (k, j))],
            out_specs=pl.BlockSpec((tmm, tn), lambda i, j, k: (i, j)),
            out_shape=jax.ShapeDtypeStruct((m, n), BF16), scratch_shapes=[pltpu.VMEM((tmm, tn), F32)],
            compiler_params=_cparams(("parallel", "parallel", "arbitrary")), name=name)(a, b)
    (out,), carried = _carried_call(
        body, ex, (m // tmm, n // tn, nk),
        [pl.BlockSpec((tk, tmm), lambda i, j, k: (k, i)), pl.BlockSpec((tk, tn), lambda i, j, k: (k, j))],
        [pl.BlockSpec((tmm, tn), lambda i, j, k: (i, j))], [jax.ShapeDtypeStruct((m, n), BF16)],
        [pltpu.VMEM((tmm, tn), F32)], ("arbitrary", "arbitrary", "arbitrary"), name, (a, b))
    return out, carried


def tn_matmuls(pairs, name):
    t = pairs[0][0].shape[0]
    tk = _tile(t, 1024)
    nk = t // tk
    n = len(pairs)

    def body(*refs):
        ins, outs, accs = refs[:2 * n], refs[2 * n:3 * n], refs[3 * n:]

        @pl.when(pl.program_id(0) == 0)
        def _():
            for acc in accs:
                acc[...] = jnp.zeros_like(acc)

        for i in range(n):
            accs[i][...] += _dot_tn(ins[2 * i][...], ins[2 * i + 1][...])

        @pl.when(pl.program_id(0) == nk - 1)
        def _():
            for out, acc in zip(outs, accs):
                out[...] = acc[...].astype(BF16)

    shapes = [(a.shape[1], b.shape[1]) for a, b in pairs]
    return pl.pallas_call(
        body, grid=(nk,),
        in_specs=[pl.BlockSpec((tk, t_.shape[1]), lambda k: (k, 0)) for pair in pairs for t_ in pair],
        out_specs=[pl.BlockSpec(shp, lambda k: (0, 0)) for shp in shapes],
        out_shape=[jax.ShapeDtypeStruct(shp, BF16) for shp in shapes],
        scratch_shapes=[pltpu.VMEM(shp, F32) for shp in shapes],
        compiler_params=_cparams(("arbitrary",)), name=name)(*[t_ for pair in pairs for t_ in pair])


def matmul_normbwd(a, wt, x, g, dres, name, k=None, ex=None):
    s = a.shape[0]
    k = k or a.shape[1]
    d = wt.shape[1]
    tm = _tile(s, 1024)
    tk = _pick(k, (1408, 1152, 512))
    nk = k // tk

    def body(a_ref, w_ref, x_ref, g_ref, r_ref, dx_ref, dg_ref, acc_ref):
        i, kk = pl.program_id(0), pl.program_id(1)

        @pl.when(kk == 0)
        def _():
            acc_ref[...] = jnp.zeros_like(acc_ref)

        @pl.when((i == 0) & (kk == 0))
        def _():
            dg_ref[...] = jnp.zeros_like(dg_ref)

        acc_ref[...] += _dot(a_ref[...], w_ref[...])

        @pl.when(kk == nk - 1)
        def _():
            xv = x_ref[...]
            r = _rstd(xv)
            y = xv * r
            dh = acc_ref[...]
            dy = dh * g_ref[...]
            dx_ref[...] = r_ref[...] + r * (dy - y * jnp.mean(dy * y, axis=-1, keepdims=True))
            dg_ref[...] += jnp.sum(dh * y, axis=0, keepdims=True)

    return _carried_call(
        body, ex, (s // tm, nk),
        [pl.BlockSpec((tm, tk), lambda i, kk: (i, kk)), pl.BlockSpec((tk, d), lambda i, kk: (kk, 0)),
         pl.BlockSpec((tm, d), lambda i, kk: (i, 0)), pl.BlockSpec((1, d), lambda i, kk: (0, 0)),
         pl.BlockSpec((tm, d), lambda i, kk: (i, 0))],
        [pl.BlockSpec((tm, d), lambda i, kk: (i, 0)), pl.BlockSpec((1, d), lambda i, kk: (0, 0))],
        [jax.ShapeDtypeStruct((s, d), F32), jax.ShapeDtypeStruct((1, d), F32)],
        [pltpu.VMEM((tm, d), F32)], ("arbitrary", "arbitrary"), name, (a, wt, x, g, dres), vmem=56 * 2 ** 20)


def swiglu_matmul(gu, w, x1, name):
    s = gu.shape[0]
    d = w.shape[1]
    tm = _tile(s, 512)

    def body(gu_ref, w_ref, x_ref, o_ref):
        acc = x_ref[...]
        for j in range(N_FF_BLKS):
            gt = gu_ref[:, j * FF_BLK:(j + 1) * FF_BLK].astype(F32)
            up = gu_ref[:, D_FF + j * FF_BLK:D_FF + (j + 1) * FF_BLK].astype(F32)
            act = (gt * _sigmoid(gt) * up).astype(BF16)
            acc += _dot(act, w_ref[j * FF_BLK:(j + 1) * FF_BLK, :])
        o_ref[...] = acc

    return pl.pallas_call(
        body, grid=(s // tm,),
        in_specs=[pl.BlockSpec((tm, 2 * D_FF), lambda i: (i, 0)), pl.BlockSpec((D_FF, d), lambda i: (0, 0)),
                  pl.BlockSpec((tm, d), lambda i: (i, 0))],
        out_specs=pl.BlockSpec((tm, d), lambda i: (i, 0)),
        out_shape=jax.ShapeDtypeStruct((s, d), F32),
        compiler_params=_cparams(("parallel",)), name=name)(gu, w, x1)


def swiglu_bwd(dx2, gu, w, name, ex=None):
    s, d = dx2.shape
    tm = _tile(s, 512)

    def body(dx_ref, gu_ref, w_ref, dgu_ref, act_ref):
        dx = dx_ref[...].astype(BF16)
        for j in range(N_FF_BLKS):
            g_cols = slice(j * FF_BLK, (j + 1) * FF_BLK)
            u_cols = slice(D_FF + j * FF_BLK, D_FF + (j + 1) * FF_BLK)
            dact = _dot_nt(dx, w_ref[j * FF_BLK:(j + 1) * FF_BLK, :])
            gt = gu_ref[:, g_cols].astype(F32)
            up = gu_ref[:, u_cols].astype(F32)
            sg = _sigmoid(gt)
            silu = gt * sg
            act_ref[:, j * FF_BLK:(j + 1) * FF_BLK] = (silu * up).astype(BF16)
            dgu_ref[:, g_cols] = (dact * up * (sg + silu * (1.0 - sg))).astype(BF16)
            dgu_ref[:, u_cols] = (dact * silu).astype(BF16)

    return _carried_call(
        body, ex, (s // tm,),
        [pl.BlockSpec((tm, d), lambda i: (i, 0)), pl.BlockSpec((tm, 2 * D_FF), lambda i: (i, 0)),
         pl.BlockSpec((D_FF, d), lambda i: (0, 0), pipeline_mode=pl.Buffered(1))],
        [pl.BlockSpec((tm, 2 * D_FF), lambda i: (i, 0)), pl.BlockSpec((tm, D_FF), lambda i: (i, 0))],
        [jax.ShapeDtypeStruct((s, 2 * D_FF), BF16), jax.ShapeDtypeStruct((s, D_FF), BF16)], [],
        ("arbitrary",), name, (dx2, gu, w), vmem=56 * 2 ** 20)


def loss_kernel(y, tgt, name):
    s, d = y.shape
    tm = _tile(s, 512)

    def body(y_ref, t_ref, l_ref, dy_ref):
        @pl.when(pl.program_id(0) == 0)
        def _():
            l_ref[...] = jnp.zeros_like(l_ref)

        err = y_ref[...] - t_ref[...]
        dy_ref[...] = err * (1.0 / d)
        l_ref[...] += jnp.sum(jnp.sum(err * err, axis=1, keepdims=True), axis=0, keepdims=True)

    return pl.pallas_call(
        body, grid=(s // tm,),
        in_specs=[pl.BlockSpec((tm, d), lambda i: (i, 0)), pl.BlockSpec((tm, d), lambda i: (i, 0))],
        out_specs=[pl.BlockSpec((8, 128), lambda i: (0, 0)), pl.BlockSpec((tm, d), lambda i: (i, 0))],
        out_shape=[jax.ShapeDtypeStruct((8, 128), F32), jax.ShapeDtypeStruct((s, d), F32)],
        compiler_params=_cparams(("arbitrary",)), name=name)(y, tgt)


def _split3(v):
    a1 = v.astype(BF16)
    r1 = v - a1.astype(F32)
    a2 = r1.astype(BF16)
    a3 = (r1 - a2.astype(F32)).astype(BF16)
    return a1, a2, a3


def _running_sum(v, carry_ref, reverse):
    tm = v.shape[0]
    row = lax.broadcasted_iota(jnp.int32, (tm, tm), 0)
    col = lax.broadcasted_iota(jnp.int32, (tm, tm), 1)
    tri = ((col >= row) if reverse else (row >= col)).astype(BF16)
    a1, a2, a3 = _split3(v)
    out = _dot(tri, a1) + _dot(tri, a2) + _dot(tri, a3) + carry_ref[...]
    carry_ref[...] = out[0:1, :] if reverse else out[tm - 1:tm, :]
    return out


HEAD_GROUP_FWD = 8
HEAD_GROUP_BWD = 8
LANE_C = 64
LANE_ONE = 67


def _lanes():
    lane = lax.broadcasted_iota(jnp.int32, (1, 128), 1)
    return lane, lane < HEAD_DIM


def _half_mean(t, lo):
    s_lo = jnp.sum(jnp.where(lo, t, 0.0), axis=-1, keepdims=True)
    s_hi = jnp.sum(jnp.where(lo, 0.0, t), axis=-1, keepdims=True)
    return jnp.where(lo, s_lo, s_hi) * (1.0 / HEAD_DIM)


def _lane_col(t, lane, idx):
    return jnp.sum(jnp.where(lane == idx, t, 0.0), axis=-1, keepdims=True)


def _swap_halves(t):
    return pltpu.roll(t, HEAD_DIM, 1)


def attn_prep(proj, h, wt_in, fb, gq2, gk2, name):
    s, d = h.shape
    tm = _tile(s, 512)
    first = N_REST // D_ATTN

    def body(q_ref, k_ref, v_ref, h_ref, wf_ref, fb_ref, gq_ref, gk_ref, qa_ref, ka_ref, va_ref, vt_ref, z_ref, carry_ref):
        lane, lo = _lanes()

        @pl.when(pl.program_id(0) == 0)
        def _():
            carry_ref[...] = jnp.zeros_like(carry_ref)

        z = _dot_nt(h_ref[...], wf_ref[...]) + fb_ref[...]
        z_ref[...] = z
        cv = _running_sum(jnp.minimum(z, 0.0) - jnp.log(1.0 + jnp.exp(-jnp.abs(z))), carry_ref, reverse=False)

        def normed(t, g):
            t = t.astype(F32)
            return t * lax.rsqrt(_half_mean(t * t, lo) + EPS) * g

        one_q = jnp.where((lane >= LANE_ONE) & (lane < LANE_ONE + 3), 1.0, 0.0)
        one_k = jnp.where((lane >= LANE_C) & (lane < LANE_C + 3), 1.0, 0.0)
        one_v = jnp.where(lane == LANE_C, 1.0, 0.0)
        for j in range(HEADS // 2):
            cols = slice(128 * j, 128 * (j + 1))
            qn = normed(q_ref[:, cols], gq_ref[...] * ATTN_SCALE)
            kn = normed(k_ref[:, cols], gk_ref[...])
            vv = v_ref[:, cols].astype(F32)
            for e in range(2):
                h = 2 * j + e
                pick = (lambda t: t) if e == 0 else _swap_halves
                pieces = [p.astype(F32) for p in _split3(_lane_col(cv, lane, h))]
                ext_q, ext_k = one_q, one_k
                for i, p in enumerate(pieces):
                    ext_q = jnp.where(lane == LANE_C + i, p, ext_q)
                    ext_k = jnp.where(lane == LANE_ONE + i, -p, ext_k)
                qa_ref[h] = jnp.where(lo, pick(qn), ext_q).astype(BF16)
                ka_ref[h] = jnp.where(lo, pick(kn), ext_k).astype(BF16)
                va = jnp.where(lo, pick(vv), one_v)
                va_ref[h] = va.astype(BF16)
                vt_ref[h] = va.T.astype(BF16)

    tile = lambda blk: pl.BlockSpec((tm, D_ATTN), lambda i: (i, blk))
    vec = pl.BlockSpec((1, 128), lambda i: (0, 0))
    out = pl.BlockSpec((HEADS, tm, 128), lambda i: (0, i, 0))
    return pl.pallas_call(
        body, grid=(s // tm,),
        in_specs=[tile(first), tile(first + 1), tile(first + 2), pl.BlockSpec((tm, d), lambda i: (i, 0)),
                  pl.BlockSpec((128, d), lambda i: (N_MAIN // 128, 0)), vec, vec, vec],
        out_specs=[out, out, out, pl.BlockSpec((HEADS, 128, tm), lambda i: (0, 0, i)),
                   pl.BlockSpec((tm, 128), lambda i: (i, 0))],
        out_shape=[jax.ShapeDtypeStruct((HEADS, s, 128), BF16)] * 3 + [jax.ShapeDtypeStruct((HEADS, 128, s), BF16),
                                                                       jax.ShapeDtypeStruct((s, 128), F32)],
        scratch_shapes=[pltpu.VMEM((1, 128), F32)],
        compiler_params=_cparams(("arbitrary",)), name=name)(proj, proj, proj, h, wt_in, fb, gq2, gk2)


def _carry(ex, n_in, n_out, n_scratch, grid):
    n_xin, n_xout = (len(ex.inputs), len(ex.out_shapes)) if ex else (0, 0)

    def split(refs):
        ins, xins = refs[:n_in], refs[n_in:n_in + n_xin]
        rest = refs[n_in + n_xin:]
        outs, xouts = rest[:n_out], rest[n_out:n_out + n_xout]
        rest = rest[n_out + n_xout:]
        return ins + outs + rest[:n_scratch], (xins, xouts, rest[n_scratch:])

    def first():
        return functools.reduce(lambda a, b: a & b, [pl.program_id(d) == 0 for d in range(len(grid))])

    def last():
        return functools.reduce(lambda a, b: a & b, [pl.program_id(d) == grid[d] - 1 for d in range(len(grid))])

    return split, first, last


def _carried_call(body, ex, grid, in_specs, out_specs, out_shape, scratch, sem, name, operands, vmem=None):
    any_spec = pl.BlockSpec(memory_space=pl.ANY)
    split, first, last = _carry(ex, len(in_specs), len(out_specs), len(scratch), grid)

    def carried(*refs):
        own, xrefs = split(refs)
        if ex:
            @pl.when(first())
            def _():
                ex.start(*xrefs)

        body(*own)
        if ex:
            @pl.when(last())
            def _():
                ex.drain(*xrefs)

    n_xin = len(ex.inputs) if ex else 0
    results = pl.pallas_call(
        carried, grid=grid, in_specs=list(in_specs) + [any_spec] * n_xin,
        out_specs=list(out_specs) + [any_spec] * (len(ex.out_shapes) if ex else 0),
        out_shape=list(out_shape) + (list(ex.out_shapes) if ex else []),
        input_output_aliases={len(in_specs) + i: len(out_specs) + o for i, o in ex.aliases.items()} if ex else {},
        scratch_shapes=list(scratch) + (ex.scratch if ex else []),
        compiler_params=_cparams(sem, vmem), name=name)(*operands, *(ex.inputs if ex else []))
    return results[:len(out_specs)], results[len(out_specs):]


def _tri_rows(t, n):
    qi = sum(jnp.where(t >= r * (r + 1) // 2, 1, 0) for r in range(1, n))
    return qi, t - qi * (qi + 1) // 2


def _tri_cols(t, n):
    ki = sum(jnp.where(t >= r * n - r * (r - 1) // 2, 1, 0) for r in range(1, n))
    return ki, ki + t - (ki * n - ki * (ki - 1) // 2)


def _causal_t(st_blk, tk, tq):
    key = lax.broadcasted_iota(jnp.int32, (tk, tq), 0)
    qry = lax.broadcasted_iota(jnp.int32, (tk, tq), 1)
    return jnp.where(qry >= key, st_blk, -jnp.inf)


def attn_forward(qa, ka, vt, name, ex=None):
    hh, s, _ = qa.shape
    tq = tk = _tile(s, 512)
    nq = s // tq
    grp = HEAD_GROUP_FWD

    def body(q_ref, k_ref, vt_ref, o_ref, lse_ref, m_ref, acc_ref):
        qi, ki = _tri_rows(pl.program_id(1), nq)

        @pl.when(ki == 0)
        def _():
            m_ref[...] = jnp.full_like(m_ref, -jnp.inf)
            acc_ref[...] = jnp.zeros_like(acc_ref)

        def step(masked):
            nxt = _dot_nt(k_ref[0], q_ref[0])
            for g in range(grp):
                st = nxt
                if g + 1 < grp:
                    nxt = _dot_nt(k_ref[g + 1], q_ref[g + 1])
                if masked:
                    st = _causal_t(st, tk, tq)
                m_old = m_ref[g]
                m_new = jnp.maximum(m_old, jnp.max(st, axis=0, keepdims=True))
                pt = jnp.exp(st - m_new).astype(BF16)
                acc_ref[g] = jnp.exp(m_old - m_new) * acc_ref[g] + _dot(vt_ref[g], pt)
                m_ref[g] = m_new

        @pl.when(ki < qi)
        def _():
            step(False)

        @pl.when(ki == qi)
        def _():
            step(True)
            for g in range(grp):
                acc = acc_ref[g]
                denom = acc[LANE_C:LANE_C + 1, :]
                o_ref[g] = (acc / denom).T.astype(BF16)
                lse_ref[g] = m_ref[g] + jnp.log(denom)

    qspec = pl.BlockSpec((grp, tq, 128), lambda h, t: (h, _tri_rows(t, nq)[0], 0))
    kspec = pl.BlockSpec((grp, tk, 128), lambda h, t: (h, _tri_rows(t, nq)[1], 0))
    vspec = pl.BlockSpec((grp, 128, tk), lambda h, t: (h, 0, _tri_rows(t, nq)[1]))
    lspec = pl.BlockSpec((grp, 1, tq), lambda h, t: (h, 0, _tri_rows(t, nq)[0]))
    return _carried_call(
        body, ex, (hh // grp, nq * (nq + 1) // 2), [qspec, kspec, vspec], [qspec, lspec],
        [jax.ShapeDtypeStruct((hh, s, 128), BF16), jax.ShapeDtypeStruct((hh, 1, s), F32)],
        [pltpu.VMEM((grp, 1, tq), F32), pltpu.VMEM((grp, 128, tq), F32)],
        ("arbitrary", "arbitrary"), name, (qa, ka, vt))


def attn_backward(qa, ka, va, oa, doa, lse, name, ex=None):
    hh, s, _ = qa.shape
    tq = tk = _tile(s, 512)
    nq = s // tq
    grp = HEAD_GROUP_BWD

    def body(q_ref, k_ref, v_ref, o_ref, do_ref, lse_ref, dq_ref, dk_ref, dv_ref, dka_ref, dva_ref):
        ki, qi = _tri_cols(pl.program_id(1), nq)

        @pl.when(pl.program_id(1) == 0)
        def _():
            dq_ref[...] = jnp.zeros_like(dq_ref)

        @pl.when(qi == ki)
        def _():
            dka_ref[...] = jnp.zeros_like(dka_ref)
            dva_ref[...] = jnp.zeros_like(dva_ref)

        def step(masked):
            rows = pl.ds(pl.multiple_of(qi * tq, tq), tq)
            products = lambda g: (_dot_nt(k_ref[g], q_ref[g]), _dot_nt(v_ref[g], do_ref[g]))
            nxt = products(0)
            for g in range(grp):
                st, dpt = nxt
                if g + 1 < grp:
                    nxt = products(g + 1)
                q, k, do = q_ref[g], k_ref[g], do_ref[g]
                if masked:
                    st = _causal_t(st, tk, tq)
                pt = jnp.exp(st - lse_ref[g])
                delta = jnp.sum((do.astype(F32) * o_ref[g].astype(F32)).T, axis=0, keepdims=True)
                dst = (pt * (dpt - delta)).astype(BF16)
                dva_ref[g] += _dot(pt.astype(BF16), do)
                dka_ref[g] += _dot(dst, q)
                dq_ref[g, rows, :] += _dot_tn(dst, k)

        @pl.when(qi > ki)
        def _():
            step(False)

        @pl.when(qi == ki)
        def _():
            step(True)

        @pl.when(qi == nq - 1)
        def _():
            dk_ref[...] = dka_ref[...]
            dv_ref[...] = dva_ref[...].astype(BF16)

    qspec = pl.BlockSpec((grp, tq, 128), lambda h, t: (h, _tri_cols(t, nq)[1], 0))
    lspec = pl.BlockSpec((grp, 1, tq), lambda h, t: (h, 0, _tri_cols(t, nq)[1]))
    kspec = pl.BlockSpec((grp, tk, 128), lambda h, t: (h, _tri_cols(t, nq)[0], 0))
    return _carried_call(
        body, ex, (hh // grp, nq * (nq + 1) // 2), [qspec, kspec, kspec, qspec, qspec, lspec],
        [pl.BlockSpec((grp, s, 128), lambda h, t: (h, 0, 0), pipeline_mode=pl.Buffered(1)), kspec, kspec],
        [jax.ShapeDtypeStruct((hh, s, 128), F32), jax.ShapeDtypeStruct((hh, s, 128), F32),
         jax.ShapeDtypeStruct((hh, s, 128), BF16)],
        [pltpu.VMEM((grp, tk, 128), F32), pltpu.VMEM((grp, tk, 128), F32)],
        ("arbitrary", "arbitrary"), name, (qa, ka, va, oa, doa, lse), vmem=58 * 2 ** 20)


def attn_post(dqa, dka, dva, proj, z, gq2, gk2, dproj, name):
    s = proj.shape[0]
    tm = _tile(s, 512)
    nt = s // tm

    def body(dq_ref, dk_ref, dv_ref, q_ref, k_ref, z_ref, gq_ref, gk_ref, dp_any, dp_ref, dgq_ref, dgk_ref, db_ref,
             carry_ref):
        lane, lo = _lanes()

        @pl.when(pl.program_id(0) == 0)
        def _():
            dgq_ref[...] = jnp.zeros_like(dgq_ref)
            dgk_ref[...] = jnp.zeros_like(dgk_ref)
            db_ref[...] = jnp.zeros_like(db_ref)
            carry_ref[...] = jnp.zeros_like(carry_ref)

        def pair(ref, j):
            return jnp.where(lo, ref[2 * j].astype(F32), _swap_halves(ref[2 * j + 1].astype(F32)))

        def norm_bwd(raw, g, dhat, scale):
            r = lax.rsqrt(_half_mean(raw * raw, lo) + EPS)
            y = raw * r
            dy = dhat * (g * scale)
            return r * (dy - y * _half_mean(dy * y, lo)), jnp.sum(dhat * y, axis=0, keepdims=True) * scale

        dc = jnp.zeros((tm, 128), F32)
        for j in range(HEADS // 2):
            cols = slice(128 * j, 128 * (j + 1))
            dq, dgq = norm_bwd(q_ref[:, cols].astype(F32), gq_ref[...], pair(dq_ref, j), ATTN_SCALE)
            dk, dgk = norm_bwd(k_ref[:, cols].astype(F32), gk_ref[...], pair(dk_ref, j), 1.0)
            dgq_ref[...] += dgq
            dgk_ref[...] += dgk
            dp_ref[:, cols] = dq.astype(BF16)
            dp_ref[:, D_ATTN + 128 * j:D_ATTN + 128 * (j + 1)] = dk.astype(BF16)
            dp_ref[:, 2 * D_ATTN + 128 * j:2 * D_ATTN + 128 * (j + 1)] = pair(dv_ref, j).astype(BF16)
            for e in range(2):
                h = 2 * j + e
                both = jnp.where(lane == LANE_C, dq_ref[h], 0.0) - jnp.where(lane == LANE_ONE, dk_ref[h], 0.0)
                dc = jnp.where(lane == h, jnp.sum(both, axis=-1, keepdims=True), dc)
        dz = _running_sum(dc, carry_ref, reverse=True) * (1.0 - _sigmoid(z_ref[...]))
        db_ref[...] += jnp.sum(dz, axis=0, keepdims=True)
        dp_ref[:, 3 * D_ATTN:3 * D_ATTN + 128] = dz.astype(BF16)
        dp_ref[:, 3 * D_ATTN + 128:] = jnp.zeros((tm, DPROJ_TAIL - 3 * D_ATTN - 128), BF16)

    heads = lambda: pl.BlockSpec((HEADS, tm, 128), lambda i: (0, nt - 1 - i, 0))
    vec = pl.BlockSpec((1, 128), lambda i: (0, 0))
    first = N_REST // D_ATTN
    return pl.pallas_call(
        body, grid=(nt,),
        in_specs=[heads(), heads(), heads(), pl.BlockSpec((tm, D_ATTN), lambda i: (nt - 1 - i, first)),
                  pl.BlockSpec((tm, D_ATTN), lambda i: (nt - 1 - i, first + 1)),
                  pl.BlockSpec((tm, 128), lambda i: (nt - 1 - i, 0)), vec, vec, pl.BlockSpec(memory_space=pl.ANY)],
        out_specs=[pl.BlockSpec((tm, DPROJ_TAIL), lambda i: (nt - 1 - i, N_REST // DPROJ_TAIL)), vec, vec, vec],
        out_shape=[jax.ShapeDtypeStruct(dproj.shape, BF16), jax.ShapeDtypeStruct((1, 128), F32),
                   jax.ShapeDtypeStruct((1, 128), F32), jax.ShapeDtypeStruct((1, 128), F32)],
        scratch_shapes=[pltpu.VMEM((1, 128), F32)], input_output_aliases={8: 0},
        compiler_params=_cparams(("arbitrary",)), name=name)(dqa, dka, dva, proj, proj, z, gq2, gk2, dproj)


def _pool_groups(tm):
    gid = lax.broadcasted_iota(jnp.int32, (1, D_POOL), 1) // (D_POOL // 4)
    win = jnp.where(gid == 0, 2.0, jnp.where(gid == 1, 4.0, jnp.where(gid == 2, 8.0, 16.0)))
    return gid, win


def _by_group(gid, v2, v4, v8, v16):
    return jnp.where(gid == 0, v2, jnp.where(gid == 1, v4, jnp.where(gid == 2, v8, v16)))


def _branches(rest_ref, halo_ref, a_ref, wa_ref, wc_ref, wp_ref, sc_ref, cw_ref, ti, tm):
    f = lambda v: v.astype(F32)
    cx, cb, cc, px = f(rest_ref[:, 0:256]), f(rest_ref[:, 256:512]), f(rest_ref[:, 512:768]), f(rest_ref[:, 768:1024])
    live = jnp.where(ti > 0, 1.0, 0.0)
    hz = f(halo_ref[:, 0:256]) * f(halo_ref[:, 512:768]) * live
    hp = f(halo_ref[:, 768:1024]) * live
    z = cc * cx
    zf = jnp.concatenate([hz, z], axis=0)
    z1 = pltpu.roll(zf, 1, 0)[HALO:]
    z2 = pltpu.roll(zf, 2, 0)[HALO:]
    cw = cw_ref[...]
    conv = cw[2:3] * z + cw[1:2] * z1 + cw[0:1] * z2
    uc = cb * conv
    pf = jnp.concatenate([hp, px], axis=0)
    s2 = pf + pltpu.roll(pf, 1, 0)
    s4 = s2 + pltpu.roll(s2, 2, 0)
    s8 = s4 + pltpu.roll(s4, 4, 0)
    s16 = s8 + pltpu.roll(s8, 8, 0)
    gid, win = _pool_groups(tm)
    t = (ti * tm + lax.broadcasted_iota(jnp.int32, (tm, 1), 0)).astype(F32)
    inv = 1.0 / jnp.minimum(t + 1.0, win)
    dpool = _by_group(gid, s2[HALO:], s4[HALO:], s8[HALO:], s16[HALO:]) * inv - px
    _, lo = _lanes()
    a_tok = [jnp.where(lo, f(a_ref[2 * j]), _swap_halves(f(a_ref[2 * j + 1]))).astype(BF16) for j in range(HEADS // 2)]
    y_attn = _dot(a_tok[0], wa_ref[0:128, :])
    for j in range(1, HEADS // 2):
        y_attn += _dot(a_tok[j], wa_ref[128 * j:128 * (j + 1), :])
    y_conv = _dot(uc.astype(BF16), wc_ref[...])
    y_pool_raw = _dot(dpool.astype(BF16), wp_ref[...])
    sg = [_sigmoid(f(rest_ref[:, 1024 + i * D_MODEL:1024 + (i + 1) * D_MODEL])) for i in range(3)]
    return dict(cx=cx, cb=cb, cc=cc, z=z, z1=z1, z2=z2, conv=conv, uc=uc, dpool=dpool, inv=inv, gid=gid, a_tok=a_tok,
                y_attn=y_attn, y_conv=y_conv, y_pool_raw=y_pool_raw, sg=sg, cw=cw)


def _mix_specs(tm, ti_of):
    blocks_per_tile = tm // HALO
    return [
        pl.BlockSpec((tm, N_REST), lambda i: (ti_of(i), 0)),
        pl.BlockSpec((HALO, 1024), lambda i: (jnp.maximum(ti_of(i) * blocks_per_tile - 1, 0), 0)),
        pl.BlockSpec((HEADS, tm, 128), lambda i: (0, ti_of(i), 0)),
        pl.BlockSpec((D_ATTN, D_MODEL), lambda i: (0, 0), pipeline_mode=pl.Buffered(1)),
        pl.BlockSpec((D_CONV, D_MODEL), lambda i: (0, 0), pipeline_mode=pl.Buffered(1)),
        pl.BlockSpec((D_POOL, D_MODEL), lambda i: (0, 0), pipeline_mode=pl.Buffered(1)),
        pl.BlockSpec((1, D_MODEL), lambda i: (0, 0)),
        pl.BlockSpec((8, D_CONV), lambda i: (0, 0)),
    ]


def mix_fwd(proj, a, x, wa, wc, wp, scale, cw, wo, name, ex=None):
    s = x.shape[0]
    tm = _tile(s, 512)

    def body(rest_ref, halo_ref, a_ref, wa_ref, wc_ref, wp_ref, sc_ref, cw_ref, wo_ref, x_ref, o_ref):
        b = _branches(rest_ref, halo_ref, a_ref, wa_ref, wc_ref, wp_ref, sc_ref, cw_ref, pl.program_id(0), tm)
        merged = b["sg"][0] * b["y_attn"] + b["sg"][1] * b["y_conv"] + b["sg"][2] * (b["y_pool_raw"] * sc_ref[...])
        o_ref[...] = x_ref[...] + _dot(merged.astype(BF16), wo_ref[...])

    (x1,), carried = _carried_call(
        body, ex, (s // tm,),
        _mix_specs(tm, lambda i: i) + [pl.BlockSpec((D_MODEL, D_MODEL), lambda i: (0, 0), pipeline_mode=pl.Buffered(1)),
                                       pl.BlockSpec((tm, D_MODEL), lambda i: (i, 0))],
        [pl.BlockSpec((tm, D_MODEL), lambda i: (i, 0))], [jax.ShapeDtypeStruct((s, D_MODEL), F32)], [],
        ("arbitrary",), name, (proj, proj, a, wa, wc, wp, scale, cw, wo, x), vmem=58 * 2 ** 20)
    return x1, carried


def mix_bwd(proj, a, dx1, wa, wc, wp, scale, cw, wo, name):
    s = dx1.shape[0]
    tm = _tile(s, 512)
    nt = s // tm
    ti_of = lambda i: nt - 1 - i
    n = tm + HALO

    def body(rest_ref, halo_ref, a_ref, wa_ref, wc_ref, wp_ref, sc_ref, cw_ref, wo_ref,
             dx_ref, dp_ref, da_ref, at_ref, mg_ref, dya_ref, dyc_ref, dyp_ref, uc_ref, dd_ref, dsc_ref, dcw_ref,
             cdc_ref, cde_ref):
        i = pl.program_id(0)
        ti = ti_of(i)

        @pl.when(i == 0)
        def _():
            cdc_ref[...] = jnp.zeros_like(cdc_ref)
            cde_ref[...] = jnp.zeros_like(cde_ref)
            dsc_ref[...] = jnp.zeros_like(dsc_ref)
            dcw_ref[...] = jnp.zeros_like(dcw_ref)

        b = _branches(rest_ref, halo_ref, a_ref, wa_ref, wc_ref, wp_ref, sc_ref, cw_ref, ti, tm)
        sg, sc = b["sg"], sc_ref[...]
        y_pool = b["y_pool_raw"] * sc
        merged = sg[0] * b["y_attn"] + sg[1] * b["y_conv"] + sg[2] * y_pool
        mg_ref[...] = merged.astype(BF16)
        dm = _dot_nt(dx_ref[...].astype(BF16), wo_ref[...])
        dys = [dm * sg[j] for j in range(3)]
        for j, y in enumerate((b["y_attn"], b["y_conv"], y_pool)):
            dp_ref[:, 1024 + j * D_MODEL:1024 + (j + 1) * D_MODEL] = (dys[j] * y * (1.0 - sg[j])).astype(BF16)
        dya = dys[0].astype(BF16)
        dya_ref[...] = dya
        _, lo = _lanes()
        for j in range(HEADS // 2):
            at_ref[:, 128 * j:128 * (j + 1)] = b["a_tok"][j]
            da = _dot_nt(dya, wa_ref[128 * j:128 * (j + 1), :])
            da_ref[2 * j] = jnp.where(lo, da, 0.0).astype(BF16)
            da_ref[2 * j + 1] = jnp.where(lo, _swap_halves(da), 0.0).astype(BF16)
        dyc = dys[1].astype(BF16)
        dyc_ref[...] = dyc
        duc = _dot_nt(dyc, wc_ref[...])
        dyp = dys[2]
        dsc_ref[...] += jnp.sum(dyp * b["y_pool_raw"], axis=0, keepdims=True)
        dypr = (dyp * sc).astype(BF16)
        dyp_ref[...] = dypr
        ddp = _dot_nt(dypr, wp_ref[...])
        uc_ref[...] = b["uc"].astype(BF16)
        dd_ref[...] = b["dpool"].astype(BF16)

        dconv = duc * b["cb"]
        dp_ref[:, 256:512] = (duc * b["conv"]).astype(BF16)
        dcf = jnp.concatenate([dconv, cdc_ref[...]], axis=0)
        cw = b["cw"]
        dz = cw[2:3] * dconv + cw[1:2] * pltpu.roll(dcf, n - 1, 0)[:tm] + cw[0:1] * pltpu.roll(dcf, n - 2, 0)[:tm]
        dp_ref[:, 0:256] = (dz * b["cc"]).astype(BF16)
        dp_ref[:, 512:768] = (dz * b["cx"]).astype(BF16)
        dcw_ref[0:1, :] += jnp.sum(dconv * b["z2"], axis=0, keepdims=True)
        dcw_ref[1:2, :] += jnp.sum(dconv * b["z1"], axis=0, keepdims=True)
        dcw_ref[2:3, :] += jnp.sum(dconv * b["z"], axis=0, keepdims=True)
        cdc_ref[...] = dconv[:HALO]

        e = ddp * b["inv"]
        ef = jnp.concatenate([e, cde_ref[...]], axis=0)
        r2 = ef + pltpu.roll(ef, n - 1, 0)
        r4 = r2 + pltpu.roll(r2, n - 2, 0)
        r8 = r4 + pltpu.roll(r4, n - 4, 0)
        r16 = r8 + pltpu.roll(r8, n - 8, 0)
        dp_ref[:, 768:1024] = (_by_group(b["gid"], r2[:tm], r4[:tm], r8[:tm], r16[:tm]) - ddp).astype(BF16)
        cde_ref[...] = e[:HALO]

    tile = lambda w: pl.BlockSpec((tm, w), lambda i: (ti_of(i), 0))
    whole = lambda r, c: pl.BlockSpec((r, c), lambda i: (0, 0))
    bf = lambda w: jax.ShapeDtypeStruct((s, w), BF16)
    return pl.pallas_call(
        body, grid=(nt,),
        in_specs=_mix_specs(tm, ti_of) + [pl.BlockSpec((D_MODEL, D_MODEL), lambda i: (0, 0), pipeline_mode=pl.Buffered(1)),
                                          tile(D_MODEL)],
        out_specs=[tile(N_REST), pl.BlockSpec((HEADS, tm, 128), lambda i: (0, ti_of(i), 0)), tile(D_ATTN),
                   tile(D_MODEL), tile(D_MODEL), tile(D_MODEL), tile(D_MODEL),
                   tile(D_CONV), tile(D_POOL), whole(1, D_MODEL), whole(8, D_CONV)],
        out_shape=[bf(DPROJ_COLS), jax.ShapeDtypeStruct((HEADS, s, 128), BF16), bf(D_ATTN),
                   bf(D_MODEL), bf(D_MODEL), bf(D_MODEL), bf(D_MODEL), bf(D_CONV), bf(D_POOL),
                   jax.ShapeDtypeStruct((1, D_MODEL), F32), jax.ShapeDtypeStruct((8, D_CONV), F32)],
        scratch_shapes=[pltpu.VMEM((HALO, D_CONV), F32), pltpu.VMEM((HALO, D_POOL), F32)],
        compiler_params=_cparams(("arbitrary",), 58 * 2 ** 20), name=name)(proj, proj, a, wa, wc, wp, scale, cw, wo, dx1)


def _adamw_math(w, g, m, v):
    m = ADAM_B1 * m + (1.0 - ADAM_B1) * g
    v = ADAM_B2 * v + (1.0 - ADAM_B2) * (g * g)
    m_hat = m / (1.0 - ADAM_B1 ** ADAM_STEP)
    v_hat = v / (1.0 - ADAM_B2 ** ADAM_STEP)
    delta = -ADAM_LR * (m_hat / (jnp.sqrt(v_hat) + ADAM_EPS) + ADAM_WD * w)
    return delta, m, v


ADAMW_PARTS_BLOCK_BYTES = 4 * 2 ** 20


def _row_tile(rows, cols, copies, itemsize):
    row_bytes = copies * (-(-cols // 128) * 128) * itemsize
    fits = [t for t in range(16, rows + 1, 16) if rows % t == 0 and t * row_bytes <= ADAMW_PARTS_BLOCK_BYTES]
    return max(fits) if fits else rows


def pair_sum(blocks, stage, me, name):
    n_slots, rows, cols = stage.shape
    tr = _row_tile(rows, cols, 1, 4)

    def body(me_ref, a_ref, b_ref, o_ref):
        o_ref[...] = (a_ref[...].astype(F32) + b_ref[...].astype(F32)).astype(BF16)

    slot = pl.BlockSpec((None, tr, cols), lambda i, r, me_ref: (i, r, 0))
    return pl.pallas_call(
        body, out_shape=jax.ShapeDtypeStruct(stage.shape, BF16),
        grid_spec=pltpu.PrefetchScalarGridSpec(
            num_scalar_prefetch=1, grid=(n_slots, rows // tr),
            in_specs=[pl.BlockSpec((None, tr, cols), lambda i, r, me_ref: (me_ref[0] ^ (2 * i), r, 0)), slot],
            out_specs=slot),
        compiler_params=_cparams(("parallel", "parallel")), name=name)(me.reshape(1), blocks, stage)


def adamw_sum(parts, w, m, v, name):
    layers, rows, cols = w.shape
    n_parts = parts.shape[1]
    if rows % 16 == 0:
        tr, tc = _row_tile(rows, cols, n_parts, parts.dtype.itemsize), cols
    else:
        tr, tc = rows, _pick(cols, (256, 128))

    def body(p_ref, w_ref, m_ref, v_ref, g_ref, d_ref, nm_ref, nv_ref):
        g = p_ref[0].astype(F32)
        for i in range(1, n_parts):
            g = g + p_ref[i].astype(F32)
        g_ref[...] = g
        d_ref[...], nm_ref[...], nv_ref[...] = _adamw_math(w_ref[...], g, m_ref[...], v_ref[...])

    spec = pl.BlockSpec((None, tr, tc), lambda l, i, j: (l, i, j))
    return pl.pallas_call(
        body, grid=(layers, rows // tr, cols // tc),
        in_specs=[pl.BlockSpec((None, n_parts, tr, tc), lambda l, i, j: (l, 0, i, j)), spec, spec, spec],
        out_specs=[spec] * 4, out_shape=[jax.ShapeDtypeStruct((layers, rows, cols), F32)] * 4,
        compiler_params=_cparams(("parallel", "parallel", "parallel")), name=name)(parts, w, m, v)


def adamw_sum_rows_major(parts, w, m, v, name):
    rows, layers, cols = w.shape
    n_parts = parts.shape[1]
    sub = cols // 128
    tr = _pick(rows, (141, 128, 47, 8))
    tiles = lambda t: t.reshape(t.shape[:-1] + (sub, 128))

    def body(p_ref, w_ref, m_ref, v_ref, g_ref, d_ref, nm_ref, nv_ref):
        g = p_ref[0].astype(F32)
        for i in range(1, n_parts):
            g = g + p_ref[i].astype(F32)
        g_ref[...] = g
        d_ref[...], nm_ref[...], nv_ref[...] = _adamw_math(w_ref[...], g, m_ref[...], v_ref[...])

    spec = pl.BlockSpec((tr, None, sub, 128), lambda l, i: (i, l, 0, 0))
    outs = pl.pallas_call(
        body, grid=(layers, rows // tr),
        in_specs=[pl.BlockSpec((None, n_parts, tr, sub, 128), lambda l, i: (l, 0, i, 0, 0)), spec, spec, spec],
        out_specs=[spec] * 4, out_shape=[jax.ShapeDtypeStruct((rows, layers, sub, 128), F32)] * 4,
        compiler_params=_cparams(("parallel", "parallel")), name=name)(tiles(parts), tiles(w), tiles(m), tiles(v))
    return [t.reshape(rows, layers, cols) for t in outs]


def _me():
    return lax.axis_index("x"), lax.axis_index("y"), lax.axis_index("c")


N_PEERS = N_DEV - 1


def all_gather(shards, name):
    n = len(shards)
    any_spec = pl.BlockSpec(memory_space=pl.ANY)

    def body(*refs):
        x_refs, out_refs = refs[:n], refs[n:2 * n]
        send_sems, recv_sems, local_sems = refs[2 * n:]
        x, y, c = _me()
        me, sibling = (x, y, c), (x, y, 1 - c)
        chips = [(1 - x, y), (x, 1 - y), (1 - x, 1 - y)]

        def copy(t, k, block, to, from_input=False):
            slot = out_refs[t].at[4 * block[0] + 2 * block[1] + block[2]]
            return pltpu.make_async_remote_copy(
                src_ref=x_refs[t] if from_input else slot, dst_ref=slot, send_sem=send_sems.at[N_PEERS * t + k],
                recv_sem=recv_sems.at[N_PEERS * t + k], device_id=to, device_id_type=pl.DeviceIdType.MESH)

        mine = [pltpu.make_async_copy(x_refs[t], out_refs[t].at[4 * x + 2 * y + c], local_sems.at[t]) for t in range(n)]
        started = []
        for t in range(n):
            mine[t].start()
            started.append(copy(t, 0, me, sibling, from_input=True))
            started += [copy(t, 1 + j, me, (*chip, c), from_input=True) for j, chip in enumerate(chips)]
        for cp in started:
            cp.start()
        for j, chip in enumerate(chips):
            for t in range(n):
                copy(t, 1 + j, (*chip, c), me).wait_recv()
                fwd = copy(t, 4 + j, (*chip, c), sibling)
                fwd.start()
                started.append(fwd)
        for t in range(n):
            copy(t, 0, sibling, me).wait_recv()
            for j, chip in enumerate(chips):
                copy(t, 4 + j, (*chip, 1 - c), me).wait_recv()
        for cp in started:
            cp.wait_send()
        for cp in mine:
            cp.wait()

    return pl.pallas_call(
        body, out_shape=[jax.ShapeDtypeStruct((N_DEV,) + s.shape, s.dtype) for s in shards],
        in_specs=[any_spec] * n, out_specs=[any_spec] * n,
        scratch_shapes=[pltpu.SemaphoreType.DMA((N_PEERS * n,)), pltpu.SemaphoreType.DMA((N_PEERS * n,)),
                        pltpu.SemaphoreType.DMA((n,))],
        name=name)(*shards)


SIBLING = 1
OTHER_CHIPS = (2, 4, 6)
SAME_CORE = (0,) + OTHER_CHIPS


class Exchange:
    def __init__(self, inputs, out_shapes, aliases, copies, local=()):
        self.inputs, self.out_shapes, self.aliases = list(inputs), list(out_shapes), aliases
        self._copies, self._local = list(copies), list(local)
        self.scratch = [pltpu.SemaphoreType.DMA((len(self._copies),)), pltpu.SemaphoreType.DMA((len(self._copies),)),
                        pltpu.SemaphoreType.DMA((max(len(self._local), 1),))]

    def _build(self, ins, outs, sems):
        send_sems, recv_sems, local_sems = sems
        x, y, c = _me()
        me = 4 * x + 2 * y + c
        local = [functools.partial(pltpu.make_async_copy, src(ins, outs, me), dst(outs, me), local_sems.at[i])
                 for i, (src, dst) in enumerate(self._local)]
        sends, recvs = [], []
        for i, (mask, src, dst) in enumerate(self._copies):
            px, py, pc = x ^ ((mask >> 2) & 1), y ^ ((mask >> 1) & 1), c ^ (mask & 1)
            pair = dict(send_sem=send_sems.at[i], recv_sem=recv_sems.at[i], device_id_type=pl.DeviceIdType.MESH)
            sends.append(functools.partial(
                pltpu.make_async_remote_copy, src_ref=src(ins, outs, me), dst_ref=dst(outs, me), device_id=(px, py, pc), **pair))
            recvs.append(functools.partial(
                pltpu.make_async_remote_copy, src_ref=src(ins, outs, me), dst_ref=dst(outs, me ^ mask), device_id=(x, y, c), **pair))
        return local, sends, recvs

    def start(self, ins, outs, sems):
        local, sends, _ = self._build(ins, outs, sems)
        for make in local + sends:
            make().start()

    def drain(self, ins, outs, sems):
        local, sends, recvs = self._build(ins, outs, sems)
        for make in recvs:
            make().wait_recv()
        for make in sends:
            make().wait_send()
        for make in local:
            make().wait()


def _bind(fn, *args):
    return functools.partial(fn, *args)


def join_exchanges(a, b):
    if a is None or b is None:
        return a or b
    na_in, na_out = len(a.inputs), len(a.out_shapes)

    def src_a(fn):
        return lambda ins, outs, me: fn(ins[:na_in], outs[:na_out], me)

    def dst_a(fn):
        return lambda outs, who: fn(outs[:na_out], who)

    def src_b(fn):
        return lambda ins, outs, me: fn(ins[na_in:], outs[na_out:], me)

    def dst_b(fn):
        return lambda outs, who: fn(outs[na_out:], who)

    copies = [(m, src_a(s), dst_a(d)) for m, s, d in a._copies] + [(m, src_b(s), dst_b(d)) for m, s, d in b._copies]
    local = [(src_a(s), dst_a(d)) for s, d in a._local] + [(src_b(s), dst_b(d)) for s, d in b._local]
    aliases = dict(a.aliases)
    aliases.update({na_in + i: na_out + o for i, o in b.aliases.items()})
    return Exchange(a.inputs + b.inputs, a.out_shapes + b.out_shapes, aliases, copies, local)


def gather_over_ici(shards):
    copies = [(mask, _bind(lambda t, ins, outs, me: ins[t], t), _bind(lambda t, outs, sender: outs[t].at[sender], t))
              for t in range(len(shards)) for mask in OTHER_CHIPS]
    local = [(_bind(lambda t, ins, outs, me: ins[t], t), _bind(lambda t, outs, me: outs[t].at[me], t))
             for t in range(len(shards))]
    return Exchange(shards, [jax.ShapeDtypeStruct((N_DEV,) + s.shape, s.dtype) for s in shards], {}, copies, local)


def gather_over_d2d(gathered):
    copies = [(SIBLING, _bind(lambda t, m, ins, outs, me: outs[t].at[me ^ m], t, m),
               _bind(lambda t, m, outs, sender: outs[t].at[sender ^ m], t, m))
              for t in range(len(gathered)) for m in SAME_CORE]
    return Exchange(gathered, [jax.ShapeDtypeStruct(g.shape, g.dtype) for g in gathered],
                    {t: t for t in range(len(gathered))}, copies)


def scatter_over_d2d(blocks):
    copies = [(SIBLING, _bind(lambda t, m, ins, outs, me: ins[t].at[me ^ SIBLING ^ m], t, m),
               _bind(lambda t, i, outs, sender: outs[t].at[i], t, i))
              for t in range(len(blocks)) for i, m in enumerate(SAME_CORE)]
    return Exchange(blocks, [jax.ShapeDtypeStruct((len(SAME_CORE),) + b.shape[1:], b.dtype) for b in blocks], {}, copies)


def scatter_over_ici(pair_sums, bufs, layer):
    n = len(pair_sums)
    copies = [(m, _bind(lambda t, i, ins, outs, me: ins[t].at[i], t, i),
               _bind(lambda t, i, outs, sender: outs[t].at[layer, i], t, i))
              for t in range(n) for i, m in enumerate(SAME_CORE) if m]
    local = [(_bind(lambda t, ins, outs, me: ins[t].at[0], t), _bind(lambda t, outs, me: outs[t].at[layer, 0], t))
             for t in range(n)]
    return Exchange(list(pair_sums) + list(bufs), [jax.ShapeDtypeStruct(b.shape, b.dtype) for b in bufs],
                    {n + t: t for t in range(n)}, copies, local)


def run_exchange(ex, name):
    any_spec = pl.BlockSpec(memory_space=pl.ANY)
    n_in, n_out = len(ex.inputs), len(ex.out_shapes)

    def body(*refs):
        ins, outs, sems = refs[:n_in], refs[n_in:n_in + n_out], refs[n_in + n_out:]
        ex.start(ins, outs, sems)
        ex.drain(ins, outs, sems)

    return pl.pallas_call(
        body, out_shape=ex.out_shapes, in_specs=[any_spec] * n_in, out_specs=[any_spec] * n_out,
        input_output_aliases=ex.aliases, scratch_shapes=ex.scratch, name=name)(*ex.inputs)


MATRICES = ("w_in", "w_attn_out", "w_conv_out", "pool_w", "w_o", "w_ffn_in", "w_ffn_out")
TRANSPOSED = ("w_in", "w_ffn_in")
EVERY = tuple(range(len(MATRICES)))
IN_PROJ_PART, ATTN_PART, MIX_PART = (0,), (1, 2, 3, 4, 5), (6,)
LATE = (0,)
EARLY = EVERY[1:]
EARLY_FIRST, EARLY_SECOND = (4, 6), (1, 2, 3, 5)
SHARD_INFO = {
    "w_in": ((DEPTH, D_IN // N_DEV, D_MODEL), 1),
    "w_attn_out": ((DEPTH, D_ATTN, D_MODEL // N_DEV), 2),
    "w_conv_out": ((DEPTH, D_CONV, D_MODEL // N_DEV), 2),
    "pool_w": ((DEPTH, 4, 64, 256 // N_DEV), 3),
    "w_o": ((DEPTH, D_MODEL // N_DEV, D_MODEL), 1),
    "w_ffn_in": ((DEPTH, 2 * D_FF // N_DEV, D_MODEL), 1),
    "w_ffn_out": ((DEPTH, D_FF // N_DEV, D_MODEL), 1),
}


def _handled(name, t):
    return jnp.transpose(t, (0, 2, 1)) if name in TRANSPOSED else t
VECTORS = ("norm_mix_g", "forget_b", "q_norm_g", "k_norm_g", "pool_scale", "norm_ffn_g")
VECTOR_SHAPES = {"norm_mix_g": (DEPTH, D_MODEL), "forget_b": (DEPTH, HEADS), "q_norm_g": (DEPTH, HEAD_DIM),
                 "k_norm_g": (DEPTH, HEAD_DIM), "pool_scale": (DEPTH, D_MODEL), "norm_ffn_g": (DEPTH, D_MODEL)}
CONV_W_FULL = (DEPTH, 3, D_CONV)


def _size(shape):
    n = 1
    for v in shape:
        n *= v
    return n


def _pack(arrays, rows, cols):
    flat = jnp.concatenate([a.reshape(-1) for a in arrays])
    return jnp.pad(flat, (0, rows * cols - flat.shape[0])).reshape(rows, cols)


def _unpack(packed, shapes):
    flat, out, off = packed.reshape(-1), [], 0
    for shp in shapes:
        out.append(flat[off:off + _size(shp)].reshape(shp))
        off += _size(shp)
    return out


def _join_shards(stacked, axis):
    moved = jnp.moveaxis(stacked, 0, axis)
    shp = list(moved.shape)
    shp[axis:axis + 2] = [shp[axis] * shp[axis + 1]]
    return moved.reshape(shp)


def _cut_shards(full, axis):
    shp = list(full.shape)
    shp[axis:axis + 1] = [N_DEV, shp[axis] // N_DEV]
    return jnp.moveaxis(full.reshape(shp), axis, 0)


N_MOVED = 1544
SHARD_ROWS = D_IN // N_DEV


def _regroup_w_in(shards):
    wt = shards.reshape(D_IN, shards.shape[2])
    pad = jnp.zeros((N_FULL - D_IN, wt.shape[1]), wt.dtype)
    return jnp.concatenate([wt[N_MOVED:], wt[:N_MOVED], pad], axis=0)


def _ungroup_w_in(wpt):
    def kernel_rows(a, b):
        if b <= N_MOVED:
            return [wpt[a + D_IN - N_MOVED:b + D_IN - N_MOVED]]
        if a >= N_MOVED:
            return [wpt[a - N_MOVED:b - N_MOVED]]
        return kernel_rows(a, N_MOVED) + kernel_rows(N_MOVED, b)

    return jnp.stack([jnp.concatenate(kernel_rows(s * SHARD_ROWS, (s + 1) * SHARD_ROWS), axis=0) for s in range(N_DEV)])


def _pool_block_diag(w):
    out = jnp.zeros((D_POOL, D_MODEL), w.dtype)
    for g in range(4):
        out = lax.dynamic_update_slice(out, w[g], (g * 64, g * 256))
    return out


def _pool_from_block_diag(wbd):
    return jnp.stack([wbd[g * 64:(g + 1) * 64, g * 256:(g + 1) * 256] for g in range(4)])


def _layer_weights(mats, vec, conv_w, l):
    wp = _pool_block_diag(mats["pool_w"])
    row = lambda v: v.reshape(1, -1)
    fb = jnp.zeros((1, 128), F32).at[0, :HEADS].set(vec["forget_b"][l])
    cw = jnp.zeros((8, D_CONV), F32).at[:3].set(conv_w[l])
    twice = lambda v: jnp.tile(v.reshape(1, -1), (1, 2))
    return dict(
        wt_in=_regroup_w_in(mats["w_in"]), wt_ffn_in=mats["w_ffn_in"], w_ffn_out=mats["w_ffn_out"],
        wa=mats["w_attn_out"], wc=mats["w_conv_out"], wp=wp, wo=mats["w_o"],
        g_mix=row(vec["norm_mix_g"][l]), g_ffn=row(vec["norm_ffn_g"][l]), gq2=twice(vec["q_norm_g"][l]),
        gk2=twice(vec["k_norm_g"][l]), scale=row(vec["pool_scale"][l]), fb=fb, cw=cw)


def _layer_fwd(x, w, l, comm):
    (proj, h), half_a = norm_matmul(x, w["g_mix"], w["wt_in"], N_MAIN, f"in_proj_{l}", comm.gather_ici(l + 1, IN_PROJ_PART))
    qa, ka, va, vt, z = attn_prep(proj, h, w["wt_in"], w["fb"], w["gq2"], w["gk2"], f"attn_prep_{l}")
    (oa, lse), half_b = attn_forward(qa, ka, vt, f"attn_fwd_{l}", comm.gather_ici(l + 1, ATTN_PART))
    x1, half_c = mix_fwd(proj, oa, x, w["wa"], w["wc"], w["wp"], w["scale"], w["cw"], w["wo"], f"mix_fwd_{l}",
                         comm.gather_ici(l + 1, MIX_PART))
    half = list(half_a) + list(half_b) + list(half_c)
    (gu, h2), gathered = norm_matmul(x1, w["g_ffn"], w["wt_ffn_in"], 2 * D_FF, f"ffn_in_{l}", comm.gather_d2d(l + 1, half))
    x2 = swiglu_matmul(gu, w["w_ffn_out"], x1, f"ffn_out_{l}")
    saved = dict(x=x, proj=proj, h=h, z=z, qa=qa, ka=ka, va=va, oa=oa, lse=lse, x1=x1, gu=gu, h2=h2)
    return x2, saved, gathered


def _layer_bwd(dx2, sv, w, l, comm):
    g = {}
    (dgu, act), stage = swiglu_bwd(dx2, sv["gu"], w["w_ffn_out"], f"ffn_out_bwd_{l}", comm.scatter_d2d(l + 1))
    sums = comm.pair_sums(l + 1, stage)
    g["w_ffn_out"] = tn_matmul(act, dx2, f"dw_ffn_out_{l}")
    g["w_ffn_in"] = tn_matmul(dgu, sv["h2"], f"dw_ffn_in_{l}")
    (dx1, dg), _ = matmul_normbwd(dgu, w["wt_ffn_in"], sv["x1"], w["g_ffn"], dx2, f"ffn_in_bwd_{l}")
    g["norm_ffn_g"] = dg[0]

    (dproj, doa, a_tok, merged, dya, dyc, dyp, uc, dd, dscale, dcw) = mix_bwd(
        sv["proj"], sv["oa"], dx1, w["wa"], w["wc"], w["wp"], w["scale"], w["cw"], w["wo"], f"mix_bwd_{l}")
    g["w_o"] = tn_matmul(merged, dx1, f"dw_o_{l}")
    g["w_attn_out"], g["w_conv_out"], dwp = tn_matmuls([(a_tok, dya), (uc, dyc), (dd, dyp)], f"dw_branches_{l}")
    g["pool_w"] = _pool_from_block_diag(dwp)
    g["pool_scale"] = dscale[0]
    g["conv_w"] = dcw[:3]

    early = comm.early(l)
    comm.grads(l, g)
    above = comm.scatter_ici(l + 1, sums)
    (dqa, dka, dva), got = attn_backward(sv["qa"], sv["ka"], sv["va"], sv["oa"], doa, sv["lse"], f"attn_bwd_{l}",
                                         join_exchanges(above, comm.scatter_d2d(l, early) if early else None))
    n_above = len(above.out_shapes) if above else 0
    comm.scattered(got[:n_above])
    early_sums = dict(zip(early, comm.pair_sums(l, got[n_above:], early))) if early else {}
    early_ici = lambda which: comm.scatter_ici(l, [early_sums[t] for t in which], which) if early else None
    dproj, dgq, dgk, db = attn_post(dqa, dka, dva, sv["proj"], sv["z"], w["gq2"], w["gk2"], dproj, f"attn_post_{l}")
    g["q_norm_g"] = dgq[0, :HEAD_DIM] + dgq[0, HEAD_DIM:]
    g["k_norm_g"] = dgk[0, :HEAD_DIM] + dgk[0, HEAD_DIM:]
    g["forget_b"] = db[0, :HEADS]

    dw_in = tn_matmul(dproj, sv["h"], f"dw_in_{l}", m_cols=N_FULL, ex=early_ici(EARLY_FIRST))
    if early:
        dw_in, got = dw_in
        comm.scattered(got, EARLY_FIRST)
    g["w_in"] = _ungroup_w_in(dw_in)
    (dx, dg), got = matmul_normbwd(dproj, w["wt_in"], sv["x"], w["g_mix"], dx1, f"in_proj_bwd_{l}", k=N_FULL,
                                   ex=early_ici(EARLY_SECOND))
    comm.scattered(got, EARLY_SECOND if early else None)
    g["norm_mix_g"] = dg[0]
    comm.grads(l, g)
    return dx


def _local_step(x, tgt, comm):
    ws, saved = [], []
    w = comm.weights(0, None)
    for l in range(DEPTH):
        ws.append(w)
        x, sv, gathered = _layer_fwd(x, w, l, comm)
        saved.append(sv)
        if l + 1 < DEPTH:
            w = comm.weights(l + 1, gathered)
    sq, dx = loss_kernel(x, tgt, "loss")
    for l in reversed(range(DEPTH)):
        dx = _layer_bwd(dx, saved[l], ws[l], l, comm)
    comm.finish()
    return sq[0, 0], dx


def kernel(x, norm_mix_g, w_in, forget_b, q_norm_g, k_norm_g, w_attn_out, conv_w, w_conv_out, pool_w, pool_scale, w_o, norm_ffn_g, w_ffn_in, w_ffn_out, loss_target, m_norm_mix_g, m_w_in, m_forget_b, m_q_norm_g, m_k_norm_g, m_w_attn_out, m_conv_w, m_w_conv_out, m_pool_w, m_pool_scale, m_w_o, m_norm_ffn_g, m_w_ffn_in, m_w_ffn_out, v_norm_mix_g, v_w_in, v_forget_b, v_q_norm_g, v_k_norm_g, v_w_attn_out, v_conv_w, v_w_conv_out, v_pool_w, v_pool_scale, v_w_o, v_norm_ffn_g, v_w_ffn_in, v_w_ffn_out):
    w = dict(norm_mix_g=norm_mix_g, w_in=w_in, forget_b=forget_b, q_norm_g=q_norm_g, k_norm_g=k_norm_g,
             w_attn_out=w_attn_out, conv_w=conv_w, w_conv_out=w_conv_out, pool_w=pool_w, pool_scale=pool_scale,
             w_o=w_o, norm_ffn_g=norm_ffn_g, w_ffn_in=w_ffn_in, w_ffn_out=w_ffn_out)
    m = dict(norm_mix_g=m_norm_mix_g, w_in=m_w_in, forget_b=m_forget_b, q_norm_g=m_q_norm_g, k_norm_g=m_k_norm_g,
             w_attn_out=m_w_attn_out, conv_w=m_conv_w, w_conv_out=m_w_conv_out, pool_w=m_pool_w,
             pool_scale=m_pool_scale, w_o=m_w_o, norm_ffn_g=m_norm_ffn_g, w_ffn_in=m_w_ffn_in, w_ffn_out=m_w_ffn_out)
    v = dict(norm_mix_g=v_norm_mix_g, w_in=v_w_in, forget_b=v_forget_b, q_norm_g=v_q_norm_g, k_norm_g=v_k_norm_g,
             w_attn_out=v_w_attn_out, conv_w=v_conv_w, w_conv_out=v_w_conv_out, pool_w=v_pool_w,
             pool_scale=v_pool_scale, w_o=v_w_o, norm_ffn_g=v_norm_ffn_g, w_ffn_in=v_w_ffn_in, w_ffn_out=v_w_ffn_out)
    me = 4 * lax.axis_index("x") + 2 * lax.axis_index("y") + lax.axis_index("c")
    layer_shard = {n: SHARD_INFO[n][0][1:] for n in MATRICES}
    cut_axis = {n: SHARD_INFO[n][1] - 1 for n in MATRICES}

    vec = {n: w[n] for n in VECTORS}
    rc = {n: (_size(layer_shard[n][:-1]), layer_shard[n][-1]) for n in MATRICES}

    class Comm:
        bufs = [lax.empty((DEPTH, len(SAME_CORE)) + layer_shard[n], BF16) for n in MATRICES]
        blocks = [None] * DEPTH
        small_g = [None] * DEPTH
        conv_full = None

        @staticmethod
        def shards(l):
            return [_handled(n, w[n])[l].astype(BF16) for n in MATRICES]

        @staticmethod
        def gather_ici(l, part):
            return gather_over_ici([Comm.shards(l)[t] for t in part]) if l < DEPTH else None

        @staticmethod
        def gather_d2d(l, half):
            return gather_over_d2d(half) if l < DEPTH else None

        @staticmethod
        def weights(l, gathered):
            if l == 0:
                *gathered, conv_g = all_gather(Comm.shards(0) + [_pack([conv_w], 8, 128)], "gather_0")
                Comm.conv_full = _join_shards(jnp.stack([_unpack(conv_g[i], [conv_w.shape])[0] for i in range(N_DEV)]), 2)
            mats = {n: t if n == "w_in" else _join_shards(t, cut_axis[n]) for n, t in zip(MATRICES, gathered)}
            return _layer_weights(mats, vec, Comm.conv_full, l)

        @staticmethod
        def grads(l, g):
            Comm.small_g[l] = g
            Comm.blocks[l] = [None if n not in g else g[n] if n == "w_in" else _cut_shards(g[n], cut_axis[n])
                              for n in MATRICES]

        @staticmethod
        def early(l):
            return EARLY if l == 0 else None

        @staticmethod
        def scatter_d2d(l, which=EVERY):
            return scatter_over_d2d([Comm.blocks[l][t] for t in which]) if l < DEPTH else None

        @staticmethod
        def pair_sums(l, stage, which=EVERY):
            if l >= DEPTH:
                return None
            return [pair_sum(Comm.blocks[l][t].reshape((N_DEV,) + rc[MATRICES[t]]),
                             s.reshape((len(SAME_CORE),) + rc[MATRICES[t]]), me,
                             f"pair_sum_{MATRICES[t]}_{l}").reshape(s.shape) for t, s in zip(which, stage)]

        @staticmethod
        def scatter_ici(l, sums, which=EVERY):
            return scatter_over_ici(sums, [Comm.bufs[t] for t in which], l) if l < DEPTH else None

        @staticmethod
        def scattered(results, which=EVERY):
            for t, r in zip(which or (), results):
                Comm.bufs[t] = r

        @staticmethod
        def finish():
            stage = run_exchange(Comm.scatter_d2d(0, LATE), "scatter_d2d_0")
            Comm.scattered(run_exchange(Comm.scatter_ici(0, Comm.pair_sums(0, stage, LATE), LATE), "scatter_ici_0"), LATE)

    small_g, received = Comm.small_g, Comm
    sq, dx = _local_step(x[0], loss_target[0], Comm)

    big = {}
    for n, parts in zip(MATRICES, received.bufs):
        if n == "w_in":
            outs = adamw_sum_rows_major(parts, *[jnp.transpose(d[n], (2, 0, 1)) for d in (w, m, v)], f"adamw_{n}")
            big[n] = [jnp.transpose(t, (1, 2, 0)) for t in outs]
            continue
        outs = adamw_sum(parts.reshape((DEPTH, len(SAME_CORE)) + rc[n]),
                         *[_handled(n, d[n]).reshape((DEPTH,) + rc[n]) for d in (w, m, v)], f"adamw_{n}")
        big[n] = [_handled(n, t.reshape((DEPTH,) + layer_shard[n])) for t in outs]

    small_shapes = [VECTOR_SHAPES[n] for n in VECTORS] + [CONV_W_FULL, (1,)]
    stacked = [jnp.stack([small_g[l][n] for l in range(DEPTH)]) for n in VECTORS + ("conv_w",)] + [sq.reshape(1)]
    sparts = all_gather([_pack(stacked, SMALL_ROWS, 128)], "gather_vector_grads")[0]
    col0 = me * (D_CONV // N_DEV)
    place = lambda t: lax.dynamic_update_slice(jnp.zeros(CONV_W_FULL, F32), t, (0, 0, col0))
    spacked = [_pack([d[n] for n in VECTORS] + [place(d["conv_w"]), jnp.zeros((1,), F32)], SMALL_ROWS, 128)[None]
               for d in (w, m, v)]
    small = [_unpack(t[0], small_shapes) for t in adamw_sum(sparts[None], *spacked, "adamw_vectors")]
    loss = (0.5 / D_MODEL) * small[0][-1][0]

    def result(kind):
        out = {n: big[n][kind] for n in MATRICES}
        out.update({n: small[kind][j] for j, n in enumerate(VECTORS)})
        out["conv_w"] = lax.dynamic_slice(small[kind][len(VECTORS)], (0, 0, col0), conv_w.shape)
        return [out[n] for n in w]

    return (loss, dx[None], *result(0), *result(1), *result(2), *result(3))
```

```python
import functools

import jax
import jax.numpy as jnp
from jax import lax
from jax.experimental import pallas as pl
from jax.experimental.pallas import tpu as pltpu

F32 = jnp.float32
BF16 = jnp.bfloat16

N_DEV = 8
DEPTH = 4
D_MODEL = 1024
HEAD_DIM = 64
HEADS = 8
D_ATTN = 512
D_CONV = 256
D_POOL = 256
D_FF = 2816
D_IN = 5640
EPS = 1e-6
ATTN_SCALE = HEAD_DIM ** -0.5

N_REST = 4096
N_MAIN = 5632
N_FULL = 5760
DPROJ_TAIL = 2048
DPROJ_COLS = N_REST + DPROJ_TAIL
FF_BLK = 256
N_FF_BLKS = D_FF // FF_BLK
HALO = 16

ADAM_LR = 0.001
ADAM_B1 = 0.9
ADAM_B2 = 0.999
ADAM_EPS = 1e-08
ADAM_WD = 0.01
ADAM_STEP = 10

SMALL_ROWS = 128

VMEM_LIMIT = 48 * 2 ** 20


def _cparams(sem, vmem=None):
    return pltpu.CompilerParams(dimension_semantics=sem, vmem_limit_bytes=vmem or VMEM_LIMIT)


def _pick(n, cands):
    for c in cands:
        if n % c == 0:
            return c
    raise ValueError(f"no tile for {n}")


def _tile(n, cap):
    t = min(cap, n)
    assert n % t == 0, (n, cap)
    return t


def _sigmoid(v):
    return 1.0 / (1.0 + jnp.exp(-v))


def _rstd(v):
    return lax.rsqrt(jnp.mean(v * v, axis=-1, keepdims=True) + EPS)


def _dot(a, b):
    return jnp.dot(a, b, preferred_element_type=F32)


def _dot_tn(a, b):
    return lax.dot_general(a, b, (((0,), (0,)), ((), ())), preferred_element_type=F32)


def _dot_nt(a, b):
    return lax.dot_general(a, b, (((1,), (1,)), ((), ())), preferred_element_type=F32)


def norm_matmul(x, g, wt, n_cols, name, ex=None):
    s, d = x.shape
    tm, tn = _tile(s, 1024), _pick(n_cols, (2816, 1408, 512))

    def body(x_ref, g_ref, w_ref, o_ref, h_ref):
        @pl.when(pl.program_id(1) == 0)
        def _():
            xv = x_ref[...]
            h_ref[...] = (xv * _rstd(xv) * g_ref[...]).astype(BF16)

        o_ref[...] = _dot_nt(h_ref[...], w_ref[...]).astype(BF16)

    return _carried_call(
        body, ex, (s // tm, n_cols // tn),
        [pl.BlockSpec((tm, d), lambda i, j: (i, 0)), pl.BlockSpec((1, d), lambda i, j: (0, 0)),
         pl.BlockSpec((tn, d), lambda i, j: (j, 0))],
        [pl.BlockSpec((tm, tn), lambda i, j: (i, j)), pl.BlockSpec((tm, d), lambda i, j: (i, 0))],
        [jax.ShapeDtypeStruct((s, n_cols), BF16), jax.ShapeDtypeStruct((s, d), BF16)], [],
        ("arbitrary", "arbitrary"), name, (x, g, wt))


def tn_matmul(a, b, name, m_cols=None, ex=None):
    t = a.shape[0]
    m = m_cols or a.shape[1]
    n = b.shape[1]
    tk = _tile(t, 1024)
    tmm = _pick(m, (1408, 1152, 1024, 512, 256))
    tn = _pick(n, (1408, 1152, 1024, 512, 128))
    nk = t // tk

    def body(a_ref, b_ref, o_ref, acc_ref):
        @pl.when(pl.program_id(2) == 0)
        def _():
            acc_ref[...] = jnp.zeros_like(acc_ref)

        acc_ref[...] += _dot_tn(a_ref[...].astype(BF16), b_ref[...].astype(BF16))

        @pl.when(pl.program_id(2) == nk - 1)
        def _():
            o_ref[...] = acc_ref[...].astype(BF16)

    if ex is None:
        return pl.pallas_call(
            body, grid=(m // tmm, n // tn, nk),
            in_specs=[pl.BlockSpec((tk, tmm), lambda i, j, k: (k, i)), pl.BlockSpec((tk, tn), lambda i, j, k: (k, j))],
            out_specs=pl.BlockSpec((tmm, tn), lambda i, j, k: (i, j)),
            out_shape=jax.ShapeDtypeStruct((m, n), BF16), scratch_shapes=[pltpu.VMEM((tmm, tn), F32)],
            compiler_params=_cparams(("parallel", "parallel", "arbitrary")), name=name)(a, b)
    (out,), carried = _carried_call(
        body, ex, (m // tmm, n // tn, nk),
        [pl.BlockSpec((tk, tmm), lambda i, j, k: (k, i)), pl.BlockSpec((tk, tn), lambda i, j, k: (k, j))],
        [pl.BlockSpec((tmm, tn), lambda i, j, k: (i, j))], [jax.ShapeDtypeStruct((m, n), BF16)],
        [pltpu.VMEM((tmm, tn), F32)], ("arbitrary", "arbitrary", "arbitrary"), name, (a, b))
    return out, carried


def tn_matmuls(pairs, name):
    t = pairs[0][0].shape[0]
    tk = _tile(t, 1024)
    nk = t // tk
    n = len(pairs)

    def body(*refs):
        ins, outs, accs = refs[:2 * n], refs[2 * n:3 * n], refs[3 * n:]

        @pl.when(pl.program_id(0) == 0)
        def _():
            for acc in accs:
                acc[...] = jnp.zeros_like(acc)

        for i in range(n):
            accs[i][...] += _dot_tn(ins[2 * i][...], ins[2 * i + 1][...])

        @pl.when(pl.program_id(0) == nk - 1)
        def _():
            for out, acc in zip(outs, accs):
                out[...] = acc[...].astype(BF16)

    shapes = [(a.shape[1], b.shape[1]) for a, b in pairs]
    return pl.pallas_call(
        body, grid=(nk,),
        in_specs=[pl.BlockSpec((tk, t_.shape[1]), lambda k: (k, 0)) for pair in pairs for t_ in pair],
        out_specs=[pl.BlockSpec(shp, lambda k: (0, 0)) for shp in shapes],
        out_shape=[jax.ShapeDtypeStruct(shp, BF16) for shp in shapes],
        scratch_shapes=[pltpu.VMEM(shp, F32) for shp in shapes],
        compiler_params=_cparams(("arbitrary",)), name=name)(*[t_ for pair in pairs for t_ in pair])


def matmul_normbwd(a, wt, x, g, dres, name, k=None, ex=None):
    s = a.shape[0]
    k = k or a.shape[1]
    d = wt.shape[1]
    tm = _tile(s, 1024)
    tk = _pick(k, (1408, 1152, 512))
    nk = k // tk

    def body(a_ref, w_ref, x_ref, g_ref, r_ref, dx_ref, dg_ref, acc_ref):
        i, kk = pl.program_id(0), pl.program_id(1)

        @pl.when(kk == 0)
        def _():
            acc_ref[...] = jnp.zeros_like(acc_ref)

        @pl.when((i == 0) & (kk == 0))
        def _():
            dg_ref[...] = jnp.zeros_like(dg_ref)

        acc_ref[...] += _dot(a_ref[...], w_ref[...])

        @pl.when(kk == nk - 1)
        def _():
            xv = x_ref[...]
            r = _rstd(xv)
            y = xv * r
            dh = acc_ref[...]
            dy = dh * g_ref[...]
            dx_ref[...] = r_ref[...] + r * (dy - y * jnp.mean(dy * y, axis=-1, keepdims=True))
            dg_ref[...] += jnp.sum(dh * y, axis=0, keepdims=True)

    return _carried_call(
        body, ex, (s // tm, nk),
        [pl.BlockSpec((tm, tk), lambda i, kk: (i, kk)), pl.BlockSpec((tk, d), lambda i, kk: (kk, 0)),
         pl.BlockSpec((tm, d), lambda i, kk: (i, 0)), pl.BlockSpec((1, d), lambda i, kk: (0, 0)),
         pl.BlockSpec((tm, d), lambda i, kk: (i, 0))],
        [pl.BlockSpec((tm, d), lambda i, kk: (i, 0)), pl.BlockSpec((1, d), lambda i, kk: (0, 0))],
        [jax.ShapeDtypeStruct((s, d), F32), jax.ShapeDtypeStruct((1, d), F32)],
        [pltpu.VMEM((tm, d), F32)], ("arbitrary", "arbitrary"), name, (a, wt, x, g, dres), vmem=56 * 2 ** 20)


def swiglu_matmul(gu, w, x1, name):
    s = gu.shape[0]
    d = w.shape[1]
    tm = _tile(s, 512)

    def body(gu_ref, w_ref, x_ref, o_ref):
        acc = x_ref[...]
        for j in range(N_FF_BLKS):
            gt = gu_ref[:, j * FF_BLK:(j + 1) * FF_BLK].astype(F32)
            up = gu_ref[:, D_FF + j * FF_BLK:D_FF + (j + 1) * FF_BLK].astype(F32)
            act = (gt * _sigmoid(gt) * up).astype(BF16)
            acc += _dot(act, w_ref[j * FF_BLK:(j + 1) * FF_BLK, :])
        o_ref[...] = acc

    return pl.pallas_call(
        body, grid=(s // tm,),
        in_specs=[pl.BlockSpec((tm, 2 * D_FF), lambda i: (i, 0)), pl.BlockSpec((D_FF, d), lambda i: (0, 0)),
                  pl.BlockSpec((tm, d), lambda i: (i, 0))],
        out_specs=pl.BlockSpec((tm, d), lambda i: (i, 0)),
        out_shape=jax.ShapeDtypeStruct((s, d), F32),
        compiler_params=_cparams(("parallel",)), name=name)(gu, w, x1)


def swiglu_bwd(dx2, gu, w, name, ex=None):
    s, d = dx2.shape
    tm = _tile(s, 512)

    def body(dx_ref, gu_ref, w_ref, dgu_ref, act_ref):
        dx = dx_ref[...].astype(BF16)
        for j in range(N_FF_BLKS):
            g_cols = slice(j * FF_BLK, (j + 1) * FF_BLK)
            u_cols = slice(D_FF + j * FF_BLK, D_FF + (j + 1) * FF_BLK)
            dact = _dot_nt(dx, w_ref[j * FF_BLK:(j + 1) * FF_BLK, :])
            gt = gu_ref[:, g_cols].astype(F32)
            up = gu_ref[:, u_cols].astype(F32)
            sg = _sigmoid(gt)
            silu = gt * sg
            act_ref[:, j * FF_BLK:(j + 1) * FF_BLK] = (silu * up).astype(BF16)
            dgu_ref[:, g_cols] = (dact * up * (sg + silu * (1.0 - sg))).astype(BF16)
            dgu_ref[:, u_cols] = (dact * silu).astype(BF16)

    return _carried_call(
        body, ex, (s // tm,),
        [pl.BlockSpec((tm, d), lambda i: (i, 0)), pl.BlockSpec((tm, 2 * D_FF), lambda i: (i, 0)),
         pl.BlockSpec((D_FF, d), lambda i: (0, 0), pipeline_mode=pl.Buffered(1))],
        [pl.BlockSpec((tm, 2 * D_FF), lambda i: (i, 0)), pl.BlockSpec((tm, D_FF), lambda i: (i, 0))],
        [jax.ShapeDtypeStruct((s, 2 * D_FF), BF16), jax.ShapeDtypeStruct((s, D_FF), BF16)], [],
        ("arbitrary",), name, (dx2, gu, w), vmem=56 * 2 ** 20)


def loss_kernel(y, tgt, name):
    s, d = y.shape
    tm = _tile(s, 512)

    def body(y_ref, t_ref, l_ref, dy_ref):
        @pl.when(pl.program_id(0) == 0)
        def _():
            l_ref[...] = jnp.zeros_like(l_ref)

        err = y_ref[...] - t_ref[...]
        dy_ref[...] = err * (1.0 / d)
        l_ref[...] += jnp.sum(jnp.sum(err * err, axis=1, keepdims=True), axis=0, keepdims=True)

    return pl.pallas_call(
        body, grid=(s // tm,),
        in_specs=[pl.BlockSpec((tm, d), lambda i: (i, 0)), pl.BlockSpec((tm, d), lambda i: (i, 0))],
        out_specs=[pl.BlockSpec((8, 128), lambda i: (0, 0)), pl.BlockSpec((tm, d), lambda i: (i, 0))],
        out_shape=[jax.ShapeDtypeStruct((8, 128), F32), jax.ShapeDtypeStruct((s, d), F32)],
        compiler_params=_cparams(("arbitrary",)), name=name)(y, tgt)


def _split3(v):
    a1 = v.astype(BF16)
    r1 = v - a1.astype(F32)
    a2 = r1.astype(BF16)
    a3 = (r1 - a2.astype(F32)).astype(BF16)
    return a1, a2, a3


def _running_sum(v, carry_ref, reverse):
    tm = v.shape[0]
    row = lax.broadcasted_iota(jnp.int32, (tm, tm), 0)
    col = lax.broadcasted_iota(jnp.int32, (tm, tm), 1)
    tri = ((col >= row) if reverse else (row >= col)).astype(BF16)
    a1, a2, a3 = _split3(v)
    out = _dot(tri, a1) + _dot(tri, a2) + _dot(tri, a3) + carry_ref[...]
    carry_ref[...] = out[0:1, :] if reverse else out[tm - 1:tm, :]
    return out


HEAD_GROUP_FWD = 8
HEAD_GROUP_BWD = 8
LANE_C = 64
LANE_ONE = 67


def _lanes():
    lane = lax.broadcasted_iota(jnp.int32, (1, 128), 1)
    return lane, lane < HEAD_DIM


def _half_mean(t, lo):
    s_lo = jnp.sum(jnp.where(lo, t, 0.0), axis=-1, keepdims=True)
    s_hi = jnp.sum(jnp.where(lo, 0.0, t), axis=-1, keepdims=True)
    return jnp.where(lo, s_lo, s_hi) * (1.0 / HEAD_DIM)


def _lane_col(t, lane, idx):
    return jnp.sum(jnp.where(lane == idx, t, 0.0), axis=-1, keepdims=True)


def _swap_halves(t):
    return pltpu.roll(t, HEAD_DIM, 1)


def attn_prep(proj, h, wt_in, fb, gq2, gk2, name):
    s, d = h.shape
    tm = _tile(s, 512)
    first = N_REST // D_ATTN

    def body(q_ref, k_ref, v_ref, h_ref, wf_ref, fb_ref, gq_ref, gk_ref, qa_ref, ka_ref, va_ref, vt_ref, z_ref, carry_ref):
        lane, lo = _lanes()

        @pl.when(pl.program_id(0) == 0)
        def _():
            carry_ref[...] = jnp.zeros_like(carry_ref)

        z = _dot_nt(h_ref[...], wf_ref[...]) + fb_ref[...]
        z_ref[...] = z
        cv = _running_sum(jnp.minimum(z, 0.0) - jnp.log(1.0 + jnp.exp(-jnp.abs(z))), carry_ref, reverse=False)

        def normed(t, g):
            t = t.astype(F32)
            return t * lax.rsqrt(_half_mean(t * t, lo) + EPS) * g

        one_q = jnp.where((lane >= LANE_ONE) & (lane < LANE_ONE + 3), 1.0, 0.0)
        one_k = jnp.where((lane >= LANE_C) & (lane < LANE_C + 3), 1.0, 0.0)
        one_v = jnp.where(lane == LANE_C, 1.0, 0.0)
        for j in range(HEADS // 2):
            cols = slice(128 * j, 128 * (j + 1))
            qn = normed(q_ref[:, cols], gq_ref[...] * ATTN_SCALE)
            kn = normed(k_ref[:, cols], gk_ref[...])
            vv = v_ref[:, cols].astype(F32)
            for e in range(2):
                h = 2 * j + e
                pick = (lambda t: t) if e == 0 else _swap_halves
                pieces = [p.astype(F32) for p in _split3(_lane_col(cv, lane, h))]
                ext_q, ext_k = one_q, one_k
                for i, p in enumerate(pieces):
                    ext_q = jnp.where(lane == LANE_C + i, p, ext_q)
                    ext_k = jnp.where(lane == LANE_ONE + i, -p, ext_k)
                qa_ref[h] = jnp.where(lo, pick(qn), ext_q).astype(BF16)
                ka_ref[h] = jnp.where(lo, pick(kn), ext_k).astype(BF16)
                va = jnp.where(lo, pick(vv), one_v)
                va_ref[h] = va.astype(BF16)
                vt_ref[h] = va.T.astype(BF16)

    tile = lambda blk: pl.BlockSpec((tm, D_ATTN), lambda i: (i, blk))
    vec = pl.BlockSpec((1, 128), lambda i: (0, 0))
    out = pl.BlockSpec((HEADS, tm, 128), lambda i: (0, i, 0))
    return pl.pallas_call(
        body, grid=(s // tm,),
        in_specs=[tile(first), tile(first + 1), tile(first + 2), pl.BlockSpec((tm, d), lambda i: (i, 0)),
                  pl.BlockSpec((128, d), lambda i: (N_MAIN // 128, 0)), vec, vec, vec],
        out_specs=[out, out, out, pl.BlockSpec((HEADS, 128, tm), lambda i: (0, 0, i)),
                   pl.BlockSpec((tm, 128), lambda i: (i, 0))],
        out_shape=[jax.ShapeDtypeStruct((HEADS, s, 128), BF16)] * 3 + [jax.ShapeDtypeStruct((HEADS, 128, s), BF16),
                                                                       jax.ShapeDtypeStruct((s, 128), F32)],
        scratch_shapes=[pltpu.VMEM((1, 128), F32)],
        compiler_params=_cparams(("arbitrary",)), name=name)(proj, proj, proj, h, wt_in, fb, gq2, gk2)


def _carry(ex, n_in, n_out, n_scratch, grid):
    n_xin, n_xout = (len(ex.inputs), len(ex.out_shapes)) if ex else (0, 0)

    def split(refs):
        ins, xins = refs[:n_in], refs[n_in:n_in + n_xin]
        rest = refs[n_in + n_xin:]
        outs, xouts = rest[:n_out], rest[n_out:n_out + n_xout]
        rest = rest[n_out + n_xout:]
        return ins + outs + rest[:n_scratch], (xins, xouts, rest[n_scratch:])

    def first():
        return functools.reduce(lambda a, b: a & b, [pl.program_id(d) == 0 for d in range(len(grid))])

    def last():
        return functools.reduce(lambda a, b: a & b, [pl.program_id(d) == grid[d] - 1 for d in range(len(grid))])

    return split, first, last


def _carried_call(body, ex, grid, in_specs, out_specs, out_shape, scratch, sem, name, operands, vmem=None):
    any_spec = pl.BlockSpec(memory_space=pl.ANY)
    split, first, last = _carry(ex, len(in_specs), len(out_specs), len(scratch), grid)

    def carried(*refs):
        own, xrefs = split(refs)
        if ex:
            @pl.when(first())
            def _():
                ex.start(*xrefs)

        body(*own)
        if ex:
            @pl.when(last())
            def _():
                ex.drain(*xrefs)

    n_xin = len(ex.inputs) if ex else 0
    results = pl.pallas_call(
        carried, grid=grid, in_specs=list(in_specs) + [any_spec] * n_xin,
        out_specs=list(out_specs) + [any_spec] * (len(ex.out_shapes) if ex else 0),
        out_shape=list(out_shape) + (list(ex.out_shapes) if ex else []),
        input_output_aliases={len(in_specs) + i: len(out_specs) + o for i, o in ex.aliases.items()} if ex else {},
        scratch_shapes=list(scratch) + (ex.scratch if ex else []),
        compiler_params=_cparams(sem, vmem), name=name)(*operands, *(ex.inputs if ex else []))
    return results[:len(out_specs)], results[len(out_specs):]


def _tri_rows(t, n):
    qi = sum(jnp.where(t >= r * (r + 1) // 2, 1, 0) for r in range(1, n))
    return qi, t - qi * (qi + 1) // 2


def _tri_cols(t, n):
    ki = sum(jnp.where(t >= r * n - r * (r - 1) // 2, 1, 0) for r in range(1, n))
    return ki, ki + t - (ki * n - ki * (ki - 1) // 2)


def _causal_t(st_blk, tk, tq):
    key = lax.broadcasted_iota(jnp.int32, (tk, tq), 0)
    qry = lax.broadcasted_iota(jnp.int32, (tk, tq), 1)
    return jnp.where(qry >= key, st_blk, -jnp.inf)


def attn_forward(qa, ka, vt, name, ex=None):
    hh, s, _ = qa.shape
    tq = tk = _tile(s, 512)
    nq = s // tq
    grp = HEAD_GROUP_FWD

    def body(q_ref, k_ref, vt_ref, o_ref, lse_ref, m_ref, acc_ref):
        qi, ki = _tri_rows(pl.program_id(1), nq)

        @pl.when(ki == 0)
        def _():
            m_ref[...] = jnp.full_like(m_ref, -jnp.inf)
            acc_ref[...] = jnp.zeros_like(acc_ref)

        def step(masked):
            nxt = _dot_nt(k_ref[0], q_ref[0])
            for g in range(grp):
                st = nxt
                if g + 1 < grp:
                    nxt = _dot_nt(k_ref[g + 1], q_ref[g + 1])
                if masked:
                    st = _causal_t(st, tk, tq)
                m_old = m_ref[g]
                m_new = jnp.maximum(m_old, jnp.max(st, axis=0, keepdims=True))
                pt = jnp.exp(st - m_new).astype(BF16)
                acc_ref[g] = jnp.exp(m_old - m_new) * acc_ref[g] + _dot(vt_ref[g], pt)
                m_ref[g] = m_new

        def diagonal_step():
            hk, hq = tk // 2, tq // 2
            for g in range(grp):
                st_a = _causal_t(_dot_nt(k_ref[g, :hk, :], q_ref[g]), hk, tq)
                st_b = _causal_t(_dot_nt(k_ref[g, hk:, :], q_ref[g, hq:, :]), hk, tq - hq)
                m_old = m_ref[g]
                m_b = jnp.concatenate([jnp.full((1, hq), -jnp.inf, F32), jnp.max(st_b, axis=0, keepdims=True)], axis=1)
                m_new = jnp.maximum(jnp.maximum(m_old, jnp.max(st_a, axis=0, keepdims=True)), m_b)
                pt_a = jnp.exp(st_a - m_new).astype(BF16)
                pt_b = jnp.exp(st_b - m_new[:, hq:]).astype(BF16)
                pv_b = jnp.concatenate([jnp.zeros((128, hq), F32), _dot(vt_ref[g, :, hk:], pt_b)], axis=1)
                acc_ref[g] = jnp.exp(m_old - m_new) * acc_ref[g] + _dot(vt_ref[g, :, :hk], pt_a) + pv_b
                m_ref[g] = m_new

        @pl.when(ki < qi)
        def _():
            step(False)

        @pl.when(ki == qi)
        def _():
            diagonal_step()
            for g in range(grp):
                acc = acc_ref[g]
                denom = acc[LANE_C:LANE_C + 1, :]
                o_ref[g] = (acc / denom).T.astype(BF16)
                lse_ref[g] = m_ref[g] + jnp.log(denom)

    qspec = pl.BlockSpec((grp, tq, 128), lambda h, t: (h, _tri_rows(t, nq)[0], 0))
    kspec = pl.BlockSpec((grp, tk, 128), lambda h, t: (h, _tri_rows(t, nq)[1], 0))
    vspec = pl.BlockSpec((grp, 128, tk), lambda h, t: (h, 0, _tri_rows(t, nq)[1]))
    lspec = pl.BlockSpec((grp, 1, tq), lambda h, t: (h, 0, _tri_rows(t, nq)[0]))
    return _carried_call(
        body, ex, (hh // grp, nq * (nq + 1) // 2), [qspec, kspec, vspec], [qspec, lspec],
        [jax.ShapeDtypeStruct((hh, s, 128), BF16), jax.ShapeDtypeStruct((hh, 1, s), F32)],
        [pltpu.VMEM((grp, 1, tq), F32), pltpu.VMEM((grp, 128, tq), F32)],
        ("arbitrary", "arbitrary"), name, (qa, ka, vt))


def attn_backward(qa, ka, va, oa, doa, lse, name, ex=None):
    hh, s, _ = qa.shape
    tq = tk = _tile(s, 512)
    nq = s // tq
    grp = HEAD_GROUP_BWD

    def body(q_ref, k_ref, v_ref, o_ref, do_ref, lse_ref, dq_ref, dk_ref, dv_ref, dka_ref, dva_ref):
        ki, qi = _tri_cols(pl.program_id(1), nq)

        @pl.when(pl.program_id(1) == 0)
        def _():
            dq_ref[...] = jnp.zeros_like(dq_ref)

        @pl.when(qi == ki)
        def _():
            dka_ref[...] = jnp.zeros_like(dka_ref)
            dva_ref[...] = jnp.zeros_like(dva_ref)

        def step(masked):
            rows = pl.ds(pl.multiple_of(qi * tq, tq), tq)
            products = lambda g: (_dot_nt(k_ref[g], q_ref[g]), _dot_nt(v_ref[g], do_ref[g]))
            nxt = products(0)
            for g in range(grp):
                st, dpt = nxt
                if g + 1 < grp:
                    nxt = products(g + 1)
                q, k, do = q_ref[g], k_ref[g], do_ref[g]
                if masked:
                    st = _causal_t(st, tk, tq)
                pt = jnp.exp(st - lse_ref[g])
                delta = jnp.sum((do.astype(F32) * o_ref[g].astype(F32)).T, axis=0, keepdims=True)
                dst = (pt * (dpt - delta)).astype(BF16)
                dva_ref[g] += _dot(pt.astype(BF16), do)
                dka_ref[g] += _dot(dst, q)
                dq_ref[g, rows, :] += _dot_tn(dst, k)

        def diagonal_step():
            hk, hq = tk // 2, tq // 2
            base = pl.multiple_of(qi * tq, tq)
            for g in range(grp):
                q, do = q_ref[g], do_ref[g]
                q_b, do_b = q_ref[g, hq:, :], do_ref[g, hq:, :]
                k_a, k_b = k_ref[g, :hk, :], k_ref[g, hk:, :]
                lse = lse_ref[g]
                delta = jnp.sum((do.astype(F32) * o_ref[g].astype(F32)).T, axis=0, keepdims=True)
                pt_a = jnp.exp(_causal_t(_dot_nt(k_a, q), hk, tq) - lse)
                pt_b = jnp.exp(_causal_t(_dot_nt(k_b, q_b), hk, tq - hq) - lse[:, hq:])
                dst_a = (pt_a * (_dot_nt(v_ref[g, :hk, :], do) - delta)).astype(BF16)
                dst_b = (pt_b * (_dot_nt(v_ref[g, hk:, :], do_b) - delta[:, hq:])).astype(BF16)
                dva_ref[g, :hk, :] += _dot(pt_a.astype(BF16), do)
                dva_ref[g, hk:, :] += _dot(pt_b.astype(BF16), do_b)
                dka_ref[g, :hk, :] += _dot(dst_a, q)
                dka_ref[g, hk:, :] += _dot(dst_b, q_b)
                dq_ref[g, pl.ds(base, tq), :] += _dot_tn(dst_a, k_a)
                dq_ref[g, pl.ds(pl.multiple_of(base + hq, hq), tq - hq), :] += _dot_tn(dst_b, k_b)

        @pl.when(qi > ki)
        def _():
            step(False)

        @pl.when(qi == ki)
        def _():
            diagonal_step()

        @pl.when(qi == nq - 1)
        def _():
            dk_ref[...] = dka_ref[...]
            dv_ref[...] = dva_ref[...].astype(BF16)

    qspec = pl.BlockSpec((grp, tq, 128), lambda h, t: (h, _tri_cols(t, nq)[1], 0))
    lspec = pl.BlockSpec((grp, 1, tq), lambda h, t: (h, 0, _tri_cols(t, nq)[1]))
    kspec = pl.BlockSpec((grp, tk, 128), lambda h, t: (h, _tri_cols(t, nq)[0], 0))
    return _carried_call(
        body, ex, (hh // grp, nq * (nq + 1) // 2), [qspec, kspec, kspec, qspec, qspec, lspec],
        [pl.BlockSpec((grp, s, 128), lambda h, t: (h, 0, 0), pipeline_mode=pl.Buffered(1)), kspec, kspec],
        [jax.ShapeDtypeStruct((hh, s, 128), F32), jax.ShapeDtypeStruct((hh, s, 128), F32),
         jax.ShapeDtypeStruct((hh, s, 128), BF16)],
        [pltpu.VMEM((grp, tk, 128), F32), pltpu.VMEM((grp, tk, 128), F32)],
        ("arbitrary", "arbitrary"), name, (qa, ka, va, oa, doa, lse), vmem=58 * 2 ** 20)


def attn_post(dqa, dka, dva, proj, z, gq2, gk2, dproj, name):
    s = proj.shape[0]
    tm = _tile(s, 512)
    nt = s // tm

    def body(dq_ref, dk_ref, dv_ref, q_ref, k_ref, z_ref, gq_ref, gk_ref, dp_any, dp_ref, dgq_ref, dgk_ref, db_ref,
             carry_ref):
        lane, lo = _lanes()

        @pl.when(pl.program_id(0) == 0)
        def _():
            dgq_ref[...] = jnp.zeros_like(dgq_ref)
            dgk_ref[...] = jnp.zeros_like(dgk_ref)
            db_ref[...] = jnp.zeros_like(db_ref)
            carry_ref[...] = jnp.zeros_like(carry_ref)

        def pair(ref, j):
            return jnp.where(lo, ref[2 * j].astype(F32), _swap_halves(ref[2 * j + 1].astype(F32)))

        def norm_bwd(raw, g, dhat, scale):
            r = lax.rsqrt(_half_mean(raw * raw, lo) + EPS)
            y = raw * r
            dy = dhat * (g * scale)
            return r * (dy - y * _half_mean(dy * y, lo)), jnp.sum(dhat * y, axis=0, keepdims=True) * scale

        dc = jnp.zeros((tm, 128), F32)
        for j in range(HEADS // 2):
            cols = slice(128 * j, 128 * (j + 1))
            dq, dgq = norm_bwd(q_ref[:, cols].astype(F32), gq_ref[...], pair(dq_ref, j), ATTN_SCALE)
            dk, dgk = norm_bwd(k_ref[:, cols].astype(F32), gk_ref[...], pair(dk_ref, j), 1.0)
            dgq_ref[...] += dgq
            dgk_ref[...] += dgk
            dp_ref[:, cols] = dq.astype(BF16)
            dp_ref[:, D_ATTN + 128 * j:D_ATTN + 128 * (j + 1)] = dk.astype(BF16)
            dp_ref[:, 2 * D_ATTN + 128 * j:2 * D_ATTN + 128 * (j + 1)] = pair(dv_ref, j).astype(BF16)
            for e in range(2):
                h = 2 * j + e
                both = jnp.where(lane == LANE_C, dq_ref[h], 0.0) - jnp.where(lane == LANE_ONE, dk_ref[h], 0.0)
                dc = jnp.where(lane == h, jnp.sum(both, axis=-1, keepdims=True), dc)
        dz = _running_sum(dc, carry_ref, reverse=True) * (1.0 - _sigmoid(z_ref[...]))
        db_ref[...] += jnp.sum(dz, axis=0, keepdims=True)
        dp_ref[:, 3 * D_ATTN:3 * D_ATTN + 128] = dz.astype(BF16)
        dp_ref[:, 3 * D_ATTN + 128:] = jnp.zeros((tm, DPROJ_TAIL - 3 * D_ATTN - 128), BF16)

    heads = lambda: pl.BlockSpec((HEADS, tm, 128), lambda i: (0, nt - 1 - i, 0))
    vec = pl.BlockSpec((1, 128), lambda i: (0, 0))
    first = N_REST // D_ATTN
    return pl.pallas_call(
        body, grid=(nt,),
        in_specs=[heads(), heads(), heads(), pl.BlockSpec((tm, D_ATTN), lambda i: (nt - 1 - i, first)),
                  pl.BlockSpec((tm, D_ATTN), lambda i: (nt - 1 - i, first + 1)),
                  pl.BlockSpec((tm, 128), lambda i: (nt - 1 - i, 0)), vec, vec, pl.BlockSpec(memory_space=pl.ANY)],
        out_specs=[pl.BlockSpec((tm, DPROJ_TAIL), lambda i: (nt - 1 - i, N_REST // DPROJ_TAIL)), vec, vec, vec],
        out_shape=[jax.ShapeDtypeStruct(dproj.shape, BF16), jax.ShapeDtypeStruct((1, 128), F32),
                   jax.ShapeDtypeStruct((1, 128), F32), jax.ShapeDtypeStruct((1, 128), F32)],
        scratch_shapes=[pltpu.VMEM((1, 128), F32)], input_output_aliases={8: 0},
        compiler_params=_cparams(("arbitrary",)), name=name)(dqa, dka, dva, proj, proj, z, gq2, gk2, dproj)


def _pool_groups(tm):
    gid = lax.broadcasted_iota(jnp.int32, (1, D_POOL), 1) // (D_POOL // 4)
    win = jnp.where(gid == 0, 2.0, jnp.where(gid == 1, 4.0, jnp.where(gid == 2, 8.0, 16.0)))
    return gid, win


def _by_group(gid, v2, v4, v8, v16):
    return jnp.where(gid == 0, v2, jnp.where(gid == 1, v4, jnp.where(gid == 2, v8, v16)))


def _branches(rest_ref, halo_ref, a_ref, wa_ref, wc_ref, wp_ref, sc_ref, cw_ref, ti, tm):
    f = lambda v: v.astype(F32)
    cx, cb, cc, px = f(rest_ref[:, 0:256]), f(rest_ref[:, 256:512]), f(rest_ref[:, 512:768]), f(rest_ref[:, 768:1024])
    live = jnp.where(ti > 0, 1.0, 0.0)
    hz = f(halo_ref[:, 0:256]) * f(halo_ref[:, 512:768]) * live
    hp = f(halo_ref[:, 768:1024]) * live
    z = cc * cx
    zf = jnp.concatenate([hz, z], axis=0)
    z1 = pltpu.roll(zf, 1, 0)[HALO:]
    z2 = pltpu.roll(zf, 2, 0)[HALO:]
    cw = cw_ref[...]
    conv = cw[2:3] * z + cw[1:2] * z1 + cw[0:1] * z2
    uc = cb * conv
    pf = jnp.concatenate([hp, px], axis=0)
    s2 = pf + pltpu.roll(pf, 1, 0)
    s4 = s2 + pltpu.roll(s2, 2, 0)
    s8 = s4 + pltpu.roll(s4, 4, 0)
    s16 = s8 + pltpu.roll(s8, 8, 0)
    gid, win = _pool_groups(tm)
    t = (ti * tm + lax.broadcasted_iota(jnp.int32, (tm, 1), 0)).astype(F32)
    inv = 1.0 / jnp.minimum(t + 1.0, win)
    dpool = _by_group(gid, s2[HALO:], s4[HALO:], s8[HALO:], s16[HALO:]) * inv - px
    _, lo = _lanes()
    a_tok = [jnp.where(lo, f(a_ref[2 * j]), _swap_halves(f(a_ref[2 * j + 1]))).astype(BF16) for j in range(HEADS // 2)]
    y_attn = _dot(a_tok[0], wa_ref[0:128, :])
    for j in range(1, HEADS // 2):
        y_attn += _dot(a_tok[j], wa_ref[128 * j:128 * (j + 1), :])
    y_conv = _dot(uc.astype(BF16), wc_ref[...])
    y_pool_raw = _dot(dpool.astype(BF16), wp_ref[...])
    sg = [_sigmoid(f(rest_ref[:, 1024 + i * D_MODEL:1024 + (i + 1) * D_MODEL])) for i in range(3)]
    return dict(cx=cx, cb=cb, cc=cc, z=z, z1=z1, z2=z2, conv=conv, uc=uc, dpool=dpool, inv=inv, gid=gid, a_tok=a_tok,
                y_attn=y_attn, y_conv=y_conv, y_pool_raw=y_pool_raw, sg=sg, cw=cw)


def _mix_specs(tm, ti_of):
    blocks_per_tile = tm // HALO
    return [
        pl.BlockSpec((tm, N_REST), lambda i: (ti_of(i), 0)),
        pl.BlockSpec((HALO, 1024), lambda i: (jnp.maximum(ti_of(i) * blocks_per_tile - 1, 0), 0)),
        pl.BlockSpec((HEADS, tm, 128), lambda i: (0, ti_of(i), 0)),
        pl.BlockSpec((D_ATTN, D_MODEL), lambda i: (0, 0), pipeline_mode=pl.Buffered(1)),
        pl.BlockSpec((D_CONV, D_MODEL), lambda i: (0, 0), pipeline_mode=pl.Buffered(1)),
        pl.BlockSpec((D_POOL, D_MODEL), lambda i: (0, 0), pipeline_mode=pl.Buffered(1)),
        pl.BlockSpec((1, D_MODEL), lambda i: (0, 0)),
        pl.BlockSpec((8, D_CONV), lambda i: (0, 0)),
    ]


def mix_fwd(proj, a, x, wa, wc, wp, scale, cw, wo, name, ex=None):
    s = x.shape[0]
    tm = _tile(s, 512)

    def body(rest_ref, halo_ref, a_ref, wa_ref, wc_ref, wp_ref, sc_ref, cw_ref, wo_ref, x_ref, o_ref):
        b = _branches(rest_ref, halo_ref, a_ref, wa_ref, wc_ref, wp_ref, sc_ref, cw_ref, pl.program_id(0), tm)
        merged = b["sg"][0] * b["y_attn"] + b["sg"][1] * b["y_conv"] + b["sg"][2] * (b["y_pool_raw"] * sc_ref[...])
        o_ref[...] = x_ref[...] + _dot(merged.astype(BF16), wo_ref[...])

    (x1,), carried = _carried_call(
        body, ex, (s // tm,),
        _mix_specs(tm, lambda i: i) + [pl.BlockSpec((D_MODEL, D_MODEL), lambda i: (0, 0), pipeline_mode=pl.Buffered(1)),
                                       pl.BlockSpec((tm, D_MODEL), lambda i: (i, 0))],
        [pl.BlockSpec((tm, D_MODEL), lambda i: (i, 0))], [jax.ShapeDtypeStruct((s, D_MODEL), F32)], [],
        ("arbitrary",), name, (proj, proj, a, wa, wc, wp, scale, cw, wo, x), vmem=58 * 2 ** 20)
    return x1, carried


def mix_bwd(proj, a, dx1, wa, wc, wp, scale, cw, wo, name):
    s = dx1.shape[0]
    tm = _tile(s, 512)
    nt = s // tm
    ti_of = lambda i: nt - 1 - i
    n = tm + HALO

    def body(rest_ref, halo_ref, a_ref, wa_ref, wc_ref, wp_ref, sc_ref, cw_ref, wo_ref,
             dx_ref, dp_ref, da_ref, at_ref, mg_ref, dya_ref, dyc_ref, dyp_ref, uc_ref, dd_ref, dsc_ref, dcw_ref,
             cdc_ref, cde_ref):
        i = pl.program_id(0)
        ti = ti_of(i)

        @pl.when(i == 0)
        def _():
            cdc_ref[...] = jnp.zeros_like(cdc_ref)
            cde_ref[...] = jnp.zeros_like(cde_ref)
            dsc_ref[...] = jnp.zeros_like(dsc_ref)
            dcw_ref[...] = jnp.zeros_like(dcw_ref)

        b = _branches(rest_ref, halo_ref, a_ref, wa_ref, wc_ref, wp_ref, sc_ref, cw_ref, ti, tm)
        sg, sc = b["sg"], sc_ref[...]
        y_pool = b["y_pool_raw"] * sc
        merged = sg[0] * b["y_attn"] + sg[1] * b["y_conv"] + sg[2] * y_pool
        mg_ref[...] = merged.astype(BF16)
        dm = _dot_nt(dx_ref[...].astype(BF16), wo_ref[...])
        dys = [dm * sg[j] for j in range(3)]
        for j, y in enumerate((b["y_attn"], b["y_conv"], y_pool)):
            dp_ref[:, 1024 + j * D_MODEL:1024 + (j + 1) * D_MODEL] = (dys[j] * y * (1.0 - sg[j])).astype(BF16)
        dya = dys[0].astype(BF16)
        dya_ref[...] = dya
        _, lo = _lanes()
        for j in range(HEADS // 2):
            at_ref[:, 128 * j:128 * (j + 1)] = b["a_tok"][j]
            da = _dot_nt(dya, wa_ref[128 * j:128 * (j + 1), :])
            da_ref[2 * j] = jnp.where(lo, da, 0.0).astype(BF16)
            da_ref[2 * j + 1] = jnp.where(lo, _swap_halves(da), 0.0).astype(BF16)
        dyc = dys[1].astype(BF16)
        dyc_ref[...] = dyc
        duc = _dot_nt(dyc, wc_ref[...])
        dyp = dys[2]
        dsc_ref[...] += jnp.sum(dyp * b["y_pool_raw"], axis=0, keepdims=True)
        dypr = (dyp * sc).astype(BF16)
        dyp_ref[...] = dypr
        ddp = _dot_nt(dypr, wp_ref[...])
        uc_ref[...] = b["uc"].astype(BF16)
        dd_ref[...] = b["dpool"].astype(BF16)

        dconv = duc * b["cb"]
        dp_ref[:, 256:512] = (duc * b["conv"]).astype(BF16)
        dcf = jnp.concatenate([dconv, cdc_ref[...]], axis=0)
        cw = b["cw"]
        dz = cw[2:3] * dconv + cw[1:2] * pltpu.roll(dcf, n - 1, 0)[:tm] + cw[0:1] * pltpu.roll(dcf, n - 2, 0)[:tm]
        dp_ref[:, 0:256] = (dz * b["cc"]).astype(BF16)
        dp_ref[:, 512:768] = (dz * b["cx"]).astype(BF16)
        dcw_ref[0:1, :] += jnp.sum(dconv * b["z2"], axis=0, keepdims=True)
        dcw_ref[1:2, :] += jnp.sum(dconv * b["z1"], axis=0, keepdims=True)
        dcw_ref[2:3, :] += jnp.sum(dconv * b["z"], axis=0, keepdims=True)
        cdc_ref[...] = dconv[:HALO]

        e = ddp * b["inv"]
        ef = jnp.concatenate([e, cde_ref[...]], axis=0)
        r2 = ef + pltpu.roll(ef, n - 1, 0)
        r4 = r2 + pltpu.roll(r2, n - 2, 0)
        r8 = r4 + pltpu.roll(r4, n - 4, 0)
        r16 = r8 + pltpu.roll(r8, n - 8, 0)
        dp_ref[:, 768:1024] = (_by_group(b["gid"], r2[:tm], r4[:tm], r8[:tm], r16[:tm]) - ddp).astype(BF16)
        cde_ref[...] = e[:HALO]

    tile = lambda w: pl.BlockSpec((tm, w), lambda i: (ti_of(i), 0))
    whole = lambda r, c: pl.BlockSpec((r, c), lambda i: (0, 0))
    bf = lambda w: jax.ShapeDtypeStruct((s, w), BF16)
    return pl.pallas_call(
        body, grid=(nt,),
        in_specs=_mix_specs(tm, ti_of) + [pl.BlockSpec((D_MODEL, D_MODEL), lambda i: (0, 0), pipeline_mode=pl.Buffered(1)),
                                          tile(D_MODEL)],
        out_specs=[tile(N_REST), pl.BlockSpec((HEADS, tm, 128), lambda i: (0, ti_of(i), 0)), tile(D_ATTN),
                   tile(D_MODEL), tile(D_MODEL), tile(D_MODEL), tile(D_MODEL),
                   tile(D_CONV), tile(D_POOL), whole(1, D_MODEL), whole(8, D_CONV)],
        out_shape=[bf(DPROJ_COLS), jax.ShapeDtypeStruct((HEADS, s, 128), BF16), bf(D_ATTN),
                   bf(D_MODEL), bf(D_MODEL), bf(D_MODEL), bf(D_MODEL), bf(D_CONV), bf(D_POOL),
                   jax.ShapeDtypeStruct((1, D_MODEL), F32), jax.ShapeDtypeStruct((8, D_CONV), F32)],
        scratch_shapes=[pltpu.VMEM((HALO, D_CONV), F32), pltpu.VMEM((HALO, D_POOL), F32)],
        compiler_params=_cparams(("arbitrary",), 58 * 2 ** 20), name=name)(proj, proj, a, wa, wc, wp, scale, cw, wo, dx1)


def _adamw_math(w, g, m, v):
    m = ADAM_B1 * m + (1.0 - ADAM_B1) * g
    v = ADAM_B2 * v + (1.0 - ADAM_B2) * (g * g)
    m_hat = m / (1.0 - ADAM_B1 ** ADAM_STEP)
    v_hat = v / (1.0 - ADAM_B2 ** ADAM_STEP)
    delta = -ADAM_LR * (m_hat / (jnp.sqrt(v_hat) + ADAM_EPS) + ADAM_WD * w)
    return delta, m, v


ADAMW_PARTS_BLOCK_BYTES = 4 * 2 ** 20


def _row_tile(rows, cols, copies, itemsize):
    row_bytes = copies * (-(-cols // 128) * 128) * itemsize
    fits = [t for t in range(16, rows + 1, 16) if rows % t == 0 and t * row_bytes <= ADAMW_PARTS_BLOCK_BYTES]
    return max(fits) if fits else rows


def pair_sum(blocks, stage, me, name):
    n_slots, rows, cols = stage.shape
    tr = _row_tile(rows, cols, 1, 4)

    def body(me_ref, a_ref, b_ref, o_ref):
        o_ref[...] = (a_ref[...].astype(F32) + b_ref[...].astype(F32)).astype(BF16)

    slot = pl.BlockSpec((None, tr, cols), lambda i, r, me_ref: (i, r, 0))
    return pl.pallas_call(
        body, out_shape=jax.ShapeDtypeStruct(stage.shape, BF16),
        grid_spec=pltpu.PrefetchScalarGridSpec(
            num_scalar_prefetch=1, grid=(n_slots, rows // tr),
            in_specs=[pl.BlockSpec((None, tr, cols), lambda i, r, me_ref: (me_ref[0] ^ (2 * i), r, 0)), slot],
            out_specs=slot),
        compiler_params=_cparams(("parallel", "parallel")), name=name)(me.reshape(1), blocks, stage)


def adamw_sum(parts, w, m, v, name):
    layers, rows, cols = w.shape
    n_parts = parts.shape[1]
    if rows % 16 == 0:
        tr, tc = _row_tile(rows, cols, n_parts, parts.dtype.itemsize), cols
    else:
        tr, tc = rows, _pick(cols, (256, 128))

    def body(p_ref, w_ref, m_ref, v_ref, g_ref, d_ref, nm_ref, nv_ref):
        g = p_ref[0].astype(F32)
        for i in range(1, n_parts):
            g = g + p_ref[i].astype(F32)
        g_ref[...] = g
        d_ref[...], nm_ref[...], nv_ref[...] = _adamw_math(w_ref[...], g, m_ref[...], v_ref[...])

    spec = pl.BlockSpec((None, tr, tc), lambda l, i, j: (l, i, j))
    return pl.pallas_call(
        body, grid=(layers, rows // tr, cols // tc),
        in_specs=[pl.BlockSpec((None, n_parts, tr, tc), lambda l, i, j: (l, 0, i, j)), spec, spec, spec],
        out_specs=[spec] * 4, out_shape=[jax.ShapeDtypeStruct((layers, rows, cols), F32)] * 4,
        compiler_params=_cparams(("parallel", "parallel", "parallel")), name=name)(parts, w, m, v)


def _me():
    return lax.axis_index("x"), lax.axis_index("y"), lax.axis_index("c")


N_PEERS = N_DEV - 1


def all_gather(shards, name):
    n = len(shards)
    any_spec = pl.BlockSpec(memory_space=pl.ANY)

    def body(*refs):
        x_refs, out_refs = refs[:n], refs[n:2 * n]
        send_sems, recv_sems, local_sems = refs[2 * n:]
        x, y, c = _me()
        me, sibling = (x, y, c), (x, y, 1 - c)
        chips = [(1 - x, y), (x, 1 - y), (1 - x, 1 - y)]

        def copy(t, k, block, to, from_input=False):
            slot = out_refs[t].at[4 * block[0] + 2 * block[1] + block[2]]
            return pltpu.make_async_remote_copy(
                src_ref=x_refs[t] if from_input else slot, dst_ref=slot, send_sem=send_sems.at[N_PEERS * t + k],
                recv_sem=recv_sems.at[N_PEERS * t + k], device_id=to, device_id_type=pl.DeviceIdType.MESH)

        mine = [pltpu.make_async_copy(x_refs[t], out_refs[t].at[4 * x + 2 * y + c], local_sems.at[t]) for t in range(n)]
        started = []
        for t in range(n):
            mine[t].start()
            started.append(copy(t, 0, me, sibling, from_input=True))
            started += [copy(t, 1 + j, me, (*chip, c), from_input=True) for j, chip in enumerate(chips)]
        for cp in started:
            cp.start()
        for j, chip in enumerate(chips):
            for t in range(n):
                copy(t, 1 + j, (*chip, c), me).wait_recv()
                fwd = copy(t, 4 + j, (*chip, c), sibling)
                fwd.start()
                started.append(fwd)
        for t in range(n):
            copy(t, 0, sibling, me).wait_recv()
            for j, chip in enumerate(chips):
                copy(t, 4 + j, (*chip, 1 - c), me).wait_recv()
        for cp in started:
            cp.wait_send()
        for cp in mine:
            cp.wait()

    return pl.pallas_call(
        body, out_shape=[jax.ShapeDtypeStruct((N_DEV,) + s.shape, s.dtype) for s in shards],
        in_specs=[any_spec] * n, out_specs=[any_spec] * n,
        scratch_shapes=[pltpu.SemaphoreType.DMA((N_PEERS * n,)), pltpu.SemaphoreType.DMA((N_PEERS * n,)),
                        pltpu.SemaphoreType.DMA((n,))],
        name=name)(*shards)


SIBLING = 1
OTHER_CHIPS = (2, 4, 6)
SAME_CORE = (0,) + OTHER_CHIPS


class Exchange:
    def __init__(self, inputs, out_shapes, aliases, copies, local=()):
        self.inputs, self.out_shapes, self.aliases = list(inputs), list(out_shapes), aliases
        self._copies, self._local = list(copies), list(local)
        self.scratch = [pltpu.SemaphoreType.DMA((len(self._copies),)), pltpu.SemaphoreType.DMA((len(self._copies),)),
                        pltpu.SemaphoreType.DMA((max(len(self._local), 1),))]

    def _build(self, ins, outs, sems):
        send_sems, recv_sems, local_sems = sems
        x, y, c = _me()
        me = 4 * x + 2 * y + c
        local = [functools.partial(pltpu.make_async_copy, src(ins, outs, me), dst(outs, me), local_sems.at[i])
                 for i, (src, dst) in enumerate(self._local)]
        sends, recvs = [], []
        for i, (mask, src, dst) in enumerate(self._copies):
            px, py, pc = x ^ ((mask >> 2) & 1), y ^ ((mask >> 1) & 1), c ^ (mask & 1)
            pair = dict(send_sem=send_sems.at[i], recv_sem=recv_sems.at[i], device_id_type=pl.DeviceIdType.MESH)
            sends.append(functools.partial(
                pltpu.make_async_remote_copy, src_ref=src(ins, outs, me), dst_ref=dst(outs, me), device_id=(px, py, pc), **pair))
            recvs.append(functools.partial(
                pltpu.make_async_remote_copy, src_ref=src(ins, outs, me), dst_ref=dst(outs, me ^ mask), device_id=(x, y, c), **pair))
        return local, sends, recvs

    def start(self, ins, outs, sems):
        local, sends, _ = self._build(ins, outs, sems)
        for make in local + sends:
            make().start()

    def drain(self, ins, outs, sems):
        local, sends, recvs = self._build(ins, outs, sems)
        for make in recvs:
            make().wait_recv()
        for make in sends:
            make().wait_send()
        for make in local:
            make().wait()


def _bind(fn, *args):
    return functools.partial(fn, *args)


def join_exchanges(a, b):
    if a is None or b is None:
        return a or b
    na_in, na_out = len(a.inputs), len(a.out_shapes)

    def src_a(fn):
        return lambda ins, outs, me: fn(ins[:na_in], outs[:na_out], me)

    def dst_a(fn):
        return lambda outs, who: fn(outs[:na_out], who)

    def src_b(fn):
        return lambda ins, outs, me: fn(ins[na_in:], outs[na_out:], me)

    def dst_b(fn):
        return lambda outs, who: fn(outs[na_out:], who)

    copies = [(m, src_a(s), dst_a(d)) for m, s, d in a._copies] + [(m, src_b(s), dst_b(d)) for m, s, d in b._copies]
    local = [(src_a(s), dst_a(d)) for s, d in a._local] + [(src_b(s), dst_b(d)) for s, d in b._local]
    aliases = dict(a.aliases)
    aliases.update({na_in + i: na_out + o for i, o in b.aliases.items()})
    return Exchange(a.inputs + b.inputs, a.out_shapes + b.out_shapes, aliases, copies, local)


def gather_over_ici(shards):
    copies = [(mask, _bind(lambda t, ins, outs, me: ins[t], t), _bind(lambda t, outs, sender: outs[t].at[sender], t))
              for t in range(len(shards)) for mask in OTHER_CHIPS]
    local = [(_bind(lambda t, ins, outs, me: ins[t], t), _bind(lambda t, outs, me: outs[t].at[me], t))
             for t in range(len(shards))]
    return Exchange(shards, [jax.ShapeDtypeStruct((N_DEV,) + s.shape, s.dtype) for s in shards], {}, copies, local)


def gather_over_d2d(gathered):
    copies = [(SIBLING, _bind(lambda t, m, ins, outs, me: outs[t].at[me ^ m], t, m),
               _bind(lambda t, m, outs, sender: outs[t].at[sender ^ m], t, m))
              for t in range(len(gathered)) for m in SAME_CORE]
    return Exchange(gathered, [jax.ShapeDtypeStruct(g.shape, g.dtype) for g in gathered],
                    {t: t for t in range(len(gathered))}, copies)


def scatter_over_d2d(blocks):
    copies = [(SIBLING, _bind(lambda t, m, ins, outs, me: ins[t].at[me ^ SIBLING ^ m], t, m),
               _bind(lambda t, i, outs, sender: outs[t].at[i], t, i))
              for t in range(len(blocks)) for i, m in enumerate(SAME_CORE)]
    return Exchange(blocks, [jax.ShapeDtypeStruct((len(SAME_CORE),) + b.shape[1:], b.dtype) for b in blocks], {}, copies)


def scatter_over_ici(pair_sums, bufs, layer):
    n = len(pair_sums)
    copies = [(m, _bind(lambda t, i, ins, outs, me: ins[t].at[i], t, i),
               _bind(lambda t, i, outs, sender: outs[t].at[layer, i], t, i))
              for t in range(n) for i, m in enumerate(SAME_CORE) if m]
    local = [(_bind(lambda t, ins, outs, me: ins[t].at[0], t), _bind(lambda t, outs, me: outs[t].at[layer, 0], t))
             for t in range(n)]
    return Exchange(list(pair_sums) + list(bufs), [jax.ShapeDtypeStruct(b.shape, b.dtype) for b in bufs],
                    {n + t: t for t in range(n)}, copies, local)


def run_exchange(ex, name):
    any_spec = pl.BlockSpec(memory_space=pl.ANY)
    n_in, n_out = len(ex.inputs), len(ex.out_shapes)

    def body(*refs):
        ins, outs, sems = refs[:n_in], refs[n_in:n_in + n_out], refs[n_in + n_out:]
        ex.start(ins, outs, sems)
        ex.drain(ins, outs, sems)

    return pl.pallas_call(
        body, out_shape=ex.out_shapes, in_specs=[any_spec] * n_in, out_specs=[any_spec] * n_out,
        input_output_aliases=ex.aliases, scratch_shapes=ex.scratch, name=name)(*ex.inputs)


MATRICES = ("w_in", "w_attn_out", "w_conv_out", "pool_w", "w_o", "w_ffn_in", "w_ffn_out")
TRANSPOSED = ("w_in", "w_ffn_in")
EVERY = tuple(range(len(MATRICES)))
IN_PROJ_PART, ATTN_PART, MIX_PART = (0,), (1, 2, 3, 4, 5), (6,)
LATE = (0,)
EARLY = EVERY[1:]
EARLY_FIRST, EARLY_SECOND = (4, 6), (1, 2, 3, 5)
SHARD_INFO = {
    "w_in": ((DEPTH, D_IN // N_DEV, D_MODEL), 1),
    "w_attn_out": ((DEPTH, D_ATTN, D_MODEL // N_DEV), 2),
    "w_conv_out": ((DEPTH, D_CONV, D_MODEL // N_DEV), 2),
    "pool_w": ((DEPTH, 4, 64, 256 // N_DEV), 3),
    "w_o": ((DEPTH, D_MODEL // N_DEV, D_MODEL), 1),
    "w_ffn_in": ((DEPTH, 2 * D_FF // N_DEV, D_MODEL), 1),
    "w_ffn_out": ((DEPTH, D_FF // N_DEV, D_MODEL), 1),
}


def _handled(name, t):
    return jnp.transpose(t, (0, 2, 1)) if name in TRANSPOSED else t
VECTORS = ("norm_mix_g", "forget_b", "q_norm_g", "k_norm_g", "pool_scale", "norm_ffn_g")
VECTOR_SHAPES = {"norm_mix_g": (DEPTH, D_MODEL), "forget_b": (DEPTH, HEADS), "q_norm_g": (DEPTH, HEAD_DIM),
                 "k_norm_g": (DEPTH, HEAD_DIM), "pool_scale": (DEPTH, D_MODEL), "norm_ffn_g": (DEPTH, D_MODEL)}
CONV_W_FULL = (DEPTH, 3, D_CONV)


def _size(shape):
    n = 1
    for v in shape:
        n *= v
    return n


def _pack(arrays, rows, cols):
    flat = jnp.concatenate([a.reshape(-1) for a in arrays])
    return jnp.pad(flat, (0, rows * cols - flat.shape[0])).reshape(rows, cols)


def _unpack(packed, shapes):
    flat, out, off = packed.reshape(-1), [], 0
    for shp in shapes:
        out.append(flat[off:off + _size(shp)].reshape(shp))
        off += _size(shp)
    return out


def _join_shards(stacked, axis):
    moved = jnp.moveaxis(stacked, 0, axis)
    shp = list(moved.shape)
    shp[axis:axis + 2] = [shp[axis] * shp[axis + 1]]
    return moved.reshape(shp)


def _cut_shards(full, axis):
    shp = list(full.shape)
    shp[axis:axis + 1] = [N_DEV, shp[axis] // N_DEV]
    return jnp.moveaxis(full.reshape(shp), axis, 0)


N_MOVED = 1544
SHARD_ROWS = D_IN // N_DEV


def _regroup_w_in(shards):
    wt = shards.reshape(D_IN, shards.shape[2])
    pad = jnp.zeros((N_FULL - D_IN, wt.shape[1]), wt.dtype)
    return jnp.concatenate([wt[N_MOVED:], wt[:N_MOVED], pad], axis=0)


def _ungroup_w_in(wpt):
    def kernel_rows(a, b):
        if b <= N_MOVED:
            return [wpt[a + D_IN - N_MOVED:b + D_IN - N_MOVED]]
        if a >= N_MOVED:
            return [wpt[a - N_MOVED:b - N_MOVED]]
        return kernel_rows(a, N_MOVED) + kernel_rows(N_MOVED, b)

    return jnp.stack([jnp.concatenate(kernel_rows(s * SHARD_ROWS, (s + 1) * SHARD_ROWS), axis=0) for s in range(N_DEV)])


def _pool_block_diag(w):
    out = jnp.zeros((D_POOL, D_MODEL), w.dtype)
    for g in range(4):
        out = lax.dynamic_update_slice(out, w[g], (g * 64, g * 256))
    return out


def _pool_from_block_diag(wbd):
    return jnp.stack([wbd[g * 64:(g + 1) * 64, g * 256:(g + 1) * 256] for g in range(4)])


def _layer_weights(mats, vec, conv_w, l):
    wp = _pool_block_diag(mats["pool_w"])
    row = lambda v: v.reshape(1, -1)
    fb = jnp.zeros((1, 128), F32).at[0, :HEADS].set(vec["forget_b"][l])
    cw = jnp.zeros((8, D_CONV), F32).at[:3].set(conv_w[l])
    twice = lambda v: jnp.tile(v.reshape(1, -1), (1, 2))
    return dict(
        wt_in=_regroup_w_in(mats["w_in"]), wt_ffn_in=mats["w_ffn_in"], w_ffn_out=mats["w_ffn_out"],
        wa=mats["w_attn_out"], wc=mats["w_conv_out"], wp=wp, wo=mats["w_o"],
        g_mix=row(vec["norm_mix_g"][l]), g_ffn=row(vec["norm_ffn_g"][l]), gq2=twice(vec["q_norm_g"][l]),
        gk2=twice(vec["k_norm_g"][l]), scale=row(vec["pool_scale"][l]), fb=fb, cw=cw)


def _layer_fwd(x, w, l, comm):
    (proj, h), half_a = norm_matmul(x, w["g_mix"], w["wt_in"], N_MAIN, f"in_proj_{l}", comm.gather_ici(l + 1, IN_PROJ_PART))
    qa, ka, va, vt, z = attn_prep(proj, h, w["wt_in"], w["fb"], w["gq2"], w["gk2"], f"attn_prep_{l}")
    (oa, lse), half_b = attn_forward(qa, ka, vt, f"attn_fwd_{l}", comm.gather_ici(l + 1, ATTN_PART))
    x1, half_c = mix_fwd(proj, oa, x, w["wa"], w["wc"], w["wp"], w["scale"], w["cw"], w["wo"], f"mix_fwd_{l}",
                         comm.gather_ici(l + 1, MIX_PART))
    half = list(half_a) + list(half_b) + list(half_c)
    (gu, h2), gathered = norm_matmul(x1, w["g_ffn"], w["wt_ffn_in"], 2 * D_FF, f"ffn_in_{l}", comm.gather_d2d(l + 1, half))
    x2 = swiglu_matmul(gu, w["w_ffn_out"], x1, f"ffn_out_{l}")
    saved = dict(x=x, proj=proj, h=h, z=z, qa=qa, ka=ka, va=va, oa=oa, lse=lse, x1=x1, gu=gu, h2=h2)
    return x2, saved, gathered


def _layer_bwd(dx2, sv, w, l, comm):
    g = {}
    (dgu, act), stage = swiglu_bwd(dx2, sv["gu"], w["w_ffn_out"], f"ffn_out_bwd_{l}", comm.scatter_d2d(l + 1))
    sums = comm.pair_sums(l + 1, stage)
    g["w_ffn_out"] = tn_matmul(act, dx2, f"dw_ffn_out_{l}")
    g["w_ffn_in"] = tn_matmul(dgu, sv["h2"], f"dw_ffn_in_{l}")
    (dx1, dg), _ = matmul_normbwd(dgu, w["wt_ffn_in"], sv["x1"], w["g_ffn"], dx2, f"ffn_in_bwd_{l}")
    g["norm_ffn_g"] = dg[0]

    (dproj, doa, a_tok, merged, dya, dyc, dyp, uc, dd, dscale, dcw) = mix_bwd(
        sv["proj"], sv["oa"], dx1, w["wa"], w["wc"], w["wp"], w["scale"], w["cw"], w["wo"], f"mix_bwd_{l}")
    g["w_o"] = tn_matmul(merged, dx1, f"dw_o_{l}")
    g["w_attn_out"], g["w_conv_out"], dwp = tn_matmuls([(a_tok, dya), (uc, dyc), (dd, dyp)], f"dw_branches_{l}")
    g["pool_w"] = _pool_from_block_diag(dwp)
    g["pool_scale"] = dscale[0]
    g["conv_w"] = dcw[:3]

    early = comm.early(l)
    comm.grads(l, g)
    above = comm.scatter_ici(l + 1, sums)
    (dqa, dka, dva), got = attn_backward(sv["qa"], sv["ka"], sv["va"], sv["oa"], doa, sv["lse"], f"attn_bwd_{l}",
                                         join_exchanges(above, comm.scatter_d2d(l, early) if early else None))
    n_above = len(above.out_shapes) if above else 0
    comm.scattered(got[:n_above])
    early_sums = dict(zip(early, comm.pair_sums(l, got[n_above:], early))) if early else {}
    early_ici = lambda which: comm.scatter_ici(l, [early_sums[t] for t in which], which) if early else None
    dproj, dgq, dgk, db = attn_post(dqa, dka, dva, sv["proj"], sv["z"], w["gq2"], w["gk2"], dproj, f"attn_post_{l}")
    g["q_norm_g"] = dgq[0, :HEAD_DIM] + dgq[0, HEAD_DIM:]
    g["k_norm_g"] = dgk[0, :HEAD_DIM] + dgk[0, HEAD_DIM:]
    g["forget_b"] = db[0, :HEADS]

    dw_in = tn_matmul(dproj, sv["h"], f"dw_in_{l}", m_cols=N_FULL, ex=early_ici(EARLY_FIRST))
    if early:
        dw_in, got = dw_in
        comm.scattered(got, EARLY_FIRST)
    g["w_in"] = _ungroup_w_in(dw_in)
    (dx, dg), got = matmul_normbwd(dproj, w["wt_in"], sv["x"], w["g_mix"], dx1, f"in_proj_bwd_{l}", k=N_FULL,
                                   ex=early_ici(EARLY_SECOND))
    comm.scattered(got, EARLY_SECOND if early else None)
    g["norm_mix_g"] = dg[0]
    comm.grads(l, g)
    return dx


def _local_step(x, tgt, comm):
    ws, saved = [], []
    w = comm.weights(0, None)
    for l in range(DEPTH):
        ws.append(w)
        x, sv, gathered = _layer_fwd(x, w, l, comm)
        saved.append(sv)
        if l + 1 < DEPTH:
            w = comm.weights(l + 1, gathered)
    sq, dx = loss_kernel(x, tgt, "loss")
    for l in reversed(range(DEPTH)):
        dx = _layer_bwd(dx, saved[l], ws[l], l, comm)
    comm.finish()
    return sq[0, 0], dx


def kernel(x, norm_mix_g, w_in, forget_b, q_norm_g, k_norm_g, w_attn_out, conv_w, w_conv_out, pool_w, pool_scale, w_o, norm_ffn_g, w_ffn_in, w_ffn_out, loss_target, m_norm_mix_g, m_w_in, m_forget_b, m_q_norm_g, m_k_norm_g, m_w_attn_out, m_conv_w, m_w_conv_out, m_pool_w, m_pool_scale, m_w_o, m_norm_ffn_g, m_w_ffn_in, m_w_ffn_out, v_norm_mix_g, v_w_in, v_forget_b, v_q_norm_g, v_k_norm_g, v_w_attn_out, v_conv_w, v_w_conv_out, v_pool_w, v_pool_scale, v_w_o, v_norm_ffn_g, v_w_ffn_in, v_w_ffn_out):
    w = dict(norm_mix_g=norm_mix_g, w_in=w_in, forget_b=forget_b, q_norm_g=q_norm_g, k_norm_g=k_norm_g,
             w_attn_out=w_attn_out, conv_w=conv_w, w_conv_out=w_conv_out, pool_w=pool_w, pool_scale=pool_scale,
             w_o=w_o, norm_ffn_g=norm_ffn_g, w_ffn_in=w_ffn_in, w_ffn_out=w_ffn_out)
    m = dict(norm_mix_g=m_norm_mix_g, w_in=m_w_in, forget_b=m_forget_b, q_norm_g=m_q_norm_g, k_norm_g=m_k_norm_g,
             w_attn_out=m_w_attn_out, conv_w=m_conv_w, w_conv_out=m_w_conv_out, pool_w=m_pool_w,
             pool_scale=m_pool_scale, w_o=m_w_o, norm_ffn_g=m_norm_ffn_g, w_ffn_in=m_w_ffn_in, w_ffn_out=m_w_ffn_out)
    v = dict(norm_mix_g=v_norm_mix_g, w_in=v_w_in, forget_b=v_forget_b, q_norm_g=v_q_norm_g, k_norm_g=v_k_norm_g,
             w_attn_out=v_w_attn_out, conv_w=v_conv_w, w_conv_out=v_w_conv_out, pool_w=v_pool_w,
             pool_scale=v_pool_scale, w_o=v_w_o, norm_ffn_g=v_norm_ffn_g, w_ffn_in=v_w_ffn_in, w_ffn_out=v_w_ffn_out)
    me = 4 * lax.axis_index("x") + 2 * lax.axis_index("y") + lax.axis_index("c")
    layer_shard = {n: SHARD_INFO[n][0][1:] for n in MATRICES}
    cut_axis = {n: SHARD_INFO[n][1] - 1 for n in MATRICES}

    vec = {n: w[n] for n in VECTORS}
    rc = {n: (_size(layer_shard[n][:-1]), layer_shard[n][-1]) for n in MATRICES}

    class Comm:
        bufs = [lax.empty((DEPTH, len(SAME_CORE)) + layer_shard[n], BF16) for n in MATRICES]
        blocks = [None] * DEPTH
        small_g = [None] * DEPTH
        conv_full = None

        @staticmethod
        def shards(l):
            return [_handled(n, w[n])[l].astype(BF16) for n in MATRICES]

        @staticmethod
        def gather_ici(l, part):
            return gather_over_ici([Comm.shards(l)[t] for t in part]) if l < DEPTH else None

        @staticmethod
        def gather_d2d(l, half):
            return gather_over_d2d(half) if l < DEPTH else None

        @staticmethod
        def weights(l, gathered):
            if l == 0:
                *gathered, conv_g = all_gather(Comm.shards(0) + [_pack([conv_w], 8, 128)], "gather_0")
                Comm.conv_full = _join_shards(jnp.stack([_unpack(conv_g[i], [conv_w.shape])[0] for i in range(N_DEV)]), 2)
            mats = {n: t if n == "w_in" else _join_shards(t, cut_axis[n]) for n, t in zip(MATRICES, gathered)}
            return _layer_weights(mats, vec, Comm.conv_full, l)

        @staticmethod
        def grads(l, g):
            Comm.small_g[l] = g
            Comm.blocks[l] = [None if n not in g else g[n] if n == "w_in" else _cut_shards(g[n], cut_axis[n])
                              for n in MATRICES]

        @staticmethod
        def early(l):
            return EARLY if l == 0 else None

        @staticmethod
        def scatter_d2d(l, which=EVERY):
            return scatter_over_d2d([Comm.blocks[l][t] for t in which]) if l < DEPTH else None

        @staticmethod
        def pair_sums(l, stage, which=EVERY):
            if l >= DEPTH:
                return None
            return [pair_sum(Comm.blocks[l][t].reshape((N_DEV,) + rc[MATRICES[t]]),
                             s.reshape((len(SAME_CORE),) + rc[MATRICES[t]]), me,
                             f"pair_sum_{MATRICES[t]}_{l}").reshape(s.shape) for t, s in zip(which, stage)]

        @staticmethod
        def scatter_ici(l, sums, which=EVERY):
            return scatter_over_ici(sums, [Comm.bufs[t] for t in which], l) if l < DEPTH else None

        @staticmethod
        def scattered(results, which=EVERY):
            for t, r in zip(which or (), results):
                Comm.bufs[t] = r

        @staticmethod
        def finish():
            stage = run_exchange(Comm.scatter_d2d(0, LATE), "scatter_d2d_0")
            Comm.scattered(run_exchange(Comm.scatter_ici(0, Comm.pair_sums(0, stage, LATE), LATE), "scatter_ici_0"), LATE)

    small_g, received = Comm.small_g, Comm
    sq, dx = _local_step(x[0], loss_target[0], Comm)

    big = {}
    for n, parts in zip(MATRICES, received.bufs):
        outs = adamw_sum(parts.reshape((DEPTH, len(SAME_CORE)) + rc[n]),
                         *[_handled(n, d[n]).reshape((DEPTH,) + rc[n]) for d in (w, m, v)], f"adamw_{n}")
        big[n] = [_handled(n, t.reshape((DEPTH,) + layer_shard[n])) for t in outs]

    small_shapes = [VECTOR_SHAPES[n] for n in VECTORS] + [CONV_W_FULL, (1,)]
    stacked = [jnp.stack([small_g[l][n] for l in range(DEPTH)]) for n in VECTORS + ("conv_w",)] + [sq.reshape(1)]
    sparts = all_gather([_pack(stacked, SMALL_ROWS, 128)], "gather_vector_grads")[0]
    col0 = me * (D_CONV // N_DEV)
    place = lambda t: lax.dynamic_update_slice(jnp.zeros(CONV_W_FULL, F32), t, (0, 0, col0))
    spacked = [_pack([d[n] for n in VECTORS] + [place(d["conv_w"]), jnp.zeros((1,), F32)], SMALL_ROWS, 128)[None]
               for d in (w, m, v)]
    small = [_unpack(t[0], small_shapes) for t in adamw_sum(sparts[None], *spacked, "adamw_vectors")]
    loss = (0.5 / D_MODEL) * small[0][-1][0]

    def result(kind):
        out = {n: big[n][kind] for n in MATRICES}
        out.update({n: small[kind][j] for j, n in enumerate(VECTORS)})
        out["conv_w"] = lax.dynamic_slice(small[kind][len(VECTORS)], (0, 0, col0), conv_w.shape)
        return [out[n] for n in w]

    return (loss, dx[None], *result(0), *result(1), *result(2), *result(3))
```

```python
import functools

import jax
import jax.numpy as jnp
from jax import lax
from jax.experimental import pallas as pl
from jax.experimental.pallas import tpu as pltpu

F32 = jnp.float32
BF16 = jnp.bfloat16

N_DEV = 8
DEPTH = 4
D_MODEL = 1024
HEAD_DIM = 64
HEADS = 8
D_ATTN = 512
D_CONV = 256
D_POOL = 256
D_FF = 2816
D_IN = 5640
EPS = 1e-6
ATTN_SCALE = HEAD_DIM ** -0.5

N_REST = 4096
N_MAIN = 5632
N_FULL = 5760
DPROJ_TAIL = 2048
DPROJ_COLS = N_REST + DPROJ_TAIL
FF_BLK = 256
N_FF_BLKS = D_FF // FF_BLK
HALO = 16

ADAM_LR = 0.001
ADAM_B1 = 0.9
ADAM_B2 = 0.999
ADAM_EPS = 1e-08
ADAM_WD = 0.01
ADAM_STEP = 10

SMALL_ROWS = 128

VMEM_LIMIT = 48 * 2 ** 20


def _cparams(sem, vmem=None):
    return pltpu.CompilerParams(dimension_semantics=sem, vmem_limit_bytes=vmem or VMEM_LIMIT)


def _pick(n, cands):
    for c in cands:
        if n % c == 0:
            return c
    raise ValueError(f"no tile for {n}")


def _tile(n, cap):
    t = min(cap, n)
    assert n % t == 0, (n, cap)
    return t


def _sigmoid(v):
    return 1.0 / (1.0 + jnp.exp(-v))


def _rstd(v):
    return lax.rsqrt(jnp.mean(v * v, axis=-1, keepdims=True) + EPS)


def _dot(a, b):
    return jnp.dot(a, b, preferred_element_type=F32)


def _dot_tn(a, b):
    return lax.dot_general(a, b, (((0,), (0,)), ((), ())), preferred_element_type=F32)


def _dot_nt(a, b):
    return lax.dot_general(a, b, (((1,), (1,)), ((), ())), preferred_element_type=F32)


def norm_matmul(x, g, wt, n_cols, name, ex=None):
    s, d = x.shape
    tm, tn = _tile(s, 1024), _pick(n_cols, (2816, 1408, 512))

    def body(x_ref, g_ref, w_ref, o_ref, h_ref):
        @pl.when(pl.program_id(1) == 0)
        def _():
            xv = x_ref[...]
            h_ref[...] = (xv * _rstd(xv) * g_ref[...]).astype(BF16)

        o_ref[...] = _dot_nt(h_ref[...], w_ref[...]).astype(BF16)

    return _carried_call(
        body, ex, (s // tm, n_cols // tn),
        [pl.BlockSpec((tm, d), lambda i, j: (i, 0)), pl.BlockSpec((1, d), lambda i, j: (0, 0)),
         pl.BlockSpec((tn, d), lambda i, j: (j, 0))],
        [pl.BlockSpec((tm, tn), lambda i, j: (i, j)), pl.BlockSpec((tm, d), lambda i, j: (i, 0))],
        [jax.ShapeDtypeStruct((s, n_cols), BF16), jax.ShapeDtypeStruct((s, d), BF16)], [],
        ("arbitrary", "arbitrary"), name, (x, g, wt))


def tn_matmul(a, b, name, m_cols=None, ex=None):
    t = a.shape[0]
    m = m_cols or a.shape[1]
    n = b.shape[1]
    tk = _tile(t, 1024)
    tmm = _pick(m, (1408, 1152, 1024, 512, 256))
    tn = _pick(n, (1408, 1152, 1024, 512, 128))
    nk = t // tk

    def body(a_ref, b_ref, o_ref, acc_ref):
        @pl.when(pl.program_id(2) == 0)
        def _():
            acc_ref[...] = jnp.zeros_like(acc_ref)

        acc_ref[...] += _dot_tn(a_ref[...].astype(BF16), b_ref[...].astype(BF16))

        @pl.when(pl.program_id(2) == nk - 1)
        def _():
            o_ref[...] = acc_ref[...].astype(BF16)

    if ex is None:
        return pl.pallas_call(
            body, grid=(m // tmm, n // tn, nk),
            in_specs=[pl.BlockSpec((tk, tmm), lambda i, j, k: (k, i)), pl.BlockSpec((tk, tn), lambda i, j, k: (k, j))],
            out_specs=pl.BlockSpec((tmm, tn), lambda i, j, k: (i, j)),
            out_shape=jax.ShapeDtypeStruct((m, n), BF16), scratch_shapes=[pltpu.VMEM((tmm, tn), F32)],
            compiler_params=_cparams(("parallel", "parallel", "arbitrary")), name=name)(a, b)
    (out,), carried = _carried_call(
        body, ex, (m // tmm, n // tn, nk),
        [pl.BlockSpec((tk, tmm), lambda i, j, k: (k, i)), pl.BlockSpec((tk, tn), lambda i, j, k: (k, j))],
        [pl.BlockSpec((tmm, tn), lambda i, j, k: (i, j))], [jax.ShapeDtypeStruct((m, n), BF16)],
        [pltpu.VMEM((tmm, tn), F32)], ("arbitrary", "arbitrary", "arbitrary"), name, (a, b))
    return out, carried


def tn_matmuls(pairs, name):
    t = pairs[0][0].shape[0]
    tk = _tile(t, 1024)
    nk = t // tk
    n = len(pairs)

    def body(*refs):
        ins, outs, accs = refs[:2 * n], refs[2 * n:3 * n], refs[3 * n:]

        @pl.when(pl.program_id(0) == 0)
        def _():
            for acc in accs:
                acc[...] = jnp.zeros_like(acc)

        for i in range(n):
            accs[i][...] += _dot_tn(ins[2 * i][...], ins[2 * i + 1][...])

        @pl.when(pl.program_id(0) == nk - 1)
        def _():
            for out, acc in zip(outs, accs):
                out[...] = acc[...].astype(BF16)

    shapes = [(a.shape[1], b.shape[1]) for a, b in pairs]
    return pl.pallas_call(
        body, grid=(nk,),
        in_specs=[pl.BlockSpec((tk, t_.shape[1]), lambda k: (k, 0)) for pair in pairs for t_ in pair],
        out_specs=[pl.BlockSpec(shp, lambda k: (0, 0)) for shp in shapes],
        out_shape=[jax.ShapeDtypeStruct(shp, BF16) for shp in shapes],
        scratch_shapes=[pltpu.VMEM(shp, F32) for shp in shapes],
        compiler_params=_cparams(("arbitrary",)), name=name)(*[t_ for pair in pairs for t_ in pair])


def matmul_normbwd(a, wt, x, g, dres, name, k=None, ex=None):
    s = a.shape[0]
    k = k or a.shape[1]
    d = wt.shape[1]
    tm = _tile(s, 1024)
    tk = _pick(k, (1408, 1152, 512))
    nk = k // tk

    def body(a_ref, w_ref, x_ref, g_ref, r_ref, dx_ref, dg_ref, acc_ref):
        i, kk = pl.program_id(0), pl.program_id(1)

        @pl.when(kk == 0)
        def _():
            acc_ref[...] = jnp.zeros_like(acc_ref)

        @pl.when((i == 0) & (kk == 0))
        def _():
            dg_ref[...] = jnp.zeros_like(dg_ref)

        acc_ref[...] += _dot(a_ref[...], w_ref[...])

        @pl.when(kk == nk - 1)
        def _():
            xv = x_ref[...]
            r = _rstd(xv)
            y = xv * r
            dh = acc_ref[...]
            dy = dh * g_ref[...]
            dx_ref[...] = r_ref[...] + r * (dy - y * jnp.mean(dy * y, axis=-1, keepdims=True))
            dg_ref[...] += jnp.sum(dh * y, axis=0, keepdims=True)

    return _carried_call(
        body, ex, (s // tm, nk),
        [pl.BlockSpec((tm, tk), lambda i, kk: (i, kk)), pl.BlockSpec((tk, d), lambda i, kk: (kk, 0)),
         pl.BlockSpec((tm, d), lambda i, kk: (i, 0)), pl.BlockSpec((1, d), lambda i, kk: (0, 0)),
         pl.BlockSpec((tm, d), lambda i, kk: (i, 0))],
        [pl.BlockSpec((tm, d), lambda i, kk: (i, 0)), pl.BlockSpec((1, d), lambda i, kk: (0, 0))],
        [jax.ShapeDtypeStruct((s, d), F32), jax.ShapeDtypeStruct((1, d), F32)],
        [pltpu.VMEM((tm, d), F32)], ("arbitrary", "arbitrary"), name, (a, wt, x, g, dres), vmem=56 * 2 ** 20)


def swiglu_matmul(gu, w, x1, name):
    s = gu.shape[0]
    d = w.shape[1]
    tm = _tile(s, 512)

    def body(gu_ref, w_ref, x_ref, o_ref):
        acc = x_ref[...]
        for j in range(N_FF_BLKS):
            gt = gu_ref[:, j * FF_BLK:(j + 1) * FF_BLK].astype(F32)
            up = gu_ref[:, D_FF + j * FF_BLK:D_FF + (j + 1) * FF_BLK].astype(F32)
            act = (gt * _sigmoid(gt) * up).astype(BF16)
            acc += _dot(act, w_ref[j * FF_BLK:(j + 1) * FF_BLK, :])
        o_ref[...] = acc

    return pl.pallas_call(
        body, grid=(s // tm,),
        in_specs=[pl.BlockSpec((tm, 2 * D_FF), lambda i: (i, 0)), pl.BlockSpec((D_FF, d), lambda i: (0, 0)),
                  pl.BlockSpec((tm, d), lambda i: (i, 0))],
        out_specs=pl.BlockSpec((tm, d), lambda i: (i, 0)),
        out_shape=jax.ShapeDtypeStruct((s, d), F32),
        compiler_params=_cparams(("parallel",)), name=name)(gu, w, x1)


def swiglu_bwd(dx2, gu, w, name, ex=None):
    s, d = dx2.shape
    tm = _tile(s, 512)

    def body(dx_ref, gu_ref, w_ref, dgu_ref, act_ref):
        dx = dx_ref[...].astype(BF16)
        for j in range(N_FF_BLKS):
            g_cols = slice(j * FF_BLK, (j + 1) * FF_BLK)
            u_cols = slice(D_FF + j * FF_BLK, D_FF + (j + 1) * FF_BLK)
            dact = _dot_nt(dx, w_ref[j * FF_BLK:(j + 1) * FF_BLK, :])
            gt = gu_ref[:, g_cols].astype(F32)
            up = gu_ref[:, u_cols].astype(F32)
            sg = _sigmoid(gt)
            silu = gt * sg
            act_ref[:, j * FF_BLK:(j + 1) * FF_BLK] = (silu * up).astype(BF16)
            dgu_ref[:, g_cols] = (dact * up * (sg + silu * (1.0 - sg))).astype(BF16)
            dgu_ref[:, u_cols] = (dact * silu).astype(BF16)

    return _carried_call(
        body, ex, (s // tm,),
        [pl.BlockSpec((tm, d), lambda i: (i, 0)), pl.BlockSpec((tm, 2 * D_FF), lambda i: (i, 0)),
         pl.BlockSpec((D_FF, d), lambda i: (0, 0), pipeline_mode=pl.Buffered(1))],
        [pl.BlockSpec((tm, 2 * D_FF), lambda i: (i, 0)), pl.BlockSpec((tm, D_FF), lambda i: (i, 0))],
        [jax.ShapeDtypeStruct((s, 2 * D_FF), BF16), jax.ShapeDtypeStruct((s, D_FF), BF16)], [],
        ("arbitrary",), name, (dx2, gu, w), vmem=56 * 2 ** 20)


def loss_kernel(y, tgt, name):
    s, d = y.shape
    tm = _tile(s, 512)

    def body(y_ref, t_ref, l_ref, dy_ref):
        @pl.when(pl.program_id(0) == 0)
        def _():
            l_ref[...] = jnp.zeros_like(l_ref)

        err = y_ref[...] - t_ref[...]
        dy_ref[...] = err * (1.0 / d)
        l_ref[...] += jnp.sum(jnp.sum(err * err, axis=1, keepdims=True), axis=0, keepdims=True)

    return pl.pallas_call(
        body, grid=(s // tm,),
        in_specs=[pl.BlockSpec((tm, d), lambda i: (i, 0)), pl.BlockSpec((tm, d), lambda i: (i, 0))],
        out_specs=[pl.BlockSpec((8, 128), lambda i: (0, 0)), pl.BlockSpec((tm, d), lambda i: (i, 0))],
        out_shape=[jax.ShapeDtypeStruct((8, 128), F32), jax.ShapeDtypeStruct((s, d), F32)],
        compiler_params=_cparams(("arbitrary",)), name=name)(y, tgt)


def _split3(v):
    a1 = v.astype(BF16)
    r1 = v - a1.astype(F32)
    a2 = r1.astype(BF16)
    a3 = (r1 - a2.astype(F32)).astype(BF16)
    return a1, a2, a3


def _running_sum(v, carry_ref, reverse):
    tm = v.shape[0]
    row = lax.broadcasted_iota(jnp.int32, (tm, tm), 0)
    col = lax.broadcasted_iota(jnp.int32, (tm, tm), 1)
    tri = ((col >= row) if reverse else (row >= col)).astype(BF16)
    a1, a2, a3 = _split3(v)
    out = _dot(tri, a1) + _dot(tri, a2) + _dot(tri, a3) + carry_ref[...]
    carry_ref[...] = out[0:1, :] if reverse else out[tm - 1:tm, :]
    return out


HEAD_GROUP_FWD = 8
HEAD_GROUP_BWD = 8
LANE_C = 64
LANE_ONE = 67


def _lanes():
    lane = lax.broadcasted_iota(jnp.int32, (1, 128), 1)
    return lane, lane < HEAD_DIM


def _half_mean(t, lo):
    s_lo = jnp.sum(jnp.where(lo, t, 0.0), axis=-1, keepdims=True)
    s_hi = jnp.sum(jnp.where(lo, 0.0, t), axis=-1, keepdims=True)
    return jnp.where(lo, s_lo, s_hi) * (1.0 / HEAD_DIM)


def _lane_col(t, lane, idx):
    return jnp.sum(jnp.where(lane == idx, t, 0.0), axis=-1, keepdims=True)


def _swap_halves(t):
    return pltpu.roll(t, HEAD_DIM, 1)


def attn_prep(proj, h, wt_in, fb, gq2, gk2, name):
    s, d = h.shape
    tm = _tile(s, 512)
    first = N_REST // D_ATTN

    def body(q_ref, k_ref, v_ref, h_ref, wf_ref, fb_ref, gq_ref, gk_ref, qa_ref, ka_ref, va_ref, vt_ref, z_ref, carry_ref):
        lane, lo = _lanes()

        @pl.when(pl.program_id(0) == 0)
        def _():
            carry_ref[...] = jnp.zeros_like(carry_ref)

        z = _dot_nt(h_ref[...], wf_ref[...]) + fb_ref[...]
        z_ref[...] = z
        cv = _running_sum(jnp.minimum(z, 0.0) - jnp.log(1.0 + jnp.exp(-jnp.abs(z))), carry_ref, reverse=False)

        def normed(t, g):
            t = t.astype(F32)
            return t * lax.rsqrt(_half_mean(t * t, lo) + EPS) * g

        one_q = jnp.where((lane >= LANE_ONE) & (lane < LANE_ONE + 3), 1.0, 0.0)
        one_k = jnp.where((lane >= LANE_C) & (lane < LANE_C + 3), 1.0, 0.0)
        one_v = jnp.where(lane == LANE_C, 1.0, 0.0)
        for j in range(HEADS // 2):
            cols = slice(128 * j, 128 * (j + 1))
            qn = normed(q_ref[:, cols], gq_ref[...] * ATTN_SCALE)
            kn = normed(k_ref[:, cols], gk_ref[...])
            vv = v_ref[:, cols].astype(F32)
            for e in range(2):
                h = 2 * j + e
                pick = (lambda t: t) if e == 0 else _swap_halves
                pieces = [p.astype(F32) for p in _split3(_lane_col(cv, lane, h))]
                ext_q, ext_k = one_q, one_k
                for i, p in enumerate(pieces):
                    ext_q = jnp.where(lane == LANE_C + i, p, ext_q)
                    ext_k = jnp.where(lane == LANE_ONE + i, -p, ext_k)
                qa_ref[h] = jnp.where(lo, pick(qn), ext_q).astype(BF16)
                ka_ref[h] = jnp.where(lo, pick(kn), ext_k).astype(BF16)
                va = jnp.where(lo, pick(vv), one_v)
                va_ref[h] = va.astype(BF16)
                vt_ref[h] = va.T.astype(BF16)

    tile = lambda blk: pl.BlockSpec((tm, D_ATTN), lambda i: (i, blk))
    vec = pl.BlockSpec((1, 128), lambda i: (0, 0))
    out = pl.BlockSpec((HEADS, tm, 128), lambda i: (0, i, 0))
    return pl.pallas_call(
        body, grid=(s // tm,),
        in_specs=[tile(first), tile(first + 1), tile(first + 2), pl.BlockSpec((tm, d), lambda i: (i, 0)),
                  pl.BlockSpec((128, d), lambda i: (N_MAIN // 128, 0)), vec, vec, vec],
        out_specs=[out, out, out, pl.BlockSpec((HEADS, 128, tm), lambda i: (0, 0, i)),
                   pl.BlockSpec((tm, 128), lambda i: (i, 0))],
        out_shape=[jax.ShapeDtypeStruct((HEADS, s, 128), BF16)] * 3 + [jax.ShapeDtypeStruct((HEADS, 128, s), BF16),
                                                                       jax.ShapeDtypeStruct((s, 128), F32)],
        scratch_shapes=[pltpu.VMEM((1, 128), F32)],
        compiler_params=_cparams(("arbitrary",)), name=name)(proj, proj, proj, h, wt_in, fb, gq2, gk2)


def _carry(ex, n_in, n_out, n_scratch, grid):
    n_xin, n_xout = (len(ex.inputs), len(ex.out_shapes)) if ex else (0, 0)

    def split(refs):
        ins, xins = refs[:n_in], refs[n_in:n_in + n_xin]
        rest = refs[n_in + n_xin:]
        outs, xouts = rest[:n_out], rest[n_out:n_out + n_xout]
        rest = rest[n_out + n_xout:]
        return ins + outs + rest[:n_scratch], (xins, xouts, rest[n_scratch:])

    def first():
        return functools.reduce(lambda a, b: a & b, [pl.program_id(d) == 0 for d in range(len(grid))])

    def last():
        return functools.reduce(lambda a, b: a & b, [pl.program_id(d) == grid[d] - 1 for d in range(len(grid))])

    return split, first, last


def _carried_call(body, ex, grid, in_specs, out_specs, out_shape, scratch, sem, name, operands, vmem=None):
    any_spec = pl.BlockSpec(memory_space=pl.ANY)
    split, first, last = _carry(ex, len(in_specs), len(out_specs), len(scratch), grid)

    def carried(*refs):
        own, xrefs = split(refs)
        if ex:
            @pl.when(first())
            def _():
                ex.start(*xrefs)

        body(*own)
        if ex:
            @pl.when(last())
            def _():
                ex.drain(*xrefs)

    n_xin = len(ex.inputs) if ex else 0
    results = pl.pallas_call(
        carried, grid=grid, in_specs=list(in_specs) + [any_spec] * n_xin,
        out_specs=list(out_specs) + [any_spec] * (len(ex.out_shapes) if ex else 0),
        out_shape=list(out_shape) + (list(ex.out_shapes) if ex else []),
        input_output_aliases={len(in_specs) + i: len(out_specs) + o for i, o in ex.aliases.items()} if ex else {},
        scratch_shapes=list(scratch) + (ex.scratch if ex else []),
        compiler_params=_cparams(sem, vmem), name=name)(*operands, *(ex.inputs if ex else []))
    return results[:len(out_specs)], results[len(out_specs):]


def _tri_rows(t, n):
    qi = sum(jnp.where(t >= r * (r + 1) // 2, 1, 0) for r in range(1, n))
    return qi, t - qi * (qi + 1) // 2


def _tri_cols(t, n):
    ki = sum(jnp.where(t >= r * n - r * (r - 1) // 2, 1, 0) for r in range(1, n))
    return ki, ki + t - (ki * n - ki * (ki - 1) // 2)


def _causal_t(st_blk, tk, tq):
    key = lax.broadcasted_iota(jnp.int32, (tk, tq), 0)
    qry = lax.broadcasted_iota(jnp.int32, (tk, tq), 1)
    return jnp.where(qry >= key, st_blk, -jnp.inf)


def attn_forward(qa, ka, vt, name, ex=None):
    hh, s, _ = qa.shape
    tq = tk = _tile(s, 512)
    nq = s // tq
    grp = HEAD_GROUP_FWD

    def body(q_ref, k_ref, vt_ref, o_ref, lse_ref, m_ref, acc_ref):
        qi, ki = _tri_rows(pl.program_id(1), nq)

        @pl.when(ki == 0)
        def _():
            m_ref[...] = jnp.full_like(m_ref, -jnp.inf)
            acc_ref[...] = jnp.zeros_like(acc_ref)

        def step(masked):
            nxt = _dot_nt(k_ref[0], q_ref[0])
            for g in range(grp):
                st = nxt
                if g + 1 < grp:
                    nxt = _dot_nt(k_ref[g + 1], q_ref[g + 1])
                if masked:
                    st = _causal_t(st, tk, tq)
                m_old = m_ref[g]
                m_new = jnp.maximum(m_old, jnp.max(st, axis=0, keepdims=True))
                pt = jnp.exp(st - m_new).astype(BF16)
                acc_ref[g] = jnp.exp(m_old - m_new) * acc_ref[g] + _dot(vt_ref[g], pt)
                m_ref[g] = m_new

        @pl.when(ki < qi)
        def _():
            step(False)

        @pl.when(ki == qi)
        def _():
            step(True)
            for g in range(grp):
                acc = acc_ref[g]
                denom = acc[LANE_C:LANE_C + 1, :]
                o_ref[g] = (acc / denom).T.astype(BF16)
                lse_ref[g] = m_ref[g] + jnp.log(denom)

    qspec = pl.BlockSpec((grp, tq, 128), lambda h, t: (h, _tri_rows(t, nq)[0], 0))
    kspec = pl.BlockSpec((grp, tk, 128), lambda h, t: (h, _tri_rows(t, nq)[1], 0))
    vspec = pl.BlockSpec((grp, 128, tk), lambda h, t: (h, 0, _tri_rows(t, nq)[1]))
    lspec = pl.BlockSpec((grp, 1, tq), lambda h, t: (h, 0, _tri_rows(t, nq)[0]))
    return _carried_call(
        body, ex, (hh // grp, nq * (nq + 1) // 2), [qspec, kspec, vspec], [qspec, lspec],
        [jax.ShapeDtypeStruct((hh, s, 128), BF16), jax.ShapeDtypeStruct((hh, 1, s), F32)],
        [pltpu.VMEM((grp, 1, tq), F32), pltpu.VMEM((grp, 128, tq), F32)],
        ("arbitrary", "arbitrary"), name, (qa, ka, vt))


def attn_backward(qa, ka, va, oa, doa, lse, name, ex=None):
    hh, s, _ = qa.shape
    tq = tk = _tile(s, 512)
    nq = s // tq
    grp = HEAD_GROUP_BWD

    def body(q_ref, k_ref, v_ref, o_ref, do_ref, lse_ref, dq_ref, dk_ref, dv_ref, dka_ref, dva_ref):
        ki, qi = _tri_cols(pl.program_id(1), nq)

        @pl.when(pl.program_id(1) == 0)
        def _():
            dq_ref[...] = jnp.zeros_like(dq_ref)

        @pl.when(qi == ki)
        def _():
            dka_ref[...] = jnp.zeros_like(dka_ref)
            dva_ref[...] = jnp.zeros_like(dva_ref)

        def step(masked):
            rows = pl.ds(pl.multiple_of(qi * tq, tq), tq)
            products = lambda g: (_dot_nt(k_ref[g], q_ref[g]), _dot_nt(v_ref[g], do_ref[g]))
            nxt = products(0)
            for g in range(grp):
                st, dpt = nxt
                if g + 1 < grp:
                    nxt = products(g + 1)
                q, k, do = q_ref[g], k_ref[g], do_ref[g]
                if masked:
                    st = _causal_t(st, tk, tq)
                pt = jnp.exp(st - lse_ref[g])
                delta = jnp.sum((do.astype(F32) * o_ref[g].astype(F32)).T, axis=0, keepdims=True)
                dst = (pt * (dpt - delta)).astype(BF16)
                dva_ref[g] += _dot(pt.astype(BF16), do)
                dka_ref[g] += _dot(dst, q)
                dq_ref[g, rows, :] += _dot_tn(dst, k)

        def diagonal_step():
            hk, hq = tk // 2, tq // 2
            base = pl.multiple_of(qi * tq, tq)
            for g in range(grp):
                q, do = q_ref[g], do_ref[g]
                q_b, do_b = q_ref[g, hq:, :], do_ref[g, hq:, :]
                k_a, k_b = k_ref[g, :hk, :], k_ref[g, hk:, :]
                lse = lse_ref[g]
                delta = jnp.sum((do.astype(F32) * o_ref[g].astype(F32)).T, axis=0, keepdims=True)
                pt_a = jnp.exp(_causal_t(_dot_nt(k_a, q), hk, tq) - lse)
                pt_b = jnp.exp(_causal_t(_dot_nt(k_b, q_b), hk, tq - hq) - lse[:, hq:])
                dst_a = (pt_a * (_dot_nt(v_ref[g, :hk, :], do) - delta)).astype(BF16)
                dst_b = (pt_b * (_dot_nt(v_ref[g, hk:, :], do_b) - delta[:, hq:])).astype(BF16)
                dva_ref[g, :hk, :] += _dot(pt_a.astype(BF16), do)
                dva_ref[g, hk:, :] += _dot(pt_b.astype(BF16), do_b)
                dka_ref[g, :hk, :] += _dot(dst_a, q)
                dka_ref[g, hk:, :] += _dot(dst_b, q_b)
                dq_ref[g, pl.ds(base, tq), :] += _dot_tn(dst_a, k_a)
                dq_ref[g, pl.ds(pl.multiple_of(base + hq, hq), tq - hq), :] += _dot_tn(dst_b, k_b)

        @pl.when(qi > ki)
        def _():
            step(False)

        @pl.when(qi == ki)
        def _():
            diagonal_step()

        @pl.when(qi == nq - 1)
        def _():
            dk_ref[...] = dka_ref[...]
            dv_ref[...] = dva_ref[...].astype(BF16)

    qspec = pl.BlockSpec((grp, tq, 128), lambda h, t: (h, _tri_cols(t, nq)[1], 0))
    lspec = pl.BlockSpec((grp, 1, tq), lambda h, t: (h, 0, _tri_cols(t, nq)[1]))
    kspec = pl.BlockSpec((grp, tk, 128), lambda h, t: (h, _tri_cols(t, nq)[0], 0))
    return _carried_call(
        body, ex, (hh // grp, nq * (nq + 1) // 2), [qspec, kspec, kspec, qspec, qspec, lspec],
        [pl.BlockSpec((grp, s, 128), lambda h, t: (h, 0, 0), pipeline_mode=pl.Buffered(1)), kspec, kspec],
        [jax.ShapeDtypeStruct((hh, s, 128), F32), jax.ShapeDtypeStruct((hh, s, 128), F32),
         jax.ShapeDtypeStruct((hh, s, 128), BF16)],
        [pltpu.VMEM((grp, tk, 128), F32), pltpu.VMEM((grp, tk, 128), F32)],
        ("arbitrary", "arbitrary"), name, (qa, ka, va, oa, doa, lse), vmem=58 * 2 ** 20)


def attn_post(dqa, dka, dva, proj, z, gq2, gk2, dproj, name):
    s = proj.shape[0]
    tm = _tile(s, 512)
    nt = s // tm

    def body(dq_ref, dk_ref, dv_ref, q_ref, k_ref, z_ref, gq_ref, gk_ref, dp_any, dp_ref, dgq_ref, dgk_ref, db_ref,
             carry_ref):
        lane, lo = _lanes()

        @pl.when(pl.program_id(0) == 0)
        def _():
            dgq_ref[...] = jnp.zeros_like(dgq_ref)
            dgk_ref[...] = jnp.zeros_like(dgk_ref)
            db_ref[...] = jnp.zeros_like(db_ref)
            carry_ref[...] = jnp.zeros_like(carry_ref)

        def pair(ref, j):
            return jnp.where(lo, ref[2 * j].astype(F32), _swap_halves(ref[2 * j + 1].astype(F32)))

        def norm_bwd(raw, g, dhat, scale):
            r = lax.rsqrt(_half_mean(raw * raw, lo) + EPS)
            y = raw * r
            dy = dhat * (g * scale)
            return r * (dy - y * _half_mean(dy * y, lo)), jnp.sum(dhat * y, axis=0, keepdims=True) * scale

        dc = jnp.zeros((tm, 128), F32)
        for j in range(HEADS // 2):
            cols = slice(128 * j, 128 * (j + 1))
            dq, dgq = norm_bwd(q_ref[:, cols].astype(F32), gq_ref[...], pair(dq_ref, j), ATTN_SCALE)
            dk, dgk = norm_bwd(k_ref[:, cols].astype(F32), gk_ref[...], pair(dk_ref, j), 1.0)
            dgq_ref[...] += dgq
            dgk_ref[...] += dgk
            dp_ref[:, cols] = dq.astype(BF16)
            dp_ref[:, D_ATTN + 128 * j:D_ATTN + 128 * (j + 1)] = dk.astype(BF16)
            dp_ref[:, 2 * D_ATTN + 128 * j:2 * D_ATTN + 128 * (j + 1)] = pair(dv_ref, j).astype(BF16)
            for e in range(2):
                h = 2 * j + e
                both = jnp.where(lane == LANE_C, dq_ref[h], 0.0) - jnp.where(lane == LANE_ONE, dk_ref[h], 0.0)
                dc = jnp.where(lane == h, jnp.sum(both, axis=-1, keepdims=True), dc)
        dz = _running_sum(dc, carry_ref, reverse=True) * (1.0 - _sigmoid(z_ref[...]))
        db_ref[...] += jnp.sum(dz, axis=0, keepdims=True)
        dp_ref[:, 3 * D_ATTN:3 * D_ATTN + 128] = dz.astype(BF16)
        dp_ref[:, 3 * D_ATTN + 128:] = jnp.zeros((tm, DPROJ_TAIL - 3 * D_ATTN - 128), BF16)

    heads = lambda: pl.BlockSpec((HEADS, tm, 128), lambda i: (0, nt - 1 - i, 0))
    vec = pl.BlockSpec((1, 128), lambda i: (0, 0))
    first = N_REST // D_ATTN
    return pl.pallas_call(
        body, grid=(nt,),
        in_specs=[heads(), heads(), heads(), pl.BlockSpec((tm, D_ATTN), lambda i: (nt - 1 - i, first)),
                  pl.BlockSpec((tm, D_ATTN), lambda i: (nt - 1 - i, first + 1)),
                  pl.BlockSpec((tm, 128), lambda i: (nt - 1 - i, 0)), vec, vec, pl.BlockSpec(memory_space=pl.ANY)],
        out_specs=[pl.BlockSpec((tm, DPROJ_TAIL), lambda i: (nt - 1 - i, N_REST // DPROJ_TAIL)), vec, vec, vec],
        out_shape=[jax.ShapeDtypeStruct(dproj.shape, BF16), jax.ShapeDtypeStruct((1, 128), F32),
                   jax.ShapeDtypeStruct((1, 128), F32), jax.ShapeDtypeStruct((1, 128), F32)],
        scratch_shapes=[pltpu.VMEM((1, 128), F32)], input_output_aliases={8: 0},
        compiler_params=_cparams(("arbitrary",)), name=name)(dqa, dka, dva, proj, proj, z, gq2, gk2, dproj)


def _pool_groups(tm):
    gid = lax.broadcasted_iota(jnp.int32, (1, D_POOL), 1) // (D_POOL // 4)
    win = jnp.where(gid == 0, 2.0, jnp.where(gid == 1, 4.0, jnp.where(gid == 2, 8.0, 16.0)))
    return gid, win


def _by_group(gid, v2, v4, v8, v16):
    return jnp.where(gid == 0, v2, jnp.where(gid == 1, v4, jnp.where(gid == 2, v8, v16)))


def _branches(rest_ref, halo_ref, a_ref, wa_ref, wc_ref, wp_ref, sc_ref, cw_ref, ti, tm):
    f = lambda v: v.astype(F32)
    cx, cb, cc, px = f(rest_ref[:, 0:256]), f(rest_ref[:, 256:512]), f(rest_ref[:, 512:768]), f(rest_ref[:, 768:1024])
    live = jnp.where(ti > 0, 1.0, 0.0)
    hz = f(halo_ref[:, 0:256]) * f(halo_ref[:, 512:768]) * live
    hp = f(halo_ref[:, 768:1024]) * live
    z = cc * cx
    zf = jnp.concatenate([hz, z], axis=0)
    z1 = pltpu.roll(zf, 1, 0)[HALO:]
    z2 = pltpu.roll(zf, 2, 0)[HALO:]
    cw = cw_ref[...]
    conv = cw[2:3] * z + cw[1:2] * z1 + cw[0:1] * z2
    uc = cb * conv
    pf = jnp.concatenate([hp, px], axis=0)
    s2 = pf + pltpu.roll(pf, 1, 0)
    s4 = s2 + pltpu.roll(s2, 2, 0)
    s8 = s4 + pltpu.roll(s4, 4, 0)
    s16 = s8 + pltpu.roll(s8, 8, 0)
    gid, win = _pool_groups(tm)
    t = (ti * tm + lax.broadcasted_iota(jnp.int32, (tm, 1), 0)).astype(F32)
    inv = 1.0 / jnp.minimum(t + 1.0, win)
    dpool = _by_group(gid, s2[HALO:], s4[HALO:], s8[HALO:], s16[HALO:]) * inv - px
    _, lo = _lanes()
    a_tok = [jnp.where(lo, f(a_ref[2 * j]), _swap_halves(f(a_ref[2 * j + 1]))).astype(BF16) for j in range(HEADS // 2)]
    y_attn = _dot(a_tok[0], wa_ref[0:128, :])
    for j in range(1, HEADS // 2):
        y_attn += _dot(a_tok[j], wa_ref[128 * j:128 * (j + 1), :])
    y_conv = _dot(uc.astype(BF16), wc_ref[...])
    y_pool_raw = _dot(dpool.astype(BF16), wp_ref[...])
    sg = [_sigmoid(f(rest_ref[:, 1024 + i * D_MODEL:1024 + (i + 1) * D_MODEL])) for i in range(3)]
    return dict(cx=cx, cb=cb, cc=cc, z=z, z1=z1, z2=z2, conv=conv, uc=uc, dpool=dpool, inv=inv, gid=gid, a_tok=a_tok,
                y_attn=y_attn, y_conv=y_conv, y_pool_raw=y_pool_raw, sg=sg, cw=cw)


def _mix_specs(tm, ti_of):
    blocks_per_tile = tm // HALO
    return [
        pl.BlockSpec((tm, N_REST), lambda i: (ti_of(i), 0)),
        pl.BlockSpec((HALO, 1024), lambda i: (jnp.maximum(ti_of(i) * blocks_per_tile - 1, 0), 0)),
        pl.BlockSpec((HEADS, tm, 128), lambda i: (0, ti_of(i), 0)),
        pl.BlockSpec((D_ATTN, D_MODEL), lambda i: (0, 0), pipeline_mode=pl.Buffered(1)),
        pl.BlockSpec((D_CONV, D_MODEL), lambda i: (0, 0), pipeline_mode=pl.Buffered(1)),
        pl.BlockSpec((D_POOL, D_MODEL), lambda i: (0, 0), pipeline_mode=pl.Buffered(1)),
        pl.BlockSpec((1, D_MODEL), lambda i: (0, 0)),
        pl.BlockSpec((8, D_CONV), lambda i: (0, 0)),
    ]


def mix_fwd(proj, a, x, wa, wc, wp, scale, cw, wo, name, ex=None):
    s = x.shape[0]
    tm = _tile(s, 512)

    def body(rest_ref, halo_ref, a_ref, wa_ref, wc_ref, wp_ref, sc_ref, cw_ref, wo_ref, x_ref, o_ref):
        b = _branches(rest_ref, halo_ref, a_ref, wa_ref, wc_ref, wp_ref, sc_ref, cw_ref, pl.program_id(0), tm)
        merged = b["sg"][0] * b["y_attn"] + b["sg"][1] * b["y_conv"] + b["sg"][2] * (b["y_pool_raw"] * sc_ref[...])
        o_ref[...] = x_ref[...] + _dot(merged.astype(BF16), wo_ref[...])

    (x1,), carried = _carried_call(
        body, ex, (s // tm,),
        _mix_specs(tm, lambda i: i) + [pl.BlockSpec((D_MODEL, D_MODEL), lambda i: (0, 0), pipeline_mode=pl.Buffered(1)),
                                       pl.BlockSpec((tm, D_MODEL), lambda i: (i, 0))],
        [pl.BlockSpec((tm, D_MODEL), lambda i: (i, 0))], [jax.ShapeDtypeStruct((s, D_MODEL), F32)], [],
        ("arbitrary",), name, (proj, proj, a, wa, wc, wp, scale, cw, wo, x), vmem=58 * 2 ** 20)
    return x1, carried


def mix_bwd(proj, a, dx1, wa, wc, wp, scale, cw, wo, name):
    s = dx1.shape[0]
    tm = _tile(s, 512)
    nt = s // tm
    ti_of = lambda i: nt - 1 - i
    n = tm + HALO

    def body(rest_ref, halo_ref, a_ref, wa_ref, wc_ref, wp_ref, sc_ref, cw_ref, wo_ref,
             dx_ref, dp_ref, da_ref, at_ref, mg_ref, dya_ref, dyc_ref, dyp_ref, uc_ref, dd_ref, dsc_ref, dcw_ref,
             cdc_ref, cde_ref):
        i = pl.program_id(0)
        ti = ti_of(i)

        @pl.when(i == 0)
        def _():
            cdc_ref[...] = jnp.zeros_like(cdc_ref)
            cde_ref[...] = jnp.zeros_like(cde_ref)
            dsc_ref[...] = jnp.zeros_like(dsc_ref)
            dcw_ref[...] = jnp.zeros_like(dcw_ref)

        b = _branches(rest_ref, halo_ref, a_ref, wa_ref, wc_ref, wp_ref, sc_ref, cw_ref, ti, tm)
        sg, sc = b["sg"], sc_ref[...]
        y_pool = b["y_pool_raw"] * sc
        merged = sg[0] * b["y_attn"] + sg[1] * b["y_conv"] + sg[2] * y_pool
        mg_ref[...] = merged.astype(BF16)
        dm = _dot_nt(dx_ref[...].astype(BF16), wo_ref[...])
        dys = [dm * sg[j] for j in range(3)]
        for j, y in enumerate((b["y_attn"], b["y_conv"], y_pool)):
            dp_ref[:, 1024 + j * D_MODEL:1024 + (j + 1) * D_MODEL] = (dys[j] * y * (1.0 - sg[j])).astype(BF16)
        dya = dys[0].astype(BF16)
        dya_ref[...] = dya
        _, lo = _lanes()
        for j in range(HEADS // 2):
            at_ref[:, 128 * j:128 * (j + 1)] = b["a_tok"][j]
            da = _dot_nt(dya, wa_ref[128 * j:128 * (j + 1), :])
            da_ref[2 * j] = jnp.where(lo, da, 0.0).astype(BF16)
            da_ref[2 * j + 1] = jnp.where(lo, _swap_halves(da), 0.0).astype(BF16)
        dyc = dys[1].astype(BF16)
        dyc_ref[...] = dyc
        duc = _dot_nt(dyc, wc_ref[...])
        dyp = dys[2]
        dsc_ref[...] += jnp.sum(dyp * b["y_pool_raw"], axis=0, keepdims=True)
        dypr = (dyp * sc).astype(BF16)
        dyp_ref[...] = dypr
        ddp = _dot_nt(dypr, wp_ref[...])
        uc_ref[...] = b["uc"].astype(BF16)
        dd_ref[...] = b["dpool"].astype(BF16)

        dconv = duc * b["cb"]
        dp_ref[:, 256:512] = (duc * b["conv"]).astype(BF16)
        dcf = jnp.concatenate([dconv, cdc_ref[...]], axis=0)
        cw = b["cw"]
        dz = cw[2:3] * dconv + cw[1:2] * pltpu.roll(dcf, n - 1, 0)[:tm] + cw[0:1] * pltpu.roll(dcf, n - 2, 0)[:tm]
        dp_ref[:, 0:256] = (dz * b["cc"]).astype(BF16)
        dp_ref[:, 512:768] = (dz * b["cx"]).astype(BF16)
        dcw_ref[0:1, :] += jnp.sum(dconv * b["z2"], axis=0, keepdims=True)
        dcw_ref[1:2, :] += jnp.sum(dconv * b["z1"], axis=0, keepdims=True)
        dcw_ref[2:3, :] += jnp.sum(dconv * b["z"], axis=0, keepdims=True)
        cdc_ref[...] = dconv[:HALO]

        e = ddp * b["inv"]
        ef = jnp.concatenate([e, cde_ref[...]], axis=0)
        r2 = ef + pltpu.roll(ef, n - 1, 0)
        r4 = r2 + pltpu.roll(r2, n - 2, 0)
        r8 = r4 + pltpu.roll(r4, n - 4, 0)
        r16 = r8 + pltpu.roll(r8, n - 8, 0)
        dp_ref[:, 768:1024] = (_by_group(b["gid"], r2[:tm], r4[:tm], r8[:tm], r16[:tm]) - ddp).astype(BF16)
        cde_ref[...] = e[:HALO]

    tile = lambda w: pl.BlockSpec((tm, w), lambda i: (ti_of(i), 0))
    whole = lambda r, c: pl.BlockSpec((r, c), lambda i: (0, 0))
    bf = lambda w: jax.ShapeDtypeStruct((s, w), BF16)
    return pl.pallas_call(
        body, grid=(nt,),
        in_specs=_mix_specs(tm, ti_of) + [pl.BlockSpec((D_MODEL, D_MODEL), lambda i: (0, 0), pipeline_mode=pl.Buffered(1)),
                                          tile(D_MODEL)],
        out_specs=[tile(N_REST), pl.BlockSpec((HEADS, tm, 128), lambda i: (0, ti_of(i), 0)), tile(D_ATTN),
                   tile(D_MODEL), tile(D_MODEL), tile(D_MODEL), tile(D_MODEL),
                   tile(D_CONV), tile(D_POOL), whole(1, D_MODEL), whole(8, D_CONV)],
        out_shape=[bf(DPROJ_COLS), jax.ShapeDtypeStruct((HEADS, s, 128), BF16), bf(D_ATTN),
                   bf(D_MODEL), bf(D_MODEL), bf(D_MODEL), bf(D_MODEL), bf(D_CONV), bf(D_POOL),
                   jax.ShapeDtypeStruct((1, D_MODEL), F32), jax.ShapeDtypeStruct((8, D_CONV), F32)],
        scratch_shapes=[pltpu.VMEM((HALO, D_CONV), F32), pltpu.VMEM((HALO, D_POOL), F32)],
        compiler_params=_cparams(("arbitrary",), 58 * 2 ** 20), name=name)(proj, proj, a, wa, wc, wp, scale, cw, wo, dx1)


def _adamw_math(w, g, m, v):
    m = ADAM_B1 * m + (1.0 - ADAM_B1) * g
    v = ADAM_B2 * v + (1.0 - ADAM_B2) * (g * g)
    m_hat = m / (1.0 - ADAM_B1 ** ADAM_STEP)
    v_hat = v / (1.0 - ADAM_B2 ** ADAM_STEP)
    delta = -ADAM_LR * (m_hat / (jnp.sqrt(v_hat) + ADAM_EPS) + ADAM_WD * w)
    return delta, m, v


ADAMW_PARTS_BLOCK_BYTES = 4 * 2 ** 20


def _row_tile(rows, cols, copies, itemsize):
    row_bytes = copies * (-(-cols // 128) * 128) * itemsize
    fits = [t for t in range(16, rows + 1, 16) if rows % t == 0 and t * row_bytes <= ADAMW_PARTS_BLOCK_BYTES]
    return max(fits) if fits else rows


def pair_sum(blocks, stage, me, name):
    n_slots, rows, cols = stage.shape
    tr = _row_tile(rows, cols, 1, 4)

    def body(me_ref, a_ref, b_ref, o_ref):
        o_ref[...] = (a_ref[...].astype(F32) + b_ref[...].astype(F32)).astype(BF16)

    slot = pl.BlockSpec((None, tr, cols), lambda i, r, me_ref: (i, r, 0))
    return pl.pallas_call(
        body, out_shape=jax.ShapeDtypeStruct(stage.shape, BF16),
        grid_spec=pltpu.PrefetchScalarGridSpec(
            num_scalar_prefetch=1, grid=(n_slots, rows // tr),
            in_specs=[pl.BlockSpec((None, tr, cols), lambda i, r, me_ref: (me_ref[0] ^ (2 * i), r, 0)), slot],
            out_specs=slot),
        compiler_params=_cparams(("parallel", "parallel")), name=name)(me.reshape(1), blocks, stage)


def adamw_sum(parts, w, m, v, name):
    layers, rows, cols = w.shape
    n_parts = parts.shape[1]
    if rows % 16 == 0:
        tr, tc = _row_tile(rows, cols, n_parts, parts.dtype.itemsize), cols
    else:
        tr, tc = rows, _pick(cols, (256, 128))

    def body(p_ref, w_ref, m_ref, v_ref, g_ref, d_ref, nm_ref, nv_ref):
        g = p_ref[0].astype(F32)
        for i in range(1, n_parts):
            g = g + p_ref[i].astype(F32)
        g_ref[...] = g
        d_ref[...], nm_ref[...], nv_ref[...] = _adamw_math(w_ref[...], g, m_ref[...], v_ref[...])

    spec = pl.BlockSpec((None, tr, tc), lambda l, i, j: (l, i, j))
    return pl.pallas_call(
        body, grid=(layers, rows // tr, cols // tc),
        in_specs=[pl.BlockSpec((None, n_parts, tr, tc), lambda l, i, j: (l, 0, i, j)), spec, spec, spec],
        out_specs=[spec] * 4, out_shape=[jax.ShapeDtypeStruct((layers, rows, cols), F32)] * 4,
        compiler_params=_cparams(("parallel", "parallel", "parallel")), name=name)(parts, w, m, v)


def _me():
    return lax.axis_index("x"), lax.axis_index("y"), lax.axis_index("c")


N_PEERS = N_DEV - 1


def all_gather(shards, name):
    n = len(shards)
    any_spec = pl.BlockSpec(memory_space=pl.ANY)

    def body(*refs):
        x_refs, out_refs = refs[:n], refs[n:2 * n]
        send_sems, recv_sems, local_sems = refs[2 * n:]
        x, y, c = _me()
        me, sibling = (x, y, c), (x, y, 1 - c)
        chips = [(1 - x, y), (x, 1 - y), (1 - x, 1 - y)]

        def copy(t, k, block, to, from_input=False):
            slot = out_refs[t].at[4 * block[0] + 2 * block[1] + block[2]]
            return pltpu.make_async_remote_copy(
                src_ref=x_refs[t] if from_input else slot, dst_ref=slot, send_sem=send_sems.at[N_PEERS * t + k],
                recv_sem=recv_sems.at[N_PEERS * t + k], device_id=to, device_id_type=pl.DeviceIdType.MESH)

        mine = [pltpu.make_async_copy(x_refs[t], out_refs[t].at[4 * x + 2 * y + c], local_sems.at[t]) for t in range(n)]
        started = []
        for t in range(n):
            mine[t].start()
            started.append(copy(t, 0, me, sibling, from_input=True))
            started += [copy(t, 1 + j, me, (*chip, c), from_input=True) for j, chip in enumerate(chips)]
        for cp in started:
            cp.start()
        for j, chip in enumerate(chips):
            for t in range(n):
                copy(t, 1 + j, (*chip, c), me).wait_recv()
                fwd = copy(t, 4 + j, (*chip, c), sibling)
                fwd.start()
                started.append(fwd)
        for t in range(n):
            copy(t, 0, sibling, me).wait_recv()
            for j, chip in enumerate(chips):
                copy(t, 4 + j, (*chip, 1 - c), me).wait_recv()
        for cp in started:
            cp.wait_send()
        for cp in mine:
            cp.wait()

    return pl.pallas_call(
        body, out_shape=[jax.ShapeDtypeStruct((N_DEV,) + s.shape, s.dtype) for s in shards],
        in_specs=[any_spec] * n, out_specs=[any_spec] * n,
        scratch_shapes=[pltpu.SemaphoreType.DMA((N_PEERS * n,)), pltpu.SemaphoreType.DMA((N_PEERS * n,)),
                        pltpu.SemaphoreType.DMA((n,))],
        name=name)(*shards)


SIBLING = 1
OTHER_CHIPS = (2, 4, 6)
SAME_CORE = (0,) + OTHER_CHIPS


class Exchange:
    def __init__(self, inputs, out_shapes, aliases, copies, local=()):
        self.inputs, self.out_shapes, self.aliases = list(inputs), list(out_shapes), aliases
        self._copies, self._local = list(copies), list(local)
        self.scratch = [pltpu.SemaphoreType.DMA((len(self._copies),)), pltpu.SemaphoreType.DMA((len(self._copies),)),
                        pltpu.SemaphoreType.DMA((max(len(self._local), 1),))]

    def _build(self, ins, outs, sems):
        send_sems, recv_sems, local_sems = sems
        x, y, c = _me()
        me = 4 * x + 2 * y + c
        local = [functools.partial(pltpu.make_async_copy, src(ins, outs, me), dst(outs, me), local_sems.at[i])
                 for i, (src, dst) in enumerate(self._local)]
        sends, recvs = [], []
        for i, (mask, src, dst) in enumerate(self._copies):
            px, py, pc = x ^ ((mask >> 2) & 1), y ^ ((mask >> 1) & 1), c ^ (mask & 1)
            pair = dict(send_sem=send_sems.at[i], recv_sem=recv_sems.at[i], device_id_type=pl.DeviceIdType.MESH)
            sends.append(functools.partial(
                pltpu.make_async_remote_copy, src_ref=src(ins, outs, me), dst_ref=dst(outs, me), device_id=(px, py, pc), **pair))
            recvs.append(functools.partial(
                pltpu.make_async_remote_copy, src_ref=src(ins, outs, me), dst_ref=dst(outs, me ^ mask), device_id=(x, y, c), **pair))
        return local, sends, recvs

    def start(self, ins, outs, sems):
        local, sends, _ = self._build(ins, outs, sems)
        for make in local + sends:
            make().start()

    def drain(self, ins, outs, sems):
        local, sends, recvs = self._build(ins, outs, sems)
        for make in recvs:
            make().wait_recv()
        for make in sends:
            make().wait_send()
        for make in local:
            make().wait()


def _bind(fn, *args):
    return functools.partial(fn, *args)


def join_exchanges(a, b):
    if a is None or b is None:
        return a or b
    na_in, na_out = len(a.inputs), len(a.out_shapes)

    def src_a(fn):
        return lambda ins, outs, me: fn(ins[:na_in], outs[:na_out], me)

    def dst_a(fn):
        return lambda outs, who: fn(outs[:na_out], who)

    def src_b(fn):
        return lambda ins, outs, me: fn(ins[na_in:], outs[na_out:], me)

    def dst_b(fn):
        return lambda outs, who: fn(outs[na_out:], who)

    copies = [(m, src_a(s), dst_a(d)) for m, s, d in a._copies] + [(m, src_b(s), dst_b(d)) for m, s, d in b._copies]
    local = [(src_a(s), dst_a(d)) for s, d in a._local] + [(src_b(s), dst_b(d)) for s, d in b._local]
    aliases = dict(a.aliases)
    aliases.update({na_in + i: na_out + o for i, o in b.aliases.items()})
    return Exchange(a.inputs + b.inputs, a.out_shapes + b.out_shapes, aliases, copies, local)


def gather_over_ici(shards):
    copies = [(mask, _bind(lambda t, ins, outs, me: ins[t], t), _bind(lambda t, outs, sender: outs[t].at[sender], t))
              for t in range(len(shards)) for mask in OTHER_CHIPS]
    local = [(_bind(lambda t, ins, outs, me: ins[t], t), _bind(lambda t, outs, me: outs[t].at[me], t))
             for t in range(len(shards))]
    return Exchange(shards, [jax.ShapeDtypeStruct((N_DEV,) + s.shape, s.dtype) for s in shards], {}, copies, local)


def gather_over_d2d(gathered):
    copies = [(SIBLING, _bind(lambda t, m, ins, outs, me: outs[t].at[me ^ m], t, m),
               _bind(lambda t, m, outs, sender: outs[t].at[sender ^ m], t, m))
              for t in range(len(gathered)) for m in SAME_CORE]
    return Exchange(gathered, [jax.ShapeDtypeStruct(g.shape, g.dtype) for g in gathered],
                    {t: t for t in range(len(gathered))}, copies)


def scatter_over_d2d(blocks):
    copies = [(SIBLING, _bind(lambda t, m, ins, outs, me: ins[t].at[me ^ SIBLING ^ m], t, m),
               _bind(lambda t, i, outs, sender: outs[t].at[i], t, i))
              for t in range(len(blocks)) for i, m in enumerate(SAME_CORE)]
    return Exchange(blocks, [jax.ShapeDtypeStruct((len(SAME_CORE),) + b.shape[1:], b.dtype) for b in blocks], {}, copies)


def scatter_over_ici(pair_sums, bufs, layer):
    n = len(pair_sums)
    copies = [(m, _bind(lambda t, i, ins, outs, me: ins[t].at[i], t, i),
               _bind(lambda t, i, outs, sender: outs[t].at[layer, i], t, i))
              for t in range(n) for i, m in enumerate(SAME_CORE) if m]
    local = [(_bind(lambda t, ins, outs, me: ins[t].at[0], t), _bind(lambda t, outs, me: outs[t].at[layer, 0], t))
             for t in range(n)]
    return Exchange(list(pair_sums) + list(bufs), [jax.ShapeDtypeStruct(b.shape, b.dtype) for b in bufs],
                    {n + t: t for t in range(n)}, copies, local)


def run_exchange(ex, name):
    any_spec = pl.BlockSpec(memory_space=pl.ANY)
    n_in, n_out = len(ex.inputs), len(ex.out_shapes)

    def body(*refs):
        ins, outs, sems = refs[:n_in], refs[n_in:n_in + n_out], refs[n_in + n_out:]
        ex.start(ins, outs, sems)
        ex.drain(ins, outs, sems)

    return pl.pallas_call(
        body, out_shape=ex.out_shapes, in_specs=[any_spec] * n_in, out_specs=[any_spec] * n_out,
        input_output_aliases=ex.aliases, scratch_shapes=ex.scratch, name=name)(*ex.inputs)


MATRICES = ("w_in", "w_attn_out", "w_conv_out", "pool_w", "w_o", "w_ffn_in", "w_ffn_out")
TRANSPOSED = ("w_in", "w_ffn_in")
EVERY = tuple(range(len(MATRICES)))
IN_PROJ_PART, ATTN_PART, MIX_PART = (0,), (1, 2, 3, 4, 5), (6,)
LATE = (0,)
EARLY = EVERY[1:]
EARLY_FIRST, EARLY_SECOND = (4, 6), (1, 2, 3, 5)
SHARD_INFO = {
    "w_in": ((DEPTH, D_IN // N_DEV, D_MODEL), 1),
    "w_attn_out": ((DEPTH, D_ATTN, D_MODEL // N_DEV), 2),
    "w_conv_out": ((DEPTH, D_CONV, D_MODEL // N_DEV), 2),
    "pool_w": ((DEPTH, 4, 64, 256 // N_DEV), 3),
    "w_o": ((DEPTH, D_MODEL // N_DEV, D_MODEL), 1),
    "w_ffn_in": ((DEPTH, 2 * D_FF // N_DEV, D_MODEL), 1),
    "w_ffn_out": ((DEPTH, D_FF // N_DEV, D_MODEL), 1),
}


def _handled(name, t):
    return jnp.transpose(t, (0, 2, 1)) if name in TRANSPOSED else t
VECTORS = ("norm_mix_g", "forget_b", "q_norm_g", "k_norm_g", "pool_scale", "norm_ffn_g")
VECTOR_SHAPES = {"norm_mix_g": (DEPTH, D_MODEL), "forget_b": (DEPTH, HEADS), "q_norm_g": (DEPTH, HEAD_DIM),
                 "k_norm_g": (DEPTH, HEAD_DIM), "pool_scale": (DEPTH, D_MODEL), "norm_ffn_g": (DEPTH, D_MODEL)}
CONV_W_FULL = (DEPTH, 3, D_CONV)


def _size(shape):
    n = 1
    for v in shape:
        n *= v
    return n


def _pack(arrays, rows, cols):
    flat = jnp.concatenate([a.reshape(-1) for a in arrays])
    return jnp.pad(flat, (0, rows * cols - flat.shape[0])).reshape(rows, cols)


def _unpack(packed, shapes):
    flat, out, off = packed.reshape(-1), [], 0
    for shp in shapes:
        out.append(flat[off:off + _size(shp)].reshape(shp))
        off += _size(shp)
    return out


def _join_shards(stacked, axis):
    moved = jnp.moveaxis(stacked, 0, axis)
    shp = list(moved.shape)
    shp[axis:axis + 2] = [shp[axis] * shp[axis + 1]]
    return moved.reshape(shp)


def _cut_shards(full, axis):
    shp = list(full.shape)
    shp[axis:axis + 1] = [N_DEV, shp[axis] // N_DEV]
    return jnp.moveaxis(full.reshape(shp), axis, 0)


N_MOVED = 1544
SHARD_ROWS = D_IN // N_DEV


def _regroup_w_in(shards):
    wt = shards.reshape(D_IN, shards.shape[2])
    pad = jnp.zeros((N_FULL - D_IN, wt.shape[1]), wt.dtype)
    return jnp.concatenate([wt[N_MOVED:], wt[:N_MOVED], pad], axis=0)


def _ungroup_w_in(wpt):
    def kernel_rows(a, b):
        if b <= N_MOVED:
            return [wpt[a + D_IN - N_MOVED:b + D_IN - N_MOVED]]
        if a >= N_MOVED:
            return [wpt[a - N_MOVED:b - N_MOVED]]
        return kernel_rows(a, N_MOVED) + kernel_rows(N_MOVED, b)

    return jnp.stack([jnp.concatenate(kernel_rows(s * SHARD_ROWS, (s + 1) * SHARD_ROWS), axis=0) for s in range(N_DEV)])


def _pool_block_diag(w):
    out = jnp.zeros((D_POOL, D_MODEL), w.dtype)
    for g in range(4):
        out = lax.dynamic_update_slice(out, w[g], (g * 64, g * 256))
    return out


def _pool_from_block_diag(wbd):
    return jnp.stack([wbd[g * 64:(g + 1) * 64, g * 256:(g + 1) * 256] for g in range(4)])


def _layer_weights(mats, vec, conv_w, l):
    wp = _pool_block_diag(mats["pool_w"])
    row = lambda v: v.reshape(1, -1)
    fb = jnp.zeros((1, 128), F32).at[0, :HEADS].set(vec["forget_b"][l])
    cw = jnp.zeros((8, D_CONV), F32).at[:3].set(conv_w[l])
    twice = lambda v: jnp.tile(v.reshape(1, -1), (1, 2))
    return dict(
        wt_in=_regroup_w_in(mats["w_in"]), wt_ffn_in=mats["w_ffn_in"], w_ffn_out=mats["w_ffn_out"],
        wa=mats["w_attn_out"], wc=mats["w_conv_out"], wp=wp, wo=mats["w_o"],
        g_mix=row(vec["norm_mix_g"][l]), g_ffn=row(vec["norm_ffn_g"][l]), gq2=twice(vec["q_norm_g"][l]),
        gk2=twice(vec["k_norm_g"][l]), scale=row(vec["pool_scale"][l]), fb=fb, cw=cw)


def _layer_fwd(x, w, l, comm):
    (proj, h), half_a = norm_matmul(x, w["g_mix"], w["wt_in"], N_MAIN, f"in_proj_{l}", comm.gather_ici(l + 1, IN_PROJ_PART))
    qa, ka, va, vt, z = attn_prep(proj, h, w["wt_in"], w["fb"], w["gq2"], w["gk2"], f"attn_prep_{l}")
    (oa, lse), half_b = attn_forward(qa, ka, vt, f"attn_fwd_{l}", comm.gather_ici(l + 1, ATTN_PART))
    x1, half_c = mix_fwd(proj, oa, x, w["wa"], w["wc"], w["wp"], w["scale"], w["cw"], w["wo"], f"mix_fwd_{l}",
                         comm.gather_ici(l + 1, MIX_PART))
    half = list(half_a) + list(half_b) + list(half_c)
    (gu, h2), gathered = norm_matmul(x1, w["g_ffn"], w["wt_ffn_in"], 2 * D_FF, f"ffn_in_{l}", comm.gather_d2d(l + 1, half))
    x2 = swiglu_matmul(gu, w["w_ffn_out"], x1, f"ffn_out_{l}")
    saved = dict(x=x, proj=proj, h=h, z=z, qa=qa, ka=ka, va=va, oa=oa, lse=lse, x1=x1, gu=gu, h2=h2)
    return x2, saved, gathered


def _layer_bwd(dx2, sv, w, l, comm):
    g = {}
    (dgu, act), stage = swiglu_bwd(dx2, sv["gu"], w["w_ffn_out"], f"ffn_out_bwd_{l}", comm.scatter_d2d(l + 1))
    sums = comm.pair_sums(l + 1, stage)
    g["w_ffn_out"] = tn_matmul(act, dx2, f"dw_ffn_out_{l}")
    g["w_ffn_in"] = tn_matmul(dgu, sv["h2"], f"dw_ffn_in_{l}")
    (dx1, dg), _ = matmul_normbwd(dgu, w["wt_ffn_in"], sv["x1"], w["g_ffn"], dx2, f"ffn_in_bwd_{l}")
    g["norm_ffn_g"] = dg[0]

    (dproj, doa, a_tok, merged, dya, dyc, dyp, uc, dd, dscale, dcw) = mix_bwd(
        sv["proj"], sv["oa"], dx1, w["wa"], w["wc"], w["wp"], w["scale"], w["cw"], w["wo"], f"mix_bwd_{l}")
    g["w_o"] = tn_matmul(merged, dx1, f"dw_o_{l}")
    g["w_attn_out"], g["w_conv_out"], dwp = tn_matmuls([(a_tok, dya), (uc, dyc), (dd, dyp)], f"dw_branches_{l}")
    g["pool_w"] = _pool_from_block_diag(dwp)
    g["pool_scale"] = dscale[0]
    g["conv_w"] = dcw[:3]

    early = comm.early(l)
    comm.grads(l, g)
    above = comm.scatter_ici(l + 1, sums)
    (dqa, dka, dva), got = attn_backward(sv["qa"], sv["ka"], sv["va"], sv["oa"], doa, sv["lse"], f"attn_bwd_{l}",
                                         join_exchanges(above, comm.scatter_d2d(l, early) if early else None))
    n_above = len(above.out_shapes) if above else 0
    comm.scattered(got[:n_above])
    early_sums = dict(zip(early, comm.pair_sums(l, got[n_above:], early))) if early else {}
    early_ici = lambda which: comm.scatter_ici(l, [early_sums[t] for t in which], which) if early else None
    dproj, dgq, dgk, db = attn_post(dqa, dka, dva, sv["proj"], sv["z"], w["gq2"], w["gk2"], dproj, f"attn_post_{l}")
    g["q_norm_g"] = dgq[0, :HEAD_DIM] + dgq[0, HEAD_DIM:]
    g["k_norm_g"] = dgk[0, :HEAD_DIM] + dgk[0, HEAD_DIM:]
    g["forget_b"] = db[0, :HEADS]

    dw_in = tn_matmul(dproj, sv["h"], f"dw_in_{l}", m_cols=N_FULL, ex=early_ici(EARLY_FIRST))
    if early:
        dw_in, got = dw_in
        comm.scattered(got, EARLY_FIRST)
    g["w_in"] = _ungroup_w_in(dw_in)
    (dx, dg), got = matmul_normbwd(dproj, w["wt_in"], sv["x"], w["g_mix"], dx1, f"in_proj_bwd_{l}", k=N_FULL,
                                   ex=early_ici(EARLY_SECOND))
    comm.scattered(got, EARLY_SECOND if early else None)
    g["norm_mix_g"] = dg[0]
    comm.grads(l, g)
    return dx


def _local_step(x, tgt, comm):
    ws, saved = [], []
    w = comm.weights(0, None)
    for l in range(DEPTH):
        ws.append(w)
        x, sv, gathered = _layer_fwd(x, w, l, comm)
        saved.append(sv)
        if l + 1 < DEPTH:
            w = comm.weights(l + 1, gathered)
    sq, dx = loss_kernel(x, tgt, "loss")
    for l in reversed(range(DEPTH)):
        dx = _layer_bwd(dx, saved[l], ws[l], l, comm)
    comm.finish()
    return sq[0, 0], dx


def kernel(x, norm_mix_g, w_in, forget_b, q_norm_g, k_norm_g, w_attn_out, conv_w, w_conv_out, pool_w, pool_scale, w_o, norm_ffn_g, w_ffn_in, w_ffn_out, loss_target, m_norm_mix_g, m_w_in, m_forget_b, m_q_norm_g, m_k_norm_g, m_w_attn_out, m_conv_w, m_w_conv_out, m_pool_w, m_pool_scale, m_w_o, m_norm_ffn_g, m_w_ffn_in, m_w_ffn_out, v_norm_mix_g, v_w_in, v_forget_b, v_q_norm_g, v_k_norm_g, v_w_attn_out, v_conv_w, v_w_conv_out, v_pool_w, v_pool_scale, v_w_o, v_norm_ffn_g, v_w_ffn_in, v_w_ffn_out):
    w = dict(norm_mix_g=norm_mix_g, w_in=w_in, forget_b=forget_b, q_norm_g=q_norm_g, k_norm_g=k_norm_g,
             w_attn_out=w_attn_out, conv_w=conv_w, w_conv_out=w_conv_out, pool_w=pool_w, pool_scale=pool_scale,
             w_o=w_o, norm_ffn_g=norm_ffn_g, w_ffn_in=w_ffn_in, w_ffn_out=w_ffn_out)
    m = dict(norm_mix_g=m_norm_mix_g, w_in=m_w_in, forget_b=m_forget_b, q_norm_g=m_q_norm_g, k_norm_g=m_k_norm_g,
             w_attn_out=m_w_attn_out, conv_w=m_conv_w, w_conv_out=m_w_conv_out, pool_w=m_pool_w,
             pool_scale=m_pool_scale, w_o=m_w_o, norm_ffn_g=m_norm_ffn_g, w_ffn_in=m_w_ffn_in, w_ffn_out=m_w_ffn_out)
    v = dict(norm_mix_g=v_norm_mix_g, w_in=v_w_in, forget_b=v_forget_b, q_norm_g=v_q_norm_g, k_norm_g=v_k_norm_g,
             w_attn_out=v_w_attn_out, conv_w=v_conv_w, w_conv_out=v_w_conv_out, pool_w=v_pool_w,
             pool_scale=v_pool_scale, w_o=v_w_o, norm_ffn_g=v_norm_ffn_g, w_ffn_in=v_w_ffn_in, w_ffn_out=v_w_ffn_out)
    me = 4 * lax.axis_index("x") + 2 * lax.axis_index("y") + lax.axis_index("c")
    layer_shard = {n: SHARD_INFO[n][0][1:] for n in MATRICES}
    cut_axis = {n: SHARD_INFO[n][1] - 1 for n in MATRICES}

    vec = {n: w[n] for n in VECTORS}
    rc = {n: (_size(layer_shard[n][:-1]), layer_shard[n][-1]) for n in MATRICES}

    class Comm:
        bufs = [lax.empty((DEPTH, len(SAME_CORE)) + layer_shard[n], BF16) for n in MATRICES]
        blocks = [None] * DEPTH
        small_g = [None] * DEPTH
        conv_full = None

        @staticmethod
        def shards(l):
            return [_handled(n, w[n])[l].astype(BF16) for n in MATRICES]

        @staticmethod
        def gather_ici(l, part):
            return gather_over_ici([Comm.shards(l)[t] for t in part]) if l < DEPTH else None

        @staticmethod
        def gather_d2d(l, half):
            return gather_over_d2d(half) if l < DEPTH else None

        @staticmethod
        def weights(l, gathered):
            if l == 0:
                *gathered, conv_g = all_gather(Comm.shards(0) + [_pack([conv_w], 8, 128)], "gather_0")
                Comm.conv_full = _join_shards(jnp.stack([_unpack(conv_g[i], [conv_w.shape])[0] for i in range(N_DEV)]), 2)
            mats = {n: t if n == "w_in" else _join_shards(t, cut_axis[n]) for n, t in zip(MATRICES, gathered)}
            return _layer_weights(mats, vec, Comm.conv_full, l)

        @staticmethod
        def grads(l, g):
            Comm.small_g[l] = g
            Comm.blocks[l] = [None if n not in g else g[n] if n == "w_in" else _cut_shards(g[n], cut_axis[n])
                              for n in MATRICES]

        @staticmethod
        def early(l):
            return EARLY if l == 0 else None

        @staticmethod
        def scatter_d2d(l, which=EVERY):
            return scatter_over_d2d([Comm.blocks[l][t] for t in which]) if l < DEPTH else None

        @staticmethod
        def pair_sums(l, stage, which=EVERY):
            if l >= DEPTH:
                return None
            return [pair_sum(Comm.blocks[l][t].reshape((N_DEV,) + rc[MATRICES[t]]),
                             s.reshape((len(SAME_CORE),) + rc[MATRICES[t]]), me,
                             f"pair_sum_{MATRICES[t]}_{l}").reshape(s.shape) for t, s in zip(which, stage)]

        @staticmethod
        def scatter_ici(l, sums, which=EVERY):
            return scatter_over_ici(sums, [Comm.bufs[t] for t in which], l) if l < DEPTH else None

        @staticmethod
        def scattered(results, which=EVERY):
            for t, r in zip(which or (), results):
                Comm.bufs[t] = r

        @staticmethod
        def finish():
            stage = run_exchange(Comm.scatter_d2d(0, LATE), "scatter_d2d_0")
            Comm.scattered(run_exchange(Comm.scatter_ici(0, Comm.pair_sums(0, stage, LATE), LATE), "scatter_ici_0"), LATE)

    small_g, received = Comm.small_g, Comm
    sq, dx = _local_step(x[0], loss_target[0], Comm)

    big = {}
    for n, parts in zip(MATRICES, received.bufs):
        outs = adamw_sum(parts.reshape((DEPTH, len(SAME_CORE)) + rc[n]),
                         *[_handled(n, d[n]).reshape((DEPTH,) + rc[n]) for d in (w, m, v)], f"adamw_{n}")
        big[n] = [_handled(n, t.reshape((DEPTH,) + layer_shard[n])) for t in outs]

    small_shapes = [VECTOR_SHAPES[n] for n in VECTORS] + [CONV_W_FULL, (1,)]
    stacked = [jnp.stack([small_g[l][n] for l in range(DEPTH)]) for n in VECTORS + ("conv_w",)] + [sq.reshape(1)]
    sparts = all_gather([_pack(stacked, SMALL_ROWS, 128)], "gather_vector_grads")[0]
    col0 = me * (D_CONV // N_DEV)
    place = lambda t: lax.dynamic_update_slice(jnp.zeros(CONV_W_FULL, F32), t, (0, 0, col0))
    spacked = [_pack([d[n] for n in VECTORS] + [place(d["conv_w"]), jnp.zeros((1,), F32)], SMALL_ROWS, 128)[None]
               for d in (w, m, v)]
    small = [_unpack(t[0], small_shapes) for t in adamw_sum(sparts[None], *spacked, "adamw_vectors")]
    loss = (0.5 / D_MODEL) * small[0][-1][0]

    def result(kind):
        out = {n: big[n][kind] for n in MATRICES}
        out.update({n: small[kind][j] for j, n in enumerate(VECTORS)})
        out["conv_w"] = lax.dynamic_slice(small[kind][len(VECTORS)], (0, 0, col0), conv_w.shape)
        return [out[n] for n in w]

    return (loss, dx[None], *result(0), *result(1), *result(2), *result(3))
```

```python
import functools

import jax
import jax.numpy as jnp
from jax import lax
from jax.experimental import pallas as pl
from jax.experimental.pallas import tpu as pltpu

F32 = jnp.float32
BF16 = jnp.bfloat16

N_DEV = 8
DEPTH = 4
D_MODEL = 1024
HEAD_DIM = 64
HEADS = 8
D_ATTN = 512
D_CONV = 256
D_POOL = 256
D_FF = 2816
D_IN = 5640
EPS = 1e-6
ATTN_SCALE = HEAD_DIM ** -0.5

N_REST = 4096
N_MAIN = 5632
N_FULL = 5760
DPROJ_TAIL = 2048
DPROJ_COLS = N_REST + DPROJ_TAIL
FF_BLK = 256
N_FF_BLKS = D_FF // FF_BLK
HALO = 16

ADAM_LR = 0.001
ADAM_B1 = 0.9
ADAM_B2 = 0.999
ADAM_EPS = 1e-08
ADAM_WD = 0.01
ADAM_STEP = 10

SMALL_ROWS = 128

VMEM_LIMIT = 48 * 2 ** 20


def _cparams(sem, vmem=None):
    return pltpu.CompilerParams(dimension_semantics=sem, vmem_limit_bytes=vmem or VMEM_LIMIT)


def _pick(n, cands):
    for c in cands:
        if n % c == 0:
            return c
    raise ValueError(f"no tile for {n}")


def _tile(n, cap):
    t = min(cap, n)
    assert n % t == 0, (n, cap)
    return t


def _sigmoid(v):
    return 1.0 / (1.0 + jnp.exp(-v))


def _rstd(v):
    return lax.rsqrt(jnp.mean(v * v, axis=-1, keepdims=True) + EPS)


def _dot(a, b):
    return jnp.dot(a, b, preferred_element_type=F32)


def _dot_tn(a, b):
    return lax.dot_general(a, b, (((0,), (0,)), ((), ())), preferred_element_type=F32)


def _dot_nt(a, b):
    return lax.dot_general(a, b, (((1,), (1,)), ((), ())), preferred_element_type=F32)


def norm_matmul(x, g, wt, n_cols, name, ex=None):
    s, d = x.shape
    tm, tn = _tile(s, 1024), _pick(n_cols, (2816, 1408, 512))

    def body(x_ref, g_ref, w_ref, o_ref, h_ref):
        @pl.when(pl.program_id(1) == 0)
        def _():
            xv = x_ref[...]
            h_ref[...] = (xv * _rstd(xv) * g_ref[...]).astype(BF16)

        o_ref[...] = _dot_nt(h_ref[...], w_ref[...]).astype(BF16)

    return _carried_call(
        body, ex, (s // tm, n_cols // tn),
        [pl.BlockSpec((tm, d), lambda i, j: (i, 0)), pl.BlockSpec((1, d), lambda i, j: (0, 0)),
         pl.BlockSpec((tn, d), lambda i, j: (j, 0))],
        [pl.BlockSpec((tm, tn), lambda i, j: (i, j)), pl.BlockSpec((tm, d), lambda i, j: (i, 0))],
        [jax.ShapeDtypeStruct((s, n_cols), BF16), jax.ShapeDtypeStruct((s, d), BF16)], [],
        ("arbitrary", "arbitrary"), name, (x, g, wt))


def tn_matmul(a, b, name, m_cols=None, ex=None):
    t = a.shape[0]
    m = m_cols or a.shape[1]
    n = b.shape[1]
    tk = _tile(t, 1024)
    tmm = _pick(m, (1408, 1152, 1024, 512, 256))
    tn = _pick(n, (1408, 1152, 1024, 512, 128))
    nk = t // tk

    def body(a_ref, b_ref, o_ref, acc_ref):
        @pl.when(pl.program_id(2) == 0)
        def _():
            acc_ref[...] = jnp.zeros_like(acc_ref)

        acc_ref[...] += _dot_tn(a_ref[...].astype(BF16), b_ref[...].astype(BF16))

        @pl.when(pl.program_id(2) == nk - 1)
        def _():
            o_ref[...] = acc_ref[...].astype(BF16)

    if ex is None:
        return pl.pallas_call(
            body, grid=(m // tmm, n // tn, nk),
            in_specs=[pl.BlockSpec((tk, tmm), lambda i, j, k: (k, i)), pl.BlockSpec((tk, tn), lambda i, j, k: (k, j))],
            out_specs=pl.BlockSpec((tmm, tn), lambda i, j, k: (i, j)),
            out_shape=jax.ShapeDtypeStruct((m, n), BF16), scratch_shapes=[pltpu.VMEM((tmm, tn), F32)],
            compiler_params=_cparams(("parallel", "parallel", "arbitrary")), name=name)(a, b)
    (out,), carried = _carried_call(
        body, ex, (m // tmm, n // tn, nk),
        [pl.BlockSpec((tk, tmm), lambda i, j, k: (k, i)), pl.BlockSpec((tk, tn), lambda i, j, k: (k, j))],
        [pl.BlockSpec((tmm, tn), lambda i, j, k: (i, j))], [jax.ShapeDtypeStruct((m, n), BF16)],
        [pltpu.VMEM((tmm, tn), F32)], ("arbitrary", "arbitrary", "arbitrary"), name, (a, b))
    return out, carried


def tn_matmuls(pairs, name):
    t = pairs[0][0].shape[0]
    tk = _tile(t, 1024)
    nk = t // tk
    n = len(pairs)

    def body(*refs):
        ins, outs, accs = refs[:2 * n], refs[2 * n:3 * n], refs[3 * n:]

        @pl.when(pl.program_id(0) == 0)
        def _():
            for acc in accs:
                acc[...] = jnp.zeros_like(acc)

        for i in range(n):
            accs[i][...] += _dot_tn(ins[2 * i][...], ins[2 * i + 1][...])

        @pl.when(pl.program_id(0) == nk - 1)
        def _():
            for out, acc in zip(outs, accs):
                out[...] = acc[...].astype(BF16)

    shapes = [(a.shape[1], b.shape[1]) for a, b in pairs]
    return pl.pallas_call(
        body, grid=(nk,),
        in_specs=[pl.BlockSpec((tk, t_.shape[1]), lambda k: (k, 0)) for pair in pairs for t_ in pair],
        out_specs=[pl.BlockSpec(shp, lambda k: (0, 0)) for shp in shapes],
        out_shape=[jax.ShapeDtypeStruct(shp, BF16) for shp in shapes],
        scratch_shapes=[pltpu.VMEM(shp, F32) for shp in shapes],
        compiler_params=_cparams(("arbitrary",)), name=name)(*[t_ for pair in pairs for t_ in pair])


def matmul_normbwd(a, wt, x, g, dres, name, k=None, ex=None):
    s = a.shape[0]
    k = k or a.shape[1]
    d = wt.shape[1]
    tm = _tile(s, 1024)
    tk = _pick(k, (1408, 1152, 512))
    nk = k // tk

    def body(a_ref, w_ref, x_ref, g_ref, r_ref, dx_ref, dg_ref, acc_ref):
        i, kk = pl.program_id(0), pl.program_id(1)

        @pl.when(kk == 0)
        def _():
            acc_ref[...] = jnp.zeros_like(acc_ref)

        @pl.when((i == 0) & (kk == 0))
        def _():
            dg_ref[...] = jnp.zeros_like(dg_ref)

        acc_ref[...] += _dot(a_ref[...], w_ref[...])

        @pl.when(kk == nk - 1)
        def _():
            xv = x_ref[...]
            r = _rstd(xv)
            y = xv * r
            dh = acc_ref[...]
            dy = dh * g_ref[...]
            dx_ref[...] = r_ref[...] + r * (dy - y * jnp.mean(dy * y, axis=-1, keepdims=True))
            dg_ref[...] += jnp.sum(dh * y, axis=0, keepdims=True)

    return _carried_call(
        body, ex, (s // tm, nk),
        [pl.BlockSpec((tm, tk), lambda i, kk: (i, kk)), pl.BlockSpec((tk, d), lambda i, kk: (kk, 0)),
         pl.BlockSpec((tm, d), lambda i, kk: (i, 0)), pl.BlockSpec((1, d), lambda i, kk: (0, 0)),
         pl.BlockSpec((tm, d), lambda i, kk: (i, 0))],
        [pl.BlockSpec((tm, d), lambda i, kk: (i, 0)), pl.BlockSpec((1, d), lambda i, kk: (0, 0))],
        [jax.ShapeDtypeStruct((s, d), F32), jax.ShapeDtypeStruct((1, d), F32)],
        [pltpu.VMEM((tm, d), F32)], ("arbitrary", "arbitrary"), name, (a, wt, x, g, dres), vmem=56 * 2 ** 20)


def swiglu_matmul(gu, w, x1, name):
    s = gu.shape[0]
    d = w.shape[1]
    tm = _tile(s, 512)

    def body(gu_ref, w_ref, x_ref, o_ref):
        acc = x_ref[...]
        for j in range(N_FF_BLKS):
            gt = gu_ref[:, j * FF_BLK:(j + 1) * FF_BLK].astype(F32)
            up = gu_ref[:, D_FF + j * FF_BLK:D_FF + (j + 1) * FF_BLK].astype(F32)
            act = (gt * _sigmoid(gt) * up).astype(BF16)
            acc += _dot(act, w_ref[j * FF_BLK:(j + 1) * FF_BLK, :])
        o_ref[...] = acc

    return pl.pallas_call(
        body, grid=(s // tm,),
        in_specs=[pl.BlockSpec((tm, 2 * D_FF), lambda i: (i, 0)), pl.BlockSpec((D_FF, d), lambda i: (0, 0)),
                  pl.BlockSpec((tm, d), lambda i: (i, 0))],
        out_specs=pl.BlockSpec((tm, d), lambda i: (i, 0)),
        out_shape=jax.ShapeDtypeStruct((s, d), F32),
        compiler_params=_cparams(("parallel",)), name=name)(gu, w, x1)


def swiglu_bwd(dx2, gu, w, name, ex=None):
    s, d = dx2.shape
    tm = _tile(s, 512)

    def body(dx_ref, gu_ref, w_ref, dgu_ref, act_ref):
        dx = dx_ref[...].astype(BF16)
        for j in range(N_FF_BLKS):
            g_cols = slice(j * FF_BLK, (j + 1) * FF_BLK)
            u_cols = slice(D_FF + j * FF_BLK, D_FF + (j + 1) * FF_BLK)
            dact = _dot_nt(dx, w_ref[j * FF_BLK:(j + 1) * FF_BLK, :])
            gt = gu_ref[:, g_cols].astype(F32)
            up = gu_ref[:, u_cols].astype(F32)
            sg = _sigmoid(gt)
            silu = gt * sg
            act_ref[:, j * FF_BLK:(j + 1) * FF_BLK] = (silu * up).astype(BF16)
            dgu_ref[:, g_cols] = (dact * up * (sg + silu * (1.0 - sg))).astype(BF16)
            dgu_ref[:, u_cols] = (dact * silu).astype(BF16)

    return _carried_call(
        body, ex, (s // tm,),
        [pl.BlockSpec((tm, d), lambda i: (i, 0)), pl.BlockSpec((tm, 2 * D_FF), lambda i: (i, 0)),
         pl.BlockSpec((D_FF, d), lambda i: (0, 0), pipeline_mode=pl.Buffered(1))],
        [pl.BlockSpec((tm, 2 * D_FF), lambda i: (i, 0)), pl.BlockSpec((tm, D_FF), lambda i: (i, 0))],
        [jax.ShapeDtypeStruct((s, 2 * D_FF), BF16), jax.ShapeDtypeStruct((s, D_FF), BF16)], [],
        ("arbitrary",), name, (dx2, gu, w), vmem=56 * 2 ** 20)


def loss_kernel(y, tgt, name):
    s, d = y.shape
    tm = _tile(s, 512)

    def body(y_ref, t_ref, l_ref, dy_ref):
        @pl.when(pl.program_id(0) == 0)
        def _():
            l_ref[...] = jnp.zeros_like(l_ref)

        err = y_ref[...] - t_ref[...]
        dy_ref[...] = err * (1.0 / d)
        l_ref[...] += jnp.sum(jnp.sum(err * err, axis=1, keepdims=True), axis=0, keepdims=True)

    return pl.pallas_call(
        body, grid=(s // tm,),
        in_specs=[pl.BlockSpec((tm, d), lambda i: (i, 0)), pl.BlockSpec((tm, d), lambda i: (i, 0))],
        out_specs=[pl.BlockSpec((8, 128), lambda i: (0, 0)), pl.BlockSpec((tm, d), lambda i: (i, 0))],
        out_shape=[jax.ShapeDtypeStruct((8, 128), F32), jax.ShapeDtypeStruct((s, d), F32)],
        compiler_params=_cparams(("arbitrary",)), name=name)(y, tgt)


def _split3(v):
    a1 = v.astype(BF16)
    r1 = v - a1.astype(F32)
    a2 = r1.astype(BF16)
    a3 = (r1 - a2.astype(F32)).astype(BF16)
    return a1, a2, a3


def _running_sum(v, carry_ref, reverse):
    tm = v.shape[0]
    row = lax.broadcasted_iota(jnp.int32, (tm, tm), 0)
    col = lax.broadcasted_iota(jnp.int32, (tm, tm), 1)
    tri = ((col >= row) if reverse else (row >= col)).astype(BF16)
    a1, a2, a3 = _split3(v)
    out = _dot(tri, a1) + _dot(tri, a2) + _dot(tri, a3) + carry_ref[...]
    carry_ref[...] = out[0:1, :] if reverse else out[tm - 1:tm, :]
    return out


HEAD_GROUP_FWD = 8
HEAD_GROUP_BWD = 8
LANE_C = 64
LANE_ONE = 67


def _lanes():
    lane = lax.broadcasted_iota(jnp.int32, (1, 128), 1)
    return lane, lane < HEAD_DIM


def _half_mean(t, lo):
    s_lo = jnp.sum(jnp.where(lo, t, 0.0), axis=-1, keepdims=True)
    s_hi = jnp.sum(jnp.where(lo, 0.0, t), axis=-1, keepdims=True)
    return jnp.where(lo, s_lo, s_hi) * (1.0 / HEAD_DIM)


def _lane_col(t, lane, idx):
    return jnp.sum(jnp.where(lane == idx, t, 0.0), axis=-1, keepdims=True)


def _swap_halves(t):
    return pltpu.roll(t, HEAD_DIM, 1)


def attn_prep(proj, h, wt_in, fb, gq2, gk2, name):
    s, d = h.shape
    tm = _tile(s, 512)
    first = N_REST // D_ATTN

    def body(q_ref, k_ref, v_ref, h_ref, wf_ref, fb_ref, gq_ref, gk_ref, qa_ref, ka_ref, va_ref, vt_ref, z_ref, carry_ref):
        lane, lo = _lanes()

        @pl.when(pl.program_id(0) == 0)
        def _():
            carry_ref[...] = jnp.zeros_like(carry_ref)

        z = _dot_nt(h_ref[...], wf_ref[...]) + fb_ref[...]
        z_ref[...] = z
        cv = _running_sum(jnp.minimum(z, 0.0) - jnp.log(1.0 + jnp.exp(-jnp.abs(z))), carry_ref, reverse=False)

        def normed(t, g):
            t = t.astype(F32)
            return t * lax.rsqrt(_half_mean(t * t, lo) + EPS) * g

        one_q = jnp.where((lane >= LANE_ONE) & (lane < LANE_ONE + 3), 1.0, 0.0)
        one_k = jnp.where((lane >= LANE_C) & (lane < LANE_C + 3), 1.0, 0.0)
        one_v = jnp.where(lane == LANE_C, 1.0, 0.0)
        for j in range(HEADS // 2):
            cols = slice(128 * j, 128 * (j + 1))
            qn = normed(q_ref[:, cols], gq_ref[...] * ATTN_SCALE)
            kn = normed(k_ref[:, cols], gk_ref[...])
            vv = v_ref[:, cols].astype(F32)
            for e in range(2):
                h = 2 * j + e
                pick = (lambda t: t) if e == 0 else _swap_halves
                pieces = [p.astype(F32) for p in _split3(_lane_col(cv, lane, h))]
                ext_q, ext_k = one_q, one_k
                for i, p in enumerate(pieces):
                    ext_q = jnp.where(lane == LANE_C + i, p, ext_q)
                    ext_k = jnp.where(lane == LANE_ONE + i, -p, ext_k)
                qa_ref[h] = jnp.where(lo, pick(qn), ext_q).astype(BF16)
                ka_ref[h] = jnp.where(lo, pick(kn), ext_k).astype(BF16)
                va = jnp.where(lo, pick(vv), one_v)
                va_ref[h] = va.astype(BF16)
                vt_ref[h] = va.T.astype(BF16)

    tile = lambda blk: pl.BlockSpec((tm, D_ATTN), lambda i: (i, blk))
    vec = pl.BlockSpec((1, 128), lambda i: (0, 0))
    out = pl.BlockSpec((HEADS, tm, 128), lambda i: (0, i, 0))
    return pl.pallas_call(
        body, grid=(s // tm,),
        in_specs=[tile(first), tile(first + 1), tile(first + 2), pl.BlockSpec((tm, d), lambda i: (i, 0)),
                  pl.BlockSpec((128, d), lambda i: (N_MAIN // 128, 0)), vec, vec, vec],
        out_specs=[out, out, out, pl.BlockSpec((HEADS, 128, tm), lambda i: (0, 0, i)),
                   pl.BlockSpec((tm, 128), lambda i: (i, 0))],
        out_shape=[jax.ShapeDtypeStruct((HEADS, s, 128), BF16)] * 3 + [jax.ShapeDtypeStruct((HEADS, 128, s), BF16),
                                                                       jax.ShapeDtypeStruct((s, 128), F32)],
        scratch_shapes=[pltpu.VMEM((1, 128), F32)],
        compiler_params=_cparams(("arbitrary",)), name=name)(proj, proj, proj, h, wt_in, fb, gq2, gk2)


def _carry(ex, n_in, n_out, n_scratch, grid):
    n_xin, n_xout = (len(ex.inputs), len(ex.out_shapes)) if ex else (0, 0)

    def split(refs):
        ins, xins = refs[:n_in], refs[n_in:n_in + n_xin]
        rest = refs[n_in + n_xin:]
        outs, xouts = rest[:n_out], rest[n_out:n_out + n_xout]
        rest = rest[n_out + n_xout:]
        return ins + outs + rest[:n_scratch], (xins, xouts, rest[n_scratch:])

    def first():
        return functools.reduce(lambda a, b: a & b, [pl.program_id(d) == 0 for d in range(len(grid))])

    def last():
        return functools.reduce(lambda a, b: a & b, [pl.program_id(d) == grid[d] - 1 for d in range(len(grid))])

    return split, first, last


def _carried_call(body, ex, grid, in_specs, out_specs, out_shape, scratch, sem, name, operands, vmem=None):
    any_spec = pl.BlockSpec(memory_space=pl.ANY)
    split, first, last = _carry(ex, len(in_specs), len(out_specs), len(scratch), grid)

    def carried(*refs):
        own, xrefs = split(refs)
        if ex:
            @pl.when(first())
            def _():
                ex.start(*xrefs)

        body(*own)
        if ex:
            @pl.when(last())
            def _():
                ex.drain(*xrefs)

    n_xin = len(ex.inputs) if ex else 0
    results = pl.pallas_call(
        carried, grid=grid, in_specs=list(in_specs) + [any_spec] * n_xin,
        out_specs=list(out_specs) + [any_spec] * (len(ex.out_shapes) if ex else 0),
        out_shape=list(out_shape) + (list(ex.out_shapes) if ex else []),
        input_output_aliases={len(in_specs) + i: len(out_specs) + o for i, o in ex.aliases.items()} if ex else {},
        scratch_shapes=list(scratch) + (ex.scratch if ex else []),
        compiler_params=_cparams(sem, vmem), name=name)(*operands, *(ex.inputs if ex else []))
    return results[:len(out_specs)], results[len(out_specs):]


def _tri_rows(t, n):
    qi = sum(jnp.where(t >= r * (r + 1) // 2, 1, 0) for r in range(1, n))
    return qi, t - qi * (qi + 1) // 2


def _tri_cols(t, n):
    ki = sum(jnp.where(t >= r * n - r * (r - 1) // 2, 1, 0) for r in range(1, n))
    return ki, ki + t - (ki * n - ki * (ki - 1) // 2)


def _causal_t(st_blk, tk, tq):
    key = lax.broadcasted_iota(jnp.int32, (tk, tq), 0)
    qry = lax.broadcasted_iota(jnp.int32, (tk, tq), 1)
    return jnp.where(qry >= key, st_blk, -jnp.inf)


def attn_forward(qa, ka, vt, name, ex=None):
    hh, s, _ = qa.shape
    tq = tk = _tile(s, 512)
    nq = s // tq
    grp = HEAD_GROUP_FWD

    def body(q_ref, k_ref, vt_ref, o_ref, lse_ref, m_ref, acc_ref):
        qi, ki = _tri_rows(pl.program_id(1), nq)

        @pl.when(ki == 0)
        def _():
            m_ref[...] = jnp.full_like(m_ref, -jnp.inf)
            acc_ref[...] = jnp.zeros_like(acc_ref)

        def step(masked):
            nxt = _dot_nt(k_ref[0], q_ref[0])
            for g in range(grp):
                st = nxt
                if g + 1 < grp:
                    nxt = _dot_nt(k_ref[g + 1], q_ref[g + 1])
                if masked:
                    st = _causal_t(st, tk, tq)
                m_old = m_ref[g]
                m_new = jnp.maximum(m_old, jnp.max(st, axis=0, keepdims=True))
                pt = jnp.exp(st - m_new).astype(BF16)
                acc_ref[g] = jnp.exp(m_old - m_new) * acc_ref[g] + _dot(vt_ref[g], pt)
                m_ref[g] = m_new

        @pl.when(ki < qi)
        def _():
            step(False)

        @pl.when(ki == qi)
        def _():
            step(True)
            for g in range(grp):
                acc = acc_ref[g]
                denom = acc[LANE_C:LANE_C + 1, :]
                o_ref[g] = (acc / denom).T.astype(BF16)
                lse_ref[g] = m_ref[g] + jnp.log(denom)

    qspec = pl.BlockSpec((grp, tq, 128), lambda h, t: (h, _tri_rows(t, nq)[0], 0))
    kspec = pl.BlockSpec((grp, tk, 128), lambda h, t: (h, _tri_rows(t, nq)[1], 0))
    vspec = pl.BlockSpec((grp, 128, tk), lambda h, t: (h, 0, _tri_rows(t, nq)[1]))
    lspec = pl.BlockSpec((grp, 1, tq), lambda h, t: (h, 0, _tri_rows(t, nq)[0]))
    return _carried_call(
        body, ex, (hh // grp, nq * (nq + 1) // 2), [qspec, kspec, vspec], [qspec, lspec],
        [jax.ShapeDtypeStruct((hh, s, 128), BF16), jax.ShapeDtypeStruct((hh, 1, s), F32)],
        [pltpu.VMEM((grp, 1, tq), F32), pltpu.VMEM((grp, 128, tq), F32)],
        ("arbitrary", "arbitrary"), name, (qa, ka, vt))


def attn_backward(qa, ka, va, oa, doa, lse, name, ex=None):
    hh, s, _ = qa.shape
    tq = tk = _tile(s, 512)
    nq = s // tq
    grp = HEAD_GROUP_BWD

    def body(q_ref, k_ref, v_ref, o_ref, do_ref, lse_ref, dq_ref, dk_ref, dv_ref, dka_ref, dva_ref):
        ki, qi = _tri_cols(pl.program_id(1), nq)

        @pl.when(pl.program_id(1) == 0)
        def _():
            dq_ref[...] = jnp.zeros_like(dq_ref)

        @pl.when(qi == ki)
        def _():
            dka_ref[...] = jnp.zeros_like(dka_ref)
            dva_ref[...] = jnp.zeros_like(dva_ref)

        def step(masked):
            rows = pl.ds(pl.multiple_of(qi * tq, tq), tq)
            products = lambda g: (_dot_nt(k_ref[g], q_ref[g]), _dot_nt(v_ref[g], do_ref[g]))
            nxt = products(0)
            for g in range(grp):
                st, dpt = nxt
                if g + 1 < grp:
                    nxt = products(g + 1)
                q, k, do = q_ref[g], k_ref[g], do_ref[g]
                if masked:
                    st = _causal_t(st, tk, tq)
                pt = jnp.exp(st - lse_ref[g])
                delta = jnp.sum((do.astype(F32) * o_ref[g].astype(F32)).T, axis=0, keepdims=True)
                dst = (pt * (dpt - delta)).astype(BF16)
                dva_ref[g] += _dot(pt.astype(BF16), do)
                dka_ref[g] += _dot(dst, q)
                dq_ref[g, rows, :] += _dot_tn(dst, k)

        def diagonal_step():
            hk, hq = tk // 2, tq // 2
            base = pl.multiple_of(qi * tq, tq)
            products = lambda g: (_dot_nt(k_ref[g, :hk, :], q_ref[g]), _dot_nt(k_ref[g, hk:, :], q_ref[g, hq:, :]),
                                  _dot_nt(v_ref[g, :hk, :], do_ref[g]), _dot_nt(v_ref[g, hk:, :], do_ref[g, hq:, :]))
            nxt = products(0)
            for g in range(grp):
                st_a, st_b, dpt_a, dpt_b = nxt
                if g + 1 < grp:
                    nxt = products(g + 1)
                q, do = q_ref[g], do_ref[g]
                q_b, do_b = q_ref[g, hq:, :], do_ref[g, hq:, :]
                k_a, k_b = k_ref[g, :hk, :], k_ref[g, hk:, :]
                lse = lse_ref[g]
                delta = jnp.sum((do.astype(F32) * o_ref[g].astype(F32)).T, axis=0, keepdims=True)
                pt_a = jnp.exp(_causal_t(st_a, hk, tq) - lse)
                pt_b = jnp.exp(_causal_t(st_b, hk, tq - hq) - lse[:, hq:])
                dst_a = (pt_a * (dpt_a - delta)).astype(BF16)
                dst_b = (pt_b * (dpt_b - delta[:, hq:])).astype(BF16)
                dva_ref[g, :hk, :] += _dot(pt_a.astype(BF16), do)
                dva_ref[g, hk:, :] += _dot(pt_b.astype(BF16), do_b)
                dka_ref[g, :hk, :] += _dot(dst_a, q)
                dka_ref[g, hk:, :] += _dot(dst_b, q_b)
                dq_ref[g, pl.ds(base, tq), :] += _dot_tn(dst_a, k_a)
                dq_ref[g, pl.ds(pl.multiple_of(base + hq, hq), tq - hq), :] += _dot_tn(dst_b, k_b)

        @pl.when(qi > ki)
        def _():
            step(False)

        @pl.when(qi == ki)
        def _():
            diagonal_step()

        @pl.when(qi == nq - 1)
        def _():
            dk_ref[...] = dka_ref[...]
            dv_ref[...] = dva_ref[...].astype(BF16)

    qspec = pl.BlockSpec((grp, tq, 128), lambda h, t: (h, _tri_cols(t, nq)[1], 0))
    lspec = pl.BlockSpec((grp, 1, tq), lambda h, t: (h, 0, _tri_cols(t, nq)[1]))
    kspec = pl.BlockSpec((grp, tk, 128), lambda h, t: (h, _tri_cols(t, nq)[0], 0))
    return _carried_call(
        body, ex, (hh // grp, nq * (nq + 1) // 2), [qspec, kspec, kspec, qspec, qspec, lspec],
        [pl.BlockSpec((grp, s, 128), lambda h, t: (h, 0, 0), pipeline_mode=pl.Buffered(1)), kspec, kspec],
        [jax.ShapeDtypeStruct((hh, s, 128), F32), jax.ShapeDtypeStruct((hh, s, 128), F32),
         jax.ShapeDtypeStruct((hh, s, 128), BF16)],
        [pltpu.VMEM((grp, tk, 128), F32), pltpu.VMEM((grp, tk, 128), F32)],
        ("arbitrary", "arbitrary"), name, (qa, ka, va, oa, doa, lse), vmem=58 * 2 ** 20)


def attn_post(dqa, dka, dva, proj, z, gq2, gk2, dproj, name):
    s = proj.shape[0]
    tm = _tile(s, 512)
    nt = s // tm

    def body(dq_ref, dk_ref, dv_ref, q_ref, k_ref, z_ref, gq_ref, gk_ref, dp_any, dp_ref, dgq_ref, dgk_ref, db_ref,
             carry_ref):
        lane, lo = _lanes()

        @pl.when(pl.program_id(0) == 0)
        def _():
            dgq_ref[...] = jnp.zeros_like(dgq_ref)
            dgk_ref[...] = jnp.zeros_like(dgk_ref)
            db_ref[...] = jnp.zeros_like(db_ref)
            carry_ref[...] = jnp.zeros_like(carry_ref)

        def pair(ref, j):
            return jnp.where(lo, ref[2 * j].astype(F32), _swap_halves(ref[2 * j + 1].astype(F32)))

        def norm_bwd(raw, g, dhat, scale):
            r = lax.rsqrt(_half_mean(raw * raw, lo) + EPS)
            y = raw * r
            dy = dhat * (g * scale)
            return r * (dy - y * _half_mean(dy * y, lo)), jnp.sum(dhat * y, axis=0, keepdims=True) * scale

        dc = jnp.zeros((tm, 128), F32)
        for j in range(HEADS // 2):
            cols = slice(128 * j, 128 * (j + 1))
            dq, dgq = norm_bwd(q_ref[:, cols].astype(F32), gq_ref[...], pair(dq_ref, j), ATTN_SCALE)
            dk, dgk = norm_bwd(k_ref[:, cols].astype(F32), gk_ref[...], pair(dk_ref, j), 1.0)
            dgq_ref[...] += dgq
            dgk_ref[...] += dgk
            dp_ref[:, cols] = dq.astype(BF16)
            dp_ref[:, D_ATTN + 128 * j:D_ATTN + 128 * (j + 1)] = dk.astype(BF16)
            dp_ref[:, 2 * D_ATTN + 128 * j:2 * D_ATTN + 128 * (j + 1)] = pair(dv_ref, j).astype(BF16)
            for e in range(2):
                h = 2 * j + e
                both = jnp.where(lane == LANE_C, dq_ref[h], 0.0) - jnp.where(lane == LANE_ONE, dk_ref[h], 0.0)
                dc = jnp.where(lane == h, jnp.sum(both, axis=-1, keepdims=True), dc)
        dz = _running_sum(dc, carry_ref, reverse=True) * (1.0 - _sigmoid(z_ref[...]))
        db_ref[...] += jnp.sum(dz, axis=0, keepdims=True)
        dp_ref[:, 3 * D_ATTN:3 * D_ATTN + 128] = dz.astype(BF16)
        dp_ref[:, 3 * D_ATTN + 128:] = jnp.zeros((tm, DPROJ_TAIL - 3 * D_ATTN - 128), BF16)

    heads = lambda: pl.BlockSpec((HEADS, tm, 128), lambda i: (0, nt - 1 - i, 0))
    vec = pl.BlockSpec((1, 128), lambda i: (0, 0))
    first = N_REST // D_ATTN
    return pl.pallas_call(
        body, grid=(nt,),
        in_specs=[heads(), heads(), heads(), pl.BlockSpec((tm, D_ATTN), lambda i: (nt - 1 - i, first)),
                  pl.BlockSpec((tm, D_ATTN), lambda i: (nt - 1 - i, first + 1)),
                  pl.BlockSpec((tm, 128), lambda i: (nt - 1 - i, 0)), vec, vec, pl.BlockSpec(memory_space=pl.ANY)],
        out_specs=[pl.BlockSpec((tm, DPROJ_TAIL), lambda i: (nt - 1 - i, N_REST // DPROJ_TAIL)), vec, vec, vec],
        out_shape=[jax.ShapeDtypeStruct(dproj.shape, BF16), jax.ShapeDtypeStruct((1, 128), F32),
                   jax.ShapeDtypeStruct((1, 128), F32), jax.ShapeDtypeStruct((1, 128), F32)],
        scratch_shapes=[pltpu.VMEM((1, 128), F32)], input_output_aliases={8: 0},
        compiler_params=_cparams(("arbitrary",)), name=name)(dqa, dka, dva, proj, proj, z, gq2, gk2, dproj)


def _pool_groups(tm):
    gid = lax.broadcasted_iota(jnp.int32, (1, D_POOL), 1) // (D_POOL // 4)
    win = jnp.where(gid == 0, 2.0, jnp.where(gid == 1, 4.0, jnp.where(gid == 2, 8.0, 16.0)))
    return gid, win


def _by_group(gid, v2, v4, v8, v16):
    return jnp.where(gid == 0, v2, jnp.where(gid == 1, v4, jnp.where(gid == 2, v8, v16)))


def _branches(rest_ref, halo_ref, a_ref, wa_ref, wc_ref, wp_ref, sc_ref, cw_ref, ti, tm):
    f = lambda v: v.astype(F32)
    cx, cb, cc, px = f(rest_ref[:, 0:256]), f(rest_ref[:, 256:512]), f(rest_ref[:, 512:768]), f(rest_ref[:, 768:1024])
    live = jnp.where(ti > 0, 1.0, 0.0)
    hz = f(halo_ref[:, 0:256]) * f(halo_ref[:, 512:768]) * live
    hp = f(halo_ref[:, 768:1024]) * live
    z = cc * cx
    zf = jnp.concatenate([hz, z], axis=0)
    z1 = pltpu.roll(zf, 1, 0)[HALO:]
    z2 = pltpu.roll(zf, 2, 0)[HALO:]
    cw = cw_ref[...]
    conv = cw[2:3] * z + cw[1:2] * z1 + cw[0:1] * z2
    uc = cb * conv
    pf = jnp.concatenate([hp, px], axis=0)
    s2 = pf + pltpu.roll(pf, 1, 0)
    s4 = s2 + pltpu.roll(s2, 2, 0)
    s8 = s4 + pltpu.roll(s4, 4, 0)
    s16 = s8 + pltpu.roll(s8, 8, 0)
    gid, win = _pool_groups(tm)
    t = (ti * tm + lax.broadcasted_iota(jnp.int32, (tm, 1), 0)).astype(F32)
    inv = 1.0 / jnp.minimum(t + 1.0, win)
    dpool = _by_group(gid, s2[HALO:], s4[HALO:], s8[HALO:], s16[HALO:]) * inv - px
    _, lo = _lanes()
    a_tok = [jnp.where(lo, f(a_ref[2 * j]), _swap_halves(f(a_ref[2 * j + 1]))).astype(BF16) for j in range(HEADS // 2)]
    y_attn = _dot(a_tok[0], wa_ref[0:128, :])
    for j in range(1, HEADS // 2):
        y_attn += _dot(a_tok[j], wa_ref[128 * j:128 * (j + 1), :])
    y_conv = _dot(uc.astype(BF16), wc_ref[...])
    y_pool_raw = _dot(dpool.astype(BF16), wp_ref[...])
    sg = [_sigmoid(f(rest_ref[:, 1024 + i * D_MODEL:1024 + (i + 1) * D_MODEL])) for i in range(3)]
    return dict(cx=cx, cb=cb, cc=cc, z=z, z1=z1, z2=z2, conv=conv, uc=uc, dpool=dpool, inv=inv, gid=gid, a_tok=a_tok,
                y_attn=y_attn, y_conv=y_conv, y_pool_raw=y_pool_raw, sg=sg, cw=cw)


def _mix_specs(tm, ti_of):
    blocks_per_tile = tm // HALO
    return [
        pl.BlockSpec((tm, N_REST), lambda i: (ti_of(i), 0)),
        pl.BlockSpec((HALO, 1024), lambda i: (jnp.maximum(ti_of(i) * blocks_per_tile - 1, 0), 0)),
        pl.BlockSpec((HEADS, tm, 128), lambda i: (0, ti_of(i), 0)),
        pl.BlockSpec((D_ATTN, D_MODEL), lambda i: (0, 0), pipeline_mode=pl.Buffered(1)),
        pl.BlockSpec((D_CONV, D_MODEL), lambda i: (0, 0), pipeline_mode=pl.Buffered(1)),
        pl.BlockSpec((D_POOL, D_MODEL), lambda i: (0, 0), pipeline_mode=pl.Buffered(1)),
        pl.BlockSpec((1, D_MODEL), lambda i: (0, 0)),
        pl.BlockSpec((8, D_CONV), lambda i: (0, 0)),
    ]


def mix_fwd(proj, a, x, wa, wc, wp, scale, cw, wo, name, ex=None):
    s = x.shape[0]
    tm = _tile(s, 512)

    def body(rest_ref, halo_ref, a_ref, wa_ref, wc_ref, wp_ref, sc_ref, cw_ref, wo_ref, x_ref, o_ref):
        b = _branches(rest_ref, halo_ref, a_ref, wa_ref, wc_ref, wp_ref, sc_ref, cw_ref, pl.program_id(0), tm)
        merged = b["sg"][0] * b["y_attn"] + b["sg"][1] * b["y_conv"] + b["sg"][2] * (b["y_pool_raw"] * sc_ref[...])
        o_ref[...] = x_ref[...] + _dot(merged.astype(BF16), wo_ref[...])

    (x1,), carried = _carried_call(
        body, ex, (s // tm,),
        _mix_specs(tm, lambda i: i) + [pl.BlockSpec((D_MODEL, D_MODEL), lambda i: (0, 0), pipeline_mode=pl.Buffered(1)),
                                       pl.BlockSpec((tm, D_MODEL), lambda i: (i, 0))],
        [pl.BlockSpec((tm, D_MODEL), lambda i: (i, 0))], [jax.ShapeDtypeStruct((s, D_MODEL), F32)], [],
        ("arbitrary",), name, (proj, proj, a, wa, wc, wp, scale, cw, wo, x), vmem=58 * 2 ** 20)
    return x1, carried


def mix_bwd(proj, a, dx1, wa, wc, wp, scale, cw, wo, name):
    s = dx1.shape[0]
    tm = _tile(s, 512)
    nt = s // tm
    ti_of = lambda i: nt - 1 - i
    n = tm + HALO

    def body(rest_ref, halo_ref, a_ref, wa_ref, wc_ref, wp_ref, sc_ref, cw_ref, wo_ref,
             dx_ref, dp_ref, da_ref, at_ref, mg_ref, dya_ref, dyc_ref, dyp_ref, uc_ref, dd_ref, dsc_ref, dcw_ref,
             cdc_ref, cde_ref):
        i = pl.program_id(0)
        ti = ti_of(i)

        @pl.when(i == 0)
        def _():
            cdc_ref[...] = jnp.zeros_like(cdc_ref)
            cde_ref[...] = jnp.zeros_like(cde_ref)
            dsc_ref[...] = jnp.zeros_like(dsc_ref)
            dcw_ref[...] = jnp.zeros_like(dcw_ref)

        b = _branches(rest_ref, halo_ref, a_ref, wa_ref, wc_ref, wp_ref, sc_ref, cw_ref, ti, tm)
        sg, sc = b["sg"], sc_ref[...]
        y_pool = b["y_pool_raw"] * sc
        merged = sg[0] * b["y_attn"] + sg[1] * b["y_conv"] + sg[2] * y_pool
        mg_ref[...] = merged.astype(BF16)
        dm = _dot_nt(dx_ref[...].astype(BF16), wo_ref[...])
        dys = [dm * sg[j] for j in range(3)]
        for j, y in enumerate((b["y_attn"], b["y_conv"], y_pool)):
            dp_ref[:, 1024 + j * D_MODEL:1024 + (j + 1) * D_MODEL] = (dys[j] * y * (1.0 - sg[j])).astype(BF16)
        dya = dys[0].astype(BF16)
        dya_ref[...] = dya
        _, lo = _lanes()
        for j in range(HEADS // 2):
            at_ref[:, 128 * j:128 * (j + 1)] = b["a_tok"][j]
            da = _dot_nt(dya, wa_ref[128 * j:128 * (j + 1), :])
            da_ref[2 * j] = jnp.where(lo, da, 0.0).astype(BF16)
            da_ref[2 * j + 1] = jnp.where(lo, _swap_halves(da), 0.0).astype(BF16)
        dyc = dys[1].astype(BF16)
        dyc_ref[...] = dyc
        duc = _dot_nt(dyc, wc_ref[...])
        dyp = dys[2]
        dsc_ref[...] += jnp.sum(dyp * b["y_pool_raw"], axis=0, keepdims=True)
        dypr = (dyp * sc).astype(BF16)
        dyp_ref[...] = dypr
        ddp = _dot_nt(dypr, wp_ref[...])
        uc_ref[...] = b["uc"].astype(BF16)
        dd_ref[...] = b["dpool"].astype(BF16)

        dconv = duc * b["cb"]
        dp_ref[:, 256:512] = (duc * b["conv"]).astype(BF16)
        dcf = jnp.concatenate([dconv, cdc_ref[...]], axis=0)
        cw = b["cw"]
        dz = cw[2:3] * dconv + cw[1:2] * pltpu.roll(dcf, n - 1, 0)[:tm] + cw[0:1] * pltpu.roll(dcf, n - 2, 0)[:tm]
        dp_ref[:, 0:256] = (dz * b["cc"]).astype(BF16)
        dp_ref[:, 512:768] = (dz * b["cx"]).astype(BF16)
        dcw_ref[0:1, :] += jnp.sum(dconv * b["z2"], axis=0, keepdims=True)
        dcw_ref[1:2, :] += jnp.sum(dconv * b["z1"], axis=0, keepdims=True)
        dcw_ref[2:3, :] += jnp.sum(dconv * b["z"], axis=0, keepdims=True)
        cdc_ref[...] = dconv[:HALO]

        e = ddp * b["inv"]
        ef = jnp.concatenate([e, cde_ref[...]], axis=0)
        r2 = ef + pltpu.roll(ef, n - 1, 0)
        r4 = r2 + pltpu.roll(r2, n - 2, 0)
        r8 = r4 + pltpu.roll(r4, n - 4, 0)
        r16 = r8 + pltpu.roll(r8, n - 8, 0)
        dp_ref[:, 768:1024] = (_by_group(b["gid"], r2[:tm], r4[:tm], r8[:tm], r16[:tm]) - ddp).astype(BF16)
        cde_ref[...] = e[:HALO]

    tile = lambda w: pl.BlockSpec((tm, w), lambda i: (ti_of(i), 0))
    whole = lambda r, c: pl.BlockSpec((r, c), lambda i: (0, 0))
    bf = lambda w: jax.ShapeDtypeStruct((s, w), BF16)
    return pl.pallas_call(
        body, grid=(nt,),
        in_specs=_mix_specs(tm, ti_of) + [pl.BlockSpec((D_MODEL, D_MODEL), lambda i: (0, 0), pipeline_mode=pl.Buffered(1)),
                                          tile(D_MODEL)],
        out_specs=[tile(N_REST), pl.BlockSpec((HEADS, tm, 128), lambda i: (0, ti_of(i), 0)), tile(D_ATTN),
                   tile(D_MODEL), tile(D_MODEL), tile(D_MODEL), tile(D_MODEL),
                   tile(D_CONV), tile(D_POOL), whole(1, D_MODEL), whole(8, D_CONV)],
        out_shape=[bf(DPROJ_COLS), jax.ShapeDtypeStruct((HEADS, s, 128), BF16), bf(D_ATTN),
                   bf(D_MODEL), bf(D_MODEL), bf(D_MODEL), bf(D_MODEL), bf(D_CONV), bf(D_POOL),
                   jax.ShapeDtypeStruct((1, D_MODEL), F32), jax.ShapeDtypeStruct((8, D_CONV), F32)],
        scratch_shapes=[pltpu.VMEM((HALO, D_CONV), F32), pltpu.VMEM((HALO, D_POOL), F32)],
        compiler_params=_cparams(("arbitrary",), 58 * 2 ** 20), name=name)(proj, proj, a, wa, wc, wp, scale, cw, wo, dx1)


def _adamw_math(w, g, m, v):
    m = ADAM_B1 * m + (1.0 - ADAM_B1) * g
    v = ADAM_B2 * v + (1.0 - ADAM_B2) * (g * g)
    m_hat = m / (1.0 - ADAM_B1 ** ADAM_STEP)
    v_hat = v / (1.0 - ADAM_B2 ** ADAM_STEP)
    delta = -ADAM_LR * (m_hat / (jnp.sqrt(v_hat) + ADAM_EPS) + ADAM_WD * w)
    return delta, m, v


ADAMW_PARTS_BLOCK_BYTES = 4 * 2 ** 20


def _row_tile(rows, cols, copies, itemsize):
    row_bytes = copies * (-(-cols // 128) * 128) * itemsize
    fits = [t for t in range(16, rows + 1, 16) if rows % t == 0 and t * row_bytes <= ADAMW_PARTS_BLOCK_BYTES]
    return max(fits) if fits else rows


def pair_sum(blocks, stage, me, name):
    n_slots, rows, cols = stage.shape
    tr = _row_tile(rows, cols, 1, 4)

    def body(me_ref, a_ref, b_ref, o_ref):
        o_ref[...] = (a_ref[...].astype(F32) + b_ref[...].astype(F32)).astype(BF16)

    slot = pl.BlockSpec((None, tr, cols), lambda i, r, me_ref: (i, r, 0))
    return pl.pallas_call(
        body, out_shape=jax.ShapeDtypeStruct(stage.shape, BF16),
        grid_spec=pltpu.PrefetchScalarGridSpec(
            num_scalar_prefetch=1, grid=(n_slots, rows // tr),
            in_specs=[pl.BlockSpec((None, tr, cols), lambda i, r, me_ref: (me_ref[0] ^ (2 * i), r, 0)), slot],
            out_specs=slot),
        compiler_params=_cparams(("parallel", "parallel")), name=name)(me.reshape(1), blocks, stage)


def adamw_sum(parts, w, m, v, name):
    layers, rows, cols = w.shape
    n_parts = parts.shape[1]
    if rows % 16 == 0:
        tr, tc = _row_tile(rows, cols, n_parts, parts.dtype.itemsize), cols
    else:
        tr, tc = rows, _pick(cols, (256, 128))

    def body(p_ref, w_ref, m_ref, v_ref, g_ref, d_ref, nm_ref, nv_ref):
        g = p_ref[0].astype(F32)
        for i in range(1, n_parts):
            g = g + p_ref[i].astype(F32)
        g_ref[...] = g
        d_ref[...], nm_ref[...], nv_ref[...] = _adamw_math(w_ref[...], g, m_ref[...], v_ref[...])

    spec = pl.BlockSpec((None, tr, tc), lambda l, i, j: (l, i, j))
    return pl.pallas_call(
        body, grid=(layers, rows // tr, cols // tc),
        in_specs=[pl.BlockSpec((None, n_parts, tr, tc), lambda l, i, j: (l, 0, i, j)), spec, spec, spec],
        out_specs=[spec] * 4, out_shape=[jax.ShapeDtypeStruct((layers, rows, cols), F32)] * 4,
        compiler_params=_cparams(("parallel", "parallel", "parallel")), name=name)(parts, w, m, v)


def _me():
    return lax.axis_index("x"), lax.axis_index("y"), lax.axis_index("c")


N_PEERS = N_DEV - 1


def all_gather(shards, name):
    n = len(shards)
    any_spec = pl.BlockSpec(memory_space=pl.ANY)

    def body(*refs):
        x_refs, out_refs = refs[:n], refs[n:2 * n]
        send_sems, recv_sems, local_sems = refs[2 * n:]
        x, y, c = _me()
        me, sibling = (x, y, c), (x, y, 1 - c)
        chips = [(1 - x, y), (x, 1 - y), (1 - x, 1 - y)]

        def copy(t, k, block, to, from_input=False):
            slot = out_refs[t].at[4 * block[0] + 2 * block[1] + block[2]]
            return pltpu.make_async_remote_copy(
                src_ref=x_refs[t] if from_input else slot, dst_ref=slot, send_sem=send_sems.at[N_PEERS * t + k],
                recv_sem=recv_sems.at[N_PEERS * t + k], device_id=to, device_id_type=pl.DeviceIdType.MESH)

        mine = [pltpu.make_async_copy(x_refs[t], out_refs[t].at[4 * x + 2 * y + c], local_sems.at[t]) for t in range(n)]
        started = []
        for t in range(n):
            mine[t].start()
            started.append(copy(t, 0, me, sibling, from_input=True))
            started += [copy(t, 1 + j, me, (*chip, c), from_input=True) for j, chip in enumerate(chips)]
        for cp in started:
            cp.start()
        for j, chip in enumerate(chips):
            for t in range(n):
                copy(t, 1 + j, (*chip, c), me).wait_recv()
                fwd = copy(t, 4 + j, (*chip, c), sibling)
                fwd.start()
                started.append(fwd)
        for t in range(n):
            copy(t, 0, sibling, me).wait_recv()
            for j, chip in enumerate(chips):
                copy(t, 4 + j, (*chip, 1 - c), me).wait_recv()
        for cp in started:
            cp.wait_send()
        for cp in mine:
            cp.wait()

    return pl.pallas_call(
        body, out_shape=[jax.ShapeDtypeStruct((N_DEV,) + s.shape, s.dtype) for s in shards],
        in_specs=[any_spec] * n, out_specs=[any_spec] * n,
        scratch_shapes=[pltpu.SemaphoreType.DMA((N_PEERS * n,)), pltpu.SemaphoreType.DMA((N_PEERS * n,)),
                        pltpu.SemaphoreType.DMA((n,))],
        name=name)(*shards)


SIBLING = 1
OTHER_CHIPS = (2, 4, 6)
SAME_CORE = (0,) + OTHER_CHIPS


class Exchange:
    def __init__(self, inputs, out_shapes, aliases, copies, local=()):
        self.inputs, self.out_shapes, self.aliases = list(inputs), list(out_shapes), aliases
        self._copies, self._local = list(copies), list(local)
        self.scratch = [pltpu.SemaphoreType.DMA((len(self._copies),)), pltpu.SemaphoreType.DMA((len(self._copies),)),
                        pltpu.SemaphoreType.DMA((max(len(self._local), 1),))]

    def _build(self, ins, outs, sems):
        send_sems, recv_sems, local_sems = sems
        x, y, c = _me()
        me = 4 * x + 2 * y + c
        local = [functools.partial(pltpu.make_async_copy, src(ins, outs, me), dst(outs, me), local_sems.at[i])
                 for i, (src, dst) in enumerate(self._local)]
        sends, recvs = [], []
        for i, (mask, src, dst) in enumerate(self._copies):
            px, py, pc = x ^ ((mask >> 2) & 1), y ^ ((mask >> 1) & 1), c ^ (mask & 1)
            pair = dict(send_sem=send_sems.at[i], recv_sem=recv_sems.at[i], device_id_type=pl.DeviceIdType.MESH)
            sends.append(functools.partial(
                pltpu.make_async_remote_copy, src_ref=src(ins, outs, me), dst_ref=dst(outs, me), device_id=(px, py, pc), **pair))
            recvs.append(functools.partial(
                pltpu.make_async_remote_copy, src_ref=src(ins, outs, me), dst_ref=dst(outs, me ^ mask), device_id=(x, y, c), **pair))
        return local, sends, recvs

    def start(self, ins, outs, sems):
        local, sends, _ = self._build(ins, outs, sems)
        for make in local + sends:
            make().start()

    def drain(self, ins, outs, sems):
        local, sends, recvs = self._build(ins, outs, sems)
        for make in recvs:
            make().wait_recv()
        for make in sends:
            make().wait_send()
        for make in local:
            make().wait()


def _bind(fn, *args):
    return functools.partial(fn, *args)


def join_exchanges(a, b):
    if a is None or b is None:
        return a or b
    na_in, na_out = len(a.inputs), len(a.out_shapes)

    def src_a(fn):
        return lambda ins, outs, me: fn(ins[:na_in], outs[:na_out], me)

    def dst_a(fn):
        return lambda outs, who: fn(outs[:na_out], who)

    def src_b(fn):
        return lambda ins, outs, me: fn(ins[na_in:], outs[na_out:], me)

    def dst_b(fn):
        return lambda outs, who: fn(outs[na_out:], who)

    copies = [(m, src_a(s), dst_a(d)) for m, s, d in a._copies] + [(m, src_b(s), dst_b(d)) for m, s, d in b._copies]
    local = [(src_a(s), dst_a(d)) for s, d in a._local] + [(src_b(s), dst_b(d)) for s, d in b._local]
    aliases = dict(a.aliases)
    aliases.update({na_in + i: na_out + o for i, o in b.aliases.items()})
    return Exchange(a.inputs + b.inputs, a.out_shapes + b.out_shapes, aliases, copies, local)


def gather_over_ici(shards):
    copies = [(mask, _bind(lambda t, ins, outs, me: ins[t], t), _bind(lambda t, outs, sender: outs[t].at[sender], t))
              for t in range(len(shards)) for mask in OTHER_CHIPS]
    local = [(_bind(lambda t, ins, outs, me: ins[t], t), _bind(lambda t, outs, me: outs[t].at[me], t))
             for t in range(len(shards))]
    return Exchange(shards, [jax.ShapeDtypeStruct((N_DEV,) + s.shape, s.dtype) for s in shards], {}, copies, local)


def gather_over_d2d(gathered):
    copies = [(SIBLING, _bind(lambda t, m, ins, outs, me: outs[t].at[me ^ m], t, m),
               _bind(lambda t, m, outs, sender: outs[t].at[sender ^ m], t, m))
              for t in range(len(gathered)) for m in SAME_CORE]
    return Exchange(gathered, [jax.ShapeDtypeStruct(g.shape, g.dtype) for g in gathered],
                    {t: t for t in range(len(gathered))}, copies)


def scatter_over_d2d(blocks):
    copies = [(SIBLING, _bind(lambda t, m, ins, outs, me: ins[t].at[me ^ SIBLING ^ m], t, m),
               _bind(lambda t, i, outs, sender: outs[t].at[i], t, i))
              for t in range(len(blocks)) for i, m in enumerate(SAME_CORE)]
    return Exchange(blocks, [jax.ShapeDtypeStruct((len(SAME_CORE),) + b.shape[1:], b.dtype) for b in blocks], {}, copies)


def scatter_over_ici(pair_sums, bufs, layer):
    n = len(pair_sums)
    copies = [(m, _bind(lambda t, i, ins, outs, me: ins[t].at[i], t, i),
               _bind(lambda t, i, outs, sender: outs[t].at[layer, i], t, i))
              for t in range(n) for i, m in enumerate(SAME_CORE) if m]
    local = [(_bind(lambda t, ins, outs, me: ins[t].at[0], t), _bind(lambda t, outs, me: outs[t].at[layer, 0], t))
             for t in range(n)]
    return Exchange(list(pair_sums) + list(bufs), [jax.ShapeDtypeStruct(b.shape, b.dtype) for b in bufs],
                    {n + t: t for t in range(n)}, copies, local)


def run_exchange(ex, name):
    any_spec = pl.BlockSpec(memory_space=pl.ANY)
    n_in, n_out = len(ex.inputs), len(ex.out_shapes)

    def body(*refs):
        ins, outs, sems = refs[:n_in], refs[n_in:n_in + n_out], refs[n_in + n_out:]
        ex.start(ins, outs, sems)
        ex.drain(ins, outs, sems)

    return pl.pallas_call(
        body, out_shape=ex.out_shapes, in_specs=[any_spec] * n_in, out_specs=[any_spec] * n_out,
        input_output_aliases=ex.aliases, scratch_shapes=ex.scratch, name=name)(*ex.inputs)


MATRICES = ("w_in", "w_attn_out", "w_conv_out", "pool_w", "w_o", "w_ffn_in", "w_ffn_out")
TRANSPOSED = ("w_in", "w_ffn_in")
EVERY = tuple(range(len(MATRICES)))
IN_PROJ_PART, ATTN_PART, MIX_PART = (0,), (1, 2, 3, 4, 5), (6,)
LATE = (0,)
EARLY = EVERY[1:]
EARLY_FIRST, EARLY_SECOND = (4, 6), (1, 2, 3, 5)
SHARD_INFO = {
    "w_in": ((DEPTH, D_IN // N_DEV, D_MODEL), 1),
    "w_attn_out": ((DEPTH, D_ATTN, D_MODEL // N_DEV), 2),
    "w_conv_out": ((DEPTH, D_CONV, D_MODEL // N_DEV), 2),
    "pool_w": ((DEPTH, 4, 64, 256 // N_DEV), 3),
    "w_o": ((DEPTH, D_MODEL // N_DEV, D_MODEL), 1),
    "w_ffn_in": ((DEPTH, 2 * D_FF // N_DEV, D_MODEL), 1),
    "w_ffn_out": ((DEPTH, D_FF // N_DEV, D_MODEL), 1),
}


def _handled(name, t):
    return jnp.transpose(t, (0, 2, 1)) if name in TRANSPOSED else t
VECTORS = ("norm_mix_g", "forget_b", "q_norm_g", "k_norm_g", "pool_scale", "norm_ffn_g")
VECTOR_SHAPES = {"norm_mix_g": (DEPTH, D_MODEL), "forget_b": (DEPTH, HEADS), "q_norm_g": (DEPTH, HEAD_DIM),
                 "k_norm_g": (DEPTH, HEAD_DIM), "pool_scale": (DEPTH, D_MODEL), "norm_ffn_g": (DEPTH, D_MODEL)}
CONV_W_FULL = (DEPTH, 3, D_CONV)


def _size(shape):
    n = 1
    for v in shape:
        n *= v
    return n


def _pack(arrays, rows, cols):
    flat = jnp.concatenate([a.reshape(-1) for a in arrays])
    return jnp.pad(flat, (0, rows * cols - flat.shape[0])).reshape(rows, cols)


def _unpack(packed, shapes):
    flat, out, off = packed.reshape(-1), [], 0
    for shp in shapes:
        out.append(flat[off:off + _size(shp)].reshape(shp))
        off += _size(shp)
    return out


def _join_shards(stacked, axis):
    moved = jnp.moveaxis(stacked, 0, axis)
    shp = list(moved.shape)
    shp[axis:axis + 2] = [shp[axis] * shp[axis + 1]]
    return moved.reshape(shp)


def _cut_shards(full, axis):
    shp = list(full.shape)
    shp[axis:axis + 1] = [N_DEV, shp[axis] // N_DEV]
    return jnp.moveaxis(full.reshape(shp), axis, 0)


N_MOVED = 1544
SHARD_ROWS = D_IN // N_DEV


def _regroup_w_in(shards):
    wt = shards.reshape(D_IN, shards.shape[2])
    pad = jnp.zeros((N_FULL - D_IN, wt.shape[1]), wt.dtype)
    return jnp.concatenate([wt[N_MOVED:], wt[:N_MOVED], pad], axis=0)


def _ungroup_w_in(wpt):
    def kernel_rows(a, b):
        if b <= N_MOVED:
            return [wpt[a + D_IN - N_MOVED:b + D_IN - N_MOVED]]
        if a >= N_MOVED:
            return [wpt[a - N_MOVED:b - N_MOVED]]
        return kernel_rows(a, N_MOVED) + kernel_rows(N_MOVED, b)

    return jnp.stack([jnp.concatenate(kernel_rows(s * SHARD_ROWS, (s + 1) * SHARD_ROWS), axis=0) for s in range(N_DEV)])


def _pool_block_diag(w):
    out = jnp.zeros((D_POOL, D_MODEL), w.dtype)
    for g in range(4):
        out = lax.dynamic_update_slice(out, w[g], (g * 64, g * 256))
    return out


def _pool_from_block_diag(wbd):
    return jnp.stack([wbd[g * 64:(g + 1) * 64, g * 256:(g + 1) * 256] for g in range(4)])


def _layer_weights(mats, vec, conv_w, l):
    wp = _pool_block_diag(mats["pool_w"])
    row = lambda v: v.reshape(1, -1)
    fb = jnp.zeros((1, 128), F32).at[0, :HEADS].set(vec["forget_b"][l])
    cw = jnp.zeros((8, D_CONV), F32).at[:3].set(conv_w[l])
    twice = lambda v: jnp.tile(v.reshape(1, -1), (1, 2))
    return dict(
        wt_in=_regroup_w_in(mats["w_in"]), wt_ffn_in=mats["w_ffn_in"], w_ffn_out=mats["w_ffn_out"],
        wa=mats["w_attn_out"], wc=mats["w_conv_out"], wp=wp, wo=mats["w_o"],
        g_mix=row(vec["norm_mix_g"][l]), g_ffn=row(vec["norm_ffn_g"][l]), gq2=twice(vec["q_norm_g"][l]),
        gk2=twice(vec["k_norm_g"][l]), scale=row(vec["pool_scale"][l]), fb=fb, cw=cw)


def _layer_fwd(x, w, l, comm):
    (proj, h), half_a = norm_matmul(x, w["g_mix"], w["wt_in"], N_MAIN, f"in_proj_{l}", comm.gather_ici(l + 1, IN_PROJ_PART))
    qa, ka, va, vt, z = attn_prep(proj, h, w["wt_in"], w["fb"], w["gq2"], w["gk2"], f"attn_prep_{l}")
    (oa, lse), half_b = attn_forward(qa, ka, vt, f"attn_fwd_{l}", comm.gather_ici(l + 1, ATTN_PART))
    x1, half_c = mix_fwd(proj, oa, x, w["wa"], w["wc"], w["wp"], w["scale"], w["cw"], w["wo"], f"mix_fwd_{l}",
                         comm.gather_ici(l + 1, MIX_PART))
    half = list(half_a) + list(half_b) + list(half_c)
    (gu, h2), gathered = norm_matmul(x1, w["g_ffn"], w["wt_ffn_in"], 2 * D_FF, f"ffn_in_{l}", comm.gather_d2d(l + 1, half))
    x2 = swiglu_matmul(gu, w["w_ffn_out"], x1, f"ffn_out_{l}")
    saved = dict(x=x, proj=proj, h=h, z=z, qa=qa, ka=ka, va=va, oa=oa, lse=lse, x1=x1, gu=gu, h2=h2)
    return x2, saved, gathered


def _layer_bwd(dx2, sv, w, l, comm):
    g = {}
    (dgu, act), stage = swiglu_bwd(dx2, sv["gu"], w["w_ffn_out"], f"ffn_out_bwd_{l}", comm.scatter_d2d(l + 1))
    sums = comm.pair_sums(l + 1, stage)
    g["w_ffn_out"] = tn_matmul(act, dx2, f"dw_ffn_out_{l}")
    g["w_ffn_in"] = tn_matmul(dgu, sv["h2"], f"dw_ffn_in_{l}")
    (dx1, dg), _ = matmul_normbwd(dgu, w["wt_ffn_in"], sv["x1"], w["g_ffn"], dx2, f"ffn_in_bwd_{l}")
    g["norm_ffn_g"] = dg[0]

    (dproj, doa, a_tok, merged, dya, dyc, dyp, uc, dd, dscale, dcw) = mix_bwd(
        sv["proj"], sv["oa"], dx1, w["wa"], w["wc"], w["wp"], w["scale"], w["cw"], w["wo"], f"mix_bwd_{l}")
    g["w_o"] = tn_matmul(merged, dx1, f"dw_o_{l}")
    g["w_attn_out"], g["w_conv_out"], dwp = tn_matmuls([(a_tok, dya), (uc, dyc), (dd, dyp)], f"dw_branches_{l}")
    g["pool_w"] = _pool_from_block_diag(dwp)
    g["pool_scale"] = dscale[0]
    g["conv_w"] = dcw[:3]

    early = comm.early(l)
    comm.grads(l, g)
    above = comm.scatter_ici(l + 1, sums)
    (dqa, dka, dva), got = attn_backward(sv["qa"], sv["ka"], sv["va"], sv["oa"], doa, sv["lse"], f"attn_bwd_{l}",
                                         join_exchanges(above, comm.scatter_d2d(l, early) if early else None))
    n_above = len(above.out_shapes) if above else 0
    comm.scattered(got[:n_above])
    early_sums = dict(zip(early, comm.pair_sums(l, got[n_above:], early))) if early else {}
    early_ici = lambda which: comm.scatter_ici(l, [early_sums[t] for t in which], which) if early else None
    dproj, dgq, dgk, db = attn_post(dqa, dka, dva, sv["proj"], sv["z"], w["gq2"], w["gk2"], dproj, f"attn_post_{l}")
    g["q_norm_g"] = dgq[0, :HEAD_DIM] + dgq[0, HEAD_DIM:]
    g["k_norm_g"] = dgk[0, :HEAD_DIM] + dgk[0, HEAD_DIM:]
    g["forget_b"] = db[0, :HEADS]

    dw_in = tn_matmul(dproj, sv["h"], f"dw_in_{l}", m_cols=N_FULL, ex=early_ici(EARLY_FIRST))
    if early:
        dw_in, got = dw_in
        comm.scattered(got, EARLY_FIRST)
    g["w_in"] = _ungroup_w_in(dw_in)
    (dx, dg), got = matmul_normbwd(dproj, w["wt_in"], sv["x"], w["g_mix"], dx1, f"in_proj_bwd_{l}", k=N_FULL,
                                   ex=early_ici(EARLY_SECOND))
    comm.scattered(got, EARLY_SECOND if early else None)
    g["norm_mix_g"] = dg[0]
    comm.grads(l, g)
    return dx


def _local_step(x, tgt, comm):
    ws, saved = [], []
    w = comm.weights(0, None)
    for l in range(DEPTH):
        ws.append(w)
        x, sv, gathered = _layer_fwd(x, w, l, comm)
        saved.append(sv)
        if l + 1 < DEPTH:
            w = comm.weights(l + 1, gathered)
    sq, dx = loss_kernel(x, tgt, "loss")
    for l in reversed(range(DEPTH)):
        dx = _layer_bwd(dx, saved[l], ws[l], l, comm)
    comm.finish()
    return sq[0, 0], dx


def kernel(x, norm_mix_g, w_in, forget_b, q_norm_g, k_norm_g, w_attn_out, conv_w, w_conv_out, pool_w, pool_scale, w_o, norm_ffn_g, w_ffn_in, w_ffn_out, loss_target, m_norm_mix_g, m_w_in, m_forget_b, m_q_norm_g, m_k_norm_g, m_w_attn_out, m_conv_w, m_w_conv_out, m_pool_w, m_pool_scale, m_w_o, m_norm_ffn_g, m_w_ffn_in, m_w_ffn_out, v_norm_mix_g, v_w_in, v_forget_b, v_q_norm_g, v_k_norm_g, v_w_attn_out, v_conv_w, v_w_conv_out, v_pool_w, v_pool_scale, v_w_o, v_norm_ffn_g, v_w_ffn_in, v_w_ffn_out):
    w = dict(norm_mix_g=norm_mix_g, w_in=w_in, forget_b=forget_b, q_norm_g=q_norm_g, k_norm_g=k_norm_g,
             w_attn_out=w_attn_out, conv_w=conv_w, w_conv_out=w_conv_out, pool_w=pool_w, pool_scale=pool_scale,
             w_o=w_o, norm_ffn_g=norm_ffn_g, w_ffn_in=w_ffn_in, w_ffn_out=w_ffn_out)
    m = dict(norm_mix_g=m_norm_mix_g, w_in=m_w_in, forget_b=m_forget_b, q_norm_g=m_q_norm_g, k_norm_g=m_k_norm_g,
             w_attn_out=m_w_attn_out, conv_w=m_conv_w, w_conv_out=m_w_conv_out, pool_w=m_pool_w,
             pool_scale=m_pool_scale, w_o=m_w_o, norm_ffn_g=m_norm_ffn_g, w_ffn_in=m_w_ffn_in, w_ffn_out=m_w_ffn_out)
    v = dict(norm_mix_g=v_norm_mix_g, w_in=v_w_in, forget_b=v_forget_b, q_norm_g=v_q_norm_g, k_norm_g=v_k_norm_g,
             w_attn_out=v_w_attn_out, conv_w=v_conv_w, w_conv_out=v_w_conv_out, pool_w=v_pool_w,
             pool_scale=v_pool_scale, w_o=v_w_o, norm_ffn_g=v_norm_ffn_g, w_ffn_in=v_w_ffn_in, w_ffn_out=v_w_ffn_out)
    me = 4 * lax.axis_index("x") + 2 * lax.axis_index("y") + lax.axis_index("c")
    layer_shard = {n: SHARD_INFO[n][0][1:] for n in MATRICES}
    cut_axis = {n: SHARD_INFO[n][1] - 1 for n in MATRICES}

    vec = {n: w[n] for n in VECTORS}
    rc = {n: (_size(layer_shard[n][:-1]), layer_shard[n][-1]) for n in MATRICES}

    class Comm:
        bufs = [lax.empty((DEPTH, len(SAME_CORE)) + layer_shard[n], BF16) for n in MATRICES]
        blocks = [None] * DEPTH
        small_g = [None] * DEPTH
        conv_full = None

        @staticmethod
        def shards(l):
            return [_handled(n, w[n])[l].astype(BF16) for n in MATRICES]

        @staticmethod
        def gather_ici(l, part):
            return gather_over_ici([Comm.shards(l)[t] for t in part]) if l < DEPTH else None

        @staticmethod
        def gather_d2d(l, half):
            return gather_over_d2d(half) if l < DEPTH else None

        @staticmethod
        def weights(l, gathered):
            if l == 0:
                *gathered, conv_g = all_gather(Comm.shards(0) + [_pack([conv_w], 8, 128)], "gather_0")
                Comm.conv_full = _join_shards(jnp.stack([_unpack(conv_g[i], [conv_w.shape])[0] for i in range(N_DEV)]), 2)
            mats = {n: t if n == "w_in" else _join_shards(t, cut_axis[n]) for n, t in zip(MATRICES, gathered)}
            return _layer_weights(mats, vec, Comm.conv_full, l)

        @staticmethod
        def grads(l, g):
            Comm.small_g[l] = g
            Comm.blocks[l] = [None if n not in g else g[n] if n == "w_in" else _cut_shards(g[n], cut_axis[n])
                              for n in MATRICES]

        @staticmethod
        def early(l):
            return EARLY if l == 0 else None

        @staticmethod
        def scatter_d2d(l, which=EVERY):
            return scatter_over_d2d([Comm.blocks[l][t] for t in which]) if l < DEPTH else None

        @staticmethod
        def pair_sums(l, stage, which=EVERY):
            if l >= DEPTH:
                return None
            return [pair_sum(Comm.blocks[l][t].reshape((N_DEV,) + rc[MATRICES[t]]),
                             s.reshape((len(SAME_CORE),) + rc[MATRICES[t]]), me,
                             f"pair_sum_{MATRICES[t]}_{l}").reshape(s.shape) for t, s in zip(which, stage)]

        @staticmethod
        def scatter_ici(l, sums, which=EVERY):
            return scatter_over_ici(sums, [Comm.bufs[t] for t in which], l) if l < DEPTH else None

        @staticmethod
        def scattered(results, which=EVERY):
            for t, r in zip(which or (), results):
                Comm.bufs[t] = r

        @staticmethod
        def finish():
            stage = run_exchange(Comm.scatter_d2d(0, LATE), "scatter_d2d_0")
            Comm.scattered(run_exchange(Comm.scatter_ici(0, Comm.pair_sums(0, stage, LATE), LATE), "scatter_ici_0"), LATE)

    small_g, received = Comm.small_g, Comm
    sq, dx = _local_step(x[0], loss_target[0], Comm)

    big = {}
    for n, parts in zip(MATRICES, received.bufs):
        outs = adamw_sum(parts.reshape((DEPTH, len(SAME_CORE)) + rc[n]),
                         *[_handled(n, d[n]).reshape((DEPTH,) + rc[n]) for d in (w, m, v)], f"adamw_{n}")
        big[n] = [_handled(n, t.reshape((DEPTH,) + layer_shard[n])) for t in outs]

    small_shapes = [VECTOR_SHAPES[n] for n in VECTORS] + [CONV_W_FULL, (1,)]
    stacked = [jnp.stack([small_g[l][n] for l in range(DEPTH)]) for n in VECTORS + ("conv_w",)] + [sq.reshape(1)]
    sparts = all_gather([_pack(stacked, SMALL_ROWS, 128)], "gather_vector_grads")[0]
    col0 = me * (D_CONV // N_DEV)
    place = lambda t: lax.dynamic_update_slice(jnp.zeros(CONV_W_FULL, F32), t, (0, 0, col0))
    spacked = [_pack([d[n] for n in VECTORS] + [place(d["conv_w"]), jnp.zeros((1,), F32)], SMALL_ROWS, 128)[None]
               for d in (w, m, v)]
    small = [_unpack(t[0], small_shapes) for t in adamw_sum(sparts[None], *spacked, "adamw_vectors")]
    loss = (0.5 / D_MODEL) * small[0][-1][0]

    def result(kind):
        out = {n: big[n][kind] for n in MATRICES}
        out.update({n: small[kind][j] for j, n in enumerate(VECTORS)})
        out["conv_w"] = lax.dynamic_slice(small[kind][len(VECTORS)], (0, 0, col0), conv_w.shape)
        return [out[n] for n in w]

    return (loss, dx[None], *result(0), *result(1), *result(2), *result(3))
```
